```python
import jax, jax.numpy as jnp
from jax import lax
import numpy as np

D_MODEL = 2048
BATCH = 8
SEQ = 4096
DEPTH = 1

HEAD_DIM = D_MODEL // 16
FOX_HEADS = 6
SWA_HEADS = 6
SWA_KV_HEADS = 2
MEM_HEADS = 4
MEM_LEN = 256
WINDOW = 128
Q_BLOCK = 128
D_FF = 5632
EPS = 1e-6
NEG_INF = -1e30

FOX_W = FOX_HEADS * HEAD_DIM
SWA_Q_W = SWA_HEADS * HEAD_DIM
SWA_KV_W = SWA_KV_HEADS * HEAD_DIM
MEM_W = MEM_HEADS * HEAD_DIM
MIX_WIDTH = FOX_W + SWA_Q_W + MEM_W
IN_SPLITS = [FOX_W, FOX_W, FOX_W, FOX_HEADS, SWA_Q_W, SWA_KV_W, SWA_KV_W, MEM_W]
IN_WIDTH = int(sum(IN_SPLITS))
IN_CUTS = [int(c) for c in np.cumsum(IN_SPLITS)[:-1]]

kernel_name = "hybrid_fox_swa_memory_macaron"


def rms_norm(x, g):
    xf = x.astype(jnp.float32)
    y = xf * lax.rsqrt(jnp.mean(xf * xf, axis=-1, keepdims=True) + EPS)
    return (y * g.astype(jnp.float32)).astype(x.dtype)


def swiglu(x, w_gate, w_up, w_down):
    return (jax.nn.silu(x @ w_gate) * (x @ w_up)) @ w_down


def alibi_slopes(n):
    return jnp.asarray(2.0 ** (-8.0 * np.arange(1, n + 1) / n), dtype=jnp.float32)


def forgetting_attention(q, k, v, log_f):
    B, S, H, D = q.shape
    nb = S // Q_BLOCK
    scale = D ** -0.5
    c = jnp.cumsum(log_f, axis=1)
    c_k = c.transpose(0, 2, 1)
    kf = k.astype(jnp.float32)
    vf = v.astype(jnp.float32)
    qb = q.astype(jnp.float32).reshape(B, nb, Q_BLOCK, H, D).transpose(1, 0, 2, 3, 4)
    cb = c.reshape(B, nb, Q_BLOCK, H).transpose(1, 0, 3, 2)
    kpos = jnp.arange(S)

    def block(args):
        qi, ci, i = args
        qpos = i * Q_BLOCK + jnp.arange(Q_BLOCK)
        s = jnp.einsum('bqhd,bkhd->bhqk', qi, kf) * scale
        s = s + ci[..., None] - c_k[:, :, None, :]
        causal = kpos[None, :] <= qpos[:, None]
        s = jnp.where(causal, s, NEG_INF)
        p = jax.nn.softmax(s, axis=-1)
        return jnp.einsum('bhqk,bkhd->bqhd', p, vf)

    out = lax.map(block, (qb, cb, jnp.arange(nb)))
    return out.transpose(1, 0, 2, 3, 4).reshape(B, S, H, D).astype(q.dtype)


def sliding_window_sink_attention(q, k, v, sinks, slopes):
    B, S, Hq, D = q.shape
    Hkv = k.shape[2]
    G = Hq // Hkv
    nb = S // WINDOW
    scale = D ** -0.5
    qb = q.astype(jnp.float32).reshape(B, nb, WINDOW, Hkv, G, D)

    def band(t):
        tb = t.astype(jnp.float32).reshape(B, nb, WINDOW, Hkv, D)
        prev = jnp.pad(tb[:, :-1], ((0, 0), (1, 0), (0, 0), (0, 0), (0, 0)))
        return jnp.concatenate([prev, tb], axis=2)

    kb, vb = band(k), band(v)
    s = jnp.einsum('bnqhgd,bnkhd->bnhgqk', qb, kb) * scale
    r = jnp.arange(WINDOW)[:, None]
    j = jnp.arange(2 * WINDOW)[None, :]
    dist = WINDOW + r - j
    in_window = (dist >= 0) & (dist < WINDOW)
    valid = in_window[None] & ((jnp.arange(nb)[:, None, None] > 0) | (j[None] >= WINDOW))
    alibi = -slopes.astype(jnp.float32).reshape(Hkv, G)[:, :, None, None] * dist.astype(jnp.float32)
    s = s + alibi[None, None]
    s = jnp.where(valid[None, :, None, None], s, NEG_INF)
    sink = jnp.broadcast_to(
        sinks.astype(jnp.float32).reshape(Hkv, G)[None, None, :, :, None, None],
        s.shape[:-1] + (1,))
    p = jax.nn.softmax(jnp.concatenate([s, sink], axis=-1), axis=-1)[..., :-1]
    out = jnp.einsum('bnhgqk,bnkhd->bnqhgd', p, vb)
    return out.reshape(B, S, Hq, D).astype(q.dtype)


def memory_attention(q, mk, mv):
    scale = q.shape[-1] ** -0.5
    s = jnp.einsum('bqhd,bmhd->bhqm', q.astype(jnp.float32), mk.astype(jnp.float32)) * scale
    p = jax.nn.softmax(s, axis=-1)
    return jnp.einsum('bhqm,bmhd->bqhd', p, mv.astype(jnp.float32)).astype(q.dtype)


def _fwd_setup_inputs(seed: int = 0) -> dict:
    key = jax.random.key(seed)
    ks = jax.random.split(key, 32)
    f32 = jnp.float32

    def w(k, shape, fan_in):
        return jax.random.normal(k, shape, f32) * (fan_in ** -0.5)

    def gain(k, shape):
        return 1.0 + 0.02 * jax.random.normal(k, shape, f32)

    L = DEPTH
    return {
        "x": jax.random.normal(ks[0], (BATCH, SEQ, D_MODEL), f32),
        "mem": jax.random.normal(ks[1], (BATCH, MEM_LEN, D_MODEL), f32),
        "ffn1_norm": gain(ks[2], (L, D_MODEL)),
        "ffn1_gate": w(ks[3], (L, D_MODEL, D_FF), D_MODEL),
        "ffn1_up": w(ks[4], (L, D_MODEL, D_FF), D_MODEL),
        "ffn1_down": w(ks[5], (L, D_FF, D_MODEL), D_FF),
        "mix_norm": gain(ks[6], (L, D_MODEL)),
        "mem_norm": gain(ks[7], (L, D_MODEL)),
        "w_in": w(ks[8], (L, D_MODEL, IN_WIDTH), D_MODEL),
        "forget_bias": jax.random.uniform(ks[9], (L, FOX_HEADS), f32, 1.0, 4.0),
        "w_mem_k": w(ks[10], (L, D_MODEL, MEM_W), D_MODEL),
        "w_mem_v": w(ks[11], (L, D_MODEL, MEM_W), D_MODEL),
        "fox_q_gain": gain(ks[12], (L, HEAD_DIM)),
        "fox_k_gain": gain(ks[13], (L, HEAD_DIM)),
        "swa_q_gain": gain(ks[14], (L, HEAD_DIM)),
        "swa_k_gain": gain(ks[15], (L, HEAD_DIM)),
        "swa_sinks": jax.random.normal(ks[16], (L, SWA_HEADS), f32),
        "mem_q_gain": gain(ks[17], (L, HEAD_DIM)),
        "mem_k_gain": gain(ks[18], (L, HEAD_DIM)),
        "w_out": w(ks[19], (L, MIX_WIDTH, D_MODEL), MIX_WIDTH),
        "ffn2_norm": gain(ks[20], (L, D_MODEL)),
        "ffn2_gate": w(ks[21], (L, D_MODEL, D_FF), D_MODEL),
        "ffn2_up": w(ks[22], (L, D_MODEL, D_FF), D_MODEL),
        "ffn2_down": w(ks[23], (L, D_FF, D_MODEL), D_FF),
    }


def _fwd_reference(x, mem, ffn1_norm, ffn1_gate, ffn1_up, ffn1_down, mix_norm, mem_norm, w_in,
              forget_bias, w_mem_k, w_mem_v, fox_q_gain, fox_k_gain, swa_q_gain, swa_k_gain,
              swa_sinks, mem_q_gain, mem_k_gain, w_out, ffn2_norm, ffn2_gate, ffn2_up, ffn2_down):
    B, S, _ = x.shape
    M = mem.shape[1]
    slopes = alibi_slopes(SWA_HEADS).astype(x.dtype)
    for l in range(DEPTH):
        x = x + 0.5 * swiglu(rms_norm(x, ffn1_norm[l]), ffn1_gate[l], ffn1_up[l], ffn1_down[l])

        h = rms_norm(x, mix_norm[l])
        proj = h @ w_in[l]
        fq, fk, fv, f_logit, sq, sk, sv, mq = jnp.split(proj, IN_CUTS, axis=-1)

        fq = rms_norm(fq.reshape(B, S, FOX_HEADS, HEAD_DIM), fox_q_gain[l])
        fk = rms_norm(fk.reshape(B, S, FOX_HEADS, HEAD_DIM), fox_k_gain[l])
        fv = fv.reshape(B, S, FOX_HEADS, HEAD_DIM)
        log_f = jax.nn.log_sigmoid(f_logit.astype(jnp.float32) + forget_bias[l].astype(jnp.float32))
        out_a = forgetting_attention(fq, fk, fv, log_f)

        sq = rms_norm(sq.reshape(B, S, SWA_HEADS, HEAD_DIM), swa_q_gain[l])
        sk = rms_norm(sk.reshape(B, S, SWA_KV_HEADS, HEAD_DIM), swa_k_gain[l])
        sv = sv.reshape(B, S, SWA_KV_HEADS, HEAD_DIM)
        out_b = sliding_window_sink_attention(sq, sk, sv, swa_sinks[l], slopes)

        mn = rms_norm(mem, mem_norm[l])
        mk = rms_norm((mn @ w_mem_k[l]).reshape(B, M, MEM_HEADS, HEAD_DIM), mem_k_gain[l])
        mv = (mn @ w_mem_v[l]).reshape(B, M, MEM_HEADS, HEAD_DIM)
        mq = rms_norm(mq.reshape(B, S, MEM_HEADS, HEAD_DIM), mem_q_gain[l])
        out_c = memory_attention(mq, mk, mv)

        mixed = jnp.concatenate([out_a.reshape(B, S, FOX_W), out_b.reshape(B, S, SWA_Q_W),
                                 out_c.reshape(B, S, MEM_W)], axis=-1)
        x = x + mixed @ w_out[l]

        x = x + 0.5 * swiglu(rms_norm(x, ffn2_norm[l]), ffn2_gate[l], ffn2_up[l], ffn2_down[l])
    return x


import jax as _jax
import jax.numpy as _jnp

TWIN_FORMAT = 'train_step'
FWD_PARAMS = ['x', 'mem', 'ffn1_norm', 'ffn1_gate', 'ffn1_up', 'ffn1_down', 'mix_norm', 'mem_norm', 'w_in', 'forget_bias', 'w_mem_k', 'w_mem_v', 'fox_q_gain', 'fox_k_gain', 'swa_q_gain', 'swa_k_gain', 'swa_sinks', 'mem_q_gain', 'mem_k_gain', 'w_out', 'ffn2_norm', 'ffn2_gate', 'ffn2_up', 'ffn2_down']
TWIN_WEIGHTS = ['ffn1_norm', 'ffn1_gate', 'ffn1_up', 'ffn1_down', 'mix_norm', 'mem_norm', 'w_in', 'forget_bias', 'w_mem_k', 'w_mem_v', 'fox_q_gain', 'fox_k_gain', 'swa_q_gain', 'swa_k_gain', 'swa_sinks', 'mem_q_gain', 'mem_k_gain', 'w_out', 'ffn2_norm', 'ffn2_gate', 'ffn2_up', 'ffn2_down']
TWIN_DIFF_INPUT = 'x'
TWIN_INPUTS = ['x', 'mem', 'ffn1_norm', 'ffn1_gate', 'ffn1_up', 'ffn1_down', 'mix_norm', 'mem_norm', 'w_in', 'forget_bias', 'w_mem_k', 'w_mem_v', 'fox_q_gain', 'fox_k_gain', 'swa_q_gain', 'swa_k_gain', 'swa_sinks', 'mem_q_gain', 'mem_k_gain', 'w_out', 'ffn2_norm', 'ffn2_gate', 'ffn2_up', 'ffn2_down', 'loss_target', 'm_ffn1_norm', 'm_ffn1_gate', 'm_ffn1_up', 'm_ffn1_down', 'm_mix_norm', 'm_mem_norm', 'm_w_in', 'm_forget_bias', 'm_w_mem_k', 'm_w_mem_v', 'm_fox_q_gain', 'm_fox_k_gain', 'm_swa_q_gain', 'm_swa_k_gain', 'm_swa_sinks', 'm_mem_q_gain', 'm_mem_k_gain', 'm_w_out', 'm_ffn2_norm', 'm_ffn2_gate', 'm_ffn2_up', 'm_ffn2_down', 'v_ffn1_norm', 'v_ffn1_gate', 'v_ffn1_up', 'v_ffn1_down', 'v_mix_norm', 'v_mem_norm', 'v_w_in', 'v_forget_bias', 'v_w_mem_k', 'v_w_mem_v', 'v_fox_q_gain', 'v_fox_k_gain', 'v_swa_q_gain', 'v_swa_k_gain', 'v_swa_sinks', 'v_mem_q_gain', 'v_mem_k_gain', 'v_w_out', 'v_ffn2_norm', 'v_ffn2_gate', 'v_ffn2_up', 'v_ffn2_down']
TWIN_OUTPUTS = ['loss', 'grad_x', 'grad_ffn1_norm', 'grad_ffn1_gate', 'grad_ffn1_up', 'grad_ffn1_down', 'grad_mix_norm', 'grad_mem_norm', 'grad_w_in', 'grad_forget_bias', 'grad_w_mem_k', 'grad_w_mem_v', 'grad_fox_q_gain', 'grad_fox_k_gain', 'grad_swa_q_gain', 'grad_swa_k_gain', 'grad_swa_sinks', 'grad_mem_q_gain', 'grad_mem_k_gain', 'grad_w_out', 'grad_ffn2_norm', 'grad_ffn2_gate', 'grad_ffn2_up', 'grad_ffn2_down', 'delta_ffn1_norm', 'delta_ffn1_gate', 'delta_ffn1_up', 'delta_ffn1_down', 'delta_mix_norm', 'delta_mem_norm', 'delta_w_in', 'delta_forget_bias', 'delta_w_mem_k', 'delta_w_mem_v', 'delta_fox_q_gain', 'delta_fox_k_gain', 'delta_swa_q_gain', 'delta_swa_k_gain', 'delta_swa_sinks', 'delta_mem_q_gain', 'delta_mem_k_gain', 'delta_w_out', 'delta_ffn2_norm', 'delta_ffn2_gate', 'delta_ffn2_up', 'delta_ffn2_down', 'new_m_ffn1_norm', 'new_m_ffn1_gate', 'new_m_ffn1_up', 'new_m_ffn1_down', 'new_m_mix_norm', 'new_m_mem_norm', 'new_m_w_in', 'new_m_forget_bias', 'new_m_w_mem_k', 'new_m_w_mem_v', 'new_m_fox_q_gain', 'new_m_fox_k_gain', 'new_m_swa_q_gain', 'new_m_swa_k_gain', 'new_m_swa_sinks', 'new_m_mem_q_gain', 'new_m_mem_k_gain', 'new_m_w_out', 'new_m_ffn2_norm', 'new_m_ffn2_gate', 'new_m_ffn2_up', 'new_m_ffn2_down', 'new_v_ffn1_norm', 'new_v_ffn1_gate', 'new_v_ffn1_up', 'new_v_ffn1_down', 'new_v_mix_norm', 'new_v_mem_norm', 'new_v_w_in', 'new_v_forget_bias', 'new_v_w_mem_k', 'new_v_w_mem_v', 'new_v_fox_q_gain', 'new_v_fox_k_gain', 'new_v_swa_q_gain', 'new_v_swa_k_gain', 'new_v_swa_sinks', 'new_v_mem_q_gain', 'new_v_mem_k_gain', 'new_v_w_out', 'new_v_ffn2_norm', 'new_v_ffn2_gate', 'new_v_ffn2_up', 'new_v_ffn2_down']
TWIN_LEAF_KINDS = {'loss': 'loss', 'grad_x': 'grad_x', 'grad_ffn1_norm': 'grad_w', 'grad_ffn1_gate': 'grad_w', 'grad_ffn1_up': 'grad_w', 'grad_ffn1_down': 'grad_w', 'grad_mix_norm': 'grad_w', 'grad_mem_norm': 'grad_w', 'grad_w_in': 'grad_w', 'grad_forget_bias': 'grad_w', 'grad_w_mem_k': 'grad_w', 'grad_w_mem_v': 'grad_w', 'grad_fox_q_gain': 'grad_w', 'grad_fox_k_gain': 'grad_w', 'grad_swa_q_gain': 'grad_w', 'grad_swa_k_gain': 'grad_w', 'grad_swa_sinks': 'grad_w', 'grad_mem_q_gain': 'grad_w', 'grad_mem_k_gain': 'grad_w', 'grad_w_out': 'grad_w', 'grad_ffn2_norm': 'grad_w', 'grad_ffn2_gate': 'grad_w', 'grad_ffn2_up': 'grad_w', 'grad_ffn2_down': 'grad_w', 'delta_ffn1_norm': 'delta_w', 'delta_ffn1_gate': 'delta_w', 'delta_ffn1_up': 'delta_w', 'delta_ffn1_down': 'delta_w', 'delta_mix_norm': 'delta_w', 'delta_mem_norm': 'delta_w', 'delta_w_in': 'delta_w', 'delta_forget_bias': 'delta_w', 'delta_w_mem_k': 'delta_w', 'delta_w_mem_v': 'delta_w', 'delta_fox_q_gain': 'delta_w', 'delta_fox_k_gain': 'delta_w', 'delta_swa_q_gain': 'delta_w', 'delta_swa_k_gain': 'delta_w', 'delta_swa_sinks': 'delta_w', 'delta_mem_q_gain': 'delta_w', 'delta_mem_k_gain': 'delta_w', 'delta_w_out': 'delta_w', 'delta_ffn2_norm': 'delta_w', 'delta_ffn2_gate': 'delta_w', 'delta_ffn2_up': 'delta_w', 'delta_ffn2_down': 'delta_w', 'new_m_ffn1_norm': 'new_m', 'new_m_ffn1_gate': 'new_m', 'new_m_ffn1_up': 'new_m', 'new_m_ffn1_down': 'new_m', 'new_m_mix_norm': 'new_m', 'new_m_mem_norm': 'new_m', 'new_m_w_in': 'new_m', 'new_m_forget_bias': 'new_m', 'new_m_w_mem_k': 'new_m', 'new_m_w_mem_v': 'new_m', 'new_m_fox_q_gain': 'new_m', 'new_m_fox_k_gain': 'new_m', 'new_m_swa_q_gain': 'new_m', 'new_m_swa_k_gain': 'new_m', 'new_m_swa_sinks': 'new_m', 'new_m_mem_q_gain': 'new_m', 'new_m_mem_k_gain': 'new_m', 'new_m_w_out': 'new_m', 'new_m_ffn2_norm': 'new_m', 'new_m_ffn2_gate': 'new_m', 'new_m_ffn2_up': 'new_m', 'new_m_ffn2_down': 'new_m', 'new_v_ffn1_norm': 'new_v', 'new_v_ffn1_gate': 'new_v', 'new_v_ffn1_up': 'new_v', 'new_v_ffn1_down': 'new_v', 'new_v_mix_norm': 'new_v', 'new_v_mem_norm': 'new_v', 'new_v_w_in': 'new_v', 'new_v_forget_bias': 'new_v', 'new_v_w_mem_k': 'new_v', 'new_v_w_mem_v': 'new_v', 'new_v_fox_q_gain': 'new_v', 'new_v_fox_k_gain': 'new_v', 'new_v_swa_q_gain': 'new_v', 'new_v_swa_k_gain': 'new_v', 'new_v_swa_sinks': 'new_v', 'new_v_mem_q_gain': 'new_v', 'new_v_mem_k_gain': 'new_v', 'new_v_w_out': 'new_v', 'new_v_ffn2_norm': 'new_v', 'new_v_ffn2_gate': 'new_v', 'new_v_ffn2_up': 'new_v', 'new_v_ffn2_down': 'new_v'}


def _forward(args):
    return _fwd_reference(*[args[k] for k in FWD_PARAMS])


def _output_shape():
    def fwd():
        inp = _fwd_setup_inputs(0)
        return _fwd_reference(*[inp[k] for k in FWD_PARAMS])
    out = _jax.eval_shape(fwd)
    return out.shape, out.dtype

N_MICROBATCH = 1
ADAM_LR = 0.001
ADAM_B1 = 0.9
ADAM_B2 = 0.999
ADAM_EPS = 1e-08
ADAM_WD = 0.01
ADAM_STEP = 10
PER_EXAMPLE_BATCH_AXIS = {'x': 0, 'mem': 0, 'loss_target': 0}
SHARED_INPUTS = []
_WEIGHT_DTYPES = {'ffn1_norm': _jnp.float32, 'ffn1_gate': _jnp.float32, 'ffn1_up': _jnp.float32, 'ffn1_down': _jnp.float32, 'mix_norm': _jnp.float32, 'mem_norm': _jnp.float32, 'w_in': _jnp.float32, 'forget_bias': _jnp.float32, 'w_mem_k': _jnp.float32, 'w_mem_v': _jnp.float32, 'fox_q_gain': _jnp.float32, 'fox_k_gain': _jnp.float32, 'swa_q_gain': _jnp.float32, 'swa_k_gain': _jnp.float32, 'swa_sinks': _jnp.float32, 'mem_q_gain': _jnp.float32, 'mem_k_gain': _jnp.float32, 'w_out': _jnp.float32, 'ffn2_norm': _jnp.float32, 'ffn2_gate': _jnp.float32, 'ffn2_up': _jnp.float32, 'ffn2_down': _jnp.float32}
MOMENT_SCALE = {'ffn1_norm': 3.079122e+00, 'ffn1_gate': 3.538857e-02, 'ffn1_up': 3.772447e-02, 'ffn1_down': 6.169150e-02, 'mix_norm': 1.389861e+00, 'mem_norm': 4.050807e-02, 'w_in': 9.022515e-02, 'forget_bias': 8.637390e+01, 'w_mem_k': 1.683702e-02, 'w_mem_v': 2.851491e-02, 'fox_q_gain': 4.922546e+00, 'fox_k_gain': 4.927648e+00, 'swa_q_gain': 3.481294e+00, 'swa_k_gain': 3.488707e+00, 'swa_sinks': 1.426673e+01, 'mem_q_gain': 5.915523e-01, 'mem_k_gain': 5.914567e-01, 'w_out': 5.976211e-02, 'ffn2_norm': 3.094834e+00, 'ffn2_gate': 2.795989e-02, 'ffn2_up': 3.140753e-02, 'ffn2_down': 5.051951e-02}


def _to_microbatches(a, axis):
    t = _jnp.moveaxis(a, axis, 0)
    t = t.reshape((N_MICROBATCH, t.shape[0] // N_MICROBATCH) + t.shape[1:])
    return _jnp.moveaxis(t, 1, axis + 1)


def setup_inputs(seed: int = 0) -> dict:
    inp = _fwd_setup_inputs(seed)
    key = _jax.random.fold_in(_jax.random.key(seed), 7919)
    shape, _ = _output_shape()
    out = dict(inp)
    out["loss_target"] = _jax.random.normal(_jax.random.fold_in(key, 0), shape, _jnp.float32)
    for i, name in enumerate(TWIN_WEIGHTS):
        w = inp[name].astype(_jnp.float32)
        if MOMENT_SCALE is None:
            s = _jnp.sqrt(_jnp.mean(_jnp.square(w)) + 1e-30)
        else:
            s = MOMENT_SCALE[name]
        km, kv = _jax.random.split(_jax.random.fold_in(key, i + 1))
        out[name] = w
        out["m_" + name] = s * _jax.random.normal(km, w.shape, _jnp.float32)
        out["v_" + name] = (s * s) * _jax.random.uniform(kv, w.shape, _jnp.float32, 0.5, 1.5)
    if N_MICROBATCH > 1:
        for name, axis in PER_EXAMPLE_BATCH_AXIS.items():
            out[name] = _to_microbatches(out[name], axis)
    return {'x': out['x'], 'mem': out['mem'], 'ffn1_norm': out['ffn1_norm'], 'ffn1_gate': out['ffn1_gate'], 'ffn1_up': out['ffn1_up'], 'ffn1_down': out['ffn1_down'], 'mix_norm': out['mix_norm'], 'mem_norm': out['mem_norm'], 'w_in': out['w_in'], 'forget_bias': out['forget_bias'], 'w_mem_k': out['w_mem_k'], 'w_mem_v': out['w_mem_v'], 'fox_q_gain': out['fox_q_gain'], 'fox_k_gain': out['fox_k_gain'], 'swa_q_gain': out['swa_q_gain'], 'swa_k_gain': out['swa_k_gain'], 'swa_sinks': out['swa_sinks'], 'mem_q_gain': out['mem_q_gain'], 'mem_k_gain': out['mem_k_gain'], 'w_out': out['w_out'], 'ffn2_norm': out['ffn2_norm'], 'ffn2_gate': out['ffn2_gate'], 'ffn2_up': out['ffn2_up'], 'ffn2_down': out['ffn2_down'], 'loss_target': out['loss_target'], 'm_ffn1_norm': out['m_ffn1_norm'], 'm_ffn1_gate': out['m_ffn1_gate'], 'm_ffn1_up': out['m_ffn1_up'], 'm_ffn1_down': out['m_ffn1_down'], 'm_mix_norm': out['m_mix_norm'], 'm_mem_norm': out['m_mem_norm'], 'm_w_in': out['m_w_in'], 'm_forget_bias': out['m_forget_bias'], 'm_w_mem_k': out['m_w_mem_k'], 'm_w_mem_v': out['m_w_mem_v'], 'm_fox_q_gain': out['m_fox_q_gain'], 'm_fox_k_gain': out['m_fox_k_gain'], 'm_swa_q_gain': out['m_swa_q_gain'], 'm_swa_k_gain': out['m_swa_k_gain'], 'm_swa_sinks': out['m_swa_sinks'], 'm_mem_q_gain': out['m_mem_q_gain'], 'm_mem_k_gain': out['m_mem_k_gain'], 'm_w_out': out['m_w_out'], 'm_ffn2_norm': out['m_ffn2_norm'], 'm_ffn2_gate': out['m_ffn2_gate'], 'm_ffn2_up': out['m_ffn2_up'], 'm_ffn2_down': out['m_ffn2_down'], 'v_ffn1_norm': out['v_ffn1_norm'], 'v_ffn1_gate': out['v_ffn1_gate'], 'v_ffn1_up': out['v_ffn1_up'], 'v_ffn1_down': out['v_ffn1_down'], 'v_mix_norm': out['v_mix_norm'], 'v_mem_norm': out['v_mem_norm'], 'v_w_in': out['v_w_in'], 'v_forget_bias': out['v_forget_bias'], 'v_w_mem_k': out['v_w_mem_k'], 'v_w_mem_v': out['v_w_mem_v'], 'v_fox_q_gain': out['v_fox_q_gain'], 'v_fox_k_gain': out['v_fox_k_gain'], 'v_swa_q_gain': out['v_swa_q_gain'], 'v_swa_k_gain': out['v_swa_k_gain'], 'v_swa_sinks': out['v_swa_sinks'], 'v_mem_q_gain': out['v_mem_q_gain'], 'v_mem_k_gain': out['v_mem_k_gain'], 'v_w_out': out['v_w_out'], 'v_ffn2_norm': out['v_ffn2_norm'], 'v_ffn2_gate': out['v_ffn2_gate'], 'v_ffn2_up': out['v_ffn2_up'], 'v_ffn2_down': out['v_ffn2_down']}


def _loss(weights, diff, rest, loss_target):
    with _jax.named_scope("forward"):
        args = {**rest, TWIN_DIFF_INPUT: diff, **{k: w.astype(_WEIGHT_DTYPES[k]) for k, w in weights.items()}}
        y = _forward(args)
    with _jax.named_scope("loss_head"):
        err = _jnp.square(y.astype(_jnp.float32) - loss_target)
        return 0.5 * _jnp.sum(_jnp.mean(err, axis=-1)) if err.ndim else 0.5 * err


def _adamw(w, g, m, v):
    m = ADAM_B1 * m + (1.0 - ADAM_B1) * g
    v = ADAM_B2 * v + (1.0 - ADAM_B2) * _jnp.square(g)
    m_hat = m / (1.0 - ADAM_B1 ** ADAM_STEP)
    v_hat = v / (1.0 - ADAM_B2 ** ADAM_STEP)
    delta = -ADAM_LR * (m_hat / (_jnp.sqrt(v_hat) + ADAM_EPS) + ADAM_WD * w)
    return delta, m, v


def reference(x, mem, ffn1_norm, ffn1_gate, ffn1_up, ffn1_down, mix_norm, mem_norm, w_in, forget_bias, w_mem_k, w_mem_v, fox_q_gain, fox_k_gain, swa_q_gain, swa_k_gain, swa_sinks, mem_q_gain, mem_k_gain, w_out, ffn2_norm, ffn2_gate, ffn2_up, ffn2_down, loss_target, m_ffn1_norm, m_ffn1_gate, m_ffn1_up, m_ffn1_down, m_mix_norm, m_mem_norm, m_w_in, m_forget_bias, m_w_mem_k, m_w_mem_v, m_fox_q_gain, m_fox_k_gain, m_swa_q_gain, m_swa_k_gain, m_swa_sinks, m_mem_q_gain, m_mem_k_gain, m_w_out, m_ffn2_norm, m_ffn2_gate, m_ffn2_up, m_ffn2_down, v_ffn1_norm, v_ffn1_gate, v_ffn1_up, v_ffn1_down, v_mix_norm, v_mem_norm, v_w_in, v_forget_bias, v_w_mem_k, v_w_mem_v, v_fox_q_gain, v_fox_k_gain, v_swa_q_gain, v_swa_k_gain, v_swa_sinks, v_mem_q_gain, v_mem_k_gain, v_w_out, v_ffn2_norm, v_ffn2_gate, v_ffn2_up, v_ffn2_down):
    given = dict(x=x, mem=mem, ffn1_norm=ffn1_norm, ffn1_gate=ffn1_gate, ffn1_up=ffn1_up, ffn1_down=ffn1_down, mix_norm=mix_norm, mem_norm=mem_norm, w_in=w_in, forget_bias=forget_bias, w_mem_k=w_mem_k, w_mem_v=w_mem_v, fox_q_gain=fox_q_gain, fox_k_gain=fox_k_gain, swa_q_gain=swa_q_gain, swa_k_gain=swa_k_gain, swa_sinks=swa_sinks, mem_q_gain=mem_q_gain, mem_k_gain=mem_k_gain, w_out=w_out, ffn2_norm=ffn2_norm, ffn2_gate=ffn2_gate, ffn2_up=ffn2_up, ffn2_down=ffn2_down, loss_target=loss_target, m_ffn1_norm=m_ffn1_norm, m_ffn1_gate=m_ffn1_gate, m_ffn1_up=m_ffn1_up, m_ffn1_down=m_ffn1_down, m_mix_norm=m_mix_norm, m_mem_norm=m_mem_norm, m_w_in=m_w_in, m_forget_bias=m_forget_bias, m_w_mem_k=m_w_mem_k, m_w_mem_v=m_w_mem_v, m_fox_q_gain=m_fox_q_gain, m_fox_k_gain=m_fox_k_gain, m_swa_q_gain=m_swa_q_gain, m_swa_k_gain=m_swa_k_gain, m_swa_sinks=m_swa_sinks, m_mem_q_gain=m_mem_q_gain, m_mem_k_gain=m_mem_k_gain, m_w_out=m_w_out, m_ffn2_norm=m_ffn2_norm, m_ffn2_gate=m_ffn2_gate, m_ffn2_up=m_ffn2_up, m_ffn2_down=m_ffn2_down, v_ffn1_norm=v_ffn1_norm, v_ffn1_gate=v_ffn1_gate, v_ffn1_up=v_ffn1_up, v_ffn1_down=v_ffn1_down, v_mix_norm=v_mix_norm, v_mem_norm=v_mem_norm, v_w_in=v_w_in, v_forget_bias=v_forget_bias, v_w_mem_k=v_w_mem_k, v_w_mem_v=v_w_mem_v, v_fox_q_gain=v_fox_q_gain, v_fox_k_gain=v_fox_k_gain, v_swa_q_gain=v_swa_q_gain, v_swa_k_gain=v_swa_k_gain, v_swa_sinks=v_swa_sinks, v_mem_q_gain=v_mem_q_gain, v_mem_k_gain=v_mem_k_gain, v_w_out=v_w_out, v_ffn2_norm=v_ffn2_norm, v_ffn2_gate=v_ffn2_gate, v_ffn2_up=v_ffn2_up, v_ffn2_down=v_ffn2_down)
    weights = {n: given[n] for n in TWIN_WEIGHTS}
    shared = {n: given[n] for n in SHARED_INPUTS}
    per_example = {n: given[n] for n in ['x', 'mem']}
    grad_fn = _jax.value_and_grad(_loss, argnums=(0, 1))

    def one_microbatch(ex, loss_target):
        ex = dict(ex)
        diff = ex.pop(TWIN_DIFF_INPUT)
        return grad_fn(weights, diff, {**shared, **ex}, loss_target)

    if N_MICROBATCH == 1:
        loss, (grad_w, grad_x) = one_microbatch(per_example, given["loss_target"])
    else:
        def body(carry, xs):
            loss_sum, grad_sum = carry
            l_k, (gw_k, gx_k) = one_microbatch(xs[0], xs[1])
            with _jax.named_scope("update"):
                return (loss_sum + l_k, _jax.tree.map(_jnp.add, grad_sum, gw_k)), gx_k

        init = (_jnp.zeros((), _jnp.float32), _jax.tree.map(_jnp.zeros_like, weights))
        (loss, grad_w), grad_x = _jax.lax.scan(body, init, (per_example, given["loss_target"]))
    with _jax.named_scope("update"):
        delta_w, new_m, new_v = {}, {}, {}
        for n in TWIN_WEIGHTS:
            delta_w[n], new_m[n], new_v[n] = _adamw(weights[n], grad_w[n], given["m_" + n], given["v_" + n])
    return (loss, grad_x, *[grad_w[n] for n in TWIN_WEIGHTS], *[delta_w[n] for n in TWIN_WEIGHTS],
            *[new_m[n] for n in TWIN_WEIGHTS], *[new_v[n] for n in TWIN_WEIGHTS])
```

```python
import functools

import numpy as np
import jax
import jax.numpy as jnp
from jax import lax
from jax.experimental import pallas as pl
from jax.experimental.pallas import tpu as pltpu

F32 = jnp.float32
BF16 = jnp.bfloat16
MESH = pl.DeviceIdType.MESH

HEAD_DIM = 128
FOX_HEADS = 6
SWA_HEADS = 6
SWA_KV_HEADS = 2
SWA_GROUP = SWA_HEADS // SWA_KV_HEADS
MEM_HEADS = 4
WINDOW = 128
EPS = 1e-6
NEG_INF = -1e30
SCALE = HEAD_DIM ** -0.5

C_FQ = 0
C_FK = C_FQ + FOX_HEADS * HEAD_DIM
C_FV = C_FK + FOX_HEADS * HEAD_DIM
C_SQ = C_FV + FOX_HEADS * HEAD_DIM
C_SK = C_SQ + SWA_HEADS * HEAD_DIM
C_SV = C_SK + SWA_KV_HEADS * HEAD_DIM
C_MQ = C_SV + SWA_KV_HEADS * HEAD_DIM
C_FL = C_MQ + MEM_HEADS * HEAD_DIM
PROJ_W = C_FL + HEAD_DIM
FOX_W = FOX_HEADS * HEAD_DIM
REF_GROUPS = [
    (0, FOX_W, C_FQ), (FOX_W, FOX_W, C_FK), (2 * FOX_W, FOX_W, C_FV), (3 * FOX_W, FOX_HEADS, C_FL),
    (3 * FOX_W + FOX_HEADS, SWA_HEADS * HEAD_DIM, C_SQ),
    (3 * FOX_W + FOX_HEADS + SWA_HEADS * HEAD_DIM, SWA_KV_HEADS * HEAD_DIM, C_SK),
    (3 * FOX_W + FOX_HEADS + (SWA_HEADS + SWA_KV_HEADS) * HEAD_DIM, SWA_KV_HEADS * HEAD_DIM, C_SV),
    (3 * FOX_W + FOX_HEADS + (SWA_HEADS + 2 * SWA_KV_HEADS) * HEAD_DIM, MEM_HEADS * HEAD_DIM, C_MQ),
]

ADAM_LR = 0.001
ADAM_B1 = 0.9
ADAM_B2 = 0.999
ADAM_EPS = 1e-08
ADAM_WD = 0.01
ADAM_STEP = 10

V7X_VMEM_LIMIT = 56 * 1024 * 1024
N_CHIPS = 4
N_DEV = 8


def _tile(n, pref, mult=128):
    t = (min(pref, n) // mult) * mult
    while t >= mult:
        if n % t == 0:
            return t
        t -= mult
    return n


def _params(sem):
    return pltpu.CompilerParams(dimension_semantics=sem, vmem_limit_bytes=V7X_VMEM_LIMIT)


_DIMS = {"nn": (((1,), (0,)), ((), ())), "nt": (((1,), (1,)), ((), ())), "tn": (((0,), (0,)), ((), ()))}


def _dot(a, b, mode):
    return lax.dot_general(a, b, _DIMS[mode], preferred_element_type=F32)


def _mm(name, grid, pairs, acc_of, acc_shapes, extras, outs, epilogue):
    n_p, n_e, n_o, n_a = len(pairs), len(extras), len(outs), len(acc_shapes)
    nk = grid[2]

    def body(*refs):
        ab = refs[:2 * n_p]
        ex = refs[2 * n_p:2 * n_p + n_e]
        out = refs[2 * n_p + n_e:2 * n_p + n_e + n_o]
        accs = refs[2 * n_p + n_e + n_o:]
        parts = [None] * n_a
        for p in range(n_p):
            d = _dot(ab[2 * p][...], ab[2 * p + 1][...], pairs[p][4])
            parts[acc_of[p]] = d if parts[acc_of[p]] is None else parts[acc_of[p]] + d

        def finish(vals):
            for o, r in zip(out, epilogue(vals, [e[...] for e in ex])):
                o[...] = r.astype(o.dtype)

        if nk == 1:
            finish(parts)
            return
        k = pl.program_id(2)

        @pl.when(k == 0)
        def _():
            for a, d in zip(accs, parts):
                a[...] = d

        @pl.when(k > 0)
        def _():
            for a, d in zip(accs, parts):
                a[...] += d

        @pl.when(k == nk - 1)
        def _():
            finish([a[...] for a in accs])

    in_specs, args = [], []
    for a, a_spec, b, b_spec, _ in pairs:
        in_specs += [a_spec, b_spec]
        args += [a, b]
    for e, e_spec in extras:
        in_specs.append(e_spec)
        args.append(e)
    res = pl.pallas_call(
        body, name=name, grid=grid, in_specs=in_specs,
        out_specs=[s for _, s in outs], out_shape=[o for o, _ in outs],
        scratch_shapes=[pltpu.VMEM(s, F32) for s in acc_shapes] if nk > 1 else [],
        compiler_params=_params(("parallel", "parallel", "arbitrary")),
    )(*args)
    return res


def _mm2d(name, a, b, mode, out_dtype, tm=512, tn=1024, tk=1024, extras=(), epilogue=None, n_out=1):
    if mode == "nn":
        (M, K), N = a.shape, b.shape[1]
    elif mode == "nt":
        (M, K), N = a.shape, b.shape[0]
    else:
        (K, M), N = a.shape, b.shape[1]
    tm, tn, tk = _tile(M, tm), _tile(N, tn), _tile(K, tk)
    a_spec = pl.BlockSpec((tk, tm), lambda i, j, k: (k, i)) if mode == "tn" else pl.BlockSpec((tm, tk), lambda i, j, k: (i, k))
    b_spec = pl.BlockSpec((tn, tk), lambda i, j, k: (j, k)) if mode == "nt" else pl.BlockSpec((tk, tn), lambda i, j, k: (k, j))
    mn = pl.BlockSpec((tm, tn), lambda i, j, k: (i, j))
    if epilogue is None:
        epilogue = lambda accs, ex: [accs[0]]
    if not isinstance(out_dtype, (list, tuple)):
        out_dtype = [out_dtype] * n_out
    res = _mm(name, (M // tm, N // tn, K // tk), [(a, a_spec, b, b_spec, mode)], [0], [(tm, tn)],
              [(e, mn) for e in extras], [(jax.ShapeDtypeStruct((M, N), d), mn) for d in out_dtype], epilogue)
    return res[0] if len(res) == 1 else res


def _sigmoid(x):
    return 1.0 / (1.0 + jnp.exp(-x))


def _ffn_gu(name, h, wg, wu):
    T, D = h.shape
    nf, _, F4 = wg.shape
    tm, tk = _tile(T, 512), _tile(D, 1024)
    a_spec = pl.BlockSpec((tm, tk), lambda i, j, k: (i, k))
    b_spec = pl.BlockSpec((None, tk, F4), lambda i, j, k: (j, k, 0))
    o_spec = pl.BlockSpec((tm, F4), lambda i, j, k: (i, j))

    def epilogue(accs, ex):
        g, u = accs
        return [g, u, g * _sigmoid(g) * u]

    sds = jax.ShapeDtypeStruct((T, nf * F4), BF16)
    return _mm(name, (T // tm, nf, D // tk), [(h, a_spec, wg, b_spec, "nn"), (h, a_spec, wu, b_spec, "nn")], [0, 1],
               [(tm, F4), (tm, F4)], [], [(sds, o_spec)] * 3, epilogue)


def _ffn_down(name, a, wd, xres):
    return _mm2d(name, a, wd, "nn", F32, tk=wd.shape[0] // N_CHIPS, extras=[xres],
                 epilogue=lambda accs, ex: [ex[0] + 0.5 * accs[0]])


def _ffn_bwd(tag, dyb, h, g, u, a, wg, wu, wd):
    T, D = h.shape
    nf, _, F4 = wg.shape

    def act_bwd(accs, ex):
        gf, uf = ex[0].astype(F32), ex[1].astype(F32)
        s = _sigmoid(gf)
        return [accs[0] * uf * s * (1.0 + gf * (1.0 - s)), accs[0] * gf * s]

    dg, du = _mm2d(tag + "_da", dyb, wd, "nt", BF16, tn=F4, extras=[g, u], epilogue=act_bwd, n_out=2)
    dwd = _mm2d(tag + "_dwd", a, dyb, "tn", F32, tm=F4, tk=512)

    tm, tk = _tile(D, 512), _tile(T, 512)
    h_spec = pl.BlockSpec((tk, tm), lambda i, j, k: (k, i))
    d_spec = pl.BlockSpec((tk, F4), lambda i, j, k: (k, j))
    w_spec = pl.BlockSpec((None, tm, F4), lambda i, j, k: (j, i, 0))
    sds = jax.ShapeDtypeStruct((nf, D, F4), F32)
    dwg, dwu = _mm(tag + "_dwgu", (D // tm, nf, T // tk), [(h, h_spec, dg, d_spec, "tn"), (h, h_spec, du, d_spec, "tn")],
                   [0, 1], [(tm, F4), (tm, F4)], [], [(sds, w_spec)] * 2, lambda accs, ex: accs)

    tm, tn = _tile(T, 512), _tile(D, 1024)
    a_spec = pl.BlockSpec((tm, F4), lambda i, j, k: (i, k))
    b_spec = pl.BlockSpec((None, tn, F4), lambda i, j, k: (k, j, 0))
    o_spec = pl.BlockSpec((tm, tn), lambda i, j, k: (i, j))
    (dh,) = _mm(tag + "_dh", (T // tm, D // tn, nf), [(dg, a_spec, wg, b_spec, "nt"), (du, a_spec, wu, b_spec, "nt")],
                [0, 0], [(tm, tn)], [], [(jax.ShapeDtypeStruct((T, D), F32), o_spec)], lambda accs, ex: accs)
    return dwg, dwu, dwd, dh


def _rms_fwd(name, x, gain):
    R, D = x.shape
    tr = _tile(R, 256, 8)

    def body(x_ref, g_ref, o_ref):
        xv = x_ref[...]
        r = lax.rsqrt(jnp.mean(xv * xv, axis=-1, keepdims=True) + EPS)
        o_ref[...] = (xv * r * g_ref[...]).astype(BF16)

    return pl.pallas_call(
        body, name=name, grid=(R // tr,),
        in_specs=[pl.BlockSpec((tr, D), lambda i: (i, 0)), pl.BlockSpec((1, D), lambda i: (0, 0))],
        out_specs=pl.BlockSpec((tr, D), lambda i: (i, 0)), out_shape=jax.ShapeDtypeStruct((R, D), BF16),
        compiler_params=_params(("parallel",)),
    )(x, gain)


def _rms_bwd(name, dh, x, gain, dres, bscale):
    R, D = x.shape
    tr = _tile(R, 256, 8)

    def body(dh_ref, x_ref, g_ref, dres_ref, dx_ref, dxb_ref, dg_ref):
        xv, dy = x_ref[...], dh_ref[...]
        r = lax.rsqrt(jnp.mean(xv * xv, axis=-1, keepdims=True) + EPS)
        xn = xv * r
        uu = dy * g_ref[...]
        dx = dres_ref[...] + r * (uu - xn * jnp.mean(xn * uu, axis=-1, keepdims=True))
        dx_ref[...] = dx
        dxb_ref[...] = (bscale * dx).astype(BF16)
        part = jnp.sum(dy * xn, axis=0, keepdims=True)

        @pl.when(pl.program_id(0) == 0)
        def _():
            dg_ref[...] = part

        @pl.when(pl.program_id(0) > 0)
        def _():
            dg_ref[...] += part

    row = pl.BlockSpec((tr, D), lambda i: (i, 0))
    vec = pl.BlockSpec((1, D), lambda i: (0, 0))
    return pl.pallas_call(
        body, name=name, grid=(R // tr,), in_specs=[row, row, vec, row], out_specs=[row, row, vec],
        out_shape=[jax.ShapeDtypeStruct((R, D), F32), jax.ShapeDtypeStruct((R, D), BF16), jax.ShapeDtypeStruct((1, D), F32)],
        compiler_params=_params(("arbitrary",)),
    )(dh, x, gain, dres)


def _loss_head(y, target):
    R, D = y.shape
    tr = _tile(R, 256, 8)

    def body(y_ref, t_ref, d_ref, db_ref, l_ref):
        e = y_ref[...] - t_ref[...]
        d = e * (1.0 / D)
        d_ref[...] = d
        db_ref[...] = (0.5 * d).astype(BF16)
        part = jnp.zeros((8, 128), F32) + (0.5 / D) * jnp.sum(e * e)

        @pl.when(pl.program_id(0) == 0)
        def _():
            l_ref[...] = part

        @pl.when(pl.program_id(0) > 0)
        def _():
            l_ref[...] += part

    row = pl.BlockSpec((tr, D), lambda i: (i, 0))
    acc = pl.BlockSpec((8, 128), lambda i: (0, 0))
    return pl.pallas_call(
        body, name="loss_head", grid=(R // tr,), in_specs=[row, row], out_specs=[row, row, acc],
        out_shape=[jax.ShapeDtypeStruct((R, D), F32), jax.ShapeDtypeStruct((R, D), BF16), jax.ShapeDtypeStruct((8, 128), F32)],
        compiler_params=_params(("arbitrary",)),
    )(y, target)


def _head_norm(xs, g):
    r = lax.rsqrt(jnp.mean(xs * xs, axis=-1, keepdims=True) + EPS)
    return xs * r * g


def _head_norm_bwd(xs, g, dy):
    r = lax.rsqrt(jnp.mean(xs * xs, axis=-1, keepdims=True) + EPS)
    xn = xs * r
    uu = dy * g
    return r * (uu - xn * jnp.mean(xn * uu, axis=-1, keepdims=True)), jnp.sum(dy * xn, axis=0, keepdims=True)


NORMED = [(C_FQ, FOX_HEADS, 0), (C_FK, FOX_HEADS, 1), (C_SQ, SWA_HEADS, 2), (C_SK, SWA_KV_HEADS, 3), (C_MQ, MEM_HEADS, 4)]
PLAIN = [(C_FV, FOX_HEADS), (C_SV, SWA_KV_HEADS)]


def _prep_fwd(proj, gains):
    T = proj.shape[0]
    tr = _tile(T, 256, 8)

    def body(p_ref, g_ref, o_ref, lf_ref):
        for start, heads, row in NORMED:
            gn = g_ref[row:row + 1, :]
            for hh in range(heads):
                sl = slice(start + hh * HEAD_DIM, start + (hh + 1) * HEAD_DIM)
                o_ref[:, sl] = _head_norm(p_ref[:, sl], gn).astype(BF16)
        for start, heads in PLAIN:
            sl = slice(start, start + heads * HEAD_DIM)
            o_ref[:, sl] = p_ref[:, sl].astype(BF16)
        zb = p_ref[:, C_FL:C_FL + HEAD_DIM] + g_ref[5:6, :]
        o_ref[:, C_FL:C_FL + HEAD_DIM] = jnp.zeros((tr, HEAD_DIM), BF16)
        lf_ref[...] = jnp.minimum(zb, 0.0) - jnp.log(1.0 + jnp.exp(-jnp.abs(zb)))

    return pl.pallas_call(
        body, name="prep_fwd", grid=(T // tr,),
        in_specs=[pl.BlockSpec((tr, PROJ_W), lambda i: (i, 0)), pl.BlockSpec((8, 128), lambda i: (0, 0))],
        out_specs=[pl.BlockSpec((tr, PROJ_W), lambda i: (i, 0)), pl.BlockSpec((tr, HEAD_DIM), lambda i: (i, 0))],
        out_shape=[jax.ShapeDtypeStruct((T, PROJ_W), BF16), jax.ShapeDtypeStruct((T, HEAD_DIM), F32)],
        compiler_params=_params(("parallel",)),
    )(proj, gains)


def _prep_bwd(proj, gains, dfq, dfk, dfv, dsq, dsk, dsv, dmq, dlogf):
    T = proj.shape[0]
    tr = _tile(T, 256, 8)
    d_normed = {C_FQ: 0, C_FK: 1, C_SQ: 3, C_SK: 4, C_MQ: 6}
    d_plain = {C_FV: 2, C_SV: 5}

    def body(p_ref, g_ref, *rest):
        d_refs, dlf_ref, o_ref, dg_ref = rest[:7], rest[7], rest[8], rest[9]
        rows = []
        for start, heads, row in NORMED:
            gn = g_ref[row:row + 1, :]
            d_ref = d_refs[d_normed[start]]
            tot = jnp.zeros((1, HEAD_DIM), F32)
            for hh in range(heads):
                sl = slice(start + hh * HEAD_DIM, start + (hh + 1) * HEAD_DIM)
                dx, dgn = _head_norm_bwd(p_ref[:, sl], gn, d_ref[:, hh * HEAD_DIM:(hh + 1) * HEAD_DIM])
                o_ref[:, sl] = dx.astype(BF16)
                tot = tot + dgn
            rows.append(tot)
        for start, heads in PLAIN:
            o_ref[:, start:start + heads * HEAD_DIM] = d_refs[d_plain[start]][...].astype(BF16)
        zb = p_ref[:, C_FL:C_FL + HEAD_DIM] + g_ref[5:6, :]
        lane = lax.broadcasted_iota(jnp.int32, (tr, HEAD_DIM), 1)
        dz = jnp.where(lane < FOX_HEADS, dlf_ref[...] * (1.0 - _sigmoid(zb)), 0.0)
        o_ref[:, C_FL:C_FL + HEAD_DIM] = dz.astype(BF16)
        rows.append(jnp.sum(dz, axis=0, keepdims=True))
        part = jnp.concatenate(rows + [jnp.zeros((2, HEAD_DIM), F32)], axis=0)

        @pl.when(pl.program_id(0) == 0)
        def _():
            dg_ref[...] = part

        @pl.when(pl.program_id(0) > 0)
        def _():
            dg_ref[...] += part

    def rows_of(w):
        return pl.BlockSpec((tr, w), lambda i: (i, 0))

    small = pl.BlockSpec((8, 128), lambda i: (0, 0))
    ds = [dfq, dfk, dfv, dsq, dsk, dsv, dmq]
    return pl.pallas_call(
        body, name="prep_bwd", grid=(T // tr,),
        in_specs=[rows_of(PROJ_W), small] + [rows_of(d.shape[1]) for d in ds] + [rows_of(HEAD_DIM)],
        out_specs=[rows_of(PROJ_W), small],
        out_shape=[jax.ShapeDtypeStruct((T, PROJ_W), BF16), jax.ShapeDtypeStruct((8, 128), F32)],
        compiler_params=_params(("arbitrary",)),
    )(proj, gains, *ds, dlogf)


def _head_norm_rows(x, gain):
    R, W = x.shape

    def body(x_ref, g_ref, o_ref):
        for hh in range(W // HEAD_DIM):
            sl = slice(hh * HEAD_DIM, (hh + 1) * HEAD_DIM)
            o_ref[:, sl] = _head_norm(x_ref[:, sl], g_ref[...]).astype(BF16)

    return pl.pallas_call(body, name="mem_k_norm", out_shape=jax.ShapeDtypeStruct((R, W), BF16))(x, gain)


def _head_norm_rows_bwd(x, gain, dy):
    R, W = x.shape

    def body(x_ref, g_ref, dy_ref, dx_ref, dg_ref):
        tot = jnp.zeros((1, HEAD_DIM), F32)
        for hh in range(W // HEAD_DIM):
            sl = slice(hh * HEAD_DIM, (hh + 1) * HEAD_DIM)
            dx, dgn = _head_norm_bwd(x_ref[:, sl], g_ref[...], dy_ref[:, sl])
            dx_ref[:, sl] = dx.astype(BF16)
            tot = tot + dgn
        dg_ref[...] = tot

    return pl.pallas_call(
        body, name="mem_k_norm_bwd",
        out_shape=[jax.ShapeDtypeStruct((R, W), BF16), jax.ShapeDtypeStruct((1, HEAD_DIM), F32)])(x, gain, dy)


def _cumsum_rows(name, x, reverse):
    T, W = x.shape
    tb = _tile(T, 512, 8)
    nb = T // tb

    def body(x_ref, o_ref, carry):
        @pl.when(pl.program_id(0) == 0)
        def _():
            carry[...] = jnp.zeros_like(carry)

        xv = x_ref[...]
        r = lax.broadcasted_iota(jnp.int32, (tb, tb), 0)
        cc = lax.broadcasted_iota(jnp.int32, (tb, tb), 1)
        tri = jnp.where((cc >= r) if reverse else (cc <= r), 1.0, 0.0).astype(F32)
        o_ref[...] = jnp.dot(tri, xv, precision=lax.Precision.HIGHEST, preferred_element_type=F32) + carry[...]
        carry[...] += jnp.sum(xv, axis=0, keepdims=True)

    idx = (lambda i: (nb - 1 - i, 0)) if reverse else (lambda i: (i, 0))
    return pl.pallas_call(
        body, name=name, grid=(nb,), in_specs=[pl.BlockSpec((tb, W), idx)], out_specs=pl.BlockSpec((tb, W), idx),
        out_shape=jax.ShapeDtypeStruct((T, W), F32), scratch_shapes=[pltpu.VMEM((1, W), F32)],
        compiler_params=_params(("arbitrary",)),
    )(x)


def _fox_fwd(qkv, cq, ck):
    T = qkv.shape[0]
    tq = _tile(T, 512)
    nq = T // tq
    HQ, HK, HV = C_FQ // HEAD_DIM, C_FK // HEAD_DIM, C_FV // HEAD_DIM

    def body(q_ref, k_ref, v_ref, cq_ref, ck_ref, o_ref, lse_ref, m_sc, l_sc, acc_sc):
        i, j = pl.program_id(1), pl.program_id(2)

        @pl.when(j == 0)
        def _():
            m_sc[...] = jnp.full_like(m_sc, NEG_INF)
            l_sc[...] = jnp.zeros_like(l_sc)
            acc_sc[...] = jnp.zeros_like(acc_sc)

        @pl.when(j <= i)
        def _():
            s = _dot(q_ref[...], k_ref[...], "nt") * SCALE + (cq_ref[...] - ck_ref[...])
            rows = i * tq + lax.broadcasted_iota(jnp.int32, (tq, tq), 0)
            cols = j * tq + lax.broadcasted_iota(jnp.int32, (tq, tq), 1)
            s = jnp.where(cols <= rows, s, NEG_INF)
            m_new = jnp.maximum(m_sc[...], jnp.max(s, axis=-1, keepdims=True))
            alpha = jnp.exp(m_sc[...] - m_new)
            p = jnp.exp(s - m_new)
            l_sc[...] = alpha * l_sc[...] + jnp.sum(p, axis=-1, keepdims=True)
            acc_sc[...] = alpha * acc_sc[...] + _dot(p.astype(BF16), v_ref[...], "nn")
            m_sc[...] = m_new

        @pl.when(j == i)
        def _():
            o_ref[...] = (acc_sc[...] / l_sc[...]).astype(BF16)
            lse_ref[...] = m_sc[...] + jnp.log(l_sc[...])

    def col(base):
        return pl.BlockSpec((tq, HEAD_DIM), lambda h, i, j: (jnp.minimum(j, i), base + h))

    return pl.pallas_call(
        body, name="fox_fwd", grid=(FOX_HEADS, nq, nq),
        in_specs=[pl.BlockSpec((tq, HEAD_DIM), lambda h, i, j: (i, HQ + h)), col(HK), col(HV),
                  pl.BlockSpec((None, tq, 1), lambda h, i, j: (h, i, 0)),
                  pl.BlockSpec((None, 1, tq), lambda h, i, j: (h, 0, jnp.minimum(j, i)))],
        out_specs=[pl.BlockSpec((tq, HEAD_DIM), lambda h, i, j: (i, h)),
                   pl.BlockSpec((None, tq, 1), lambda h, i, j: (h, i, 0))],
        out_shape=[jax.ShapeDtypeStruct((T, FOX_W), BF16), jax.ShapeDtypeStruct((FOX_HEADS, T, 1), F32)],
        scratch_shapes=[pltpu.VMEM((tq, 1), F32), pltpu.VMEM((tq, 1), F32), pltpu.VMEM((tq, HEAD_DIM), F32)],
        compiler_params=_params(("parallel", "arbitrary", "arbitrary")),
    )(qkv, qkv, qkv, cq, ck)


def _fox_bwd(qkv, cq, ck, out, lse, dmix):
    T = qkv.shape[0]
    tq = _tile(T, 512)
    nq = T // tq
    HQ, HK, HV = C_FQ // HEAD_DIM, C_FK // HEAD_DIM, C_FV // HEAD_DIM

    def body(q_ref, k_ref, v_ref, cq_ref, ck_ref, o_ref, lse_ref, do_ref, dq_ref, dk_ref, dv_ref, dck_ref, dk_sc, dv_sc, dc_sc):
        kj, qi = pl.program_id(1), pl.program_id(2)

        @pl.when(qi == kj)
        def _():
            dk_sc[...] = jnp.zeros_like(dk_sc)
            dv_sc[...] = jnp.zeros_like(dv_sc)
            dc_sc[...] = jnp.zeros_like(dc_sc)

        @pl.when(qi >= kj)
        def _():
            q, k, v, do = q_ref[...], k_ref[...], v_ref[...], do_ref[...]
            s = _dot(q, k, "nt") * SCALE + (cq_ref[...] - ck_ref[...])
            rows = qi * tq + lax.broadcasted_iota(jnp.int32, (tq, tq), 0)
            cols = kj * tq + lax.broadcasted_iota(jnp.int32, (tq, tq), 1)
            p = jnp.where(cols <= rows, jnp.exp(s - lse_ref[...]), 0.0)
            dp = _dot(do, v, "nt")
            delta = jnp.sum(do.astype(F32) * o_ref[...].astype(F32), axis=-1, keepdims=True)
            ds = p * (dp - delta)
            dsb = ds.astype(BF16)
            dv_sc[...] += _dot(p.astype(BF16), do, "tn")
            dk_sc[...] += _dot(dsb, q, "tn")
            dc_sc[...] += jnp.sum(ds, axis=0, keepdims=True)
            dq_part = _dot(dsb, k, "nn") * SCALE
            rows_q = pl.ds(pl.multiple_of(qi * tq, tq), tq)

            @pl.when(kj == 0)
            def _():
                dq_ref[rows_q, :] = dq_part

            @pl.when(kj > 0)
            def _():
                dq_ref[rows_q, :] += dq_part

        @pl.when(qi == nq - 1)
        def _():
            dk_ref[...] = dk_sc[...] * SCALE
            dv_ref[...] = dv_sc[...]
            dck_ref[...] = -dc_sc[...]

    def qrow(base):
        return pl.BlockSpec((tq, HEAD_DIM), lambda h, kj, qi: (jnp.maximum(qi, kj), base + h))

    def krow(base):
        return pl.BlockSpec((tq, HEAD_DIM), lambda h, kj, qi: (kj, base + h))

    qcol = pl.BlockSpec((None, tq, 1), lambda h, kj, qi: (h, jnp.maximum(qi, kj), 0))
    kvec = pl.BlockSpec((None, 1, tq), lambda h, kj, qi: (h, 0, kj))
    sds = jax.ShapeDtypeStruct((T, FOX_W), F32)
    return pl.pallas_call(
        body, name="fox_bwd", grid=(FOX_HEADS, nq, nq),
        in_specs=[qrow(HQ), krow(HK), krow(HV), qcol, kvec, qrow(0), qcol, qrow(0)],
        out_specs=[pl.BlockSpec((T, HEAD_DIM), lambda h, kj, qi: (0, h)), krow(0), krow(0), kvec],
        out_shape=[sds, sds, sds, jax.ShapeDtypeStruct((FOX_HEADS, 1, T), F32)],
        scratch_shapes=[pltpu.VMEM((tq, HEAD_DIM), F32), pltpu.VMEM((tq, HEAD_DIM), F32), pltpu.VMEM((1, tq), F32)],
        compiler_params=_params(("parallel", "arbitrary", "arbitrary")),
    )(qkv, qkv, qkv, cq, ck, out, lse, dmix)


GW = SWA_GROUP * HEAD_DIM
GR = SWA_GROUP * WINDOW


def _swa_scores(q_ref, kp_ref, kc_ref, slope_ref, n):
    q = q_ref[...]
    qs = jnp.concatenate([q[:, t * HEAD_DIM:(t + 1) * HEAD_DIM] for t in range(SWA_GROUP)], axis=0)
    kb = jnp.concatenate([kp_ref[...], kc_ref[...]], axis=0)
    r = lax.broadcasted_iota(jnp.int32, (GR, 2 * WINDOW), 0) & (WINDOW - 1)
    jj = lax.broadcasted_iota(jnp.int32, (GR, 2 * WINDOW), 1)
    dist = WINDOW + r - jj
    valid = (dist >= 0) & (dist < WINDOW) & ((n > 0) | (jj >= WINDOW))
    s = _dot(qs, kb, "nt") * SCALE - slope_ref[...] * dist.astype(F32)
    return qs, kb, jnp.where(valid, s, NEG_INF), valid


def _swa_specs():
    HQ, HK, HV = C_SQ // GW, C_SK // HEAD_DIM, C_SV // HEAD_DIM
    q_spec = pl.BlockSpec((WINDOW, GW), lambda g, n: (n, HQ + g))

    def prev(base):
        return pl.BlockSpec((WINDOW, HEAD_DIM), lambda g, n: (jnp.maximum(n - 1, 0), base + g))

    def cur(base):
        return pl.BlockSpec((WINDOW, HEAD_DIM), lambda g, n: (n, base + g))

    col = pl.BlockSpec((None, GR, 1), lambda g, n: (g, 0, 0))
    return q_spec, prev(HK), cur(HK), prev(HV), cur(HV), col


def _swa_fwd(qkv, slopes, sinks):
    T = qkv.shape[0]
    nb = T // WINDOW
    assert C_SQ % GW == 0

    def body(q_ref, kp_ref, kc_ref, vp_ref, vc_ref, slope_ref, sink_ref, o_ref, lse_ref):
        n = pl.program_id(1)
        _, _, s, _ = _swa_scores(q_ref, kp_ref, kc_ref, slope_ref, n)
        m = jnp.maximum(jnp.max(s, axis=-1, keepdims=True), sink_ref[...])
        p = jnp.exp(s - m)
        l = jnp.sum(p, axis=-1, keepdims=True) + jnp.exp(sink_ref[...] - m)
        vb = jnp.concatenate([vp_ref[...], vc_ref[...]], axis=0)
        o = _dot(p.astype(BF16), vb, "nn") / l
        for t in range(SWA_GROUP):
            o_ref[:, t * HEAD_DIM:(t + 1) * HEAD_DIM] = o[t * WINDOW:(t + 1) * WINDOW, :].astype(BF16)
        lse_ref[...] = m + jnp.log(l)

    q_spec, kp, kc, vp, vc, col = _swa_specs()
    return pl.pallas_call(
        body, name="swa_fwd", grid=(SWA_KV_HEADS, nb), in_specs=[q_spec, kp, kc, vp, vc, col, col],
        out_specs=[pl.BlockSpec((WINDOW, GW), lambda g, n: (n, g)), pl.BlockSpec((None, None, GR, 1), lambda g, n: (g, n, 0, 0))],
        out_shape=[jax.ShapeDtypeStruct((T, SWA_HEADS * HEAD_DIM), BF16), jax.ShapeDtypeStruct((SWA_KV_HEADS, nb, GR, 1), F32)],
        compiler_params=_params(("parallel", "arbitrary")),
    )(qkv, qkv, qkv, qkv, qkv, slopes, sinks)


def _swa_bwd(qkv, slopes, sinks, out, lse, dmix):
    T = qkv.shape[0]
    nb = T // WINDOW
    DO = FOX_W // GW
    assert FOX_W % GW == 0

    def body(q_ref, kp_ref, kc_ref, vp_ref, vc_ref, slope_ref, sink_ref, o_ref, lse_ref, do_ref,
             dq_ref, dk_ref, dv_ref, dsink_ref, sink_sc):
        n = pl.program_id(1)

        @pl.when(n == 0)
        def _():
            dk_ref[...] = jnp.zeros_like(dk_ref)
            dv_ref[...] = jnp.zeros_like(dv_ref)
            sink_sc[...] = jnp.zeros_like(sink_sc)

        qs, kb, s, valid = _swa_scores(q_ref, kp_ref, kc_ref, slope_ref, n)
        lse = lse_ref[...]
        p = jnp.where(valid, jnp.exp(s - lse), 0.0)
        vb = jnp.concatenate([vp_ref[...], vc_ref[...]], axis=0)
        do = jnp.concatenate([do_ref[:, t * HEAD_DIM:(t + 1) * HEAD_DIM] for t in range(SWA_GROUP)], axis=0)
        oo = jnp.concatenate([o_ref[:, t * HEAD_DIM:(t + 1) * HEAD_DIM] for t in range(SWA_GROUP)], axis=0)
        dp = _dot(do, vb, "nt")
        delta = jnp.sum(do.astype(F32) * oo.astype(F32), axis=-1, keepdims=True)
        ds = p * (dp - delta)
        dsb = ds.astype(BF16)
        dq = _dot(dsb, kb, "nn") * SCALE
        for t in range(SWA_GROUP):
            dq_ref[:, t * HEAD_DIM:(t + 1) * HEAD_DIM] = dq[t * WINDOW:(t + 1) * WINDOW, :]
        dkb = _dot(dsb, qs, "tn") * SCALE
        dvb = _dot(p.astype(BF16), do, "tn")
        r_prev = pl.ds(pl.multiple_of(jnp.maximum(n - 1, 0) * WINDOW, WINDOW), WINDOW)
        r_cur = pl.ds(pl.multiple_of(n * WINDOW, WINDOW), WINDOW)
        dk_ref[r_prev, :] += dkb[:WINDOW, :]
        dk_ref[r_cur, :] += dkb[WINDOW:, :]
        dv_ref[r_prev, :] += dvb[:WINDOW, :]
        dv_ref[r_cur, :] += dvb[WINDOW:, :]
        sink_sc[...] -= jnp.exp(sink_ref[...] - lse) * delta

        @pl.when(n == nb - 1)
        def _():
            tot = [jnp.zeros((1, 128), F32) + jnp.sum(sink_sc[t * WINDOW:(t + 1) * WINDOW, :]) for t in range(SWA_GROUP)]
            dsink_ref[...] = jnp.concatenate(tot + [jnp.zeros((8 - SWA_GROUP, 128), F32)], axis=0)

    q_spec, kp, kc, vp, vc, col = _swa_specs()
    kv_acc = pl.BlockSpec((T, HEAD_DIM), lambda g, n: (0, g))
    return pl.pallas_call(
        body, name="swa_bwd", grid=(SWA_KV_HEADS, nb),
        in_specs=[q_spec, kp, kc, vp, vc, col, col, pl.BlockSpec((WINDOW, GW), lambda g, n: (n, g)),
                  pl.BlockSpec((None, None, GR, 1), lambda g, n: (g, n, 0, 0)), pl.BlockSpec((WINDOW, GW), lambda g, n: (n, DO + g))],
        out_specs=[pl.BlockSpec((WINDOW, GW), lambda g, n: (n, g)), kv_acc, kv_acc, pl.BlockSpec((None, 8, 128), lambda g, n: (g, 0, 0))],
        out_shape=[jax.ShapeDtypeStruct((T, SWA_HEADS * HEAD_DIM), F32), jax.ShapeDtypeStruct((T, SWA_KV_HEADS * HEAD_DIM), F32),
                   jax.ShapeDtypeStruct((T, SWA_KV_HEADS * HEAD_DIM), F32), jax.ShapeDtypeStruct((SWA_KV_HEADS, 8, 128), F32)],
        scratch_shapes=[pltpu.VMEM((GR, 1), F32)],
        compiler_params=_params(("parallel", "arbitrary")),
    )(qkv, qkv, qkv, qkv, qkv, slopes, sinks, out, lse, dmix)


def _mem_fwd(qkv, mk, mv):
    T, ML = qkv.shape[0], mk.shape[0]
    tq = _tile(T, 512)
    HQ = C_MQ // HEAD_DIM

    def body(q_ref, k_ref, v_ref, o_ref, lse_ref):
        s = _dot(q_ref[...], k_ref[...], "nt") * SCALE
        m = jnp.max(s, axis=-1, keepdims=True)
        p = jnp.exp(s - m)
        l = jnp.sum(p, axis=-1, keepdims=True)
        o_ref[...] = (_dot(p.astype(BF16), v_ref[...], "nn") / l).astype(BF16)
        lse_ref[...] = m + jnp.log(l)

    kv = pl.BlockSpec((ML, HEAD_DIM), lambda h, i: (0, h))
    return pl.pallas_call(
        body, name="mem_fwd", grid=(MEM_HEADS, T // tq),
        in_specs=[pl.BlockSpec((tq, HEAD_DIM), lambda h, i: (i, HQ + h)), kv, kv],
        out_specs=[pl.BlockSpec((tq, HEAD_DIM), lambda h, i: (i, h)), pl.BlockSpec((None, tq, 1), lambda h, i: (h, i, 0))],
        out_shape=[jax.ShapeDtypeStruct((T, MEM_HEADS * HEAD_DIM), BF16), jax.ShapeDtypeStruct((MEM_HEADS, T, 1), F32)],
        compiler_params=_params(("parallel", "arbitrary")),
    )(qkv, mk, mv)


def _mem_bwd(qkv, mk, mv, out, lse, dmix):
    T, ML = qkv.shape[0], mk.shape[0]
    tq = _tile(T, 512)
    HQ = C_MQ // HEAD_DIM
    DO = (FOX_W + SWA_HEADS * HEAD_DIM) // HEAD_DIM

    def body(q_ref, k_ref, v_ref, o_ref, lse_ref, do_ref, dq_ref, dk_ref, dv_ref):
        q, k, v, do = q_ref[...], k_ref[...], v_ref[...], do_ref[...]
        p = jnp.exp(_dot(q, k, "nt") * SCALE - lse_ref[...])
        dp = _dot(do, v, "nt")
        delta = jnp.sum(do.astype(F32) * o_ref[...].astype(F32), axis=-1, keepdims=True)
        dsb = (p * (dp - delta)).astype(BF16)
        dq_ref[...] = _dot(dsb, k, "nn") * SCALE
        dk_part = _dot(dsb, q, "tn") * SCALE
        dv_part = _dot(p.astype(BF16), do, "tn")

        @pl.when(pl.program_id(1) == 0)
        def _():
            dk_ref[...] = dk_part
            dv_ref[...] = dv_part

        @pl.when(pl.program_id(1) > 0)
        def _():
            dk_ref[...] += dk_part
            dv_ref[...] += dv_part

    kv = pl.BlockSpec((ML, HEAD_DIM), lambda h, i: (0, h))
    qb = pl.BlockSpec((tq, HEAD_DIM), lambda h, i: (i, h))
    return pl.pallas_call(
        body, name="mem_bwd", grid=(MEM_HEADS, T // tq),
        in_specs=[pl.BlockSpec((tq, HEAD_DIM), lambda h, i: (i, HQ + h)), kv, kv, qb,
                  pl.BlockSpec((None, tq, 1), lambda h, i: (h, i, 0)), pl.BlockSpec((tq, HEAD_DIM), lambda h, i: (i, DO + h))],
        out_specs=[qb, kv, kv],
        out_shape=[jax.ShapeDtypeStruct((T, MEM_HEADS * HEAD_DIM), F32), jax.ShapeDtypeStruct((ML, MEM_HEADS * HEAD_DIM), F32),
                   jax.ShapeDtypeStruct((ML, MEM_HEADS * HEAD_DIM), F32)],
        compiler_params=_params(("parallel", "arbitrary")),
    )(qkv, mk, mv, out, lse, dmix)


HBM = pl.BlockSpec(memory_space=pltpu.HBM)


def _place():
    x, y, c = lax.axis_index("x"), lax.axis_index("y"), lax.axis_index("c")
    chips = [(1 - x, y), (x, 1 - y), (1 - x, 1 - y)]
    return x, y, c, chips


def _remote(src, dst, send_sem, recv_sem, device):
    return pltpu.make_async_remote_copy(src_ref=src, dst_ref=dst, send_sem=send_sem, recv_sem=recv_sem,
                                        device_id=device, device_id_type=MESH)


def _gather_weights(shards):
    n = len(shards)

    def body(*refs):
        src, out = refs[:n], refs[n:2 * n]
        send, recv, local = refs[2 * n:]
        x, y, c, chips = _place()
        me = 2 * x + y
        ids = [2 * cx + cy for cx, cy in chips]
        copies, kept = [], []
        for a in range(n):
            half = src[a].shape[0] // 2
            rows = pl.ds(c * half, half)
            mine = pltpu.make_async_copy(src[a], out[a].at[me], local.at[a])
            mine.start()
            kept.append(mine)
            for j, (cx, cy) in enumerate(chips):
                cp = _remote(src[a].at[rows], out[a].at[me, rows], send.at[a, j], recv.at[a, j], (cx, cy, c))
                cp.start()
                copies.append(cp)
        for a in range(n):
            half = src[a].shape[0] // 2
            rows = pl.ds(c * half, half)
            for j in range(3):
                landed = out[a].at[ids[j], rows]
                _remote(landed, landed, send.at[a, j], recv.at[a, j], (x, y, c)).wait_recv()
                cp = _remote(landed, landed, send.at[a, 3 + j], recv.at[a, 3 + j], (x, y, 1 - c))
                cp.start()
                copies.append(cp)
        for a in range(n):
            half = src[a].shape[0] // 2
            other = pl.ds((1 - c) * half, half)
            for j in range(3):
                landed = out[a].at[ids[j], other]
                _remote(landed, landed, send.at[a, 3 + j], recv.at[a, 3 + j], (x, y, c)).wait_recv()
        for cp in copies:
            cp.wait_send()
        for mine in kept:
            mine.wait()

    return pl.pallas_call(
        body, name="gather_weights", in_specs=[HBM] * n, out_specs=[HBM] * n,
        out_shape=[jax.ShapeDtypeStruct((N_CHIPS,) + s.shape, s.dtype) for s in shards],
        scratch_shapes=[pltpu.SemaphoreType.DMA((n, 6)), pltpu.SemaphoreType.DMA((n, 6)), pltpu.SemaphoreType.DMA((n,))],
    )(*shards)


def _pair_exchange(grads):
    n = len(grads)

    def body(*refs):
        src, mine, theirs = refs[:n], refs[n:2 * n], refs[2 * n:3 * n]
        send, recv, local = refs[3 * n:]
        x, y, c, chips = _place()
        order = [2 * x + y] + [2 * cx + cy for cx, cy in chips]
        copies = []
        for a in range(n):
            half = src[a].shape[1] // 2
            for j in range(N_CHIPS):
                keep = pltpu.make_async_copy(src[a].at[order[j], pl.ds(c * half, half)], mine[a].at[j], local.at[a, j])
                keep.start()
                cp = _remote(src[a].at[order[j], pl.ds((1 - c) * half, half)], theirs[a].at[j], send.at[a, j], recv.at[a, j], (x, y, 1 - c))
                cp.start()
                copies.append((keep, cp))
        for keep, cp in copies:
            keep.wait()
            cp.wait()

    half_shapes = [jax.ShapeDtypeStruct((N_CHIPS, g.shape[1] // 2, g.shape[2]), g.dtype) for g in grads]
    res = pl.pallas_call(
        body, name="grad_pair_exchange", in_specs=[HBM] * n, out_specs=[HBM] * (2 * n), out_shape=half_shapes * 2,
        scratch_shapes=[pltpu.SemaphoreType.DMA((n, N_CHIPS)), pltpu.SemaphoreType.DMA((n, N_CHIPS)), pltpu.SemaphoreType.DMA((n, N_CHIPS))],
    )(*grads)
    return res[:n], res[n:]


def _chip_exchange(parts):
    n = len(parts)

    def body(*refs):
        src, dst = refs[:n], refs[n:2 * n]
        send, recv = refs[2 * n:]
        x, y, c, chips = _place()
        copies = []
        for a in range(n):
            for j, (cx, cy) in enumerate(chips):
                cp = _remote(src[a].at[j], dst[a].at[j], send.at[a, j], recv.at[a, j], (cx, cy, c))
                cp.start()
                copies.append(cp)
        for cp in copies:
            cp.wait()

    return pl.pallas_call(
        body, name="grad_chip_exchange", in_specs=[HBM] * n, out_specs=[HBM] * n,
        out_shape=[jax.ShapeDtypeStruct(p.shape, p.dtype) for p in parts],
        scratch_shapes=[pltpu.SemaphoreType.DMA((n, 3)), pltpu.SemaphoreType.DMA((n, 3))],
    )(*parts)


def _pair_share(halves):
    n = len(halves)

    def body(*refs):
        src, dst = refs[:n], refs[n:2 * n]
        send, recv, local = refs[2 * n:]
        x, y, c, _ = _place()
        copies = []
        for a in range(n):
            half = src[a].shape[0]
            rows = pl.ds(c * half, half)
            keep = pltpu.make_async_copy(src[a], dst[a].at[rows], local.at[a])
            keep.start()
            cp = _remote(src[a], dst[a].at[rows], send.at[a], recv.at[a], (x, y, 1 - c))
            cp.start()
            copies.append((keep, cp, half))
        for a, (keep, cp, half) in enumerate(copies):
            keep.wait()
            cp.wait_send()
            theirs = dst[a].at[pl.ds((1 - c) * half, half)]
            _remote(theirs, theirs, send.at[a], recv.at[a], (x, y, c)).wait_recv()

    return pl.pallas_call(
        body, name="grad_pair_share", in_specs=[HBM] * n, out_specs=[HBM] * n,
        out_shape=[jax.ShapeDtypeStruct((2 * h.shape[0], h.shape[1]), h.dtype) for h in halves],
        scratch_shapes=[pltpu.SemaphoreType.DMA((n,)), pltpu.SemaphoreType.DMA((n,)), pltpu.SemaphoreType.DMA((n,))],
    )(*halves)


def _all_reduce_small(buf):
    R, W = buf.shape

    def body(src_ref, out_ref, slots, send, recv):
        x, y, c, _ = _place()
        me = 4 * x + 2 * y + c
        copies = []
        for dx in range(2):
            for dy in range(2):
                for dc in range(2):
                    if dx == dy == dc == 0:
                        continue
                    k = 4 * dx + 2 * dy + dc
                    peer = (x ^ dx, y ^ dy, c ^ dc)
                    cp = _remote(src_ref, slots.at[me], send.at[k], recv.at[k], peer)
                    cp.start()
                    copies.append((cp, k))
        slots[me] = src_ref[...]
        for cp, k in copies:
            cp.wait_send()
            landed = slots.at[me ^ k]
            _remote(landed, landed, send.at[k], recv.at[k], (x, y, c)).wait_recv()
        total = slots[0]
        for d in range(1, N_DEV):
            total = total + slots[d]
        out_ref[...] = total

    return pl.pallas_call(
        body, name="all_reduce_small", in_specs=[pl.BlockSpec(memory_space=pltpu.VMEM)],
        out_specs=pl.BlockSpec(memory_space=pltpu.VMEM), out_shape=jax.ShapeDtypeStruct((R, W), F32),
        scratch_shapes=[pltpu.VMEM((N_DEV, R, W), F32), pltpu.SemaphoreType.DMA((N_DEV,)), pltpu.SemaphoreType.DMA((N_DEV,))],
    )(buf)


def _pair_sum_bf16(name, mine, theirs):
    _, R2, C = mine.shape
    tr = _tile(R2, 256, 16)
    spec_in = pl.BlockSpec((None, tr, C), lambda j, i: (j + 1, i, 0))
    spec_out = pl.BlockSpec((None, tr, C), lambda j, i: (j, i, 0))

    def body(a_ref, b_ref, o_ref):
        o_ref[...] = (a_ref[...] + b_ref[...]).astype(BF16)

    return pl.pallas_call(
        body, name=name, grid=(3, R2 // tr), in_specs=[spec_in, spec_in], out_specs=spec_out,
        out_shape=jax.ShapeDtypeStruct((3, R2, C), BF16), compiler_params=_params(("parallel", "parallel")),
    )(mine, theirs)


def _chip_sum(name, mine, theirs, arrived):
    _, R2, C = mine.shape
    tr = _tile(R2, 256, 16)
    own = pl.BlockSpec((None, tr, C), lambda i: (0, i, 0))
    others = pl.BlockSpec((3, tr, C), lambda i: (0, i, 0))

    def body(a_ref, b_ref, r_ref, o_ref):
        tot = a_ref[...] + b_ref[...]
        for j in range(3):
            tot = tot + r_ref[j].astype(F32)
        o_ref[...] = tot

    return pl.pallas_call(
        body, name=name, grid=(R2 // tr,), in_specs=[own, own, others], out_specs=pl.BlockSpec((tr, C), lambda i: (i, 0)),
        out_shape=jax.ShapeDtypeStruct((R2, C), F32), compiler_params=_params(("parallel",)),
    )(mine, theirs, arrived)


def _adamw(name, w, g, m, v):
    R, C = w.shape
    tr = _tile(R, 128, 8)
    c1 = 1.0 / (1.0 - ADAM_B1 ** ADAM_STEP)
    c2 = 1.0 / (1.0 - ADAM_B2 ** ADAM_STEP)

    def body(w_ref, g_ref, m_ref, v_ref, d_ref, mo_ref, vo_ref):
        gv = g_ref[...]
        mn = ADAM_B1 * m_ref[...] + (1.0 - ADAM_B1) * gv
        vn = ADAM_B2 * v_ref[...] + (1.0 - ADAM_B2) * (gv * gv)
        d_ref[...] = -ADAM_LR * ((mn * c1) / (jnp.sqrt(vn * c2) + ADAM_EPS) + ADAM_WD * w_ref[...])
        mo_ref[...] = mn
        vo_ref[...] = vn

    spec = pl.BlockSpec((tr, C), lambda i: (i, 0))
    sds = jax.ShapeDtypeStruct((R, C), F32)
    return pl.pallas_call(body, name=name, grid=(R // tr,), in_specs=[spec] * 4, out_specs=[spec] * 3, out_shape=[sds] * 3,
                          compiler_params=_params(("parallel",)))(w, g, m, v)


SMALL = ["ffn1_norm", "mix_norm", "mem_norm", "forget_bias", "fox_q_gain", "fox_k_gain", "swa_q_gain", "swa_k_gain", "swa_sinks",
         "mem_q_gain", "mem_k_gain", "ffn2_norm"]
LARGE = ["ffn1_gate", "ffn1_up", "ffn1_down", "w_in", "w_mem_k", "w_mem_v", "w_out", "ffn2_gate", "ffn2_up", "ffn2_down"]
WEIGHTS = ["ffn1_norm", "ffn1_gate", "ffn1_up", "ffn1_down", "mix_norm", "mem_norm", "w_in", "forget_bias", "w_mem_k", "w_mem_v",
           "fox_q_gain", "fox_k_gain", "swa_q_gain", "swa_k_gain", "swa_sinks", "mem_q_gain", "mem_k_gain", "w_out", "ffn2_norm",
           "ffn2_gate", "ffn2_up", "ffn2_down"]


def _pad_proj_cols(w):
    out = jnp.zeros((w.shape[0], PROJ_W), w.dtype)
    for start, width, pstart in REF_GROUPS:
        out = lax.dynamic_update_slice(out, w[:, start:start + width], (0, pstart))
    return out


def _unpad_proj_cols(w):
    return jnp.concatenate([w[:, pstart:pstart + width] for _, width, pstart in REF_GROUPS], axis=1)


def _pack_small(vals):
    flat = jnp.concatenate([vals[k].reshape(-1).astype(F32) for k in SMALL + ["loss"]])
    n = flat.shape[0]
    total = -(-n // 1024) * 1024
    return jnp.pad(flat, (0, total - n)).reshape(total // 128, 128)


def _unpack_small(buf, shapes):
    flat = buf.reshape(-1)
    out, off = {}, 0
    for k in SMALL + ["loss"]:
        size = int(np.prod(shapes[k]))
        out[k] = flat[off:off + size].reshape(shapes[k])
        off += size
    return out


def kernel(x, mem, ffn1_norm, ffn1_gate, ffn1_up, ffn1_down, mix_norm, mem_norm, w_in, forget_bias, w_mem_k, w_mem_v, fox_q_gain, fox_k_gain, swa_q_gain, swa_k_gain, swa_sinks, mem_q_gain, mem_k_gain, w_out, ffn2_norm, ffn2_gate, ffn2_up, ffn2_down, loss_target, m_ffn1_norm, m_ffn1_gate, m_ffn1_up, m_ffn1_down, m_mix_norm, m_mem_norm, m_w_in, m_forget_bias, m_w_mem_k, m_w_mem_v, m_fox_q_gain, m_fox_k_gain, m_swa_q_gain, m_swa_k_gain, m_swa_sinks, m_mem_q_gain, m_mem_k_gain, m_w_out, m_ffn2_norm, m_ffn2_gate, m_ffn2_up, m_ffn2_down, v_ffn1_norm, v_ffn1_gate, v_ffn1_up, v_ffn1_down, v_mix_norm, v_mem_norm, v_w_in, v_forget_bias, v_w_mem_k, v_w_mem_v, v_fox_q_gain, v_fox_k_gain, v_swa_q_gain, v_swa_k_gain, v_swa_sinks, v_mem_q_gain, v_mem_k_gain, v_w_out, v_ffn2_norm, v_ffn2_gate, v_ffn2_up, v_ffn2_down):
    given = dict(locals())
    T, D = x.shape[1], x.shape[2]
    ML = mem.shape[1]
    xin = x.reshape(T, D)
    target = loss_target.reshape(T, D)
    memin = mem.reshape(ML, D)

    shard = {k: given[k][0] for k in LARGE}
    shard["w_in"] = _pad_proj_cols(shard["w_in"])
    full = dict(zip(LARGE, _gather_weights([shard[k].astype(BF16) for k in LARGE])))
    wg1, wu1, wg2, wu2 = full["ffn1_gate"], full["ffn1_up"], full["ffn2_gate"], full["ffn2_up"]
    wd1 = full["ffn1_down"].reshape(-1, D)
    wd2 = full["ffn2_down"].reshape(-1, D)
    win = full["w_in"].reshape(D, PROJ_W)
    wmk = full["w_mem_k"].reshape(D, MEM_HEADS * HEAD_DIM)
    wmv = full["w_mem_v"].reshape(D, MEM_HEADS * HEAD_DIM)
    wo = full["w_out"].reshape(-1, D)

    gains = jnp.concatenate([fox_q_gain, fox_k_gain, swa_q_gain, swa_k_gain, mem_q_gain,
                             jnp.pad(forget_bias, ((0, 0), (0, HEAD_DIM - FOX_HEADS))), jnp.zeros((2, HEAD_DIM), F32)], axis=0)
    slopes_np = 2.0 ** (-8.0 * np.arange(1, SWA_HEADS + 1) / SWA_HEADS)
    slopes = jnp.asarray(np.repeat(slopes_np, WINDOW).reshape(SWA_KV_HEADS, GR, 1), F32)
    sinks = jnp.repeat(swa_sinks.reshape(SWA_HEADS), WINDOW).reshape(SWA_KV_HEADS, GR, 1)

    h1 = _rms_fwd("ffn1_norm_fwd", xin, ffn1_norm)
    g1, u1, a1 = _ffn_gu("ffn1_gate_up", h1, wg1, wu1)
    x1 = _ffn_down("ffn1_down", a1, wd1, xin)
    h2 = _rms_fwd("mix_norm_fwd", x1, mix_norm)
    proj = _mm2d("proj_in", h2, win, "nn", F32, tn=1408)
    qkv, logf = _prep_fwd(proj, gains)
    cum = _cumsum_rows("forget_cumsum", logf, False)
    cum_h = cum[:, :FOX_HEADS].T
    cq, ck = cum_h.reshape(FOX_HEADS, T, 1), cum_h.reshape(FOX_HEADS, 1, T)
    mn = _rms_fwd("mem_norm_fwd", memin, mem_norm)
    mk_raw = _mm2d("mem_k_proj", mn, wmk, "nn", F32)
    mv = _mm2d("mem_v_proj", mn, wmv, "nn", BF16)
    mk = _head_norm_rows(mk_raw, mem_k_gain)
    out_a, lse_a = _fox_fwd(qkv, cq, ck)
    out_b, lse_b = _swa_fwd(qkv, slopes, sinks)
    out_c, lse_c = _mem_fwd(qkv, mk, mv)
    mixed = jnp.concatenate([out_a, out_b, out_c], axis=1)
    x2 = _mm2d("mix_out", mixed, wo, "nn", F32, extras=[x1], epilogue=lambda accs, ex: [ex[0] + accs[0]])
    h3 = _rms_fwd("ffn2_norm_fwd", x2, ffn2_norm)
    g2, u2, a2 = _ffn_gu("ffn2_gate_up", h3, wg2, wu2)
    x3 = _ffn_down("ffn2_down", a2, wd2, x2)
    dx3, dyb3, loss_part = _loss_head(x3, target)

    grads, small = {}, {"loss": loss_part[0, 0]}
    grads["ffn2_gate"], grads["ffn2_up"], dwd2, dh3 = _ffn_bwd("ffn2", dyb3, h3, g2, u2, a2, wg2, wu2, wd2)
    grads["ffn2_down"] = dwd2.reshape(N_CHIPS, -1, D)
    dx2, dx2b, small["ffn2_norm"] = _rms_bwd("ffn2_norm_bwd", dh3, x2, ffn2_norm, dx3, 1.0)
    dmix = _mm2d("mix_out_dx", dx2b, wo, "nt", BF16)
    grads["w_out"] = _mm2d("mix_out_dw", mixed, dx2b, "tn", F32, tk=512).reshape(N_CHIPS, -1, D)
    dfq, dfk, dfv, dck = _fox_bwd(qkv, cq, ck, out_a, lse_a, dmix)
    dsq, dsk, dsv, dsink = _swa_bwd(qkv, slopes, sinks, out_b, lse_b, dmix)
    dmq, dmk, dmv = _mem_bwd(qkv, mk, mv, out_c, lse_c, dmix)
    small["swa_sinks"] = dsink[:, :SWA_GROUP, 0].reshape(1, SWA_HEADS)
    dcum = jnp.pad(dck.reshape(FOX_HEADS, T).T, ((0, 0), (0, HEAD_DIM - FOX_HEADS)))
    dlogf = _cumsum_rows("forget_cumsum_bwd", dcum, True)
    dproj, dgains = _prep_bwd(proj, gains, dfq, dfk, dfv, dsq, dsk, dsv, dmq, dlogf)
    for row, k in enumerate(["fox_q_gain", "fox_k_gain", "swa_q_gain", "swa_k_gain", "mem_q_gain"]):
        small[k] = dgains[row:row + 1, :]
    small["forget_bias"] = dgains[5:6, :FOX_HEADS]
    grads["w_in"] = _mm2d("proj_in_dw", h2, dproj, "tn", F32, tn=1408, tk=512).reshape(N_CHIPS, -1, PROJ_W)
    dh2 = _mm2d("proj_in_dx", dproj, win, "nt", F32, tk=1408)
    dx1, dyb1, small["mix_norm"] = _rms_bwd("mix_norm_bwd", dh2, x1, mix_norm, dx2, 0.5)
    dmk_raw, small["mem_k_gain"] = _head_norm_rows_bwd(mk_raw, mem_k_gain, dmk)
    dmvb = dmv.astype(BF16)
    grads["w_mem_k"] = _mm2d("mem_k_dw", mn, dmk_raw, "tn", F32).reshape(N_CHIPS, -1, MEM_HEADS * HEAD_DIM)
    grads["w_mem_v"] = _mm2d("mem_v_dw", mn, dmvb, "tn", F32).reshape(N_CHIPS, -1, MEM_HEADS * HEAD_DIM)
    dmn = _mm2d("mem_k_dx", dmk_raw, wmk, "nt", F32)
    dmn = _mm2d("mem_v_dx", dmvb, wmv, "nt", F32, extras=[dmn], epilogue=lambda accs, ex: [ex[0] + accs[0]])
    _, _, small["mem_norm"] = _rms_bwd("mem_norm_bwd", dmn, memin, mem_norm, jnp.zeros_like(memin), 1.0)
    grads["ffn1_gate"], grads["ffn1_up"], dwd1, dh1 = _ffn_bwd("ffn1", dyb1, h1, g1, u1, a1, wg1, wu1, wd1)
    grads["ffn1_down"] = dwd1.reshape(N_CHIPS, -1, D)
    grad_x, _, small["ffn1_norm"] = _rms_bwd("ffn1_norm_bwd", dh1, xin, ffn1_norm, dx1, 1.0)

    mine, theirs = _pair_exchange([grads[k] for k in LARGE])
    to_chips = [_pair_sum_bf16("pair_sum_" + k, a, b) for k, a, b in zip(LARGE, mine, theirs)]
    arrived = _chip_exchange(to_chips)
    halves = [_chip_sum("chip_sum_" + k, a, b, r) for k, a, b, r in zip(LARGE, mine, theirs, arrived)]
    reduced = dict(zip(LARGE, _pair_share(halves)))
    reduced["w_in"] = _unpad_proj_cols(reduced["w_in"])

    shapes = {k: given[k].shape for k in SMALL}
    shapes["loss"] = ()
    red_small = _unpack_small(_all_reduce_small(_pack_small(small)), shapes)
    loss = red_small["loss"]

    res = {}
    for k in LARGE:
        gk = reduced[k]
        d, mo, vo = _adamw("adamw_" + k, given[k][0], gk, given["m_" + k][0], given["v_" + k][0])
        res[k] = tuple(t[None] for t in (gk, d, mo, vo))
    zero = {"loss": jnp.zeros((), F32)}
    packed = [_pack_small({**zero, **{k: src[k] for k in SMALL}}) for src in (
        {k: given[k] for k in SMALL}, red_small, {k: given["m_" + k] for k in SMALL}, {k: given["v_" + k] for k in SMALL})]
    d_s, m_s, v_s = (_unpack_small(t, shapes) for t in _adamw("adamw_small", *packed))
    for k in SMALL:
        res[k] = (red_small[k], d_s[k], m_s[k], v_s[k])

    outs = [loss, grad_x.reshape(1, T, D)]
    for part in range(4):
        outs += [res[k][part] for k in WEIGHTS]
    return tuple(outs)
```

```python
import functools

import numpy as np
import jax
import jax.numpy as jnp
from jax import lax
from jax.experimental import pallas as pl
from jax.experimental.pallas import tpu as pltpu

F32 = jnp.float32
BF16 = jnp.bfloat16
MESH = pl.DeviceIdType.MESH

HEAD_DIM = 128
FOX_HEADS = 6
SWA_HEADS = 6
SWA_KV_HEADS = 2
SWA_GROUP = SWA_HEADS // SWA_KV_HEADS
MEM_HEADS = 4
WINDOW = 128
EPS = 1e-6
NEG_INF = -1e30
SCALE = HEAD_DIM ** -0.5

C_FQ = 0
C_FK = C_FQ + FOX_HEADS * HEAD_DIM
C_FV = C_FK + FOX_HEADS * HEAD_DIM
C_SQ = C_FV + FOX_HEADS * HEAD_DIM
C_SK = C_SQ + SWA_HEADS * HEAD_DIM
C_SV = C_SK + SWA_KV_HEADS * HEAD_DIM
C_MQ = C_SV + SWA_KV_HEADS * HEAD_DIM
C_FL = C_MQ + MEM_HEADS * HEAD_DIM
PROJ_W = C_FL + HEAD_DIM
FOX_W = FOX_HEADS * HEAD_DIM
REF_GROUPS = [
    (0, FOX_W, C_FQ), (FOX_W, FOX_W, C_FK), (2 * FOX_W, FOX_W, C_FV), (3 * FOX_W, FOX_HEADS, C_FL),
    (3 * FOX_W + FOX_HEADS, SWA_HEADS * HEAD_DIM, C_SQ),
    (3 * FOX_W + FOX_HEADS + SWA_HEADS * HEAD_DIM, SWA_KV_HEADS * HEAD_DIM, C_SK),
    (3 * FOX_W + FOX_HEADS + (SWA_HEADS + SWA_KV_HEADS) * HEAD_DIM, SWA_KV_HEADS * HEAD_DIM, C_SV),
    (3 * FOX_W + FOX_HEADS + (SWA_HEADS + 2 * SWA_KV_HEADS) * HEAD_DIM, MEM_HEADS * HEAD_DIM, C_MQ),
]

ADAM_LR = 0.001
ADAM_B1 = 0.9
ADAM_B2 = 0.999
ADAM_EPS = 1e-08
ADAM_WD = 0.01
ADAM_STEP = 10

V7X_VMEM_LIMIT = 56 * 1024 * 1024
N_CHIPS = 4
N_DEV = 8


def _tile(n, pref, mult=128):
    t = (min(pref, n) // mult) * mult
    while t >= mult:
        if n % t == 0:
            return t
        t -= mult
    return n


def _params(sem):
    return pltpu.CompilerParams(dimension_semantics=sem, vmem_limit_bytes=V7X_VMEM_LIMIT)


_DIMS = {"nn": (((1,), (0,)), ((), ())), "nt": (((1,), (1,)), ((), ())), "tn": (((0,), (0,)), ((), ()))}


def _dot(a, b, mode):
    return lax.dot_general(a, b, _DIMS[mode], preferred_element_type=F32)


def _mm(name, grid, pairs, acc_of, acc_shapes, extras, outs, epilogue):
    n_p, n_e, n_o, n_a = len(pairs), len(extras), len(outs), len(acc_shapes)
    nk = grid[2]

    def body(*refs):
        ab = refs[:2 * n_p]
        ex = refs[2 * n_p:2 * n_p + n_e]
        out = refs[2 * n_p + n_e:2 * n_p + n_e + n_o]
        accs = refs[2 * n_p + n_e + n_o:]
        parts = [None] * n_a
        for p in range(n_p):
            d = _dot(ab[2 * p][...], ab[2 * p + 1][...], pairs[p][4])
            parts[acc_of[p]] = d if parts[acc_of[p]] is None else parts[acc_of[p]] + d

        def finish(vals):
            for o, r in zip(out, epilogue(vals, [e[...] for e in ex])):
                o[...] = r.astype(o.dtype)

        if nk == 1:
            finish(parts)
            return
        k = pl.program_id(2)

        @pl.when(k == 0)
        def _():
            for a, d in zip(accs, parts):
                a[...] = d

        @pl.when(k > 0)
        def _():
            for a, d in zip(accs, parts):
                a[...] += d

        @pl.when(k == nk - 1)
        def _():
            finish([a[...] for a in accs])

    in_specs, args = [], []
    for a, a_spec, b, b_spec, _ in pairs:
        in_specs += [a_spec, b_spec]
        args += [a, b]
    for e, e_spec in extras:
        in_specs.append(e_spec)
        args.append(e)
    res = pl.pallas_call(
        body, name=name, grid=grid, in_specs=in_specs,
        out_specs=[s for _, s in outs], out_shape=[o for o, _ in outs],
        scratch_shapes=[pltpu.VMEM(s, F32) for s in acc_shapes] if nk > 1 else [],
        compiler_params=_params(("parallel", "parallel", "arbitrary")),
    )(*args)
    return res


def _mm2d(name, a, b, mode, out_dtype, tm=512, tn=1024, tk=1024, extras=(), epilogue=None, n_out=1):
    if mode == "nn":
        (M, K), N = a.shape, b.shape[1]
    elif mode == "nt":
        (M, K), N = a.shape, b.shape[0]
    else:
        (K, M), N = a.shape, b.shape[1]
    tm, tn, tk = _tile(M, tm), _tile(N, tn), _tile(K, tk)
    a_spec = pl.BlockSpec((tk, tm), lambda i, j, k: (k, i)) if mode == "tn" else pl.BlockSpec((tm, tk), lambda i, j, k: (i, k))
    b_spec = pl.BlockSpec((tn, tk), lambda i, j, k: (j, k)) if mode == "nt" else pl.BlockSpec((tk, tn), lambda i, j, k: (k, j))
    mn = pl.BlockSpec((tm, tn), lambda i, j, k: (i, j))
    if epilogue is None:
        epilogue = lambda accs, ex: [accs[0]]
    if not isinstance(out_dtype, (list, tuple)):
        out_dtype = [out_dtype] * n_out
    res = _mm(name, (M // tm, N // tn, K // tk), [(a, a_spec, b, b_spec, mode)], [0], [(tm, tn)],
              [(e, mn) for e in extras], [(jax.ShapeDtypeStruct((M, N), d), mn) for d in out_dtype], epilogue)
    return res[0] if len(res) == 1 else res


def _sigmoid(x):
    return 1.0 / (1.0 + jnp.exp(-x))


def _ffn_gu(name, h, wg, wu):
    T, D = h.shape
    nf, _, F4 = wg.shape
    tm, tk = _tile(T, 512), _tile(D, 1024)
    a_spec = pl.BlockSpec((tm, tk), lambda i, j, k: (i, k))
    b_spec = pl.BlockSpec((None, tk, F4), lambda i, j, k: (j, k, 0))
    o_spec = pl.BlockSpec((tm, F4), lambda i, j, k: (i, j))

    def epilogue(accs, ex):
        g, u = accs
        return [g, u, g * _sigmoid(g) * u]

    sds = jax.ShapeDtypeStruct((T, nf * F4), BF16)
    return _mm(name, (T // tm, nf, D // tk), [(h, a_spec, wg, b_spec, "nn"), (h, a_spec, wu, b_spec, "nn")], [0, 1],
               [(tm, F4), (tm, F4)], [], [(sds, o_spec)] * 3, epilogue)


def _ffn_down(name, a, wd, xres):
    return _mm2d(name, a, wd, "nn", F32, tk=wd.shape[0] // N_CHIPS, extras=[xres],
                 epilogue=lambda accs, ex: [ex[0] + 0.5 * accs[0]])


def _ffn_bwd(tag, dyb, h, g, u, a, wg, wu, wd):
    T, D = h.shape
    nf, _, F4 = wg.shape

    def act_bwd(accs, ex):
        gf, uf = ex[0].astype(F32), ex[1].astype(F32)
        s = _sigmoid(gf)
        return [accs[0] * uf * s * (1.0 + gf * (1.0 - s)), accs[0] * gf * s]

    dg, du = _mm2d(tag + "_da", dyb, wd, "nt", BF16, tn=F4, extras=[g, u], epilogue=act_bwd, n_out=2)
    dwd = _mm2d(tag + "_dwd", a, dyb, "tn", F32, tm=F4, tk=512)

    tm, tk = _tile(D, 512), _tile(T, 512)
    h_spec = pl.BlockSpec((tk, tm), lambda i, j, k: (k, i))
    d_spec = pl.BlockSpec((tk, F4), lambda i, j, k: (k, j))
    w_spec = pl.BlockSpec((None, tm, F4), lambda i, j, k: (j, i, 0))
    sds = jax.ShapeDtypeStruct((nf, D, F4), F32)
    dwg, dwu = _mm(tag + "_dwgu", (D // tm, nf, T // tk), [(h, h_spec, dg, d_spec, "tn"), (h, h_spec, du, d_spec, "tn")],
                   [0, 1], [(tm, F4), (tm, F4)], [], [(sds, w_spec)] * 2, lambda accs, ex: accs)

    tm, tn = _tile(T, 512), _tile(D, 1024)
    a_spec = pl.BlockSpec((tm, F4), lambda i, j, k: (i, k))
    b_spec = pl.BlockSpec((None, tn, F4), lambda i, j, k: (k, j, 0))
    o_spec = pl.BlockSpec((tm, tn), lambda i, j, k: (i, j))
    (dh,) = _mm(tag + "_dh", (T // tm, D // tn, nf), [(dg, a_spec, wg, b_spec, "nt"), (du, a_spec, wu, b_spec, "nt")],
                [0, 0], [(tm, tn)], [], [(jax.ShapeDtypeStruct((T, D), F32), o_spec)], lambda accs, ex: accs)
    return dwg, dwu, dwd, dh


def _rms_fwd(name, x, gain):
    R, D = x.shape
    tr = _tile(R, 256, 8)

    def body(x_ref, g_ref, o_ref):
        xv = x_ref[...]
        r = lax.rsqrt(jnp.mean(xv * xv, axis=-1, keepdims=True) + EPS)
        o_ref[...] = (xv * r * g_ref[...]).astype(BF16)

    return pl.pallas_call(
        body, name=name, grid=(R // tr,),
        in_specs=[pl.BlockSpec((tr, D), lambda i: (i, 0)), pl.BlockSpec((1, D), lambda i: (0, 0))],
        out_specs=pl.BlockSpec((tr, D), lambda i: (i, 0)), out_shape=jax.ShapeDtypeStruct((R, D), BF16),
        compiler_params=_params(("parallel",)),
    )(x, gain)


def _rms_bwd(name, dh, x, gain, dres, bscale):
    R, D = x.shape
    tr = _tile(R, 256, 8)

    def body(dh_ref, x_ref, g_ref, dres_ref, dx_ref, dxb_ref, dg_ref):
        xv, dy = x_ref[...], dh_ref[...]
        r = lax.rsqrt(jnp.mean(xv * xv, axis=-1, keepdims=True) + EPS)
        xn = xv * r
        uu = dy * g_ref[...]
        dx = dres_ref[...] + r * (uu - xn * jnp.mean(xn * uu, axis=-1, keepdims=True))
        dx_ref[...] = dx
        dxb_ref[...] = (bscale * dx).astype(BF16)
        part = jnp.sum(dy * xn, axis=0, keepdims=True)

        @pl.when(pl.program_id(0) == 0)
        def _():
            dg_ref[...] = part

        @pl.when(pl.program_id(0) > 0)
        def _():
            dg_ref[...] += part

    row = pl.BlockSpec((tr, D), lambda i: (i, 0))
    vec = pl.BlockSpec((1, D), lambda i: (0, 0))
    return pl.pallas_call(
        body, name=name, grid=(R // tr,), in_specs=[row, row, vec, row], out_specs=[row, row, vec],
        out_shape=[jax.ShapeDtypeStruct((R, D), F32), jax.ShapeDtypeStruct((R, D), BF16), jax.ShapeDtypeStruct((1, D), F32)],
        compiler_params=_params(("arbitrary",)),
    )(dh, x, gain, dres)


def _loss_head(y, target):
    R, D = y.shape
    tr = _tile(R, 256, 8)

    def body(y_ref, t_ref, d_ref, db_ref, l_ref):
        e = y_ref[...] - t_ref[...]
        d = e * (1.0 / D)
        d_ref[...] = d
        db_ref[...] = (0.5 * d).astype(BF16)
        part = jnp.zeros((8, 128), F32) + (0.5 / D) * jnp.sum(e * e)

        @pl.when(pl.program_id(0) == 0)
        def _():
            l_ref[...] = part

        @pl.when(pl.program_id(0) > 0)
        def _():
            l_ref[...] += part

    row = pl.BlockSpec((tr, D), lambda i: (i, 0))
    acc = pl.BlockSpec((8, 128), lambda i: (0, 0))
    return pl.pallas_call(
        body, name="loss_head", grid=(R // tr,), in_specs=[row, row], out_specs=[row, row, acc],
        out_shape=[jax.ShapeDtypeStruct((R, D), F32), jax.ShapeDtypeStruct((R, D), BF16), jax.ShapeDtypeStruct((8, 128), F32)],
        compiler_params=_params(("arbitrary",)),
    )(y, target)


def _head_norm(xs, g):
    r = lax.rsqrt(jnp.mean(xs * xs, axis=-1, keepdims=True) + EPS)
    return xs * r * g


def _head_norm_bwd(xs, g, dy):
    r = lax.rsqrt(jnp.mean(xs * xs, axis=-1, keepdims=True) + EPS)
    xn = xs * r
    uu = dy * g
    return r * (uu - xn * jnp.mean(xn * uu, axis=-1, keepdims=True)), jnp.sum(dy * xn, axis=0, keepdims=True)


NORMED = [(C_FQ, FOX_HEADS, 0), (C_FK, FOX_HEADS, 1), (C_SQ, SWA_HEADS, 2), (C_SK, SWA_KV_HEADS, 3), (C_MQ, MEM_HEADS, 4)]
PLAIN = [(C_FV, FOX_HEADS), (C_SV, SWA_KV_HEADS)]


def _prep_fwd(proj, gains):
    T = proj.shape[0]
    tr = _tile(T, 256, 8)

    def body(p_ref, g_ref, o_ref, lf_ref):
        for start, heads, row in NORMED:
            gn = g_ref[row:row + 1, :]
            for hh in range(heads):
                sl = slice(start + hh * HEAD_DIM, start + (hh + 1) * HEAD_DIM)
                o_ref[:, sl] = _head_norm(p_ref[:, sl], gn).astype(BF16)
        for start, heads in PLAIN:
            sl = slice(start, start + heads * HEAD_DIM)
            o_ref[:, sl] = p_ref[:, sl].astype(BF16)
        zb = p_ref[:, C_FL:C_FL + HEAD_DIM] + g_ref[5:6, :]
        o_ref[:, C_FL:C_FL + HEAD_DIM] = jnp.zeros((tr, HEAD_DIM), BF16)
        lf_ref[...] = jnp.minimum(zb, 0.0) - jnp.log(1.0 + jnp.exp(-jnp.abs(zb)))

    return pl.pallas_call(
        body, name="prep_fwd", grid=(T // tr,),
        in_specs=[pl.BlockSpec((tr, PROJ_W), lambda i: (i, 0)), pl.BlockSpec((8, 128), lambda i: (0, 0))],
        out_specs=[pl.BlockSpec((tr, PROJ_W), lambda i: (i, 0)), pl.BlockSpec((tr, HEAD_DIM), lambda i: (i, 0))],
        out_shape=[jax.ShapeDtypeStruct((T, PROJ_W), BF16), jax.ShapeDtypeStruct((T, HEAD_DIM), F32)],
        compiler_params=_params(("parallel",)),
    )(proj, gains)


def _prep_bwd(proj, gains, dfq, dfk, dfv, dsq, dsk, dsv, dmq, dlogf):
    T = proj.shape[0]
    tr = _tile(T, 256, 8)
    d_normed = {C_FQ: 0, C_FK: 1, C_SQ: 3, C_SK: 4, C_MQ: 6}
    d_plain = {C_FV: 2, C_SV: 5}

    def body(p_ref, g_ref, *rest):
        d_refs, dlf_ref, o_ref, dg_ref = rest[:7], rest[7], rest[8], rest[9]
        rows = []
        for start, heads, row in NORMED:
            gn = g_ref[row:row + 1, :]
            d_ref = d_refs[d_normed[start]]
            tot = jnp.zeros((1, HEAD_DIM), F32)
            for hh in range(heads):
                sl = slice(start + hh * HEAD_DIM, start + (hh + 1) * HEAD_DIM)
                dx, dgn = _head_norm_bwd(p_ref[:, sl], gn, d_ref[:, hh * HEAD_DIM:(hh + 1) * HEAD_DIM])
                o_ref[:, sl] = dx.astype(BF16)
                tot = tot + dgn
            rows.append(tot)
        for start, heads in PLAIN:
            o_ref[:, start:start + heads * HEAD_DIM] = d_refs[d_plain[start]][...].astype(BF16)
        zb = p_ref[:, C_FL:C_FL + HEAD_DIM] + g_ref[5:6, :]
        lane = lax.broadcasted_iota(jnp.int32, (tr, HEAD_DIM), 1)
        dz = jnp.where(lane < FOX_HEADS, dlf_ref[...] * (1.0 - _sigmoid(zb)), 0.0)
        o_ref[:, C_FL:C_FL + HEAD_DIM] = dz.astype(BF16)
        rows.append(jnp.sum(dz, axis=0, keepdims=True))
        part = jnp.concatenate(rows + [jnp.zeros((2, HEAD_DIM), F32)], axis=0)

        @pl.when(pl.program_id(0) == 0)
        def _():
            dg_ref[...] = part

        @pl.when(pl.program_id(0) > 0)
        def _():
            dg_ref[...] += part

    def rows_of(w):
        return pl.BlockSpec((tr, w), lambda i: (i, 0))

    small = pl.BlockSpec((8, 128), lambda i: (0, 0))
    ds = [dfq, dfk, dfv, dsq, dsk, dsv, dmq]
    return pl.pallas_call(
        body, name="prep_bwd", grid=(T // tr,),
        in_specs=[rows_of(PROJ_W), small] + [rows_of(d.shape[1]) for d in ds] + [rows_of(HEAD_DIM)],
        out_specs=[rows_of(PROJ_W), small],
        out_shape=[jax.ShapeDtypeStruct((T, PROJ_W), BF16), jax.ShapeDtypeStruct((8, 128), F32)],
        compiler_params=_params(("arbitrary",)),
    )(proj, gains, *ds, dlogf)


def _head_norm_rows(x, gain):
    R, W = x.shape

    def body(x_ref, g_ref, o_ref):
        for hh in range(W // HEAD_DIM):
            sl = slice(hh * HEAD_DIM, (hh + 1) * HEAD_DIM)
            o_ref[:, sl] = _head_norm(x_ref[:, sl], g_ref[...]).astype(BF16)

    return pl.pallas_call(body, name="mem_k_norm", out_shape=jax.ShapeDtypeStruct((R, W), BF16))(x, gain)


def _head_norm_rows_bwd(x, gain, dy):
    R, W = x.shape

    def body(x_ref, g_ref, dy_ref, dx_ref, dg_ref):
        tot = jnp.zeros((1, HEAD_DIM), F32)
        for hh in range(W // HEAD_DIM):
            sl = slice(hh * HEAD_DIM, (hh + 1) * HEAD_DIM)
            dx, dgn = _head_norm_bwd(x_ref[:, sl], g_ref[...], dy_ref[:, sl])
            dx_ref[:, sl] = dx.astype(BF16)
            tot = tot + dgn
        dg_ref[...] = tot

    return pl.pallas_call(
        body, name="mem_k_norm_bwd",
        out_shape=[jax.ShapeDtypeStruct((R, W), BF16), jax.ShapeDtypeStruct((1, HEAD_DIM), F32)])(x, gain, dy)


def _cumsum_rows(name, x, reverse):
    T, W = x.shape
    tb = _tile(T, 512, 8)
    nb = T // tb

    def body(x_ref, o_ref, carry):
        @pl.when(pl.program_id(0) == 0)
        def _():
            carry[...] = jnp.zeros_like(carry)

        xv = x_ref[...]
        r = lax.broadcasted_iota(jnp.int32, (tb, tb), 0)
        cc = lax.broadcasted_iota(jnp.int32, (tb, tb), 1)
        tri = jnp.where((cc >= r) if reverse else (cc <= r), 1.0, 0.0).astype(F32)
        o_ref[...] = jnp.dot(tri, xv, precision=lax.Precision.HIGHEST, preferred_element_type=F32) + carry[...]
        carry[...] += jnp.sum(xv, axis=0, keepdims=True)

    idx = (lambda i: (nb - 1 - i, 0)) if reverse else (lambda i: (i, 0))
    return pl.pallas_call(
        body, name=name, grid=(nb,), in_specs=[pl.BlockSpec((tb, W), idx)], out_specs=pl.BlockSpec((tb, W), idx),
        out_shape=jax.ShapeDtypeStruct((T, W), F32), scratch_shapes=[pltpu.VMEM((1, W), F32)],
        compiler_params=_params(("arbitrary",)),
    )(x)


def _fox_fwd(qkv, cq, ck):
    T = qkv.shape[0]
    tq = _tile(T, 512)
    nq = T // tq
    HQ, HK, HV = C_FQ // HEAD_DIM, C_FK // HEAD_DIM, C_FV // HEAD_DIM

    def body(q_ref, k_ref, v_ref, cq_ref, ck_ref, o_ref, lse_ref, m_sc, l_sc, acc_sc):
        i, j = pl.program_id(1), pl.program_id(2)

        @pl.when(j == 0)
        def _():
            m_sc[...] = jnp.full_like(m_sc, NEG_INF)
            l_sc[...] = jnp.zeros_like(l_sc)
            acc_sc[...] = jnp.zeros_like(acc_sc)

        @pl.when(j <= i)
        def _():
            s = _dot(q_ref[...], k_ref[...], "nt") * SCALE + (cq_ref[...] - ck_ref[...])
            rows = i * tq + lax.broadcasted_iota(jnp.int32, (tq, tq), 0)
            cols = j * tq + lax.broadcasted_iota(jnp.int32, (tq, tq), 1)
            s = jnp.where(cols <= rows, s, NEG_INF)
            m_new = jnp.maximum(m_sc[...], jnp.max(s, axis=-1, keepdims=True))
            alpha = jnp.exp(m_sc[...] - m_new)
            p = jnp.exp(s - m_new)
            l_sc[...] = alpha * l_sc[...] + jnp.sum(p, axis=-1, keepdims=True)
            acc_sc[...] = alpha * acc_sc[...] + _dot(p.astype(BF16), v_ref[...], "nn")
            m_sc[...] = m_new

        @pl.when(j == i)
        def _():
            o_ref[...] = (acc_sc[...] / l_sc[...]).astype(BF16)
            lse_ref[...] = m_sc[...] + jnp.log(l_sc[...])

    def col(base):
        return pl.BlockSpec((tq, HEAD_DIM), lambda h, i, j: (jnp.minimum(j, i), base + h))

    return pl.pallas_call(
        body, name="fox_fwd", grid=(FOX_HEADS, nq, nq),
        in_specs=[pl.BlockSpec((tq, HEAD_DIM), lambda h, i, j: (i, HQ + h)), col(HK), col(HV),
                  pl.BlockSpec((None, tq, 1), lambda h, i, j: (h, i, 0)),
                  pl.BlockSpec((None, 1, tq), lambda h, i, j: (h, 0, jnp.minimum(j, i)))],
        out_specs=[pl.BlockSpec((tq, HEAD_DIM), lambda h, i, j: (i, h)),
                   pl.BlockSpec((None, tq, 1), lambda h, i, j: (h, i, 0))],
        out_shape=[jax.ShapeDtypeStruct((T, FOX_W), BF16), jax.ShapeDtypeStruct((FOX_HEADS, T, 1), F32)],
        scratch_shapes=[pltpu.VMEM((tq, 1), F32), pltpu.VMEM((tq, 1), F32), pltpu.VMEM((tq, HEAD_DIM), F32)],
        compiler_params=_params(("parallel", "arbitrary", "arbitrary")),
    )(qkv, qkv, qkv, cq, ck)


def _fox_bwd(qkv, cq, ck, out, lse, dmix):
    T = qkv.shape[0]
    tq = _tile(T, 512)
    nq = T // tq
    HQ, HK, HV = C_FQ // HEAD_DIM, C_FK // HEAD_DIM, C_FV // HEAD_DIM

    def body(q_ref, k_ref, v_ref, cq_ref, ck_ref, o_ref, lse_ref, do_ref, dq_ref, dk_ref, dv_ref, dck_ref, dk_sc, dv_sc, dc_sc):
        kj, qi = pl.program_id(1), pl.program_id(2)

        @pl.when(qi == kj)
        def _():
            dk_sc[...] = jnp.zeros_like(dk_sc)
            dv_sc[...] = jnp.zeros_like(dv_sc)
            dc_sc[...] = jnp.zeros_like(dc_sc)

        @pl.when(qi >= kj)
        def _():
            q, k, v, do = q_ref[...], k_ref[...], v_ref[...], do_ref[...]
            s = _dot(q, k, "nt") * SCALE + (cq_ref[...] - ck_ref[...])
            rows = qi * tq + lax.broadcasted_iota(jnp.int32, (tq, tq), 0)
            cols = kj * tq + lax.broadcasted_iota(jnp.int32, (tq, tq), 1)
            p = jnp.where(cols <= rows, jnp.exp(s - lse_ref[...]), 0.0)
            dp = _dot(do, v, "nt")
            delta = jnp.sum(do.astype(F32) * o_ref[...].astype(F32), axis=-1, keepdims=True)
            ds = p * (dp - delta)
            dsb = ds.astype(BF16)
            dv_sc[...] += _dot(p.astype(BF16), do, "tn")
            dk_sc[...] += _dot(dsb, q, "tn")
            dc_sc[...] += jnp.sum(ds, axis=0, keepdims=True)
            dq_part = _dot(dsb, k, "nn") * SCALE
            rows_q = pl.ds(pl.multiple_of(qi * tq, tq), tq)

            @pl.when(kj == 0)
            def _():
                dq_ref[rows_q, :] = dq_part

            @pl.when(kj > 0)
            def _():
                dq_ref[rows_q, :] += dq_part

        @pl.when(qi == nq - 1)
        def _():
            dk_ref[...] = dk_sc[...] * SCALE
            dv_ref[...] = dv_sc[...]
            dck_ref[...] = -dc_sc[...]

    def qrow(base):
        return pl.BlockSpec((tq, HEAD_DIM), lambda h, kj, qi: (jnp.maximum(qi, kj), base + h))

    def krow(base):
        return pl.BlockSpec((tq, HEAD_DIM), lambda h, kj, qi: (kj, base + h))

    qcol = pl.BlockSpec((None, tq, 1), lambda h, kj, qi: (h, jnp.maximum(qi, kj), 0))
    kvec = pl.BlockSpec((None, 1, tq), lambda h, kj, qi: (h, 0, kj))
    sds = jax.ShapeDtypeStruct((T, FOX_W), F32)
    return pl.pallas_call(
        body, name="fox_bwd", grid=(FOX_HEADS, nq, nq),
        in_specs=[qrow(HQ), krow(HK), krow(HV), qcol, kvec, qrow(0), qcol, qrow(0)],
        out_specs=[pl.BlockSpec((T, HEAD_DIM), lambda h, kj, qi: (0, h)), krow(0), krow(0), kvec],
        out_shape=[sds, sds, sds, jax.ShapeDtypeStruct((FOX_HEADS, 1, T), F32)],
        scratch_shapes=[pltpu.VMEM((tq, HEAD_DIM), F32), pltpu.VMEM((tq, HEAD_DIM), F32), pltpu.VMEM((1, tq), F32)],
        compiler_params=_params(("parallel", "arbitrary", "arbitrary")),
    )(qkv, qkv, qkv, cq, ck, out, lse, dmix)


GW = SWA_GROUP * HEAD_DIM
GR = SWA_GROUP * WINDOW


def _swa_scores(q_ref, kp_ref, kc_ref, slope_ref, n):
    q = q_ref[...]
    qs = jnp.concatenate([q[:, t * HEAD_DIM:(t + 1) * HEAD_DIM] for t in range(SWA_GROUP)], axis=0)
    kb = jnp.concatenate([kp_ref[...], kc_ref[...]], axis=0)
    r = lax.broadcasted_iota(jnp.int32, (GR, 2 * WINDOW), 0) & (WINDOW - 1)
    jj = lax.broadcasted_iota(jnp.int32, (GR, 2 * WINDOW), 1)
    dist = WINDOW + r - jj
    valid = (dist >= 0) & (dist < WINDOW) & ((n > 0) | (jj >= WINDOW))
    s = _dot(qs, kb, "nt") * SCALE - slope_ref[...] * dist.astype(F32)
    return qs, kb, jnp.where(valid, s, NEG_INF), valid


def _swa_specs():
    HQ, HK, HV = C_SQ // GW, C_SK // HEAD_DIM, C_SV // HEAD_DIM
    q_spec = pl.BlockSpec((WINDOW, GW), lambda g, n: (n, HQ + g))

    def prev(base):
        return pl.BlockSpec((WINDOW, HEAD_DIM), lambda g, n: (jnp.maximum(n - 1, 0), base + g))

    def cur(base):
        return pl.BlockSpec((WINDOW, HEAD_DIM), lambda g, n: (n, base + g))

    col = pl.BlockSpec((None, GR, 1), lambda g, n: (g, 0, 0))
    return q_spec, prev(HK), cur(HK), prev(HV), cur(HV), col


def _swa_fwd(qkv, slopes, sinks):
    T = qkv.shape[0]
    nb = T // WINDOW
    assert C_SQ % GW == 0

    def body(q_ref, kp_ref, kc_ref, vp_ref, vc_ref, slope_ref, sink_ref, o_ref, lse_ref):
        n = pl.program_id(1)
        _, _, s, _ = _swa_scores(q_ref, kp_ref, kc_ref, slope_ref, n)
        m = jnp.maximum(jnp.max(s, axis=-1, keepdims=True), sink_ref[...])
        p = jnp.exp(s - m)
        l = jnp.sum(p, axis=-1, keepdims=True) + jnp.exp(sink_ref[...] - m)
        vb = jnp.concatenate([vp_ref[...], vc_ref[...]], axis=0)
        o = _dot(p.astype(BF16), vb, "nn") / l
        for t in range(SWA_GROUP):
            o_ref[:, t * HEAD_DIM:(t + 1) * HEAD_DIM] = o[t * WINDOW:(t + 1) * WINDOW, :].astype(BF16)
        lse_ref[...] = m + jnp.log(l)

    q_spec, kp, kc, vp, vc, col = _swa_specs()
    return pl.pallas_call(
        body, name="swa_fwd", grid=(SWA_KV_HEADS, nb), in_specs=[q_spec, kp, kc, vp, vc, col, col],
        out_specs=[pl.BlockSpec((WINDOW, GW), lambda g, n: (n, g)), pl.BlockSpec((None, None, GR, 1), lambda g, n: (g, n, 0, 0))],
        out_shape=[jax.ShapeDtypeStruct((T, SWA_HEADS * HEAD_DIM), BF16), jax.ShapeDtypeStruct((SWA_KV_HEADS, nb, GR, 1), F32)],
        compiler_params=_params(("parallel", "arbitrary")),
    )(qkv, qkv, qkv, qkv, qkv, slopes, sinks)


def _swa_bwd(qkv, slopes, sinks, out, lse, dmix):
    T = qkv.shape[0]
    nb = T // WINDOW
    DO = FOX_W // GW
    assert FOX_W % GW == 0

    def body(q_ref, kp_ref, kc_ref, vp_ref, vc_ref, slope_ref, sink_ref, o_ref, lse_ref, do_ref,
             dq_ref, dk_ref, dv_ref, dsink_ref, sink_sc):
        n = pl.program_id(1)

        @pl.when(n == 0)
        def _():
            dk_ref[...] = jnp.zeros_like(dk_ref)
            dv_ref[...] = jnp.zeros_like(dv_ref)
            sink_sc[...] = jnp.zeros_like(sink_sc)

        qs, kb, s, valid = _swa_scores(q_ref, kp_ref, kc_ref, slope_ref, n)
        lse = lse_ref[...]
        p = jnp.where(valid, jnp.exp(s - lse), 0.0)
        vb = jnp.concatenate([vp_ref[...], vc_ref[...]], axis=0)
        do = jnp.concatenate([do_ref[:, t * HEAD_DIM:(t + 1) * HEAD_DIM] for t in range(SWA_GROUP)], axis=0)
        oo = jnp.concatenate([o_ref[:, t * HEAD_DIM:(t + 1) * HEAD_DIM] for t in range(SWA_GROUP)], axis=0)
        dp = _dot(do, vb, "nt")
        delta = jnp.sum(do.astype(F32) * oo.astype(F32), axis=-1, keepdims=True)
        ds = p * (dp - delta)
        dsb = ds.astype(BF16)
        dq = _dot(dsb, kb, "nn") * SCALE
        for t in range(SWA_GROUP):
            dq_ref[:, t * HEAD_DIM:(t + 1) * HEAD_DIM] = dq[t * WINDOW:(t + 1) * WINDOW, :]
        dkb = _dot(dsb, qs, "tn") * SCALE
        dvb = _dot(p.astype(BF16), do, "tn")
        r_prev = pl.ds(pl.multiple_of(jnp.maximum(n - 1, 0) * WINDOW, WINDOW), WINDOW)
        r_cur = pl.ds(pl.multiple_of(n * WINDOW, WINDOW), WINDOW)
        dk_ref[r_prev, :] += dkb[:WINDOW, :]
        dk_ref[r_cur, :] += dkb[WINDOW:, :]
        dv_ref[r_prev, :] += dvb[:WINDOW, :]
        dv_ref[r_cur, :] += dvb[WINDOW:, :]
        sink_sc[...] -= jnp.exp(sink_ref[...] - lse) * delta

        @pl.when(n == nb - 1)
        def _():
            tot = [jnp.zeros((1, 128), F32) + jnp.sum(sink_sc[t * WINDOW:(t + 1) * WINDOW, :]) for t in range(SWA_GROUP)]
            dsink_ref[...] = jnp.concatenate(tot + [jnp.zeros((8 - SWA_GROUP, 128), F32)], axis=0)

    q_spec, kp, kc, vp, vc, col = _swa_specs()
    kv_acc = pl.BlockSpec((T, HEAD_DIM), lambda g, n: (0, g))
    return pl.pallas_call(
        body, name="swa_bwd", grid=(SWA_KV_HEADS, nb),
        in_specs=[q_spec, kp, kc, vp, vc, col, col, pl.BlockSpec((WINDOW, GW), lambda g, n: (n, g)),
                  pl.BlockSpec((None, None, GR, 1), lambda g, n: (g, n, 0, 0)), pl.BlockSpec((WINDOW, GW), lambda g, n: (n, DO + g))],
        out_specs=[pl.BlockSpec((WINDOW, GW), lambda g, n: (n, g)), kv_acc, kv_acc, pl.BlockSpec((None, 8, 128), lambda g, n: (g, 0, 0))],
        out_shape=[jax.ShapeDtypeStruct((T, SWA_HEADS * HEAD_DIM), F32), jax.ShapeDtypeStruct((T, SWA_KV_HEADS * HEAD_DIM), F32),
                   jax.ShapeDtypeStruct((T, SWA_KV_HEADS * HEAD_DIM), F32), jax.ShapeDtypeStruct((SWA_KV_HEADS, 8, 128), F32)],
        scratch_shapes=[pltpu.VMEM((GR, 1), F32)],
        compiler_params=_params(("parallel", "arbitrary")),
    )(qkv, qkv, qkv, qkv, qkv, slopes, sinks, out, lse, dmix)


def _mem_fwd(qkv, mk, mv):
    T, ML = qkv.shape[0], mk.shape[0]
    tq = _tile(T, 512)
    HQ = C_MQ // HEAD_DIM

    def body(q_ref, k_ref, v_ref, o_ref, lse_ref):
        s = _dot(q_ref[...], k_ref[...], "nt") * SCALE
        m = jnp.max(s, axis=-1, keepdims=True)
        p = jnp.exp(s - m)
        l = jnp.sum(p, axis=-1, keepdims=True)
        o_ref[...] = (_dot(p.astype(BF16), v_ref[...], "nn") / l).astype(BF16)
        lse_ref[...] = m + jnp.log(l)

    kv = pl.BlockSpec((ML, HEAD_DIM), lambda h, i: (0, h))
    return pl.pallas_call(
        body, name="mem_fwd", grid=(MEM_HEADS, T // tq),
        in_specs=[pl.BlockSpec((tq, HEAD_DIM), lambda h, i: (i, HQ + h)), kv, kv],
        out_specs=[pl.BlockSpec((tq, HEAD_DIM), lambda h, i: (i, h)), pl.BlockSpec((None, tq, 1), lambda h, i: (h, i, 0))],
        out_shape=[jax.ShapeDtypeStruct((T, MEM_HEADS * HEAD_DIM), BF16), jax.ShapeDtypeStruct((MEM_HEADS, T, 1), F32)],
        compiler_params=_params(("parallel", "arbitrary")),
    )(qkv, mk, mv)


def _mem_bwd(qkv, mk, mv, out, lse, dmix):
    T, ML = qkv.shape[0], mk.shape[0]
    tq = _tile(T, 512)
    HQ = C_MQ // HEAD_DIM
    DO = (FOX_W + SWA_HEADS * HEAD_DIM) // HEAD_DIM

    def body(q_ref, k_ref, v_ref, o_ref, lse_ref, do_ref, dq_ref, dk_ref, dv_ref):
        q, k, v, do = q_ref[...], k_ref[...], v_ref[...], do_ref[...]
        p = jnp.exp(_dot(q, k, "nt") * SCALE - lse_ref[...])
        dp = _dot(do, v, "nt")
        delta = jnp.sum(do.astype(F32) * o_ref[...].astype(F32), axis=-1, keepdims=True)
        dsb = (p * (dp - delta)).astype(BF16)
        dq_ref[...] = _dot(dsb, k, "nn") * SCALE
        dk_part = _dot(dsb, q, "tn") * SCALE
        dv_part = _dot(p.astype(BF16), do, "tn")

        @pl.when(pl.program_id(1) == 0)
        def _():
            dk_ref[...] = dk_part
            dv_ref[...] = dv_part

        @pl.when(pl.program_id(1) > 0)
        def _():
            dk_ref[...] += dk_part
            dv_ref[...] += dv_part

    kv = pl.BlockSpec((ML, HEAD_DIM), lambda h, i: (0, h))
    qb = pl.BlockSpec((tq, HEAD_DIM), lambda h, i: (i, h))
    return pl.pallas_call(
        body, name="mem_bwd", grid=(MEM_HEADS, T // tq),
        in_specs=[pl.BlockSpec((tq, HEAD_DIM), lambda h, i: (i, HQ + h)), kv, kv, qb,
                  pl.BlockSpec((None, tq, 1), lambda h, i: (h, i, 0)), pl.BlockSpec((tq, HEAD_DIM), lambda h, i: (i, DO + h))],
        out_specs=[qb, kv, kv],
        out_shape=[jax.ShapeDtypeStruct((T, MEM_HEADS * HEAD_DIM), F32), jax.ShapeDtypeStruct((ML, MEM_HEADS * HEAD_DIM), F32),
                   jax.ShapeDtypeStruct((ML, MEM_HEADS * HEAD_DIM), F32)],
        compiler_params=_params(("parallel", "arbitrary")),
    )(qkv, mk, mv, out, lse, dmix)


HBM = pl.BlockSpec(memory_space=pltpu.HBM)


def _place():
    x, y, c = lax.axis_index("x"), lax.axis_index("y"), lax.axis_index("c")
    chips = [(1 - x, y), (x, 1 - y), (1 - x, 1 - y)]
    return x, y, c, chips


def _remote(src, dst, send_sem, recv_sem, device):
    return pltpu.make_async_remote_copy(src_ref=src, dst_ref=dst, send_sem=send_sem, recv_sem=recv_sem,
                                        device_id=device, device_id_type=MESH)


def _place_ids():
    x, y, c = lax.axis_index("x"), lax.axis_index("y"), lax.axis_index("c")
    order = [2 * x + y, 2 * (1 - x) + y, 2 * x + (1 - y), 2 * (1 - x) + (1 - y)]
    return jnp.stack([2 * x + y, c] + order).astype(jnp.int32)


def _cast_place(name, w, ids):
    R, C = w.shape
    tr = _tile(R, 256, 16)

    def body(ids_ref, w_ref, o_ref):
        o_ref[...] = w_ref[...].astype(BF16)

    return pl.pallas_call(
        body, name=name,
        grid_spec=pltpu.PrefetchScalarGridSpec(
            num_scalar_prefetch=1, grid=(R // tr,), in_specs=[pl.BlockSpec((tr, C), lambda i, ids: (i, 0))],
            out_specs=pl.BlockSpec((None, tr, C), lambda i, ids: (ids[0], i, 0))),
        out_shape=jax.ShapeDtypeStruct((N_CHIPS, R, C), BF16), compiler_params=_params(("parallel",)),
    )(ids, w)


def _gather_weights(placed):
    n = len(placed)

    def body(*refs):
        out = refs[n:2 * n]
        send, recv = refs[2 * n:]
        x, y, c, chips = _place()
        me = 2 * x + y
        ids = [2 * cx + cy for cx, cy in chips]
        copies = []
        for a in range(n):
            half = out[a].shape[1] // 2
            mine = out[a].at[me, pl.ds(c * half, half)]
            for j, (cx, cy) in enumerate(chips):
                cp = _remote(mine, mine, send.at[a, j], recv.at[a, j], (cx, cy, c))
                cp.start()
                copies.append(cp)
        for a in range(n):
            half = out[a].shape[1] // 2
            for j in range(3):
                landed = out[a].at[ids[j], pl.ds(c * half, half)]
                _remote(landed, landed, send.at[a, j], recv.at[a, j], (x, y, c)).wait_recv()
                cp = _remote(landed, landed, send.at[a, 3 + j], recv.at[a, 3 + j], (x, y, 1 - c))
                cp.start()
                copies.append(cp)
        for a in range(n):
            half = out[a].shape[1] // 2
            for j in range(3):
                landed = out[a].at[ids[j], pl.ds((1 - c) * half, half)]
                _remote(landed, landed, send.at[a, 3 + j], recv.at[a, 3 + j], (x, y, c)).wait_recv()
        for cp in copies:
            cp.wait_send()

    return pl.pallas_call(
        body, name="gather_weights", in_specs=[HBM] * n, out_specs=[HBM] * n,
        out_shape=[jax.ShapeDtypeStruct(s.shape, s.dtype) for s in placed], input_output_aliases={a: a for a in range(n)},
        scratch_shapes=[pltpu.SemaphoreType.DMA((n, 6)), pltpu.SemaphoreType.DMA((n, 6))],
    )(*placed)


def _pair_exchange(grads):
    n = len(grads)

    def body(*refs):
        src, theirs = refs[:n], refs[n:2 * n]
        send, recv = refs[2 * n:]
        x, y, c, chips = _place()
        order = [2 * x + y] + [2 * cx + cy for cx, cy in chips]
        copies = []
        for a in range(n):
            half = src[a].shape[1] // 2
            for j in range(N_CHIPS):
                cp = _remote(src[a].at[order[j], pl.ds((1 - c) * half, half)], theirs[a].at[j], send.at[a, j], recv.at[a, j], (x, y, 1 - c))
                cp.start()
                copies.append(cp)
        for cp in copies:
            cp.wait()

    return pl.pallas_call(
        body, name="grad_pair_exchange", in_specs=[HBM] * n, out_specs=[HBM] * n,
        out_shape=[jax.ShapeDtypeStruct((N_CHIPS, g.shape[1] // 2, g.shape[2]), g.dtype) for g in grads],
        scratch_shapes=[pltpu.SemaphoreType.DMA((n, N_CHIPS)), pltpu.SemaphoreType.DMA((n, N_CHIPS))],
    )(*grads)


def _chip_exchange(parts):
    n = len(parts)

    def body(*refs):
        src, dst = refs[:n], refs[n:2 * n]
        send, recv = refs[2 * n:]
        x, y, c, chips = _place()
        copies = []
        for a in range(n):
            for j, (cx, cy) in enumerate(chips):
                cp = _remote(src[a].at[j], dst[a].at[j], send.at[a, j], recv.at[a, j], (cx, cy, c))
                cp.start()
                copies.append(cp)
        for cp in copies:
            cp.wait()

    return pl.pallas_call(
        body, name="grad_chip_exchange", in_specs=[HBM] * n, out_specs=[HBM] * n,
        out_shape=[jax.ShapeDtypeStruct(p.shape, p.dtype) for p in parts],
        scratch_shapes=[pltpu.SemaphoreType.DMA((n, 3)), pltpu.SemaphoreType.DMA((n, 3))],
    )(*parts)


def _pair_share(shards):
    n = len(shards)

    def body(*refs):
        buf = refs[n:2 * n]
        send, recv = refs[2 * n:]
        x, y, c, _ = _place()
        copies = []
        for a in range(n):
            half = buf[a].shape[0] // 2
            mine = buf[a].at[pl.ds(c * half, half)]
            cp = _remote(mine, mine, send.at[a], recv.at[a], (x, y, 1 - c))
            cp.start()
            copies.append(cp)
        for a, cp in enumerate(copies):
            half = buf[a].shape[0] // 2
            cp.wait_send()
            theirs = buf[a].at[pl.ds((1 - c) * half, half)]
            _remote(theirs, theirs, send.at[a], recv.at[a], (x, y, c)).wait_recv()

    return pl.pallas_call(
        body, name="grad_pair_share", in_specs=[HBM] * n, out_specs=[HBM] * n,
        out_shape=[jax.ShapeDtypeStruct(s.shape, s.dtype) for s in shards], input_output_aliases={a: a for a in range(n)},
        scratch_shapes=[pltpu.SemaphoreType.DMA((n,)), pltpu.SemaphoreType.DMA((n,))],
    )(*shards)


def _all_reduce_small(buf):
    R, W = buf.shape

    def body(src_ref, out_ref, slots, send, recv):
        x, y, c, _ = _place()
        me = 4 * x + 2 * y + c
        copies = []
        for dx in range(2):
            for dy in range(2):
                for dc in range(2):
                    if dx == dy == dc == 0:
                        continue
                    k = 4 * dx + 2 * dy + dc
                    peer = (x ^ dx, y ^ dy, c ^ dc)
                    cp = _remote(src_ref, slots.at[me], send.at[k], recv.at[k], peer)
                    cp.start()
                    copies.append((cp, k))
        slots[me] = src_ref[...]
        for cp, k in copies:
            cp.wait_send()
            landed = slots.at[me ^ k]
            _remote(landed, landed, send.at[k], recv.at[k], (x, y, c)).wait_recv()
        total = slots[0]
        for d in range(1, N_DEV):
            total = total + slots[d]
        out_ref[...] = total

    return pl.pallas_call(
        body, name="all_reduce_small", in_specs=[pl.BlockSpec(memory_space=pltpu.VMEM)],
        out_specs=pl.BlockSpec(memory_space=pltpu.VMEM), out_shape=jax.ShapeDtypeStruct((R, W), F32),
        scratch_shapes=[pltpu.VMEM((N_DEV, R, W), F32), pltpu.SemaphoreType.DMA((N_DEV,)), pltpu.SemaphoreType.DMA((N_DEV,))],
    )(buf)


def _pair_sum_bf16(name, grad, theirs, ids):
    _, R2, C = theirs.shape
    tr = _tile(R2, 256, 16)
    nrb = R2 // tr

    def body(ids_ref, a_ref, b_ref, o_ref):
        o_ref[...] = (a_ref[...] + b_ref[...]).astype(BF16)

    return pl.pallas_call(
        body, name=name,
        grid_spec=pltpu.PrefetchScalarGridSpec(
            num_scalar_prefetch=1, grid=(3, nrb),
            in_specs=[pl.BlockSpec((None, tr, C), lambda j, i, ids: (ids[3 + j], ids[1] * nrb + i, 0)),
                      pl.BlockSpec((None, tr, C), lambda j, i, ids: (j + 1, i, 0))],
            out_specs=pl.BlockSpec((None, tr, C), lambda j, i, ids: (j, i, 0))),
        out_shape=jax.ShapeDtypeStruct((3, R2, C), BF16), compiler_params=_params(("parallel", "parallel")),
    )(ids, grad, theirs)


def _chip_sum(name, grad, theirs, arrived, ids):
    _, R2, C = theirs.shape
    tr = _tile(R2, 256, 16)
    nrb = R2 // tr

    def body(ids_ref, a_ref, b_ref, r_ref, o_ref):
        tot = a_ref[...] + b_ref[...]
        for j in range(3):
            tot = tot + r_ref[j].astype(F32)
        o_ref[...] = tot

    return pl.pallas_call(
        body, name=name,
        grid_spec=pltpu.PrefetchScalarGridSpec(
            num_scalar_prefetch=1, grid=(nrb,),
            in_specs=[pl.BlockSpec((None, tr, C), lambda i, ids: (ids[0], ids[1] * nrb + i, 0)),
                      pl.BlockSpec((None, tr, C), lambda i, ids: (0, i, 0)),
                      pl.BlockSpec((3, tr, C), lambda i, ids: (0, i, 0))],
            out_specs=pl.BlockSpec((tr, C), lambda i, ids: (ids[1] * nrb + i, 0))),
        out_shape=jax.ShapeDtypeStruct((2 * R2, C), F32), compiler_params=_params(("parallel",)),
    )(ids, grad, theirs, arrived)


def _adamw(name, w, g, m, v):
    R, C = w.shape
    tr = _tile(R, 128, 8)
    c1 = 1.0 / (1.0 - ADAM_B1 ** ADAM_STEP)
    c2 = 1.0 / (1.0 - ADAM_B2 ** ADAM_STEP)

    def body(w_ref, g_ref, m_ref, v_ref, d_ref, mo_ref, vo_ref):
        gv = g_ref[...]
        mn = ADAM_B1 * m_ref[...] + (1.0 - ADAM_B1) * gv
        vn = ADAM_B2 * v_ref[...] + (1.0 - ADAM_B2) * (gv * gv)
        d_ref[...] = -ADAM_LR * ((mn * c1) / (jnp.sqrt(vn * c2) + ADAM_EPS) + ADAM_WD * w_ref[...])
        mo_ref[...] = mn
        vo_ref[...] = vn

    spec = pl.BlockSpec((tr, C), lambda i: (i, 0))
    sds = jax.ShapeDtypeStruct((R, C), F32)
    return pl.pallas_call(body, name=name, grid=(R // tr,), in_specs=[spec] * 4, out_specs=[spec] * 3, out_shape=[sds] * 3,
                          compiler_params=_params(("parallel",)))(w, g, m, v)


SMALL = ["ffn1_norm", "mix_norm", "mem_norm", "forget_bias", "fox_q_gain", "fox_k_gain", "swa_q_gain", "swa_k_gain", "swa_sinks",
         "mem_q_gain", "mem_k_gain", "ffn2_norm"]
LARGE = ["ffn1_gate", "ffn1_up", "ffn1_down", "w_in", "w_mem_k", "w_mem_v", "w_out", "ffn2_gate", "ffn2_up", "ffn2_down"]
WEIGHTS = ["ffn1_norm", "ffn1_gate", "ffn1_up", "ffn1_down", "mix_norm", "mem_norm", "w_in", "forget_bias", "w_mem_k", "w_mem_v",
           "fox_q_gain", "fox_k_gain", "swa_q_gain", "swa_k_gain", "swa_sinks", "mem_q_gain", "mem_k_gain", "w_out", "ffn2_norm",
           "ffn2_gate", "ffn2_up", "ffn2_down"]


def _pad_proj_cols(w):
    out = jnp.zeros((w.shape[0], PROJ_W), w.dtype)
    for start, width, pstart in REF_GROUPS:
        out = lax.dynamic_update_slice(out, w[:, start:start + width], (0, pstart))
    return out


def _unpad_proj_cols(w):
    return jnp.concatenate([w[:, pstart:pstart + width] for _, width, pstart in REF_GROUPS], axis=1)


def _pack_small(vals):
    flat = jnp.concatenate([vals[k].reshape(-1).astype(F32) for k in SMALL + ["loss"]])
    n = flat.shape[0]
    total = -(-n // 1024) * 1024
    return jnp.pad(flat, (0, total - n)).reshape(total // 128, 128)


def _unpack_small(buf, shapes):
    flat = buf.reshape(-1)
    out, off = {}, 0
    for k in SMALL + ["loss"]:
        size = int(np.prod(shapes[k]))
        out[k] = flat[off:off + size].reshape(shapes[k])
        off += size
    return out


def kernel(x, mem, ffn1_norm, ffn1_gate, ffn1_up, ffn1_down, mix_norm, mem_norm, w_in, forget_bias, w_mem_k, w_mem_v, fox_q_gain, fox_k_gain, swa_q_gain, swa_k_gain, swa_sinks, mem_q_gain, mem_k_gain, w_out, ffn2_norm, ffn2_gate, ffn2_up, ffn2_down, loss_target, m_ffn1_norm, m_ffn1_gate, m_ffn1_up, m_ffn1_down, m_mix_norm, m_mem_norm, m_w_in, m_forget_bias, m_w_mem_k, m_w_mem_v, m_fox_q_gain, m_fox_k_gain, m_swa_q_gain, m_swa_k_gain, m_swa_sinks, m_mem_q_gain, m_mem_k_gain, m_w_out, m_ffn2_norm, m_ffn2_gate, m_ffn2_up, m_ffn2_down, v_ffn1_norm, v_ffn1_gate, v_ffn1_up, v_ffn1_down, v_mix_norm, v_mem_norm, v_w_in, v_forget_bias, v_w_mem_k, v_w_mem_v, v_fox_q_gain, v_fox_k_gain, v_swa_q_gain, v_swa_k_gain, v_swa_sinks, v_mem_q_gain, v_mem_k_gain, v_w_out, v_ffn2_norm, v_ffn2_gate, v_ffn2_up, v_ffn2_down):
    given = dict(locals())
    T, D = x.shape[1], x.shape[2]
    ML = mem.shape[1]
    xin = x.reshape(T, D)
    target = loss_target.reshape(T, D)
    memin = mem.reshape(ML, D)

    ids = _place_ids()
    shard = {k: given[k][0] for k in LARGE}
    shard["w_in"] = _pad_proj_cols(shard["w_in"])
    full = dict(zip(LARGE, _gather_weights([_cast_place("cast_" + k, shard[k], ids) for k in LARGE])))
    wg1, wu1, wg2, wu2 = full["ffn1_gate"], full["ffn1_up"], full["ffn2_gate"], full["ffn2_up"]
    wd1 = full["ffn1_down"].reshape(-1, D)
    wd2 = full["ffn2_down"].reshape(-1, D)
    win = full["w_in"].reshape(D, PROJ_W)
    wmk = full["w_mem_k"].reshape(D, MEM_HEADS * HEAD_DIM)
    wmv = full["w_mem_v"].reshape(D, MEM_HEADS * HEAD_DIM)
    wo = full["w_out"].reshape(-1, D)

    gains = jnp.concatenate([fox_q_gain, fox_k_gain, swa_q_gain, swa_k_gain, mem_q_gain,
                             jnp.pad(forget_bias, ((0, 0), (0, HEAD_DIM - FOX_HEADS))), jnp.zeros((2, HEAD_DIM), F32)], axis=0)
    slopes_np = 2.0 ** (-8.0 * np.arange(1, SWA_HEADS + 1) / SWA_HEADS)
    slopes = jnp.asarray(np.repeat(slopes_np, WINDOW).reshape(SWA_KV_HEADS, GR, 1), F32)
    sinks = jnp.repeat(swa_sinks.reshape(SWA_HEADS), WINDOW).reshape(SWA_KV_HEADS, GR, 1)

    h1 = _rms_fwd("ffn1_norm_fwd", xin, ffn1_norm)
    g1, u1, a1 = _ffn_gu("ffn1_gate_up", h1, wg1, wu1)
    x1 = _ffn_down("ffn1_down", a1, wd1, xin)
    h2 = _rms_fwd("mix_norm_fwd", x1, mix_norm)
    proj = _mm2d("proj_in", h2, win, "nn", F32, tn=1408)
    qkv, logf = _prep_fwd(proj, gains)
    cum = _cumsum_rows("forget_cumsum", logf, False)
    cum_h = cum[:, :FOX_HEADS].T
    cq, ck = cum_h.reshape(FOX_HEADS, T, 1), cum_h.reshape(FOX_HEADS, 1, T)
    mn = _rms_fwd("mem_norm_fwd", memin, mem_norm)
    mk_raw = _mm2d("mem_k_proj", mn, wmk, "nn", F32)
    mv = _mm2d("mem_v_proj", mn, wmv, "nn", BF16)
    mk = _head_norm_rows(mk_raw, mem_k_gain)
    out_a, lse_a = _fox_fwd(qkv, cq, ck)
    out_b, lse_b = _swa_fwd(qkv, slopes, sinks)
    out_c, lse_c = _mem_fwd(qkv, mk, mv)
    mixed = jnp.concatenate([out_a, out_b, out_c], axis=1)
    x2 = _mm2d("mix_out", mixed, wo, "nn", F32, extras=[x1], epilogue=lambda accs, ex: [ex[0] + accs[0]])
    h3 = _rms_fwd("ffn2_norm_fwd", x2, ffn2_norm)
    g2, u2, a2 = _ffn_gu("ffn2_gate_up", h3, wg2, wu2)
    x3 = _ffn_down("ffn2_down", a2, wd2, x2)
    dx3, dyb3, loss_part = _loss_head(x3, target)

    grads, small = {}, {"loss": loss_part[0, 0]}
    grads["ffn2_gate"], grads["ffn2_up"], dwd2, dh3 = _ffn_bwd("ffn2", dyb3, h3, g2, u2, a2, wg2, wu2, wd2)
    grads["ffn2_down"] = dwd2.reshape(N_CHIPS, -1, D)
    dx2, dx2b, small["ffn2_norm"] = _rms_bwd("ffn2_norm_bwd", dh3, x2, ffn2_norm, dx3, 1.0)
    dmix = _mm2d("mix_out_dx", dx2b, wo, "nt", BF16)
    grads["w_out"] = _mm2d("mix_out_dw", mixed, dx2b, "tn", F32, tk=512).reshape(N_CHIPS, -1, D)
    dfq, dfk, dfv, dck = _fox_bwd(qkv, cq, ck, out_a, lse_a, dmix)
    dsq, dsk, dsv, dsink = _swa_bwd(qkv, slopes, sinks, out_b, lse_b, dmix)
    dmq, dmk, dmv = _mem_bwd(qkv, mk, mv, out_c, lse_c, dmix)
    small["swa_sinks"] = dsink[:, :SWA_GROUP, 0].reshape(1, SWA_HEADS)
    dcum = jnp.pad(dck.reshape(FOX_HEADS, T).T, ((0, 0), (0, HEAD_DIM - FOX_HEADS)))
    dlogf = _cumsum_rows("forget_cumsum_bwd", dcum, True)
    dproj, dgains = _prep_bwd(proj, gains, dfq, dfk, dfv, dsq, dsk, dsv, dmq, dlogf)
    for row, k in enumerate(["fox_q_gain", "fox_k_gain", "swa_q_gain", "swa_k_gain", "mem_q_gain"]):
        small[k] = dgains[row:row + 1, :]
    small["forget_bias"] = dgains[5:6, :FOX_HEADS]
    grads["w_in"] = _mm2d("proj_in_dw", h2, dproj, "tn", F32, tn=1408, tk=512).reshape(N_CHIPS, -1, PROJ_W)
    dh2 = _mm2d("proj_in_dx", dproj, win, "nt", F32, tk=1408)
    dx1, dyb1, small["mix_norm"] = _rms_bwd("mix_norm_bwd", dh2, x1, mix_norm, dx2, 0.5)
    dmk_raw, small["mem_k_gain"] = _head_norm_rows_bwd(mk_raw, mem_k_gain, dmk)
    dmvb = dmv.astype(BF16)
    grads["w_mem_k"] = _mm2d("mem_k_dw", mn, dmk_raw, "tn", F32).reshape(N_CHIPS, -1, MEM_HEADS * HEAD_DIM)
    grads["w_mem_v"] = _mm2d("mem_v_dw", mn, dmvb, "tn", F32).reshape(N_CHIPS, -1, MEM_HEADS * HEAD_DIM)
    dmn = _mm2d("mem_k_dx", dmk_raw, wmk, "nt", F32)
    dmn = _mm2d("mem_v_dx", dmvb, wmv, "nt", F32, extras=[dmn], epilogue=lambda accs, ex: [ex[0] + accs[0]])
    _, _, small["mem_norm"] = _rms_bwd("mem_norm_bwd", dmn, memin, mem_norm, jnp.zeros_like(memin), 1.0)
    grads["ffn1_gate"], grads["ffn1_up"], dwd1, dh1 = _ffn_bwd("ffn1", dyb1, h1, g1, u1, a1, wg1, wu1, wd1)
    grads["ffn1_down"] = dwd1.reshape(N_CHIPS, -1, D)
    grad_x, _, small["ffn1_norm"] = _rms_bwd("ffn1_norm_bwd", dh1, xin, ffn1_norm, dx1, 1.0)

    theirs = _pair_exchange([grads[k] for k in LARGE])
    to_chips = [_pair_sum_bf16("pair_sum_" + k, grads[k], b, ids) for k, b in zip(LARGE, theirs)]
    arrived = _chip_exchange(to_chips)
    halves = [_chip_sum("chip_sum_" + k, grads[k], b, r, ids) for k, b, r in zip(LARGE, theirs, arrived)]
    reduced = dict(zip(LARGE, _pair_share(halves)))
    reduced["w_in"] = _unpad_proj_cols(reduced["w_in"])

    shapes = {k: given[k].shape for k in SMALL}
    shapes["loss"] = ()
    red_small = _unpack_small(_all_reduce_small(_pack_small(small)), shapes)
    loss = red_small["loss"]

    res = {}
    for k in LARGE:
        gk = reduced[k]
        d, mo, vo = _adamw("adamw_" + k, given[k][0], gk, given["m_" + k][0], given["v_" + k][0])
        res[k] = tuple(t[None] for t in (gk, d, mo, vo))
    zero = {"loss": jnp.zeros((), F32)}
    packed = [_pack_small({**zero, **{k: src[k] for k in SMALL}}) for src in (
        {k: given[k] for k in SMALL}, red_small, {k: given["m_" + k] for k in SMALL}, {k: given["v_" + k] for k in SMALL})]
    d_s, m_s, v_s = (_unpack_small(t, shapes) for t in _adamw("adamw_small", *packed))
    for k in SMALL:
        res[k] = (red_small[k], d_s[k], m_s[k], v_s[k])

    outs = [loss, grad_x.reshape(1, T, D)]
    for part in range(4):
        outs += [res[k][part] for k in WEIGHTS]
    return tuple(outs)
```

```python
import functools

import numpy as np
import jax
import jax.numpy as jnp
from jax import lax
from jax.experimental import pallas as pl
from jax.experimental.pallas import tpu as pltpu

F32 = jnp.float32
BF16 = jnp.bfloat16
MESH = pl.DeviceIdType.MESH

HEAD_DIM = 128
FOX_HEADS = 6
SWA_HEADS = 6
SWA_KV_HEADS = 2
SWA_GROUP = SWA_HEADS // SWA_KV_HEADS
MEM_HEADS = 4
WINDOW = 128
EPS = 1e-6
NEG_INF = -1e30
SCALE = HEAD_DIM ** -0.5

C_FQ = 0
C_FK = C_FQ + FOX_HEADS * HEAD_DIM
C_FV = C_FK + FOX_HEADS * HEAD_DIM
C_SQ = C_FV + FOX_HEADS * HEAD_DIM
C_SK = C_SQ + SWA_HEADS * HEAD_DIM
C_SV = C_SK + SWA_KV_HEADS * HEAD_DIM
C_MQ = C_SV + SWA_KV_HEADS * HEAD_DIM
C_FL = C_MQ + MEM_HEADS * HEAD_DIM
PROJ_W = C_FL + HEAD_DIM
FOX_W = FOX_HEADS * HEAD_DIM
REF_GROUPS = [
    (0, FOX_W, C_FQ), (FOX_W, FOX_W, C_FK), (2 * FOX_W, FOX_W, C_FV), (3 * FOX_W, FOX_HEADS, C_FL),
    (3 * FOX_W + FOX_HEADS, SWA_HEADS * HEAD_DIM, C_SQ),
    (3 * FOX_W + FOX_HEADS + SWA_HEADS * HEAD_DIM, SWA_KV_HEADS * HEAD_DIM, C_SK),
    (3 * FOX_W + FOX_HEADS + (SWA_HEADS + SWA_KV_HEADS) * HEAD_DIM, SWA_KV_HEADS * HEAD_DIM, C_SV),
    (3 * FOX_W + FOX_HEADS + (SWA_HEADS + 2 * SWA_KV_HEADS) * HEAD_DIM, MEM_HEADS * HEAD_DIM, C_MQ),
]

ADAM_LR = 0.001
ADAM_B1 = 0.9
ADAM_B2 = 0.999
ADAM_EPS = 1e-08
ADAM_WD = 0.01
ADAM_STEP = 10

V7X_VMEM_LIMIT = 56 * 1024 * 1024
N_CHIPS = 4
N_DEV = 8


def _tile(n, pref, mult=128):
    t = (min(pref, n) // mult) * mult
    while t >= mult:
        if n % t == 0:
            return t
        t -= mult
    return n


def _params(sem):
    return pltpu.CompilerParams(dimension_semantics=sem, vmem_limit_bytes=V7X_VMEM_LIMIT)


_DIMS = {"nn": (((1,), (0,)), ((), ())), "nt": (((1,), (1,)), ((), ())), "tn": (((0,), (0,)), ((), ()))}


def _dot(a, b, mode):
    return lax.dot_general(a, b, _DIMS[mode], preferred_element_type=F32)


def _mm(name, grid, pairs, acc_of, acc_shapes, extras, outs, epilogue):
    n_p, n_e, n_o, n_a = len(pairs), len(extras), len(outs), len(acc_shapes)
    nk = grid[2]

    def body(*refs):
        ab = refs[:2 * n_p]
        ex = refs[2 * n_p:2 * n_p + n_e]
        out = refs[2 * n_p + n_e:2 * n_p + n_e + n_o]
        accs = refs[2 * n_p + n_e + n_o:]
        parts = [None] * n_a
        for p in range(n_p):
            d = _dot(ab[2 * p][...], ab[2 * p + 1][...], pairs[p][4])
            parts[acc_of[p]] = d if parts[acc_of[p]] is None else parts[acc_of[p]] + d

        def finish(vals):
            for o, r in zip(out, epilogue(vals, [e[...] for e in ex])):
                o[...] = r.astype(o.dtype)

        if nk == 1:
            finish(parts)
            return
        k = pl.program_id(2)

        @pl.when(k == 0)
        def _():
            for a, d in zip(accs, parts):
                a[...] = d

        @pl.when(k > 0)
        def _():
            for a, d in zip(accs, parts):
                a[...] += d

        @pl.when(k == nk - 1)
        def _():
            finish([a[...] for a in accs])

    in_specs, args = [], []
    for a, a_spec, b, b_spec, _ in pairs:
        in_specs += [a_spec, b_spec]
        args += [a, b]
    for e, e_spec in extras:
        in_specs.append(e_spec)
        args.append(e)
    res = pl.pallas_call(
        body, name=name, grid=grid, in_specs=in_specs,
        out_specs=[s for _, s in outs], out_shape=[o for o, _ in outs],
        scratch_shapes=[pltpu.VMEM(s, F32) for s in acc_shapes] if nk > 1 else [],
        compiler_params=_params(("parallel", "parallel", "arbitrary")),
    )(*args)
    return res


def _mm2d(name, a, b, mode, out_dtype, tm=512, tn=1024, tk=1024, extras=(), epilogue=None, n_out=1):
    if mode == "nn":
        (M, K), N = a.shape, b.shape[1]
    elif mode == "nt":
        (M, K), N = a.shape, b.shape[0]
    else:
        (K, M), N = a.shape, b.shape[1]
    tm, tn, tk = _tile(M, tm), _tile(N, tn), _tile(K, tk)
    a_spec = pl.BlockSpec((tk, tm), lambda i, j, k: (k, i)) if mode == "tn" else pl.BlockSpec((tm, tk), lambda i, j, k: (i, k))
    b_spec = pl.BlockSpec((tn, tk), lambda i, j, k: (j, k)) if mode == "nt" else pl.BlockSpec((tk, tn), lambda i, j, k: (k, j))
    mn = pl.BlockSpec((tm, tn), lambda i, j, k: (i, j))
    if epilogue is None:
        epilogue = lambda accs, ex: [accs[0]]
    if not isinstance(out_dtype, (list, tuple)):
        out_dtype = [out_dtype] * n_out
    res = _mm(name, (M // tm, N // tn, K // tk), [(a, a_spec, b, b_spec, mode)], [0], [(tm, tn)],
              [(e, mn) for e in extras], [(jax.ShapeDtypeStruct((M, N), d), mn) for d in out_dtype], epilogue)
    return res[0] if len(res) == 1 else res


def _sigmoid(x):
    return 1.0 / (1.0 + jnp.exp(-x))


def _ffn_gu(name, h, wg, wu):
    T, D = h.shape
    nf, _, F4 = wg.shape
    tm, tk = _tile(T, 512), _tile(D, 1024)
    a_spec = pl.BlockSpec((tm, tk), lambda i, j, k: (i, k))
    b_spec = pl.BlockSpec((None, tk, F4), lambda i, j, k: (j, k, 0))
    o_spec = pl.BlockSpec((tm, F4), lambda i, j, k: (i, j))

    def epilogue(accs, ex):
        g, u = accs
        return [g, u, g * _sigmoid(g) * u]

    sds = jax.ShapeDtypeStruct((T, nf * F4), BF16)
    return _mm(name, (T // tm, nf, D // tk), [(h, a_spec, wg, b_spec, "nn"), (h, a_spec, wu, b_spec, "nn")], [0, 1],
               [(tm, F4), (tm, F4)], [], [(sds, o_spec)] * 3, epilogue)


def _ffn_down(name, a, wd, xres):
    return _mm2d(name, a, wd, "nn", F32, tk=wd.shape[0] // N_CHIPS, extras=[xres],
                 epilogue=lambda accs, ex: [ex[0] + 0.5 * accs[0]])


def _ffn_bwd(tag, dyb, h, g, u, a, wg, wu, wd):
    T, D = h.shape
    nf, _, F4 = wg.shape

    def act_bwd(accs, ex):
        gf, uf = ex[0].astype(F32), ex[1].astype(F32)
        s = _sigmoid(gf)
        return [accs[0] * uf * s * (1.0 + gf * (1.0 - s)), accs[0] * gf * s]

    dg, du = _mm2d(tag + "_da", dyb, wd, "nt", BF16, tn=F4, extras=[g, u], epilogue=act_bwd, n_out=2)
    dwd = _mm2d(tag + "_dwd", a, dyb, "tn", F32, tm=F4, tk=512)

    tm, tk = _tile(D, 512), _tile(T, 512)
    h_spec = pl.BlockSpec((tk, tm), lambda i, j, k: (k, i))
    d_spec = pl.BlockSpec((tk, F4), lambda i, j, k: (k, j))
    w_spec = pl.BlockSpec((None, tm, F4), lambda i, j, k: (j, i, 0))
    sds = jax.ShapeDtypeStruct((nf, D, F4), F32)
    dwg, dwu = _mm(tag + "_dwgu", (D // tm, nf, T // tk), [(h, h_spec, dg, d_spec, "tn"), (h, h_spec, du, d_spec, "tn")],
                   [0, 1], [(tm, F4), (tm, F4)], [], [(sds, w_spec)] * 2, lambda accs, ex: accs)

    tm, tn = _tile(T, 512), _tile(D, 1024)
    a_spec = pl.BlockSpec((tm, F4), lambda i, j, k: (i, k))
    b_spec = pl.BlockSpec((None, tn, F4), lambda i, j, k: (k, j, 0))
    o_spec = pl.BlockSpec((tm, tn), lambda i, j, k: (i, j))
    (dh,) = _mm(tag + "_dh", (T // tm, D // tn, nf), [(dg, a_spec, wg, b_spec, "nt"), (du, a_spec, wu, b_spec, "nt")],
                [0, 0], [(tm, tn)], [], [(jax.ShapeDtypeStruct((T, D), F32), o_spec)], lambda accs, ex: accs)
    return dwg, dwu, dwd, dh


def _rms_fwd(name, x, gain):
    R, D = x.shape
    tr = _tile(R, 256, 8)

    def body(x_ref, g_ref, o_ref):
        xv = x_ref[...]
        r = lax.rsqrt(jnp.mean(xv * xv, axis=-1, keepdims=True) + EPS)
        o_ref[...] = (xv * r * g_ref[...]).astype(BF16)

    return pl.pallas_call(
        body, name=name, grid=(R // tr,),
        in_specs=[pl.BlockSpec((tr, D), lambda i: (i, 0)), pl.BlockSpec((1, D), lambda i: (0, 0))],
        out_specs=pl.BlockSpec((tr, D), lambda i: (i, 0)), out_shape=jax.ShapeDtypeStruct((R, D), BF16),
        compiler_params=_params(("parallel",)),
    )(x, gain)


def _rms_bwd(name, dh, x, gain, dres, bscale):
    R, D = x.shape
    tr = _tile(R, 256, 8)

    def body(dh_ref, x_ref, g_ref, dres_ref, dx_ref, dxb_ref, dg_ref):
        xv, dy = x_ref[...], dh_ref[...]
        r = lax.rsqrt(jnp.mean(xv * xv, axis=-1, keepdims=True) + EPS)
        xn = xv * r
        uu = dy * g_ref[...]
        dx = dres_ref[...] + r * (uu - xn * jnp.mean(xn * uu, axis=-1, keepdims=True))
        dx_ref[...] = dx
        dxb_ref[...] = (bscale * dx).astype(BF16)
        part = jnp.sum(dy * xn, axis=0, keepdims=True)

        @pl.when(pl.program_id(0) == 0)
        def _():
            dg_ref[...] = part

        @pl.when(pl.program_id(0) > 0)
        def _():
            dg_ref[...] += part

    row = pl.BlockSpec((tr, D), lambda i: (i, 0))
    vec = pl.BlockSpec((1, D), lambda i: (0, 0))
    return pl.pallas_call(
        body, name=name, grid=(R // tr,), in_specs=[row, row, vec, row], out_specs=[row, row, vec],
        out_shape=[jax.ShapeDtypeStruct((R, D), F32), jax.ShapeDtypeStruct((R, D), BF16), jax.ShapeDtypeStruct((1, D), F32)],
        compiler_params=_params(("arbitrary",)),
    )(dh, x, gain, dres)


def _loss_head(y, target):
    R, D = y.shape
    tr = _tile(R, 256, 8)

    def body(y_ref, t_ref, d_ref, db_ref, l_ref):
        e = y_ref[...] - t_ref[...]
        d = e * (1.0 / D)
        d_ref[...] = d
        db_ref[...] = (0.5 * d).astype(BF16)
        part = jnp.zeros((8, 128), F32) + (0.5 / D) * jnp.sum(e * e)

        @pl.when(pl.program_id(0) == 0)
        def _():
            l_ref[...] = part

        @pl.when(pl.program_id(0) > 0)
        def _():
            l_ref[...] += part

    row = pl.BlockSpec((tr, D), lambda i: (i, 0))
    acc = pl.BlockSpec((8, 128), lambda i: (0, 0))
    return pl.pallas_call(
        body, name="loss_head", grid=(R // tr,), in_specs=[row, row], out_specs=[row, row, acc],
        out_shape=[jax.ShapeDtypeStruct((R, D), F32), jax.ShapeDtypeStruct((R, D), BF16), jax.ShapeDtypeStruct((8, 128), F32)],
        compiler_params=_params(("arbitrary",)),
    )(y, target)


def _head_norm(xs, g):
    r = lax.rsqrt(jnp.mean(xs * xs, axis=-1, keepdims=True) + EPS)
    return xs * r * g


def _head_norm_bwd(xs, g, dy):
    r = lax.rsqrt(jnp.mean(xs * xs, axis=-1, keepdims=True) + EPS)
    xn = xs * r
    uu = dy * g
    return r * (uu - xn * jnp.mean(xn * uu, axis=-1, keepdims=True)), jnp.sum(dy * xn, axis=0, keepdims=True)


NORMED = [(C_FQ, FOX_HEADS, 0), (C_FK, FOX_HEADS, 1), (C_SQ, SWA_HEADS, 2), (C_SK, SWA_KV_HEADS, 3), (C_MQ, MEM_HEADS, 4)]
PLAIN = [(C_FV, FOX_HEADS), (C_SV, SWA_KV_HEADS)]


def _prep_fwd(proj, gains):
    T = proj.shape[0]
    tr = _tile(T, 256, 8)

    def body(p_ref, g_ref, o_ref, lf_ref):
        for start, heads, row in NORMED:
            gn = g_ref[row:row + 1, :]
            for hh in range(heads):
                sl = slice(start + hh * HEAD_DIM, start + (hh + 1) * HEAD_DIM)
                o_ref[:, sl] = _head_norm(p_ref[:, sl], gn).astype(BF16)
        for start, heads in PLAIN:
            sl = slice(start, start + heads * HEAD_DIM)
            o_ref[:, sl] = p_ref[:, sl].astype(BF16)
        zb = p_ref[:, C_FL:C_FL + HEAD_DIM] + g_ref[5:6, :]
        o_ref[:, C_FL:C_FL + HEAD_DIM] = jnp.zeros((tr, HEAD_DIM), BF16)
        lf_ref[...] = jnp.minimum(zb, 0.0) - jnp.log(1.0 + jnp.exp(-jnp.abs(zb)))

    return pl.pallas_call(
        body, name="prep_fwd", grid=(T // tr,),
        in_specs=[pl.BlockSpec((tr, PROJ_W), lambda i: (i, 0)), pl.BlockSpec((8, 128), lambda i: (0, 0))],
        out_specs=[pl.BlockSpec((tr, PROJ_W), lambda i: (i, 0)), pl.BlockSpec((tr, HEAD_DIM), lambda i: (i, 0))],
        out_shape=[jax.ShapeDtypeStruct((T, PROJ_W), BF16), jax.ShapeDtypeStruct((T, HEAD_DIM), F32)],
        compiler_params=_params(("parallel",)),
    )(proj, gains)


def _prep_bwd(proj, gains, dfq, dfk, dfv, dsq, dsk, dsv, dmq, dlogf):
    T = proj.shape[0]
    tr = _tile(T, 256, 8)
    d_normed = {C_FQ: 0, C_FK: 1, C_SQ: 3, C_SK: 4, C_MQ: 6}
    d_plain = {C_FV: 2, C_SV: 5}

    def body(p_ref, g_ref, *rest):
        d_refs, dlf_ref, o_ref, dg_ref = rest[:7], rest[7], rest[8], rest[9]
        rows = []
        for start, heads, row in NORMED:
            gn = g_ref[row:row + 1, :]
            d_ref = d_refs[d_normed[start]]
            tot = jnp.zeros((1, HEAD_DIM), F32)
            for hh in range(heads):
                sl = slice(start + hh * HEAD_DIM, start + (hh + 1) * HEAD_DIM)
                dx, dgn = _head_norm_bwd(p_ref[:, sl], gn, d_ref[:, hh * HEAD_DIM:(hh + 1) * HEAD_DIM])
                o_ref[:, sl] = dx.astype(BF16)
                tot = tot + dgn
            rows.append(tot)
        for start, heads in PLAIN:
            o_ref[:, start:start + heads * HEAD_DIM] = d_refs[d_plain[start]][...].astype(BF16)
        zb = p_ref[:, C_FL:C_FL + HEAD_DIM] + g_ref[5:6, :]
        lane = lax.broadcasted_iota(jnp.int32, (tr, HEAD_DIM), 1)
        dz = jnp.where(lane < FOX_HEADS, dlf_ref[...] * (1.0 - _sigmoid(zb)), 0.0)
        o_ref[:, C_FL:C_FL + HEAD_DIM] = dz.astype(BF16)
        rows.append(jnp.sum(dz, axis=0, keepdims=True))
        part = jnp.concatenate(rows + [jnp.zeros((2, HEAD_DIM), F32)], axis=0)

        @pl.when(pl.program_id(0) == 0)
        def _():
            dg_ref[...] = part

        @pl.when(pl.program_id(0) > 0)
        def _():
            dg_ref[...] += part

    def rows_of(w):
        return pl.BlockSpec((tr, w), lambda i: (i, 0))

    small = pl.BlockSpec((8, 128), lambda i: (0, 0))
    ds = [dfq, dfk, dfv, dsq, dsk, dsv, dmq]
    return pl.pallas_call(
        body, name="prep_bwd", grid=(T // tr,),
        in_specs=[rows_of(PROJ_W), small] + [rows_of(d.shape[1]) for d in ds] + [rows_of(HEAD_DIM)],
        out_specs=[rows_of(PROJ_W), small],
        out_shape=[jax.ShapeDtypeStruct((T, PROJ_W), BF16), jax.ShapeDtypeStruct((8, 128), F32)],
        compiler_params=_params(("arbitrary",)),
    )(proj, gains, *ds, dlogf)


def _head_norm_rows(x, gain):
    R, W = x.shape

    def body(x_ref, g_ref, o_ref):
        for hh in range(W // HEAD_DIM):
            sl = slice(hh * HEAD_DIM, (hh + 1) * HEAD_DIM)
            o_ref[:, sl] = _head_norm(x_ref[:, sl], g_ref[...]).astype(BF16)

    return pl.pallas_call(body, name="mem_k_norm", out_shape=jax.ShapeDtypeStruct((R, W), BF16))(x, gain)


def _head_norm_rows_bwd(x, gain, dy):
    R, W = x.shape

    def body(x_ref, g_ref, dy_ref, dx_ref, dg_ref):
        tot = jnp.zeros((1, HEAD_DIM), F32)
        for hh in range(W // HEAD_DIM):
            sl = slice(hh * HEAD_DIM, (hh + 1) * HEAD_DIM)
            dx, dgn = _head_norm_bwd(x_ref[:, sl], g_ref[...], dy_ref[:, sl])
            dx_ref[:, sl] = dx.astype(BF16)
            tot = tot + dgn
        dg_ref[...] = tot

    return pl.pallas_call(
        body, name="mem_k_norm_bwd",
        out_shape=[jax.ShapeDtypeStruct((R, W), BF16), jax.ShapeDtypeStruct((1, HEAD_DIM), F32)])(x, gain, dy)


def _cumsum_rows(name, x, reverse):
    T, W = x.shape
    tb = _tile(T, 512, 8)
    nb = T // tb

    def body(x_ref, o_ref, carry):
        @pl.when(pl.program_id(0) == 0)
        def _():
            carry[...] = jnp.zeros_like(carry)

        xv = x_ref[...]
        r = lax.broadcasted_iota(jnp.int32, (tb, tb), 0)
        cc = lax.broadcasted_iota(jnp.int32, (tb, tb), 1)
        tri = jnp.where((cc >= r) if reverse else (cc <= r), 1.0, 0.0).astype(F32)
        o_ref[...] = jnp.dot(tri, xv, precision=lax.Precision.HIGHEST, preferred_element_type=F32) + carry[...]
        carry[...] += jnp.sum(xv, axis=0, keepdims=True)

    idx = (lambda i: (nb - 1 - i, 0)) if reverse else (lambda i: (i, 0))
    return pl.pallas_call(
        body, name=name, grid=(nb,), in_specs=[pl.BlockSpec((tb, W), idx)], out_specs=pl.BlockSpec((tb, W), idx),
        out_shape=jax.ShapeDtypeStruct((T, W), F32), scratch_shapes=[pltpu.VMEM((1, W), F32)],
        compiler_params=_params(("arbitrary",)),
    )(x)


def _fox_fwd(qkv, cq, ck):
    T = qkv.shape[0]
    tq = _tile(T, 512)
    nq = T // tq
    HQ, HK, HV = C_FQ // HEAD_DIM, C_FK // HEAD_DIM, C_FV // HEAD_DIM

    def body(q_ref, k_ref, v_ref, cq_ref, ck_ref, o_ref, lse_ref, m_sc, l_sc, acc_sc):
        i, j = pl.program_id(1), pl.program_id(2)

        @pl.when(j == 0)
        def _():
            m_sc[...] = jnp.full_like(m_sc, NEG_INF)
            l_sc[...] = jnp.zeros_like(l_sc)
            acc_sc[...] = jnp.zeros_like(acc_sc)

        @pl.when(j <= i)
        def _():
            s = _dot(q_ref[...], k_ref[...], "nt") * SCALE + (cq_ref[...] - ck_ref[...])
            rows = i * tq + lax.broadcasted_iota(jnp.int32, (tq, tq), 0)
            cols = j * tq + lax.broadcasted_iota(jnp.int32, (tq, tq), 1)
            s = jnp.where(cols <= rows, s, NEG_INF)
            m_new = jnp.maximum(m_sc[...], jnp.max(s, axis=-1, keepdims=True))
            alpha = jnp.exp(m_sc[...] - m_new)
            p = jnp.exp(s - m_new)
            l_sc[...] = alpha * l_sc[...] + jnp.sum(p, axis=-1, keepdims=True)
            acc_sc[...] = alpha * acc_sc[...] + _dot(p.astype(BF16), v_ref[...], "nn")
            m_sc[...] = m_new

        @pl.when(j == i)
        def _():
            o_ref[...] = (acc_sc[...] / l_sc[...]).astype(BF16)
            lse_ref[...] = m_sc[...] + jnp.log(l_sc[...])

    def col(base):
        return pl.BlockSpec((tq, HEAD_DIM), lambda h, i, j: (jnp.minimum(j, i), base + h))

    return pl.pallas_call(
        body, name="fox_fwd", grid=(FOX_HEADS, nq, nq),
        in_specs=[pl.BlockSpec((tq, HEAD_DIM), lambda h, i, j: (i, HQ + h)), col(HK), col(HV),
                  pl.BlockSpec((None, tq, 1), lambda h, i, j: (h, i, 0)),
                  pl.BlockSpec((None, 1, tq), lambda h, i, j: (h, 0, jnp.minimum(j, i)))],
        out_specs=[pl.BlockSpec((tq, HEAD_DIM), lambda h, i, j: (i, h)),
                   pl.BlockSpec((None, tq, 1), lambda h, i, j: (h, i, 0))],
        out_shape=[jax.ShapeDtypeStruct((T, FOX_W), BF16), jax.ShapeDtypeStruct((FOX_HEADS, T, 1), F32)],
        scratch_shapes=[pltpu.VMEM((tq, 1), F32), pltpu.VMEM((tq, 1), F32), pltpu.VMEM((tq, HEAD_DIM), F32)],
        compiler_params=_params(("parallel", "arbitrary", "arbitrary")),
    )(qkv, qkv, qkv, cq, ck)


def _fox_bwd(qkv, cq, ck, out, lse, dmix):
    T = qkv.shape[0]
    tq = _tile(T, 512)
    nq = T // tq
    HQ, HK, HV = C_FQ // HEAD_DIM, C_FK // HEAD_DIM, C_FV // HEAD_DIM

    def body(q_ref, k_ref, v_ref, cq_ref, ck_ref, o_ref, lse_ref, do_ref, dq_ref, dk_ref, dv_ref, dck_ref, dk_sc, dv_sc, dc_sc):
        kj, qi = pl.program_id(1), pl.program_id(2)

        @pl.when(qi == kj)
        def _():
            dk_sc[...] = jnp.zeros_like(dk_sc)
            dv_sc[...] = jnp.zeros_like(dv_sc)
            dc_sc[...] = jnp.zeros_like(dc_sc)

        @pl.when(qi >= kj)
        def _():
            q, k, v, do = q_ref[...], k_ref[...], v_ref[...], do_ref[...]
            s = _dot(q, k, "nt") * SCALE + (cq_ref[...] - ck_ref[...])
            rows = qi * tq + lax.broadcasted_iota(jnp.int32, (tq, tq), 0)
            cols = kj * tq + lax.broadcasted_iota(jnp.int32, (tq, tq), 1)
            p = jnp.where(cols <= rows, jnp.exp(s - lse_ref[...]), 0.0)
            dp = _dot(do, v, "nt")
            delta = jnp.sum(do.astype(F32) * o_ref[...].astype(F32), axis=-1, keepdims=True)
            ds = p * (dp - delta)
            dsb = ds.astype(BF16)
            dv_sc[...] += _dot(p.astype(BF16), do, "tn")
            dk_sc[...] += _dot(dsb, q, "tn")
            dc_sc[...] += jnp.sum(ds, axis=0, keepdims=True)
            dq_part = _dot(dsb, k, "nn") * SCALE
            rows_q = pl.ds(pl.multiple_of(qi * tq, tq), tq)

            @pl.when(kj == 0)
            def _():
                dq_ref[rows_q, :] = dq_part

            @pl.when(kj > 0)
            def _():
                dq_ref[rows_q, :] += dq_part

        @pl.when(qi == nq - 1)
        def _():
            dk_ref[...] = dk_sc[...] * SCALE
            dv_ref[...] = dv_sc[...]
            dck_ref[...] = -dc_sc[...]

    def qrow(base):
        return pl.BlockSpec((tq, HEAD_DIM), lambda h, kj, qi: (jnp.maximum(qi, kj), base + h))

    def krow(base):
        return pl.BlockSpec((tq, HEAD_DIM), lambda h, kj, qi: (kj, base + h))

    qcol = pl.BlockSpec((None, tq, 1), lambda h, kj, qi: (h, jnp.maximum(qi, kj), 0))
    kvec = pl.BlockSpec((None, 1, tq), lambda h, kj, qi: (h, 0, kj))
    sds = jax.ShapeDtypeStruct((T, FOX_W), F32)
    return pl.pallas_call(
        body, name="fox_bwd", grid=(FOX_HEADS, nq, nq),
        in_specs=[qrow(HQ), krow(HK), krow(HV), qcol, kvec, qrow(0), qcol, qrow(0)],
        out_specs=[pl.BlockSpec((T, HEAD_DIM), lambda h, kj, qi: (0, h)), krow(0), krow(0), kvec],
        out_shape=[sds, sds, sds, jax.ShapeDtypeStruct((FOX_HEADS, 1, T), F32)],
        scratch_shapes=[pltpu.VMEM((tq, HEAD_DIM), F32), pltpu.VMEM((tq, HEAD_DIM), F32), pltpu.VMEM((1, tq), F32)],
        compiler_params=_params(("parallel", "arbitrary", "arbitrary")),
    )(qkv, qkv, qkv, cq, ck, out, lse, dmix)


GW = SWA_GROUP * HEAD_DIM
GR = SWA_GROUP * WINDOW


def _swa_scores(q_ref, kp_ref, kc_ref, slope_ref, n):
    q = q_ref[...]
    qs = jnp.concatenate([q[:, t * HEAD_DIM:(t + 1) * HEAD_DIM] for t in range(SWA_GROUP)], axis=0)
    kb = jnp.concatenate([kp_ref[...], kc_ref[...]], axis=0)
    r = lax.broadcasted_iota(jnp.int32, (GR, 2 * WINDOW), 0) & (WINDOW - 1)
    jj = lax.broadcasted_iota(jnp.int32, (GR, 2 * WINDOW), 1)
    dist = WINDOW + r - jj
    valid = (dist >= 0) & (dist < WINDOW) & ((n > 0) | (jj >= WINDOW))
    s = _dot(qs, kb, "nt") * SCALE - slope_ref[...] * dist.astype(F32)
    return qs, kb, jnp.where(valid, s, NEG_INF), valid


def _swa_specs():
    HQ, HK, HV = C_SQ // GW, C_SK // HEAD_DIM, C_SV // HEAD_DIM
    q_spec = pl.BlockSpec((WINDOW, GW), lambda g, n: (n, HQ + g))

    def prev(base):
        return pl.BlockSpec((WINDOW, HEAD_DIM), lambda g, n: (jnp.maximum(n - 1, 0), base + g))

    def cur(base):
        return pl.BlockSpec((WINDOW, HEAD_DIM), lambda g, n: (n, base + g))

    col = pl.BlockSpec((None, GR, 1), lambda g, n: (g, 0, 0))
    return q_spec, prev(HK), cur(HK), prev(HV), cur(HV), col


def _swa_fwd(qkv, slopes, sinks):
    T = qkv.shape[0]
    nb = T // WINDOW
    assert C_SQ % GW == 0

    def body(q_ref, kp_ref, kc_ref, vp_ref, vc_ref, slope_ref, sink_ref, o_ref, lse_ref):
        n = pl.program_id(1)
        _, _, s, _ = _swa_scores(q_ref, kp_ref, kc_ref, slope_ref, n)
        m = jnp.maximum(jnp.max(s, axis=-1, keepdims=True), sink_ref[...])
        p = jnp.exp(s - m)
        l = jnp.sum(p, axis=-1, keepdims=True) + jnp.exp(sink_ref[...] - m)
        vb = jnp.concatenate([vp_ref[...], vc_ref[...]], axis=0)
        o = _dot(p.astype(BF16), vb, "nn") / l
        for t in range(SWA_GROUP):
            o_ref[:, t * HEAD_DIM:(t + 1) * HEAD_DIM] = o[t * WINDOW:(t + 1) * WINDOW, :].astype(BF16)
        lse_ref[...] = m + jnp.log(l)

    q_spec, kp, kc, vp, vc, col = _swa_specs()
    return pl.pallas_call(
        body, name="swa_fwd", grid=(SWA_KV_HEADS, nb), in_specs=[q_spec, kp, kc, vp, vc, col, col],
        out_specs=[pl.BlockSpec((WINDOW, GW), lambda g, n: (n, g)), pl.BlockSpec((None, None, GR, 1), lambda g, n: (g, n, 0, 0))],
        out_shape=[jax.ShapeDtypeStruct((T, SWA_HEADS * HEAD_DIM), BF16), jax.ShapeDtypeStruct((SWA_KV_HEADS, nb, GR, 1), F32)],
        compiler_params=_params(("parallel", "arbitrary")),
    )(qkv, qkv, qkv, qkv, qkv, slopes, sinks)


def _swa_bwd(qkv, slopes, sinks, out, lse, dmix):
    T = qkv.shape[0]
    nb = T // WINDOW
    DO = FOX_W // GW
    assert FOX_W % GW == 0

    def body(q_ref, kp_ref, kc_ref, vp_ref, vc_ref, slope_ref, sink_ref, o_ref, lse_ref, do_ref,
             dq_ref, dk_ref, dv_ref, dsink_ref, sink_sc):
        n = pl.program_id(1)

        @pl.when(n == 0)
        def _():
            dk_ref[...] = jnp.zeros_like(dk_ref)
            dv_ref[...] = jnp.zeros_like(dv_ref)
            sink_sc[...] = jnp.zeros_like(sink_sc)

        qs, kb, s, valid = _swa_scores(q_ref, kp_ref, kc_ref, slope_ref, n)
        lse = lse_ref[...]
        p = jnp.where(valid, jnp.exp(s - lse), 0.0)
        vb = jnp.concatenate([vp_ref[...], vc_ref[...]], axis=0)
        do = jnp.concatenate([do_ref[:, t * HEAD_DIM:(t + 1) * HEAD_DIM] for t in range(SWA_GROUP)], axis=0)
        oo = jnp.concatenate([o_ref[:, t * HEAD_DIM:(t + 1) * HEAD_DIM] for t in range(SWA_GROUP)], axis=0)
        dp = _dot(do, vb, "nt")
        delta = jnp.sum(do.astype(F32) * oo.astype(F32), axis=-1, keepdims=True)
        ds = p * (dp - delta)
        dsb = ds.astype(BF16)
        dq = _dot(dsb, kb, "nn") * SCALE
        for t in range(SWA_GROUP):
            dq_ref[:, t * HEAD_DIM:(t + 1) * HEAD_DIM] = dq[t * WINDOW:(t + 1) * WINDOW, :]
        dkb = _dot(dsb, qs, "tn") * SCALE
        dvb = _dot(p.astype(BF16), do, "tn")
        r_prev = pl.ds(pl.multiple_of(jnp.maximum(n - 1, 0) * WINDOW, WINDOW), WINDOW)
        r_cur = pl.ds(pl.multiple_of(n * WINDOW, WINDOW), WINDOW)
        dk_ref[r_prev, :] += dkb[:WINDOW, :]
        dk_ref[r_cur, :] += dkb[WINDOW:, :]
        dv_ref[r_prev, :] += dvb[:WINDOW, :]
        dv_ref[r_cur, :] += dvb[WINDOW:, :]
        sink_sc[...] -= jnp.exp(sink_ref[...] - lse) * delta

        @pl.when(n == nb - 1)
        def _():
            tot = [jnp.zeros((1, 128), F32) + jnp.sum(sink_sc[t * WINDOW:(t + 1) * WINDOW, :]) for t in range(SWA_GROUP)]
            dsink_ref[...] = jnp.concatenate(tot + [jnp.zeros((8 - SWA_GROUP, 128), F32)], axis=0)

    q_spec, kp, kc, vp, vc, col = _swa_specs()
    kv_acc = pl.BlockSpec((T, HEAD_DIM), lambda g, n: (0, g))
    return pl.pallas_call(
        body, name="swa_bwd", grid=(SWA_KV_HEADS, nb),
        in_specs=[q_spec, kp, kc, vp, vc, col, col, pl.BlockSpec((WINDOW, GW), lambda g, n: (n, g)),
                  pl.BlockSpec((None, None, GR, 1), lambda g, n: (g, n, 0, 0)), pl.BlockSpec((WINDOW, GW), lambda g, n: (n, DO + g))],
        out_specs=[pl.BlockSpec((WINDOW, GW), lambda g, n: (n, g)), kv_acc, kv_acc, pl.BlockSpec((None, 8, 128), lambda g, n: (g, 0, 0))],
        out_shape=[jax.ShapeDtypeStruct((T, SWA_HEADS * HEAD_DIM), F32), jax.ShapeDtypeStruct((T, SWA_KV_HEADS * HEAD_DIM), F32),
                   jax.ShapeDtypeStruct((T, SWA_KV_HEADS * HEAD_DIM), F32), jax.ShapeDtypeStruct((SWA_KV_HEADS, 8, 128), F32)],
        scratch_shapes=[pltpu.VMEM((GR, 1), F32)],
        compiler_params=_params(("parallel", "arbitrary")),
    )(qkv, qkv, qkv, qkv, qkv, slopes, sinks, out, lse, dmix)


def _mem_fwd(qkv, mk, mv):
    T, ML = qkv.shape[0], mk.shape[0]
    tq = _tile(T, 512)
    HQ = C_MQ // HEAD_DIM

    def body(q_ref, k_ref, v_ref, o_ref, lse_ref):
        s = _dot(q_ref[...], k_ref[...], "nt") * SCALE
        m = jnp.max(s, axis=-1, keepdims=True)
        p = jnp.exp(s - m)
        l = jnp.sum(p, axis=-1, keepdims=True)
        o_ref[...] = (_dot(p.astype(BF16), v_ref[...], "nn") / l).astype(BF16)
        lse_ref[...] = m + jnp.log(l)

    kv = pl.BlockSpec((ML, HEAD_DIM), lambda h, i: (0, h))
    return pl.pallas_call(
        body, name="mem_fwd", grid=(MEM_HEADS, T // tq),
        in_specs=[pl.BlockSpec((tq, HEAD_DIM), lambda h, i: (i, HQ + h)), kv, kv],
        out_specs=[pl.BlockSpec((tq, HEAD_DIM), lambda h, i: (i, h)), pl.BlockSpec((None, tq, 1), lambda h, i: (h, i, 0))],
        out_shape=[jax.ShapeDtypeStruct((T, MEM_HEADS * HEAD_DIM), BF16), jax.ShapeDtypeStruct((MEM_HEADS, T, 1), F32)],
        compiler_params=_params(("parallel", "arbitrary")),
    )(qkv, mk, mv)


def _mem_bwd(qkv, mk, mv, out, lse, dmix):
    T, ML = qkv.shape[0], mk.shape[0]
    tq = _tile(T, 512)
    HQ = C_MQ // HEAD_DIM
    DO = (FOX_W + SWA_HEADS * HEAD_DIM) // HEAD_DIM

    def body(q_ref, k_ref, v_ref, o_ref, lse_ref, do_ref, dq_ref, dk_ref, dv_ref):
        q, k, v, do = q_ref[...], k_ref[...], v_ref[...], do_ref[...]
        p = jnp.exp(_dot(q, k, "nt") * SCALE - lse_ref[...])
        dp = _dot(do, v, "nt")
        delta = jnp.sum(do.astype(F32) * o_ref[...].astype(F32), axis=-1, keepdims=True)
        dsb = (p * (dp - delta)).astype(BF16)
        dq_ref[...] = _dot(dsb, k, "nn") * SCALE
        dk_part = _dot(dsb, q, "tn") * SCALE
        dv_part = _dot(p.astype(BF16), do, "tn")

        @pl.when(pl.program_id(1) == 0)
        def _():
            dk_ref[...] = dk_part
            dv_ref[...] = dv_part

        @pl.when(pl.program_id(1) > 0)
        def _():
            dk_ref[...] += dk_part
            dv_ref[...] += dv_part

    kv = pl.BlockSpec((ML, HEAD_DIM), lambda h, i: (0, h))
    qb = pl.BlockSpec((tq, HEAD_DIM), lambda h, i: (i, h))
    return pl.pallas_call(
        body, name="mem_bwd", grid=(MEM_HEADS, T // tq),
        in_specs=[pl.BlockSpec((tq, HEAD_DIM), lambda h, i: (i, HQ + h)), kv, kv, qb,
                  pl.BlockSpec((None, tq, 1), lambda h, i: (h, i, 0)), pl.BlockSpec((tq, HEAD_DIM), lambda h, i: (i, DO + h))],
        out_specs=[qb, kv, kv],
        out_shape=[jax.ShapeDtypeStruct((T, MEM_HEADS * HEAD_DIM), F32), jax.ShapeDtypeStruct((ML, MEM_HEADS * HEAD_DIM), F32),
                   jax.ShapeDtypeStruct((ML, MEM_HEADS * HEAD_DIM), F32)],
        compiler_params=_params(("parallel", "arbitrary")),
    )(qkv, mk, mv, out, lse, dmix)


HBM = pl.BlockSpec(memory_space=pltpu.HBM)


def _place():
    x, y, c = lax.axis_index("x"), lax.axis_index("y"), lax.axis_index("c")
    chips = [(1 - x, y), (x, 1 - y), (1 - x, 1 - y)]
    return x, y, c, chips


def _remote(src, dst, send_sem, recv_sem, device):
    return pltpu.make_async_remote_copy(src_ref=src, dst_ref=dst, send_sem=send_sem, recv_sem=recv_sem,
                                        device_id=device, device_id_type=MESH)


def _place_ids():
    x, y, c = lax.axis_index("x"), lax.axis_index("y"), lax.axis_index("c")
    order = [2 * x + y, 2 * (1 - x) + y, 2 * x + (1 - y), 2 * (1 - x) + (1 - y)]
    return jnp.stack([2 * x + y, c] + order).astype(jnp.int32)


def _cast_place(name, w, ids):
    R, C = w.shape
    tr = _tile(R, 256, 16)

    def body(ids_ref, w_ref, o_ref):
        o_ref[...] = w_ref[...].astype(BF16)

    return pl.pallas_call(
        body, name=name,
        grid_spec=pltpu.PrefetchScalarGridSpec(
            num_scalar_prefetch=1, grid=(R // tr,), in_specs=[pl.BlockSpec((tr, C), lambda i, ids: (i, 0))],
            out_specs=pl.BlockSpec((None, tr, C), lambda i, ids: (ids[0], i, 0))),
        out_shape=jax.ShapeDtypeStruct((N_CHIPS, R, C), BF16), compiler_params=_params(("parallel",)),
    )(ids, w)


SEM = pl.BlockSpec(memory_space=pltpu.SEMAPHORE)
EFFECT = pltpu.SideEffectType.DATAFLOW_SIDE_EFFECTING


def _hbm(a):
    return pltpu.with_memory_space_constraint(a, pltpu.HBM)


def _gather_start(name, placed, after):
    n = len(placed)

    ns = 3 * n

    def body(*refs):
        send, recv = refs[n + 1:n + 1 + ns], refs[n + 1 + ns:n + 1 + 2 * ns]
        buf = refs[n + 1 + 2 * ns:2 * n + 1 + 2 * ns]
        token = refs[2 * n + 1 + 2 * ns]
        x, y, c, chips = _place()
        me = 2 * x + y
        for a in range(n):
            half = buf[a].shape[1] // 2
            mine = buf[a].at[me, pl.ds(c * half, half)]
            for j, (cx, cy) in enumerate(chips):
                _remote(mine, mine, send[3 * a + j], recv[3 * a + j], (cx, cy, c)).start()
        token[...] = jnp.zeros_like(token)

    res = pl.pallas_call(
        body, name=name, in_specs=[HBM] * n + [pl.BlockSpec(memory_space=pl.ANY)],
        out_specs=[SEM] * (2 * ns) + [HBM] * n + [pl.BlockSpec(memory_space=pltpu.VMEM)],
        out_shape=[pltpu.SemaphoreType.DMA(())] * (2 * ns)
        + [pltpu.HBM(s.shape, s.dtype) for s in placed] + [jax.ShapeDtypeStruct((8, 128), F32)],
        input_output_aliases={a: 2 * ns + a for a in range(n)},
        compiler_params=pltpu.CompilerParams(has_side_effects=EFFECT),
    )(*[_hbm(s) for s in placed], after)
    return list(res[:ns]), list(res[ns:2 * ns]), list(res[2 * ns:2 * ns + n]), res[2 * ns + n]


def _gather_wait(name, send, recv, bufs, after):
    n = len(bufs)

    ns = 3 * n

    def body(*refs):
        buf = refs[:n]
        send_ref, recv_ref = refs[n:n + ns], refs[n + ns:n + 2 * ns]
        x, y, c, chips = _place()
        ids = [2 * cx + cy for cx, cy in chips]
        for a in range(n):
            half = buf[a].shape[1] // 2
            for j in range(3):
                landed = buf[a].at[ids[j], pl.ds(c * half, half)]
                cp = _remote(landed, landed, send_ref[3 * a + j], recv_ref[3 * a + j], (x, y, c))
                cp.wait_send()
                cp.wait_recv()

    res = pl.pallas_call(
        body, name=name, in_specs=[HBM] * n + [SEM] * (2 * ns) + [pl.BlockSpec(memory_space=pl.ANY)], out_specs=[HBM] * n,
        out_shape=[pltpu.HBM(s.shape, s.dtype) for s in bufs], input_output_aliases={a: a for a in range(n)},
        compiler_params=pltpu.CompilerParams(has_side_effects=EFFECT),
    )(*bufs, *send, *recv, after)
    return list(res)


def _gather_forward(name, bufs):
    n = len(bufs)

    def body(*refs):
        buf = refs[n:2 * n]
        send, recv = refs[2 * n:]
        x, y, c, chips = _place()
        ids = [2 * cx + cy for cx, cy in chips]
        copies = []
        for a in range(n):
            half = buf[a].shape[1] // 2
            for j in range(3):
                landed = buf[a].at[ids[j], pl.ds(c * half, half)]
                cp = _remote(landed, landed, send.at[a, j], recv.at[a, j], (x, y, 1 - c))
                cp.start()
                copies.append(cp)
        for a in range(n):
            half = buf[a].shape[1] // 2
            for j in range(3):
                landed = buf[a].at[ids[j], pl.ds((1 - c) * half, half)]
                _remote(landed, landed, send.at[a, j], recv.at[a, j], (x, y, c)).wait_recv()
        for cp in copies:
            cp.wait_send()

    return pl.pallas_call(
        body, name=name, in_specs=[HBM] * n, out_specs=[HBM] * n,
        out_shape=[jax.ShapeDtypeStruct(s.shape, s.dtype) for s in bufs], input_output_aliases={a: a for a in range(n)},
        scratch_shapes=[pltpu.SemaphoreType.DMA((n, 3)), pltpu.SemaphoreType.DMA((n, 3))],
    )(*bufs)


def _pair_exchange(grads):
    n = len(grads)

    def body(*refs):
        src, theirs = refs[:n], refs[n:2 * n]
        send, recv = refs[2 * n:]
        x, y, c, chips = _place()
        order = [2 * x + y] + [2 * cx + cy for cx, cy in chips]
        copies = []
        for a in range(n):
            half = src[a].shape[1] // 2
            for j in range(N_CHIPS):
                cp = _remote(src[a].at[order[j], pl.ds((1 - c) * half, half)], theirs[a].at[j], send.at[a, j], recv.at[a, j], (x, y, 1 - c))
                cp.start()
                copies.append(cp)
        for cp in copies:
            cp.wait()

    return pl.pallas_call(
        body, name="grad_pair_exchange", in_specs=[HBM] * n, out_specs=[HBM] * n,
        out_shape=[jax.ShapeDtypeStruct((N_CHIPS, g.shape[1] // 2, g.shape[2]), g.dtype) for g in grads],
        scratch_shapes=[pltpu.SemaphoreType.DMA((n, N_CHIPS)), pltpu.SemaphoreType.DMA((n, N_CHIPS))],
    )(*grads)


def _chip_exchange(parts):
    n = len(parts)

    def body(*refs):
        src, dst = refs[:n], refs[n:2 * n]
        send, recv = refs[2 * n:]
        x, y, c, chips = _place()
        copies = []
        for a in range(n):
            for j, (cx, cy) in enumerate(chips):
                cp = _remote(src[a].at[j], dst[a].at[j], send.at[a, j], recv.at[a, j], (cx, cy, c))
                cp.start()
                copies.append(cp)
        for cp in copies:
            cp.wait()

    return pl.pallas_call(
        body, name="grad_chip_exchange", in_specs=[HBM] * n, out_specs=[HBM] * n,
        out_shape=[jax.ShapeDtypeStruct(p.shape, p.dtype) for p in parts],
        scratch_shapes=[pltpu.SemaphoreType.DMA((n, 3)), pltpu.SemaphoreType.DMA((n, 3))],
    )(*parts)


def _pair_share(shards):
    n = len(shards)

    def body(*refs):
        buf = refs[n:2 * n]
        send, recv = refs[2 * n:]
        x, y, c, _ = _place()
        copies = []
        for a in range(n):
            half = buf[a].shape[0] // 2
            mine = buf[a].at[pl.ds(c * half, half)]
            cp = _remote(mine, mine, send.at[a], recv.at[a], (x, y, 1 - c))
            cp.start()
            copies.append(cp)
        for a, cp in enumerate(copies):
            half = buf[a].shape[0] // 2
            cp.wait_send()
            theirs = buf[a].at[pl.ds((1 - c) * half, half)]
            _remote(theirs, theirs, send.at[a], recv.at[a], (x, y, c)).wait_recv()

    return pl.pallas_call(
        body, name="grad_pair_share", in_specs=[HBM] * n, out_specs=[HBM] * n,
        out_shape=[jax.ShapeDtypeStruct(s.shape, s.dtype) for s in shards], input_output_aliases={a: a for a in range(n)},
        scratch_shapes=[pltpu.SemaphoreType.DMA((n,)), pltpu.SemaphoreType.DMA((n,))],
    )(*shards)


def _all_reduce_small(buf):
    R, W = buf.shape

    def body(src_ref, out_ref, slots, send, recv):
        x, y, c, _ = _place()
        me = 4 * x + 2 * y + c
        copies = []
        for dx in range(2):
            for dy in range(2):
                for dc in range(2):
                    if dx == dy == dc == 0:
                        continue
                    k = 4 * dx + 2 * dy + dc
                    peer = (x ^ dx, y ^ dy, c ^ dc)
                    cp = _remote(src_ref, slots.at[me], send.at[k], recv.at[k], peer)
                    cp.start()
                    copies.append((cp, k))
        slots[me] = src_ref[...]
        for cp, k in copies:
            cp.wait_send()
            landed = slots.at[me ^ k]
            _remote(landed, landed, send.at[k], recv.at[k], (x, y, c)).wait_recv()
        total = slots[0]
        for d in range(1, N_DEV):
            total = total + slots[d]
        out_ref[...] = total

    return pl.pallas_call(
        body, name="all_reduce_small", in_specs=[pl.BlockSpec(memory_space=pltpu.VMEM)],
        out_specs=pl.BlockSpec(memory_space=pltpu.VMEM), out_shape=jax.ShapeDtypeStruct((R, W), F32),
        scratch_shapes=[pltpu.VMEM((N_DEV, R, W), F32), pltpu.SemaphoreType.DMA((N_DEV,)), pltpu.SemaphoreType.DMA((N_DEV,))],
    )(buf)


def _pair_sum_bf16(name, grad, theirs, ids):
    _, R2, C = theirs.shape
    tr = _tile(R2, 256, 16)
    nrb = R2 // tr

    def body(ids_ref, a_ref, b_ref, o_ref):
        o_ref[...] = (a_ref[...] + b_ref[...]).astype(BF16)

    return pl.pallas_call(
        body, name=name,
        grid_spec=pltpu.PrefetchScalarGridSpec(
            num_scalar_prefetch=1, grid=(3, nrb),
            in_specs=[pl.BlockSpec((None, tr, C), lambda j, i, ids: (ids[3 + j], ids[1] * nrb + i, 0)),
                      pl.BlockSpec((None, tr, C), lambda j, i, ids: (j + 1, i, 0))],
            out_specs=pl.BlockSpec((None, tr, C), lambda j, i, ids: (j, i, 0))),
        out_shape=jax.ShapeDtypeStruct((3, R2, C), BF16), compiler_params=_params(("parallel", "parallel")),
    )(ids, grad, theirs)


def _chip_sum(name, grad, theirs, arrived, ids):
    _, R2, C = theirs.shape
    tr = _tile(R2, 256, 16)
    nrb = R2 // tr

    def body(ids_ref, a_ref, b_ref, r_ref, o_ref):
        tot = a_ref[...] + b_ref[...]
        for j in range(3):
            tot = tot + r_ref[j].astype(F32)
        o_ref[...] = tot

    return pl.pallas_call(
        body, name=name,
        grid_spec=pltpu.PrefetchScalarGridSpec(
            num_scalar_prefetch=1, grid=(nrb,),
            in_specs=[pl.BlockSpec((None, tr, C), lambda i, ids: (ids[0], ids[1] * nrb + i, 0)),
                      pl.BlockSpec((None, tr, C), lambda i, ids: (0, i, 0)),
                      pl.BlockSpec((3, tr, C), lambda i, ids: (0, i, 0))],
            out_specs=pl.BlockSpec((tr, C), lambda i, ids: (ids[1] * nrb + i, 0))),
        out_shape=jax.ShapeDtypeStruct((2 * R2, C), F32), compiler_params=_params(("parallel",)),
    )(ids, grad, theirs, arrived)


def _adamw(name, w, g, m, v):
    R, C = w.shape
    tr = _tile(R, 128, 8)
    c1 = 1.0 / (1.0 - ADAM_B1 ** ADAM_STEP)
    c2 = 1.0 / (1.0 - ADAM_B2 ** ADAM_STEP)

    def body(w_ref, g_ref, m_ref, v_ref, d_ref, mo_ref, vo_ref):
        gv = g_ref[...]
        mn = ADAM_B1 * m_ref[...] + (1.0 - ADAM_B1) * gv
        vn = ADAM_B2 * v_ref[...] + (1.0 - ADAM_B2) * (gv * gv)
        d_ref[...] = -ADAM_LR * ((mn * c1) / (jnp.sqrt(vn * c2) + ADAM_EPS) + ADAM_WD * w_ref[...])
        mo_ref[...] = mn
        vo_ref[...] = vn

    spec = pl.BlockSpec((tr, C), lambda i: (i, 0))
    sds = jax.ShapeDtypeStruct((R, C), F32)
    return pl.pallas_call(body, name=name, grid=(R // tr,), in_specs=[spec] * 4, out_specs=[spec] * 3, out_shape=[sds] * 3,
                          compiler_params=_params(("parallel",)))(w, g, m, v)


SMALL = ["ffn1_norm", "mix_norm", "mem_norm", "forget_bias", "fox_q_gain", "fox_k_gain", "swa_q_gain", "swa_k_gain", "swa_sinks",
         "mem_q_gain", "mem_k_gain", "ffn2_norm"]
LARGE = ["ffn1_gate", "ffn1_up", "ffn1_down", "w_in", "w_mem_k", "w_mem_v", "w_out", "ffn2_gate", "ffn2_up", "ffn2_down"]
GATHER_GROUPS = [["ffn1_gate", "ffn1_up"], ["ffn1_down", "w_in", "w_mem_k", "w_mem_v"], ["w_out", "ffn2_gate", "ffn2_up", "ffn2_down"]]
WEIGHTS = ["ffn1_norm", "ffn1_gate", "ffn1_up", "ffn1_down", "mix_norm", "mem_norm", "w_in", "forget_bias", "w_mem_k", "w_mem_v",
           "fox_q_gain", "fox_k_gain", "swa_q_gain", "swa_k_gain", "swa_sinks", "mem_q_gain", "mem_k_gain", "w_out", "ffn2_norm",
           "ffn2_gate", "ffn2_up", "ffn2_down"]


def _pad_proj_cols(w):
    out = jnp.zeros((w.shape[0], PROJ_W), w.dtype)
    for start, width, pstart in REF_GROUPS:
        out = lax.dynamic_update_slice(out, w[:, start:start + width], (0, pstart))
    return out


def _unpad_proj_cols(w):
    return jnp.concatenate([w[:, pstart:pstart + width] for _, width, pstart in REF_GROUPS], axis=1)


def _pack_small(vals):
    flat = jnp.concatenate([vals[k].reshape(-1).astype(F32) for k in SMALL + ["loss"]])
    n = flat.shape[0]
    total = -(-n // 1024) * 1024
    return jnp.pad(flat, (0, total - n)).reshape(total // 128, 128)


def _unpack_small(buf, shapes):
    flat = buf.reshape(-1)
    out, off = {}, 0
    for k in SMALL + ["loss"]:
        size = int(np.prod(shapes[k]))
        out[k] = flat[off:off + size].reshape(shapes[k])
        off += size
    return out


def kernel(x, mem, ffn1_norm, ffn1_gate, ffn1_up, ffn1_down, mix_norm, mem_norm, w_in, forget_bias, w_mem_k, w_mem_v, fox_q_gain, fox_k_gain, swa_q_gain, swa_k_gain, swa_sinks, mem_q_gain, mem_k_gain, w_out, ffn2_norm, ffn2_gate, ffn2_up, ffn2_down, loss_target, m_ffn1_norm, m_ffn1_gate, m_ffn1_up, m_ffn1_down, m_mix_norm, m_mem_norm, m_w_in, m_forget_bias, m_w_mem_k, m_w_mem_v, m_fox_q_gain, m_fox_k_gain, m_swa_q_gain, m_swa_k_gain, m_swa_sinks, m_mem_q_gain, m_mem_k_gain, m_w_out, m_ffn2_norm, m_ffn2_gate, m_ffn2_up, m_ffn2_down, v_ffn1_norm, v_ffn1_gate, v_ffn1_up, v_ffn1_down, v_mix_norm, v_mem_norm, v_w_in, v_forget_bias, v_w_mem_k, v_w_mem_v, v_fox_q_gain, v_fox_k_gain, v_swa_q_gain, v_swa_k_gain, v_swa_sinks, v_mem_q_gain, v_mem_k_gain, v_w_out, v_ffn2_norm, v_ffn2_gate, v_ffn2_up, v_ffn2_down):
    given = dict(locals())
    T, D = x.shape[1], x.shape[2]
    ML = mem.shape[1]
    xin = x.reshape(T, D)
    target = loss_target.reshape(T, D)
    memin = mem.reshape(ML, D)

    ids = _place_ids()
    shard = {k: given[k][0] for k in LARGE}
    shard["w_in"] = _pad_proj_cols(shard["w_in"])
    placed = {k: _cast_place("cast_" + k, shard[k], ids) for k in LARGE}
    started, after = [], ids
    for gi, group in enumerate(GATHER_GROUPS):
        send, recv, bufs, after = _gather_start("gather_start_%d" % gi, [placed[k] for k in group], after)
        started.append((send, recv, bufs))

    def arrive(gi, done):
        send, recv, bufs = started[gi]
        bufs = _gather_wait("gather_wait_%d" % gi, send, recv, bufs, done)
        return dict(zip(GATHER_GROUPS[gi], _gather_forward("gather_forward_%d" % gi, bufs)))

    gains = jnp.concatenate([fox_q_gain, fox_k_gain, swa_q_gain, swa_k_gain, mem_q_gain,
                             jnp.pad(forget_bias, ((0, 0), (0, HEAD_DIM - FOX_HEADS))), jnp.zeros((2, HEAD_DIM), F32)], axis=0)
    slopes_np = 2.0 ** (-8.0 * np.arange(1, SWA_HEADS + 1) / SWA_HEADS)
    slopes = jnp.asarray(np.repeat(slopes_np, WINDOW).reshape(SWA_KV_HEADS, GR, 1), F32)
    sinks = jnp.repeat(swa_sinks.reshape(SWA_HEADS), WINDOW).reshape(SWA_KV_HEADS, GR, 1)

    h1 = _rms_fwd("ffn1_norm_fwd", xin, ffn1_norm + after[0, 0])
    full = arrive(0, h1)
    wg1, wu1 = full["ffn1_gate"], full["ffn1_up"]
    g1, u1, a1 = _ffn_gu("ffn1_gate_up", h1, wg1, wu1)
    full = arrive(1, a1)
    wd1 = full["ffn1_down"].reshape(-1, D)
    win = full["w_in"].reshape(D, PROJ_W)
    wmk = full["w_mem_k"].reshape(D, MEM_HEADS * HEAD_DIM)
    wmv = full["w_mem_v"].reshape(D, MEM_HEADS * HEAD_DIM)
    x1 = _ffn_down("ffn1_down", a1, wd1, xin)
    h2 = _rms_fwd("mix_norm_fwd", x1, mix_norm)
    proj = _mm2d("proj_in", h2, win, "nn", F32, tn=1408)
    qkv, logf = _prep_fwd(proj, gains)
    cum = _cumsum_rows("forget_cumsum", logf, False)
    cum_h = cum[:, :FOX_HEADS].T
    cq, ck = cum_h.reshape(FOX_HEADS, T, 1), cum_h.reshape(FOX_HEADS, 1, T)
    mn = _rms_fwd("mem_norm_fwd", memin, mem_norm)
    mk_raw = _mm2d("mem_k_proj", mn, wmk, "nn", F32)
    mv = _mm2d("mem_v_proj", mn, wmv, "nn", BF16)
    mk = _head_norm_rows(mk_raw, mem_k_gain)
    out_a, lse_a = _fox_fwd(qkv, cq, ck)
    out_b, lse_b = _swa_fwd(qkv, slopes, sinks)
    out_c, lse_c = _mem_fwd(qkv, mk, mv)
    mixed = jnp.concatenate([out_a, out_b, out_c], axis=1)
    full = arrive(2, mixed)
    wo = full["w_out"].reshape(-1, D)
    wg2, wu2, wd2 = full["ffn2_gate"], full["ffn2_up"], full["ffn2_down"].reshape(-1, D)
    x2 = _mm2d("mix_out", mixed, wo, "nn", F32, extras=[x1], epilogue=lambda accs, ex: [ex[0] + accs[0]])
    h3 = _rms_fwd("ffn2_norm_fwd", x2, ffn2_norm)
    g2, u2, a2 = _ffn_gu("ffn2_gate_up", h3, wg2, wu2)
    x3 = _ffn_down("ffn2_down", a2, wd2, x2)
    dx3, dyb3, loss_part = _loss_head(x3, target)

    grads, small = {}, {"loss": loss_part[0, 0]}
    grads["ffn2_gate"], grads["ffn2_up"], dwd2, dh3 = _ffn_bwd("ffn2", dyb3, h3, g2, u2, a2, wg2, wu2, wd2)
    grads["ffn2_down"] = dwd2.reshape(N_CHIPS, -1, D)
    dx2, dx2b, small["ffn2_norm"] = _rms_bwd("ffn2_norm_bwd", dh3, x2, ffn2_norm, dx3, 1.0)
    dmix = _mm2d("mix_out_dx", dx2b, wo, "nt", BF16)
    grads["w_out"] = _mm2d("mix_out_dw", mixed, dx2b, "tn", F32, tk=512).reshape(N_CHIPS, -1, D)
    dfq, dfk, dfv, dck = _fox_bwd(qkv, cq, ck, out_a, lse_a, dmix)
    dsq, dsk, dsv, dsink = _swa_bwd(qkv, slopes, sinks, out_b, lse_b, dmix)
    dmq, dmk, dmv = _mem_bwd(qkv, mk, mv, out_c, lse_c, dmix)
    small["swa_sinks"] = dsink[:, :SWA_GROUP, 0].reshape(1, SWA_HEADS)
    dcum = jnp.pad(dck.reshape(FOX_HEADS, T).T, ((0, 0), (0, HEAD_DIM - FOX_HEADS)))
    dlogf = _cumsum_rows("forget_cumsum_bwd", dcum, True)
    dproj, dgains = _prep_bwd(proj, gains, dfq, dfk, dfv, dsq, dsk, dsv, dmq, dlogf)
    for row, k in enumerate(["fox_q_gain", "fox_k_gain", "swa_q_gain", "swa_k_gain", "mem_q_gain"]):
        small[k] = dgains[row:row + 1, :]
    small["forget_bias"] = dgains[5:6, :FOX_HEADS]
    grads["w_in"] = _mm2d("proj_in_dw", h2, dproj, "tn", F32, tn=1408, tk=512).reshape(N_CHIPS, -1, PROJ_W)
    dh2 = _mm2d("proj_in_dx", dproj, win, "nt", F32, tk=1408)
    dx1, dyb1, small["mix_norm"] = _rms_bwd("mix_norm_bwd", dh2, x1, mix_norm, dx2, 0.5)
    dmk_raw, small["mem_k_gain"] = _head_norm_rows_bwd(mk_raw, mem_k_gain, dmk)
    dmvb = dmv.astype(BF16)
    grads["w_mem_k"] = _mm2d("mem_k_dw", mn, dmk_raw, "tn", F32).reshape(N_CHIPS, -1, MEM_HEADS * HEAD_DIM)
    grads["w_mem_v"] = _mm2d("mem_v_dw", mn, dmvb, "tn", F32).reshape(N_CHIPS, -1, MEM_HEADS * HEAD_DIM)
    dmn = _mm2d("mem_k_dx", dmk_raw, wmk, "nt", F32)
    dmn = _mm2d("mem_v_dx", dmvb, wmv, "nt", F32, extras=[dmn], epilogue=lambda accs, ex: [ex[0] + accs[0]])
    _, _, small["mem_norm"] = _rms_bwd("mem_norm_bwd", dmn, memin, mem_norm, jnp.zeros_like(memin), 1.0)
    grads["ffn1_gate"], grads["ffn1_up"], dwd1, dh1 = _ffn_bwd("ffn1", dyb1, h1, g1, u1, a1, wg1, wu1, wd1)
    grads["ffn1_down"] = dwd1.reshape(N_CHIPS, -1, D)
    grad_x, _, small["ffn1_norm"] = _rms_bwd("ffn1_norm_bwd", dh1, xin, ffn1_norm, dx1, 1.0)

    theirs = _pair_exchange([grads[k] for k in LARGE])
    to_chips = [_pair_sum_bf16("pair_sum_" + k, grads[k], b, ids) for k, b in zip(LARGE, theirs)]
    arrived = _chip_exchange(to_chips)
    halves = [_chip_sum("chip_sum_" + k, grads[k], b, r, ids) for k, b, r in zip(LARGE, theirs, arrived)]
    reduced = dict(zip(LARGE, _pair_share(halves)))
    reduced["w_in"] = _unpad_proj_cols(reduced["w_in"])

    shapes = {k: given[k].shape for k in SMALL}
    shapes["loss"] = ()
    red_small = _unpack_small(_all_reduce_small(_pack_small(small)), shapes)
    loss = red_small["loss"]

    res = {}
    for k in LARGE:
        gk = reduced[k]
        d, mo, vo = _adamw("adamw_" + k, given[k][0], gk, given["m_" + k][0], given["v_" + k][0])
        res[k] = tuple(t[None] for t in (gk, d, mo, vo))
    zero = {"loss": jnp.zeros((), F32)}
    packed = [_pack_small({**zero, **{k: src[k] for k in SMALL}}) for src in (
        {k: given[k] for k in SMALL}, red_small, {k: given["m_" + k] for k in SMALL}, {k: given["v_" + k] for k in SMALL})]
    d_s, m_s, v_s = (_unpack_small(t, shapes) for t in _adamw("adamw_small", *packed))
    for k in SMALL:
        res[k] = (red_small[k], d_s[k], m_s[k], v_s[k])

    outs = [loss, grad_x.reshape(1, T, D)]
    for part in range(4):
        outs += [res[k][part] for k in WEIGHTS]
    return tuple(outs)
```

```python
import functools

import numpy as np
import jax
import jax.numpy as jnp
from jax import lax
from jax.experimental import pallas as pl
from jax.experimental.pallas import tpu as pltpu

F32 = jnp.float32
BF16 = jnp.bfloat16
MESH = pl.DeviceIdType.MESH

HEAD_DIM = 128
FOX_HEADS = 6
SWA_HEADS = 6
SWA_KV_HEADS = 2
SWA_GROUP = SWA_HEADS // SWA_KV_HEADS
MEM_HEADS = 4
WINDOW = 128
EPS = 1e-6
NEG_INF = -1e30
SCALE = HEAD_DIM ** -0.5

C_FQ = 0
C_FK = C_FQ + FOX_HEADS * HEAD_DIM
C_FV = C_FK + FOX_HEADS * HEAD_DIM
C_SQ = C_FV + FOX_HEADS * HEAD_DIM
C_SK = C_SQ + SWA_HEADS * HEAD_DIM
C_SV = C_SK + SWA_KV_HEADS * HEAD_DIM
C_MQ = C_SV + SWA_KV_HEADS * HEAD_DIM
C_FL = C_MQ + MEM_HEADS * HEAD_DIM
PROJ_W = C_FL + HEAD_DIM
FOX_W = FOX_HEADS * HEAD_DIM
REF_GROUPS = [
    (0, FOX_W, C_FQ), (FOX_W, FOX_W, C_FK), (2 * FOX_W, FOX_W, C_FV), (3 * FOX_W, FOX_HEADS, C_FL),
    (3 * FOX_W + FOX_HEADS, SWA_HEADS * HEAD_DIM, C_SQ),
    (3 * FOX_W + FOX_HEADS + SWA_HEADS * HEAD_DIM, SWA_KV_HEADS * HEAD_DIM, C_SK),
    (3 * FOX_W + FOX_HEADS + (SWA_HEADS + SWA_KV_HEADS) * HEAD_DIM, SWA_KV_HEADS * HEAD_DIM, C_SV),
    (3 * FOX_W + FOX_HEADS + (SWA_HEADS + 2 * SWA_KV_HEADS) * HEAD_DIM, MEM_HEADS * HEAD_DIM, C_MQ),
]

ADAM_LR = 0.001
ADAM_B1 = 0.9
ADAM_B2 = 0.999
ADAM_EPS = 1e-08
ADAM_WD = 0.01
ADAM_STEP = 10

V7X_VMEM_LIMIT = 56 * 1024 * 1024
N_CHIPS = 4
N_DEV = 8


def _tile(n, pref, mult=128):
    t = (min(pref, n) // mult) * mult
    while t >= mult:
        if n % t == 0:
            return t
        t -= mult
    return n


def _params(sem):
    return pltpu.CompilerParams(dimension_semantics=sem, vmem_limit_bytes=V7X_VMEM_LIMIT)


_DIMS = {"nn": (((1,), (0,)), ((), ())), "nt": (((1,), (1,)), ((), ())), "tn": (((0,), (0,)), ((), ()))}


def _dot(a, b, mode):
    return lax.dot_general(a, b, _DIMS[mode], preferred_element_type=F32)


def _mm(name, grid, pairs, acc_of, acc_shapes, extras, outs, epilogue, after=None):
    n_p, n_e, n_o, n_a = len(pairs), len(extras), len(outs), len(acc_shapes)
    n_w = 0 if after is None else 1
    nk = grid[2]

    def body(*refs):
        ab = refs[:2 * n_p]
        ex = refs[2 * n_p:2 * n_p + n_e]
        out = refs[2 * n_p + n_e + n_w:2 * n_p + n_e + n_w + n_o]
        accs = refs[2 * n_p + n_e + n_w + n_o:]
        parts = [None] * n_a
        for p in range(n_p):
            d = _dot(ab[2 * p][...], ab[2 * p + 1][...], pairs[p][4])
            parts[acc_of[p]] = d if parts[acc_of[p]] is None else parts[acc_of[p]] + d

        def finish(vals):
            for o, r in zip(out, epilogue(vals, [e[...] for e in ex])):
                o[...] = r.astype(o.dtype)

        if nk == 1:
            finish(parts)
            return
        k = pl.program_id(2)

        @pl.when(k == 0)
        def _():
            for a, d in zip(accs, parts):
                a[...] = d

        @pl.when(k > 0)
        def _():
            for a, d in zip(accs, parts):
                a[...] += d

        @pl.when(k == nk - 1)
        def _():
            finish([a[...] for a in accs])

    in_specs, args = [], []
    for a, a_spec, b, b_spec, _ in pairs:
        in_specs += [a_spec, b_spec]
        args += [a, b]
    for e, e_spec in extras:
        in_specs.append(e_spec)
        args.append(e)
    if after is not None:
        in_specs.append(pl.BlockSpec(memory_space=pl.ANY))
        args.append(after)
    res = pl.pallas_call(
        body, name=name, grid=grid, in_specs=in_specs,
        out_specs=[s for _, s in outs], out_shape=[o for o, _ in outs],
        scratch_shapes=[pltpu.VMEM(s, F32) for s in acc_shapes] if nk > 1 else [],
        compiler_params=_params(("parallel", "parallel", "arbitrary")),
    )(*args)
    return res


def _mm2d(name, a, b, mode, out_dtype, tm=512, tn=1024, tk=1024, extras=(), epilogue=None, n_out=1, after=None):
    if mode == "nn":
        (M, K), N = a.shape, b.shape[1]
    elif mode == "nt":
        (M, K), N = a.shape, b.shape[0]
    else:
        (K, M), N = a.shape, b.shape[1]
    tm, tn, tk = _tile(M, tm), _tile(N, tn), _tile(K, tk)
    a_spec = pl.BlockSpec((tk, tm), lambda i, j, k: (k, i)) if mode == "tn" else pl.BlockSpec((tm, tk), lambda i, j, k: (i, k))
    b_spec = pl.BlockSpec((tn, tk), lambda i, j, k: (j, k)) if mode == "nt" else pl.BlockSpec((tk, tn), lambda i, j, k: (k, j))
    mn = pl.BlockSpec((tm, tn), lambda i, j, k: (i, j))
    if epilogue is None:
        epilogue = lambda accs, ex: [accs[0]]
    if not isinstance(out_dtype, (list, tuple)):
        out_dtype = [out_dtype] * n_out
    res = _mm(name, (M // tm, N // tn, K // tk), [(a, a_spec, b, b_spec, mode)], [0], [(tm, tn)],
              [(e, mn) for e in extras], [(jax.ShapeDtypeStruct((M, N), d), mn) for d in out_dtype], epilogue, after=after)
    return res[0] if len(res) == 1 else res


def _sigmoid(x):
    return 1.0 / (1.0 + jnp.exp(-x))


def _ffn_gu(name, h, wg, wu):
    T, D = h.shape
    nf, _, F4 = wg.shape
    tm, tk = _tile(T, 512), _tile(D, 1024)
    a_spec = pl.BlockSpec((tm, tk), lambda i, j, k: (i, k))
    b_spec = pl.BlockSpec((None, tk, F4), lambda i, j, k: (j, k, 0))
    o_spec = pl.BlockSpec((tm, F4), lambda i, j, k: (i, j))

    def epilogue(accs, ex):
        g, u = accs
        return [g, u, g * _sigmoid(g) * u]

    sds = jax.ShapeDtypeStruct((T, nf * F4), BF16)
    return _mm(name, (T // tm, nf, D // tk), [(h, a_spec, wg, b_spec, "nn"), (h, a_spec, wu, b_spec, "nn")], [0, 1],
               [(tm, F4), (tm, F4)], [], [(sds, o_spec)] * 3, epilogue)


def _ffn_down(name, a, wd, xres):
    return _mm2d(name, a, wd, "nn", F32, tk=wd.shape[0] // N_CHIPS, extras=[xres],
                 epilogue=lambda accs, ex: [ex[0] + 0.5 * accs[0]])


def _ffn_bwd_w(tag, dyb, h, g, u, a, wd, nf):
    T, D = h.shape
    F4 = wd.shape[0] // nf

    def act_bwd(accs, ex):
        gf, uf = ex[0].astype(F32), ex[1].astype(F32)
        s = _sigmoid(gf)
        return [accs[0] * uf * s * (1.0 + gf * (1.0 - s)), accs[0] * gf * s]

    dg, du = _mm2d(tag + "_da", dyb, wd, "nt", BF16, tn=F4, extras=[g, u], epilogue=act_bwd, n_out=2)
    dwd = _mm2d(tag + "_dwd", a, dyb, "tn", F32, tm=F4, tk=512)

    tm, tk = _tile(D, 512), _tile(T, 512)
    h_spec = pl.BlockSpec((tk, tm), lambda i, j, k: (k, i))
    d_spec = pl.BlockSpec((tk, F4), lambda i, j, k: (k, j))
    w_spec = pl.BlockSpec((None, tm, F4), lambda i, j, k: (j, i, 0))
    sds = jax.ShapeDtypeStruct((nf, D, F4), F32)
    dwg, dwu = _mm(tag + "_dwgu", (D // tm, nf, T // tk), [(h, h_spec, dg, d_spec, "tn"), (h, h_spec, du, d_spec, "tn")],
                   [0, 1], [(tm, F4), (tm, F4)], [], [(sds, w_spec)] * 2, lambda accs, ex: accs)
    return dg, du, dwg, dwu, dwd


def _ffn_bwd_x(tag, dg, du, wg, wu, after):
    T = dg.shape[0]
    nf, D, F4 = wg.shape
    tm, tn = _tile(T, 512), _tile(D, 1024)
    a_spec = pl.BlockSpec((tm, F4), lambda i, j, k: (i, k))
    b_spec = pl.BlockSpec((None, tn, F4), lambda i, j, k: (k, j, 0))
    o_spec = pl.BlockSpec((tm, tn), lambda i, j, k: (i, j))
    (dh,) = _mm(tag + "_dh", (T // tm, D // tn, nf), [(dg, a_spec, wg, b_spec, "nt"), (du, a_spec, wu, b_spec, "nt")],
                [0, 0], [(tm, tn)], [], [(jax.ShapeDtypeStruct((T, D), F32), o_spec)], lambda accs, ex: accs, after=after)
    return dh


def _rms_fwd(name, x, gain):
    R, D = x.shape
    tr = _tile(R, 256, 8)

    def body(x_ref, g_ref, o_ref):
        xv = x_ref[...]
        r = lax.rsqrt(jnp.mean(xv * xv, axis=-1, keepdims=True) + EPS)
        o_ref[...] = (xv * r * g_ref[...]).astype(BF16)

    return pl.pallas_call(
        body, name=name, grid=(R // tr,),
        in_specs=[pl.BlockSpec((tr, D), lambda i: (i, 0)), pl.BlockSpec((1, D), lambda i: (0, 0))],
        out_specs=pl.BlockSpec((tr, D), lambda i: (i, 0)), out_shape=jax.ShapeDtypeStruct((R, D), BF16),
        compiler_params=_params(("parallel",)),
    )(x, gain)


def _rms_bwd(name, dh, x, gain, dres, bscale):
    R, D = x.shape
    tr = _tile(R, 256, 8)

    def body(dh_ref, x_ref, g_ref, dres_ref, dx_ref, dxb_ref, dg_ref):
        xv, dy = x_ref[...], dh_ref[...]
        r = lax.rsqrt(jnp.mean(xv * xv, axis=-1, keepdims=True) + EPS)
        xn = xv * r
        uu = dy * g_ref[...]
        dx = dres_ref[...] + r * (uu - xn * jnp.mean(xn * uu, axis=-1, keepdims=True))
        dx_ref[...] = dx
        dxb_ref[...] = (bscale * dx).astype(BF16)
        part = jnp.sum(dy * xn, axis=0, keepdims=True)

        @pl.when(pl.program_id(0) == 0)
        def _():
            dg_ref[...] = part

        @pl.when(pl.program_id(0) > 0)
        def _():
            dg_ref[...] += part

    row = pl.BlockSpec((tr, D), lambda i: (i, 0))
    vec = pl.BlockSpec((1, D), lambda i: (0, 0))
    return pl.pallas_call(
        body, name=name, grid=(R // tr,), in_specs=[row, row, vec, row], out_specs=[row, row, vec],
        out_shape=[jax.ShapeDtypeStruct((R, D), F32), jax.ShapeDtypeStruct((R, D), BF16), jax.ShapeDtypeStruct((1, D), F32)],
        compiler_params=_params(("arbitrary",)),
    )(dh, x, gain, dres)


def _loss_head(y, target):
    R, D = y.shape
    tr = _tile(R, 256, 8)

    def body(y_ref, t_ref, d_ref, db_ref, l_ref):
        e = y_ref[...] - t_ref[...]
        d = e * (1.0 / D)
        d_ref[...] = d
        db_ref[...] = (0.5 * d).astype(BF16)
        part = jnp.zeros((8, 128), F32) + (0.5 / D) * jnp.sum(e * e)

        @pl.when(pl.program_id(0) == 0)
        def _():
            l_ref[...] = part

        @pl.when(pl.program_id(0) > 0)
        def _():
            l_ref[...] += part

    row = pl.BlockSpec((tr, D), lambda i: (i, 0))
    acc = pl.BlockSpec((8, 128), lambda i: (0, 0))
    return pl.pallas_call(
        body, name="loss_head", grid=(R // tr,), in_specs=[row, row], out_specs=[row, row, acc],
        out_shape=[jax.ShapeDtypeStruct((R, D), F32), jax.ShapeDtypeStruct((R, D), BF16), jax.ShapeDtypeStruct((8, 128), F32)],
        compiler_params=_params(("arbitrary",)),
    )(y, target)


def _head_norm(xs, g):
    r = lax.rsqrt(jnp.mean(xs * xs, axis=-1, keepdims=True) + EPS)
    return xs * r * g


def _head_norm_bwd(xs, g, dy):
    r = lax.rsqrt(jnp.mean(xs * xs, axis=-1, keepdims=True) + EPS)
    xn = xs * r
    uu = dy * g
    return r * (uu - xn * jnp.mean(xn * uu, axis=-1, keepdims=True)), jnp.sum(dy * xn, axis=0, keepdims=True)


NORMED = [(C_FQ, FOX_HEADS, 0), (C_FK, FOX_HEADS, 1), (C_SQ, SWA_HEADS, 2), (C_SK, SWA_KV_HEADS, 3), (C_MQ, MEM_HEADS, 4)]
PLAIN = [(C_FV, FOX_HEADS), (C_SV, SWA_KV_HEADS)]


def _prep_fwd(proj, gains):
    T = proj.shape[0]
    tr = _tile(T, 256, 8)

    def body(p_ref, g_ref, o_ref, lf_ref):
        for start, heads, row in NORMED:
            gn = g_ref[row:row + 1, :]
            for hh in range(heads):
                sl = slice(start + hh * HEAD_DIM, start + (hh + 1) * HEAD_DIM)
                o_ref[:, sl] = _head_norm(p_ref[:, sl], gn).astype(BF16)
        for start, heads in PLAIN:
            sl = slice(start, start + heads * HEAD_DIM)
            o_ref[:, sl] = p_ref[:, sl].astype(BF16)
        zb = p_ref[:, C_FL:C_FL + HEAD_DIM] + g_ref[5:6, :]
        o_ref[:, C_FL:C_FL + HEAD_DIM] = jnp.zeros((tr, HEAD_DIM), BF16)
        lf_ref[...] = jnp.minimum(zb, 0.0) - jnp.log(1.0 + jnp.exp(-jnp.abs(zb)))

    return pl.pallas_call(
        body, name="prep_fwd", grid=(T // tr,),
        in_specs=[pl.BlockSpec((tr, PROJ_W), lambda i: (i, 0)), pl.BlockSpec((8, 128), lambda i: (0, 0))],
        out_specs=[pl.BlockSpec((tr, PROJ_W), lambda i: (i, 0)), pl.BlockSpec((tr, HEAD_DIM), lambda i: (i, 0))],
        out_shape=[jax.ShapeDtypeStruct((T, PROJ_W), BF16), jax.ShapeDtypeStruct((T, HEAD_DIM), F32)],
        compiler_params=_params(("parallel",)),
    )(proj, gains)


def _prep_bwd(proj, gains, dfq, dfk, dfv, dsq, dsk, dsv, dmq, dlogf):
    T = proj.shape[0]
    tr = _tile(T, 256, 8)
    d_normed = {C_FQ: 0, C_FK: 1, C_SQ: 3, C_SK: 4, C_MQ: 6}
    d_plain = {C_FV: 2, C_SV: 5}

    def body(p_ref, g_ref, *rest):
        d_refs, dlf_ref, o_ref, dg_ref = rest[:7], rest[7], rest[8], rest[9]
        rows = []
        for start, heads, row in NORMED:
            gn = g_ref[row:row + 1, :]
            d_ref = d_refs[d_normed[start]]
            tot = jnp.zeros((1, HEAD_DIM), F32)
            for hh in range(heads):
                sl = slice(start + hh * HEAD_DIM, start + (hh + 1) * HEAD_DIM)
                dx, dgn = _head_norm_bwd(p_ref[:, sl], gn, d_ref[:, hh * HEAD_DIM:(hh + 1) * HEAD_DIM])
                o_ref[:, sl] = dx.astype(BF16)
                tot = tot + dgn
            rows.append(tot)
        for start, heads in PLAIN:
            o_ref[:, start:start + heads * HEAD_DIM] = d_refs[d_plain[start]][...].astype(BF16)
        zb = p_ref[:, C_FL:C_FL + HEAD_DIM] + g_ref[5:6, :]
        lane = lax.broadcasted_iota(jnp.int32, (tr, HEAD_DIM), 1)
        dz = jnp.where(lane < FOX_HEADS, dlf_ref[...] * (1.0 - _sigmoid(zb)), 0.0)
        o_ref[:, C_FL:C_FL + HEAD_DIM] = dz.astype(BF16)
        rows.append(jnp.sum(dz, axis=0, keepdims=True))
        part = jnp.concatenate(rows + [jnp.zeros((2, HEAD_DIM), F32)], axis=0)

        @pl.when(pl.program_id(0) == 0)
        def _():
            dg_ref[...] = part

        @pl.when(pl.program_id(0) > 0)
        def _():
            dg_ref[...] += part

    def rows_of(w):
        return pl.BlockSpec((tr, w), lambda i: (i, 0))

    small = pl.BlockSpec((8, 128), lambda i: (0, 0))
    ds = [dfq, dfk, dfv, dsq, dsk, dsv, dmq]
    return pl.pallas_call(
        body, name="prep_bwd", grid=(T // tr,),
        in_specs=[rows_of(PROJ_W), small] + [rows_of(d.shape[1]) for d in ds] + [rows_of(HEAD_DIM)],
        out_specs=[rows_of(PROJ_W), small],
        out_shape=[jax.ShapeDtypeStruct((T, PROJ_W), BF16), jax.ShapeDtypeStruct((8, 128), F32)],
        compiler_params=_params(("arbitrary",)),
    )(proj, gains, *ds, dlogf)


def _head_norm_rows(x, gain):
    R, W = x.shape

    def body(x_ref, g_ref, o_ref):
        for hh in range(W // HEAD_DIM):
            sl = slice(hh * HEAD_DIM, (hh + 1) * HEAD_DIM)
            o_ref[:, sl] = _head_norm(x_ref[:, sl], g_ref[...]).astype(BF16)

    return pl.pallas_call(body, name="mem_k_norm", out_shape=jax.ShapeDtypeStruct((R, W), BF16))(x, gain)


def _head_norm_rows_bwd(x, gain, dy):
    R, W = x.shape

    def body(x_ref, g_ref, dy_ref, dx_ref, dg_ref):
        tot = jnp.zeros((1, HEAD_DIM), F32)
        for hh in range(W // HEAD_DIM):
            sl = slice(hh * HEAD_DIM, (hh + 1) * HEAD_DIM)
            dx, dgn = _head_norm_bwd(x_ref[:, sl], g_ref[...], dy_ref[:, sl])
            dx_ref[:, sl] = dx.astype(BF16)
            tot = tot + dgn
        dg_ref[...] = tot

    return pl.pallas_call(
        body, name="mem_k_norm_bwd",
        out_shape=[jax.ShapeDtypeStruct((R, W), BF16), jax.ShapeDtypeStruct((1, HEAD_DIM), F32)])(x, gain, dy)


def _cumsum_rows(name, x, reverse):
    T, W = x.shape
    tb = _tile(T, 512, 8)
    nb = T // tb

    def body(x_ref, o_ref, carry):
        @pl.when(pl.program_id(0) == 0)
        def _():
            carry[...] = jnp.zeros_like(carry)

        xv = x_ref[...]
        r = lax.broadcasted_iota(jnp.int32, (tb, tb), 0)
        cc = lax.broadcasted_iota(jnp.int32, (tb, tb), 1)
        tri = jnp.where((cc >= r) if reverse else (cc <= r), 1.0, 0.0).astype(F32)
        o_ref[...] = jnp.dot(tri, xv, precision=lax.Precision.HIGHEST, preferred_element_type=F32) + carry[...]
        carry[...] += jnp.sum(xv, axis=0, keepdims=True)

    idx = (lambda i: (nb - 1 - i, 0)) if reverse else (lambda i: (i, 0))
    return pl.pallas_call(
        body, name=name, grid=(nb,), in_specs=[pl.BlockSpec((tb, W), idx)], out_specs=pl.BlockSpec((tb, W), idx),
        out_shape=jax.ShapeDtypeStruct((T, W), F32), scratch_shapes=[pltpu.VMEM((1, W), F32)],
        compiler_params=_params(("arbitrary",)),
    )(x)


def _fox_fwd(qkv, cq, ck):
    T = qkv.shape[0]
    tq = _tile(T, 512)
    nq = T // tq
    HQ, HK, HV = C_FQ // HEAD_DIM, C_FK // HEAD_DIM, C_FV // HEAD_DIM

    def body(q_ref, k_ref, v_ref, cq_ref, ck_ref, o_ref, lse_ref, m_sc, l_sc, acc_sc):
        i, j = pl.program_id(1), pl.program_id(2)

        @pl.when(j == 0)
        def _():
            m_sc[...] = jnp.full_like(m_sc, NEG_INF)
            l_sc[...] = jnp.zeros_like(l_sc)
            acc_sc[...] = jnp.zeros_like(acc_sc)

        @pl.when(j <= i)
        def _():
            s = _dot(q_ref[...], k_ref[...], "nt") * SCALE + (cq_ref[...] - ck_ref[...])
            rows = i * tq + lax.broadcasted_iota(jnp.int32, (tq, tq), 0)
            cols = j * tq + lax.broadcasted_iota(jnp.int32, (tq, tq), 1)
            s = jnp.where(cols <= rows, s, NEG_INF)
            m_new = jnp.maximum(m_sc[...], jnp.max(s, axis=-1, keepdims=True))
            alpha = jnp.exp(m_sc[...] - m_new)
            p = jnp.exp(s - m_new)
            l_sc[...] = alpha * l_sc[...] + jnp.sum(p, axis=-1, keepdims=True)
            acc_sc[...] = alpha * acc_sc[...] + _dot(p.astype(BF16), v_ref[...], "nn")
            m_sc[...] = m_new

        @pl.when(j == i)
        def _():
            o_ref[...] = (acc_sc[...] / l_sc[...]).astype(BF16)
            lse_ref[...] = m_sc[...] + jnp.log(l_sc[...])

    def col(base):
        return pl.BlockSpec((tq, HEAD_DIM), lambda h, i, j: (jnp.minimum(j, i), base + h))

    return pl.pallas_call(
        body, name="fox_fwd", grid=(FOX_HEADS, nq, nq),
        in_specs=[pl.BlockSpec((tq, HEAD_DIM), lambda h, i, j: (i, HQ + h)), col(HK), col(HV),
                  pl.BlockSpec((None, tq, 1), lambda h, i, j: (h, i, 0)),
                  pl.BlockSpec((None, 1, tq), lambda h, i, j: (h, 0, jnp.minimum(j, i)))],
        out_specs=[pl.BlockSpec((tq, HEAD_DIM), lambda h, i, j: (i, h)),
                   pl.BlockSpec((None, tq, 1), lambda h, i, j: (h, i, 0))],
        out_shape=[jax.ShapeDtypeStruct((T, FOX_W), BF16), jax.ShapeDtypeStruct((FOX_HEADS, T, 1), F32)],
        scratch_shapes=[pltpu.VMEM((tq, 1), F32), pltpu.VMEM((tq, 1), F32), pltpu.VMEM((tq, HEAD_DIM), F32)],
        compiler_params=_params(("parallel", "arbitrary", "arbitrary")),
    )(qkv, qkv, qkv, cq, ck)


def _fox_bwd(qkv, cq, ck, out, lse, dmix):
    T = qkv.shape[0]
    tq = _tile(T, 512)
    nq = T // tq
    HQ, HK, HV = C_FQ // HEAD_DIM, C_FK // HEAD_DIM, C_FV // HEAD_DIM

    def body(q_ref, k_ref, v_ref, cq_ref, ck_ref, o_ref, lse_ref, do_ref, dq_ref, dk_ref, dv_ref, dck_ref, dk_sc, dv_sc, dc_sc):
        kj, qi = pl.program_id(1), pl.program_id(2)

        @pl.when(qi == kj)
        def _():
            dk_sc[...] = jnp.zeros_like(dk_sc)
            dv_sc[...] = jnp.zeros_like(dv_sc)
            dc_sc[...] = jnp.zeros_like(dc_sc)

        @pl.when(qi >= kj)
        def _():
            q, k, v, do = q_ref[...], k_ref[...], v_ref[...], do_ref[...]
            s = _dot(q, k, "nt") * SCALE + (cq_ref[...] - ck_ref[...])
            rows = qi * tq + lax.broadcasted_iota(jnp.int32, (tq, tq), 0)
            cols = kj * tq + lax.broadcasted_iota(jnp.int32, (tq, tq), 1)
            p = jnp.where(cols <= rows, jnp.exp(s - lse_ref[...]), 0.0)
            dp = _dot(do, v, "nt")
            delta = jnp.sum(do.astype(F32) * o_ref[...].astype(F32), axis=-1, keepdims=True)
            ds = p * (dp - delta)
            dsb = ds.astype(BF16)
            dv_sc[...] += _dot(p.astype(BF16), do, "tn")
            dk_sc[...] += _dot(dsb, q, "tn")
            dc_sc[...] += jnp.sum(ds, axis=0, keepdims=True)
            dq_part = _dot(dsb, k, "nn") * SCALE
            rows_q = pl.ds(pl.multiple_of(qi * tq, tq), tq)

            @pl.when(kj == 0)
            def _():
                dq_ref[rows_q, :] = dq_part

            @pl.when(kj > 0)
            def _():
                dq_ref[rows_q, :] += dq_part

        @pl.when(qi == nq - 1)
        def _():
            dk_ref[...] = dk_sc[...] * SCALE
            dv_ref[...] = dv_sc[...]
            dck_ref[...] = -dc_sc[...]

    def qrow(base):
        return pl.BlockSpec((tq, HEAD_DIM), lambda h, kj, qi: (jnp.maximum(qi, kj), base + h))

    def krow(base):
        return pl.BlockSpec((tq, HEAD_DIM), lambda h, kj, qi: (kj, base + h))

    qcol = pl.BlockSpec((None, tq, 1), lambda h, kj, qi: (h, jnp.maximum(qi, kj), 0))
    kvec = pl.BlockSpec((None, 1, tq), lambda h, kj, qi: (h, 0, kj))
    sds = jax.ShapeDtypeStruct((T, FOX_W), F32)
    return pl.pallas_call(
        body, name="fox_bwd", grid=(FOX_HEADS, nq, nq),
        in_specs=[qrow(HQ), krow(HK), krow(HV), qcol, kvec, qrow(0), qcol, qrow(0)],
        out_specs=[pl.BlockSpec((T, HEAD_DIM), lambda h, kj, qi: (0, h)), krow(0), krow(0), kvec],
        out_shape=[sds, sds, sds, jax.ShapeDtypeStruct((FOX_HEADS, 1, T), F32)],
        scratch_shapes=[pltpu.VMEM((tq, HEAD_DIM), F32), pltpu.VMEM((tq, HEAD_DIM), F32), pltpu.VMEM((1, tq), F32)],
        compiler_params=_params(("parallel", "arbitrary", "arbitrary")),
    )(qkv, qkv, qkv, cq, ck, out, lse, dmix)


GW = SWA_GROUP * HEAD_DIM
GR = SWA_GROUP * WINDOW


def _swa_scores(q_ref, kp_ref, kc_ref, slope_ref, n):
    q = q_ref[...]
    qs = jnp.concatenate([q[:, t * HEAD_DIM:(t + 1) * HEAD_DIM] for t in range(SWA_GROUP)], axis=0)
    kb = jnp.concatenate([kp_ref[...], kc_ref[...]], axis=0)
    r = lax.broadcasted_iota(jnp.int32, (GR, 2 * WINDOW), 0) & (WINDOW - 1)
    jj = lax.broadcasted_iota(jnp.int32, (GR, 2 * WINDOW), 1)
    dist = WINDOW + r - jj
    valid = (dist >= 0) & (dist < WINDOW) & ((n > 0) | (jj >= WINDOW))
    s = _dot(qs, kb, "nt") * SCALE - slope_ref[...] * dist.astype(F32)
    return qs, kb, jnp.where(valid, s, NEG_INF), valid


def _swa_specs():
    HQ, HK, HV = C_SQ // GW, C_SK // HEAD_DIM, C_SV // HEAD_DIM
    q_spec = pl.BlockSpec((WINDOW, GW), lambda g, n: (n, HQ + g))

    def prev(base):
        return pl.BlockSpec((WINDOW, HEAD_DIM), lambda g, n: (jnp.maximum(n - 1, 0), base + g))

    def cur(base):
        return pl.BlockSpec((WINDOW, HEAD_DIM), lambda g, n: (n, base + g))

    col = pl.BlockSpec((None, GR, 1), lambda g, n: (g, 0, 0))
    return q_spec, prev(HK), cur(HK), prev(HV), cur(HV), col


def _swa_fwd(qkv, slopes, sinks):
    T = qkv.shape[0]
    nb = T // WINDOW
    assert C_SQ % GW == 0

    def body(q_ref, kp_ref, kc_ref, vp_ref, vc_ref, slope_ref, sink_ref, o_ref, lse_ref):
        n = pl.program_id(1)
        _, _, s, _ = _swa_scores(q_ref, kp_ref, kc_ref, slope_ref, n)
        m = jnp.maximum(jnp.max(s, axis=-1, keepdims=True), sink_ref[...])
        p = jnp.exp(s - m)
        l = jnp.sum(p, axis=-1, keepdims=True) + jnp.exp(sink_ref[...] - m)
        vb = jnp.concatenate([vp_ref[...], vc_ref[...]], axis=0)
        o = _dot(p.astype(BF16), vb, "nn") / l
        for t in range(SWA_GROUP):
            o_ref[:, t * HEAD_DIM:(t + 1) * HEAD_DIM] = o[t * WINDOW:(t + 1) * WINDOW, :].astype(BF16)
        lse_ref[...] = m + jnp.log(l)

    q_spec, kp, kc, vp, vc, col = _swa_specs()
    return pl.pallas_call(
        body, name="swa_fwd", grid=(SWA_KV_HEADS, nb), in_specs=[q_spec, kp, kc, vp, vc, col, col],
        out_specs=[pl.BlockSpec((WINDOW, GW), lambda g, n: (n, g)), pl.BlockSpec((None, None, GR, 1), lambda g, n: (g, n, 0, 0))],
        out_shape=[jax.ShapeDtypeStruct((T, SWA_HEADS * HEAD_DIM), BF16), jax.ShapeDtypeStruct((SWA_KV_HEADS, nb, GR, 1), F32)],
        compiler_params=_params(("parallel", "arbitrary")),
    )(qkv, qkv, qkv, qkv, qkv, slopes, sinks)


def _swa_bwd(qkv, slopes, sinks, out, lse, dmix):
    T = qkv.shape[0]
    nb = T // WINDOW
    DO = FOX_W // GW
    assert FOX_W % GW == 0

    def body(q_ref, kp_ref, kc_ref, vp_ref, vc_ref, slope_ref, sink_ref, o_ref, lse_ref, do_ref,
             dq_ref, dk_ref, dv_ref, dsink_ref, sink_sc):
        n = pl.program_id(1)

        @pl.when(n == 0)
        def _():
            dk_ref[...] = jnp.zeros_like(dk_ref)
            dv_ref[...] = jnp.zeros_like(dv_ref)
            sink_sc[...] = jnp.zeros_like(sink_sc)

        qs, kb, s, valid = _swa_scores(q_ref, kp_ref, kc_ref, slope_ref, n)
        lse = lse_ref[...]
        p = jnp.where(valid, jnp.exp(s - lse), 0.0)
        vb = jnp.concatenate([vp_ref[...], vc_ref[...]], axis=0)
        do = jnp.concatenate([do_ref[:, t * HEAD_DIM:(t + 1) * HEAD_DIM] for t in range(SWA_GROUP)], axis=0)
        oo = jnp.concatenate([o_ref[:, t * HEAD_DIM:(t + 1) * HEAD_DIM] for t in range(SWA_GROUP)], axis=0)
        dp = _dot(do, vb, "nt")
        delta = jnp.sum(do.astype(F32) * oo.astype(F32), axis=-1, keepdims=True)
        ds = p * (dp - delta)
        dsb = ds.astype(BF16)
        dq = _dot(dsb, kb, "nn") * SCALE
        for t in range(SWA_GROUP):
            dq_ref[:, t * HEAD_DIM:(t + 1) * HEAD_DIM] = dq[t * WINDOW:(t + 1) * WINDOW, :]
        dkb = _dot(dsb, qs, "tn") * SCALE
        dvb = _dot(p.astype(BF16), do, "tn")
        r_prev = pl.ds(pl.multiple_of(jnp.maximum(n - 1, 0) * WINDOW, WINDOW), WINDOW)
        r_cur = pl.ds(pl.multiple_of(n * WINDOW, WINDOW), WINDOW)
        dk_ref[r_prev, :] += dkb[:WINDOW, :]
        dk_ref[r_cur, :] += dkb[WINDOW:, :]
        dv_ref[r_prev, :] += dvb[:WINDOW, :]
        dv_ref[r_cur, :] += dvb[WINDOW:, :]
        sink_sc[...] -= jnp.exp(sink_ref[...] - lse) * delta

        @pl.when(n == nb - 1)
        def _():
            tot = [jnp.zeros((1, 128), F32) + jnp.sum(sink_sc[t * WINDOW:(t + 1) * WINDOW, :]) for t in range(SWA_GROUP)]
            dsink_ref[...] = jnp.concatenate(tot + [jnp.zeros((8 - SWA_GROUP, 128), F32)], axis=0)

    q_spec, kp, kc, vp, vc, col = _swa_specs()
    kv_acc = pl.BlockSpec((T, HEAD_DIM), lambda g, n: (0, g))
    return pl.pallas_call(
        body, name="swa_bwd", grid=(SWA_KV_HEADS, nb),
        in_specs=[q_spec, kp, kc, vp, vc, col, col, pl.BlockSpec((WINDOW, GW), lambda g, n: (n, g)),
                  pl.BlockSpec((None, None, GR, 1), lambda g, n: (g, n, 0, 0)), pl.BlockSpec((WINDOW, GW), lambda g, n: (n, DO + g))],
        out_specs=[pl.BlockSpec((WINDOW, GW), lambda g, n: (n, g)), kv_acc, kv_acc, pl.BlockSpec((None, 8, 128), lambda g, n: (g, 0, 0))],
        out_shape=[jax.ShapeDtypeStruct((T, SWA_HEADS * HEAD_DIM), F32), jax.ShapeDtypeStruct((T, SWA_KV_HEADS * HEAD_DIM), F32),
                   jax.ShapeDtypeStruct((T, SWA_KV_HEADS * HEAD_DIM), F32), jax.ShapeDtypeStruct((SWA_KV_HEADS, 8, 128), F32)],
        scratch_shapes=[pltpu.VMEM((GR, 1), F32)],
        compiler_params=_params(("parallel", "arbitrary")),
    )(qkv, qkv, qkv, qkv, qkv, slopes, sinks, out, lse, dmix)


def _mem_fwd(qkv, mk, mv):
    T, ML = qkv.shape[0], mk.shape[0]
    tq = _tile(T, 512)
    HQ = C_MQ // HEAD_DIM

    def body(q_ref, k_ref, v_ref, o_ref, lse_ref):
        s = _dot(q_ref[...], k_ref[...], "nt") * SCALE
        m = jnp.max(s, axis=-1, keepdims=True)
        p = jnp.exp(s - m)
        l = jnp.sum(p, axis=-1, keepdims=True)
        o_ref[...] = (_dot(p.astype(BF16), v_ref[...], "nn") / l).astype(BF16)
        lse_ref[...] = m + jnp.log(l)

    kv = pl.BlockSpec((ML, HEAD_DIM), lambda h, i: (0, h))
    return pl.pallas_call(
        body, name="mem_fwd", grid=(MEM_HEADS, T // tq),
        in_specs=[pl.BlockSpec((tq, HEAD_DIM), lambda h, i: (i, HQ + h)), kv, kv],
        out_specs=[pl.BlockSpec((tq, HEAD_DIM), lambda h, i: (i, h)), pl.BlockSpec((None, tq, 1), lambda h, i: (h, i, 0))],
        out_shape=[jax.ShapeDtypeStruct((T, MEM_HEADS * HEAD_DIM), BF16), jax.ShapeDtypeStruct((MEM_HEADS, T, 1), F32)],
        compiler_params=_params(("parallel", "arbitrary")),
    )(qkv, mk, mv)


def _mem_bwd(qkv, mk, mv, out, lse, dmix):
    T, ML = qkv.shape[0], mk.shape[0]
    tq = _tile(T, 512)
    HQ = C_MQ // HEAD_DIM
    DO = (FOX_W + SWA_HEADS * HEAD_DIM) // HEAD_DIM

    def body(q_ref, k_ref, v_ref, o_ref, lse_ref, do_ref, dq_ref, dk_ref, dv_ref):
        q, k, v, do = q_ref[...], k_ref[...], v_ref[...], do_ref[...]
        p = jnp.exp(_dot(q, k, "nt") * SCALE - lse_ref[...])
        dp = _dot(do, v, "nt")
        delta = jnp.sum(do.astype(F32) * o_ref[...].astype(F32), axis=-1, keepdims=True)
        dsb = (p * (dp - delta)).astype(BF16)
        dq_ref[...] = _dot(dsb, k, "nn") * SCALE
        dk_part = _dot(dsb, q, "tn") * SCALE
        dv_part = _dot(p.astype(BF16), do, "tn")

        @pl.when(pl.program_id(1) == 0)
        def _():
            dk_ref[...] = dk_part
            dv_ref[...] = dv_part

        @pl.when(pl.program_id(1) > 0)
        def _():
            dk_ref[...] += dk_part
            dv_ref[...] += dv_part

    kv = pl.BlockSpec((ML, HEAD_DIM), lambda h, i: (0, h))
    qb = pl.BlockSpec((tq, HEAD_DIM), lambda h, i: (i, h))
    return pl.pallas_call(
        body, name="mem_bwd", grid=(MEM_HEADS, T // tq),
        in_specs=[pl.BlockSpec((tq, HEAD_DIM), lambda h, i: (i, HQ + h)), kv, kv, qb,
                  pl.BlockSpec((None, tq, 1), lambda h, i: (h, i, 0)), pl.BlockSpec((tq, HEAD_DIM), lambda h, i: (i, DO + h))],
        out_specs=[qb, kv, kv],
        out_shape=[jax.ShapeDtypeStruct((T, MEM_HEADS * HEAD_DIM), F32), jax.ShapeDtypeStruct((ML, MEM_HEADS * HEAD_DIM), F32),
                   jax.ShapeDtypeStruct((ML, MEM_HEADS * HEAD_DIM), F32)],
        compiler_params=_params(("parallel", "arbitrary")),
    )(qkv, mk, mv, out, lse, dmix)


HBM = pl.BlockSpec(memory_space=pltpu.HBM)


def _place():
    x, y, c = lax.axis_index("x"), lax.axis_index("y"), lax.axis_index("c")
    chips = [(1 - x, y), (x, 1 - y), (1 - x, 1 - y)]
    return x, y, c, chips


def _remote(src, dst, send_sem, recv_sem, device):
    return pltpu.make_async_remote_copy(src_ref=src, dst_ref=dst, send_sem=send_sem, recv_sem=recv_sem,
                                        device_id=device, device_id_type=MESH)


def _place_ids():
    x, y, c = lax.axis_index("x"), lax.axis_index("y"), lax.axis_index("c")
    order = [2 * x + y, 2 * (1 - x) + y, 2 * x + (1 - y), 2 * (1 - x) + (1 - y)]
    return jnp.stack([2 * x + y, c] + order).astype(jnp.int32)


def _cast_place(name, w, ids):
    R, C = w.shape
    tr = _tile(R, 256, 16)

    def body(ids_ref, w_ref, o_ref):
        o_ref[...] = w_ref[...].astype(BF16)

    return pl.pallas_call(
        body, name=name,
        grid_spec=pltpu.PrefetchScalarGridSpec(
            num_scalar_prefetch=1, grid=(R // tr,), in_specs=[pl.BlockSpec((tr, C), lambda i, ids: (i, 0))],
            out_specs=pl.BlockSpec((None, tr, C), lambda i, ids: (ids[0], i, 0))),
        out_shape=jax.ShapeDtypeStruct((N_CHIPS, R, C), BF16), compiler_params=_params(("parallel",)),
    )(ids, w)


SEM = pl.BlockSpec(memory_space=pltpu.SEMAPHORE)
EFFECT = pltpu.SideEffectType.DATAFLOW_SIDE_EFFECTING


def _hbm(a):
    return pltpu.with_memory_space_constraint(a, pltpu.HBM)


def _gather_start(name, placed, after):
    n = len(placed)

    ns = 3 * n

    def body(*refs):
        send, recv = refs[n + 1:n + 1 + ns], refs[n + 1 + ns:n + 1 + 2 * ns]
        buf = refs[n + 1 + 2 * ns:2 * n + 1 + 2 * ns]
        token = refs[2 * n + 1 + 2 * ns]
        x, y, c, chips = _place()
        me = 2 * x + y
        for a in range(n):
            half = buf[a].shape[1] // 2
            mine = buf[a].at[me, pl.ds(c * half, half)]
            for j, (cx, cy) in enumerate(chips):
                _remote(mine, mine, send[3 * a + j], recv[3 * a + j], (cx, cy, c)).start()
        token[...] = jnp.zeros_like(token)

    res = pl.pallas_call(
        body, name=name, in_specs=[HBM] * n + [pl.BlockSpec(memory_space=pl.ANY)],
        out_specs=[SEM] * (2 * ns) + [HBM] * n + [pl.BlockSpec(memory_space=pltpu.VMEM)],
        out_shape=[pltpu.SemaphoreType.DMA(())] * (2 * ns)
        + [pltpu.HBM(s.shape, s.dtype) for s in placed] + [jax.ShapeDtypeStruct((8, 128), F32)],
        input_output_aliases={a: 2 * ns + a for a in range(n)},
        compiler_params=pltpu.CompilerParams(has_side_effects=EFFECT),
    )(*[_hbm(s) for s in placed], after)
    return list(res[:ns]), list(res[ns:2 * ns]), list(res[2 * ns:2 * ns + n]), res[2 * ns + n]


def _gather_wait(name, send, recv, bufs, after):
    n = len(bufs)

    ns = 3 * n

    def body(*refs):
        buf = refs[:n]
        send_ref, recv_ref = refs[n:n + ns], refs[n + ns:n + 2 * ns]
        x, y, c, chips = _place()
        ids = [2 * cx + cy for cx, cy in chips]
        for a in range(n):
            half = buf[a].shape[1] // 2
            for j in range(3):
                landed = buf[a].at[ids[j], pl.ds(c * half, half)]
                cp = _remote(landed, landed, send_ref[3 * a + j], recv_ref[3 * a + j], (x, y, c))
                cp.wait_send()
                cp.wait_recv()

    res = pl.pallas_call(
        body, name=name, in_specs=[HBM] * n + [SEM] * (2 * ns) + [pl.BlockSpec(memory_space=pl.ANY)], out_specs=[HBM] * n,
        out_shape=[pltpu.HBM(s.shape, s.dtype) for s in bufs], input_output_aliases={a: a for a in range(n)},
        compiler_params=pltpu.CompilerParams(has_side_effects=EFFECT),
    )(*bufs, *send, *recv, after)
    return list(res)


def _gather_forward(name, bufs):
    n = len(bufs)

    def body(*refs):
        buf = refs[n:2 * n]
        send, recv = refs[2 * n:]
        x, y, c, chips = _place()
        ids = [2 * cx + cy for cx, cy in chips]
        copies = []
        for a in range(n):
            half = buf[a].shape[1] // 2
            for j in range(3):
                landed = buf[a].at[ids[j], pl.ds(c * half, half)]
                cp = _remote(landed, landed, send.at[a, j], recv.at[a, j], (x, y, 1 - c))
                cp.start()
                copies.append(cp)
        for a in range(n):
            half = buf[a].shape[1] // 2
            for j in range(3):
                landed = buf[a].at[ids[j], pl.ds((1 - c) * half, half)]
                _remote(landed, landed, send.at[a, j], recv.at[a, j], (x, y, c)).wait_recv()
        for cp in copies:
            cp.wait_send()

    return pl.pallas_call(
        body, name=name, in_specs=[HBM] * n, out_specs=[HBM] * n,
        out_shape=[jax.ShapeDtypeStruct(s.shape, s.dtype) for s in bufs], input_output_aliases={a: a for a in range(n)},
        scratch_shapes=[pltpu.SemaphoreType.DMA((n, 3)), pltpu.SemaphoreType.DMA((n, 3))],
    )(*bufs)


def _pair_exchange(name, grads):
    n = len(grads)

    def body(*refs):
        src, theirs = refs[:n], refs[n:2 * n]
        send, recv = refs[2 * n:]
        x, y, c, chips = _place()
        order = [2 * x + y] + [2 * cx + cy for cx, cy in chips]
        copies = []
        for a in range(n):
            half = src[a].shape[1] // 2
            for j in range(N_CHIPS):
                cp = _remote(src[a].at[order[j], pl.ds((1 - c) * half, half)], theirs[a].at[j], send.at[a, j], recv.at[a, j], (x, y, 1 - c))
                cp.start()
                copies.append(cp)
        for cp in copies:
            cp.wait()

    return pl.pallas_call(
        body, name=name, in_specs=[HBM] * n, out_specs=[HBM] * n,
        out_shape=[jax.ShapeDtypeStruct((N_CHIPS, g.shape[1] // 2, g.shape[2]), g.dtype) for g in grads],
        scratch_shapes=[pltpu.SemaphoreType.DMA((n, N_CHIPS)), pltpu.SemaphoreType.DMA((n, N_CHIPS))],
    )(*grads)


def _chip_start(name, parts):
    n = len(parts)
    ns = 3 * n

    def body(*refs):
        send, recv = refs[2 * n:2 * n + ns], refs[2 * n + ns:2 * n + 2 * ns]
        src = refs[2 * n + 2 * ns:3 * n + 2 * ns]
        land = refs[3 * n + 2 * ns:4 * n + 2 * ns]
        token = refs[4 * n + 2 * ns]
        x, y, c, chips = _place()
        for a in range(n):
            for j, (cx, cy) in enumerate(chips):
                _remote(src[a].at[j], land[a].at[j], send[3 * a + j], recv[3 * a + j], (cx, cy, c)).start()
        token[...] = jnp.zeros_like(token)

    res = pl.pallas_call(
        body, name=name, in_specs=[HBM] * (2 * n),
        out_specs=[SEM] * (2 * ns) + [HBM] * (2 * n) + [pl.BlockSpec(memory_space=pltpu.VMEM)],
        out_shape=[pltpu.SemaphoreType.DMA(())] * (2 * ns) + [pltpu.HBM(p.shape, p.dtype) for p in parts] * 2
        + [jax.ShapeDtypeStruct((8, 128), F32)],
        input_output_aliases={a: 2 * ns + a for a in range(2 * n)},
        compiler_params=pltpu.CompilerParams(has_side_effects=EFFECT),
    )(*[_hbm(p) for p in parts], *[_hbm(lax.empty(p.shape, p.dtype)) for p in parts])
    return list(res[:ns]), list(res[ns:2 * ns]), list(res[2 * ns:2 * ns + n]), list(res[2 * ns + n:2 * ns + 2 * n]), res[2 * ns + 2 * n]


def _chip_wait(name, send, recv, parts, lands, after):
    n = len(parts)
    ns = 3 * n

    def body(*refs):
        src, land = refs[:n], refs[n:2 * n]
        send_ref, recv_ref = refs[2 * n:2 * n + ns], refs[2 * n + ns:2 * n + 2 * ns]
        x, y, c, _ = _place()
        for a in range(n):
            for j in range(3):
                cp = _remote(src[a].at[j], land[a].at[j], send_ref[3 * a + j], recv_ref[3 * a + j], (x, y, c))
                cp.wait_send()
                cp.wait_recv()

    res = pl.pallas_call(
        body, name=name, in_specs=[HBM] * (2 * n) + [SEM] * (2 * ns) + [pl.BlockSpec(memory_space=pl.ANY)],
        out_specs=[HBM] * (2 * n), out_shape=[pltpu.HBM(p.shape, p.dtype) for p in parts] * 2,
        input_output_aliases={a: a for a in range(2 * n)},
        compiler_params=pltpu.CompilerParams(has_side_effects=EFFECT),
    )(*parts, *lands, *send, *recv, after)
    return list(res[n:])


def _pair_share(name, shards):
    n = len(shards)

    def body(*refs):
        buf = refs[n:2 * n]
        send, recv = refs[2 * n:]
        x, y, c, _ = _place()
        copies = []
        for a in range(n):
            half = buf[a].shape[0] // 2
            mine = buf[a].at[pl.ds(c * half, half)]
            cp = _remote(mine, mine, send.at[a], recv.at[a], (x, y, 1 - c))
            cp.start()
            copies.append(cp)
        for a, cp in enumerate(copies):
            half = buf[a].shape[0] // 2
            cp.wait_send()
            theirs = buf[a].at[pl.ds((1 - c) * half, half)]
            _remote(theirs, theirs, send.at[a], recv.at[a], (x, y, c)).wait_recv()

    return pl.pallas_call(
        body, name=name, in_specs=[HBM] * n, out_specs=[HBM] * n,
        out_shape=[jax.ShapeDtypeStruct(s.shape, s.dtype) for s in shards], input_output_aliases={a: a for a in range(n)},
        scratch_shapes=[pltpu.SemaphoreType.DMA((n,)), pltpu.SemaphoreType.DMA((n,))],
    )(*shards)


def _all_reduce_small(buf):
    R, W = buf.shape

    def body(src_ref, out_ref, slots, send, recv):
        x, y, c, _ = _place()
        me = 4 * x + 2 * y + c
        copies = []
        for dx in range(2):
            for dy in range(2):
                for dc in range(2):
                    if dx == dy == dc == 0:
                        continue
                    k = 4 * dx + 2 * dy + dc
                    peer = (x ^ dx, y ^ dy, c ^ dc)
                    cp = _remote(src_ref, slots.at[me], send.at[k], recv.at[k], peer)
                    cp.start()
                    copies.append((cp, k))
        slots[me] = src_ref[...]
        for cp, k in copies:
            cp.wait_send()
            landed = slots.at[me ^ k]
            _remote(landed, landed, send.at[k], recv.at[k], (x, y, c)).wait_recv()
        total = slots[0]
        for d in range(1, N_DEV):
            total = total + slots[d]
        out_ref[...] = total

    return pl.pallas_call(
        body, name="all_reduce_small", in_specs=[pl.BlockSpec(memory_space=pltpu.VMEM)],
        out_specs=pl.BlockSpec(memory_space=pltpu.VMEM), out_shape=jax.ShapeDtypeStruct((R, W), F32),
        scratch_shapes=[pltpu.VMEM((N_DEV, R, W), F32), pltpu.SemaphoreType.DMA((N_DEV,)), pltpu.SemaphoreType.DMA((N_DEV,))],
    )(buf)


def _pair_sum_bf16(name, grad, theirs, ids):
    _, R2, C = theirs.shape
    tr = _tile(R2, 256, 16)
    nrb = R2 // tr

    def body(ids_ref, a_ref, b_ref, o_ref):
        o_ref[...] = (a_ref[...] + b_ref[...]).astype(BF16)

    return pl.pallas_call(
        body, name=name,
        grid_spec=pltpu.PrefetchScalarGridSpec(
            num_scalar_prefetch=1, grid=(3, nrb),
            in_specs=[pl.BlockSpec((None, tr, C), lambda j, i, ids: (ids[3 + j], ids[1] * nrb + i, 0)),
                      pl.BlockSpec((None, tr, C), lambda j, i, ids: (j + 1, i, 0))],
            out_specs=pl.BlockSpec((None, tr, C), lambda j, i, ids: (j, i, 0))),
        out_shape=jax.ShapeDtypeStruct((3, R2, C), BF16), compiler_params=_params(("parallel", "parallel")),
    )(ids, grad, theirs)


def _chip_sum(name, grad, theirs, arrived, ids):
    _, R2, C = theirs.shape
    tr = _tile(R2, 256, 16)
    nrb = R2 // tr

    def body(ids_ref, a_ref, b_ref, r_ref, o_ref):
        tot = a_ref[...] + b_ref[...]
        for j in range(3):
            tot = tot + r_ref[j].astype(F32)
        o_ref[...] = tot

    return pl.pallas_call(
        body, name=name,
        grid_spec=pltpu.PrefetchScalarGridSpec(
            num_scalar_prefetch=1, grid=(nrb,),
            in_specs=[pl.BlockSpec((None, tr, C), lambda i, ids: (ids[0], ids[1] * nrb + i, 0)),
                      pl.BlockSpec((None, tr, C), lambda i, ids: (0, i, 0)),
                      pl.BlockSpec((3, tr, C), lambda i, ids: (0, i, 0))],
            out_specs=pl.BlockSpec((tr, C), lambda i, ids: (ids[1] * nrb + i, 0))),
        out_shape=jax.ShapeDtypeStruct((2 * R2, C), F32), compiler_params=_params(("parallel",)),
    )(ids, grad, theirs, arrived)


def _adamw(name, w, g, m, v):
    R, C = w.shape
    tr = _tile(R, 128, 8)
    c1 = 1.0 / (1.0 - ADAM_B1 ** ADAM_STEP)
    c2 = 1.0 / (1.0 - ADAM_B2 ** ADAM_STEP)

    def body(w_ref, g_ref, m_ref, v_ref, d_ref, mo_ref, vo_ref):
        gv = g_ref[...]
        mn = ADAM_B1 * m_ref[...] + (1.0 - ADAM_B1) * gv
        vn = ADAM_B2 * v_ref[...] + (1.0 - ADAM_B2) * (gv * gv)
        d_ref[...] = -ADAM_LR * ((mn * c1) / (jnp.sqrt(vn * c2) + ADAM_EPS) + ADAM_WD * w_ref[...])
        mo_ref[...] = mn
        vo_ref[...] = vn

    spec = pl.BlockSpec((tr, C), lambda i: (i, 0))
    sds = jax.ShapeDtypeStruct((R, C), F32)
    return pl.pallas_call(body, name=name, grid=(R // tr,), in_specs=[spec] * 4, out_specs=[spec] * 3, out_shape=[sds] * 3,
                          compiler_params=_params(("parallel",)))(w, g, m, v)


SMALL = ["ffn1_norm", "mix_norm", "mem_norm", "forget_bias", "fox_q_gain", "fox_k_gain", "swa_q_gain", "swa_k_gain", "swa_sinks",
         "mem_q_gain", "mem_k_gain", "ffn2_norm"]
LARGE = ["ffn1_gate", "ffn1_up", "ffn1_down", "w_in", "w_mem_k", "w_mem_v", "w_out", "ffn2_gate", "ffn2_up", "ffn2_down"]
GATHER_GROUPS = [["ffn1_gate", "ffn1_up"], ["ffn1_down", "w_in", "w_mem_k", "w_mem_v"], ["w_out", "ffn2_gate", "ffn2_up", "ffn2_down"]]
WEIGHTS = ["ffn1_norm", "ffn1_gate", "ffn1_up", "ffn1_down", "mix_norm", "mem_norm", "w_in", "forget_bias", "w_mem_k", "w_mem_v",
           "fox_q_gain", "fox_k_gain", "swa_q_gain", "swa_k_gain", "swa_sinks", "mem_q_gain", "mem_k_gain", "w_out", "ffn2_norm",
           "ffn2_gate", "ffn2_up", "ffn2_down"]


def _pad_proj_cols(w):
    out = jnp.zeros((w.shape[0], PROJ_W), w.dtype)
    for start, width, pstart in REF_GROUPS:
        out = lax.dynamic_update_slice(out, w[:, start:start + width], (0, pstart))
    return out


def _unpad_proj_cols(w):
    return jnp.concatenate([w[:, pstart:pstart + width] for _, width, pstart in REF_GROUPS], axis=1)


def _pack_small(vals):
    flat = jnp.concatenate([vals[k].reshape(-1).astype(F32) for k in SMALL + ["loss"]])
    n = flat.shape[0]
    total = -(-n // 1024) * 1024
    return jnp.pad(flat, (0, total - n)).reshape(total // 128, 128)


def _unpack_small(buf, shapes):
    flat = buf.reshape(-1)
    out, off = {}, 0
    for k in SMALL + ["loss"]:
        size = int(np.prod(shapes[k]))
        out[k] = flat[off:off + size].reshape(shapes[k])
        off += size
    return out


def kernel(x, mem, ffn1_norm, ffn1_gate, ffn1_up, ffn1_down, mix_norm, mem_norm, w_in, forget_bias, w_mem_k, w_mem_v, fox_q_gain, fox_k_gain, swa_q_gain, swa_k_gain, swa_sinks, mem_q_gain, mem_k_gain, w_out, ffn2_norm, ffn2_gate, ffn2_up, ffn2_down, loss_target, m_ffn1_norm, m_ffn1_gate, m_ffn1_up, m_ffn1_down, m_mix_norm, m_mem_norm, m_w_in, m_forget_bias, m_w_mem_k, m_w_mem_v, m_fox_q_gain, m_fox_k_gain, m_swa_q_gain, m_swa_k_gain, m_swa_sinks, m_mem_q_gain, m_mem_k_gain, m_w_out, m_ffn2_norm, m_ffn2_gate, m_ffn2_up, m_ffn2_down, v_ffn1_norm, v_ffn1_gate, v_ffn1_up, v_ffn1_down, v_mix_norm, v_mem_norm, v_w_in, v_forget_bias, v_w_mem_k, v_w_mem_v, v_fox_q_gain, v_fox_k_gain, v_swa_q_gain, v_swa_k_gain, v_swa_sinks, v_mem_q_gain, v_mem_k_gain, v_w_out, v_ffn2_norm, v_ffn2_gate, v_ffn2_up, v_ffn2_down):
    given = dict(locals())
    T, D = x.shape[1], x.shape[2]
    ML = mem.shape[1]
    xin = x.reshape(T, D)
    target = loss_target.reshape(T, D)
    memin = mem.reshape(ML, D)

    ids = _place_ids()
    shard = {k: given[k][0] for k in LARGE}
    shard["w_in"] = _pad_proj_cols(shard["w_in"])
    placed = {k: _cast_place("cast_" + k, shard[k], ids) for k in LARGE}
    started, after = [], ids
    for gi, group in enumerate(GATHER_GROUPS):
        send, recv, bufs, after = _gather_start("gather_start_%d" % gi, [placed[k] for k in group], after)
        started.append((send, recv, bufs))

    def arrive(gi, done):
        send, recv, bufs = started[gi]
        bufs = _gather_wait("gather_wait_%d" % gi, send, recv, bufs, done)
        return dict(zip(GATHER_GROUPS[gi], _gather_forward("gather_forward_%d" % gi, bufs)))

    gains = jnp.concatenate([fox_q_gain, fox_k_gain, swa_q_gain, swa_k_gain, mem_q_gain,
                             jnp.pad(forget_bias, ((0, 0), (0, HEAD_DIM - FOX_HEADS))), jnp.zeros((2, HEAD_DIM), F32)], axis=0)
    slopes_np = 2.0 ** (-8.0 * np.arange(1, SWA_HEADS + 1) / SWA_HEADS)
    slopes = jnp.asarray(np.repeat(slopes_np, WINDOW).reshape(SWA_KV_HEADS, GR, 1), F32)
    sinks = jnp.repeat(swa_sinks.reshape(SWA_HEADS), WINDOW).reshape(SWA_KV_HEADS, GR, 1)

    h1 = _rms_fwd("ffn1_norm_fwd", xin, ffn1_norm + after[0, 0])
    full = arrive(0, h1)
    wg1, wu1 = full["ffn1_gate"], full["ffn1_up"]
    g1, u1, a1 = _ffn_gu("ffn1_gate_up", h1, wg1, wu1)
    full = arrive(1, a1)
    wd1 = full["ffn1_down"].reshape(-1, D)
    win = full["w_in"].reshape(D, PROJ_W)
    wmk = full["w_mem_k"].reshape(D, MEM_HEADS * HEAD_DIM)
    wmv = full["w_mem_v"].reshape(D, MEM_HEADS * HEAD_DIM)
    x1 = _ffn_down("ffn1_down", a1, wd1, xin)
    h2 = _rms_fwd("mix_norm_fwd", x1, mix_norm)
    proj = _mm2d("proj_in", h2, win, "nn", F32, tn=1408)
    qkv, logf = _prep_fwd(proj, gains)
    cum = _cumsum_rows("forget_cumsum", logf, False)
    cum_h = cum[:, :FOX_HEADS].T
    cq, ck = cum_h.reshape(FOX_HEADS, T, 1), cum_h.reshape(FOX_HEADS, 1, T)
    mn = _rms_fwd("mem_norm_fwd", memin, mem_norm)
    mk_raw = _mm2d("mem_k_proj", mn, wmk, "nn", F32)
    mv = _mm2d("mem_v_proj", mn, wmv, "nn", BF16)
    mk = _head_norm_rows(mk_raw, mem_k_gain)
    out_a, lse_a = _fox_fwd(qkv, cq, ck)
    out_b, lse_b = _swa_fwd(qkv, slopes, sinks)
    out_c, lse_c = _mem_fwd(qkv, mk, mv)
    mixed = jnp.concatenate([out_a, out_b, out_c], axis=1)
    full = arrive(2, mixed)
    wo = full["w_out"].reshape(-1, D)
    wg2, wu2, wd2 = full["ffn2_gate"], full["ffn2_up"], full["ffn2_down"].reshape(-1, D)
    x2 = _mm2d("mix_out", mixed, wo, "nn", F32, extras=[x1], epilogue=lambda accs, ex: [ex[0] + accs[0]])
    h3 = _rms_fwd("ffn2_norm_fwd", x2, ffn2_norm)
    g2, u2, a2 = _ffn_gu("ffn2_gate_up", h3, wg2, wu2)
    x3 = _ffn_down("ffn2_down", a2, wd2, x2)
    dx3, dyb3, loss_part = _loss_head(x3, target)

    grads, small, res = {}, {"loss": loss_part[0, 0]}, {}

    def send_off(tag, group):
        theirs = _pair_exchange("grad_pair_exchange_" + tag, [grads[k] for k in group])
        to_chips = [_pair_sum_bf16("pair_sum_" + k, grads[k], b, ids) for k, b in zip(group, theirs)]
        send, recv, parts, lands, token = _chip_start("grad_chip_start_" + tag, to_chips)
        return (group, theirs, send, recv, parts, lands), token

    def finish(tag, state, done):
        group, theirs, send, recv, parts, lands = state
        arrived = _chip_wait("grad_chip_wait_" + tag, send, recv, parts, lands, done)
        halves = [_chip_sum("chip_sum_" + k, grads[k], b, r, ids) for k, b, r in zip(group, theirs, arrived)]
        reduced = dict(zip(group, _pair_share("grad_pair_share_" + tag, halves)))
        last = None
        for k in group:
            gk = _unpad_proj_cols(reduced[k]) if k == "w_in" else reduced[k]
            d, mo, vo = _adamw("adamw_" + k, given[k][0], gk, given["m_" + k][0], given["v_" + k][0])
            res[k] = tuple(t[None] for t in (gk, d, mo, vo))
            last = vo
        return last

    dg2, du2, grads["ffn2_gate"], grads["ffn2_up"], dwd2 = _ffn_bwd_w("ffn2", dyb3, h3, g2, u2, a2, wd2, N_CHIPS)
    grads["ffn2_down"] = dwd2.reshape(N_CHIPS, -1, D)
    state_a, token_a = send_off("a", ["ffn2_gate", "ffn2_up", "ffn2_down"])
    dh3 = _ffn_bwd_x("ffn2", dg2, du2, wg2, wu2, token_a)
    dx2, dx2b, small["ffn2_norm"] = _rms_bwd("ffn2_norm_bwd", dh3, x2, ffn2_norm, dx3, 1.0)
    dmix = _mm2d("mix_out_dx", dx2b, wo, "nt", BF16)
    grads["w_out"] = _mm2d("mix_out_dw", mixed, dx2b, "tn", F32, tk=512).reshape(N_CHIPS, -1, D)
    dfq, dfk, dfv, dck = _fox_bwd(qkv, cq, ck, out_a, lse_a, dmix)
    dsq, dsk, dsv, dsink = _swa_bwd(qkv, slopes, sinks, out_b, lse_b, dmix)
    dmq, dmk, dmv = _mem_bwd(qkv, mk, mv, out_c, lse_c, dmix)
    small["swa_sinks"] = dsink[:, :SWA_GROUP, 0].reshape(1, SWA_HEADS)
    dcum = jnp.pad(dck.reshape(FOX_HEADS, T).T, ((0, 0), (0, HEAD_DIM - FOX_HEADS)))
    dlogf = _cumsum_rows("forget_cumsum_bwd", dcum, True)
    dproj, dgains = _prep_bwd(proj, gains, dfq, dfk, dfv, dsq, dsk, dsv, dmq, dlogf)
    for row, k in enumerate(["fox_q_gain", "fox_k_gain", "swa_q_gain", "swa_k_gain", "mem_q_gain"]):
        small[k] = dgains[row:row + 1, :]
    small["forget_bias"] = dgains[5:6, :FOX_HEADS]
    grads["w_in"] = _mm2d("proj_in_dw", h2, dproj, "tn", F32, tn=1408, tk=512).reshape(N_CHIPS, -1, PROJ_W)
    dmk_raw, small["mem_k_gain"] = _head_norm_rows_bwd(mk_raw, mem_k_gain, dmk)
    dmvb = dmv.astype(BF16)
    grads["w_mem_k"] = _mm2d("mem_k_dw", mn, dmk_raw, "tn", F32).reshape(N_CHIPS, -1, MEM_HEADS * HEAD_DIM)
    grads["w_mem_v"] = _mm2d("mem_v_dw", mn, dmvb, "tn", F32).reshape(N_CHIPS, -1, MEM_HEADS * HEAD_DIM)
    dmn = _mm2d("mem_k_dx", dmk_raw, wmk, "nt", F32)
    dmn = _mm2d("mem_v_dx", dmvb, wmv, "nt", F32, extras=[dmn], epilogue=lambda accs, ex: [ex[0] + accs[0]])
    _, _, small["mem_norm"] = _rms_bwd("mem_norm_bwd", dmn, memin, mem_norm, jnp.zeros_like(memin), 1.0)
    state_b, token_b = send_off("b", ["w_out", "w_in", "w_mem_k", "w_mem_v"])
    dh2 = _mm2d("proj_in_dx", dproj, win, "nt", F32, tk=1408, after=token_b)
    dx1, dyb1, small["mix_norm"] = _rms_bwd("mix_norm_bwd", dh2, x1, mix_norm, dx2, 0.5)
    dg1, du1, grads["ffn1_gate"], grads["ffn1_up"], dwd1 = _ffn_bwd_w("ffn1", dyb1, h1, g1, u1, a1, wd1, N_CHIPS)
    grads["ffn1_down"] = dwd1.reshape(N_CHIPS, -1, D)
    state_c, token_c = send_off("c", ["ffn1_gate", "ffn1_up", "ffn1_down"])
    dh1 = _ffn_bwd_x("ffn1", dg1, du1, wg1, wu1, token_c)
    grad_x, _, small["ffn1_norm"] = _rms_bwd("ffn1_norm_bwd", dh1, xin, ffn1_norm, dx1, 1.0)

    shapes = {k: given[k].shape for k in SMALL}
    shapes["loss"] = ()
    red_small = _unpack_small(_all_reduce_small(_pack_small(small)), shapes)
    loss = red_small["loss"]
    zero = {"loss": jnp.zeros((), F32)}
    packed = [_pack_small({**zero, **{k: src[k] for k in SMALL}}) for src in (
        {k: given[k] for k in SMALL}, red_small, {k: given["m_" + k] for k in SMALL}, {k: given["v_" + k] for k in SMALL})]
    small_out = _adamw("adamw_small", *packed)
    d_s, m_s, v_s = (_unpack_small(t, shapes) for t in small_out)
    for k in SMALL:
        res[k] = (red_small[k], d_s[k], m_s[k], v_s[k])

    done = finish("a", state_a, small_out[0])
    done = finish("b", state_b, done)
    finish("c", state_c, done)

    outs = [loss, grad_x.reshape(1, T, D)]
    for part in range(4):
        outs += [res[k][part] for k in WEIGHTS]
    return tuple(outs)
```

```python
import functools

import numpy as np
import jax
import jax.numpy as jnp
from jax import lax
from jax.experimental import pallas as pl
from jax.experimental.pallas import tpu as pltpu

F32 = jnp.float32
BF16 = jnp.bfloat16
MESH = pl.DeviceIdType.MESH

HEAD_DIM = 128
FOX_HEADS = 6
SWA_HEADS = 6
SWA_KV_HEADS = 2
SWA_GROUP = SWA_HEADS // SWA_KV_HEADS
MEM_HEADS = 4
WINDOW = 128
EPS = 1e-6
NEG_INF = -1e30
SCALE = HEAD_DIM ** -0.5

C_FQ = 0
C_FK = C_FQ + FOX_HEADS * HEAD_DIM
C_FV = C_FK + FOX_HEADS * HEAD_DIM
C_SQ = C_FV + FOX_HEADS * HEAD_DIM
C_SK = C_SQ + SWA_HEADS * HEAD_DIM
C_SV = C_SK + SWA_KV_HEADS * HEAD_DIM
C_MQ = C_SV + SWA_KV_HEADS * HEAD_DIM
C_FL = C_MQ + MEM_HEADS * HEAD_DIM
PROJ_W = C_FL + HEAD_DIM
FOX_W = FOX_HEADS * HEAD_DIM
REF_GROUPS = [
    (0, FOX_W, C_FQ), (FOX_W, FOX_W, C_FK), (2 * FOX_W, FOX_W, C_FV), (3 * FOX_W, FOX_HEADS, C_FL),
    (3 * FOX_W + FOX_HEADS, SWA_HEADS * HEAD_DIM, C_SQ),
    (3 * FOX_W + FOX_HEADS + SWA_HEADS * HEAD_DIM, SWA_KV_HEADS * HEAD_DIM, C_SK),
    (3 * FOX_W + FOX_HEADS + (SWA_HEADS + SWA_KV_HEADS) * HEAD_DIM, SWA_KV_HEADS * HEAD_DIM, C_SV),
    (3 * FOX_W + FOX_HEADS + (SWA_HEADS + 2 * SWA_KV_HEADS) * HEAD_DIM, MEM_HEADS * HEAD_DIM, C_MQ),
]

ADAM_LR = 0.001
ADAM_B1 = 0.9
ADAM_B2 = 0.999
ADAM_EPS = 1e-08
ADAM_WD = 0.01
ADAM_STEP = 10

V7X_VMEM_LIMIT = 56 * 1024 * 1024
N_CHIPS = 4
N_DEV = 8


def _tile(n, pref, mult=128):
    t = (min(pref, n) // mult) * mult
    while t >= mult:
        if n % t == 0:
            return t
        t -= mult
    return n


def _params(sem):
    return pltpu.CompilerParams(dimension_semantics=sem, vmem_limit_bytes=V7X_VMEM_LIMIT)


_DIMS = {"nn": (((1,), (0,)), ((), ())), "nt": (((1,), (1,)), ((), ())), "tn": (((0,), (0,)), ((), ()))}


def _dot(a, b, mode):
    return lax.dot_general(a, b, _DIMS[mode], preferred_element_type=F32)


def _mm(name, grid, pairs, acc_of, acc_shapes, extras, outs, epilogue, after=None):
    n_p, n_e, n_o, n_a = len(pairs), len(extras), len(outs), len(acc_shapes)
    n_w = 0 if after is None else 1
    nk = grid[2]
    n_in = sum(1 if a is None else 2 for a, *_ in pairs)

    def body(*refs):
        ex = refs[n_in:n_in + n_e]
        out = refs[n_in + n_e + n_w:n_in + n_e + n_w + n_o]
        accs = refs[n_in + n_e + n_w + n_o:]
        parts = [None] * n_a
        at = 0
        for p in range(n_p):
            if pairs[p][0] is None:
                a_ref, b_ref = refs[0], refs[at]
                at += 1
            else:
                a_ref, b_ref = refs[at], refs[at + 1]
                at += 2
            d = _dot(a_ref[...], b_ref[...], pairs[p][4])
            parts[acc_of[p]] = d if parts[acc_of[p]] is None else parts[acc_of[p]] + d

        def finish(vals):
            for o, r in zip(out, epilogue(vals, [e[...] for e in ex])):
                o[...] = r.astype(o.dtype)

        if nk == 1:
            finish(parts)
            return
        k = pl.program_id(2)

        @pl.when(k == 0)
        def _():
            for a, d in zip(accs, parts):
                a[...] = d

        @pl.when((k > 0) & (k < nk - 1))
        def _():
            for a, d in zip(accs, parts):
                a[...] += d

        @pl.when(k == nk - 1)
        def _():
            finish([a[...] + d for a, d in zip(accs, parts)])

    in_specs, args = [], []
    for a, a_spec, b, b_spec, _ in pairs:
        if a is not None:
            in_specs.append(a_spec)
            args.append(a)
        in_specs.append(b_spec)
        args.append(b)
    for e, e_spec in extras:
        in_specs.append(e_spec)
        args.append(e)
    if after is not None:
        in_specs.append(pl.BlockSpec(memory_space=pl.ANY))
        args.append(after)
    res = pl.pallas_call(
        body, name=name, grid=grid, in_specs=in_specs,
        out_specs=[s for _, s in outs], out_shape=[o for o, _ in outs],
        scratch_shapes=[pltpu.VMEM(s, F32) for s in acc_shapes] if nk > 1 else [],
        compiler_params=_params(("parallel", "parallel", "arbitrary")),
    )(*args)
    return res


def _mm2d(name, a, b, mode, out_dtype, tm=512, tn=1024, tk=1024, extras=(), epilogue=None, n_out=1, after=None, n_outer=False):
    if mode == "nn":
        (M, K), N = a.shape, b.shape[1]
    elif mode == "nt":
        (M, K), N = a.shape, b.shape[0]
    else:
        (K, M), N = a.shape, b.shape[1]
    tm, tn, tk = _tile(M, tm), _tile(N, tn), _tile(K, tk)

    def spec(shape, index):
        if n_outer:
            return pl.BlockSpec(shape, lambda j, i, k: index(i, j, k))
        return pl.BlockSpec(shape, index)

    a_spec = spec((tk, tm), lambda i, j, k: (k, i)) if mode == "tn" else spec((tm, tk), lambda i, j, k: (i, k))
    b_spec = spec((tn, tk), lambda i, j, k: (j, k)) if mode == "nt" else spec((tk, tn), lambda i, j, k: (k, j))
    mn = spec((tm, tn), lambda i, j, k: (i, j))
    if epilogue is None:
        epilogue = lambda accs, ex: [accs[0]]
    if not isinstance(out_dtype, (list, tuple)):
        out_dtype = [out_dtype] * n_out
    grid = (N // tn, M // tm, K // tk) if n_outer else (M // tm, N // tn, K // tk)
    res = _mm(name, grid, [(a, a_spec, b, b_spec, mode)], [0], [(tm, tn)],
              [(e, mn) for e in extras], [(jax.ShapeDtypeStruct((M, N), d), mn) for d in out_dtype], epilogue, after=after)
    return res[0] if len(res) == 1 else res


def _sigmoid(x):
    return 1.0 / (1.0 + jnp.exp(-x))


def _sigmoid_fast(x):
    return pl.reciprocal(1.0 + jnp.exp(-x), approx=True)


def _ffn_gu(name, h, wg, wu):
    T, D = h.shape
    nf, _, F4 = wg.shape
    tm, tk = _tile(T, 512), _tile(D, 2048)
    a_spec = pl.BlockSpec((tm, tk), lambda j, i, k: (i, k))
    b_spec = pl.BlockSpec((None, tk, F4), lambda j, i, k: (j, k, 0))
    o_spec = pl.BlockSpec((tm, F4), lambda j, i, k: (i, j))

    def epilogue(accs, ex):
        g, u = accs
        return [g, u, g * _sigmoid_fast(g) * u]

    sds = jax.ShapeDtypeStruct((T, nf * F4), BF16)
    return _mm(name, (nf, T // tm, D // tk), [(h, a_spec, wg, b_spec, "nn"), (None, None, wu, b_spec, "nn")], [0, 1],
               [(tm, F4), (tm, F4)], [], [(sds, o_spec)] * 3, epilogue)


def _ffn_down(name, a, wd, xres):
    return _mm2d(name, a, wd, "nn", F32, tm=1024, tk=wd.shape[0] // N_CHIPS, extras=[xres],
                 epilogue=lambda accs, ex: [ex[0] + 0.5 * accs[0]])


def _ffn_bwd_w(tag, dyb, h, g, u, a, wd, nf):
    T, D = h.shape
    F4 = wd.shape[0] // nf

    def act_bwd(accs, ex):
        gf, uf = ex[0].astype(F32), ex[1].astype(F32)
        s = _sigmoid_fast(gf)
        return [accs[0] * uf * s * (1.0 + gf * (1.0 - s)), accs[0] * gf * s]

    dg, du = _mm2d(tag + "_da", dyb, wd, "nt", BF16, tn=F4, tk=2048, extras=[g, u], epilogue=act_bwd, n_out=2, n_outer=True)
    dwd = _mm2d(tag + "_dwd", a, dyb, "tn", F32, tm=F4, tk=1024)

    tm, tk = _tile(D, 512), _tile(T, 1024)
    h_spec = pl.BlockSpec((tk, tm), lambda i, j, k: (k, i))
    d_spec = pl.BlockSpec((tk, F4), lambda i, j, k: (k, j))
    w_spec = pl.BlockSpec((None, tm, F4), lambda i, j, k: (j, i, 0))
    sds = jax.ShapeDtypeStruct((nf, D, F4), F32)
    dwg, dwu = _mm(tag + "_dwgu", (D // tm, nf, T // tk), [(h, h_spec, dg, d_spec, "tn"), (None, None, du, d_spec, "tn")],
                   [0, 1], [(tm, F4), (tm, F4)], [], [(sds, w_spec)] * 2, lambda accs, ex: accs)
    return dg, du, dwg, dwu, dwd


def _ffn_bwd_x(tag, dg, du, wg, wu, after):
    T = dg.shape[0]
    nf, D, F4 = wg.shape
    tm, tn = _tile(T, 512), _tile(D, 1024)
    a_spec = pl.BlockSpec((tm, F4), lambda i, j, k: (i, k))
    b_spec = pl.BlockSpec((None, tn, F4), lambda i, j, k: (k, j, 0))
    o_spec = pl.BlockSpec((tm, tn), lambda i, j, k: (i, j))
    (dh,) = _mm(tag + "_dh", (T // tm, D // tn, nf), [(dg, a_spec, wg, b_spec, "nt"), (du, a_spec, wu, b_spec, "nt")],
                [0, 0], [(tm, tn)], [], [(jax.ShapeDtypeStruct((T, D), F32), o_spec)], lambda accs, ex: accs, after=after)
    return dh


def _rms_fwd(name, x, gain):
    R, D = x.shape
    tr = _tile(R, 256, 8)

    def body(x_ref, g_ref, o_ref):
        xv = x_ref[...]
        r = lax.rsqrt(jnp.mean(xv * xv, axis=-1, keepdims=True) + EPS)
        o_ref[...] = (xv * r * g_ref[...]).astype(BF16)

    return pl.pallas_call(
        body, name=name, grid=(R // tr,),
        in_specs=[pl.BlockSpec((tr, D), lambda i: (i, 0)), pl.BlockSpec((1, D), lambda i: (0, 0))],
        out_specs=pl.BlockSpec((tr, D), lambda i: (i, 0)), out_shape=jax.ShapeDtypeStruct((R, D), BF16),
        compiler_params=_params(("parallel",)),
    )(x, gain)


def _rms_bwd(name, dh, x, gain, dres, bscale):
    R, D = x.shape
    tr = _tile(R, 256, 8)

    def body(dh_ref, x_ref, g_ref, dres_ref, dx_ref, dxb_ref, dg_ref):
        xv, dy = x_ref[...], dh_ref[...]
        r = lax.rsqrt(jnp.mean(xv * xv, axis=-1, keepdims=True) + EPS)
        xn = xv * r
        uu = dy * g_ref[...]
        dx = dres_ref[...] + r * (uu - xn * jnp.mean(xn * uu, axis=-1, keepdims=True))
        dx_ref[...] = dx
        dxb_ref[...] = (bscale * dx).astype(BF16)
        part = jnp.sum(dy * xn, axis=0, keepdims=True)

        @pl.when(pl.program_id(0) == 0)
        def _():
            dg_ref[...] = part

        @pl.when(pl.program_id(0) > 0)
        def _():
            dg_ref[...] += part

    row = pl.BlockSpec((tr, D), lambda i: (i, 0))
    vec = pl.BlockSpec((1, D), lambda i: (0, 0))
    return pl.pallas_call(
        body, name=name, grid=(R // tr,), in_specs=[row, row, vec, row], out_specs=[row, row, vec],
        out_shape=[jax.ShapeDtypeStruct((R, D), F32), jax.ShapeDtypeStruct((R, D), BF16), jax.ShapeDtypeStruct((1, D), F32)],
        compiler_params=_params(("arbitrary",)),
    )(dh, x, gain, dres)


def _loss_head(y, target):
    R, D = y.shape
    tr = _tile(R, 256, 8)

    def body(y_ref, t_ref, d_ref, db_ref, l_ref):
        e = y_ref[...] - t_ref[...]
        d = e * (1.0 / D)
        d_ref[...] = d
        db_ref[...] = (0.5 * d).astype(BF16)
        part = jnp.zeros((8, 128), F32) + (0.5 / D) * jnp.sum(e * e)

        @pl.when(pl.program_id(0) == 0)
        def _():
            l_ref[...] = part

        @pl.when(pl.program_id(0) > 0)
        def _():
            l_ref[...] += part

    row = pl.BlockSpec((tr, D), lambda i: (i, 0))
    acc = pl.BlockSpec((8, 128), lambda i: (0, 0))
    return pl.pallas_call(
        body, name="loss_head", grid=(R // tr,), in_specs=[row, row], out_specs=[row, row, acc],
        out_shape=[jax.ShapeDtypeStruct((R, D), F32), jax.ShapeDtypeStruct((R, D), BF16), jax.ShapeDtypeStruct((8, 128), F32)],
        compiler_params=_params(("arbitrary",)),
    )(y, target)


def _head_norm(xs, g):
    r = lax.rsqrt(jnp.mean(xs * xs, axis=-1, keepdims=True) + EPS)
    return xs * r * g


def _head_norm_bwd(xs, g, dy):
    r = lax.rsqrt(jnp.mean(xs * xs, axis=-1, keepdims=True) + EPS)
    xn = xs * r
    uu = dy * g
    return r * (uu - xn * jnp.mean(xn * uu, axis=-1, keepdims=True)), jnp.sum(dy * xn, axis=0, keepdims=True)


NORMED = [(C_FQ, FOX_HEADS, 0), (C_FK, FOX_HEADS, 1), (C_SQ, SWA_HEADS, 2), (C_SK, SWA_KV_HEADS, 3), (C_MQ, MEM_HEADS, 4)]
PLAIN = [(C_FV, FOX_HEADS), (C_SV, SWA_KV_HEADS)]


def _prep_fwd(proj, gains):
    T = proj.shape[0]
    tr = _tile(T, 256, 8)

    def body(p_ref, g_ref, o_ref, lf_ref):
        for start, heads, row in NORMED:
            gn = g_ref[row:row + 1, :]
            for hh in range(heads):
                sl = slice(start + hh * HEAD_DIM, start + (hh + 1) * HEAD_DIM)
                o_ref[:, sl] = _head_norm(p_ref[:, sl], gn).astype(BF16)
        for start, heads in PLAIN:
            sl = slice(start, start + heads * HEAD_DIM)
            o_ref[:, sl] = p_ref[:, sl].astype(BF16)
        zb = p_ref[:, C_FL:C_FL + HEAD_DIM] + g_ref[5:6, :]
        o_ref[:, C_FL:C_FL + HEAD_DIM] = jnp.zeros((tr, HEAD_DIM), BF16)
        lf_ref[...] = jnp.minimum(zb, 0.0) - jnp.log(1.0 + jnp.exp(-jnp.abs(zb)))

    return pl.pallas_call(
        body, name="prep_fwd", grid=(T // tr,),
        in_specs=[pl.BlockSpec((tr, PROJ_W), lambda i: (i, 0)), pl.BlockSpec((8, 128), lambda i: (0, 0))],
        out_specs=[pl.BlockSpec((tr, PROJ_W), lambda i: (i, 0)), pl.BlockSpec((tr, HEAD_DIM), lambda i: (i, 0))],
        out_shape=[jax.ShapeDtypeStruct((T, PROJ_W), BF16), jax.ShapeDtypeStruct((T, HEAD_DIM), F32)],
        compiler_params=_params(("parallel",)),
    )(proj, gains)


def _prep_bwd(proj, gains, dfq, dfk, dfv, dsq, dsk, dsv, dmq, dlogf):
    T = proj.shape[0]
    tr = _tile(T, 256, 8)
    d_normed = {C_FQ: 0, C_FK: 1, C_SQ: 3, C_SK: 4, C_MQ: 6}
    d_plain = {C_FV: 2, C_SV: 5}

    def body(p_ref, g_ref, *rest):
        d_refs, dlf_ref, o_ref, dg_ref = rest[:7], rest[7], rest[8], rest[9]
        rows = []
        for start, heads, row in NORMED:
            gn = g_ref[row:row + 1, :]
            d_ref = d_refs[d_normed[start]]
            tot = jnp.zeros((1, HEAD_DIM), F32)
            for hh in range(heads):
                sl = slice(start + hh * HEAD_DIM, start + (hh + 1) * HEAD_DIM)
                dx, dgn = _head_norm_bwd(p_ref[:, sl], gn, d_ref[:, hh * HEAD_DIM:(hh + 1) * HEAD_DIM])
                o_ref[:, sl] = dx.astype(BF16)
                tot = tot + dgn
            rows.append(tot)
        for start, heads in PLAIN:
            o_ref[:, start:start + heads * HEAD_DIM] = d_refs[d_plain[start]][...].astype(BF16)
        zb = p_ref[:, C_FL:C_FL + HEAD_DIM] + g_ref[5:6, :]
        lane = lax.broadcasted_iota(jnp.int32, (tr, HEAD_DIM), 1)
        dz = jnp.where(lane < FOX_HEADS, dlf_ref[...] * (1.0 - _sigmoid(zb)), 0.0)
        o_ref[:, C_FL:C_FL + HEAD_DIM] = dz.astype(BF16)
        rows.append(jnp.sum(dz, axis=0, keepdims=True))
        part = jnp.concatenate(rows + [jnp.zeros((2, HEAD_DIM), F32)], axis=0)

        @pl.when(pl.program_id(0) == 0)
        def _():
            dg_ref[...] = part

        @pl.when(pl.program_id(0) > 0)
        def _():
            dg_ref[...] += part

    def rows_of(w):
        return pl.BlockSpec((tr, w), lambda i: (i, 0))

    small = pl.BlockSpec((8, 128), lambda i: (0, 0))
    ds = [dfq, dfk, dfv, dsq, dsk, dsv, dmq]
    return pl.pallas_call(
        body, name="prep_bwd", grid=(T // tr,),
        in_specs=[rows_of(PROJ_W), small] + [rows_of(d.shape[1]) for d in ds] + [rows_of(HEAD_DIM)],
        out_specs=[rows_of(PROJ_W), small],
        out_shape=[jax.ShapeDtypeStruct((T, PROJ_W), BF16), jax.ShapeDtypeStruct((8, 128), F32)],
        compiler_params=_params(("arbitrary",)),
    )(proj, gains, *ds, dlogf)


def _head_norm_rows(x, gain):
    R, W = x.shape

    def body(x_ref, g_ref, o_ref):
        for hh in range(W // HEAD_DIM):
            sl = slice(hh * HEAD_DIM, (hh + 1) * HEAD_DIM)
            o_ref[:, sl] = _head_norm(x_ref[:, sl], g_ref[...]).astype(BF16)

    return pl.pallas_call(body, name="mem_k_norm", out_shape=jax.ShapeDtypeStruct((R, W), BF16))(x, gain)


def _head_norm_rows_bwd(x, gain, dy):
    R, W = x.shape

    def body(x_ref, g_ref, dy_ref, dx_ref, dg_ref):
        tot = jnp.zeros((1, HEAD_DIM), F32)
        for hh in range(W // HEAD_DIM):
            sl = slice(hh * HEAD_DIM, (hh + 1) * HEAD_DIM)
            dx, dgn = _head_norm_bwd(x_ref[:, sl], g_ref[...], dy_ref[:, sl])
            dx_ref[:, sl] = dx.astype(BF16)
            tot = tot + dgn
        dg_ref[...] = tot

    return pl.pallas_call(
        body, name="mem_k_norm_bwd",
        out_shape=[jax.ShapeDtypeStruct((R, W), BF16), jax.ShapeDtypeStruct((1, HEAD_DIM), F32)])(x, gain, dy)


def _cumsum_rows(name, x, reverse):
    T, W = x.shape
    tb = _tile(T, 512, 8)
    nb = T // tb

    def body(x_ref, o_ref, carry):
        @pl.when(pl.program_id(0) == 0)
        def _():
            carry[...] = jnp.zeros_like(carry)

        xv = x_ref[...]
        r = lax.broadcasted_iota(jnp.int32, (tb, tb), 0)
        cc = lax.broadcasted_iota(jnp.int32, (tb, tb), 1)
        tri = jnp.where((cc >= r) if reverse else (cc <= r), 1.0, 0.0).astype(F32)
        o_ref[...] = jnp.dot(tri, xv, precision=lax.Precision.HIGHEST, preferred_element_type=F32) + carry[...]
        carry[...] += jnp.sum(xv, axis=0, keepdims=True)

    idx = (lambda i: (nb - 1 - i, 0)) if reverse else (lambda i: (i, 0))
    return pl.pallas_call(
        body, name=name, grid=(nb,), in_specs=[pl.BlockSpec((tb, W), idx)], out_specs=pl.BlockSpec((tb, W), idx),
        out_shape=jax.ShapeDtypeStruct((T, W), F32), scratch_shapes=[pltpu.VMEM((1, W), F32)],
        compiler_params=_params(("arbitrary",)),
    )(x)


def _triangle(nq, by_column):
    if by_column:
        blocks = [(i, j) for j in range(nq) for i in range(j, nq)]
    else:
        blocks = [(i, j) for i in range(nq) for j in range(i + 1)]
    return jnp.asarray(np.array(blocks, np.int32).T)


def _fox_scores(q, k, cq, ck, on_diagonal):
    s = _dot(q, k, "nt") * SCALE + (cq - ck)
    if on_diagonal:
        n = s.shape[0]
        s = jnp.where(lax.broadcasted_iota(jnp.int32, (n, n), 1) <= lax.broadcasted_iota(jnp.int32, (n, n), 0), s, NEG_INF)
    return s


def _fox_fwd(qkv, cq, ck):
    T = qkv.shape[0]
    tq = _tile(T, 512)
    nq = T // tq
    steps = nq * (nq + 1) // 2
    HQ, HK, HV = C_FQ // HEAD_DIM, C_FK // HEAD_DIM, C_FV // HEAD_DIM

    def body(tab, q_ref, k_ref, v_ref, cq_ref, ck_ref, o_ref, of_ref, lse_ref, m_sc, l_sc, acc_sc):
        i, j = tab[0, pl.program_id(1)], tab[1, pl.program_id(1)]

        @pl.when(j == 0)
        def _():
            m_sc[...] = jnp.full_like(m_sc, NEG_INF)
            l_sc[...] = jnp.zeros_like(l_sc)
            acc_sc[...] = jnp.zeros_like(acc_sc)

        def step(on_diagonal):
            s = _fox_scores(q_ref[...], k_ref[...], cq_ref[...], ck_ref[...], on_diagonal)
            m_new = jnp.maximum(m_sc[...], jnp.max(s, axis=-1, keepdims=True))
            alpha = jnp.exp(m_sc[...] - m_new)
            p = jnp.exp(s - m_new)
            l_sc[...] = alpha * l_sc[...] + jnp.sum(p, axis=-1, keepdims=True)
            acc_sc[...] = alpha * acc_sc[...] + _dot(p.astype(BF16), v_ref[...], "nn")
            m_sc[...] = m_new

        @pl.when(j < i)
        def _():
            step(False)

        @pl.when(j == i)
        def _():
            step(True)
            o = acc_sc[...] / l_sc[...]
            o_ref[...] = o.astype(BF16)
            of_ref[...] = o
            lse_ref[...] = m_sc[...] + jnp.log(l_sc[...])

    def rows(base):
        return pl.BlockSpec((tq, HEAD_DIM), lambda h, s, tab: (tab[0, s], base + h))

    def cols(base):
        return pl.BlockSpec((tq, HEAD_DIM), lambda h, s, tab: (tab[1, s], base + h))

    rvec = pl.BlockSpec((None, tq, 1), lambda h, s, tab: (h, tab[0, s], 0))
    return pl.pallas_call(
        body, name="fox_fwd",
        grid_spec=pltpu.PrefetchScalarGridSpec(
            num_scalar_prefetch=1, grid=(FOX_HEADS, steps),
            in_specs=[rows(HQ), cols(HK), cols(HV), rvec, pl.BlockSpec((None, 1, tq), lambda h, s, tab: (h, 0, tab[1, s]))],
            out_specs=[rows(0), rows(0), rvec],
            scratch_shapes=[pltpu.VMEM((tq, 1), F32), pltpu.VMEM((tq, 1), F32), pltpu.VMEM((tq, HEAD_DIM), F32)]),
        out_shape=[jax.ShapeDtypeStruct((T, FOX_W), BF16), jax.ShapeDtypeStruct((T, FOX_W), F32),
                   jax.ShapeDtypeStruct((FOX_HEADS, T, 1), F32)],
        compiler_params=_params(("parallel", "arbitrary")),
    )(_triangle(nq, False), qkv, qkv, qkv, cq, ck)


def _fox_bwd(qkv, cq, ck, out_f32, lse, dmix):
    T = qkv.shape[0]
    tq = _tile(T, 512)
    nq = T // tq
    steps = nq * (nq + 1) // 2
    HQ, HK, HV = C_FQ // HEAD_DIM, C_FK // HEAD_DIM, C_FV // HEAD_DIM

    def body(tab, q_ref, k_ref, v_ref, cq_ref, ck_ref, o_ref, lse_ref, do_ref, dq_ref, dk_ref, dv_ref, dck_ref, dk_sc, dv_sc, dc_sc):
        qi, kj = tab[0, pl.program_id(1)], tab[1, pl.program_id(1)]

        @pl.when(qi == kj)
        def _():
            dk_sc[...] = jnp.zeros_like(dk_sc)
            dv_sc[...] = jnp.zeros_like(dv_sc)
            dc_sc[...] = jnp.zeros_like(dc_sc)

        def step(on_diagonal):
            q, k, v, do = q_ref[...], k_ref[...], v_ref[...], do_ref[...]
            p = jnp.exp(_fox_scores(q, k, cq_ref[...], ck_ref[...], on_diagonal) - lse_ref[...])
            dp = _dot(do, v, "nt")
            delta = jnp.sum(do.astype(F32) * o_ref[...], axis=-1, keepdims=True)
            ds = p * (dp - delta)
            dsb = ds.astype(BF16)
            dv_sc[...] += _dot(p.astype(BF16), do, "tn")
            dk_sc[...] += _dot(dsb, q, "tn")
            dc_sc[...] += jnp.sum(ds, axis=0, keepdims=True)
            dq_part = _dot(dsb, k, "nn") * SCALE
            rows_q = pl.ds(pl.multiple_of(qi * tq, tq), tq)

            @pl.when(kj == 0)
            def _():
                dq_ref[rows_q, :] = dq_part

            @pl.when(kj > 0)
            def _():
                dq_ref[rows_q, :] += dq_part

        @pl.when(qi > kj)
        def _():
            step(False)

        @pl.when(qi == kj)
        def _():
            step(True)

        @pl.when(qi == nq - 1)
        def _():
            dk_ref[...] = dk_sc[...] * SCALE
            dv_ref[...] = dv_sc[...]
            dck_ref[...] = -dc_sc[...]

    def rows(base):
        return pl.BlockSpec((tq, HEAD_DIM), lambda h, s, tab: (tab[0, s], base + h))

    def cols(base):
        return pl.BlockSpec((tq, HEAD_DIM), lambda h, s, tab: (tab[1, s], base + h))

    rvec = pl.BlockSpec((None, tq, 1), lambda h, s, tab: (h, tab[0, s], 0))
    cvec = pl.BlockSpec((None, 1, tq), lambda h, s, tab: (h, 0, tab[1, s]))
    sds = jax.ShapeDtypeStruct((T, FOX_W), F32)
    return pl.pallas_call(
        body, name="fox_bwd",
        grid_spec=pltpu.PrefetchScalarGridSpec(
            num_scalar_prefetch=1, grid=(FOX_HEADS, steps),
            in_specs=[rows(HQ), cols(HK), cols(HV), rvec, cvec, rows(0), rvec, rows(0)],
            out_specs=[pl.BlockSpec((T, HEAD_DIM), lambda h, s, tab: (0, h)), cols(0), cols(0), cvec],
            scratch_shapes=[pltpu.VMEM((tq, HEAD_DIM), F32), pltpu.VMEM((tq, HEAD_DIM), F32), pltpu.VMEM((1, tq), F32)]),
        out_shape=[sds, sds, sds, jax.ShapeDtypeStruct((FOX_HEADS, 1, T), F32)],
        compiler_params=_params(("parallel", "arbitrary")),
    )(_triangle(nq, True), qkv, qkv, qkv, cq, ck, out_f32, lse, dmix)


GW = SWA_GROUP * HEAD_DIM
GR = SWA_GROUP * WINDOW


def _swa_scores(q_ref, kp_ref, kc_ref, slope_ref, n):
    q = q_ref[...]
    qs = jnp.concatenate([q[:, t * HEAD_DIM:(t + 1) * HEAD_DIM] for t in range(SWA_GROUP)], axis=0)
    kb = jnp.concatenate([kp_ref[...], kc_ref[...]], axis=0)
    r = lax.broadcasted_iota(jnp.int32, (GR, 2 * WINDOW), 0) & (WINDOW - 1)
    jj = lax.broadcasted_iota(jnp.int32, (GR, 2 * WINDOW), 1)
    dist = WINDOW + r - jj
    valid = (dist >= 0) & (dist < WINDOW) & ((n > 0) | (jj >= WINDOW))
    s = _dot(qs, kb, "nt") * SCALE - slope_ref[...] * dist.astype(F32)
    return qs, kb, jnp.where(valid, s, NEG_INF), valid


def _swa_specs():
    HQ, HK, HV = C_SQ // GW, C_SK // HEAD_DIM, C_SV // HEAD_DIM
    q_spec = pl.BlockSpec((WINDOW, GW), lambda g, n: (n, HQ + g))

    def prev(base):
        return pl.BlockSpec((WINDOW, HEAD_DIM), lambda g, n: (jnp.maximum(n - 1, 0), base + g))

    def cur(base):
        return pl.BlockSpec((WINDOW, HEAD_DIM), lambda g, n: (n, base + g))

    col = pl.BlockSpec((None, GR, 1), lambda g, n: (g, 0, 0))
    return q_spec, prev(HK), cur(HK), prev(HV), cur(HV), col


def _swa_fwd(qkv, slopes, sinks):
    T = qkv.shape[0]
    nb = T // WINDOW
    assert C_SQ % GW == 0

    def body(q_ref, kp_ref, kc_ref, vp_ref, vc_ref, slope_ref, sink_ref, o_ref, lse_ref):
        n = pl.program_id(1)
        _, _, s, _ = _swa_scores(q_ref, kp_ref, kc_ref, slope_ref, n)
        m = jnp.maximum(jnp.max(s, axis=-1, keepdims=True), sink_ref[...])
        p = jnp.exp(s - m)
        l = jnp.sum(p, axis=-1, keepdims=True) + jnp.exp(sink_ref[...] - m)
        vb = jnp.concatenate([vp_ref[...], vc_ref[...]], axis=0)
        o = _dot(p.astype(BF16), vb, "nn") / l
        for t in range(SWA_GROUP):
            o_ref[:, t * HEAD_DIM:(t + 1) * HEAD_DIM] = o[t * WINDOW:(t + 1) * WINDOW, :].astype(BF16)
        lse_ref[...] = m + jnp.log(l)

    q_spec, kp, kc, vp, vc, col = _swa_specs()
    return pl.pallas_call(
        body, name="swa_fwd", grid=(SWA_KV_HEADS, nb), in_specs=[q_spec, kp, kc, vp, vc, col, col],
        out_specs=[pl.BlockSpec((WINDOW, GW), lambda g, n: (n, g)), pl.BlockSpec((None, None, GR, 1), lambda g, n: (g, n, 0, 0))],
        out_shape=[jax.ShapeDtypeStruct((T, SWA_HEADS * HEAD_DIM), BF16), jax.ShapeDtypeStruct((SWA_KV_HEADS, nb, GR, 1), F32)],
        compiler_params=_params(("parallel", "arbitrary")),
    )(qkv, qkv, qkv, qkv, qkv, slopes, sinks)


def _swa_bwd(qkv, slopes, sinks, out, lse, dmix):
    T = qkv.shape[0]
    nb = T // WINDOW
    DO = FOX_W // GW
    assert FOX_W % GW == 0

    def body(q_ref, kp_ref, kc_ref, vp_ref, vc_ref, slope_ref, sink_ref, o_ref, lse_ref, do_ref,
             dq_ref, dk_ref, dv_ref, dsink_ref, sink_sc):
        n = pl.program_id(1)

        @pl.when(n == 0)
        def _():
            dk_ref[...] = jnp.zeros_like(dk_ref)
            dv_ref[...] = jnp.zeros_like(dv_ref)
            sink_sc[...] = jnp.zeros_like(sink_sc)

        qs, kb, s, valid = _swa_scores(q_ref, kp_ref, kc_ref, slope_ref, n)
        lse = lse_ref[...]
        p = jnp.where(valid, jnp.exp(s - lse), 0.0)
        vb = jnp.concatenate([vp_ref[...], vc_ref[...]], axis=0)
        do = jnp.concatenate([do_ref[:, t * HEAD_DIM:(t + 1) * HEAD_DIM] for t in range(SWA_GROUP)], axis=0)
        oo = jnp.concatenate([o_ref[:, t * HEAD_DIM:(t + 1) * HEAD_DIM] for t in range(SWA_GROUP)], axis=0)
        dp = _dot(do, vb, "nt")
        delta = jnp.sum(do.astype(F32) * oo.astype(F32), axis=-1, keepdims=True)
        ds = p * (dp - delta)
        dsb = ds.astype(BF16)
        dq = _dot(dsb, kb, "nn") * SCALE
        for t in range(SWA_GROUP):
            dq_ref[:, t * HEAD_DIM:(t + 1) * HEAD_DIM] = dq[t * WINDOW:(t + 1) * WINDOW, :]
        dkb = _dot(dsb, qs, "tn") * SCALE
        dvb = _dot(p.astype(BF16), do, "tn")
        r_prev = pl.ds(pl.multiple_of(jnp.maximum(n - 1, 0) * WINDOW, WINDOW), WINDOW)
        r_cur = pl.ds(pl.multiple_of(n * WINDOW, WINDOW), WINDOW)
        dk_ref[r_prev, :] += dkb[:WINDOW, :]
        dk_ref[r_cur, :] += dkb[WINDOW:, :]
        dv_ref[r_prev, :] += dvb[:WINDOW, :]
        dv_ref[r_cur, :] += dvb[WINDOW:, :]
        sink_sc[...] -= jnp.exp(sink_ref[...] - lse) * delta

        @pl.when(n == nb - 1)
        def _():
            tot = [jnp.zeros((1, 128), F32) + jnp.sum(sink_sc[t * WINDOW:(t + 1) * WINDOW, :]) for t in range(SWA_GROUP)]
            dsink_ref[...] = jnp.concatenate(tot + [jnp.zeros((8 - SWA_GROUP, 128), F32)], axis=0)

    q_spec, kp, kc, vp, vc, col = _swa_specs()
    kv_acc = pl.BlockSpec((T, HEAD_DIM), lambda g, n: (0, g))
    return pl.pallas_call(
        body, name="swa_bwd", grid=(SWA_KV_HEADS, nb),
        in_specs=[q_spec, kp, kc, vp, vc, col, col, pl.BlockSpec((WINDOW, GW), lambda g, n: (n, g)),
                  pl.BlockSpec((None, None, GR, 1), lambda g, n: (g, n, 0, 0)), pl.BlockSpec((WINDOW, GW), lambda g, n: (n, DO + g))],
        out_specs=[pl.BlockSpec((WINDOW, GW), lambda g, n: (n, g)), kv_acc, kv_acc, pl.BlockSpec((None, 8, 128), lambda g, n: (g, 0, 0))],
        out_shape=[jax.ShapeDtypeStruct((T, SWA_HEADS * HEAD_DIM), F32), jax.ShapeDtypeStruct((T, SWA_KV_HEADS * HEAD_DIM), F32),
                   jax.ShapeDtypeStruct((T, SWA_KV_HEADS * HEAD_DIM), F32), jax.ShapeDtypeStruct((SWA_KV_HEADS, 8, 128), F32)],
        scratch_shapes=[pltpu.VMEM((GR, 1), F32)],
        compiler_params=_params(("parallel", "arbitrary")),
    )(qkv, qkv, qkv, qkv, qkv, slopes, sinks, out, lse, dmix)


def _mem_fwd(qkv, mk, mv):
    T, ML = qkv.shape[0], mk.shape[0]
    tq = _tile(T, 512)
    HQ = C_MQ // HEAD_DIM

    def body(q_ref, k_ref, v_ref, o_ref, lse_ref):
        s = _dot(q_ref[...], k_ref[...], "nt") * SCALE
        m = jnp.max(s, axis=-1, keepdims=True)
        p = jnp.exp(s - m)
        l = jnp.sum(p, axis=-1, keepdims=True)
        o_ref[...] = (_dot(p.astype(BF16), v_ref[...], "nn") / l).astype(BF16)
        lse_ref[...] = m + jnp.log(l)

    kv = pl.BlockSpec((ML, HEAD_DIM), lambda h, i: (0, h))
    return pl.pallas_call(
        body, name="mem_fwd", grid=(MEM_HEADS, T // tq),
        in_specs=[pl.BlockSpec((tq, HEAD_DIM), lambda h, i: (i, HQ + h)), kv, kv],
        out_specs=[pl.BlockSpec((tq, HEAD_DIM), lambda h, i: (i, h)), pl.BlockSpec((None, tq, 1), lambda h, i: (h, i, 0))],
        out_shape=[jax.ShapeDtypeStruct((T, MEM_HEADS * HEAD_DIM), BF16), jax.ShapeDtypeStruct((MEM_HEADS, T, 1), F32)],
        compiler_params=_params(("parallel", "arbitrary")),
    )(qkv, mk, mv)


def _mem_bwd(qkv, mk, mv, out, lse, dmix):
    T, ML = qkv.shape[0], mk.shape[0]
    tq = _tile(T, 512)
    HQ = C_MQ // HEAD_DIM
    DO = (FOX_W + SWA_HEADS * HEAD_DIM) // HEAD_DIM

    def body(q_ref, k_ref, v_ref, o_ref, lse_ref, do_ref, dq_ref, dk_ref, dv_ref):
        q, k, v, do = q_ref[...], k_ref[...], v_ref[...], do_ref[...]
        p = jnp.exp(_dot(q, k, "nt") * SCALE - lse_ref[...])
        dp = _dot(do, v, "nt")
        delta = jnp.sum(do.astype(F32) * o_ref[...].astype(F32), axis=-1, keepdims=True)
        dsb = (p * (dp - delta)).astype(BF16)
        dq_ref[...] = _dot(dsb, k, "nn") * SCALE
        dk_part = _dot(dsb, q, "tn") * SCALE
        dv_part = _dot(p.astype(BF16), do, "tn")

        @pl.when(pl.program_id(1) == 0)
        def _():
            dk_ref[...] = dk_part
            dv_ref[...] = dv_part

        @pl.when(pl.program_id(1) > 0)
        def _():
            dk_ref[...] += dk_part
            dv_ref[...] += dv_part

    kv = pl.BlockSpec((ML, HEAD_DIM), lambda h, i: (0, h))
    qb = pl.BlockSpec((tq, HEAD_DIM), lambda h, i: (i, h))
    return pl.pallas_call(
        body, name="mem_bwd", grid=(MEM_HEADS, T // tq),
        in_specs=[pl.BlockSpec((tq, HEAD_DIM), lambda h, i: (i, HQ + h)), kv, kv, qb,
                  pl.BlockSpec((None, tq, 1), lambda h, i: (h, i, 0)), pl.BlockSpec((tq, HEAD_DIM), lambda h, i: (i, DO + h))],
        out_specs=[qb, kv, kv],
        out_shape=[jax.ShapeDtypeStruct((T, MEM_HEADS * HEAD_DIM), F32), jax.ShapeDtypeStruct((ML, MEM_HEADS * HEAD_DIM), F32),
                   jax.ShapeDtypeStruct((ML, MEM_HEADS * HEAD_DIM), F32)],
        compiler_params=_params(("parallel", "arbitrary")),
    )(qkv, mk, mv, out, lse, dmix)


HBM = pl.BlockSpec(memory_space=pltpu.HBM)


def _place():
    x, y, c = lax.axis_index("x"), lax.axis_index("y"), lax.axis_index("c")
    chips = [(1 - x, y), (x, 1 - y), (1 - x, 1 - y)]
    return x, y, c, chips


def _remote(src, dst, send_sem, recv_sem, device):
    return pltpu.make_async_remote_copy(src_ref=src, dst_ref=dst, send_sem=send_sem, recv_sem=recv_sem,
                                        device_id=device, device_id_type=MESH)


def _place_ids():
    x, y, c = lax.axis_index("x"), lax.axis_index("y"), lax.axis_index("c")
    order = [2 * x + y, 2 * (1 - x) + y, 2 * x + (1 - y), 2 * (1 - x) + (1 - y)]
    return jnp.stack([2 * x + y, c] + order).astype(jnp.int32)


def _cast_place(name, w, ids):
    R, C = w.shape
    tr = _tile(R, 256, 16)

    def body(ids_ref, w_ref, o_ref):
        o_ref[...] = w_ref[...].astype(BF16)

    return pl.pallas_call(
        body, name=name,
        grid_spec=pltpu.PrefetchScalarGridSpec(
            num_scalar_prefetch=1, grid=(R // tr,), in_specs=[pl.BlockSpec((tr, C), lambda i, ids: (i, 0))],
            out_specs=pl.BlockSpec((None, tr, C), lambda i, ids: (ids[0], i, 0))),
        out_shape=jax.ShapeDtypeStruct((N_CHIPS, R, C), BF16), compiler_params=_params(("parallel",)),
    )(ids, w)


SEM = pl.BlockSpec(memory_space=pltpu.SEMAPHORE)
EFFECT = pltpu.SideEffectType.DATAFLOW_SIDE_EFFECTING


def _hbm(a):
    return pltpu.with_memory_space_constraint(a, pltpu.HBM)


def _gather_start(name, placed, after):
    n = len(placed)

    ns = 3 * n

    def body(*refs):
        send, recv = refs[n + 1:n + 1 + ns], refs[n + 1 + ns:n + 1 + 2 * ns]
        buf = refs[n + 1 + 2 * ns:2 * n + 1 + 2 * ns]
        token = refs[2 * n + 1 + 2 * ns]
        x, y, c, chips = _place()
        me = 2 * x + y
        for a in range(n):
            half = buf[a].shape[1] // 2
            mine = buf[a].at[me, pl.ds(c * half, half)]
            for j, (cx, cy) in enumerate(chips):
                _remote(mine, mine, send[3 * a + j], recv[3 * a + j], (cx, cy, c)).start()
        token[...] = jnp.zeros_like(token)

    res = pl.pallas_call(
        body, name=name, in_specs=[HBM] * n + [pl.BlockSpec(memory_space=pl.ANY)],
        out_specs=[SEM] * (2 * ns) + [HBM] * n + [pl.BlockSpec(memory_space=pltpu.VMEM)],
        out_shape=[pltpu.SemaphoreType.DMA(())] * (2 * ns)
        + [pltpu.HBM(s.shape, s.dtype) for s in placed] + [jax.ShapeDtypeStruct((8, 128), F32)],
        input_output_aliases={a: 2 * ns + a for a in range(n)},
        compiler_params=pltpu.CompilerParams(has_side_effects=EFFECT),
    )(*[_hbm(s) for s in placed], after)
    return list(res[:ns]), list(res[ns:2 * ns]), list(res[2 * ns:2 * ns + n]), res[2 * ns + n]


def _gather_wait(name, send, recv, bufs, after):
    n = len(bufs)

    ns = 3 * n

    def body(*refs):
        buf = refs[:n]
        send_ref, recv_ref = refs[n:n + ns], refs[n + ns:n + 2 * ns]
        x, y, c, chips = _place()
        ids = [2 * cx + cy for cx, cy in chips]
        for a in range(n):
            half = buf[a].shape[1] // 2
            for j in range(3):
                landed = buf[a].at[ids[j], pl.ds(c * half, half)]
                cp = _remote(landed, landed, send_ref[3 * a + j], recv_ref[3 * a + j], (x, y, c))
                cp.wait_send()
                cp.wait_recv()

    res = pl.pallas_call(
        body, name=name, in_specs=[HBM] * n + [SEM] * (2 * ns) + [pl.BlockSpec(memory_space=pl.ANY)], out_specs=[HBM] * n,
        out_shape=[pltpu.HBM(s.shape, s.dtype) for s in bufs], input_output_aliases={a: a for a in range(n)},
        compiler_params=pltpu.CompilerParams(has_side_effects=EFFECT),
    )(*bufs, *send, *recv, after)
    return list(res)


def _gather_forward(name, bufs):
    n = len(bufs)

    def body(*refs):
        buf = refs[n:2 * n]
        send, recv = refs[2 * n:]
        x, y, c, chips = _place()
        ids = [2 * cx + cy for cx, cy in chips]
        copies = []
        for a in range(n):
            half = buf[a].shape[1] // 2
            for j in range(3):
                landed = buf[a].at[ids[j], pl.ds(c * half, half)]
                cp = _remote(landed, landed, send.at[a, j], recv.at[a, j], (x, y, 1 - c))
                cp.start()
                copies.append(cp)
        for a in range(n):
            half = buf[a].shape[1] // 2
            for j in range(3):
                landed = buf[a].at[ids[j], pl.ds((1 - c) * half, half)]
                _remote(landed, landed, send.at[a, j], recv.at[a, j], (x, y, c)).wait_recv()
        for cp in copies:
            cp.wait_send()

    return pl.pallas_call(
        body, name=name, in_specs=[HBM] * n, out_specs=[HBM] * n,
        out_shape=[jax.ShapeDtypeStruct(s.shape, s.dtype) for s in bufs], input_output_aliases={a: a for a in range(n)},
        scratch_shapes=[pltpu.SemaphoreType.DMA((n, 3)), pltpu.SemaphoreType.DMA((n, 3))],
    )(*bufs)


def _pair_exchange(name, grads):
    n = len(grads)

    def body(*refs):
        src, theirs = refs[:n], refs[n:2 * n]
        send, recv = refs[2 * n:]
        x, y, c, chips = _place()
        order = [2 * x + y] + [2 * cx + cy for cx, cy in chips]
        copies = []
        for a in range(n):
            half = src[a].shape[1] // 2
            for j in range(N_CHIPS):
                cp = _remote(src[a].at[order[j], pl.ds((1 - c) * half, half)], theirs[a].at[j], send.at[a, j], recv.at[a, j], (x, y, 1 - c))
                cp.start()
                copies.append(cp)
        for cp in copies:
            cp.wait()

    return pl.pallas_call(
        body, name=name, in_specs=[HBM] * n, out_specs=[HBM] * n,
        out_shape=[jax.ShapeDtypeStruct((N_CHIPS, g.shape[1] // 2, g.shape[2]), g.dtype) for g in grads],
        scratch_shapes=[pltpu.SemaphoreType.DMA((n, N_CHIPS)), pltpu.SemaphoreType.DMA((n, N_CHIPS))],
    )(*grads)


def _chip_start(name, parts):
    n = len(parts)
    ns = 3 * n

    def body(*refs):
        send, recv = refs[2 * n:2 * n + ns], refs[2 * n + ns:2 * n + 2 * ns]
        src = refs[2 * n + 2 * ns:3 * n + 2 * ns]
        land = refs[3 * n + 2 * ns:4 * n + 2 * ns]
        token = refs[4 * n + 2 * ns]
        x, y, c, chips = _place()
        for a in range(n):
            for j, (cx, cy) in enumerate(chips):
                _remote(src[a].at[j], land[a].at[j], send[3 * a + j], recv[3 * a + j], (cx, cy, c)).start()
        token[...] = jnp.zeros_like(token)

    res = pl.pallas_call(
        body, name=name, in_specs=[HBM] * (2 * n),
        out_specs=[SEM] * (2 * ns) + [HBM] * (2 * n) + [pl.BlockSpec(memory_space=pltpu.VMEM)],
        out_shape=[pltpu.SemaphoreType.DMA(())] * (2 * ns) + [pltpu.HBM(p.shape, p.dtype) for p in parts] * 2
        + [jax.ShapeDtypeStruct((8, 128), F32)],
        input_output_aliases={a: 2 * ns + a for a in range(2 * n)},
        compiler_params=pltpu.CompilerParams(has_side_effects=EFFECT),
    )(*[_hbm(p) for p in parts], *[_hbm(lax.empty(p.shape, p.dtype)) for p in parts])
    return list(res[:ns]), list(res[ns:2 * ns]), list(res[2 * ns:2 * ns + n]), list(res[2 * ns + n:2 * ns + 2 * n]), res[2 * ns + 2 * n]


def _chip_wait(name, send, recv, parts, lands, after):
    n = len(parts)
    ns = 3 * n

    def body(*refs):
        src, land = refs[:n], refs[n:2 * n]
        send_ref, recv_ref = refs[2 * n:2 * n + ns], refs[2 * n + ns:2 * n + 2 * ns]
        x, y, c, _ = _place()
        for a in range(n):
            for j in range(3):
                cp = _remote(src[a].at[j], land[a].at[j], send_ref[3 * a + j], recv_ref[3 * a + j], (x, y, c))
                cp.wait_send()
                cp.wait_recv()

    res = pl.pallas_call(
        body, name=name, in_specs=[HBM] * (2 * n) + [SEM] * (2 * ns) + [pl.BlockSpec(memory_space=pl.ANY)],
        out_specs=[HBM] * (2 * n), out_shape=[pltpu.HBM(p.shape, p.dtype) for p in parts] * 2,
        input_output_aliases={a: a for a in range(2 * n)},
        compiler_params=pltpu.CompilerParams(has_side_effects=EFFECT),
    )(*parts, *lands, *send, *recv, after)
    return list(res[n:])


def _pair_share(name, shards):
    n = len(shards)

    def body(*refs):
        buf = refs[n:2 * n]
        send, recv = refs[2 * n:]
        x, y, c, _ = _place()
        copies = []
        for a in range(n):
            half = buf[a].shape[0] // 2
            mine = buf[a].at[pl.ds(c * half, half)]
            cp = _remote(mine, mine, send.at[a], recv.at[a], (x, y, 1 - c))
            cp.start()
            copies.append(cp)
        for a, cp in enumerate(copies):
            half = buf[a].shape[0] // 2
            cp.wait_send()
            theirs = buf[a].at[pl.ds((1 - c) * half, half)]
            _remote(theirs, theirs, send.at[a], recv.at[a], (x, y, c)).wait_recv()

    return pl.pallas_call(
        body, name=name, in_specs=[HBM] * n, out_specs=[HBM] * n,
        out_shape=[jax.ShapeDtypeStruct(s.shape, s.dtype) for s in shards], input_output_aliases={a: a for a in range(n)},
        scratch_shapes=[pltpu.SemaphoreType.DMA((n,)), pltpu.SemaphoreType.DMA((n,))],
    )(*shards)


def _all_reduce_small(buf):
    R, W = buf.shape

    def body(src_ref, out_ref, slots, send, recv):
        x, y, c, _ = _place()
        me = 4 * x + 2 * y + c
        copies = []
        for dx in range(2):
            for dy in range(2):
                for dc in range(2):
                    if dx == dy == dc == 0:
                        continue
                    k = 4 * dx + 2 * dy + dc
                    peer = (x ^ dx, y ^ dy, c ^ dc)
                    cp = _remote(src_ref, slots.at[me], send.at[k], recv.at[k], peer)
                    cp.start()
                    copies.append((cp, k))
        slots[me] = src_ref[...]
        for cp, k in copies:
            cp.wait_send()
            landed = slots.at[me ^ k]
            _remote(landed, landed, send.at[k], recv.at[k], (x, y, c)).wait_recv()
        total = slots[0]
        for d in range(1, N_DEV):
            total = total + slots[d]
        out_ref[...] = total

    return pl.pallas_call(
        body, name="all_reduce_small", in_specs=[pl.BlockSpec(memory_space=pltpu.VMEM)],
        out_specs=pl.BlockSpec(memory_space=pltpu.VMEM), out_shape=jax.ShapeDtypeStruct((R, W), F32),
        scratch_shapes=[pltpu.VMEM((N_DEV, R, W), F32), pltpu.SemaphoreType.DMA((N_DEV,)), pltpu.SemaphoreType.DMA((N_DEV,))],
    )(buf)


def _pair_sum_bf16(name, grad, theirs, ids):
    _, R2, C = theirs.shape
    tr = _tile(R2, 256, 16)
    nrb = R2 // tr

    def body(ids_ref, a_ref, b_ref, o_ref):
        o_ref[...] = (a_ref[...] + b_ref[...]).astype(BF16)

    return pl.pallas_call(
        body, name=name,
        grid_spec=pltpu.PrefetchScalarGridSpec(
            num_scalar_prefetch=1, grid=(3, nrb),
            in_specs=[pl.BlockSpec((None, tr, C), lambda j, i, ids: (ids[3 + j], ids[1] * nrb + i, 0)),
                      pl.BlockSpec((None, tr, C), lambda j, i, ids: (j + 1, i, 0))],
            out_specs=pl.BlockSpec((None, tr, C), lambda j, i, ids: (j, i, 0))),
        out_shape=jax.ShapeDtypeStruct((3, R2, C), BF16), compiler_params=_params(("parallel", "parallel")),
    )(ids, grad, theirs)


def _chip_sum(name, grad, theirs, arrived, ids):
    _, R2, C = theirs.shape
    tr = _tile(R2, 256, 16)
    nrb = R2 // tr

    def body(ids_ref, a_ref, b_ref, r_ref, o_ref):
        tot = a_ref[...] + b_ref[...]
        for j in range(3):
            tot = tot + r_ref[j].astype(F32)
        o_ref[...] = tot

    return pl.pallas_call(
        body, name=name,
        grid_spec=pltpu.PrefetchScalarGridSpec(
            num_scalar_prefetch=1, grid=(nrb,),
            in_specs=[pl.BlockSpec((None, tr, C), lambda i, ids: (ids[0], ids[1] * nrb + i, 0)),
                      pl.BlockSpec((None, tr, C), lambda i, ids: (0, i, 0)),
                      pl.BlockSpec((3, tr, C), lambda i, ids: (0, i, 0))],
            out_specs=pl.BlockSpec((tr, C), lambda i, ids: (ids[1] * nrb + i, 0))),
        out_shape=jax.ShapeDtypeStruct((2 * R2, C), F32), compiler_params=_params(("parallel",)),
    )(ids, grad, theirs, arrived)


def _adamw(name, w, g, m, v):
    R, C = w.shape
    tr = _tile(R, 128, 8)
    c1 = 1.0 / (1.0 - ADAM_B1 ** ADAM_STEP)
    c2 = 1.0 / (1.0 - ADAM_B2 ** ADAM_STEP)

    def body(w_ref, g_ref, m_ref, v_ref, d_ref, mo_ref, vo_ref):
        gv = g_ref[...]
        mn = ADAM_B1 * m_ref[...] + (1.0 - ADAM_B1) * gv
        vn = ADAM_B2 * v_ref[...] + (1.0 - ADAM_B2) * (gv * gv)
        d_ref[...] = -ADAM_LR * ((mn * c1) / (jnp.sqrt(vn * c2) + ADAM_EPS) + ADAM_WD * w_ref[...])
        mo_ref[...] = mn
        vo_ref[...] = vn

    spec = pl.BlockSpec((tr, C), lambda i: (i, 0))
    sds = jax.ShapeDtypeStruct((R, C), F32)
    return pl.pallas_call(body, name=name, grid=(R // tr,), in_specs=[spec] * 4, out_specs=[spec] * 3, out_shape=[sds] * 3,
                          compiler_params=_params(("parallel",)))(w, g, m, v)


SMALL = ["ffn1_norm", "mix_norm", "mem_norm", "forget_bias", "fox_q_gain", "fox_k_gain", "swa_q_gain", "swa_k_gain", "swa_sinks",
         "mem_q_gain", "mem_k_gain", "ffn2_norm"]
LARGE = ["ffn1_gate", "ffn1_up", "ffn1_down", "w_in", "w_mem_k", "w_mem_v", "w_out", "ffn2_gate", "ffn2_up", "ffn2_down"]
GATHER_GROUPS = [["ffn1_gate", "ffn1_up"], ["ffn1_down", "w_in", "w_mem_k", "w_mem_v"], ["w_out", "ffn2_gate", "ffn2_up", "ffn2_down"]]
WEIGHTS = ["ffn1_norm", "ffn1_gate", "ffn1_up", "ffn1_down", "mix_norm", "mem_norm", "w_in", "forget_bias", "w_mem_k", "w_mem_v",
           "fox_q_gain", "fox_k_gain", "swa_q_gain", "swa_k_gain", "swa_sinks", "mem_q_gain", "mem_k_gain", "w_out", "ffn2_norm",
           "ffn2_gate", "ffn2_up", "ffn2_down"]


def _pad_proj_cols(w):
    out = jnp.zeros((w.shape[0], PROJ_W), w.dtype)
    for start, width, pstart in REF_GROUPS:
        out = lax.dynamic_update_slice(out, w[:, start:start + width], (0, pstart))
    return out


def _unpad_proj_cols(w):
    return jnp.concatenate([w[:, pstart:pstart + width] for _, width, pstart in REF_GROUPS], axis=1)


def _pack_small(vals):
    flat = jnp.concatenate([vals[k].reshape(-1).astype(F32) for k in SMALL + ["loss"]])
    n = flat.shape[0]
    total = -(-n // 1024) * 1024
    return jnp.pad(flat, (0, total - n)).reshape(total // 128, 128)


def _unpack_small(buf, shapes):
    flat = buf.reshape(-1)
    out, off = {}, 0
    for k in SMALL + ["loss"]:
        size = int(np.prod(shapes[k]))
        out[k] = flat[off:off + size].reshape(shapes[k])
        off += size
    return out


def kernel(x, mem, ffn1_norm, ffn1_gate, ffn1_up, ffn1_down, mix_norm, mem_norm, w_in, forget_bias, w_mem_k, w_mem_v, fox_q_gain, fox_k_gain, swa_q_gain, swa_k_gain, swa_sinks, mem_q_gain, mem_k_gain, w_out, ffn2_norm, ffn2_gate, ffn2_up, ffn2_down, loss_target, m_ffn1_norm, m_ffn1_gate, m_ffn1_up, m_ffn1_down, m_mix_norm, m_mem_norm, m_w_in, m_forget_bias, m_w_mem_k, m_w_mem_v, m_fox_q_gain, m_fox_k_gain, m_swa_q_gain, m_swa_k_gain, m_swa_sinks, m_mem_q_gain, m_mem_k_gain, m_w_out, m_ffn2_norm, m_ffn2_gate, m_ffn2_up, m_ffn2_down, v_ffn1_norm, v_ffn1_gate, v_ffn1_up, v_ffn1_down, v_mix_norm, v_mem_norm, v_w_in, v_forget_bias, v_w_mem_k, v_w_mem_v, v_fox_q_gain, v_fox_k_gain, v_swa_q_gain, v_swa_k_gain, v_swa_sinks, v_mem_q_gain, v_mem_k_gain, v_w_out, v_ffn2_norm, v_ffn2_gate, v_ffn2_up, v_ffn2_down):
    given = dict(locals())
    T, D = x.shape[1], x.shape[2]
    ML = mem.shape[1]
    xin = x.reshape(T, D)
    target = loss_target.reshape(T, D)
    memin = mem.reshape(ML, D)

    ids = _place_ids()
    shard = {k: given[k][0] for k in LARGE}
    shard["w_in"] = _pad_proj_cols(shard["w_in"])
    started, after = [], ids
    for gi, group in enumerate(GATHER_GROUPS):
        placed = [_cast_place("cast_" + k, shard[k], ids) for k in group]
        send, recv, bufs, after = _gather_start("gather_start_%d" % gi, placed, after)
        started.append((send, recv, bufs))

    def arrive(gi, done):
        send, recv, bufs = started[gi]
        bufs = _gather_wait("gather_wait_%d" % gi, send, recv, bufs, done)
        return dict(zip(GATHER_GROUPS[gi], _gather_forward("gather_forward_%d" % gi, bufs)))

    gains = jnp.concatenate([fox_q_gain, fox_k_gain, swa_q_gain, swa_k_gain, mem_q_gain,
                             jnp.pad(forget_bias, ((0, 0), (0, HEAD_DIM - FOX_HEADS))), jnp.zeros((2, HEAD_DIM), F32)], axis=0)
    slopes_np = 2.0 ** (-8.0 * np.arange(1, SWA_HEADS + 1) / SWA_HEADS)
    slopes = jnp.asarray(np.repeat(slopes_np, WINDOW).reshape(SWA_KV_HEADS, GR, 1), F32)
    sinks = jnp.repeat(swa_sinks.reshape(SWA_HEADS), WINDOW).reshape(SWA_KV_HEADS, GR, 1)

    h1 = _rms_fwd("ffn1_norm_fwd", xin, ffn1_norm + after[0, 0])
    full = arrive(0, h1)
    wg1, wu1 = full["ffn1_gate"], full["ffn1_up"]
    g1, u1, a1 = _ffn_gu("ffn1_gate_up", h1, wg1, wu1)
    full = arrive(1, a1)
    wd1 = full["ffn1_down"].reshape(-1, D)
    win = full["w_in"].reshape(D, PROJ_W)
    wmk = full["w_mem_k"].reshape(D, MEM_HEADS * HEAD_DIM)
    wmv = full["w_mem_v"].reshape(D, MEM_HEADS * HEAD_DIM)
    x1 = _ffn_down("ffn1_down", a1, wd1, xin)
    h2 = _rms_fwd("mix_norm_fwd", x1, mix_norm)
    proj = _mm2d("proj_in", h2, win, "nn", F32, tn=1408)
    qkv, logf = _prep_fwd(proj, gains)
    cum = _cumsum_rows("forget_cumsum", logf, False)
    cum_h = cum[:, :FOX_HEADS].T
    cq, ck = cum_h.reshape(FOX_HEADS, T, 1), cum_h.reshape(FOX_HEADS, 1, T)
    mn = _rms_fwd("mem_norm_fwd", memin, mem_norm)
    mk_raw = _mm2d("mem_k_proj", mn, wmk, "nn", F32)
    mv = _mm2d("mem_v_proj", mn, wmv, "nn", BF16)
    mk = _head_norm_rows(mk_raw, mem_k_gain)
    out_a, out_a_f32, lse_a = _fox_fwd(qkv, cq, ck)
    out_b, lse_b = _swa_fwd(qkv, slopes, sinks)
    out_c, lse_c = _mem_fwd(qkv, mk, mv)
    mixed = jnp.concatenate([out_a, out_b, out_c], axis=1)
    full = arrive(2, mixed)
    wo = full["w_out"].reshape(-1, D)
    wg2, wu2, wd2 = full["ffn2_gate"], full["ffn2_up"], full["ffn2_down"].reshape(-1, D)
    x2 = _mm2d("mix_out", mixed, wo, "nn", F32, extras=[x1], epilogue=lambda accs, ex: [ex[0] + accs[0]])
    h3 = _rms_fwd("ffn2_norm_fwd", x2, ffn2_norm)
    g2, u2, a2 = _ffn_gu("ffn2_gate_up", h3, wg2, wu2)
    x3 = _ffn_down("ffn2_down", a2, wd2, x2)
    dx3, dyb3, loss_part = _loss_head(x3, target)

    grads, small, res = {}, {"loss": loss_part[0, 0]}, {}

    def send_off(tag, group):
        theirs = _pair_exchange("grad_pair_exchange_" + tag, [grads[k] for k in group])
        to_chips = [_pair_sum_bf16("pair_sum_" + k, grads[k], b, ids) for k, b in zip(group, theirs)]
        send, recv, parts, lands, token = _chip_start("grad_chip_start_" + tag, to_chips)
        return (group, theirs, send, recv, parts, lands), token

    def finish(tag, state, done):
        group, theirs, send, recv, parts, lands = state
        arrived = _chip_wait("grad_chip_wait_" + tag, send, recv, parts, lands, done)
        halves = [_chip_sum("chip_sum_" + k, grads[k], b, r, ids) for k, b, r in zip(group, theirs, arrived)]
        reduced = dict(zip(group, _pair_share("grad_pair_share_" + tag, halves)))
        last = None
        for k in group:
            gk = _unpad_proj_cols(reduced[k]) if k == "w_in" else reduced[k]
            d, mo, vo = _adamw("adamw_" + k, given[k][0], gk, given["m_" + k][0], given["v_" + k][0])
            res[k] = tuple(t[None] for t in (gk, d, mo, vo))
            last = vo
        return last

    dg2, du2, grads["ffn2_gate"], grads["ffn2_up"], dwd2 = _ffn_bwd_w("ffn2", dyb3, h3, g2, u2, a2, wd2, N_CHIPS)
    grads["ffn2_down"] = dwd2.reshape(N_CHIPS, -1, D)
    state_a, token_a = send_off("a", ["ffn2_gate", "ffn2_up", "ffn2_down"])
    dh3 = _ffn_bwd_x("ffn2", dg2, du2, wg2, wu2, token_a)
    dx2, dx2b, small["ffn2_norm"] = _rms_bwd("ffn2_norm_bwd", dh3, x2, ffn2_norm, dx3, 1.0)
    dmix = _mm2d("mix_out_dx", dx2b, wo, "nt", BF16)
    grads["w_out"] = _mm2d("mix_out_dw", mixed, dx2b, "tn", F32, tk=512).reshape(N_CHIPS, -1, D)
    dfq, dfk, dfv, dck = _fox_bwd(qkv, cq, ck, out_a_f32, lse_a, dmix)
    dsq, dsk, dsv, dsink = _swa_bwd(qkv, slopes, sinks, out_b, lse_b, dmix)
    dmq, dmk, dmv = _mem_bwd(qkv, mk, mv, out_c, lse_c, dmix)
    small["swa_sinks"] = dsink[:, :SWA_GROUP, 0].reshape(1, SWA_HEADS)
    dcum = jnp.pad(dck.reshape(FOX_HEADS, T).T, ((0, 0), (0, HEAD_DIM - FOX_HEADS)))
    dlogf = _cumsum_rows("forget_cumsum_bwd", dcum, True)
    dproj, dgains = _prep_bwd(proj, gains, dfq, dfk, dfv, dsq, dsk, dsv, dmq, dlogf)
    for row, k in enumerate(["fox_q_gain", "fox_k_gain", "swa_q_gain", "swa_k_gain", "mem_q_gain"]):
        small[k] = dgains[row:row + 1, :]
    small["forget_bias"] = dgains[5:6, :FOX_HEADS]
    grads["w_in"] = _mm2d("proj_in_dw", h2, dproj, "tn", F32, tn=1408, tk=512).reshape(N_CHIPS, -1, PROJ_W)
    dmk_raw, small["mem_k_gain"] = _head_norm_rows_bwd(mk_raw, mem_k_gain, dmk)
    dmvb = dmv.astype(BF16)
    grads["w_mem_k"] = _mm2d("mem_k_dw", mn, dmk_raw, "tn", F32).reshape(N_CHIPS, -1, MEM_HEADS * HEAD_DIM)
    grads["w_mem_v"] = _mm2d("mem_v_dw", mn, dmvb, "tn", F32).reshape(N_CHIPS, -1, MEM_HEADS * HEAD_DIM)
    dmn = _mm2d("mem_k_dx", dmk_raw, wmk, "nt", F32)
    dmn = _mm2d("mem_v_dx", dmvb, wmv, "nt", F32, extras=[dmn], epilogue=lambda accs, ex: [ex[0] + accs[0]])
    _, _, small["mem_norm"] = _rms_bwd("mem_norm_bwd", dmn, memin, mem_norm, jnp.zeros_like(memin), 1.0)
    state_b, token_b = send_off("b", ["w_out", "w_in", "w_mem_k", "w_mem_v"])
    dh2 = _mm2d("proj_in_dx", dproj, win, "nt", F32, tk=1408, after=token_b)
    dx1, dyb1, small["mix_norm"] = _rms_bwd("mix_norm_bwd", dh2, x1, mix_norm, dx2, 0.5)
    dg1, du1, grads["ffn1_gate"], grads["ffn1_up"], dwd1 = _ffn_bwd_w("ffn1", dyb1, h1, g1, u1, a1, wd1, N_CHIPS)
    grads["ffn1_down"] = dwd1.reshape(N_CHIPS, -1, D)
    state_c, token_c = send_off("c", ["ffn1_gate", "ffn1_up", "ffn1_down"])
    dh1 = _ffn_bwd_x("ffn1", dg1, du1, wg1, wu1, token_c)
    grad_x, _, small["ffn1_norm"] = _rms_bwd("ffn1_norm_bwd", dh1, xin, ffn1_norm, dx1, 1.0)

    shapes = {k: given[k].shape for k in SMALL}
    shapes["loss"] = ()
    red_small = _unpack_small(_all_reduce_small(_pack_small(small)), shapes)
    loss = red_small["loss"]
    zero = {"loss": jnp.zeros((), F32)}
    packed = [_pack_small({**zero, **{k: src[k] for k in SMALL}}) for src in (
        {k: given[k] for k in SMALL}, red_small, {k: given["m_" + k] for k in SMALL}, {k: given["v_" + k] for k in SMALL})]
    small_out = _adamw("adamw_small", *packed)
    d_s, m_s, v_s = (_unpack_small(t, shapes) for t in small_out)
    for k in SMALL:
        res[k] = (red_small[k], d_s[k], m_s[k], v_s[k])

    done = finish("a", state_a, small_out[0])
    done = finish("b", state_b, done)
    finish("c", state_c, done)

    outs = [loss, grad_x.reshape(1, T, D)]
    for part in range(4):
        outs += [res[k][part] for k in WEIGHTS]
    return tuple(outs)
```

```python
import functools

import numpy as np
import jax
import jax.numpy as jnp
from jax import lax
from jax.experimental import pallas as pl
from jax.experimental.pallas import tpu as pltpu

F32 = jnp.float32
BF16 = jnp.bfloat16
MESH = pl.DeviceIdType.MESH

HEAD_DIM = 128
FOX_HEADS = 6
SWA_HEADS = 6
SWA_KV_HEADS = 2
SWA_GROUP = SWA_HEADS // SWA_KV_HEADS
MEM_HEADS = 4
WINDOW = 128
EPS = 1e-6
NEG_INF = -1e30
SCALE = HEAD_DIM ** -0.5

C_FQ = 0
C_FK = C_FQ + FOX_HEADS * HEAD_DIM
C_FV = C_FK + FOX_HEADS * HEAD_DIM
C_SQ = C_FV + FOX_HEADS * HEAD_DIM
C_SK = C_SQ + SWA_HEADS * HEAD_DIM
C_SV = C_SK + SWA_KV_HEADS * HEAD_DIM
C_MQ = C_SV + SWA_KV_HEADS * HEAD_DIM
C_FL = C_MQ + MEM_HEADS * HEAD_DIM
PROJ_W = C_FL + HEAD_DIM
FOX_W = FOX_HEADS * HEAD_DIM
REF_GROUPS = [
    (0, FOX_W, C_FQ), (FOX_W, FOX_W, C_FK), (2 * FOX_W, FOX_W, C_FV), (3 * FOX_W, FOX_HEADS, C_FL),
    (3 * FOX_W + FOX_HEADS, SWA_HEADS * HEAD_DIM, C_SQ),
    (3 * FOX_W + FOX_HEADS + SWA_HEADS * HEAD_DIM, SWA_KV_HEADS * HEAD_DIM, C_SK),
    (3 * FOX_W + FOX_HEADS + (SWA_HEADS + SWA_KV_HEADS) * HEAD_DIM, SWA_KV_HEADS * HEAD_DIM, C_SV),
    (3 * FOX_W + FOX_HEADS + (SWA_HEADS + 2 * SWA_KV_HEADS) * HEAD_DIM, MEM_HEADS * HEAD_DIM, C_MQ),
]

ADAM_LR = 0.001
ADAM_B1 = 0.9
ADAM_B2 = 0.999
ADAM_EPS = 1e-08
ADAM_WD = 0.01
ADAM_STEP = 10

V7X_VMEM_LIMIT = 56 * 1024 * 1024
N_CHIPS = 4
N_DEV = 8


def _tile(n, pref, mult=128):
    t = (min(pref, n) // mult) * mult
    while t >= mult:
        if n % t == 0:
            return t
        t -= mult
    return n


def _params(sem):
    return pltpu.CompilerParams(dimension_semantics=sem, vmem_limit_bytes=V7X_VMEM_LIMIT)


_DIMS = {"nn": (((1,), (0,)), ((), ())), "nt": (((1,), (1,)), ((), ())), "tn": (((0,), (0,)), ((), ()))}


def _dot(a, b, mode):
    return lax.dot_general(a, b, _DIMS[mode], preferred_element_type=F32)


def _mm(name, grid, pairs, acc_of, acc_shapes, extras, outs, epilogue, after=None):
    n_p, n_e, n_o, n_a = len(pairs), len(extras), len(outs), len(acc_shapes)
    n_w = 0 if after is None else 1
    nk = grid[2]
    n_in = sum(1 if a is None else 2 for a, *_ in pairs)

    def body(*refs):
        ex = refs[n_in:n_in + n_e]
        out = refs[n_in + n_e + n_w:n_in + n_e + n_w + n_o]
        accs = refs[n_in + n_e + n_w + n_o:]
        parts = [None] * n_a
        at = 0
        for p in range(n_p):
            if pairs[p][0] is None:
                a_ref, b_ref = refs[0], refs[at]
                at += 1
            else:
                a_ref, b_ref = refs[at], refs[at + 1]
                at += 2
            d = _dot(a_ref[...], b_ref[...], pairs[p][4])
            parts[acc_of[p]] = d if parts[acc_of[p]] is None else parts[acc_of[p]] + d

        def finish(vals):
            for o, r in zip(out, epilogue(vals, [e[...] for e in ex])):
                o[...] = r.astype(o.dtype)

        if nk == 1:
            finish(parts)
            return
        k = pl.program_id(2)

        @pl.when(k == 0)
        def _():
            for a, d in zip(accs, parts):
                a[...] = d

        @pl.when((k > 0) & (k < nk - 1))
        def _():
            for a, d in zip(accs, parts):
                a[...] += d

        @pl.when(k == nk - 1)
        def _():
            finish([a[...] + d for a, d in zip(accs, parts)])

    in_specs, args = [], []
    for a, a_spec, b, b_spec, _ in pairs:
        if a is not None:
            in_specs.append(a_spec)
            args.append(a)
        in_specs.append(b_spec)
        args.append(b)
    for e, e_spec in extras:
        in_specs.append(e_spec)
        args.append(e)
    if after is not None:
        in_specs.append(pl.BlockSpec(memory_space=pl.ANY))
        args.append(after)
    res = pl.pallas_call(
        body, name=name, grid=grid, in_specs=in_specs,
        out_specs=[s for _, s in outs], out_shape=[o for o, _ in outs],
        scratch_shapes=[pltpu.VMEM(s, F32) for s in acc_shapes] if nk > 1 else [],
        compiler_params=_params(("parallel", "parallel", "arbitrary")),
    )(*args)
    return res


def _mm2d(name, a, b, mode, out_dtype, tm=512, tn=1024, tk=1024, extras=(), epilogue=None, n_out=1, after=None, n_outer=False):
    if mode == "nn":
        (M, K), N = a.shape, b.shape[1]
    elif mode == "nt":
        (M, K), N = a.shape, b.shape[0]
    else:
        (K, M), N = a.shape, b.shape[1]
    tm, tn, tk = _tile(M, tm), _tile(N, tn), _tile(K, tk)

    def spec(shape, index):
        if n_outer:
            return pl.BlockSpec(shape, lambda j, i, k: index(i, j, k))
        return pl.BlockSpec(shape, index)

    a_spec = spec((tk, tm), lambda i, j, k: (k, i)) if mode == "tn" else spec((tm, tk), lambda i, j, k: (i, k))
    b_spec = spec((tn, tk), lambda i, j, k: (j, k)) if mode == "nt" else spec((tk, tn), lambda i, j, k: (k, j))
    mn = spec((tm, tn), lambda i, j, k: (i, j))
    if epilogue is None:
        epilogue = lambda accs, ex: [accs[0]]
    if not isinstance(out_dtype, (list, tuple)):
        out_dtype = [out_dtype] * n_out
    grid = (N // tn, M // tm, K // tk) if n_outer else (M // tm, N // tn, K // tk)
    res = _mm(name, grid, [(a, a_spec, b, b_spec, mode)], [0], [(tm, tn)],
              [(e, mn) for e in extras], [(jax.ShapeDtypeStruct((M, N), d), mn) for d in out_dtype], epilogue, after=after)
    return res[0] if len(res) == 1 else res


def _sigmoid(x):
    return 1.0 / (1.0 + jnp.exp(-x))


def _sigmoid_fast(x):
    return pl.reciprocal(1.0 + jnp.exp(-x), approx=True)


def _ffn_gu(name, h, wg, wu):
    T, D = h.shape
    nf, _, F4 = wg.shape
    tm, tk = _tile(T, 512), _tile(D, 2048)
    a_spec = pl.BlockSpec((tm, tk), lambda j, i, k: (i, k))
    b_spec = pl.BlockSpec((None, tk, F4), lambda j, i, k: (j, k, 0))
    o_spec = pl.BlockSpec((tm, F4), lambda j, i, k: (i, j))

    def epilogue(accs, ex):
        g, u = accs
        return [g, u, g * _sigmoid_fast(g) * u]

    sds = jax.ShapeDtypeStruct((T, nf * F4), BF16)
    return _mm(name, (nf, T // tm, D // tk), [(h, a_spec, wg, b_spec, "nn"), (None, None, wu, b_spec, "nn")], [0, 1],
               [(tm, F4), (tm, F4)], [], [(sds, o_spec)] * 3, epilogue)


def _ffn_down(name, a, wd, xres):
    return _mm2d(name, a, wd, "nn", F32, tm=1024, tk=wd.shape[0] // N_CHIPS, extras=[xres],
                 epilogue=lambda accs, ex: [ex[0] + 0.5 * accs[0]])


def _ffn_bwd_w(tag, dyb, h, g, u, a, wd, nf):
    T, D = h.shape
    F4 = wd.shape[0] // nf

    def act_bwd(accs, ex):
        gf, uf = ex[0].astype(F32), ex[1].astype(F32)
        s = _sigmoid_fast(gf)
        return [accs[0] * uf * s * (1.0 + gf * (1.0 - s)), accs[0] * gf * s]

    dg, du = _mm2d(tag + "_da", dyb, wd, "nt", BF16, tn=F4, tk=2048, extras=[g, u], epilogue=act_bwd, n_out=2, n_outer=True)
    dwd = _mm2d(tag + "_dwd", a, dyb, "tn", F32, tm=F4, tk=1024)

    tm, tk = _tile(D, 512), _tile(T, 1024)
    h_spec = pl.BlockSpec((tk, tm), lambda i, j, k: (k, i))
    d_spec = pl.BlockSpec((tk, F4), lambda i, j, k: (k, j))
    w_spec = pl.BlockSpec((None, tm, F4), lambda i, j, k: (j, i, 0))
    sds = jax.ShapeDtypeStruct((nf, D, F4), F32)
    dwg, dwu = _mm(tag + "_dwgu", (D // tm, nf, T // tk), [(h, h_spec, dg, d_spec, "tn"), (None, None, du, d_spec, "tn")],
                   [0, 1], [(tm, F4), (tm, F4)], [], [(sds, w_spec)] * 2, lambda accs, ex: accs)
    return dg, du, dwg, dwu, dwd


def _ffn_bwd_x(tag, dg, du, wg, wu, after):
    T = dg.shape[0]
    nf, D, F4 = wg.shape
    tm, tn = _tile(T, 512), _tile(D, 1024)
    a_spec = pl.BlockSpec((tm, F4), lambda i, j, k: (i, k))
    b_spec = pl.BlockSpec((None, tn, F4), lambda i, j, k: (k, j, 0))
    o_spec = pl.BlockSpec((tm, tn), lambda i, j, k: (i, j))
    (dh,) = _mm(tag + "_dh", (T // tm, D // tn, nf), [(dg, a_spec, wg, b_spec, "nt"), (du, a_spec, wu, b_spec, "nt")],
                [0, 0], [(tm, tn)], [], [(jax.ShapeDtypeStruct((T, D), F32), o_spec)], lambda accs, ex: accs, after=after)
    return dh


def _rms_fwd(name, x, gain):
    R, D = x.shape
    tr = _tile(R, 256, 8)

    def body(x_ref, g_ref, o_ref):
        xv = x_ref[...]
        r = lax.rsqrt(jnp.mean(xv * xv, axis=-1, keepdims=True) + EPS)
        o_ref[...] = (xv * r * g_ref[...]).astype(BF16)

    return pl.pallas_call(
        body, name=name, grid=(R // tr,),
        in_specs=[pl.BlockSpec((tr, D), lambda i: (i, 0)), pl.BlockSpec((1, D), lambda i: (0, 0))],
        out_specs=pl.BlockSpec((tr, D), lambda i: (i, 0)), out_shape=jax.ShapeDtypeStruct((R, D), BF16),
        compiler_params=_params(("parallel",)),
    )(x, gain)


def _rms_bwd(name, dh, x, gain, dres, bscale):
    R, D = x.shape
    tr = _tile(R, 256, 8)

    def body(dh_ref, x_ref, g_ref, dres_ref, dx_ref, dxb_ref, dg_ref):
        xv, dy = x_ref[...], dh_ref[...]
        r = lax.rsqrt(jnp.mean(xv * xv, axis=-1, keepdims=True) + EPS)
        xn = xv * r
        uu = dy * g_ref[...]
        dx = dres_ref[...] + r * (uu - xn * jnp.mean(xn * uu, axis=-1, keepdims=True))
        dx_ref[...] = dx
        dxb_ref[...] = (bscale * dx).astype(BF16)
        part = jnp.sum(dy * xn, axis=0, keepdims=True)

        @pl.when(pl.program_id(0) == 0)
        def _():
            dg_ref[...] = part

        @pl.when(pl.program_id(0) > 0)
        def _():
            dg_ref[...] += part

    row = pl.BlockSpec((tr, D), lambda i: (i, 0))
    vec = pl.BlockSpec((1, D), lambda i: (0, 0))
    return pl.pallas_call(
        body, name=name, grid=(R // tr,), in_specs=[row, row, vec, row], out_specs=[row, row, vec],
        out_shape=[jax.ShapeDtypeStruct((R, D), F32), jax.ShapeDtypeStruct((R, D), BF16), jax.ShapeDtypeStruct((1, D), F32)],
        compiler_params=_params(("arbitrary",)),
    )(dh, x, gain, dres)


def _loss_head(y, target):
    R, D = y.shape
    tr = _tile(R, 256, 8)

    def body(y_ref, t_ref, d_ref, db_ref, l_ref):
        e = y_ref[...] - t_ref[...]
        d = e * (1.0 / D)
        d_ref[...] = d
        db_ref[...] = (0.5 * d).astype(BF16)
        part = jnp.zeros((8, 128), F32) + (0.5 / D) * jnp.sum(e * e)

        @pl.when(pl.program_id(0) == 0)
        def _():
            l_ref[...] = part

        @pl.when(pl.program_id(0) > 0)
        def _():
            l_ref[...] += part

    row = pl.BlockSpec((tr, D), lambda i: (i, 0))
    acc = pl.BlockSpec((8, 128), lambda i: (0, 0))
    return pl.pallas_call(
        body, name="loss_head", grid=(R // tr,), in_specs=[row, row], out_specs=[row, row, acc],
        out_shape=[jax.ShapeDtypeStruct((R, D), F32), jax.ShapeDtypeStruct((R, D), BF16), jax.ShapeDtypeStruct((8, 128), F32)],
        compiler_params=_params(("arbitrary",)),
    )(y, target)


def _head_norm(xs, g):
    r = lax.rsqrt(jnp.mean(xs * xs, axis=-1, keepdims=True) + EPS)
    return xs * r * g


def _head_norm_bwd(xs, g, dy):
    r = lax.rsqrt(jnp.mean(xs * xs, axis=-1, keepdims=True) + EPS)
    xn = xs * r
    uu = dy * g
    return r * (uu - xn * jnp.mean(xn * uu, axis=-1, keepdims=True)), jnp.sum(dy * xn, axis=0, keepdims=True)


NORMED = [(C_FQ, FOX_HEADS, 0), (C_FK, FOX_HEADS, 1), (C_SQ, SWA_HEADS, 2), (C_SK, SWA_KV_HEADS, 3), (C_MQ, MEM_HEADS, 4)]
PLAIN = [(C_FV, FOX_HEADS), (C_SV, SWA_KV_HEADS)]


def _prep_fwd(proj, gains):
    T = proj.shape[0]
    tr = _tile(T, 256, 8)

    def body(p_ref, g_ref, o_ref, lf_ref):
        for start, heads, row in NORMED:
            gn = g_ref[row:row + 1, :]
            for hh in range(heads):
                sl = slice(start + hh * HEAD_DIM, start + (hh + 1) * HEAD_DIM)
                o_ref[:, sl] = _head_norm(p_ref[:, sl], gn).astype(BF16)
        for start, heads in PLAIN:
            sl = slice(start, start + heads * HEAD_DIM)
            o_ref[:, sl] = p_ref[:, sl].astype(BF16)
        zb = p_ref[:, C_FL:C_FL + HEAD_DIM] + g_ref[5:6, :]
        o_ref[:, C_FL:C_FL + HEAD_DIM] = jnp.zeros((tr, HEAD_DIM), BF16)
        lf_ref[...] = jnp.minimum(zb, 0.0) - jnp.log(1.0 + jnp.exp(-jnp.abs(zb)))

    return pl.pallas_call(
        body, name="prep_fwd", grid=(T // tr,),
        in_specs=[pl.BlockSpec((tr, PROJ_W), lambda i: (i, 0)), pl.BlockSpec((8, 128), lambda i: (0, 0))],
        out_specs=[pl.BlockSpec((tr, PROJ_W), lambda i: (i, 0)), pl.BlockSpec((tr, HEAD_DIM), lambda i: (i, 0))],
        out_shape=[jax.ShapeDtypeStruct((T, PROJ_W), BF16), jax.ShapeDtypeStruct((T, HEAD_DIM), F32)],
        compiler_params=_params(("parallel",)),
    )(proj, gains)


def _prep_bwd(proj, gains, dfq, dfk, dfv, dsq, dsk, dsv, dmq, dlogf):
    T = proj.shape[0]
    tr = _tile(T, 256, 8)
    d_normed = {C_FQ: 0, C_FK: 1, C_SQ: 3, C_SK: 4, C_MQ: 6}
    d_plain = {C_FV: 2, C_SV: 5}

    def body(p_ref, g_ref, *rest):
        d_refs, dlf_ref, o_ref, dg_ref = rest[:7], rest[7], rest[8], rest[9]
        rows = []
        for start, heads, row in NORMED:
            gn = g_ref[row:row + 1, :]
            d_ref = d_refs[d_normed[start]]
            tot = jnp.zeros((1, HEAD_DIM), F32)
            for hh in range(heads):
                sl = slice(start + hh * HEAD_DIM, start + (hh + 1) * HEAD_DIM)
                dx, dgn = _head_norm_bwd(p_ref[:, sl], gn, d_ref[:, hh * HEAD_DIM:(hh + 1) * HEAD_DIM])
                o_ref[:, sl] = dx.astype(BF16)
                tot = tot + dgn
            rows.append(tot)
        for start, heads in PLAIN:
            o_ref[:, start:start + heads * HEAD_DIM] = d_refs[d_plain[start]][...].astype(BF16)
        zb = p_ref[:, C_FL:C_FL + HEAD_DIM] + g_ref[5:6, :]
        lane = lax.broadcasted_iota(jnp.int32, (tr, HEAD_DIM), 1)
        dz = jnp.where(lane < FOX_HEADS, dlf_ref[...] * (1.0 - _sigmoid(zb)), 0.0)
        o_ref[:, C_FL:C_FL + HEAD_DIM] = dz.astype(BF16)
        rows.append(jnp.sum(dz, axis=0, keepdims=True))
        part = jnp.concatenate(rows + [jnp.zeros((2, HEAD_DIM), F32)], axis=0)

        @pl.when(pl.program_id(0) == 0)
        def _():
            dg_ref[...] = part

        @pl.when(pl.program_id(0) > 0)
        def _():
            dg_ref[...] += part

    def rows_of(w):
        return pl.BlockSpec((tr, w), lambda i: (i, 0))

    small = pl.BlockSpec((8, 128), lambda i: (0, 0))
    ds = [dfq, dfk, dfv, dsq, dsk, dsv, dmq]
    return pl.pallas_call(
        body, name="prep_bwd", grid=(T // tr,),
        in_specs=[rows_of(PROJ_W), small] + [rows_of(d.shape[1]) for d in ds] + [rows_of(HEAD_DIM)],
        out_specs=[rows_of(PROJ_W), small],
        out_shape=[jax.ShapeDtypeStruct((T, PROJ_W), BF16), jax.ShapeDtypeStruct((8, 128), F32)],
        compiler_params=_params(("arbitrary",)),
    )(proj, gains, *ds, dlogf)


def _head_norm_rows(x, gain):
    R, W = x.shape

    def body(x_ref, g_ref, o_ref):
        for hh in range(W // HEAD_DIM):
            sl = slice(hh * HEAD_DIM, (hh + 1) * HEAD_DIM)
            o_ref[:, sl] = _head_norm(x_ref[:, sl], g_ref[...]).astype(BF16)

    return pl.pallas_call(body, name="mem_k_norm", out_shape=jax.ShapeDtypeStruct((R, W), BF16))(x, gain)


def _head_norm_rows_bwd(x, gain, dy):
    R, W = x.shape

    def body(x_ref, g_ref, dy_ref, dx_ref, dg_ref):
        tot = jnp.zeros((1, HEAD_DIM), F32)
        for hh in range(W // HEAD_DIM):
            sl = slice(hh * HEAD_DIM, (hh + 1) * HEAD_DIM)
            dx, dgn = _head_norm_bwd(x_ref[:, sl], g_ref[...], dy_ref[:, sl])
            dx_ref[:, sl] = dx.astype(BF16)
            tot = tot + dgn
        dg_ref[...] = tot

    return pl.pallas_call(
        body, name="mem_k_norm_bwd",
        out_shape=[jax.ShapeDtypeStruct((R, W), BF16), jax.ShapeDtypeStruct((1, HEAD_DIM), F32)])(x, gain, dy)


def _cumsum_rows(name, x, reverse):
    T, W = x.shape
    tb = _tile(T, 512, 8)
    nb = T // tb

    def body(x_ref, o_ref, carry):
        @pl.when(pl.program_id(0) == 0)
        def _():
            carry[...] = jnp.zeros_like(carry)

        xv = x_ref[...]
        r = lax.broadcasted_iota(jnp.int32, (tb, tb), 0)
        cc = lax.broadcasted_iota(jnp.int32, (tb, tb), 1)
        tri = jnp.where((cc >= r) if reverse else (cc <= r), 1.0, 0.0).astype(F32)
        o_ref[...] = jnp.dot(tri, xv, precision=lax.Precision.HIGHEST, preferred_element_type=F32) + carry[...]
        carry[...] += jnp.sum(xv, axis=0, keepdims=True)

    idx = (lambda i: (nb - 1 - i, 0)) if reverse else (lambda i: (i, 0))
    return pl.pallas_call(
        body, name=name, grid=(nb,), in_specs=[pl.BlockSpec((tb, W), idx)], out_specs=pl.BlockSpec((tb, W), idx),
        out_shape=jax.ShapeDtypeStruct((T, W), F32), scratch_shapes=[pltpu.VMEM((1, W), F32)],
        compiler_params=_params(("arbitrary",)),
    )(x)


def _triangle(nq, by_column):
    if by_column:
        blocks = [(i, j) for j in range(nq) for i in range(j, nq)]
    else:
        blocks = [(i, j) for i in range(nq) for j in range(i + 1)]
    return jnp.asarray(np.array(blocks, np.int32).T)


def _fox_scores(q, k, cq, ck, on_diagonal):
    s = _dot(q, k, "nt") * SCALE + (cq - ck)
    if on_diagonal:
        n = s.shape[0]
        s = jnp.where(lax.broadcasted_iota(jnp.int32, (n, n), 1) <= lax.broadcasted_iota(jnp.int32, (n, n), 0), s, NEG_INF)
    return s


def _fox_fwd(qkv, cq, ck):
    T = qkv.shape[0]
    tq = _tile(T, 512)
    nq = T // tq
    steps = nq * (nq + 1) // 2
    HQ, HK, HV = C_FQ // HEAD_DIM, C_FK // HEAD_DIM, C_FV // HEAD_DIM

    def body(tab, q_ref, k_ref, v_ref, cq_ref, ck_ref, o_ref, of_ref, lse_ref, m_sc, l_sc, acc_sc):
        i, j = tab[0, pl.program_id(1)], tab[1, pl.program_id(1)]

        @pl.when(j == 0)
        def _():
            m_sc[...] = jnp.full_like(m_sc, NEG_INF)
            l_sc[...] = jnp.zeros_like(l_sc)
            acc_sc[...] = jnp.zeros_like(acc_sc)

        def step(on_diagonal):
            s = _fox_scores(q_ref[...], k_ref[...], cq_ref[...], ck_ref[...], on_diagonal)
            m_new = jnp.maximum(m_sc[...], jnp.max(s, axis=-1, keepdims=True))
            alpha = jnp.exp(m_sc[...] - m_new)
            p = jnp.exp(s - m_new)
            l_sc[...] = alpha * l_sc[...] + jnp.sum(p, axis=-1, keepdims=True)
            acc_sc[...] = alpha * acc_sc[...] + _dot(p.astype(BF16), v_ref[...], "nn")
            m_sc[...] = m_new

        @pl.when(j < i)
        def _():
            step(False)

        @pl.when(j == i)
        def _():
            step(True)
            o = acc_sc[...] / l_sc[...]
            o_ref[...] = o.astype(BF16)
            of_ref[...] = o
            lse_ref[...] = m_sc[...] + jnp.log(l_sc[...])

    def rows(base):
        return pl.BlockSpec((tq, HEAD_DIM), lambda h, s, tab: (tab[0, s], base + h))

    def cols(base):
        return pl.BlockSpec((tq, HEAD_DIM), lambda h, s, tab: (tab[1, s], base + h))

    rvec = pl.BlockSpec((None, tq, 1), lambda h, s, tab: (h, tab[0, s], 0))
    return pl.pallas_call(
        body, name="fox_fwd",
        grid_spec=pltpu.PrefetchScalarGridSpec(
            num_scalar_prefetch=1, grid=(FOX_HEADS, steps),
            in_specs=[rows(HQ), cols(HK), cols(HV), rvec, pl.BlockSpec((None, 1, tq), lambda h, s, tab: (h, 0, tab[1, s]))],
            out_specs=[rows(0), rows(0), rvec],
            scratch_shapes=[pltpu.VMEM((tq, 1), F32), pltpu.VMEM((tq, 1), F32), pltpu.VMEM((tq, HEAD_DIM), F32)]),
        out_shape=[jax.ShapeDtypeStruct((T, FOX_W), BF16), jax.ShapeDtypeStruct((T, FOX_W), F32),
                   jax.ShapeDtypeStruct((FOX_HEADS, T, 1), F32)],
        compiler_params=_params(("parallel", "arbitrary")),
    )(_triangle(nq, False), qkv, qkv, qkv, cq, ck)


def _fox_bwd(qkv, cq, ck, out_f32, lse, dmix):
    T = qkv.shape[0]
    tq = _tile(T, 512)
    nq = T // tq
    steps = nq * (nq + 1) // 2
    HQ, HK, HV = C_FQ // HEAD_DIM, C_FK // HEAD_DIM, C_FV // HEAD_DIM

    def body(tab, q_ref, k_ref, v_ref, cq_ref, ck_ref, o_ref, lse_ref, do_ref, dq_ref, dk_ref, dv_ref, dck_ref, dk_sc, dv_sc, dc_sc):
        qi, kj = tab[0, pl.program_id(1)], tab[1, pl.program_id(1)]

        @pl.when(qi == kj)
        def _():
            dk_sc[...] = jnp.zeros_like(dk_sc)
            dv_sc[...] = jnp.zeros_like(dv_sc)
            dc_sc[...] = jnp.zeros_like(dc_sc)

        def step(on_diagonal):
            q, k, v, do = q_ref[...], k_ref[...], v_ref[...], do_ref[...]
            p = jnp.exp(_fox_scores(q, k, cq_ref[...], ck_ref[...], on_diagonal) - lse_ref[...])
            dp = _dot(do, v, "nt")
            delta = jnp.sum(do.astype(F32) * o_ref[...], axis=-1, keepdims=True)
            ds = p * (dp - delta)
            dsb = ds.astype(BF16)
            dv_sc[...] += _dot(p.astype(BF16), do, "tn")
            dk_sc[...] += _dot(dsb, q, "tn")
            dc_sc[...] += jnp.sum(ds, axis=0, keepdims=True)
            dq_part = _dot(dsb, k, "nn") * SCALE
            rows_q = pl.ds(pl.multiple_of(qi * tq, tq), tq)

            @pl.when(kj == 0)
            def _():
                dq_ref[rows_q, :] = dq_part

            @pl.when(kj > 0)
            def _():
                dq_ref[rows_q, :] += dq_part

        @pl.when(qi > kj)
        def _():
            step(False)

        @pl.when(qi == kj)
        def _():
            step(True)

        @pl.when(qi == nq - 1)
        def _():
            dk_ref[...] = dk_sc[...] * SCALE
            dv_ref[...] = dv_sc[...]
            dck_ref[...] = -dc_sc[...]

    def rows(base):
        return pl.BlockSpec((tq, HEAD_DIM), lambda h, s, tab: (tab[0, s], base + h))

    def cols(base):
        return pl.BlockSpec((tq, HEAD_DIM), lambda h, s, tab: (tab[1, s], base + h))

    rvec = pl.BlockSpec((None, tq, 1), lambda h, s, tab: (h, tab[0, s], 0))
    cvec = pl.BlockSpec((None, 1, tq), lambda h, s, tab: (h, 0, tab[1, s]))
    sds = jax.ShapeDtypeStruct((T, FOX_W), F32)
    return pl.pallas_call(
        body, name="fox_bwd",
        grid_spec=pltpu.PrefetchScalarGridSpec(
            num_scalar_prefetch=1, grid=(FOX_HEADS, steps),
            in_specs=[rows(HQ), cols(HK), cols(HV), rvec, cvec, rows(0), rvec, rows(0)],
            out_specs=[pl.BlockSpec((T, HEAD_DIM), lambda h, s, tab: (0, h)), cols(0), cols(0), cvec],
            scratch_shapes=[pltpu.VMEM((tq, HEAD_DIM), F32), pltpu.VMEM((tq, HEAD_DIM), F32), pltpu.VMEM((1, tq), F32)]),
        out_shape=[sds, sds, sds, jax.ShapeDtypeStruct((FOX_HEADS, 1, T), F32)],
        compiler_params=_params(("parallel", "arbitrary")),
    )(_triangle(nq, True), qkv, qkv, qkv, cq, ck, out_f32, lse, dmix)


GW = SWA_GROUP * HEAD_DIM
GR = SWA_GROUP * WINDOW


def _swa_scores(q_ref, kp_ref, kc_ref, slope_ref, n):
    q = q_ref[...]
    qs = jnp.concatenate([q[:, t * HEAD_DIM:(t + 1) * HEAD_DIM] for t in range(SWA_GROUP)], axis=0)
    kb = jnp.concatenate([kp_ref[...], kc_ref[...]], axis=0)
    r = lax.broadcasted_iota(jnp.int32, (GR, 2 * WINDOW), 0) & (WINDOW - 1)
    jj = lax.broadcasted_iota(jnp.int32, (GR, 2 * WINDOW), 1)
    dist = WINDOW + r - jj
    valid = (dist >= 0) & (dist < WINDOW) & ((n > 0) | (jj >= WINDOW))
    s = _dot(qs, kb, "nt") * SCALE - slope_ref[...] * dist.astype(F32)
    return qs, kb, jnp.where(valid, s, NEG_INF), valid


def _swa_specs():
    HQ, HK, HV = C_SQ // GW, C_SK // HEAD_DIM, C_SV // HEAD_DIM
    q_spec = pl.BlockSpec((WINDOW, GW), lambda g, n: (n, HQ + g))

    def prev(base):
        return pl.BlockSpec((WINDOW, HEAD_DIM), lambda g, n: (jnp.maximum(n - 1, 0), base + g))

    def cur(base):
        return pl.BlockSpec((WINDOW, HEAD_DIM), lambda g, n: (n, base + g))

    col = pl.BlockSpec((None, GR, 1), lambda g, n: (g, 0, 0))
    return q_spec, prev(HK), cur(HK), prev(HV), cur(HV), col


def _swa_fwd(qkv, slopes, sinks):
    T = qkv.shape[0]
    nb = T // WINDOW
    assert C_SQ % GW == 0

    def body(q_ref, kp_ref, kc_ref, vp_ref, vc_ref, slope_ref, sink_ref, o_ref, lse_ref):
        n = pl.program_id(1)
        _, _, s, _ = _swa_scores(q_ref, kp_ref, kc_ref, slope_ref, n)
        m = jnp.maximum(jnp.max(s, axis=-1, keepdims=True), sink_ref[...])
        p = jnp.exp(s - m)
        l = jnp.sum(p, axis=-1, keepdims=True) + jnp.exp(sink_ref[...] - m)
        vb = jnp.concatenate([vp_ref[...], vc_ref[...]], axis=0)
        o = _dot(p.astype(BF16), vb, "nn") / l
        for t in range(SWA_GROUP):
            o_ref[:, t * HEAD_DIM:(t + 1) * HEAD_DIM] = o[t * WINDOW:(t + 1) * WINDOW, :].astype(BF16)
        lse_ref[...] = m + jnp.log(l)

    q_spec, kp, kc, vp, vc, col = _swa_specs()
    return pl.pallas_call(
        body, name="swa_fwd", grid=(SWA_KV_HEADS, nb), in_specs=[q_spec, kp, kc, vp, vc, col, col],
        out_specs=[pl.BlockSpec((WINDOW, GW), lambda g, n: (n, g)), pl.BlockSpec((None, None, GR, 1), lambda g, n: (g, n, 0, 0))],
        out_shape=[jax.ShapeDtypeStruct((T, SWA_HEADS * HEAD_DIM), BF16), jax.ShapeDtypeStruct((SWA_KV_HEADS, nb, GR, 1), F32)],
        compiler_params=_params(("parallel", "arbitrary")),
    )(qkv, qkv, qkv, qkv, qkv, slopes, sinks)


def _swa_bwd(qkv, slopes, sinks, out, lse, dmix):
    T = qkv.shape[0]
    nb = T // WINDOW
    DO = FOX_W // GW
    assert FOX_W % GW == 0

    def body(q_ref, kp_ref, kc_ref, vp_ref, vc_ref, slope_ref, sink_ref, o_ref, lse_ref, do_ref,
             dq_ref, dk_ref, dv_ref, dsink_ref, sink_sc):
        n = pl.program_id(1)

        @pl.when(n == 0)
        def _():
            dk_ref[...] = jnp.zeros_like(dk_ref)
            dv_ref[...] = jnp.zeros_like(dv_ref)
            sink_sc[...] = jnp.zeros_like(sink_sc)

        qs, kb, s, valid = _swa_scores(q_ref, kp_ref, kc_ref, slope_ref, n)
        lse = lse_ref[...]
        p = jnp.where(valid, jnp.exp(s - lse), 0.0)
        vb = jnp.concatenate([vp_ref[...], vc_ref[...]], axis=0)
        do = jnp.concatenate([do_ref[:, t * HEAD_DIM:(t + 1) * HEAD_DIM] for t in range(SWA_GROUP)], axis=0)
        oo = jnp.concatenate([o_ref[:, t * HEAD_DIM:(t + 1) * HEAD_DIM] for t in range(SWA_GROUP)], axis=0)
        dp = _dot(do, vb, "nt")
        delta = jnp.sum(do.astype(F32) * oo.astype(F32), axis=-1, keepdims=True)
        ds = p * (dp - delta)
        dsb = ds.astype(BF16)
        dq = _dot(dsb, kb, "nn") * SCALE
        for t in range(SWA_GROUP):
            dq_ref[:, t * HEAD_DIM:(t + 1) * HEAD_DIM] = dq[t * WINDOW:(t + 1) * WINDOW, :]
        dkb = _dot(dsb, qs, "tn") * SCALE
        dvb = _dot(p.astype(BF16), do, "tn")
        r_prev = pl.ds(pl.multiple_of(jnp.maximum(n - 1, 0) * WINDOW, WINDOW), WINDOW)
        r_cur = pl.ds(pl.multiple_of(n * WINDOW, WINDOW), WINDOW)
        dk_ref[r_prev, :] += dkb[:WINDOW, :]
        dk_ref[r_cur, :] += dkb[WINDOW:, :]
        dv_ref[r_prev, :] += dvb[:WINDOW, :]
        dv_ref[r_cur, :] += dvb[WINDOW:, :]
        sink_sc[...] -= jnp.exp(sink_ref[...] - lse) * delta

        @pl.when(n == nb - 1)
        def _():
            tot = [jnp.zeros((1, 128), F32) + jnp.sum(sink_sc[t * WINDOW:(t + 1) * WINDOW, :]) for t in range(SWA_GROUP)]
            dsink_ref[...] = jnp.concatenate(tot + [jnp.zeros((8 - SWA_GROUP, 128), F32)], axis=0)

    q_spec, kp, kc, vp, vc, col = _swa_specs()
    kv_acc = pl.BlockSpec((T, HEAD_DIM), lambda g, n: (0, g))
    return pl.pallas_call(
        body, name="swa_bwd", grid=(SWA_KV_HEADS, nb),
        in_specs=[q_spec, kp, kc, vp, vc, col, col, pl.BlockSpec((WINDOW, GW), lambda g, n: (n, g)),
                  pl.BlockSpec((None, None, GR, 1), lambda g, n: (g, n, 0, 0)), pl.BlockSpec((WINDOW, GW), lambda g, n: (n, DO + g))],
        out_specs=[pl.BlockSpec((WINDOW, GW), lambda g, n: (n, g)), kv_acc, kv_acc, pl.BlockSpec((None, 8, 128), lambda g, n: (g, 0, 0))],
        out_shape=[jax.ShapeDtypeStruct((T, SWA_HEADS * HEAD_DIM), F32), jax.ShapeDtypeStruct((T, SWA_KV_HEADS * HEAD_DIM), F32),
                   jax.ShapeDtypeStruct((T, SWA_KV_HEADS * HEAD_DIM), F32), jax.ShapeDtypeStruct((SWA_KV_HEADS, 8, 128), F32)],
        scratch_shapes=[pltpu.VMEM((GR, 1), F32)],
        compiler_params=_params(("parallel", "arbitrary")),
    )(qkv, qkv, qkv, qkv, qkv, slopes, sinks, out, lse, dmix)


def _mem_fwd(qkv, mk, mv):
    T, ML = qkv.shape[0], mk.shape[0]
    tq = _tile(T, 512)
    HQ = C_MQ // HEAD_DIM

    def body(q_ref, k_ref, v_ref, o_ref, lse_ref):
        s = _dot(q_ref[...], k_ref[...], "nt") * SCALE
        m = jnp.max(s, axis=-1, keepdims=True)
        p = jnp.exp(s - m)
        l = jnp.sum(p, axis=-1, keepdims=True)
        o_ref[...] = (_dot(p.astype(BF16), v_ref[...], "nn") / l).astype(BF16)
        lse_ref[...] = m + jnp.log(l)

    kv = pl.BlockSpec((ML, HEAD_DIM), lambda h, i: (0, h))
    return pl.pallas_call(
        body, name="mem_fwd", grid=(MEM_HEADS, T // tq),
        in_specs=[pl.BlockSpec((tq, HEAD_DIM), lambda h, i: (i, HQ + h)), kv, kv],
        out_specs=[pl.BlockSpec((tq, HEAD_DIM), lambda h, i: (i, h)), pl.BlockSpec((None, tq, 1), lambda h, i: (h, i, 0))],
        out_shape=[jax.ShapeDtypeStruct((T, MEM_HEADS * HEAD_DIM), BF16), jax.ShapeDtypeStruct((MEM_HEADS, T, 1), F32)],
        compiler_params=_params(("parallel", "arbitrary")),
    )(qkv, mk, mv)


def _mem_bwd(qkv, mk, mv, out, lse, dmix):
    T, ML = qkv.shape[0], mk.shape[0]
    tq = _tile(T, 512)
    HQ = C_MQ // HEAD_DIM
    DO = (FOX_W + SWA_HEADS * HEAD_DIM) // HEAD_DIM

    def body(q_ref, k_ref, v_ref, o_ref, lse_ref, do_ref, dq_ref, dk_ref, dv_ref):
        q, k, v, do = q_ref[...], k_ref[...], v_ref[...], do_ref[...]
        p = jnp.exp(_dot(q, k, "nt") * SCALE - lse_ref[...])
        dp = _dot(do, v, "nt")
        delta = jnp.sum(do.astype(F32) * o_ref[...].astype(F32), axis=-1, keepdims=True)
        dsb = (p * (dp - delta)).astype(BF16)
        dq_ref[...] = _dot(dsb, k, "nn") * SCALE
        dk_part = _dot(dsb, q, "tn") * SCALE
        dv_part = _dot(p.astype(BF16), do, "tn")

        @pl.when(pl.program_id(1) == 0)
        def _():
            dk_ref[...] = dk_part
            dv_ref[...] = dv_part

        @pl.when(pl.program_id(1) > 0)
        def _():
            dk_ref[...] += dk_part
            dv_ref[...] += dv_part

    kv = pl.BlockSpec((ML, HEAD_DIM), lambda h, i: (0, h))
    qb = pl.BlockSpec((tq, HEAD_DIM), lambda h, i: (i, h))
    return pl.pallas_call(
        body, name="mem_bwd", grid=(MEM_HEADS, T // tq),
        in_specs=[pl.BlockSpec((tq, HEAD_DIM), lambda h, i: (i, HQ + h)), kv, kv, qb,
                  pl.BlockSpec((None, tq, 1), lambda h, i: (h, i, 0)), pl.BlockSpec((tq, HEAD_DIM), lambda h, i: (i, DO + h))],
        out_specs=[qb, kv, kv],
        out_shape=[jax.ShapeDtypeStruct((T, MEM_HEADS * HEAD_DIM), F32), jax.ShapeDtypeStruct((ML, MEM_HEADS * HEAD_DIM), F32),
                   jax.ShapeDtypeStruct((ML, MEM_HEADS * HEAD_DIM), F32)],
        compiler_params=_params(("parallel", "arbitrary")),
    )(qkv, mk, mv, out, lse, dmix)


HBM = pl.BlockSpec(memory_space=pltpu.HBM)


def _place():
    x, y, c = lax.axis_index("x"), lax.axis_index("y"), lax.axis_index("c")
    chips = [(1 - x, y), (x, 1 - y), (1 - x, 1 - y)]
    return x, y, c, chips


def _remote(src, dst, send_sem, recv_sem, device):
    return pltpu.make_async_remote_copy(src_ref=src, dst_ref=dst, send_sem=send_sem, recv_sem=recv_sem,
                                        device_id=device, device_id_type=MESH)


def _place_ids():
    x, y, c = lax.axis_index("x"), lax.axis_index("y"), lax.axis_index("c")
    order = [2 * x + y, 2 * (1 - x) + y, 2 * x + (1 - y), 2 * (1 - x) + (1 - y)]
    return jnp.stack([2 * x + y, c] + order).astype(jnp.int32)


def _cast_place(name, w, ids):
    R, C = w.shape
    tr = _tile(R, 256, 16)

    def body(ids_ref, w_ref, o_ref):
        o_ref[...] = w_ref[...].astype(BF16)

    return pl.pallas_call(
        body, name=name,
        grid_spec=pltpu.PrefetchScalarGridSpec(
            num_scalar_prefetch=1, grid=(R // tr,), in_specs=[pl.BlockSpec((tr, C), lambda i, ids: (i, 0))],
            out_specs=pl.BlockSpec((None, tr, C), lambda i, ids: (ids[0], i, 0))),
        out_shape=jax.ShapeDtypeStruct((N_CHIPS, R, C), BF16), compiler_params=_params(("parallel",)),
    )(ids, w)


SEM = pl.BlockSpec(memory_space=pltpu.SEMAPHORE)
EFFECT = pltpu.SideEffectType.DATAFLOW_SIDE_EFFECTING


def _hbm(a):
    return pltpu.with_memory_space_constraint(a, pltpu.HBM)


def _gather_start(name, placed, after):
    n = len(placed)

    ns = 3 * n

    def body(*refs):
        send, recv = refs[n + 1:n + 1 + ns], refs[n + 1 + ns:n + 1 + 2 * ns]
        buf = refs[n + 1 + 2 * ns:2 * n + 1 + 2 * ns]
        token = refs[2 * n + 1 + 2 * ns]
        x, y, c, chips = _place()
        me = 2 * x + y
        for a in range(n):
            half = buf[a].shape[1] // 2
            mine = buf[a].at[me, pl.ds(c * half, half)]
            for j, (cx, cy) in enumerate(chips):
                _remote(mine, mine, send[3 * a + j], recv[3 * a + j], (cx, cy, c)).start()
        token[...] = jnp.zeros_like(token)

    res = pl.pallas_call(
        body, name=name, in_specs=[HBM] * n + [pl.BlockSpec(memory_space=pl.ANY)],
        out_specs=[SEM] * (2 * ns) + [HBM] * n + [pl.BlockSpec(memory_space=pltpu.VMEM)],
        out_shape=[pltpu.SemaphoreType.DMA(())] * (2 * ns)
        + [pltpu.HBM(s.shape, s.dtype) for s in placed] + [jax.ShapeDtypeStruct((8, 128), F32)],
        input_output_aliases={a: 2 * ns + a for a in range(n)},
        compiler_params=pltpu.CompilerParams(has_side_effects=EFFECT),
    )(*[_hbm(s) for s in placed], after)
    return list(res[:ns]), list(res[ns:2 * ns]), list(res[2 * ns:2 * ns + n]), res[2 * ns + n]


def _gather_wait(name, send, recv, bufs, after):
    n = len(bufs)

    ns = 3 * n

    def body(*refs):
        buf = refs[:n]
        send_ref, recv_ref = refs[n:n + ns], refs[n + ns:n + 2 * ns]
        x, y, c, chips = _place()
        ids = [2 * cx + cy for cx, cy in chips]
        for a in range(n):
            half = buf[a].shape[1] // 2
            for j in range(3):
                landed = buf[a].at[ids[j], pl.ds(c * half, half)]
                cp = _remote(landed, landed, send_ref[3 * a + j], recv_ref[3 * a + j], (x, y, c))
                cp.wait_send()
                cp.wait_recv()

    res = pl.pallas_call(
        body, name=name, in_specs=[HBM] * n + [SEM] * (2 * ns) + [pl.BlockSpec(memory_space=pl.ANY)], out_specs=[HBM] * n,
        out_shape=[pltpu.HBM(s.shape, s.dtype) for s in bufs], input_output_aliases={a: a for a in range(n)},
        compiler_params=pltpu.CompilerParams(has_side_effects=EFFECT),
    )(*bufs, *send, *recv, after)
    return list(res)


def _gather_forward(name, bufs):
    n = len(bufs)

    def body(*refs):
        buf = refs[n:2 * n]
        send, recv = refs[2 * n:]
        x, y, c, chips = _place()
        ids = [2 * cx + cy for cx, cy in chips]
        copies = []
        for a in range(n):
            half = buf[a].shape[1] // 2
            for j in range(3):
                landed = buf[a].at[ids[j], pl.ds(c * half, half)]
                cp = _remote(landed, landed, send.at[a, j], recv.at[a, j], (x, y, 1 - c))
                cp.start()
                copies.append(cp)
        for a in range(n):
            half = buf[a].shape[1] // 2
            for j in range(3):
                landed = buf[a].at[ids[j], pl.ds((1 - c) * half, half)]
                _remote(landed, landed, send.at[a, j], recv.at[a, j], (x, y, c)).wait_recv()
        for cp in copies:
            cp.wait_send()

    return pl.pallas_call(
        body, name=name, in_specs=[HBM] * n, out_specs=[HBM] * n,
        out_shape=[jax.ShapeDtypeStruct(s.shape, s.dtype) for s in bufs], input_output_aliases={a: a for a in range(n)},
        scratch_shapes=[pltpu.SemaphoreType.DMA((n, 3)), pltpu.SemaphoreType.DMA((n, 3))],
    )(*bufs)


def _pair_exchange(name, grads):
    n = len(grads)

    def body(*refs):
        src, theirs = refs[:n], refs[n:2 * n]
        send, recv = refs[2 * n:]
        x, y, c, chips = _place()
        order = [2 * x + y] + [2 * cx + cy for cx, cy in chips]
        copies = []
        for a in range(n):
            half = src[a].shape[1] // 2
            for j in range(N_CHIPS):
                cp = _remote(src[a].at[order[j], pl.ds((1 - c) * half, half)], theirs[a].at[j], send.at[a, j], recv.at[a, j], (x, y, 1 - c))
                cp.start()
                copies.append(cp)
        for cp in copies:
            cp.wait()

    return pl.pallas_call(
        body, name=name, in_specs=[HBM] * n, out_specs=[HBM] * n,
        out_shape=[jax.ShapeDtypeStruct((N_CHIPS, g.shape[1] // 2, g.shape[2]), g.dtype) for g in grads],
        scratch_shapes=[pltpu.SemaphoreType.DMA((n, N_CHIPS)), pltpu.SemaphoreType.DMA((n, N_CHIPS))],
    )(*grads)


def _pair_start(name, grads):
    n = len(grads)
    ns = N_CHIPS * n

    def body(*refs):
        send, recv = refs[2 * n:2 * n + ns], refs[2 * n + ns:2 * n + 2 * ns]
        src = refs[2 * n + 2 * ns:3 * n + 2 * ns]
        land = refs[3 * n + 2 * ns:4 * n + 2 * ns]
        token = refs[4 * n + 2 * ns]
        x, y, c, chips = _place()
        order = [2 * x + y] + [2 * cx + cy for cx, cy in chips]
        for a in range(n):
            half = src[a].shape[1] // 2
            for j in range(N_CHIPS):
                _remote(src[a].at[order[j], pl.ds((1 - c) * half, half)], land[a].at[j],
                        send[N_CHIPS * a + j], recv[N_CHIPS * a + j], (x, y, 1 - c)).start()
        token[...] = jnp.zeros_like(token)

    lands = [jax.ShapeDtypeStruct((N_CHIPS, g.shape[1] // 2, g.shape[2]), g.dtype) for g in grads]
    res = pl.pallas_call(
        body, name=name, in_specs=[HBM] * (2 * n),
        out_specs=[SEM] * (2 * ns) + [HBM] * (2 * n) + [pl.BlockSpec(memory_space=pltpu.VMEM)],
        out_shape=[pltpu.SemaphoreType.DMA(())] * (2 * ns) + [pltpu.HBM(g.shape, g.dtype) for g in grads]
        + [pltpu.HBM(l.shape, l.dtype) for l in lands] + [jax.ShapeDtypeStruct((8, 128), F32)],
        input_output_aliases={a: 2 * ns + a for a in range(2 * n)},
        compiler_params=pltpu.CompilerParams(has_side_effects=EFFECT),
    )(*[_hbm(g) for g in grads], *[_hbm(lax.empty(l.shape, l.dtype)) for l in lands])
    return list(res[:ns]), list(res[ns:2 * ns]), list(res[2 * ns:2 * ns + n]), list(res[2 * ns + n:2 * ns + 2 * n]), res[2 * ns + 2 * n]


def _pair_wait(name, send, recv, grads, lands, after):
    n = len(grads)
    ns = N_CHIPS * n

    def body(*refs):
        src, land = refs[:n], refs[n:2 * n]
        send_ref, recv_ref = refs[2 * n:2 * n + ns], refs[2 * n + ns:2 * n + 2 * ns]
        x, y, c, _ = _place()
        for a in range(n):
            for j in range(N_CHIPS):
                cp = _remote(land[a].at[j], land[a].at[j], send_ref[N_CHIPS * a + j], recv_ref[N_CHIPS * a + j], (x, y, c))
                cp.wait_send()
                cp.wait_recv()

    res = pl.pallas_call(
        body, name=name, in_specs=[HBM] * (2 * n) + [SEM] * (2 * ns) + [pl.BlockSpec(memory_space=pl.ANY)],
        out_specs=[HBM] * (2 * n), out_shape=[pltpu.HBM(g.shape, g.dtype) for g in grads] + [pltpu.HBM(l.shape, l.dtype) for l in lands],
        input_output_aliases={a: a for a in range(2 * n)},
        compiler_params=pltpu.CompilerParams(has_side_effects=EFFECT),
    )(*grads, *lands, *send, *recv, after)
    return list(res[:n]), list(res[n:])


def _chip_start(name, parts):
    n = len(parts)
    ns = 3 * n

    def body(*refs):
        send, recv = refs[2 * n:2 * n + ns], refs[2 * n + ns:2 * n + 2 * ns]
        src = refs[2 * n + 2 * ns:3 * n + 2 * ns]
        land = refs[3 * n + 2 * ns:4 * n + 2 * ns]
        token = refs[4 * n + 2 * ns]
        x, y, c, chips = _place()
        for a in range(n):
            for j, (cx, cy) in enumerate(chips):
                _remote(src[a].at[j], land[a].at[j], send[3 * a + j], recv[3 * a + j], (cx, cy, c)).start()
        token[...] = jnp.zeros_like(token)

    res = pl.pallas_call(
        body, name=name, in_specs=[HBM] * (2 * n),
        out_specs=[SEM] * (2 * ns) + [HBM] * (2 * n) + [pl.BlockSpec(memory_space=pltpu.VMEM)],
        out_shape=[pltpu.SemaphoreType.DMA(())] * (2 * ns) + [pltpu.HBM(p.shape, p.dtype) for p in parts] * 2
        + [jax.ShapeDtypeStruct((8, 128), F32)],
        input_output_aliases={a: 2 * ns + a for a in range(2 * n)},
        compiler_params=pltpu.CompilerParams(has_side_effects=EFFECT),
    )(*[_hbm(p) for p in parts], *[_hbm(lax.empty(p.shape, p.dtype)) for p in parts])
    return list(res[:ns]), list(res[ns:2 * ns]), list(res[2 * ns:2 * ns + n]), list(res[2 * ns + n:2 * ns + 2 * n]), res[2 * ns + 2 * n]


def _chip_wait(name, send, recv, parts, lands, after):
    n = len(parts)
    ns = 3 * n

    def body(*refs):
        src, land = refs[:n], refs[n:2 * n]
        send_ref, recv_ref = refs[2 * n:2 * n + ns], refs[2 * n + ns:2 * n + 2 * ns]
        x, y, c, _ = _place()
        for a in range(n):
            for j in range(3):
                cp = _remote(src[a].at[j], land[a].at[j], send_ref[3 * a + j], recv_ref[3 * a + j], (x, y, c))
                cp.wait_send()
                cp.wait_recv()

    res = pl.pallas_call(
        body, name=name, in_specs=[HBM] * (2 * n) + [SEM] * (2 * ns) + [pl.BlockSpec(memory_space=pl.ANY)],
        out_specs=[HBM] * (2 * n), out_shape=[pltpu.HBM(p.shape, p.dtype) for p in parts] * 2,
        input_output_aliases={a: a for a in range(2 * n)},
        compiler_params=pltpu.CompilerParams(has_side_effects=EFFECT),
    )(*parts, *lands, *send, *recv, after)
    return list(res[n:])


def _pair_share(name, shards):
    n = len(shards)

    def body(*refs):
        buf = refs[n:2 * n]
        send, recv = refs[2 * n:]
        x, y, c, _ = _place()
        copies = []
        for a in range(n):
            half = buf[a].shape[0] // 2
            mine = buf[a].at[pl.ds(c * half, half)]
            cp = _remote(mine, mine, send.at[a], recv.at[a], (x, y, 1 - c))
            cp.start()
            copies.append(cp)
        for a, cp in enumerate(copies):
            half = buf[a].shape[0] // 2
            cp.wait_send()
            theirs = buf[a].at[pl.ds((1 - c) * half, half)]
            _remote(theirs, theirs, send.at[a], recv.at[a], (x, y, c)).wait_recv()

    return pl.pallas_call(
        body, name=name, in_specs=[HBM] * n, out_specs=[HBM] * n,
        out_shape=[jax.ShapeDtypeStruct(s.shape, s.dtype) for s in shards], input_output_aliases={a: a for a in range(n)},
        scratch_shapes=[pltpu.SemaphoreType.DMA((n,)), pltpu.SemaphoreType.DMA((n,))],
    )(*shards)


def _all_reduce_small(buf):
    R, W = buf.shape

    def body(src_ref, out_ref, slots, send, recv):
        x, y, c, _ = _place()
        me = 4 * x + 2 * y + c
        copies = []
        for dx in range(2):
            for dy in range(2):
                for dc in range(2):
                    if dx == dy == dc == 0:
                        continue
                    k = 4 * dx + 2 * dy + dc
                    peer = (x ^ dx, y ^ dy, c ^ dc)
                    cp = _remote(src_ref, slots.at[me], send.at[k], recv.at[k], peer)
                    cp.start()
                    copies.append((cp, k))
        slots[me] = src_ref[...]
        for cp, k in copies:
            cp.wait_send()
            landed = slots.at[me ^ k]
            _remote(landed, landed, send.at[k], recv.at[k], (x, y, c)).wait_recv()
        total = slots[0]
        for d in range(1, N_DEV):
            total = total + slots[d]
        out_ref[...] = total

    return pl.pallas_call(
        body, name="all_reduce_small", in_specs=[pl.BlockSpec(memory_space=pltpu.VMEM)],
        out_specs=pl.BlockSpec(memory_space=pltpu.VMEM), out_shape=jax.ShapeDtypeStruct((R, W), F32),
        scratch_shapes=[pltpu.VMEM((N_DEV, R, W), F32), pltpu.SemaphoreType.DMA((N_DEV,)), pltpu.SemaphoreType.DMA((N_DEV,))],
    )(buf)


def _pair_sum_bf16(name, grad, theirs, ids):
    _, R2, C = theirs.shape
    tr = _tile(R2, 256, 16)
    nrb = R2 // tr

    def body(ids_ref, a_ref, b_ref, o_ref):
        o_ref[...] = (a_ref[...] + b_ref[...]).astype(BF16)

    return pl.pallas_call(
        body, name=name,
        grid_spec=pltpu.PrefetchScalarGridSpec(
            num_scalar_prefetch=1, grid=(3, nrb),
            in_specs=[pl.BlockSpec((None, tr, C), lambda j, i, ids: (ids[3 + j], ids[1] * nrb + i, 0)),
                      pl.BlockSpec((None, tr, C), lambda j, i, ids: (j + 1, i, 0))],
            out_specs=pl.BlockSpec((None, tr, C), lambda j, i, ids: (j, i, 0))),
        out_shape=jax.ShapeDtypeStruct((3, R2, C), BF16), compiler_params=_params(("parallel", "parallel")),
    )(ids, grad, theirs)


def _chip_sum(name, grad, theirs, arrived, ids):
    _, R2, C = theirs.shape
    tr = _tile(R2, 256, 16)
    nrb = R2 // tr

    def body(ids_ref, a_ref, b_ref, r_ref, o_ref):
        tot = a_ref[...] + b_ref[...]
        for j in range(3):
            tot = tot + r_ref[j].astype(F32)
        o_ref[...] = tot

    return pl.pallas_call(
        body, name=name,
        grid_spec=pltpu.PrefetchScalarGridSpec(
            num_scalar_prefetch=1, grid=(nrb,),
            in_specs=[pl.BlockSpec((None, tr, C), lambda i, ids: (ids[0], ids[1] * nrb + i, 0)),
                      pl.BlockSpec((None, tr, C), lambda i, ids: (0, i, 0)),
                      pl.BlockSpec((3, tr, C), lambda i, ids: (0, i, 0))],
            out_specs=pl.BlockSpec((tr, C), lambda i, ids: (ids[1] * nrb + i, 0))),
        out_shape=jax.ShapeDtypeStruct((2 * R2, C), F32), compiler_params=_params(("parallel",)),
    )(ids, grad, theirs, arrived)


def _adamw(name, w, g, m, v):
    R, C = w.shape
    tr = _tile(R, 128, 8)
    c1 = 1.0 / (1.0 - ADAM_B1 ** ADAM_STEP)
    c2 = 1.0 / (1.0 - ADAM_B2 ** ADAM_STEP)

    def body(w_ref, g_ref, m_ref, v_ref, d_ref, mo_ref, vo_ref):
        gv = g_ref[...]
        mn = ADAM_B1 * m_ref[...] + (1.0 - ADAM_B1) * gv
        vn = ADAM_B2 * v_ref[...] + (1.0 - ADAM_B2) * (gv * gv)
        d_ref[...] = -ADAM_LR * ((mn * c1) / (jnp.sqrt(vn * c2) + ADAM_EPS) + ADAM_WD * w_ref[...])
        mo_ref[...] = mn
        vo_ref[...] = vn

    spec = pl.BlockSpec((tr, C), lambda i: (i, 0))
    sds = jax.ShapeDtypeStruct((R, C), F32)
    return pl.pallas_call(body, name=name, grid=(R // tr,), in_specs=[spec] * 4, out_specs=[spec] * 3, out_shape=[sds] * 3,
                          compiler_params=_params(("parallel",)))(w, g, m, v)


SMALL = ["ffn1_norm", "mix_norm", "mem_norm", "forget_bias", "fox_q_gain", "fox_k_gain", "swa_q_gain", "swa_k_gain", "swa_sinks",
         "mem_q_gain", "mem_k_gain", "ffn2_norm"]
LARGE = ["ffn1_gate", "ffn1_up", "ffn1_down", "w_in", "w_mem_k", "w_mem_v", "w_out", "ffn2_gate", "ffn2_up", "ffn2_down"]
GATHER_GROUPS = [["ffn1_gate", "ffn1_up"], ["ffn1_down", "w_in", "w_mem_k", "w_mem_v"], ["w_out", "ffn2_gate", "ffn2_up", "ffn2_down"]]
WEIGHTS = ["ffn1_norm", "ffn1_gate", "ffn1_up", "ffn1_down", "mix_norm", "mem_norm", "w_in", "forget_bias", "w_mem_k", "w_mem_v",
           "fox_q_gain", "fox_k_gain", "swa_q_gain", "swa_k_gain", "swa_sinks", "mem_q_gain", "mem_k_gain", "w_out", "ffn2_norm",
           "ffn2_gate", "ffn2_up", "ffn2_down"]


def _pad_proj_cols(w):
    out = jnp.zeros((w.shape[0], PROJ_W), w.dtype)
    for start, width, pstart in REF_GROUPS:
        out = lax.dynamic_update_slice(out, w[:, start:start + width], (0, pstart))
    return out


def _unpad_proj_cols(w):
    return jnp.concatenate([w[:, pstart:pstart + width] for _, width, pstart in REF_GROUPS], axis=1)


def _pack_small(vals):
    flat = jnp.concatenate([vals[k].reshape(-1).astype(F32) for k in SMALL + ["loss"]])
    n = flat.shape[0]
    total = -(-n // 1024) * 1024
    return jnp.pad(flat, (0, total - n)).reshape(total // 128, 128)


def _unpack_small(buf, shapes):
    flat = buf.reshape(-1)
    out, off = {}, 0
    for k in SMALL + ["loss"]:
        size = int(np.prod(shapes[k]))
        out[k] = flat[off:off + size].reshape(shapes[k])
        off += size
    return out


def kernel(x, mem, ffn1_norm, ffn1_gate, ffn1_up, ffn1_down, mix_norm, mem_norm, w_in, forget_bias, w_mem_k, w_mem_v, fox_q_gain, fox_k_gain, swa_q_gain, swa_k_gain, swa_sinks, mem_q_gain, mem_k_gain, w_out, ffn2_norm, ffn2_gate, ffn2_up, ffn2_down, loss_target, m_ffn1_norm, m_ffn1_gate, m_ffn1_up, m_ffn1_down, m_mix_norm, m_mem_norm, m_w_in, m_forget_bias, m_w_mem_k, m_w_mem_v, m_fox_q_gain, m_fox_k_gain, m_swa_q_gain, m_swa_k_gain, m_swa_sinks, m_mem_q_gain, m_mem_k_gain, m_w_out, m_ffn2_norm, m_ffn2_gate, m_ffn2_up, m_ffn2_down, v_ffn1_norm, v_ffn1_gate, v_ffn1_up, v_ffn1_down, v_mix_norm, v_mem_norm, v_w_in, v_forget_bias, v_w_mem_k, v_w_mem_v, v_fox_q_gain, v_fox_k_gain, v_swa_q_gain, v_swa_k_gain, v_swa_sinks, v_mem_q_gain, v_mem_k_gain, v_w_out, v_ffn2_norm, v_ffn2_gate, v_ffn2_up, v_ffn2_down):
    given = dict(locals())
    T, D = x.shape[1], x.shape[2]
    ML = mem.shape[1]
    xin = x.reshape(T, D)
    target = loss_target.reshape(T, D)
    memin = mem.reshape(ML, D)

    ids = _place_ids()
    shard = {k: given[k][0] for k in LARGE}
    shard["w_in"] = _pad_proj_cols(shard["w_in"])
    started, after = [], ids
    for gi, group in enumerate(GATHER_GROUPS):
        placed = [_cast_place("cast_" + k, shard[k], ids) for k in group]
        send, recv, bufs, after = _gather_start("gather_start_%d" % gi, placed, after)
        started.append((send, recv, bufs))

    def arrive(gi, done):
        send, recv, bufs = started[gi]
        bufs = _gather_wait("gather_wait_%d" % gi, send, recv, bufs, done)
        return dict(zip(GATHER_GROUPS[gi], _gather_forward("gather_forward_%d" % gi, bufs)))

    gains = jnp.concatenate([fox_q_gain, fox_k_gain, swa_q_gain, swa_k_gain, mem_q_gain,
                             jnp.pad(forget_bias, ((0, 0), (0, HEAD_DIM - FOX_HEADS))), jnp.zeros((2, HEAD_DIM), F32)], axis=0)
    slopes_np = 2.0 ** (-8.0 * np.arange(1, SWA_HEADS + 1) / SWA_HEADS)
    slopes = jnp.asarray(np.repeat(slopes_np, WINDOW).reshape(SWA_KV_HEADS, GR, 1), F32)
    sinks = jnp.repeat(swa_sinks.reshape(SWA_HEADS), WINDOW).reshape(SWA_KV_HEADS, GR, 1)

    h1 = _rms_fwd("ffn1_norm_fwd", xin, ffn1_norm + after[0, 0])
    full = arrive(0, h1)
    wg1, wu1 = full["ffn1_gate"], full["ffn1_up"]
    g1, u1, a1 = _ffn_gu("ffn1_gate_up", h1, wg1, wu1)
    full = arrive(1, a1)
    wd1 = full["ffn1_down"].reshape(-1, D)
    win = full["w_in"].reshape(D, PROJ_W)
    wmk = full["w_mem_k"].reshape(D, MEM_HEADS * HEAD_DIM)
    wmv = full["w_mem_v"].reshape(D, MEM_HEADS * HEAD_DIM)
    x1 = _ffn_down("ffn1_down", a1, wd1, xin)
    h2 = _rms_fwd("mix_norm_fwd", x1, mix_norm)
    proj = _mm2d("proj_in", h2, win, "nn", F32, tn=1408, tk=2048, n_outer=True)
    qkv, logf = _prep_fwd(proj, gains)
    cum = _cumsum_rows("forget_cumsum", logf, False)
    cum_h = cum[:, :FOX_HEADS].T
    cq, ck = cum_h.reshape(FOX_HEADS, T, 1), cum_h.reshape(FOX_HEADS, 1, T)
    mn = _rms_fwd("mem_norm_fwd", memin, mem_norm)
    mk_raw = _mm2d("mem_k_proj", mn, wmk, "nn", F32)
    mv = _mm2d("mem_v_proj", mn, wmv, "nn", BF16)
    mk = _head_norm_rows(mk_raw, mem_k_gain)
    out_a, out_a_f32, lse_a = _fox_fwd(qkv, cq, ck)
    out_b, lse_b = _swa_fwd(qkv, slopes, sinks)
    out_c, lse_c = _mem_fwd(qkv, mk, mv)
    mixed = jnp.concatenate([out_a, out_b, out_c], axis=1)
    full = arrive(2, mixed)
    wo = full["w_out"].reshape(-1, D)
    wg2, wu2, wd2 = full["ffn2_gate"], full["ffn2_up"], full["ffn2_down"].reshape(-1, D)
    x2 = _mm2d("mix_out", mixed, wo, "nn", F32, tk=2048, n_outer=True, extras=[x1], epilogue=lambda accs, ex: [ex[0] + accs[0]])
    h3 = _rms_fwd("ffn2_norm_fwd", x2, ffn2_norm)
    g2, u2, a2 = _ffn_gu("ffn2_gate_up", h3, wg2, wu2)
    x3 = _ffn_down("ffn2_down", a2, wd2, x2)
    dx3, dyb3, loss_part = _loss_head(x3, target)

    grads, small, res = {}, {"loss": loss_part[0, 0]}, {}

    def send_off(tag, group):
        theirs = _pair_exchange("grad_pair_exchange_" + tag, [grads[k] for k in group])
        to_chips = [_pair_sum_bf16("pair_sum_" + k, grads[k], b, ids) for k, b in zip(group, theirs)]
        send, recv, parts, lands, token = _chip_start("grad_chip_start_" + tag, to_chips)
        return (group, theirs, send, recv, parts, lands), token

    def pair_off(tag, group):
        send, recv, own, lands, token = _pair_start("grad_pair_start_" + tag, [grads[k] for k in group])
        return (group, send, recv, own, lands), token

    def chip_off(tag, started, done):
        group, send, recv, own, lands = started
        own, theirs = _pair_wait("grad_pair_wait_" + tag, send, recv, own, lands, done)
        grads.update(zip(group, own))
        to_chips = [_pair_sum_bf16("pair_sum_" + k, grads[k], b, ids) for k, b in zip(group, theirs)]
        send, recv, parts, lands, token = _chip_start("grad_chip_start_" + tag, to_chips)
        return (group, theirs, send, recv, parts, lands), token

    def finish(tag, state, done):
        group, theirs, send, recv, parts, lands = state
        arrived = _chip_wait("grad_chip_wait_" + tag, send, recv, parts, lands, done)
        halves = [_chip_sum("chip_sum_" + k, grads[k], b, r, ids) for k, b, r in zip(group, theirs, arrived)]
        reduced = dict(zip(group, _pair_share("grad_pair_share_" + tag, halves)))
        last = None
        for k in group:
            gk = _unpad_proj_cols(reduced[k]) if k == "w_in" else reduced[k]
            d, mo, vo = _adamw("adamw_" + k, given[k][0], gk, given["m_" + k][0], given["v_" + k][0])
            res[k] = tuple(t[None] for t in (gk, d, mo, vo))
            last = vo
        return last

    dg2, du2, grads["ffn2_gate"], grads["ffn2_up"], dwd2 = _ffn_bwd_w("ffn2", dyb3, h3, g2, u2, a2, wd2, N_CHIPS)
    grads["ffn2_down"] = dwd2.reshape(N_CHIPS, -1, D)
    started, token = pair_off("a", ["ffn2_gate", "ffn2_up", "ffn2_down"])
    dh3 = _ffn_bwd_x("ffn2", dg2, du2, wg2, wu2, token)
    state_a, _ = chip_off("a", started, dh3)
    dx2, dx2b, small["ffn2_norm"] = _rms_bwd("ffn2_norm_bwd", dh3, x2, ffn2_norm, dx3, 1.0)
    dmix = _mm2d("mix_out_dx", dx2b, wo, "nt", BF16, tk=2048, n_outer=True)
    grads["w_out"] = _mm2d("mix_out_dw", mixed, dx2b, "tn", F32, tk=1024).reshape(N_CHIPS, -1, D)
    dfq, dfk, dfv, dck = _fox_bwd(qkv, cq, ck, out_a_f32, lse_a, dmix)
    dsq, dsk, dsv, dsink = _swa_bwd(qkv, slopes, sinks, out_b, lse_b, dmix)
    dmq, dmk, dmv = _mem_bwd(qkv, mk, mv, out_c, lse_c, dmix)
    small["swa_sinks"] = dsink[:, :SWA_GROUP, 0].reshape(1, SWA_HEADS)
    dcum = jnp.pad(dck.reshape(FOX_HEADS, T).T, ((0, 0), (0, HEAD_DIM - FOX_HEADS)))
    dlogf = _cumsum_rows("forget_cumsum_bwd", dcum, True)
    dproj, dgains = _prep_bwd(proj, gains, dfq, dfk, dfv, dsq, dsk, dsv, dmq, dlogf)
    for row, k in enumerate(["fox_q_gain", "fox_k_gain", "swa_q_gain", "swa_k_gain", "mem_q_gain"]):
        small[k] = dgains[row:row + 1, :]
    small["forget_bias"] = dgains[5:6, :FOX_HEADS]
    grads["w_in"] = _mm2d("proj_in_dw", h2, dproj, "tn", F32, tn=1408, tk=1024).reshape(N_CHIPS, -1, PROJ_W)
    dmk_raw, small["mem_k_gain"] = _head_norm_rows_bwd(mk_raw, mem_k_gain, dmk)
    dmvb = dmv.astype(BF16)
    grads["w_mem_k"] = _mm2d("mem_k_dw", mn, dmk_raw, "tn", F32).reshape(N_CHIPS, -1, MEM_HEADS * HEAD_DIM)
    grads["w_mem_v"] = _mm2d("mem_v_dw", mn, dmvb, "tn", F32).reshape(N_CHIPS, -1, MEM_HEADS * HEAD_DIM)
    dmn = _mm2d("mem_k_dx", dmk_raw, wmk, "nt", F32)
    dmn = _mm2d("mem_v_dx", dmvb, wmv, "nt", F32, extras=[dmn], epilogue=lambda accs, ex: [ex[0] + accs[0]])
    _, _, small["mem_norm"] = _rms_bwd("mem_norm_bwd", dmn, memin, mem_norm, jnp.zeros_like(memin), 1.0)
    started, token = pair_off("b", ["w_out", "w_in", "w_mem_k", "w_mem_v"])
    dh2 = _mm2d("proj_in_dx", dproj, win, "nt", F32, tm=1024, tk=1408, after=token)
    state_b, _ = chip_off("b", started, dh2)
    dx1, dyb1, small["mix_norm"] = _rms_bwd("mix_norm_bwd", dh2, x1, mix_norm, dx2, 0.5)
    dg1, du1, grads["ffn1_gate"], grads["ffn1_up"], dwd1 = _ffn_bwd_w("ffn1", dyb1, h1, g1, u1, a1, wd1, N_CHIPS)
    grads["ffn1_down"] = dwd1.reshape(N_CHIPS, -1, D)
    state_c, token_c = send_off("c", ["ffn1_gate", "ffn1_up", "ffn1_down"])
    dh1 = _ffn_bwd_x("ffn1", dg1, du1, wg1, wu1, token_c)
    grad_x, _, small["ffn1_norm"] = _rms_bwd("ffn1_norm_bwd", dh1, xin, ffn1_norm, dx1, 1.0)

    shapes = {k: given[k].shape for k in SMALL}
    shapes["loss"] = ()
    red_small = _unpack_small(_all_reduce_small(_pack_small(small)), shapes)
    loss = red_small["loss"]
    zero = {"loss": jnp.zeros((), F32)}
    packed = [_pack_small({**zero, **{k: src[k] for k in SMALL}}) for src in (
        {k: given[k] for k in SMALL}, red_small, {k: given["m_" + k] for k in SMALL}, {k: given["v_" + k] for k in SMALL})]
    small_out = _adamw("adamw_small", *packed)
    d_s, m_s, v_s = (_unpack_small(t, shapes) for t in small_out)
    for k in SMALL:
        res[k] = (red_small[k], d_s[k], m_s[k], v_s[k])

    done = finish("a", state_a, small_out[0])
    done = finish("b", state_b, done)
    finish("c", state_c, done)

    outs = [loss, grad_x.reshape(1, T, D)]
    for part in range(4):
        outs += [res[k][part] for k in WEIGHTS]
    return tuple(outs)
```

```python
import functools

import numpy as np
import jax
import jax.numpy as jnp
from jax import lax
from jax.experimental import pallas as pl
from jax.experimental.pallas import tpu as pltpu

F32 = jnp.float32
BF16 = jnp.bfloat16
MESH = pl.DeviceIdType.MESH

HEAD_DIM = 128
FOX_HEADS = 6
SWA_HEADS = 6
SWA_KV_HEADS = 2
SWA_GROUP = SWA_HEADS // SWA_KV_HEADS
MEM_HEADS = 4
WINDOW = 128
EPS = 1e-6
NEG_INF = -1e30
SCALE = HEAD_DIM ** -0.5

C_FQ = 0
C_FK = C_FQ + FOX_HEADS * HEAD_DIM
C_FV = C_FK + FOX_HEADS * HEAD_DIM
C_SQ = C_FV + FOX_HEADS * HEAD_DIM
C_SK = C_SQ + SWA_HEADS * HEAD_DIM
C_SV = C_SK + SWA_KV_HEADS * HEAD_DIM
C_MQ = C_SV + SWA_KV_HEADS * HEAD_DIM
C_FL = C_MQ + MEM_HEADS * HEAD_DIM
PROJ_W = C_FL + HEAD_DIM
FOX_W = FOX_HEADS * HEAD_DIM
REF_GROUPS = [
    (0, FOX_W, C_FQ), (FOX_W, FOX_W, C_FK), (2 * FOX_W, FOX_W, C_FV), (3 * FOX_W, FOX_HEADS, C_FL),
    (3 * FOX_W + FOX_HEADS, SWA_HEADS * HEAD_DIM, C_SQ),
    (3 * FOX_W + FOX_HEADS + SWA_HEADS * HEAD_DIM, SWA_KV_HEADS * HEAD_DIM, C_SK),
    (3 * FOX_W + FOX_HEADS + (SWA_HEADS + SWA_KV_HEADS) * HEAD_DIM, SWA_KV_HEADS * HEAD_DIM, C_SV),
    (3 * FOX_W + FOX_HEADS + (SWA_HEADS + 2 * SWA_KV_HEADS) * HEAD_DIM, MEM_HEADS * HEAD_DIM, C_MQ),
]

ADAM_LR = 0.001
ADAM_B1 = 0.9
ADAM_B2 = 0.999
ADAM_EPS = 1e-08
ADAM_WD = 0.01
ADAM_STEP = 10

V7X_VMEM_LIMIT = 56 * 1024 * 1024
N_CHIPS = 4
N_DEV = 8


def _tile(n, pref, mult=128):
    t = (min(pref, n) // mult) * mult
    while t >= mult:
        if n % t == 0:
            return t
        t -= mult
    return n


def _params(sem):
    return pltpu.CompilerParams(dimension_semantics=sem, vmem_limit_bytes=V7X_VMEM_LIMIT)


_DIMS = {"nn": (((1,), (0,)), ((), ())), "nt": (((1,), (1,)), ((), ())), "tn": (((0,), (0,)), ((), ()))}


def _dot(a, b, mode):
    return lax.dot_general(a, b, _DIMS[mode], preferred_element_type=F32)


def _mm(name, grid, pairs, acc_of, acc_shapes, extras, outs, epilogue, after=None):
    n_p, n_e, n_o, n_a = len(pairs), len(extras), len(outs), len(acc_shapes)
    n_w = 0 if after is None else 1
    nk = grid[2]
    n_in = sum(1 if a is None else 2 for a, *_ in pairs)

    def body(*refs):
        ex = refs[n_in:n_in + n_e]
        out = refs[n_in + n_e + n_w:n_in + n_e + n_w + n_o]
        accs = refs[n_in + n_e + n_w + n_o:]
        parts = [None] * n_a
        at = 0
        for p in range(n_p):
            if pairs[p][0] is None:
                a_ref, b_ref = refs[0], refs[at]
                at += 1
            else:
                a_ref, b_ref = refs[at], refs[at + 1]
                at += 2
            d = _dot(a_ref[...], b_ref[...], pairs[p][4])
            parts[acc_of[p]] = d if parts[acc_of[p]] is None else parts[acc_of[p]] + d

        def finish(vals):
            for o, r in zip(out, epilogue(vals, [e[...] for e in ex])):
                o[...] = r.astype(o.dtype)

        if nk == 1:
            finish(parts)
            return
        k = pl.program_id(2)

        @pl.when(k == 0)
        def _():
            for a, d in zip(accs, parts):
                a[...] = d

        @pl.when((k > 0) & (k < nk - 1))
        def _():
            for a, d in zip(accs, parts):
                a[...] += d

        @pl.when(k == nk - 1)
        def _():
            finish([a[...] + d for a, d in zip(accs, parts)])

    in_specs, args = [], []
    for a, a_spec, b, b_spec, _ in pairs:
        if a is not None:
            in_specs.append(a_spec)
            args.append(a)
        in_specs.append(b_spec)
        args.append(b)
    for e, e_spec in extras:
        in_specs.append(e_spec)
        args.append(e)
    if after is not None:
        in_specs.append(pl.BlockSpec(memory_space=pl.ANY))
        args.append(after)
    res = pl.pallas_call(
        body, name=name, grid=grid, in_specs=in_specs,
        out_specs=[s for _, s in outs], out_shape=[o for o, _ in outs],
        scratch_shapes=[pltpu.VMEM(s, F32) for s in acc_shapes] if nk > 1 else [],
        compiler_params=_params(("parallel", "parallel", "arbitrary")),
    )(*args)
    return res


def _mm2d(name, a, b, mode, out_dtype, tm=512, tn=1024, tk=1024, extras=(), epilogue=None, n_out=1, after=None, n_outer=False):
    if mode == "nn":
        (M, K), N = a.shape, b.shape[1]
    elif mode == "nt":
        (M, K), N = a.shape, b.shape[0]
    else:
        (K, M), N = a.shape, b.shape[1]
    tm, tn, tk = _tile(M, tm), _tile(N, tn), _tile(K, tk)

    def spec(shape, index):
        if n_outer:
            return pl.BlockSpec(shape, lambda j, i, k: index(i, j, k))
        return pl.BlockSpec(shape, index)

    a_spec = spec((tk, tm), lambda i, j, k: (k, i)) if mode == "tn" else spec((tm, tk), lambda i, j, k: (i, k))
    b_spec = spec((tn, tk), lambda i, j, k: (j, k)) if mode == "nt" else spec((tk, tn), lambda i, j, k: (k, j))
    mn = spec((tm, tn), lambda i, j, k: (i, j))
    if epilogue is None:
        epilogue = lambda accs, ex: [accs[0]]
    if not isinstance(out_dtype, (list, tuple)):
        out_dtype = [out_dtype] * n_out
    grid = (N // tn, M // tm, K // tk) if n_outer else (M // tm, N // tn, K // tk)
    res = _mm(name, grid, [(a, a_spec, b, b_spec, mode)], [0], [(tm, tn)],
              [(e, mn) for e in extras], [(jax.ShapeDtypeStruct((M, N), d), mn) for d in out_dtype], epilogue, after=after)
    return res[0] if len(res) == 1 else res


def _sigmoid(x):
    return 1.0 / (1.0 + jnp.exp(-x))


def _sigmoid_fast(x):
    return pl.reciprocal(1.0 + jnp.exp(-x), approx=True)


def _ffn_gu(name, h, wg, wu):
    T, D = h.shape
    nf, _, F4 = wg.shape
    tm, tk = _tile(T, 512), _tile(D, 2048)
    a_spec = pl.BlockSpec((tm, tk), lambda j, i, k: (i, k))
    b_spec = pl.BlockSpec((None, tk, F4), lambda j, i, k: (j, k, 0))
    o_spec = pl.BlockSpec((tm, F4), lambda j, i, k: (i, j))

    def epilogue(accs, ex):
        g, u = accs
        return [g, u, g * _sigmoid_fast(g) * u]

    sds = jax.ShapeDtypeStruct((T, nf * F4), BF16)
    return _mm(name, (nf, T // tm, D // tk), [(h, a_spec, wg, b_spec, "nn"), (None, None, wu, b_spec, "nn")], [0, 1],
               [(tm, F4), (tm, F4)], [], [(sds, o_spec)] * 3, epilogue)


def _ffn_down(name, a, wd, xres):
    return _mm2d(name, a, wd, "nn", F32, tm=1024, tk=wd.shape[0] // N_CHIPS, extras=[xres],
                 epilogue=lambda accs, ex: [ex[0] + 0.5 * accs[0]])


def _ffn_bwd_w(tag, dyb, h, g, u, a, wd, nf):
    T, D = h.shape
    F4 = wd.shape[0] // nf

    def act_bwd(accs, ex):
        gf, uf = ex[0].astype(F32), ex[1].astype(F32)
        s = _sigmoid_fast(gf)
        return [accs[0] * uf * s * (1.0 + gf * (1.0 - s)), accs[0] * gf * s]

    dg, du = _mm2d(tag + "_da", dyb, wd, "nt", BF16, tn=F4, tk=2048, extras=[g, u], epilogue=act_bwd, n_out=2, n_outer=True)
    dwd = _mm2d(tag + "_dwd", a, dyb, "tn", F32, tm=F4, tk=1024)

    tm, tk = _tile(D, 512), _tile(T, 1024)
    h_spec = pl.BlockSpec((tk, tm), lambda i, j, k: (k, i))
    d_spec = pl.BlockSpec((tk, F4), lambda i, j, k: (k, j))
    w_spec = pl.BlockSpec((None, tm, F4), lambda i, j, k: (j, i, 0))
    sds = jax.ShapeDtypeStruct((nf, D, F4), F32)
    dwg, dwu = _mm(tag + "_dwgu", (D // tm, nf, T // tk), [(h, h_spec, dg, d_spec, "tn"), (None, None, du, d_spec, "tn")],
                   [0, 1], [(tm, F4), (tm, F4)], [], [(sds, w_spec)] * 2, lambda accs, ex: accs)
    return dg, du, dwg, dwu, dwd


def _ffn_bwd_x(tag, dg, du, wg, wu, after):
    T = dg.shape[0]
    nf, D, F4 = wg.shape
    tm, tn = _tile(T, 512), _tile(D, 1024)
    a_spec = pl.BlockSpec((tm, F4), lambda i, j, k: (i, k))
    b_spec = pl.BlockSpec((None, tn, F4), lambda i, j, k: (k, j, 0))
    o_spec = pl.BlockSpec((tm, tn), lambda i, j, k: (i, j))
    (dh,) = _mm(tag + "_dh", (T // tm, D // tn, nf), [(dg, a_spec, wg, b_spec, "nt"), (du, a_spec, wu, b_spec, "nt")],
                [0, 0], [(tm, tn)], [], [(jax.ShapeDtypeStruct((T, D), F32), o_spec)], lambda accs, ex: accs, after=after)
    return dh


def _rms_fwd(name, x, gain):
    R, D = x.shape
    tr = _tile(R, 256, 8)

    def body(x_ref, g_ref, o_ref):
        xv = x_ref[...]
        r = lax.rsqrt(jnp.mean(xv * xv, axis=-1, keepdims=True) + EPS)
        o_ref[...] = (xv * r * g_ref[...]).astype(BF16)

    return pl.pallas_call(
        body, name=name, grid=(R // tr,),
        in_specs=[pl.BlockSpec((tr, D), lambda i: (i, 0)), pl.BlockSpec((1, D), lambda i: (0, 0))],
        out_specs=pl.BlockSpec((tr, D), lambda i: (i, 0)), out_shape=jax.ShapeDtypeStruct((R, D), BF16),
        compiler_params=_params(("parallel",)),
    )(x, gain)


def _rms_bwd(name, dh, x, gain, dres, bscale):
    R, D = x.shape
    tr = _tile(R, 256, 8)

    def body(dh_ref, x_ref, g_ref, dres_ref, dx_ref, dxb_ref, dg_ref):
        xv, dy = x_ref[...], dh_ref[...]
        r = lax.rsqrt(jnp.mean(xv * xv, axis=-1, keepdims=True) + EPS)
        xn = xv * r
        uu = dy * g_ref[...]
        dx = dres_ref[...] + r * (uu - xn * jnp.mean(xn * uu, axis=-1, keepdims=True))
        dx_ref[...] = dx
        dxb_ref[...] = (bscale * dx).astype(BF16)
        part = jnp.sum(dy * xn, axis=0, keepdims=True)

        @pl.when(pl.program_id(0) == 0)
        def _():
            dg_ref[...] = part

        @pl.when(pl.program_id(0) > 0)
        def _():
            dg_ref[...] += part

    row = pl.BlockSpec((tr, D), lambda i: (i, 0))
    vec = pl.BlockSpec((1, D), lambda i: (0, 0))
    return pl.pallas_call(
        body, name=name, grid=(R // tr,), in_specs=[row, row, vec, row], out_specs=[row, row, vec],
        out_shape=[jax.ShapeDtypeStruct((R, D), F32), jax.ShapeDtypeStruct((R, D), BF16), jax.ShapeDtypeStruct((1, D), F32)],
        compiler_params=_params(("arbitrary",)),
    )(dh, x, gain, dres)


def _loss_head(y, target):
    R, D = y.shape
    tr = _tile(R, 256, 8)

    def body(y_ref, t_ref, d_ref, db_ref, l_ref):
        e = y_ref[...] - t_ref[...]
        d = e * (1.0 / D)
        d_ref[...] = d
        db_ref[...] = (0.5 * d).astype(BF16)
        part = jnp.zeros((8, 128), F32) + (0.5 / D) * jnp.sum(e * e)

        @pl.when(pl.program_id(0) == 0)
        def _():
            l_ref[...] = part

        @pl.when(pl.program_id(0) > 0)
        def _():
            l_ref[...] += part

    row = pl.BlockSpec((tr, D), lambda i: (i, 0))
    acc = pl.BlockSpec((8, 128), lambda i: (0, 0))
    return pl.pallas_call(
        body, name="loss_head", grid=(R // tr,), in_specs=[row, row], out_specs=[row, row, acc],
        out_shape=[jax.ShapeDtypeStruct((R, D), F32), jax.ShapeDtypeStruct((R, D), BF16), jax.ShapeDtypeStruct((8, 128), F32)],
        compiler_params=_params(("arbitrary",)),
    )(y, target)


def _head_norm(xs, g):
    r = lax.rsqrt(jnp.mean(xs * xs, axis=-1, keepdims=True) + EPS)
    return xs * r * g


def _head_norm_bwd(xs, g, dy):
    r = lax.rsqrt(jnp.mean(xs * xs, axis=-1, keepdims=True) + EPS)
    xn = xs * r
    uu = dy * g
    return r * (uu - xn * jnp.mean(xn * uu, axis=-1, keepdims=True)), jnp.sum(dy * xn, axis=0, keepdims=True)


NORMED = [(C_FQ, FOX_HEADS, 0), (C_FK, FOX_HEADS, 1), (C_SQ, SWA_HEADS, 2), (C_SK, SWA_KV_HEADS, 3), (C_MQ, MEM_HEADS, 4)]
PLAIN = [(C_FV, FOX_HEADS), (C_SV, SWA_KV_HEADS)]


def _prep_fwd(proj, gains):
    T = proj.shape[0]
    tr = _tile(T, 256, 8)

    def body(p_ref, g_ref, o_ref, lf_ref):
        for start, heads, row in NORMED:
            gn = g_ref[row:row + 1, :]
            for hh in range(heads):
                sl = slice(start + hh * HEAD_DIM, start + (hh + 1) * HEAD_DIM)
                o_ref[:, sl] = _head_norm(p_ref[:, sl], gn).astype(BF16)
        for start, heads in PLAIN:
            sl = slice(start, start + heads * HEAD_DIM)
            o_ref[:, sl] = p_ref[:, sl].astype(BF16)
        zb = p_ref[:, C_FL:C_FL + HEAD_DIM] + g_ref[5:6, :]
        o_ref[:, C_FL:C_FL + HEAD_DIM] = jnp.zeros((tr, HEAD_DIM), BF16)
        lf_ref[...] = jnp.minimum(zb, 0.0) - jnp.log(1.0 + jnp.exp(-jnp.abs(zb)))

    return pl.pallas_call(
        body, name="prep_fwd", grid=(T // tr,),
        in_specs=[pl.BlockSpec((tr, PROJ_W), lambda i: (i, 0)), pl.BlockSpec((8, 128), lambda i: (0, 0))],
        out_specs=[pl.BlockSpec((tr, PROJ_W), lambda i: (i, 0)), pl.BlockSpec((tr, HEAD_DIM), lambda i: (i, 0))],
        out_shape=[jax.ShapeDtypeStruct((T, PROJ_W), BF16), jax.ShapeDtypeStruct((T, HEAD_DIM), F32)],
        compiler_params=_params(("parallel",)),
    )(proj, gains)


def _prep_bwd(proj, gains, dfq, dfk, dfv, dsq, dsk, dsv, dmq, dlogf):
    T = proj.shape[0]
    tr = _tile(T, 256, 8)
    d_normed = {C_FQ: 0, C_FK: 1, C_SQ: 3, C_SK: 4, C_MQ: 6}
    d_plain = {C_FV: 2, C_SV: 5}

    def body(p_ref, g_ref, *rest):
        d_refs, dlf_ref, o_ref, dg_ref = rest[:7], rest[7], rest[8], rest[9]
        rows = []
        for start, heads, row in NORMED:
            gn = g_ref[row:row + 1, :]
            d_ref = d_refs[d_normed[start]]
            tot = jnp.zeros((1, HEAD_DIM), F32)
            for hh in range(heads):
                sl = slice(start + hh * HEAD_DIM, start + (hh + 1) * HEAD_DIM)
                dx, dgn = _head_norm_bwd(p_ref[:, sl], gn, d_ref[:, hh * HEAD_DIM:(hh + 1) * HEAD_DIM])
                o_ref[:, sl] = dx.astype(BF16)
                tot = tot + dgn
            rows.append(tot)
        for start, heads in PLAIN:
            o_ref[:, start:start + heads * HEAD_DIM] = d_refs[d_plain[start]][...].astype(BF16)
        zb = p_ref[:, C_FL:C_FL + HEAD_DIM] + g_ref[5:6, :]
        lane = lax.broadcasted_iota(jnp.int32, (tr, HEAD_DIM), 1)
        dz = jnp.where(lane < FOX_HEADS, dlf_ref[...] * (1.0 - _sigmoid(zb)), 0.0)
        o_ref[:, C_FL:C_FL + HEAD_DIM] = dz.astype(BF16)
        rows.append(jnp.sum(dz, axis=0, keepdims=True))
        part = jnp.concatenate(rows + [jnp.zeros((2, HEAD_DIM), F32)], axis=0)

        @pl.when(pl.program_id(0) == 0)
        def _():
            dg_ref[...] = part

        @pl.when(pl.program_id(0) > 0)
        def _():
            dg_ref[...] += part

    def rows_of(w):
        return pl.BlockSpec((tr, w), lambda i: (i, 0))

    small = pl.BlockSpec((8, 128), lambda i: (0, 0))
    ds = [dfq, dfk, dfv, dsq, dsk, dsv, dmq]
    return pl.pallas_call(
        body, name="prep_bwd", grid=(T // tr,),
        in_specs=[rows_of(PROJ_W), small] + [rows_of(d.shape[1]) for d in ds] + [rows_of(HEAD_DIM)],
        out_specs=[rows_of(PROJ_W), small],
        out_shape=[jax.ShapeDtypeStruct((T, PROJ_W), BF16), jax.ShapeDtypeStruct((8, 128), F32)],
        compiler_params=_params(("arbitrary",)),
    )(proj, gains, *ds, dlogf)


def _head_norm_rows(x, gain):
    R, W = x.shape

    def body(x_ref, g_ref, o_ref):
        for hh in range(W // HEAD_DIM):
            sl = slice(hh * HEAD_DIM, (hh + 1) * HEAD_DIM)
            o_ref[:, sl] = _head_norm(x_ref[:, sl], g_ref[...]).astype(BF16)

    return pl.pallas_call(body, name="mem_k_norm", out_shape=jax.ShapeDtypeStruct((R, W), BF16))(x, gain)


def _head_norm_rows_bwd(x, gain, dy):
    R, W = x.shape

    def body(x_ref, g_ref, dy_ref, dx_ref, dg_ref):
        tot = jnp.zeros((1, HEAD_DIM), F32)
        for hh in range(W // HEAD_DIM):
            sl = slice(hh * HEAD_DIM, (hh + 1) * HEAD_DIM)
            dx, dgn = _head_norm_bwd(x_ref[:, sl], g_ref[...], dy_ref[:, sl])
            dx_ref[:, sl] = dx.astype(BF16)
            tot = tot + dgn
        dg_ref[...] = tot

    return pl.pallas_call(
        body, name="mem_k_norm_bwd",
        out_shape=[jax.ShapeDtypeStruct((R, W), BF16), jax.ShapeDtypeStruct((1, HEAD_DIM), F32)])(x, gain, dy)


def _cumsum_rows(name, x, reverse):
    T, W = x.shape
    tb = _tile(T, 512, 8)
    nb = T // tb

    def body(x_ref, o_ref, carry):
        @pl.when(pl.program_id(0) == 0)
        def _():
            carry[...] = jnp.zeros_like(carry)

        xv = x_ref[...]
        r = lax.broadcasted_iota(jnp.int32, (tb, tb), 0)
        cc = lax.broadcasted_iota(jnp.int32, (tb, tb), 1)
        tri = jnp.where((cc >= r) if reverse else (cc <= r), 1.0, 0.0).astype(F32)
        o_ref[...] = jnp.dot(tri, xv, precision=lax.Precision.HIGHEST, preferred_element_type=F32) + carry[...]
        carry[...] += jnp.sum(xv, axis=0, keepdims=True)

    idx = (lambda i: (nb - 1 - i, 0)) if reverse else (lambda i: (i, 0))
    return pl.pallas_call(
        body, name=name, grid=(nb,), in_specs=[pl.BlockSpec((tb, W), idx)], out_specs=pl.BlockSpec((tb, W), idx),
        out_shape=jax.ShapeDtypeStruct((T, W), F32), scratch_shapes=[pltpu.VMEM((1, W), F32)],
        compiler_params=_params(("arbitrary",)),
    )(x)


def _triangle(nq, by_column):
    if by_column:
        blocks = [(i, j) for j in range(nq) for i in range(j, nq)]
    else:
        blocks = [(i, j) for i in range(nq) for j in range(i + 1)]
    return jnp.asarray(np.array(blocks, np.int32).T)


def _fox_scores(q, k, cq, ck, on_diagonal):
    s = _dot(q, k, "nt") * SCALE + (cq - ck)
    if on_diagonal:
        n = s.shape[0]
        s = jnp.where(lax.broadcasted_iota(jnp.int32, (n, n), 1) <= lax.broadcasted_iota(jnp.int32, (n, n), 0), s, NEG_INF)
    return s


def _fox_fwd(qkv, cq, ck):
    T = qkv.shape[0]
    tq = _tile(T, 512)
    nq = T // tq
    steps = nq * (nq + 1) // 2
    HQ, HK, HV = C_FQ // HEAD_DIM, C_FK // HEAD_DIM, C_FV // HEAD_DIM

    def body(tab, q_ref, k_ref, v_ref, cq_ref, ck_ref, o_ref, of_ref, lse_ref, m_sc, l_sc, acc_sc):
        i, j = tab[0, pl.program_id(1)], tab[1, pl.program_id(1)]

        @pl.when(j == 0)
        def _():
            m_sc[...] = jnp.full_like(m_sc, NEG_INF)
            l_sc[...] = jnp.zeros_like(l_sc)
            acc_sc[...] = jnp.zeros_like(acc_sc)

        def step(on_diagonal):
            s = _fox_scores(q_ref[...], k_ref[...], cq_ref[...], ck_ref[...], on_diagonal)
            m_new = jnp.maximum(m_sc[...], jnp.max(s, axis=-1, keepdims=True))
            alpha = jnp.exp(m_sc[...] - m_new)
            p = jnp.exp(s - m_new)
            l_sc[...] = alpha * l_sc[...] + jnp.sum(p, axis=-1, keepdims=True)
            acc_sc[...] = alpha * acc_sc[...] + _dot(p.astype(BF16), v_ref[...], "nn")
            m_sc[...] = m_new

        @pl.when(j < i)
        def _():
            step(False)

        @pl.when(j == i)
        def _():
            step(True)
            o = acc_sc[...] / l_sc[...]
            o_ref[...] = o.astype(BF16)
            of_ref[...] = o
            lse_ref[...] = m_sc[...] + jnp.log(l_sc[...])

    def rows(base):
        return pl.BlockSpec((tq, HEAD_DIM), lambda h, s, tab: (tab[0, s], base + h))

    def cols(base):
        return pl.BlockSpec((tq, HEAD_DIM), lambda h, s, tab: (tab[1, s], base + h))

    rvec = pl.BlockSpec((None, tq, 1), lambda h, s, tab: (h, tab[0, s], 0))
    return pl.pallas_call(
        body, name="fox_fwd",
        grid_spec=pltpu.PrefetchScalarGridSpec(
            num_scalar_prefetch=1, grid=(FOX_HEADS, steps),
            in_specs=[rows(HQ), cols(HK), cols(HV), rvec, pl.BlockSpec((None, 1, tq), lambda h, s, tab: (h, 0, tab[1, s]))],
            out_specs=[rows(0), rows(0), rvec],
            scratch_shapes=[pltpu.VMEM((tq, 1), F32), pltpu.VMEM((tq, 1), F32), pltpu.VMEM((tq, HEAD_DIM), F32)]),
        out_shape=[jax.ShapeDtypeStruct((T, FOX_W), BF16), jax.ShapeDtypeStruct((T, FOX_W), F32),
                   jax.ShapeDtypeStruct((FOX_HEADS, T, 1), F32)],
        compiler_params=_params(("parallel", "arbitrary")),
    )(_triangle(nq, False), qkv, qkv, qkv, cq, ck)


def _fox_bwd(qkv, cq, ck, out_f32, lse, dmix):
    T = qkv.shape[0]
    tq = _tile(T, 512)
    nq = T // tq
    steps = nq * (nq + 1) // 2
    HQ, HK, HV = C_FQ // HEAD_DIM, C_FK // HEAD_DIM, C_FV // HEAD_DIM

    def body(tab, q_ref, k_ref, v_ref, cq_ref, ck_ref, o_ref, lse_ref, do_ref, dq_ref, dk_ref, dv_ref, dck_ref, dk_sc, dv_sc, dc_sc):
        qi, kj = tab[0, pl.program_id(1)], tab[1, pl.program_id(1)]

        @pl.when(qi == kj)
        def _():
            dk_sc[...] = jnp.zeros_like(dk_sc)
            dv_sc[...] = jnp.zeros_like(dv_sc)
            dc_sc[...] = jnp.zeros_like(dc_sc)

        def step(on_diagonal):
            q, k, v, do = q_ref[...], k_ref[...], v_ref[...], do_ref[...]
            p = jnp.exp(_fox_scores(q, k, cq_ref[...], ck_ref[...], on_diagonal) - lse_ref[...])
            dp = _dot(do, v, "nt")
            delta = jnp.sum(do.astype(F32) * o_ref[...], axis=-1, keepdims=True)
            ds = p * (dp - delta)
            dsb = ds.astype(BF16)
            dv_sc[...] += _dot(p.astype(BF16), do, "tn")
            dk_sc[...] += _dot(dsb, q, "tn")
            dc_sc[...] += jnp.sum(ds, axis=0, keepdims=True)
            dq_part = _dot(dsb, k, "nn") * SCALE
            rows_q = pl.ds(pl.multiple_of(qi * tq, tq), tq)

            @pl.when(kj == 0)
            def _():
                dq_ref[rows_q, :] = dq_part

            @pl.when(kj > 0)
            def _():
                dq_ref[rows_q, :] += dq_part

        @pl.when(qi > kj)
        def _():
            step(False)

        @pl.when(qi == kj)
        def _():
            step(True)

        @pl.when(qi == nq - 1)
        def _():
            dk_ref[...] = dk_sc[...] * SCALE
            dv_ref[...] = dv_sc[...]
            dck_ref[...] = -dc_sc[...]

    def rows(base):
        return pl.BlockSpec((tq, HEAD_DIM), lambda h, s, tab: (tab[0, s], base + h))

    def cols(base):
        return pl.BlockSpec((tq, HEAD_DIM), lambda h, s, tab: (tab[1, s], base + h))

    rvec = pl.BlockSpec((None, tq, 1), lambda h, s, tab: (h, tab[0, s], 0))
    cvec = pl.BlockSpec((None, 1, tq), lambda h, s, tab: (h, 0, tab[1, s]))
    sds = jax.ShapeDtypeStruct((T, FOX_W), F32)
    return pl.pallas_call(
        body, name="fox_bwd",
        grid_spec=pltpu.PrefetchScalarGridSpec(
            num_scalar_prefetch=1, grid=(FOX_HEADS, steps),
            in_specs=[rows(HQ), cols(HK), cols(HV), rvec, cvec, rows(0), rvec, rows(0)],
            out_specs=[pl.BlockSpec((T, HEAD_DIM), lambda h, s, tab: (0, h)), cols(0), cols(0), cvec],
            scratch_shapes=[pltpu.VMEM((tq, HEAD_DIM), F32), pltpu.VMEM((tq, HEAD_DIM), F32), pltpu.VMEM((1, tq), F32)]),
        out_shape=[sds, sds, sds, jax.ShapeDtypeStruct((FOX_HEADS, 1, T), F32)],
        compiler_params=_params(("parallel", "arbitrary")),
    )(_triangle(nq, True), qkv, qkv, qkv, cq, ck, out_f32, lse, dmix)


GW = SWA_GROUP * HEAD_DIM
GR = SWA_GROUP * WINDOW


def _swa_scores(q_ref, kp_ref, kc_ref, slope_ref, n):
    q = q_ref[...]
    qs = jnp.concatenate([q[:, t * HEAD_DIM:(t + 1) * HEAD_DIM] for t in range(SWA_GROUP)], axis=0)
    kb = jnp.concatenate([kp_ref[...], kc_ref[...]], axis=0)
    r = lax.broadcasted_iota(jnp.int32, (GR, 2 * WINDOW), 0) & (WINDOW - 1)
    jj = lax.broadcasted_iota(jnp.int32, (GR, 2 * WINDOW), 1)
    dist = WINDOW + r - jj
    valid = (dist >= 0) & (dist < WINDOW) & ((n > 0) | (jj >= WINDOW))
    s = _dot(qs, kb, "nt") * SCALE - slope_ref[...] * dist.astype(F32)
    return qs, kb, jnp.where(valid, s, NEG_INF), valid


def _swa_specs():
    HQ, HK, HV = C_SQ // GW, C_SK // HEAD_DIM, C_SV // HEAD_DIM
    q_spec = pl.BlockSpec((WINDOW, GW), lambda g, n: (n, HQ + g))

    def prev(base):
        return pl.BlockSpec((WINDOW, HEAD_DIM), lambda g, n: (jnp.maximum(n - 1, 0), base + g))

    def cur(base):
        return pl.BlockSpec((WINDOW, HEAD_DIM), lambda g, n: (n, base + g))

    col = pl.BlockSpec((None, GR, 1), lambda g, n: (g, 0, 0))
    return q_spec, prev(HK), cur(HK), prev(HV), cur(HV), col


def _swa_fwd(qkv, slopes, sinks):
    T = qkv.shape[0]
    nb = T // WINDOW
    assert C_SQ % GW == 0

    def body(q_ref, kp_ref, kc_ref, vp_ref, vc_ref, slope_ref, sink_ref, o_ref, lse_ref):
        n = pl.program_id(1)
        _, _, s, _ = _swa_scores(q_ref, kp_ref, kc_ref, slope_ref, n)
        m = jnp.maximum(jnp.max(s, axis=-1, keepdims=True), sink_ref[...])
        p = jnp.exp(s - m)
        l = jnp.sum(p, axis=-1, keepdims=True) + jnp.exp(sink_ref[...] - m)
        vb = jnp.concatenate([vp_ref[...], vc_ref[...]], axis=0)
        o = _dot(p.astype(BF16), vb, "nn") / l
        for t in range(SWA_GROUP):
            o_ref[:, t * HEAD_DIM:(t + 1) * HEAD_DIM] = o[t * WINDOW:(t + 1) * WINDOW, :].astype(BF16)
        lse_ref[...] = m + jnp.log(l)

    q_spec, kp, kc, vp, vc, col = _swa_specs()
    return pl.pallas_call(
        body, name="swa_fwd", grid=(SWA_KV_HEADS, nb), in_specs=[q_spec, kp, kc, vp, vc, col, col],
        out_specs=[pl.BlockSpec((WINDOW, GW), lambda g, n: (n, g)), pl.BlockSpec((None, None, GR, 1), lambda g, n: (g, n, 0, 0))],
        out_shape=[jax.ShapeDtypeStruct((T, SWA_HEADS * HEAD_DIM), BF16), jax.ShapeDtypeStruct((SWA_KV_HEADS, nb, GR, 1), F32)],
        compiler_params=_params(("parallel", "arbitrary")),
    )(qkv, qkv, qkv, qkv, qkv, slopes, sinks)


def _swa_bwd(qkv, slopes, sinks, out, lse, dmix):
    T = qkv.shape[0]
    nb = T // WINDOW
    DO = FOX_W // GW
    assert FOX_W % GW == 0

    def body(q_ref, kp_ref, kc_ref, vp_ref, vc_ref, slope_ref, sink_ref, o_ref, lse_ref, do_ref,
             dq_ref, dk_ref, dv_ref, dsink_ref, sink_sc):
        n = pl.program_id(1)

        @pl.when(n == 0)
        def _():
            dk_ref[...] = jnp.zeros_like(dk_ref)
            dv_ref[...] = jnp.zeros_like(dv_ref)
            sink_sc[...] = jnp.zeros_like(sink_sc)

        qs, kb, s, valid = _swa_scores(q_ref, kp_ref, kc_ref, slope_ref, n)
        lse = lse_ref[...]
        p = jnp.where(valid, jnp.exp(s - lse), 0.0)
        vb = jnp.concatenate([vp_ref[...], vc_ref[...]], axis=0)
        do = jnp.concatenate([do_ref[:, t * HEAD_DIM:(t + 1) * HEAD_DIM] for t in range(SWA_GROUP)], axis=0)
        oo = jnp.concatenate([o_ref[:, t * HEAD_DIM:(t + 1) * HEAD_DIM] for t in range(SWA_GROUP)], axis=0)
        dp = _dot(do, vb, "nt")
        delta = jnp.sum(do.astype(F32) * oo.astype(F32), axis=-1, keepdims=True)
        ds = p * (dp - delta)
        dsb = ds.astype(BF16)
        dq = _dot(dsb, kb, "nn") * SCALE
        for t in range(SWA_GROUP):
            dq_ref[:, t * HEAD_DIM:(t + 1) * HEAD_DIM] = dq[t * WINDOW:(t + 1) * WINDOW, :]
        dkb = _dot(dsb, qs, "tn") * SCALE
        dvb = _dot(p.astype(BF16), do, "tn")
        r_prev = pl.ds(pl.multiple_of(jnp.maximum(n - 1, 0) * WINDOW, WINDOW), WINDOW)
        r_cur = pl.ds(pl.multiple_of(n * WINDOW, WINDOW), WINDOW)
        dk_ref[r_prev, :] += dkb[:WINDOW, :]
        dk_ref[r_cur, :] += dkb[WINDOW:, :]
        dv_ref[r_prev, :] += dvb[:WINDOW, :]
        dv_ref[r_cur, :] += dvb[WINDOW:, :]
        sink_sc[...] -= jnp.exp(sink_ref[...] - lse) * delta

        @pl.when(n == nb - 1)
        def _():
            tot = [jnp.zeros((1, 128), F32) + jnp.sum(sink_sc[t * WINDOW:(t + 1) * WINDOW, :]) for t in range(SWA_GROUP)]
            dsink_ref[...] = jnp.concatenate(tot + [jnp.zeros((8 - SWA_GROUP, 128), F32)], axis=0)

    q_spec, kp, kc, vp, vc, col = _swa_specs()
    kv_acc = pl.BlockSpec((T, HEAD_DIM), lambda g, n: (0, g))
    return pl.pallas_call(
        body, name="swa_bwd", grid=(SWA_KV_HEADS, nb),
        in_specs=[q_spec, kp, kc, vp, vc, col, col, pl.BlockSpec((WINDOW, GW), lambda g, n: (n, g)),
                  pl.BlockSpec((None, None, GR, 1), lambda g, n: (g, n, 0, 0)), pl.BlockSpec((WINDOW, GW), lambda g, n: (n, DO + g))],
        out_specs=[pl.BlockSpec((WINDOW, GW), lambda g, n: (n, g)), kv_acc, kv_acc, pl.BlockSpec((None, 8, 128), lambda g, n: (g, 0, 0))],
        out_shape=[jax.ShapeDtypeStruct((T, SWA_HEADS * HEAD_DIM), F32), jax.ShapeDtypeStruct((T, SWA_KV_HEADS * HEAD_DIM), F32),
                   jax.ShapeDtypeStruct((T, SWA_KV_HEADS * HEAD_DIM), F32), jax.ShapeDtypeStruct((SWA_KV_HEADS, 8, 128), F32)],
        scratch_shapes=[pltpu.VMEM((GR, 1), F32)],
        compiler_params=_params(("parallel", "arbitrary")),
    )(qkv, qkv, qkv, qkv, qkv, slopes, sinks, out, lse, dmix)


def _mem_fwd(qkv, mk, mv):
    T, ML = qkv.shape[0], mk.shape[0]
    tq = _tile(T, 512)
    HQ = C_MQ // HEAD_DIM

    def body(q_ref, k_ref, v_ref, o_ref, lse_ref):
        s = _dot(q_ref[...], k_ref[...], "nt") * SCALE
        m = jnp.max(s, axis=-1, keepdims=True)
        p = jnp.exp(s - m)
        l = jnp.sum(p, axis=-1, keepdims=True)
        o_ref[...] = (_dot(p.astype(BF16), v_ref[...], "nn") / l).astype(BF16)
        lse_ref[...] = m + jnp.log(l)

    kv = pl.BlockSpec((ML, HEAD_DIM), lambda h, i: (0, h))
    return pl.pallas_call(
        body, name="mem_fwd", grid=(MEM_HEADS, T // tq),
        in_specs=[pl.BlockSpec((tq, HEAD_DIM), lambda h, i: (i, HQ + h)), kv, kv],
        out_specs=[pl.BlockSpec((tq, HEAD_DIM), lambda h, i: (i, h)), pl.BlockSpec((None, tq, 1), lambda h, i: (h, i, 0))],
        out_shape=[jax.ShapeDtypeStruct((T, MEM_HEADS * HEAD_DIM), BF16), jax.ShapeDtypeStruct((MEM_HEADS, T, 1), F32)],
        compiler_params=_params(("parallel", "arbitrary")),
    )(qkv, mk, mv)


def _mem_bwd(qkv, mk, mv, out, lse, dmix):
    T, ML = qkv.shape[0], mk.shape[0]
    tq = _tile(T, 512)
    HQ = C_MQ // HEAD_DIM
    DO = (FOX_W + SWA_HEADS * HEAD_DIM) // HEAD_DIM

    def body(q_ref, k_ref, v_ref, o_ref, lse_ref, do_ref, dq_ref, dk_ref, dv_ref):
        q, k, v, do = q_ref[...], k_ref[...], v_ref[...], do_ref[...]
        p = jnp.exp(_dot(q, k, "nt") * SCALE - lse_ref[...])
        dp = _dot(do, v, "nt")
        delta = jnp.sum(do.astype(F32) * o_ref[...].astype(F32), axis=-1, keepdims=True)
        dsb = (p * (dp - delta)).astype(BF16)
        dq_ref[...] = _dot(dsb, k, "nn") * SCALE
        dk_part = _dot(dsb, q, "tn") * SCALE
        dv_part = _dot(p.astype(BF16), do, "tn")

        @pl.when(pl.program_id(1) == 0)
        def _():
            dk_ref[...] = dk_part
            dv_ref[...] = dv_part

        @pl.when(pl.program_id(1) > 0)
        def _():
            dk_ref[...] += dk_part
            dv_ref[...] += dv_part

    kv = pl.BlockSpec((ML, HEAD_DIM), lambda h, i: (0, h))
    qb = pl.BlockSpec((tq, HEAD_DIM), lambda h, i: (i, h))
    return pl.pallas_call(
        body, name="mem_bwd", grid=(MEM_HEADS, T // tq),
        in_specs=[pl.BlockSpec((tq, HEAD_DIM), lambda h, i: (i, HQ + h)), kv, kv, qb,
                  pl.BlockSpec((None, tq, 1), lambda h, i: (h, i, 0)), pl.BlockSpec((tq, HEAD_DIM), lambda h, i: (i, DO + h))],
        out_specs=[qb, kv, kv],
        out_shape=[jax.ShapeDtypeStruct((T, MEM_HEADS * HEAD_DIM), F32), jax.ShapeDtypeStruct((ML, MEM_HEADS * HEAD_DIM), F32),
                   jax.ShapeDtypeStruct((ML, MEM_HEADS * HEAD_DIM), F32)],
        compiler_params=_params(("parallel", "arbitrary")),
    )(qkv, mk, mv, out, lse, dmix)


HBM = pl.BlockSpec(memory_space=pltpu.HBM)


def _place():
    x, y, c = lax.axis_index("x"), lax.axis_index("y"), lax.axis_index("c")
    chips = [(1 - x, y), (x, 1 - y), (1 - x, 1 - y)]
    return x, y, c, chips


def _remote(src, dst, send_sem, recv_sem, device):
    return pltpu.make_async_remote_copy(src_ref=src, dst_ref=dst, send_sem=send_sem, recv_sem=recv_sem,
                                        device_id=device, device_id_type=MESH)


def _place_ids():
    x, y, c = lax.axis_index("x"), lax.axis_index("y"), lax.axis_index("c")
    order = [2 * x + y, 2 * (1 - x) + y, 2 * x + (1 - y), 2 * (1 - x) + (1 - y)]
    return jnp.stack([2 * x + y, c] + order).astype(jnp.int32)


def _cast_place(name, w, ids):
    R, C = w.shape
    tr = _tile(R, 256, 16)

    def body(ids_ref, w_ref, o_ref):
        o_ref[...] = w_ref[...].astype(BF16)

    return pl.pallas_call(
        body, name=name,
        grid_spec=pltpu.PrefetchScalarGridSpec(
            num_scalar_prefetch=1, grid=(R // tr,), in_specs=[pl.BlockSpec((tr, C), lambda i, ids: (i, 0))],
            out_specs=pl.BlockSpec((None, tr, C), lambda i, ids: (ids[0], i, 0))),
        out_shape=jax.ShapeDtypeStruct((N_CHIPS, R, C), BF16), compiler_params=_params(("parallel",)),
    )(ids, w)


SEM = pl.BlockSpec(memory_space=pltpu.SEMAPHORE)
EFFECT = pltpu.SideEffectType.DATAFLOW_SIDE_EFFECTING


def _hbm(a):
    return pltpu.with_memory_space_constraint(a, pltpu.HBM)


def _gather_start(name, placed, after):
    n = len(placed)

    ns = 3 * n

    def body(*refs):
        send, recv = refs[n + 1:n + 1 + ns], refs[n + 1 + ns:n + 1 + 2 * ns]
        buf = refs[n + 1 + 2 * ns:2 * n + 1 + 2 * ns]
        token = refs[2 * n + 1 + 2 * ns]
        x, y, c, chips = _place()
        me = 2 * x + y
        for a in range(n):
            half = buf[a].shape[1] // 2
            mine = buf[a].at[me, pl.ds(c * half, half)]
            for j, (cx, cy) in enumerate(chips):
                _remote(mine, mine, send[3 * a + j], recv[3 * a + j], (cx, cy, c)).start()
        token[...] = jnp.zeros_like(token)

    res = pl.pallas_call(
        body, name=name, in_specs=[HBM] * n + [pl.BlockSpec(memory_space=pl.ANY)],
        out_specs=[SEM] * (2 * ns) + [HBM] * n + [pl.BlockSpec(memory_space=pltpu.VMEM)],
        out_shape=[pltpu.SemaphoreType.DMA(())] * (2 * ns)
        + [pltpu.HBM(s.shape, s.dtype) for s in placed] + [jax.ShapeDtypeStruct((8, 128), F32)],
        input_output_aliases={a: 2 * ns + a for a in range(n)},
        compiler_params=pltpu.CompilerParams(has_side_effects=EFFECT),
    )(*[_hbm(s) for s in placed], after)
    return list(res[:ns]), list(res[ns:2 * ns]), list(res[2 * ns:2 * ns + n]), res[2 * ns + n]


def _gather_wait(name, send, recv, bufs, after):
    n = len(bufs)

    ns = 3 * n

    def body(*refs):
        buf = refs[:n]
        send_ref, recv_ref = refs[n:n + ns], refs[n + ns:n + 2 * ns]
        x, y, c, chips = _place()
        ids = [2 * cx + cy for cx, cy in chips]
        for a in range(n):
            half = buf[a].shape[1] // 2
            for j in range(3):
                landed = buf[a].at[ids[j], pl.ds(c * half, half)]
                cp = _remote(landed, landed, send_ref[3 * a + j], recv_ref[3 * a + j], (x, y, c))
                cp.wait_send()
                cp.wait_recv()

    res = pl.pallas_call(
        body, name=name, in_specs=[HBM] * n + [SEM] * (2 * ns) + [pl.BlockSpec(memory_space=pl.ANY)], out_specs=[HBM] * n,
        out_shape=[pltpu.HBM(s.shape, s.dtype) for s in bufs], input_output_aliases={a: a for a in range(n)},
        compiler_params=pltpu.CompilerParams(has_side_effects=EFFECT),
    )(*bufs, *send, *recv, after)
    return list(res)


def _gather_forward(name, bufs):
    n = len(bufs)

    def body(*refs):
        buf = refs[n:2 * n]
        send, recv = refs[2 * n:]
        x, y, c, chips = _place()
        ids = [2 * cx + cy for cx, cy in chips]
        copies = []
        for a in range(n):
            half = buf[a].shape[1] // 2
            for j in range(3):
                landed = buf[a].at[ids[j], pl.ds(c * half, half)]
                cp = _remote(landed, landed, send.at[a, j], recv.at[a, j], (x, y, 1 - c))
                cp.start()
                copies.append(cp)
        for a in range(n):
            half = buf[a].shape[1] // 2
            for j in range(3):
                landed = buf[a].at[ids[j], pl.ds((1 - c) * half, half)]
                _remote(landed, landed, send.at[a, j], recv.at[a, j], (x, y, c)).wait_recv()
        for cp in copies:
            cp.wait_send()

    return pl.pallas_call(
        body, name=name, in_specs=[HBM] * n, out_specs=[HBM] * n,
        out_shape=[jax.ShapeDtypeStruct(s.shape, s.dtype) for s in bufs], input_output_aliases={a: a for a in range(n)},
        scratch_shapes=[pltpu.SemaphoreType.DMA((n, 3)), pltpu.SemaphoreType.DMA((n, 3))],
    )(*bufs)


def _pair_exchange(name, grads):
    n = len(grads)

    def body(*refs):
        src, theirs = refs[:n], refs[n:2 * n]
        send, recv = refs[2 * n:]
        x, y, c, chips = _place()
        order = [2 * x + y] + [2 * cx + cy for cx, cy in chips]
        copies = []
        for a in range(n):
            half = src[a].shape[1] // 2
            for j in range(N_CHIPS):
                cp = _remote(src[a].at[order[j], pl.ds((1 - c) * half, half)], theirs[a].at[j], send.at[a, j], recv.at[a, j], (x, y, 1 - c))
                cp.start()
                copies.append(cp)
        for cp in copies:
            cp.wait()

    return pl.pallas_call(
        body, name=name, in_specs=[HBM] * n, out_specs=[HBM] * n,
        out_shape=[jax.ShapeDtypeStruct((N_CHIPS, g.shape[1] // 2, g.shape[2]), g.dtype) for g in grads],
        scratch_shapes=[pltpu.SemaphoreType.DMA((n, N_CHIPS)), pltpu.SemaphoreType.DMA((n, N_CHIPS))],
    )(*grads)


def _pair_start(name, grads):
    n = len(grads)
    ns = N_CHIPS * n

    def body(*refs):
        send, recv = refs[2 * n:2 * n + ns], refs[2 * n + ns:2 * n + 2 * ns]
        src = refs[2 * n + 2 * ns:3 * n + 2 * ns]
        land = refs[3 * n + 2 * ns:4 * n + 2 * ns]
        token = refs[4 * n + 2 * ns]
        x, y, c, chips = _place()
        order = [2 * x + y] + [2 * cx + cy for cx, cy in chips]
        for a in range(n):
            half = src[a].shape[1] // 2
            for j in range(N_CHIPS):
                _remote(src[a].at[order[j], pl.ds((1 - c) * half, half)], land[a].at[j],
                        send[N_CHIPS * a + j], recv[N_CHIPS * a + j], (x, y, 1 - c)).start()
        token[...] = jnp.zeros_like(token)

    lands = [jax.ShapeDtypeStruct((N_CHIPS, g.shape[1] // 2, g.shape[2]), g.dtype) for g in grads]
    res = pl.pallas_call(
        body, name=name, in_specs=[HBM] * (2 * n),
        out_specs=[SEM] * (2 * ns) + [HBM] * (2 * n) + [pl.BlockSpec(memory_space=pltpu.VMEM)],
        out_shape=[pltpu.SemaphoreType.DMA(())] * (2 * ns) + [pltpu.HBM(g.shape, g.dtype) for g in grads]
        + [pltpu.HBM(l.shape, l.dtype) for l in lands] + [jax.ShapeDtypeStruct((8, 128), F32)],
        input_output_aliases={a: 2 * ns + a for a in range(2 * n)},
        compiler_params=pltpu.CompilerParams(has_side_effects=EFFECT),
    )(*[_hbm(g) for g in grads], *[_hbm(lax.empty(l.shape, l.dtype)) for l in lands])
    return list(res[:ns]), list(res[ns:2 * ns]), list(res[2 * ns:2 * ns + n]), list(res[2 * ns + n:2 * ns + 2 * n]), res[2 * ns + 2 * n]


def _pair_wait(name, send, recv, grads, lands, after):
    n = len(grads)
    ns = N_CHIPS * n

    def body(*refs):
        src, land = refs[:n], refs[n:2 * n]
        send_ref, recv_ref = refs[2 * n:2 * n + ns], refs[2 * n + ns:2 * n + 2 * ns]
        x, y, c, _ = _place()
        for a in range(n):
            for j in range(N_CHIPS):
                cp = _remote(land[a].at[j], land[a].at[j], send_ref[N_CHIPS * a + j], recv_ref[N_CHIPS * a + j], (x, y, c))
                cp.wait_send()
                cp.wait_recv()

    res = pl.pallas_call(
        body, name=name, in_specs=[HBM] * (2 * n) + [SEM] * (2 * ns) + [pl.BlockSpec(memory_space=pl.ANY)],
        out_specs=[HBM] * (2 * n), out_shape=[pltpu.HBM(g.shape, g.dtype) for g in grads] + [pltpu.HBM(l.shape, l.dtype) for l in lands],
        input_output_aliases={a: a for a in range(2 * n)},
        compiler_params=pltpu.CompilerParams(has_side_effects=EFFECT),
    )(*grads, *lands, *send, *recv, after)
    return list(res[:n]), list(res[n:])


def _chip_start(name, parts):
    n = len(parts)
    ns = 3 * n

    def body(*refs):
        send, recv = refs[2 * n:2 * n + ns], refs[2 * n + ns:2 * n + 2 * ns]
        src = refs[2 * n + 2 * ns:3 * n + 2 * ns]
        land = refs[3 * n + 2 * ns:4 * n + 2 * ns]
        token = refs[4 * n + 2 * ns]
        x, y, c, chips = _place()
        for a in range(n):
            for j, (cx, cy) in enumerate(chips):
                _remote(src[a].at[j], land[a].at[j], send[3 * a + j], recv[3 * a + j], (cx, cy, c)).start()
        token[...] = jnp.zeros_like(token)

    res = pl.pallas_call(
        body, name=name, in_specs=[HBM] * (2 * n),
        out_specs=[SEM] * (2 * ns) + [HBM] * (2 * n) + [pl.BlockSpec(memory_space=pltpu.VMEM)],
        out_shape=[pltpu.SemaphoreType.DMA(())] * (2 * ns) + [pltpu.HBM(p.shape, p.dtype) for p in parts] * 2
        + [jax.ShapeDtypeStruct((8, 128), F32)],
        input_output_aliases={a: 2 * ns + a for a in range(2 * n)},
        compiler_params=pltpu.CompilerParams(has_side_effects=EFFECT),
    )(*[_hbm(p) for p in parts], *[_hbm(lax.empty(p.shape, p.dtype)) for p in parts])
    return list(res[:ns]), list(res[ns:2 * ns]), list(res[2 * ns:2 * ns + n]), list(res[2 * ns + n:2 * ns + 2 * n]), res[2 * ns + 2 * n]


def _chip_wait(name, send, recv, parts, lands, after):
    n = len(parts)
    ns = 3 * n

    def body(*refs):
        src, land = refs[:n], refs[n:2 * n]
        send_ref, recv_ref = refs[2 * n:2 * n + ns], refs[2 * n + ns:2 * n + 2 * ns]
        x, y, c, _ = _place()
        for a in range(n):
            for j in range(3):
                cp = _remote(src[a].at[j], land[a].at[j], send_ref[3 * a + j], recv_ref[3 * a + j], (x, y, c))
                cp.wait_send()
                cp.wait_recv()

    res = pl.pallas_call(
        body, name=name, in_specs=[HBM] * (2 * n) + [SEM] * (2 * ns) + [pl.BlockSpec(memory_space=pl.ANY)],
        out_specs=[HBM] * (2 * n), out_shape=[pltpu.HBM(p.shape, p.dtype) for p in parts] * 2,
        input_output_aliases={a: a for a in range(2 * n)},
        compiler_params=pltpu.CompilerParams(has_side_effects=EFFECT),
    )(*parts, *lands, *send, *recv, after)
    return list(res[n:])


def _pair_share(name, shards):
    n = len(shards)

    def body(*refs):
        buf = refs[n:2 * n]
        send, recv = refs[2 * n:]
        x, y, c, _ = _place()
        copies = []
        for a in range(n):
            half = buf[a].shape[0] // 2
            mine = buf[a].at[pl.ds(c * half, half)]
            cp = _remote(mine, mine, send.at[a], recv.at[a], (x, y, 1 - c))
            cp.start()
            copies.append(cp)
        for a, cp in enumerate(copies):
            half = buf[a].shape[0] // 2
            cp.wait_send()
            theirs = buf[a].at[pl.ds((1 - c) * half, half)]
            _remote(theirs, theirs, send.at[a], recv.at[a], (x, y, c)).wait_recv()

    return pl.pallas_call(
        body, name=name, in_specs=[HBM] * n, out_specs=[HBM] * n,
        out_shape=[jax.ShapeDtypeStruct(s.shape, s.dtype) for s in shards], input_output_aliases={a: a for a in range(n)},
        scratch_shapes=[pltpu.SemaphoreType.DMA((n,)), pltpu.SemaphoreType.DMA((n,))],
    )(*shards)


def _all_reduce_small(buf):
    R, W = buf.shape

    def body(src_ref, out_ref, slots, send, recv):
        x, y, c, _ = _place()
        me = 4 * x + 2 * y + c
        copies = []
        for dx in range(2):
            for dy in range(2):
                for dc in range(2):
                    if dx == dy == dc == 0:
                        continue
                    k = 4 * dx + 2 * dy + dc
                    peer = (x ^ dx, y ^ dy, c ^ dc)
                    cp = _remote(src_ref, slots.at[me], send.at[k], recv.at[k], peer)
                    cp.start()
                    copies.append((cp, k))
        slots[me] = src_ref[...]
        for cp, k in copies:
            cp.wait_send()
            landed = slots.at[me ^ k]
            _remote(landed, landed, send.at[k], recv.at[k], (x, y, c)).wait_recv()
        total = slots[0]
        for d in range(1, N_DEV):
            total = total + slots[d]
        out_ref[...] = total

    return pl.pallas_call(
        body, name="all_reduce_small", in_specs=[pl.BlockSpec(memory_space=pltpu.VMEM)],
        out_specs=pl.BlockSpec(memory_space=pltpu.VMEM), out_shape=jax.ShapeDtypeStruct((R, W), F32),
        scratch_shapes=[pltpu.VMEM((N_DEV, R, W), F32), pltpu.SemaphoreType.DMA((N_DEV,)), pltpu.SemaphoreType.DMA((N_DEV,))],
    )(buf)


def _pair_sum_bf16(name, grad, theirs, ids):
    _, R2, C = theirs.shape
    tr = _tile(R2, 256, 16)
    nrb = R2 // tr

    def body(ids_ref, a_ref, b_ref, o_ref):
        o_ref[...] = (a_ref[...] + b_ref[...]).astype(BF16)

    return pl.pallas_call(
        body, name=name,
        grid_spec=pltpu.PrefetchScalarGridSpec(
            num_scalar_prefetch=1, grid=(3, nrb),
            in_specs=[pl.BlockSpec((None, tr, C), lambda j, i, ids: (ids[3 + j], ids[1] * nrb + i, 0)),
                      pl.BlockSpec((None, tr, C), lambda j, i, ids: (j + 1, i, 0))],
            out_specs=pl.BlockSpec((None, tr, C), lambda j, i, ids: (j, i, 0))),
        out_shape=jax.ShapeDtypeStruct((3, R2, C), BF16), compiler_params=_params(("parallel", "parallel")),
    )(ids, grad, theirs)


def _chip_sum(name, grad, theirs, arrived, ids):
    _, R2, C = theirs.shape
    tr = _tile(R2, 256, 16)
    nrb = R2 // tr

    def body(ids_ref, a_ref, b_ref, r_ref, o_ref):
        tot = a_ref[...] + b_ref[...]
        for j in range(3):
            tot = tot + r_ref[j].astype(F32)
        o_ref[...] = tot

    return pl.pallas_call(
        body, name=name,
        grid_spec=pltpu.PrefetchScalarGridSpec(
            num_scalar_prefetch=1, grid=(nrb,),
            in_specs=[pl.BlockSpec((None, tr, C), lambda i, ids: (ids[0], ids[1] * nrb + i, 0)),
                      pl.BlockSpec((None, tr, C), lambda i, ids: (0, i, 0)),
                      pl.BlockSpec((3, tr, C), lambda i, ids: (0, i, 0))],
            out_specs=pl.BlockSpec((tr, C), lambda i, ids: (ids[1] * nrb + i, 0))),
        out_shape=jax.ShapeDtypeStruct((2 * R2, C), F32), compiler_params=_params(("parallel",)),
    )(ids, grad, theirs, arrived)


def _adamw(name, w, g, m, v):
    R, C = w.shape
    tr = _tile(R, 128, 8)
    c1 = 1.0 / (1.0 - ADAM_B1 ** ADAM_STEP)
    c2 = 1.0 / (1.0 - ADAM_B2 ** ADAM_STEP)

    def body(w_ref, g_ref, m_ref, v_ref, d_ref, mo_ref, vo_ref):
        gv = g_ref[...]
        mn = ADAM_B1 * m_ref[...] + (1.0 - ADAM_B1) * gv
        vn = ADAM_B2 * v_ref[...] + (1.0 - ADAM_B2) * (gv * gv)
        d_ref[...] = -ADAM_LR * ((mn * c1) / (jnp.sqrt(vn * c2) + ADAM_EPS) + ADAM_WD * w_ref[...])
        mo_ref[...] = mn
        vo_ref[...] = vn

    spec = pl.BlockSpec((tr, C), lambda i: (i, 0))
    sds = jax.ShapeDtypeStruct((R, C), F32)
    return pl.pallas_call(body, name=name, grid=(R // tr,), in_specs=[spec] * 4, out_specs=[spec] * 3, out_shape=[sds] * 3,
                          compiler_params=_params(("parallel",)))(w, g, m, v)


SMALL = ["ffn1_norm", "mix_norm", "mem_norm", "forget_bias", "fox_q_gain", "fox_k_gain", "swa_q_gain", "swa_k_gain", "swa_sinks",
         "mem_q_gain", "mem_k_gain", "ffn2_norm"]
LARGE = ["ffn1_gate", "ffn1_up", "ffn1_down", "w_in", "w_mem_k", "w_mem_v", "w_out", "ffn2_gate", "ffn2_up", "ffn2_down"]
GATHER_GROUPS = [["ffn1_gate", "ffn1_up"], ["ffn1_down", "w_in", "w_mem_k", "w_mem_v"], ["w_out", "ffn2_gate", "ffn2_up", "ffn2_down"]]
WEIGHTS = ["ffn1_norm", "ffn1_gate", "ffn1_up", "ffn1_down", "mix_norm", "mem_norm", "w_in", "forget_bias", "w_mem_k", "w_mem_v",
           "fox_q_gain", "fox_k_gain", "swa_q_gain", "swa_k_gain", "swa_sinks", "mem_q_gain", "mem_k_gain", "w_out", "ffn2_norm",
           "ffn2_gate", "ffn2_up", "ffn2_down"]


def _pad_proj_cols(w):
    out = jnp.zeros((w.shape[0], PROJ_W), w.dtype)
    for start, width, pstart in REF_GROUPS:
        out = lax.dynamic_update_slice(out, w[:, start:start + width], (0, pstart))
    return out


def _unpad_proj_cols(w):
    return jnp.concatenate([w[:, pstart:pstart + width] for _, width, pstart in REF_GROUPS], axis=1)


def _pack_small(vals):
    flat = jnp.concatenate([vals[k].reshape(-1).astype(F32) for k in SMALL + ["loss"]])
    n = flat.shape[0]
    total = -(-n // 1024) * 1024
    return jnp.pad(flat, (0, total - n)).reshape(total // 128, 128)


def _unpack_small(buf, shapes):
    flat = buf.reshape(-1)
    out, off = {}, 0
    for k in SMALL + ["loss"]:
        size = int(np.prod(shapes[k]))
        out[k] = flat[off:off + size].reshape(shapes[k])
        off += size
    return out


def kernel(x, mem, ffn1_norm, ffn1_gate, ffn1_up, ffn1_down, mix_norm, mem_norm, w_in, forget_bias, w_mem_k, w_mem_v, fox_q_gain, fox_k_gain, swa_q_gain, swa_k_gain, swa_sinks, mem_q_gain, mem_k_gain, w_out, ffn2_norm, ffn2_gate, ffn2_up, ffn2_down, loss_target, m_ffn1_norm, m_ffn1_gate, m_ffn1_up, m_ffn1_down, m_mix_norm, m_mem_norm, m_w_in, m_forget_bias, m_w_mem_k, m_w_mem_v, m_fox_q_gain, m_fox_k_gain, m_swa_q_gain, m_swa_k_gain, m_swa_sinks, m_mem_q_gain, m_mem_k_gain, m_w_out, m_ffn2_norm, m_ffn2_gate, m_ffn2_up, m_ffn2_down, v_ffn1_norm, v_ffn1_gate, v_ffn1_up, v_ffn1_down, v_mix_norm, v_mem_norm, v_w_in, v_forget_bias, v_w_mem_k, v_w_mem_v, v_fox_q_gain, v_fox_k_gain, v_swa_q_gain, v_swa_k_gain, v_swa_sinks, v_mem_q_gain, v_mem_k_gain, v_w_out, v_ffn2_norm, v_ffn2_gate, v_ffn2_up, v_ffn2_down):
    given = dict(locals())
    T, D = x.shape[1], x.shape[2]
    ML = mem.shape[1]
    xin = x.reshape(T, D)
    target = loss_target.reshape(T, D)
    memin = mem.reshape(ML, D)

    ids = _place_ids()
    shard = {k: given[k][0] for k in LARGE}
    shard["w_in"] = _pad_proj_cols(shard["w_in"])
    started, after = [], ids
    for gi, group in enumerate(GATHER_GROUPS):
        placed = [_cast_place("cast_" + k, shard[k], ids) for k in group]
        send, recv, bufs, after = _gather_start("gather_start_%d" % gi, placed, after)
        started.append((send, recv, bufs))

    def arrive(gi, done):
        send, recv, bufs = started[gi]
        bufs = _gather_wait("gather_wait_%d" % gi, send, recv, bufs, done)
        return dict(zip(GATHER_GROUPS[gi], _gather_forward("gather_forward_%d" % gi, bufs)))

    gains = jnp.concatenate([fox_q_gain, fox_k_gain, swa_q_gain, swa_k_gain, mem_q_gain,
                             jnp.pad(forget_bias, ((0, 0), (0, HEAD_DIM - FOX_HEADS))), jnp.zeros((2, HEAD_DIM), F32)], axis=0)
    slopes_np = 2.0 ** (-8.0 * np.arange(1, SWA_HEADS + 1) / SWA_HEADS)
    slopes = jnp.asarray(np.repeat(slopes_np, WINDOW).reshape(SWA_KV_HEADS, GR, 1), F32)
    sinks = jnp.repeat(swa_sinks.reshape(SWA_HEADS), WINDOW).reshape(SWA_KV_HEADS, GR, 1)

    h1 = _rms_fwd("ffn1_norm_fwd", xin, ffn1_norm + after[0, 0])
    full = arrive(0, h1)
    wg1, wu1 = full["ffn1_gate"], full["ffn1_up"]
    g1, u1, a1 = _ffn_gu("ffn1_gate_up", h1, wg1, wu1)
    full = arrive(1, a1)
    wd1 = full["ffn1_down"].reshape(-1, D)
    win = full["w_in"].reshape(D, PROJ_W)
    wmk = full["w_mem_k"].reshape(D, MEM_HEADS * HEAD_DIM)
    wmv = full["w_mem_v"].reshape(D, MEM_HEADS * HEAD_DIM)
    x1 = _ffn_down("ffn1_down", a1, wd1, xin)
    h2 = _rms_fwd("mix_norm_fwd", x1, mix_norm)
    proj = _mm2d("proj_in", h2, win, "nn", F32, tn=1408, tk=2048, n_outer=True)
    qkv, logf = _prep_fwd(proj, gains)
    cum = _cumsum_rows("forget_cumsum", logf, False)
    cum_h = cum[:, :FOX_HEADS].T
    cq, ck = cum_h.reshape(FOX_HEADS, T, 1), cum_h.reshape(FOX_HEADS, 1, T)
    mn = _rms_fwd("mem_norm_fwd", memin, mem_norm)
    mk_raw = _mm2d("mem_k_proj", mn, wmk, "nn", F32)
    mv = _mm2d("mem_v_proj", mn, wmv, "nn", BF16)
    mk = _head_norm_rows(mk_raw, mem_k_gain)
    out_a, out_a_f32, lse_a = _fox_fwd(qkv, cq, ck)
    out_b, lse_b = _swa_fwd(qkv, slopes, sinks)
    out_c, lse_c = _mem_fwd(qkv, mk, mv)
    mixed = jnp.concatenate([out_a, out_b, out_c], axis=1)
    full = arrive(2, mixed)
    wo = full["w_out"].reshape(-1, D)
    wg2, wu2, wd2 = full["ffn2_gate"], full["ffn2_up"], full["ffn2_down"].reshape(-1, D)
    x2 = _mm2d("mix_out", mixed, wo, "nn", F32, tk=2048, n_outer=True, extras=[x1], epilogue=lambda accs, ex: [ex[0] + accs[0]])
    h3 = _rms_fwd("ffn2_norm_fwd", x2, ffn2_norm)
    g2, u2, a2 = _ffn_gu("ffn2_gate_up", h3, wg2, wu2)
    x3 = _ffn_down("ffn2_down", a2, wd2, x2)
    dx3, dyb3, loss_part = _loss_head(x3, target)

    grads, small, res = {}, {"loss": loss_part[0, 0]}, {}

    def send_off(tag, group):
        theirs = _pair_exchange("grad_pair_exchange_" + tag, [grads[k] for k in group])
        to_chips = [_pair_sum_bf16("pair_sum_" + k, grads[k], b, ids) for k, b in zip(group, theirs)]
        send, recv, parts, lands, token = _chip_start("grad_chip_start_" + tag, to_chips)
        return (group, theirs, send, recv, parts, lands), token

    def pair_off(tag, group):
        send, recv, own, lands, token = _pair_start("grad_pair_start_" + tag, [grads[k] for k in group])
        return (group, send, recv, own, lands), token

    def chip_off(tag, started, done):
        group, send, recv, own, lands = started
        own, theirs = _pair_wait("grad_pair_wait_" + tag, send, recv, own, lands, done)
        grads.update(zip(group, own))
        to_chips = [_pair_sum_bf16("pair_sum_" + k, grads[k], b, ids) for k, b in zip(group, theirs)]
        send, recv, parts, lands, token = _chip_start("grad_chip_start_" + tag, to_chips)
        return (group, theirs, send, recv, parts, lands), token

    def finish(tag, state, done):
        group, theirs, send, recv, parts, lands = state
        arrived = _chip_wait("grad_chip_wait_" + tag, send, recv, parts, lands, done)
        halves = [_chip_sum("chip_sum_" + k, grads[k], b, r, ids) for k, b, r in zip(group, theirs, arrived)]
        reduced = dict(zip(group, _pair_share("grad_pair_share_" + tag, halves)))
        last = None
        for k in group:
            gk = _unpad_proj_cols(reduced[k]) if k == "w_in" else reduced[k]
            d, mo, vo = _adamw("adamw_" + k, given[k][0], gk, given["m_" + k][0], given["v_" + k][0])
            res[k] = tuple(t[None] for t in (gk, d, mo, vo))
            last = vo
        return last

    dg2, du2, grads["ffn2_gate"], grads["ffn2_up"], dwd2 = _ffn_bwd_w("ffn2", dyb3, h3, g2, u2, a2, wd2, N_CHIPS)
    grads["ffn2_down"] = dwd2.reshape(N_CHIPS, -1, D)
    started, token = pair_off("a", ["ffn2_gate", "ffn2_up", "ffn2_down"])
    dh3 = _ffn_bwd_x("ffn2", dg2, du2, wg2, wu2, token)
    state_a, token = chip_off("a", started, dh3)
    dx2, dx2b, small["ffn2_norm"] = _rms_bwd("ffn2_norm_bwd", dh3, x2, ffn2_norm + token[0, 0], dx3, 1.0)
    dmix = _mm2d("mix_out_dx", dx2b, wo, "nt", BF16, tk=2048, n_outer=True)
    grads["w_out"] = _mm2d("mix_out_dw", mixed, dx2b, "tn", F32, tk=1024).reshape(N_CHIPS, -1, D)
    dfq, dfk, dfv, dck = _fox_bwd(qkv, cq, ck, out_a_f32, lse_a, dmix)
    dsq, dsk, dsv, dsink = _swa_bwd(qkv, slopes, sinks, out_b, lse_b, dmix)
    dmq, dmk, dmv = _mem_bwd(qkv, mk, mv, out_c, lse_c, dmix)
    small["swa_sinks"] = dsink[:, :SWA_GROUP, 0].reshape(1, SWA_HEADS)
    dcum = jnp.pad(dck.reshape(FOX_HEADS, T).T, ((0, 0), (0, HEAD_DIM - FOX_HEADS)))
    dlogf = _cumsum_rows("forget_cumsum_bwd", dcum, True)
    dproj, dgains = _prep_bwd(proj, gains, dfq, dfk, dfv, dsq, dsk, dsv, dmq, dlogf)
    for row, k in enumerate(["fox_q_gain", "fox_k_gain", "swa_q_gain", "swa_k_gain", "mem_q_gain"]):
        small[k] = dgains[row:row + 1, :]
    small["forget_bias"] = dgains[5:6, :FOX_HEADS]
    grads["w_in"] = _mm2d("proj_in_dw", h2, dproj, "tn", F32, tn=1408, tk=1024).reshape(N_CHIPS, -1, PROJ_W)
    dmk_raw, small["mem_k_gain"] = _head_norm_rows_bwd(mk_raw, mem_k_gain, dmk)
    dmvb = dmv.astype(BF16)
    grads["w_mem_k"] = _mm2d("mem_k_dw", mn, dmk_raw, "tn", F32).reshape(N_CHIPS, -1, MEM_HEADS * HEAD_DIM)
    grads["w_mem_v"] = _mm2d("mem_v_dw", mn, dmvb, "tn", F32).reshape(N_CHIPS, -1, MEM_HEADS * HEAD_DIM)
    dmn = _mm2d("mem_k_dx", dmk_raw, wmk, "nt", F32)
    dmn = _mm2d("mem_v_dx", dmvb, wmv, "nt", F32, extras=[dmn], epilogue=lambda accs, ex: [ex[0] + accs[0]])
    _, _, small["mem_norm"] = _rms_bwd("mem_norm_bwd", dmn, memin, mem_norm, jnp.zeros_like(memin), 1.0)
    started, token = pair_off("b", ["w_out", "w_in", "w_mem_k", "w_mem_v"])
    dh2 = _mm2d("proj_in_dx", dproj, win, "nt", F32, tm=1024, tk=1408, after=token)
    state_b, token = chip_off("b", started, dh2)
    dx1, dyb1, small["mix_norm"] = _rms_bwd("mix_norm_bwd", dh2, x1, mix_norm + token[0, 0], dx2, 0.5)
    dg1, du1, grads["ffn1_gate"], grads["ffn1_up"], dwd1 = _ffn_bwd_w("ffn1", dyb1, h1, g1, u1, a1, wd1, N_CHIPS)
    grads["ffn1_down"] = dwd1.reshape(N_CHIPS, -1, D)
    state_c, token_c = send_off("c", ["ffn1_gate", "ffn1_up", "ffn1_down"])
    dh1 = _ffn_bwd_x("ffn1", dg1, du1, wg1, wu1, token_c)
    grad_x, _, small["ffn1_norm"] = _rms_bwd("ffn1_norm_bwd", dh1, xin, ffn1_norm, dx1, 1.0)

    shapes = {k: given[k].shape for k in SMALL}
    shapes["loss"] = ()
    red_small = _unpack_small(_all_reduce_small(_pack_small(small)), shapes)
    loss = red_small["loss"]
    zero = {"loss": jnp.zeros((), F32)}
    packed = [_pack_small({**zero, **{k: src[k] for k in SMALL}}) for src in (
        {k: given[k] for k in SMALL}, red_small, {k: given["m_" + k] for k in SMALL}, {k: given["v_" + k] for k in SMALL})]
    small_out = _adamw("adamw_small", *packed)
    d_s, m_s, v_s = (_unpack_small(t, shapes) for t in small_out)
    for k in SMALL:
        res[k] = (red_small[k], d_s[k], m_s[k], v_s[k])

    done = finish("a", state_a, small_out[0])
    done = finish("b", state_b, done)
    finish("c", state_c, done)

    outs = [loss, grad_x.reshape(1, T, D)]
    for part in range(4):
        outs += [res[k][part] for k in WEIGHTS]
    return tuple(outs)
```

```python
import functools

import numpy as np
import jax
import jax.numpy as jnp
from jax import lax
from jax.experimental import pallas as pl
from jax.experimental.pallas import tpu as pltpu

F32 = jnp.float32
BF16 = jnp.bfloat16
MESH = pl.DeviceIdType.MESH

HEAD_DIM = 128
FOX_HEADS = 6
SWA_HEADS = 6
SWA_KV_HEADS = 2
SWA_GROUP = SWA_HEADS // SWA_KV_HEADS
MEM_HEADS = 4
WINDOW = 128
EPS = 1e-6
NEG_INF = -1e30
SCALE = HEAD_DIM ** -0.5

C_FQ = 0
C_FK = C_FQ + FOX_HEADS * HEAD_DIM
C_FV = C_FK + FOX_HEADS * HEAD_DIM
C_SQ = C_FV + FOX_HEADS * HEAD_DIM
C_SK = C_SQ + SWA_HEADS * HEAD_DIM
C_SV = C_SK + SWA_KV_HEADS * HEAD_DIM
C_MQ = C_SV + SWA_KV_HEADS * HEAD_DIM
C_FL = C_MQ + MEM_HEADS * HEAD_DIM
PROJ_W = C_FL + HEAD_DIM
FOX_W = FOX_HEADS * HEAD_DIM
REF_GROUPS = [
    (0, FOX_W, C_FQ), (FOX_W, FOX_W, C_FK), (2 * FOX_W, FOX_W, C_FV), (3 * FOX_W, FOX_HEADS, C_FL),
    (3 * FOX_W + FOX_HEADS, SWA_HEADS * HEAD_DIM, C_SQ),
    (3 * FOX_W + FOX_HEADS + SWA_HEADS * HEAD_DIM, SWA_KV_HEADS * HEAD_DIM, C_SK),
    (3 * FOX_W + FOX_HEADS + (SWA_HEADS + SWA_KV_HEADS) * HEAD_DIM, SWA_KV_HEADS * HEAD_DIM, C_SV),
    (3 * FOX_W + FOX_HEADS + (SWA_HEADS + 2 * SWA_KV_HEADS) * HEAD_DIM, MEM_HEADS * HEAD_DIM, C_MQ),
]

ADAM_LR = 0.001
ADAM_B1 = 0.9
ADAM_B2 = 0.999
ADAM_EPS = 1e-08
ADAM_WD = 0.01
ADAM_STEP = 10

V7X_VMEM_LIMIT = 56 * 1024 * 1024
N_CHIPS = 4
N_DEV = 8


def _tile(n, pref, mult=128):
    t = (min(pref, n) // mult) * mult
    while t >= mult:
        if n % t == 0:
            return t
        t -= mult
    return n


def _params(sem):
    return pltpu.CompilerParams(dimension_semantics=sem, vmem_limit_bytes=V7X_VMEM_LIMIT)


_DIMS = {"nn": (((1,), (0,)), ((), ())), "nt": (((1,), (1,)), ((), ())), "tn": (((0,), (0,)), ((), ()))}


def _dot(a, b, mode):
    return lax.dot_general(a, b, _DIMS[mode], preferred_element_type=F32)


def _mm(name, grid, pairs, acc_of, acc_shapes, extras, outs, epilogue, after=None):
    n_p, n_e, n_o, n_a = len(pairs), len(extras), len(outs), len(acc_shapes)
    n_w = 0 if after is None else 1
    nk = grid[2]
    n_in = sum(1 if a is None else 2 for a, *_ in pairs)

    def body(*refs):
        ex = refs[n_in:n_in + n_e]
        out = refs[n_in + n_e + n_w:n_in + n_e + n_w + n_o]
        accs = refs[n_in + n_e + n_w + n_o:]
        parts = [None] * n_a
        at = 0
        for p in range(n_p):
            if pairs[p][0] is None:
                a_ref, b_ref = refs[0], refs[at]
                at += 1
            else:
                a_ref, b_ref = refs[at], refs[at + 1]
                at += 2
            d = _dot(a_ref[...], b_ref[...], pairs[p][4])
            parts[acc_of[p]] = d if parts[acc_of[p]] is None else parts[acc_of[p]] + d

        def finish(vals):
            for o, r in zip(out, epilogue(vals, [e[...] for e in ex])):
                o[...] = r.astype(o.dtype)

        if nk == 1:
            finish(parts)
            return
        k = pl.program_id(2)

        @pl.when(k == 0)
        def _():
            for a, d in zip(accs, parts):
                a[...] = d

        @pl.when((k > 0) & (k < nk - 1))
        def _():
            for a, d in zip(accs, parts):
                a[...] += d

        @pl.when(k == nk - 1)
        def _():
            finish([a[...] + d for a, d in zip(accs, parts)])

    in_specs, args = [], []
    for a, a_spec, b, b_spec, _ in pairs:
        if a is not None:
            in_specs.append(a_spec)
            args.append(a)
        in_specs.append(b_spec)
        args.append(b)
    for e, e_spec in extras:
        in_specs.append(e_spec)
        args.append(e)
    if after is not None:
        in_specs.append(pl.BlockSpec(memory_space=pl.ANY))
        args.append(after)
    res = pl.pallas_call(
        body, name=name, grid=grid, in_specs=in_specs,
        out_specs=[s for _, s in outs], out_shape=[o for o, _ in outs],
        scratch_shapes=[pltpu.VMEM(s, F32) for s in acc_shapes] if nk > 1 else [],
        compiler_params=_params(("parallel", "parallel", "arbitrary")),
    )(*args)
    return res


def _mm2d(name, a, b, mode, out_dtype, tm=512, tn=1024, tk=1024, extras=(), epilogue=None, n_out=1, after=None, n_outer=False):
    if mode == "nn":
        (M, K), N = a.shape, b.shape[1]
    elif mode == "nt":
        (M, K), N = a.shape, b.shape[0]
    else:
        (K, M), N = a.shape, b.shape[1]
    tm, tn, tk = _tile(M, tm), _tile(N, tn), _tile(K, tk)

    def spec(shape, index):
        if n_outer:
            return pl.BlockSpec(shape, lambda j, i, k: index(i, j, k))
        return pl.BlockSpec(shape, index)

    a_spec = spec((tk, tm), lambda i, j, k: (k, i)) if mode == "tn" else spec((tm, tk), lambda i, j, k: (i, k))
    b_spec = spec((tn, tk), lambda i, j, k: (j, k)) if mode == "nt" else spec((tk, tn), lambda i, j, k: (k, j))
    mn = spec((tm, tn), lambda i, j, k: (i, j))
    if epilogue is None:
        epilogue = lambda accs, ex: [accs[0]]
    if not isinstance(out_dtype, (list, tuple)):
        out_dtype = [out_dtype] * n_out
    grid = (N // tn, M // tm, K // tk) if n_outer else (M // tm, N // tn, K // tk)
    res = _mm(name, grid, [(a, a_spec, b, b_spec, mode)], [0], [(tm, tn)],
              [(e, mn) for e in extras], [(jax.ShapeDtypeStruct((M, N), d), mn) for d in out_dtype], epilogue, after=after)
    return res[0] if len(res) == 1 else res


def _sigmoid(x):
    return 1.0 / (1.0 + jnp.exp(-x))


def _sigmoid_fast(x):
    return pl.reciprocal(1.0 + jnp.exp(-x), approx=True)


def _ffn_gu(name, h, wg, wu):
    T, D = h.shape
    nf, _, F4 = wg.shape
    tm, tk = _tile(T, 512), _tile(D, 2048)
    a_spec = pl.BlockSpec((tm, tk), lambda j, i, k: (i, k))
    b_spec = pl.BlockSpec((None, tk, F4), lambda j, i, k: (j, k, 0))
    o_spec = pl.BlockSpec((tm, F4), lambda j, i, k: (i, j))

    def epilogue(accs, ex):
        g, u = accs
        return [g, u, g * _sigmoid_fast(g) * u]

    sds = jax.ShapeDtypeStruct((T, nf * F4), BF16)
    return _mm(name, (nf, T // tm, D // tk), [(h, a_spec, wg, b_spec, "nn"), (None, None, wu, b_spec, "nn")], [0, 1],
               [(tm, F4), (tm, F4)], [], [(sds, o_spec)] * 3, epilogue)


def _ffn_down(name, a, wd, xres):
    return _mm2d(name, a, wd, "nn", F32, tm=1024, tk=wd.shape[0] // N_CHIPS, extras=[xres],
                 epilogue=lambda accs, ex: [ex[0] + 0.5 * accs[0]])


def _ffn_bwd_w(tag, dyb, h, g, u, a, wd, nf):
    T, D = h.shape
    F4 = wd.shape[0] // nf

    def act_bwd(accs, ex):
        gf, uf = ex[0].astype(F32), ex[1].astype(F32)
        s = _sigmoid_fast(gf)
        return [accs[0] * uf * s * (1.0 + gf * (1.0 - s)), accs[0] * gf * s]

    dg, du = _mm2d(tag + "_da", dyb, wd, "nt", BF16, tn=F4, tk=2048, extras=[g, u], epilogue=act_bwd, n_out=2, n_outer=True)
    dwd = _mm2d(tag + "_dwd", a, dyb, "tn", F32, tm=F4, tk=1024)

    tm, tk = _tile(D, 512), _tile(T, 1024)
    h_spec = pl.BlockSpec((tk, tm), lambda i, j, k: (k, i))
    d_spec = pl.BlockSpec((tk, F4), lambda i, j, k: (k, j))
    w_spec = pl.BlockSpec((None, tm, F4), lambda i, j, k: (j, i, 0))
    sds = jax.ShapeDtypeStruct((nf, D, F4), F32)
    dwg, dwu = _mm(tag + "_dwgu", (D // tm, nf, T // tk), [(h, h_spec, dg, d_spec, "tn"), (None, None, du, d_spec, "tn")],
                   [0, 1], [(tm, F4), (tm, F4)], [], [(sds, w_spec)] * 2, lambda accs, ex: accs)
    return dg, du, dwg, dwu, dwd


def _ffn_bwd_x(tag, dg, du, wg, wu, after):
    T = dg.shape[0]
    nf, D, F4 = wg.shape
    tm, tn = _tile(T, 512), _tile(D, 1024)
    a_spec = pl.BlockSpec((tm, F4), lambda i, j, k: (i, k))
    b_spec = pl.BlockSpec((None, tn, F4), lambda i, j, k: (k, j, 0))
    o_spec = pl.BlockSpec((tm, tn), lambda i, j, k: (i, j))
    (dh,) = _mm(tag + "_dh", (T // tm, D // tn, nf), [(dg, a_spec, wg, b_spec, "nt"), (du, a_spec, wu, b_spec, "nt")],
                [0, 0], [(tm, tn)], [], [(jax.ShapeDtypeStruct((T, D), F32), o_spec)], lambda accs, ex: accs, after=after)
    return dh


def _rms_fwd(name, x, gain):
    R, D = x.shape
    tr = _tile(R, 256, 8)

    def body(x_ref, g_ref, o_ref):
        xv = x_ref[...]
        r = lax.rsqrt(jnp.mean(xv * xv, axis=-1, keepdims=True) + EPS)
        o_ref[...] = (xv * r * g_ref[...]).astype(BF16)

    return pl.pallas_call(
        body, name=name, grid=(R // tr,),
        in_specs=[pl.BlockSpec((tr, D), lambda i: (i, 0)), pl.BlockSpec((1, D), lambda i: (0, 0))],
        out_specs=pl.BlockSpec((tr, D), lambda i: (i, 0)), out_shape=jax.ShapeDtypeStruct((R, D), BF16),
        compiler_params=_params(("parallel",)),
    )(x, gain)


def _rms_bwd(name, dh, x, gain, dres, bscale):
    R, D = x.shape
    tr = _tile(R, 256, 8)

    def body(dh_ref, x_ref, g_ref, dres_ref, dx_ref, dxb_ref, dg_ref):
        xv, dy = x_ref[...], dh_ref[...]
        r = lax.rsqrt(jnp.mean(xv * xv, axis=-1, keepdims=True) + EPS)
        xn = xv * r
        uu = dy * g_ref[...]
        dx = dres_ref[...] + r * (uu - xn * jnp.mean(xn * uu, axis=-1, keepdims=True))
        dx_ref[...] = dx
        dxb_ref[...] = (bscale * dx).astype(BF16)
        part = jnp.sum(dy * xn, axis=0, keepdims=True)

        @pl.when(pl.program_id(0) == 0)
        def _():
            dg_ref[...] = part

        @pl.when(pl.program_id(0) > 0)
        def _():
            dg_ref[...] += part

    row = pl.BlockSpec((tr, D), lambda i: (i, 0))
    vec = pl.BlockSpec((1, D), lambda i: (0, 0))
    return pl.pallas_call(
        body, name=name, grid=(R // tr,), in_specs=[row, row, vec, row], out_specs=[row, row, vec],
        out_shape=[jax.ShapeDtypeStruct((R, D), F32), jax.ShapeDtypeStruct((R, D), BF16), jax.ShapeDtypeStruct((1, D), F32)],
        compiler_params=_params(("arbitrary",)),
    )(dh, x, gain, dres)


def _loss_head(y, target):
    R, D = y.shape
    tr = _tile(R, 256, 8)

    def body(y_ref, t_ref, d_ref, db_ref, l_ref):
        e = y_ref[...] - t_ref[...]
        d = e * (1.0 / D)
        d_ref[...] = d
        db_ref[...] = (0.5 * d).astype(BF16)
        part = jnp.zeros((8, 128), F32) + (0.5 / D) * jnp.sum(e * e)

        @pl.when(pl.program_id(0) == 0)
        def _():
            l_ref[...] = part

        @pl.when(pl.program_id(0) > 0)
        def _():
            l_ref[...] += part

    row = pl.BlockSpec((tr, D), lambda i: (i, 0))
    acc = pl.BlockSpec((8, 128), lambda i: (0, 0))
    return pl.pallas_call(
        body, name="loss_head", grid=(R // tr,), in_specs=[row, row], out_specs=[row, row, acc],
        out_shape=[jax.ShapeDtypeStruct((R, D), F32), jax.ShapeDtypeStruct((R, D), BF16), jax.ShapeDtypeStruct((8, 128), F32)],
        compiler_params=_params(("arbitrary",)),
    )(y, target)


def _head_norm(xs, g):
    r = lax.rsqrt(jnp.mean(xs * xs, axis=-1, keepdims=True) + EPS)
    return xs * r * g


def _head_norm_bwd(xs, g, dy):
    r = lax.rsqrt(jnp.mean(xs * xs, axis=-1, keepdims=True) + EPS)
    xn = xs * r
    uu = dy * g
    return r * (uu - xn * jnp.mean(xn * uu, axis=-1, keepdims=True)), jnp.sum(dy * xn, axis=0, keepdims=True)


NORMED = [(C_FQ, FOX_HEADS, 0), (C_FK, FOX_HEADS, 1), (C_SQ, SWA_HEADS, 2), (C_SK, SWA_KV_HEADS, 3), (C_MQ, MEM_HEADS, 4)]
PLAIN = [(C_FV, FOX_HEADS), (C_SV, SWA_KV_HEADS)]


def _prep_fwd(proj, gains):
    T = proj.shape[0]
    tr = _tile(T, 256, 8)

    def body(p_ref, g_ref, o_ref, lf_ref):
        for start, heads, row in NORMED:
            gn = g_ref[row:row + 1, :]
            for hh in range(heads):
                sl = slice(start + hh * HEAD_DIM, start + (hh + 1) * HEAD_DIM)
                o_ref[:, sl] = _head_norm(p_ref[:, sl], gn).astype(BF16)
        for start, heads in PLAIN:
            sl = slice(start, start + heads * HEAD_DIM)
            o_ref[:, sl] = p_ref[:, sl].astype(BF16)
        zb = p_ref[:, C_FL:C_FL + HEAD_DIM] + g_ref[5:6, :]
        o_ref[:, C_FL:C_FL + HEAD_DIM] = jnp.zeros((tr, HEAD_DIM), BF16)
        lf_ref[...] = jnp.minimum(zb, 0.0) - jnp.log(1.0 + jnp.exp(-jnp.abs(zb)))

    return pl.pallas_call(
        body, name="prep_fwd", grid=(T // tr,),
        in_specs=[pl.BlockSpec((tr, PROJ_W), lambda i: (i, 0)), pl.BlockSpec((8, 128), lambda i: (0, 0))],
        out_specs=[pl.BlockSpec((tr, PROJ_W), lambda i: (i, 0)), pl.BlockSpec((tr, HEAD_DIM), lambda i: (i, 0))],
        out_shape=[jax.ShapeDtypeStruct((T, PROJ_W), BF16), jax.ShapeDtypeStruct((T, HEAD_DIM), F32)],
        compiler_params=_params(("parallel",)),
    )(proj, gains)


def _prep_bwd(proj, gains, dfq, dfk, dfv, dsq, dsk, dsv, dmq, dlogf):
    T = proj.shape[0]
    tr = _tile(T, 256, 8)
    d_normed = {C_FQ: 0, C_FK: 1, C_SQ: 3, C_SK: 4, C_MQ: 6}
    d_plain = {C_FV: 2, C_SV: 5}

    def body(p_ref, g_ref, *rest):
        d_refs, dlf_ref, o_ref, dg_ref = rest[:7], rest[7], rest[8], rest[9]
        rows = []
        for start, heads, row in NORMED:
            gn = g_ref[row:row + 1, :]
            d_ref = d_refs[d_normed[start]]
            tot = jnp.zeros((1, HEAD_DIM), F32)
            for hh in range(heads):
                sl = slice(start + hh * HEAD_DIM, start + (hh + 1) * HEAD_DIM)
                dx, dgn = _head_norm_bwd(p_ref[:, sl], gn, d_ref[:, hh * HEAD_DIM:(hh + 1) * HEAD_DIM])
                o_ref[:, sl] = dx.astype(BF16)
                tot = tot + dgn
            rows.append(tot)
        for start, heads in PLAIN:
            o_ref[:, start:start + heads * HEAD_DIM] = d_refs[d_plain[start]][...].astype(BF16)
        zb = p_ref[:, C_FL:C_FL + HEAD_DIM] + g_ref[5:6, :]
        lane = lax.broadcasted_iota(jnp.int32, (tr, HEAD_DIM), 1)
        dz = jnp.where(lane < FOX_HEADS, dlf_ref[...] * (1.0 - _sigmoid(zb)), 0.0)
        o_ref[:, C_FL:C_FL + HEAD_DIM] = dz.astype(BF16)
        rows.append(jnp.sum(dz, axis=0, keepdims=True))
        part = jnp.concatenate(rows + [jnp.zeros((2, HEAD_DIM), F32)], axis=0)

        @pl.when(pl.program_id(0) == 0)
        def _():
            dg_ref[...] = part

        @pl.when(pl.program_id(0) > 0)
        def _():
            dg_ref[...] += part

    def rows_of(w):
        return pl.BlockSpec((tr, w), lambda i: (i, 0))

    small = pl.BlockSpec((8, 128), lambda i: (0, 0))
    ds = [dfq, dfk, dfv, dsq, dsk, dsv, dmq]
    return pl.pallas_call(
        body, name="prep_bwd", grid=(T // tr,),
        in_specs=[rows_of(PROJ_W), small] + [rows_of(d.shape[1]) for d in ds] + [rows_of(HEAD_DIM)],
        out_specs=[rows_of(PROJ_W), small],
        out_shape=[jax.ShapeDtypeStruct((T, PROJ_W), BF16), jax.ShapeDtypeStruct((8, 128), F32)],
        compiler_params=_params(("arbitrary",)),
    )(proj, gains, *ds, dlogf)


def _head_norm_rows(x, gain):
    R, W = x.shape

    def body(x_ref, g_ref, o_ref):
        for hh in range(W // HEAD_DIM):
            sl = slice(hh * HEAD_DIM, (hh + 1) * HEAD_DIM)
            o_ref[:, sl] = _head_norm(x_ref[:, sl], g_ref[...]).astype(BF16)

    return pl.pallas_call(body, name="mem_k_norm", out_shape=jax.ShapeDtypeStruct((R, W), BF16))(x, gain)


def _head_norm_rows_bwd(x, gain, dy):
    R, W = x.shape

    def body(x_ref, g_ref, dy_ref, dx_ref, dg_ref):
        tot = jnp.zeros((1, HEAD_DIM), F32)
        for hh in range(W // HEAD_DIM):
            sl = slice(hh * HEAD_DIM, (hh + 1) * HEAD_DIM)
            dx, dgn = _head_norm_bwd(x_ref[:, sl], g_ref[...], dy_ref[:, sl])
            dx_ref[:, sl] = dx.astype(BF16)
            tot = tot + dgn
        dg_ref[...] = tot

    return pl.pallas_call(
        body, name="mem_k_norm_bwd",
        out_shape=[jax.ShapeDtypeStruct((R, W), BF16), jax.ShapeDtypeStruct((1, HEAD_DIM), F32)])(x, gain, dy)


def _cumsum_rows(name, xs, reverse):
    T, W = xs[0].shape
    tb = _tile(T, 512, 8)
    nb = T // tb

    def body(*refs):
        o_ref, carry = refs[len(xs)], refs[len(xs) + 1]

        @pl.when(pl.program_id(0) == 0)
        def _():
            carry[...] = jnp.zeros_like(carry)

        xv = refs[0][...]
        for x_ref in refs[1:len(xs)]:
            xv = xv + x_ref[...]
        r = lax.broadcasted_iota(jnp.int32, (tb, tb), 0)
        cc = lax.broadcasted_iota(jnp.int32, (tb, tb), 1)
        tri = jnp.where((cc >= r) if reverse else (cc <= r), 1.0, 0.0).astype(F32)
        o_ref[...] = jnp.dot(tri, xv, precision=lax.Precision.HIGHEST, preferred_element_type=F32) + carry[...]
        carry[...] += jnp.sum(xv, axis=0, keepdims=True)

    idx = (lambda i: (nb - 1 - i, 0)) if reverse else (lambda i: (i, 0))
    return pl.pallas_call(
        body, name=name, grid=(nb,), in_specs=[pl.BlockSpec((tb, W), idx)] * len(xs), out_specs=pl.BlockSpec((tb, W), idx),
        out_shape=jax.ShapeDtypeStruct((T, W), F32), scratch_shapes=[pltpu.VMEM((1, W), F32)],
        compiler_params=_params(("arbitrary",)),
    )(*xs)


def _triangle(nq, by_column):
    if by_column:
        blocks = [(i, j) for j in range(nq) for i in range(j, nq)]
    else:
        blocks = [(i, j) for i in range(nq) for j in range(i + 1)]
    return jnp.asarray(np.array(blocks, np.int32).T)


def _fox_scores(q, k, cq, ck, on_diagonal):
    s = _dot(q, k, "nt") * SCALE + (cq - ck)
    if on_diagonal:
        n = s.shape[0]
        s = jnp.where(lax.broadcasted_iota(jnp.int32, (n, n), 1) <= lax.broadcasted_iota(jnp.int32, (n, n), 0), s, NEG_INF)
    return s


def _fox_fwd(qkv, cq, ck):
    T = qkv.shape[0]
    tq = _tile(T, 512)
    nq = T // tq
    steps = nq * (nq + 1) // 2
    HQ, HK, HV = C_FQ // HEAD_DIM, C_FK // HEAD_DIM, C_FV // HEAD_DIM

    def body(tab, q_ref, k_ref, v_ref, cq_ref, ck_ref, o_ref, of_ref, lse_ref, m_sc, l_sc, acc_sc):
        i, j = tab[0, pl.program_id(1)], tab[1, pl.program_id(1)]

        @pl.when(j == 0)
        def _():
            m_sc[...] = jnp.full_like(m_sc, NEG_INF)
            l_sc[...] = jnp.zeros_like(l_sc)
            acc_sc[...] = jnp.zeros_like(acc_sc)

        def step(on_diagonal):
            s = _fox_scores(q_ref[...], k_ref[...], cq_ref[...], ck_ref[...], on_diagonal)
            m_new = jnp.maximum(m_sc[...], jnp.max(s, axis=-1, keepdims=True))
            alpha = jnp.exp(m_sc[...] - m_new)
            p = jnp.exp(s - m_new)
            l_sc[...] = alpha * l_sc[...] + jnp.sum(p, axis=-1, keepdims=True)
            acc_sc[...] = alpha * acc_sc[...] + _dot(p.astype(BF16), v_ref[...], "nn")
            m_sc[...] = m_new

        @pl.when(j < i)
        def _():
            step(False)

        @pl.when(j == i)
        def _():
            step(True)
            o = acc_sc[...] / l_sc[...]
            o_ref[...] = o.astype(BF16)
            of_ref[...] = o
            lse_ref[...] = m_sc[...] + jnp.log(l_sc[...])

    def rows(base):
        return pl.BlockSpec((tq, HEAD_DIM), lambda h, s, tab: (tab[0, s], base + h))

    def cols(base):
        return pl.BlockSpec((tq, HEAD_DIM), lambda h, s, tab: (tab[1, s], base + h))

    rvec = pl.BlockSpec((None, tq, 1), lambda h, s, tab: (h, tab[0, s], 0))
    return pl.pallas_call(
        body, name="fox_fwd",
        grid_spec=pltpu.PrefetchScalarGridSpec(
            num_scalar_prefetch=1, grid=(FOX_HEADS, steps),
            in_specs=[rows(HQ), cols(HK), cols(HV), rvec, pl.BlockSpec((None, 1, tq), lambda h, s, tab: (h, 0, tab[1, s]))],
            out_specs=[rows(0), rows(0), rvec],
            scratch_shapes=[pltpu.VMEM((tq, 1), F32), pltpu.VMEM((tq, 1), F32), pltpu.VMEM((tq, HEAD_DIM), F32)]),
        out_shape=[jax.ShapeDtypeStruct((T, FOX_W), BF16), jax.ShapeDtypeStruct((T, FOX_W), F32),
                   jax.ShapeDtypeStruct((FOX_HEADS, T, 1), F32)],
        compiler_params=_params(("parallel", "arbitrary")),
    )(_triangle(nq, False), qkv, qkv, qkv, cq, ck)


def _fox_bwd(qkv, cq, ck, out_f32, lse, dmix):
    T = qkv.shape[0]
    tq = _tile(T, 512)
    nq = T // tq
    steps = nq * (nq + 1) // 2
    HQ, HK, HV = C_FQ // HEAD_DIM, C_FK // HEAD_DIM, C_FV // HEAD_DIM

    def body(tab, q_ref, k_ref, v_ref, cq_ref, ck_ref, o_ref, lse_ref, do_ref, dq_ref, dk_ref, dv_ref, dck_ref, dcq_ref,
             dk_sc, dv_sc, dc_sc):
        qi, kj = tab[0, pl.program_id(1)], tab[1, pl.program_id(1)]

        @pl.when(qi == kj)
        def _():
            dk_sc[...] = jnp.zeros_like(dk_sc)
            dv_sc[...] = jnp.zeros_like(dv_sc)
            dc_sc[...] = jnp.zeros_like(dc_sc)

        def step(on_diagonal):
            q, k, v, do = q_ref[...], k_ref[...], v_ref[...], do_ref[...]
            p = jnp.exp(_fox_scores(q, k, cq_ref[...], ck_ref[...], on_diagonal) - lse_ref[...])
            dp = _dot(do, v, "nt")
            delta = jnp.sum(do.astype(F32) * o_ref[...], axis=-1, keepdims=True)
            ds = p * (dp - delta)
            dsb = ds.astype(BF16)
            dv_sc[...] += _dot(p.astype(BF16), do, "tn")
            dk_sc[...] += _dot(dsb, q, "tn")
            dc_sc[...] += jnp.sum(ds, axis=0, keepdims=True)
            dq_part = _dot(dsb, k, "nn") * SCALE
            dcq_part = jnp.sum(ds, axis=-1, keepdims=True)
            rows_q = pl.ds(pl.multiple_of(qi * tq, tq), tq)

            @pl.when(kj == 0)
            def _():
                dq_ref[rows_q, :] = dq_part
                dcq_ref[rows_q, :] = dcq_part

            @pl.when(kj > 0)
            def _():
                dq_ref[rows_q, :] += dq_part
                dcq_ref[rows_q, :] += dcq_part

        @pl.when(qi > kj)
        def _():
            step(False)

        @pl.when(qi == kj)
        def _():
            step(True)

        @pl.when(qi == nq - 1)
        def _():
            dk_ref[...] = dk_sc[...] * SCALE
            dv_ref[...] = dv_sc[...]
            dck_ref[...] = -dc_sc[...]

    def rows(base):
        return pl.BlockSpec((tq, HEAD_DIM), lambda h, s, tab: (tab[0, s], base + h))

    def cols(base):
        return pl.BlockSpec((tq, HEAD_DIM), lambda h, s, tab: (tab[1, s], base + h))

    rvec = pl.BlockSpec((None, tq, 1), lambda h, s, tab: (h, tab[0, s], 0))
    cvec = pl.BlockSpec((None, 1, tq), lambda h, s, tab: (h, 0, tab[1, s]))
    sds = jax.ShapeDtypeStruct((T, FOX_W), F32)
    return pl.pallas_call(
        body, name="fox_bwd",
        grid_spec=pltpu.PrefetchScalarGridSpec(
            num_scalar_prefetch=1, grid=(FOX_HEADS, steps),
            in_specs=[rows(HQ), cols(HK), cols(HV), rvec, cvec, rows(0), rvec, rows(0)],
            out_specs=[pl.BlockSpec((T, HEAD_DIM), lambda h, s, tab: (0, h)), cols(0), cols(0), cvec,
                       pl.BlockSpec((None, T, 1), lambda h, s, tab: (h, 0, 0))],
            scratch_shapes=[pltpu.VMEM((tq, HEAD_DIM), F32), pltpu.VMEM((tq, HEAD_DIM), F32), pltpu.VMEM((1, tq), F32)]),
        out_shape=[sds, sds, sds, jax.ShapeDtypeStruct((FOX_HEADS, 1, T), F32), jax.ShapeDtypeStruct((FOX_HEADS, T, 1), F32)],
        compiler_params=_params(("parallel", "arbitrary")),
    )(_triangle(nq, True), qkv, qkv, qkv, cq, ck, out_f32, lse, dmix)


GW = SWA_GROUP * HEAD_DIM
GR = SWA_GROUP * WINDOW


def _swa_scores(q_ref, kp_ref, kc_ref, slope_ref, n):
    q = q_ref[...]
    qs = jnp.concatenate([q[:, t * HEAD_DIM:(t + 1) * HEAD_DIM] for t in range(SWA_GROUP)], axis=0)
    kb = jnp.concatenate([kp_ref[...], kc_ref[...]], axis=0)
    r = lax.broadcasted_iota(jnp.int32, (GR, 2 * WINDOW), 0) & (WINDOW - 1)
    jj = lax.broadcasted_iota(jnp.int32, (GR, 2 * WINDOW), 1)
    dist = WINDOW + r - jj
    valid = (dist >= 0) & (dist < WINDOW) & ((n > 0) | (jj >= WINDOW))
    s = _dot(qs, kb, "nt") * SCALE - slope_ref[...] * dist.astype(F32)
    return qs, kb, jnp.where(valid, s, NEG_INF), valid


def _swa_specs():
    HQ, HK, HV = C_SQ // GW, C_SK // HEAD_DIM, C_SV // HEAD_DIM
    q_spec = pl.BlockSpec((WINDOW, GW), lambda g, n: (n, HQ + g))

    def prev(base):
        return pl.BlockSpec((WINDOW, HEAD_DIM), lambda g, n: (jnp.maximum(n - 1, 0), base + g))

    def cur(base):
        return pl.BlockSpec((WINDOW, HEAD_DIM), lambda g, n: (n, base + g))

    col = pl.BlockSpec((None, GR, 1), lambda g, n: (g, 0, 0))
    return q_spec, prev(HK), cur(HK), prev(HV), cur(HV), col


def _swa_fwd(qkv, slopes, sinks):
    T = qkv.shape[0]
    nb = T // WINDOW
    assert C_SQ % GW == 0

    def body(q_ref, kp_ref, kc_ref, vp_ref, vc_ref, slope_ref, sink_ref, o_ref, lse_ref):
        n = pl.program_id(1)
        _, _, s, _ = _swa_scores(q_ref, kp_ref, kc_ref, slope_ref, n)
        m = jnp.maximum(jnp.max(s, axis=-1, keepdims=True), sink_ref[...])
        p = jnp.exp(s - m)
        l = jnp.sum(p, axis=-1, keepdims=True) + jnp.exp(sink_ref[...] - m)
        vb = jnp.concatenate([vp_ref[...], vc_ref[...]], axis=0)
        o = _dot(p.astype(BF16), vb, "nn") / l
        for t in range(SWA_GROUP):
            o_ref[:, t * HEAD_DIM:(t + 1) * HEAD_DIM] = o[t * WINDOW:(t + 1) * WINDOW, :].astype(BF16)
        lse_ref[...] = m + jnp.log(l)

    q_spec, kp, kc, vp, vc, col = _swa_specs()
    return pl.pallas_call(
        body, name="swa_fwd", grid=(SWA_KV_HEADS, nb), in_specs=[q_spec, kp, kc, vp, vc, col, col],
        out_specs=[pl.BlockSpec((WINDOW, GW), lambda g, n: (n, g)), pl.BlockSpec((None, None, GR, 1), lambda g, n: (g, n, 0, 0))],
        out_shape=[jax.ShapeDtypeStruct((T, SWA_HEADS * HEAD_DIM), BF16), jax.ShapeDtypeStruct((SWA_KV_HEADS, nb, GR, 1), F32)],
        compiler_params=_params(("parallel", "arbitrary")),
    )(qkv, qkv, qkv, qkv, qkv, slopes, sinks)


def _swa_bwd(qkv, slopes, sinks, out, lse, dmix):
    T = qkv.shape[0]
    nb = T // WINDOW
    DO = FOX_W // GW
    assert FOX_W % GW == 0

    def body(q_ref, kp_ref, kc_ref, vp_ref, vc_ref, slope_ref, sink_ref, o_ref, lse_ref, do_ref,
             dq_ref, dk_ref, dv_ref, dsink_ref, sink_sc):
        n = pl.program_id(1)

        @pl.when(n == 0)
        def _():
            dk_ref[...] = jnp.zeros_like(dk_ref)
            dv_ref[...] = jnp.zeros_like(dv_ref)
            sink_sc[...] = jnp.zeros_like(sink_sc)

        qs, kb, s, valid = _swa_scores(q_ref, kp_ref, kc_ref, slope_ref, n)
        lse = lse_ref[...]
        p = jnp.where(valid, jnp.exp(s - lse), 0.0)
        vb = jnp.concatenate([vp_ref[...], vc_ref[...]], axis=0)
        do = jnp.concatenate([do_ref[:, t * HEAD_DIM:(t + 1) * HEAD_DIM] for t in range(SWA_GROUP)], axis=0)
        oo = jnp.concatenate([o_ref[:, t * HEAD_DIM:(t + 1) * HEAD_DIM] for t in range(SWA_GROUP)], axis=0)
        dp = _dot(do, vb, "nt")
        delta = jnp.sum(do.astype(F32) * oo.astype(F32), axis=-1, keepdims=True)
        ds = p * (dp - delta)
        dsb = ds.astype(BF16)
        dq = _dot(dsb, kb, "nn") * SCALE
        for t in range(SWA_GROUP):
            dq_ref[:, t * HEAD_DIM:(t + 1) * HEAD_DIM] = dq[t * WINDOW:(t + 1) * WINDOW, :]
        dkb = _dot(dsb, qs, "tn") * SCALE
        dvb = _dot(p.astype(BF16), do, "tn")
        r_prev = pl.ds(pl.multiple_of(jnp.maximum(n - 1, 0) * WINDOW, WINDOW), WINDOW)
        r_cur = pl.ds(pl.multiple_of(n * WINDOW, WINDOW), WINDOW)
        dk_ref[r_prev, :] += dkb[:WINDOW, :]
        dk_ref[r_cur, :] += dkb[WINDOW:, :]
        dv_ref[r_prev, :] += dvb[:WINDOW, :]
        dv_ref[r_cur, :] += dvb[WINDOW:, :]
        sink_sc[...] -= jnp.exp(sink_ref[...] - lse) * delta

        @pl.when(n == nb - 1)
        def _():
            tot = [jnp.zeros((1, 128), F32) + jnp.sum(sink_sc[t * WINDOW:(t + 1) * WINDOW, :]) for t in range(SWA_GROUP)]
            dsink_ref[...] = jnp.concatenate(tot + [jnp.zeros((8 - SWA_GROUP, 128), F32)], axis=0)

    q_spec, kp, kc, vp, vc, col = _swa_specs()
    kv_acc = pl.BlockSpec((T, HEAD_DIM), lambda g, n: (0, g))
    return pl.pallas_call(
        body, name="swa_bwd", grid=(SWA_KV_HEADS, nb),
        in_specs=[q_spec, kp, kc, vp, vc, col, col, pl.BlockSpec((WINDOW, GW), lambda g, n: (n, g)),
                  pl.BlockSpec((None, None, GR, 1), lambda g, n: (g, n, 0, 0)), pl.BlockSpec((WINDOW, GW), lambda g, n: (n, DO + g))],
        out_specs=[pl.BlockSpec((WINDOW, GW), lambda g, n: (n, g)), kv_acc, kv_acc, pl.BlockSpec((None, 8, 128), lambda g, n: (g, 0, 0))],
        out_shape=[jax.ShapeDtypeStruct((T, SWA_HEADS * HEAD_DIM), F32), jax.ShapeDtypeStruct((T, SWA_KV_HEADS * HEAD_DIM), F32),
                   jax.ShapeDtypeStruct((T, SWA_KV_HEADS * HEAD_DIM), F32), jax.ShapeDtypeStruct((SWA_KV_HEADS, 8, 128), F32)],
        scratch_shapes=[pltpu.VMEM((GR, 1), F32)],
        compiler_params=_params(("parallel", "arbitrary")),
    )(qkv, qkv, qkv, qkv, qkv, slopes, sinks, out, lse, dmix)


def _mem_fwd(qkv, mk, mv):
    T, ML = qkv.shape[0], mk.shape[0]
    tq = _tile(T, 512)
    HQ = C_MQ // HEAD_DIM

    def body(q_ref, k_ref, v_ref, o_ref, lse_ref):
        s = _dot(q_ref[...], k_ref[...], "nt") * SCALE
        m = jnp.max(s, axis=-1, keepdims=True)
        p = jnp.exp(s - m)
        l = jnp.sum(p, axis=-1, keepdims=True)
        o_ref[...] = (_dot(p.astype(BF16), v_ref[...], "nn") / l).astype(BF16)
        lse_ref[...] = m + jnp.log(l)

    kv = pl.BlockSpec((ML, HEAD_DIM), lambda h, i: (0, h))
    return pl.pallas_call(
        body, name="mem_fwd", grid=(MEM_HEADS, T // tq),
        in_specs=[pl.BlockSpec((tq, HEAD_DIM), lambda h, i: (i, HQ + h)), kv, kv],
        out_specs=[pl.BlockSpec((tq, HEAD_DIM), lambda h, i: (i, h)), pl.BlockSpec((None, tq, 1), lambda h, i: (h, i, 0))],
        out_shape=[jax.ShapeDtypeStruct((T, MEM_HEADS * HEAD_DIM), BF16), jax.ShapeDtypeStruct((MEM_HEADS, T, 1), F32)],
        compiler_params=_params(("parallel", "arbitrary")),
    )(qkv, mk, mv)


def _mem_bwd(qkv, mk, mv, out, lse, dmix):
    T, ML = qkv.shape[0], mk.shape[0]
    tq = _tile(T, 512)
    HQ = C_MQ // HEAD_DIM
    DO = (FOX_W + SWA_HEADS * HEAD_DIM) // HEAD_DIM

    def body(q_ref, k_ref, v_ref, o_ref, lse_ref, do_ref, dq_ref, dk_ref, dv_ref):
        q, k, v, do = q_ref[...], k_ref[...], v_ref[...], do_ref[...]
        p = jnp.exp(_dot(q, k, "nt") * SCALE - lse_ref[...])
        dp = _dot(do, v, "nt")
        delta = jnp.sum(do.astype(F32) * o_ref[...].astype(F32), axis=-1, keepdims=True)
        dsb = (p * (dp - delta)).astype(BF16)
        dq_ref[...] = _dot(dsb, k, "nn") * SCALE
        dk_part = _dot(dsb, q, "tn") * SCALE
        dv_part = _dot(p.astype(BF16), do, "tn")

        @pl.when(pl.program_id(1) == 0)
        def _():
            dk_ref[...] = dk_part
            dv_ref[...] = dv_part

        @pl.when(pl.program_id(1) > 0)
        def _():
            dk_ref[...] += dk_part
            dv_ref[...] += dv_part

    kv = pl.BlockSpec((ML, HEAD_DIM), lambda h, i: (0, h))
    qb = pl.BlockSpec((tq, HEAD_DIM), lambda h, i: (i, h))
    return pl.pallas_call(
        body, name="mem_bwd", grid=(MEM_HEADS, T // tq),
        in_specs=[pl.BlockSpec((tq, HEAD_DIM), lambda h, i: (i, HQ + h)), kv, kv, qb,
                  pl.BlockSpec((None, tq, 1), lambda h, i: (h, i, 0)), pl.BlockSpec((tq, HEAD_DIM), lambda h, i: (i, DO + h))],
        out_specs=[qb, kv, kv],
        out_shape=[jax.ShapeDtypeStruct((T, MEM_HEADS * HEAD_DIM), F32), jax.ShapeDtypeStruct((ML, MEM_HEADS * HEAD_DIM), F32),
                   jax.ShapeDtypeStruct((ML, MEM_HEADS * HEAD_DIM), F32)],
        compiler_params=_params(("parallel", "arbitrary")),
    )(qkv, mk, mv, out, lse, dmix)


HBM = pl.BlockSpec(memory_space=pltpu.HBM)


def _place():
    x, y, c = lax.axis_index("x"), lax.axis_index("y"), lax.axis_index("c")
    chips = [(1 - x, y), (x, 1 - y), (1 - x, 1 - y)]
    return x, y, c, chips


def _remote(src, dst, send_sem, recv_sem, device):
    return pltpu.make_async_remote_copy(src_ref=src, dst_ref=dst, send_sem=send_sem, recv_sem=recv_sem,
                                        device_id=device, device_id_type=MESH)


def _place_ids():
    x, y, c = lax.axis_index("x"), lax.axis_index("y"), lax.axis_index("c")
    order = [2 * x + y, 2 * (1 - x) + y, 2 * x + (1 - y), 2 * (1 - x) + (1 - y)]
    return jnp.stack([2 * x + y, c] + order).astype(jnp.int32)


def _cast_place(name, w, ids):
    R, C = w.shape
    tr = _tile(R, 256, 16)

    def body(ids_ref, w_ref, o_ref):
        o_ref[...] = w_ref[...].astype(BF16)

    return pl.pallas_call(
        body, name=name,
        grid_spec=pltpu.PrefetchScalarGridSpec(
            num_scalar_prefetch=1, grid=(R // tr,), in_specs=[pl.BlockSpec((tr, C), lambda i, ids: (i, 0))],
            out_specs=pl.BlockSpec((None, tr, C), lambda i, ids: (ids[0], i, 0))),
        out_shape=jax.ShapeDtypeStruct((N_CHIPS, R, C), BF16), compiler_params=_params(("parallel",)),
    )(ids, w)


SEM = pl.BlockSpec(memory_space=pltpu.SEMAPHORE)
EFFECT = pltpu.SideEffectType.DATAFLOW_SIDE_EFFECTING


def _hbm(a):
    return pltpu.with_memory_space_constraint(a, pltpu.HBM)


def _gather_start(name, placed, after):
    n = len(placed)

    ns = 3 * n

    def body(*refs):
        send, recv = refs[n + 1:n + 1 + ns], refs[n + 1 + ns:n + 1 + 2 * ns]
        buf = refs[n + 1 + 2 * ns:2 * n + 1 + 2 * ns]
        token = refs[2 * n + 1 + 2 * ns]
        x, y, c, chips = _place()
        me = 2 * x + y
        for a in range(n):
            half = buf[a].shape[1] // 2
            mine = buf[a].at[me, pl.ds(c * half, half)]
            for j, (cx, cy) in enumerate(chips):
                _remote(mine, mine, send[3 * a + j], recv[3 * a + j], (cx, cy, c)).start()
        token[...] = jnp.zeros_like(token)

    res = pl.pallas_call(
        body, name=name, in_specs=[HBM] * n + [pl.BlockSpec(memory_space=pl.ANY)],
        out_specs=[SEM] * (2 * ns) + [HBM] * n + [pl.BlockSpec(memory_space=pltpu.VMEM)],
        out_shape=[pltpu.SemaphoreType.DMA(())] * (2 * ns)
        + [pltpu.HBM(s.shape, s.dtype) for s in placed] + [jax.ShapeDtypeStruct((8, 128), F32)],
        input_output_aliases={a: 2 * ns + a for a in range(n)},
        compiler_params=pltpu.CompilerParams(has_side_effects=EFFECT),
    )(*[_hbm(s) for s in placed], after)
    return list(res[:ns]), list(res[ns:2 * ns]), list(res[2 * ns:2 * ns + n]), res[2 * ns + n]


def _gather_wait(name, send, recv, bufs, after):
    n = len(bufs)

    ns = 3 * n

    def body(*refs):
        buf = refs[:n]
        send_ref, recv_ref = refs[n:n + ns], refs[n + ns:n + 2 * ns]
        x, y, c, chips = _place()
        ids = [2 * cx + cy for cx, cy in chips]
        for a in range(n):
            half = buf[a].shape[1] // 2
            for j in range(3):
                landed = buf[a].at[ids[j], pl.ds(c * half, half)]
                cp = _remote(landed, landed, send_ref[3 * a + j], recv_ref[3 * a + j], (x, y, c))
                cp.wait_send()
                cp.wait_recv()

    res = pl.pallas_call(
        body, name=name, in_specs=[HBM] * n + [SEM] * (2 * ns) + [pl.BlockSpec(memory_space=pl.ANY)], out_specs=[HBM] * n,
        out_shape=[pltpu.HBM(s.shape, s.dtype) for s in bufs], input_output_aliases={a: a for a in range(n)},
        compiler_params=pltpu.CompilerParams(has_side_effects=EFFECT),
    )(*bufs, *send, *recv, after)
    return list(res)


def _gather_forward(name, bufs):
    n = len(bufs)

    def body(*refs):
        buf = refs[n:2 * n]
        send, recv = refs[2 * n:]
        x, y, c, chips = _place()
        ids = [2 * cx + cy for cx, cy in chips]
        copies = []
        for a in range(n):
            half = buf[a].shape[1] // 2
            for j in range(3):
                landed = buf[a].at[ids[j], pl.ds(c * half, half)]
                cp = _remote(landed, landed, send.at[a, j], recv.at[a, j], (x, y, 1 - c))
                cp.start()
                copies.append(cp)
        for a in range(n):
            half = buf[a].shape[1] // 2
            for j in range(3):
                landed = buf[a].at[ids[j], pl.ds((1 - c) * half, half)]
                _remote(landed, landed, send.at[a, j], recv.at[a, j], (x, y, c)).wait_recv()
        for cp in copies:
            cp.wait_send()

    return pl.pallas_call(
        body, name=name, in_specs=[HBM] * n, out_specs=[HBM] * n,
        out_shape=[jax.ShapeDtypeStruct(s.shape, s.dtype) for s in bufs], input_output_aliases={a: a for a in range(n)},
        scratch_shapes=[pltpu.SemaphoreType.DMA((n, 3)), pltpu.SemaphoreType.DMA((n, 3))],
    )(*bufs)


def _pair_exchange(name, grads):
    n = len(grads)

    def body(*refs):
        src, theirs = refs[:n], refs[n:2 * n]
        send, recv = refs[2 * n:]
        x, y, c, chips = _place()
        order = [2 * x + y] + [2 * cx + cy for cx, cy in chips]
        copies = []
        for a in range(n):
            half = src[a].shape[1] // 2
            for j in range(N_CHIPS):
                cp = _remote(src[a].at[order[j], pl.ds((1 - c) * half, half)], theirs[a].at[j], send.at[a, j], recv.at[a, j], (x, y, 1 - c))
                cp.start()
                copies.append(cp)
        for cp in copies:
            cp.wait()

    return pl.pallas_call(
        body, name=name, in_specs=[HBM] * n, out_specs=[HBM] * n,
        out_shape=[jax.ShapeDtypeStruct((N_CHIPS, g.shape[1] // 2, g.shape[2]), g.dtype) for g in grads],
        scratch_shapes=[pltpu.SemaphoreType.DMA((n, N_CHIPS)), pltpu.SemaphoreType.DMA((n, N_CHIPS))],
    )(*grads)


def _pair_start(name, grads):
    n = len(grads)
    ns = N_CHIPS * n

    def body(*refs):
        send, recv = refs[2 * n:2 * n + ns], refs[2 * n + ns:2 * n + 2 * ns]
        src = refs[2 * n + 2 * ns:3 * n + 2 * ns]
        land = refs[3 * n + 2 * ns:4 * n + 2 * ns]
        token = refs[4 * n + 2 * ns]
        x, y, c, chips = _place()
        order = [2 * x + y] + [2 * cx + cy for cx, cy in chips]
        for a in range(n):
            half = src[a].shape[1] // 2
            for j in range(N_CHIPS):
                _remote(src[a].at[order[j], pl.ds((1 - c) * half, half)], land[a].at[j],
                        send[N_CHIPS * a + j], recv[N_CHIPS * a + j], (x, y, 1 - c)).start()
        token[...] = jnp.zeros_like(token)

    lands = [jax.ShapeDtypeStruct((N_CHIPS, g.shape[1] // 2, g.shape[2]), g.dtype) for g in grads]
    res = pl.pallas_call(
        body, name=name, in_specs=[HBM] * (2 * n),
        out_specs=[SEM] * (2 * ns) + [HBM] * (2 * n) + [pl.BlockSpec(memory_space=pltpu.VMEM)],
        out_shape=[pltpu.SemaphoreType.DMA(())] * (2 * ns) + [pltpu.HBM(g.shape, g.dtype) for g in grads]
        + [pltpu.HBM(l.shape, l.dtype) for l in lands] + [jax.ShapeDtypeStruct((8, 128), F32)],
        input_output_aliases={a: 2 * ns + a for a in range(2 * n)},
        compiler_params=pltpu.CompilerParams(has_side_effects=EFFECT),
    )(*[_hbm(g) for g in grads], *[_hbm(lax.empty(l.shape, l.dtype)) for l in lands])
    return list(res[:ns]), list(res[ns:2 * ns]), list(res[2 * ns:2 * ns + n]), list(res[2 * ns + n:2 * ns + 2 * n]), res[2 * ns + 2 * n]


def _pair_wait(name, send, recv, grads, lands, after):
    n = len(grads)
    ns = N_CHIPS * n

    def body(*refs):
        src, land = refs[:n], refs[n:2 * n]
        send_ref, recv_ref = refs[2 * n:2 * n + ns], refs[2 * n + ns:2 * n + 2 * ns]
        x, y, c, _ = _place()
        for a in range(n):
            for j in range(N_CHIPS):
                cp = _remote(land[a].at[j], land[a].at[j], send_ref[N_CHIPS * a + j], recv_ref[N_CHIPS * a + j], (x, y, c))
                cp.wait_send()
                cp.wait_recv()

    res = pl.pallas_call(
        body, name=name, in_specs=[HBM] * (2 * n) + [SEM] * (2 * ns) + [pl.BlockSpec(memory_space=pl.ANY)],
        out_specs=[HBM] * (2 * n), out_shape=[pltpu.HBM(g.shape, g.dtype) for g in grads] + [pltpu.HBM(l.shape, l.dtype) for l in lands],
        input_output_aliases={a: a for a in range(2 * n)},
        compiler_params=pltpu.CompilerParams(has_side_effects=EFFECT),
    )(*grads, *lands, *send, *recv, after)
    return list(res[:n]), list(res[n:])


def _chip_start(name, parts):
    n = len(parts)
    ns = 3 * n

    def body(*refs):
        send, recv = refs[2 * n:2 * n + ns], refs[2 * n + ns:2 * n + 2 * ns]
        src = refs[2 * n + 2 * ns:3 * n + 2 * ns]
        land = refs[3 * n + 2 * ns:4 * n + 2 * ns]
        token = refs[4 * n + 2 * ns]
        x, y, c, chips = _place()
        for a in range(n):
            for j, (cx, cy) in enumerate(chips):
                _remote(src[a].at[j], land[a].at[j], send[3 * a + j], recv[3 * a + j], (cx, cy, c)).start()
        token[...] = jnp.zeros_like(token)

    res = pl.pallas_call(
        body, name=name, in_specs=[HBM] * (2 * n),
        out_specs=[SEM] * (2 * ns) + [HBM] * (2 * n) + [pl.BlockSpec(memory_space=pltpu.VMEM)],
        out_shape=[pltpu.SemaphoreType.DMA(())] * (2 * ns) + [pltpu.HBM(p.shape, p.dtype) for p in parts] * 2
        + [jax.ShapeDtypeStruct((8, 128), F32)],
        input_output_aliases={a: 2 * ns + a for a in range(2 * n)},
        compiler_params=pltpu.CompilerParams(has_side_effects=EFFECT),
    )(*[_hbm(p) for p in parts], *[_hbm(lax.empty(p.shape, p.dtype)) for p in parts])
    return list(res[:ns]), list(res[ns:2 * ns]), list(res[2 * ns:2 * ns + n]), list(res[2 * ns + n:2 * ns + 2 * n]), res[2 * ns + 2 * n]


def _chip_wait(name, send, recv, parts, lands, after):
    n = len(parts)
    ns = 3 * n

    def body(*refs):
        src, land = refs[:n], refs[n:2 * n]
        send_ref, recv_ref = refs[2 * n:2 * n + ns], refs[2 * n + ns:2 * n + 2 * ns]
        x, y, c, _ = _place()
        for a in range(n):
            for j in range(3):
                cp = _remote(src[a].at[j], land[a].at[j], send_ref[3 * a + j], recv_ref[3 * a + j], (x, y, c))
                cp.wait_send()
                cp.wait_recv()

    res = pl.pallas_call(
        body, name=name, in_specs=[HBM] * (2 * n) + [SEM] * (2 * ns) + [pl.BlockSpec(memory_space=pl.ANY)],
        out_specs=[HBM] * (2 * n), out_shape=[pltpu.HBM(p.shape, p.dtype) for p in parts] * 2,
        input_output_aliases={a: a for a in range(2 * n)},
        compiler_params=pltpu.CompilerParams(has_side_effects=EFFECT),
    )(*parts, *lands, *send, *recv, after)
    return list(res[n:])


def _pair_share(name, shards):
    n = len(shards)

    def body(*refs):
        buf = refs[n:2 * n]
        send, recv = refs[2 * n:]
        x, y, c, _ = _place()
        copies = []
        for a in range(n):
            half = buf[a].shape[0] // 2
            mine = buf[a].at[pl.ds(c * half, half)]
            cp = _remote(mine, mine, send.at[a], recv.at[a], (x, y, 1 - c))
            cp.start()
            copies.append(cp)
        for a, cp in enumerate(copies):
            half = buf[a].shape[0] // 2
            cp.wait_send()
            theirs = buf[a].at[pl.ds((1 - c) * half, half)]
            _remote(theirs, theirs, send.at[a], recv.at[a], (x, y, c)).wait_recv()

    return pl.pallas_call(
        body, name=name, in_specs=[HBM] * n, out_specs=[HBM] * n,
        out_shape=[jax.ShapeDtypeStruct(s.shape, s.dtype) for s in shards], input_output_aliases={a: a for a in range(n)},
        scratch_shapes=[pltpu.SemaphoreType.DMA((n,)), pltpu.SemaphoreType.DMA((n,))],
    )(*shards)


def _all_reduce_small(buf):
    R, W = buf.shape

    def body(src_ref, out_ref, slots, send, recv):
        x, y, c, _ = _place()
        me = 4 * x + 2 * y + c
        copies = []
        for dx in range(2):
            for dy in range(2):
                for dc in range(2):
                    if dx == dy == dc == 0:
                        continue
                    k = 4 * dx + 2 * dy + dc
                    peer = (x ^ dx, y ^ dy, c ^ dc)
                    cp = _remote(src_ref, slots.at[me], send.at[k], recv.at[k], peer)
                    cp.start()
                    copies.append((cp, k))
        slots[me] = src_ref[...]
        for cp, k in copies:
            cp.wait_send()
            landed = slots.at[me ^ k]
            _remote(landed, landed, send.at[k], recv.at[k], (x, y, c)).wait_recv()
        total = slots[0]
        for d in range(1, N_DEV):
            total = total + slots[d]
        out_ref[...] = total

    return pl.pallas_call(
        body, name="all_reduce_small", in_specs=[pl.BlockSpec(memory_space=pltpu.VMEM)],
        out_specs=pl.BlockSpec(memory_space=pltpu.VMEM), out_shape=jax.ShapeDtypeStruct((R, W), F32),
        scratch_shapes=[pltpu.VMEM((N_DEV, R, W), F32), pltpu.SemaphoreType.DMA((N_DEV,)), pltpu.SemaphoreType.DMA((N_DEV,))],
    )(buf)


def _pair_sum_bf16(name, grad, theirs, ids):
    _, R2, C = theirs.shape
    tr = _tile(R2, 256, 16)
    nrb = R2 // tr

    def body(ids_ref, a_ref, b_ref, o_ref):
        o_ref[...] = (a_ref[...] + b_ref[...]).astype(BF16)

    return pl.pallas_call(
        body, name=name,
        grid_spec=pltpu.PrefetchScalarGridSpec(
            num_scalar_prefetch=1, grid=(3, nrb),
            in_specs=[pl.BlockSpec((None, tr, C), lambda j, i, ids: (ids[3 + j], ids[1] * nrb + i, 0)),
                      pl.BlockSpec((None, tr, C), lambda j, i, ids: (j + 1, i, 0))],
            out_specs=pl.BlockSpec((None, tr, C), lambda j, i, ids: (j, i, 0))),
        out_shape=jax.ShapeDtypeStruct((3, R2, C), BF16), compiler_params=_params(("parallel", "parallel")),
    )(ids, grad, theirs)


def _chip_sum(name, grad, theirs, arrived, ids):
    _, R2, C = theirs.shape
    tr = _tile(R2, 256, 16)
    nrb = R2 // tr

    def body(ids_ref, a_ref, b_ref, r_ref, o_ref):
        tot = a_ref[...] + b_ref[...]
        for j in range(3):
            tot = tot + r_ref[j].astype(F32)
        o_ref[...] = tot

    return pl.pallas_call(
        body, name=name,
        grid_spec=pltpu.PrefetchScalarGridSpec(
            num_scalar_prefetch=1, grid=(nrb,),
            in_specs=[pl.BlockSpec((None, tr, C), lambda i, ids: (ids[0], ids[1] * nrb + i, 0)),
                      pl.BlockSpec((None, tr, C), lambda i, ids: (0, i, 0)),
                      pl.BlockSpec((3, tr, C), lambda i, ids: (0, i, 0))],
            out_specs=pl.BlockSpec((tr, C), lambda i, ids: (ids[1] * nrb + i, 0))),
        out_shape=jax.ShapeDtypeStruct((2 * R2, C), F32), compiler_params=_params(("parallel",)),
    )(ids, grad, theirs, arrived)


def _adamw(name, w, g, m, v):
    R, C = w.shape
    tr = _tile(R, 128, 8)
    c1 = 1.0 / (1.0 - ADAM_B1 ** ADAM_STEP)
    c2 = 1.0 / (1.0 - ADAM_B2 ** ADAM_STEP)

    def body(w_ref, g_ref, m_ref, v_ref, d_ref, mo_ref, vo_ref):
        gv = g_ref[...]
        mn = ADAM_B1 * m_ref[...] + (1.0 - ADAM_B1) * gv
        vn = ADAM_B2 * v_ref[...] + (1.0 - ADAM_B2) * (gv * gv)
        d_ref[...] = -ADAM_LR * ((mn * c1) / (jnp.sqrt(vn * c2) + ADAM_EPS) + ADAM_WD * w_ref[...])
        mo_ref[...] = mn
        vo_ref[...] = vn

    spec = pl.BlockSpec((tr, C), lambda i: (i, 0))
    sds = jax.ShapeDtypeStruct((R, C), F32)
    return pl.pallas_call(body, name=name, grid=(R // tr,), in_specs=[spec] * 4, out_specs=[spec] * 3, out_shape=[sds] * 3,
                          compiler_params=_params(("parallel",)))(w, g, m, v)


SMALL = ["ffn1_norm", "mix_norm", "mem_norm", "forget_bias", "fox_q_gain", "fox_k_gain", "swa_q_gain", "swa_k_gain", "swa_sinks",
         "mem_q_gain", "mem_k_gain", "ffn2_norm"]
LARGE = ["ffn1_gate", "ffn1_up", "ffn1_down", "w_in", "w_mem_k", "w_mem_v", "w_out", "ffn2_gate", "ffn2_up", "ffn2_down"]
GATHER_GROUPS = [["ffn1_gate", "ffn1_up"], ["ffn1_down", "w_in", "w_mem_k", "w_mem_v"], ["w_out", "ffn2_gate", "ffn2_up", "ffn2_down"]]
WEIGHTS = ["ffn1_norm", "ffn1_gate", "ffn1_up", "ffn1_down", "mix_norm", "mem_norm", "w_in", "forget_bias", "w_mem_k", "w_mem_v",
           "fox_q_gain", "fox_k_gain", "swa_q_gain", "swa_k_gain", "swa_sinks", "mem_q_gain", "mem_k_gain", "w_out", "ffn2_norm",
           "ffn2_gate", "ffn2_up", "ffn2_down"]


def _pad_proj_cols(w):
    out = jnp.zeros((w.shape[0], PROJ_W), w.dtype)
    for start, width, pstart in REF_GROUPS:
        out = lax.dynamic_update_slice(out, w[:, start:start + width], (0, pstart))
    return out


def _unpad_proj_cols(w):
    return jnp.concatenate([w[:, pstart:pstart + width] for _, width, pstart in REF_GROUPS], axis=1)


def _pack_small(vals):
    flat = jnp.concatenate([vals[k].reshape(-1).astype(F32) for k in SMALL + ["loss"]])
    n = flat.shape[0]
    total = -(-n // 1024) * 1024
    return jnp.pad(flat, (0, total - n)).reshape(total // 128, 128)


def _unpack_small(buf, shapes):
    flat = buf.reshape(-1)
    out, off = {}, 0
    for k in SMALL + ["loss"]:
        size = int(np.prod(shapes[k]))
        out[k] = flat[off:off + size].reshape(shapes[k])
        off += size
    return out


def kernel(x, mem, ffn1_norm, ffn1_gate, ffn1_up, ffn1_down, mix_norm, mem_norm, w_in, forget_bias, w_mem_k, w_mem_v, fox_q_gain, fox_k_gain, swa_q_gain, swa_k_gain, swa_sinks, mem_q_gain, mem_k_gain, w_out, ffn2_norm, ffn2_gate, ffn2_up, ffn2_down, loss_target, m_ffn1_norm, m_ffn1_gate, m_ffn1_up, m_ffn1_down, m_mix_norm, m_mem_norm, m_w_in, m_forget_bias, m_w_mem_k, m_w_mem_v, m_fox_q_gain, m_fox_k_gain, m_swa_q_gain, m_swa_k_gain, m_swa_sinks, m_mem_q_gain, m_mem_k_gain, m_w_out, m_ffn2_norm, m_ffn2_gate, m_ffn2_up, m_ffn2_down, v_ffn1_norm, v_ffn1_gate, v_ffn1_up, v_ffn1_down, v_mix_norm, v_mem_norm, v_w_in, v_forget_bias, v_w_mem_k, v_w_mem_v, v_fox_q_gain, v_fox_k_gain, v_swa_q_gain, v_swa_k_gain, v_swa_sinks, v_mem_q_gain, v_mem_k_gain, v_w_out, v_ffn2_norm, v_ffn2_gate, v_ffn2_up, v_ffn2_down):
    given = dict(locals())
    T, D = x.shape[1], x.shape[2]
    ML = mem.shape[1]
    xin = x.reshape(T, D)
    target = loss_target.reshape(T, D)
    memin = mem.reshape(ML, D)

    ids = _place_ids()
    shard = {k: given[k][0] for k in LARGE}
    shard["w_in"] = _pad_proj_cols(shard["w_in"])
    started, after = [], ids
    for gi, group in enumerate(GATHER_GROUPS):
        placed = [_cast_place("cast_" + k, shard[k], ids) for k in group]
        send, recv, bufs, after = _gather_start("gather_start_%d" % gi, placed, after)
        started.append((send, recv, bufs))

    def arrive(gi, done):
        send, recv, bufs = started[gi]
        bufs = _gather_wait("gather_wait_%d" % gi, send, recv, bufs, done)
        return dict(zip(GATHER_GROUPS[gi], _gather_forward("gather_forward_%d" % gi, bufs)))

    gains = jnp.concatenate([fox_q_gain, fox_k_gain, swa_q_gain, swa_k_gain, mem_q_gain,
                             jnp.pad(forget_bias, ((0, 0), (0, HEAD_DIM - FOX_HEADS))), jnp.zeros((2, HEAD_DIM), F32)], axis=0)
    slopes_np = 2.0 ** (-8.0 * np.arange(1, SWA_HEADS + 1) / SWA_HEADS)
    slopes = jnp.asarray(np.repeat(slopes_np, WINDOW).reshape(SWA_KV_HEADS, GR, 1), F32)
    sinks = jnp.repeat(swa_sinks.reshape(SWA_HEADS), WINDOW).reshape(SWA_KV_HEADS, GR, 1)

    h1 = _rms_fwd("ffn1_norm_fwd", xin, ffn1_norm + after[0, 0])
    full = arrive(0, h1)
    wg1, wu1 = full["ffn1_gate"], full["ffn1_up"]
    g1, u1, a1 = _ffn_gu("ffn1_gate_up", h1, wg1, wu1)
    full = arrive(1, a1)
    wd1 = full["ffn1_down"].reshape(-1, D)
    win = full["w_in"].reshape(D, PROJ_W)
    wmk = full["w_mem_k"].reshape(D, MEM_HEADS * HEAD_DIM)
    wmv = full["w_mem_v"].reshape(D, MEM_HEADS * HEAD_DIM)
    x1 = _ffn_down("ffn1_down", a1, wd1, xin)
    h2 = _rms_fwd("mix_norm_fwd", x1, mix_norm)
    proj = _mm2d("proj_in", h2, win, "nn", F32, tn=1408, tk=2048, n_outer=True)
    qkv, logf = _prep_fwd(proj, gains)
    cum = _cumsum_rows("forget_cumsum", [logf], False)
    cum_h = cum[:, :FOX_HEADS].T
    cq, ck = cum_h.reshape(FOX_HEADS, T, 1), cum_h.reshape(FOX_HEADS, 1, T)
    mn = _rms_fwd("mem_norm_fwd", memin, mem_norm)
    mk_raw = _mm2d("mem_k_proj", mn, wmk, "nn", F32)
    mv = _mm2d("mem_v_proj", mn, wmv, "nn", BF16)
    mk = _head_norm_rows(mk_raw, mem_k_gain)
    out_a, out_a_f32, lse_a = _fox_fwd(qkv, cq, ck)
    out_b, lse_b = _swa_fwd(qkv, slopes, sinks)
    out_c, lse_c = _mem_fwd(qkv, mk, mv)
    mixed = jnp.concatenate([out_a, out_b, out_c], axis=1)
    full = arrive(2, mixed)
    wo = full["w_out"].reshape(-1, D)
    wg2, wu2, wd2 = full["ffn2_gate"], full["ffn2_up"], full["ffn2_down"].reshape(-1, D)
    x2 = _mm2d("mix_out", mixed, wo, "nn", F32, tk=2048, n_outer=True, extras=[x1], epilogue=lambda accs, ex: [ex[0] + accs[0]])
    h3 = _rms_fwd("ffn2_norm_fwd", x2, ffn2_norm)
    g2, u2, a2 = _ffn_gu("ffn2_gate_up", h3, wg2, wu2)
    x3 = _ffn_down("ffn2_down", a2, wd2, x2)
    dx3, dyb3, loss_part = _loss_head(x3, target)

    grads, small, res = {}, {"loss": loss_part[0, 0]}, {}

    def send_off(tag, group):
        theirs = _pair_exchange("grad_pair_exchange_" + tag, [grads[k] for k in group])
        to_chips = [_pair_sum_bf16("pair_sum_" + k, grads[k], b, ids) for k, b in zip(group, theirs)]
        send, recv, parts, lands, token = _chip_start("grad_chip_start_" + tag, to_chips)
        return (group, theirs, send, recv, parts, lands), token

    def pair_off(tag, group):
        send, recv, own, lands, token = _pair_start("grad_pair_start_" + tag, [grads[k] for k in group])
        return (group, send, recv, own, lands), token

    def chip_off(tag, started, done):
        group, send, recv, own, lands = started
        own, theirs = _pair_wait("grad_pair_wait_" + tag, send, recv, own, lands, done)
        grads.update(zip(group, own))
        to_chips = [_pair_sum_bf16("pair_sum_" + k, grads[k], b, ids) for k, b in zip(group, theirs)]
        send, recv, parts, lands, token = _chip_start("grad_chip_start_" + tag, to_chips)
        return (group, theirs, send, recv, parts, lands), token

    def finish(tag, state, done):
        group, theirs, send, recv, parts, lands = state
        arrived = _chip_wait("grad_chip_wait_" + tag, send, recv, parts, lands, done)
        halves = [_chip_sum("chip_sum_" + k, grads[k], b, r, ids) for k, b, r in zip(group, theirs, arrived)]
        reduced = dict(zip(group, _pair_share("grad_pair_share_" + tag, halves)))
        last = None
        for k in group:
            gk = _unpad_proj_cols(reduced[k]) if k == "w_in" else reduced[k]
            d, mo, vo = _adamw("adamw_" + k, given[k][0], gk, given["m_" + k][0], given["v_" + k][0])
            res[k] = tuple(t[None] for t in (gk, d, mo, vo))
            last = vo
        return last

    dg2, du2, grads["ffn2_gate"], grads["ffn2_up"], dwd2 = _ffn_bwd_w("ffn2", dyb3, h3, g2, u2, a2, wd2, N_CHIPS)
    grads["ffn2_down"] = dwd2.reshape(N_CHIPS, -1, D)
    started, token = pair_off("a", ["ffn2_gate", "ffn2_up", "ffn2_down"])
    dh3 = _ffn_bwd_x("ffn2", dg2, du2, wg2, wu2, token)
    state_a, token = chip_off("a", started, dh3)
    dx2, dx2b, small["ffn2_norm"] = _rms_bwd("ffn2_norm_bwd", dh3, x2, ffn2_norm + token[0, 0], dx3, 1.0)
    dmix = _mm2d("mix_out_dx", dx2b, wo, "nt", BF16, tk=2048, n_outer=True)
    grads["w_out"] = _mm2d("mix_out_dw", mixed, dx2b, "tn", F32, tk=1024).reshape(N_CHIPS, -1, D)
    dfq, dfk, dfv, dck, dcq = _fox_bwd(qkv, cq, ck, out_a_f32, lse_a, dmix)
    dsq, dsk, dsv, dsink = _swa_bwd(qkv, slopes, sinks, out_b, lse_b, dmix)
    dmq, dmk, dmv = _mem_bwd(qkv, mk, mv, out_c, lse_c, dmix)
    small["swa_sinks"] = dsink[:, :SWA_GROUP, 0].reshape(1, SWA_HEADS)
    dcum = [jnp.pad(d.reshape(FOX_HEADS, T).T, ((0, 0), (0, HEAD_DIM - FOX_HEADS))) for d in (dck, dcq)]
    dlogf = _cumsum_rows("forget_cumsum_bwd", dcum, True)
    dproj, dgains = _prep_bwd(proj, gains, dfq, dfk, dfv, dsq, dsk, dsv, dmq, dlogf)
    for row, k in enumerate(["fox_q_gain", "fox_k_gain", "swa_q_gain", "swa_k_gain", "mem_q_gain"]):
        small[k] = dgains[row:row + 1, :]
    small["forget_bias"] = dgains[5:6, :FOX_HEADS]
    grads["w_in"] = _mm2d("proj_in_dw", h2, dproj, "tn", F32, tn=1408, tk=1024).reshape(N_CHIPS, -1, PROJ_W)
    dmk_raw, small["mem_k_gain"] = _head_norm_rows_bwd(mk_raw, mem_k_gain, dmk)
    dmvb = dmv.astype(BF16)
    grads["w_mem_k"] = _mm2d("mem_k_dw", mn, dmk_raw, "tn", F32).reshape(N_CHIPS, -1, MEM_HEADS * HEAD_DIM)
    grads["w_mem_v"] = _mm2d("mem_v_dw", mn, dmvb, "tn", F32).reshape(N_CHIPS, -1, MEM_HEADS * HEAD_DIM)
    dmn = _mm2d("mem_k_dx", dmk_raw, wmk, "nt", F32)
    dmn = _mm2d("mem_v_dx", dmvb, wmv, "nt", F32, extras=[dmn], epilogue=lambda accs, ex: [ex[0] + accs[0]])
    _, _, small["mem_norm"] = _rms_bwd("mem_norm_bwd", dmn, memin, mem_norm, jnp.zeros_like(memin), 1.0)
    started, token = pair_off("b", ["w_out", "w_in", "w_mem_k", "w_mem_v"])
    dh2 = _mm2d("proj_in_dx", dproj, win, "nt", F32, tm=1024, tk=1408, after=token)
    state_b, token = chip_off("b", started, dh2)
    dx1, dyb1, small["mix_norm"] = _rms_bwd("mix_norm_bwd", dh2, x1, mix_norm + token[0, 0], dx2, 0.5)
    dg1, du1, grads["ffn1_gate"], grads["ffn1_up"], dwd1 = _ffn_bwd_w("ffn1", dyb1, h1, g1, u1, a1, wd1, N_CHIPS)
    grads["ffn1_down"] = dwd1.reshape(N_CHIPS, -1, D)
    state_c, token_c = send_off("c", ["ffn1_gate", "ffn1_up", "ffn1_down"])
    dh1 = _ffn_bwd_x("ffn1", dg1, du1, wg1, wu1, token_c)
    grad_x, _, small["ffn1_norm"] = _rms_bwd("ffn1_norm_bwd", dh1, xin, ffn1_norm, dx1, 1.0)

    shapes = {k: given[k].shape for k in SMALL}
    shapes["loss"] = ()
    red_small = _unpack_small(_all_reduce_small(_pack_small(small)), shapes)
    loss = red_small["loss"]
    zero = {"loss": jnp.zeros((), F32)}
    packed = [_pack_small({**zero, **{k: src[k] for k in SMALL}}) for src in (
        {k: given[k] for k in SMALL}, red_small, {k: given["m_" + k] for k in SMALL}, {k: given["v_" + k] for k in SMALL})]
    small_out = _adamw("adamw_small", *packed)
    d_s, m_s, v_s = (_unpack_small(t, shapes) for t in small_out)
    for k in SMALL:
        res[k] = (red_small[k], d_s[k], m_s[k], v_s[k])

    done = finish("a", state_a, small_out[0])
    done = finish("b", state_b, done)
    finish("c", state_c, done)

    outs = [loss, grad_x.reshape(1, T, D)]
    for part in range(4):
        outs += [res[k][part] for k in WEIGHTS]
    return tuple(outs)
```

```python
import functools

import numpy as np
import jax
import jax.numpy as jnp
from jax import lax
from jax.experimental import pallas as pl
from jax.experimental.pallas import tpu as pltpu

F32 = jnp.float32
BF16 = jnp.bfloat16
MESH = pl.DeviceIdType.MESH

HEAD_DIM = 128
FOX_HEADS = 6
SWA_HEADS = 6
SWA_KV_HEADS = 2
SWA_GROUP = SWA_HEADS // SWA_KV_HEADS
MEM_HEADS = 4
WINDOW = 128
EPS = 1e-6
NEG_INF = -1e30
SCALE = HEAD_DIM ** -0.5

C_FQ = 0
C_FK = C_FQ + FOX_HEADS * HEAD_DIM
C_FV = C_FK + FOX_HEADS * HEAD_DIM
C_SQ = C_FV + FOX_HEADS * HEAD_DIM
C_SK = C_SQ + SWA_HEADS * HEAD_DIM
C_SV = C_SK + SWA_KV_HEADS * HEAD_DIM
C_MQ = C_SV + SWA_KV_HEADS * HEAD_DIM
C_FL = C_MQ + MEM_HEADS * HEAD_DIM
PROJ_W = C_FL + HEAD_DIM
FOX_W = FOX_HEADS * HEAD_DIM
REF_GROUPS = [
    (0, FOX_W, C_FQ), (FOX_W, FOX_W, C_FK), (2 * FOX_W, FOX_W, C_FV), (3 * FOX_W, FOX_HEADS, C_FL),
    (3 * FOX_W + FOX_HEADS, SWA_HEADS * HEAD_DIM, C_SQ),
    (3 * FOX_W + FOX_HEADS + SWA_HEADS * HEAD_DIM, SWA_KV_HEADS * HEAD_DIM, C_SK),
    (3 * FOX_W + FOX_HEADS + (SWA_HEADS + SWA_KV_HEADS) * HEAD_DIM, SWA_KV_HEADS * HEAD_DIM, C_SV),
    (3 * FOX_W + FOX_HEADS + (SWA_HEADS + 2 * SWA_KV_HEADS) * HEAD_DIM, MEM_HEADS * HEAD_DIM, C_MQ),
]

ADAM_LR = 0.001
ADAM_B1 = 0.9
ADAM_B2 = 0.999
ADAM_EPS = 1e-08
ADAM_WD = 0.01
ADAM_STEP = 10

V7X_VMEM_LIMIT = 56 * 1024 * 1024
N_CHIPS = 4
N_DEV = 8


def _tile(n, pref, mult=128):
    t = (min(pref, n) // mult) * mult
    while t >= mult:
        if n % t == 0:
            return t
        t -= mult
    return n


def _params(sem):
    return pltpu.CompilerParams(dimension_semantics=sem, vmem_limit_bytes=V7X_VMEM_LIMIT)


_DIMS = {"nn": (((1,), (0,)), ((), ())), "nt": (((1,), (1,)), ((), ())), "tn": (((0,), (0,)), ((), ()))}


def _dot(a, b, mode):
    return lax.dot_general(a, b, _DIMS[mode], preferred_element_type=F32)


def _mm(name, grid, pairs, acc_of, acc_shapes, extras, outs, epilogue, after=None):
    n_p, n_e, n_o, n_a = len(pairs), len(extras), len(outs), len(acc_shapes)
    n_w = 0 if after is None else 1
    nk = grid[2]
    n_in = sum(1 if a is None else 2 for a, *_ in pairs)

    def body(*refs):
        ex = refs[n_in:n_in + n_e]
        out = refs[n_in + n_e + n_w:n_in + n_e + n_w + n_o]
        accs = refs[n_in + n_e + n_w + n_o:]
        parts = [None] * n_a
        at = 0
        for p in range(n_p):
            if pairs[p][0] is None:
                a_ref, b_ref = refs[0], refs[at]
                at += 1
            else:
                a_ref, b_ref = refs[at], refs[at + 1]
                at += 2
            d = _dot(a_ref[...], b_ref[...], pairs[p][4])
            parts[acc_of[p]] = d if parts[acc_of[p]] is None else parts[acc_of[p]] + d

        def finish(vals):
            for o, r in zip(out, epilogue(vals, [e[...] for e in ex])):
                o[...] = r.astype(o.dtype)

        if nk == 1:
            finish(parts)
            return
        k = pl.program_id(2)

        @pl.when(k == 0)
        def _():
            for a, d in zip(accs, parts):
                a[...] = d

        @pl.when((k > 0) & (k < nk - 1))
        def _():
            for a, d in zip(accs, parts):
                a[...] += d

        @pl.when(k == nk - 1)
        def _():
            finish([a[...] + d for a, d in zip(accs, parts)])

    in_specs, args = [], []
    for a, a_spec, b, b_spec, _ in pairs:
        if a is not None:
            in_specs.append(a_spec)
            args.append(a)
        in_specs.append(b_spec)
        args.append(b)
    for e, e_spec in extras:
        in_specs.append(e_spec)
        args.append(e)
    if after is not None:
        in_specs.append(pl.BlockSpec(memory_space=pl.ANY))
        args.append(after)
    res = pl.pallas_call(
        body, name=name, grid=grid, in_specs=in_specs,
        out_specs=[s for _, s in outs], out_shape=[o for o, _ in outs],
        scratch_shapes=[pltpu.VMEM(s, F32) for s in acc_shapes] if nk > 1 else [],
        compiler_params=_params(("parallel", "parallel", "arbitrary")),
    )(*args)
    return res


def _mm2d(name, a, b, mode, out_dtype, tm=512, tn=1024, tk=1024, extras=(), epilogue=None, n_out=1, after=None, n_outer=False):
    if mode == "nn":
        (M, K), N = a.shape, b.shape[1]
    elif mode == "nt":
        (M, K), N = a.shape, b.shape[0]
    else:
        (K, M), N = a.shape, b.shape[1]
    tm, tn, tk = _tile(M, tm), _tile(N, tn), _tile(K, tk)

    def spec(shape, index):
        if n_outer:
            return pl.BlockSpec(shape, lambda j, i, k: index(i, j, k))
        return pl.BlockSpec(shape, index)

    a_spec = spec((tk, tm), lambda i, j, k: (k, i)) if mode == "tn" else spec((tm, tk), lambda i, j, k: (i, k))
    b_spec = spec((tn, tk), lambda i, j, k: (j, k)) if mode == "nt" else spec((tk, tn), lambda i, j, k: (k, j))
    mn = spec((tm, tn), lambda i, j, k: (i, j))
    if epilogue is None:
        epilogue = lambda accs, ex: [accs[0]]
    if not isinstance(out_dtype, (list, tuple)):
        out_dtype = [out_dtype] * n_out
    grid = (N // tn, M // tm, K // tk) if n_outer else (M // tm, N // tn, K // tk)
    res = _mm(name, grid, [(a, a_spec, b, b_spec, mode)], [0], [(tm, tn)],
              [(e, mn) for e in extras], [(jax.ShapeDtypeStruct((M, N), d), mn) for d in out_dtype], epilogue, after=after)
    return res[0] if len(res) == 1 else res


def _sigmoid(x):
    return 1.0 / (1.0 + jnp.exp(-x))


def _sigmoid_fast(x):
    return pl.reciprocal(1.0 + jnp.exp(-x), approx=True)


def _ffn_gu(name, h, wg, wu):
    T, D = h.shape
    nf, _, F4 = wg.shape
    tm, tk = _tile(T, 512), _tile(D, 2048)
    a_spec = pl.BlockSpec((tm, tk), lambda j, i, k: (i, k))
    b_spec = pl.BlockSpec((None, tk, F4), lambda j, i, k: (j, k, 0))
    o_spec = pl.BlockSpec((tm, F4), lambda j, i, k: (i, j))

    def epilogue(accs, ex):
        g, u = accs
        return [g, u, g * _sigmoid_fast(g) * u]

    sds = jax.ShapeDtypeStruct((T, nf * F4), BF16)
    return _mm(name, (nf, T // tm, D // tk), [(h, a_spec, wg, b_spec, "nn"), (None, None, wu, b_spec, "nn")], [0, 1],
               [(tm, F4), (tm, F4)], [], [(sds, o_spec)] * 3, epilogue)


def _ffn_down(name, a, wd, xres):
    return _mm2d(name, a, wd, "nn", F32, tm=1024, tk=wd.shape[0] // N_CHIPS, extras=[xres],
                 epilogue=lambda accs, ex: [ex[0] + 0.5 * accs[0]])


def _ffn_bwd_down(tag, a, dyb, nf, after=None):
    return _mm2d(tag + "_dwd", a, dyb, "tn", F32, tm=a.shape[1] // nf, tk=1024, after=after)


def _ffn_bwd_act(tag, dyb, wd, g, u, nf, after=None):
    def act_bwd(accs, ex):
        gf, uf = ex[0].astype(F32), ex[1].astype(F32)
        s = _sigmoid_fast(gf)
        return [accs[0] * uf * s * (1.0 + gf * (1.0 - s)), accs[0] * gf * s]

    return _mm2d(tag + "_da", dyb, wd, "nt", BF16, tn=wd.shape[0] // nf, tk=2048, extras=[g, u], epilogue=act_bwd, n_out=2,
                 n_outer=True, after=after)


def _ffn_bwd_gate_up(tag, h, dg, du, nf, after=None):
    T, D = h.shape
    F4 = dg.shape[1] // nf
    tm, tk = _tile(D, 512), _tile(T, 1024)
    h_spec = pl.BlockSpec((tk, tm), lambda i, j, k: (k, i))
    d_spec = pl.BlockSpec((tk, F4), lambda i, j, k: (k, j))
    w_spec = pl.BlockSpec((None, tm, F4), lambda i, j, k: (j, i, 0))
    sds = jax.ShapeDtypeStruct((nf, D, F4), F32)
    return _mm(tag + "_dwgu", (D // tm, nf, T // tk), [(h, h_spec, dg, d_spec, "tn"), (None, None, du, d_spec, "tn")],
               [0, 1], [(tm, F4), (tm, F4)], [], [(sds, w_spec)] * 2, lambda accs, ex: accs, after=after)


def _ffn_bwd_x(tag, dg, du, wg, wu, after):
    T = dg.shape[0]
    nf, D, F4 = wg.shape
    tm, tn = _tile(T, 1024), _tile(D, 1024)
    a_spec = pl.BlockSpec((tm, F4), lambda i, j, k: (i, k))
    b_spec = pl.BlockSpec((None, tn, F4), lambda i, j, k: (k, j, 0))
    o_spec = pl.BlockSpec((tm, tn), lambda i, j, k: (i, j))
    (dh,) = _mm(tag + "_dh", (T // tm, D // tn, nf), [(dg, a_spec, wg, b_spec, "nt"), (du, a_spec, wu, b_spec, "nt")],
                [0, 0], [(tm, tn)], [], [(jax.ShapeDtypeStruct((T, D), F32), o_spec)], lambda accs, ex: accs, after=after)
    return dh


def _rms_fwd(name, x, gain):
    R, D = x.shape
    tr = _tile(R, 256, 8)

    def body(x_ref, g_ref, o_ref):
        xv = x_ref[...]
        r = lax.rsqrt(jnp.mean(xv * xv, axis=-1, keepdims=True) + EPS)
        o_ref[...] = (xv * r * g_ref[...]).astype(BF16)

    return pl.pallas_call(
        body, name=name, grid=(R // tr,),
        in_specs=[pl.BlockSpec((tr, D), lambda i: (i, 0)), pl.BlockSpec((1, D), lambda i: (0, 0))],
        out_specs=pl.BlockSpec((tr, D), lambda i: (i, 0)), out_shape=jax.ShapeDtypeStruct((R, D), BF16),
        compiler_params=_params(("parallel",)),
    )(x, gain)


def _rms_bwd(name, dh, x, gain, dres, bscale):
    R, D = x.shape
    tr = _tile(R, 256, 8)

    def body(dh_ref, x_ref, g_ref, dres_ref, dx_ref, dxb_ref, dg_ref):
        xv, dy = x_ref[...], dh_ref[...]
        r = lax.rsqrt(jnp.mean(xv * xv, axis=-1, keepdims=True) + EPS)
        xn = xv * r
        uu = dy * g_ref[...]
        dx = dres_ref[...] + r * (uu - xn * jnp.mean(xn * uu, axis=-1, keepdims=True))
        dx_ref[...] = dx
        dxb_ref[...] = (bscale * dx).astype(BF16)
        part = jnp.sum(dy * xn, axis=0, keepdims=True)

        @pl.when(pl.program_id(0) == 0)
        def _():
            dg_ref[...] = part

        @pl.when(pl.program_id(0) > 0)
        def _():
            dg_ref[...] += part

    row = pl.BlockSpec((tr, D), lambda i: (i, 0))
    vec = pl.BlockSpec((1, D), lambda i: (0, 0))
    return pl.pallas_call(
        body, name=name, grid=(R // tr,), in_specs=[row, row, vec, row], out_specs=[row, row, vec],
        out_shape=[jax.ShapeDtypeStruct((R, D), F32), jax.ShapeDtypeStruct((R, D), BF16), jax.ShapeDtypeStruct((1, D), F32)],
        compiler_params=_params(("arbitrary",)),
    )(dh, x, gain, dres)


def _loss_head(y, target):
    R, D = y.shape
    tr = _tile(R, 256, 8)

    def body(y_ref, t_ref, d_ref, db_ref, l_ref):
        e = y_ref[...] - t_ref[...]
        d = e * (1.0 / D)
        d_ref[...] = d
        db_ref[...] = (0.5 * d).astype(BF16)
        part = jnp.zeros((8, 128), F32) + (0.5 / D) * jnp.sum(e * e)

        @pl.when(pl.program_id(0) == 0)
        def _():
            l_ref[...] = part

        @pl.when(pl.program_id(0) > 0)
        def _():
            l_ref[...] += part

    row = pl.BlockSpec((tr, D), lambda i: (i, 0))
    acc = pl.BlockSpec((8, 128), lambda i: (0, 0))
    return pl.pallas_call(
        body, name="loss_head", grid=(R // tr,), in_specs=[row, row], out_specs=[row, row, acc],
        out_shape=[jax.ShapeDtypeStruct((R, D), F32), jax.ShapeDtypeStruct((R, D), BF16), jax.ShapeDtypeStruct((8, 128), F32)],
        compiler_params=_params(("arbitrary",)),
    )(y, target)


def _head_norm(xs, g):
    r = lax.rsqrt(jnp.mean(xs * xs, axis=-1, keepdims=True) + EPS)
    return xs * r * g


def _head_norm_bwd(xs, g, dy):
    r = lax.rsqrt(jnp.mean(xs * xs, axis=-1, keepdims=True) + EPS)
    xn = xs * r
    uu = dy * g
    return r * (uu - xn * jnp.mean(xn * uu, axis=-1, keepdims=True)), jnp.sum(dy * xn, axis=0, keepdims=True)


NORMED = [(C_FQ, FOX_HEADS, 0), (C_FK, FOX_HEADS, 1), (C_SQ, SWA_HEADS, 2), (C_SK, SWA_KV_HEADS, 3), (C_MQ, MEM_HEADS, 4)]
PLAIN = [(C_FV, FOX_HEADS), (C_SV, SWA_KV_HEADS)]


def _prep_fwd(proj, gains):
    T = proj.shape[0]
    tr = _tile(T, 256, 8)

    def body(p_ref, g_ref, o_ref, lf_ref):
        for start, heads, row in NORMED:
            gn = g_ref[row:row + 1, :]
            for hh in range(heads):
                sl = slice(start + hh * HEAD_DIM, start + (hh + 1) * HEAD_DIM)
                o_ref[:, sl] = _head_norm(p_ref[:, sl], gn).astype(BF16)
        for start, heads in PLAIN:
            sl = slice(start, start + heads * HEAD_DIM)
            o_ref[:, sl] = p_ref[:, sl].astype(BF16)
        zb = p_ref[:, C_FL:C_FL + HEAD_DIM] + g_ref[5:6, :]
        o_ref[:, C_FL:C_FL + HEAD_DIM] = jnp.zeros((tr, HEAD_DIM), BF16)
        lf_ref[...] = jnp.minimum(zb, 0.0) - jnp.log(1.0 + jnp.exp(-jnp.abs(zb)))

    return pl.pallas_call(
        body, name="prep_fwd", grid=(T // tr,),
        in_specs=[pl.BlockSpec((tr, PROJ_W), lambda i: (i, 0)), pl.BlockSpec((8, 128), lambda i: (0, 0))],
        out_specs=[pl.BlockSpec((tr, PROJ_W), lambda i: (i, 0)), pl.BlockSpec((tr, HEAD_DIM), lambda i: (i, 0))],
        out_shape=[jax.ShapeDtypeStruct((T, PROJ_W), BF16), jax.ShapeDtypeStruct((T, HEAD_DIM), F32)],
        compiler_params=_params(("parallel",)),
    )(proj, gains)


def _prep_bwd(proj, gains, dfq, dfk, dfv, dsq, dsk, dsv, dmq, dlogf):
    T = proj.shape[0]
    tr = _tile(T, 256, 8)
    d_normed = {C_FQ: 0, C_FK: 1, C_SQ: 3, C_SK: 4, C_MQ: 6}
    d_plain = {C_FV: 2, C_SV: 5}

    def body(p_ref, g_ref, *rest):
        d_refs, dlf_ref, o_ref, dg_ref = rest[:7], rest[7], rest[8], rest[9]
        rows = []
        for start, heads, row in NORMED:
            gn = g_ref[row:row + 1, :]
            d_ref = d_refs[d_normed[start]]
            tot = jnp.zeros((1, HEAD_DIM), F32)
            for hh in range(heads):
                sl = slice(start + hh * HEAD_DIM, start + (hh + 1) * HEAD_DIM)
                dx, dgn = _head_norm_bwd(p_ref[:, sl], gn, d_ref[:, hh * HEAD_DIM:(hh + 1) * HEAD_DIM])
                o_ref[:, sl] = dx.astype(BF16)
                tot = tot + dgn
            rows.append(tot)
        for start, heads in PLAIN:
            o_ref[:, start:start + heads * HEAD_DIM] = d_refs[d_plain[start]][...].astype(BF16)
        zb = p_ref[:, C_FL:C_FL + HEAD_DIM] + g_ref[5:6, :]
        lane = lax.broadcasted_iota(jnp.int32, (tr, HEAD_DIM), 1)
        dz = jnp.where(lane < FOX_HEADS, dlf_ref[...] * (1.0 - _sigmoid(zb)), 0.0)
        o_ref[:, C_FL:C_FL + HEAD_DIM] = dz.astype(BF16)
        rows.append(jnp.sum(dz, axis=0, keepdims=True))
        part = jnp.concatenate(rows + [jnp.zeros((2, HEAD_DIM), F32)], axis=0)

        @pl.when(pl.program_id(0) == 0)
        def _():
            dg_ref[...] = part

        @pl.when(pl.program_id(0) > 0)
        def _():
            dg_ref[...] += part

    def rows_of(w):
        return pl.BlockSpec((tr, w), lambda i: (i, 0))

    small = pl.BlockSpec((8, 128), lambda i: (0, 0))
    ds = [dfq, dfk, dfv, dsq, dsk, dsv, dmq]
    return pl.pallas_call(
        body, name="prep_bwd", grid=(T // tr,),
        in_specs=[rows_of(PROJ_W), small] + [rows_of(d.shape[1]) for d in ds] + [rows_of(HEAD_DIM)],
        out_specs=[rows_of(PROJ_W), small],
        out_shape=[jax.ShapeDtypeStruct((T, PROJ_W), BF16), jax.ShapeDtypeStruct((8, 128), F32)],
        compiler_params=_params(("arbitrary",)),
    )(proj, gains, *ds, dlogf)


def _head_norm_rows(x, gain):
    R, W = x.shape

    def body(x_ref, g_ref, o_ref):
        for hh in range(W // HEAD_DIM):
            sl = slice(hh * HEAD_DIM, (hh + 1) * HEAD_DIM)
            o_ref[:, sl] = _head_norm(x_ref[:, sl], g_ref[...]).astype(BF16)

    return pl.pallas_call(body, name="mem_k_norm", out_shape=jax.ShapeDtypeStruct((R, W), BF16))(x, gain)


def _head_norm_rows_bwd(x, gain, dy):
    R, W = x.shape

    def body(x_ref, g_ref, dy_ref, dx_ref, dg_ref):
        tot = jnp.zeros((1, HEAD_DIM), F32)
        for hh in range(W // HEAD_DIM):
            sl = slice(hh * HEAD_DIM, (hh + 1) * HEAD_DIM)
            dx, dgn = _head_norm_bwd(x_ref[:, sl], g_ref[...], dy_ref[:, sl])
            dx_ref[:, sl] = dx.astype(BF16)
            tot = tot + dgn
        dg_ref[...] = tot

    return pl.pallas_call(
        body, name="mem_k_norm_bwd",
        out_shape=[jax.ShapeDtypeStruct((R, W), BF16), jax.ShapeDtypeStruct((1, HEAD_DIM), F32)])(x, gain, dy)


def _cumsum_rows(name, xs, reverse):
    T, W = xs[0].shape
    tb = _tile(T, 512, 8)
    nb = T // tb

    def body(*refs):
        o_ref, carry = refs[len(xs)], refs[len(xs) + 1]

        @pl.when(pl.program_id(0) == 0)
        def _():
            carry[...] = jnp.zeros_like(carry)

        xv = refs[0][...]
        for x_ref in refs[1:len(xs)]:
            xv = xv + x_ref[...]
        r = lax.broadcasted_iota(jnp.int32, (tb, tb), 0)
        cc = lax.broadcasted_iota(jnp.int32, (tb, tb), 1)
        tri = jnp.where((cc >= r) if reverse else (cc <= r), 1.0, 0.0).astype(F32)
        o_ref[...] = jnp.dot(tri, xv, precision=lax.Precision.HIGHEST, preferred_element_type=F32) + carry[...]
        carry[...] += jnp.sum(xv, axis=0, keepdims=True)

    idx = (lambda i: (nb - 1 - i, 0)) if reverse else (lambda i: (i, 0))
    return pl.pallas_call(
        body, name=name, grid=(nb,), in_specs=[pl.BlockSpec((tb, W), idx)] * len(xs), out_specs=pl.BlockSpec((tb, W), idx),
        out_shape=jax.ShapeDtypeStruct((T, W), F32), scratch_shapes=[pltpu.VMEM((1, W), F32)],
        compiler_params=_params(("arbitrary",)),
    )(*xs)


def _triangle(nq, by_column):
    if by_column:
        blocks = [(i, j) for j in range(nq) for i in range(j, nq)]
    else:
        blocks = [(i, j) for i in range(nq) for j in range(i + 1)]
    return jnp.asarray(np.array(blocks, np.int32).T)


def _fox_scores(q, k, cq, ck, on_diagonal):
    s = _dot(q, k, "nt") * SCALE + (cq - ck)
    if on_diagonal:
        n = s.shape[0]
        s = jnp.where(lax.broadcasted_iota(jnp.int32, (n, n), 1) <= lax.broadcasted_iota(jnp.int32, (n, n), 0), s, NEG_INF)
    return s


def _fox_fwd(qkv, cq, ck):
    T = qkv.shape[0]
    tq = _tile(T, 512)
    nq = T // tq
    steps = nq * (nq + 1) // 2
    HQ, HK, HV = C_FQ // HEAD_DIM, C_FK // HEAD_DIM, C_FV // HEAD_DIM

    def body(tab, q_ref, k_ref, v_ref, cq_ref, ck_ref, o_ref, of_ref, lse_ref, m_sc, l_sc, acc_sc):
        i, j = tab[0, pl.program_id(1)], tab[1, pl.program_id(1)]

        @pl.when(j == 0)
        def _():
            m_sc[...] = jnp.full_like(m_sc, NEG_INF)
            l_sc[...] = jnp.zeros_like(l_sc)
            acc_sc[...] = jnp.zeros_like(acc_sc)

        def step(on_diagonal):
            s = _fox_scores(q_ref[...], k_ref[...], cq_ref[...], ck_ref[...], on_diagonal)
            m_new = jnp.maximum(m_sc[...], jnp.max(s, axis=-1, keepdims=True))
            alpha = jnp.exp(m_sc[...] - m_new)
            p = jnp.exp(s - m_new)
            l_sc[...] = alpha * l_sc[...] + jnp.sum(p, axis=-1, keepdims=True)
            acc_sc[...] = alpha * acc_sc[...] + _dot(p.astype(BF16), v_ref[...], "nn")
            m_sc[...] = m_new

        @pl.when(j < i)
        def _():
            step(False)

        @pl.when(j == i)
        def _():
            step(True)
            o = acc_sc[...] / l_sc[...]
            o_ref[...] = o.astype(BF16)
            of_ref[...] = o
            lse_ref[...] = m_sc[...] + jnp.log(l_sc[...])

    def rows(base):
        return pl.BlockSpec((tq, HEAD_DIM), lambda h, s, tab: (tab[0, s], base + h))

    def cols(base):
        return pl.BlockSpec((tq, HEAD_DIM), lambda h, s, tab: (tab[1, s], base + h))

    rvec = pl.BlockSpec((None, tq, 1), lambda h, s, tab: (h, tab[0, s], 0))
    return pl.pallas_call(
        body, name="fox_fwd",
        grid_spec=pltpu.PrefetchScalarGridSpec(
            num_scalar_prefetch=1, grid=(FOX_HEADS, steps),
            in_specs=[rows(HQ), cols(HK), cols(HV), rvec, pl.BlockSpec((None, 1, tq), lambda h, s, tab: (h, 0, tab[1, s]))],
            out_specs=[rows(0), rows(0), rvec],
            scratch_shapes=[pltpu.VMEM((tq, 1), F32), pltpu.VMEM((tq, 1), F32), pltpu.VMEM((tq, HEAD_DIM), F32)]),
        out_shape=[jax.ShapeDtypeStruct((T, FOX_W), BF16), jax.ShapeDtypeStruct((T, FOX_W), F32),
                   jax.ShapeDtypeStruct((FOX_HEADS, T, 1), F32)],
        compiler_params=_params(("parallel", "arbitrary")),
    )(_triangle(nq, False), qkv, qkv, qkv, cq, ck)


def _fox_bwd(qkv, cq, ck, out_f32, lse, dmix):
    T = qkv.shape[0]
    tq = _tile(T, 512)
    nq = T // tq
    steps = nq * (nq + 1) // 2
    HQ, HK, HV = C_FQ // HEAD_DIM, C_FK // HEAD_DIM, C_FV // HEAD_DIM

    def body(tab, q_ref, k_ref, v_ref, cq_ref, ck_ref, o_ref, lse_ref, do_ref, dq_ref, dk_ref, dv_ref, dck_ref, dcq_ref,
             dk_sc, dv_sc, dc_sc):
        qi, kj = tab[0, pl.program_id(1)], tab[1, pl.program_id(1)]

        @pl.when(qi == kj)
        def _():
            dk_sc[...] = jnp.zeros_like(dk_sc)
            dv_sc[...] = jnp.zeros_like(dv_sc)
            dc_sc[...] = jnp.zeros_like(dc_sc)

        def step(on_diagonal):
            q, k, v, do = q_ref[...], k_ref[...], v_ref[...], do_ref[...]
            p = jnp.exp(_fox_scores(q, k, cq_ref[...], ck_ref[...], on_diagonal) - lse_ref[...])
            dp = _dot(do, v, "nt")
            delta = jnp.sum(do.astype(F32) * o_ref[...], axis=-1, keepdims=True)
            ds = p * (dp - delta)
            dsb = ds.astype(BF16)
            dv_sc[...] += _dot(p.astype(BF16), do, "tn")
            dk_sc[...] += _dot(dsb, q, "tn")
            dc_sc[...] += jnp.sum(ds, axis=0, keepdims=True)
            dq_part = _dot(dsb, k, "nn") * SCALE
            dcq_part = jnp.sum(ds, axis=-1, keepdims=True)
            rows_q = pl.ds(pl.multiple_of(qi * tq, tq), tq)

            @pl.when(kj == 0)
            def _():
                dq_ref[rows_q, :] = dq_part
                dcq_ref[rows_q, :] = dcq_part

            @pl.when(kj > 0)
            def _():
                dq_ref[rows_q, :] += dq_part
                dcq_ref[rows_q, :] += dcq_part

        @pl.when(qi > kj)
        def _():
            step(False)

        @pl.when(qi == kj)
        def _():
            step(True)

        @pl.when(qi == nq - 1)
        def _():
            dk_ref[...] = dk_sc[...] * SCALE
            dv_ref[...] = dv_sc[...]
            dck_ref[...] = -dc_sc[...]

    def rows(base):
        return pl.BlockSpec((tq, HEAD_DIM), lambda h, s, tab: (tab[0, s], base + h))

    def cols(base):
        return pl.BlockSpec((tq, HEAD_DIM), lambda h, s, tab: (tab[1, s], base + h))

    rvec = pl.BlockSpec((None, tq, 1), lambda h, s, tab: (h, tab[0, s], 0))
    cvec = pl.BlockSpec((None, 1, tq), lambda h, s, tab: (h, 0, tab[1, s]))
    sds = jax.ShapeDtypeStruct((T, FOX_W), F32)
    return pl.pallas_call(
        body, name="fox_bwd",
        grid_spec=pltpu.PrefetchScalarGridSpec(
            num_scalar_prefetch=1, grid=(FOX_HEADS, steps),
            in_specs=[rows(HQ), cols(HK), cols(HV), rvec, cvec, rows(0), rvec, rows(0)],
            out_specs=[pl.BlockSpec((T, HEAD_DIM), lambda h, s, tab: (0, h)), cols(0), cols(0), cvec,
                       pl.BlockSpec((None, T, 1), lambda h, s, tab: (h, 0, 0))],
            scratch_shapes=[pltpu.VMEM((tq, HEAD_DIM), F32), pltpu.VMEM((tq, HEAD_DIM), F32), pltpu.VMEM((1, tq), F32)]),
        out_shape=[sds, sds, sds, jax.ShapeDtypeStruct((FOX_HEADS, 1, T), F32), jax.ShapeDtypeStruct((FOX_HEADS, T, 1), F32)],
        compiler_params=_params(("parallel", "arbitrary")),
    )(_triangle(nq, True), qkv, qkv, qkv, cq, ck, out_f32, lse, dmix)


GW = SWA_GROUP * HEAD_DIM
GR = SWA_GROUP * WINDOW


def _swa_scores(q_ref, kp_ref, kc_ref, slope_ref, n):
    q = q_ref[...]
    qs = jnp.concatenate([q[:, t * HEAD_DIM:(t + 1) * HEAD_DIM] for t in range(SWA_GROUP)], axis=0)
    kb = jnp.concatenate([kp_ref[...], kc_ref[...]], axis=0)
    r = lax.broadcasted_iota(jnp.int32, (GR, 2 * WINDOW), 0) & (WINDOW - 1)
    jj = lax.broadcasted_iota(jnp.int32, (GR, 2 * WINDOW), 1)
    dist = WINDOW + r - jj
    valid = (dist >= 0) & (dist < WINDOW) & ((n > 0) | (jj >= WINDOW))
    s = _dot(qs, kb, "nt") * SCALE - slope_ref[...] * dist.astype(F32)
    return qs, kb, jnp.where(valid, s, NEG_INF), valid


def _swa_specs():
    HQ, HK, HV = C_SQ // GW, C_SK // HEAD_DIM, C_SV // HEAD_DIM
    q_spec = pl.BlockSpec((WINDOW, GW), lambda g, n: (n, HQ + g))

    def prev(base):
        return pl.BlockSpec((WINDOW, HEAD_DIM), lambda g, n: (jnp.maximum(n - 1, 0), base + g))

    def cur(base):
        return pl.BlockSpec((WINDOW, HEAD_DIM), lambda g, n: (n, base + g))

    col = pl.BlockSpec((None, GR, 1), lambda g, n: (g, 0, 0))
    return q_spec, prev(HK), cur(HK), prev(HV), cur(HV), col


def _swa_fwd(qkv, slopes, sinks):
    T = qkv.shape[0]
    nb = T // WINDOW
    assert C_SQ % GW == 0

    def body(q_ref, kp_ref, kc_ref, vp_ref, vc_ref, slope_ref, sink_ref, o_ref, lse_ref):
        n = pl.program_id(1)
        _, _, s, _ = _swa_scores(q_ref, kp_ref, kc_ref, slope_ref, n)
        m = jnp.maximum(jnp.max(s, axis=-1, keepdims=True), sink_ref[...])
        p = jnp.exp(s - m)
        l = jnp.sum(p, axis=-1, keepdims=True) + jnp.exp(sink_ref[...] - m)
        vb = jnp.concatenate([vp_ref[...], vc_ref[...]], axis=0)
        o = _dot(p.astype(BF16), vb, "nn") / l
        for t in range(SWA_GROUP):
            o_ref[:, t * HEAD_DIM:(t + 1) * HEAD_DIM] = o[t * WINDOW:(t + 1) * WINDOW, :].astype(BF16)
        lse_ref[...] = m + jnp.log(l)

    q_spec, kp, kc, vp, vc, col = _swa_specs()
    return pl.pallas_call(
        body, name="swa_fwd", grid=(SWA_KV_HEADS, nb), in_specs=[q_spec, kp, kc, vp, vc, col, col],
        out_specs=[pl.BlockSpec((WINDOW, GW), lambda g, n: (n, g)), pl.BlockSpec((None, None, GR, 1), lambda g, n: (g, n, 0, 0))],
        out_shape=[jax.ShapeDtypeStruct((T, SWA_HEADS * HEAD_DIM), BF16), jax.ShapeDtypeStruct((SWA_KV_HEADS, nb, GR, 1), F32)],
        compiler_params=_params(("parallel", "arbitrary")),
    )(qkv, qkv, qkv, qkv, qkv, slopes, sinks)


def _swa_bwd(qkv, slopes, sinks, out, lse, dmix):
    T = qkv.shape[0]
    nb = T // WINDOW
    DO = FOX_W // GW
    assert FOX_W % GW == 0

    def body(q_ref, kp_ref, kc_ref, vp_ref, vc_ref, slope_ref, sink_ref, o_ref, lse_ref, do_ref,
             dq_ref, dk_ref, dv_ref, dsink_ref, sink_sc):
        n = pl.program_id(1)

        @pl.when(n == 0)
        def _():
            dk_ref[...] = jnp.zeros_like(dk_ref)
            dv_ref[...] = jnp.zeros_like(dv_ref)
            sink_sc[...] = jnp.zeros_like(sink_sc)

        qs, kb, s, valid = _swa_scores(q_ref, kp_ref, kc_ref, slope_ref, n)
        lse = lse_ref[...]
        p = jnp.where(valid, jnp.exp(s - lse), 0.0)
        vb = jnp.concatenate([vp_ref[...], vc_ref[...]], axis=0)
        do = jnp.concatenate([do_ref[:, t * HEAD_DIM:(t + 1) * HEAD_DIM] for t in range(SWA_GROUP)], axis=0)
        oo = jnp.concatenate([o_ref[:, t * HEAD_DIM:(t + 1) * HEAD_DIM] for t in range(SWA_GROUP)], axis=0)
        dp = _dot(do, vb, "nt")
        delta = jnp.sum(do.astype(F32) * oo.astype(F32), axis=-1, keepdims=True)
        ds = p * (dp - delta)
        dsb = ds.astype(BF16)
        dq = _dot(dsb, kb, "nn") * SCALE
        for t in range(SWA_GROUP):
            dq_ref[:, t * HEAD_DIM:(t + 1) * HEAD_DIM] = dq[t * WINDOW:(t + 1) * WINDOW, :]
        dkb = _dot(dsb, qs, "tn") * SCALE
        dvb = _dot(p.astype(BF16), do, "tn")
        r_prev = pl.ds(pl.multiple_of(jnp.maximum(n - 1, 0) * WINDOW, WINDOW), WINDOW)
        r_cur = pl.ds(pl.multiple_of(n * WINDOW, WINDOW), WINDOW)
        dk_ref[r_prev, :] += dkb[:WINDOW, :]
        dk_ref[r_cur, :] += dkb[WINDOW:, :]
        dv_ref[r_prev, :] += dvb[:WINDOW, :]
        dv_ref[r_cur, :] += dvb[WINDOW:, :]
        sink_sc[...] -= jnp.exp(sink_ref[...] - lse) * delta

        @pl.when(n == nb - 1)
        def _():
            tot = [jnp.zeros((1, 128), F32) + jnp.sum(sink_sc[t * WINDOW:(t + 1) * WINDOW, :]) for t in range(SWA_GROUP)]
            dsink_ref[...] = jnp.concatenate(tot + [jnp.zeros((8 - SWA_GROUP, 128), F32)], axis=0)

    q_spec, kp, kc, vp, vc, col = _swa_specs()
    kv_acc = pl.BlockSpec((T, HEAD_DIM), lambda g, n: (0, g))
    return pl.pallas_call(
        body, name="swa_bwd", grid=(SWA_KV_HEADS, nb),
        in_specs=[q_spec, kp, kc, vp, vc, col, col, pl.BlockSpec((WINDOW, GW), lambda g, n: (n, g)),
                  pl.BlockSpec((None, None, GR, 1), lambda g, n: (g, n, 0, 0)), pl.BlockSpec((WINDOW, GW), lambda g, n: (n, DO + g))],
        out_specs=[pl.BlockSpec((WINDOW, GW), lambda g, n: (n, g)), kv_acc, kv_acc, pl.BlockSpec((None, 8, 128), lambda g, n: (g, 0, 0))],
        out_shape=[jax.ShapeDtypeStruct((T, SWA_HEADS * HEAD_DIM), F32), jax.ShapeDtypeStruct((T, SWA_KV_HEADS * HEAD_DIM), F32),
                   jax.ShapeDtypeStruct((T, SWA_KV_HEADS * HEAD_DIM), F32), jax.ShapeDtypeStruct((SWA_KV_HEADS, 8, 128), F32)],
        scratch_shapes=[pltpu.VMEM((GR, 1), F32)],
        compiler_params=_params(("parallel", "arbitrary")),
    )(qkv, qkv, qkv, qkv, qkv, slopes, sinks, out, lse, dmix)


def _mem_fwd(qkv, mk, mv):
    T, ML = qkv.shape[0], mk.shape[0]
    tq = _tile(T, 512)
    HQ = C_MQ // HEAD_DIM

    def body(q_ref, k_ref, v_ref, o_ref, lse_ref):
        s = _dot(q_ref[...], k_ref[...], "nt") * SCALE
        m = jnp.max(s, axis=-1, keepdims=True)
        p = jnp.exp(s - m)
        l = jnp.sum(p, axis=-1, keepdims=True)
        o_ref[...] = (_dot(p.astype(BF16), v_ref[...], "nn") / l).astype(BF16)
        lse_ref[...] = m + jnp.log(l)

    kv = pl.BlockSpec((ML, HEAD_DIM), lambda h, i: (0, h))
    return pl.pallas_call(
        body, name="mem_fwd", grid=(MEM_HEADS, T // tq),
        in_specs=[pl.BlockSpec((tq, HEAD_DIM), lambda h, i: (i, HQ + h)), kv, kv],
        out_specs=[pl.BlockSpec((tq, HEAD_DIM), lambda h, i: (i, h)), pl.BlockSpec((None, tq, 1), lambda h, i: (h, i, 0))],
        out_shape=[jax.ShapeDtypeStruct((T, MEM_HEADS * HEAD_DIM), BF16), jax.ShapeDtypeStruct((MEM_HEADS, T, 1), F32)],
        compiler_params=_params(("parallel", "arbitrary")),
    )(qkv, mk, mv)


def _mem_bwd(qkv, mk, mv, out, lse, dmix):
    T, ML = qkv.shape[0], mk.shape[0]
    tq = _tile(T, 512)
    HQ = C_MQ // HEAD_DIM
    DO = (FOX_W + SWA_HEADS * HEAD_DIM) // HEAD_DIM

    def body(q_ref, k_ref, v_ref, o_ref, lse_ref, do_ref, dq_ref, dk_ref, dv_ref):
        q, k, v, do = q_ref[...], k_ref[...], v_ref[...], do_ref[...]
        p = jnp.exp(_dot(q, k, "nt") * SCALE - lse_ref[...])
        dp = _dot(do, v, "nt")
        delta = jnp.sum(do.astype(F32) * o_ref[...].astype(F32), axis=-1, keepdims=True)
        dsb = (p * (dp - delta)).astype(BF16)
        dq_ref[...] = _dot(dsb, k, "nn") * SCALE
        dk_part = _dot(dsb, q, "tn") * SCALE
        dv_part = _dot(p.astype(BF16), do, "tn")

        @pl.when(pl.program_id(1) == 0)
        def _():
            dk_ref[...] = dk_part
            dv_ref[...] = dv_part

        @pl.when(pl.program_id(1) > 0)
        def _():
            dk_ref[...] += dk_part
            dv_ref[...] += dv_part

    kv = pl.BlockSpec((ML, HEAD_DIM), lambda h, i: (0, h))
    qb = pl.BlockSpec((tq, HEAD_DIM), lambda h, i: (i, h))
    return pl.pallas_call(
        body, name="mem_bwd", grid=(MEM_HEADS, T // tq),
        in_specs=[pl.BlockSpec((tq, HEAD_DIM), lambda h, i: (i, HQ + h)), kv, kv, qb,
                  pl.BlockSpec((None, tq, 1), lambda h, i: (h, i, 0)), pl.BlockSpec((tq, HEAD_DIM), lambda h, i: (i, DO + h))],
        out_specs=[qb, kv, kv],
        out_shape=[jax.ShapeDtypeStruct((T, MEM_HEADS * HEAD_DIM), F32), jax.ShapeDtypeStruct((ML, MEM_HEADS * HEAD_DIM), F32),
                   jax.ShapeDtypeStruct((ML, MEM_HEADS * HEAD_DIM), F32)],
        compiler_params=_params(("parallel", "arbitrary")),
    )(qkv, mk, mv, out, lse, dmix)


HBM = pl.BlockSpec(memory_space=pltpu.HBM)


def _place():
    x, y, c = lax.axis_index("x"), lax.axis_index("y"), lax.axis_index("c")
    chips = [(1 - x, y), (x, 1 - y), (1 - x, 1 - y)]
    return x, y, c, chips


def _remote(src, dst, send_sem, recv_sem, device):
    return pltpu.make_async_remote_copy(src_ref=src, dst_ref=dst, send_sem=send_sem, recv_sem=recv_sem,
                                        device_id=device, device_id_type=MESH)


def _place_ids():
    x, y, c = lax.axis_index("x"), lax.axis_index("y"), lax.axis_index("c")
    order = [2 * x + y, 2 * (1 - x) + y, 2 * x + (1 - y), 2 * (1 - x) + (1 - y)]
    return jnp.stack([2 * x + y, c] + order).astype(jnp.int32)


def _cast_place(name, w, ids):
    R, C = w.shape
    tr = _tile(R, 256, 16)

    def body(ids_ref, w_ref, o_ref):
        o_ref[...] = w_ref[...].astype(BF16)

    return pl.pallas_call(
        body, name=name,
        grid_spec=pltpu.PrefetchScalarGridSpec(
            num_scalar_prefetch=1, grid=(R // tr,), in_specs=[pl.BlockSpec((tr, C), lambda i, ids: (i, 0))],
            out_specs=pl.BlockSpec((None, tr, C), lambda i, ids: (ids[0], i, 0))),
        out_shape=jax.ShapeDtypeStruct((N_CHIPS, R, C), BF16), compiler_params=_params(("parallel",)),
    )(ids, w)


SEM = pl.BlockSpec(memory_space=pltpu.SEMAPHORE)
EFFECT = pltpu.SideEffectType.DATAFLOW_SIDE_EFFECTING


def _hbm(a):
    return pltpu.with_memory_space_constraint(a, pltpu.HBM)


def _gather_start(name, placed, after):
    n = len(placed)

    ns = 3 * n

    def body(*refs):
        send, recv = refs[n + 1:n + 1 + ns], refs[n + 1 + ns:n + 1 + 2 * ns]
        buf = refs[n + 1 + 2 * ns:2 * n + 1 + 2 * ns]
        token = refs[2 * n + 1 + 2 * ns]
        x, y, c, chips = _place()
        me = 2 * x + y
        for a in range(n):
            half = buf[a].shape[1] // 2
            mine = buf[a].at[me, pl.ds(c * half, half)]
            for j, (cx, cy) in enumerate(chips):
                _remote(mine, mine, send[3 * a + j], recv[3 * a + j], (cx, cy, c)).start()
        token[...] = jnp.zeros_like(token)

    res = pl.pallas_call(
        body, name=name, in_specs=[HBM] * n + [pl.BlockSpec(memory_space=pl.ANY)],
        out_specs=[SEM] * (2 * ns) + [HBM] * n + [pl.BlockSpec(memory_space=pltpu.VMEM)],
        out_shape=[pltpu.SemaphoreType.DMA(())] * (2 * ns)
        + [pltpu.HBM(s.shape, s.dtype) for s in placed] + [jax.ShapeDtypeStruct((8, 128), F32)],
        input_output_aliases={a: 2 * ns + a for a in range(n)},
        compiler_params=pltpu.CompilerParams(has_side_effects=EFFECT),
    )(*[_hbm(s) for s in placed], after)
    return list(res[:ns]), list(res[ns:2 * ns]), list(res[2 * ns:2 * ns + n]), res[2 * ns + n]


def _gather_wait(name, send, recv, bufs, after):
    n = len(bufs)

    ns = 3 * n

    def body(*refs):
        buf = refs[:n]
        send_ref, recv_ref = refs[n:n + ns], refs[n + ns:n + 2 * ns]
        x, y, c, chips = _place()
        ids = [2 * cx + cy for cx, cy in chips]
        for a in range(n):
            half = buf[a].shape[1] // 2
            for j in range(3):
                landed = buf[a].at[ids[j], pl.ds(c * half, half)]
                cp = _remote(landed, landed, send_ref[3 * a + j], recv_ref[3 * a + j], (x, y, c))
                cp.wait_send()
                cp.wait_recv()

    res = pl.pallas_call(
        body, name=name, in_specs=[HBM] * n + [SEM] * (2 * ns) + [pl.BlockSpec(memory_space=pl.ANY)], out_specs=[HBM] * n,
        out_shape=[pltpu.HBM(s.shape, s.dtype) for s in bufs], input_output_aliases={a: a for a in range(n)},
        compiler_params=pltpu.CompilerParams(has_side_effects=EFFECT),
    )(*bufs, *send, *recv, after)
    return list(res)


def _gather_forward(name, bufs):
    n = len(bufs)

    def body(*refs):
        buf = refs[n:2 * n]
        send, recv = refs[2 * n:]
        x, y, c, chips = _place()
        ids = [2 * cx + cy for cx, cy in chips]
        copies = []
        for a in range(n):
            half = buf[a].shape[1] // 2
            for j in range(3):
                landed = buf[a].at[ids[j], pl.ds(c * half, half)]
                cp = _remote(landed, landed, send.at[a, j], recv.at[a, j], (x, y, 1 - c))
                cp.start()
                copies.append(cp)
        for a in range(n):
            half = buf[a].shape[1] // 2
            for j in range(3):
                landed = buf[a].at[ids[j], pl.ds((1 - c) * half, half)]
                _remote(landed, landed, send.at[a, j], recv.at[a, j], (x, y, c)).wait_recv()
        for cp in copies:
            cp.wait_send()

    return pl.pallas_call(
        body, name=name, in_specs=[HBM] * n, out_specs=[HBM] * n,
        out_shape=[jax.ShapeDtypeStruct(s.shape, s.dtype) for s in bufs], input_output_aliases={a: a for a in range(n)},
        scratch_shapes=[pltpu.SemaphoreType.DMA((n, 3)), pltpu.SemaphoreType.DMA((n, 3))],
    )(*bufs)


def _pair_start(name, grads):
    n = len(grads)
    ns = N_CHIPS * n

    def body(*refs):
        send, recv = refs[2 * n:2 * n + ns], refs[2 * n + ns:2 * n + 2 * ns]
        src = refs[2 * n + 2 * ns:3 * n + 2 * ns]
        land = refs[3 * n + 2 * ns:4 * n + 2 * ns]
        token = refs[4 * n + 2 * ns]
        x, y, c, chips = _place()
        order = [2 * x + y] + [2 * cx + cy for cx, cy in chips]
        for a in range(n):
            half = src[a].shape[1] // 2
            for j in range(N_CHIPS):
                _remote(src[a].at[order[j], pl.ds((1 - c) * half, half)], land[a].at[j],
                        send[N_CHIPS * a + j], recv[N_CHIPS * a + j], (x, y, 1 - c)).start()
        token[...] = jnp.zeros_like(token)

    lands = [jax.ShapeDtypeStruct((N_CHIPS, g.shape[1] // 2, g.shape[2]), g.dtype) for g in grads]
    res = pl.pallas_call(
        body, name=name, in_specs=[HBM] * (2 * n),
        out_specs=[SEM] * (2 * ns) + [HBM] * (2 * n) + [pl.BlockSpec(memory_space=pltpu.VMEM)],
        out_shape=[pltpu.SemaphoreType.DMA(())] * (2 * ns) + [pltpu.HBM(g.shape, g.dtype) for g in grads]
        + [pltpu.HBM(l.shape, l.dtype) for l in lands] + [jax.ShapeDtypeStruct((8, 128), F32)],
        input_output_aliases={a: 2 * ns + a for a in range(2 * n)},
        compiler_params=pltpu.CompilerParams(has_side_effects=EFFECT),
    )(*[_hbm(g) for g in grads], *[_hbm(lax.empty(l.shape, l.dtype)) for l in lands])
    return list(res[:ns]), list(res[ns:2 * ns]), list(res[2 * ns:2 * ns + n]), list(res[2 * ns + n:2 * ns + 2 * n]), res[2 * ns + 2 * n]


def _pair_wait(name, send, recv, grads, lands, after):
    n = len(grads)
    ns = N_CHIPS * n

    def body(*refs):
        src, land = refs[:n], refs[n:2 * n]
        send_ref, recv_ref = refs[2 * n:2 * n + ns], refs[2 * n + ns:2 * n + 2 * ns]
        x, y, c, _ = _place()
        for a in range(n):
            for j in range(N_CHIPS):
                cp = _remote(land[a].at[j], land[a].at[j], send_ref[N_CHIPS * a + j], recv_ref[N_CHIPS * a + j], (x, y, c))
                cp.wait_send()
                cp.wait_recv()

    res = pl.pallas_call(
        body, name=name, in_specs=[HBM] * (2 * n) + [SEM] * (2 * ns) + [pl.BlockSpec(memory_space=pl.ANY)],
        out_specs=[HBM] * (2 * n), out_shape=[pltpu.HBM(g.shape, g.dtype) for g in grads] + [pltpu.HBM(l.shape, l.dtype) for l in lands],
        input_output_aliases={a: a for a in range(2 * n)},
        compiler_params=pltpu.CompilerParams(has_side_effects=EFFECT),
    )(*grads, *lands, *send, *recv, after)
    return list(res[:n]), list(res[n:])


def _chip_start(name, parts):
    n = len(parts)
    ns = 3 * n

    def body(*refs):
        send, recv = refs[2 * n:2 * n + ns], refs[2 * n + ns:2 * n + 2 * ns]
        src = refs[2 * n + 2 * ns:3 * n + 2 * ns]
        land = refs[3 * n + 2 * ns:4 * n + 2 * ns]
        token = refs[4 * n + 2 * ns]
        x, y, c, chips = _place()
        for a in range(n):
            for j, (cx, cy) in enumerate(chips):
                _remote(src[a].at[j], land[a].at[j], send[3 * a + j], recv[3 * a + j], (cx, cy, c)).start()
        token[...] = jnp.zeros_like(token)

    res = pl.pallas_call(
        body, name=name, in_specs=[HBM] * (2 * n),
        out_specs=[SEM] * (2 * ns) + [HBM] * (2 * n) + [pl.BlockSpec(memory_space=pltpu.VMEM)],
        out_shape=[pltpu.SemaphoreType.DMA(())] * (2 * ns) + [pltpu.HBM(p.shape, p.dtype) for p in parts] * 2
        + [jax.ShapeDtypeStruct((8, 128), F32)],
        input_output_aliases={a: 2 * ns + a for a in range(2 * n)},
        compiler_params=pltpu.CompilerParams(has_side_effects=EFFECT),
    )(*[_hbm(p) for p in parts], *[_hbm(lax.empty(p.shape, p.dtype)) for p in parts])
    return list(res[:ns]), list(res[ns:2 * ns]), list(res[2 * ns:2 * ns + n]), list(res[2 * ns + n:2 * ns + 2 * n]), res[2 * ns + 2 * n]


def _chip_wait(name, send, recv, parts, lands, after):
    n = len(parts)
    ns = 3 * n

    def body(*refs):
        src, land = refs[:n], refs[n:2 * n]
        send_ref, recv_ref = refs[2 * n:2 * n + ns], refs[2 * n + ns:2 * n + 2 * ns]
        x, y, c, _ = _place()
        for a in range(n):
            for j in range(3):
                cp = _remote(src[a].at[j], land[a].at[j], send_ref[3 * a + j], recv_ref[3 * a + j], (x, y, c))
                cp.wait_send()
                cp.wait_recv()

    res = pl.pallas_call(
        body, name=name, in_specs=[HBM] * (2 * n) + [SEM] * (2 * ns) + [pl.BlockSpec(memory_space=pl.ANY)],
        out_specs=[HBM] * (2 * n), out_shape=[pltpu.HBM(p.shape, p.dtype) for p in parts] * 2,
        input_output_aliases={a: a for a in range(2 * n)},
        compiler_params=pltpu.CompilerParams(has_side_effects=EFFECT),
    )(*parts, *lands, *send, *recv, after)
    return list(res[n:])


def _pair_share(name, shards):
    n = len(shards)

    def body(*refs):
        buf = refs[n:2 * n]
        send, recv = refs[2 * n:]
        x, y, c, _ = _place()
        copies = []
        for a in range(n):
            half = buf[a].shape[0] // 2
            mine = buf[a].at[pl.ds(c * half, half)]
            cp = _remote(mine, mine, send.at[a], recv.at[a], (x, y, 1 - c))
            cp.start()
            copies.append(cp)
        for a, cp in enumerate(copies):
            half = buf[a].shape[0] // 2
            cp.wait_send()
            theirs = buf[a].at[pl.ds((1 - c) * half, half)]
            _remote(theirs, theirs, send.at[a], recv.at[a], (x, y, c)).wait_recv()

    return pl.pallas_call(
        body, name=name, in_specs=[HBM] * n, out_specs=[HBM] * n,
        out_shape=[jax.ShapeDtypeStruct(s.shape, s.dtype) for s in shards], input_output_aliases={a: a for a in range(n)},
        scratch_shapes=[pltpu.SemaphoreType.DMA((n,)), pltpu.SemaphoreType.DMA((n,))],
    )(*shards)


def _all_reduce_small(buf):
    R, W = buf.shape

    def body(src_ref, out_ref, slots, send, recv):
        x, y, c, _ = _place()
        me = 4 * x + 2 * y + c
        copies = []
        for dx in range(2):
            for dy in range(2):
                for dc in range(2):
                    if dx == dy == dc == 0:
                        continue
                    k = 4 * dx + 2 * dy + dc
                    peer = (x ^ dx, y ^ dy, c ^ dc)
                    cp = _remote(src_ref, slots.at[me], send.at[k], recv.at[k], peer)
                    cp.start()
                    copies.append((cp, k))
        slots[me] = src_ref[...]
        for cp, k in copies:
            cp.wait_send()
            landed = slots.at[me ^ k]
            _remote(landed, landed, send.at[k], recv.at[k], (x, y, c)).wait_recv()
        total = slots[0]
        for d in range(1, N_DEV):
            total = total + slots[d]
        out_ref[...] = total

    return pl.pallas_call(
        body, name="all_reduce_small", in_specs=[pl.BlockSpec(memory_space=pltpu.VMEM)],
        out_specs=pl.BlockSpec(memory_space=pltpu.VMEM), out_shape=jax.ShapeDtypeStruct((R, W), F32),
        scratch_shapes=[pltpu.VMEM((N_DEV, R, W), F32), pltpu.SemaphoreType.DMA((N_DEV,)), pltpu.SemaphoreType.DMA((N_DEV,))],
    )(buf)


def _pair_sum_bf16(name, grad, theirs, ids):
    _, R2, C = theirs.shape
    tr = _tile(R2, 256, 16)
    nrb = R2 // tr

    def body(ids_ref, a_ref, b_ref, o_ref):
        o_ref[...] = (a_ref[...] + b_ref[...]).astype(BF16)

    return pl.pallas_call(
        body, name=name,
        grid_spec=pltpu.PrefetchScalarGridSpec(
            num_scalar_prefetch=1, grid=(3, nrb),
            in_specs=[pl.BlockSpec((None, tr, C), lambda j, i, ids: (ids[3 + j], ids[1] * nrb + i, 0)),
                      pl.BlockSpec((None, tr, C), lambda j, i, ids: (j + 1, i, 0))],
            out_specs=pl.BlockSpec((None, tr, C), lambda j, i, ids: (j, i, 0))),
        out_shape=jax.ShapeDtypeStruct((3, R2, C), BF16), compiler_params=_params(("parallel", "parallel")),
    )(ids, grad, theirs)


def _chip_sum(name, grad, theirs, arrived, ids):
    _, R2, C = theirs.shape
    tr = _tile(R2, 256, 16)
    nrb = R2 // tr

    def body(ids_ref, a_ref, b_ref, r_ref, o_ref):
        tot = a_ref[...] + b_ref[...]
        for j in range(3):
            tot = tot + r_ref[j].astype(F32)
        o_ref[...] = tot

    return pl.pallas_call(
        body, name=name,
        grid_spec=pltpu.PrefetchScalarGridSpec(
            num_scalar_prefetch=1, grid=(nrb,),
            in_specs=[pl.BlockSpec((None, tr, C), lambda i, ids: (ids[0], ids[1] * nrb + i, 0)),
                      pl.BlockSpec((None, tr, C), lambda i, ids: (0, i, 0)),
                      pl.BlockSpec((3, tr, C), lambda i, ids: (0, i, 0))],
            out_specs=pl.BlockSpec((tr, C), lambda i, ids: (ids[1] * nrb + i, 0))),
        out_shape=jax.ShapeDtypeStruct((2 * R2, C), F32), compiler_params=_params(("parallel",)),
    )(ids, grad, theirs, arrived)


def _adamw(name, w, g, m, v):
    R, C = w.shape
    tr = _tile(R, 128, 8)
    c1 = 1.0 / (1.0 - ADAM_B1 ** ADAM_STEP)
    c2 = 1.0 / (1.0 - ADAM_B2 ** ADAM_STEP)

    def body(w_ref, g_ref, m_ref, v_ref, d_ref, mo_ref, vo_ref):
        gv = g_ref[...]
        mn = ADAM_B1 * m_ref[...] + (1.0 - ADAM_B1) * gv
        vn = ADAM_B2 * v_ref[...] + (1.0 - ADAM_B2) * (gv * gv)
        d_ref[...] = -ADAM_LR * ((mn * c1) / (jnp.sqrt(vn * c2) + ADAM_EPS) + ADAM_WD * w_ref[...])
        mo_ref[...] = mn
        vo_ref[...] = vn

    spec = pl.BlockSpec((tr, C), lambda i: (i, 0))
    sds = jax.ShapeDtypeStruct((R, C), F32)
    return pl.pallas_call(body, name=name, grid=(R // tr,), in_specs=[spec] * 4, out_specs=[spec] * 3, out_shape=[sds] * 3,
                          compiler_params=_params(("parallel",)))(w, g, m, v)


SMALL = ["ffn1_norm", "mix_norm", "mem_norm", "forget_bias", "fox_q_gain", "fox_k_gain", "swa_q_gain", "swa_k_gain", "swa_sinks",
         "mem_q_gain", "mem_k_gain", "ffn2_norm"]
LARGE = ["ffn1_gate", "ffn1_up", "ffn1_down", "w_in", "w_mem_k", "w_mem_v", "w_out", "ffn2_gate", "ffn2_up", "ffn2_down"]
GATHER_GROUPS = [["ffn1_gate", "ffn1_up"], ["ffn1_down", "w_in", "w_mem_k", "w_mem_v"], ["w_out", "ffn2_gate", "ffn2_up", "ffn2_down"]]
WEIGHTS = ["ffn1_norm", "ffn1_gate", "ffn1_up", "ffn1_down", "mix_norm", "mem_norm", "w_in", "forget_bias", "w_mem_k", "w_mem_v",
           "fox_q_gain", "fox_k_gain", "swa_q_gain", "swa_k_gain", "swa_sinks", "mem_q_gain", "mem_k_gain", "w_out", "ffn2_norm",
           "ffn2_gate", "ffn2_up", "ffn2_down"]


def _pad_proj_cols(w):
    out = jnp.zeros((w.shape[0], PROJ_W), w.dtype)
    for start, width, pstart in REF_GROUPS:
        out = lax.dynamic_update_slice(out, w[:, start:start + width], (0, pstart))
    return out


def _unpad_proj_cols(w):
    return jnp.concatenate([w[:, pstart:pstart + width] for _, width, pstart in REF_GROUPS], axis=1)


def _pack_small(vals):
    flat = jnp.concatenate([vals[k].reshape(-1).astype(F32) for k in SMALL + ["loss"]])
    n = flat.shape[0]
    total = -(-n // 1024) * 1024
    return jnp.pad(flat, (0, total - n)).reshape(total // 128, 128)


def _unpack_small(buf, shapes):
    flat = buf.reshape(-1)
    out, off = {}, 0
    for k in SMALL + ["loss"]:
        size = int(np.prod(shapes[k]))
        out[k] = flat[off:off + size].reshape(shapes[k])
        off += size
    return out


def kernel(x, mem, ffn1_norm, ffn1_gate, ffn1_up, ffn1_down, mix_norm, mem_norm, w_in, forget_bias, w_mem_k, w_mem_v, fox_q_gain, fox_k_gain, swa_q_gain, swa_k_gain, swa_sinks, mem_q_gain, mem_k_gain, w_out, ffn2_norm, ffn2_gate, ffn2_up, ffn2_down, loss_target, m_ffn1_norm, m_ffn1_gate, m_ffn1_up, m_ffn1_down, m_mix_norm, m_mem_norm, m_w_in, m_forget_bias, m_w_mem_k, m_w_mem_v, m_fox_q_gain, m_fox_k_gain, m_swa_q_gain, m_swa_k_gain, m_swa_sinks, m_mem_q_gain, m_mem_k_gain, m_w_out, m_ffn2_norm, m_ffn2_gate, m_ffn2_up, m_ffn2_down, v_ffn1_norm, v_ffn1_gate, v_ffn1_up, v_ffn1_down, v_mix_norm, v_mem_norm, v_w_in, v_forget_bias, v_w_mem_k, v_w_mem_v, v_fox_q_gain, v_fox_k_gain, v_swa_q_gain, v_swa_k_gain, v_swa_sinks, v_mem_q_gain, v_mem_k_gain, v_w_out, v_ffn2_norm, v_ffn2_gate, v_ffn2_up, v_ffn2_down):
    given = dict(locals())
    T, D = x.shape[1], x.shape[2]
    ML = mem.shape[1]
    xin = x.reshape(T, D)
    target = loss_target.reshape(T, D)
    memin = mem.reshape(ML, D)

    ids = _place_ids()
    shard = {k: given[k][0] for k in LARGE}
    shard["w_in"] = _pad_proj_cols(shard["w_in"])
    started, after = [], ids
    for gi, group in enumerate(GATHER_GROUPS):
        placed = [_cast_place("cast_" + k, shard[k], ids) for k in group]
        send, recv, bufs, after = _gather_start("gather_start_%d" % gi, placed, after)
        started.append((send, recv, bufs))

    def arrive(gi, done):
        send, recv, bufs = started[gi]
        bufs = _gather_wait("gather_wait_%d" % gi, send, recv, bufs, done)
        return dict(zip(GATHER_GROUPS[gi], _gather_forward("gather_forward_%d" % gi, bufs)))

    gains = jnp.concatenate([fox_q_gain, fox_k_gain, swa_q_gain, swa_k_gain, mem_q_gain,
                             jnp.pad(forget_bias, ((0, 0), (0, HEAD_DIM - FOX_HEADS))), jnp.zeros((2, HEAD_DIM), F32)], axis=0)
    slopes_np = 2.0 ** (-8.0 * np.arange(1, SWA_HEADS + 1) / SWA_HEADS)
    slopes = jnp.asarray(np.repeat(slopes_np, WINDOW).reshape(SWA_KV_HEADS, GR, 1), F32)
    sinks = jnp.repeat(swa_sinks.reshape(SWA_HEADS), WINDOW).reshape(SWA_KV_HEADS, GR, 1)

    h1 = _rms_fwd("ffn1_norm_fwd", xin, ffn1_norm + after[0, 0])
    full = arrive(0, h1)
    wg1, wu1 = full["ffn1_gate"], full["ffn1_up"]
    g1, u1, a1 = _ffn_gu("ffn1_gate_up", h1, wg1, wu1)
    full = arrive(1, a1)
    wd1 = full["ffn1_down"].reshape(-1, D)
    win = full["w_in"].reshape(D, PROJ_W)
    wmk = full["w_mem_k"].reshape(D, MEM_HEADS * HEAD_DIM)
    wmv = full["w_mem_v"].reshape(D, MEM_HEADS * HEAD_DIM)
    x1 = _ffn_down("ffn1_down", a1, wd1, xin)
    h2 = _rms_fwd("mix_norm_fwd", x1, mix_norm)
    proj = _mm2d("proj_in", h2, win, "nn", F32, tn=1408, tk=2048, n_outer=True)
    qkv, logf = _prep_fwd(proj, gains)
    cum = _cumsum_rows("forget_cumsum", [logf], False)
    cum_h = cum[:, :FOX_HEADS].T
    cq, ck = cum_h.reshape(FOX_HEADS, T, 1), cum_h.reshape(FOX_HEADS, 1, T)
    mn = _rms_fwd("mem_norm_fwd", memin, mem_norm)
    mk_raw = _mm2d("mem_k_proj", mn, wmk, "nn", F32)
    mv = _mm2d("mem_v_proj", mn, wmv, "nn", BF16)
    mk = _head_norm_rows(mk_raw, mem_k_gain)
    out_a, out_a_f32, lse_a = _fox_fwd(qkv, cq, ck)
    out_b, lse_b = _swa_fwd(qkv, slopes, sinks)
    out_c, lse_c = _mem_fwd(qkv, mk, mv)
    mixed = jnp.concatenate([out_a, out_b, out_c], axis=1)
    full = arrive(2, mixed)
    wo = full["w_out"].reshape(-1, D)
    wg2, wu2, wd2 = full["ffn2_gate"], full["ffn2_up"], full["ffn2_down"].reshape(-1, D)
    x2 = _mm2d("mix_out", mixed, wo, "nn", F32, tk=2048, n_outer=True, extras=[x1], epilogue=lambda accs, ex: [ex[0] + accs[0]])
    h3 = _rms_fwd("ffn2_norm_fwd", x2, ffn2_norm)
    g2, u2, a2 = _ffn_gu("ffn2_gate_up", h3, wg2, wu2)
    x3 = _ffn_down("ffn2_down", a2, wd2, x2)
    dx3, dyb3, loss_part = _loss_head(x3, target)

    grads, small, res = {}, {"loss": loss_part[0, 0]}, {}

    def pair_off(tag, group):
        send, recv, own, lands, token = _pair_start("grad_pair_start_" + tag, [grads[k] for k in group])
        return (group, send, recv, own, lands), token

    def chip_off(tag, started, done):
        group, send, recv, own, lands = started
        own, theirs = _pair_wait("grad_pair_wait_" + tag, send, recv, own, lands, done)
        grads.update(zip(group, own))
        to_chips = [_pair_sum_bf16("pair_sum_" + k, grads[k], b, ids) for k, b in zip(group, theirs)]
        send, recv, parts, lands, token = _chip_start("grad_chip_start_" + tag, to_chips)
        return (group, theirs, send, recv, parts, lands), token

    def finish(tag, state, done):
        group, theirs, send, recv, parts, lands = state
        arrived = _chip_wait("grad_chip_wait_" + tag, send, recv, parts, lands, done)
        halves = [_chip_sum("chip_sum_" + k, grads[k], b, r, ids) for k, b, r in zip(group, theirs, arrived)]
        reduced = dict(zip(group, _pair_share("grad_pair_share_" + tag, halves)))
        last = None
        for k in group:
            gk = _unpad_proj_cols(reduced[k]) if k == "w_in" else reduced[k]
            d, mo, vo = _adamw("adamw_" + k, given[k][0], gk, given["m_" + k][0], given["v_" + k][0])
            res[k] = tuple(t[None] for t in (gk, d, mo, vo))
            last = vo
        return last

    dg2, du2 = _ffn_bwd_act("ffn2", dyb3, wd2, g2, u2, N_CHIPS)
    grads["ffn2_down"] = _ffn_bwd_down("ffn2", a2, dyb3, N_CHIPS).reshape(N_CHIPS, -1, D)
    grads["ffn2_gate"], grads["ffn2_up"] = _ffn_bwd_gate_up("ffn2", h3, dg2, du2, N_CHIPS)
    started, token = pair_off("a", ["ffn2_gate", "ffn2_up", "ffn2_down"])
    dh3 = _ffn_bwd_x("ffn2", dg2, du2, wg2, wu2, token)
    state_a, token = chip_off("a", started, dh3)
    dx2, dx2b, small["ffn2_norm"] = _rms_bwd("ffn2_norm_bwd", dh3, x2, ffn2_norm + token[0, 0], dx3, 1.0)
    dmix = _mm2d("mix_out_dx", dx2b, wo, "nt", BF16, tk=2048, n_outer=True)
    grads["w_out"] = _mm2d("mix_out_dw", mixed, dx2b, "tn", F32, tk=1024).reshape(N_CHIPS, -1, D)
    dfq, dfk, dfv, dck, dcq = _fox_bwd(qkv, cq, ck, out_a_f32, lse_a, dmix)
    dsq, dsk, dsv, dsink = _swa_bwd(qkv, slopes, sinks, out_b, lse_b, dmix)
    dmq, dmk, dmv = _mem_bwd(qkv, mk, mv, out_c, lse_c, dmix)
    small["swa_sinks"] = dsink[:, :SWA_GROUP, 0].reshape(1, SWA_HEADS)
    dcum = [jnp.pad(d.reshape(FOX_HEADS, T).T, ((0, 0), (0, HEAD_DIM - FOX_HEADS))) for d in (dck, dcq)]
    dlogf = _cumsum_rows("forget_cumsum_bwd", dcum, True)
    dproj, dgains = _prep_bwd(proj, gains, dfq, dfk, dfv, dsq, dsk, dsv, dmq, dlogf)
    for row, k in enumerate(["fox_q_gain", "fox_k_gain", "swa_q_gain", "swa_k_gain", "mem_q_gain"]):
        small[k] = dgains[row:row + 1, :]
    small["forget_bias"] = dgains[5:6, :FOX_HEADS]
    grads["w_in"] = _mm2d("proj_in_dw", h2, dproj, "tn", F32, tn=1408, tk=1024).reshape(N_CHIPS, -1, PROJ_W)
    dmk_raw, small["mem_k_gain"] = _head_norm_rows_bwd(mk_raw, mem_k_gain, dmk)
    dmvb = dmv.astype(BF16)
    grads["w_mem_k"] = _mm2d("mem_k_dw", mn, dmk_raw, "tn", F32).reshape(N_CHIPS, -1, MEM_HEADS * HEAD_DIM)
    grads["w_mem_v"] = _mm2d("mem_v_dw", mn, dmvb, "tn", F32).reshape(N_CHIPS, -1, MEM_HEADS * HEAD_DIM)
    dmn = _mm2d("mem_k_dx", dmk_raw, wmk, "nt", F32)
    dmn = _mm2d("mem_v_dx", dmvb, wmv, "nt", F32, extras=[dmn], epilogue=lambda accs, ex: [ex[0] + accs[0]])
    _, _, small["mem_norm"] = _rms_bwd("mem_norm_bwd", dmn, memin, mem_norm, jnp.zeros_like(memin), 1.0)
    started, token = pair_off("b", ["w_out", "w_in", "w_mem_k", "w_mem_v"])
    dh2 = _mm2d("proj_in_dx", dproj, win, "nt", F32, tm=1024, tk=1408, after=token)
    state_b, token = chip_off("b", started, dh2)
    dx1, dyb1, small["mix_norm"] = _rms_bwd("mix_norm_bwd", dh2, x1, mix_norm + token[0, 0], dx2, 0.5)
    grads["ffn1_down"] = _ffn_bwd_down("ffn1", a1, dyb1, N_CHIPS).reshape(N_CHIPS, -1, D)
    started, token = pair_off("c", ["ffn1_down"])
    dg1, du1 = _ffn_bwd_act("ffn1", dyb1, wd1, g1, u1, N_CHIPS, after=token)
    state_c, token = chip_off("c", started, dg1)
    grads["ffn1_gate"], grads["ffn1_up"] = _ffn_bwd_gate_up("ffn1", h1, dg1, du1, N_CHIPS, after=token)
    started, token = pair_off("d", ["ffn1_gate", "ffn1_up"])
    dh1 = _ffn_bwd_x("ffn1", dg1, du1, wg1, wu1, token)
    state_d, token = chip_off("d", started, dh1)
    grad_x, _, small["ffn1_norm"] = _rms_bwd("ffn1_norm_bwd", dh1, xin, ffn1_norm + token[0, 0], dx1, 1.0)

    shapes = {k: given[k].shape for k in SMALL}
    shapes["loss"] = ()
    red_small = _unpack_small(_all_reduce_small(_pack_small(small)), shapes)
    loss = red_small["loss"]
    zero = {"loss": jnp.zeros((), F32)}
    packed = [_pack_small({**zero, **{k: src[k] for k in SMALL}}) for src in (
        {k: given[k] for k in SMALL}, red_small, {k: given["m_" + k] for k in SMALL}, {k: given["v_" + k] for k in SMALL})]
    small_out = _adamw("adamw_small", *packed)
    d_s, m_s, v_s = (_unpack_small(t, shapes) for t in small_out)
    for k in SMALL:
        res[k] = (red_small[k], d_s[k], m_s[k], v_s[k])

    done = finish("a", state_a, small_out[0])
    done = finish("b", state_b, done)
    done = finish("c", state_c, done)
    finish("d", state_d, done)

    outs = [loss, grad_x.reshape(1, T, D)]
    for part in range(4):
        outs += [res[k][part] for k in WEIGHTS]
    return tuple(outs)
```

```python
import functools

import numpy as np
import jax
import jax.numpy as jnp
from jax import lax
from jax.experimental import pallas as pl
from jax.experimental.pallas import tpu as pltpu

F32 = jnp.float32
BF16 = jnp.bfloat16
MESH = pl.DeviceIdType.MESH

HEAD_DIM = 128
FOX_HEADS = 6
SWA_HEADS = 6
SWA_KV_HEADS = 2
SWA_GROUP = SWA_HEADS // SWA_KV_HEADS
MEM_HEADS = 4
WINDOW = 128
EPS = 1e-6
NEG_INF = -1e30
SCALE = HEAD_DIM ** -0.5

C_FQ = 0
C_FK = C_FQ + FOX_HEADS * HEAD_DIM
C_FV = C_FK + FOX_HEADS * HEAD_DIM
C_SQ = C_FV + FOX_HEADS * HEAD_DIM
C_SK = C_SQ + SWA_HEADS * HEAD_DIM
C_SV = C_SK + SWA_KV_HEADS * HEAD_DIM
C_MQ = C_SV + SWA_KV_HEADS * HEAD_DIM
C_FL = C_MQ + MEM_HEADS * HEAD_DIM
PROJ_W = C_FL + HEAD_DIM
FOX_W = FOX_HEADS * HEAD_DIM
REF_GROUPS = [
    (0, FOX_W, C_FQ), (FOX_W, FOX_W, C_FK), (2 * FOX_W, FOX_W, C_FV), (3 * FOX_W, FOX_HEADS, C_FL),
    (3 * FOX_W + FOX_HEADS, SWA_HEADS * HEAD_DIM, C_SQ),
    (3 * FOX_W + FOX_HEADS + SWA_HEADS * HEAD_DIM, SWA_KV_HEADS * HEAD_DIM, C_SK),
    (3 * FOX_W + FOX_HEADS + (SWA_HEADS + SWA_KV_HEADS) * HEAD_DIM, SWA_KV_HEADS * HEAD_DIM, C_SV),
    (3 * FOX_W + FOX_HEADS + (SWA_HEADS + 2 * SWA_KV_HEADS) * HEAD_DIM, MEM_HEADS * HEAD_DIM, C_MQ),
]

ADAM_LR = 0.001
ADAM_B1 = 0.9
ADAM_B2 = 0.999
ADAM_EPS = 1e-08
ADAM_WD = 0.01
ADAM_STEP = 10

V7X_VMEM_LIMIT = 56 * 1024 * 1024
N_CHIPS = 4
N_DEV = 8


def _tile(n, pref, mult=128):
    t = (min(pref, n) // mult) * mult
    while t >= mult:
        if n % t == 0:
            return t
        t -= mult
    return n


def _params(sem):
    return pltpu.CompilerParams(dimension_semantics=sem, vmem_limit_bytes=V7X_VMEM_LIMIT)


_DIMS = {"nn": (((1,), (0,)), ((), ())), "nt": (((1,), (1,)), ((), ())), "tn": (((0,), (0,)), ((), ()))}


def _dot(a, b, mode):
    return lax.dot_general(a, b, _DIMS[mode], preferred_element_type=F32)


def _mm(name, grid, pairs, acc_of, acc_shapes, extras, outs, epilogue, after=None):
    n_p, n_e, n_o, n_a = len(pairs), len(extras), len(outs), len(acc_shapes)
    n_w = 0 if after is None else 1
    nk = grid[2]
    n_in = sum(1 if a is None else 2 for a, *_ in pairs)

    def body(*refs):
        ex = refs[n_in:n_in + n_e]
        out = refs[n_in + n_e + n_w:n_in + n_e + n_w + n_o]
        accs = refs[n_in + n_e + n_w + n_o:]
        parts = [None] * n_a
        at = 0
        for p in range(n_p):
            if pairs[p][0] is None:
                a_ref, b_ref = refs[0], refs[at]
                at += 1
            else:
                a_ref, b_ref = refs[at], refs[at + 1]
                at += 2
            d = _dot(a_ref[...], b_ref[...], pairs[p][4])
            parts[acc_of[p]] = d if parts[acc_of[p]] is None else parts[acc_of[p]] + d

        def finish(vals):
            for o, r in zip(out, epilogue(vals, [e[...] for e in ex])):
                o[...] = r.astype(o.dtype)

        if nk == 1:
            finish(parts)
            return
        k = pl.program_id(2)

        @pl.when(k == 0)
        def _():
            for a, d in zip(accs, parts):
                a[...] = d

        @pl.when((k > 0) & (k < nk - 1))
        def _():
            for a, d in zip(accs, parts):
                a[...] += d

        @pl.when(k == nk - 1)
        def _():
            finish([a[...] + d for a, d in zip(accs, parts)])

    in_specs, args = [], []
    for a, a_spec, b, b_spec, _ in pairs:
        if a is not None:
            in_specs.append(a_spec)
            args.append(a)
        in_specs.append(b_spec)
        args.append(b)
    for e, e_spec in extras:
        in_specs.append(e_spec)
        args.append(e)
    if after is not None:
        in_specs.append(pl.BlockSpec(memory_space=pl.ANY))
        args.append(after)
    res = pl.pallas_call(
        body, name=name, grid=grid, in_specs=in_specs,
        out_specs=[s for _, s in outs], out_shape=[o for o, _ in outs],
        scratch_shapes=[pltpu.VMEM(s, F32) for s in acc_shapes] if nk > 1 else [],
        compiler_params=_params(("parallel", "parallel", "arbitrary")),
    )(*args)
    return res


def _mm2d(name, a, b, mode, out_dtype, tm=512, tn=1024, tk=1024, extras=(), epilogue=None, n_out=1, after=None, n_outer=False):
    if mode == "nn":
        (M, K), N = a.shape, b.shape[1]
    elif mode == "nt":
        (M, K), N = a.shape, b.shape[0]
    else:
        (K, M), N = a.shape, b.shape[1]
    tm, tn, tk = _tile(M, tm), _tile(N, tn), _tile(K, tk)

    def spec(shape, index):
        if n_outer:
            return pl.BlockSpec(shape, lambda j, i, k: index(i, j, k))
        return pl.BlockSpec(shape, index)

    a_spec = spec((tk, tm), lambda i, j, k: (k, i)) if mode == "tn" else spec((tm, tk), lambda i, j, k: (i, k))
    b_spec = spec((tn, tk), lambda i, j, k: (j, k)) if mode == "nt" else spec((tk, tn), lambda i, j, k: (k, j))
    mn = spec((tm, tn), lambda i, j, k: (i, j))
    if epilogue is None:
        epilogue = lambda accs, ex: [accs[0]]
    if not isinstance(out_dtype, (list, tuple)):
        out_dtype = [out_dtype] * n_out
    grid = (N // tn, M // tm, K // tk) if n_outer else (M // tm, N // tn, K // tk)
    res = _mm(name, grid, [(a, a_spec, b, b_spec, mode)], [0], [(tm, tn)],
              [(e, mn) for e in extras], [(jax.ShapeDtypeStruct((M, N), d), mn) for d in out_dtype], epilogue, after=after)
    return res[0] if len(res) == 1 else res


def _sigmoid(x):
    return 1.0 / (1.0 + jnp.exp(-x))


def _sigmoid_fast(x):
    return pl.reciprocal(1.0 + jnp.exp(-x), approx=True)


def _ffn_gu(name, h, wg, wu):
    T, D = h.shape
    nf, _, F4 = wg.shape
    tm, tk = _tile(T, 512), _tile(D, 2048)
    a_spec = pl.BlockSpec((tm, tk), lambda j, i, k: (i, k))
    b_spec = pl.BlockSpec((None, tk, F4), lambda j, i, k: (j, k, 0))
    o_spec = pl.BlockSpec((tm, F4), lambda j, i, k: (i, j))

    def epilogue(accs, ex):
        g, u = accs
        return [g, u, g * _sigmoid_fast(g) * u]

    sds = jax.ShapeDtypeStruct((T, nf * F4), BF16)
    return _mm(name, (nf, T // tm, D // tk), [(h, a_spec, wg, b_spec, "nn"), (None, None, wu, b_spec, "nn")], [0, 1],
               [(tm, F4), (tm, F4)], [], [(sds, o_spec)] * 3, epilogue)


def _ffn_down(name, a, wd, xres):
    return _mm2d(name, a, wd, "nn", F32, tm=1024, tk=wd.shape[0] // N_CHIPS, extras=[xres],
                 epilogue=lambda accs, ex: [ex[0] + 0.5 * accs[0]])


def _ffn_bwd_down(tag, a, dyb, nf, after=None):
    return _mm2d(tag + "_dwd", a, dyb, "tn", F32, tm=a.shape[1] // nf, tk=1024, after=after)


def _ffn_bwd_act(tag, dyb, wd, g, u, nf, after=None):
    def act_bwd(accs, ex):
        gf, uf = ex[0].astype(F32), ex[1].astype(F32)
        s = _sigmoid_fast(gf)
        return [accs[0] * uf * s * (1.0 + gf * (1.0 - s)), accs[0] * gf * s]

    return _mm2d(tag + "_da", dyb, wd, "nt", BF16, tn=wd.shape[0] // nf, tk=2048, extras=[g, u], epilogue=act_bwd, n_out=2,
                 n_outer=True, after=after)


def _ffn_bwd_gate_up(tag, h, dg, du, nf, after=None):
    T, D = h.shape
    F4 = dg.shape[1] // nf
    tm, tk = _tile(D, 512), _tile(T, 1024)
    h_spec = pl.BlockSpec((tk, tm), lambda i, j, k: (k, i))
    d_spec = pl.BlockSpec((tk, F4), lambda i, j, k: (k, j))
    w_spec = pl.BlockSpec((None, tm, F4), lambda i, j, k: (j, i, 0))
    sds = jax.ShapeDtypeStruct((nf, D, F4), F32)
    return _mm(tag + "_dwgu", (D // tm, nf, T // tk), [(h, h_spec, dg, d_spec, "tn"), (None, None, du, d_spec, "tn")],
               [0, 1], [(tm, F4), (tm, F4)], [], [(sds, w_spec)] * 2, lambda accs, ex: accs, after=after)


def _ffn_bwd_x(tag, dg, du, wg, wu, after):
    T = dg.shape[0]
    nf, D, F4 = wg.shape
    tm, tn = _tile(T, 1024), _tile(D, 1024)
    a_spec = pl.BlockSpec((tm, F4), lambda i, j, k: (i, k))
    b_spec = pl.BlockSpec((None, tn, F4), lambda i, j, k: (k, j, 0))
    o_spec = pl.BlockSpec((tm, tn), lambda i, j, k: (i, j))
    (dh,) = _mm(tag + "_dh", (T // tm, D // tn, nf), [(dg, a_spec, wg, b_spec, "nt"), (du, a_spec, wu, b_spec, "nt")],
                [0, 0], [(tm, tn)], [], [(jax.ShapeDtypeStruct((T, D), F32), o_spec)], lambda accs, ex: accs, after=after)
    return dh


def _rms_fwd(name, x, gain):
    R, D = x.shape
    tr = _tile(R, 256, 8)

    def body(x_ref, g_ref, o_ref):
        xv = x_ref[...]
        r = lax.rsqrt(jnp.mean(xv * xv, axis=-1, keepdims=True) + EPS)
        o_ref[...] = (xv * r * g_ref[...]).astype(BF16)

    return pl.pallas_call(
        body, name=name, grid=(R // tr,),
        in_specs=[pl.BlockSpec((tr, D), lambda i: (i, 0)), pl.BlockSpec((1, D), lambda i: (0, 0))],
        out_specs=pl.BlockSpec((tr, D), lambda i: (i, 0)), out_shape=jax.ShapeDtypeStruct((R, D), BF16),
        compiler_params=_params(("parallel",)),
    )(x, gain)


def _rms_bwd(name, dh, x, gain, dres, bscale):
    R, D = x.shape
    tr = _tile(R, 256, 8)

    def body(dh_ref, x_ref, g_ref, dres_ref, dx_ref, dxb_ref, dg_ref):
        xv, dy = x_ref[...], dh_ref[...]
        r = lax.rsqrt(jnp.mean(xv * xv, axis=-1, keepdims=True) + EPS)
        xn = xv * r
        uu = dy * g_ref[...]
        dx = dres_ref[...] + r * (uu - xn * jnp.mean(xn * uu, axis=-1, keepdims=True))
        dx_ref[...] = dx
        dxb_ref[...] = (bscale * dx).astype(BF16)
        part = jnp.sum(dy * xn, axis=0, keepdims=True)

        @pl.when(pl.program_id(0) == 0)
        def _():
            dg_ref[...] = part

        @pl.when(pl.program_id(0) > 0)
        def _():
            dg_ref[...] += part

    row = pl.BlockSpec((tr, D), lambda i: (i, 0))
    vec = pl.BlockSpec((1, D), lambda i: (0, 0))
    return pl.pallas_call(
        body, name=name, grid=(R // tr,), in_specs=[row, row, vec, row], out_specs=[row, row, vec],
        out_shape=[jax.ShapeDtypeStruct((R, D), F32), jax.ShapeDtypeStruct((R, D), BF16), jax.ShapeDtypeStruct((1, D), F32)],
        compiler_params=_params(("arbitrary",)),
    )(dh, x, gain, dres)


def _loss_head(y, target):
    R, D = y.shape
    tr = _tile(R, 256, 8)

    def body(y_ref, t_ref, d_ref, db_ref, l_ref):
        e = y_ref[...] - t_ref[...]
        d = e * (1.0 / D)
        d_ref[...] = d
        db_ref[...] = (0.5 * d).astype(BF16)
        part = jnp.zeros((8, 128), F32) + (0.5 / D) * jnp.sum(e * e)

        @pl.when(pl.program_id(0) == 0)
        def _():
            l_ref[...] = part

        @pl.when(pl.program_id(0) > 0)
        def _():
            l_ref[...] += part

    row = pl.BlockSpec((tr, D), lambda i: (i, 0))
    acc = pl.BlockSpec((8, 128), lambda i: (0, 0))
    return pl.pallas_call(
        body, name="loss_head", grid=(R // tr,), in_specs=[row, row], out_specs=[row, row, acc],
        out_shape=[jax.ShapeDtypeStruct((R, D), F32), jax.ShapeDtypeStruct((R, D), BF16), jax.ShapeDtypeStruct((8, 128), F32)],
        compiler_params=_params(("arbitrary",)),
    )(y, target)


def _head_norm(xs, g):
    r = lax.rsqrt(jnp.mean(xs * xs, axis=-1, keepdims=True) + EPS)
    return xs * r * g


def _head_norm_bwd(xs, g, dy):
    r = lax.rsqrt(jnp.mean(xs * xs, axis=-1, keepdims=True) + EPS)
    xn = xs * r
    uu = dy * g
    return r * (uu - xn * jnp.mean(xn * uu, axis=-1, keepdims=True)), jnp.sum(dy * xn, axis=0, keepdims=True)


NORMED = [(C_FQ, FOX_HEADS, 0), (C_FK, FOX_HEADS, 1), (C_SQ, SWA_HEADS, 2), (C_SK, SWA_KV_HEADS, 3), (C_MQ, MEM_HEADS, 4)]
PLAIN = [(C_FV, FOX_HEADS), (C_SV, SWA_KV_HEADS)]


def _prep_fwd(proj, gains):
    T = proj.shape[0]
    tr = _tile(T, 256, 8)

    def body(p_ref, g_ref, o_ref, lf_ref):
        for start, heads, row in NORMED:
            gn = g_ref[row:row + 1, :]
            for hh in range(heads):
                sl = slice(start + hh * HEAD_DIM, start + (hh + 1) * HEAD_DIM)
                o_ref[:, sl] = _head_norm(p_ref[:, sl], gn).astype(BF16)
        for start, heads in PLAIN:
            sl = slice(start, start + heads * HEAD_DIM)
            o_ref[:, sl] = p_ref[:, sl].astype(BF16)
        zb = p_ref[:, C_FL:C_FL + HEAD_DIM] + g_ref[5:6, :]
        o_ref[:, C_FL:C_FL + HEAD_DIM] = jnp.zeros((tr, HEAD_DIM), BF16)
        lf_ref[...] = jnp.minimum(zb, 0.0) - jnp.log(1.0 + jnp.exp(-jnp.abs(zb)))

    return pl.pallas_call(
        body, name="prep_fwd", grid=(T // tr,),
        in_specs=[pl.BlockSpec((tr, PROJ_W), lambda i: (i, 0)), pl.BlockSpec((8, 128), lambda i: (0, 0))],
        out_specs=[pl.BlockSpec((tr, PROJ_W), lambda i: (i, 0)), pl.BlockSpec((tr, HEAD_DIM), lambda i: (i, 0))],
        out_shape=[jax.ShapeDtypeStruct((T, PROJ_W), BF16), jax.ShapeDtypeStruct((T, HEAD_DIM), F32)],
        compiler_params=_params(("parallel",)),
    )(proj, gains)


def _prep_bwd(proj, gains, dfq, dfk, dfv, dsq, dsk, dsv, dmq, dlogf):
    T = proj.shape[0]
    tr = _tile(T, 256, 8)
    d_normed = {C_FQ: 0, C_FK: 1, C_SQ: 3, C_SK: 4, C_MQ: 6}
    d_plain = {C_FV: 2, C_SV: 5}

    def body(p_ref, g_ref, *rest):
        d_refs, dlf_ref, o_ref, dg_ref = rest[:7], rest[7], rest[8], rest[9]
        rows = []
        for start, heads, row in NORMED:
            gn = g_ref[row:row + 1, :]
            d_ref = d_refs[d_normed[start]]
            tot = jnp.zeros((1, HEAD_DIM), F32)
            for hh in range(heads):
                sl = slice(start + hh * HEAD_DIM, start + (hh + 1) * HEAD_DIM)
                dx, dgn = _head_norm_bwd(p_ref[:, sl], gn, d_ref[:, hh * HEAD_DIM:(hh + 1) * HEAD_DIM])
                o_ref[:, sl] = dx.astype(BF16)
                tot = tot + dgn
            rows.append(tot)
        for start, heads in PLAIN:
            o_ref[:, start:start + heads * HEAD_DIM] = d_refs[d_plain[start]][...].astype(BF16)
        zb = p_ref[:, C_FL:C_FL + HEAD_DIM] + g_ref[5:6, :]
        lane = lax.broadcasted_iota(jnp.int32, (tr, HEAD_DIM), 1)
        dz = jnp.where(lane < FOX_HEADS, dlf_ref[...] * (1.0 - _sigmoid(zb)), 0.0)
        o_ref[:, C_FL:C_FL + HEAD_DIM] = dz.astype(BF16)
        rows.append(jnp.sum(dz, axis=0, keepdims=True))
        part = jnp.concatenate(rows + [jnp.zeros((2, HEAD_DIM), F32)], axis=0)

        @pl.when(pl.program_id(0) == 0)
        def _():
            dg_ref[...] = part

        @pl.when(pl.program_id(0) > 0)
        def _():
            dg_ref[...] += part

    def rows_of(w):
        return pl.BlockSpec((tr, w), lambda i: (i, 0))

    small = pl.BlockSpec((8, 128), lambda i: (0, 0))
    ds = [dfq, dfk, dfv, dsq, dsk, dsv, dmq]
    return pl.pallas_call(
        body, name="prep_bwd", grid=(T // tr,),
        in_specs=[rows_of(PROJ_W), small] + [rows_of(d.shape[1]) for d in ds] + [rows_of(HEAD_DIM)],
        out_specs=[rows_of(PROJ_W), small],
        out_shape=[jax.ShapeDtypeStruct((T, PROJ_W), BF16), jax.ShapeDtypeStruct((8, 128), F32)],
        compiler_params=_params(("arbitrary",)),
    )(proj, gains, *ds, dlogf)


def _head_norm_rows(x, gain):
    R, W = x.shape

    def body(x_ref, g_ref, o_ref):
        for hh in range(W // HEAD_DIM):
            sl = slice(hh * HEAD_DIM, (hh + 1) * HEAD_DIM)
            o_ref[:, sl] = _head_norm(x_ref[:, sl], g_ref[...]).astype(BF16)

    return pl.pallas_call(body, name="mem_k_norm", out_shape=jax.ShapeDtypeStruct((R, W), BF16))(x, gain)


def _head_norm_rows_bwd(x, gain, dy):
    R, W = x.shape

    def body(x_ref, g_ref, dy_ref, dx_ref, dg_ref):
        tot = jnp.zeros((1, HEAD_DIM), F32)
        for hh in range(W // HEAD_DIM):
            sl = slice(hh * HEAD_DIM, (hh + 1) * HEAD_DIM)
            dx, dgn = _head_norm_bwd(x_ref[:, sl], g_ref[...], dy_ref[:, sl])
            dx_ref[:, sl] = dx.astype(BF16)
            tot = tot + dgn
        dg_ref[...] = tot

    return pl.pallas_call(
        body, name="mem_k_norm_bwd",
        out_shape=[jax.ShapeDtypeStruct((R, W), BF16), jax.ShapeDtypeStruct((1, HEAD_DIM), F32)])(x, gain, dy)


def _cumsum_rows(name, xs, reverse):
    T, W = xs[0].shape
    tb = _tile(T, 512, 8)
    nb = T // tb

    def body(*refs):
        o_ref, carry = refs[len(xs)], refs[len(xs) + 1]

        @pl.when(pl.program_id(0) == 0)
        def _():
            carry[...] = jnp.zeros_like(carry)

        xv = refs[0][...]
        for x_ref in refs[1:len(xs)]:
            xv = xv + x_ref[...]
        r = lax.broadcasted_iota(jnp.int32, (tb, tb), 0)
        cc = lax.broadcasted_iota(jnp.int32, (tb, tb), 1)
        tri = jnp.where((cc >= r) if reverse else (cc <= r), 1.0, 0.0).astype(F32)
        o_ref[...] = jnp.dot(tri, xv, precision=lax.Precision.HIGHEST, preferred_element_type=F32) + carry[...]
        carry[...] += jnp.sum(xv, axis=0, keepdims=True)

    idx = (lambda i: (nb - 1 - i, 0)) if reverse else (lambda i: (i, 0))
    return pl.pallas_call(
        body, name=name, grid=(nb,), in_specs=[pl.BlockSpec((tb, W), idx)] * len(xs), out_specs=pl.BlockSpec((tb, W), idx),
        out_shape=jax.ShapeDtypeStruct((T, W), F32), scratch_shapes=[pltpu.VMEM((1, W), F32)],
        compiler_params=_params(("arbitrary",)),
    )(*xs)


def _triangle(nq, by_column):
    if by_column:
        blocks = [(i, j) for j in range(nq) for i in range(j, nq)]
    else:
        blocks = [(i, j) for i in range(nq) for j in range(i + 1)]
    return jnp.asarray(np.array(blocks, np.int32).T)


def _fox_scores(q, k, cq, ck, on_diagonal):
    s = _dot(q, k, "nt") * SCALE + (cq - ck)
    if on_diagonal:
        n = s.shape[0]
        s = jnp.where(lax.broadcasted_iota(jnp.int32, (n, n), 1) <= lax.broadcasted_iota(jnp.int32, (n, n), 0), s, NEG_INF)
    return s


def _fox_fwd(qkv, cq, ck):
    T = qkv.shape[0]
    tq = _tile(T, 512)
    nq = T // tq
    steps = nq * (nq + 1) // 2
    HQ, HK, HV = C_FQ // HEAD_DIM, C_FK // HEAD_DIM, C_FV // HEAD_DIM

    def body(tab, q_ref, k_ref, v_ref, cq_ref, ck_ref, o_ref, of_ref, lse_ref, m_sc, l_sc, acc_sc):
        i, j = tab[0, pl.program_id(1)], tab[1, pl.program_id(1)]

        @pl.when(j == 0)
        def _():
            m_sc[...] = jnp.full_like(m_sc, NEG_INF)
            l_sc[...] = jnp.zeros_like(l_sc)
            acc_sc[...] = jnp.zeros_like(acc_sc)

        def step(on_diagonal):
            s = _fox_scores(q_ref[...], k_ref[...], cq_ref[...], ck_ref[...], on_diagonal)
            m_new = jnp.maximum(m_sc[...], jnp.max(s, axis=-1, keepdims=True))
            alpha = jnp.exp(m_sc[...] - m_new)
            p = jnp.exp(s - m_new)
            l_sc[...] = alpha * l_sc[...] + jnp.sum(p, axis=-1, keepdims=True)
            acc_sc[...] = alpha * acc_sc[...] + _dot(p.astype(BF16), v_ref[...], "nn")
            m_sc[...] = m_new

        @pl.when(j < i)
        def _():
            step(False)

        @pl.when(j == i)
        def _():
            step(True)
            o = acc_sc[...] / l_sc[...]
            o_ref[...] = o.astype(BF16)
            of_ref[...] = o
            lse_ref[...] = m_sc[...] + jnp.log(l_sc[...])

    def rows(base):
        return pl.BlockSpec((tq, HEAD_DIM), lambda h, s, tab: (tab[0, s], base + h))

    def cols(base):
        return pl.BlockSpec((tq, HEAD_DIM), lambda h, s, tab: (tab[1, s], base + h))

    rvec = pl.BlockSpec((None, tq, 1), lambda h, s, tab: (h, tab[0, s], 0))
    return pl.pallas_call(
        body, name="fox_fwd",
        grid_spec=pltpu.PrefetchScalarGridSpec(
            num_scalar_prefetch=1, grid=(FOX_HEADS, steps),
            in_specs=[rows(HQ), cols(HK), cols(HV), rvec, pl.BlockSpec((None, 1, tq), lambda h, s, tab: (h, 0, tab[1, s]))],
            out_specs=[rows(0), rows(0), rvec],
            scratch_shapes=[pltpu.VMEM((tq, 1), F32), pltpu.VMEM((tq, 1), F32), pltpu.VMEM((tq, HEAD_DIM), F32)]),
        out_shape=[jax.ShapeDtypeStruct((T, FOX_W), BF16), jax.ShapeDtypeStruct((T, FOX_W), F32),
                   jax.ShapeDtypeStruct((FOX_HEADS, T, 1), F32)],
        compiler_params=_params(("parallel", "arbitrary")),
    )(_triangle(nq, False), qkv, qkv, qkv, cq, ck)


def _fox_bwd(qkv, cq, ck, out_f32, lse, dmix):
    T = qkv.shape[0]
    tq = _tile(T, 512)
    nq = T // tq
    steps = nq * (nq + 1) // 2
    HQ, HK, HV = C_FQ // HEAD_DIM, C_FK // HEAD_DIM, C_FV // HEAD_DIM

    def body(tab, q_ref, k_ref, v_ref, cq_ref, ck_ref, o_ref, lse_ref, do_ref, dq_ref, dk_ref, dv_ref, dck_ref, dcq_ref,
             dk_sc, dv_sc, dc_sc):
        qi, kj = tab[0, pl.program_id(1)], tab[1, pl.program_id(1)]

        @pl.when(qi == kj)
        def _():
            dk_sc[...] = jnp.zeros_like(dk_sc)
            dv_sc[...] = jnp.zeros_like(dv_sc)
            dc_sc[...] = jnp.zeros_like(dc_sc)

        def step(on_diagonal):
            q, k, v, do = q_ref[...], k_ref[...], v_ref[...], do_ref[...]
            p = jnp.exp(_fox_scores(q, k, cq_ref[...], ck_ref[...], on_diagonal) - lse_ref[...])
            dp = _dot(do, v, "nt")
            delta = jnp.sum(do.astype(F32) * o_ref[...], axis=-1, keepdims=True)
            ds = p * (dp - delta)
            dsb = ds.astype(BF16)
            dv_sc[...] += _dot(p.astype(BF16), do, "tn")
            dk_sc[...] += _dot(dsb, q, "tn")
            dc_sc[...] += jnp.sum(ds, axis=0, keepdims=True)
            dq_part = _dot(dsb, k, "nn") * SCALE
            dcq_part = jnp.sum(ds, axis=-1, keepdims=True)
            rows_q = pl.ds(pl.multiple_of(qi * tq, tq), tq)

            @pl.when(kj == 0)
            def _():
                dq_ref[rows_q, :] = dq_part
                dcq_ref[rows_q, :] = dcq_part

            @pl.when(kj > 0)
            def _():
                dq_ref[rows_q, :] += dq_part
                dcq_ref[rows_q, :] += dcq_part

        @pl.when(qi > kj)
        def _():
            step(False)

        @pl.when(qi == kj)
        def _():
            step(True)

        @pl.when(qi == nq - 1)
        def _():
            dk_ref[...] = dk_sc[...] * SCALE
            dv_ref[...] = dv_sc[...]
            dck_ref[...] = -dc_sc[...]

    def rows(base):
        return pl.BlockSpec((tq, HEAD_DIM), lambda h, s, tab: (tab[0, s], base + h))

    def cols(base):
        return pl.BlockSpec((tq, HEAD_DIM), lambda h, s, tab: (tab[1, s], base + h))

    rvec = pl.BlockSpec((None, tq, 1), lambda h, s, tab: (h, tab[0, s], 0))
    cvec = pl.BlockSpec((None, 1, tq), lambda h, s, tab: (h, 0, tab[1, s]))
    sds = jax.ShapeDtypeStruct((T, FOX_W), F32)
    return pl.pallas_call(
        body, name="fox_bwd",
        grid_spec=pltpu.PrefetchScalarGridSpec(
            num_scalar_prefetch=1, grid=(FOX_HEADS, steps),
            in_specs=[rows(HQ), cols(HK), cols(HV), rvec, cvec, rows(0), rvec, rows(0)],
            out_specs=[pl.BlockSpec((T, HEAD_DIM), lambda h, s, tab: (0, h)), cols(0), cols(0), cvec,
                       pl.BlockSpec((None, T, 1), lambda h, s, tab: (h, 0, 0))],
            scratch_shapes=[pltpu.VMEM((tq, HEAD_DIM), F32), pltpu.VMEM((tq, HEAD_DIM), F32), pltpu.VMEM((1, tq), F32)]),
        out_shape=[sds, sds, sds, jax.ShapeDtypeStruct((FOX_HEADS, 1, T), F32), jax.ShapeDtypeStruct((FOX_HEADS, T, 1), F32)],
        compiler_params=_params(("parallel", "arbitrary")),
    )(_triangle(nq, True), qkv, qkv, qkv, cq, ck, out_f32, lse, dmix)


GW = SWA_GROUP * HEAD_DIM
GR = SWA_GROUP * WINDOW


def _swa_scores(q_ref, kp_ref, kc_ref, slope_ref, n):
    q = q_ref[...]
    qs = jnp.concatenate([q[:, t * HEAD_DIM:(t + 1) * HEAD_DIM] for t in range(SWA_GROUP)], axis=0)
    kb = jnp.concatenate([kp_ref[...], kc_ref[...]], axis=0)
    r = lax.broadcasted_iota(jnp.int32, (GR, 2 * WINDOW), 0) & (WINDOW - 1)
    jj = lax.broadcasted_iota(jnp.int32, (GR, 2 * WINDOW), 1)
    dist = WINDOW + r - jj
    valid = (dist >= 0) & (dist < WINDOW) & ((n > 0) | (jj >= WINDOW))
    s = _dot(qs, kb, "nt") * SCALE - slope_ref[...] * dist.astype(F32)
    return qs, kb, jnp.where(valid, s, NEG_INF), valid


def _swa_specs():
    HQ, HK, HV = C_SQ // GW, C_SK // HEAD_DIM, C_SV // HEAD_DIM
    q_spec = pl.BlockSpec((WINDOW, GW), lambda g, n: (n, HQ + g))

    def prev(base):
        return pl.BlockSpec((WINDOW, HEAD_DIM), lambda g, n: (jnp.maximum(n - 1, 0), base + g))

    def cur(base):
        return pl.BlockSpec((WINDOW, HEAD_DIM), lambda g, n: (n, base + g))

    col = pl.BlockSpec((None, GR, 1), lambda g, n: (g, 0, 0))
    return q_spec, prev(HK), cur(HK), prev(HV), cur(HV), col


def _swa_fwd(qkv, slopes, sinks):
    T = qkv.shape[0]
    nb = T // WINDOW
    assert C_SQ % GW == 0

    def body(q_ref, kp_ref, kc_ref, vp_ref, vc_ref, slope_ref, sink_ref, o_ref, lse_ref):
        n = pl.program_id(1)
        _, _, s, _ = _swa_scores(q_ref, kp_ref, kc_ref, slope_ref, n)
        m = jnp.maximum(jnp.max(s, axis=-1, keepdims=True), sink_ref[...])
        p = jnp.exp(s - m)
        l = jnp.sum(p, axis=-1, keepdims=True) + jnp.exp(sink_ref[...] - m)
        vb = jnp.concatenate([vp_ref[...], vc_ref[...]], axis=0)
        o = _dot(p.astype(BF16), vb, "nn") / l
        for t in range(SWA_GROUP):
            o_ref[:, t * HEAD_DIM:(t + 1) * HEAD_DIM] = o[t * WINDOW:(t + 1) * WINDOW, :].astype(BF16)
        lse_ref[...] = m + jnp.log(l)

    q_spec, kp, kc, vp, vc, col = _swa_specs()
    return pl.pallas_call(
        body, name="swa_fwd", grid=(SWA_KV_HEADS, nb), in_specs=[q_spec, kp, kc, vp, vc, col, col],
        out_specs=[pl.BlockSpec((WINDOW, GW), lambda g, n: (n, g)), pl.BlockSpec((None, None, GR, 1), lambda g, n: (g, n, 0, 0))],
        out_shape=[jax.ShapeDtypeStruct((T, SWA_HEADS * HEAD_DIM), BF16), jax.ShapeDtypeStruct((SWA_KV_HEADS, nb, GR, 1), F32)],
        compiler_params=_params(("parallel", "arbitrary")),
    )(qkv, qkv, qkv, qkv, qkv, slopes, sinks)


def _swa_bwd(qkv, slopes, sinks, out, lse, dmix):
    T = qkv.shape[0]
    nb = T // WINDOW
    DO = FOX_W // GW
    assert FOX_W % GW == 0

    def body(q_ref, kp_ref, kc_ref, vp_ref, vc_ref, slope_ref, sink_ref, o_ref, lse_ref, do_ref,
             dq_ref, dk_ref, dv_ref, dsink_ref, sink_sc):
        n = pl.program_id(1)

        @pl.when(n == 0)
        def _():
            dk_ref[...] = jnp.zeros_like(dk_ref)
            dv_ref[...] = jnp.zeros_like(dv_ref)
            sink_sc[...] = jnp.zeros_like(sink_sc)

        qs, kb, s, valid = _swa_scores(q_ref, kp_ref, kc_ref, slope_ref, n)
        lse = lse_ref[...]
        p = jnp.where(valid, jnp.exp(s - lse), 0.0)
        vb = jnp.concatenate([vp_ref[...], vc_ref[...]], axis=0)
        do = jnp.concatenate([do_ref[:, t * HEAD_DIM:(t + 1) * HEAD_DIM] for t in range(SWA_GROUP)], axis=0)
        oo = jnp.concatenate([o_ref[:, t * HEAD_DIM:(t + 1) * HEAD_DIM] for t in range(SWA_GROUP)], axis=0)
        dp = _dot(do, vb, "nt")
        delta = jnp.sum(do.astype(F32) * oo.astype(F32), axis=-1, keepdims=True)
        ds = p * (dp - delta)
        dsb = ds.astype(BF16)
        dq = _dot(dsb, kb, "nn") * SCALE
        for t in range(SWA_GROUP):
            dq_ref[:, t * HEAD_DIM:(t + 1) * HEAD_DIM] = dq[t * WINDOW:(t + 1) * WINDOW, :]
        dkb = _dot(dsb, qs, "tn") * SCALE
        dvb = _dot(p.astype(BF16), do, "tn")
        r_prev = pl.ds(pl.multiple_of(jnp.maximum(n - 1, 0) * WINDOW, WINDOW), WINDOW)
        r_cur = pl.ds(pl.multiple_of(n * WINDOW, WINDOW), WINDOW)
        dk_ref[r_prev, :] += dkb[:WINDOW, :]
        dk_ref[r_cur, :] += dkb[WINDOW:, :]
        dv_ref[r_prev, :] += dvb[:WINDOW, :]
        dv_ref[r_cur, :] += dvb[WINDOW:, :]
        sink_sc[...] -= jnp.exp(sink_ref[...] - lse) * delta

        @pl.when(n == nb - 1)
        def _():
            tot = [jnp.zeros((1, 128), F32) + jnp.sum(sink_sc[t * WINDOW:(t + 1) * WINDOW, :]) for t in range(SWA_GROUP)]
            dsink_ref[...] = jnp.concatenate(tot + [jnp.zeros((8 - SWA_GROUP, 128), F32)], axis=0)

    q_spec, kp, kc, vp, vc, col = _swa_specs()
    kv_acc = pl.BlockSpec((T, HEAD_DIM), lambda g, n: (0, g))
    return pl.pallas_call(
        body, name="swa_bwd", grid=(SWA_KV_HEADS, nb),
        in_specs=[q_spec, kp, kc, vp, vc, col, col, pl.BlockSpec((WINDOW, GW), lambda g, n: (n, g)),
                  pl.BlockSpec((None, None, GR, 1), lambda g, n: (g, n, 0, 0)), pl.BlockSpec((WINDOW, GW), lambda g, n: (n, DO + g))],
        out_specs=[pl.BlockSpec((WINDOW, GW), lambda g, n: (n, g)), kv_acc, kv_acc, pl.BlockSpec((None, 8, 128), lambda g, n: (g, 0, 0))],
        out_shape=[jax.ShapeDtypeStruct((T, SWA_HEADS * HEAD_DIM), F32), jax.ShapeDtypeStruct((T, SWA_KV_HEADS * HEAD_DIM), F32),
                   jax.ShapeDtypeStruct((T, SWA_KV_HEADS * HEAD_DIM), F32), jax.ShapeDtypeStruct((SWA_KV_HEADS, 8, 128), F32)],
        scratch_shapes=[pltpu.VMEM((GR, 1), F32)],
        compiler_params=_params(("parallel", "arbitrary")),
    )(qkv, qkv, qkv, qkv, qkv, slopes, sinks, out, lse, dmix)


def _mem_fwd(qkv, mk, mv):
    T, ML = qkv.shape[0], mk.shape[0]
    tq = _tile(T, 512)
    HQ = C_MQ // HEAD_DIM

    def body(q_ref, k_ref, v_ref, o_ref, lse_ref):
        s = _dot(q_ref[...], k_ref[...], "nt") * SCALE
        m = jnp.max(s, axis=-1, keepdims=True)
        p = jnp.exp(s - m)
        l = jnp.sum(p, axis=-1, keepdims=True)
        o_ref[...] = (_dot(p.astype(BF16), v_ref[...], "nn") / l).astype(BF16)
        lse_ref[...] = m + jnp.log(l)

    kv = pl.BlockSpec((ML, HEAD_DIM), lambda h, i: (0, h))
    return pl.pallas_call(
        body, name="mem_fwd", grid=(MEM_HEADS, T // tq),
        in_specs=[pl.BlockSpec((tq, HEAD_DIM), lambda h, i: (i, HQ + h)), kv, kv],
        out_specs=[pl.BlockSpec((tq, HEAD_DIM), lambda h, i: (i, h)), pl.BlockSpec((None, tq, 1), lambda h, i: (h, i, 0))],
        out_shape=[jax.ShapeDtypeStruct((T, MEM_HEADS * HEAD_DIM), BF16), jax.ShapeDtypeStruct((MEM_HEADS, T, 1), F32)],
        compiler_params=_params(("parallel", "arbitrary")),
    )(qkv, mk, mv)


def _mem_bwd(qkv, mk, mv, out, lse, dmix):
    T, ML = qkv.shape[0], mk.shape[0]
    tq = _tile(T, 512)
    HQ = C_MQ // HEAD_DIM
    DO = (FOX_W + SWA_HEADS * HEAD_DIM) // HEAD_DIM

    def body(q_ref, k_ref, v_ref, o_ref, lse_ref, do_ref, dq_ref, dk_ref, dv_ref):
        q, k, v, do = q_ref[...], k_ref[...], v_ref[...], do_ref[...]
        p = jnp.exp(_dot(q, k, "nt") * SCALE - lse_ref[...])
        dp = _dot(do, v, "nt")
        delta = jnp.sum(do.astype(F32) * o_ref[...].astype(F32), axis=-1, keepdims=True)
        dsb = (p * (dp - delta)).astype(BF16)
        dq_ref[...] = _dot(dsb, k, "nn") * SCALE
        dk_part = _dot(dsb, q, "tn") * SCALE
        dv_part = _dot(p.astype(BF16), do, "tn")

        @pl.when(pl.program_id(1) == 0)
        def _():
            dk_ref[...] = dk_part
            dv_ref[...] = dv_part

        @pl.when(pl.program_id(1) > 0)
        def _():
            dk_ref[...] += dk_part
            dv_ref[...] += dv_part

    kv = pl.BlockSpec((ML, HEAD_DIM), lambda h, i: (0, h))
    qb = pl.BlockSpec((tq, HEAD_DIM), lambda h, i: (i, h))
    return pl.pallas_call(
        body, name="mem_bwd", grid=(MEM_HEADS, T // tq),
        in_specs=[pl.BlockSpec((tq, HEAD_DIM), lambda h, i: (i, HQ + h)), kv, kv, qb,
                  pl.BlockSpec((None, tq, 1), lambda h, i: (h, i, 0)), pl.BlockSpec((tq, HEAD_DIM), lambda h, i: (i, DO + h))],
        out_specs=[qb, kv, kv],
        out_shape=[jax.ShapeDtypeStruct((T, MEM_HEADS * HEAD_DIM), F32), jax.ShapeDtypeStruct((ML, MEM_HEADS * HEAD_DIM), F32),
                   jax.ShapeDtypeStruct((ML, MEM_HEADS * HEAD_DIM), F32)],
        compiler_params=_params(("parallel", "arbitrary")),
    )(qkv, mk, mv, out, lse, dmix)


HBM = pl.BlockSpec(memory_space=pltpu.HBM)


def _place():
    x, y, c = lax.axis_index("x"), lax.axis_index("y"), lax.axis_index("c")
    chips = [(1 - x, y), (x, 1 - y), (1 - x, 1 - y)]
    return x, y, c, chips


def _remote(src, dst, send_sem, recv_sem, device):
    return pltpu.make_async_remote_copy(src_ref=src, dst_ref=dst, send_sem=send_sem, recv_sem=recv_sem,
                                        device_id=device, device_id_type=MESH)


def _place_ids():
    x, y, c = lax.axis_index("x"), lax.axis_index("y"), lax.axis_index("c")
    order = [2 * x + y, 2 * (1 - x) + y, 2 * x + (1 - y), 2 * (1 - x) + (1 - y)]
    return jnp.stack([2 * x + y, c] + order).astype(jnp.int32)


def _cast_place(name, w, ids):
    R, C = w.shape
    tr = _tile(R, 256, 16)

    def body(ids_ref, w_ref, o_ref):
        o_ref[...] = w_ref[...].astype(BF16)

    return pl.pallas_call(
        body, name=name,
        grid_spec=pltpu.PrefetchScalarGridSpec(
            num_scalar_prefetch=1, grid=(R // tr,), in_specs=[pl.BlockSpec((tr, C), lambda i, ids: (i, 0))],
            out_specs=pl.BlockSpec((None, tr, C), lambda i, ids: (ids[0], i, 0))),
        out_shape=jax.ShapeDtypeStruct((N_CHIPS, R, C), BF16), compiler_params=_params(("parallel",)),
    )(ids, w)


SEM = pl.BlockSpec(memory_space=pltpu.SEMAPHORE)
EFFECT = pltpu.SideEffectType.DATAFLOW_SIDE_EFFECTING


def _hbm(a):
    return pltpu.with_memory_space_constraint(a, pltpu.HBM)


def _gather_start(name, placed, after):
    n = len(placed)

    ns = 3 * n

    def body(*refs):
        send, recv = refs[n + 1:n + 1 + ns], refs[n + 1 + ns:n + 1 + 2 * ns]
        buf = refs[n + 1 + 2 * ns:2 * n + 1 + 2 * ns]
        token = refs[2 * n + 1 + 2 * ns]
        x, y, c, chips = _place()
        me = 2 * x + y
        for a in range(n):
            half = buf[a].shape[1] // 2
            mine = buf[a].at[me, pl.ds(c * half, half)]
            for j, (cx, cy) in enumerate(chips):
                _remote(mine, mine, send[3 * a + j], recv[3 * a + j], (cx, cy, c)).start()
        token[...] = jnp.zeros_like(token)

    res = pl.pallas_call(
        body, name=name, in_specs=[HBM] * n + [pl.BlockSpec(memory_space=pl.ANY)],
        out_specs=[SEM] * (2 * ns) + [HBM] * n + [pl.BlockSpec(memory_space=pltpu.VMEM)],
        out_shape=[pltpu.SemaphoreType.DMA(())] * (2 * ns)
        + [pltpu.HBM(s.shape, s.dtype) for s in placed] + [jax.ShapeDtypeStruct((8, 128), F32)],
        input_output_aliases={a: 2 * ns + a for a in range(n)},
        compiler_params=pltpu.CompilerParams(has_side_effects=EFFECT),
    )(*[_hbm(s) for s in placed], after)
    return list(res[:ns]), list(res[ns:2 * ns]), list(res[2 * ns:2 * ns + n]), res[2 * ns + n]


def _gather_wait(name, send, recv, bufs, after):
    n = len(bufs)

    ns = 3 * n

    def body(*refs):
        buf = refs[:n]
        send_ref, recv_ref = refs[n:n + ns], refs[n + ns:n + 2 * ns]
        x, y, c, chips = _place()
        ids = [2 * cx + cy for cx, cy in chips]
        for a in range(n):
            half = buf[a].shape[1] // 2
            for j in range(3):
                landed = buf[a].at[ids[j], pl.ds(c * half, half)]
                cp = _remote(landed, landed, send_ref[3 * a + j], recv_ref[3 * a + j], (x, y, c))
                cp.wait_send()
                cp.wait_recv()

    res = pl.pallas_call(
        body, name=name, in_specs=[HBM] * n + [SEM] * (2 * ns) + [pl.BlockSpec(memory_space=pl.ANY)], out_specs=[HBM] * n,
        out_shape=[pltpu.HBM(s.shape, s.dtype) for s in bufs], input_output_aliases={a: a for a in range(n)},
        compiler_params=pltpu.CompilerParams(has_side_effects=EFFECT),
    )(*bufs, *send, *recv, after)
    return list(res)


def _gather_forward(name, bufs):
    n = len(bufs)

    def body(*refs):
        buf = refs[n:2 * n]
        send, recv = refs[2 * n:]
        x, y, c, chips = _place()
        ids = [2 * cx + cy for cx, cy in chips]
        copies = []
        for a in range(n):
            half = buf[a].shape[1] // 2
            for j in range(3):
                landed = buf[a].at[ids[j], pl.ds(c * half, half)]
                cp = _remote(landed, landed, send.at[a, j], recv.at[a, j], (x, y, 1 - c))
                cp.start()
                copies.append(cp)
        for a in range(n):
            half = buf[a].shape[1] // 2
            for j in range(3):
                landed = buf[a].at[ids[j], pl.ds((1 - c) * half, half)]
                _remote(landed, landed, send.at[a, j], recv.at[a, j], (x, y, c)).wait_recv()
        for cp in copies:
            cp.wait_send()

    return pl.pallas_call(
        body, name=name, in_specs=[HBM] * n, out_specs=[HBM] * n,
        out_shape=[jax.ShapeDtypeStruct(s.shape, s.dtype) for s in bufs], input_output_aliases={a: a for a in range(n)},
        scratch_shapes=[pltpu.SemaphoreType.DMA((n, 3)), pltpu.SemaphoreType.DMA((n, 3))],
    )(*bufs)


def _pair_start(name, grads):
    n = len(grads)
    ns = N_CHIPS * n

    def body(*refs):
        send, recv = refs[2 * n:2 * n + ns], refs[2 * n + ns:2 * n + 2 * ns]
        src = refs[2 * n + 2 * ns:3 * n + 2 * ns]
        land = refs[3 * n + 2 * ns:4 * n + 2 * ns]
        token = refs[4 * n + 2 * ns]
        x, y, c, chips = _place()
        order = [2 * x + y] + [2 * cx + cy for cx, cy in chips]
        for a in range(n):
            half = src[a].shape[1] // 2
            for j in range(N_CHIPS):
                _remote(src[a].at[order[j], pl.ds((1 - c) * half, half)], land[a].at[j],
                        send[N_CHIPS * a + j], recv[N_CHIPS * a + j], (x, y, 1 - c)).start()
        token[...] = jnp.zeros_like(token)

    lands = [jax.ShapeDtypeStruct((N_CHIPS, g.shape[1] // 2, g.shape[2]), g.dtype) for g in grads]
    res = pl.pallas_call(
        body, name=name, in_specs=[HBM] * (2 * n),
        out_specs=[SEM] * (2 * ns) + [HBM] * (2 * n) + [pl.BlockSpec(memory_space=pltpu.VMEM)],
        out_shape=[pltpu.SemaphoreType.DMA(())] * (2 * ns) + [pltpu.HBM(g.shape, g.dtype) for g in grads]
        + [pltpu.HBM(l.shape, l.dtype) for l in lands] + [jax.ShapeDtypeStruct((8, 128), F32)],
        input_output_aliases={a: 2 * ns + a for a in range(2 * n)},
        compiler_params=pltpu.CompilerParams(has_side_effects=EFFECT),
    )(*[_hbm(g) for g in grads], *[_hbm(lax.empty(l.shape, l.dtype)) for l in lands])
    return list(res[:ns]), list(res[ns:2 * ns]), list(res[2 * ns:2 * ns + n]), list(res[2 * ns + n:2 * ns + 2 * n]), res[2 * ns + 2 * n]


def _pair_wait(name, send, recv, grads, lands, after):
    n = len(grads)
    ns = N_CHIPS * n

    def body(*refs):
        src, land = refs[:n], refs[n:2 * n]
        send_ref, recv_ref = refs[2 * n:2 * n + ns], refs[2 * n + ns:2 * n + 2 * ns]
        x, y, c, _ = _place()
        for a in range(n):
            for j in range(N_CHIPS):
                cp = _remote(land[a].at[j], land[a].at[j], send_ref[N_CHIPS * a + j], recv_ref[N_CHIPS * a + j], (x, y, c))
                cp.wait_send()
                cp.wait_recv()

    res = pl.pallas_call(
        body, name=name, in_specs=[HBM] * (2 * n) + [SEM] * (2 * ns) + [pl.BlockSpec(memory_space=pl.ANY)],
        out_specs=[HBM] * (2 * n), out_shape=[pltpu.HBM(g.shape, g.dtype) for g in grads] + [pltpu.HBM(l.shape, l.dtype) for l in lands],
        input_output_aliases={a: a for a in range(2 * n)},
        compiler_params=pltpu.CompilerParams(has_side_effects=EFFECT),
    )(*grads, *lands, *send, *recv, after)
    return list(res[:n]), list(res[n:])


def _chip_start(name, parts):
    n = len(parts)
    ns = 3 * n

    def body(*refs):
        send, recv = refs[2 * n:2 * n + ns], refs[2 * n + ns:2 * n + 2 * ns]
        src = refs[2 * n + 2 * ns:3 * n + 2 * ns]
        land = refs[3 * n + 2 * ns:4 * n + 2 * ns]
        token = refs[4 * n + 2 * ns]
        x, y, c, chips = _place()
        for a in range(n):
            for j, (cx, cy) in enumerate(chips):
                _remote(src[a].at[j], land[a].at[j], send[3 * a + j], recv[3 * a + j], (cx, cy, c)).start()
        token[...] = jnp.zeros_like(token)

    res = pl.pallas_call(
        body, name=name, in_specs=[HBM] * (2 * n),
        out_specs=[SEM] * (2 * ns) + [HBM] * (2 * n) + [pl.BlockSpec(memory_space=pltpu.VMEM)],
        out_shape=[pltpu.SemaphoreType.DMA(())] * (2 * ns) + [pltpu.HBM(p.shape, p.dtype) for p in parts] * 2
        + [jax.ShapeDtypeStruct((8, 128), F32)],
        input_output_aliases={a: 2 * ns + a for a in range(2 * n)},
        compiler_params=pltpu.CompilerParams(has_side_effects=EFFECT),
    )(*[_hbm(p) for p in parts], *[_hbm(lax.empty(p.shape, p.dtype)) for p in parts])
    return list(res[:ns]), list(res[ns:2 * ns]), list(res[2 * ns:2 * ns + n]), list(res[2 * ns + n:2 * ns + 2 * n]), res[2 * ns + 2 * n]


def _chip_wait(name, send, recv, parts, lands, after):
    n = len(parts)
    ns = 3 * n

    def body(*refs):
        src, land = refs[:n], refs[n:2 * n]
        send_ref, recv_ref = refs[2 * n:2 * n + ns], refs[2 * n + ns:2 * n + 2 * ns]
        x, y, c, _ = _place()
        for a in range(n):
            for j in range(3):
                cp = _remote(src[a].at[j], land[a].at[j], send_ref[3 * a + j], recv_ref[3 * a + j], (x, y, c))
                cp.wait_send()
                cp.wait_recv()

    res = pl.pallas_call(
        body, name=name, in_specs=[HBM] * (2 * n) + [SEM] * (2 * ns) + [pl.BlockSpec(memory_space=pl.ANY)],
        out_specs=[HBM] * (2 * n), out_shape=[pltpu.HBM(p.shape, p.dtype) for p in parts] * 2,
        input_output_aliases={a: a for a in range(2 * n)},
        compiler_params=pltpu.CompilerParams(has_side_effects=EFFECT),
    )(*parts, *lands, *send, *recv, after)
    return list(res[n:])


def _pair_share(name, shards):
    n = len(shards)

    def body(*refs):
        buf = refs[n:2 * n]
        send, recv = refs[2 * n:]
        x, y, c, _ = _place()
        copies = []
        for a in range(n):
            half = buf[a].shape[0] // 2
            mine = buf[a].at[pl.ds(c * half, half)]
            cp = _remote(mine, mine, send.at[a], recv.at[a], (x, y, 1 - c))
            cp.start()
            copies.append(cp)
        for a, cp in enumerate(copies):
            half = buf[a].shape[0] // 2
            cp.wait_send()
            theirs = buf[a].at[pl.ds((1 - c) * half, half)]
            _remote(theirs, theirs, send.at[a], recv.at[a], (x, y, c)).wait_recv()

    return pl.pallas_call(
        body, name=name, in_specs=[HBM] * n, out_specs=[HBM] * n,
        out_shape=[jax.ShapeDtypeStruct(s.shape, s.dtype) for s in shards], input_output_aliases={a: a for a in range(n)},
        scratch_shapes=[pltpu.SemaphoreType.DMA((n,)), pltpu.SemaphoreType.DMA((n,))],
    )(*shards)


def _small_start(buf):
    R, W = buf.shape
    ns = N_DEV - 1

    def body(*refs):
        send, recv = refs[2:2 + ns], refs[2 + ns:2 + 2 * ns]
        src, land, token = refs[2 + 2 * ns], refs[3 + 2 * ns], refs[4 + 2 * ns]
        x, y, c, _ = _place()
        me = 4 * x + 2 * y + c
        for k in range(1, N_DEV):
            peer = (x ^ (k >> 2), y ^ ((k >> 1) & 1), c ^ (k & 1))
            _remote(src, land.at[me], send[k - 1], recv[k - 1], peer).start()
        token[...] = jnp.zeros_like(token)

    res = pl.pallas_call(
        body, name="small_start", in_specs=[HBM, HBM],
        out_specs=[SEM] * (2 * ns) + [HBM, HBM, pl.BlockSpec(memory_space=pltpu.VMEM)],
        out_shape=[pltpu.SemaphoreType.DMA(())] * (2 * ns) + [pltpu.HBM((R, W), F32), pltpu.HBM((N_DEV, R, W), F32),
                                                                jax.ShapeDtypeStruct((8, 128), F32)],
        input_output_aliases={0: 2 * ns, 1: 2 * ns + 1},
        compiler_params=pltpu.CompilerParams(has_side_effects=EFFECT),
    )(_hbm(buf), _hbm(jnp.zeros((N_DEV, R, W), F32)))
    return list(res[:ns]), list(res[ns:2 * ns]), res[2 * ns], res[2 * ns + 1], res[2 * ns + 2]


def _small_wait(send, recv, buf, land, after):
    ns = N_DEV - 1

    def body(*refs):
        land_ref = refs[1]
        send_ref, recv_ref = refs[2:2 + ns], refs[2 + ns:2 + 2 * ns]
        x, y, c, _ = _place()
        me = 4 * x + 2 * y + c
        for k in range(1, N_DEV):
            landed = land_ref.at[me ^ k]
            cp = _remote(landed, landed, send_ref[k - 1], recv_ref[k - 1], (x, y, c))
            cp.wait_send()
            cp.wait_recv()

    return pl.pallas_call(
        body, name="small_wait", in_specs=[HBM, HBM] + [SEM] * (2 * ns) + [pl.BlockSpec(memory_space=pl.ANY)],
        out_specs=[HBM, HBM], out_shape=[pltpu.HBM(buf.shape, buf.dtype), pltpu.HBM(land.shape, land.dtype)],
        input_output_aliases={0: 0, 1: 1}, compiler_params=pltpu.CompilerParams(has_side_effects=EFFECT),
    )(buf, land, *send, *recv, after)


def _small_sum(buf, land):
    def body(buf_ref, land_ref, out_ref):
        x, y, c, _ = _place()
        me = 4 * x + 2 * y + c
        total = None
        for d in range(N_DEV):
            term = jnp.where(me == d, buf_ref[...], land_ref[d])
            total = term if total is None else total + term
        out_ref[...] = total

    return pl.pallas_call(body, name="small_sum", out_shape=jax.ShapeDtypeStruct(buf.shape, F32))(buf, land)


def _pair_sum_bf16(name, grad, theirs, ids):
    _, R2, C = theirs.shape
    tr = _tile(R2, 256, 16)
    nrb = R2 // tr

    def body(ids_ref, a_ref, b_ref, o_ref):
        o_ref[...] = (a_ref[...] + b_ref[...]).astype(BF16)

    return pl.pallas_call(
        body, name=name,
        grid_spec=pltpu.PrefetchScalarGridSpec(
            num_scalar_prefetch=1, grid=(3, nrb),
            in_specs=[pl.BlockSpec((None, tr, C), lambda j, i, ids: (ids[3 + j], ids[1] * nrb + i, 0)),
                      pl.BlockSpec((None, tr, C), lambda j, i, ids: (j + 1, i, 0))],
            out_specs=pl.BlockSpec((None, tr, C), lambda j, i, ids: (j, i, 0))),
        out_shape=jax.ShapeDtypeStruct((3, R2, C), BF16), compiler_params=_params(("parallel", "parallel")),
    )(ids, grad, theirs)


def _chip_sum(name, grad, theirs, arrived, ids):
    _, R2, C = theirs.shape
    tr = _tile(R2, 256, 16)
    nrb = R2 // tr

    def body(ids_ref, a_ref, b_ref, r_ref, o_ref):
        tot = a_ref[...] + b_ref[...]
        for j in range(3):
            tot = tot + r_ref[j].astype(F32)
        o_ref[...] = tot

    return pl.pallas_call(
        body, name=name,
        grid_spec=pltpu.PrefetchScalarGridSpec(
            num_scalar_prefetch=1, grid=(nrb,),
            in_specs=[pl.BlockSpec((None, tr, C), lambda i, ids: (ids[0], ids[1] * nrb + i, 0)),
                      pl.BlockSpec((None, tr, C), lambda i, ids: (0, i, 0)),
                      pl.BlockSpec((3, tr, C), lambda i, ids: (0, i, 0))],
            out_specs=pl.BlockSpec((tr, C), lambda i, ids: (ids[1] * nrb + i, 0))),
        out_shape=jax.ShapeDtypeStruct((2 * R2, C), F32), compiler_params=_params(("parallel",)),
    )(ids, grad, theirs, arrived)


def _adamw(name, w, g, m, v):
    R, C = w.shape
    tr = _tile(R, 128, 8)
    c1 = 1.0 / (1.0 - ADAM_B1 ** ADAM_STEP)
    c2 = 1.0 / (1.0 - ADAM_B2 ** ADAM_STEP)

    def body(w_ref, g_ref, m_ref, v_ref, d_ref, mo_ref, vo_ref):
        gv = g_ref[...]
        mn = ADAM_B1 * m_ref[...] + (1.0 - ADAM_B1) * gv
        vn = ADAM_B2 * v_ref[...] + (1.0 - ADAM_B2) * (gv * gv)
        d_ref[...] = -ADAM_LR * ((mn * c1) / (jnp.sqrt(vn * c2) + ADAM_EPS) + ADAM_WD * w_ref[...])
        mo_ref[...] = mn
        vo_ref[...] = vn

    spec = pl.BlockSpec((tr, C), lambda i: (i, 0))
    sds = jax.ShapeDtypeStruct((R, C), F32)
    return pl.pallas_call(body, name=name, grid=(R // tr,), in_specs=[spec] * 4, out_specs=[spec] * 3, out_shape=[sds] * 3,
                          compiler_params=_params(("parallel",)))(w, g, m, v)


SMALL = ["ffn1_norm", "mix_norm", "mem_norm", "forget_bias", "fox_q_gain", "fox_k_gain", "swa_q_gain", "swa_k_gain", "swa_sinks",
         "mem_q_gain", "mem_k_gain", "ffn2_norm"]
LARGE = ["ffn1_gate", "ffn1_up", "ffn1_down", "w_in", "w_mem_k", "w_mem_v", "w_out", "ffn2_gate", "ffn2_up", "ffn2_down"]
GATHER_GROUPS = [["ffn1_gate", "ffn1_up"], ["ffn1_down", "w_in", "w_mem_k", "w_mem_v"], ["w_out", "ffn2_gate", "ffn2_up", "ffn2_down"]]
WEIGHTS = ["ffn1_norm", "ffn1_gate", "ffn1_up", "ffn1_down", "mix_norm", "mem_norm", "w_in", "forget_bias", "w_mem_k", "w_mem_v",
           "fox_q_gain", "fox_k_gain", "swa_q_gain", "swa_k_gain", "swa_sinks", "mem_q_gain", "mem_k_gain", "w_out", "ffn2_norm",
           "ffn2_gate", "ffn2_up", "ffn2_down"]


def _pad_proj_cols(w):
    out = jnp.zeros((w.shape[0], PROJ_W), w.dtype)
    for start, width, pstart in REF_GROUPS:
        out = lax.dynamic_update_slice(out, w[:, start:start + width], (0, pstart))
    return out


def _unpad_proj_cols(w):
    return jnp.concatenate([w[:, pstart:pstart + width] for _, width, pstart in REF_GROUPS], axis=1)


def _pack_small(vals):
    flat = jnp.concatenate([vals[k].reshape(-1).astype(F32) for k in SMALL + ["loss"]])
    n = flat.shape[0]
    total = -(-n // 1024) * 1024
    return jnp.pad(flat, (0, total - n)).reshape(total // 128, 128)


def _unpack_small(buf, shapes):
    flat = buf.reshape(-1)
    out, off = {}, 0
    for k in SMALL + ["loss"]:
        size = int(np.prod(shapes[k]))
        out[k] = flat[off:off + size].reshape(shapes[k])
        off += size
    return out


def kernel(x, mem, ffn1_norm, ffn1_gate, ffn1_up, ffn1_down, mix_norm, mem_norm, w_in, forget_bias, w_mem_k, w_mem_v, fox_q_gain, fox_k_gain, swa_q_gain, swa_k_gain, swa_sinks, mem_q_gain, mem_k_gain, w_out, ffn2_norm, ffn2_gate, ffn2_up, ffn2_down, loss_target, m_ffn1_norm, m_ffn1_gate, m_ffn1_up, m_ffn1_down, m_mix_norm, m_mem_norm, m_w_in, m_forget_bias, m_w_mem_k, m_w_mem_v, m_fox_q_gain, m_fox_k_gain, m_swa_q_gain, m_swa_k_gain, m_swa_sinks, m_mem_q_gain, m_mem_k_gain, m_w_out, m_ffn2_norm, m_ffn2_gate, m_ffn2_up, m_ffn2_down, v_ffn1_norm, v_ffn1_gate, v_ffn1_up, v_ffn1_down, v_mix_norm, v_mem_norm, v_w_in, v_forget_bias, v_w_mem_k, v_w_mem_v, v_fox_q_gain, v_fox_k_gain, v_swa_q_gain, v_swa_k_gain, v_swa_sinks, v_mem_q_gain, v_mem_k_gain, v_w_out, v_ffn2_norm, v_ffn2_gate, v_ffn2_up, v_ffn2_down):
    given = dict(locals())
    T, D = x.shape[1], x.shape[2]
    ML = mem.shape[1]
    xin = x.reshape(T, D)
    target = loss_target.reshape(T, D)
    memin = mem.reshape(ML, D)

    ids = _place_ids()
    shard = {k: given[k][0] for k in LARGE}
    shard["w_in"] = _pad_proj_cols(shard["w_in"])
    started, after = [], ids
    for gi, group in enumerate(GATHER_GROUPS):
        placed = [_cast_place("cast_" + k, shard[k], ids) for k in group]
        send, recv, bufs, after = _gather_start("gather_start_%d" % gi, placed, after)
        started.append((send, recv, bufs))

    def arrive(gi, done):
        send, recv, bufs = started[gi]
        bufs = _gather_wait("gather_wait_%d" % gi, send, recv, bufs, done)
        return dict(zip(GATHER_GROUPS[gi], _gather_forward("gather_forward_%d" % gi, bufs)))

    gains = jnp.concatenate([fox_q_gain, fox_k_gain, swa_q_gain, swa_k_gain, mem_q_gain,
                             jnp.pad(forget_bias, ((0, 0), (0, HEAD_DIM - FOX_HEADS))), jnp.zeros((2, HEAD_DIM), F32)], axis=0)
    slopes_np = 2.0 ** (-8.0 * np.arange(1, SWA_HEADS + 1) / SWA_HEADS)
    slopes = jnp.asarray(np.repeat(slopes_np, WINDOW).reshape(SWA_KV_HEADS, GR, 1), F32)
    sinks = jnp.repeat(swa_sinks.reshape(SWA_HEADS), WINDOW).reshape(SWA_KV_HEADS, GR, 1)

    h1 = _rms_fwd("ffn1_norm_fwd", xin, ffn1_norm + after[0, 0])
    full = arrive(0, h1)
    wg1, wu1 = full["ffn1_gate"], full["ffn1_up"]
    g1, u1, a1 = _ffn_gu("ffn1_gate_up", h1, wg1, wu1)
    full = arrive(1, a1)
    wd1 = full["ffn1_down"].reshape(-1, D)
    win = full["w_in"].reshape(D, PROJ_W)
    wmk = full["w_mem_k"].reshape(D, MEM_HEADS * HEAD_DIM)
    wmv = full["w_mem_v"].reshape(D, MEM_HEADS * HEAD_DIM)
    x1 = _ffn_down("ffn1_down", a1, wd1, xin)
    h2 = _rms_fwd("mix_norm_fwd", x1, mix_norm)
    proj = _mm2d("proj_in", h2, win, "nn", F32, tn=1408, tk=2048, n_outer=True)
    qkv, logf = _prep_fwd(proj, gains)
    cum = _cumsum_rows("forget_cumsum", [logf], False)
    cum_h = cum[:, :FOX_HEADS].T
    cq, ck = cum_h.reshape(FOX_HEADS, T, 1), cum_h.reshape(FOX_HEADS, 1, T)
    mn = _rms_fwd("mem_norm_fwd", memin, mem_norm)
    mk_raw = _mm2d("mem_k_proj", mn, wmk, "nn", F32)
    mv = _mm2d("mem_v_proj", mn, wmv, "nn", BF16)
    mk = _head_norm_rows(mk_raw, mem_k_gain)
    out_a, out_a_f32, lse_a = _fox_fwd(qkv, cq, ck)
    out_b, lse_b = _swa_fwd(qkv, slopes, sinks)
    out_c, lse_c = _mem_fwd(qkv, mk, mv)
    mixed = jnp.concatenate([out_a, out_b, out_c], axis=1)
    full = arrive(2, mixed)
    wo = full["w_out"].reshape(-1, D)
    wg2, wu2, wd2 = full["ffn2_gate"], full["ffn2_up"], full["ffn2_down"].reshape(-1, D)
    x2 = _mm2d("mix_out", mixed, wo, "nn", F32, tk=2048, n_outer=True, extras=[x1], epilogue=lambda accs, ex: [ex[0] + accs[0]])
    h3 = _rms_fwd("ffn2_norm_fwd", x2, ffn2_norm)
    g2, u2, a2 = _ffn_gu("ffn2_gate_up", h3, wg2, wu2)
    x3 = _ffn_down("ffn2_down", a2, wd2, x2)
    dx3, dyb3, loss_part = _loss_head(x3, target)

    grads, small, res = {}, {"loss": loss_part[0, 0]}, {}

    def pair_off(tag, group):
        send, recv, own, lands, token = _pair_start("grad_pair_start_" + tag, [grads[k] for k in group])
        return (group, send, recv, own, lands), token

    def chip_off(tag, started, done):
        group, send, recv, own, lands = started
        own, theirs = _pair_wait("grad_pair_wait_" + tag, send, recv, own, lands, done)
        grads.update(zip(group, own))
        to_chips = [_pair_sum_bf16("pair_sum_" + k, grads[k], b, ids) for k, b in zip(group, theirs)]
        send, recv, parts, lands, token = _chip_start("grad_chip_start_" + tag, to_chips)
        return (group, theirs, send, recv, parts, lands), token

    def finish(tag, state, done):
        group, theirs, send, recv, parts, lands = state
        arrived = _chip_wait("grad_chip_wait_" + tag, send, recv, parts, lands, done)
        halves = [_chip_sum("chip_sum_" + k, grads[k], b, r, ids) for k, b, r in zip(group, theirs, arrived)]
        reduced = dict(zip(group, _pair_share("grad_pair_share_" + tag, halves)))
        last = None
        for k in group:
            gk = _unpad_proj_cols(reduced[k]) if k == "w_in" else reduced[k]
            d, mo, vo = _adamw("adamw_" + k, given[k][0], gk, given["m_" + k][0], given["v_" + k][0])
            res[k] = tuple(t[None] for t in (gk, d, mo, vo))
            last = vo
        return last

    dg2, du2 = _ffn_bwd_act("ffn2", dyb3, wd2, g2, u2, N_CHIPS)
    grads["ffn2_down"] = _ffn_bwd_down("ffn2", a2, dyb3, N_CHIPS).reshape(N_CHIPS, -1, D)
    grads["ffn2_gate"], grads["ffn2_up"] = _ffn_bwd_gate_up("ffn2", h3, dg2, du2, N_CHIPS)
    started, token = pair_off("a", ["ffn2_gate", "ffn2_up", "ffn2_down"])
    dh3 = _ffn_bwd_x("ffn2", dg2, du2, wg2, wu2, token)
    state_a, token = chip_off("a", started, dh3)
    dx2, dx2b, small["ffn2_norm"] = _rms_bwd("ffn2_norm_bwd", dh3, x2, ffn2_norm + token[0, 0], dx3, 1.0)
    dmix = _mm2d("mix_out_dx", dx2b, wo, "nt", BF16, tk=2048, n_outer=True)
    grads["w_out"] = _mm2d("mix_out_dw", mixed, dx2b, "tn", F32, tk=1024).reshape(N_CHIPS, -1, D)
    dfq, dfk, dfv, dck, dcq = _fox_bwd(qkv, cq, ck, out_a_f32, lse_a, dmix)
    dsq, dsk, dsv, dsink = _swa_bwd(qkv, slopes, sinks, out_b, lse_b, dmix)
    dmq, dmk, dmv = _mem_bwd(qkv, mk, mv, out_c, lse_c, dmix)
    small["swa_sinks"] = dsink[:, :SWA_GROUP, 0].reshape(1, SWA_HEADS)
    dcum = [jnp.pad(d.reshape(FOX_HEADS, T).T, ((0, 0), (0, HEAD_DIM - FOX_HEADS))) for d in (dck, dcq)]
    dlogf = _cumsum_rows("forget_cumsum_bwd", dcum, True)
    dproj, dgains = _prep_bwd(proj, gains, dfq, dfk, dfv, dsq, dsk, dsv, dmq, dlogf)
    for row, k in enumerate(["fox_q_gain", "fox_k_gain", "swa_q_gain", "swa_k_gain", "mem_q_gain"]):
        small[k] = dgains[row:row + 1, :]
    small["forget_bias"] = dgains[5:6, :FOX_HEADS]
    grads["w_in"] = _mm2d("proj_in_dw", h2, dproj, "tn", F32, tn=1408, tk=1024).reshape(N_CHIPS, -1, PROJ_W)
    dmk_raw, small["mem_k_gain"] = _head_norm_rows_bwd(mk_raw, mem_k_gain, dmk)
    dmvb = dmv.astype(BF16)
    grads["w_mem_k"] = _mm2d("mem_k_dw", mn, dmk_raw, "tn", F32).reshape(N_CHIPS, -1, MEM_HEADS * HEAD_DIM)
    grads["w_mem_v"] = _mm2d("mem_v_dw", mn, dmvb, "tn", F32).reshape(N_CHIPS, -1, MEM_HEADS * HEAD_DIM)
    dmn = _mm2d("mem_k_dx", dmk_raw, wmk, "nt", F32)
    dmn = _mm2d("mem_v_dx", dmvb, wmv, "nt", F32, extras=[dmn], epilogue=lambda accs, ex: [ex[0] + accs[0]])
    _, _, small["mem_norm"] = _rms_bwd("mem_norm_bwd", dmn, memin, mem_norm, jnp.zeros_like(memin), 1.0)
    started, token = pair_off("b", ["w_out", "w_in", "w_mem_k", "w_mem_v"])
    dh2 = _mm2d("proj_in_dx", dproj, win, "nt", F32, tm=1024, tk=1408, after=token)
    state_b, token = chip_off("b", started, dh2)
    dx1, dyb1, small["mix_norm"] = _rms_bwd("mix_norm_bwd", dh2, x1, mix_norm + token[0, 0], dx2, 0.5)
    grads["ffn1_down"] = _ffn_bwd_down("ffn1", a1, dyb1, N_CHIPS).reshape(N_CHIPS, -1, D)
    started, token = pair_off("c", ["ffn1_down"])
    dg1, du1 = _ffn_bwd_act("ffn1", dyb1, wd1, g1, u1, N_CHIPS, after=token)
    state_c, token = chip_off("c", started, dg1)
    grads["ffn1_gate"], grads["ffn1_up"] = _ffn_bwd_gate_up("ffn1", h1, dg1, du1, N_CHIPS, after=token)
    started, token = pair_off("d", ["ffn1_gate", "ffn1_up"])
    dh1 = _ffn_bwd_x("ffn1", dg1, du1, wg1, wu1, token)
    state_d, token = chip_off("d", started, dh1)
    grad_x, _, small["ffn1_norm"] = _rms_bwd("ffn1_norm_bwd", dh1, xin, ffn1_norm + token[0, 0], dx1, 1.0)

    s_send, s_recv, s_buf, s_land, token = _small_start(_pack_small(small))

    done = finish("a", state_a, token)
    done = finish("b", state_b, done)
    done = finish("c", state_c, done)
    done = finish("d", state_d, done)

    shapes = {k: given[k].shape for k in SMALL}
    shapes["loss"] = ()
    s_buf, s_land = _small_wait(s_send, s_recv, s_buf, s_land, done)
    red_small = _unpack_small(_small_sum(s_buf, s_land), shapes)
    loss = red_small["loss"]
    zero = {"loss": jnp.zeros((), F32)}
    packed = [_pack_small({**zero, **{k: src[k] for k in SMALL}}) for src in (
        {k: given[k] for k in SMALL}, red_small, {k: given["m_" + k] for k in SMALL}, {k: given["v_" + k] for k in SMALL})]
    d_s, m_s, v_s = (_unpack_small(t, shapes) for t in _adamw("adamw_small", *packed))
    for k in SMALL:
        res[k] = (red_small[k], d_s[k], m_s[k], v_s[k])

    outs = [loss, grad_x.reshape(1, T, D)]
    for part in range(4):
        outs += [res[k][part] for k in WEIGHTS]
    return tuple(outs)
```

```python
import functools

import numpy as np
import jax
import jax.numpy as jnp
from jax import lax
from jax.experimental import pallas as pl
from jax.experimental.pallas import tpu as pltpu

F32 = jnp.float32
BF16 = jnp.bfloat16
MESH = pl.DeviceIdType.MESH

HEAD_DIM = 128
FOX_HEADS = 6
SWA_HEADS = 6
SWA_KV_HEADS = 2
SWA_GROUP = SWA_HEADS // SWA_KV_HEADS
MEM_HEADS = 4
WINDOW = 128
EPS = 1e-6
NEG_INF = -1e30
SCALE = HEAD_DIM ** -0.5

C_FQ = 0
C_FK = C_FQ + FOX_HEADS * HEAD_DIM
C_FV = C_FK + FOX_HEADS * HEAD_DIM
C_SQ = C_FV + FOX_HEADS * HEAD_DIM
C_SK = C_SQ + SWA_HEADS * HEAD_DIM
C_SV = C_SK + SWA_KV_HEADS * HEAD_DIM
C_MQ = C_SV + SWA_KV_HEADS * HEAD_DIM
C_FL = C_MQ + MEM_HEADS * HEAD_DIM
PROJ_W = C_FL + HEAD_DIM
FOX_W = FOX_HEADS * HEAD_DIM
REF_GROUPS = [
    (0, FOX_W, C_FQ), (FOX_W, FOX_W, C_FK), (2 * FOX_W, FOX_W, C_FV), (3 * FOX_W, FOX_HEADS, C_FL),
    (3 * FOX_W + FOX_HEADS, SWA_HEADS * HEAD_DIM, C_SQ),
    (3 * FOX_W + FOX_HEADS + SWA_HEADS * HEAD_DIM, SWA_KV_HEADS * HEAD_DIM, C_SK),
    (3 * FOX_W + FOX_HEADS + (SWA_HEADS + SWA_KV_HEADS) * HEAD_DIM, SWA_KV_HEADS * HEAD_DIM, C_SV),
    (3 * FOX_W + FOX_HEADS + (SWA_HEADS + 2 * SWA_KV_HEADS) * HEAD_DIM, MEM_HEADS * HEAD_DIM, C_MQ),
]

ADAM_LR = 0.001
ADAM_B1 = 0.9
ADAM_B2 = 0.999
ADAM_EPS = 1e-08
ADAM_WD = 0.01
ADAM_STEP = 10

V7X_VMEM_LIMIT = 56 * 1024 * 1024
N_CHIPS = 4
N_DEV = 8


def _tile(n, pref, mult=128):
    t = (min(pref, n) // mult) * mult
    while t >= mult:
        if n % t == 0:
            return t
        t -= mult
    return n


def _params(sem):
    return pltpu.CompilerParams(dimension_semantics=sem, vmem_limit_bytes=V7X_VMEM_LIMIT)


_DIMS = {"nn": (((1,), (0,)), ((), ())), "nt": (((1,), (1,)), ((), ())), "tn": (((0,), (0,)), ((), ()))}


def _dot(a, b, mode):
    return lax.dot_general(a, b, _DIMS[mode], preferred_element_type=F32)


def _mm(name, grid, pairs, acc_of, acc_shapes, extras, outs, epilogue, after=None):
    n_p, n_e, n_o, n_a = len(pairs), len(extras), len(outs), len(acc_shapes)
    n_w = 0 if after is None else 1
    nk = grid[2]
    n_in = sum(1 if a is None else 2 for a, *_ in pairs)

    def body(*refs):
        ex = refs[n_in:n_in + n_e]
        out = refs[n_in + n_e + n_w:n_in + n_e + n_w + n_o]
        accs = refs[n_in + n_e + n_w + n_o:]
        parts = [None] * n_a
        at = 0
        for p in range(n_p):
            if pairs[p][0] is None:
                a_ref, b_ref = refs[0], refs[at]
                at += 1
            else:
                a_ref, b_ref = refs[at], refs[at + 1]
                at += 2
            d = _dot(a_ref[...], b_ref[...], pairs[p][4])
            parts[acc_of[p]] = d if parts[acc_of[p]] is None else parts[acc_of[p]] + d

        def finish(vals):
            for o, r in zip(out, epilogue(vals, [e[...] for e in ex])):
                o[...] = r.astype(o.dtype)

        if nk == 1:
            finish(parts)
            return
        k = pl.program_id(2)

        @pl.when(k == 0)
        def _():
            for a, d in zip(accs, parts):
                a[...] = d

        @pl.when((k > 0) & (k < nk - 1))
        def _():
            for a, d in zip(accs, parts):
                a[...] += d

        @pl.when(k == nk - 1)
        def _():
            finish([a[...] + d for a, d in zip(accs, parts)])

    in_specs, args = [], []
    for a, a_spec, b, b_spec, _ in pairs:
        if a is not None:
            in_specs.append(a_spec)
            args.append(a)
        in_specs.append(b_spec)
        args.append(b)
    for e, e_spec in extras:
        in_specs.append(e_spec)
        args.append(e)
    if after is not None:
        in_specs.append(pl.BlockSpec(memory_space=pl.ANY))
        args.append(after)
    res = pl.pallas_call(
        body, name=name, grid=grid, in_specs=in_specs,
        out_specs=[s for _, s in outs], out_shape=[o for o, _ in outs],
        scratch_shapes=[pltpu.VMEM(s, F32) for s in acc_shapes] if nk > 1 else [],
        compiler_params=_params(("parallel", "parallel", "arbitrary")),
    )(*args)
    return res


def _mm2d(name, a, b, mode, out_dtype, tm=512, tn=1024, tk=1024, extras=(), epilogue=None, n_out=1, after=None, n_outer=False):
    if mode == "nn":
        (M, K), N = a.shape, b.shape[1]
    elif mode == "nt":
        (M, K), N = a.shape, b.shape[0]
    else:
        (K, M), N = a.shape, b.shape[1]
    tm, tn, tk = _tile(M, tm), _tile(N, tn), _tile(K, tk)

    def spec(shape, index):
        if n_outer:
            return pl.BlockSpec(shape, lambda j, i, k: index(i, j, k))
        return pl.BlockSpec(shape, index)

    a_spec = spec((tk, tm), lambda i, j, k: (k, i)) if mode == "tn" else spec((tm, tk), lambda i, j, k: (i, k))
    b_spec = spec((tn, tk), lambda i, j, k: (j, k)) if mode == "nt" else spec((tk, tn), lambda i, j, k: (k, j))
    mn = spec((tm, tn), lambda i, j, k: (i, j))
    if epilogue is None:
        epilogue = lambda accs, ex: [accs[0]]
    if not isinstance(out_dtype, (list, tuple)):
        out_dtype = [out_dtype] * n_out
    grid = (N // tn, M // tm, K // tk) if n_outer else (M // tm, N // tn, K // tk)
    res = _mm(name, grid, [(a, a_spec, b, b_spec, mode)], [0], [(tm, tn)],
              [(e, mn) for e in extras], [(jax.ShapeDtypeStruct((M, N), d), mn) for d in out_dtype], epilogue, after=after)
    return res[0] if len(res) == 1 else res


def _sigmoid(x):
    return 1.0 / (1.0 + jnp.exp(-x))


def _sigmoid_fast(x):
    return pl.reciprocal(1.0 + jnp.exp(-x), approx=True)


def _ffn_gu(name, h, wg, wu):
    T, D = h.shape
    nf, _, F4 = wg.shape
    tm, tk = _tile(T, 512), _tile(D, 2048)
    a_spec = pl.BlockSpec((tm, tk), lambda j, i, k: (i, k))
    b_spec = pl.BlockSpec((None, tk, F4), lambda j, i, k: (j, k, 0))
    o_spec = pl.BlockSpec((tm, F4), lambda j, i, k: (i, j))

    def epilogue(accs, ex):
        g, u = accs
        return [g, u, g * _sigmoid_fast(g) * u]

    sds = jax.ShapeDtypeStruct((T, nf * F4), BF16)
    return _mm(name, (nf, T // tm, D // tk), [(h, a_spec, wg, b_spec, "nn"), (None, None, wu, b_spec, "nn")], [0, 1],
               [(tm, F4), (tm, F4)], [], [(sds, o_spec)] * 3, epilogue)


def _ffn_down(name, a, wd, xres):
    return _mm2d(name, a, wd, "nn", F32, tm=1024, tk=wd.shape[0] // N_CHIPS, extras=[xres],
                 epilogue=lambda accs, ex: [ex[0] + 0.5 * accs[0]])


def _ffn_bwd_down(tag, a, dyb, nf, after=None):
    return _mm2d(tag + "_dwd", a, dyb, "tn", F32, tm=a.shape[1] // nf, tk=1024, after=after)


def _ffn_bwd_act(tag, dyb, wd, g, u, nf, after=None):
    def act_bwd(accs, ex):
        gf, uf = ex[0].astype(F32), ex[1].astype(F32)
        s = _sigmoid_fast(gf)
        return [accs[0] * uf * s * (1.0 + gf * (1.0 - s)), accs[0] * gf * s]

    return _mm2d(tag + "_da", dyb, wd, "nt", BF16, tn=wd.shape[0] // nf, tk=2048, extras=[g, u], epilogue=act_bwd, n_out=2,
                 n_outer=True, after=after)


def _ffn_bwd_gate_up(tag, h, dg, du, nf, after=None):
    T, D = h.shape
    F4 = dg.shape[1] // nf
    tm, tk = _tile(D, 512), _tile(T, 1024)
    h_spec = pl.BlockSpec((tk, tm), lambda i, j, k: (k, i))
    d_spec = pl.BlockSpec((tk, F4), lambda i, j, k: (k, j))
    w_spec = pl.BlockSpec((None, tm, F4), lambda i, j, k: (j, i, 0))
    sds = jax.ShapeDtypeStruct((nf, D, F4), F32)
    return _mm(tag + "_dwgu", (D // tm, nf, T // tk), [(h, h_spec, dg, d_spec, "tn"), (None, None, du, d_spec, "tn")],
               [0, 1], [(tm, F4), (tm, F4)], [], [(sds, w_spec)] * 2, lambda accs, ex: accs, after=after)


def _ffn_bwd_x(tag, dg, du, wg, wu, after):
    T = dg.shape[0]
    nf, D, F4 = wg.shape
    tm, tn = _tile(T, 1024), _tile(D, 1024)
    a_spec = pl.BlockSpec((tm, F4), lambda i, j, k: (i, k))
    b_spec = pl.BlockSpec((None, tn, F4), lambda i, j, k: (k, j, 0))
    o_spec = pl.BlockSpec((tm, tn), lambda i, j, k: (i, j))
    (dh,) = _mm(tag + "_dh", (T // tm, D // tn, nf), [(dg, a_spec, wg, b_spec, "nt"), (du, a_spec, wu, b_spec, "nt")],
                [0, 0], [(tm, tn)], [], [(jax.ShapeDtypeStruct((T, D), F32), o_spec)], lambda accs, ex: accs, after=after)
    return dh


def _rms_fwd(name, x, gain):
    R, D = x.shape
    tr = _tile(R, 256, 8)

    def body(x_ref, g_ref, o_ref):
        xv = x_ref[...]
        r = lax.rsqrt(jnp.mean(xv * xv, axis=-1, keepdims=True) + EPS)
        o_ref[...] = (xv * r * g_ref[...]).astype(BF16)

    return pl.pallas_call(
        body, name=name, grid=(R // tr,),
        in_specs=[pl.BlockSpec((tr, D), lambda i: (i, 0)), pl.BlockSpec((1, D), lambda i: (0, 0))],
        out_specs=pl.BlockSpec((tr, D), lambda i: (i, 0)), out_shape=jax.ShapeDtypeStruct((R, D), BF16),
        compiler_params=_params(("parallel",)),
    )(x, gain)


def _rms_bwd(name, dh, x, gain, dres, bscale):
    R, D = x.shape
    tr = _tile(R, 256, 8)

    def body(dh_ref, x_ref, g_ref, dres_ref, dx_ref, dxb_ref, dg_ref):
        xv, dy = x_ref[...], dh_ref[...]
        r = lax.rsqrt(jnp.mean(xv * xv, axis=-1, keepdims=True) + EPS)
        xn = xv * r
        uu = dy * g_ref[...]
        dx = dres_ref[...] + r * (uu - xn * jnp.mean(xn * uu, axis=-1, keepdims=True))
        dx_ref[...] = dx
        dxb_ref[...] = (bscale * dx).astype(BF16)
        part = jnp.sum(dy * xn, axis=0, keepdims=True)

        @pl.when(pl.program_id(0) == 0)
        def _():
            dg_ref[...] = part

        @pl.when(pl.program_id(0) > 0)
        def _():
            dg_ref[...] += part

    row = pl.BlockSpec((tr, D), lambda i: (i, 0))
    vec = pl.BlockSpec((1, D), lambda i: (0, 0))
    return pl.pallas_call(
        body, name=name, grid=(R // tr,), in_specs=[row, row, vec, row], out_specs=[row, row, vec],
        out_shape=[jax.ShapeDtypeStruct((R, D), F32), jax.ShapeDtypeStruct((R, D), BF16), jax.ShapeDtypeStruct((1, D), F32)],
        compiler_params=_params(("arbitrary",)),
    )(dh, x, gain, dres)


def _loss_head(y, target):
    R, D = y.shape
    tr = _tile(R, 256, 8)

    def body(y_ref, t_ref, d_ref, db_ref, l_ref):
        e = y_ref[...] - t_ref[...]
        d = e * (1.0 / D)
        d_ref[...] = d
        db_ref[...] = (0.5 * d).astype(BF16)
        part = jnp.zeros((8, 128), F32) + (0.5 / D) * jnp.sum(e * e)

        @pl.when(pl.program_id(0) == 0)
        def _():
            l_ref[...] = part

        @pl.when(pl.program_id(0) > 0)
        def _():
            l_ref[...] += part

    row = pl.BlockSpec((tr, D), lambda i: (i, 0))
    acc = pl.BlockSpec((8, 128), lambda i: (0, 0))
    return pl.pallas_call(
        body, name="loss_head", grid=(R // tr,), in_specs=[row, row], out_specs=[row, row, acc],
        out_shape=[jax.ShapeDtypeStruct((R, D), F32), jax.ShapeDtypeStruct((R, D), BF16), jax.ShapeDtypeStruct((8, 128), F32)],
        compiler_params=_params(("arbitrary",)),
    )(y, target)


def _head_norm(xs, g):
    r = lax.rsqrt(jnp.mean(xs * xs, axis=-1, keepdims=True) + EPS)
    return xs * r * g


def _head_norm_bwd(xs, g, dy):
    r = lax.rsqrt(jnp.mean(xs * xs, axis=-1, keepdims=True) + EPS)
    xn = xs * r
    uu = dy * g
    return r * (uu - xn * jnp.mean(xn * uu, axis=-1, keepdims=True)), jnp.sum(dy * xn, axis=0, keepdims=True)


NORMED = [(C_FQ, FOX_HEADS, 0), (C_FK, FOX_HEADS, 1), (C_SQ, SWA_HEADS, 2), (C_SK, SWA_KV_HEADS, 3), (C_MQ, MEM_HEADS, 4)]
PLAIN = [(C_FV, FOX_HEADS), (C_SV, SWA_KV_HEADS)]


def _prep_fwd(proj, gains):
    T = proj.shape[0]
    tr = _tile(T, 256, 8)

    def body(p_ref, g_ref, o_ref, lf_ref):
        for start, heads, row in NORMED:
            gn = g_ref[row:row + 1, :]
            for hh in range(heads):
                sl = slice(start + hh * HEAD_DIM, start + (hh + 1) * HEAD_DIM)
                o_ref[:, sl] = _head_norm(p_ref[:, sl], gn).astype(BF16)
        for start, heads in PLAIN:
            sl = slice(start, start + heads * HEAD_DIM)
            o_ref[:, sl] = p_ref[:, sl].astype(BF16)
        zb = p_ref[:, C_FL:C_FL + HEAD_DIM] + g_ref[5:6, :]
        o_ref[:, C_FL:C_FL + HEAD_DIM] = jnp.zeros((tr, HEAD_DIM), BF16)
        lf_ref[...] = jnp.minimum(zb, 0.0) - jnp.log(1.0 + jnp.exp(-jnp.abs(zb)))

    return pl.pallas_call(
        body, name="prep_fwd", grid=(T // tr,),
        in_specs=[pl.BlockSpec((tr, PROJ_W), lambda i: (i, 0)), pl.BlockSpec((8, 128), lambda i: (0, 0))],
        out_specs=[pl.BlockSpec((tr, PROJ_W), lambda i: (i, 0)), pl.BlockSpec((tr, HEAD_DIM), lambda i: (i, 0))],
        out_shape=[jax.ShapeDtypeStruct((T, PROJ_W), BF16), jax.ShapeDtypeStruct((T, HEAD_DIM), F32)],
        compiler_params=_params(("parallel",)),
    )(proj, gains)


def _prep_bwd(proj, gains, dfq, dfk, dfv, dsq, dsk, dsv, dmq, dlogf):
    T = proj.shape[0]
    tr = _tile(T, 256, 8)
    d_normed = {C_FQ: 0, C_FK: 1, C_SQ: 3, C_SK: 4, C_MQ: 6}
    d_plain = {C_FV: 2, C_SV: 5}

    def body(p_ref, g_ref, *rest):
        d_refs, dlf_ref, o_ref, dg_ref = rest[:7], rest[7], rest[8], rest[9]
        rows = []
        for start, heads, row in NORMED:
            gn = g_ref[row:row + 1, :]
            d_ref = d_refs[d_normed[start]]
            tot = jnp.zeros((1, HEAD_DIM), F32)
            for hh in range(heads):
                sl = slice(start + hh * HEAD_DIM, start + (hh + 1) * HEAD_DIM)
                dx, dgn = _head_norm_bwd(p_ref[:, sl], gn, d_ref[:, hh * HEAD_DIM:(hh + 1) * HEAD_DIM])
                o_ref[:, sl] = dx.astype(BF16)
                tot = tot + dgn
            rows.append(tot)
        for start, heads in PLAIN:
            o_ref[:, start:start + heads * HEAD_DIM] = d_refs[d_plain[start]][...].astype(BF16)
        zb = p_ref[:, C_FL:C_FL + HEAD_DIM] + g_ref[5:6, :]
        lane = lax.broadcasted_iota(jnp.int32, (tr, HEAD_DIM), 1)
        dz = jnp.where(lane < FOX_HEADS, dlf_ref[...] * (1.0 - _sigmoid(zb)), 0.0)
        o_ref[:, C_FL:C_FL + HEAD_DIM] = dz.astype(BF16)
        rows.append(jnp.sum(dz, axis=0, keepdims=True))
        part = jnp.concatenate(rows + [jnp.zeros((2, HEAD_DIM), F32)], axis=0)

        @pl.when(pl.program_id(0) == 0)
        def _():
            dg_ref[...] = part

        @pl.when(pl.program_id(0) > 0)
        def _():
            dg_ref[...] += part

    def rows_of(w):
        return pl.BlockSpec((tr, w), lambda i: (i, 0))

    small = pl.BlockSpec((8, 128), lambda i: (0, 0))
    ds = [dfq, dfk, dfv, dsq, dsk, dsv, dmq]
    return pl.pallas_call(
        body, name="prep_bwd", grid=(T // tr,),
        in_specs=[rows_of(PROJ_W), small] + [rows_of(d.shape[1]) for d in ds] + [rows_of(HEAD_DIM)],
        out_specs=[rows_of(PROJ_W), small],
        out_shape=[jax.ShapeDtypeStruct((T, PROJ_W), BF16), jax.ShapeDtypeStruct((8, 128), F32)],
        compiler_params=_params(("arbitrary",)),
    )(proj, gains, *ds, dlogf)


def _head_norm_rows(x, gain):
    R, W = x.shape

    def body(x_ref, g_ref, o_ref):
        for hh in range(W // HEAD_DIM):
            sl = slice(hh * HEAD_DIM, (hh + 1) * HEAD_DIM)
            o_ref[:, sl] = _head_norm(x_ref[:, sl], g_ref[...]).astype(BF16)

    return pl.pallas_call(body, name="mem_k_norm", out_shape=jax.ShapeDtypeStruct((R, W), BF16))(x, gain)


def _head_norm_rows_bwd(x, gain, dy):
    R, W = x.shape

    def body(x_ref, g_ref, dy_ref, dx_ref, dg_ref):
        tot = jnp.zeros((1, HEAD_DIM), F32)
        for hh in range(W // HEAD_DIM):
            sl = slice(hh * HEAD_DIM, (hh + 1) * HEAD_DIM)
            dx, dgn = _head_norm_bwd(x_ref[:, sl], g_ref[...], dy_ref[:, sl])
            dx_ref[:, sl] = dx.astype(BF16)
            tot = tot + dgn
        dg_ref[...] = tot

    return pl.pallas_call(
        body, name="mem_k_norm_bwd",
        out_shape=[jax.ShapeDtypeStruct((R, W), BF16), jax.ShapeDtypeStruct((1, HEAD_DIM), F32)])(x, gain, dy)


def _cumsum_rows(name, xs, reverse):
    T, W = xs[0].shape
    tb = _tile(T, 512, 8)
    nb = T // tb

    def body(*refs):
        o_ref, carry = refs[len(xs)], refs[len(xs) + 1]

        @pl.when(pl.program_id(0) == 0)
        def _():
            carry[...] = jnp.zeros_like(carry)

        xv = refs[0][...]
        for x_ref in refs[1:len(xs)]:
            xv = xv + x_ref[...]
        r = lax.broadcasted_iota(jnp.int32, (tb, tb), 0)
        cc = lax.broadcasted_iota(jnp.int32, (tb, tb), 1)
        tri = jnp.where((cc >= r) if reverse else (cc <= r), 1.0, 0.0).astype(F32)
        o_ref[...] = jnp.dot(tri, xv, precision=lax.Precision.HIGHEST, preferred_element_type=F32) + carry[...]
        carry[...] += jnp.sum(xv, axis=0, keepdims=True)

    idx = (lambda i: (nb - 1 - i, 0)) if reverse else (lambda i: (i, 0))
    return pl.pallas_call(
        body, name=name, grid=(nb,), in_specs=[pl.BlockSpec((tb, W), idx)] * len(xs), out_specs=pl.BlockSpec((tb, W), idx),
        out_shape=jax.ShapeDtypeStruct((T, W), F32), scratch_shapes=[pltpu.VMEM((1, W), F32)],
        compiler_params=_params(("arbitrary",)),
    )(*xs)


def _triangle(nq, by_column):
    if by_column:
        blocks = [(i, j) for j in range(nq) for i in range(j, nq)]
    else:
        blocks = [(i, j) for i in range(nq) for j in range(i + 1)]
    return jnp.asarray(np.array(blocks, np.int32).T)


def _fox_scores_t(k, q, cq_row, ck_rep, on_diagonal):
    n = q.shape[0]
    s = _dot(k, q, "nt") * SCALE + (cq_row - jnp.tile(ck_rep, (1, n // HEAD_DIM)))
    if on_diagonal:
        s = jnp.where(lax.broadcasted_iota(jnp.int32, (n, n), 0) <= lax.broadcasted_iota(jnp.int32, (n, n), 1), s, NEG_INF)
    return s


def _fox_fwd(qkv, v_t, cq_row, ck_rep):
    T = qkv.shape[0]
    tq = _tile(T, 512)
    nq = T // tq
    steps = nq * (nq + 1) // 2
    HQ, HK = C_FQ // HEAD_DIM, C_FK // HEAD_DIM

    def body(tab, q_ref, k_ref, vt_ref, cq_ref, ck_ref, ot_ref, lse_ref, m_sc, l_sc, acc_sc):
        i, j = tab[0, pl.program_id(1)], tab[1, pl.program_id(1)]

        @pl.when(j == 0)
        def _():
            m_sc[...] = jnp.full_like(m_sc, NEG_INF)
            l_sc[...] = jnp.zeros_like(l_sc)
            acc_sc[...] = jnp.zeros_like(acc_sc)

        def step(on_diagonal):
            s = _fox_scores_t(k_ref[...], q_ref[...], cq_ref[...], ck_ref[...], on_diagonal)
            m_new = jnp.maximum(m_sc[...], jnp.max(s, axis=0, keepdims=True))
            alpha = jnp.exp(m_sc[...] - m_new)
            p = jnp.exp(s - m_new)
            l_sc[...] = alpha * l_sc[...] + jnp.sum(p, axis=0, keepdims=True)
            acc_sc[...] = alpha * acc_sc[...] + _dot(vt_ref[...], p.astype(BF16), "nn")
            m_sc[...] = m_new

        @pl.when(j < i)
        def _():
            step(False)

        @pl.when(j == i)
        def _():
            step(True)
            ot_ref[...] = acc_sc[...] / l_sc[...]
            lse_ref[...] = m_sc[...] + jnp.log(l_sc[...])

    qrow = pl.BlockSpec((None, 1, tq), lambda h, s, tab: (h, 0, tab[0, s]))
    return pl.pallas_call(
        body, name="fox_fwd",
        grid_spec=pltpu.PrefetchScalarGridSpec(
            num_scalar_prefetch=1, grid=(FOX_HEADS, steps),
            in_specs=[pl.BlockSpec((tq, HEAD_DIM), lambda h, s, tab: (tab[0, s], HQ + h)),
                      pl.BlockSpec((tq, HEAD_DIM), lambda h, s, tab: (tab[1, s], HK + h)),
                      pl.BlockSpec((HEAD_DIM, tq), lambda h, s, tab: (h, tab[1, s])), qrow,
                      pl.BlockSpec((None, tq, HEAD_DIM), lambda h, s, tab: (h, tab[1, s], 0))],
            out_specs=[pl.BlockSpec((HEAD_DIM, tq), lambda h, s, tab: (h, tab[0, s])), qrow],
            scratch_shapes=[pltpu.VMEM((1, tq), F32), pltpu.VMEM((1, tq), F32), pltpu.VMEM((HEAD_DIM, tq), F32)]),
        out_shape=[jax.ShapeDtypeStruct((FOX_W, T), F32), jax.ShapeDtypeStruct((FOX_HEADS, 1, T), F32)],
        compiler_params=_params(("parallel", "arbitrary")),
    )(_triangle(nq, False), qkv, qkv, v_t, cq_row, ck_rep)


def _fox_bwd(qkv, k_t, cq_row, ck_rep, out_t, lse, dmix, dout_t):
    T = qkv.shape[0]
    tq = _tile(T, 512)
    nq = T // tq
    steps = nq * (nq + 1) // 2
    HQ, HK, HV = C_FQ // HEAD_DIM, C_FK // HEAD_DIM, C_FV // HEAD_DIM

    def body(tab, q_ref, k_ref, kt_ref, v_ref, cq_ref, ck_ref, ot_ref, lse_ref, do_ref, dot_ref,
             dqt_ref, dk_ref, dv_ref, dck_ref, dcq_ref, dk_sc, dv_sc, dc_sc):
        qi, kj = tab[0, pl.program_id(1)], tab[1, pl.program_id(1)]

        @pl.when(qi == kj)
        def _():
            dk_sc[...] = jnp.zeros_like(dk_sc)
            dv_sc[...] = jnp.zeros_like(dv_sc)
            dc_sc[...] = jnp.zeros_like(dc_sc)

        def step(on_diagonal):
            q, k, v, do = q_ref[...], k_ref[...], v_ref[...], do_ref[...]
            p = jnp.exp(_fox_scores_t(k, q, cq_ref[...], ck_ref[...], on_diagonal) - lse_ref[...])
            dp = _dot(v, do, "nt")
            delta = jnp.sum(dot_ref[...].astype(F32) * ot_ref[...], axis=0, keepdims=True)
            ds = p * (dp - delta)
            dsb = ds.astype(BF16)
            dv_sc[...] += _dot(p.astype(BF16), do, "nn")
            dk_sc[...] += _dot(dsb, q, "nn")
            dc_sc[...] += jnp.sum(ds, axis=1, keepdims=True)
            dq_part = _dot(kt_ref[...], dsb, "nn") * SCALE
            dcq_part = jnp.sum(ds, axis=0, keepdims=True)

            @pl.when(kj == 0)
            def _():
                dqt_ref[qi] = dq_part
                dcq_ref[qi] = dcq_part

            @pl.when(kj > 0)
            def _():
                dqt_ref[qi] += dq_part
                dcq_ref[qi] += dcq_part

        @pl.when(qi > kj)
        def _():
            step(False)

        @pl.when(qi == kj)
        def _():
            step(True)

        @pl.when(qi == nq - 1)
        def _():
            dk_ref[...] = dk_sc[...] * SCALE
            dv_ref[...] = dv_sc[...]
            dck_ref[...] = -dc_sc[...]

    def rows(base):
        return pl.BlockSpec((tq, HEAD_DIM), lambda h, s, tab: (tab[0, s], base + h))

    def cols(base):
        return pl.BlockSpec((tq, HEAD_DIM), lambda h, s, tab: (tab[1, s], base + h))

    def rows_t(h, s, tab):
        return (h, tab[0, s])

    qrow = pl.BlockSpec((None, 1, tq), lambda h, s, tab: (h, 0, tab[0, s]))
    sds = jax.ShapeDtypeStruct((T, FOX_W), F32)
    return pl.pallas_call(
        body, name="fox_bwd",
        grid_spec=pltpu.PrefetchScalarGridSpec(
            num_scalar_prefetch=1, grid=(FOX_HEADS, steps),
            in_specs=[rows(HQ), cols(HK), pl.BlockSpec((HEAD_DIM, tq), lambda h, s, tab: (h, tab[1, s])), cols(HV), qrow,
                      pl.BlockSpec((None, tq, HEAD_DIM), lambda h, s, tab: (h, tab[1, s], 0)),
                      pl.BlockSpec((HEAD_DIM, tq), rows_t), qrow, rows(0), pl.BlockSpec((HEAD_DIM, tq), rows_t)],
            out_specs=[pl.BlockSpec((None, nq, HEAD_DIM, tq), lambda h, s, tab: (h, 0, 0, 0)), cols(0), cols(0),
                       pl.BlockSpec((None, tq, 1), lambda h, s, tab: (h, tab[1, s], 0)),
                       pl.BlockSpec((None, nq, 1, tq), lambda h, s, tab: (h, 0, 0, 0))],
            scratch_shapes=[pltpu.VMEM((tq, HEAD_DIM), F32), pltpu.VMEM((tq, HEAD_DIM), F32), pltpu.VMEM((tq, 1), F32)]),
        out_shape=[jax.ShapeDtypeStruct((FOX_HEADS, nq, HEAD_DIM, tq), F32), sds, sds,
                   jax.ShapeDtypeStruct((FOX_HEADS, T, 1), F32), jax.ShapeDtypeStruct((FOX_HEADS, nq, 1, tq), F32)],
        compiler_params=_params(("parallel", "arbitrary")),
    )(_triangle(nq, True), qkv, qkv, k_t, qkv, cq_row, ck_rep, out_t, lse, dmix, dout_t)


GW = SWA_GROUP * HEAD_DIM
GR = SWA_GROUP * WINDOW


def _swa_scores(q_ref, kp_ref, kc_ref, slope_ref, n):
    q = q_ref[...]
    qs = jnp.concatenate([q[:, t * HEAD_DIM:(t + 1) * HEAD_DIM] for t in range(SWA_GROUP)], axis=0)
    kb = jnp.concatenate([kp_ref[...], kc_ref[...]], axis=0)
    r = lax.broadcasted_iota(jnp.int32, (GR, 2 * WINDOW), 0) & (WINDOW - 1)
    jj = lax.broadcasted_iota(jnp.int32, (GR, 2 * WINDOW), 1)
    dist = WINDOW + r - jj
    valid = (dist >= 0) & (dist < WINDOW) & ((n > 0) | (jj >= WINDOW))
    s = _dot(qs, kb, "nt") * SCALE - slope_ref[...] * dist.astype(F32)
    return qs, kb, jnp.where(valid, s, NEG_INF), valid


def _swa_specs():
    HQ, HK, HV = C_SQ // GW, C_SK // HEAD_DIM, C_SV // HEAD_DIM
    q_spec = pl.BlockSpec((WINDOW, GW), lambda g, n: (n, HQ + g))

    def prev(base):
        return pl.BlockSpec((WINDOW, HEAD_DIM), lambda g, n: (jnp.maximum(n - 1, 0), base + g))

    def cur(base):
        return pl.BlockSpec((WINDOW, HEAD_DIM), lambda g, n: (n, base + g))

    col = pl.BlockSpec((None, GR, 1), lambda g, n: (g, 0, 0))
    return q_spec, prev(HK), cur(HK), prev(HV), cur(HV), col


def _swa_fwd(qkv, slopes, sinks):
    T = qkv.shape[0]
    nb = T // WINDOW
    assert C_SQ % GW == 0

    def body(q_ref, kp_ref, kc_ref, vp_ref, vc_ref, slope_ref, sink_ref, o_ref, lse_ref):
        n = pl.program_id(1)
        _, _, s, _ = _swa_scores(q_ref, kp_ref, kc_ref, slope_ref, n)
        m = jnp.maximum(jnp.max(s, axis=-1, keepdims=True), sink_ref[...])
        p = jnp.exp(s - m)
        l = jnp.sum(p, axis=-1, keepdims=True) + jnp.exp(sink_ref[...] - m)
        vb = jnp.concatenate([vp_ref[...], vc_ref[...]], axis=0)
        o = _dot(p.astype(BF16), vb, "nn") / l
        for t in range(SWA_GROUP):
            o_ref[:, t * HEAD_DIM:(t + 1) * HEAD_DIM] = o[t * WINDOW:(t + 1) * WINDOW, :].astype(BF16)
        lse_ref[...] = m + jnp.log(l)

    q_spec, kp, kc, vp, vc, col = _swa_specs()
    return pl.pallas_call(
        body, name="swa_fwd", grid=(SWA_KV_HEADS, nb), in_specs=[q_spec, kp, kc, vp, vc, col, col],
        out_specs=[pl.BlockSpec((WINDOW, GW), lambda g, n: (n, g)), pl.BlockSpec((None, None, GR, 1), lambda g, n: (g, n, 0, 0))],
        out_shape=[jax.ShapeDtypeStruct((T, SWA_HEADS * HEAD_DIM), BF16), jax.ShapeDtypeStruct((SWA_KV_HEADS, nb, GR, 1), F32)],
        compiler_params=_params(("parallel", "arbitrary")),
    )(qkv, qkv, qkv, qkv, qkv, slopes, sinks)


def _swa_bwd(qkv, slopes, sinks, out, lse, dmix):
    T = qkv.shape[0]
    nb = T // WINDOW
    DO = FOX_W // GW
    assert FOX_W % GW == 0

    def body(q_ref, kp_ref, kc_ref, vp_ref, vc_ref, slope_ref, sink_ref, o_ref, lse_ref, do_ref,
             dq_ref, dk_ref, dv_ref, dsink_ref, sink_sc):
        n = pl.program_id(1)

        @pl.when(n == 0)
        def _():
            dk_ref[...] = jnp.zeros_like(dk_ref)
            dv_ref[...] = jnp.zeros_like(dv_ref)
            sink_sc[...] = jnp.zeros_like(sink_sc)

        qs, kb, s, valid = _swa_scores(q_ref, kp_ref, kc_ref, slope_ref, n)
        lse = lse_ref[...]
        p = jnp.where(valid, jnp.exp(s - lse), 0.0)
        vb = jnp.concatenate([vp_ref[...], vc_ref[...]], axis=0)
        do = jnp.concatenate([do_ref[:, t * HEAD_DIM:(t + 1) * HEAD_DIM] for t in range(SWA_GROUP)], axis=0)
        oo = jnp.concatenate([o_ref[:, t * HEAD_DIM:(t + 1) * HEAD_DIM] for t in range(SWA_GROUP)], axis=0)
        dp = _dot(do, vb, "nt")
        delta = jnp.sum(do.astype(F32) * oo.astype(F32), axis=-1, keepdims=True)
        ds = p * (dp - delta)
        dsb = ds.astype(BF16)
        dq = _dot(dsb, kb, "nn") * SCALE
        for t in range(SWA_GROUP):
            dq_ref[:, t * HEAD_DIM:(t + 1) * HEAD_DIM] = dq[t * WINDOW:(t + 1) * WINDOW, :]
        dkb = _dot(dsb, qs, "tn") * SCALE
        dvb = _dot(p.astype(BF16), do, "tn")
        r_prev = pl.ds(pl.multiple_of(jnp.maximum(n - 1, 0) * WINDOW, WINDOW), WINDOW)
        r_cur = pl.ds(pl.multiple_of(n * WINDOW, WINDOW), WINDOW)
        dk_ref[r_prev, :] += dkb[:WINDOW, :]
        dk_ref[r_cur, :] += dkb[WINDOW:, :]
        dv_ref[r_prev, :] += dvb[:WINDOW, :]
        dv_ref[r_cur, :] += dvb[WINDOW:, :]
        sink_sc[...] -= jnp.exp(sink_ref[...] - lse) * delta

        @pl.when(n == nb - 1)
        def _():
            tot = [jnp.zeros((1, 128), F32) + jnp.sum(sink_sc[t * WINDOW:(t + 1) * WINDOW, :]) for t in range(SWA_GROUP)]
            dsink_ref[...] = jnp.concatenate(tot + [jnp.zeros((8 - SWA_GROUP, 128), F32)], axis=0)

    q_spec, kp, kc, vp, vc, col = _swa_specs()
    kv_acc = pl.BlockSpec((T, HEAD_DIM), lambda g, n: (0, g))
    return pl.pallas_call(
        body, name="swa_bwd", grid=(SWA_KV_HEADS, nb),
        in_specs=[q_spec, kp, kc, vp, vc, col, col, pl.BlockSpec((WINDOW, GW), lambda g, n: (n, g)),
                  pl.BlockSpec((None, None, GR, 1), lambda g, n: (g, n, 0, 0)), pl.BlockSpec((WINDOW, GW), lambda g, n: (n, DO + g))],
        out_specs=[pl.BlockSpec((WINDOW, GW), lambda g, n: (n, g)), kv_acc, kv_acc, pl.BlockSpec((None, 8, 128), lambda g, n: (g, 0, 0))],
        out_shape=[jax.ShapeDtypeStruct((T, SWA_HEADS * HEAD_DIM), F32), jax.ShapeDtypeStruct((T, SWA_KV_HEADS * HEAD_DIM), F32),
                   jax.ShapeDtypeStruct((T, SWA_KV_HEADS * HEAD_DIM), F32), jax.ShapeDtypeStruct((SWA_KV_HEADS, 8, 128), F32)],
        scratch_shapes=[pltpu.VMEM((GR, 1), F32)],
        compiler_params=_params(("parallel", "arbitrary")),
    )(qkv, qkv, qkv, qkv, qkv, slopes, sinks, out, lse, dmix)


def _mem_fwd(qkv, mk, mv):
    T, ML = qkv.shape[0], mk.shape[0]
    tq = _tile(T, 512)
    HQ = C_MQ // HEAD_DIM

    def body(q_ref, k_ref, v_ref, o_ref, lse_ref):
        s = _dot(q_ref[...], k_ref[...], "nt") * SCALE
        m = jnp.max(s, axis=-1, keepdims=True)
        p = jnp.exp(s - m)
        l = jnp.sum(p, axis=-1, keepdims=True)
        o_ref[...] = (_dot(p.astype(BF16), v_ref[...], "nn") / l).astype(BF16)
        lse_ref[...] = m + jnp.log(l)

    kv = pl.BlockSpec((ML, HEAD_DIM), lambda h, i: (0, h))
    return pl.pallas_call(
        body, name="mem_fwd", grid=(MEM_HEADS, T // tq),
        in_specs=[pl.BlockSpec((tq, HEAD_DIM), lambda h, i: (i, HQ + h)), kv, kv],
        out_specs=[pl.BlockSpec((tq, HEAD_DIM), lambda h, i: (i, h)), pl.BlockSpec((None, tq, 1), lambda h, i: (h, i, 0))],
        out_shape=[jax.ShapeDtypeStruct((T, MEM_HEADS * HEAD_DIM), BF16), jax.ShapeDtypeStruct((MEM_HEADS, T, 1), F32)],
        compiler_params=_params(("parallel", "arbitrary")),
    )(qkv, mk, mv)


def _mem_bwd(qkv, mk, mv, out, lse, dmix):
    T, ML = qkv.shape[0], mk.shape[0]
    tq = _tile(T, 512)
    HQ = C_MQ // HEAD_DIM
    DO = (FOX_W + SWA_HEADS * HEAD_DIM) // HEAD_DIM

    def body(q_ref, k_ref, v_ref, o_ref, lse_ref, do_ref, dq_ref, dk_ref, dv_ref):
        q, k, v, do = q_ref[...], k_ref[...], v_ref[...], do_ref[...]
        p = jnp.exp(_dot(q, k, "nt") * SCALE - lse_ref[...])
        dp = _dot(do, v, "nt")
        delta = jnp.sum(do.astype(F32) * o_ref[...].astype(F32), axis=-1, keepdims=True)
        dsb = (p * (dp - delta)).astype(BF16)
        dq_ref[...] = _dot(dsb, k, "nn") * SCALE
        dk_part = _dot(dsb, q, "tn") * SCALE
        dv_part = _dot(p.astype(BF16), do, "tn")

        @pl.when(pl.program_id(1) == 0)
        def _():
            dk_ref[...] = dk_part
            dv_ref[...] = dv_part

        @pl.when(pl.program_id(1) > 0)
        def _():
            dk_ref[...] += dk_part
            dv_ref[...] += dv_part

    kv = pl.BlockSpec((ML, HEAD_DIM), lambda h, i: (0, h))
    qb = pl.BlockSpec((tq, HEAD_DIM), lambda h, i: (i, h))
    return pl.pallas_call(
        body, name="mem_bwd", grid=(MEM_HEADS, T // tq),
        in_specs=[pl.BlockSpec((tq, HEAD_DIM), lambda h, i: (i, HQ + h)), kv, kv, qb,
                  pl.BlockSpec((None, tq, 1), lambda h, i: (h, i, 0)), pl.BlockSpec((tq, HEAD_DIM), lambda h, i: (i, DO + h))],
        out_specs=[qb, kv, kv],
        out_shape=[jax.ShapeDtypeStruct((T, MEM_HEADS * HEAD_DIM), F32), jax.ShapeDtypeStruct((ML, MEM_HEADS * HEAD_DIM), F32),
                   jax.ShapeDtypeStruct((ML, MEM_HEADS * HEAD_DIM), F32)],
        compiler_params=_params(("parallel", "arbitrary")),
    )(qkv, mk, mv, out, lse, dmix)


HBM = pl.BlockSpec(memory_space=pltpu.HBM)


def _place():
    x, y, c = lax.axis_index("x"), lax.axis_index("y"), lax.axis_index("c")
    chips = [(1 - x, y), (x, 1 - y), (1 - x, 1 - y)]
    return x, y, c, chips


def _remote(src, dst, send_sem, recv_sem, device):
    return pltpu.make_async_remote_copy(src_ref=src, dst_ref=dst, send_sem=send_sem, recv_sem=recv_sem,
                                        device_id=device, device_id_type=MESH)


def _place_ids():
    x, y, c = lax.axis_index("x"), lax.axis_index("y"), lax.axis_index("c")
    order = [2 * x + y, 2 * (1 - x) + y, 2 * x + (1 - y), 2 * (1 - x) + (1 - y)]
    return jnp.stack([2 * x + y, c] + order).astype(jnp.int32)


def _cast_place(name, w, ids):
    R, C = w.shape
    tr = _tile(R, 256, 16)

    def body(ids_ref, w_ref, o_ref):
        o_ref[...] = w_ref[...].astype(BF16)

    return pl.pallas_call(
        body, name=name,
        grid_spec=pltpu.PrefetchScalarGridSpec(
            num_scalar_prefetch=1, grid=(R // tr,), in_specs=[pl.BlockSpec((tr, C), lambda i, ids: (i, 0))],
            out_specs=pl.BlockSpec((None, tr, C), lambda i, ids: (ids[0], i, 0))),
        out_shape=jax.ShapeDtypeStruct((N_CHIPS, R, C), BF16), compiler_params=_params(("parallel",)),
    )(ids, w)


SEM = pl.BlockSpec(memory_space=pltpu.SEMAPHORE)
EFFECT = pltpu.SideEffectType.DATAFLOW_SIDE_EFFECTING


def _hbm(a):
    return pltpu.with_memory_space_constraint(a, pltpu.HBM)


def _gather_start(name, placed, after):
    n = len(placed)

    ns = 3 * n

    def body(*refs):
        send, recv = refs[n + 1:n + 1 + ns], refs[n + 1 + ns:n + 1 + 2 * ns]
        buf = refs[n + 1 + 2 * ns:2 * n + 1 + 2 * ns]
        token = refs[2 * n + 1 + 2 * ns]
        x, y, c, chips = _place()
        me = 2 * x + y
        for a in range(n):
            half = buf[a].shape[1] // 2
            mine = buf[a].at[me, pl.ds(c * half, half)]
            for j, (cx, cy) in enumerate(chips):
                _remote(mine, mine, send[3 * a + j], recv[3 * a + j], (cx, cy, c)).start()
        token[...] = jnp.zeros_like(token)

    res = pl.pallas_call(
        body, name=name, in_specs=[HBM] * n + [pl.BlockSpec(memory_space=pl.ANY)],
        out_specs=[SEM] * (2 * ns) + [HBM] * n + [pl.BlockSpec(memory_space=pltpu.VMEM)],
        out_shape=[pltpu.SemaphoreType.DMA(())] * (2 * ns)
        + [pltpu.HBM(s.shape, s.dtype) for s in placed] + [jax.ShapeDtypeStruct((8, 128), F32)],
        input_output_aliases={a: 2 * ns + a for a in range(n)},
        compiler_params=pltpu.CompilerParams(has_side_effects=EFFECT),
    )(*[_hbm(s) for s in placed], after)
    return list(res[:ns]), list(res[ns:2 * ns]), list(res[2 * ns:2 * ns + n]), res[2 * ns + n]


def _gather_wait(name, send, recv, bufs, after):
    n = len(bufs)

    ns = 3 * n

    def body(*refs):
        buf = refs[:n]
        send_ref, recv_ref = refs[n:n + ns], refs[n + ns:n + 2 * ns]
        x, y, c, chips = _place()
        ids = [2 * cx + cy for cx, cy in chips]
        for a in range(n):
            half = buf[a].shape[1] // 2
            for j in range(3):
                landed = buf[a].at[ids[j], pl.ds(c * half, half)]
                cp = _remote(landed, landed, send_ref[3 * a + j], recv_ref[3 * a + j], (x, y, c))
                cp.wait_send()
                cp.wait_recv()

    res = pl.pallas_call(
        body, name=name, in_specs=[HBM] * n + [SEM] * (2 * ns) + [pl.BlockSpec(memory_space=pl.ANY)], out_specs=[HBM] * n,
        out_shape=[pltpu.HBM(s.shape, s.dtype) for s in bufs], input_output_aliases={a: a for a in range(n)},
        compiler_params=pltpu.CompilerParams(has_side_effects=EFFECT),
    )(*bufs, *send, *recv, after)
    return list(res)


def _gather_forward(name, bufs):
    n = len(bufs)

    def body(*refs):
        buf = refs[n:2 * n]
        send, recv = refs[2 * n:]
        x, y, c, chips = _place()
        ids = [2 * cx + cy for cx, cy in chips]
        copies = []
        for a in range(n):
            half = buf[a].shape[1] // 2
            for j in range(3):
                landed = buf[a].at[ids[j], pl.ds(c * half, half)]
                cp = _remote(landed, landed, send.at[a, j], recv.at[a, j], (x, y, 1 - c))
                cp.start()
                copies.append(cp)
        for a in range(n):
            half = buf[a].shape[1] // 2
            for j in range(3):
                landed = buf[a].at[ids[j], pl.ds((1 - c) * half, half)]
                _remote(landed, landed, send.at[a, j], recv.at[a, j], (x, y, c)).wait_recv()
        for cp in copies:
            cp.wait_send()

    return pl.pallas_call(
        body, name=name, in_specs=[HBM] * n, out_specs=[HBM] * n,
        out_shape=[jax.ShapeDtypeStruct(s.shape, s.dtype) for s in bufs], input_output_aliases={a: a for a in range(n)},
        scratch_shapes=[pltpu.SemaphoreType.DMA((n, 3)), pltpu.SemaphoreType.DMA((n, 3))],
    )(*bufs)


def _pair_start(name, grads):
    n = len(grads)
    ns = N_CHIPS * n

    def body(*refs):
        send, recv = refs[2 * n:2 * n + ns], refs[2 * n + ns:2 * n + 2 * ns]
        src = refs[2 * n + 2 * ns:3 * n + 2 * ns]
        land = refs[3 * n + 2 * ns:4 * n + 2 * ns]
        token = refs[4 * n + 2 * ns]
        x, y, c, chips = _place()
        order = [2 * x + y] + [2 * cx + cy for cx, cy in chips]
        for a in range(n):
            half = src[a].shape[1] // 2
            for j in range(N_CHIPS):
                _remote(src[a].at[order[j], pl.ds((1 - c) * half, half)], land[a].at[j],
                        send[N_CHIPS * a + j], recv[N_CHIPS * a + j], (x, y, 1 - c)).start()
        token[...] = jnp.zeros_like(token)

    lands = [jax.ShapeDtypeStruct((N_CHIPS, g.shape[1] // 2, g.shape[2]), g.dtype) for g in grads]
    res = pl.pallas_call(
        body, name=name, in_specs=[HBM] * (2 * n),
        out_specs=[SEM] * (2 * ns) + [HBM] * (2 * n) + [pl.BlockSpec(memory_space=pltpu.VMEM)],
        out_shape=[pltpu.SemaphoreType.DMA(())] * (2 * ns) + [pltpu.HBM(g.shape, g.dtype) for g in grads]
        + [pltpu.HBM(l.shape, l.dtype) for l in lands] + [jax.ShapeDtypeStruct((8, 128), F32)],
        input_output_aliases={a: 2 * ns + a for a in range(2 * n)},
        compiler_params=pltpu.CompilerParams(has_side_effects=EFFECT),
    )(*[_hbm(g) for g in grads], *[_hbm(lax.empty(l.shape, l.dtype)) for l in lands])
    return list(res[:ns]), list(res[ns:2 * ns]), list(res[2 * ns:2 * ns + n]), list(res[2 * ns + n:2 * ns + 2 * n]), res[2 * ns + 2 * n]


def _pair_wait(name, send, recv, grads, lands, after):
    n = len(grads)
    ns = N_CHIPS * n

    def body(*refs):
        src, land = refs[:n], refs[n:2 * n]
        send_ref, recv_ref = refs[2 * n:2 * n + ns], refs[2 * n + ns:2 * n + 2 * ns]
        x, y, c, _ = _place()
        for a in range(n):
            for j in range(N_CHIPS):
                cp = _remote(land[a].at[j], land[a].at[j], send_ref[N_CHIPS * a + j], recv_ref[N_CHIPS * a + j], (x, y, c))
                cp.wait_send()
                cp.wait_recv()

    res = pl.pallas_call(
        body, name=name, in_specs=[HBM] * (2 * n) + [SEM] * (2 * ns) + [pl.BlockSpec(memory_space=pl.ANY)],
        out_specs=[HBM] * (2 * n), out_shape=[pltpu.HBM(g.shape, g.dtype) for g in grads] + [pltpu.HBM(l.shape, l.dtype) for l in lands],
        input_output_aliases={a: a for a in range(2 * n)},
        compiler_params=pltpu.CompilerParams(has_side_effects=EFFECT),
    )(*grads, *lands, *send, *recv, after)
    return list(res[:n]), list(res[n:])


def _chip_start(name, parts):
    n = len(parts)
    ns = 3 * n

    def body(*refs):
        send, recv = refs[2 * n:2 * n + ns], refs[2 * n + ns:2 * n + 2 * ns]
        src = refs[2 * n + 2 * ns:3 * n + 2 * ns]
        land = refs[3 * n + 2 * ns:4 * n + 2 * ns]
        token = refs[4 * n + 2 * ns]
        x, y, c, chips = _place()
        for a in range(n):
            for j, (cx, cy) in enumerate(chips):
                _remote(src[a].at[j], land[a].at[j], send[3 * a + j], recv[3 * a + j], (cx, cy, c)).start()
        token[...] = jnp.zeros_like(token)

    res = pl.pallas_call(
        body, name=name, in_specs=[HBM] * (2 * n),
        out_specs=[SEM] * (2 * ns) + [HBM] * (2 * n) + [pl.BlockSpec(memory_space=pltpu.VMEM)],
        out_shape=[pltpu.SemaphoreType.DMA(())] * (2 * ns) + [pltpu.HBM(p.shape, p.dtype) for p in parts] * 2
        + [jax.ShapeDtypeStruct((8, 128), F32)],
        input_output_aliases={a: 2 * ns + a for a in range(2 * n)},
        compiler_params=pltpu.CompilerParams(has_side_effects=EFFECT),
    )(*[_hbm(p) for p in parts], *[_hbm(lax.empty(p.shape, p.dtype)) for p in parts])
    return list(res[:ns]), list(res[ns:2 * ns]), list(res[2 * ns:2 * ns + n]), list(res[2 * ns + n:2 * ns + 2 * n]), res[2 * ns + 2 * n]


def _chip_wait(name, send, recv, parts, lands, after):
    n = len(parts)
    ns = 3 * n

    def body(*refs):
        src, land = refs[:n], refs[n:2 * n]
        send_ref, recv_ref = refs[2 * n:2 * n + ns], refs[2 * n + ns:2 * n + 2 * ns]
        x, y, c, _ = _place()
        for a in range(n):
            for j in range(3):
                cp = _remote(src[a].at[j], land[a].at[j], send_ref[3 * a + j], recv_ref[3 * a + j], (x, y, c))
                cp.wait_send()
                cp.wait_recv()

    res = pl.pallas_call(
        body, name=name, in_specs=[HBM] * (2 * n) + [SEM] * (2 * ns) + [pl.BlockSpec(memory_space=pl.ANY)],
        out_specs=[HBM] * (2 * n), out_shape=[pltpu.HBM(p.shape, p.dtype) for p in parts] * 2,
        input_output_aliases={a: a for a in range(2 * n)},
        compiler_params=pltpu.CompilerParams(has_side_effects=EFFECT),
    )(*parts, *lands, *send, *recv, after)
    return list(res[n:])


def _pair_share(name, shards):
    n = len(shards)

    def body(*refs):
        buf = refs[n:2 * n]
        send, recv = refs[2 * n:]
        x, y, c, _ = _place()
        copies = []
        for a in range(n):
            half = buf[a].shape[0] // 2
            mine = buf[a].at[pl.ds(c * half, half)]
            cp = _remote(mine, mine, send.at[a], recv.at[a], (x, y, 1 - c))
            cp.start()
            copies.append(cp)
        for a, cp in enumerate(copies):
            half = buf[a].shape[0] // 2
            cp.wait_send()
            theirs = buf[a].at[pl.ds((1 - c) * half, half)]
            _remote(theirs, theirs, send.at[a], recv.at[a], (x, y, c)).wait_recv()

    return pl.pallas_call(
        body, name=name, in_specs=[HBM] * n, out_specs=[HBM] * n,
        out_shape=[jax.ShapeDtypeStruct(s.shape, s.dtype) for s in shards], input_output_aliases={a: a for a in range(n)},
        scratch_shapes=[pltpu.SemaphoreType.DMA((n,)), pltpu.SemaphoreType.DMA((n,))],
    )(*shards)


def _small_start(buf):
    R, W = buf.shape
    ns = N_DEV - 1

    def body(*refs):
        send, recv = refs[2:2 + ns], refs[2 + ns:2 + 2 * ns]
        src, land, token = refs[2 + 2 * ns], refs[3 + 2 * ns], refs[4 + 2 * ns]
        x, y, c, _ = _place()
        me = 4 * x + 2 * y + c
        for k in range(1, N_DEV):
            peer = (x ^ (k >> 2), y ^ ((k >> 1) & 1), c ^ (k & 1))
            _remote(src, land.at[me], send[k - 1], recv[k - 1], peer).start()
        token[...] = jnp.zeros_like(token)

    res = pl.pallas_call(
        body, name="small_start", in_specs=[HBM, HBM],
        out_specs=[SEM] * (2 * ns) + [HBM, HBM, pl.BlockSpec(memory_space=pltpu.VMEM)],
        out_shape=[pltpu.SemaphoreType.DMA(())] * (2 * ns) + [pltpu.HBM((R, W), F32), pltpu.HBM((N_DEV, R, W), F32),
                                                                jax.ShapeDtypeStruct((8, 128), F32)],
        input_output_aliases={0: 2 * ns, 1: 2 * ns + 1},
        compiler_params=pltpu.CompilerParams(has_side_effects=EFFECT),
    )(_hbm(buf), _hbm(jnp.zeros((N_DEV, R, W), F32)))
    return list(res[:ns]), list(res[ns:2 * ns]), res[2 * ns], res[2 * ns + 1], res[2 * ns + 2]


def _small_wait(send, recv, buf, land, after):
    ns = N_DEV - 1

    def body(*refs):
        land_ref = refs[1]
        send_ref, recv_ref = refs[2:2 + ns], refs[2 + ns:2 + 2 * ns]
        x, y, c, _ = _place()
        me = 4 * x + 2 * y + c
        for k in range(1, N_DEV):
            landed = land_ref.at[me ^ k]
            cp = _remote(landed, landed, send_ref[k - 1], recv_ref[k - 1], (x, y, c))
            cp.wait_send()
            cp.wait_recv()

    return pl.pallas_call(
        body, name="small_wait", in_specs=[HBM, HBM] + [SEM] * (2 * ns) + [pl.BlockSpec(memory_space=pl.ANY)],
        out_specs=[HBM, HBM], out_shape=[pltpu.HBM(buf.shape, buf.dtype), pltpu.HBM(land.shape, land.dtype)],
        input_output_aliases={0: 0, 1: 1}, compiler_params=pltpu.CompilerParams(has_side_effects=EFFECT),
    )(buf, land, *send, *recv, after)


def _small_sum(buf, land):
    def body(buf_ref, land_ref, out_ref):
        x, y, c, _ = _place()
        me = 4 * x + 2 * y + c
        total = None
        for d in range(N_DEV):
            term = jnp.where(me == d, buf_ref[...], land_ref[d])
            total = term if total is None else total + term
        out_ref[...] = total

    return pl.pallas_call(body, name="small_sum", out_shape=jax.ShapeDtypeStruct(buf.shape, F32))(buf, land)


def _pair_sum_bf16(name, grad, theirs, ids):
    _, R2, C = theirs.shape
    tr = _tile(R2, 256, 16)
    nrb = R2 // tr

    def body(ids_ref, a_ref, b_ref, o_ref):
        o_ref[...] = (a_ref[...] + b_ref[...]).astype(BF16)

    return pl.pallas_call(
        body, name=name,
        grid_spec=pltpu.PrefetchScalarGridSpec(
            num_scalar_prefetch=1, grid=(3, nrb),
            in_specs=[pl.BlockSpec((None, tr, C), lambda j, i, ids: (ids[3 + j], ids[1] * nrb + i, 0)),
                      pl.BlockSpec((None, tr, C), lambda j, i, ids: (j + 1, i, 0))],
            out_specs=pl.BlockSpec((None, tr, C), lambda j, i, ids: (j, i, 0))),
        out_shape=jax.ShapeDtypeStruct((3, R2, C), BF16), compiler_params=_params(("parallel", "parallel")),
    )(ids, grad, theirs)


def _chip_sum(name, grad, theirs, arrived, ids):
    _, R2, C = theirs.shape
    tr = _tile(R2, 256, 16)
    nrb = R2 // tr

    def body(ids_ref, a_ref, b_ref, r_ref, o_ref):
        tot = a_ref[...] + b_ref[...]
        for j in range(3):
            tot = tot + r_ref[j].astype(F32)
        o_ref[...] = tot

    return pl.pallas_call(
        body, name=name,
        grid_spec=pltpu.PrefetchScalarGridSpec(
            num_scalar_prefetch=1, grid=(nrb,),
            in_specs=[pl.BlockSpec((None, tr, C), lambda i, ids: (ids[0], ids[1] * nrb + i, 0)),
                      pl.BlockSpec((None, tr, C), lambda i, ids: (0, i, 0)),
                      pl.BlockSpec((3, tr, C), lambda i, ids: (0, i, 0))],
            out_specs=pl.BlockSpec((tr, C), lambda i, ids: (ids[1] * nrb + i, 0))),
        out_shape=jax.ShapeDtypeStruct((2 * R2, C), F32), compiler_params=_params(("parallel",)),
    )(ids, grad, theirs, arrived)


def _adamw(name, w, g, m, v):
    R, C = w.shape
    tr = _tile(R, 128, 8)
    c1 = 1.0 / (1.0 - ADAM_B1 ** ADAM_STEP)
    c2 = 1.0 / (1.0 - ADAM_B2 ** ADAM_STEP)

    def body(w_ref, g_ref, m_ref, v_ref, d_ref, mo_ref, vo_ref):
        gv = g_ref[...]
        mn = ADAM_B1 * m_ref[...] + (1.0 - ADAM_B1) * gv
        vn = ADAM_B2 * v_ref[...] + (1.0 - ADAM_B2) * (gv * gv)
        d_ref[...] = -ADAM_LR * ((mn * c1) / (jnp.sqrt(vn * c2) + ADAM_EPS) + ADAM_WD * w_ref[...])
        mo_ref[...] = mn
        vo_ref[...] = vn

    spec = pl.BlockSpec((tr, C), lambda i: (i, 0))
    sds = jax.ShapeDtypeStruct((R, C), F32)
    return pl.pallas_call(body, name=name, grid=(R // tr,), in_specs=[spec] * 4, out_specs=[spec] * 3, out_shape=[sds] * 3,
                          compiler_params=_params(("parallel",)))(w, g, m, v)


SMALL = ["ffn1_norm", "mix_norm", "mem_norm", "forget_bias", "fox_q_gain", "fox_k_gain", "swa_q_gain", "swa_k_gain", "swa_sinks",
         "mem_q_gain", "mem_k_gain", "ffn2_norm"]
LARGE = ["ffn1_gate", "ffn1_up", "ffn1_down", "w_in", "w_mem_k", "w_mem_v", "w_out", "ffn2_gate", "ffn2_up", "ffn2_down"]
GATHER_GROUPS = [["ffn1_gate", "ffn1_up"], ["ffn1_down", "w_in", "w_mem_k", "w_mem_v"], ["w_out", "ffn2_gate", "ffn2_up", "ffn2_down"]]
WEIGHTS = ["ffn1_norm", "ffn1_gate", "ffn1_up", "ffn1_down", "mix_norm", "mem_norm", "w_in", "forget_bias", "w_mem_k", "w_mem_v",
           "fox_q_gain", "fox_k_gain", "swa_q_gain", "swa_k_gain", "swa_sinks", "mem_q_gain", "mem_k_gain", "w_out", "ffn2_norm",
           "ffn2_gate", "ffn2_up", "ffn2_down"]


def _pad_proj_cols(w):
    out = jnp.zeros((w.shape[0], PROJ_W), w.dtype)
    for start, width, pstart in REF_GROUPS:
        out = lax.dynamic_update_slice(out, w[:, start:start + width], (0, pstart))
    return out


def _unpad_proj_cols(w):
    return jnp.concatenate([w[:, pstart:pstart + width] for _, width, pstart in REF_GROUPS], axis=1)


def _pack_small(vals):
    flat = jnp.concatenate([vals[k].reshape(-1).astype(F32) for k in SMALL + ["loss"]])
    n = flat.shape[0]
    total = -(-n // 1024) * 1024
    return jnp.pad(flat, (0, total - n)).reshape(total // 128, 128)


def _unpack_small(buf, shapes):
    flat = buf.reshape(-1)
    out, off = {}, 0
    for k in SMALL + ["loss"]:
        size = int(np.prod(shapes[k]))
        out[k] = flat[off:off + size].reshape(shapes[k])
        off += size
    return out


def kernel(x, mem, ffn1_norm, ffn1_gate, ffn1_up, ffn1_down, mix_norm, mem_norm, w_in, forget_bias, w_mem_k, w_mem_v, fox_q_gain, fox_k_gain, swa_q_gain, swa_k_gain, swa_sinks, mem_q_gain, mem_k_gain, w_out, ffn2_norm, ffn2_gate, ffn2_up, ffn2_down, loss_target, m_ffn1_norm, m_ffn1_gate, m_ffn1_up, m_ffn1_down, m_mix_norm, m_mem_norm, m_w_in, m_forget_bias, m_w_mem_k, m_w_mem_v, m_fox_q_gain, m_fox_k_gain, m_swa_q_gain, m_swa_k_gain, m_swa_sinks, m_mem_q_gain, m_mem_k_gain, m_w_out, m_ffn2_norm, m_ffn2_gate, m_ffn2_up, m_ffn2_down, v_ffn1_norm, v_ffn1_gate, v_ffn1_up, v_ffn1_down, v_mix_norm, v_mem_norm, v_w_in, v_forget_bias, v_w_mem_k, v_w_mem_v, v_fox_q_gain, v_fox_k_gain, v_swa_q_gain, v_swa_k_gain, v_swa_sinks, v_mem_q_gain, v_mem_k_gain, v_w_out, v_ffn2_norm, v_ffn2_gate, v_ffn2_up, v_ffn2_down):
    given = dict(locals())
    T, D = x.shape[1], x.shape[2]
    ML = mem.shape[1]
    xin = x.reshape(T, D)
    target = loss_target.reshape(T, D)
    memin = mem.reshape(ML, D)

    ids = _place_ids()
    shard = {k: given[k][0] for k in LARGE}
    shard["w_in"] = _pad_proj_cols(shard["w_in"])
    started, after = [], ids
    for gi, group in enumerate(GATHER_GROUPS):
        placed = [_cast_place("cast_" + k, shard[k], ids) for k in group]
        send, recv, bufs, after = _gather_start("gather_start_%d" % gi, placed, after)
        started.append((send, recv, bufs))

    def arrive(gi, done):
        send, recv, bufs = started[gi]
        bufs = _gather_wait("gather_wait_%d" % gi, send, recv, bufs, done)
        return dict(zip(GATHER_GROUPS[gi], _gather_forward("gather_forward_%d" % gi, bufs)))

    gains = jnp.concatenate([fox_q_gain, fox_k_gain, swa_q_gain, swa_k_gain, mem_q_gain,
                             jnp.pad(forget_bias, ((0, 0), (0, HEAD_DIM - FOX_HEADS))), jnp.zeros((2, HEAD_DIM), F32)], axis=0)
    slopes_np = 2.0 ** (-8.0 * np.arange(1, SWA_HEADS + 1) / SWA_HEADS)
    slopes = jnp.asarray(np.repeat(slopes_np, WINDOW).reshape(SWA_KV_HEADS, GR, 1), F32)
    sinks = jnp.repeat(swa_sinks.reshape(SWA_HEADS), WINDOW).reshape(SWA_KV_HEADS, GR, 1)

    h1 = _rms_fwd("ffn1_norm_fwd", xin, ffn1_norm + after[0, 0])
    full = arrive(0, h1)
    wg1, wu1 = full["ffn1_gate"], full["ffn1_up"]
    g1, u1, a1 = _ffn_gu("ffn1_gate_up", h1, wg1, wu1)
    full = arrive(1, a1)
    wd1 = full["ffn1_down"].reshape(-1, D)
    win = full["w_in"].reshape(D, PROJ_W)
    wmk = full["w_mem_k"].reshape(D, MEM_HEADS * HEAD_DIM)
    wmv = full["w_mem_v"].reshape(D, MEM_HEADS * HEAD_DIM)
    x1 = _ffn_down("ffn1_down", a1, wd1, xin)
    h2 = _rms_fwd("mix_norm_fwd", x1, mix_norm)
    proj = _mm2d("proj_in", h2, win, "nn", F32, tn=1408, tk=2048, n_outer=True)
    qkv, logf = _prep_fwd(proj, gains)
    cum = _cumsum_rows("forget_cumsum", [logf], False)
    cum_h = cum[:, :FOX_HEADS].T
    cq_row = cum_h.reshape(FOX_HEADS, 1, T)
    ck_rep = jnp.broadcast_to(cum_h[:, :, None], (FOX_HEADS, T, HEAD_DIM))
    k_t, v_t = qkv[:, C_FK:C_FK + FOX_W].T, qkv[:, C_FV:C_FV + FOX_W].T
    mn = _rms_fwd("mem_norm_fwd", memin, mem_norm)
    mk_raw = _mm2d("mem_k_proj", mn, wmk, "nn", F32)
    mv = _mm2d("mem_v_proj", mn, wmv, "nn", BF16)
    mk = _head_norm_rows(mk_raw, mem_k_gain)
    out_a_t, lse_a = _fox_fwd(qkv, v_t, cq_row, ck_rep)
    out_a = out_a_t.T.astype(BF16)
    out_b, lse_b = _swa_fwd(qkv, slopes, sinks)
    out_c, lse_c = _mem_fwd(qkv, mk, mv)
    mixed = jnp.concatenate([out_a, out_b, out_c], axis=1)
    full = arrive(2, mixed)
    wo = full["w_out"].reshape(-1, D)
    wg2, wu2, wd2 = full["ffn2_gate"], full["ffn2_up"], full["ffn2_down"].reshape(-1, D)
    x2 = _mm2d("mix_out", mixed, wo, "nn", F32, tk=2048, n_outer=True, extras=[x1], epilogue=lambda accs, ex: [ex[0] + accs[0]])
    h3 = _rms_fwd("ffn2_norm_fwd", x2, ffn2_norm)
    g2, u2, a2 = _ffn_gu("ffn2_gate_up", h3, wg2, wu2)
    x3 = _ffn_down("ffn2_down", a2, wd2, x2)
    dx3, dyb3, loss_part = _loss_head(x3, target)

    grads, small, res = {}, {"loss": loss_part[0, 0]}, {}

    def pair_off(tag, group):
        send, recv, own, lands, token = _pair_start("grad_pair_start_" + tag, [grads[k] for k in group])
        return (group, send, recv, own, lands), token

    def chip_off(tag, started, done):
        group, send, recv, own, lands = started
        own, theirs = _pair_wait("grad_pair_wait_" + tag, send, recv, own, lands, done)
        grads.update(zip(group, own))
        to_chips = [_pair_sum_bf16("pair_sum_" + k, grads[k], b, ids) for k, b in zip(group, theirs)]
        send, recv, parts, lands, token = _chip_start("grad_chip_start_" + tag, to_chips)
        return (group, theirs, send, recv, parts, lands), token

    def finish(tag, state, done):
        group, theirs, send, recv, parts, lands = state
        arrived = _chip_wait("grad_chip_wait_" + tag, send, recv, parts, lands, done)
        halves = [_chip_sum("chip_sum_" + k, grads[k], b, r, ids) for k, b, r in zip(group, theirs, arrived)]
        reduced = dict(zip(group, _pair_share("grad_pair_share_" + tag, halves)))
        last = None
        for k in group:
            gk = _unpad_proj_cols(reduced[k]) if k == "w_in" else reduced[k]
            d, mo, vo = _adamw("adamw_" + k, given[k][0], gk, given["m_" + k][0], given["v_" + k][0])
            res[k] = tuple(t[None] for t in (gk, d, mo, vo))
            last = vo
        return last

    dg2, du2 = _ffn_bwd_act("ffn2", dyb3, wd2, g2, u2, N_CHIPS)
    grads["ffn2_down"] = _ffn_bwd_down("ffn2", a2, dyb3, N_CHIPS).reshape(N_CHIPS, -1, D)
    grads["ffn2_gate"], grads["ffn2_up"] = _ffn_bwd_gate_up("ffn2", h3, dg2, du2, N_CHIPS)
    started, token = pair_off("a", ["ffn2_gate", "ffn2_up", "ffn2_down"])
    dh3 = _ffn_bwd_x("ffn2", dg2, du2, wg2, wu2, token)
    state_a, token = chip_off("a", started, dh3)
    dx2, dx2b, small["ffn2_norm"] = _rms_bwd("ffn2_norm_bwd", dh3, x2, ffn2_norm + token[0, 0], dx3, 1.0)
    dmix = _mm2d("mix_out_dx", dx2b, wo, "nt", BF16, tk=2048, n_outer=True)
    grads["w_out"] = _mm2d("mix_out_dw", mixed, dx2b, "tn", F32, tk=1024).reshape(N_CHIPS, -1, D)
    dq_t, dfk, dfv, dck, dcq = _fox_bwd(qkv, k_t, cq_row, ck_rep, out_a_t, lse_a, dmix, dmix[:, :FOX_W].T)
    dfq = dq_t.transpose(1, 3, 0, 2).reshape(T, FOX_W)
    dsq, dsk, dsv, dsink = _swa_bwd(qkv, slopes, sinks, out_b, lse_b, dmix)
    dmq, dmk, dmv = _mem_bwd(qkv, mk, mv, out_c, lse_c, dmix)
    small["swa_sinks"] = dsink[:, :SWA_GROUP, 0].reshape(1, SWA_HEADS)
    dcum = [jnp.pad(d.reshape(FOX_HEADS, T).T, ((0, 0), (0, HEAD_DIM - FOX_HEADS))) for d in (dck, dcq)]
    dlogf = _cumsum_rows("forget_cumsum_bwd", dcum, True)
    dproj, dgains = _prep_bwd(proj, gains, dfq, dfk, dfv, dsq, dsk, dsv, dmq, dlogf)
    for row, k in enumerate(["fox_q_gain", "fox_k_gain", "swa_q_gain", "swa_k_gain", "mem_q_gain"]):
        small[k] = dgains[row:row + 1, :]
    small["forget_bias"] = dgains[5:6, :FOX_HEADS]
    grads["w_in"] = _mm2d("proj_in_dw", h2, dproj, "tn", F32, tn=1408, tk=1024).reshape(N_CHIPS, -1, PROJ_W)
    dmk_raw, small["mem_k_gain"] = _head_norm_rows_bwd(mk_raw, mem_k_gain, dmk)
    dmvb = dmv.astype(BF16)
    grads["w_mem_k"] = _mm2d("mem_k_dw", mn, dmk_raw, "tn", F32).reshape(N_CHIPS, -1, MEM_HEADS * HEAD_DIM)
    grads["w_mem_v"] = _mm2d("mem_v_dw", mn, dmvb, "tn", F32).reshape(N_CHIPS, -1, MEM_HEADS * HEAD_DIM)
    dmn = _mm2d("mem_k_dx", dmk_raw, wmk, "nt", F32)
    dmn = _mm2d("mem_v_dx", dmvb, wmv, "nt", F32, extras=[dmn], epilogue=lambda accs, ex: [ex[0] + accs[0]])
    _, _, small["mem_norm"] = _rms_bwd("mem_norm_bwd", dmn, memin, mem_norm, jnp.zeros_like(memin), 1.0)
    started, token = pair_off("b", ["w_out", "w_in", "w_mem_k", "w_mem_v"])
    dh2 = _mm2d("proj_in_dx", dproj, win, "nt", F32, tm=1024, tk=1408, after=token)
    state_b, token = chip_off("b", started, dh2)
    dx1, dyb1, small["mix_norm"] = _rms_bwd("mix_norm_bwd", dh2, x1, mix_norm + token[0, 0], dx2, 0.5)
    grads["ffn1_down"] = _ffn_bwd_down("ffn1", a1, dyb1, N_CHIPS).reshape(N_CHIPS, -1, D)
    started, token = pair_off("c", ["ffn1_down"])
    dg1, du1 = _ffn_bwd_act("ffn1", dyb1, wd1, g1, u1, N_CHIPS, after=token)
    state_c, token = chip_off("c", started, dg1)
    grads["ffn1_gate"], grads["ffn1_up"] = _ffn_bwd_gate_up("ffn1", h1, dg1, du1, N_CHIPS, after=token)
    started, token = pair_off("d", ["ffn1_gate", "ffn1_up"])
    dh1 = _ffn_bwd_x("ffn1", dg1, du1, wg1, wu1, token)
    state_d, token = chip_off("d", started, dh1)
    grad_x, _, small["ffn1_norm"] = _rms_bwd("ffn1_norm_bwd", dh1, xin, ffn1_norm + token[0, 0], dx1, 1.0)

    s_send, s_recv, s_buf, s_land, token = _small_start(_pack_small(small))

    done = finish("a", state_a, token)
    done = finish("b", state_b, done)
    done = finish("c", state_c, done)
    done = finish("d", state_d, done)

    shapes = {k: given[k].shape for k in SMALL}
    shapes["loss"] = ()
    s_buf, s_land = _small_wait(s_send, s_recv, s_buf, s_land, done)
    red_small = _unpack_small(_small_sum(s_buf, s_land), shapes)
    loss = red_small["loss"]
    zero = {"loss": jnp.zeros((), F32)}
    packed = [_pack_small({**zero, **{k: src[k] for k in SMALL}}) for src in (
        {k: given[k] for k in SMALL}, red_small, {k: given["m_" + k] for k in SMALL}, {k: given["v_" + k] for k in SMALL})]
    d_s, m_s, v_s = (_unpack_small(t, shapes) for t in _adamw("adamw_small", *packed))
    for k in SMALL:
        res[k] = (red_small[k], d_s[k], m_s[k], v_s[k])

    outs = [loss, grad_x.reshape(1, T, D)]
    for part in range(4):
        outs += [res[k][part] for k in WEIGHTS]
    return tuple(outs)
```

```python
import functools

import numpy as np
import jax
import jax.numpy as jnp
from jax import lax
from jax.experimental import pallas as pl
from jax.experimental.pallas import tpu as pltpu

F32 = jnp.float32
BF16 = jnp.bfloat16
MESH = pl.DeviceIdType.MESH

HEAD_DIM = 128
FOX_HEADS = 6
SWA_HEADS = 6
SWA_KV_HEADS = 2
SWA_GROUP = SWA_HEADS // SWA_KV_HEADS
MEM_HEADS = 4
WINDOW = 128
EPS = 1e-6
NEG_INF = -1e30
SCALE = HEAD_DIM ** -0.5

C_FQ = 0
C_FK = C_FQ + FOX_HEADS * HEAD_DIM
C_FV = C_FK + FOX_HEADS * HEAD_DIM
C_SQ = C_FV + FOX_HEADS * HEAD_DIM
C_SK = C_SQ + SWA_HEADS * HEAD_DIM
C_SV = C_SK + SWA_KV_HEADS * HEAD_DIM
C_MQ = C_SV + SWA_KV_HEADS * HEAD_DIM
C_FL = C_MQ + MEM_HEADS * HEAD_DIM
PROJ_W = C_FL + HEAD_DIM
FOX_W = FOX_HEADS * HEAD_DIM
REF_GROUPS = [
    (0, FOX_W, C_FQ), (FOX_W, FOX_W, C_FK), (2 * FOX_W, FOX_W, C_FV), (3 * FOX_W, FOX_HEADS, C_FL),
    (3 * FOX_W + FOX_HEADS, SWA_HEADS * HEAD_DIM, C_SQ),
    (3 * FOX_W + FOX_HEADS + SWA_HEADS * HEAD_DIM, SWA_KV_HEADS * HEAD_DIM, C_SK),
    (3 * FOX_W + FOX_HEADS + (SWA_HEADS + SWA_KV_HEADS) * HEAD_DIM, SWA_KV_HEADS * HEAD_DIM, C_SV),
    (3 * FOX_W + FOX_HEADS + (SWA_HEADS + 2 * SWA_KV_HEADS) * HEAD_DIM, MEM_HEADS * HEAD_DIM, C_MQ),
]

ADAM_LR = 0.001
ADAM_B1 = 0.9
ADAM_B2 = 0.999
ADAM_EPS = 1e-08
ADAM_WD = 0.01
ADAM_STEP = 10

V7X_VMEM_LIMIT = 56 * 1024 * 1024
N_CHIPS = 4
N_DEV = 8


def _tile(n, pref, mult=128):
    t = (min(pref, n) // mult) * mult
    while t >= mult:
        if n % t == 0:
            return t
        t -= mult
    return n


def _params(sem):
    return pltpu.CompilerParams(dimension_semantics=sem, vmem_limit_bytes=V7X_VMEM_LIMIT)


_DIMS = {"nn": (((1,), (0,)), ((), ())), "nt": (((1,), (1,)), ((), ())), "tn": (((0,), (0,)), ((), ()))}


def _dot(a, b, mode):
    return lax.dot_general(a, b, _DIMS[mode], preferred_element_type=F32)


def _mm(name, grid, pairs, acc_of, acc_shapes, extras, outs, epilogue, after=None):
    n_p, n_e, n_o, n_a = len(pairs), len(extras), len(outs), len(acc_shapes)
    n_w = 0 if after is None else 1
    nk = grid[2]
    n_in = sum(1 if a is None else 2 for a, *_ in pairs)

    def body(*refs):
        ex = refs[n_in:n_in + n_e]
        out = refs[n_in + n_e + n_w:n_in + n_e + n_w + n_o]
        accs = refs[n_in + n_e + n_w + n_o:]
        parts = [None] * n_a
        at = 0
        for p in range(n_p):
            if pairs[p][0] is None:
                a_ref, b_ref = refs[0], refs[at]
                at += 1
            else:
                a_ref, b_ref = refs[at], refs[at + 1]
                at += 2
            d = _dot(a_ref[...], b_ref[...], pairs[p][4])
            parts[acc_of[p]] = d if parts[acc_of[p]] is None else parts[acc_of[p]] + d

        def finish(vals):
            for o, r in zip(out, epilogue(vals, [e[...] for e in ex])):
                o[...] = r.astype(o.dtype)

        if nk == 1:
            finish(parts)
            return
        k = pl.program_id(2)

        @pl.when(k == 0)
        def _():
            for a, d in zip(accs, parts):
                a[...] = d

        @pl.when((k > 0) & (k < nk - 1))
        def _():
            for a, d in zip(accs, parts):
                a[...] += d

        @pl.when(k == nk - 1)
        def _():
            finish([a[...] + d for a, d in zip(accs, parts)])

    in_specs, args = [], []
    for a, a_spec, b, b_spec, _ in pairs:
        if a is not None:
            in_specs.append(a_spec)
            args.append(a)
        in_specs.append(b_spec)
        args.append(b)
    for e, e_spec in extras:
        in_specs.append(e_spec)
        args.append(e)
    if after is not None:
        in_specs.append(pl.BlockSpec(memory_space=pl.ANY))
        args.append(after)
    res = pl.pallas_call(
        body, name=name, grid=grid, in_specs=in_specs,
        out_specs=[s for _, s in outs], out_shape=[o for o, _ in outs],
        scratch_shapes=[pltpu.VMEM(s, F32) for s in acc_shapes] if nk > 1 else [],
        compiler_params=_params(("parallel", "parallel", "arbitrary")),
    )(*args)
    return res


def _mm2d(name, a, b, mode, out_dtype, tm=512, tn=1024, tk=1024, extras=(), epilogue=None, n_out=1, after=None, n_outer=False):
    if mode == "nn":
        (M, K), N = a.shape, b.shape[1]
    elif mode == "nt":
        (M, K), N = a.shape, b.shape[0]
    else:
        (K, M), N = a.shape, b.shape[1]
    tm, tn, tk = _tile(M, tm), _tile(N, tn), _tile(K, tk)

    def spec(shape, index):
        if n_outer:
            return pl.BlockSpec(shape, lambda j, i, k: index(i, j, k))
        return pl.BlockSpec(shape, index)

    a_spec = spec((tk, tm), lambda i, j, k: (k, i)) if mode == "tn" else spec((tm, tk), lambda i, j, k: (i, k))
    b_spec = spec((tn, tk), lambda i, j, k: (j, k)) if mode == "nt" else spec((tk, tn), lambda i, j, k: (k, j))
    mn = spec((tm, tn), lambda i, j, k: (i, j))
    if epilogue is None:
        epilogue = lambda accs, ex: [accs[0]]
    if not isinstance(out_dtype, (list, tuple)):
        out_dtype = [out_dtype] * n_out
    grid = (N // tn, M // tm, K // tk) if n_outer else (M // tm, N // tn, K // tk)
    res = _mm(name, grid, [(a, a_spec, b, b_spec, mode)], [0], [(tm, tn)],
              [(e, mn) for e in extras], [(jax.ShapeDtypeStruct((M, N), d), mn) for d in out_dtype], epilogue, after=after)
    return res[0] if len(res) == 1 else res


def _sigmoid(x):
    return 1.0 / (1.0 + jnp.exp(-x))


def _sigmoid_fast(x):
    return pl.reciprocal(1.0 + jnp.exp(-x), approx=True)


def _ffn_gu(name, h, wg, wu):
    T, D = h.shape
    nf, _, F4 = wg.shape
    tm, tk = _tile(T, 512), _tile(D, 2048)
    a_spec = pl.BlockSpec((tm, tk), lambda j, i, k: (i, k))
    b_spec = pl.BlockSpec((None, tk, F4), lambda j, i, k: (j, k, 0))
    o_spec = pl.BlockSpec((tm, F4), lambda j, i, k: (i, j))

    def epilogue(accs, ex):
        g, u = accs
        return [g, u, g * _sigmoid_fast(g) * u]

    sds = jax.ShapeDtypeStruct((T, nf * F4), BF16)
    return _mm(name, (nf, T // tm, D // tk), [(h, a_spec, wg, b_spec, "nn"), (None, None, wu, b_spec, "nn")], [0, 1],
               [(tm, F4), (tm, F4)], [], [(sds, o_spec)] * 3, epilogue)


def _ffn_down(name, a, wd, xres):
    return _mm2d(name, a, wd, "nn", F32, tm=1024, tk=wd.shape[0] // N_CHIPS, extras=[xres],
                 epilogue=lambda accs, ex: [ex[0] + 0.5 * accs[0]])


def _ffn_bwd_down(tag, a, dyb, nf, after=None):
    return _mm2d(tag + "_dwd", a, dyb, "tn", F32, tm=a.shape[1] // nf, tk=1024, after=after)


def _ffn_bwd_act(tag, dyb, wd, g, u, nf, after=None):
    def act_bwd(accs, ex):
        gf, uf = ex[0].astype(F32), ex[1].astype(F32)
        s = _sigmoid_fast(gf)
        return [accs[0] * uf * s * (1.0 + gf * (1.0 - s)), accs[0] * gf * s]

    return _mm2d(tag + "_da", dyb, wd, "nt", BF16, tn=wd.shape[0] // nf, tk=2048, extras=[g, u], epilogue=act_bwd, n_out=2,
                 n_outer=True, after=after)


def _ffn_bwd_gate_up(tag, h, dg, du, nf, after=None):
    T, D = h.shape
    F4 = dg.shape[1] // nf
    tm, tk = _tile(D, 512), _tile(T, 1024)
    h_spec = pl.BlockSpec((tk, tm), lambda i, j, k: (k, i))
    d_spec = pl.BlockSpec((tk, F4), lambda i, j, k: (k, j))
    w_spec = pl.BlockSpec((None, tm, F4), lambda i, j, k: (j, i, 0))
    sds = jax.ShapeDtypeStruct((nf, D, F4), F32)
    return _mm(tag + "_dwgu", (D // tm, nf, T // tk), [(h, h_spec, dg, d_spec, "tn"), (None, None, du, d_spec, "tn")],
               [0, 1], [(tm, F4), (tm, F4)], [], [(sds, w_spec)] * 2, lambda accs, ex: accs, after=after)


def _ffn_bwd_x(tag, dg, du, wg, wu, after):
    T = dg.shape[0]
    nf, D, F4 = wg.shape
    tm, tn = _tile(T, 1024), _tile(D, 1024)
    a_spec = pl.BlockSpec((tm, F4), lambda i, j, k: (i, k))
    b_spec = pl.BlockSpec((None, tn, F4), lambda i, j, k: (k, j, 0))
    o_spec = pl.BlockSpec((tm, tn), lambda i, j, k: (i, j))
    (dh,) = _mm(tag + "_dh", (T // tm, D // tn, nf), [(dg, a_spec, wg, b_spec, "nt"), (du, a_spec, wu, b_spec, "nt")],
                [0, 0], [(tm, tn)], [], [(jax.ShapeDtypeStruct((T, D), F32), o_spec)], lambda accs, ex: accs, after=after)
    return dh


def _rms_fwd(name, x, gain):
    R, D = x.shape
    tr = _tile(R, 256, 8)

    def body(x_ref, g_ref, o_ref):
        xv = x_ref[...]
        r = lax.rsqrt(jnp.mean(xv * xv, axis=-1, keepdims=True) + EPS)
        o_ref[...] = (xv * r * g_ref[...]).astype(BF16)

    return pl.pallas_call(
        body, name=name, grid=(R // tr,),
        in_specs=[pl.BlockSpec((tr, D), lambda i: (i, 0)), pl.BlockSpec((1, D), lambda i: (0, 0))],
        out_specs=pl.BlockSpec((tr, D), lambda i: (i, 0)), out_shape=jax.ShapeDtypeStruct((R, D), BF16),
        compiler_params=_params(("parallel",)),
    )(x, gain)


def _rms_bwd(name, dh, x, gain, dres, bscale):
    R, D = x.shape
    tr = _tile(R, 256, 8)

    def body(dh_ref, x_ref, g_ref, dres_ref, dx_ref, dxb_ref, dg_ref):
        xv, dy = x_ref[...], dh_ref[...]
        r = lax.rsqrt(jnp.mean(xv * xv, axis=-1, keepdims=True) + EPS)
        xn = xv * r
        uu = dy * g_ref[...]
        dx = dres_ref[...] + r * (uu - xn * jnp.mean(xn * uu, axis=-1, keepdims=True))
        dx_ref[...] = dx
        dxb_ref[...] = (bscale * dx).astype(BF16)
        part = jnp.sum(dy * xn, axis=0, keepdims=True)

        @pl.when(pl.program_id(0) == 0)
        def _():
            dg_ref[...] = part

        @pl.when(pl.program_id(0) > 0)
        def _():
            dg_ref[...] += part

    row = pl.BlockSpec((tr, D), lambda i: (i, 0))
    vec = pl.BlockSpec((1, D), lambda i: (0, 0))
    return pl.pallas_call(
        body, name=name, grid=(R // tr,), in_specs=[row, row, vec, row], out_specs=[row, row, vec],
        out_shape=[jax.ShapeDtypeStruct((R, D), F32), jax.ShapeDtypeStruct((R, D), BF16), jax.ShapeDtypeStruct((1, D), F32)],
        compiler_params=_params(("arbitrary",)),
    )(dh, x, gain, dres)


def _loss_head(y, target):
    R, D = y.shape
    tr = _tile(R, 256, 8)

    def body(y_ref, t_ref, d_ref, db_ref, l_ref):
        e = y_ref[...] - t_ref[...]
        d = e * (1.0 / D)
        d_ref[...] = d
        db_ref[...] = (0.5 * d).astype(BF16)
        part = jnp.zeros((8, 128), F32) + (0.5 / D) * jnp.sum(e * e)

        @pl.when(pl.program_id(0) == 0)
        def _():
            l_ref[...] = part

        @pl.when(pl.program_id(0) > 0)
        def _():
            l_ref[...] += part

    row = pl.BlockSpec((tr, D), lambda i: (i, 0))
    acc = pl.BlockSpec((8, 128), lambda i: (0, 0))
    return pl.pallas_call(
        body, name="loss_head", grid=(R // tr,), in_specs=[row, row], out_specs=[row, row, acc],
        out_shape=[jax.ShapeDtypeStruct((R, D), F32), jax.ShapeDtypeStruct((R, D), BF16), jax.ShapeDtypeStruct((8, 128), F32)],
        compiler_params=_params(("arbitrary",)),
    )(y, target)


def _head_norm(xs, g):
    r = lax.rsqrt(jnp.mean(xs * xs, axis=-1, keepdims=True) + EPS)
    return xs * r * g


def _head_norm_bwd(xs, g, dy):
    r = lax.rsqrt(jnp.mean(xs * xs, axis=-1, keepdims=True) + EPS)
    xn = xs * r
    uu = dy * g
    return r * (uu - xn * jnp.mean(xn * uu, axis=-1, keepdims=True)), jnp.sum(dy * xn, axis=0, keepdims=True)


NORMED = [(C_FQ, FOX_HEADS, 0), (C_FK, FOX_HEADS, 1), (C_SQ, SWA_HEADS, 2), (C_SK, SWA_KV_HEADS, 3), (C_MQ, MEM_HEADS, 4)]
PLAIN = [(C_FV, FOX_HEADS), (C_SV, SWA_KV_HEADS)]


def _prep_fwd(proj, gains):
    T = proj.shape[0]
    tr = _tile(T, 256, 128)

    def body(p_ref, g_ref, o_ref, lf_ref, kt_ref, vt_ref):
        for start, heads, row in NORMED:
            gn = g_ref[row:row + 1, :]
            for hh in range(heads):
                sl = slice(start + hh * HEAD_DIM, start + (hh + 1) * HEAD_DIM)
                y = _head_norm(p_ref[:, sl], gn)
                o_ref[:, sl] = y.astype(BF16)
                if start == C_FK:
                    kt_ref[hh * HEAD_DIM:(hh + 1) * HEAD_DIM, :] = y.T.astype(BF16)
        for start, heads in PLAIN:
            sl = slice(start, start + heads * HEAD_DIM)
            o_ref[:, sl] = p_ref[:, sl].astype(BF16)
        for hh in range(FOX_HEADS):
            sl = slice(C_FV + hh * HEAD_DIM, C_FV + (hh + 1) * HEAD_DIM)
            vt_ref[hh * HEAD_DIM:(hh + 1) * HEAD_DIM, :] = p_ref[:, sl].T.astype(BF16)
        zb = p_ref[:, C_FL:C_FL + HEAD_DIM] + g_ref[5:6, :]
        o_ref[:, C_FL:C_FL + HEAD_DIM] = jnp.zeros((tr, HEAD_DIM), BF16)
        lf_ref[...] = jnp.minimum(zb, 0.0) - jnp.log(1.0 + jnp.exp(-jnp.abs(zb)))

    return pl.pallas_call(
        body, name="prep_fwd", grid=(T // tr,),
        in_specs=[pl.BlockSpec((tr, PROJ_W), lambda i: (i, 0)), pl.BlockSpec((8, 128), lambda i: (0, 0))],
        out_specs=[pl.BlockSpec((tr, PROJ_W), lambda i: (i, 0)), pl.BlockSpec((tr, HEAD_DIM), lambda i: (i, 0)),
                   pl.BlockSpec((FOX_W, tr), lambda i: (0, i)), pl.BlockSpec((FOX_W, tr), lambda i: (0, i))],
        out_shape=[jax.ShapeDtypeStruct((T, PROJ_W), BF16), jax.ShapeDtypeStruct((T, HEAD_DIM), F32),
                   jax.ShapeDtypeStruct((FOX_W, T), BF16), jax.ShapeDtypeStruct((FOX_W, T), BF16)],
        compiler_params=_params(("parallel",)),
    )(proj, gains)


def _prep_bwd(proj, gains, dfq, dfk, dfv, dsq, dsk, dsv, dmq, dlogf):
    T = proj.shape[0]
    tr = _tile(T, 256, 8)
    d_normed = {C_FQ: 0, C_FK: 1, C_SQ: 3, C_SK: 4, C_MQ: 6}
    d_plain = {C_FV: 2, C_SV: 5}

    def body(p_ref, g_ref, *rest):
        d_refs, dlf_ref, o_ref, dg_ref = rest[:7], rest[7], rest[8], rest[9]
        rows = []
        for start, heads, row in NORMED:
            gn = g_ref[row:row + 1, :]
            d_ref = d_refs[d_normed[start]]
            tot = jnp.zeros((1, HEAD_DIM), F32)
            for hh in range(heads):
                sl = slice(start + hh * HEAD_DIM, start + (hh + 1) * HEAD_DIM)
                dx, dgn = _head_norm_bwd(p_ref[:, sl], gn, d_ref[:, hh * HEAD_DIM:(hh + 1) * HEAD_DIM])
                o_ref[:, sl] = dx.astype(BF16)
                tot = tot + dgn
            rows.append(tot)
        for start, heads in PLAIN:
            o_ref[:, start:start + heads * HEAD_DIM] = d_refs[d_plain[start]][...].astype(BF16)
        zb = p_ref[:, C_FL:C_FL + HEAD_DIM] + g_ref[5:6, :]
        lane = lax.broadcasted_iota(jnp.int32, (tr, HEAD_DIM), 1)
        dz = jnp.where(lane < FOX_HEADS, dlf_ref[...] * (1.0 - _sigmoid(zb)), 0.0)
        o_ref[:, C_FL:C_FL + HEAD_DIM] = dz.astype(BF16)
        rows.append(jnp.sum(dz, axis=0, keepdims=True))
        part = jnp.concatenate(rows + [jnp.zeros((2, HEAD_DIM), F32)], axis=0)

        @pl.when(pl.program_id(0) == 0)
        def _():
            dg_ref[...] = part

        @pl.when(pl.program_id(0) > 0)
        def _():
            dg_ref[...] += part

    def rows_of(w):
        return pl.BlockSpec((tr, w), lambda i: (i, 0))

    small = pl.BlockSpec((8, 128), lambda i: (0, 0))
    ds = [dfq, dfk, dfv, dsq, dsk, dsv, dmq]
    return pl.pallas_call(
        body, name="prep_bwd", grid=(T // tr,),
        in_specs=[rows_of(PROJ_W), small] + [rows_of(d.shape[1]) for d in ds] + [rows_of(HEAD_DIM)],
        out_specs=[rows_of(PROJ_W), small],
        out_shape=[jax.ShapeDtypeStruct((T, PROJ_W), BF16), jax.ShapeDtypeStruct((8, 128), F32)],
        compiler_params=_params(("arbitrary",)),
    )(proj, gains, *ds, dlogf)


def _head_norm_rows(x, gain):
    R, W = x.shape

    def body(x_ref, g_ref, o_ref):
        for hh in range(W // HEAD_DIM):
            sl = slice(hh * HEAD_DIM, (hh + 1) * HEAD_DIM)
            o_ref[:, sl] = _head_norm(x_ref[:, sl], g_ref[...]).astype(BF16)

    return pl.pallas_call(body, name="mem_k_norm", out_shape=jax.ShapeDtypeStruct((R, W), BF16))(x, gain)


def _head_norm_rows_bwd(x, gain, dy):
    R, W = x.shape

    def body(x_ref, g_ref, dy_ref, dx_ref, dg_ref):
        tot = jnp.zeros((1, HEAD_DIM), F32)
        for hh in range(W // HEAD_DIM):
            sl = slice(hh * HEAD_DIM, (hh + 1) * HEAD_DIM)
            dx, dgn = _head_norm_bwd(x_ref[:, sl], g_ref[...], dy_ref[:, sl])
            dx_ref[:, sl] = dx.astype(BF16)
            tot = tot + dgn
        dg_ref[...] = tot

    return pl.pallas_call(
        body, name="mem_k_norm_bwd",
        out_shape=[jax.ShapeDtypeStruct((R, W), BF16), jax.ShapeDtypeStruct((1, HEAD_DIM), F32)])(x, gain, dy)


def _cumsum_rows(name, xs, reverse):
    T, W = xs[0].shape
    tb = _tile(T, 512, 8)
    nb = T // tb

    def body(*refs):
        o_ref, carry = refs[len(xs)], refs[len(xs) + 1]

        @pl.when(pl.program_id(0) == 0)
        def _():
            carry[...] = jnp.zeros_like(carry)

        xv = refs[0][...]
        for x_ref in refs[1:len(xs)]:
            xv = xv + x_ref[...]
        r = lax.broadcasted_iota(jnp.int32, (tb, tb), 0)
        cc = lax.broadcasted_iota(jnp.int32, (tb, tb), 1)
        tri = jnp.where((cc >= r) if reverse else (cc <= r), 1.0, 0.0).astype(F32)
        o_ref[...] = jnp.dot(tri, xv, precision=lax.Precision.HIGHEST, preferred_element_type=F32) + carry[...]
        carry[...] += jnp.sum(xv, axis=0, keepdims=True)

    idx = (lambda i: (nb - 1 - i, 0)) if reverse else (lambda i: (i, 0))
    return pl.pallas_call(
        body, name=name, grid=(nb,), in_specs=[pl.BlockSpec((tb, W), idx)] * len(xs), out_specs=pl.BlockSpec((tb, W), idx),
        out_shape=jax.ShapeDtypeStruct((T, W), F32), scratch_shapes=[pltpu.VMEM((1, W), F32)],
        compiler_params=_params(("arbitrary",)),
    )(*xs)


def _triangle(nq, by_column):
    if by_column:
        blocks = [(i, j) for j in range(nq) for i in range(j, nq)]
    else:
        blocks = [(i, j) for i in range(nq) for j in range(i + 1)]
    return jnp.asarray(np.array(blocks, np.int32).T)


def _fox_scores_t(k, q, cq_row, ck_rep, on_diagonal):
    n = q.shape[0]
    s = _dot(k, q, "nt") * SCALE + (cq_row - jnp.tile(ck_rep, (1, n // HEAD_DIM)))
    if on_diagonal:
        s = jnp.where(lax.broadcasted_iota(jnp.int32, (n, n), 0) <= lax.broadcasted_iota(jnp.int32, (n, n), 1), s, NEG_INF)
    return s


def _fox_fwd(qkv, v_t, cq_row, ck_rep):
    T = qkv.shape[0]
    tq = _tile(T, 512)
    nq = T // tq
    steps = nq * (nq + 1) // 2
    HQ, HK = C_FQ // HEAD_DIM, C_FK // HEAD_DIM

    def body(tab, q_ref, k_ref, vt_ref, cq_ref, ck_ref, o_ref, of_ref, lse_ref, m_sc, l_sc, acc_sc):
        i, j = tab[0, pl.program_id(1)], tab[1, pl.program_id(1)]

        @pl.when(j == 0)
        def _():
            m_sc[...] = jnp.full_like(m_sc, NEG_INF)
            l_sc[...] = jnp.zeros_like(l_sc)
            acc_sc[...] = jnp.zeros_like(acc_sc)

        def step(on_diagonal):
            s = _fox_scores_t(k_ref[...], q_ref[...], cq_ref[...], ck_ref[...], on_diagonal)
            m_new = jnp.maximum(m_sc[...], jnp.max(s, axis=0, keepdims=True))
            alpha = jnp.exp(m_sc[...] - m_new)
            p = jnp.exp(s - m_new)
            l_sc[...] = alpha * l_sc[...] + jnp.sum(p, axis=0, keepdims=True)
            acc_sc[...] = alpha * acc_sc[...] + _dot(vt_ref[...], p.astype(BF16), "nn")
            m_sc[...] = m_new

        @pl.when(j < i)
        def _():
            step(False)

        @pl.when(j == i)
        def _():
            step(True)
            o = (acc_sc[...] / l_sc[...]).T
            o_ref[...] = o.astype(BF16)
            of_ref[...] = o
            lse_ref[...] = m_sc[...] + jnp.log(l_sc[...])

    qrow = pl.BlockSpec((None, 1, tq), lambda h, s, tab: (h, 0, tab[0, s]))
    return pl.pallas_call(
        body, name="fox_fwd",
        grid_spec=pltpu.PrefetchScalarGridSpec(
            num_scalar_prefetch=1, grid=(FOX_HEADS, steps),
            in_specs=[pl.BlockSpec((tq, HEAD_DIM), lambda h, s, tab: (tab[0, s], HQ + h)),
                      pl.BlockSpec((tq, HEAD_DIM), lambda h, s, tab: (tab[1, s], HK + h)),
                      pl.BlockSpec((HEAD_DIM, tq), lambda h, s, tab: (h, tab[1, s])), qrow,
                      pl.BlockSpec((None, tq, HEAD_DIM), lambda h, s, tab: (h, tab[1, s], 0))],
            out_specs=[pl.BlockSpec((tq, HEAD_DIM), lambda h, s, tab: (tab[0, s], h)),
                       pl.BlockSpec((tq, HEAD_DIM), lambda h, s, tab: (tab[0, s], h)), qrow],
            scratch_shapes=[pltpu.VMEM((1, tq), F32), pltpu.VMEM((1, tq), F32), pltpu.VMEM((HEAD_DIM, tq), F32)]),
        out_shape=[jax.ShapeDtypeStruct((T, FOX_W), BF16), jax.ShapeDtypeStruct((T, FOX_W), F32),
                   jax.ShapeDtypeStruct((FOX_HEADS, 1, T), F32)],
        compiler_params=_params(("parallel", "arbitrary")),
    )(_triangle(nq, False), qkv, qkv, v_t, cq_row, ck_rep)


def _fox_delta(dmix, out_f32):
    T = out_f32.shape[0]
    tr = _tile(T, 512, 8)

    def body(do_ref, o_ref, d_ref):
        lane = lax.broadcasted_iota(jnp.int32, (tr, HEAD_DIM), 1)
        acc = jnp.zeros((tr, HEAD_DIM), F32)
        for hh in range(FOX_HEADS):
            sl = slice(hh * HEAD_DIM, (hh + 1) * HEAD_DIM)
            d = jnp.sum(do_ref[:, sl].astype(F32) * o_ref[:, sl], axis=-1, keepdims=True)
            acc = jnp.where(lane == hh, d, acc)
        d_ref[...] = acc

    blk = pl.BlockSpec((tr, FOX_W), lambda i: (i, 0))
    return pl.pallas_call(
        body, name="fox_delta", grid=(T // tr,), in_specs=[blk, blk], out_specs=pl.BlockSpec((tr, HEAD_DIM), lambda i: (i, 0)),
        out_shape=jax.ShapeDtypeStruct((T, HEAD_DIM), F32), compiler_params=_params(("parallel",)),
    )(dmix, out_f32)


def _fox_bwd(qkv, k_t, cq_row, ck_rep, delta_row, lse, dmix):
    T = qkv.shape[0]
    tq = _tile(T, 512)
    nq = T // tq
    steps = nq * (nq + 1) // 2
    HQ, HK, HV = C_FQ // HEAD_DIM, C_FK // HEAD_DIM, C_FV // HEAD_DIM

    def body(tab, q_ref, k_ref, kt_ref, v_ref, cq_ref, ck_ref, delta_ref, lse_ref, do_ref,
             dq_ref, dk_ref, dv_ref, dck_ref, dcq_ref, dk_sc, dv_sc, dc_sc, dqt_sc):
        qi, kj = tab[0, pl.program_id(1)], tab[1, pl.program_id(1)]

        @pl.when(qi == kj)
        def _():
            dk_sc[...] = jnp.zeros_like(dk_sc)
            dv_sc[...] = jnp.zeros_like(dv_sc)
            dc_sc[...] = jnp.zeros_like(dc_sc)

        def step(on_diagonal):
            q, k, v, do = q_ref[...], k_ref[...], v_ref[...], do_ref[...]
            p = jnp.exp(_fox_scores_t(k, q, cq_ref[...], ck_ref[...], on_diagonal) - lse_ref[...])
            dp = _dot(v, do, "nt")
            ds = p * (dp - delta_ref[...])
            dsb = ds.astype(BF16)
            dv_sc[...] += _dot(p.astype(BF16), do, "nn")
            dk_sc[...] += _dot(dsb, q, "nn")
            dc_sc[...] += jnp.sum(ds, axis=1, keepdims=True)
            dq_part = _dot(kt_ref[...], dsb, "nn") * SCALE
            dcq_part = jnp.sum(ds, axis=0, keepdims=True)

            @pl.when(kj == 0)
            def _():
                dqt_sc[qi] = dq_part
                dcq_ref[qi] = dcq_part

            @pl.when(kj > 0)
            def _():
                dqt_sc[qi] += dq_part
                dcq_ref[qi] += dcq_part

            if on_diagonal:
                dq_ref[...] = dqt_sc[qi].T

        @pl.when(qi > kj)
        def _():
            step(False)

        @pl.when(qi == kj)
        def _():
            step(True)

        @pl.when(qi == nq - 1)
        def _():
            dk_ref[...] = dk_sc[...] * SCALE
            dv_ref[...] = dv_sc[...]
            dck_ref[...] = -dc_sc[...]

    def rows(base):
        return pl.BlockSpec((tq, HEAD_DIM), lambda h, s, tab: (tab[0, s], base + h))

    def cols(base):
        return pl.BlockSpec((tq, HEAD_DIM), lambda h, s, tab: (tab[1, s], base + h))

    qrow = pl.BlockSpec((None, 1, tq), lambda h, s, tab: (h, 0, tab[0, s]))
    sds = jax.ShapeDtypeStruct((T, FOX_W), F32)
    return pl.pallas_call(
        body, name="fox_bwd",
        grid_spec=pltpu.PrefetchScalarGridSpec(
            num_scalar_prefetch=1, grid=(FOX_HEADS, steps),
            in_specs=[rows(HQ), cols(HK), pl.BlockSpec((HEAD_DIM, tq), lambda h, s, tab: (h, tab[1, s])), cols(HV), qrow,
                      pl.BlockSpec((None, tq, HEAD_DIM), lambda h, s, tab: (h, tab[1, s], 0)), qrow, qrow, rows(0)],
            out_specs=[cols(0), cols(0), cols(0), pl.BlockSpec((None, tq, 1), lambda h, s, tab: (h, tab[1, s], 0)),
                       pl.BlockSpec((None, nq, 1, tq), lambda h, s, tab: (h, 0, 0, 0))],
            scratch_shapes=[pltpu.VMEM((tq, HEAD_DIM), F32), pltpu.VMEM((tq, HEAD_DIM), F32), pltpu.VMEM((tq, 1), F32),
                            pltpu.VMEM((nq, HEAD_DIM, tq), F32)]),
        out_shape=[sds, sds, sds, jax.ShapeDtypeStruct((FOX_HEADS, T, 1), F32), jax.ShapeDtypeStruct((FOX_HEADS, nq, 1, tq), F32)],
        compiler_params=_params(("parallel", "arbitrary")),
    )(_triangle(nq, True), qkv, qkv, k_t, qkv, cq_row, ck_rep, delta_row, lse, dmix)


GW = SWA_GROUP * HEAD_DIM
GR = SWA_GROUP * WINDOW


def _swa_scores(q_ref, kp_ref, kc_ref, slope_ref, n):
    q = q_ref[...]
    qs = jnp.concatenate([q[:, t * HEAD_DIM:(t + 1) * HEAD_DIM] for t in range(SWA_GROUP)], axis=0)
    kb = jnp.concatenate([kp_ref[...], kc_ref[...]], axis=0)
    r = lax.broadcasted_iota(jnp.int32, (GR, 2 * WINDOW), 0) & (WINDOW - 1)
    jj = lax.broadcasted_iota(jnp.int32, (GR, 2 * WINDOW), 1)
    dist = WINDOW + r - jj
    valid = (dist >= 0) & (dist < WINDOW) & ((n > 0) | (jj >= WINDOW))
    s = _dot(qs, kb, "nt") * SCALE - slope_ref[...] * dist.astype(F32)
    return qs, kb, jnp.where(valid, s, NEG_INF), valid


def _swa_specs():
    HQ, HK, HV = C_SQ // GW, C_SK // HEAD_DIM, C_SV // HEAD_DIM
    q_spec = pl.BlockSpec((WINDOW, GW), lambda g, n: (n, HQ + g))

    def prev(base):
        return pl.BlockSpec((WINDOW, HEAD_DIM), lambda g, n: (jnp.maximum(n - 1, 0), base + g))

    def cur(base):
        return pl.BlockSpec((WINDOW, HEAD_DIM), lambda g, n: (n, base + g))

    col = pl.BlockSpec((None, GR, 1), lambda g, n: (g, 0, 0))
    return q_spec, prev(HK), cur(HK), prev(HV), cur(HV), col


def _swa_fwd(qkv, slopes, sinks):
    T = qkv.shape[0]
    nb = T // WINDOW
    assert C_SQ % GW == 0

    def body(q_ref, kp_ref, kc_ref, vp_ref, vc_ref, slope_ref, sink_ref, o_ref, lse_ref):
        n = pl.program_id(1)
        _, _, s, _ = _swa_scores(q_ref, kp_ref, kc_ref, slope_ref, n)
        m = jnp.maximum(jnp.max(s, axis=-1, keepdims=True), sink_ref[...])
        p = jnp.exp(s - m)
        l = jnp.sum(p, axis=-1, keepdims=True) + jnp.exp(sink_ref[...] - m)
        vb = jnp.concatenate([vp_ref[...], vc_ref[...]], axis=0)
        o = _dot(p.astype(BF16), vb, "nn") / l
        for t in range(SWA_GROUP):
            o_ref[:, t * HEAD_DIM:(t + 1) * HEAD_DIM] = o[t * WINDOW:(t + 1) * WINDOW, :].astype(BF16)
        lse_ref[...] = m + jnp.log(l)

    q_spec, kp, kc, vp, vc, col = _swa_specs()
    return pl.pallas_call(
        body, name="swa_fwd", grid=(SWA_KV_HEADS, nb), in_specs=[q_spec, kp, kc, vp, vc, col, col],
        out_specs=[pl.BlockSpec((WINDOW, GW), lambda g, n: (n, g)), pl.BlockSpec((None, None, GR, 1), lambda g, n: (g, n, 0, 0))],
        out_shape=[jax.ShapeDtypeStruct((T, SWA_HEADS * HEAD_DIM), BF16), jax.ShapeDtypeStruct((SWA_KV_HEADS, nb, GR, 1), F32)],
        compiler_params=_params(("parallel", "arbitrary")),
    )(qkv, qkv, qkv, qkv, qkv, slopes, sinks)


def _swa_bwd(qkv, slopes, sinks, out, lse, dmix):
    T = qkv.shape[0]
    nb = T // WINDOW
    DO = FOX_W // GW
    assert FOX_W % GW == 0

    def body(q_ref, kp_ref, kc_ref, vp_ref, vc_ref, slope_ref, sink_ref, o_ref, lse_ref, do_ref,
             dq_ref, dk_ref, dv_ref, dsink_ref, sink_sc):
        n = pl.program_id(1)

        @pl.when(n == 0)
        def _():
            dk_ref[...] = jnp.zeros_like(dk_ref)
            dv_ref[...] = jnp.zeros_like(dv_ref)
            sink_sc[...] = jnp.zeros_like(sink_sc)

        qs, kb, s, valid = _swa_scores(q_ref, kp_ref, kc_ref, slope_ref, n)
        lse = lse_ref[...]
        p = jnp.where(valid, jnp.exp(s - lse), 0.0)
        vb = jnp.concatenate([vp_ref[...], vc_ref[...]], axis=0)
        do = jnp.concatenate([do_ref[:, t * HEAD_DIM:(t + 1) * HEAD_DIM] for t in range(SWA_GROUP)], axis=0)
        oo = jnp.concatenate([o_ref[:, t * HEAD_DIM:(t + 1) * HEAD_DIM] for t in range(SWA_GROUP)], axis=0)
        dp = _dot(do, vb, "nt")
        delta = jnp.sum(do.astype(F32) * oo.astype(F32), axis=-1, keepdims=True)
        ds = p * (dp - delta)
        dsb = ds.astype(BF16)
        dq = _dot(dsb, kb, "nn") * SCALE
        for t in range(SWA_GROUP):
            dq_ref[:, t * HEAD_DIM:(t + 1) * HEAD_DIM] = dq[t * WINDOW:(t + 1) * WINDOW, :]
        dkb = _dot(dsb, qs, "tn") * SCALE
        dvb = _dot(p.astype(BF16), do, "tn")
        r_prev = pl.ds(pl.multiple_of(jnp.maximum(n - 1, 0) * WINDOW, WINDOW), WINDOW)
        r_cur = pl.ds(pl.multiple_of(n * WINDOW, WINDOW), WINDOW)
        dk_ref[r_prev, :] += dkb[:WINDOW, :]
        dk_ref[r_cur, :] += dkb[WINDOW:, :]
        dv_ref[r_prev, :] += dvb[:WINDOW, :]
        dv_ref[r_cur, :] += dvb[WINDOW:, :]
        sink_sc[...] -= jnp.exp(sink_ref[...] - lse) * delta

        @pl.when(n == nb - 1)
        def _():
            tot = [jnp.zeros((1, 128), F32) + jnp.sum(sink_sc[t * WINDOW:(t + 1) * WINDOW, :]) for t in range(SWA_GROUP)]
            dsink_ref[...] = jnp.concatenate(tot + [jnp.zeros((8 - SWA_GROUP, 128), F32)], axis=0)

    q_spec, kp, kc, vp, vc, col = _swa_specs()
    kv_acc = pl.BlockSpec((T, HEAD_DIM), lambda g, n: (0, g))
    return pl.pallas_call(
        body, name="swa_bwd", grid=(SWA_KV_HEADS, nb),
        in_specs=[q_spec, kp, kc, vp, vc, col, col, pl.BlockSpec((WINDOW, GW), lambda g, n: (n, g)),
                  pl.BlockSpec((None, None, GR, 1), lambda g, n: (g, n, 0, 0)), pl.BlockSpec((WINDOW, GW), lambda g, n: (n, DO + g))],
        out_specs=[pl.BlockSpec((WINDOW, GW), lambda g, n: (n, g)), kv_acc, kv_acc, pl.BlockSpec((None, 8, 128), lambda g, n: (g, 0, 0))],
        out_shape=[jax.ShapeDtypeStruct((T, SWA_HEADS * HEAD_DIM), F32), jax.ShapeDtypeStruct((T, SWA_KV_HEADS * HEAD_DIM), F32),
                   jax.ShapeDtypeStruct((T, SWA_KV_HEADS * HEAD_DIM), F32), jax.ShapeDtypeStruct((SWA_KV_HEADS, 8, 128), F32)],
        scratch_shapes=[pltpu.VMEM((GR, 1), F32)],
        compiler_params=_params(("parallel", "arbitrary")),
    )(qkv, qkv, qkv, qkv, qkv, slopes, sinks, out, lse, dmix)


def _mem_fwd(qkv, mk, mv):
    T, ML = qkv.shape[0], mk.shape[0]
    tq = _tile(T, 512)
    HQ = C_MQ // HEAD_DIM

    def body(q_ref, k_ref, v_ref, o_ref, lse_ref):
        s = _dot(q_ref[...], k_ref[...], "nt") * SCALE
        m = jnp.max(s, axis=-1, keepdims=True)
        p = jnp.exp(s - m)
        l = jnp.sum(p, axis=-1, keepdims=True)
        o_ref[...] = (_dot(p.astype(BF16), v_ref[...], "nn") / l).astype(BF16)
        lse_ref[...] = m + jnp.log(l)

    kv = pl.BlockSpec((ML, HEAD_DIM), lambda h, i: (0, h))
    return pl.pallas_call(
        body, name="mem_fwd", grid=(MEM_HEADS, T // tq),
        in_specs=[pl.BlockSpec((tq, HEAD_DIM), lambda h, i: (i, HQ + h)), kv, kv],
        out_specs=[pl.BlockSpec((tq, HEAD_DIM), lambda h, i: (i, h)), pl.BlockSpec((None, tq, 1), lambda h, i: (h, i, 0))],
        out_shape=[jax.ShapeDtypeStruct((T, MEM_HEADS * HEAD_DIM), BF16), jax.ShapeDtypeStruct((MEM_HEADS, T, 1), F32)],
        compiler_params=_params(("parallel", "arbitrary")),
    )(qkv, mk, mv)


def _mem_bwd(qkv, mk, mv, out, lse, dmix):
    T, ML = qkv.shape[0], mk.shape[0]
    tq = _tile(T, 512)
    HQ = C_MQ // HEAD_DIM
    DO = (FOX_W + SWA_HEADS * HEAD_DIM) // HEAD_DIM

    def body(q_ref, k_ref, v_ref, o_ref, lse_ref, do_ref, dq_ref, dk_ref, dv_ref):
        q, k, v, do = q_ref[...], k_ref[...], v_ref[...], do_ref[...]
        p = jnp.exp(_dot(q, k, "nt") * SCALE - lse_ref[...])
        dp = _dot(do, v, "nt")
        delta = jnp.sum(do.astype(F32) * o_ref[...].astype(F32), axis=-1, keepdims=True)
        dsb = (p * (dp - delta)).astype(BF16)
        dq_ref[...] = _dot(dsb, k, "nn") * SCALE
        dk_part = _dot(dsb, q, "tn") * SCALE
        dv_part = _dot(p.astype(BF16), do, "tn")

        @pl.when(pl.program_id(1) == 0)
        def _():
            dk_ref[...] = dk_part
            dv_ref[...] = dv_part

        @pl.when(pl.program_id(1) > 0)
        def _():
            dk_ref[...] += dk_part
            dv_ref[...] += dv_part

    kv = pl.BlockSpec((ML, HEAD_DIM), lambda h, i: (0, h))
    qb = pl.BlockSpec((tq, HEAD_DIM), lambda h, i: (i, h))
    return pl.pallas_call(
        body, name="mem_bwd", grid=(MEM_HEADS, T // tq),
        in_specs=[pl.BlockSpec((tq, HEAD_DIM), lambda h, i: (i, HQ + h)), kv, kv, qb,
                  pl.BlockSpec((None, tq, 1), lambda h, i: (h, i, 0)), pl.BlockSpec((tq, HEAD_DIM), lambda h, i: (i, DO + h))],
        out_specs=[qb, kv, kv],
        out_shape=[jax.ShapeDtypeStruct((T, MEM_HEADS * HEAD_DIM), F32), jax.ShapeDtypeStruct((ML, MEM_HEADS * HEAD_DIM), F32),
                   jax.ShapeDtypeStruct((ML, MEM_HEADS * HEAD_DIM), F32)],
        compiler_params=_params(("parallel", "arbitrary")),
    )(qkv, mk, mv, out, lse, dmix)


HBM = pl.BlockSpec(memory_space=pltpu.HBM)


def _place():
    x, y, c = lax.axis_index("x"), lax.axis_index("y"), lax.axis_index("c")
    chips = [(1 - x, y), (x, 1 - y), (1 - x, 1 - y)]
    return x, y, c, chips


def _remote(src, dst, send_sem, recv_sem, device):
    return pltpu.make_async_remote_copy(src_ref=src, dst_ref=dst, send_sem=send_sem, recv_sem=recv_sem,
                                        device_id=device, device_id_type=MESH)


def _place_ids():
    x, y, c = lax.axis_index("x"), lax.axis_index("y"), lax.axis_index("c")
    order = [2 * x + y, 2 * (1 - x) + y, 2 * x + (1 - y), 2 * (1 - x) + (1 - y)]
    return jnp.stack([2 * x + y, c] + order).astype(jnp.int32)


def _cast_place(name, w, ids):
    R, C = w.shape
    tr = _tile(R, 256, 16)

    def body(ids_ref, w_ref, o_ref):
        o_ref[...] = w_ref[...].astype(BF16)

    return pl.pallas_call(
        body, name=name,
        grid_spec=pltpu.PrefetchScalarGridSpec(
            num_scalar_prefetch=1, grid=(R // tr,), in_specs=[pl.BlockSpec((tr, C), lambda i, ids: (i, 0))],
            out_specs=pl.BlockSpec((None, tr, C), lambda i, ids: (ids[0], i, 0))),
        out_shape=jax.ShapeDtypeStruct((N_CHIPS, R, C), BF16), compiler_params=_params(("parallel",)),
    )(ids, w)


SEM = pl.BlockSpec(memory_space=pltpu.SEMAPHORE)
EFFECT = pltpu.SideEffectType.DATAFLOW_SIDE_EFFECTING


def _hbm(a):
    return pltpu.with_memory_space_constraint(a, pltpu.HBM)


def _gather_start(name, placed, after):
    n = len(placed)

    ns = 3 * n

    def body(*refs):
        send, recv = refs[n + 1:n + 1 + ns], refs[n + 1 + ns:n + 1 + 2 * ns]
        buf = refs[n + 1 + 2 * ns:2 * n + 1 + 2 * ns]
        token = refs[2 * n + 1 + 2 * ns]
        x, y, c, chips = _place()
        me = 2 * x + y
        for a in range(n):
            half = buf[a].shape[1] // 2
            mine = buf[a].at[me, pl.ds(c * half, half)]
            for j, (cx, cy) in enumerate(chips):
                _remote(mine, mine, send[3 * a + j], recv[3 * a + j], (cx, cy, c)).start()
        token[...] = jnp.zeros_like(token)

    res = pl.pallas_call(
        body, name=name, in_specs=[HBM] * n + [pl.BlockSpec(memory_space=pl.ANY)],
        out_specs=[SEM] * (2 * ns) + [HBM] * n + [pl.BlockSpec(memory_space=pltpu.VMEM)],
        out_shape=[pltpu.SemaphoreType.DMA(())] * (2 * ns)
        + [pltpu.HBM(s.shape, s.dtype) for s in placed] + [jax.ShapeDtypeStruct((8, 128), F32)],
        input_output_aliases={a: 2 * ns + a for a in range(n)},
        compiler_params=pltpu.CompilerParams(has_side_effects=EFFECT),
    )(*[_hbm(s) for s in placed], after)
    return list(res[:ns]), list(res[ns:2 * ns]), list(res[2 * ns:2 * ns + n]), res[2 * ns + n]


def _gather_wait(name, send, recv, bufs, after):
    n = len(bufs)

    ns = 3 * n

    def body(*refs):
        buf = refs[:n]
        send_ref, recv_ref = refs[n:n + ns], refs[n + ns:n + 2 * ns]
        x, y, c, chips = _place()
        ids = [2 * cx + cy for cx, cy in chips]
        for a in range(n):
            half = buf[a].shape[1] // 2
            for j in range(3):
                landed = buf[a].at[ids[j], pl.ds(c * half, half)]
                cp = _remote(landed, landed, send_ref[3 * a + j], recv_ref[3 * a + j], (x, y, c))
                cp.wait_send()
                cp.wait_recv()

    res = pl.pallas_call(
        body, name=name, in_specs=[HBM] * n + [SEM] * (2 * ns) + [pl.BlockSpec(memory_space=pl.ANY)], out_specs=[HBM] * n,
        out_shape=[pltpu.HBM(s.shape, s.dtype) for s in bufs], input_output_aliases={a: a for a in range(n)},
        compiler_params=pltpu.CompilerParams(has_side_effects=EFFECT),
    )(*bufs, *send, *recv, after)
    return list(res)


def _gather_forward(name, bufs):
    n = len(bufs)

    def body(*refs):
        buf = refs[n:2 * n]
        send, recv = refs[2 * n:]
        x, y, c, chips = _place()
        ids = [2 * cx + cy for cx, cy in chips]
        copies = []
        for a in range(n):
            half = buf[a].shape[1] // 2
            for j in range(3):
                landed = buf[a].at[ids[j], pl.ds(c * half, half)]
                cp = _remote(landed, landed, send.at[a, j], recv.at[a, j], (x, y, 1 - c))
                cp.start()
                copies.append(cp)
        for a in range(n):
            half = buf[a].shape[1] // 2
            for j in range(3):
                landed = buf[a].at[ids[j], pl.ds((1 - c) * half, half)]
                _remote(landed, landed, send.at[a, j], recv.at[a, j], (x, y, c)).wait_recv()
        for cp in copies:
            cp.wait_send()

    return pl.pallas_call(
        body, name=name, in_specs=[HBM] * n, out_specs=[HBM] * n,
        out_shape=[jax.ShapeDtypeStruct(s.shape, s.dtype) for s in bufs], input_output_aliases={a: a for a in range(n)},
        scratch_shapes=[pltpu.SemaphoreType.DMA((n, 3)), pltpu.SemaphoreType.DMA((n, 3))],
    )(*bufs)


def _pair_start(name, grads):
    n = len(grads)
    ns = N_CHIPS * n

    def body(*refs):
        send, recv = refs[2 * n:2 * n + ns], refs[2 * n + ns:2 * n + 2 * ns]
        src = refs[2 * n + 2 * ns:3 * n + 2 * ns]
        land = refs[3 * n + 2 * ns:4 * n + 2 * ns]
        token = refs[4 * n + 2 * ns]
        x, y, c, chips = _place()
        order = [2 * x + y] + [2 * cx + cy for cx, cy in chips]
        for a in range(n):
            half = src[a].shape[1] // 2
            for j in range(N_CHIPS):
                _remote(src[a].at[order[j], pl.ds((1 - c) * half, half)], land[a].at[j],
                        send[N_CHIPS * a + j], recv[N_CHIPS * a + j], (x, y, 1 - c)).start()
        token[...] = jnp.zeros_like(token)

    lands = [jax.ShapeDtypeStruct((N_CHIPS, g.shape[1] // 2, g.shape[2]), g.dtype) for g in grads]
    res = pl.pallas_call(
        body, name=name, in_specs=[HBM] * (2 * n),
        out_specs=[SEM] * (2 * ns) + [HBM] * (2 * n) + [pl.BlockSpec(memory_space=pltpu.VMEM)],
        out_shape=[pltpu.SemaphoreType.DMA(())] * (2 * ns) + [pltpu.HBM(g.shape, g.dtype) for g in grads]
        + [pltpu.HBM(l.shape, l.dtype) for l in lands] + [jax.ShapeDtypeStruct((8, 128), F32)],
        input_output_aliases={a: 2 * ns + a for a in range(2 * n)},
        compiler_params=pltpu.CompilerParams(has_side_effects=EFFECT),
    )(*[_hbm(g) for g in grads], *[_hbm(lax.empty(l.shape, l.dtype)) for l in lands])
    return list(res[:ns]), list(res[ns:2 * ns]), list(res[2 * ns:2 * ns + n]), list(res[2 * ns + n:2 * ns + 2 * n]), res[2 * ns + 2 * n]


def _pair_wait(name, send, recv, grads, lands, after):
    n = len(grads)
    ns = N_CHIPS * n

    def body(*refs):
        src, land = refs[:n], refs[n:2 * n]
        send_ref, recv_ref = refs[2 * n:2 * n + ns], refs[2 * n + ns:2 * n + 2 * ns]
        x, y, c, _ = _place()
        for a in range(n):
            for j in range(N_CHIPS):
                cp = _remote(land[a].at[j], land[a].at[j], send_ref[N_CHIPS * a + j], recv_ref[N_CHIPS * a + j], (x, y, c))
                cp.wait_send()
                cp.wait_recv()

    res = pl.pallas_call(
        body, name=name, in_specs=[HBM] * (2 * n) + [SEM] * (2 * ns) + [pl.BlockSpec(memory_space=pl.ANY)],
        out_specs=[HBM] * (2 * n), out_shape=[pltpu.HBM(g.shape, g.dtype) for g in grads] + [pltpu.HBM(l.shape, l.dtype) for l in lands],
        input_output_aliases={a: a for a in range(2 * n)},
        compiler_params=pltpu.CompilerParams(has_side_effects=EFFECT),
    )(*grads, *lands, *send, *recv, after)
    return list(res[:n]), list(res[n:])


def _chip_start(name, parts):
    n = len(parts)
    ns = 3 * n

    def body(*refs):
        send, recv = refs[2 * n:2 * n + ns], refs[2 * n + ns:2 * n + 2 * ns]
        src = refs[2 * n + 2 * ns:3 * n + 2 * ns]
        land = refs[3 * n + 2 * ns:4 * n + 2 * ns]
        token = refs[4 * n + 2 * ns]
        x, y, c, chips = _place()
        for a in range(n):
            for j, (cx, cy) in enumerate(chips):
                _remote(src[a].at[j], land[a].at[j], send[3 * a + j], recv[3 * a + j], (cx, cy, c)).start()
        token[...] = jnp.zeros_like(token)

    res = pl.pallas_call(
        body, name=name, in_specs=[HBM] * (2 * n),
        out_specs=[SEM] * (2 * ns) + [HBM] * (2 * n) + [pl.BlockSpec(memory_space=pltpu.VMEM)],
        out_shape=[pltpu.SemaphoreType.DMA(())] * (2 * ns) + [pltpu.HBM(p.shape, p.dtype) for p in parts] * 2
        + [jax.ShapeDtypeStruct((8, 128), F32)],
        input_output_aliases={a: 2 * ns + a for a in range(2 * n)},
        compiler_params=pltpu.CompilerParams(has_side_effects=EFFECT),
    )(*[_hbm(p) for p in parts], *[_hbm(lax.empty(p.shape, p.dtype)) for p in parts])
    return list(res[:ns]), list(res[ns:2 * ns]), list(res[2 * ns:2 * ns + n]), list(res[2 * ns + n:2 * ns + 2 * n]), res[2 * ns + 2 * n]


def _chip_wait(name, send, recv, parts, lands, after):
    n = len(parts)
    ns = 3 * n

    def body(*refs):
        src, land = refs[:n], refs[n:2 * n]
        send_ref, recv_ref = refs[2 * n:2 * n + ns], refs[2 * n + ns:2 * n + 2 * ns]
        x, y, c, _ = _place()
        for a in range(n):
            for j in range(3):
                cp = _remote(src[a].at[j], land[a].at[j], send_ref[3 * a + j], recv_ref[3 * a + j], (x, y, c))
                cp.wait_send()
                cp.wait_recv()

    res = pl.pallas_call(
        body, name=name, in_specs=[HBM] * (2 * n) + [SEM] * (2 * ns) + [pl.BlockSpec(memory_space=pl.ANY)],
        out_specs=[HBM] * (2 * n), out_shape=[pltpu.HBM(p.shape, p.dtype) for p in parts] * 2,
        input_output_aliases={a: a for a in range(2 * n)},
        compiler_params=pltpu.CompilerParams(has_side_effects=EFFECT),
    )(*parts, *lands, *send, *recv, after)
    return list(res[n:])


def _pair_share(name, shards):
    n = len(shards)

    def body(*refs):
        buf = refs[n:2 * n]
        send, recv = refs[2 * n:]
        x, y, c, _ = _place()
        copies = []
        for a in range(n):
            half = buf[a].shape[0] // 2
            mine = buf[a].at[pl.ds(c * half, half)]
            cp = _remote(mine, mine, send.at[a], recv.at[a], (x, y, 1 - c))
            cp.start()
            copies.append(cp)
        for a, cp in enumerate(copies):
            half = buf[a].shape[0] // 2
            cp.wait_send()
            theirs = buf[a].at[pl.ds((1 - c) * half, half)]
            _remote(theirs, theirs, send.at[a], recv.at[a], (x, y, c)).wait_recv()

    return pl.pallas_call(
        body, name=name, in_specs=[HBM] * n, out_specs=[HBM] * n,
        out_shape=[jax.ShapeDtypeStruct(s.shape, s.dtype) for s in shards], input_output_aliases={a: a for a in range(n)},
        scratch_shapes=[pltpu.SemaphoreType.DMA((n,)), pltpu.SemaphoreType.DMA((n,))],
    )(*shards)


def _small_start(buf):
    R, W = buf.shape
    ns = N_DEV - 1

    def body(*refs):
        send, recv = refs[2:2 + ns], refs[2 + ns:2 + 2 * ns]
        src, land, token = refs[2 + 2 * ns], refs[3 + 2 * ns], refs[4 + 2 * ns]
        x, y, c, _ = _place()
        me = 4 * x + 2 * y + c
        for k in range(1, N_DEV):
            peer = (x ^ (k >> 2), y ^ ((k >> 1) & 1), c ^ (k & 1))
            _remote(src, land.at[me], send[k - 1], recv[k - 1], peer).start()
        token[...] = jnp.zeros_like(token)

    res = pl.pallas_call(
        body, name="small_start", in_specs=[HBM, HBM],
        out_specs=[SEM] * (2 * ns) + [HBM, HBM, pl.BlockSpec(memory_space=pltpu.VMEM)],
        out_shape=[pltpu.SemaphoreType.DMA(())] * (2 * ns) + [pltpu.HBM((R, W), F32), pltpu.HBM((N_DEV, R, W), F32),
                                                                jax.ShapeDtypeStruct((8, 128), F32)],
        input_output_aliases={0: 2 * ns, 1: 2 * ns + 1},
        compiler_params=pltpu.CompilerParams(has_side_effects=EFFECT),
    )(_hbm(buf), _hbm(jnp.zeros((N_DEV, R, W), F32)))
    return list(res[:ns]), list(res[ns:2 * ns]), res[2 * ns], res[2 * ns + 1], res[2 * ns + 2]


def _small_wait(send, recv, buf, land, after):
    ns = N_DEV - 1

    def body(*refs):
        land_ref = refs[1]
        send_ref, recv_ref = refs[2:2 + ns], refs[2 + ns:2 + 2 * ns]
        x, y, c, _ = _place()
        me = 4 * x + 2 * y + c
        for k in range(1, N_DEV):
            landed = land_ref.at[me ^ k]
            cp = _remote(landed, landed, send_ref[k - 1], recv_ref[k - 1], (x, y, c))
            cp.wait_send()
            cp.wait_recv()

    return pl.pallas_call(
        body, name="small_wait", in_specs=[HBM, HBM] + [SEM] * (2 * ns) + [pl.BlockSpec(memory_space=pl.ANY)],
        out_specs=[HBM, HBM], out_shape=[pltpu.HBM(buf.shape, buf.dtype), pltpu.HBM(land.shape, land.dtype)],
        input_output_aliases={0: 0, 1: 1}, compiler_params=pltpu.CompilerParams(has_side_effects=EFFECT),
    )(buf, land, *send, *recv, after)


def _small_sum(buf, land):
    def body(buf_ref, land_ref, out_ref):
        x, y, c, _ = _place()
        me = 4 * x + 2 * y + c
        total = None
        for d in range(N_DEV):
            term = jnp.where(me == d, buf_ref[...], land_ref[d])
            total = term if total is None else total + term
        out_ref[...] = total

    return pl.pallas_call(body, name="small_sum", out_shape=jax.ShapeDtypeStruct(buf.shape, F32))(buf, land)


def _pair_sum_bf16(name, grad, theirs, ids):
    _, R2, C = theirs.shape
    tr = _tile(R2, 256, 16)
    nrb = R2 // tr

    def body(ids_ref, a_ref, b_ref, o_ref):
        o_ref[...] = (a_ref[...] + b_ref[...]).astype(BF16)

    return pl.pallas_call(
        body, name=name,
        grid_spec=pltpu.PrefetchScalarGridSpec(
            num_scalar_prefetch=1, grid=(3, nrb),
            in_specs=[pl.BlockSpec((None, tr, C), lambda j, i, ids: (ids[3 + j], ids[1] * nrb + i, 0)),
                      pl.BlockSpec((None, tr, C), lambda j, i, ids: (j + 1, i, 0))],
            out_specs=pl.BlockSpec((None, tr, C), lambda j, i, ids: (j, i, 0))),
        out_shape=jax.ShapeDtypeStruct((3, R2, C), BF16), compiler_params=_params(("parallel", "parallel")),
    )(ids, grad, theirs)


def _chip_sum(name, grad, theirs, arrived, ids):
    _, R2, C = theirs.shape
    tr = _tile(R2, 256, 16)
    nrb = R2 // tr

    def body(ids_ref, a_ref, b_ref, r_ref, o_ref):
        tot = a_ref[...] + b_ref[...]
        for j in range(3):
            tot = tot + r_ref[j].astype(F32)
        o_ref[...] = tot

    return pl.pallas_call(
        body, name=name,
        grid_spec=pltpu.PrefetchScalarGridSpec(
            num_scalar_prefetch=1, grid=(nrb,),
            in_specs=[pl.BlockSpec((None, tr, C), lambda i, ids: (ids[0], ids[1] * nrb + i, 0)),
                      pl.BlockSpec((None, tr, C), lambda i, ids: (0, i, 0)),
                      pl.BlockSpec((3, tr, C), lambda i, ids: (0, i, 0))],
            out_specs=pl.BlockSpec((tr, C), lambda i, ids: (ids[1] * nrb + i, 0))),
        out_shape=jax.ShapeDtypeStruct((2 * R2, C), F32), compiler_params=_params(("parallel",)),
    )(ids, grad, theirs, arrived)


def _adamw(name, w, g, m, v):
    R, C = w.shape
    tr = _tile(R, 128, 8)
    c1 = 1.0 / (1.0 - ADAM_B1 ** ADAM_STEP)
    c2 = 1.0 / (1.0 - ADAM_B2 ** ADAM_STEP)

    def body(w_ref, g_ref, m_ref, v_ref, d_ref, mo_ref, vo_ref):
        gv = g_ref[...]
        mn = ADAM_B1 * m_ref[...] + (1.0 - ADAM_B1) * gv
        vn = ADAM_B2 * v_ref[...] + (1.0 - ADAM_B2) * (gv * gv)
        d_ref[...] = -ADAM_LR * ((mn * c1) / (jnp.sqrt(vn * c2) + ADAM_EPS) + ADAM_WD * w_ref[...])
        mo_ref[...] = mn
        vo_ref[...] = vn

    spec = pl.BlockSpec((tr, C), lambda i: (i, 0))
    sds = jax.ShapeDtypeStruct((R, C), F32)
    return pl.pallas_call(body, name=name, grid=(R // tr,), in_specs=[spec] * 4, out_specs=[spec] * 3, out_shape=[sds] * 3,
                          compiler_params=_params(("parallel",)))(w, g, m, v)


SMALL = ["ffn1_norm", "mix_norm", "mem_norm", "forget_bias", "fox_q_gain", "fox_k_gain", "swa_q_gain", "swa_k_gain", "swa_sinks",
         "mem_q_gain", "mem_k_gain", "ffn2_norm"]
LARGE = ["ffn1_gate", "ffn1_up", "ffn1_down", "w_in", "w_mem_k", "w_mem_v", "w_out", "ffn2_gate", "ffn2_up", "ffn2_down"]
GATHER_GROUPS = [["ffn1_gate", "ffn1_up"], ["ffn1_down", "w_in", "w_mem_k", "w_mem_v"], ["w_out", "ffn2_gate", "ffn2_up", "ffn2_down"]]
WEIGHTS = ["ffn1_norm", "ffn1_gate", "ffn1_up", "ffn1_down", "mix_norm", "mem_norm", "w_in", "forget_bias", "w_mem_k", "w_mem_v",
           "fox_q_gain", "fox_k_gain", "swa_q_gain", "swa_k_gain", "swa_sinks", "mem_q_gain", "mem_k_gain", "w_out", "ffn2_norm",
           "ffn2_gate", "ffn2_up", "ffn2_down"]


def _pad_proj_cols(w):
    out = jnp.zeros((w.shape[0], PROJ_W), w.dtype)
    for start, width, pstart in REF_GROUPS:
        out = lax.dynamic_update_slice(out, w[:, start:start + width], (0, pstart))
    return out


def _unpad_proj_cols(w):
    return jnp.concatenate([w[:, pstart:pstart + width] for _, width, pstart in REF_GROUPS], axis=1)


def _pack_small(vals):
    flat = jnp.concatenate([vals[k].reshape(-1).astype(F32) for k in SMALL + ["loss"]])
    n = flat.shape[0]
    total = -(-n // 1024) * 1024
    return jnp.pad(flat, (0, total - n)).reshape(total // 128, 128)


def _unpack_small(buf, shapes):
    flat = buf.reshape(-1)
    out, off = {}, 0
    for k in SMALL + ["loss"]:
        size = int(np.prod(shapes[k]))
        out[k] = flat[off:off + size].reshape(shapes[k])
        off += size
    return out


def kernel(x, mem, ffn1_norm, ffn1_gate, ffn1_up, ffn1_down, mix_norm, mem_norm, w_in, forget_bias, w_mem_k, w_mem_v, fox_q_gain, fox_k_gain, swa_q_gain, swa_k_gain, swa_sinks, mem_q_gain, mem_k_gain, w_out, ffn2_norm, ffn2_gate, ffn2_up, ffn2_down, loss_target, m_ffn1_norm, m_ffn1_gate, m_ffn1_up, m_ffn1_down, m_mix_norm, m_mem_norm, m_w_in, m_forget_bias, m_w_mem_k, m_w_mem_v, m_fox_q_gain, m_fox_k_gain, m_swa_q_gain, m_swa_k_gain, m_swa_sinks, m_mem_q_gain, m_mem_k_gain, m_w_out, m_ffn2_norm, m_ffn2_gate, m_ffn2_up, m_ffn2_down, v_ffn1_norm, v_ffn1_gate, v_ffn1_up, v_ffn1_down, v_mix_norm, v_mem_norm, v_w_in, v_forget_bias, v_w_mem_k, v_w_mem_v, v_fox_q_gain, v_fox_k_gain, v_swa_q_gain, v_swa_k_gain, v_swa_sinks, v_mem_q_gain, v_mem_k_gain, v_w_out, v_ffn2_norm, v_ffn2_gate, v_ffn2_up, v_ffn2_down):
    given = dict(locals())
    T, D = x.shape[1], x.shape[2]
    ML = mem.shape[1]
    xin = x.reshape(T, D)
    target = loss_target.reshape(T, D)
    memin = mem.reshape(ML, D)

    ids = _place_ids()
    shard = {k: given[k][0] for k in LARGE}
    shard["w_in"] = _pad_proj_cols(shard["w_in"])
    started, after = [], ids
    for gi, group in enumerate(GATHER_GROUPS):
        placed = [_cast_place("cast_" + k, shard[k], ids) for k in group]
        send, recv, bufs, after = _gather_start("gather_start_%d" % gi, placed, after)
        started.append((send, recv, bufs))

    def arrive(gi, done):
        send, recv, bufs = started[gi]
        bufs = _gather_wait("gather_wait_%d" % gi, send, recv, bufs, done)
        return dict(zip(GATHER_GROUPS[gi], _gather_forward("gather_forward_%d" % gi, bufs)))

    gains = jnp.concatenate([fox_q_gain, fox_k_gain, swa_q_gain, swa_k_gain, mem_q_gain,
                             jnp.pad(forget_bias, ((0, 0), (0, HEAD_DIM - FOX_HEADS))), jnp.zeros((2, HEAD_DIM), F32)], axis=0)
    slopes_np = 2.0 ** (-8.0 * np.arange(1, SWA_HEADS + 1) / SWA_HEADS)
    slopes = jnp.asarray(np.repeat(slopes_np, WINDOW).reshape(SWA_KV_HEADS, GR, 1), F32)
    sinks = jnp.repeat(swa_sinks.reshape(SWA_HEADS), WINDOW).reshape(SWA_KV_HEADS, GR, 1)

    h1 = _rms_fwd("ffn1_norm_fwd", xin, ffn1_norm + after[0, 0])
    full = arrive(0, h1)
    wg1, wu1 = full["ffn1_gate"], full["ffn1_up"]
    g1, u1, a1 = _ffn_gu("ffn1_gate_up", h1, wg1, wu1)
    full = arrive(1, a1)
    wd1 = full["ffn1_down"].reshape(-1, D)
    win = full["w_in"].reshape(D, PROJ_W)
    wmk = full["w_mem_k"].reshape(D, MEM_HEADS * HEAD_DIM)
    wmv = full["w_mem_v"].reshape(D, MEM_HEADS * HEAD_DIM)
    x1 = _ffn_down("ffn1_down", a1, wd1, xin)
    h2 = _rms_fwd("mix_norm_fwd", x1, mix_norm)
    proj = _mm2d("proj_in", h2, win, "nn", F32, tn=1408, tk=2048, n_outer=True)
    qkv, logf, k_t, v_t = _prep_fwd(proj, gains)
    cum = _cumsum_rows("forget_cumsum", [logf], False)
    cum_h = cum[:, :FOX_HEADS].T
    cq_row = cum_h.reshape(FOX_HEADS, 1, T)
    ck_rep = jnp.broadcast_to(cum_h[:, :, None], (FOX_HEADS, T, HEAD_DIM))
    mn = _rms_fwd("mem_norm_fwd", memin, mem_norm)
    mk_raw = _mm2d("mem_k_proj", mn, wmk, "nn", F32)
    mv = _mm2d("mem_v_proj", mn, wmv, "nn", BF16)
    mk = _head_norm_rows(mk_raw, mem_k_gain)
    out_a, out_a_f32, lse_a = _fox_fwd(qkv, v_t, cq_row, ck_rep)
    out_b, lse_b = _swa_fwd(qkv, slopes, sinks)
    out_c, lse_c = _mem_fwd(qkv, mk, mv)
    mixed = jnp.concatenate([out_a, out_b, out_c], axis=1)
    full = arrive(2, mixed)
    wo = full["w_out"].reshape(-1, D)
    wg2, wu2, wd2 = full["ffn2_gate"], full["ffn2_up"], full["ffn2_down"].reshape(-1, D)
    x2 = _mm2d("mix_out", mixed, wo, "nn", F32, tk=2048, n_outer=True, extras=[x1], epilogue=lambda accs, ex: [ex[0] + accs[0]])
    h3 = _rms_fwd("ffn2_norm_fwd", x2, ffn2_norm)
    g2, u2, a2 = _ffn_gu("ffn2_gate_up", h3, wg2, wu2)
    x3 = _ffn_down("ffn2_down", a2, wd2, x2)
    dx3, dyb3, loss_part = _loss_head(x3, target)

    grads, small, res = {}, {"loss": loss_part[0, 0]}, {}

    def pair_off(tag, group):
        send, recv, own, lands, token = _pair_start("grad_pair_start_" + tag, [grads[k] for k in group])
        return (group, send, recv, own, lands), token

    def chip_off(tag, started, done):
        group, send, recv, own, lands = started
        own, theirs = _pair_wait("grad_pair_wait_" + tag, send, recv, own, lands, done)
        grads.update(zip(group, own))
        to_chips = [_pair_sum_bf16("pair_sum_" + k, grads[k], b, ids) for k, b in zip(group, theirs)]
        send, recv, parts, lands, token = _chip_start("grad_chip_start_" + tag, to_chips)
        return (group, theirs, send, recv, parts, lands), token

    def finish(tag, state, done):
        group, theirs, send, recv, parts, lands = state
        arrived = _chip_wait("grad_chip_wait_" + tag, send, recv, parts, lands, done)
        halves = [_chip_sum("chip_sum_" + k, grads[k], b, r, ids) for k, b, r in zip(group, theirs, arrived)]
        reduced = dict(zip(group, _pair_share("grad_pair_share_" + tag, halves)))
        last = None
        for k in group:
            gk = _unpad_proj_cols(reduced[k]) if k == "w_in" else reduced[k]
            d, mo, vo = _adamw("adamw_" + k, given[k][0], gk, given["m_" + k][0], given["v_" + k][0])
            res[k] = tuple(t[None] for t in (gk, d, mo, vo))
            last = vo
        return last

    dg2, du2 = _ffn_bwd_act("ffn2", dyb3, wd2, g2, u2, N_CHIPS)
    grads["ffn2_down"] = _ffn_bwd_down("ffn2", a2, dyb3, N_CHIPS).reshape(N_CHIPS, -1, D)
    grads["ffn2_gate"], grads["ffn2_up"] = _ffn_bwd_gate_up("ffn2", h3, dg2, du2, N_CHIPS)
    started, token = pair_off("a", ["ffn2_gate", "ffn2_up", "ffn2_down"])
    dh3 = _ffn_bwd_x("ffn2", dg2, du2, wg2, wu2, token)
    state_a, token = chip_off("a", started, dh3)
    dx2, dx2b, small["ffn2_norm"] = _rms_bwd("ffn2_norm_bwd", dh3, x2, ffn2_norm + token[0, 0], dx3, 1.0)
    dmix = _mm2d("mix_out_dx", dx2b, wo, "nt", BF16, tk=2048, n_outer=True)
    grads["w_out"] = _mm2d("mix_out_dw", mixed, dx2b, "tn", F32, tk=1024).reshape(N_CHIPS, -1, D)
    delta_row = _fox_delta(dmix, out_a_f32)[:, :FOX_HEADS].T.reshape(FOX_HEADS, 1, T)
    dfq, dfk, dfv, dck, dcq = _fox_bwd(qkv, k_t, cq_row, ck_rep, delta_row, lse_a, dmix)
    dsq, dsk, dsv, dsink = _swa_bwd(qkv, slopes, sinks, out_b, lse_b, dmix)
    dmq, dmk, dmv = _mem_bwd(qkv, mk, mv, out_c, lse_c, dmix)
    small["swa_sinks"] = dsink[:, :SWA_GROUP, 0].reshape(1, SWA_HEADS)
    dcum = [jnp.pad(d.reshape(FOX_HEADS, T).T, ((0, 0), (0, HEAD_DIM - FOX_HEADS))) for d in (dck, dcq)]
    dlogf = _cumsum_rows("forget_cumsum_bwd", dcum, True)
    dproj, dgains = _prep_bwd(proj, gains, dfq, dfk, dfv, dsq, dsk, dsv, dmq, dlogf)
    for row, k in enumerate(["fox_q_gain", "fox_k_gain", "swa_q_gain", "swa_k_gain", "mem_q_gain"]):
        small[k] = dgains[row:row + 1, :]
    small["forget_bias"] = dgains[5:6, :FOX_HEADS]
    grads["w_in"] = _mm2d("proj_in_dw", h2, dproj, "tn", F32, tn=1408, tk=1024).reshape(N_CHIPS, -1, PROJ_W)
    dmk_raw, small["mem_k_gain"] = _head_norm_rows_bwd(mk_raw, mem_k_gain, dmk)
    dmvb = dmv.astype(BF16)
    grads["w_mem_k"] = _mm2d("mem_k_dw", mn, dmk_raw, "tn", F32).reshape(N_CHIPS, -1, MEM_HEADS * HEAD_DIM)
    grads["w_mem_v"] = _mm2d("mem_v_dw", mn, dmvb, "tn", F32).reshape(N_CHIPS, -1, MEM_HEADS * HEAD_DIM)
    dmn = _mm2d("mem_k_dx", dmk_raw, wmk, "nt", F32)
    dmn = _mm2d("mem_v_dx", dmvb, wmv, "nt", F32, extras=[dmn], epilogue=lambda accs, ex: [ex[0] + accs[0]])
    _, _, small["mem_norm"] = _rms_bwd("mem_norm_bwd", dmn, memin, mem_norm, jnp.zeros_like(memin), 1.0)
    started, token = pair_off("b", ["w_out", "w_in", "w_mem_k", "w_mem_v"])
    dh2 = _mm2d("proj_in_dx", dproj, win, "nt", F32, tm=1024, tk=1408, after=token)
    state_b, token = chip_off("b", started, dh2)
    dx1, dyb1, small["mix_norm"] = _rms_bwd("mix_norm_bwd", dh2, x1, mix_norm + token[0, 0], dx2, 0.5)
    grads["ffn1_down"] = _ffn_bwd_down("ffn1", a1, dyb1, N_CHIPS).reshape(N_CHIPS, -1, D)
    started, token = pair_off("c", ["ffn1_down"])
    dg1, du1 = _ffn_bwd_act("ffn1", dyb1, wd1, g1, u1, N_CHIPS, after=token)
    state_c, token = chip_off("c", started, dg1)
    grads["ffn1_gate"], grads["ffn1_up"] = _ffn_bwd_gate_up("ffn1", h1, dg1, du1, N_CHIPS, after=token)
    started, token = pair_off("d", ["ffn1_gate", "ffn1_up"])
    dh1 = _ffn_bwd_x("ffn1", dg1, du1, wg1, wu1, token)
    state_d, token = chip_off("d", started, dh1)
    grad_x, _, small["ffn1_norm"] = _rms_bwd("ffn1_norm_bwd", dh1, xin, ffn1_norm + token[0, 0], dx1, 1.0)

    s_send, s_recv, s_buf, s_land, token = _small_start(_pack_small(small))

    done = finish("a", state_a, token)
    done = finish("b", state_b, done)
    done = finish("c", state_c, done)
    done = finish("d", state_d, done)

    shapes = {k: given[k].shape for k in SMALL}
    shapes["loss"] = ()
    s_buf, s_land = _small_wait(s_send, s_recv, s_buf, s_land, done)
    red_small = _unpack_small(_small_sum(s_buf, s_land), shapes)
    loss = red_small["loss"]
    zero = {"loss": jnp.zeros((), F32)}
    packed = [_pack_small({**zero, **{k: src[k] for k in SMALL}}) for src in (
        {k: given[k] for k in SMALL}, red_small, {k: given["m_" + k] for k in SMALL}, {k: given["v_" + k] for k in SMALL})]
    d_s, m_s, v_s = (_unpack_small(t, shapes) for t in _adamw("adamw_small", *packed))
    for k in SMALL:
        res[k] = (red_small[k], d_s[k], m_s[k], v_s[k])

    outs = [loss, grad_x.reshape(1, T, D)]
    for part in range(4):
        outs += [res[k][part] for k in WEIGHTS]
    return tuple(outs)
```

```python
import functools

import numpy as np
import jax
import jax.numpy as jnp
from jax import lax
from jax.experimental import pallas as pl
from jax.experimental.pallas import tpu as pltpu

F32 = jnp.float32
BF16 = jnp.bfloat16
MESH = pl.DeviceIdType.MESH

HEAD_DIM = 128
FOX_HEADS = 6
SWA_HEADS = 6
SWA_KV_HEADS = 2
SWA_GROUP = SWA_HEADS // SWA_KV_HEADS
MEM_HEADS = 4
WINDOW = 128
EPS = 1e-6
NEG_INF = -1e30
SCALE = HEAD_DIM ** -0.5

C_FQ = 0
C_FK = C_FQ + FOX_HEADS * HEAD_DIM
C_FV = C_FK + FOX_HEADS * HEAD_DIM
C_SQ = C_FV + FOX_HEADS * HEAD_DIM
C_SK = C_SQ + SWA_HEADS * HEAD_DIM
C_SV = C_SK + SWA_KV_HEADS * HEAD_DIM
C_MQ = C_SV + SWA_KV_HEADS * HEAD_DIM
C_FL = C_MQ + MEM_HEADS * HEAD_DIM
PROJ_W = C_FL + HEAD_DIM
FOX_W = FOX_HEADS * HEAD_DIM
REF_GROUPS = [
    (0, FOX_W, C_FQ), (FOX_W, FOX_W, C_FK), (2 * FOX_W, FOX_W, C_FV), (3 * FOX_W, FOX_HEADS, C_FL),
    (3 * FOX_W + FOX_HEADS, SWA_HEADS * HEAD_DIM, C_SQ),
    (3 * FOX_W + FOX_HEADS + SWA_HEADS * HEAD_DIM, SWA_KV_HEADS * HEAD_DIM, C_SK),
    (3 * FOX_W + FOX_HEADS + (SWA_HEADS + SWA_KV_HEADS) * HEAD_DIM, SWA_KV_HEADS * HEAD_DIM, C_SV),
    (3 * FOX_W + FOX_HEADS + (SWA_HEADS + 2 * SWA_KV_HEADS) * HEAD_DIM, MEM_HEADS * HEAD_DIM, C_MQ),
]

ADAM_LR = 0.001
ADAM_B1 = 0.9
ADAM_B2 = 0.999
ADAM_EPS = 1e-08
ADAM_WD = 0.01
ADAM_STEP = 10

V7X_VMEM_LIMIT = 56 * 1024 * 1024
N_CHIPS = 4
N_DEV = 8


def _tile(n, pref, mult=128):
    t = (min(pref, n) // mult) * mult
    while t >= mult:
        if n % t == 0:
            return t
        t -= mult
    return n


def _params(sem):
    return pltpu.CompilerParams(dimension_semantics=sem, vmem_limit_bytes=V7X_VMEM_LIMIT)


_DIMS = {"nn": (((1,), (0,)), ((), ())), "nt": (((1,), (1,)), ((), ())), "tn": (((0,), (0,)), ((), ()))}


def _dot(a, b, mode):
    return lax.dot_general(a, b, _DIMS[mode], preferred_element_type=F32)


def _mm(name, grid, pairs, acc_of, acc_shapes, extras, outs, epilogue, after=None):
    n_p, n_e, n_o, n_a = len(pairs), len(extras), len(outs), len(acc_shapes)
    n_w = 0 if after is None else 1
    nk = grid[2]
    n_in = sum(1 if a is None else 2 for a, *_ in pairs)

    def body(*refs):
        ex = refs[n_in:n_in + n_e]
        out = refs[n_in + n_e + n_w:n_in + n_e + n_w + n_o]
        accs = refs[n_in + n_e + n_w + n_o:]
        parts = [None] * n_a
        at = 0
        for p in range(n_p):
            if pairs[p][0] is None:
                a_ref, b_ref = refs[0], refs[at]
                at += 1
            else:
                a_ref, b_ref = refs[at], refs[at + 1]
                at += 2
            d = _dot(a_ref[...], b_ref[...], pairs[p][4])
            parts[acc_of[p]] = d if parts[acc_of[p]] is None else parts[acc_of[p]] + d

        def finish(vals):
            for o, r in zip(out, epilogue(vals, [e[...] for e in ex])):
                o[...] = r.astype(o.dtype)

        if nk == 1:
            finish(parts)
            return
        k = pl.program_id(2)

        @pl.when(k == 0)
        def _():
            for a, d in zip(accs, parts):
                a[...] = d

        @pl.when((k > 0) & (k < nk - 1))
        def _():
            for a, d in zip(accs, parts):
                a[...] += d

        @pl.when(k == nk - 1)
        def _():
            finish([a[...] + d for a, d in zip(accs, parts)])

    in_specs, args = [], []
    for a, a_spec, b, b_spec, _ in pairs:
        if a is not None:
            in_specs.append(a_spec)
            args.append(a)
        in_specs.append(b_spec)
        args.append(b)
    for e, e_spec in extras:
        in_specs.append(e_spec)
        args.append(e)
    if after is not None:
        in_specs.append(pl.BlockSpec(memory_space=pl.ANY))
        args.append(after)
    res = pl.pallas_call(
        body, name=name, grid=grid, in_specs=in_specs,
        out_specs=[s for _, s in outs], out_shape=[o for o, _ in outs],
        scratch_shapes=[pltpu.VMEM(s, F32) for s in acc_shapes] if nk > 1 else [],
        compiler_params=_params(("parallel", "parallel", "arbitrary")),
    )(*args)
    return res


def _mm2d(name, a, b, mode, out_dtype, tm=512, tn=1024, tk=1024, extras=(), epilogue=None, n_out=1, after=None, n_outer=False):
    if mode == "nn":
        (M, K), N = a.shape, b.shape[1]
    elif mode == "nt":
        (M, K), N = a.shape, b.shape[0]
    else:
        (K, M), N = a.shape, b.shape[1]
    tm, tn, tk = _tile(M, tm), _tile(N, tn), _tile(K, tk)

    def spec(shape, index):
        if n_outer:
            return pl.BlockSpec(shape, lambda j, i, k: index(i, j, k))
        return pl.BlockSpec(shape, index)

    a_spec = spec((tk, tm), lambda i, j, k: (k, i)) if mode == "tn" else spec((tm, tk), lambda i, j, k: (i, k))
    b_spec = spec((tn, tk), lambda i, j, k: (j, k)) if mode == "nt" else spec((tk, tn), lambda i, j, k: (k, j))
    mn = spec((tm, tn), lambda i, j, k: (i, j))
    if epilogue is None:
        epilogue = lambda accs, ex: [accs[0]]
    if not isinstance(out_dtype, (list, tuple)):
        out_dtype = [out_dtype] * n_out
    grid = (N // tn, M // tm, K // tk) if n_outer else (M // tm, N // tn, K // tk)
    res = _mm(name, grid, [(a, a_spec, b, b_spec, mode)], [0], [(tm, tn)],
              [(e, mn) for e in extras], [(jax.ShapeDtypeStruct((M, N), d), mn) for d in out_dtype], epilogue, after=after)
    return res[0] if len(res) == 1 else res


def _sigmoid(x):
    return 1.0 / (1.0 + jnp.exp(-x))


def _sigmoid_fast(x):
    return pl.reciprocal(1.0 + jnp.exp(-x), approx=True)


def _ffn_gu(name, h, wg, wu):
    T, D = h.shape
    nf, _, F4 = wg.shape
    tm, tk = _tile(T, 512), _tile(D, 2048)
    a_spec = pl.BlockSpec((tm, tk), lambda j, i, k: (i, k))
    b_spec = pl.BlockSpec((None, tk, F4), lambda j, i, k: (j, k, 0))
    o_spec = pl.BlockSpec((tm, F4), lambda j, i, k: (i, j))

    def epilogue(accs, ex):
        g, u = accs
        return [g, u, g * _sigmoid_fast(g) * u]

    sds = jax.ShapeDtypeStruct((T, nf * F4), BF16)
    return _mm(name, (nf, T // tm, D // tk), [(h, a_spec, wg, b_spec, "nn"), (None, None, wu, b_spec, "nn")], [0, 1],
               [(tm, F4), (tm, F4)], [], [(sds, o_spec)] * 3, epilogue)


def _ffn_down(name, a, wd, xres):
    return _mm2d(name, a, wd, "nn", F32, tm=1024, tk=wd.shape[0] // N_CHIPS, extras=[xres],
                 epilogue=lambda accs, ex: [ex[0] + 0.5 * accs[0]])


def _ffn_bwd_down(tag, a, dyb, nf, after=None):
    return _mm2d(tag + "_dwd", a, dyb, "tn", F32, tm=a.shape[1] // nf, tk=1024, after=after)


def _ffn_bwd_act(tag, dyb, wd, g, u, nf, after=None):
    def act_bwd(accs, ex):
        gf, uf = ex[0].astype(F32), ex[1].astype(F32)
        s = _sigmoid_fast(gf)
        return [accs[0] * uf * s * (1.0 + gf * (1.0 - s)), accs[0] * gf * s]

    return _mm2d(tag + "_da", dyb, wd, "nt", BF16, tn=wd.shape[0] // nf, tk=2048, extras=[g, u], epilogue=act_bwd, n_out=2,
                 n_outer=True, after=after)


def _ffn_bwd_gate_up(tag, h, dg, du, nf, after=None):
    T, D = h.shape
    F4 = dg.shape[1] // nf
    tm, tk = _tile(D, 512), _tile(T, 1024)
    h_spec = pl.BlockSpec((tk, tm), lambda i, j, k: (k, i))
    d_spec = pl.BlockSpec((tk, F4), lambda i, j, k: (k, j))
    w_spec = pl.BlockSpec((None, tm, F4), lambda i, j, k: (j, i, 0))
    sds = jax.ShapeDtypeStruct((nf, D, F4), F32)
    return _mm(tag + "_dwgu", (D // tm, nf, T // tk), [(h, h_spec, dg, d_spec, "tn"), (None, None, du, d_spec, "tn")],
               [0, 1], [(tm, F4), (tm, F4)], [], [(sds, w_spec)] * 2, lambda accs, ex: accs, after=after)


def _ffn_bwd_x(tag, dg, du, wg, wu, after):
    T = dg.shape[0]
    nf, D, F4 = wg.shape
    tm, tn = _tile(T, 1024), _tile(D, 1024)
    a_spec = pl.BlockSpec((tm, F4), lambda i, j, k: (i, k))
    b_spec = pl.BlockSpec((None, tn, F4), lambda i, j, k: (k, j, 0))
    o_spec = pl.BlockSpec((tm, tn), lambda i, j, k: (i, j))
    (dh,) = _mm(tag + "_dh", (T // tm, D // tn, nf), [(dg, a_spec, wg, b_spec, "nt"), (du, a_spec, wu, b_spec, "nt")],
                [0, 0], [(tm, tn)], [], [(jax.ShapeDtypeStruct((T, D), F32), o_spec)], lambda accs, ex: accs, after=after)
    return dh


def _rms_fwd(name, x, gain):
    R, D = x.shape
    tr = _tile(R, 256, 8)

    def body(x_ref, g_ref, o_ref):
        xv = x_ref[...]
        r = lax.rsqrt(jnp.mean(xv * xv, axis=-1, keepdims=True) + EPS)
        o_ref[...] = (xv * r * g_ref[...]).astype(BF16)

    return pl.pallas_call(
        body, name=name, grid=(R // tr,),
        in_specs=[pl.BlockSpec((tr, D), lambda i: (i, 0)), pl.BlockSpec((1, D), lambda i: (0, 0))],
        out_specs=pl.BlockSpec((tr, D), lambda i: (i, 0)), out_shape=jax.ShapeDtypeStruct((R, D), BF16),
        compiler_params=_params(("parallel",)),
    )(x, gain)


def _rms_bwd(name, dh, x, gain, dres, bscale):
    R, D = x.shape
    tr = _tile(R, 256, 8)

    def body(dh_ref, x_ref, g_ref, dres_ref, dx_ref, dxb_ref, dg_ref):
        xv, dy = x_ref[...], dh_ref[...]
        r = lax.rsqrt(jnp.mean(xv * xv, axis=-1, keepdims=True) + EPS)
        xn = xv * r
        uu = dy * g_ref[...]
        dx = dres_ref[...] + r * (uu - xn * jnp.mean(xn * uu, axis=-1, keepdims=True))
        dx_ref[...] = dx
        dxb_ref[...] = (bscale * dx).astype(BF16)
        part = jnp.sum(dy * xn, axis=0, keepdims=True)

        @pl.when(pl.program_id(0) == 0)
        def _():
            dg_ref[...] = part

        @pl.when(pl.program_id(0) > 0)
        def _():
            dg_ref[...] += part

    row = pl.BlockSpec((tr, D), lambda i: (i, 0))
    vec = pl.BlockSpec((1, D), lambda i: (0, 0))
    return pl.pallas_call(
        body, name=name, grid=(R // tr,), in_specs=[row, row, vec, row], out_specs=[row, row, vec],
        out_shape=[jax.ShapeDtypeStruct((R, D), F32), jax.ShapeDtypeStruct((R, D), BF16), jax.ShapeDtypeStruct((1, D), F32)],
        compiler_params=_params(("arbitrary",)),
    )(dh, x, gain, dres)


def _loss_head(y, target):
    R, D = y.shape
    tr = _tile(R, 256, 8)

    def body(y_ref, t_ref, d_ref, db_ref, l_ref):
        e = y_ref[...] - t_ref[...]
        d = e * (1.0 / D)
        d_ref[...] = d
        db_ref[...] = (0.5 * d).astype(BF16)
        part = jnp.zeros((8, 128), F32) + (0.5 / D) * jnp.sum(e * e)

        @pl.when(pl.program_id(0) == 0)
        def _():
            l_ref[...] = part

        @pl.when(pl.program_id(0) > 0)
        def _():
            l_ref[...] += part

    row = pl.BlockSpec((tr, D), lambda i: (i, 0))
    acc = pl.BlockSpec((8, 128), lambda i: (0, 0))
    return pl.pallas_call(
        body, name="loss_head", grid=(R // tr,), in_specs=[row, row], out_specs=[row, row, acc],
        out_shape=[jax.ShapeDtypeStruct((R, D), F32), jax.ShapeDtypeStruct((R, D), BF16), jax.ShapeDtypeStruct((8, 128), F32)],
        compiler_params=_params(("arbitrary",)),
    )(y, target)


def _head_norm(xs, g):
    r = lax.rsqrt(jnp.mean(xs * xs, axis=-1, keepdims=True) + EPS)
    return xs * r * g


def _head_norm_bwd(xs, g, dy):
    r = lax.rsqrt(jnp.mean(xs * xs, axis=-1, keepdims=True) + EPS)
    xn = xs * r
    uu = dy * g
    return r * (uu - xn * jnp.mean(xn * uu, axis=-1, keepdims=True)), jnp.sum(dy * xn, axis=0, keepdims=True)


NORMED = [(C_FQ, FOX_HEADS, 0), (C_FK, FOX_HEADS, 1), (C_SQ, SWA_HEADS, 2), (C_SK, SWA_KV_HEADS, 3), (C_MQ, MEM_HEADS, 4)]
PLAIN = [(C_FV, FOX_HEADS), (C_SV, SWA_KV_HEADS)]


def _prep_fwd(proj, gains):
    T = proj.shape[0]
    tr = _tile(T, 256, 128)

    def body(p_ref, g_ref, o_ref, lf_ref, kt_ref, vt_ref):
        for start, heads, row in NORMED:
            gn = g_ref[row:row + 1, :]
            for hh in range(heads):
                sl = slice(start + hh * HEAD_DIM, start + (hh + 1) * HEAD_DIM)
                y = _head_norm(p_ref[:, sl], gn)
                o_ref[:, sl] = y.astype(BF16)
                if start == C_FK:
                    kt_ref[hh * HEAD_DIM:(hh + 1) * HEAD_DIM, :] = y.T.astype(BF16)
        for start, heads in PLAIN:
            sl = slice(start, start + heads * HEAD_DIM)
            o_ref[:, sl] = p_ref[:, sl].astype(BF16)
        for hh in range(FOX_HEADS):
            sl = slice(C_FV + hh * HEAD_DIM, C_FV + (hh + 1) * HEAD_DIM)
            vt_ref[hh * HEAD_DIM:(hh + 1) * HEAD_DIM, :] = p_ref[:, sl].T.astype(BF16)
        zb = p_ref[:, C_FL:C_FL + HEAD_DIM] + g_ref[5:6, :]
        o_ref[:, C_FL:C_FL + HEAD_DIM] = jnp.zeros((tr, HEAD_DIM), BF16)
        lf_ref[...] = jnp.minimum(zb, 0.0) - jnp.log(1.0 + jnp.exp(-jnp.abs(zb)))

    return pl.pallas_call(
        body, name="prep_fwd", grid=(T // tr,),
        in_specs=[pl.BlockSpec((tr, PROJ_W), lambda i: (i, 0)), pl.BlockSpec((8, 128), lambda i: (0, 0))],
        out_specs=[pl.BlockSpec((tr, PROJ_W), lambda i: (i, 0)), pl.BlockSpec((tr, HEAD_DIM), lambda i: (i, 0)),
                   pl.BlockSpec((FOX_W, tr), lambda i: (0, i)), pl.BlockSpec((FOX_W, tr), lambda i: (0, i))],
        out_shape=[jax.ShapeDtypeStruct((T, PROJ_W), BF16), jax.ShapeDtypeStruct((T, HEAD_DIM), F32),
                   jax.ShapeDtypeStruct((FOX_W, T), BF16), jax.ShapeDtypeStruct((FOX_W, T), BF16)],
        compiler_params=_params(("parallel",)),
    )(proj, gains)


def _prep_bwd(proj, gains, dfq, dfk, dfv, dsq, dsk, dsv, dmq, dlogf):
    T = proj.shape[0]
    tr = _tile(T, 256, 8)
    d_normed = {C_FQ: 0, C_FK: 1, C_SQ: 3, C_SK: 4, C_MQ: 6}
    d_plain = {C_FV: 2, C_SV: 5}

    def body(p_ref, g_ref, *rest):
        d_refs, dlf_ref, o_ref, dg_ref = rest[:7], rest[7], rest[8], rest[9]
        rows = []
        for start, heads, row in NORMED:
            gn = g_ref[row:row + 1, :]
            d_ref = d_refs[d_normed[start]]
            tot = jnp.zeros((1, HEAD_DIM), F32)
            for hh in range(heads):
                sl = slice(start + hh * HEAD_DIM, start + (hh + 1) * HEAD_DIM)
                dx, dgn = _head_norm_bwd(p_ref[:, sl], gn, d_ref[:, hh * HEAD_DIM:(hh + 1) * HEAD_DIM])
                o_ref[:, sl] = dx.astype(BF16)
                tot = tot + dgn
            rows.append(tot)
        for start, heads in PLAIN:
            o_ref[:, start:start + heads * HEAD_DIM] = d_refs[d_plain[start]][...].astype(BF16)
        zb = p_ref[:, C_FL:C_FL + HEAD_DIM] + g_ref[5:6, :]
        lane = lax.broadcasted_iota(jnp.int32, (tr, HEAD_DIM), 1)
        dz = jnp.where(lane < FOX_HEADS, dlf_ref[...] * (1.0 - _sigmoid(zb)), 0.0)
        o_ref[:, C_FL:C_FL + HEAD_DIM] = dz.astype(BF16)
        rows.append(jnp.sum(dz, axis=0, keepdims=True))
        part = jnp.concatenate(rows + [jnp.zeros((2, HEAD_DIM), F32)], axis=0)

        @pl.when(pl.program_id(0) == 0)
        def _():
            dg_ref[...] = part

        @pl.when(pl.program_id(0) > 0)
        def _():
            dg_ref[...] += part

    def rows_of(w):
        return pl.BlockSpec((tr, w), lambda i: (i, 0))

    small = pl.BlockSpec((8, 128), lambda i: (0, 0))
    ds = [dfq, dfk, dfv, dsq, dsk, dsv, dmq]
    return pl.pallas_call(
        body, name="prep_bwd", grid=(T // tr,),
        in_specs=[rows_of(PROJ_W), small] + [rows_of(d.shape[1]) for d in ds] + [rows_of(HEAD_DIM)],
        out_specs=[rows_of(PROJ_W), small],
        out_shape=[jax.ShapeDtypeStruct((T, PROJ_W), BF16), jax.ShapeDtypeStruct((8, 128), F32)],
        compiler_params=_params(("arbitrary",)),
    )(proj, gains, *ds, dlogf)


def _head_norm_rows(x, gain):
    R, W = x.shape

    def body(x_ref, g_ref, o_ref):
        for hh in range(W // HEAD_DIM):
            sl = slice(hh * HEAD_DIM, (hh + 1) * HEAD_DIM)
            o_ref[:, sl] = _head_norm(x_ref[:, sl], g_ref[...]).astype(BF16)

    return pl.pallas_call(body, name="mem_k_norm", out_shape=jax.ShapeDtypeStruct((R, W), BF16))(x, gain)


def _head_norm_rows_bwd(x, gain, dy):
    R, W = x.shape

    def body(x_ref, g_ref, dy_ref, dx_ref, dg_ref):
        tot = jnp.zeros((1, HEAD_DIM), F32)
        for hh in range(W // HEAD_DIM):
            sl = slice(hh * HEAD_DIM, (hh + 1) * HEAD_DIM)
            dx, dgn = _head_norm_bwd(x_ref[:, sl], g_ref[...], dy_ref[:, sl])
            dx_ref[:, sl] = dx.astype(BF16)
            tot = tot + dgn
        dg_ref[...] = tot

    return pl.pallas_call(
        body, name="mem_k_norm_bwd",
        out_shape=[jax.ShapeDtypeStruct((R, W), BF16), jax.ShapeDtypeStruct((1, HEAD_DIM), F32)])(x, gain, dy)


def _cumsum_rows(name, xs, reverse):
    T, W = xs[0].shape
    tb = _tile(T, 512, 8)
    nb = T // tb

    def body(*refs):
        o_ref, carry = refs[len(xs)], refs[len(xs) + 1]

        @pl.when(pl.program_id(0) == 0)
        def _():
            carry[...] = jnp.zeros_like(carry)

        xv = refs[0][...]
        for x_ref in refs[1:len(xs)]:
            xv = xv + x_ref[...]
        r = lax.broadcasted_iota(jnp.int32, (tb, tb), 0)
        cc = lax.broadcasted_iota(jnp.int32, (tb, tb), 1)
        tri = jnp.where((cc >= r) if reverse else (cc <= r), 1.0, 0.0).astype(F32)
        o_ref[...] = jnp.dot(tri, xv, precision=lax.Precision.HIGHEST, preferred_element_type=F32) + carry[...]
        carry[...] += jnp.sum(xv, axis=0, keepdims=True)

    idx = (lambda i: (nb - 1 - i, 0)) if reverse else (lambda i: (i, 0))
    return pl.pallas_call(
        body, name=name, grid=(nb,), in_specs=[pl.BlockSpec((tb, W), idx)] * len(xs), out_specs=pl.BlockSpec((tb, W), idx),
        out_shape=jax.ShapeDtypeStruct((T, W), F32), scratch_shapes=[pltpu.VMEM((1, W), F32)],
        compiler_params=_params(("arbitrary",)),
    )(*xs)


def _triangle(nq, by_column):
    if by_column:
        blocks = [(i, j) for j in range(nq) for i in range(j, nq)]
    else:
        blocks = [(i, j) for i in range(nq) for j in range(i + 1)]
    return jnp.asarray(np.array(blocks, np.int32).T)


def _fox_scores_t(k, q, cq_row, ck_rep, on_diagonal):
    n = q.shape[0]
    s = _dot(k, q, "nt") * SCALE + (cq_row - jnp.tile(ck_rep, (1, n // HEAD_DIM)))
    if on_diagonal:
        s = jnp.where(lax.broadcasted_iota(jnp.int32, (n, n), 0) <= lax.broadcasted_iota(jnp.int32, (n, n), 1), s, NEG_INF)
    return s


def _fox_fwd(qkv, v_t, cq_row, ck_rep):
    T = qkv.shape[0]
    tq = _tile(T, 512)
    nq = T // tq
    steps = nq * (nq + 1) // 2
    HQ, HK = C_FQ // HEAD_DIM, C_FK // HEAD_DIM

    def body(tab, q_ref, k_ref, vt_ref, cq_ref, ck_ref, o_ref, of_ref, lse_ref, m_sc, l_sc, acc_sc):
        i, j = tab[0, pl.program_id(1)], tab[1, pl.program_id(1)]

        @pl.when(j == 0)
        def _():
            m_sc[...] = jnp.full_like(m_sc, NEG_INF)
            l_sc[...] = jnp.zeros_like(l_sc)
            acc_sc[...] = jnp.zeros_like(acc_sc)

        def step(on_diagonal):
            s = _fox_scores_t(k_ref[...], q_ref[...], cq_ref[...], ck_ref[...], on_diagonal)
            m_new = jnp.maximum(m_sc[...], jnp.max(s, axis=0, keepdims=True))
            alpha = jnp.exp(m_sc[...] - m_new)
            p = jnp.exp(s - m_new)
            l_sc[...] = alpha * l_sc[...] + jnp.sum(p, axis=0, keepdims=True)
            acc_sc[...] = alpha * acc_sc[...] + _dot(vt_ref[...], p.astype(BF16), "nn")
            m_sc[...] = m_new

        @pl.when(j < i)
        def _():
            step(False)

        @pl.when(j == i)
        def _():
            step(True)
            o = (acc_sc[...] / l_sc[...]).T
            o_ref[...] = o.astype(BF16)
            of_ref[...] = o
            lse_ref[...] = m_sc[...] + jnp.log(l_sc[...])

    qrow = pl.BlockSpec((None, 1, tq), lambda h, s, tab: (h, 0, tab[0, s]))
    return pl.pallas_call(
        body, name="fox_fwd",
        grid_spec=pltpu.PrefetchScalarGridSpec(
            num_scalar_prefetch=1, grid=(FOX_HEADS, steps),
            in_specs=[pl.BlockSpec((tq, HEAD_DIM), lambda h, s, tab: (tab[0, s], HQ + h)),
                      pl.BlockSpec((tq, HEAD_DIM), lambda h, s, tab: (tab[1, s], HK + h)),
                      pl.BlockSpec((HEAD_DIM, tq), lambda h, s, tab: (h, tab[1, s])), qrow,
                      pl.BlockSpec((None, tq, HEAD_DIM), lambda h, s, tab: (h, tab[1, s], 0))],
            out_specs=[pl.BlockSpec((tq, HEAD_DIM), lambda h, s, tab: (tab[0, s], h)),
                       pl.BlockSpec((tq, HEAD_DIM), lambda h, s, tab: (tab[0, s], h)), qrow],
            scratch_shapes=[pltpu.VMEM((1, tq), F32), pltpu.VMEM((1, tq), F32), pltpu.VMEM((HEAD_DIM, tq), F32)]),
        out_shape=[jax.ShapeDtypeStruct((T, FOX_W), BF16), jax.ShapeDtypeStruct((T, FOX_W), F32),
                   jax.ShapeDtypeStruct((FOX_HEADS, 1, T), F32)],
        compiler_params=_params(("parallel", "arbitrary")),
    )(_triangle(nq, False), qkv, qkv, v_t, cq_row, ck_rep)


def _fox_delta(dmix, out_f32):
    T = out_f32.shape[0]
    tr = _tile(T, 512, 8)

    def body(do_ref, o_ref, d_ref):
        lane = lax.broadcasted_iota(jnp.int32, (tr, HEAD_DIM), 1)
        acc = jnp.zeros((tr, HEAD_DIM), F32)
        for hh in range(FOX_HEADS):
            sl = slice(hh * HEAD_DIM, (hh + 1) * HEAD_DIM)
            d = jnp.sum(do_ref[:, sl].astype(F32) * o_ref[:, sl], axis=-1, keepdims=True)
            acc = jnp.where(lane == hh, d, acc)
        d_ref[...] = acc

    blk = pl.BlockSpec((tr, FOX_W), lambda i: (i, 0))
    return pl.pallas_call(
        body, name="fox_delta", grid=(T // tr,), in_specs=[blk, blk], out_specs=pl.BlockSpec((tr, HEAD_DIM), lambda i: (i, 0)),
        out_shape=jax.ShapeDtypeStruct((T, HEAD_DIM), F32), compiler_params=_params(("parallel",)),
    )(dmix, out_f32)


def _fox_bwd(qkv, k_t, cq_row, ck_rep, delta_row, lse, dmix):
    T = qkv.shape[0]
    tq = _tile(T, 512)
    nq = T // tq
    steps = nq * (nq + 1) // 2
    HQ, HK, HV = C_FQ // HEAD_DIM, C_FK // HEAD_DIM, C_FV // HEAD_DIM

    def body(tab, q_ref, k_ref, kt_ref, v_ref, cq_ref, ck_ref, delta_ref, lse_ref, do_ref,
             dq_ref, dk_ref, dv_ref, dck_ref, dcq_ref, dk_sc, dv_sc, dc_sc, dqt_sc):
        qi, kj = tab[0, pl.program_id(1)], tab[1, pl.program_id(1)]

        @pl.when(qi == kj)
        def _():
            dk_sc[...] = jnp.zeros_like(dk_sc)
            dv_sc[...] = jnp.zeros_like(dv_sc)
            dc_sc[...] = jnp.zeros_like(dc_sc)

        def step(on_diagonal):
            q, k, v, do = q_ref[...], k_ref[...], v_ref[...], do_ref[...]
            p = jnp.exp(_fox_scores_t(k, q, cq_ref[...], ck_ref[...], on_diagonal) - lse_ref[...])
            dp = _dot(v, do, "nt")
            ds = p * (dp - delta_ref[...])
            dsb = ds.astype(BF16)
            dv_sc[...] += _dot(p.astype(BF16), do, "nn")
            dk_sc[...] += _dot(dsb, q, "nn")
            dc_sc[...] += jnp.sum(ds, axis=1, keepdims=True)
            dq_part = _dot(kt_ref[...], dsb, "nn") * SCALE
            dcq_part = jnp.sum(ds, axis=0, keepdims=True)

            @pl.when(kj == 0)
            def _():
                dqt_sc[qi] = dq_part
                dcq_ref[qi] = dcq_part

            @pl.when(kj > 0)
            def _():
                dqt_sc[qi] += dq_part
                dcq_ref[qi] += dcq_part

            if on_diagonal:
                dq_ref[...] = dqt_sc[qi].T

        @pl.when(qi > kj)
        def _():
            step(False)

        @pl.when(qi == kj)
        def _():
            step(True)

        @pl.when(qi == nq - 1)
        def _():
            dk_ref[...] = dk_sc[...] * SCALE
            dv_ref[...] = dv_sc[...]
            dck_ref[...] = -dc_sc[...]

    def rows(base):
        return pl.BlockSpec((tq, HEAD_DIM), lambda h, s, tab: (tab[0, s], base + h))

    def cols(base):
        return pl.BlockSpec((tq, HEAD_DIM), lambda h, s, tab: (tab[1, s], base + h))

    qrow = pl.BlockSpec((None, 1, tq), lambda h, s, tab: (h, 0, tab[0, s]))
    sds = jax.ShapeDtypeStruct((T, FOX_W), F32)
    return pl.pallas_call(
        body, name="fox_bwd",
        grid_spec=pltpu.PrefetchScalarGridSpec(
            num_scalar_prefetch=1, grid=(FOX_HEADS, steps),
            in_specs=[rows(HQ), cols(HK), pl.BlockSpec((HEAD_DIM, tq), lambda h, s, tab: (h, tab[1, s])), cols(HV), qrow,
                      pl.BlockSpec((None, tq, HEAD_DIM), lambda h, s, tab: (h, tab[1, s], 0)), qrow, qrow, rows(0)],
            out_specs=[cols(0), cols(0), cols(0), pl.BlockSpec((None, tq, 1), lambda h, s, tab: (h, tab[1, s], 0)),
                       pl.BlockSpec((None, nq, 1, tq), lambda h, s, tab: (h, 0, 0, 0))],
            scratch_shapes=[pltpu.VMEM((tq, HEAD_DIM), F32), pltpu.VMEM((tq, HEAD_DIM), F32), pltpu.VMEM((tq, 1), F32),
                            pltpu.VMEM((nq, HEAD_DIM, tq), F32)]),
        out_shape=[sds, sds, sds, jax.ShapeDtypeStruct((FOX_HEADS, T, 1), F32), jax.ShapeDtypeStruct((FOX_HEADS, nq, 1, tq), F32)],
        compiler_params=_params(("parallel", "arbitrary")),
    )(_triangle(nq, True), qkv, qkv, k_t, qkv, cq_row, ck_rep, delta_row, lse, dmix)


GW = SWA_GROUP * HEAD_DIM
GR = SWA_GROUP * WINDOW


def _swa_scores(q_ref, kp_ref, kc_ref, slope_ref, n):
    q = q_ref[...]
    qs = jnp.concatenate([q[:, t * HEAD_DIM:(t + 1) * HEAD_DIM] for t in range(SWA_GROUP)], axis=0)
    kb = jnp.concatenate([kp_ref[...], kc_ref[...]], axis=0)
    r = lax.broadcasted_iota(jnp.int32, (GR, 2 * WINDOW), 0) & (WINDOW - 1)
    jj = lax.broadcasted_iota(jnp.int32, (GR, 2 * WINDOW), 1)
    dist = WINDOW + r - jj
    valid = (dist >= 0) & (dist < WINDOW) & ((n > 0) | (jj >= WINDOW))
    s = _dot(qs, kb, "nt") * SCALE - slope_ref[...] * dist.astype(F32)
    return qs, kb, jnp.where(valid, s, NEG_INF), valid


def _swa_specs():
    HQ, HK, HV = C_SQ // GW, C_SK // HEAD_DIM, C_SV // HEAD_DIM
    q_spec = pl.BlockSpec((WINDOW, GW), lambda g, n: (n, HQ + g))

    def prev(base):
        return pl.BlockSpec((WINDOW, HEAD_DIM), lambda g, n: (jnp.maximum(n - 1, 0), base + g))

    def cur(base):
        return pl.BlockSpec((WINDOW, HEAD_DIM), lambda g, n: (n, base + g))

    col = pl.BlockSpec((None, GR, 1), lambda g, n: (g, 0, 0))
    return q_spec, prev(HK), cur(HK), prev(HV), cur(HV), col


def _swa_fwd(qkv, slopes, sinks):
    T = qkv.shape[0]
    nb = T // WINDOW
    assert C_SQ % GW == 0

    def body(q_ref, kp_ref, kc_ref, vp_ref, vc_ref, slope_ref, sink_ref, o_ref, lse_ref):
        n = pl.program_id(1)
        _, _, s, _ = _swa_scores(q_ref, kp_ref, kc_ref, slope_ref, n)
        m = jnp.maximum(jnp.max(s, axis=-1, keepdims=True), sink_ref[...])
        p = jnp.exp(s - m)
        l = jnp.sum(p, axis=-1, keepdims=True) + jnp.exp(sink_ref[...] - m)
        vb = jnp.concatenate([vp_ref[...], vc_ref[...]], axis=0)
        o = _dot(p.astype(BF16), vb, "nn") / l
        for t in range(SWA_GROUP):
            o_ref[:, t * HEAD_DIM:(t + 1) * HEAD_DIM] = o[t * WINDOW:(t + 1) * WINDOW, :].astype(BF16)
        lse_ref[...] = m + jnp.log(l)

    q_spec, kp, kc, vp, vc, col = _swa_specs()
    return pl.pallas_call(
        body, name="swa_fwd", grid=(SWA_KV_HEADS, nb), in_specs=[q_spec, kp, kc, vp, vc, col, col],
        out_specs=[pl.BlockSpec((WINDOW, GW), lambda g, n: (n, g)), pl.BlockSpec((None, None, GR, 1), lambda g, n: (g, n, 0, 0))],
        out_shape=[jax.ShapeDtypeStruct((T, SWA_HEADS * HEAD_DIM), BF16), jax.ShapeDtypeStruct((SWA_KV_HEADS, nb, GR, 1), F32)],
        compiler_params=_params(("parallel", "arbitrary")),
    )(qkv, qkv, qkv, qkv, qkv, slopes, sinks)


def _swa_bwd(qkv, slopes, sinks, out, lse, dmix):
    T = qkv.shape[0]
    nb = T // WINDOW
    DO = FOX_W // GW
    assert FOX_W % GW == 0

    def body(q_ref, kp_ref, kc_ref, vp_ref, vc_ref, slope_ref, sink_ref, o_ref, lse_ref, do_ref,
             dq_ref, dk_ref, dv_ref, dsink_ref, sink_sc):
        n = pl.program_id(1)

        @pl.when(n == 0)
        def _():
            dk_ref[...] = jnp.zeros_like(dk_ref)
            dv_ref[...] = jnp.zeros_like(dv_ref)
            sink_sc[...] = jnp.zeros_like(sink_sc)

        qs, kb, s, valid = _swa_scores(q_ref, kp_ref, kc_ref, slope_ref, n)
        lse = lse_ref[...]
        p = jnp.where(valid, jnp.exp(s - lse), 0.0)
        vb = jnp.concatenate([vp_ref[...], vc_ref[...]], axis=0)
        do = jnp.concatenate([do_ref[:, t * HEAD_DIM:(t + 1) * HEAD_DIM] for t in range(SWA_GROUP)], axis=0)
        oo = jnp.concatenate([o_ref[:, t * HEAD_DIM:(t + 1) * HEAD_DIM] for t in range(SWA_GROUP)], axis=0)
        dp = _dot(do, vb, "nt")
        delta = jnp.sum(do.astype(F32) * oo.astype(F32), axis=-1, keepdims=True)
        ds = p * (dp - delta)
        dsb = ds.astype(BF16)
        dq = _dot(dsb, kb, "nn") * SCALE
        for t in range(SWA_GROUP):
            dq_ref[:, t * HEAD_DIM:(t + 1) * HEAD_DIM] = dq[t * WINDOW:(t + 1) * WINDOW, :]
        dkb = _dot(dsb, qs, "tn") * SCALE
        dvb = _dot(p.astype(BF16), do, "tn")
        r_prev = pl.ds(pl.multiple_of(jnp.maximum(n - 1, 0) * WINDOW, WINDOW), WINDOW)
        r_cur = pl.ds(pl.multiple_of(n * WINDOW, WINDOW), WINDOW)
        dk_ref[r_prev, :] += dkb[:WINDOW, :]
        dk_ref[r_cur, :] += dkb[WINDOW:, :]
        dv_ref[r_prev, :] += dvb[:WINDOW, :]
        dv_ref[r_cur, :] += dvb[WINDOW:, :]
        sink_sc[...] -= jnp.exp(sink_ref[...] - lse) * delta

        @pl.when(n == nb - 1)
        def _():
            tot = [jnp.zeros((1, 128), F32) + jnp.sum(sink_sc[t * WINDOW:(t + 1) * WINDOW, :]) for t in range(SWA_GROUP)]
            dsink_ref[...] = jnp.concatenate(tot + [jnp.zeros((8 - SWA_GROUP, 128), F32)], axis=0)

    q_spec, kp, kc, vp, vc, col = _swa_specs()
    kv_acc = pl.BlockSpec((T, HEAD_DIM), lambda g, n: (0, g))
    return pl.pallas_call(
        body, name="swa_bwd", grid=(SWA_KV_HEADS, nb),
        in_specs=[q_spec, kp, kc, vp, vc, col, col, pl.BlockSpec((WINDOW, GW), lambda g, n: (n, g)),
                  pl.BlockSpec((None, None, GR, 1), lambda g, n: (g, n, 0, 0)), pl.BlockSpec((WINDOW, GW), lambda g, n: (n, DO + g))],
        out_specs=[pl.BlockSpec((WINDOW, GW), lambda g, n: (n, g)), kv_acc, kv_acc, pl.BlockSpec((None, 8, 128), lambda g, n: (g, 0, 0))],
        out_shape=[jax.ShapeDtypeStruct((T, SWA_HEADS * HEAD_DIM), F32), jax.ShapeDtypeStruct((T, SWA_KV_HEADS * HEAD_DIM), F32),
                   jax.ShapeDtypeStruct((T, SWA_KV_HEADS * HEAD_DIM), F32), jax.ShapeDtypeStruct((SWA_KV_HEADS, 8, 128), F32)],
        scratch_shapes=[pltpu.VMEM((GR, 1), F32)],
        compiler_params=_params(("parallel", "arbitrary")),
    )(qkv, qkv, qkv, qkv, qkv, slopes, sinks, out, lse, dmix)


def _mem_fwd(qkv, mk, mv):
    T, ML = qkv.shape[0], mk.shape[0]
    tq = _tile(T, 512)
    HQ = C_MQ // HEAD_DIM

    def body(q_ref, k_ref, v_ref, o_ref, lse_ref):
        s = _dot(q_ref[...], k_ref[...], "nt") * SCALE
        m = jnp.max(s, axis=-1, keepdims=True)
        p = jnp.exp(s - m)
        l = jnp.sum(p, axis=-1, keepdims=True)
        o_ref[...] = (_dot(p.astype(BF16), v_ref[...], "nn") / l).astype(BF16)
        lse_ref[...] = m + jnp.log(l)

    kv = pl.BlockSpec((ML, HEAD_DIM), lambda h, i: (0, h))
    return pl.pallas_call(
        body, name="mem_fwd", grid=(MEM_HEADS, T // tq),
        in_specs=[pl.BlockSpec((tq, HEAD_DIM), lambda h, i: (i, HQ + h)), kv, kv],
        out_specs=[pl.BlockSpec((tq, HEAD_DIM), lambda h, i: (i, h)), pl.BlockSpec((None, tq, 1), lambda h, i: (h, i, 0))],
        out_shape=[jax.ShapeDtypeStruct((T, MEM_HEADS * HEAD_DIM), BF16), jax.ShapeDtypeStruct((MEM_HEADS, T, 1), F32)],
        compiler_params=_params(("parallel", "arbitrary")),
    )(qkv, mk, mv)


def _mem_bwd(qkv, mk, mv, out, lse, dmix):
    T, ML = qkv.shape[0], mk.shape[0]
    tq = _tile(T, 512)
    HQ = C_MQ // HEAD_DIM
    DO = (FOX_W + SWA_HEADS * HEAD_DIM) // HEAD_DIM

    def body(q_ref, k_ref, v_ref, o_ref, lse_ref, do_ref, dq_ref, dk_ref, dv_ref):
        q, k, v, do = q_ref[...], k_ref[...], v_ref[...], do_ref[...]
        p = jnp.exp(_dot(q, k, "nt") * SCALE - lse_ref[...])
        dp = _dot(do, v, "nt")
        delta = jnp.sum(do.astype(F32) * o_ref[...].astype(F32), axis=-1, keepdims=True)
        dsb = (p * (dp - delta)).astype(BF16)
        dq_ref[...] = _dot(dsb, k, "nn") * SCALE
        dk_part = _dot(dsb, q, "tn") * SCALE
        dv_part = _dot(p.astype(BF16), do, "tn")

        @pl.when(pl.program_id(1) == 0)
        def _():
            dk_ref[...] = dk_part
            dv_ref[...] = dv_part

        @pl.when(pl.program_id(1) > 0)
        def _():
            dk_ref[...] += dk_part
            dv_ref[...] += dv_part

    kv = pl.BlockSpec((ML, HEAD_DIM), lambda h, i: (0, h))
    qb = pl.BlockSpec((tq, HEAD_DIM), lambda h, i: (i, h))
    return pl.pallas_call(
        body, name="mem_bwd", grid=(MEM_HEADS, T // tq),
        in_specs=[pl.BlockSpec((tq, HEAD_DIM), lambda h, i: (i, HQ + h)), kv, kv, qb,
                  pl.BlockSpec((None, tq, 1), lambda h, i: (h, i, 0)), pl.BlockSpec((tq, HEAD_DIM), lambda h, i: (i, DO + h))],
        out_specs=[qb, kv, kv],
        out_shape=[jax.ShapeDtypeStruct((T, MEM_HEADS * HEAD_DIM), F32), jax.ShapeDtypeStruct((ML, MEM_HEADS * HEAD_DIM), F32),
                   jax.ShapeDtypeStruct((ML, MEM_HEADS * HEAD_DIM), F32)],
        compiler_params=_params(("parallel", "arbitrary")),
    )(qkv, mk, mv, out, lse, dmix)


HBM = pl.BlockSpec(memory_space=pltpu.HBM)


def _place():
    x, y, c = lax.axis_index("x"), lax.axis_index("y"), lax.axis_index("c")
    chips = [(1 - x, y), (x, 1 - y), (1 - x, 1 - y)]
    return x, y, c, chips


def _remote(src, dst, send_sem, recv_sem, device):
    return pltpu.make_async_remote_copy(src_ref=src, dst_ref=dst, send_sem=send_sem, recv_sem=recv_sem,
                                        device_id=device, device_id_type=MESH)


def _place_ids():
    x, y, c = lax.axis_index("x"), lax.axis_index("y"), lax.axis_index("c")
    order = [2 * x + y, 2 * (1 - x) + y, 2 * x + (1 - y), 2 * (1 - x) + (1 - y)]
    return jnp.stack([2 * x + y, c] + order).astype(jnp.int32)


def _cast_place(name, w, ids, after):
    R, C = w.shape
    tr = _tile(R, 256, 16)

    def body(ids_ref, w_ref, after_ref, o_ref):
        o_ref[...] = w_ref[...].astype(BF16)

    return pl.pallas_call(
        body, name=name,
        grid_spec=pltpu.PrefetchScalarGridSpec(
            num_scalar_prefetch=1, grid=(R // tr,),
            in_specs=[pl.BlockSpec((tr, C), lambda i, ids: (i, 0)), pl.BlockSpec(memory_space=pl.ANY)],
            out_specs=pl.BlockSpec((None, tr, C), lambda i, ids: (ids[0], i, 0))),
        out_shape=jax.ShapeDtypeStruct((N_CHIPS, R, C), BF16), compiler_params=_params(("parallel",)),
    )(ids, w, after)


SEM = pl.BlockSpec(memory_space=pltpu.SEMAPHORE)
EFFECT = pltpu.SideEffectType.DATAFLOW_SIDE_EFFECTING


def _hbm(a):
    return pltpu.with_memory_space_constraint(a, pltpu.HBM)


def _gather_start(name, placed, after):
    n = len(placed)

    ns = 3 * n

    def body(*refs):
        send, recv = refs[n + 1:n + 1 + ns], refs[n + 1 + ns:n + 1 + 2 * ns]
        buf = refs[n + 1 + 2 * ns:2 * n + 1 + 2 * ns]
        token = refs[2 * n + 1 + 2 * ns]
        x, y, c, chips = _place()
        me = 2 * x + y
        for a in range(n):
            half = buf[a].shape[1] // 2
            mine = buf[a].at[me, pl.ds(c * half, half)]
            for j, (cx, cy) in enumerate(chips):
                _remote(mine, mine, send[3 * a + j], recv[3 * a + j], (cx, cy, c)).start()
        token[...] = jnp.zeros_like(token)

    res = pl.pallas_call(
        body, name=name, in_specs=[HBM] * n + [pl.BlockSpec(memory_space=pl.ANY)],
        out_specs=[SEM] * (2 * ns) + [HBM] * n + [pl.BlockSpec(memory_space=pltpu.VMEM)],
        out_shape=[pltpu.SemaphoreType.DMA(())] * (2 * ns)
        + [pltpu.HBM(s.shape, s.dtype) for s in placed] + [jax.ShapeDtypeStruct((8, 128), F32)],
        input_output_aliases={a: 2 * ns + a for a in range(n)},
        compiler_params=pltpu.CompilerParams(has_side_effects=EFFECT),
    )(*[_hbm(s) for s in placed], after)
    return list(res[:ns]), list(res[ns:2 * ns]), list(res[2 * ns:2 * ns + n]), res[2 * ns + n]


def _gather_wait(name, send, recv, bufs, after):
    n = len(bufs)

    ns = 3 * n

    def body(*refs):
        buf = refs[:n]
        send_ref, recv_ref = refs[n:n + ns], refs[n + ns:n + 2 * ns]
        x, y, c, chips = _place()
        ids = [2 * cx + cy for cx, cy in chips]
        for a in range(n):
            half = buf[a].shape[1] // 2
            for j in range(3):
                landed = buf[a].at[ids[j], pl.ds(c * half, half)]
                cp = _remote(landed, landed, send_ref[3 * a + j], recv_ref[3 * a + j], (x, y, c))
                cp.wait_send()
                cp.wait_recv()

    res = pl.pallas_call(
        body, name=name, in_specs=[HBM] * n + [SEM] * (2 * ns) + [pl.BlockSpec(memory_space=pl.ANY)], out_specs=[HBM] * n,
        out_shape=[pltpu.HBM(s.shape, s.dtype) for s in bufs], input_output_aliases={a: a for a in range(n)},
        compiler_params=pltpu.CompilerParams(has_side_effects=EFFECT),
    )(*bufs, *send, *recv, after)
    return list(res)


def _gather_forward(name, bufs):
    n = len(bufs)

    def body(*refs):
        buf = refs[n:2 * n]
        send, recv = refs[2 * n:]
        x, y, c, chips = _place()
        ids = [2 * cx + cy for cx, cy in chips]
        copies = []
        for a in range(n):
            half = buf[a].shape[1] // 2
            for j in range(3):
                landed = buf[a].at[ids[j], pl.ds(c * half, half)]
                cp = _remote(landed, landed, send.at[a, j], recv.at[a, j], (x, y, 1 - c))
                cp.start()
                copies.append(cp)
        for a in range(n):
            half = buf[a].shape[1] // 2
            for j in range(3):
                landed = buf[a].at[ids[j], pl.ds((1 - c) * half, half)]
                _remote(landed, landed, send.at[a, j], recv.at[a, j], (x, y, c)).wait_recv()
        for cp in copies:
            cp.wait_send()

    return pl.pallas_call(
        body, name=name, in_specs=[HBM] * n, out_specs=[HBM] * n,
        out_shape=[jax.ShapeDtypeStruct(s.shape, s.dtype) for s in bufs], input_output_aliases={a: a for a in range(n)},
        scratch_shapes=[pltpu.SemaphoreType.DMA((n, 3)), pltpu.SemaphoreType.DMA((n, 3))],
    )(*bufs)


def _pair_start(name, grads):
    n = len(grads)
    ns = N_CHIPS * n

    def body(*refs):
        send, recv = refs[2 * n:2 * n + ns], refs[2 * n + ns:2 * n + 2 * ns]
        src = refs[2 * n + 2 * ns:3 * n + 2 * ns]
        land = refs[3 * n + 2 * ns:4 * n + 2 * ns]
        token = refs[4 * n + 2 * ns]
        x, y, c, chips = _place()
        order = [2 * x + y] + [2 * cx + cy for cx, cy in chips]
        for a in range(n):
            half = src[a].shape[1] // 2
            for j in range(N_CHIPS):
                _remote(src[a].at[order[j], pl.ds((1 - c) * half, half)], land[a].at[j],
                        send[N_CHIPS * a + j], recv[N_CHIPS * a + j], (x, y, 1 - c)).start()
        token[...] = jnp.zeros_like(token)

    lands = [jax.ShapeDtypeStruct((N_CHIPS, g.shape[1] // 2, g.shape[2]), g.dtype) for g in grads]
    res = pl.pallas_call(
        body, name=name, in_specs=[HBM] * (2 * n),
        out_specs=[SEM] * (2 * ns) + [HBM] * (2 * n) + [pl.BlockSpec(memory_space=pltpu.VMEM)],
        out_shape=[pltpu.SemaphoreType.DMA(())] * (2 * ns) + [pltpu.HBM(g.shape, g.dtype) for g in grads]
        + [pltpu.HBM(l.shape, l.dtype) for l in lands] + [jax.ShapeDtypeStruct((8, 128), F32)],
        input_output_aliases={a: 2 * ns + a for a in range(2 * n)},
        compiler_params=pltpu.CompilerParams(has_side_effects=EFFECT),
    )(*[_hbm(g) for g in grads], *[_hbm(lax.empty(l.shape, l.dtype)) for l in lands])
    return list(res[:ns]), list(res[ns:2 * ns]), list(res[2 * ns:2 * ns + n]), list(res[2 * ns + n:2 * ns + 2 * n]), res[2 * ns + 2 * n]


def _pair_wait(name, send, recv, grads, lands, after):
    n = len(grads)
    ns = N_CHIPS * n

    def body(*refs):
        src, land = refs[:n], refs[n:2 * n]
        send_ref, recv_ref = refs[2 * n:2 * n + ns], refs[2 * n + ns:2 * n + 2 * ns]
        x, y, c, _ = _place()
        for a in range(n):
            for j in range(N_CHIPS):
                cp = _remote(land[a].at[j], land[a].at[j], send_ref[N_CHIPS * a + j], recv_ref[N_CHIPS * a + j], (x, y, c))
                cp.wait_send()
                cp.wait_recv()

    res = pl.pallas_call(
        body, name=name, in_specs=[HBM] * (2 * n) + [SEM] * (2 * ns) + [pl.BlockSpec(memory_space=pl.ANY)],
        out_specs=[HBM] * (2 * n), out_shape=[pltpu.HBM(g.shape, g.dtype) for g in grads] + [pltpu.HBM(l.shape, l.dtype) for l in lands],
        input_output_aliases={a: a for a in range(2 * n)},
        compiler_params=pltpu.CompilerParams(has_side_effects=EFFECT),
    )(*grads, *lands, *send, *recv, after)
    return list(res[:n]), list(res[n:])


def _chip_start(name, parts):
    n = len(parts)
    ns = 3 * n

    def body(*refs):
        send, recv = refs[2 * n:2 * n + ns], refs[2 * n + ns:2 * n + 2 * ns]
        src = refs[2 * n + 2 * ns:3 * n + 2 * ns]
        land = refs[3 * n + 2 * ns:4 * n + 2 * ns]
        token = refs[4 * n + 2 * ns]
        x, y, c, chips = _place()
        for a in range(n):
            for j, (cx, cy) in enumerate(chips):
                _remote(src[a].at[j], land[a].at[j], send[3 * a + j], recv[3 * a + j], (cx, cy, c)).start()
        token[...] = jnp.zeros_like(token)

    res = pl.pallas_call(
        body, name=name, in_specs=[HBM] * (2 * n),
        out_specs=[SEM] * (2 * ns) + [HBM] * (2 * n) + [pl.BlockSpec(memory_space=pltpu.VMEM)],
        out_shape=[pltpu.SemaphoreType.DMA(())] * (2 * ns) + [pltpu.HBM(p.shape, p.dtype) for p in parts] * 2
        + [jax.ShapeDtypeStruct((8, 128), F32)],
        input_output_aliases={a: 2 * ns + a for a in range(2 * n)},
        compiler_params=pltpu.CompilerParams(has_side_effects=EFFECT),
    )(*[_hbm(p) for p in parts], *[_hbm(lax.empty(p.shape, p.dtype)) for p in parts])
    return list(res[:ns]), list(res[ns:2 * ns]), list(res[2 * ns:2 * ns + n]), list(res[2 * ns + n:2 * ns + 2 * n]), res[2 * ns + 2 * n]


def _chip_wait(name, send, recv, parts, lands, after):
    n = len(parts)
    ns = 3 * n

    def body(*refs):
        src, land = refs[:n], refs[n:2 * n]
        send_ref, recv_ref = refs[2 * n:2 * n + ns], refs[2 * n + ns:2 * n + 2 * ns]
        x, y, c, _ = _place()
        for a in range(n):
            for j in range(3):
                cp = _remote(src[a].at[j], land[a].at[j], send_ref[3 * a + j], recv_ref[3 * a + j], (x, y, c))
                cp.wait_send()
                cp.wait_recv()

    res = pl.pallas_call(
        body, name=name, in_specs=[HBM] * (2 * n) + [SEM] * (2 * ns) + [pl.BlockSpec(memory_space=pl.ANY)],
        out_specs=[HBM] * (2 * n), out_shape=[pltpu.HBM(p.shape, p.dtype) for p in parts] * 2,
        input_output_aliases={a: a for a in range(2 * n)},
        compiler_params=pltpu.CompilerParams(has_side_effects=EFFECT),
    )(*parts, *lands, *send, *recv, after)
    return list(res[n:])


def _pair_share(name, shards):
    n = len(shards)

    def body(*refs):
        buf = refs[n:2 * n]
        send, recv = refs[2 * n:]
        x, y, c, _ = _place()
        copies = []
        for a in range(n):
            half = buf[a].shape[0] // 2
            mine = buf[a].at[pl.ds(c * half, half)]
            cp = _remote(mine, mine, send.at[a], recv.at[a], (x, y, 1 - c))
            cp.start()
            copies.append(cp)
        for a, cp in enumerate(copies):
            half = buf[a].shape[0] // 2
            cp.wait_send()
            theirs = buf[a].at[pl.ds((1 - c) * half, half)]
            _remote(theirs, theirs, send.at[a], recv.at[a], (x, y, c)).wait_recv()

    return pl.pallas_call(
        body, name=name, in_specs=[HBM] * n, out_specs=[HBM] * n,
        out_shape=[jax.ShapeDtypeStruct(s.shape, s.dtype) for s in shards], input_output_aliases={a: a for a in range(n)},
        scratch_shapes=[pltpu.SemaphoreType.DMA((n,)), pltpu.SemaphoreType.DMA((n,))],
    )(*shards)


def _small_start(buf):
    R, W = buf.shape
    ns = N_DEV - 1

    def body(*refs):
        send, recv = refs[2:2 + ns], refs[2 + ns:2 + 2 * ns]
        src, land, token = refs[2 + 2 * ns], refs[3 + 2 * ns], refs[4 + 2 * ns]
        x, y, c, _ = _place()
        me = 4 * x + 2 * y + c
        for k in range(1, N_DEV):
            peer = (x ^ (k >> 2), y ^ ((k >> 1) & 1), c ^ (k & 1))
            _remote(src, land.at[me], send[k - 1], recv[k - 1], peer).start()
        token[...] = jnp.zeros_like(token)

    res = pl.pallas_call(
        body, name="small_start", in_specs=[HBM, HBM],
        out_specs=[SEM] * (2 * ns) + [HBM, HBM, pl.BlockSpec(memory_space=pltpu.VMEM)],
        out_shape=[pltpu.SemaphoreType.DMA(())] * (2 * ns) + [pltpu.HBM((R, W), F32), pltpu.HBM((N_DEV, R, W), F32),
                                                                jax.ShapeDtypeStruct((8, 128), F32)],
        input_output_aliases={0: 2 * ns, 1: 2 * ns + 1},
        compiler_params=pltpu.CompilerParams(has_side_effects=EFFECT),
    )(_hbm(buf), _hbm(jnp.zeros((N_DEV, R, W), F32)))
    return list(res[:ns]), list(res[ns:2 * ns]), res[2 * ns], res[2 * ns + 1], res[2 * ns + 2]


def _small_wait(send, recv, buf, land, after):
    ns = N_DEV - 1

    def body(*refs):
        land_ref = refs[1]
        send_ref, recv_ref = refs[2:2 + ns], refs[2 + ns:2 + 2 * ns]
        x, y, c, _ = _place()
        me = 4 * x + 2 * y + c
        for k in range(1, N_DEV):
            landed = land_ref.at[me ^ k]
            cp = _remote(landed, landed, send_ref[k - 1], recv_ref[k - 1], (x, y, c))
            cp.wait_send()
            cp.wait_recv()

    return pl.pallas_call(
        body, name="small_wait", in_specs=[HBM, HBM] + [SEM] * (2 * ns) + [pl.BlockSpec(memory_space=pl.ANY)],
        out_specs=[HBM, HBM], out_shape=[pltpu.HBM(buf.shape, buf.dtype), pltpu.HBM(land.shape, land.dtype)],
        input_output_aliases={0: 0, 1: 1}, compiler_params=pltpu.CompilerParams(has_side_effects=EFFECT),
    )(buf, land, *send, *recv, after)


def _small_sum(buf, land):
    def body(buf_ref, land_ref, out_ref):
        x, y, c, _ = _place()
        me = 4 * x + 2 * y + c
        total = None
        for d in range(N_DEV):
            term = jnp.where(me == d, buf_ref[...], land_ref[d])
            total = term if total is None else total + term
        out_ref[...] = total

    return pl.pallas_call(body, name="small_sum", out_shape=jax.ShapeDtypeStruct(buf.shape, F32))(buf, land)


def _pair_sum_bf16(name, grad, theirs, ids):
    _, R2, C = theirs.shape
    tr = _tile(R2, 256, 16)
    nrb = R2 // tr

    def body(ids_ref, a_ref, b_ref, o_ref):
        o_ref[...] = (a_ref[...] + b_ref[...]).astype(BF16)

    return pl.pallas_call(
        body, name=name,
        grid_spec=pltpu.PrefetchScalarGridSpec(
            num_scalar_prefetch=1, grid=(3, nrb),
            in_specs=[pl.BlockSpec((None, tr, C), lambda j, i, ids: (ids[3 + j], ids[1] * nrb + i, 0)),
                      pl.BlockSpec((None, tr, C), lambda j, i, ids: (j + 1, i, 0))],
            out_specs=pl.BlockSpec((None, tr, C), lambda j, i, ids: (j, i, 0))),
        out_shape=jax.ShapeDtypeStruct((3, R2, C), BF16), compiler_params=_params(("parallel", "parallel")),
    )(ids, grad, theirs)


def _chip_sum(name, grad, theirs, arrived, ids):
    _, R2, C = theirs.shape
    tr = _tile(R2, 256, 16)
    nrb = R2 // tr

    def body(ids_ref, a_ref, b_ref, r_ref, o_ref):
        tot = a_ref[...] + b_ref[...]
        for j in range(3):
            tot = tot + r_ref[j].astype(F32)
        o_ref[...] = tot

    return pl.pallas_call(
        body, name=name,
        grid_spec=pltpu.PrefetchScalarGridSpec(
            num_scalar_prefetch=1, grid=(nrb,),
            in_specs=[pl.BlockSpec((None, tr, C), lambda i, ids: (ids[0], ids[1] * nrb + i, 0)),
                      pl.BlockSpec((None, tr, C), lambda i, ids: (0, i, 0)),
                      pl.BlockSpec((3, tr, C), lambda i, ids: (0, i, 0))],
            out_specs=pl.BlockSpec((tr, C), lambda i, ids: (ids[1] * nrb + i, 0))),
        out_shape=jax.ShapeDtypeStruct((2 * R2, C), F32), compiler_params=_params(("parallel",)),
    )(ids, grad, theirs, arrived)


def _adamw(name, w, g, m, v):
    R, C = w.shape
    tr = _tile(R, 128, 8)
    c1 = 1.0 / (1.0 - ADAM_B1 ** ADAM_STEP)
    c2 = 1.0 / (1.0 - ADAM_B2 ** ADAM_STEP)

    def body(w_ref, g_ref, m_ref, v_ref, d_ref, mo_ref, vo_ref):
        gv = g_ref[...]
        mn = ADAM_B1 * m_ref[...] + (1.0 - ADAM_B1) * gv
        vn = ADAM_B2 * v_ref[...] + (1.0 - ADAM_B2) * (gv * gv)
        d_ref[...] = -ADAM_LR * ((mn * c1) / (jnp.sqrt(vn * c2) + ADAM_EPS) + ADAM_WD * w_ref[...])
        mo_ref[...] = mn
        vo_ref[...] = vn

    spec = pl.BlockSpec((tr, C), lambda i: (i, 0))
    sds = jax.ShapeDtypeStruct((R, C), F32)
    return pl.pallas_call(body, name=name, grid=(R // tr,), in_specs=[spec] * 4, out_specs=[spec] * 3, out_shape=[sds] * 3,
                          compiler_params=_params(("parallel",)))(w, g, m, v)


SMALL = ["ffn1_norm", "mix_norm", "mem_norm", "forget_bias", "fox_q_gain", "fox_k_gain", "swa_q_gain", "swa_k_gain", "swa_sinks",
         "mem_q_gain", "mem_k_gain", "ffn2_norm"]
LARGE = ["ffn1_gate", "ffn1_up", "ffn1_down", "w_in", "w_mem_k", "w_mem_v", "w_out", "ffn2_gate", "ffn2_up", "ffn2_down"]
GATHER_GROUPS = [["ffn1_gate", "ffn1_up"], ["ffn1_down", "w_in", "w_mem_k", "w_mem_v"], ["w_out", "ffn2_gate", "ffn2_up", "ffn2_down"]]
WEIGHTS = ["ffn1_norm", "ffn1_gate", "ffn1_up", "ffn1_down", "mix_norm", "mem_norm", "w_in", "forget_bias", "w_mem_k", "w_mem_v",
           "fox_q_gain", "fox_k_gain", "swa_q_gain", "swa_k_gain", "swa_sinks", "mem_q_gain", "mem_k_gain", "w_out", "ffn2_norm",
           "ffn2_gate", "ffn2_up", "ffn2_down"]


def _pad_proj_cols(w):
    out = jnp.zeros((w.shape[0], PROJ_W), w.dtype)
    for start, width, pstart in REF_GROUPS:
        out = lax.dynamic_update_slice(out, w[:, start:start + width], (0, pstart))
    return out


def _unpad_proj_cols(w):
    return jnp.concatenate([w[:, pstart:pstart + width] for _, width, pstart in REF_GROUPS], axis=1)


def _pack_small(vals):
    flat = jnp.concatenate([vals[k].reshape(-1).astype(F32) for k in SMALL + ["loss"]])
    n = flat.shape[0]
    total = -(-n // 1024) * 1024
    return jnp.pad(flat, (0, total - n)).reshape(total // 128, 128)


def _unpack_small(buf, shapes):
    flat = buf.reshape(-1)
    out, off = {}, 0
    for k in SMALL + ["loss"]:
        size = int(np.prod(shapes[k]))
        out[k] = flat[off:off + size].reshape(shapes[k])
        off += size
    return out


def kernel(x, mem, ffn1_norm, ffn1_gate, ffn1_up, ffn1_down, mix_norm, mem_norm, w_in, forget_bias, w_mem_k, w_mem_v, fox_q_gain, fox_k_gain, swa_q_gain, swa_k_gain, swa_sinks, mem_q_gain, mem_k_gain, w_out, ffn2_norm, ffn2_gate, ffn2_up, ffn2_down, loss_target, m_ffn1_norm, m_ffn1_gate, m_ffn1_up, m_ffn1_down, m_mix_norm, m_mem_norm, m_w_in, m_forget_bias, m_w_mem_k, m_w_mem_v, m_fox_q_gain, m_fox_k_gain, m_swa_q_gain, m_swa_k_gain, m_swa_sinks, m_mem_q_gain, m_mem_k_gain, m_w_out, m_ffn2_norm, m_ffn2_gate, m_ffn2_up, m_ffn2_down, v_ffn1_norm, v_ffn1_gate, v_ffn1_up, v_ffn1_down, v_mix_norm, v_mem_norm, v_w_in, v_forget_bias, v_w_mem_k, v_w_mem_v, v_fox_q_gain, v_fox_k_gain, v_swa_q_gain, v_swa_k_gain, v_swa_sinks, v_mem_q_gain, v_mem_k_gain, v_w_out, v_ffn2_norm, v_ffn2_gate, v_ffn2_up, v_ffn2_down):
    given = dict(locals())
    T, D = x.shape[1], x.shape[2]
    ML = mem.shape[1]
    xin = x.reshape(T, D)
    target = loss_target.reshape(T, D)
    memin = mem.reshape(ML, D)

    ids = _place_ids()
    shard = {k: given[k][0] for k in LARGE}
    started, after = [], ids
    for gi, group in enumerate(GATHER_GROUPS):
        if "w_in" in group:
            shard["w_in"] = _pad_proj_cols(shard["w_in"] + after[0, 0])
        placed = [_cast_place("cast_" + k, shard[k], ids, after) for k in group]
        send, recv, bufs, after = _gather_start("gather_start_%d" % gi, placed, after)
        started.append((send, recv, bufs))

    def arrive(gi, done):
        send, recv, bufs = started[gi]
        bufs = _gather_wait("gather_wait_%d" % gi, send, recv, bufs, done)
        return dict(zip(GATHER_GROUPS[gi], _gather_forward("gather_forward_%d" % gi, bufs)))

    gains = jnp.concatenate([fox_q_gain, fox_k_gain, swa_q_gain, swa_k_gain, mem_q_gain,
                             jnp.pad(forget_bias, ((0, 0), (0, HEAD_DIM - FOX_HEADS))), jnp.zeros((2, HEAD_DIM), F32)], axis=0)
    slopes_np = 2.0 ** (-8.0 * np.arange(1, SWA_HEADS + 1) / SWA_HEADS)
    slopes = jnp.asarray(np.repeat(slopes_np, WINDOW).reshape(SWA_KV_HEADS, GR, 1), F32)
    sinks = jnp.repeat(swa_sinks.reshape(SWA_HEADS), WINDOW).reshape(SWA_KV_HEADS, GR, 1)

    h1 = _rms_fwd("ffn1_norm_fwd", xin, ffn1_norm + after[0, 0])
    full = arrive(0, h1)
    wg1, wu1 = full["ffn1_gate"], full["ffn1_up"]
    g1, u1, a1 = _ffn_gu("ffn1_gate_up", h1, wg1, wu1)
    full = arrive(1, a1)
    wd1 = full["ffn1_down"].reshape(-1, D)
    win = full["w_in"].reshape(D, PROJ_W)
    wmk = full["w_mem_k"].reshape(D, MEM_HEADS * HEAD_DIM)
    wmv = full["w_mem_v"].reshape(D, MEM_HEADS * HEAD_DIM)
    x1 = _ffn_down("ffn1_down", a1, wd1, xin)
    h2 = _rms_fwd("mix_norm_fwd", x1, mix_norm)
    proj = _mm2d("proj_in", h2, win, "nn", F32, tn=1408, tk=2048, n_outer=True)
    qkv, logf, k_t, v_t = _prep_fwd(proj, gains)
    cum = _cumsum_rows("forget_cumsum", [logf], False)
    cum_h = cum[:, :FOX_HEADS].T
    cq_row = cum_h.reshape(FOX_HEADS, 1, T)
    ck_rep = jnp.broadcast_to(cum_h[:, :, None], (FOX_HEADS, T, HEAD_DIM))
    mn = _rms_fwd("mem_norm_fwd", memin, mem_norm)
    mk_raw = _mm2d("mem_k_proj", mn, wmk, "nn", F32)
    mv = _mm2d("mem_v_proj", mn, wmv, "nn", BF16)
    mk = _head_norm_rows(mk_raw, mem_k_gain)
    out_a, out_a_f32, lse_a = _fox_fwd(qkv, v_t, cq_row, ck_rep)
    out_b, lse_b = _swa_fwd(qkv, slopes, sinks)
    out_c, lse_c = _mem_fwd(qkv, mk, mv)
    mixed = jnp.concatenate([out_a, out_b, out_c], axis=1)
    full = arrive(2, mixed)
    wo = full["w_out"].reshape(-1, D)
    wg2, wu2, wd2 = full["ffn2_gate"], full["ffn2_up"], full["ffn2_down"].reshape(-1, D)
    x2 = _mm2d("mix_out", mixed, wo, "nn", F32, tk=2048, n_outer=True, extras=[x1], epilogue=lambda accs, ex: [ex[0] + accs[0]])
    h3 = _rms_fwd("ffn2_norm_fwd", x2, ffn2_norm)
    g2, u2, a2 = _ffn_gu("ffn2_gate_up", h3, wg2, wu2)
    x3 = _ffn_down("ffn2_down", a2, wd2, x2)
    dx3, dyb3, loss_part = _loss_head(x3, target)

    grads, small, res = {}, {"loss": loss_part[0, 0]}, {}

    def pair_off(tag, group):
        send, recv, own, lands, token = _pair_start("grad_pair_start_" + tag, [grads[k] for k in group])
        return (group, send, recv, own, lands), token

    def chip_off(tag, started, done):
        group, send, recv, own, lands = started
        own, theirs = _pair_wait("grad_pair_wait_" + tag, send, recv, own, lands, done)
        grads.update(zip(group, own))
        to_chips = [_pair_sum_bf16("pair_sum_" + k, grads[k], b, ids) for k, b in zip(group, theirs)]
        send, recv, parts, lands, token = _chip_start("grad_chip_start_" + tag, to_chips)
        return (group, theirs, send, recv, parts, lands), token

    def finish(tag, state, done):
        group, theirs, send, recv, parts, lands = state
        arrived = _chip_wait("grad_chip_wait_" + tag, send, recv, parts, lands, done)
        halves = [_chip_sum("chip_sum_" + k, grads[k], b, r, ids) for k, b, r in zip(group, theirs, arrived)]
        reduced = dict(zip(group, _pair_share("grad_pair_share_" + tag, halves)))
        last = None
        for k in group:
            gk = _unpad_proj_cols(reduced[k]) if k == "w_in" else reduced[k]
            d, mo, vo = _adamw("adamw_" + k, given[k][0], gk, given["m_" + k][0], given["v_" + k][0])
            res[k] = tuple(t[None] for t in (gk, d, mo, vo))
            last = vo
        return last

    dg2, du2 = _ffn_bwd_act("ffn2", dyb3, wd2, g2, u2, N_CHIPS)
    grads["ffn2_down"] = _ffn_bwd_down("ffn2", a2, dyb3, N_CHIPS).reshape(N_CHIPS, -1, D)
    grads["ffn2_gate"], grads["ffn2_up"] = _ffn_bwd_gate_up("ffn2", h3, dg2, du2, N_CHIPS)
    started, token = pair_off("a", ["ffn2_gate", "ffn2_up", "ffn2_down"])
    dh3 = _ffn_bwd_x("ffn2", dg2, du2, wg2, wu2, token)
    state_a, token = chip_off("a", started, dh3)
    dx2, dx2b, small["ffn2_norm"] = _rms_bwd("ffn2_norm_bwd", dh3, x2, ffn2_norm + token[0, 0], dx3, 1.0)
    dmix = _mm2d("mix_out_dx", dx2b, wo, "nt", BF16, tk=2048, n_outer=True)
    grads["w_out"] = _mm2d("mix_out_dw", mixed, dx2b, "tn", F32, tk=1024).reshape(N_CHIPS, -1, D)
    delta_row = _fox_delta(dmix, out_a_f32)[:, :FOX_HEADS].T.reshape(FOX_HEADS, 1, T)
    dfq, dfk, dfv, dck, dcq = _fox_bwd(qkv, k_t, cq_row, ck_rep, delta_row, lse_a, dmix)
    dsq, dsk, dsv, dsink = _swa_bwd(qkv, slopes, sinks, out_b, lse_b, dmix)
    dmq, dmk, dmv = _mem_bwd(qkv, mk, mv, out_c, lse_c, dmix)
    small["swa_sinks"] = dsink[:, :SWA_GROUP, 0].reshape(1, SWA_HEADS)
    dcum = [jnp.pad(d.reshape(FOX_HEADS, T).T, ((0, 0), (0, HEAD_DIM - FOX_HEADS))) for d in (dck, dcq)]
    dlogf = _cumsum_rows("forget_cumsum_bwd", dcum, True)
    dproj, dgains = _prep_bwd(proj, gains, dfq, dfk, dfv, dsq, dsk, dsv, dmq, dlogf)
    for row, k in enumerate(["fox_q_gain", "fox_k_gain", "swa_q_gain", "swa_k_gain", "mem_q_gain"]):
        small[k] = dgains[row:row + 1, :]
    small["forget_bias"] = dgains[5:6, :FOX_HEADS]
    grads["w_in"] = _mm2d("proj_in_dw", h2, dproj, "tn", F32, tn=1408, tk=1024).reshape(N_CHIPS, -1, PROJ_W)
    dmk_raw, small["mem_k_gain"] = _head_norm_rows_bwd(mk_raw, mem_k_gain, dmk)
    dmvb = dmv.astype(BF16)
    grads["w_mem_k"] = _mm2d("mem_k_dw", mn, dmk_raw, "tn", F32).reshape(N_CHIPS, -1, MEM_HEADS * HEAD_DIM)
    grads["w_mem_v"] = _mm2d("mem_v_dw", mn, dmvb, "tn", F32).reshape(N_CHIPS, -1, MEM_HEADS * HEAD_DIM)
    dmn = _mm2d("mem_k_dx", dmk_raw, wmk, "nt", F32)
    dmn = _mm2d("mem_v_dx", dmvb, wmv, "nt", F32, extras=[dmn], epilogue=lambda accs, ex: [ex[0] + accs[0]])
    _, _, small["mem_norm"] = _rms_bwd("mem_norm_bwd", dmn, memin, mem_norm, jnp.zeros_like(memin), 1.0)
    started, token = pair_off("b", ["w_out", "w_in", "w_mem_k", "w_mem_v"])
    dh2 = _mm2d("proj_in_dx", dproj, win, "nt", F32, tm=1024, tk=1408, after=token)
    state_b, token = chip_off("b", started, dh2)
    dx1, dyb1, small["mix_norm"] = _rms_bwd("mix_norm_bwd", dh2, x1, mix_norm + token[0, 0], dx2, 0.5)
    grads["ffn1_down"] = _ffn_bwd_down("ffn1", a1, dyb1, N_CHIPS).reshape(N_CHIPS, -1, D)
    started, token = pair_off("c", ["ffn1_down"])
    dg1, du1 = _ffn_bwd_act("ffn1", dyb1, wd1, g1, u1, N_CHIPS, after=token)
    state_c, token = chip_off("c", started, dg1)
    grads["ffn1_gate"], grads["ffn1_up"] = _ffn_bwd_gate_up("ffn1", h1, dg1, du1, N_CHIPS, after=token)
    started, token = pair_off("d", ["ffn1_gate", "ffn1_up"])
    dh1 = _ffn_bwd_x("ffn1", dg1, du1, wg1, wu1, token)
    state_d, token = chip_off("d", started, dh1)
    grad_x, _, small["ffn1_norm"] = _rms_bwd("ffn1_norm_bwd", dh1, xin, ffn1_norm + token[0, 0], dx1, 1.0)

    s_send, s_recv, s_buf, s_land, token = _small_start(_pack_small(small))

    done = finish("a", state_a, token)
    done = finish("b", state_b, done)
    done = finish("c", state_c, done)
    done = finish("d", state_d, done)

    shapes = {k: given[k].shape for k in SMALL}
    shapes["loss"] = ()
    s_buf, s_land = _small_wait(s_send, s_recv, s_buf, s_land, done)
    red_small = _unpack_small(_small_sum(s_buf, s_land), shapes)
    loss = red_small["loss"]
    zero = {"loss": jnp.zeros((), F32)}
    packed = [_pack_small({**zero, **{k: src[k] for k in SMALL}}) for src in (
        {k: given[k] for k in SMALL}, red_small, {k: given["m_" + k] for k in SMALL}, {k: given["v_" + k] for k in SMALL})]
    d_s, m_s, v_s = (_unpack_small(t, shapes) for t in _adamw("adamw_small", *packed))
    for k in SMALL:
        res[k] = (red_small[k], d_s[k], m_s[k], v_s[k])

    outs = [loss, grad_x.reshape(1, T, D)]
    for part in range(4):
        outs += [res[k][part] for k in WEIGHTS]
    return tuple(outs)
```

```python
import functools

import numpy as np
import jax
import jax.numpy as jnp
from jax import lax
from jax.experimental import pallas as pl
from jax.experimental.pallas import tpu as pltpu

F32 = jnp.float32
BF16 = jnp.bfloat16
MESH = pl.DeviceIdType.MESH

HEAD_DIM = 128
FOX_HEADS = 6
SWA_HEADS = 6
SWA_KV_HEADS = 2
SWA_GROUP = SWA_HEADS // SWA_KV_HEADS
MEM_HEADS = 4
WINDOW = 128
EPS = 1e-6
NEG_INF = -1e30
SCALE = HEAD_DIM ** -0.5

C_FQ = 0
C_FK = C_FQ + FOX_HEADS * HEAD_DIM
C_FV = C_FK + FOX_HEADS * HEAD_DIM
C_SQ = C_FV + FOX_HEADS * HEAD_DIM
C_SK = C_SQ + SWA_HEADS * HEAD_DIM
C_SV = C_SK + SWA_KV_HEADS * HEAD_DIM
C_MQ = C_SV + SWA_KV_HEADS * HEAD_DIM
C_FL = C_MQ + MEM_HEADS * HEAD_DIM
PROJ_W = C_FL + HEAD_DIM
FOX_W = FOX_HEADS * HEAD_DIM
REF_GROUPS = [
    (0, FOX_W, C_FQ), (FOX_W, FOX_W, C_FK), (2 * FOX_W, FOX_W, C_FV), (3 * FOX_W, FOX_HEADS, C_FL),
    (3 * FOX_W + FOX_HEADS, SWA_HEADS * HEAD_DIM, C_SQ),
    (3 * FOX_W + FOX_HEADS + SWA_HEADS * HEAD_DIM, SWA_KV_HEADS * HEAD_DIM, C_SK),
    (3 * FOX_W + FOX_HEADS + (SWA_HEADS + SWA_KV_HEADS) * HEAD_DIM, SWA_KV_HEADS * HEAD_DIM, C_SV),
    (3 * FOX_W + FOX_HEADS + (SWA_HEADS + 2 * SWA_KV_HEADS) * HEAD_DIM, MEM_HEADS * HEAD_DIM, C_MQ),
]

ADAM_LR = 0.001
ADAM_B1 = 0.9
ADAM_B2 = 0.999
ADAM_EPS = 1e-08
ADAM_WD = 0.01
ADAM_STEP = 10

V7X_VMEM_LIMIT = 56 * 1024 * 1024
N_CHIPS = 4
N_DEV = 8


def _tile(n, pref, mult=128):
    t = (min(pref, n) // mult) * mult
    while t >= mult:
        if n % t == 0:
            return t
        t -= mult
    return n


def _params(sem):
    return pltpu.CompilerParams(dimension_semantics=sem, vmem_limit_bytes=V7X_VMEM_LIMIT)


_DIMS = {"nn": (((1,), (0,)), ((), ())), "nt": (((1,), (1,)), ((), ())), "tn": (((0,), (0,)), ((), ()))}


def _dot(a, b, mode):
    return lax.dot_general(a, b, _DIMS[mode], preferred_element_type=F32)


def _mm(name, grid, pairs, acc_of, acc_shapes, extras, outs, epilogue, after=None):
    n_p, n_e, n_o, n_a = len(pairs), len(extras), len(outs), len(acc_shapes)
    n_w = 0 if after is None else 1
    nk = grid[2]
    n_in = sum(1 if a is None else 2 for a, *_ in pairs)

    def body(*refs):
        ex = refs[n_in:n_in + n_e]
        out = refs[n_in + n_e + n_w:n_in + n_e + n_w + n_o]
        accs = refs[n_in + n_e + n_w + n_o:]
        parts = [None] * n_a
        at = 0
        for p in range(n_p):
            if pairs[p][0] is None:
                a_ref, b_ref = refs[0], refs[at]
                at += 1
            else:
                a_ref, b_ref = refs[at], refs[at + 1]
                at += 2
            d = _dot(a_ref[...], b_ref[...], pairs[p][4])
            parts[acc_of[p]] = d if parts[acc_of[p]] is None else parts[acc_of[p]] + d

        def finish(vals):
            for o, r in zip(out, epilogue(vals, [e[...] for e in ex])):
                o[...] = r.astype(o.dtype)

        if nk == 1:
            finish(parts)
            return
        k = pl.program_id(2)

        @pl.when(k == 0)
        def _():
            for a, d in zip(accs, parts):
                a[...] = d

        @pl.when((k > 0) & (k < nk - 1))
        def _():
            for a, d in zip(accs, parts):
                a[...] += d

        @pl.when(k == nk - 1)
        def _():
            finish([a[...] + d for a, d in zip(accs, parts)])

    in_specs, args = [], []
    for a, a_spec, b, b_spec, _ in pairs:
        if a is not None:
            in_specs.append(a_spec)
            args.append(a)
        in_specs.append(b_spec)
        args.append(b)
    for e, e_spec in extras:
        in_specs.append(e_spec)
        args.append(e)
    if after is not None:
        in_specs.append(pl.BlockSpec(memory_space=pl.ANY))
        args.append(after)
    res = pl.pallas_call(
        body, name=name, grid=grid, in_specs=in_specs,
        out_specs=[s for _, s in outs], out_shape=[o for o, _ in outs],
        scratch_shapes=[pltpu.VMEM(s, F32) for s in acc_shapes] if nk > 1 else [],
        compiler_params=_params(("parallel", "parallel", "arbitrary")),
    )(*args)
    return res


def _mm2d(name, a, b, mode, out_dtype, tm=512, tn=1024, tk=1024, extras=(), epilogue=None, n_out=1, after=None, n_outer=False):
    if mode == "nn":
        (M, K), N = a.shape, b.shape[1]
    elif mode == "nt":
        (M, K), N = a.shape, b.shape[0]
    else:
        (K, M), N = a.shape, b.shape[1]
    tm, tn, tk = _tile(M, tm), _tile(N, tn), _tile(K, tk)

    def spec(shape, index):
        if n_outer:
            return pl.BlockSpec(shape, lambda j, i, k: index(i, j, k))
        return pl.BlockSpec(shape, index)

    a_spec = spec((tk, tm), lambda i, j, k: (k, i)) if mode == "tn" else spec((tm, tk), lambda i, j, k: (i, k))
    b_spec = spec((tn, tk), lambda i, j, k: (j, k)) if mode == "nt" else spec((tk, tn), lambda i, j, k: (k, j))
    mn = spec((tm, tn), lambda i, j, k: (i, j))
    if epilogue is None:
        epilogue = lambda accs, ex: [accs[0]]
    if not isinstance(out_dtype, (list, tuple)):
        out_dtype = [out_dtype] * n_out
    grid = (N // tn, M // tm, K // tk) if n_outer else (M // tm, N // tn, K // tk)
    res = _mm(name, grid, [(a, a_spec, b, b_spec, mode)], [0], [(tm, tn)],
              [(e, mn) for e in extras], [(jax.ShapeDtypeStruct((M, N), d), mn) for d in out_dtype], epilogue, after=after)
    return res[0] if len(res) == 1 else res


def _sigmoid(x):
    return 1.0 / (1.0 + jnp.exp(-x))


def _sigmoid_fast(x):
    return pl.reciprocal(1.0 + jnp.exp(-x), approx=True)


def _ffn_gu(name, h, wg, wu):
    T, D = h.shape
    nf, _, F4 = wg.shape
    tm, tk = _tile(T, 512), _tile(D, 2048)
    a_spec = pl.BlockSpec((tm, tk), lambda j, i, k: (i, k))
    b_spec = pl.BlockSpec((None, tk, F4), lambda j, i, k: (j, k, 0))
    o_spec = pl.BlockSpec((tm, F4), lambda j, i, k: (i, j))

    def epilogue(accs, ex):
        g, u = accs
        return [g, u, g * _sigmoid_fast(g) * u]

    sds = jax.ShapeDtypeStruct((T, nf * F4), BF16)
    return _mm(name, (nf, T // tm, D // tk), [(h, a_spec, wg, b_spec, "nn"), (None, None, wu, b_spec, "nn")], [0, 1],
               [(tm, F4), (tm, F4)], [], [(sds, o_spec)] * 3, epilogue)


def _ffn_down(name, a, wd, xres):
    return _mm2d(name, a, wd, "nn", F32, tm=1024, tk=wd.shape[0] // N_CHIPS, extras=[xres],
                 epilogue=lambda accs, ex: [ex[0] + 0.5 * accs[0]])


def _ffn_bwd_down(tag, a, dyb, nf, after=None):
    return _mm2d(tag + "_dwd", a, dyb, "tn", F32, tm=a.shape[1] // nf, tk=1024, after=after)


def _ffn_bwd_act(tag, dyb, wd, g, u, nf, after=None):
    def act_bwd(accs, ex):
        gf, uf = ex[0].astype(F32), ex[1].astype(F32)
        s = _sigmoid_fast(gf)
        return [accs[0] * uf * s * (1.0 + gf * (1.0 - s)), accs[0] * gf * s]

    return _mm2d(tag + "_da", dyb, wd, "nt", BF16, tn=wd.shape[0] // nf, tk=2048, extras=[g, u], epilogue=act_bwd, n_out=2,
                 n_outer=True, after=after)


def _ffn_bwd_gate_up(tag, h, dg, du, nf, after=None):
    T, D = h.shape
    F4 = dg.shape[1] // nf
    tm, tk = _tile(D, 512), _tile(T, 1024)
    h_spec = pl.BlockSpec((tk, tm), lambda i, j, k: (k, i))
    d_spec = pl.BlockSpec((tk, F4), lambda i, j, k: (k, j))
    w_spec = pl.BlockSpec((None, tm, F4), lambda i, j, k: (j, i, 0))
    sds = jax.ShapeDtypeStruct((nf, D, F4), F32)
    return _mm(tag + "_dwgu", (D // tm, nf, T // tk), [(h, h_spec, dg, d_spec, "tn"), (None, None, du, d_spec, "tn")],
               [0, 1], [(tm, F4), (tm, F4)], [], [(sds, w_spec)] * 2, lambda accs, ex: accs, after=after)


def _ffn_bwd_x(tag, dg, du, wg, wu, after):
    T = dg.shape[0]
    nf, D, F4 = wg.shape
    tm, tn = _tile(T, 1024), _tile(D, 1024)
    a_spec = pl.BlockSpec((tm, F4), lambda i, j, k: (i, k))
    b_spec = pl.BlockSpec((None, tn, F4), lambda i, j, k: (k, j, 0))
    o_spec = pl.BlockSpec((tm, tn), lambda i, j, k: (i, j))
    (dh,) = _mm(tag + "_dh", (T // tm, D // tn, nf), [(dg, a_spec, wg, b_spec, "nt"), (du, a_spec, wu, b_spec, "nt")],
                [0, 0], [(tm, tn)], [], [(jax.ShapeDtypeStruct((T, D), F32), o_spec)], lambda accs, ex: accs, after=after)
    return dh


def _rms_fwd(name, x, gain):
    R, D = x.shape
    tr = _tile(R, 256, 8)

    def body(x_ref, g_ref, o_ref):
        xv = x_ref[...]
        r = lax.rsqrt(jnp.mean(xv * xv, axis=-1, keepdims=True) + EPS)
        o_ref[...] = (xv * r * g_ref[...]).astype(BF16)

    return pl.pallas_call(
        body, name=name, grid=(R // tr,),
        in_specs=[pl.BlockSpec((tr, D), lambda i: (i, 0)), pl.BlockSpec((1, D), lambda i: (0, 0))],
        out_specs=pl.BlockSpec((tr, D), lambda i: (i, 0)), out_shape=jax.ShapeDtypeStruct((R, D), BF16),
        compiler_params=_params(("parallel",)),
    )(x, gain)


def _rms_bwd(name, dh, x, gain, dres, bscale):
    R, D = x.shape
    tr = _tile(R, 256, 8)

    def body(dh_ref, x_ref, g_ref, dres_ref, dx_ref, dxb_ref, dg_ref):
        xv, dy = x_ref[...], dh_ref[...]
        r = lax.rsqrt(jnp.mean(xv * xv, axis=-1, keepdims=True) + EPS)
        xn = xv * r
        uu = dy * g_ref[...]
        dx = dres_ref[...] + r * (uu - xn * jnp.mean(xn * uu, axis=-1, keepdims=True))
        dx_ref[...] = dx
        dxb_ref[...] = (bscale * dx).astype(BF16)
        part = jnp.sum(dy * xn, axis=0, keepdims=True)

        @pl.when(pl.program_id(0) == 0)
        def _():
            dg_ref[...] = part

        @pl.when(pl.program_id(0) > 0)
        def _():
            dg_ref[...] += part

    row = pl.BlockSpec((tr, D), lambda i: (i, 0))
    vec = pl.BlockSpec((1, D), lambda i: (0, 0))
    return pl.pallas_call(
        body, name=name, grid=(R // tr,), in_specs=[row, row, vec, row], out_specs=[row, row, vec],
        out_shape=[jax.ShapeDtypeStruct((R, D), F32), jax.ShapeDtypeStruct((R, D), BF16), jax.ShapeDtypeStruct((1, D), F32)],
        compiler_params=_params(("arbitrary",)),
    )(dh, x, gain, dres)


def _loss_head(y, target):
    R, D = y.shape
    tr = _tile(R, 256, 8)

    def body(y_ref, t_ref, d_ref, db_ref, l_ref):
        e = y_ref[...] - t_ref[...]
        d = e * (1.0 / D)
        d_ref[...] = d
        db_ref[...] = (0.5 * d).astype(BF16)
        part = jnp.zeros((8, 128), F32) + (0.5 / D) * jnp.sum(e * e)

        @pl.when(pl.program_id(0) == 0)
        def _():
            l_ref[...] = part

        @pl.when(pl.program_id(0) > 0)
        def _():
            l_ref[...] += part

    row = pl.BlockSpec((tr, D), lambda i: (i, 0))
    acc = pl.BlockSpec((8, 128), lambda i: (0, 0))
    return pl.pallas_call(
        body, name="loss_head", grid=(R // tr,), in_specs=[row, row], out_specs=[row, row, acc],
        out_shape=[jax.ShapeDtypeStruct((R, D), F32), jax.ShapeDtypeStruct((R, D), BF16), jax.ShapeDtypeStruct((8, 128), F32)],
        compiler_params=_params(("arbitrary",)),
    )(y, target)


def _head_norm(xs, g):
    r = lax.rsqrt(jnp.mean(xs * xs, axis=-1, keepdims=True) + EPS)
    return xs * r * g


def _head_norm_bwd(xs, g, dy):
    r = lax.rsqrt(jnp.mean(xs * xs, axis=-1, keepdims=True) + EPS)
    xn = xs * r
    uu = dy * g
    return r * (uu - xn * jnp.mean(xn * uu, axis=-1, keepdims=True)), jnp.sum(dy * xn, axis=0, keepdims=True)


NORMED = [(C_FQ, FOX_HEADS, 0), (C_FK, FOX_HEADS, 1), (C_SQ, SWA_HEADS, 2), (C_SK, SWA_KV_HEADS, 3), (C_MQ, MEM_HEADS, 4)]
PLAIN = [(C_FV, FOX_HEADS), (C_SV, SWA_KV_HEADS)]


def _prep_fwd(proj, gains):
    T = proj.shape[0]
    tr = _tile(T, 256, 128)

    def body(p_ref, g_ref, o_ref, lf_ref, kt_ref, vt_ref):
        for start, heads, row in NORMED:
            gn = g_ref[row:row + 1, :]
            for hh in range(heads):
                sl = slice(start + hh * HEAD_DIM, start + (hh + 1) * HEAD_DIM)
                y = _head_norm(p_ref[:, sl], gn)
                o_ref[:, sl] = y.astype(BF16)
                if start == C_FK:
                    kt_ref[hh * HEAD_DIM:(hh + 1) * HEAD_DIM, :] = y.T.astype(BF16)
        for start, heads in PLAIN:
            sl = slice(start, start + heads * HEAD_DIM)
            o_ref[:, sl] = p_ref[:, sl].astype(BF16)
        for hh in range(FOX_HEADS):
            sl = slice(C_FV + hh * HEAD_DIM, C_FV + (hh + 1) * HEAD_DIM)
            vt_ref[hh * HEAD_DIM:(hh + 1) * HEAD_DIM, :] = p_ref[:, sl].T.astype(BF16)
        zb = p_ref[:, C_FL:C_FL + HEAD_DIM] + g_ref[5:6, :]
        o_ref[:, C_FL:C_FL + HEAD_DIM] = jnp.zeros((tr, HEAD_DIM), BF16)
        lf_ref[...] = jnp.minimum(zb, 0.0) - jnp.log(1.0 + jnp.exp(-jnp.abs(zb)))

    return pl.pallas_call(
        body, name="prep_fwd", grid=(T // tr,),
        in_specs=[pl.BlockSpec((tr, PROJ_W), lambda i: (i, 0)), pl.BlockSpec((8, 128), lambda i: (0, 0))],
        out_specs=[pl.BlockSpec((tr, PROJ_W), lambda i: (i, 0)), pl.BlockSpec((tr, HEAD_DIM), lambda i: (i, 0)),
                   pl.BlockSpec((FOX_W, tr), lambda i: (0, i)), pl.BlockSpec((FOX_W, tr), lambda i: (0, i))],
        out_shape=[jax.ShapeDtypeStruct((T, PROJ_W), BF16), jax.ShapeDtypeStruct((T, HEAD_DIM), F32),
                   jax.ShapeDtypeStruct((FOX_W, T), BF16), jax.ShapeDtypeStruct((FOX_W, T), BF16)],
        compiler_params=_params(("parallel",)),
    )(proj, gains)


def _prep_bwd(proj, gains, dfq, dfk, dfv, dsq, dsk, dsv, dmq, dlogf):
    T = proj.shape[0]
    tr = _tile(T, 256, 8)
    d_normed = {C_FQ: 0, C_FK: 1, C_SQ: 3, C_SK: 4, C_MQ: 6}
    d_plain = {C_FV: 2, C_SV: 5}

    def body(p_ref, g_ref, *rest):
        d_refs, dlf_ref, o_ref, dg_ref = rest[:7], rest[7], rest[8], rest[9]
        rows = []
        for start, heads, row in NORMED:
            gn = g_ref[row:row + 1, :]
            d_ref = d_refs[d_normed[start]]
            tot = jnp.zeros((1, HEAD_DIM), F32)
            for hh in range(heads):
                sl = slice(start + hh * HEAD_DIM, start + (hh + 1) * HEAD_DIM)
                dx, dgn = _head_norm_bwd(p_ref[:, sl], gn, d_ref[:, hh * HEAD_DIM:(hh + 1) * HEAD_DIM])
                o_ref[:, sl] = dx.astype(BF16)
                tot = tot + dgn
            rows.append(tot)
        for start, heads in PLAIN:
            o_ref[:, start:start + heads * HEAD_DIM] = d_refs[d_plain[start]][...].astype(BF16)
        zb = p_ref[:, C_FL:C_FL + HEAD_DIM] + g_ref[5:6, :]
        lane = lax.broadcasted_iota(jnp.int32, (tr, HEAD_DIM), 1)
        dz = jnp.where(lane < FOX_HEADS, dlf_ref[...] * (1.0 - _sigmoid(zb)), 0.0)
        o_ref[:, C_FL:C_FL + HEAD_DIM] = dz.astype(BF16)
        rows.append(jnp.sum(dz, axis=0, keepdims=True))
        part = jnp.concatenate(rows + [jnp.zeros((2, HEAD_DIM), F32)], axis=0)

        @pl.when(pl.program_id(0) == 0)
        def _():
            dg_ref[...] = part

        @pl.when(pl.program_id(0) > 0)
        def _():
            dg_ref[...] += part

    def rows_of(w):
        return pl.BlockSpec((tr, w), lambda i: (i, 0))

    small = pl.BlockSpec((8, 128), lambda i: (0, 0))
    ds = [dfq, dfk, dfv, dsq, dsk, dsv, dmq]
    return pl.pallas_call(
        body, name="prep_bwd", grid=(T // tr,),
        in_specs=[rows_of(PROJ_W), small] + [rows_of(d.shape[1]) for d in ds] + [rows_of(HEAD_DIM)],
        out_specs=[rows_of(PROJ_W), small],
        out_shape=[jax.ShapeDtypeStruct((T, PROJ_W), BF16), jax.ShapeDtypeStruct((8, 128), F32)],
        compiler_params=_params(("arbitrary",)),
    )(proj, gains, *ds, dlogf)


def _head_norm_rows(x, gain):
    R, W = x.shape

    def body(x_ref, g_ref, o_ref):
        for hh in range(W // HEAD_DIM):
            sl = slice(hh * HEAD_DIM, (hh + 1) * HEAD_DIM)
            o_ref[:, sl] = _head_norm(x_ref[:, sl], g_ref[...]).astype(BF16)

    return pl.pallas_call(body, name="mem_k_norm", out_shape=jax.ShapeDtypeStruct((R, W), BF16))(x, gain)


def _head_norm_rows_bwd(x, gain, dy):
    R, W = x.shape

    def body(x_ref, g_ref, dy_ref, dx_ref, dg_ref):
        tot = jnp.zeros((1, HEAD_DIM), F32)
        for hh in range(W // HEAD_DIM):
            sl = slice(hh * HEAD_DIM, (hh + 1) * HEAD_DIM)
            dx, dgn = _head_norm_bwd(x_ref[:, sl], g_ref[...], dy_ref[:, sl])
            dx_ref[:, sl] = dx.astype(BF16)
            tot = tot + dgn
        dg_ref[...] = tot

    return pl.pallas_call(
        body, name="mem_k_norm_bwd",
        out_shape=[jax.ShapeDtypeStruct((R, W), BF16), jax.ShapeDtypeStruct((1, HEAD_DIM), F32)])(x, gain, dy)


def _cumsum_rows(name, xs, reverse, columns=None):
    T, W = xs[0].shape
    tb = _tile(T, 512, 8)
    nb = T // tb
    n_in = len(xs) + (0 if columns is None else 1)

    def body(*refs):
        o_ref, carry = refs[n_in], refs[n_in + 1]

        @pl.when(pl.program_id(0) == 0)
        def _():
            carry[...] = jnp.zeros_like(carry)

        xv = refs[0][...]
        for x_ref in refs[1:len(xs)]:
            xv = xv + x_ref[...]
        if columns is not None:
            lane = lax.broadcasted_iota(jnp.int32, (tb, W), 1)
            for hh in range(columns.shape[0]):
                xv = xv + jnp.where(lane == hh, refs[len(xs)][hh], 0.0)
        r = lax.broadcasted_iota(jnp.int32, (tb, tb), 0)
        cc = lax.broadcasted_iota(jnp.int32, (tb, tb), 1)
        tri = jnp.where((cc >= r) if reverse else (cc <= r), 1.0, 0.0).astype(F32)
        o_ref[...] = jnp.dot(tri, xv, precision=lax.Precision.HIGHEST, preferred_element_type=F32) + carry[...]
        carry[...] += jnp.sum(xv, axis=0, keepdims=True)

    idx = (lambda i: (nb - 1 - i, 0)) if reverse else (lambda i: (i, 0))
    in_specs = [pl.BlockSpec((tb, W), idx)] * len(xs)
    if columns is not None:
        in_specs.append(pl.BlockSpec((columns.shape[0], tb, 1), lambda i: (0, idx(i)[0], 0)))
    return pl.pallas_call(
        body, name=name, grid=(nb,), in_specs=in_specs, out_specs=pl.BlockSpec((tb, W), idx),
        out_shape=jax.ShapeDtypeStruct((T, W), F32), scratch_shapes=[pltpu.VMEM((1, W), F32)],
        compiler_params=_params(("arbitrary",)),
    )(*xs, *([] if columns is None else [columns]))


def _triangle(nq, by_column):
    if by_column:
        blocks = [(i, j) for j in range(nq) for i in range(j, nq)]
    else:
        blocks = [(i, j) for i in range(nq) for j in range(i + 1)]
    return jnp.asarray(np.array(blocks, np.int32).T)


def _fox_scores_t(k, q, cq_row, ck_rep, on_diagonal):
    n = q.shape[0]
    s = _dot(k, q, "nt") * SCALE + (cq_row - jnp.tile(ck_rep, (1, n // HEAD_DIM)))
    if on_diagonal:
        s = jnp.where(lax.broadcasted_iota(jnp.int32, (n, n), 0) <= lax.broadcasted_iota(jnp.int32, (n, n), 1), s, NEG_INF)
    return s


def _fox_fwd(qkv, v_t, cq_row, ck_rep):
    T = qkv.shape[0]
    tq = _tile(T, 512)
    nq = T // tq
    steps = nq * (nq + 1) // 2
    HQ, HK = C_FQ // HEAD_DIM, C_FK // HEAD_DIM

    def body(tab, q_ref, k_ref, vt_ref, cq_ref, ck_ref, o_ref, of_ref, lse_ref, m_sc, l_sc, acc_sc):
        i, j = tab[0, pl.program_id(1)], tab[1, pl.program_id(1)]

        @pl.when(j == 0)
        def _():
            m_sc[...] = jnp.full_like(m_sc, NEG_INF)
            l_sc[...] = jnp.zeros_like(l_sc)
            acc_sc[...] = jnp.zeros_like(acc_sc)

        def step(on_diagonal):
            s = _fox_scores_t(k_ref[...], q_ref[...], cq_ref[...], ck_ref[...], on_diagonal)
            m_new = jnp.maximum(m_sc[...], jnp.max(s, axis=0, keepdims=True))
            alpha = jnp.exp(m_sc[...] - m_new)
            p = jnp.exp(s - m_new)
            l_sc[...] = alpha * l_sc[...] + jnp.sum(p, axis=0, keepdims=True)
            acc_sc[...] = alpha * acc_sc[...] + _dot(vt_ref[...], p.astype(BF16), "nn")
            m_sc[...] = m_new

        @pl.when(j < i)
        def _():
            step(False)

        @pl.when(j == i)
        def _():
            step(True)
            o = (acc_sc[...] / l_sc[...]).T
            o_ref[...] = o.astype(BF16)
            of_ref[...] = o
            lse_ref[...] = m_sc[...] + jnp.log(l_sc[...])

    qrow = pl.BlockSpec((None, 1, tq), lambda h, s, tab: (h, 0, tab[0, s]))
    return pl.pallas_call(
        body, name="fox_fwd",
        grid_spec=pltpu.PrefetchScalarGridSpec(
            num_scalar_prefetch=1, grid=(FOX_HEADS, steps),
            in_specs=[pl.BlockSpec((tq, HEAD_DIM), lambda h, s, tab: (tab[0, s], HQ + h)),
                      pl.BlockSpec((tq, HEAD_DIM), lambda h, s, tab: (tab[1, s], HK + h)),
                      pl.BlockSpec((HEAD_DIM, tq), lambda h, s, tab: (h, tab[1, s])), qrow,
                      pl.BlockSpec((None, tq, HEAD_DIM), lambda h, s, tab: (h, tab[1, s], 0))],
            out_specs=[pl.BlockSpec((tq, HEAD_DIM), lambda h, s, tab: (tab[0, s], h)),
                       pl.BlockSpec((tq, HEAD_DIM), lambda h, s, tab: (tab[0, s], h)), qrow],
            scratch_shapes=[pltpu.VMEM((1, tq), F32), pltpu.VMEM((1, tq), F32), pltpu.VMEM((HEAD_DIM, tq), F32)]),
        out_shape=[jax.ShapeDtypeStruct((T, FOX_W), BF16), jax.ShapeDtypeStruct((T, FOX_W), F32),
                   jax.ShapeDtypeStruct((FOX_HEADS, 1, T), F32)],
        compiler_params=_params(("parallel", "arbitrary")),
    )(_triangle(nq, False), qkv, qkv, v_t, cq_row, ck_rep)


def _fox_delta(dmix, out_f32):
    T = out_f32.shape[0]
    tr = _tile(T, 512, 8)

    def body(do_ref, o_ref, d_ref):
        lane = lax.broadcasted_iota(jnp.int32, (tr, HEAD_DIM), 1)
        acc = jnp.zeros((tr, HEAD_DIM), F32)
        for hh in range(FOX_HEADS):
            sl = slice(hh * HEAD_DIM, (hh + 1) * HEAD_DIM)
            d = jnp.sum(do_ref[:, sl].astype(F32) * o_ref[:, sl], axis=-1, keepdims=True)
            acc = jnp.where(lane == hh, d, acc)
        d_ref[...] = acc

    blk = pl.BlockSpec((tr, FOX_W), lambda i: (i, 0))
    return pl.pallas_call(
        body, name="fox_delta", grid=(T // tr,), in_specs=[blk, blk], out_specs=pl.BlockSpec((tr, HEAD_DIM), lambda i: (i, 0)),
        out_shape=jax.ShapeDtypeStruct((T, HEAD_DIM), F32), compiler_params=_params(("parallel",)),
    )(dmix, out_f32)


def _fox_bwd(qkv, k_t, cq_row, ck_rep, delta_row, lse, dmix):
    T = qkv.shape[0]
    tq = _tile(T, 512)
    nq = T // tq
    steps = nq * (nq + 1) // 2
    HQ, HK, HV = C_FQ // HEAD_DIM, C_FK // HEAD_DIM, C_FV // HEAD_DIM

    def body(tab, q_ref, k_ref, kt_ref, v_ref, cq_ref, ck_ref, delta_ref, lse_ref, do_ref,
             dq_ref, dk_ref, dv_ref, dck_ref, dcq_ref, dk_sc, dv_sc, dc_sc, dqt_sc):
        qi, kj = tab[0, pl.program_id(1)], tab[1, pl.program_id(1)]

        @pl.when(qi == kj)
        def _():
            dk_sc[...] = jnp.zeros_like(dk_sc)
            dv_sc[...] = jnp.zeros_like(dv_sc)
            dc_sc[...] = jnp.zeros_like(dc_sc)

        def step(on_diagonal):
            q, k, v, do = q_ref[...], k_ref[...], v_ref[...], do_ref[...]
            p = jnp.exp(_fox_scores_t(k, q, cq_ref[...], ck_ref[...], on_diagonal) - lse_ref[...])
            dp = _dot(v, do, "nt")
            ds = p * (dp - delta_ref[...])
            dsb = ds.astype(BF16)
            dv_sc[...] += _dot(p.astype(BF16), do, "nn")
            dk_sc[...] += _dot(dsb, q, "nn")
            dc_sc[...] += jnp.sum(ds, axis=1, keepdims=True)
            dq_part = _dot(kt_ref[...], dsb, "nn") * SCALE
            dcq_part = jnp.sum(ds, axis=0, keepdims=True)

            @pl.when(kj == 0)
            def _():
                dqt_sc[qi] = dq_part
                dcq_ref[qi] = dcq_part

            @pl.when(kj > 0)
            def _():
                dqt_sc[qi] += dq_part
                dcq_ref[qi] += dcq_part

            if on_diagonal:
                dq_ref[...] = dqt_sc[qi].T

        @pl.when(qi > kj)
        def _():
            step(False)

        @pl.when(qi == kj)
        def _():
            step(True)

        @pl.when(qi == nq - 1)
        def _():
            dk_ref[...] = dk_sc[...] * SCALE
            dv_ref[...] = dv_sc[...]
            dck_ref[...] = -dc_sc[...]

    def rows(base):
        return pl.BlockSpec((tq, HEAD_DIM), lambda h, s, tab: (tab[0, s], base + h))

    def cols(base):
        return pl.BlockSpec((tq, HEAD_DIM), lambda h, s, tab: (tab[1, s], base + h))

    qrow = pl.BlockSpec((None, 1, tq), lambda h, s, tab: (h, 0, tab[0, s]))
    sds = jax.ShapeDtypeStruct((T, FOX_W), F32)
    return pl.pallas_call(
        body, name="fox_bwd",
        grid_spec=pltpu.PrefetchScalarGridSpec(
            num_scalar_prefetch=1, grid=(FOX_HEADS, steps),
            in_specs=[rows(HQ), cols(HK), pl.BlockSpec((HEAD_DIM, tq), lambda h, s, tab: (h, tab[1, s])), cols(HV), qrow,
                      pl.BlockSpec((None, tq, HEAD_DIM), lambda h, s, tab: (h, tab[1, s], 0)), qrow, qrow, rows(0)],
            out_specs=[cols(0), cols(0), cols(0), pl.BlockSpec((None, tq, 1), lambda h, s, tab: (h, tab[1, s], 0)),
                       pl.BlockSpec((None, nq, 1, tq), lambda h, s, tab: (h, 0, 0, 0))],
            scratch_shapes=[pltpu.VMEM((tq, HEAD_DIM), F32), pltpu.VMEM((tq, HEAD_DIM), F32), pltpu.VMEM((tq, 1), F32),
                            pltpu.VMEM((nq, HEAD_DIM, tq), F32)]),
        out_shape=[sds, sds, sds, jax.ShapeDtypeStruct((FOX_HEADS, T, 1), F32), jax.ShapeDtypeStruct((FOX_HEADS, nq, 1, tq), F32)],
        compiler_params=_params(("parallel", "arbitrary")),
    )(_triangle(nq, True), qkv, qkv, k_t, qkv, cq_row, ck_rep, delta_row, lse, dmix)


GW = SWA_GROUP * HEAD_DIM
GR = SWA_GROUP * WINDOW


def _swa_scores(q_ref, kp_ref, kc_ref, slope_ref, n):
    q = q_ref[...]
    qs = jnp.concatenate([q[:, t * HEAD_DIM:(t + 1) * HEAD_DIM] for t in range(SWA_GROUP)], axis=0)
    kb = jnp.concatenate([kp_ref[...], kc_ref[...]], axis=0)
    r = lax.broadcasted_iota(jnp.int32, (GR, 2 * WINDOW), 0) & (WINDOW - 1)
    jj = lax.broadcasted_iota(jnp.int32, (GR, 2 * WINDOW), 1)
    dist = WINDOW + r - jj
    valid = (dist >= 0) & (dist < WINDOW) & ((n > 0) | (jj >= WINDOW))
    s = _dot(qs, kb, "nt") * SCALE - slope_ref[...] * dist.astype(F32)
    return qs, kb, jnp.where(valid, s, NEG_INF), valid


def _swa_specs():
    HQ, HK, HV = C_SQ // GW, C_SK // HEAD_DIM, C_SV // HEAD_DIM
    q_spec = pl.BlockSpec((WINDOW, GW), lambda g, n: (n, HQ + g))

    def prev(base):
        return pl.BlockSpec((WINDOW, HEAD_DIM), lambda g, n: (jnp.maximum(n - 1, 0), base + g))

    def cur(base):
        return pl.BlockSpec((WINDOW, HEAD_DIM), lambda g, n: (n, base + g))

    col = pl.BlockSpec((None, GR, 1), lambda g, n: (g, 0, 0))
    return q_spec, prev(HK), cur(HK), prev(HV), cur(HV), col


def _swa_fwd(qkv, slopes, sinks):
    T = qkv.shape[0]
    nb = T // WINDOW
    assert C_SQ % GW == 0

    def body(q_ref, kp_ref, kc_ref, vp_ref, vc_ref, slope_ref, sink_ref, o_ref, lse_ref):
        n = pl.program_id(1)
        _, _, s, _ = _swa_scores(q_ref, kp_ref, kc_ref, slope_ref, n)
        m = jnp.maximum(jnp.max(s, axis=-1, keepdims=True), sink_ref[...])
        p = jnp.exp(s - m)
        l = jnp.sum(p, axis=-1, keepdims=True) + jnp.exp(sink_ref[...] - m)
        vb = jnp.concatenate([vp_ref[...], vc_ref[...]], axis=0)
        o = _dot(p.astype(BF16), vb, "nn") / l
        for t in range(SWA_GROUP):
            o_ref[:, t * HEAD_DIM:(t + 1) * HEAD_DIM] = o[t * WINDOW:(t + 1) * WINDOW, :].astype(BF16)
        lse_ref[...] = m + jnp.log(l)

    q_spec, kp, kc, vp, vc, col = _swa_specs()
    return pl.pallas_call(
        body, name="swa_fwd", grid=(SWA_KV_HEADS, nb), in_specs=[q_spec, kp, kc, vp, vc, col, col],
        out_specs=[pl.BlockSpec((WINDOW, GW), lambda g, n: (n, g)), pl.BlockSpec((None, None, GR, 1), lambda g, n: (g, n, 0, 0))],
        out_shape=[jax.ShapeDtypeStruct((T, SWA_HEADS * HEAD_DIM), BF16), jax.ShapeDtypeStruct((SWA_KV_HEADS, nb, GR, 1), F32)],
        compiler_params=_params(("parallel", "arbitrary")),
    )(qkv, qkv, qkv, qkv, qkv, slopes, sinks)


def _swa_bwd(qkv, slopes, sinks, out, lse, dmix):
    T = qkv.shape[0]
    nb = T // WINDOW
    DO = FOX_W // GW
    assert FOX_W % GW == 0

    def body(q_ref, kp_ref, kc_ref, vp_ref, vc_ref, slope_ref, sink_ref, o_ref, lse_ref, do_ref,
             dq_ref, dk_ref, dv_ref, dsink_ref, sink_sc):
        n = pl.program_id(1)

        @pl.when(n == 0)
        def _():
            dk_ref[...] = jnp.zeros_like(dk_ref)
            dv_ref[...] = jnp.zeros_like(dv_ref)
            sink_sc[...] = jnp.zeros_like(sink_sc)

        qs, kb, s, valid = _swa_scores(q_ref, kp_ref, kc_ref, slope_ref, n)
        lse = lse_ref[...]
        p = jnp.where(valid, jnp.exp(s - lse), 0.0)
        vb = jnp.concatenate([vp_ref[...], vc_ref[...]], axis=0)
        do = jnp.concatenate([do_ref[:, t * HEAD_DIM:(t + 1) * HEAD_DIM] for t in range(SWA_GROUP)], axis=0)
        oo = jnp.concatenate([o_ref[:, t * HEAD_DIM:(t + 1) * HEAD_DIM] for t in range(SWA_GROUP)], axis=0)
        dp = _dot(do, vb, "nt")
        delta = jnp.sum(do.astype(F32) * oo.astype(F32), axis=-1, keepdims=True)
        ds = p * (dp - delta)
        dsb = ds.astype(BF16)
        dq = _dot(dsb, kb, "nn") * SCALE
        for t in range(SWA_GROUP):
            dq_ref[:, t * HEAD_DIM:(t + 1) * HEAD_DIM] = dq[t * WINDOW:(t + 1) * WINDOW, :]
        dkb = _dot(dsb, qs, "tn") * SCALE
        dvb = _dot(p.astype(BF16), do, "tn")
        r_prev = pl.ds(pl.multiple_of(jnp.maximum(n - 1, 0) * WINDOW, WINDOW), WINDOW)
        r_cur = pl.ds(pl.multiple_of(n * WINDOW, WINDOW), WINDOW)
        dk_ref[r_prev, :] += dkb[:WINDOW, :]
        dk_ref[r_cur, :] += dkb[WINDOW:, :]
        dv_ref[r_prev, :] += dvb[:WINDOW, :]
        dv_ref[r_cur, :] += dvb[WINDOW:, :]
        sink_sc[...] -= jnp.exp(sink_ref[...] - lse) * delta

        @pl.when(n == nb - 1)
        def _():
            tot = [jnp.zeros((1, 128), F32) + jnp.sum(sink_sc[t * WINDOW:(t + 1) * WINDOW, :]) for t in range(SWA_GROUP)]
            dsink_ref[...] = jnp.concatenate(tot + [jnp.zeros((8 - SWA_GROUP, 128), F32)], axis=0)

    q_spec, kp, kc, vp, vc, col = _swa_specs()
    kv_acc = pl.BlockSpec((T, HEAD_DIM), lambda g, n: (0, g))
    return pl.pallas_call(
        body, name="swa_bwd", grid=(SWA_KV_HEADS, nb),
        in_specs=[q_spec, kp, kc, vp, vc, col, col, pl.BlockSpec((WINDOW, GW), lambda g, n: (n, g)),
                  pl.BlockSpec((None, None, GR, 1), lambda g, n: (g, n, 0, 0)), pl.BlockSpec((WINDOW, GW), lambda g, n: (n, DO + g))],
        out_specs=[pl.BlockSpec((WINDOW, GW), lambda g, n: (n, g)), kv_acc, kv_acc, pl.BlockSpec((None, 8, 128), lambda g, n: (g, 0, 0))],
        out_shape=[jax.ShapeDtypeStruct((T, SWA_HEADS * HEAD_DIM), F32), jax.ShapeDtypeStruct((T, SWA_KV_HEADS * HEAD_DIM), F32),
                   jax.ShapeDtypeStruct((T, SWA_KV_HEADS * HEAD_DIM), F32), jax.ShapeDtypeStruct((SWA_KV_HEADS, 8, 128), F32)],
        scratch_shapes=[pltpu.VMEM((GR, 1), F32)],
        compiler_params=_params(("parallel", "arbitrary")),
    )(qkv, qkv, qkv, qkv, qkv, slopes, sinks, out, lse, dmix)


def _mem_fwd(qkv, mk, mv):
    T, ML = qkv.shape[0], mk.shape[0]
    tq = _tile(T, 512)
    HQ = C_MQ // HEAD_DIM

    def body(q_ref, k_ref, v_ref, o_ref, lse_ref):
        s = _dot(q_ref[...], k_ref[...], "nt") * SCALE
        m = jnp.max(s, axis=-1, keepdims=True)
        p = jnp.exp(s - m)
        l = jnp.sum(p, axis=-1, keepdims=True)
        o_ref[...] = (_dot(p.astype(BF16), v_ref[...], "nn") / l).astype(BF16)
        lse_ref[...] = m + jnp.log(l)

    kv = pl.BlockSpec((ML, HEAD_DIM), lambda h, i: (0, h))
    return pl.pallas_call(
        body, name="mem_fwd", grid=(MEM_HEADS, T // tq),
        in_specs=[pl.BlockSpec((tq, HEAD_DIM), lambda h, i: (i, HQ + h)), kv, kv],
        out_specs=[pl.BlockSpec((tq, HEAD_DIM), lambda h, i: (i, h)), pl.BlockSpec((None, tq, 1), lambda h, i: (h, i, 0))],
        out_shape=[jax.ShapeDtypeStruct((T, MEM_HEADS * HEAD_DIM), BF16), jax.ShapeDtypeStruct((MEM_HEADS, T, 1), F32)],
        compiler_params=_params(("parallel", "arbitrary")),
    )(qkv, mk, mv)


def _mem_bwd(qkv, mk, mv, out, lse, dmix):
    T, ML = qkv.shape[0], mk.shape[0]
    tq = _tile(T, 512)
    HQ = C_MQ // HEAD_DIM
    DO = (FOX_W + SWA_HEADS * HEAD_DIM) // HEAD_DIM

    def body(q_ref, k_ref, v_ref, o_ref, lse_ref, do_ref, dq_ref, dk_ref, dv_ref):
        q, k, v, do = q_ref[...], k_ref[...], v_ref[...], do_ref[...]
        p = jnp.exp(_dot(q, k, "nt") * SCALE - lse_ref[...])
        dp = _dot(do, v, "nt")
        delta = jnp.sum(do.astype(F32) * o_ref[...].astype(F32), axis=-1, keepdims=True)
        dsb = (p * (dp - delta)).astype(BF16)
        dq_ref[...] = _dot(dsb, k, "nn") * SCALE
        dk_part = _dot(dsb, q, "tn") * SCALE
        dv_part = _dot(p.astype(BF16), do, "tn")

        @pl.when(pl.program_id(1) == 0)
        def _():
            dk_ref[...] = dk_part
            dv_ref[...] = dv_part

        @pl.when(pl.program_id(1) > 0)
        def _():
            dk_ref[...] += dk_part
            dv_ref[...] += dv_part

    kv = pl.BlockSpec((ML, HEAD_DIM), lambda h, i: (0, h))
    qb = pl.BlockSpec((tq, HEAD_DIM), lambda h, i: (i, h))
    return pl.pallas_call(
        body, name="mem_bwd", grid=(MEM_HEADS, T // tq),
        in_specs=[pl.BlockSpec((tq, HEAD_DIM), lambda h, i: (i, HQ + h)), kv, kv, qb,
                  pl.BlockSpec((None, tq, 1), lambda h, i: (h, i, 0)), pl.BlockSpec((tq, HEAD_DIM), lambda h, i: (i, DO + h))],
        out_specs=[qb, kv, kv],
        out_shape=[jax.ShapeDtypeStruct((T, MEM_HEADS * HEAD_DIM), F32), jax.ShapeDtypeStruct((ML, MEM_HEADS * HEAD_DIM), F32),
                   jax.ShapeDtypeStruct((ML, MEM_HEADS * HEAD_DIM), F32)],
        compiler_params=_params(("parallel", "arbitrary")),
    )(qkv, mk, mv, out, lse, dmix)


HBM = pl.BlockSpec(memory_space=pltpu.HBM)


def _place():
    x, y, c = lax.axis_index("x"), lax.axis_index("y"), lax.axis_index("c")
    chips = [(1 - x, y), (x, 1 - y), (1 - x, 1 - y)]
    return x, y, c, chips


def _remote(src, dst, send_sem, recv_sem, device):
    return pltpu.make_async_remote_copy(src_ref=src, dst_ref=dst, send_sem=send_sem, recv_sem=recv_sem,
                                        device_id=device, device_id_type=MESH)


def _place_ids():
    x, y, c = lax.axis_index("x"), lax.axis_index("y"), lax.axis_index("c")
    order = [2 * x + y, 2 * (1 - x) + y, 2 * x + (1 - y), 2 * (1 - x) + (1 - y)]
    return jnp.stack([2 * x + y, c] + order).astype(jnp.int32)


def _cast_place(name, w, ids, after):
    R, C = w.shape
    tr = _tile(R, 256, 16)

    def body(ids_ref, w_ref, after_ref, o_ref):
        o_ref[...] = w_ref[...].astype(BF16)

    return pl.pallas_call(
        body, name=name,
        grid_spec=pltpu.PrefetchScalarGridSpec(
            num_scalar_prefetch=1, grid=(R // tr,),
            in_specs=[pl.BlockSpec((tr, C), lambda i, ids: (i, 0)), pl.BlockSpec(memory_space=pl.ANY)],
            out_specs=pl.BlockSpec((None, tr, C), lambda i, ids: (ids[0], i, 0))),
        out_shape=jax.ShapeDtypeStruct((N_CHIPS, R, C), BF16), compiler_params=_params(("parallel",)),
    )(ids, w, after)


SEM = pl.BlockSpec(memory_space=pltpu.SEMAPHORE)
EFFECT = pltpu.SideEffectType.DATAFLOW_SIDE_EFFECTING


def _hbm(a):
    return pltpu.with_memory_space_constraint(a, pltpu.HBM)


def _gather_start(name, placed, after):
    n = len(placed)

    ns = 3 * n

    def body(*refs):
        send, recv = refs[n + 1:n + 1 + ns], refs[n + 1 + ns:n + 1 + 2 * ns]
        buf = refs[n + 1 + 2 * ns:2 * n + 1 + 2 * ns]
        token = refs[2 * n + 1 + 2 * ns]
        x, y, c, chips = _place()
        me = 2 * x + y
        for a in range(n):
            half = buf[a].shape[1] // 2
            mine = buf[a].at[me, pl.ds(c * half, half)]
            for j, (cx, cy) in enumerate(chips):
                _remote(mine, mine, send[3 * a + j], recv[3 * a + j], (cx, cy, c)).start()
        token[...] = jnp.zeros_like(token)

    res = pl.pallas_call(
        body, name=name, in_specs=[HBM] * n + [pl.BlockSpec(memory_space=pl.ANY)],
        out_specs=[SEM] * (2 * ns) + [HBM] * n + [pl.BlockSpec(memory_space=pltpu.VMEM)],
        out_shape=[pltpu.SemaphoreType.DMA(())] * (2 * ns)
        + [pltpu.HBM(s.shape, s.dtype) for s in placed] + [jax.ShapeDtypeStruct((8, 128), F32)],
        input_output_aliases={a: 2 * ns + a for a in range(n)},
        compiler_params=pltpu.CompilerParams(has_side_effects=EFFECT),
    )(*[_hbm(s) for s in placed], after)
    return list(res[:ns]), list(res[ns:2 * ns]), list(res[2 * ns:2 * ns + n]), res[2 * ns + n]


def _gather_wait(name, send, recv, bufs, after):
    n = len(bufs)

    ns = 3 * n

    def body(*refs):
        buf = refs[:n]
        send_ref, recv_ref = refs[n:n + ns], refs[n + ns:n + 2 * ns]
        x, y, c, chips = _place()
        ids = [2 * cx + cy for cx, cy in chips]
        for a in range(n):
            half = buf[a].shape[1] // 2
            for j in range(3):
                landed = buf[a].at[ids[j], pl.ds(c * half, half)]
                cp = _remote(landed, landed, send_ref[3 * a + j], recv_ref[3 * a + j], (x, y, c))
                cp.wait_send()
                cp.wait_recv()

    res = pl.pallas_call(
        body, name=name, in_specs=[HBM] * n + [SEM] * (2 * ns) + [pl.BlockSpec(memory_space=pl.ANY)], out_specs=[HBM] * n,
        out_shape=[pltpu.HBM(s.shape, s.dtype) for s in bufs], input_output_aliases={a: a for a in range(n)},
        compiler_params=pltpu.CompilerParams(has_side_effects=EFFECT),
    )(*bufs, *send, *recv, after)
    return list(res)


def _gather_forward(name, bufs):
    n = len(bufs)

    def body(*refs):
        buf = refs[n:2 * n]
        send, recv = refs[2 * n:]
        x, y, c, chips = _place()
        ids = [2 * cx + cy for cx, cy in chips]
        copies = []
        for a in range(n):
            half = buf[a].shape[1] // 2
            for j in range(3):
                landed = buf[a].at[ids[j], pl.ds(c * half, half)]
                cp = _remote(landed, landed, send.at[a, j], recv.at[a, j], (x, y, 1 - c))
                cp.start()
                copies.append(cp)
        for a in range(n):
            half = buf[a].shape[1] // 2
            for j in range(3):
                landed = buf[a].at[ids[j], pl.ds((1 - c) * half, half)]
                _remote(landed, landed, send.at[a, j], recv.at[a, j], (x, y, c)).wait_recv()
        for cp in copies:
            cp.wait_send()

    return pl.pallas_call(
        body, name=name, in_specs=[HBM] * n, out_specs=[HBM] * n,
        out_shape=[jax.ShapeDtypeStruct(s.shape, s.dtype) for s in bufs], input_output_aliases={a: a for a in range(n)},
        scratch_shapes=[pltpu.SemaphoreType.DMA((n, 3)), pltpu.SemaphoreType.DMA((n, 3))],
    )(*bufs)


def _pair_start(name, grads):
    n = len(grads)
    ns = N_CHIPS * n

    def body(*refs):
        send, recv = refs[2 * n:2 * n + ns], refs[2 * n + ns:2 * n + 2 * ns]
        src = refs[2 * n + 2 * ns:3 * n + 2 * ns]
        land = refs[3 * n + 2 * ns:4 * n + 2 * ns]
        token = refs[4 * n + 2 * ns]
        x, y, c, chips = _place()
        order = [2 * x + y] + [2 * cx + cy for cx, cy in chips]
        for a in range(n):
            half = src[a].shape[1] // 2
            for j in range(N_CHIPS):
                _remote(src[a].at[order[j], pl.ds((1 - c) * half, half)], land[a].at[j],
                        send[N_CHIPS * a + j], recv[N_CHIPS * a + j], (x, y, 1 - c)).start()
        token[...] = jnp.zeros_like(token)

    lands = [jax.ShapeDtypeStruct((N_CHIPS, g.shape[1] // 2, g.shape[2]), g.dtype) for g in grads]
    res = pl.pallas_call(
        body, name=name, in_specs=[HBM] * (2 * n),
        out_specs=[SEM] * (2 * ns) + [HBM] * (2 * n) + [pl.BlockSpec(memory_space=pltpu.VMEM)],
        out_shape=[pltpu.SemaphoreType.DMA(())] * (2 * ns) + [pltpu.HBM(g.shape, g.dtype) for g in grads]
        + [pltpu.HBM(l.shape, l.dtype) for l in lands] + [jax.ShapeDtypeStruct((8, 128), F32)],
        input_output_aliases={a: 2 * ns + a for a in range(2 * n)},
        compiler_params=pltpu.CompilerParams(has_side_effects=EFFECT),
    )(*[_hbm(g) for g in grads], *[_hbm(lax.empty(l.shape, l.dtype)) for l in lands])
    return list(res[:ns]), list(res[ns:2 * ns]), list(res[2 * ns:2 * ns + n]), list(res[2 * ns + n:2 * ns + 2 * n]), res[2 * ns + 2 * n]


def _pair_wait(name, send, recv, grads, lands, after):
    n = len(grads)
    ns = N_CHIPS * n

    def body(*refs):
        src, land = refs[:n], refs[n:2 * n]
        send_ref, recv_ref = refs[2 * n:2 * n + ns], refs[2 * n + ns:2 * n + 2 * ns]
        x, y, c, _ = _place()
        for a in range(n):
            for j in range(N_CHIPS):
                cp = _remote(land[a].at[j], land[a].at[j], send_ref[N_CHIPS * a + j], recv_ref[N_CHIPS * a + j], (x, y, c))
                cp.wait_send()
                cp.wait_recv()

    res = pl.pallas_call(
        body, name=name, in_specs=[HBM] * (2 * n) + [SEM] * (2 * ns) + [pl.BlockSpec(memory_space=pl.ANY)],
        out_specs=[HBM] * (2 * n), out_shape=[pltpu.HBM(g.shape, g.dtype) for g in grads] + [pltpu.HBM(l.shape, l.dtype) for l in lands],
        input_output_aliases={a: a for a in range(2 * n)},
        compiler_params=pltpu.CompilerParams(has_side_effects=EFFECT),
    )(*grads, *lands, *send, *recv, after)
    return list(res[:n]), list(res[n:])


def _chip_start(name, parts):
    n = len(parts)
    ns = 3 * n

    def body(*refs):
        send, recv = refs[2 * n:2 * n + ns], refs[2 * n + ns:2 * n + 2 * ns]
        src = refs[2 * n + 2 * ns:3 * n + 2 * ns]
        land = refs[3 * n + 2 * ns:4 * n + 2 * ns]
        token = refs[4 * n + 2 * ns]
        x, y, c, chips = _place()
        for a in range(n):
            for j, (cx, cy) in enumerate(chips):
                _remote(src[a].at[j], land[a].at[j], send[3 * a + j], recv[3 * a + j], (cx, cy, c)).start()
        token[...] = jnp.zeros_like(token)

    res = pl.pallas_call(
        body, name=name, in_specs=[HBM] * (2 * n),
        out_specs=[SEM] * (2 * ns) + [HBM] * (2 * n) + [pl.BlockSpec(memory_space=pltpu.VMEM)],
        out_shape=[pltpu.SemaphoreType.DMA(())] * (2 * ns) + [pltpu.HBM(p.shape, p.dtype) for p in parts] * 2
        + [jax.ShapeDtypeStruct((8, 128), F32)],
        input_output_aliases={a: 2 * ns + a for a in range(2 * n)},
        compiler_params=pltpu.CompilerParams(has_side_effects=EFFECT),
    )(*[_hbm(p) for p in parts], *[_hbm(lax.empty(p.shape, p.dtype)) for p in parts])
    return list(res[:ns]), list(res[ns:2 * ns]), list(res[2 * ns:2 * ns + n]), list(res[2 * ns + n:2 * ns + 2 * n]), res[2 * ns + 2 * n]


def _chip_wait(name, send, recv, parts, lands, after):
    n = len(parts)
    ns = 3 * n

    def body(*refs):
        src, land = refs[:n], refs[n:2 * n]
        send_ref, recv_ref = refs[2 * n:2 * n + ns], refs[2 * n + ns:2 * n + 2 * ns]
        x, y, c, _ = _place()
        for a in range(n):
            for j in range(3):
                cp = _remote(src[a].at[j], land[a].at[j], send_ref[3 * a + j], recv_ref[3 * a + j], (x, y, c))
                cp.wait_send()
                cp.wait_recv()

    res = pl.pallas_call(
        body, name=name, in_specs=[HBM] * (2 * n) + [SEM] * (2 * ns) + [pl.BlockSpec(memory_space=pl.ANY)],
        out_specs=[HBM] * (2 * n), out_shape=[pltpu.HBM(p.shape, p.dtype) for p in parts] * 2,
        input_output_aliases={a: a for a in range(2 * n)},
        compiler_params=pltpu.CompilerParams(has_side_effects=EFFECT),
    )(*parts, *lands, *send, *recv, after)
    return list(res[n:])


def _pair_share(name, shards):
    n = len(shards)

    def body(*refs):
        buf = refs[n:2 * n]
        send, recv = refs[2 * n:]
        x, y, c, _ = _place()
        copies = []
        for a in range(n):
            half = buf[a].shape[0] // 2
            mine = buf[a].at[pl.ds(c * half, half)]
            cp = _remote(mine, mine, send.at[a], recv.at[a], (x, y, 1 - c))
            cp.start()
            copies.append(cp)
        for a, cp in enumerate(copies):
            half = buf[a].shape[0] // 2
            cp.wait_send()
            theirs = buf[a].at[pl.ds((1 - c) * half, half)]
            _remote(theirs, theirs, send.at[a], recv.at[a], (x, y, c)).wait_recv()

    return pl.pallas_call(
        body, name=name, in_specs=[HBM] * n, out_specs=[HBM] * n,
        out_shape=[jax.ShapeDtypeStruct(s.shape, s.dtype) for s in shards], input_output_aliases={a: a for a in range(n)},
        scratch_shapes=[pltpu.SemaphoreType.DMA((n,)), pltpu.SemaphoreType.DMA((n,))],
    )(*shards)


def _small_start(buf):
    R, W = buf.shape
    ns = N_DEV - 1

    def body(*refs):
        send, recv = refs[2:2 + ns], refs[2 + ns:2 + 2 * ns]
        src, land, token = refs[2 + 2 * ns], refs[3 + 2 * ns], refs[4 + 2 * ns]
        x, y, c, _ = _place()
        me = 4 * x + 2 * y + c
        for k in range(1, N_DEV):
            peer = (x ^ (k >> 2), y ^ ((k >> 1) & 1), c ^ (k & 1))
            _remote(src, land.at[me], send[k - 1], recv[k - 1], peer).start()
        token[...] = jnp.zeros_like(token)

    res = pl.pallas_call(
        body, name="small_start", in_specs=[HBM, HBM],
        out_specs=[SEM] * (2 * ns) + [HBM, HBM, pl.BlockSpec(memory_space=pltpu.VMEM)],
        out_shape=[pltpu.SemaphoreType.DMA(())] * (2 * ns) + [pltpu.HBM((R, W), F32), pltpu.HBM((N_DEV, R, W), F32),
                                                                jax.ShapeDtypeStruct((8, 128), F32)],
        input_output_aliases={0: 2 * ns, 1: 2 * ns + 1},
        compiler_params=pltpu.CompilerParams(has_side_effects=EFFECT),
    )(_hbm(buf), _hbm(jnp.zeros((N_DEV, R, W), F32)))
    return list(res[:ns]), list(res[ns:2 * ns]), res[2 * ns], res[2 * ns + 1], res[2 * ns + 2]


def _small_wait(send, recv, buf, land, after):
    ns = N_DEV - 1

    def body(*refs):
        land_ref = refs[1]
        send_ref, recv_ref = refs[2:2 + ns], refs[2 + ns:2 + 2 * ns]
        x, y, c, _ = _place()
        me = 4 * x + 2 * y + c
        for k in range(1, N_DEV):
            landed = land_ref.at[me ^ k]
            cp = _remote(landed, landed, send_ref[k - 1], recv_ref[k - 1], (x, y, c))
            cp.wait_send()
            cp.wait_recv()

    return pl.pallas_call(
        body, name="small_wait", in_specs=[HBM, HBM] + [SEM] * (2 * ns) + [pl.BlockSpec(memory_space=pl.ANY)],
        out_specs=[HBM, HBM], out_shape=[pltpu.HBM(buf.shape, buf.dtype), pltpu.HBM(land.shape, land.dtype)],
        input_output_aliases={0: 0, 1: 1}, compiler_params=pltpu.CompilerParams(has_side_effects=EFFECT),
    )(buf, land, *send, *recv, after)


def _small_sum(buf, land):
    def body(buf_ref, land_ref, out_ref):
        x, y, c, _ = _place()
        me = 4 * x + 2 * y + c
        total = None
        for d in range(N_DEV):
            term = jnp.where(me == d, buf_ref[...], land_ref[d])
            total = term if total is None else total + term
        out_ref[...] = total

    return pl.pallas_call(body, name="small_sum", out_shape=jax.ShapeDtypeStruct(buf.shape, F32))(buf, land)


def _pair_sum_bf16(name, grad, theirs, ids):
    _, R2, C = theirs.shape
    tr = _tile(R2, 256, 16)
    nrb = R2 // tr

    def body(ids_ref, a_ref, b_ref, o_ref):
        o_ref[...] = (a_ref[...] + b_ref[...]).astype(BF16)

    return pl.pallas_call(
        body, name=name,
        grid_spec=pltpu.PrefetchScalarGridSpec(
            num_scalar_prefetch=1, grid=(3, nrb),
            in_specs=[pl.BlockSpec((None, tr, C), lambda j, i, ids: (ids[3 + j], ids[1] * nrb + i, 0)),
                      pl.BlockSpec((None, tr, C), lambda j, i, ids: (j + 1, i, 0))],
            out_specs=pl.BlockSpec((None, tr, C), lambda j, i, ids: (j, i, 0))),
        out_shape=jax.ShapeDtypeStruct((3, R2, C), BF16), compiler_params=_params(("parallel", "parallel")),
    )(ids, grad, theirs)


def _chip_sum(name, grad, theirs, arrived, ids):
    _, R2, C = theirs.shape
    tr = _tile(R2, 256, 16)
    nrb = R2 // tr

    def body(ids_ref, a_ref, b_ref, r_ref, o_ref):
        tot = a_ref[...] + b_ref[...]
        for j in range(3):
            tot = tot + r_ref[j].astype(F32)
        o_ref[...] = tot

    return pl.pallas_call(
        body, name=name,
        grid_spec=pltpu.PrefetchScalarGridSpec(
            num_scalar_prefetch=1, grid=(nrb,),
            in_specs=[pl.BlockSpec((None, tr, C), lambda i, ids: (ids[0], ids[1] * nrb + i, 0)),
                      pl.BlockSpec((None, tr, C), lambda i, ids: (0, i, 0)),
                      pl.BlockSpec((3, tr, C), lambda i, ids: (0, i, 0))],
            out_specs=pl.BlockSpec((tr, C), lambda i, ids: (ids[1] * nrb + i, 0))),
        out_shape=jax.ShapeDtypeStruct((2 * R2, C), F32), compiler_params=_params(("parallel",)),
    )(ids, grad, theirs, arrived)


def _adamw(name, w, g, m, v, emit_grad=False):
    R, C = w.shape
    tr = _tile(R, 128, 8)
    c1 = 1.0 / (1.0 - ADAM_B1 ** ADAM_STEP)
    c2 = 1.0 / (1.0 - ADAM_B2 ** ADAM_STEP)
    n_out = 4 if emit_grad else 3

    def body(w_ref, g_ref, m_ref, v_ref, d_ref, mo_ref, vo_ref, *rest):
        gv = g_ref[...]
        mn = ADAM_B1 * m_ref[...] + (1.0 - ADAM_B1) * gv
        vn = ADAM_B2 * v_ref[...] + (1.0 - ADAM_B2) * (gv * gv)
        d_ref[...] = -ADAM_LR * ((mn * c1) / (jnp.sqrt(vn * c2) + ADAM_EPS) + ADAM_WD * w_ref[...])
        mo_ref[...] = mn
        vo_ref[...] = vn
        if emit_grad:
            rest[0][...] = gv

    spec = pl.BlockSpec((tr, C), lambda i: (i, 0))
    sds = jax.ShapeDtypeStruct((R, C), F32)
    return pl.pallas_call(body, name=name, grid=(R // tr,), in_specs=[spec] * 4, out_specs=[spec] * n_out, out_shape=[sds] * n_out,
                          compiler_params=_params(("parallel",)))(w, g, m, v)


SMALL = ["ffn1_norm", "mix_norm", "mem_norm", "forget_bias", "fox_q_gain", "fox_k_gain", "swa_q_gain", "swa_k_gain", "swa_sinks",
         "mem_q_gain", "mem_k_gain", "ffn2_norm"]
LARGE = ["ffn1_gate", "ffn1_up", "ffn1_down", "w_in", "w_mem_k", "w_mem_v", "w_out", "ffn2_gate", "ffn2_up", "ffn2_down"]
GATHER_GROUPS = [["ffn1_gate", "ffn1_up"], ["ffn1_down", "w_in", "w_mem_k", "w_mem_v"], ["w_out", "ffn2_gate", "ffn2_up", "ffn2_down"]]
WEIGHTS = ["ffn1_norm", "ffn1_gate", "ffn1_up", "ffn1_down", "mix_norm", "mem_norm", "w_in", "forget_bias", "w_mem_k", "w_mem_v",
           "fox_q_gain", "fox_k_gain", "swa_q_gain", "swa_k_gain", "swa_sinks", "mem_q_gain", "mem_k_gain", "w_out", "ffn2_norm",
           "ffn2_gate", "ffn2_up", "ffn2_down"]


def _pad_proj_cols(w):
    out = jnp.zeros((w.shape[0], PROJ_W), w.dtype)
    for start, width, pstart in REF_GROUPS:
        out = lax.dynamic_update_slice(out, w[:, start:start + width], (0, pstart))
    return out


def _unpad_proj_cols(w):
    return jnp.concatenate([w[:, pstart:pstart + width] for _, width, pstart in REF_GROUPS], axis=1)


def _pack_small(vals):
    flat = jnp.concatenate([vals[k].reshape(-1).astype(F32) for k in SMALL + ["loss"]])
    n = flat.shape[0]
    total = -(-n // 1024) * 1024
    return jnp.pad(flat, (0, total - n)).reshape(total // 128, 128)


def _unpack_small(buf, shapes):
    flat = buf.reshape(-1)
    out, off = {}, 0
    for k in SMALL + ["loss"]:
        size = int(np.prod(shapes[k]))
        out[k] = flat[off:off + size].reshape(shapes[k])
        off += size
    return out


def kernel(x, mem, ffn1_norm, ffn1_gate, ffn1_up, ffn1_down, mix_norm, mem_norm, w_in, forget_bias, w_mem_k, w_mem_v, fox_q_gain, fox_k_gain, swa_q_gain, swa_k_gain, swa_sinks, mem_q_gain, mem_k_gain, w_out, ffn2_norm, ffn2_gate, ffn2_up, ffn2_down, loss_target, m_ffn1_norm, m_ffn1_gate, m_ffn1_up, m_ffn1_down, m_mix_norm, m_mem_norm, m_w_in, m_forget_bias, m_w_mem_k, m_w_mem_v, m_fox_q_gain, m_fox_k_gain, m_swa_q_gain, m_swa_k_gain, m_swa_sinks, m_mem_q_gain, m_mem_k_gain, m_w_out, m_ffn2_norm, m_ffn2_gate, m_ffn2_up, m_ffn2_down, v_ffn1_norm, v_ffn1_gate, v_ffn1_up, v_ffn1_down, v_mix_norm, v_mem_norm, v_w_in, v_forget_bias, v_w_mem_k, v_w_mem_v, v_fox_q_gain, v_fox_k_gain, v_swa_q_gain, v_swa_k_gain, v_swa_sinks, v_mem_q_gain, v_mem_k_gain, v_w_out, v_ffn2_norm, v_ffn2_gate, v_ffn2_up, v_ffn2_down):
    given = dict(locals())
    T, D = x.shape[1], x.shape[2]
    ML = mem.shape[1]
    xin = x.reshape(T, D)
    target = loss_target.reshape(T, D)
    memin = mem.reshape(ML, D)

    ids = _place_ids()
    shard = {k: given[k][0] for k in LARGE}
    started, after = [], ids
    for gi, group in enumerate(GATHER_GROUPS):
        if "w_in" in group:
            shard["w_in"] = _pad_proj_cols(shard["w_in"] + after[0, 0])
        placed = [_cast_place("cast_" + k, shard[k], ids, after) for k in group]
        send, recv, bufs, after = _gather_start("gather_start_%d" % gi, placed, after)
        started.append((send, recv, bufs))

    def arrive(gi, done):
        send, recv, bufs = started[gi]
        bufs = _gather_wait("gather_wait_%d" % gi, send, recv, bufs, done)
        return dict(zip(GATHER_GROUPS[gi], _gather_forward("gather_forward_%d" % gi, bufs)))

    gains = jnp.concatenate([fox_q_gain, fox_k_gain, swa_q_gain, swa_k_gain, mem_q_gain,
                             jnp.pad(forget_bias, ((0, 0), (0, HEAD_DIM - FOX_HEADS))), jnp.zeros((2, HEAD_DIM), F32)], axis=0)
    slopes_np = 2.0 ** (-8.0 * np.arange(1, SWA_HEADS + 1) / SWA_HEADS)
    slopes = jnp.asarray(np.repeat(slopes_np, WINDOW).reshape(SWA_KV_HEADS, GR, 1), F32)
    sinks = jnp.repeat(swa_sinks.reshape(SWA_HEADS), WINDOW).reshape(SWA_KV_HEADS, GR, 1)

    h1 = _rms_fwd("ffn1_norm_fwd", xin, ffn1_norm + after[0, 0])
    full = arrive(0, h1)
    wg1, wu1 = full["ffn1_gate"], full["ffn1_up"]
    g1, u1, a1 = _ffn_gu("ffn1_gate_up", h1, wg1, wu1)
    full = arrive(1, a1)
    wd1 = full["ffn1_down"].reshape(-1, D)
    win = full["w_in"].reshape(D, PROJ_W)
    wmk = full["w_mem_k"].reshape(D, MEM_HEADS * HEAD_DIM)
    wmv = full["w_mem_v"].reshape(D, MEM_HEADS * HEAD_DIM)
    x1 = _ffn_down("ffn1_down", a1, wd1, xin)
    h2 = _rms_fwd("mix_norm_fwd", x1, mix_norm)
    proj = _mm2d("proj_in", h2, win, "nn", F32, tn=1408, tk=2048, n_outer=True)
    qkv, logf, k_t, v_t = _prep_fwd(proj, gains)
    cum = _cumsum_rows("forget_cumsum", [logf], False)
    cum_h = cum[:, :FOX_HEADS].T
    cq_row = cum_h.reshape(FOX_HEADS, 1, T)
    ck_rep = jnp.broadcast_to(cum_h[:, :, None], (FOX_HEADS, T, HEAD_DIM))
    mn = _rms_fwd("mem_norm_fwd", memin, mem_norm)
    mk_raw = _mm2d("mem_k_proj", mn, wmk, "nn", F32)
    mv = _mm2d("mem_v_proj", mn, wmv, "nn", BF16)
    mk = _head_norm_rows(mk_raw, mem_k_gain)
    out_a, out_a_f32, lse_a = _fox_fwd(qkv, v_t, cq_row, ck_rep)
    out_b, lse_b = _swa_fwd(qkv, slopes, sinks)
    out_c, lse_c = _mem_fwd(qkv, mk, mv)
    mixed = jnp.concatenate([out_a, out_b, out_c], axis=1)
    full = arrive(2, mixed)
    wo = full["w_out"].reshape(-1, D)
    wg2, wu2, wd2 = full["ffn2_gate"], full["ffn2_up"], full["ffn2_down"].reshape(-1, D)
    x2 = _mm2d("mix_out", mixed, wo, "nn", F32, tk=2048, n_outer=True, extras=[x1], epilogue=lambda accs, ex: [ex[0] + accs[0]])
    h3 = _rms_fwd("ffn2_norm_fwd", x2, ffn2_norm)
    g2, u2, a2 = _ffn_gu("ffn2_gate_up", h3, wg2, wu2)
    x3 = _ffn_down("ffn2_down", a2, wd2, x2)
    dx3, dyb3, loss_part = _loss_head(x3, target)

    grads, small, res = {}, {"loss": loss_part[0, 0]}, {}

    def pair_off(tag, group):
        send, recv, own, lands, token = _pair_start("grad_pair_start_" + tag, [grads[k] for k in group])
        return (group, send, recv, own, lands), token

    def chip_off(tag, started, done):
        group, send, recv, own, lands = started
        own, theirs = _pair_wait("grad_pair_wait_" + tag, send, recv, own, lands, done)
        grads.update(zip(group, own))
        to_chips = [_pair_sum_bf16("pair_sum_" + k, grads[k], b, ids) for k, b in zip(group, theirs)]
        send, recv, parts, lands, token = _chip_start("grad_chip_start_" + tag, to_chips)
        return (group, theirs, send, recv, parts, lands), token

    def finish(tag, state, done):
        group, theirs, send, recv, parts, lands = state
        arrived = _chip_wait("grad_chip_wait_" + tag, send, recv, parts, lands, done)
        halves = [_chip_sum("chip_sum_" + k, grads[k], b, r, ids) for k, b, r in zip(group, theirs, arrived)]
        reduced = dict(zip(group, _pair_share("grad_pair_share_" + tag, halves)))
        last = None
        for k in group:
            if k == "w_in":
                gk = _unpad_proj_cols(reduced[k])
                d, mo, vo = _adamw("adamw_" + k, given[k][0], gk, given["m_" + k][0], given["v_" + k][0])
            else:
                d, mo, vo, gk = _adamw("adamw_" + k, given[k][0], reduced[k], given["m_" + k][0], given["v_" + k][0], emit_grad=True)
            res[k] = tuple(t[None] for t in (gk, d, mo, vo))
            last = vo
        return last

    dg2, du2 = _ffn_bwd_act("ffn2", dyb3, wd2, g2, u2, N_CHIPS)
    grads["ffn2_down"] = _ffn_bwd_down("ffn2", a2, dyb3, N_CHIPS).reshape(N_CHIPS, -1, D)
    grads["ffn2_gate"], grads["ffn2_up"] = _ffn_bwd_gate_up("ffn2", h3, dg2, du2, N_CHIPS)
    started, token = pair_off("a", ["ffn2_gate", "ffn2_up", "ffn2_down"])
    dh3 = _ffn_bwd_x("ffn2", dg2, du2, wg2, wu2, token)
    state_a, token = chip_off("a", started, dh3)
    dx2, dx2b, small["ffn2_norm"] = _rms_bwd("ffn2_norm_bwd", dh3, x2, ffn2_norm + token[0, 0], dx3, 1.0)
    dmix = _mm2d("mix_out_dx", dx2b, wo, "nt", BF16, tk=2048, n_outer=True)
    grads["w_out"] = _mm2d("mix_out_dw", mixed, dx2b, "tn", F32, tk=1024).reshape(N_CHIPS, -1, D)
    delta_row = _fox_delta(dmix, out_a_f32)[:, :FOX_HEADS].T.reshape(FOX_HEADS, 1, T)
    dfq, dfk, dfv, dck, dcq = _fox_bwd(qkv, k_t, cq_row, ck_rep, delta_row, lse_a, dmix)
    dsq, dsk, dsv, dsink = _swa_bwd(qkv, slopes, sinks, out_b, lse_b, dmix)
    dmq, dmk, dmv = _mem_bwd(qkv, mk, mv, out_c, lse_c, dmix)
    small["swa_sinks"] = dsink[:, :SWA_GROUP, 0].reshape(1, SWA_HEADS)
    dcum = jnp.pad(dcq.reshape(FOX_HEADS, T).T, ((0, 0), (0, HEAD_DIM - FOX_HEADS)))
    dlogf = _cumsum_rows("forget_cumsum_bwd", [dcum], True, columns=dck)
    dproj, dgains = _prep_bwd(proj, gains, dfq, dfk, dfv, dsq, dsk, dsv, dmq, dlogf)
    for row, k in enumerate(["fox_q_gain", "fox_k_gain", "swa_q_gain", "swa_k_gain", "mem_q_gain"]):
        small[k] = dgains[row:row + 1, :]
    small["forget_bias"] = dgains[5:6, :FOX_HEADS]
    grads["w_in"] = _mm2d("proj_in_dw", h2, dproj, "tn", F32, tn=1408, tk=1024).reshape(N_CHIPS, -1, PROJ_W)
    dmk_raw, small["mem_k_gain"] = _head_norm_rows_bwd(mk_raw, mem_k_gain, dmk)
    dmvb = dmv.astype(BF16)
    grads["w_mem_k"] = _mm2d("mem_k_dw", mn, dmk_raw, "tn", F32).reshape(N_CHIPS, -1, MEM_HEADS * HEAD_DIM)
    grads["w_mem_v"] = _mm2d("mem_v_dw", mn, dmvb, "tn", F32).reshape(N_CHIPS, -1, MEM_HEADS * HEAD_DIM)
    dmn = _mm2d("mem_k_dx", dmk_raw, wmk, "nt", F32)
    dmn = _mm2d("mem_v_dx", dmvb, wmv, "nt", F32, extras=[dmn], epilogue=lambda accs, ex: [ex[0] + accs[0]])
    _, _, small["mem_norm"] = _rms_bwd("mem_norm_bwd", dmn, memin, mem_norm, jnp.zeros_like(memin), 1.0)
    started, token = pair_off("b", ["w_out", "w_in", "w_mem_k", "w_mem_v"])
    dh2 = _mm2d("proj_in_dx", dproj, win, "nt", F32, tm=1024, tk=1408, after=token)
    state_b, token = chip_off("b", started, dh2)
    dx1, dyb1, small["mix_norm"] = _rms_bwd("mix_norm_bwd", dh2, x1, mix_norm + token[0, 0], dx2, 0.5)
    grads["ffn1_down"] = _ffn_bwd_down("ffn1", a1, dyb1, N_CHIPS).reshape(N_CHIPS, -1, D)
    started, token = pair_off("c", ["ffn1_down"])
    dg1, du1 = _ffn_bwd_act("ffn1", dyb1, wd1, g1, u1, N_CHIPS, after=token)
    state_c, token = chip_off("c", started, dg1)
    grads["ffn1_gate"], grads["ffn1_up"] = _ffn_bwd_gate_up("ffn1", h1, dg1, du1, N_CHIPS, after=token)
    started, token = pair_off("d", ["ffn1_gate", "ffn1_up"])
    dh1 = _ffn_bwd_x("ffn1", dg1, du1, wg1, wu1, token)
    state_d, token = chip_off("d", started, dh1)
    grad_x, _, small["ffn1_norm"] = _rms_bwd("ffn1_norm_bwd", dh1, xin, ffn1_norm + token[0, 0], dx1, 1.0)

    s_send, s_recv, s_buf, s_land, token = _small_start(_pack_small(small))

    done = finish("a", state_a, token)
    done = finish("b", state_b, done)
    done = finish("c", state_c, done)
    done = finish("d", state_d, done)

    shapes = {k: given[k].shape for k in SMALL}
    shapes["loss"] = ()
    s_buf, s_land = _small_wait(s_send, s_recv, s_buf, s_land, done)
    red_small = _unpack_small(_small_sum(s_buf, s_land), shapes)
    loss = red_small["loss"]
    zero = {"loss": jnp.zeros((), F32)}
    packed = [_pack_small({**zero, **{k: src[k] for k in SMALL}}) for src in (
        {k: given[k] for k in SMALL}, red_small, {k: given["m_" + k] for k in SMALL}, {k: given["v_" + k] for k in SMALL})]
    d_s, m_s, v_s = (_unpack_small(t, shapes) for t in _adamw("adamw_small", *packed))
    for k in SMALL:
        res[k] = (red_small[k], d_s[k], m_s[k], v_s[k])

    outs = [loss, grad_x.reshape(1, T, D)]
    for part in range(4):
        outs += [res[k][part] for k in WEIGHTS]
    return tuple(outs)
```

```python
import functools

import numpy as np
import jax
import jax.numpy as jnp
from jax import lax
from jax.experimental import pallas as pl
from jax.experimental.pallas import tpu as pltpu

F32 = jnp.float32
BF16 = jnp.bfloat16
MESH = pl.DeviceIdType.MESH

HEAD_DIM = 128
FOX_HEADS = 6
SWA_HEADS = 6
SWA_KV_HEADS = 2
SWA_GROUP = SWA_HEADS // SWA_KV_HEADS
MEM_HEADS = 4
WINDOW = 128
EPS = 1e-6
NEG_INF = -1e30
SCALE = HEAD_DIM ** -0.5

C_FQ = 0
C_FK = C_FQ + FOX_HEADS * HEAD_DIM
C_FV = C_FK + FOX_HEADS * HEAD_DIM
C_SQ = C_FV + FOX_HEADS * HEAD_DIM
C_SK = C_SQ + SWA_HEADS * HEAD_DIM
C_SV = C_SK + SWA_KV_HEADS * HEAD_DIM
C_MQ = C_SV + SWA_KV_HEADS * HEAD_DIM
C_FL = C_MQ + MEM_HEADS * HEAD_DIM
PROJ_W = C_FL + HEAD_DIM
FOX_W = FOX_HEADS * HEAD_DIM
REF_GROUPS = [
    (0, FOX_W, C_FQ), (FOX_W, FOX_W, C_FK), (2 * FOX_W, FOX_W, C_FV), (3 * FOX_W, FOX_HEADS, C_FL),
    (3 * FOX_W + FOX_HEADS, SWA_HEADS * HEAD_DIM, C_SQ),
    (3 * FOX_W + FOX_HEADS + SWA_HEADS * HEAD_DIM, SWA_KV_HEADS * HEAD_DIM, C_SK),
    (3 * FOX_W + FOX_HEADS + (SWA_HEADS + SWA_KV_HEADS) * HEAD_DIM, SWA_KV_HEADS * HEAD_DIM, C_SV),
    (3 * FOX_W + FOX_HEADS + (SWA_HEADS + 2 * SWA_KV_HEADS) * HEAD_DIM, MEM_HEADS * HEAD_DIM, C_MQ),
]

ADAM_LR = 0.001
ADAM_B1 = 0.9
ADAM_B2 = 0.999
ADAM_EPS = 1e-08
ADAM_WD = 0.01
ADAM_STEP = 10

V7X_VMEM_LIMIT = 56 * 1024 * 1024
N_CHIPS = 4
N_DEV = 8


def _tile(n, pref, mult=128):
    t = (min(pref, n) // mult) * mult
    while t >= mult:
        if n % t == 0:
            return t
        t -= mult
    return n


def _params(sem):
    return pltpu.CompilerParams(dimension_semantics=sem, vmem_limit_bytes=V7X_VMEM_LIMIT)


_DIMS = {"nn": (((1,), (0,)), ((), ())), "nt": (((1,), (1,)), ((), ())), "tn": (((0,), (0,)), ((), ()))}


def _dot(a, b, mode):
    return lax.dot_general(a, b, _DIMS[mode], preferred_element_type=F32)


def _mm(name, grid, pairs, acc_of, acc_shapes, extras, outs, epilogue, after=None):
    n_p, n_e, n_o, n_a = len(pairs), len(extras), len(outs), len(acc_shapes)
    n_w = 0 if after is None else 1
    nk = grid[2]
    n_in = sum(1 if a is None else 2 for a, *_ in pairs)

    def body(*refs):
        ex = refs[n_in:n_in + n_e]
        out = refs[n_in + n_e + n_w:n_in + n_e + n_w + n_o]
        accs = refs[n_in + n_e + n_w + n_o:]
        parts = [None] * n_a
        at = 0
        for p in range(n_p):
            if pairs[p][0] is None:
                a_ref, b_ref = refs[0], refs[at]
                at += 1
            else:
                a_ref, b_ref = refs[at], refs[at + 1]
                at += 2
            d = _dot(a_ref[...], b_ref[...], pairs[p][4])
            parts[acc_of[p]] = d if parts[acc_of[p]] is None else parts[acc_of[p]] + d

        def finish(vals):
            for o, r in zip(out, epilogue(vals, [e[...] for e in ex])):
                o[...] = r.astype(o.dtype)

        if nk == 1:
            finish(parts)
            return
        k = pl.program_id(2)

        @pl.when(k == 0)
        def _():
            for a, d in zip(accs, parts):
                a[...] = d

        @pl.when((k > 0) & (k < nk - 1))
        def _():
            for a, d in zip(accs, parts):
                a[...] += d

        @pl.when(k == nk - 1)
        def _():
            finish([a[...] + d for a, d in zip(accs, parts)])

    in_specs, args = [], []
    for a, a_spec, b, b_spec, _ in pairs:
        if a is not None:
            in_specs.append(a_spec)
            args.append(a)
        in_specs.append(b_spec)
        args.append(b)
    for e, e_spec in extras:
        in_specs.append(e_spec)
        args.append(e)
    if after is not None:
        in_specs.append(pl.BlockSpec(memory_space=pl.ANY))
        args.append(after)
    res = pl.pallas_call(
        body, name=name, grid=grid, in_specs=in_specs,
        out_specs=[s for _, s in outs], out_shape=[o for o, _ in outs],
        scratch_shapes=[pltpu.VMEM(s, F32) for s in acc_shapes] if nk > 1 else [],
        compiler_params=_params(("parallel", "parallel", "arbitrary")),
    )(*args)
    return res


def _mm2d(name, a, b, mode, out_dtype, tm=512, tn=1024, tk=1024, extras=(), epilogue=None, n_out=1, after=None, n_outer=False,
          resident=False):
    if mode == "nn":
        (M, K), N = a.shape, b.shape[1]
    elif mode == "nt":
        (M, K), N = a.shape, b.shape[0]
    else:
        (K, M), N = a.shape, b.shape[1]
    tm, tn, tk = _tile(M, tm), _tile(N, tn), _tile(K, tk)
    assert not resident or tk == K

    def spec(shape, index, single=False):
        mode_kw = {"pipeline_mode": pl.Buffered(1)} if single else {}
        if n_outer:
            return pl.BlockSpec(shape, lambda j, i, k: index(i, j, k), **mode_kw)
        return pl.BlockSpec(shape, index, **mode_kw)

    single_a, single_b = resident and not n_outer, resident and n_outer
    a_spec = spec((tk, tm), lambda i, j, k: (k, i), single_a) if mode == "tn" else spec((tm, tk), lambda i, j, k: (i, k), single_a)
    b_spec = spec((tn, tk), lambda i, j, k: (j, k), single_b) if mode == "nt" else spec((tk, tn), lambda i, j, k: (k, j), single_b)
    mn = spec((tm, tn), lambda i, j, k: (i, j))
    if epilogue is None:
        epilogue = lambda accs, ex: [accs[0]]
    if not isinstance(out_dtype, (list, tuple)):
        out_dtype = [out_dtype] * n_out
    grid = (N // tn, M // tm, K // tk) if n_outer else (M // tm, N // tn, K // tk)
    res = _mm(name, grid, [(a, a_spec, b, b_spec, mode)], [0], [(tm, tn)],
              [(e, mn) for e in extras], [(jax.ShapeDtypeStruct((M, N), d), mn) for d in out_dtype], epilogue, after=after)
    return res[0] if len(res) == 1 else res


def _sigmoid(x):
    return 1.0 / (1.0 + jnp.exp(-x))


def _sigmoid_fast(x):
    return pl.reciprocal(1.0 + jnp.exp(-x), approx=True)


def _ffn_gu(name, h, wg, wu):
    T, D = h.shape
    nf, _, F4 = wg.shape
    tm, tk = _tile(T, 512), _tile(D, 2048)
    a_spec = pl.BlockSpec((tm, tk), lambda j, i, k: (i, k))
    b_spec = pl.BlockSpec((None, tk, F4), lambda j, i, k: (j, k, 0))
    o_spec = pl.BlockSpec((tm, F4), lambda j, i, k: (i, j))

    def epilogue(accs, ex):
        g, u = accs
        return [g, u, g * _sigmoid_fast(g) * u]

    sds = jax.ShapeDtypeStruct((T, nf * F4), BF16)
    return _mm(name, (nf, T // tm, D // tk), [(h, a_spec, wg, b_spec, "nn"), (None, None, wu, b_spec, "nn")], [0, 1],
               [(tm, F4), (tm, F4)], [], [(sds, o_spec)] * 3, epilogue)


def _ffn_down(name, a, wd, xres):
    return _mm2d(name, a, wd, "nn", F32, tm=512, tn=1024, tk=wd.shape[0], n_outer=True, resident=True, extras=[xres],
                 epilogue=lambda accs, ex: [ex[0] + 0.5 * accs[0]])


def _ffn_bwd_down(tag, a, dyb, nf, after=None):
    return _mm2d(tag + "_dwd", a, dyb, "tn", F32, tm=a.shape[1] // nf, tn=1024, tk=a.shape[0], resident=True, after=after)


def _ffn_bwd_act(tag, dyb, wd, g, u, nf, after=None):
    def act_bwd(accs, ex):
        gf, uf = ex[0].astype(F32), ex[1].astype(F32)
        s = _sigmoid_fast(gf)
        return [accs[0] * uf * s * (1.0 + gf * (1.0 - s)), accs[0] * gf * s]

    return _mm2d(tag + "_da", dyb, wd, "nt", BF16, tn=wd.shape[0] // nf, tk=2048, extras=[g, u], epilogue=act_bwd, n_out=2,
                 n_outer=True, after=after)


def _ffn_bwd_gate_up(tag, h, dg, du, nf, after=None):
    T, D = h.shape
    F4 = dg.shape[1] // nf
    tm = _tile(D, 512)
    h_spec = pl.BlockSpec((T, tm), lambda j, i, k: (0, i))
    d_spec = pl.BlockSpec((T, F4), lambda j, i, k: (0, j), pipeline_mode=pl.Buffered(1))
    w_spec = pl.BlockSpec((None, tm, F4), lambda j, i, k: (j, i, 0))
    sds = jax.ShapeDtypeStruct((nf, D, F4), F32)
    return _mm(tag + "_dwgu", (nf, D // tm, 1), [(h, h_spec, dg, d_spec, "tn"), (None, None, du, d_spec, "tn")],
               [0, 1], [(tm, F4), (tm, F4)], [], [(sds, w_spec)] * 2, lambda accs, ex: accs, after=after)


def _ffn_bwd_x(tag, dg, du, wg, wu, after):
    T = dg.shape[0]
    nf, D, F4 = wg.shape
    tm, tn = _tile(T, 1024), _tile(D, 1024)
    a_spec = pl.BlockSpec((tm, F4), lambda i, j, k: (i, k))
    b_spec = pl.BlockSpec((None, tn, F4), lambda i, j, k: (k, j, 0))
    o_spec = pl.BlockSpec((tm, tn), lambda i, j, k: (i, j))
    (dh,) = _mm(tag + "_dh", (T // tm, D // tn, nf), [(dg, a_spec, wg, b_spec, "nt"), (du, a_spec, wu, b_spec, "nt")],
                [0, 0], [(tm, tn)], [], [(jax.ShapeDtypeStruct((T, D), F32), o_spec)], lambda accs, ex: accs, after=after)
    return dh


def _rms_fwd(name, x, gain):
    R, D = x.shape
    tr = _tile(R, 256, 8)

    def body(x_ref, g_ref, o_ref):
        xv = x_ref[...]
        r = lax.rsqrt(jnp.mean(xv * xv, axis=-1, keepdims=True) + EPS)
        o_ref[...] = (xv * r * g_ref[...]).astype(BF16)

    return pl.pallas_call(
        body, name=name, grid=(R // tr,),
        in_specs=[pl.BlockSpec((tr, D), lambda i: (i, 0)), pl.BlockSpec((1, D), lambda i: (0, 0))],
        out_specs=pl.BlockSpec((tr, D), lambda i: (i, 0)), out_shape=jax.ShapeDtypeStruct((R, D), BF16),
        compiler_params=_params(("parallel",)),
    )(x, gain)


def _rms_bwd(name, dh, x, gain, dres, bscale):
    R, D = x.shape
    tr = _tile(R, 256, 8)

    def body(dh_ref, x_ref, g_ref, dres_ref, dx_ref, dxb_ref, dg_ref):
        xv, dy = x_ref[...], dh_ref[...]
        r = lax.rsqrt(jnp.mean(xv * xv, axis=-1, keepdims=True) + EPS)
        xn = xv * r
        uu = dy * g_ref[...]
        dx = dres_ref[...] + r * (uu - xn * jnp.mean(xn * uu, axis=-1, keepdims=True))
        dx_ref[...] = dx
        dxb_ref[...] = (bscale * dx).astype(BF16)
        part = jnp.sum(dy * xn, axis=0, keepdims=True)

        @pl.when(pl.program_id(0) == 0)
        def _():
            dg_ref[...] = part

        @pl.when(pl.program_id(0) > 0)
        def _():
            dg_ref[...] += part

    row = pl.BlockSpec((tr, D), lambda i: (i, 0))
    vec = pl.BlockSpec((1, D), lambda i: (0, 0))
    return pl.pallas_call(
        body, name=name, grid=(R // tr,), in_specs=[row, row, vec, row], out_specs=[row, row, vec],
        out_shape=[jax.ShapeDtypeStruct((R, D), F32), jax.ShapeDtypeStruct((R, D), BF16), jax.ShapeDtypeStruct((1, D), F32)],
        compiler_params=_params(("arbitrary",)),
    )(dh, x, gain, dres)


def _loss_head(y, target):
    R, D = y.shape
    tr = _tile(R, 256, 8)

    def body(y_ref, t_ref, d_ref, db_ref, l_ref):
        e = y_ref[...] - t_ref[...]
        d = e * (1.0 / D)
        d_ref[...] = d
        db_ref[...] = (0.5 * d).astype(BF16)
        part = jnp.zeros((8, 128), F32) + (0.5 / D) * jnp.sum(e * e)

        @pl.when(pl.program_id(0) == 0)
        def _():
            l_ref[...] = part

        @pl.when(pl.program_id(0) > 0)
        def _():
            l_ref[...] += part

    row = pl.BlockSpec((tr, D), lambda i: (i, 0))
    acc = pl.BlockSpec((8, 128), lambda i: (0, 0))
    return pl.pallas_call(
        body, name="loss_head", grid=(R // tr,), in_specs=[row, row], out_specs=[row, row, acc],
        out_shape=[jax.ShapeDtypeStruct((R, D), F32), jax.ShapeDtypeStruct((R, D), BF16), jax.ShapeDtypeStruct((8, 128), F32)],
        compiler_params=_params(("arbitrary",)),
    )(y, target)


def _head_norm(xs, g):
    r = lax.rsqrt(jnp.mean(xs * xs, axis=-1, keepdims=True) + EPS)
    return xs * r * g


def _head_norm_bwd(xs, g, dy):
    r = lax.rsqrt(jnp.mean(xs * xs, axis=-1, keepdims=True) + EPS)
    xn = xs * r
    uu = dy * g
    return r * (uu - xn * jnp.mean(xn * uu, axis=-1, keepdims=True)), jnp.sum(dy * xn, axis=0, keepdims=True)


NORMED = [(C_FQ, FOX_HEADS, 0), (C_FK, FOX_HEADS, 1), (C_SQ, SWA_HEADS, 2), (C_SK, SWA_KV_HEADS, 3), (C_MQ, MEM_HEADS, 4)]
PLAIN = [(C_FV, FOX_HEADS), (C_SV, SWA_KV_HEADS)]


def _prep_fwd(proj, gains):
    T = proj.shape[0]
    tr = _tile(T, 256, 128)

    def body(p_ref, g_ref, o_ref, lf_ref, kt_ref, vt_ref):
        for start, heads, row in NORMED:
            gn = g_ref[row:row + 1, :]
            for hh in range(heads):
                sl = slice(start + hh * HEAD_DIM, start + (hh + 1) * HEAD_DIM)
                y = _head_norm(p_ref[:, sl], gn)
                o_ref[:, sl] = y.astype(BF16)
                if start == C_FK:
                    kt_ref[hh * HEAD_DIM:(hh + 1) * HEAD_DIM, :] = y.T.astype(BF16)
        for start, heads in PLAIN:
            sl = slice(start, start + heads * HEAD_DIM)
            o_ref[:, sl] = p_ref[:, sl].astype(BF16)
        for hh in range(FOX_HEADS):
            sl = slice(C_FV + hh * HEAD_DIM, C_FV + (hh + 1) * HEAD_DIM)
            vt_ref[hh * HEAD_DIM:(hh + 1) * HEAD_DIM, :] = p_ref[:, sl].T.astype(BF16)
        zb = p_ref[:, C_FL:C_FL + HEAD_DIM] + g_ref[5:6, :]
        o_ref[:, C_FL:C_FL + HEAD_DIM] = jnp.zeros((tr, HEAD_DIM), BF16)
        lf_ref[...] = jnp.minimum(zb, 0.0) - jnp.log(1.0 + jnp.exp(-jnp.abs(zb)))

    return pl.pallas_call(
        body, name="prep_fwd", grid=(T // tr,),
        in_specs=[pl.BlockSpec((tr, PROJ_W), lambda i: (i, 0)), pl.BlockSpec((8, 128), lambda i: (0, 0))],
        out_specs=[pl.BlockSpec((tr, PROJ_W), lambda i: (i, 0)), pl.BlockSpec((tr, HEAD_DIM), lambda i: (i, 0)),
                   pl.BlockSpec((FOX_W, tr), lambda i: (0, i)), pl.BlockSpec((FOX_W, tr), lambda i: (0, i))],
        out_shape=[jax.ShapeDtypeStruct((T, PROJ_W), BF16), jax.ShapeDtypeStruct((T, HEAD_DIM), F32),
                   jax.ShapeDtypeStruct((FOX_W, T), BF16), jax.ShapeDtypeStruct((FOX_W, T), BF16)],
        compiler_params=_params(("parallel",)),
    )(proj, gains)


def _prep_bwd(proj, gains, dfq, dfk, dfv, dsq, dsk, dsv, dmq, dlogf):
    T = proj.shape[0]
    tr = _tile(T, 256, 8)
    d_normed = {C_FQ: 0, C_FK: 1, C_SQ: 3, C_SK: 4, C_MQ: 6}
    d_plain = {C_FV: 2, C_SV: 5}

    def body(p_ref, g_ref, *rest):
        d_refs, dlf_ref, o_ref, dg_ref = rest[:7], rest[7], rest[8], rest[9]
        rows = []
        for start, heads, row in NORMED:
            gn = g_ref[row:row + 1, :]
            d_ref = d_refs[d_normed[start]]
            tot = jnp.zeros((1, HEAD_DIM), F32)
            for hh in range(heads):
                sl = slice(start + hh * HEAD_DIM, start + (hh + 1) * HEAD_DIM)
                dx, dgn = _head_norm_bwd(p_ref[:, sl], gn, d_ref[:, hh * HEAD_DIM:(hh + 1) * HEAD_DIM])
                o_ref[:, sl] = dx.astype(BF16)
                tot = tot + dgn
            rows.append(tot)
        for start, heads in PLAIN:
            o_ref[:, start:start + heads * HEAD_DIM] = d_refs[d_plain[start]][...].astype(BF16)
        zb = p_ref[:, C_FL:C_FL + HEAD_DIM] + g_ref[5:6, :]
        lane = lax.broadcasted_iota(jnp.int32, (tr, HEAD_DIM), 1)
        dz = jnp.where(lane < FOX_HEADS, dlf_ref[...] * (1.0 - _sigmoid(zb)), 0.0)
        o_ref[:, C_FL:C_FL + HEAD_DIM] = dz.astype(BF16)
        rows.append(jnp.sum(dz, axis=0, keepdims=True))
        part = jnp.concatenate(rows + [jnp.zeros((2, HEAD_DIM), F32)], axis=0)

        @pl.when(pl.program_id(0) == 0)
        def _():
            dg_ref[...] = part

        @pl.when(pl.program_id(0) > 0)
        def _():
            dg_ref[...] += part

    def rows_of(w):
        return pl.BlockSpec((tr, w), lambda i: (i, 0))

    small = pl.BlockSpec((8, 128), lambda i: (0, 0))
    ds = [dfq, dfk, dfv, dsq, dsk, dsv, dmq]
    return pl.pallas_call(
        body, name="prep_bwd", grid=(T // tr,),
        in_specs=[rows_of(PROJ_W), small] + [rows_of(d.shape[1]) for d in ds] + [rows_of(HEAD_DIM)],
        out_specs=[rows_of(PROJ_W), small],
        out_shape=[jax.ShapeDtypeStruct((T, PROJ_W), BF16), jax.ShapeDtypeStruct((8, 128), F32)],
        compiler_params=_params(("arbitrary",)),
    )(proj, gains, *ds, dlogf)


def _head_norm_rows(x, gain):
    R, W = x.shape

    def body(x_ref, g_ref, o_ref):
        for hh in range(W // HEAD_DIM):
            sl = slice(hh * HEAD_DIM, (hh + 1) * HEAD_DIM)
            o_ref[:, sl] = _head_norm(x_ref[:, sl], g_ref[...]).astype(BF16)

    return pl.pallas_call(body, name="mem_k_norm", out_shape=jax.ShapeDtypeStruct((R, W), BF16))(x, gain)


def _head_norm_rows_bwd(x, gain, dy):
    R, W = x.shape

    def body(x_ref, g_ref, dy_ref, dx_ref, dg_ref):
        tot = jnp.zeros((1, HEAD_DIM), F32)
        for hh in range(W // HEAD_DIM):
            sl = slice(hh * HEAD_DIM, (hh + 1) * HEAD_DIM)
            dx, dgn = _head_norm_bwd(x_ref[:, sl], g_ref[...], dy_ref[:, sl])
            dx_ref[:, sl] = dx.astype(BF16)
            tot = tot + dgn
        dg_ref[...] = tot

    return pl.pallas_call(
        body, name="mem_k_norm_bwd",
        out_shape=[jax.ShapeDtypeStruct((R, W), BF16), jax.ShapeDtypeStruct((1, HEAD_DIM), F32)])(x, gain, dy)


def _cumsum_rows(name, xs, reverse, columns=None):
    T, W = xs[0].shape
    tb = _tile(T, 512, 8)
    nb = T // tb
    n_in = len(xs) + (0 if columns is None else 1)

    def body(*refs):
        o_ref, carry = refs[n_in], refs[n_in + 1]

        @pl.when(pl.program_id(0) == 0)
        def _():
            carry[...] = jnp.zeros_like(carry)

        xv = refs[0][...]
        for x_ref in refs[1:len(xs)]:
            xv = xv + x_ref[...]
        if columns is not None:
            lane = lax.broadcasted_iota(jnp.int32, (tb, W), 1)
            for hh in range(columns.shape[0]):
                xv = xv + jnp.where(lane == hh, refs[len(xs)][hh], 0.0)
        r = lax.broadcasted_iota(jnp.int32, (tb, tb), 0)
        cc = lax.broadcasted_iota(jnp.int32, (tb, tb), 1)
        tri = jnp.where((cc >= r) if reverse else (cc <= r), 1.0, 0.0).astype(F32)
        o_ref[...] = jnp.dot(tri, xv, precision=lax.Precision.HIGHEST, preferred_element_type=F32) + carry[...]
        carry[...] += jnp.sum(xv, axis=0, keepdims=True)

    idx = (lambda i: (nb - 1 - i, 0)) if reverse else (lambda i: (i, 0))
    in_specs = [pl.BlockSpec((tb, W), idx)] * len(xs)
    if columns is not None:
        in_specs.append(pl.BlockSpec((columns.shape[0], tb, 1), lambda i: (0, idx(i)[0], 0)))
    return pl.pallas_call(
        body, name=name, grid=(nb,), in_specs=in_specs, out_specs=pl.BlockSpec((tb, W), idx),
        out_shape=jax.ShapeDtypeStruct((T, W), F32), scratch_shapes=[pltpu.VMEM((1, W), F32)],
        compiler_params=_params(("arbitrary",)),
    )(*xs, *([] if columns is None else [columns]))


def _triangle(nq, by_column):
    if by_column:
        blocks = [(i, j) for j in range(nq) for i in range(j, nq)]
    else:
        blocks = [(i, j) for i in range(nq) for j in range(i + 1)]
    return jnp.asarray(np.array(blocks, np.int32).T)


def _fox_scores_t(k, q, cq_row, ck_rep, on_diagonal):
    n = q.shape[0]
    s = _dot(k, q, "nt") * SCALE + (cq_row - jnp.tile(ck_rep, (1, n // HEAD_DIM)))
    if on_diagonal:
        s = jnp.where(lax.broadcasted_iota(jnp.int32, (n, n), 0) <= lax.broadcasted_iota(jnp.int32, (n, n), 1), s, NEG_INF)
    return s


def _fox_fwd(qkv, v_t, cq_row, ck_rep):
    T = qkv.shape[0]
    tq = _tile(T, 512)
    nq = T // tq
    steps = nq * (nq + 1) // 2
    HQ, HK = C_FQ // HEAD_DIM, C_FK // HEAD_DIM

    def body(tab, q_ref, k_ref, vt_ref, cq_ref, ck_ref, o_ref, of_ref, lse_ref, m_sc, l_sc, acc_sc):
        i, j = tab[0, pl.program_id(1)], tab[1, pl.program_id(1)]

        @pl.when(j == 0)
        def _():
            m_sc[...] = jnp.full_like(m_sc, NEG_INF)
            l_sc[...] = jnp.zeros_like(l_sc)
            acc_sc[...] = jnp.zeros_like(acc_sc)

        def step(on_diagonal):
            s = _fox_scores_t(k_ref[...], q_ref[...], cq_ref[...], ck_ref[...], on_diagonal)
            m_new = jnp.maximum(m_sc[...], jnp.max(s, axis=0, keepdims=True))
            alpha = jnp.exp(m_sc[...] - m_new)
            p = jnp.exp(s - m_new)
            l_sc[...] = alpha * l_sc[...] + jnp.sum(p, axis=0, keepdims=True)
            acc_sc[...] = alpha * acc_sc[...] + _dot(vt_ref[...], p.astype(BF16), "nn")
            m_sc[...] = m_new

        @pl.when(j < i)
        def _():
            step(False)

        @pl.when(j == i)
        def _():
            step(True)
            o = (acc_sc[...] / l_sc[...]).T
            o_ref[...] = o.astype(BF16)
            of_ref[...] = o
            lse_ref[...] = m_sc[...] + jnp.log(l_sc[...])

    qrow = pl.BlockSpec((None, 1, tq), lambda h, s, tab: (h, 0, tab[0, s]))
    return pl.pallas_call(
        body, name="fox_fwd",
        grid_spec=pltpu.PrefetchScalarGridSpec(
            num_scalar_prefetch=1, grid=(FOX_HEADS, steps),
            in_specs=[pl.BlockSpec((tq, HEAD_DIM), lambda h, s, tab: (tab[0, s], HQ + h)),
                      pl.BlockSpec((tq, HEAD_DIM), lambda h, s, tab: (tab[1, s], HK + h)),
                      pl.BlockSpec((HEAD_DIM, tq), lambda h, s, tab: (h, tab[1, s])), qrow,
                      pl.BlockSpec((None, tq, HEAD_DIM), lambda h, s, tab: (h, tab[1, s], 0))],
            out_specs=[pl.BlockSpec((tq, HEAD_DIM), lambda h, s, tab: (tab[0, s], h)),
                       pl.BlockSpec((tq, HEAD_DIM), lambda h, s, tab: (tab[0, s], h)), qrow],
            scratch_shapes=[pltpu.VMEM((1, tq), F32), pltpu.VMEM((1, tq), F32), pltpu.VMEM((HEAD_DIM, tq), F32)]),
        out_shape=[jax.ShapeDtypeStruct((T, FOX_W), BF16), jax.ShapeDtypeStruct((T, FOX_W), F32),
                   jax.ShapeDtypeStruct((FOX_HEADS, 1, T), F32)],
        compiler_params=_params(("parallel", "arbitrary")),
    )(_triangle(nq, False), qkv, qkv, v_t, cq_row, ck_rep)


def _fox_delta(dmix, out_f32):
    T = out_f32.shape[0]
    tr = _tile(T, 512, 8)

    def body(do_ref, o_ref, d_ref):
        lane = lax.broadcasted_iota(jnp.int32, (tr, HEAD_DIM), 1)
        acc = jnp.zeros((tr, HEAD_DIM), F32)
        for hh in range(FOX_HEADS):
            sl = slice(hh * HEAD_DIM, (hh + 1) * HEAD_DIM)
            d = jnp.sum(do_ref[:, sl].astype(F32) * o_ref[:, sl], axis=-1, keepdims=True)
            acc = jnp.where(lane == hh, d, acc)
        d_ref[...] = acc

    blk = pl.BlockSpec((tr, FOX_W), lambda i: (i, 0))
    return pl.pallas_call(
        body, name="fox_delta", grid=(T // tr,), in_specs=[blk, blk], out_specs=pl.BlockSpec((tr, HEAD_DIM), lambda i: (i, 0)),
        out_shape=jax.ShapeDtypeStruct((T, HEAD_DIM), F32), compiler_params=_params(("parallel",)),
    )(dmix, out_f32)


def _fox_bwd(qkv, k_t, cq_row, ck_rep, delta_row, lse, dmix):
    T = qkv.shape[0]
    tq = _tile(T, 512)
    nq = T // tq
    steps = nq * (nq + 1) // 2
    HQ, HK, HV = C_FQ // HEAD_DIM, C_FK // HEAD_DIM, C_FV // HEAD_DIM

    def body(tab, q_ref, k_ref, kt_ref, v_ref, cq_ref, ck_ref, delta_ref, lse_ref, do_ref,
             dq_ref, dk_ref, dv_ref, dck_ref, dcq_ref, dk_sc, dv_sc, dc_sc, dqt_sc):
        qi, kj = tab[0, pl.program_id(1)], tab[1, pl.program_id(1)]

        @pl.when(qi == kj)
        def _():
            dk_sc[...] = jnp.zeros_like(dk_sc)
            dv_sc[...] = jnp.zeros_like(dv_sc)
            dc_sc[...] = jnp.zeros_like(dc_sc)

        def step(on_diagonal):
            q, k, v, do = q_ref[...], k_ref[...], v_ref[...], do_ref[...]
            p = jnp.exp(_fox_scores_t(k, q, cq_ref[...], ck_ref[...], on_diagonal) - lse_ref[...])
            dp = _dot(v, do, "nt")
            ds = p * (dp - delta_ref[...])
            dsb = ds.astype(BF16)
            dv_sc[...] += _dot(p.astype(BF16), do, "nn")
            dk_sc[...] += _dot(dsb, q, "nn")
            dc_sc[...] += jnp.sum(ds, axis=1, keepdims=True)
            dq_part = _dot(kt_ref[...], dsb, "nn") * SCALE
            dcq_part = jnp.sum(ds, axis=0, keepdims=True)

            @pl.when(kj == 0)
            def _():
                dqt_sc[qi] = dq_part
                dcq_ref[qi] = dcq_part

            @pl.when(kj > 0)
            def _():
                dqt_sc[qi] += dq_part
                dcq_ref[qi] += dcq_part

            if on_diagonal:
                dq_ref[...] = dqt_sc[qi].T

        @pl.when(qi > kj)
        def _():
            step(False)

        @pl.when(qi == kj)
        def _():
            step(True)

        @pl.when(qi == nq - 1)
        def _():
            dk_ref[...] = dk_sc[...] * SCALE
            dv_ref[...] = dv_sc[...]
            dck_ref[...] = -dc_sc[...]

    def rows(base):
        return pl.BlockSpec((tq, HEAD_DIM), lambda h, s, tab: (tab[0, s], base + h))

    def cols(base):
        return pl.BlockSpec((tq, HEAD_DIM), lambda h, s, tab: (tab[1, s], base + h))

    qrow = pl.BlockSpec((None, 1, tq), lambda h, s, tab: (h, 0, tab[0, s]))
    sds = jax.ShapeDtypeStruct((T, FOX_W), F32)
    return pl.pallas_call(
        body, name="fox_bwd",
        grid_spec=pltpu.PrefetchScalarGridSpec(
            num_scalar_prefetch=1, grid=(FOX_HEADS, steps),
            in_specs=[rows(HQ), cols(HK), pl.BlockSpec((HEAD_DIM, tq), lambda h, s, tab: (h, tab[1, s])), cols(HV), qrow,
                      pl.BlockSpec((None, tq, HEAD_DIM), lambda h, s, tab: (h, tab[1, s], 0)), qrow, qrow, rows(0)],
            out_specs=[cols(0), cols(0), cols(0), pl.BlockSpec((None, tq, 1), lambda h, s, tab: (h, tab[1, s], 0)),
                       pl.BlockSpec((None, nq, 1, tq), lambda h, s, tab: (h, 0, 0, 0))],
            scratch_shapes=[pltpu.VMEM((tq, HEAD_DIM), F32), pltpu.VMEM((tq, HEAD_DIM), F32), pltpu.VMEM((tq, 1), F32),
                            pltpu.VMEM((nq, HEAD_DIM, tq), F32)]),
        out_shape=[sds, sds, sds, jax.ShapeDtypeStruct((FOX_HEADS, T, 1), F32), jax.ShapeDtypeStruct((FOX_HEADS, nq, 1, tq), F32)],
        compiler_params=_params(("parallel", "arbitrary")),
    )(_triangle(nq, True), qkv, qkv, k_t, qkv, cq_row, ck_rep, delta_row, lse, dmix)


GW = SWA_GROUP * HEAD_DIM
GR = SWA_GROUP * WINDOW


def _swa_scores(q_ref, kp_ref, kc_ref, slope_ref, n):
    q = q_ref[...]
    qs = jnp.concatenate([q[:, t * HEAD_DIM:(t + 1) * HEAD_DIM] for t in range(SWA_GROUP)], axis=0)
    kb = jnp.concatenate([kp_ref[...], kc_ref[...]], axis=0)
    r = lax.broadcasted_iota(jnp.int32, (GR, 2 * WINDOW), 0) & (WINDOW - 1)
    jj = lax.broadcasted_iota(jnp.int32, (GR, 2 * WINDOW), 1)
    dist = WINDOW + r - jj
    valid = (dist >= 0) & (dist < WINDOW) & ((n > 0) | (jj >= WINDOW))
    s = _dot(qs, kb, "nt") * SCALE - slope_ref[...] * dist.astype(F32)
    return qs, kb, jnp.where(valid, s, NEG_INF), valid


def _swa_specs():
    HQ, HK, HV = C_SQ // GW, C_SK // HEAD_DIM, C_SV // HEAD_DIM
    q_spec = pl.BlockSpec((WINDOW, GW), lambda g, n: (n, HQ + g))

    def prev(base):
        return pl.BlockSpec((WINDOW, HEAD_DIM), lambda g, n: (jnp.maximum(n - 1, 0), base + g))

    def cur(base):
        return pl.BlockSpec((WINDOW, HEAD_DIM), lambda g, n: (n, base + g))

    col = pl.BlockSpec((None, GR, 1), lambda g, n: (g, 0, 0))
    return q_spec, prev(HK), cur(HK), prev(HV), cur(HV), col


def _swa_fwd(qkv, slopes, sinks):
    T = qkv.shape[0]
    nb = T // WINDOW
    assert C_SQ % GW == 0

    def body(q_ref, kp_ref, kc_ref, vp_ref, vc_ref, slope_ref, sink_ref, o_ref, lse_ref):
        n = pl.program_id(1)
        _, _, s, _ = _swa_scores(q_ref, kp_ref, kc_ref, slope_ref, n)
        m = jnp.maximum(jnp.max(s, axis=-1, keepdims=True), sink_ref[...])
        p = jnp.exp(s - m)
        l = jnp.sum(p, axis=-1, keepdims=True) + jnp.exp(sink_ref[...] - m)
        vb = jnp.concatenate([vp_ref[...], vc_ref[...]], axis=0)
        o = _dot(p.astype(BF16), vb, "nn") / l
        for t in range(SWA_GROUP):
            o_ref[:, t * HEAD_DIM:(t + 1) * HEAD_DIM] = o[t * WINDOW:(t + 1) * WINDOW, :].astype(BF16)
        lse_ref[...] = m + jnp.log(l)

    q_spec, kp, kc, vp, vc, col = _swa_specs()
    return pl.pallas_call(
        body, name="swa_fwd", grid=(SWA_KV_HEADS, nb), in_specs=[q_spec, kp, kc, vp, vc, col, col],
        out_specs=[pl.BlockSpec((WINDOW, GW), lambda g, n: (n, g)), pl.BlockSpec((None, None, GR, 1), lambda g, n: (g, n, 0, 0))],
        out_shape=[jax.ShapeDtypeStruct((T, SWA_HEADS * HEAD_DIM), BF16), jax.ShapeDtypeStruct((SWA_KV_HEADS, nb, GR, 1), F32)],
        compiler_params=_params(("parallel", "arbitrary")),
    )(qkv, qkv, qkv, qkv, qkv, slopes, sinks)


def _swa_bwd(qkv, slopes, sinks, out, lse, dmix):
    T = qkv.shape[0]
    nb = T // WINDOW
    DO = FOX_W // GW
    assert FOX_W % GW == 0

    def body(q_ref, kp_ref, kc_ref, vp_ref, vc_ref, slope_ref, sink_ref, o_ref, lse_ref, do_ref,
             dq_ref, dk_ref, dv_ref, dsink_ref, sink_sc):
        n = pl.program_id(1)

        @pl.when(n == 0)
        def _():
            dk_ref[...] = jnp.zeros_like(dk_ref)
            dv_ref[...] = jnp.zeros_like(dv_ref)
            sink_sc[...] = jnp.zeros_like(sink_sc)

        qs, kb, s, valid = _swa_scores(q_ref, kp_ref, kc_ref, slope_ref, n)
        lse = lse_ref[...]
        p = jnp.where(valid, jnp.exp(s - lse), 0.0)
        vb = jnp.concatenate([vp_ref[...], vc_ref[...]], axis=0)
        do = jnp.concatenate([do_ref[:, t * HEAD_DIM:(t + 1) * HEAD_DIM] for t in range(SWA_GROUP)], axis=0)
        oo = jnp.concatenate([o_ref[:, t * HEAD_DIM:(t + 1) * HEAD_DIM] for t in range(SWA_GROUP)], axis=0)
        dp = _dot(do, vb, "nt")
        delta = jnp.sum(do.astype(F32) * oo.astype(F32), axis=-1, keepdims=True)
        ds = p * (dp - delta)
        dsb = ds.astype(BF16)
        dq = _dot(dsb, kb, "nn") * SCALE
        for t in range(SWA_GROUP):
            dq_ref[:, t * HEAD_DIM:(t + 1) * HEAD_DIM] = dq[t * WINDOW:(t + 1) * WINDOW, :]
        dkb = _dot(dsb, qs, "tn") * SCALE
        dvb = _dot(p.astype(BF16), do, "tn")
        r_prev = pl.ds(pl.multiple_of(jnp.maximum(n - 1, 0) * WINDOW, WINDOW), WINDOW)
        r_cur = pl.ds(pl.multiple_of(n * WINDOW, WINDOW), WINDOW)
        dk_ref[r_prev, :] += dkb[:WINDOW, :]
        dk_ref[r_cur, :] += dkb[WINDOW:, :]
        dv_ref[r_prev, :] += dvb[:WINDOW, :]
        dv_ref[r_cur, :] += dvb[WINDOW:, :]
        sink_sc[...] -= jnp.exp(sink_ref[...] - lse) * delta

        @pl.when(n == nb - 1)
        def _():
            tot = [jnp.zeros((1, 128), F32) + jnp.sum(sink_sc[t * WINDOW:(t + 1) * WINDOW, :]) for t in range(SWA_GROUP)]
            dsink_ref[...] = jnp.concatenate(tot + [jnp.zeros((8 - SWA_GROUP, 128), F32)], axis=0)

    q_spec, kp, kc, vp, vc, col = _swa_specs()
    kv_acc = pl.BlockSpec((T, HEAD_DIM), lambda g, n: (0, g))
    return pl.pallas_call(
        body, name="swa_bwd", grid=(SWA_KV_HEADS, nb),
        in_specs=[q_spec, kp, kc, vp, vc, col, col, pl.BlockSpec((WINDOW, GW), lambda g, n: (n, g)),
                  pl.BlockSpec((None, None, GR, 1), lambda g, n: (g, n, 0, 0)), pl.BlockSpec((WINDOW, GW), lambda g, n: (n, DO + g))],
        out_specs=[pl.BlockSpec((WINDOW, GW), lambda g, n: (n, g)), kv_acc, kv_acc, pl.BlockSpec((None, 8, 128), lambda g, n: (g, 0, 0))],
        out_shape=[jax.ShapeDtypeStruct((T, SWA_HEADS * HEAD_DIM), F32), jax.ShapeDtypeStruct((T, SWA_KV_HEADS * HEAD_DIM), F32),
                   jax.ShapeDtypeStruct((T, SWA_KV_HEADS * HEAD_DIM), F32), jax.ShapeDtypeStruct((SWA_KV_HEADS, 8, 128), F32)],
        scratch_shapes=[pltpu.VMEM((GR, 1), F32)],
        compiler_params=_params(("parallel", "arbitrary")),
    )(qkv, qkv, qkv, qkv, qkv, slopes, sinks, out, lse, dmix)


def _mem_fwd(qkv, mk, mv):
    T, ML = qkv.shape[0], mk.shape[0]
    tq = _tile(T, 512)
    HQ = C_MQ // HEAD_DIM

    def body(q_ref, k_ref, v_ref, o_ref, lse_ref):
        s = _dot(q_ref[...], k_ref[...], "nt") * SCALE
        m = jnp.max(s, axis=-1, keepdims=True)
        p = jnp.exp(s - m)
        l = jnp.sum(p, axis=-1, keepdims=True)
        o_ref[...] = (_dot(p.astype(BF16), v_ref[...], "nn") / l).astype(BF16)
        lse_ref[...] = m + jnp.log(l)

    kv = pl.BlockSpec((ML, HEAD_DIM), lambda h, i: (0, h))
    return pl.pallas_call(
        body, name="mem_fwd", grid=(MEM_HEADS, T // tq),
        in_specs=[pl.BlockSpec((tq, HEAD_DIM), lambda h, i: (i, HQ + h)), kv, kv],
        out_specs=[pl.BlockSpec((tq, HEAD_DIM), lambda h, i: (i, h)), pl.BlockSpec((None, tq, 1), lambda h, i: (h, i, 0))],
        out_shape=[jax.ShapeDtypeStruct((T, MEM_HEADS * HEAD_DIM), BF16), jax.ShapeDtypeStruct((MEM_HEADS, T, 1), F32)],
        compiler_params=_params(("parallel", "arbitrary")),
    )(qkv, mk, mv)


def _mem_bwd(qkv, mk, mv, out, lse, dmix):
    T, ML = qkv.shape[0], mk.shape[0]
    tq = _tile(T, 512)
    HQ = C_MQ // HEAD_DIM
    DO = (FOX_W + SWA_HEADS * HEAD_DIM) // HEAD_DIM

    def body(q_ref, k_ref, v_ref, o_ref, lse_ref, do_ref, dq_ref, dk_ref, dv_ref):
        q, k, v, do = q_ref[...], k_ref[...], v_ref[...], do_ref[...]
        p = jnp.exp(_dot(q, k, "nt") * SCALE - lse_ref[...])
        dp = _dot(do, v, "nt")
        delta = jnp.sum(do.astype(F32) * o_ref[...].astype(F32), axis=-1, keepdims=True)
        dsb = (p * (dp - delta)).astype(BF16)
        dq_ref[...] = _dot(dsb, k, "nn") * SCALE
        dk_part = _dot(dsb, q, "tn") * SCALE
        dv_part = _dot(p.astype(BF16), do, "tn")

        @pl.when(pl.program_id(1) == 0)
        def _():
            dk_ref[...] = dk_part
            dv_ref[...] = dv_part

        @pl.when(pl.program_id(1) > 0)
        def _():
            dk_ref[...] += dk_part
            dv_ref[...] += dv_part

    kv = pl.BlockSpec((ML, HEAD_DIM), lambda h, i: (0, h))
    qb = pl.BlockSpec((tq, HEAD_DIM), lambda h, i: (i, h))
    return pl.pallas_call(
        body, name="mem_bwd", grid=(MEM_HEADS, T // tq),
        in_specs=[pl.BlockSpec((tq, HEAD_DIM), lambda h, i: (i, HQ + h)), kv, kv, qb,
                  pl.BlockSpec((None, tq, 1), lambda h, i: (h, i, 0)), pl.BlockSpec((tq, HEAD_DIM), lambda h, i: (i, DO + h))],
        out_specs=[qb, kv, kv],
        out_shape=[jax.ShapeDtypeStruct((T, MEM_HEADS * HEAD_DIM), F32), jax.ShapeDtypeStruct((ML, MEM_HEADS * HEAD_DIM), F32),
                   jax.ShapeDtypeStruct((ML, MEM_HEADS * HEAD_DIM), F32)],
        compiler_params=_params(("parallel", "arbitrary")),
    )(qkv, mk, mv, out, lse, dmix)


HBM = pl.BlockSpec(memory_space=pltpu.HBM)


def _place():
    x, y, c = lax.axis_index("x"), lax.axis_index("y"), lax.axis_index("c")
    chips = [(1 - x, y), (x, 1 - y), (1 - x, 1 - y)]
    return x, y, c, chips


def _remote(src, dst, send_sem, recv_sem, device):
    return pltpu.make_async_remote_copy(src_ref=src, dst_ref=dst, send_sem=send_sem, recv_sem=recv_sem,
                                        device_id=device, device_id_type=MESH)


def _place_ids():
    x, y, c = lax.axis_index("x"), lax.axis_index("y"), lax.axis_index("c")
    order = [2 * x + y, 2 * (1 - x) + y, 2 * x + (1 - y), 2 * (1 - x) + (1 - y)]
    return jnp.stack([2 * x + y, c] + order).astype(jnp.int32)


def _cast_place(name, w, ids, after):
    R, C = w.shape
    tr = _tile(R, 256, 16)

    def body(ids_ref, w_ref, after_ref, o_ref):
        o_ref[...] = w_ref[...].astype(BF16)

    return pl.pallas_call(
        body, name=name,
        grid_spec=pltpu.PrefetchScalarGridSpec(
            num_scalar_prefetch=1, grid=(R // tr,),
            in_specs=[pl.BlockSpec((tr, C), lambda i, ids: (i, 0)), pl.BlockSpec(memory_space=pl.ANY)],
            out_specs=pl.BlockSpec((None, tr, C), lambda i, ids: (ids[0], i, 0))),
        out_shape=jax.ShapeDtypeStruct((N_CHIPS, R, C), BF16), compiler_params=_params(("parallel",)),
    )(ids, w, after)


SEM = pl.BlockSpec(memory_space=pltpu.SEMAPHORE)
EFFECT = pltpu.SideEffectType.DATAFLOW_SIDE_EFFECTING


def _hbm(a):
    return pltpu.with_memory_space_constraint(a, pltpu.HBM)


def _gather_start(name, placed, after):
    n = len(placed)

    ns = 3 * n

    def body(*refs):
        send, recv = refs[n + 1:n + 1 + ns], refs[n + 1 + ns:n + 1 + 2 * ns]
        buf = refs[n + 1 + 2 * ns:2 * n + 1 + 2 * ns]
        token = refs[2 * n + 1 + 2 * ns]
        x, y, c, chips = _place()
        me = 2 * x + y
        for a in range(n):
            half = buf[a].shape[1] // 2
            mine = buf[a].at[me, pl.ds(c * half, half)]
            for j, (cx, cy) in enumerate(chips):
                _remote(mine, mine, send[3 * a + j], recv[3 * a + j], (cx, cy, c)).start()
        token[...] = jnp.zeros_like(token)

    res = pl.pallas_call(
        body, name=name, in_specs=[HBM] * n + [pl.BlockSpec(memory_space=pl.ANY)],
        out_specs=[SEM] * (2 * ns) + [HBM] * n + [pl.BlockSpec(memory_space=pltpu.VMEM)],
        out_shape=[pltpu.SemaphoreType.DMA(())] * (2 * ns)
        + [pltpu.HBM(s.shape, s.dtype) for s in placed] + [jax.ShapeDtypeStruct((8, 128), F32)],
        input_output_aliases={a: 2 * ns + a for a in range(n)},
        compiler_params=pltpu.CompilerParams(has_side_effects=EFFECT),
    )(*[_hbm(s) for s in placed], after)
    return list(res[:ns]), list(res[ns:2 * ns]), list(res[2 * ns:2 * ns + n]), res[2 * ns + n]


def _gather_wait(name, send, recv, bufs, after):
    n = len(bufs)

    ns = 3 * n

    def body(*refs):
        buf = refs[:n]
        send_ref, recv_ref = refs[n:n + ns], refs[n + ns:n + 2 * ns]
        x, y, c, chips = _place()
        ids = [2 * cx + cy for cx, cy in chips]
        for a in range(n):
            half = buf[a].shape[1] // 2
            for j in range(3):
                landed = buf[a].at[ids[j], pl.ds(c * half, half)]
                cp = _remote(landed, landed, send_ref[3 * a + j], recv_ref[3 * a + j], (x, y, c))
                cp.wait_send()
                cp.wait_recv()

    res = pl.pallas_call(
        body, name=name, in_specs=[HBM] * n + [SEM] * (2 * ns) + [pl.BlockSpec(memory_space=pl.ANY)], out_specs=[HBM] * n,
        out_shape=[pltpu.HBM(s.shape, s.dtype) for s in bufs], input_output_aliases={a: a for a in range(n)},
        compiler_params=pltpu.CompilerParams(has_side_effects=EFFECT),
    )(*bufs, *send, *recv, after)
    return list(res)


def _gather_forward(name, bufs):
    n = len(bufs)

    def body(*refs):
        buf = refs[n:2 * n]
        send, recv = refs[2 * n:]
        x, y, c, chips = _place()
        ids = [2 * cx + cy for cx, cy in chips]
        copies = []
        for a in range(n):
            half = buf[a].shape[1] // 2
            for j in range(3):
                landed = buf[a].at[ids[j], pl.ds(c * half, half)]
                cp = _remote(landed, landed, send.at[a, j], recv.at[a, j], (x, y, 1 - c))
                cp.start()
                copies.append(cp)
        for a in range(n):
            half = buf[a].shape[1] // 2
            for j in range(3):
                landed = buf[a].at[ids[j], pl.ds((1 - c) * half, half)]
                _remote(landed, landed, send.at[a, j], recv.at[a, j], (x, y, c)).wait_recv()
        for cp in copies:
            cp.wait_send()

    return pl.pallas_call(
        body, name=name, in_specs=[HBM] * n, out_specs=[HBM] * n,
        out_shape=[jax.ShapeDtypeStruct(s.shape, s.dtype) for s in bufs], input_output_aliases={a: a for a in range(n)},
        scratch_shapes=[pltpu.SemaphoreType.DMA((n, 3)), pltpu.SemaphoreType.DMA((n, 3))],
    )(*bufs)


def _pair_start(name, grads):
    n = len(grads)
    ns = N_CHIPS * n

    def body(*refs):
        send, recv = refs[2 * n:2 * n + ns], refs[2 * n + ns:2 * n + 2 * ns]
        src = refs[2 * n + 2 * ns:3 * n + 2 * ns]
        land = refs[3 * n + 2 * ns:4 * n + 2 * ns]
        token = refs[4 * n + 2 * ns]
        x, y, c, chips = _place()
        order = [2 * x + y] + [2 * cx + cy for cx, cy in chips]
        for a in range(n):
            half = src[a].shape[1] // 2
            for j in range(N_CHIPS):
                _remote(src[a].at[order[j], pl.ds((1 - c) * half, half)], land[a].at[j],
                        send[N_CHIPS * a + j], recv[N_CHIPS * a + j], (x, y, 1 - c)).start()
        token[...] = jnp.zeros_like(token)

    lands = [jax.ShapeDtypeStruct((N_CHIPS, g.shape[1] // 2, g.shape[2]), g.dtype) for g in grads]
    res = pl.pallas_call(
        body, name=name, in_specs=[HBM] * (2 * n),
        out_specs=[SEM] * (2 * ns) + [HBM] * (2 * n) + [pl.BlockSpec(memory_space=pltpu.VMEM)],
        out_shape=[pltpu.SemaphoreType.DMA(())] * (2 * ns) + [pltpu.HBM(g.shape, g.dtype) for g in grads]
        + [pltpu.HBM(l.shape, l.dtype) for l in lands] + [jax.ShapeDtypeStruct((8, 128), F32)],
        input_output_aliases={a: 2 * ns + a for a in range(2 * n)},
        compiler_params=pltpu.CompilerParams(has_side_effects=EFFECT),
    )(*[_hbm(g) for g in grads], *[_hbm(lax.empty(l.shape, l.dtype)) for l in lands])
    return list(res[:ns]), list(res[ns:2 * ns]), list(res[2 * ns:2 * ns + n]), list(res[2 * ns + n:2 * ns + 2 * n]), res[2 * ns + 2 * n]


def _pair_wait(name, send, recv, grads, lands, after):
    n = len(grads)
    ns = N_CHIPS * n

    def body(*refs):
        src, land = refs[:n], refs[n:2 * n]
        send_ref, recv_ref = refs[2 * n:2 * n + ns], refs[2 * n + ns:2 * n + 2 * ns]
        x, y, c, _ = _place()
        for a in range(n):
            for j in range(N_CHIPS):
                cp = _remote(land[a].at[j], land[a].at[j], send_ref[N_CHIPS * a + j], recv_ref[N_CHIPS * a + j], (x, y, c))
                cp.wait_send()
                cp.wait_recv()

    res = pl.pallas_call(
        body, name=name, in_specs=[HBM] * (2 * n) + [SEM] * (2 * ns) + [pl.BlockSpec(memory_space=pl.ANY)],
        out_specs=[HBM] * (2 * n), out_shape=[pltpu.HBM(g.shape, g.dtype) for g in grads] + [pltpu.HBM(l.shape, l.dtype) for l in lands],
        input_output_aliases={a: a for a in range(2 * n)},
        compiler_params=pltpu.CompilerParams(has_side_effects=EFFECT),
    )(*grads, *lands, *send, *recv, after)
    return list(res[:n]), list(res[n:])


def _chip_start(name, parts):
    n = len(parts)
    ns = 3 * n

    def body(*refs):
        send, recv = refs[2 * n:2 * n + ns], refs[2 * n + ns:2 * n + 2 * ns]
        src = refs[2 * n + 2 * ns:3 * n + 2 * ns]
        land = refs[3 * n + 2 * ns:4 * n + 2 * ns]
        token = refs[4 * n + 2 * ns]
        x, y, c, chips = _place()
        for a in range(n):
            for j, (cx, cy) in enumerate(chips):
                _remote(src[a].at[j], land[a].at[j], send[3 * a + j], recv[3 * a + j], (cx, cy, c)).start()
        token[...] = jnp.zeros_like(token)

    res = pl.pallas_call(
        body, name=name, in_specs=[HBM] * (2 * n),
        out_specs=[SEM] * (2 * ns) + [HBM] * (2 * n) + [pl.BlockSpec(memory_space=pltpu.VMEM)],
        out_shape=[pltpu.SemaphoreType.DMA(())] * (2 * ns) + [pltpu.HBM(p.shape, p.dtype) for p in parts] * 2
        + [jax.ShapeDtypeStruct((8, 128), F32)],
        input_output_aliases={a: 2 * ns + a for a in range(2 * n)},
        compiler_params=pltpu.CompilerParams(has_side_effects=EFFECT),
    )(*[_hbm(p) for p in parts], *[_hbm(lax.empty(p.shape, p.dtype)) for p in parts])
    return list(res[:ns]), list(res[ns:2 * ns]), list(res[2 * ns:2 * ns + n]), list(res[2 * ns + n:2 * ns + 2 * n]), res[2 * ns + 2 * n]


def _chip_wait(name, send, recv, parts, lands, after):
    n = len(parts)
    ns = 3 * n

    def body(*refs):
        src, land = refs[:n], refs[n:2 * n]
        send_ref, recv_ref = refs[2 * n:2 * n + ns], refs[2 * n + ns:2 * n + 2 * ns]
        x, y, c, _ = _place()
        for a in range(n):
            for j in range(3):
                cp = _remote(src[a].at[j], land[a].at[j], send_ref[3 * a + j], recv_ref[3 * a + j], (x, y, c))
                cp.wait_send()
                cp.wait_recv()

    res = pl.pallas_call(
        body, name=name, in_specs=[HBM] * (2 * n) + [SEM] * (2 * ns) + [pl.BlockSpec(memory_space=pl.ANY)],
        out_specs=[HBM] * (2 * n), out_shape=[pltpu.HBM(p.shape, p.dtype) for p in parts] * 2,
        input_output_aliases={a: a for a in range(2 * n)},
        compiler_params=pltpu.CompilerParams(has_side_effects=EFFECT),
    )(*parts, *lands, *send, *recv, after)
    return list(res[n:])


def _pair_share(name, shards):
    n = len(shards)

    def body(*refs):
        buf = refs[n:2 * n]
        send, recv = refs[2 * n:]
        x, y, c, _ = _place()
        copies = []
        for a in range(n):
            half = buf[a].shape[0] // 2
            mine = buf[a].at[pl.ds(c * half, half)]
            cp = _remote(mine, mine, send.at[a], recv.at[a], (x, y, 1 - c))
            cp.start()
            copies.append(cp)
        for a, cp in enumerate(copies):
            half = buf[a].shape[0] // 2
            cp.wait_send()
            theirs = buf[a].at[pl.ds((1 - c) * half, half)]
            _remote(theirs, theirs, send.at[a], recv.at[a], (x, y, c)).wait_recv()

    return pl.pallas_call(
        body, name=name, in_specs=[HBM] * n, out_specs=[HBM] * n,
        out_shape=[jax.ShapeDtypeStruct(s.shape, s.dtype) for s in shards], input_output_aliases={a: a for a in range(n)},
        scratch_shapes=[pltpu.SemaphoreType.DMA((n,)), pltpu.SemaphoreType.DMA((n,))],
    )(*shards)


def _small_start(buf):
    R, W = buf.shape
    ns = N_DEV - 1

    def body(*refs):
        send, recv = refs[2:2 + ns], refs[2 + ns:2 + 2 * ns]
        src, land, token = refs[2 + 2 * ns], refs[3 + 2 * ns], refs[4 + 2 * ns]
        x, y, c, _ = _place()
        me = 4 * x + 2 * y + c
        for k in range(1, N_DEV):
            peer = (x ^ (k >> 2), y ^ ((k >> 1) & 1), c ^ (k & 1))
            _remote(src, land.at[me], send[k - 1], recv[k - 1], peer).start()
        token[...] = jnp.zeros_like(token)

    res = pl.pallas_call(
        body, name="small_start", in_specs=[HBM, HBM],
        out_specs=[SEM] * (2 * ns) + [HBM, HBM, pl.BlockSpec(memory_space=pltpu.VMEM)],
        out_shape=[pltpu.SemaphoreType.DMA(())] * (2 * ns) + [pltpu.HBM((R, W), F32), pltpu.HBM((N_DEV, R, W), F32),
                                                                jax.ShapeDtypeStruct((8, 128), F32)],
        input_output_aliases={0: 2 * ns, 1: 2 * ns + 1},
        compiler_params=pltpu.CompilerParams(has_side_effects=EFFECT),
    )(_hbm(buf), _hbm(jnp.zeros((N_DEV, R, W), F32)))
    return list(res[:ns]), list(res[ns:2 * ns]), res[2 * ns], res[2 * ns + 1], res[2 * ns + 2]


def _small_wait(send, recv, buf, land, after):
    ns = N_DEV - 1

    def body(*refs):
        land_ref = refs[1]
        send_ref, recv_ref = refs[2:2 + ns], refs[2 + ns:2 + 2 * ns]
        x, y, c, _ = _place()
        me = 4 * x + 2 * y + c
        for k in range(1, N_DEV):
            landed = land_ref.at[me ^ k]
            cp = _remote(landed, landed, send_ref[k - 1], recv_ref[k - 1], (x, y, c))
            cp.wait_send()
            cp.wait_recv()

    return pl.pallas_call(
        body, name="small_wait", in_specs=[HBM, HBM] + [SEM] * (2 * ns) + [pl.BlockSpec(memory_space=pl.ANY)],
        out_specs=[HBM, HBM], out_shape=[pltpu.HBM(buf.shape, buf.dtype), pltpu.HBM(land.shape, land.dtype)],
        input_output_aliases={0: 0, 1: 1}, compiler_params=pltpu.CompilerParams(has_side_effects=EFFECT),
    )(buf, land, *send, *recv, after)


def _small_sum(buf, land):
    def body(buf_ref, land_ref, out_ref):
        x, y, c, _ = _place()
        me = 4 * x + 2 * y + c
        total = None
        for d in range(N_DEV):
            term = jnp.where(me == d, buf_ref[...], land_ref[d])
            total = term if total is None else total + term
        out_ref[...] = total

    return pl.pallas_call(body, name="small_sum", out_shape=jax.ShapeDtypeStruct(buf.shape, F32))(buf, land)


def _pair_sum_bf16(name, grad, theirs, ids):
    _, R2, C = theirs.shape
    tr = _tile(R2, 256, 16)
    nrb = R2 // tr

    def body(ids_ref, a_ref, b_ref, o_ref):
        o_ref[...] = (a_ref[...] + b_ref[...]).astype(BF16)

    return pl.pallas_call(
        body, name=name,
        grid_spec=pltpu.PrefetchScalarGridSpec(
            num_scalar_prefetch=1, grid=(3, nrb),
            in_specs=[pl.BlockSpec((None, tr, C), lambda j, i, ids: (ids[3 + j], ids[1] * nrb + i, 0)),
                      pl.BlockSpec((None, tr, C), lambda j, i, ids: (j + 1, i, 0))],
            out_specs=pl.BlockSpec((None, tr, C), lambda j, i, ids: (j, i, 0))),
        out_shape=jax.ShapeDtypeStruct((3, R2, C), BF16), compiler_params=_params(("parallel", "parallel")),
    )(ids, grad, theirs)


def _chip_sum(name, grad, theirs, arrived, ids):
    _, R2, C = theirs.shape
    tr = _tile(R2, 256, 16)
    nrb = R2 // tr

    def body(ids_ref, a_ref, b_ref, r_ref, o_ref):
        tot = a_ref[...] + b_ref[...]
        for j in range(3):
            tot = tot + r_ref[j].astype(F32)
        o_ref[...] = tot

    return pl.pallas_call(
        body, name=name,
        grid_spec=pltpu.PrefetchScalarGridSpec(
            num_scalar_prefetch=1, grid=(nrb,),
            in_specs=[pl.BlockSpec((None, tr, C), lambda i, ids: (ids[0], ids[1] * nrb + i, 0)),
                      pl.BlockSpec((None, tr, C), lambda i, ids: (0, i, 0)),
                      pl.BlockSpec((3, tr, C), lambda i, ids: (0, i, 0))],
            out_specs=pl.BlockSpec((tr, C), lambda i, ids: (ids[1] * nrb + i, 0))),
        out_shape=jax.ShapeDtypeStruct((2 * R2, C), F32), compiler_params=_params(("parallel",)),
    )(ids, grad, theirs, arrived)


def _adamw(name, w, g, m, v, emit_grad=False):
    R, C = w.shape
    tr = _tile(R, 128, 8)
    c1 = 1.0 / (1.0 - ADAM_B1 ** ADAM_STEP)
    c2 = 1.0 / (1.0 - ADAM_B2 ** ADAM_STEP)
    n_out = 4 if emit_grad else 3

    def body(w_ref, g_ref, m_ref, v_ref, d_ref, mo_ref, vo_ref, *rest):
        gv = g_ref[...]
        mn = ADAM_B1 * m_ref[...] + (1.0 - ADAM_B1) * gv
        vn = ADAM_B2 * v_ref[...] + (1.0 - ADAM_B2) * (gv * gv)
        d_ref[...] = -ADAM_LR * ((mn * c1) / (jnp.sqrt(vn * c2) + ADAM_EPS) + ADAM_WD * w_ref[...])
        mo_ref[...] = mn
        vo_ref[...] = vn
        if emit_grad:
            rest[0][...] = gv

    spec = pl.BlockSpec((tr, C), lambda i: (i, 0))
    sds = jax.ShapeDtypeStruct((R, C), F32)
    return pl.pallas_call(body, name=name, grid=(R // tr,), in_specs=[spec] * 4, out_specs=[spec] * n_out, out_shape=[sds] * n_out,
                          compiler_params=_params(("parallel",)))(w, g, m, v)


SMALL = ["ffn1_norm", "mix_norm", "mem_norm", "forget_bias", "fox_q_gain", "fox_k_gain", "swa_q_gain", "swa_k_gain", "swa_sinks",
         "mem_q_gain", "mem_k_gain", "ffn2_norm"]
LARGE = ["ffn1_gate", "ffn1_up", "ffn1_down", "w_in", "w_mem_k", "w_mem_v", "w_out", "ffn2_gate", "ffn2_up", "ffn2_down"]
GATHER_GROUPS = [["ffn1_gate", "ffn1_up"], ["ffn1_down", "w_in", "w_mem_k", "w_mem_v"], ["w_out", "ffn2_gate", "ffn2_up", "ffn2_down"]]
WEIGHTS = ["ffn1_norm", "ffn1_gate", "ffn1_up", "ffn1_down", "mix_norm", "mem_norm", "w_in", "forget_bias", "w_mem_k", "w_mem_v",
           "fox_q_gain", "fox_k_gain", "swa_q_gain", "swa_k_gain", "swa_sinks", "mem_q_gain", "mem_k_gain", "w_out", "ffn2_norm",
           "ffn2_gate", "ffn2_up", "ffn2_down"]


def _pad_proj_cols(w):
    out = jnp.zeros((w.shape[0], PROJ_W), w.dtype)
    for start, width, pstart in REF_GROUPS:
        out = lax.dynamic_update_slice(out, w[:, start:start + width], (0, pstart))
    return out


def _unpad_proj_cols(w):
    return jnp.concatenate([w[:, pstart:pstart + width] for _, width, pstart in REF_GROUPS], axis=1)


def _pack_small(vals):
    flat = jnp.concatenate([vals[k].reshape(-1).astype(F32) for k in SMALL + ["loss"]])
    n = flat.shape[0]
    total = -(-n // 1024) * 1024
    return jnp.pad(flat, (0, total - n)).reshape(total // 128, 128)


def _unpack_small(buf, shapes):
    flat = buf.reshape(-1)
    out, off = {}, 0
    for k in SMALL + ["loss"]:
        size = int(np.prod(shapes[k]))
        out[k] = flat[off:off + size].reshape(shapes[k])
        off += size
    return out


def kernel(x, mem, ffn1_norm, ffn1_gate, ffn1_up, ffn1_down, mix_norm, mem_norm, w_in, forget_bias, w_mem_k, w_mem_v, fox_q_gain, fox_k_gain, swa_q_gain, swa_k_gain, swa_sinks, mem_q_gain, mem_k_gain, w_out, ffn2_norm, ffn2_gate, ffn2_up, ffn2_down, loss_target, m_ffn1_norm, m_ffn1_gate, m_ffn1_up, m_ffn1_down, m_mix_norm, m_mem_norm, m_w_in, m_forget_bias, m_w_mem_k, m_w_mem_v, m_fox_q_gain, m_fox_k_gain, m_swa_q_gain, m_swa_k_gain, m_swa_sinks, m_mem_q_gain, m_mem_k_gain, m_w_out, m_ffn2_norm, m_ffn2_gate, m_ffn2_up, m_ffn2_down, v_ffn1_norm, v_ffn1_gate, v_ffn1_up, v_ffn1_down, v_mix_norm, v_mem_norm, v_w_in, v_forget_bias, v_w_mem_k, v_w_mem_v, v_fox_q_gain, v_fox_k_gain, v_swa_q_gain, v_swa_k_gain, v_swa_sinks, v_mem_q_gain, v_mem_k_gain, v_w_out, v_ffn2_norm, v_ffn2_gate, v_ffn2_up, v_ffn2_down):
    given = dict(locals())
    T, D = x.shape[1], x.shape[2]
    ML = mem.shape[1]
    xin = x.reshape(T, D)
    target = loss_target.reshape(T, D)
    memin = mem.reshape(ML, D)

    ids = _place_ids()
    shard = {k: given[k][0] for k in LARGE}
    started, after = [], ids
    for gi, group in enumerate(GATHER_GROUPS):
        if "w_in" in group:
            shard["w_in"] = _pad_proj_cols(shard["w_in"] + after[0, 0])
        placed = [_cast_place("cast_" + k, shard[k], ids, after) for k in group]
        send, recv, bufs, after = _gather_start("gather_start_%d" % gi, placed, after)
        started.append((send, recv, bufs))

    def arrive(gi, done):
        send, recv, bufs = started[gi]
        bufs = _gather_wait("gather_wait_%d" % gi, send, recv, bufs, done)
        return dict(zip(GATHER_GROUPS[gi], _gather_forward("gather_forward_%d" % gi, bufs)))

    gains = jnp.concatenate([fox_q_gain, fox_k_gain, swa_q_gain, swa_k_gain, mem_q_gain,
                             jnp.pad(forget_bias, ((0, 0), (0, HEAD_DIM - FOX_HEADS))), jnp.zeros((2, HEAD_DIM), F32)], axis=0)
    slopes_np = 2.0 ** (-8.0 * np.arange(1, SWA_HEADS + 1) / SWA_HEADS)
    slopes = jnp.asarray(np.repeat(slopes_np, WINDOW).reshape(SWA_KV_HEADS, GR, 1), F32)
    sinks = jnp.repeat(swa_sinks.reshape(SWA_HEADS), WINDOW).reshape(SWA_KV_HEADS, GR, 1)

    h1 = _rms_fwd("ffn1_norm_fwd", xin, ffn1_norm + after[0, 0])
    full = arrive(0, h1)
    wg1, wu1 = full["ffn1_gate"], full["ffn1_up"]
    g1, u1, a1 = _ffn_gu("ffn1_gate_up", h1, wg1, wu1)
    full = arrive(1, a1)
    wd1 = full["ffn1_down"].reshape(-1, D)
    win = full["w_in"].reshape(D, PROJ_W)
    wmk = full["w_mem_k"].reshape(D, MEM_HEADS * HEAD_DIM)
    wmv = full["w_mem_v"].reshape(D, MEM_HEADS * HEAD_DIM)
    x1 = _ffn_down("ffn1_down", a1, wd1, xin)
    h2 = _rms_fwd("mix_norm_fwd", x1, mix_norm)
    proj = _mm2d("proj_in", h2, win, "nn", F32, tn=1408, tk=2048, n_outer=True)
    qkv, logf, k_t, v_t = _prep_fwd(proj, gains)
    cum = _cumsum_rows("forget_cumsum", [logf], False)
    cum_h = cum[:, :FOX_HEADS].T
    cq_row = cum_h.reshape(FOX_HEADS, 1, T)
    ck_rep = jnp.broadcast_to(cum_h[:, :, None], (FOX_HEADS, T, HEAD_DIM))
    mn = _rms_fwd("mem_norm_fwd", memin, mem_norm)
    mk_raw = _mm2d("mem_k_proj", mn, wmk, "nn", F32)
    mv = _mm2d("mem_v_proj", mn, wmv, "nn", BF16)
    mk = _head_norm_rows(mk_raw, mem_k_gain)
    out_a, out_a_f32, lse_a = _fox_fwd(qkv, v_t, cq_row, ck_rep)
    out_b, lse_b = _swa_fwd(qkv, slopes, sinks)
    out_c, lse_c = _mem_fwd(qkv, mk, mv)
    mixed = jnp.concatenate([out_a, out_b, out_c], axis=1)
    full = arrive(2, mixed)
    wo = full["w_out"].reshape(-1, D)
    wg2, wu2, wd2 = full["ffn2_gate"], full["ffn2_up"], full["ffn2_down"].reshape(-1, D)
    x2 = _mm2d("mix_out", mixed, wo, "nn", F32, tk=2048, n_outer=True, extras=[x1], epilogue=lambda accs, ex: [ex[0] + accs[0]])
    h3 = _rms_fwd("ffn2_norm_fwd", x2, ffn2_norm)
    g2, u2, a2 = _ffn_gu("ffn2_gate_up", h3, wg2, wu2)
    x3 = _ffn_down("ffn2_down", a2, wd2, x2)
    dx3, dyb3, loss_part = _loss_head(x3, target)

    grads, small, res = {}, {"loss": loss_part[0, 0]}, {}

    def pair_off(tag, group):
        send, recv, own, lands, token = _pair_start("grad_pair_start_" + tag, [grads[k] for k in group])
        return (group, send, recv, own, lands), token

    def chip_off(tag, started, done):
        group, send, recv, own, lands = started
        own, theirs = _pair_wait("grad_pair_wait_" + tag, send, recv, own, lands, done)
        grads.update(zip(group, own))
        to_chips = [_pair_sum_bf16("pair_sum_" + k, grads[k], b, ids) for k, b in zip(group, theirs)]
        send, recv, parts, lands, token = _chip_start("grad_chip_start_" + tag, to_chips)
        return (group, theirs, send, recv, parts, lands), token

    def finish(tag, state, done):
        group, theirs, send, recv, parts, lands = state
        arrived = _chip_wait("grad_chip_wait_" + tag, send, recv, parts, lands, done)
        halves = [_chip_sum("chip_sum_" + k, grads[k], b, r, ids) for k, b, r in zip(group, theirs, arrived)]
        reduced = dict(zip(group, _pair_share("grad_pair_share_" + tag, halves)))
        last = None
        for k in group:
            if k == "w_in":
                gk = _unpad_proj_cols(reduced[k])
                d, mo, vo = _adamw("adamw_" + k, given[k][0], gk, given["m_" + k][0], given["v_" + k][0])
            else:
                d, mo, vo, gk = _adamw("adamw_" + k, given[k][0], reduced[k], given["m_" + k][0], given["v_" + k][0], emit_grad=True)
            res[k] = tuple(t[None] for t in (gk, d, mo, vo))
            last = vo
        return last

    dg2, du2 = _ffn_bwd_act("ffn2", dyb3, wd2, g2, u2, N_CHIPS)
    grads["ffn2_down"] = _ffn_bwd_down("ffn2", a2, dyb3, N_CHIPS).reshape(N_CHIPS, -1, D)
    grads["ffn2_gate"], grads["ffn2_up"] = _ffn_bwd_gate_up("ffn2", h3, dg2, du2, N_CHIPS)
    started, token = pair_off("a", ["ffn2_gate", "ffn2_up", "ffn2_down"])
    dh3 = _ffn_bwd_x("ffn2", dg2, du2, wg2, wu2, token)
    state_a, token = chip_off("a", started, dh3)
    dx2, dx2b, small["ffn2_norm"] = _rms_bwd("ffn2_norm_bwd", dh3, x2, ffn2_norm + token[0, 0], dx3, 1.0)
    dmix = _mm2d("mix_out_dx", dx2b, wo, "nt", BF16, tk=2048, n_outer=True)
    grads["w_out"] = _mm2d("mix_out_dw", mixed, dx2b, "tn", F32, tk=T, n_outer=True, resident=True).reshape(N_CHIPS, -1, D)
    delta_row = _fox_delta(dmix, out_a_f32)[:, :FOX_HEADS].T.reshape(FOX_HEADS, 1, T)
    dfq, dfk, dfv, dck, dcq = _fox_bwd(qkv, k_t, cq_row, ck_rep, delta_row, lse_a, dmix)
    dsq, dsk, dsv, dsink = _swa_bwd(qkv, slopes, sinks, out_b, lse_b, dmix)
    dmq, dmk, dmv = _mem_bwd(qkv, mk, mv, out_c, lse_c, dmix)
    small["swa_sinks"] = dsink[:, :SWA_GROUP, 0].reshape(1, SWA_HEADS)
    dcum = jnp.pad(dcq.reshape(FOX_HEADS, T).T, ((0, 0), (0, HEAD_DIM - FOX_HEADS)))
    dlogf = _cumsum_rows("forget_cumsum_bwd", [dcum], True, columns=dck)
    dproj, dgains = _prep_bwd(proj, gains, dfq, dfk, dfv, dsq, dsk, dsv, dmq, dlogf)
    for row, k in enumerate(["fox_q_gain", "fox_k_gain", "swa_q_gain", "swa_k_gain", "mem_q_gain"]):
        small[k] = dgains[row:row + 1, :]
    small["forget_bias"] = dgains[5:6, :FOX_HEADS]
    grads["w_in"] = _mm2d("proj_in_dw", h2, dproj, "tn", F32, tn=1408, tk=T, n_outer=True, resident=True).reshape(N_CHIPS, -1, PROJ_W)
    dmk_raw, small["mem_k_gain"] = _head_norm_rows_bwd(mk_raw, mem_k_gain, dmk)
    dmvb = dmv.astype(BF16)
    grads["w_mem_k"] = _mm2d("mem_k_dw", mn, dmk_raw, "tn", F32).reshape(N_CHIPS, -1, MEM_HEADS * HEAD_DIM)
    grads["w_mem_v"] = _mm2d("mem_v_dw", mn, dmvb, "tn", F32).reshape(N_CHIPS, -1, MEM_HEADS * HEAD_DIM)
    dmn = _mm2d("mem_k_dx", dmk_raw, wmk, "nt", F32)
    dmn = _mm2d("mem_v_dx", dmvb, wmv, "nt", F32, extras=[dmn], epilogue=lambda accs, ex: [ex[0] + accs[0]])
    _, _, small["mem_norm"] = _rms_bwd("mem_norm_bwd", dmn, memin, mem_norm, jnp.zeros_like(memin), 1.0)
    started, token = pair_off("b", ["w_out", "w_in", "w_mem_k", "w_mem_v"])
    dh2 = _mm2d("proj_in_dx", dproj, win, "nt", F32, tm=1024, tk=1408, after=token)
    state_b, token = chip_off("b", started, dh2)
    dx1, dyb1, small["mix_norm"] = _rms_bwd("mix_norm_bwd", dh2, x1, mix_norm + token[0, 0], dx2, 0.5)
    grads["ffn1_down"] = _ffn_bwd_down("ffn1", a1, dyb1, N_CHIPS).reshape(N_CHIPS, -1, D)
    started, token = pair_off("c", ["ffn1_down"])
    dg1, du1 = _ffn_bwd_act("ffn1", dyb1, wd1, g1, u1, N_CHIPS, after=token)
    state_c, token = chip_off("c", started, dg1)
    grads["ffn1_gate"], grads["ffn1_up"] = _ffn_bwd_gate_up("ffn1", h1, dg1, du1, N_CHIPS, after=token)
    started, token = pair_off("d", ["ffn1_gate", "ffn1_up"])
    dh1 = _ffn_bwd_x("ffn1", dg1, du1, wg1, wu1, token)
    state_d, token = chip_off("d", started, dh1)
    grad_x, _, small["ffn1_norm"] = _rms_bwd("ffn1_norm_bwd", dh1, xin, ffn1_norm + token[0, 0], dx1, 1.0)

    s_send, s_recv, s_buf, s_land, token = _small_start(_pack_small(small))

    done = finish("a", state_a, token)
    done = finish("b", state_b, done)
    done = finish("c", state_c, done)
    done = finish("d", state_d, done)

    shapes = {k: given[k].shape for k in SMALL}
    shapes["loss"] = ()
    s_buf, s_land = _small_wait(s_send, s_recv, s_buf, s_land, done)
    red_small = _unpack_small(_small_sum(s_buf, s_land), shapes)
    loss = red_small["loss"]
    zero = {"loss": jnp.zeros((), F32)}
    packed = [_pack_small({**zero, **{k: src[k] for k in SMALL}}) for src in (
        {k: given[k] for k in SMALL}, red_small, {k: given["m_" + k] for k in SMALL}, {k: given["v_" + k] for k in SMALL})]
    d_s, m_s, v_s = (_unpack_small(t, shapes) for t in _adamw("adamw_small", *packed))
    for k in SMALL:
        res[k] = (red_small[k], d_s[k], m_s[k], v_s[k])

    outs = [loss, grad_x.reshape(1, T, D)]
    for part in range(4):
        outs += [res[k][part] for k in WEIGHTS]
    return tuple(outs)
```

```python
import functools

import numpy as np
import jax
import jax.numpy as jnp
from jax import lax
from jax.experimental import pallas as pl
from jax.experimental.pallas import tpu as pltpu

F32 = jnp.float32
BF16 = jnp.bfloat16
MESH = pl.DeviceIdType.MESH

HEAD_DIM = 128
FOX_HEADS = 6
SWA_HEADS = 6
SWA_KV_HEADS = 2
SWA_GROUP = SWA_HEADS // SWA_KV_HEADS
MEM_HEADS = 4
WINDOW = 128
EPS = 1e-6
NEG_INF = -1e30
SCALE = HEAD_DIM ** -0.5

C_FQ = 0
C_FK = C_FQ + FOX_HEADS * HEAD_DIM
C_FV = C_FK + FOX_HEADS * HEAD_DIM
C_SQ = C_FV + FOX_HEADS * HEAD_DIM
C_SK = C_SQ + SWA_HEADS * HEAD_DIM
C_SV = C_SK + SWA_KV_HEADS * HEAD_DIM
C_MQ = C_SV + SWA_KV_HEADS * HEAD_DIM
C_FL = C_MQ + MEM_HEADS * HEAD_DIM
PROJ_W = C_FL + HEAD_DIM
FOX_W = FOX_HEADS * HEAD_DIM
REF_GROUPS = [
    (0, FOX_W, C_FQ), (FOX_W, FOX_W, C_FK), (2 * FOX_W, FOX_W, C_FV), (3 * FOX_W, FOX_HEADS, C_FL),
    (3 * FOX_W + FOX_HEADS, SWA_HEADS * HEAD_DIM, C_SQ),
    (3 * FOX_W + FOX_HEADS + SWA_HEADS * HEAD_DIM, SWA_KV_HEADS * HEAD_DIM, C_SK),
    (3 * FOX_W + FOX_HEADS + (SWA_HEADS + SWA_KV_HEADS) * HEAD_DIM, SWA_KV_HEADS * HEAD_DIM, C_SV),
    (3 * FOX_W + FOX_HEADS + (SWA_HEADS + 2 * SWA_KV_HEADS) * HEAD_DIM, MEM_HEADS * HEAD_DIM, C_MQ),
]

ADAM_LR = 0.001
ADAM_B1 = 0.9
ADAM_B2 = 0.999
ADAM_EPS = 1e-08
ADAM_WD = 0.01
ADAM_STEP = 10

V7X_VMEM_LIMIT = 56 * 1024 * 1024
N_CHIPS = 4
N_DEV = 8


def _tile(n, pref, mult=128):
    t = (min(pref, n) // mult) * mult
    while t >= mult:
        if n % t == 0:
            return t
        t -= mult
    return n


def _params(sem):
    return pltpu.CompilerParams(dimension_semantics=sem, vmem_limit_bytes=V7X_VMEM_LIMIT)


_DIMS = {"nn": (((1,), (0,)), ((), ())), "nt": (((1,), (1,)), ((), ())), "tn": (((0,), (0,)), ((), ()))}


def _dot(a, b, mode):
    return lax.dot_general(a, b, _DIMS[mode], preferred_element_type=F32)


def _mm(name, grid, pairs, acc_of, acc_shapes, extras, outs, epilogue, after=None):
    n_p, n_e, n_o, n_a = len(pairs), len(extras), len(outs), len(acc_shapes)
    n_w = 0 if after is None else 1
    nk = grid[2]
    n_in = sum(1 if a is None else 2 for a, *_ in pairs)

    def body(*refs):
        ex = refs[n_in:n_in + n_e]
        out = refs[n_in + n_e + n_w:n_in + n_e + n_w + n_o]
        accs = refs[n_in + n_e + n_w + n_o:]
        parts = [None] * n_a
        at = 0
        for p in range(n_p):
            if pairs[p][0] is None:
                a_ref, b_ref = refs[0], refs[at]
                at += 1
            else:
                a_ref, b_ref = refs[at], refs[at + 1]
                at += 2
            d = _dot(a_ref[...], b_ref[...], pairs[p][4])
            parts[acc_of[p]] = d if parts[acc_of[p]] is None else parts[acc_of[p]] + d

        def finish(vals):
            for o, r in zip(out, epilogue(vals, [e[...] for e in ex])):
                o[...] = r.astype(o.dtype)

        if nk == 1:
            finish(parts)
            return
        k = pl.program_id(2)

        @pl.when(k == 0)
        def _():
            for a, d in zip(accs, parts):
                a[...] = d

        @pl.when((k > 0) & (k < nk - 1))
        def _():
            for a, d in zip(accs, parts):
                a[...] += d

        @pl.when(k == nk - 1)
        def _():
            finish([a[...] + d for a, d in zip(accs, parts)])

    in_specs, args = [], []
    for a, a_spec, b, b_spec, _ in pairs:
        if a is not None:
            in_specs.append(a_spec)
            args.append(a)
        in_specs.append(b_spec)
        args.append(b)
    for e, e_spec in extras:
        in_specs.append(e_spec)
        args.append(e)
    if after is not None:
        in_specs.append(pl.BlockSpec(memory_space=pl.ANY))
        args.append(after)
    res = pl.pallas_call(
        body, name=name, grid=grid, in_specs=in_specs,
        out_specs=[s for _, s in outs], out_shape=[o for o, _ in outs],
        scratch_shapes=[pltpu.VMEM(s, F32) for s in acc_shapes] if nk > 1 else [],
        compiler_params=_params(("parallel", "parallel", "arbitrary")),
    )(*args)
    return res


def _mm2d(name, a, b, mode, out_dtype, tm=512, tn=1024, tk=1024, extras=(), epilogue=None, n_out=1, after=None, n_outer=False,
          resident=False):
    if mode == "nn":
        (M, K), N = a.shape, b.shape[1]
    elif mode == "nt":
        (M, K), N = a.shape, b.shape[0]
    else:
        (K, M), N = a.shape, b.shape[1]
    tm, tn, tk = _tile(M, tm), _tile(N, tn), _tile(K, tk)
    assert not resident or tk == K

    def spec(shape, index, single=False):
        mode_kw = {"pipeline_mode": pl.Buffered(1)} if single else {}
        if n_outer:
            return pl.BlockSpec(shape, lambda j, i, k: index(i, j, k), **mode_kw)
        return pl.BlockSpec(shape, index, **mode_kw)

    single_a, single_b = resident and not n_outer, resident and n_outer
    a_spec = spec((tk, tm), lambda i, j, k: (k, i), single_a) if mode == "tn" else spec((tm, tk), lambda i, j, k: (i, k), single_a)
    b_spec = spec((tn, tk), lambda i, j, k: (j, k), single_b) if mode == "nt" else spec((tk, tn), lambda i, j, k: (k, j), single_b)
    mn = spec((tm, tn), lambda i, j, k: (i, j))
    if epilogue is None:
        epilogue = lambda accs, ex: [accs[0]]
    if not isinstance(out_dtype, (list, tuple)):
        out_dtype = [out_dtype] * n_out
    grid = (N // tn, M // tm, K // tk) if n_outer else (M // tm, N // tn, K // tk)
    res = _mm(name, grid, [(a, a_spec, b, b_spec, mode)], [0], [(tm, tn)],
              [(e, mn) for e in extras], [(jax.ShapeDtypeStruct((M, N), d), mn) for d in out_dtype], epilogue, after=after)
    return res[0] if len(res) == 1 else res


def _sigmoid(x):
    return 1.0 / (1.0 + jnp.exp(-x))


def _sigmoid_fast(x):
    return pl.reciprocal(1.0 + jnp.exp(-x), approx=True)


def _ffn_gu(name, h, wg, wu):
    T, D = h.shape
    nf, _, F4 = wg.shape
    tm, tk = _tile(T, 512), _tile(D, 2048)
    a_spec = pl.BlockSpec((tm, tk), lambda j, i, k: (i, k))
    b_spec = pl.BlockSpec((None, tk, F4), lambda j, i, k: (j, k, 0))
    o_spec = pl.BlockSpec((tm, F4), lambda j, i, k: (i, j))

    def epilogue(accs, ex):
        g, u = accs
        s = _sigmoid_fast(g)
        gs = g * s
        return [(s + s * (g - gs)) * u, gs, gs * u]

    sds = jax.ShapeDtypeStruct((T, nf * F4), BF16)
    return _mm(name, (nf, T // tm, D // tk), [(h, a_spec, wg, b_spec, "nn"), (None, None, wu, b_spec, "nn")], [0, 1],
               [(tm, F4), (tm, F4)], [], [(sds, o_spec)] * 3, epilogue)


def _ffn_down(name, a, wd, xres):
    return _mm2d(name, a, wd, "nn", F32, tm=512, tn=1024, tk=wd.shape[0], n_outer=True, resident=True, extras=[xres],
                 epilogue=lambda accs, ex: [ex[0] + 0.5 * accs[0]])


def _ffn_bwd_down(tag, a, dyb, nf, after=None):
    return _mm2d(tag + "_dwd", a, dyb, "tn", F32, tm=a.shape[1] // nf, tn=1024, tk=a.shape[0], resident=True, after=after)


def _ffn_bwd_act(tag, dyb, wd, da_dg, da_du, nf, after=None):
    def act_bwd(accs, ex):
        return [accs[0] * ex[0].astype(F32), accs[0] * ex[1].astype(F32)]

    return _mm2d(tag + "_da", dyb, wd, "nt", BF16, tn=wd.shape[0] // nf, tk=2048, extras=[da_dg, da_du], epilogue=act_bwd, n_out=2,
                 n_outer=True, after=after)


def _ffn_bwd_gate_up(tag, h, dg, du, nf, after=None):
    T, D = h.shape
    F4 = dg.shape[1] // nf
    tm = _tile(D, 512)
    h_spec = pl.BlockSpec((T, tm), lambda j, i, k: (0, i))
    d_spec = pl.BlockSpec((T, F4), lambda j, i, k: (0, j), pipeline_mode=pl.Buffered(1))
    w_spec = pl.BlockSpec((None, tm, F4), lambda j, i, k: (j, i, 0))
    sds = jax.ShapeDtypeStruct((nf, D, F4), F32)
    return _mm(tag + "_dwgu", (nf, D // tm, 1), [(h, h_spec, dg, d_spec, "tn"), (None, None, du, d_spec, "tn")],
               [0, 1], [(tm, F4), (tm, F4)], [], [(sds, w_spec)] * 2, lambda accs, ex: accs, after=after)


def _ffn_bwd_x(tag, dg, du, wg, wu, after):
    T = dg.shape[0]
    nf, D, F4 = wg.shape
    tm, tn = _tile(T, 1024), _tile(D, 1024)
    a_spec = pl.BlockSpec((tm, F4), lambda i, j, k: (i, k))
    b_spec = pl.BlockSpec((None, tn, F4), lambda i, j, k: (k, j, 0))
    o_spec = pl.BlockSpec((tm, tn), lambda i, j, k: (i, j))
    (dh,) = _mm(tag + "_dh", (T // tm, D // tn, nf), [(dg, a_spec, wg, b_spec, "nt"), (du, a_spec, wu, b_spec, "nt")],
                [0, 0], [(tm, tn)], [], [(jax.ShapeDtypeStruct((T, D), F32), o_spec)], lambda accs, ex: accs, after=after)
    return dh


def _rms_fwd(name, x, gain):
    R, D = x.shape
    tr = _tile(R, 256, 8)

    def body(x_ref, g_ref, o_ref):
        xv = x_ref[...]
        r = lax.rsqrt(jnp.mean(xv * xv, axis=-1, keepdims=True) + EPS)
        o_ref[...] = (xv * r * g_ref[...]).astype(BF16)

    return pl.pallas_call(
        body, name=name, grid=(R // tr,),
        in_specs=[pl.BlockSpec((tr, D), lambda i: (i, 0)), pl.BlockSpec((1, D), lambda i: (0, 0))],
        out_specs=pl.BlockSpec((tr, D), lambda i: (i, 0)), out_shape=jax.ShapeDtypeStruct((R, D), BF16),
        compiler_params=_params(("parallel",)),
    )(x, gain)


def _rms_bwd(name, dh, x, gain, dres, bscale):
    R, D = x.shape
    tr = _tile(R, 256, 8)

    def body(dh_ref, x_ref, g_ref, dres_ref, dx_ref, dxb_ref, dg_ref):
        xv, dy = x_ref[...], dh_ref[...]
        r = lax.rsqrt(jnp.mean(xv * xv, axis=-1, keepdims=True) + EPS)
        xn = xv * r
        uu = dy * g_ref[...]
        dx = dres_ref[...] + r * (uu - xn * jnp.mean(xn * uu, axis=-1, keepdims=True))
        dx_ref[...] = dx
        dxb_ref[...] = (bscale * dx).astype(BF16)
        part = jnp.sum(dy * xn, axis=0, keepdims=True)

        @pl.when(pl.program_id(0) == 0)
        def _():
            dg_ref[...] = part

        @pl.when(pl.program_id(0) > 0)
        def _():
            dg_ref[...] += part

    row = pl.BlockSpec((tr, D), lambda i: (i, 0))
    vec = pl.BlockSpec((1, D), lambda i: (0, 0))
    return pl.pallas_call(
        body, name=name, grid=(R // tr,), in_specs=[row, row, vec, row], out_specs=[row, row, vec],
        out_shape=[jax.ShapeDtypeStruct((R, D), F32), jax.ShapeDtypeStruct((R, D), BF16), jax.ShapeDtypeStruct((1, D), F32)],
        compiler_params=_params(("arbitrary",)),
    )(dh, x, gain, dres)


def _loss_head(y, target):
    R, D = y.shape
    tr = _tile(R, 256, 8)

    def body(y_ref, t_ref, d_ref, db_ref, l_ref):
        e = y_ref[...] - t_ref[...]
        d = e * (1.0 / D)
        d_ref[...] = d
        db_ref[...] = (0.5 * d).astype(BF16)
        part = jnp.zeros((8, 128), F32) + (0.5 / D) * jnp.sum(e * e)

        @pl.when(pl.program_id(0) == 0)
        def _():
            l_ref[...] = part

        @pl.when(pl.program_id(0) > 0)
        def _():
            l_ref[...] += part

    row = pl.BlockSpec((tr, D), lambda i: (i, 0))
    acc = pl.BlockSpec((8, 128), lambda i: (0, 0))
    return pl.pallas_call(
        body, name="loss_head", grid=(R // tr,), in_specs=[row, row], out_specs=[row, row, acc],
        out_shape=[jax.ShapeDtypeStruct((R, D), F32), jax.ShapeDtypeStruct((R, D), BF16), jax.ShapeDtypeStruct((8, 128), F32)],
        compiler_params=_params(("arbitrary",)),
    )(y, target)


def _head_norm(xs, g):
    r = lax.rsqrt(jnp.mean(xs * xs, axis=-1, keepdims=True) + EPS)
    return xs * r * g


def _head_norm_bwd(xs, g, dy):
    r = lax.rsqrt(jnp.mean(xs * xs, axis=-1, keepdims=True) + EPS)
    xn = xs * r
    uu = dy * g
    return r * (uu - xn * jnp.mean(xn * uu, axis=-1, keepdims=True)), jnp.sum(dy * xn, axis=0, keepdims=True)


NORMED = [(C_FQ, FOX_HEADS, 0), (C_FK, FOX_HEADS, 1), (C_SQ, SWA_HEADS, 2), (C_SK, SWA_KV_HEADS, 3), (C_MQ, MEM_HEADS, 4)]
PLAIN = [(C_FV, FOX_HEADS), (C_SV, SWA_KV_HEADS)]


def _prep_fwd(proj, gains):
    T = proj.shape[0]
    tr = _tile(T, 256, 128)

    def body(p_ref, g_ref, o_ref, lf_ref, kt_ref, vt_ref):
        for start, heads, row in NORMED:
            gn = g_ref[row:row + 1, :]
            for hh in range(heads):
                sl = slice(start + hh * HEAD_DIM, start + (hh + 1) * HEAD_DIM)
                y = _head_norm(p_ref[:, sl], gn)
                o_ref[:, sl] = y.astype(BF16)
                if start == C_FK:
                    kt_ref[hh * HEAD_DIM:(hh + 1) * HEAD_DIM, :] = y.T.astype(BF16)
        for start, heads in PLAIN:
            sl = slice(start, start + heads * HEAD_DIM)
            o_ref[:, sl] = p_ref[:, sl].astype(BF16)
        for hh in range(FOX_HEADS):
            sl = slice(C_FV + hh * HEAD_DIM, C_FV + (hh + 1) * HEAD_DIM)
            vt_ref[hh * HEAD_DIM:(hh + 1) * HEAD_DIM, :] = p_ref[:, sl].T.astype(BF16)
        zb = p_ref[:, C_FL:C_FL + HEAD_DIM] + g_ref[5:6, :]
        o_ref[:, C_FL:C_FL + HEAD_DIM] = jnp.zeros((tr, HEAD_DIM), BF16)
        lf_ref[...] = jnp.minimum(zb, 0.0) - jnp.log(1.0 + jnp.exp(-jnp.abs(zb)))

    return pl.pallas_call(
        body, name="prep_fwd", grid=(T // tr,),
        in_specs=[pl.BlockSpec((tr, PROJ_W), lambda i: (i, 0)), pl.BlockSpec((8, 128), lambda i: (0, 0))],
        out_specs=[pl.BlockSpec((tr, PROJ_W), lambda i: (i, 0)), pl.BlockSpec((tr, HEAD_DIM), lambda i: (i, 0)),
                   pl.BlockSpec((FOX_W, tr), lambda i: (0, i)), pl.BlockSpec((FOX_W, tr), lambda i: (0, i))],
        out_shape=[jax.ShapeDtypeStruct((T, PROJ_W), BF16), jax.ShapeDtypeStruct((T, HEAD_DIM), F32),
                   jax.ShapeDtypeStruct((FOX_W, T), BF16), jax.ShapeDtypeStruct((FOX_W, T), BF16)],
        compiler_params=_params(("parallel",)),
    )(proj, gains)


def _prep_bwd(proj, gains, dfq, dfk, dfv, dsq, dsk, dsv, dmq, dlogf):
    T = proj.shape[0]
    tr = _tile(T, 256, 8)
    d_normed = {C_FQ: 0, C_FK: 1, C_SQ: 3, C_SK: 4, C_MQ: 6}
    d_plain = {C_FV: 2, C_SV: 5}

    def body(p_ref, g_ref, *rest):
        d_refs, dlf_ref, o_ref, dg_ref = rest[:7], rest[7], rest[8], rest[9]
        rows = []
        for start, heads, row in NORMED:
            gn = g_ref[row:row + 1, :]
            d_ref = d_refs[d_normed[start]]
            tot = jnp.zeros((1, HEAD_DIM), F32)
            for hh in range(heads):
                sl = slice(start + hh * HEAD_DIM, start + (hh + 1) * HEAD_DIM)
                dx, dgn = _head_norm_bwd(p_ref[:, sl], gn, d_ref[:, hh * HEAD_DIM:(hh + 1) * HEAD_DIM])
                o_ref[:, sl] = dx.astype(BF16)
                tot = tot + dgn
            rows.append(tot)
        for start, heads in PLAIN:
            o_ref[:, start:start + heads * HEAD_DIM] = d_refs[d_plain[start]][...].astype(BF16)
        zb = p_ref[:, C_FL:C_FL + HEAD_DIM] + g_ref[5:6, :]
        lane = lax.broadcasted_iota(jnp.int32, (tr, HEAD_DIM), 1)
        dz = jnp.where(lane < FOX_HEADS, dlf_ref[...] * (1.0 - _sigmoid(zb)), 0.0)
        o_ref[:, C_FL:C_FL + HEAD_DIM] = dz.astype(BF16)
        rows.append(jnp.sum(dz, axis=0, keepdims=True))
        part = jnp.concatenate(rows + [jnp.zeros((2, HEAD_DIM), F32)], axis=0)

        @pl.when(pl.program_id(0) == 0)
        def _():
            dg_ref[...] = part

        @pl.when(pl.program_id(0) > 0)
        def _():
            dg_ref[...] += part

    def rows_of(w):
        return pl.BlockSpec((tr, w), lambda i: (i, 0))

    small = pl.BlockSpec((8, 128), lambda i: (0, 0))
    ds = [dfq, dfk, dfv, dsq, dsk, dsv, dmq]
    return pl.pallas_call(
        body, name="prep_bwd", grid=(T // tr,),
        in_specs=[rows_of(PROJ_W), small] + [rows_of(d.shape[1]) for d in ds] + [rows_of(HEAD_DIM)],
        out_specs=[rows_of(PROJ_W), small],
        out_shape=[jax.ShapeDtypeStruct((T, PROJ_W), BF16), jax.ShapeDtypeStruct((8, 128), F32)],
        compiler_params=_params(("arbitrary",)),
    )(proj, gains, *ds, dlogf)


def _head_norm_rows(x, gain):
    R, W = x.shape

    def body(x_ref, g_ref, o_ref):
        for hh in range(W // HEAD_DIM):
            sl = slice(hh * HEAD_DIM, (hh + 1) * HEAD_DIM)
            o_ref[:, sl] = _head_norm(x_ref[:, sl], g_ref[...]).astype(BF16)

    return pl.pallas_call(body, name="mem_k_norm", out_shape=jax.ShapeDtypeStruct((R, W), BF16))(x, gain)


def _head_norm_rows_bwd(x, gain, dy):
    R, W = x.shape

    def body(x_ref, g_ref, dy_ref, dx_ref, dg_ref):
        tot = jnp.zeros((1, HEAD_DIM), F32)
        for hh in range(W // HEAD_DIM):
            sl = slice(hh * HEAD_DIM, (hh + 1) * HEAD_DIM)
            dx, dgn = _head_norm_bwd(x_ref[:, sl], g_ref[...], dy_ref[:, sl])
            dx_ref[:, sl] = dx.astype(BF16)
            tot = tot + dgn
        dg_ref[...] = tot

    return pl.pallas_call(
        body, name="mem_k_norm_bwd",
        out_shape=[jax.ShapeDtypeStruct((R, W), BF16), jax.ShapeDtypeStruct((1, HEAD_DIM), F32)])(x, gain, dy)


def _cumsum_rows(name, xs, reverse, columns=None):
    T, W = xs[0].shape
    tb = _tile(T, 512, 8)
    nb = T // tb
    n_in = len(xs) + (0 if columns is None else 1)

    def body(*refs):
        o_ref, carry = refs[n_in], refs[n_in + 1]

        @pl.when(pl.program_id(0) == 0)
        def _():
            carry[...] = jnp.zeros_like(carry)

        xv = refs[0][...]
        for x_ref in refs[1:len(xs)]:
            xv = xv + x_ref[...]
        if columns is not None:
            lane = lax.broadcasted_iota(jnp.int32, (tb, W), 1)
            for hh in range(columns.shape[0]):
                xv = xv + jnp.where(lane == hh, refs[len(xs)][hh], 0.0)
        r = lax.broadcasted_iota(jnp.int32, (tb, tb), 0)
        cc = lax.broadcasted_iota(jnp.int32, (tb, tb), 1)
        tri = jnp.where((cc >= r) if reverse else (cc <= r), 1.0, 0.0).astype(F32)
        o_ref[...] = jnp.dot(tri, xv, precision=lax.Precision.HIGHEST, preferred_element_type=F32) + carry[...]
        carry[...] += jnp.sum(xv, axis=0, keepdims=True)

    idx = (lambda i: (nb - 1 - i, 0)) if reverse else (lambda i: (i, 0))
    in_specs = [pl.BlockSpec((tb, W), idx)] * len(xs)
    if columns is not None:
        in_specs.append(pl.BlockSpec((columns.shape[0], tb, 1), lambda i: (0, idx(i)[0], 0)))
    return pl.pallas_call(
        body, name=name, grid=(nb,), in_specs=in_specs, out_specs=pl.BlockSpec((tb, W), idx),
        out_shape=jax.ShapeDtypeStruct((T, W), F32), scratch_shapes=[pltpu.VMEM((1, W), F32)],
        compiler_params=_params(("arbitrary",)),
    )(*xs, *([] if columns is None else [columns]))


def _triangle(nq, by_column):
    if by_column:
        blocks = [(i, j) for j in range(nq) for i in range(j, nq)]
    else:
        blocks = [(i, j) for i in range(nq) for j in range(i + 1)]
    return jnp.asarray(np.array(blocks, np.int32).T)


def _fox_scores_t(k, q, cq_row, ck_rep, on_diagonal):
    n = q.shape[0]
    s = _dot(k, q, "nt") * SCALE + (cq_row - jnp.tile(ck_rep, (1, n // HEAD_DIM)))
    if on_diagonal:
        s = jnp.where(lax.broadcasted_iota(jnp.int32, (n, n), 0) <= lax.broadcasted_iota(jnp.int32, (n, n), 1), s, NEG_INF)
    return s


def _fox_fwd(qkv, v_t, cq_row, ck_rep):
    T = qkv.shape[0]
    tq = _tile(T, 512)
    nq = T // tq
    steps = nq * (nq + 1) // 2
    HQ, HK = C_FQ // HEAD_DIM, C_FK // HEAD_DIM

    def body(tab, q_ref, k_ref, vt_ref, cq_ref, ck_ref, o_ref, of_ref, lse_ref, m_sc, l_sc, acc_sc):
        i, j = tab[0, pl.program_id(1)], tab[1, pl.program_id(1)]

        @pl.when(j == 0)
        def _():
            m_sc[...] = jnp.full_like(m_sc, NEG_INF)
            l_sc[...] = jnp.zeros_like(l_sc)
            acc_sc[...] = jnp.zeros_like(acc_sc)

        def step(on_diagonal):
            s = _fox_scores_t(k_ref[...], q_ref[...], cq_ref[...], ck_ref[...], on_diagonal)
            m_new = jnp.maximum(m_sc[...], jnp.max(s, axis=0, keepdims=True))
            alpha = jnp.exp(m_sc[...] - m_new)
            p = jnp.exp(s - m_new)
            l_sc[...] = alpha * l_sc[...] + jnp.sum(p, axis=0, keepdims=True)
            acc_sc[...] = alpha * acc_sc[...] + _dot(vt_ref[...], p.astype(BF16), "nn")
            m_sc[...] = m_new

        @pl.when(j < i)
        def _():
            step(False)

        @pl.when(j == i)
        def _():
            step(True)
            o = (acc_sc[...] / l_sc[...]).T
            o_ref[...] = o.astype(BF16)
            of_ref[...] = o
            lse_ref[...] = m_sc[...] + jnp.log(l_sc[...])

    qrow = pl.BlockSpec((None, 1, tq), lambda h, s, tab: (h, 0, tab[0, s]))
    return pl.pallas_call(
        body, name="fox_fwd",
        grid_spec=pltpu.PrefetchScalarGridSpec(
            num_scalar_prefetch=1, grid=(FOX_HEADS, steps),
            in_specs=[pl.BlockSpec((tq, HEAD_DIM), lambda h, s, tab: (tab[0, s], HQ + h)),
                      pl.BlockSpec((tq, HEAD_DIM), lambda h, s, tab: (tab[1, s], HK + h)),
                      pl.BlockSpec((HEAD_DIM, tq), lambda h, s, tab: (h, tab[1, s])), qrow,
                      pl.BlockSpec((None, tq, HEAD_DIM), lambda h, s, tab: (h, tab[1, s], 0))],
            out_specs=[pl.BlockSpec((tq, HEAD_DIM), lambda h, s, tab: (tab[0, s], h)),
                       pl.BlockSpec((tq, HEAD_DIM), lambda h, s, tab: (tab[0, s], h)), qrow],
            scratch_shapes=[pltpu.VMEM((1, tq), F32), pltpu.VMEM((1, tq), F32), pltpu.VMEM((HEAD_DIM, tq), F32)]),
        out_shape=[jax.ShapeDtypeStruct((T, FOX_W), BF16), jax.ShapeDtypeStruct((T, FOX_W), F32),
                   jax.ShapeDtypeStruct((FOX_HEADS, 1, T), F32)],
        compiler_params=_params(("parallel", "arbitrary")),
    )(_triangle(nq, False), qkv, qkv, v_t, cq_row, ck_rep)


def _fox_delta(dmix, out_f32):
    T = out_f32.shape[0]
    tr = _tile(T, 512, 8)

    def body(do_ref, o_ref, d_ref):
        lane = lax.broadcasted_iota(jnp.int32, (tr, HEAD_DIM), 1)
        acc = jnp.zeros((tr, HEAD_DIM), F32)
        for hh in range(FOX_HEADS):
            sl = slice(hh * HEAD_DIM, (hh + 1) * HEAD_DIM)
            d = jnp.sum(do_ref[:, sl].astype(F32) * o_ref[:, sl], axis=-1, keepdims=True)
            acc = jnp.where(lane == hh, d, acc)
        d_ref[...] = acc

    blk = pl.BlockSpec((tr, FOX_W), lambda i: (i, 0))
    return pl.pallas_call(
        body, name="fox_delta", grid=(T // tr,), in_specs=[blk, blk], out_specs=pl.BlockSpec((tr, HEAD_DIM), lambda i: (i, 0)),
        out_shape=jax.ShapeDtypeStruct((T, HEAD_DIM), F32), compiler_params=_params(("parallel",)),
    )(dmix, out_f32)


def _fox_bwd(qkv, k_t, cq_row, ck_rep, delta_row, lse, dmix):
    T = qkv.shape[0]
    tq = _tile(T, 512)
    nq = T // tq
    steps = nq * (nq + 1) // 2
    HQ, HK, HV = C_FQ // HEAD_DIM, C_FK // HEAD_DIM, C_FV // HEAD_DIM

    def body(tab, q_ref, k_ref, kt_ref, v_ref, cq_ref, ck_ref, delta_ref, lse_ref, do_ref,
             dq_ref, dk_ref, dv_ref, dck_ref, dcq_ref, dk_sc, dv_sc, dc_sc, dqt_sc):
        qi, kj = tab[0, pl.program_id(1)], tab[1, pl.program_id(1)]

        @pl.when(qi == kj)
        def _():
            dk_sc[...] = jnp.zeros_like(dk_sc)
            dv_sc[...] = jnp.zeros_like(dv_sc)
            dc_sc[...] = jnp.zeros_like(dc_sc)

        def step(on_diagonal):
            q, k, v, do = q_ref[...], k_ref[...], v_ref[...], do_ref[...]
            p = jnp.exp(_fox_scores_t(k, q, cq_ref[...], ck_ref[...], on_diagonal) - lse_ref[...])
            dp = _dot(v, do, "nt")
            ds = p * (dp - delta_ref[...])
            dsb = ds.astype(BF16)
            dv_sc[...] += _dot(p.astype(BF16), do, "nn")
            dk_sc[...] += _dot(dsb, q, "nn")
            dc_sc[...] += jnp.sum(ds, axis=1, keepdims=True)
            dq_part = _dot(kt_ref[...], dsb, "nn") * SCALE
            dcq_part = jnp.sum(ds, axis=0, keepdims=True)

            @pl.when(kj == 0)
            def _():
                dqt_sc[qi] = dq_part
                dcq_ref[qi] = dcq_part

            @pl.when(kj > 0)
            def _():
                dqt_sc[qi] += dq_part
                dcq_ref[qi] += dcq_part

            if on_diagonal:
                dq_ref[...] = dqt_sc[qi].T

        @pl.when(qi > kj)
        def _():
            step(False)

        @pl.when(qi == kj)
        def _():
            step(True)

        @pl.when(qi == nq - 1)
        def _():
            dk_ref[...] = dk_sc[...] * SCALE
            dv_ref[...] = dv_sc[...]
            dck_ref[...] = -dc_sc[...]

    def rows(base):
        return pl.BlockSpec((tq, HEAD_DIM), lambda h, s, tab: (tab[0, s], base + h))

    def cols(base):
        return pl.BlockSpec((tq, HEAD_DIM), lambda h, s, tab: (tab[1, s], base + h))

    qrow = pl.BlockSpec((None, 1, tq), lambda h, s, tab: (h, 0, tab[0, s]))
    sds = jax.ShapeDtypeStruct((T, FOX_W), F32)
    return pl.pallas_call(
        body, name="fox_bwd",
        grid_spec=pltpu.PrefetchScalarGridSpec(
            num_scalar_prefetch=1, grid=(FOX_HEADS, steps),
            in_specs=[rows(HQ), cols(HK), pl.BlockSpec((HEAD_DIM, tq), lambda h, s, tab: (h, tab[1, s])), cols(HV), qrow,
                      pl.BlockSpec((None, tq, HEAD_DIM), lambda h, s, tab: (h, tab[1, s], 0)), qrow, qrow, rows(0)],
            out_specs=[cols(0), cols(0), cols(0), pl.BlockSpec((None, tq, 1), lambda h, s, tab: (h, tab[1, s], 0)),
                       pl.BlockSpec((None, nq, 1, tq), lambda h, s, tab: (h, 0, 0, 0))],
            scratch_shapes=[pltpu.VMEM((tq, HEAD_DIM), F32), pltpu.VMEM((tq, HEAD_DIM), F32), pltpu.VMEM((tq, 1), F32),
                            pltpu.VMEM((nq, HEAD_DIM, tq), F32)]),
        out_shape=[sds, sds, sds, jax.ShapeDtypeStruct((FOX_HEADS, T, 1), F32), jax.ShapeDtypeStruct((FOX_HEADS, nq, 1, tq), F32)],
        compiler_params=_params(("parallel", "arbitrary")),
    )(_triangle(nq, True), qkv, qkv, k_t, qkv, cq_row, ck_rep, delta_row, lse, dmix)


GW = SWA_GROUP * HEAD_DIM
GR = SWA_GROUP * WINDOW


def _swa_scores(q_ref, kp_ref, kc_ref, slope_ref, n):
    q = q_ref[...]
    qs = jnp.concatenate([q[:, t * HEAD_DIM:(t + 1) * HEAD_DIM] for t in range(SWA_GROUP)], axis=0)
    kb = jnp.concatenate([kp_ref[...], kc_ref[...]], axis=0)
    r = lax.broadcasted_iota(jnp.int32, (GR, 2 * WINDOW), 0) & (WINDOW - 1)
    jj = lax.broadcasted_iota(jnp.int32, (GR, 2 * WINDOW), 1)
    dist = WINDOW + r - jj
    valid = (dist >= 0) & (dist < WINDOW) & ((n > 0) | (jj >= WINDOW))
    s = _dot(qs, kb, "nt") * SCALE - slope_ref[...] * dist.astype(F32)
    return qs, kb, jnp.where(valid, s, NEG_INF), valid


def _swa_specs():
    HQ, HK, HV = C_SQ // GW, C_SK // HEAD_DIM, C_SV // HEAD_DIM
    q_spec = pl.BlockSpec((WINDOW, GW), lambda g, n: (n, HQ + g))

    def prev(base):
        return pl.BlockSpec((WINDOW, HEAD_DIM), lambda g, n: (jnp.maximum(n - 1, 0), base + g))

    def cur(base):
        return pl.BlockSpec((WINDOW, HEAD_DIM), lambda g, n: (n, base + g))

    col = pl.BlockSpec((None, GR, 1), lambda g, n: (g, 0, 0))
    return q_spec, prev(HK), cur(HK), prev(HV), cur(HV), col


def _swa_fwd(qkv, slopes, sinks):
    T = qkv.shape[0]
    nb = T // WINDOW
    assert C_SQ % GW == 0

    def body(q_ref, kp_ref, kc_ref, vp_ref, vc_ref, slope_ref, sink_ref, o_ref, lse_ref):
        n = pl.program_id(1)
        _, _, s, _ = _swa_scores(q_ref, kp_ref, kc_ref, slope_ref, n)
        m = jnp.maximum(jnp.max(s, axis=-1, keepdims=True), sink_ref[...])
        p = jnp.exp(s - m)
        l = jnp.sum(p, axis=-1, keepdims=True) + jnp.exp(sink_ref[...] - m)
        vb = jnp.concatenate([vp_ref[...], vc_ref[...]], axis=0)
        o = _dot(p.astype(BF16), vb, "nn") / l
        for t in range(SWA_GROUP):
            o_ref[:, t * HEAD_DIM:(t + 1) * HEAD_DIM] = o[t * WINDOW:(t + 1) * WINDOW, :].astype(BF16)
        lse_ref[...] = m + jnp.log(l)

    q_spec, kp, kc, vp, vc, col = _swa_specs()
    return pl.pallas_call(
        body, name="swa_fwd", grid=(SWA_KV_HEADS, nb), in_specs=[q_spec, kp, kc, vp, vc, col, col],
        out_specs=[pl.BlockSpec((WINDOW, GW), lambda g, n: (n, g)), pl.BlockSpec((None, None, GR, 1), lambda g, n: (g, n, 0, 0))],
        out_shape=[jax.ShapeDtypeStruct((T, SWA_HEADS * HEAD_DIM), BF16), jax.ShapeDtypeStruct((SWA_KV_HEADS, nb, GR, 1), F32)],
        compiler_params=_params(("parallel", "arbitrary")),
    )(qkv, qkv, qkv, qkv, qkv, slopes, sinks)


def _swa_bwd(qkv, slopes, sinks, out, lse, dmix):
    T = qkv.shape[0]
    nb = T // WINDOW
    DO = FOX_W // GW
    assert FOX_W % GW == 0

    def body(q_ref, kp_ref, kc_ref, vp_ref, vc_ref, slope_ref, sink_ref, o_ref, lse_ref, do_ref,
             dq_ref, dk_ref, dv_ref, dsink_ref, sink_sc):
        n = pl.program_id(1)

        @pl.when(n == 0)
        def _():
            dk_ref[...] = jnp.zeros_like(dk_ref)
            dv_ref[...] = jnp.zeros_like(dv_ref)
            sink_sc[...] = jnp.zeros_like(sink_sc)

        qs, kb, s, valid = _swa_scores(q_ref, kp_ref, kc_ref, slope_ref, n)
        lse = lse_ref[...]
        p = jnp.where(valid, jnp.exp(s - lse), 0.0)
        vb = jnp.concatenate([vp_ref[...], vc_ref[...]], axis=0)
        do = jnp.concatenate([do_ref[:, t * HEAD_DIM:(t + 1) * HEAD_DIM] for t in range(SWA_GROUP)], axis=0)
        oo = jnp.concatenate([o_ref[:, t * HEAD_DIM:(t + 1) * HEAD_DIM] for t in range(SWA_GROUP)], axis=0)
        dp = _dot(do, vb, "nt")
        delta = jnp.sum(do.astype(F32) * oo.astype(F32), axis=-1, keepdims=True)
        ds = p * (dp - delta)
        dsb = ds.astype(BF16)
        dq = _dot(dsb, kb, "nn") * SCALE
        for t in range(SWA_GROUP):
            dq_ref[:, t * HEAD_DIM:(t + 1) * HEAD_DIM] = dq[t * WINDOW:(t + 1) * WINDOW, :]
        dkb = _dot(dsb, qs, "tn") * SCALE
        dvb = _dot(p.astype(BF16), do, "tn")
        r_prev = pl.ds(pl.multiple_of(jnp.maximum(n - 1, 0) * WINDOW, WINDOW), WINDOW)
        r_cur = pl.ds(pl.multiple_of(n * WINDOW, WINDOW), WINDOW)
        dk_ref[r_prev, :] += dkb[:WINDOW, :]
        dk_ref[r_cur, :] += dkb[WINDOW:, :]
        dv_ref[r_prev, :] += dvb[:WINDOW, :]
        dv_ref[r_cur, :] += dvb[WINDOW:, :]
        sink_sc[...] -= jnp.exp(sink_ref[...] - lse) * delta

        @pl.when(n == nb - 1)
        def _():
            tot = [jnp.zeros((1, 128), F32) + jnp.sum(sink_sc[t * WINDOW:(t + 1) * WINDOW, :]) for t in range(SWA_GROUP)]
            dsink_ref[...] = jnp.concatenate(tot + [jnp.zeros((8 - SWA_GROUP, 128), F32)], axis=0)

    q_spec, kp, kc, vp, vc, col = _swa_specs()
    kv_acc = pl.BlockSpec((T, HEAD_DIM), lambda g, n: (0, g))
    return pl.pallas_call(
        body, name="swa_bwd", grid=(SWA_KV_HEADS, nb),
        in_specs=[q_spec, kp, kc, vp, vc, col, col, pl.BlockSpec((WINDOW, GW), lambda g, n: (n, g)),
                  pl.BlockSpec((None, None, GR, 1), lambda g, n: (g, n, 0, 0)), pl.BlockSpec((WINDOW, GW), lambda g, n: (n, DO + g))],
        out_specs=[pl.BlockSpec((WINDOW, GW), lambda g, n: (n, g)), kv_acc, kv_acc, pl.BlockSpec((None, 8, 128), lambda g, n: (g, 0, 0))],
        out_shape=[jax.ShapeDtypeStruct((T, SWA_HEADS * HEAD_DIM), F32), jax.ShapeDtypeStruct((T, SWA_KV_HEADS * HEAD_DIM), F32),
                   jax.ShapeDtypeStruct((T, SWA_KV_HEADS * HEAD_DIM), F32), jax.ShapeDtypeStruct((SWA_KV_HEADS, 8, 128), F32)],
        scratch_shapes=[pltpu.VMEM((GR, 1), F32)],
        compiler_params=_params(("parallel", "arbitrary")),
    )(qkv, qkv, qkv, qkv, qkv, slopes, sinks, out, lse, dmix)


def _mem_fwd(qkv, mk, mv):
    T, ML = qkv.shape[0], mk.shape[0]
    tq = _tile(T, 512)
    HQ = C_MQ // HEAD_DIM

    def body(q_ref, k_ref, v_ref, o_ref, lse_ref):
        s = _dot(q_ref[...], k_ref[...], "nt") * SCALE
        m = jnp.max(s, axis=-1, keepdims=True)
        p = jnp.exp(s - m)
        l = jnp.sum(p, axis=-1, keepdims=True)
        o_ref[...] = (_dot(p.astype(BF16), v_ref[...], "nn") / l).astype(BF16)
        lse_ref[...] = m + jnp.log(l)

    kv = pl.BlockSpec((ML, HEAD_DIM), lambda h, i: (0, h))
    return pl.pallas_call(
        body, name="mem_fwd", grid=(MEM_HEADS, T // tq),
        in_specs=[pl.BlockSpec((tq, HEAD_DIM), lambda h, i: (i, HQ + h)), kv, kv],
        out_specs=[pl.BlockSpec((tq, HEAD_DIM), lambda h, i: (i, h)), pl.BlockSpec((None, tq, 1), lambda h, i: (h, i, 0))],
        out_shape=[jax.ShapeDtypeStruct((T, MEM_HEADS * HEAD_DIM), BF16), jax.ShapeDtypeStruct((MEM_HEADS, T, 1), F32)],
        compiler_params=_params(("parallel", "arbitrary")),
    )(qkv, mk, mv)


def _mem_bwd(qkv, mk, mv, out, lse, dmix):
    T, ML = qkv.shape[0], mk.shape[0]
    tq = _tile(T, 512)
    HQ = C_MQ // HEAD_DIM
    DO = (FOX_W + SWA_HEADS * HEAD_DIM) // HEAD_DIM

    def body(q_ref, k_ref, v_ref, o_ref, lse_ref, do_ref, dq_ref, dk_ref, dv_ref):
        q, k, v, do = q_ref[...], k_ref[...], v_ref[...], do_ref[...]
        p = jnp.exp(_dot(q, k, "nt") * SCALE - lse_ref[...])
        dp = _dot(do, v, "nt")
        delta = jnp.sum(do.astype(F32) * o_ref[...].astype(F32), axis=-1, keepdims=True)
        dsb = (p * (dp - delta)).astype(BF16)
        dq_ref[...] = _dot(dsb, k, "nn") * SCALE
        dk_part = _dot(dsb, q, "tn") * SCALE
        dv_part = _dot(p.astype(BF16), do, "tn")

        @pl.when(pl.program_id(1) == 0)
        def _():
            dk_ref[...] = dk_part
            dv_ref[...] = dv_part

        @pl.when(pl.program_id(1) > 0)
        def _():
            dk_ref[...] += dk_part
            dv_ref[...] += dv_part

    kv = pl.BlockSpec((ML, HEAD_DIM), lambda h, i: (0, h))
    qb = pl.BlockSpec((tq, HEAD_DIM), lambda h, i: (i, h))
    return pl.pallas_call(
        body, name="mem_bwd", grid=(MEM_HEADS, T // tq),
        in_specs=[pl.BlockSpec((tq, HEAD_DIM), lambda h, i: (i, HQ + h)), kv, kv, qb,
                  pl.BlockSpec((None, tq, 1), lambda h, i: (h, i, 0)), pl.BlockSpec((tq, HEAD_DIM), lambda h, i: (i, DO + h))],
        out_specs=[qb, kv, kv],
        out_shape=[jax.ShapeDtypeStruct((T, MEM_HEADS * HEAD_DIM), F32), jax.ShapeDtypeStruct((ML, MEM_HEADS * HEAD_DIM), F32),
                   jax.ShapeDtypeStruct((ML, MEM_HEADS * HEAD_DIM), F32)],
        compiler_params=_params(("parallel", "arbitrary")),
    )(qkv, mk, mv, out, lse, dmix)


HBM = pl.BlockSpec(memory_space=pltpu.HBM)


def _place():
    x, y, c = lax.axis_index("x"), lax.axis_index("y"), lax.axis_index("c")
    chips = [(1 - x, y), (x, 1 - y), (1 - x, 1 - y)]
    return x, y, c, chips


def _remote(src, dst, send_sem, recv_sem, device):
    return pltpu.make_async_remote_copy(src_ref=src, dst_ref=dst, send_sem=send_sem, recv_sem=recv_sem,
                                        device_id=device, device_id_type=MESH)


def _place_ids():
    x, y, c = lax.axis_index("x"), lax.axis_index("y"), lax.axis_index("c")
    order = [2 * x + y, 2 * (1 - x) + y, 2 * x + (1 - y), 2 * (1 - x) + (1 - y)]
    return jnp.stack([2 * x + y, c] + order).astype(jnp.int32)


def _cast_place(name, w, ids, after):
    R, C = w.shape
    tr = _tile(R, 256, 16)

    def body(ids_ref, w_ref, after_ref, o_ref):
        o_ref[...] = w_ref[...].astype(BF16)

    return pl.pallas_call(
        body, name=name,
        grid_spec=pltpu.PrefetchScalarGridSpec(
            num_scalar_prefetch=1, grid=(R // tr,),
            in_specs=[pl.BlockSpec((tr, C), lambda i, ids: (i, 0)), pl.BlockSpec(memory_space=pl.ANY)],
            out_specs=pl.BlockSpec((None, tr, C), lambda i, ids: (ids[0], i, 0))),
        out_shape=jax.ShapeDtypeStruct((N_CHIPS, R, C), BF16), compiler_params=_params(("parallel",)),
    )(ids, w, after)


SEM = pl.BlockSpec(memory_space=pltpu.SEMAPHORE)
EFFECT = pltpu.SideEffectType.DATAFLOW_SIDE_EFFECTING


def _hbm(a):
    return pltpu.with_memory_space_constraint(a, pltpu.HBM)


def _gather_start(name, placed, after):
    n = len(placed)

    ns = 3 * n

    def body(*refs):
        send, recv = refs[n + 1:n + 1 + ns], refs[n + 1 + ns:n + 1 + 2 * ns]
        buf = refs[n + 1 + 2 * ns:2 * n + 1 + 2 * ns]
        token = refs[2 * n + 1 + 2 * ns]
        x, y, c, chips = _place()
        me = 2 * x + y
        for a in range(n):
            half = buf[a].shape[1] // 2
            mine = buf[a].at[me, pl.ds(c * half, half)]
            for j, (cx, cy) in enumerate(chips):
                _remote(mine, mine, send[3 * a + j], recv[3 * a + j], (cx, cy, c)).start()
        token[...] = jnp.zeros_like(token)

    res = pl.pallas_call(
        body, name=name, in_specs=[HBM] * n + [pl.BlockSpec(memory_space=pl.ANY)],
        out_specs=[SEM] * (2 * ns) + [HBM] * n + [pl.BlockSpec(memory_space=pltpu.VMEM)],
        out_shape=[pltpu.SemaphoreType.DMA(())] * (2 * ns)
        + [pltpu.HBM(s.shape, s.dtype) for s in placed] + [jax.ShapeDtypeStruct((8, 128), F32)],
        input_output_aliases={a: 2 * ns + a for a in range(n)},
        compiler_params=pltpu.CompilerParams(has_side_effects=EFFECT),
    )(*[_hbm(s) for s in placed], after)
    return list(res[:ns]), list(res[ns:2 * ns]), list(res[2 * ns:2 * ns + n]), res[2 * ns + n]


def _gather_wait(name, send, recv, bufs, after):
    n = len(bufs)

    ns = 3 * n

    def body(*refs):
        buf = refs[:n]
        send_ref, recv_ref = refs[n:n + ns], refs[n + ns:n + 2 * ns]
        x, y, c, chips = _place()
        ids = [2 * cx + cy for cx, cy in chips]
        for a in range(n):
            half = buf[a].shape[1] // 2
            for j in range(3):
                landed = buf[a].at[ids[j], pl.ds(c * half, half)]
                cp = _remote(landed, landed, send_ref[3 * a + j], recv_ref[3 * a + j], (x, y, c))
                cp.wait_send()
                cp.wait_recv()

    res = pl.pallas_call(
        body, name=name, in_specs=[HBM] * n + [SEM] * (2 * ns) + [pl.BlockSpec(memory_space=pl.ANY)], out_specs=[HBM] * n,
        out_shape=[pltpu.HBM(s.shape, s.dtype) for s in bufs], input_output_aliases={a: a for a in range(n)},
        compiler_params=pltpu.CompilerParams(has_side_effects=EFFECT),
    )(*bufs, *send, *recv, after)
    return list(res)


def _gather_forward(name, bufs):
    n = len(bufs)

    def body(*refs):
        buf = refs[n:2 * n]
        send, recv = refs[2 * n:]
        x, y, c, chips = _place()
        ids = [2 * cx + cy for cx, cy in chips]
        copies = []
        for a in range(n):
            half = buf[a].shape[1] // 2
            for j in range(3):
                landed = buf[a].at[ids[j], pl.ds(c * half, half)]
                cp = _remote(landed, landed, send.at[a, j], recv.at[a, j], (x, y, 1 - c))
                cp.start()
                copies.append(cp)
        for a in range(n):
            half = buf[a].shape[1] // 2
            for j in range(3):
                landed = buf[a].at[ids[j], pl.ds((1 - c) * half, half)]
                _remote(landed, landed, send.at[a, j], recv.at[a, j], (x, y, c)).wait_recv()
        for cp in copies:
            cp.wait_send()

    return pl.pallas_call(
        body, name=name, in_specs=[HBM] * n, out_specs=[HBM] * n,
        out_shape=[jax.ShapeDtypeStruct(s.shape, s.dtype) for s in bufs], input_output_aliases={a: a for a in range(n)},
        scratch_shapes=[pltpu.SemaphoreType.DMA((n, 3)), pltpu.SemaphoreType.DMA((n, 3))],
    )(*bufs)


def _pair_start(name, grads):
    n = len(grads)
    ns = N_CHIPS * n

    def body(*refs):
        send, recv = refs[2 * n:2 * n + ns], refs[2 * n + ns:2 * n + 2 * ns]
        src = refs[2 * n + 2 * ns:3 * n + 2 * ns]
        land = refs[3 * n + 2 * ns:4 * n + 2 * ns]
        token = refs[4 * n + 2 * ns]
        x, y, c, chips = _place()
        order = [2 * x + y] + [2 * cx + cy for cx, cy in chips]
        for a in range(n):
            half = src[a].shape[1] // 2
            for j in range(N_CHIPS):
                _remote(src[a].at[order[j], pl.ds((1 - c) * half, half)], land[a].at[j],
                        send[N_CHIPS * a + j], recv[N_CHIPS * a + j], (x, y, 1 - c)).start()
        token[...] = jnp.zeros_like(token)

    lands = [jax.ShapeDtypeStruct((N_CHIPS, g.shape[1] // 2, g.shape[2]), g.dtype) for g in grads]
    res = pl.pallas_call(
        body, name=name, in_specs=[HBM] * (2 * n),
        out_specs=[SEM] * (2 * ns) + [HBM] * (2 * n) + [pl.BlockSpec(memory_space=pltpu.VMEM)],
        out_shape=[pltpu.SemaphoreType.DMA(())] * (2 * ns) + [pltpu.HBM(g.shape, g.dtype) for g in grads]
        + [pltpu.HBM(l.shape, l.dtype) for l in lands] + [jax.ShapeDtypeStruct((8, 128), F32)],
        input_output_aliases={a: 2 * ns + a for a in range(2 * n)},
        compiler_params=pltpu.CompilerParams(has_side_effects=EFFECT),
    )(*[_hbm(g) for g in grads], *[_hbm(lax.empty(l.shape, l.dtype)) for l in lands])
    return list(res[:ns]), list(res[ns:2 * ns]), list(res[2 * ns:2 * ns + n]), list(res[2 * ns + n:2 * ns + 2 * n]), res[2 * ns + 2 * n]


def _pair_wait(name, send, recv, grads, lands, after):
    n = len(grads)
    ns = N_CHIPS * n

    def body(*refs):
        src, land = refs[:n], refs[n:2 * n]
        send_ref, recv_ref = refs[2 * n:2 * n + ns], refs[2 * n + ns:2 * n + 2 * ns]
        x, y, c, _ = _place()
        for a in range(n):
            for j in range(N_CHIPS):
                cp = _remote(land[a].at[j], land[a].at[j], send_ref[N_CHIPS * a + j], recv_ref[N_CHIPS * a + j], (x, y, c))
                cp.wait_send()
                cp.wait_recv()

    res = pl.pallas_call(
        body, name=name, in_specs=[HBM] * (2 * n) + [SEM] * (2 * ns) + [pl.BlockSpec(memory_space=pl.ANY)],
        out_specs=[HBM] * (2 * n), out_shape=[pltpu.HBM(g.shape, g.dtype) for g in grads] + [pltpu.HBM(l.shape, l.dtype) for l in lands],
        input_output_aliases={a: a for a in range(2 * n)},
        compiler_params=pltpu.CompilerParams(has_side_effects=EFFECT),
    )(*grads, *lands, *send, *recv, after)
    return list(res[:n]), list(res[n:])


def _chip_start(name, parts):
    n = len(parts)
    ns = 3 * n

    def body(*refs):
        send, recv = refs[2 * n:2 * n + ns], refs[2 * n + ns:2 * n + 2 * ns]
        src = refs[2 * n + 2 * ns:3 * n + 2 * ns]
        land = refs[3 * n + 2 * ns:4 * n + 2 * ns]
        token = refs[4 * n + 2 * ns]
        x, y, c, chips = _place()
        for a in range(n):
            for j, (cx, cy) in enumerate(chips):
                _remote(src[a].at[j], land[a].at[j], send[3 * a + j], recv[3 * a + j], (cx, cy, c)).start()
        token[...] = jnp.zeros_like(token)

    res = pl.pallas_call(
        body, name=name, in_specs=[HBM] * (2 * n),
        out_specs=[SEM] * (2 * ns) + [HBM] * (2 * n) + [pl.BlockSpec(memory_space=pltpu.VMEM)],
        out_shape=[pltpu.SemaphoreType.DMA(())] * (2 * ns) + [pltpu.HBM(p.shape, p.dtype) for p in parts] * 2
        + [jax.ShapeDtypeStruct((8, 128), F32)],
        input_output_aliases={a: 2 * ns + a for a in range(2 * n)},
        compiler_params=pltpu.CompilerParams(has_side_effects=EFFECT),
    )(*[_hbm(p) for p in parts], *[_hbm(lax.empty(p.shape, p.dtype)) for p in parts])
    return list(res[:ns]), list(res[ns:2 * ns]), list(res[2 * ns:2 * ns + n]), list(res[2 * ns + n:2 * ns + 2 * n]), res[2 * ns + 2 * n]


def _chip_wait(name, send, recv, parts, lands, after):
    n = len(parts)
    ns = 3 * n

    def body(*refs):
        src, land = refs[:n], refs[n:2 * n]
        send_ref, recv_ref = refs[2 * n:2 * n + ns], refs[2 * n + ns:2 * n + 2 * ns]
        x, y, c, _ = _place()
        for a in range(n):
            for j in range(3):
                cp = _remote(src[a].at[j], land[a].at[j], send_ref[3 * a + j], recv_ref[3 * a + j], (x, y, c))
                cp.wait_send()
                cp.wait_recv()

    res = pl.pallas_call(
        body, name=name, in_specs=[HBM] * (2 * n) + [SEM] * (2 * ns) + [pl.BlockSpec(memory_space=pl.ANY)],
        out_specs=[HBM] * (2 * n), out_shape=[pltpu.HBM(p.shape, p.dtype) for p in parts] * 2,
        input_output_aliases={a: a for a in range(2 * n)},
        compiler_params=pltpu.CompilerParams(has_side_effects=EFFECT),
    )(*parts, *lands, *send, *recv, after)
    return list(res[n:])


def _pair_share(name, shards):
    n = len(shards)

    def body(*refs):
        buf = refs[n:2 * n]
        send, recv = refs[2 * n:]
        x, y, c, _ = _place()
        copies = []
        for a in range(n):
            half = buf[a].shape[0] // 2
            mine = buf[a].at[pl.ds(c * half, half)]
            cp = _remote(mine, mine, send.at[a], recv.at[a], (x, y, 1 - c))
            cp.start()
            copies.append(cp)
        for a, cp in enumerate(copies):
            half = buf[a].shape[0] // 2
            cp.wait_send()
            theirs = buf[a].at[pl.ds((1 - c) * half, half)]
            _remote(theirs, theirs, send.at[a], recv.at[a], (x, y, c)).wait_recv()

    return pl.pallas_call(
        body, name=name, in_specs=[HBM] * n, out_specs=[HBM] * n,
        out_shape=[jax.ShapeDtypeStruct(s.shape, s.dtype) for s in shards], input_output_aliases={a: a for a in range(n)},
        scratch_shapes=[pltpu.SemaphoreType.DMA((n,)), pltpu.SemaphoreType.DMA((n,))],
    )(*shards)


def _small_start(buf):
    R, W = buf.shape
    ns = N_DEV - 1

    def body(*refs):
        send, recv = refs[2:2 + ns], refs[2 + ns:2 + 2 * ns]
        src, land, token = refs[2 + 2 * ns], refs[3 + 2 * ns], refs[4 + 2 * ns]
        x, y, c, _ = _place()
        me = 4 * x + 2 * y + c
        for k in range(1, N_DEV):
            peer = (x ^ (k >> 2), y ^ ((k >> 1) & 1), c ^ (k & 1))
            _remote(src, land.at[me], send[k - 1], recv[k - 1], peer).start()
        token[...] = jnp.zeros_like(token)

    res = pl.pallas_call(
        body, name="small_start", in_specs=[HBM, HBM],
        out_specs=[SEM] * (2 * ns) + [HBM, HBM, pl.BlockSpec(memory_space=pltpu.VMEM)],
        out_shape=[pltpu.SemaphoreType.DMA(())] * (2 * ns) + [pltpu.HBM((R, W), F32), pltpu.HBM((N_DEV, R, W), F32),
                                                                jax.ShapeDtypeStruct((8, 128), F32)],
        input_output_aliases={0: 2 * ns, 1: 2 * ns + 1},
        compiler_params=pltpu.CompilerParams(has_side_effects=EFFECT),
    )(_hbm(buf), _hbm(jnp.zeros((N_DEV, R, W), F32)))
    return list(res[:ns]), list(res[ns:2 * ns]), res[2 * ns], res[2 * ns + 1], res[2 * ns + 2]


def _small_wait(send, recv, buf, land, after):
    ns = N_DEV - 1

    def body(*refs):
        land_ref = refs[1]
        send_ref, recv_ref = refs[2:2 + ns], refs[2 + ns:2 + 2 * ns]
        x, y, c, _ = _place()
        me = 4 * x + 2 * y + c
        for k in range(1, N_DEV):
            landed = land_ref.at[me ^ k]
            cp = _remote(landed, landed, send_ref[k - 1], recv_ref[k - 1], (x, y, c))
            cp.wait_send()
            cp.wait_recv()

    return pl.pallas_call(
        body, name="small_wait", in_specs=[HBM, HBM] + [SEM] * (2 * ns) + [pl.BlockSpec(memory_space=pl.ANY)],
        out_specs=[HBM, HBM], out_shape=[pltpu.HBM(buf.shape, buf.dtype), pltpu.HBM(land.shape, land.dtype)],
        input_output_aliases={0: 0, 1: 1}, compiler_params=pltpu.CompilerParams(has_side_effects=EFFECT),
    )(buf, land, *send, *recv, after)


def _small_sum(buf, land):
    def body(buf_ref, land_ref, out_ref):
        x, y, c, _ = _place()
        me = 4 * x + 2 * y + c
        total = None
        for d in range(N_DEV):
            term = jnp.where(me == d, buf_ref[...], land_ref[d])
            total = term if total is None else total + term
        out_ref[...] = total

    return pl.pallas_call(body, name="small_sum", out_shape=jax.ShapeDtypeStruct(buf.shape, F32))(buf, land)


def _pair_sum_bf16(name, grad, theirs, ids):
    _, R2, C = theirs.shape
    tr = _tile(R2, 256, 16)
    nrb = R2 // tr

    def body(ids_ref, a_ref, b_ref, o_ref):
        o_ref[...] = (a_ref[...] + b_ref[...]).astype(BF16)

    return pl.pallas_call(
        body, name=name,
        grid_spec=pltpu.PrefetchScalarGridSpec(
            num_scalar_prefetch=1, grid=(3, nrb),
            in_specs=[pl.BlockSpec((None, tr, C), lambda j, i, ids: (ids[3 + j], ids[1] * nrb + i, 0)),
                      pl.BlockSpec((None, tr, C), lambda j, i, ids: (j + 1, i, 0))],
            out_specs=pl.BlockSpec((None, tr, C), lambda j, i, ids: (j, i, 0))),
        out_shape=jax.ShapeDtypeStruct((3, R2, C), BF16), compiler_params=_params(("parallel", "parallel")),
    )(ids, grad, theirs)


def _chip_sum(name, grad, theirs, arrived, ids):
    _, R2, C = theirs.shape
    tr = _tile(R2, 256, 16)
    nrb = R2 // tr

    def body(ids_ref, a_ref, b_ref, r_ref, o_ref):
        tot = a_ref[...] + b_ref[...]
        for j in range(3):
            tot = tot + r_ref[j].astype(F32)
        o_ref[...] = tot

    return pl.pallas_call(
        body, name=name,
        grid_spec=pltpu.PrefetchScalarGridSpec(
            num_scalar_prefetch=1, grid=(nrb,),
            in_specs=[pl.BlockSpec((None, tr, C), lambda i, ids: (ids[0], ids[1] * nrb + i, 0)),
                      pl.BlockSpec((None, tr, C), lambda i, ids: (0, i, 0)),
                      pl.BlockSpec((3, tr, C), lambda i, ids: (0, i, 0))],
            out_specs=pl.BlockSpec((tr, C), lambda i, ids: (ids[1] * nrb + i, 0))),
        out_shape=jax.ShapeDtypeStruct((2 * R2, C), F32), compiler_params=_params(("parallel",)),
    )(ids, grad, theirs, arrived)


def _adamw(name, w, g, m, v, emit_grad=False):
    R, C = w.shape
    tr = _tile(R, 128, 8)
    c1 = 1.0 / (1.0 - ADAM_B1 ** ADAM_STEP)
    c2 = 1.0 / (1.0 - ADAM_B2 ** ADAM_STEP)
    n_out = 4 if emit_grad else 3

    def body(w_ref, g_ref, m_ref, v_ref, d_ref, mo_ref, vo_ref, *rest):
        gv = g_ref[...]
        mn = ADAM_B1 * m_ref[...] + (1.0 - ADAM_B1) * gv
        vn = ADAM_B2 * v_ref[...] + (1.0 - ADAM_B2) * (gv * gv)
        d_ref[...] = -ADAM_LR * ((mn * c1) / (jnp.sqrt(vn * c2) + ADAM_EPS) + ADAM_WD * w_ref[...])
        mo_ref[...] = mn
        vo_ref[...] = vn
        if emit_grad:
            rest[0][...] = gv

    spec = pl.BlockSpec((tr, C), lambda i: (i, 0))
    sds = jax.ShapeDtypeStruct((R, C), F32)
    return pl.pallas_call(body, name=name, grid=(R // tr,), in_specs=[spec] * 4, out_specs=[spec] * n_out, out_shape=[sds] * n_out,
                          compiler_params=_params(("parallel",)))(w, g, m, v)


SMALL = ["ffn1_norm", "mix_norm", "mem_norm", "forget_bias", "fox_q_gain", "fox_k_gain", "swa_q_gain", "swa_k_gain", "swa_sinks",
         "mem_q_gain", "mem_k_gain", "ffn2_norm"]
LARGE = ["ffn1_gate", "ffn1_up", "ffn1_down", "w_in", "w_mem_k", "w_mem_v", "w_out", "ffn2_gate", "ffn2_up", "ffn2_down"]
GATHER_GROUPS = [["ffn1_gate", "ffn1_up"], ["ffn1_down", "w_in", "w_mem_k", "w_mem_v"], ["w_out", "ffn2_gate", "ffn2_up", "ffn2_down"]]
WEIGHTS = ["ffn1_norm", "ffn1_gate", "ffn1_up", "ffn1_down", "mix_norm", "mem_norm", "w_in", "forget_bias", "w_mem_k", "w_mem_v",
           "fox_q_gain", "fox_k_gain", "swa_q_gain", "swa_k_gain", "swa_sinks", "mem_q_gain", "mem_k_gain", "w_out", "ffn2_norm",
           "ffn2_gate", "ffn2_up", "ffn2_down"]


def _pad_proj_cols(w):
    out = jnp.zeros((w.shape[0], PROJ_W), w.dtype)
    for start, width, pstart in REF_GROUPS:
        out = lax.dynamic_update_slice(out, w[:, start:start + width], (0, pstart))
    return out


def _unpad_proj_cols(w):
    return jnp.concatenate([w[:, pstart:pstart + width] for _, width, pstart in REF_GROUPS], axis=1)


def _pack_small(vals):
    flat = jnp.concatenate([vals[k].reshape(-1).astype(F32) for k in SMALL + ["loss"]])
    n = flat.shape[0]
    total = -(-n // 1024) * 1024
    return jnp.pad(flat, (0, total - n)).reshape(total // 128, 128)


def _unpack_small(buf, shapes):
    flat = buf.reshape(-1)
    out, off = {}, 0
    for k in SMALL + ["loss"]:
        size = int(np.prod(shapes[k]))
        out[k] = flat[off:off + size].reshape(shapes[k])
        off += size
    return out


def kernel(x, mem, ffn1_norm, ffn1_gate, ffn1_up, ffn1_down, mix_norm, mem_norm, w_in, forget_bias, w_mem_k, w_mem_v, fox_q_gain, fox_k_gain, swa_q_gain, swa_k_gain, swa_sinks, mem_q_gain, mem_k_gain, w_out, ffn2_norm, ffn2_gate, ffn2_up, ffn2_down, loss_target, m_ffn1_norm, m_ffn1_gate, m_ffn1_up, m_ffn1_down, m_mix_norm, m_mem_norm, m_w_in, m_forget_bias, m_w_mem_k, m_w_mem_v, m_fox_q_gain, m_fox_k_gain, m_swa_q_gain, m_swa_k_gain, m_swa_sinks, m_mem_q_gain, m_mem_k_gain, m_w_out, m_ffn2_norm, m_ffn2_gate, m_ffn2_up, m_ffn2_down, v_ffn1_norm, v_ffn1_gate, v_ffn1_up, v_ffn1_down, v_mix_norm, v_mem_norm, v_w_in, v_forget_bias, v_w_mem_k, v_w_mem_v, v_fox_q_gain, v_fox_k_gain, v_swa_q_gain, v_swa_k_gain, v_swa_sinks, v_mem_q_gain, v_mem_k_gain, v_w_out, v_ffn2_norm, v_ffn2_gate, v_ffn2_up, v_ffn2_down):
    given = dict(locals())
    T, D = x.shape[1], x.shape[2]
    ML = mem.shape[1]
    xin = x.reshape(T, D)
    target = loss_target.reshape(T, D)
    memin = mem.reshape(ML, D)

    ids = _place_ids()
    shard = {k: given[k][0] for k in LARGE}
    started, after = [], ids
    for gi, group in enumerate(GATHER_GROUPS):
        if "w_in" in group:
            shard["w_in"] = _pad_proj_cols(shard["w_in"] + after[0, 0])
        placed = [_cast_place("cast_" + k, shard[k], ids, after) for k in group]
        send, recv, bufs, after = _gather_start("gather_start_%d" % gi, placed, after)
        started.append((send, recv, bufs))

    def arrive(gi, done):
        send, recv, bufs = started[gi]
        bufs = _gather_wait("gather_wait_%d" % gi, send, recv, bufs, done)
        return dict(zip(GATHER_GROUPS[gi], _gather_forward("gather_forward_%d" % gi, bufs)))

    gains = jnp.concatenate([fox_q_gain, fox_k_gain, swa_q_gain, swa_k_gain, mem_q_gain,
                             jnp.pad(forget_bias, ((0, 0), (0, HEAD_DIM - FOX_HEADS))), jnp.zeros((2, HEAD_DIM), F32)], axis=0)
    slopes_np = 2.0 ** (-8.0 * np.arange(1, SWA_HEADS + 1) / SWA_HEADS)
    slopes = jnp.asarray(np.repeat(slopes_np, WINDOW).reshape(SWA_KV_HEADS, GR, 1), F32)
    sinks = jnp.repeat(swa_sinks.reshape(SWA_HEADS), WINDOW).reshape(SWA_KV_HEADS, GR, 1)

    h1 = _rms_fwd("ffn1_norm_fwd", xin, ffn1_norm + after[0, 0])
    full = arrive(0, h1)
    wg1, wu1 = full["ffn1_gate"], full["ffn1_up"]
    fg1, fu1, a1 = _ffn_gu("ffn1_gate_up", h1, wg1, wu1)
    full = arrive(1, a1)
    wd1 = full["ffn1_down"].reshape(-1, D)
    win = full["w_in"].reshape(D, PROJ_W)
    wmk = full["w_mem_k"].reshape(D, MEM_HEADS * HEAD_DIM)
    wmv = full["w_mem_v"].reshape(D, MEM_HEADS * HEAD_DIM)
    x1 = _ffn_down("ffn1_down", a1, wd1, xin)
    h2 = _rms_fwd("mix_norm_fwd", x1, mix_norm)
    proj = _mm2d("proj_in", h2, win, "nn", F32, tn=1408, tk=2048, n_outer=True)
    qkv, logf, k_t, v_t = _prep_fwd(proj, gains)
    cum = _cumsum_rows("forget_cumsum", [logf], False)
    cum_h = cum[:, :FOX_HEADS].T
    cq_row = cum_h.reshape(FOX_HEADS, 1, T)
    ck_rep = jnp.broadcast_to(cum_h[:, :, None], (FOX_HEADS, T, HEAD_DIM))
    mn = _rms_fwd("mem_norm_fwd", memin, mem_norm)
    mk_raw = _mm2d("mem_k_proj", mn, wmk, "nn", F32)
    mv = _mm2d("mem_v_proj", mn, wmv, "nn", BF16)
    mk = _head_norm_rows(mk_raw, mem_k_gain)
    out_a, out_a_f32, lse_a = _fox_fwd(qkv, v_t, cq_row, ck_rep)
    out_b, lse_b = _swa_fwd(qkv, slopes, sinks)
    out_c, lse_c = _mem_fwd(qkv, mk, mv)
    mixed = jnp.concatenate([out_a, out_b, out_c], axis=1)
    full = arrive(2, mixed)
    wo = full["w_out"].reshape(-1, D)
    wg2, wu2, wd2 = full["ffn2_gate"], full["ffn2_up"], full["ffn2_down"].reshape(-1, D)
    x2 = _mm2d("mix_out", mixed, wo, "nn", F32, tk=2048, n_outer=True, extras=[x1], epilogue=lambda accs, ex: [ex[0] + accs[0]])
    h3 = _rms_fwd("ffn2_norm_fwd", x2, ffn2_norm)
    fg2, fu2, a2 = _ffn_gu("ffn2_gate_up", h3, wg2, wu2)
    x3 = _ffn_down("ffn2_down", a2, wd2, x2)
    dx3, dyb3, loss_part = _loss_head(x3, target)

    grads, small, res = {}, {"loss": loss_part[0, 0]}, {}

    def pair_off(tag, group):
        send, recv, own, lands, token = _pair_start("grad_pair_start_" + tag, [grads[k] for k in group])
        return (group, send, recv, own, lands), token

    def chip_off(tag, started, done):
        group, send, recv, own, lands = started
        own, theirs = _pair_wait("grad_pair_wait_" + tag, send, recv, own, lands, done)
        grads.update(zip(group, own))
        to_chips = [_pair_sum_bf16("pair_sum_" + k, grads[k], b, ids) for k, b in zip(group, theirs)]
        send, recv, parts, lands, token = _chip_start("grad_chip_start_" + tag, to_chips)
        return (group, theirs, send, recv, parts, lands), token

    def finish(tag, state, done):
        group, theirs, send, recv, parts, lands = state
        arrived = _chip_wait("grad_chip_wait_" + tag, send, recv, parts, lands, done)
        halves = [_chip_sum("chip_sum_" + k, grads[k], b, r, ids) for k, b, r in zip(group, theirs, arrived)]
        reduced = dict(zip(group, _pair_share("grad_pair_share_" + tag, halves)))
        last = None
        for k in group:
            if k == "w_in":
                gk = _unpad_proj_cols(reduced[k])
                d, mo, vo = _adamw("adamw_" + k, given[k][0], gk, given["m_" + k][0], given["v_" + k][0])
            else:
                d, mo, vo, gk = _adamw("adamw_" + k, given[k][0], reduced[k], given["m_" + k][0], given["v_" + k][0], emit_grad=True)
            res[k] = tuple(t[None] for t in (gk, d, mo, vo))
            last = vo
        return last

    dg2, du2 = _ffn_bwd_act("ffn2", dyb3, wd2, fg2, fu2, N_CHIPS)
    grads["ffn2_down"] = _ffn_bwd_down("ffn2", a2, dyb3, N_CHIPS).reshape(N_CHIPS, -1, D)
    grads["ffn2_gate"], grads["ffn2_up"] = _ffn_bwd_gate_up("ffn2", h3, dg2, du2, N_CHIPS)
    started, token = pair_off("a", ["ffn2_gate", "ffn2_up", "ffn2_down"])
    dh3 = _ffn_bwd_x("ffn2", dg2, du2, wg2, wu2, token)
    state_a, token = chip_off("a", started, dh3)
    dx2, dx2b, small["ffn2_norm"] = _rms_bwd("ffn2_norm_bwd", dh3, x2, ffn2_norm + token[0, 0], dx3, 1.0)
    dmix = _mm2d("mix_out_dx", dx2b, wo, "nt", BF16, tk=2048, n_outer=True)
    grads["w_out"] = _mm2d("mix_out_dw", mixed, dx2b, "tn", F32, tk=T, n_outer=True, resident=True).reshape(N_CHIPS, -1, D)
    delta_row = _fox_delta(dmix, out_a_f32)[:, :FOX_HEADS].T.reshape(FOX_HEADS, 1, T)
    dfq, dfk, dfv, dck, dcq = _fox_bwd(qkv, k_t, cq_row, ck_rep, delta_row, lse_a, dmix)
    dsq, dsk, dsv, dsink = _swa_bwd(qkv, slopes, sinks, out_b, lse_b, dmix)
    dmq, dmk, dmv = _mem_bwd(qkv, mk, mv, out_c, lse_c, dmix)
    small["swa_sinks"] = dsink[:, :SWA_GROUP, 0].reshape(1, SWA_HEADS)
    dcum = jnp.pad(dcq.reshape(FOX_HEADS, T).T, ((0, 0), (0, HEAD_DIM - FOX_HEADS)))
    dlogf = _cumsum_rows("forget_cumsum_bwd", [dcum], True, columns=dck)
    dproj, dgains = _prep_bwd(proj, gains, dfq, dfk, dfv, dsq, dsk, dsv, dmq, dlogf)
    for row, k in enumerate(["fox_q_gain", "fox_k_gain", "swa_q_gain", "swa_k_gain", "mem_q_gain"]):
        small[k] = dgains[row:row + 1, :]
    small["forget_bias"] = dgains[5:6, :FOX_HEADS]
    grads["w_in"] = _mm2d("proj_in_dw", h2, dproj, "tn", F32, tn=1408, tk=T, n_outer=True, resident=True).reshape(N_CHIPS, -1, PROJ_W)
    dmk_raw, small["mem_k_gain"] = _head_norm_rows_bwd(mk_raw, mem_k_gain, dmk)
    dmvb = dmv.astype(BF16)
    grads["w_mem_k"] = _mm2d("mem_k_dw", mn, dmk_raw, "tn", F32).reshape(N_CHIPS, -1, MEM_HEADS * HEAD_DIM)
    grads["w_mem_v"] = _mm2d("mem_v_dw", mn, dmvb, "tn", F32).reshape(N_CHIPS, -1, MEM_HEADS * HEAD_DIM)
    dmn = _mm2d("mem_k_dx", dmk_raw, wmk, "nt", F32)
    dmn = _mm2d("mem_v_dx", dmvb, wmv, "nt", F32, extras=[dmn], epilogue=lambda accs, ex: [ex[0] + accs[0]])
    _, _, small["mem_norm"] = _rms_bwd("mem_norm_bwd", dmn, memin, mem_norm, jnp.zeros_like(memin), 1.0)
    started, token = pair_off("b", ["w_out", "w_in", "w_mem_k", "w_mem_v"])
    dh2 = _mm2d("proj_in_dx", dproj, win, "nt", F32, tm=1024, tk=1408, after=token)
    state_b, token = chip_off("b", started, dh2)
    dx1, dyb1, small["mix_norm"] = _rms_bwd("mix_norm_bwd", dh2, x1, mix_norm + token[0, 0], dx2, 0.5)
    grads["ffn1_down"] = _ffn_bwd_down("ffn1", a1, dyb1, N_CHIPS).reshape(N_CHIPS, -1, D)
    started, token = pair_off("c", ["ffn1_down"])
    dg1, du1 = _ffn_bwd_act("ffn1", dyb1, wd1, fg1, fu1, N_CHIPS, after=token)
    state_c, token = chip_off("c", started, dg1)
    grads["ffn1_gate"], grads["ffn1_up"] = _ffn_bwd_gate_up("ffn1", h1, dg1, du1, N_CHIPS, after=token)
    started, token = pair_off("d", ["ffn1_gate", "ffn1_up"])
    dh1 = _ffn_bwd_x("ffn1", dg1, du1, wg1, wu1, token)
    state_d, token = chip_off("d", started, dh1)
    grad_x, _, small["ffn1_norm"] = _rms_bwd("ffn1_norm_bwd", dh1, xin, ffn1_norm + token[0, 0], dx1, 1.0)

    s_send, s_recv, s_buf, s_land, token = _small_start(_pack_small(small))

    done = finish("a", state_a, token)
    done = finish("b", state_b, done)
    done = finish("c", state_c, done)
    done = finish("d", state_d, done)

    shapes = {k: given[k].shape for k in SMALL}
    shapes["loss"] = ()
    s_buf, s_land = _small_wait(s_send, s_recv, s_buf, s_land, done)
    red_small = _unpack_small(_small_sum(s_buf, s_land), shapes)
    loss = red_small["loss"]
    zero = {"loss": jnp.zeros((), F32)}
    packed = [_pack_small({**zero, **{k: src[k] for k in SMALL}}) for src in (
        {k: given[k] for k in SMALL}, red_small, {k: given["m_" + k] for k in SMALL}, {k: given["v_" + k] for k in SMALL})]
    d_s, m_s, v_s = (_unpack_small(t, shapes) for t in _adamw("adamw_small", *packed))
    for k in SMALL:
        res[k] = (red_small[k], d_s[k], m_s[k], v_s[k])

    outs = [loss, grad_x.reshape(1, T, D)]
    for part in range(4):
        outs += [res[k][part] for k in WEIGHTS]
    return tuple(outs)
```

```python
import functools

import numpy as np
import jax
import jax.numpy as jnp
from jax import lax
from jax.experimental import pallas as pl
from jax.experimental.pallas import tpu as pltpu

F32 = jnp.float32
BF16 = jnp.bfloat16
MESH = pl.DeviceIdType.MESH

HEAD_DIM = 128
FOX_HEADS = 6
SWA_HEADS = 6
SWA_KV_HEADS = 2
SWA_GROUP = SWA_HEADS // SWA_KV_HEADS
MEM_HEADS = 4
WINDOW = 128
EPS = 1e-6
NEG_INF = -1e30
SCALE = HEAD_DIM ** -0.5

C_FQ = 0
C_FK = C_FQ + FOX_HEADS * HEAD_DIM
C_FV = C_FK + FOX_HEADS * HEAD_DIM
C_SQ = C_FV + FOX_HEADS * HEAD_DIM
C_SK = C_SQ + SWA_HEADS * HEAD_DIM
C_SV = C_SK + SWA_KV_HEADS * HEAD_DIM
C_MQ = C_SV + SWA_KV_HEADS * HEAD_DIM
C_FL = C_MQ + MEM_HEADS * HEAD_DIM
PROJ_W = C_FL + HEAD_DIM
FOX_W = FOX_HEADS * HEAD_DIM
REF_GROUPS = [
    (0, FOX_W, C_FQ), (FOX_W, FOX_W, C_FK), (2 * FOX_W, FOX_W, C_FV), (3 * FOX_W, FOX_HEADS, C_FL),
    (3 * FOX_W + FOX_HEADS, SWA_HEADS * HEAD_DIM, C_SQ),
    (3 * FOX_W + FOX_HEADS + SWA_HEADS * HEAD_DIM, SWA_KV_HEADS * HEAD_DIM, C_SK),
    (3 * FOX_W + FOX_HEADS + (SWA_HEADS + SWA_KV_HEADS) * HEAD_DIM, SWA_KV_HEADS * HEAD_DIM, C_SV),
    (3 * FOX_W + FOX_HEADS + (SWA_HEADS + 2 * SWA_KV_HEADS) * HEAD_DIM, MEM_HEADS * HEAD_DIM, C_MQ),
]

ADAM_LR = 0.001
ADAM_B1 = 0.9
ADAM_B2 = 0.999
ADAM_EPS = 1e-08
ADAM_WD = 0.01
ADAM_STEP = 10

V7X_VMEM_LIMIT = 56 * 1024 * 1024
N_CHIPS = 4
N_DEV = 8


def _tile(n, pref, mult=128):
    t = (min(pref, n) // mult) * mult
    while t >= mult:
        if n % t == 0:
            return t
        t -= mult
    return n


def _params(sem):
    return pltpu.CompilerParams(dimension_semantics=sem, vmem_limit_bytes=V7X_VMEM_LIMIT)


_DIMS = {"nn": (((1,), (0,)), ((), ())), "nt": (((1,), (1,)), ((), ())), "tn": (((0,), (0,)), ((), ()))}


def _dot(a, b, mode):
    return lax.dot_general(a, b, _DIMS[mode], preferred_element_type=F32)


def _mm(name, grid, pairs, acc_of, acc_shapes, extras, outs, epilogue, after=None):
    n_p, n_e, n_o, n_a = len(pairs), len(extras), len(outs), len(acc_shapes)
    n_w = 0 if after is None else 1
    nk = grid[2]
    n_in = sum(1 if a is None else 2 for a, *_ in pairs)

    def body(*refs):
        ex = refs[n_in:n_in + n_e]
        out = refs[n_in + n_e + n_w:n_in + n_e + n_w + n_o]
        accs = refs[n_in + n_e + n_w + n_o:]
        parts = [None] * n_a
        at = 0
        for p in range(n_p):
            if pairs[p][0] is None:
                a_ref, b_ref = refs[0], refs[at]
                at += 1
            else:
                a_ref, b_ref = refs[at], refs[at + 1]
                at += 2
            d = _dot(a_ref[...], b_ref[...], pairs[p][4])
            parts[acc_of[p]] = d if parts[acc_of[p]] is None else parts[acc_of[p]] + d

        def finish(vals):
            for o, r in zip(out, epilogue(vals, [e[...] for e in ex])):
                o[...] = r.astype(o.dtype)

        if nk == 1:
            finish(parts)
            return
        k = pl.program_id(2)

        @pl.when(k == 0)
        def _():
            for a, d in zip(accs, parts):
                a[...] = d

        @pl.when((k > 0) & (k < nk - 1))
        def _():
            for a, d in zip(accs, parts):
                a[...] += d

        @pl.when(k == nk - 1)
        def _():
            finish([a[...] + d for a, d in zip(accs, parts)])

    in_specs, args = [], []
    for a, a_spec, b, b_spec, _ in pairs:
        if a is not None:
            in_specs.append(a_spec)
            args.append(a)
        in_specs.append(b_spec)
        args.append(b)
    for e, e_spec in extras:
        in_specs.append(e_spec)
        args.append(e)
    if after is not None:
        in_specs.append(pl.BlockSpec(memory_space=pl.ANY))
        args.append(after)
    res = pl.pallas_call(
        body, name=name, grid=grid, in_specs=in_specs,
        out_specs=[s for _, s in outs], out_shape=[o for o, _ in outs],
        scratch_shapes=[pltpu.VMEM(s, F32) for s in acc_shapes] if nk > 1 else [],
        compiler_params=_params(("parallel", "parallel", "arbitrary")),
    )(*args)
    return res


def _mm2d(name, a, b, mode, out_dtype, tm=512, tn=1024, tk=1024, extras=(), epilogue=None, n_out=1, after=None, n_outer=False,
          resident=False, rows=(), tile_stat=False):
    if mode == "nn":
        (M, K), N = a.shape, b.shape[1]
    elif mode == "nt":
        (M, K), N = a.shape, b.shape[0]
    else:
        (K, M), N = a.shape, b.shape[1]
    tm, tn, tk = _tile(M, tm), _tile(N, tn), _tile(K, tk)
    assert not resident or tk == K

    def spec(shape, index, single=False):
        mode_kw = {"pipeline_mode": pl.Buffered(1)} if single else {}
        if n_outer:
            return pl.BlockSpec(shape, lambda j, i, k: index(i, j, k), **mode_kw)
        return pl.BlockSpec(shape, index, **mode_kw)

    single_a, single_b = resident and not n_outer, resident and n_outer
    a_spec = spec((tk, tm), lambda i, j, k: (k, i), single_a) if mode == "tn" else spec((tm, tk), lambda i, j, k: (i, k), single_a)
    b_spec = spec((tn, tk), lambda i, j, k: (j, k), single_b) if mode == "nt" else spec((tk, tn), lambda i, j, k: (k, j), single_b)
    mn = spec((tm, tn), lambda i, j, k: (i, j))
    if epilogue is None:
        epilogue = lambda accs, ex: [accs[0]]
    if not isinstance(out_dtype, (list, tuple)):
        out_dtype = [out_dtype] * n_out
    grid = (N // tn, M // tm, K // tk) if n_outer else (M // tm, N // tn, K // tk)
    outs = [(jax.ShapeDtypeStruct((M, N), d), mn) for d in out_dtype]
    if tile_stat:
        assert tn == N
        outs.append((jax.ShapeDtypeStruct((8 * (M // tm), 128), F32), spec((8, 128), lambda i, j, k: (i, 0))))
    res = _mm(name, grid, [(a, a_spec, b, b_spec, mode)], [0], [(tm, tn)],
              [(e, mn) for e in extras] + [(r, spec((1, tn), lambda i, j, k: (0, j))) for r in rows], outs, epilogue, after=after)
    return res[0] if len(res) == 1 else res


def _sigmoid(x):
    return 1.0 / (1.0 + jnp.exp(-x))


def _sigmoid_fast(x):
    return pl.reciprocal(1.0 + jnp.exp(-x), approx=True)


def _ffn_gu(name, h, wg, wu):
    T, D = h.shape
    nf, _, F4 = wg.shape
    tm, tk = _tile(T, 512), _tile(D, 2048)
    a_spec = pl.BlockSpec((tm, tk), lambda j, i, k: (i, k))
    b_spec = pl.BlockSpec((None, tk, F4), lambda j, i, k: (j, k, 0))
    o_spec = pl.BlockSpec((tm, F4), lambda j, i, k: (i, j))

    def epilogue(accs, ex):
        g, u = accs
        s = _sigmoid_fast(g)
        gs = g * s
        return [(s + s * (g - gs)) * u, gs, gs * u]

    sds = jax.ShapeDtypeStruct((T, nf * F4), BF16)
    return _mm(name, (nf, T // tm, D // tk), [(h, a_spec, wg, b_spec, "nn"), (None, None, wu, b_spec, "nn")], [0, 1],
               [(tm, F4), (tm, F4)], [], [(sds, o_spec)] * 3, epilogue)


def _rms_rows(x, gain):
    return x * lax.rsqrt(jnp.mean(x * x, axis=-1, keepdims=True) + EPS) * gain


def _residual_norm(name, a, w, xres, scale, gain, tm):
    def epilogue(accs, ex):
        y = ex[0] + scale * accs[0]
        return [y, _rms_rows(y, ex[1])]

    return _mm2d(name, a, w, "nn", [F32, BF16], tm=tm, tn=w.shape[1], tk=w.shape[0], n_outer=True, resident=True, extras=[xres],
                 rows=[gain], epilogue=epilogue)


def _ffn_down_loss(name, a, wd, xres, target):
    D = wd.shape[1]

    def epilogue(accs, ex):
        e = ex[0] + 0.5 * accs[0] - ex[1]
        d = e * (1.0 / D)
        return [d, 0.5 * d, jnp.zeros((8, 128), F32) + (0.5 / D) * jnp.sum(e * e)]

    return _mm2d(name, a, wd, "nn", [F32, BF16], tm=256, tn=D, tk=wd.shape[0], n_outer=True, resident=True, extras=[xres, target],
                 tile_stat=True, epilogue=epilogue)


def _ffn_bwd_down(tag, a, dyb, nf, after=None):
    return _mm2d(tag + "_dwd", a, dyb, "tn", F32, tm=a.shape[1] // nf, tn=1024, tk=a.shape[0], resident=True, after=after)


def _ffn_bwd_act(tag, dyb, wd, da_dg, da_du, nf, after=None):
    def act_bwd(accs, ex):
        return [accs[0] * ex[0].astype(F32), accs[0] * ex[1].astype(F32)]

    return _mm2d(tag + "_da", dyb, wd, "nt", BF16, tn=wd.shape[0] // nf, tk=2048, extras=[da_dg, da_du], epilogue=act_bwd, n_out=2,
                 n_outer=True, after=after)


def _ffn_bwd_gate_up(tag, h, dg, du, nf, after=None):
    T, D = h.shape
    F4 = dg.shape[1] // nf
    tm = _tile(D, 512)
    h_spec = pl.BlockSpec((T, tm), lambda j, i, k: (0, i))
    d_spec = pl.BlockSpec((T, F4), lambda j, i, k: (0, j), pipeline_mode=pl.Buffered(1))
    w_spec = pl.BlockSpec((None, tm, F4), lambda j, i, k: (j, i, 0))
    sds = jax.ShapeDtypeStruct((nf, D, F4), F32)
    return _mm(tag + "_dwgu", (nf, D // tm, 1), [(h, h_spec, dg, d_spec, "tn"), (None, None, du, d_spec, "tn")],
               [0, 1], [(tm, F4), (tm, F4)], [], [(sds, w_spec)] * 2, lambda accs, ex: accs, after=after)


def _ffn_bwd_x(tag, dg, du, wg, wu, after):
    T = dg.shape[0]
    nf, D, F4 = wg.shape
    tm, tn = _tile(T, 1024), _tile(D, 1024)
    a_spec = pl.BlockSpec((tm, F4), lambda i, j, k: (i, k))
    b_spec = pl.BlockSpec((None, tn, F4), lambda i, j, k: (k, j, 0))
    o_spec = pl.BlockSpec((tm, tn), lambda i, j, k: (i, j))
    (dh,) = _mm(tag + "_dh", (T // tm, D // tn, nf), [(dg, a_spec, wg, b_spec, "nt"), (du, a_spec, wu, b_spec, "nt")],
                [0, 0], [(tm, tn)], [], [(jax.ShapeDtypeStruct((T, D), F32), o_spec)], lambda accs, ex: accs, after=after)
    return dh


def _rms_fwd(name, x, gain):
    R, D = x.shape
    tr = _tile(R, 256, 8)

    def body(x_ref, g_ref, o_ref):
        xv = x_ref[...]
        r = lax.rsqrt(jnp.mean(xv * xv, axis=-1, keepdims=True) + EPS)
        o_ref[...] = (xv * r * g_ref[...]).astype(BF16)

    return pl.pallas_call(
        body, name=name, grid=(R // tr,),
        in_specs=[pl.BlockSpec((tr, D), lambda i: (i, 0)), pl.BlockSpec((1, D), lambda i: (0, 0))],
        out_specs=pl.BlockSpec((tr, D), lambda i: (i, 0)), out_shape=jax.ShapeDtypeStruct((R, D), BF16),
        compiler_params=_params(("parallel",)),
    )(x, gain)


def _rms_bwd(name, dh, x, gain, dres, bscale):
    R, D = x.shape
    tr = _tile(R, 256, 8)

    def body(dh_ref, x_ref, g_ref, dres_ref, dx_ref, dxb_ref, dg_ref):
        xv, dy = x_ref[...], dh_ref[...]
        r = lax.rsqrt(jnp.mean(xv * xv, axis=-1, keepdims=True) + EPS)
        xn = xv * r
        uu = dy * g_ref[...]
        dx = dres_ref[...] + r * (uu - xn * jnp.mean(xn * uu, axis=-1, keepdims=True))
        dx_ref[...] = dx
        dxb_ref[...] = (bscale * dx).astype(BF16)
        part = jnp.sum(dy * xn, axis=0, keepdims=True)

        @pl.when(pl.program_id(0) == 0)
        def _():
            dg_ref[...] = part

        @pl.when(pl.program_id(0) > 0)
        def _():
            dg_ref[...] += part

    row = pl.BlockSpec((tr, D), lambda i: (i, 0))
    vec = pl.BlockSpec((1, D), lambda i: (0, 0))
    return pl.pallas_call(
        body, name=name, grid=(R // tr,), in_specs=[row, row, vec, row], out_specs=[row, row, vec],
        out_shape=[jax.ShapeDtypeStruct((R, D), F32), jax.ShapeDtypeStruct((R, D), BF16), jax.ShapeDtypeStruct((1, D), F32)],
        compiler_params=_params(("arbitrary",)),
    )(dh, x, gain, dres)


def _head_norm(xs, g):
    r = lax.rsqrt(jnp.mean(xs * xs, axis=-1, keepdims=True) + EPS)
    return xs * r * g


def _head_norm_bwd(xs, g, dy):
    r = lax.rsqrt(jnp.mean(xs * xs, axis=-1, keepdims=True) + EPS)
    xn = xs * r
    uu = dy * g
    return r * (uu - xn * jnp.mean(xn * uu, axis=-1, keepdims=True)), jnp.sum(dy * xn, axis=0, keepdims=True)


NORMED = [(C_FQ, FOX_HEADS, 0), (C_FK, FOX_HEADS, 1), (C_SQ, SWA_HEADS, 2), (C_SK, SWA_KV_HEADS, 3), (C_MQ, MEM_HEADS, 4)]
PLAIN = [(C_FV, FOX_HEADS), (C_SV, SWA_KV_HEADS)]


def _prep_fwd(proj, gains):
    T = proj.shape[0]
    tr = _tile(T, 256, 128)

    def body(p_ref, g_ref, o_ref, lf_ref, kt_ref, vt_ref):
        for start, heads, row in NORMED:
            gn = g_ref[row:row + 1, :]
            for hh in range(heads):
                sl = slice(start + hh * HEAD_DIM, start + (hh + 1) * HEAD_DIM)
                y = _head_norm(p_ref[:, sl], gn)
                o_ref[:, sl] = y.astype(BF16)
                if start == C_FK:
                    kt_ref[hh * HEAD_DIM:(hh + 1) * HEAD_DIM, :] = y.T.astype(BF16)
        for start, heads in PLAIN:
            sl = slice(start, start + heads * HEAD_DIM)
            o_ref[:, sl] = p_ref[:, sl].astype(BF16)
        for hh in range(FOX_HEADS):
            sl = slice(C_FV + hh * HEAD_DIM, C_FV + (hh + 1) * HEAD_DIM)
            vt_ref[hh * HEAD_DIM:(hh + 1) * HEAD_DIM, :] = p_ref[:, sl].T.astype(BF16)
        zb = p_ref[:, C_FL:C_FL + HEAD_DIM] + g_ref[5:6, :]
        o_ref[:, C_FL:C_FL + HEAD_DIM] = jnp.zeros((tr, HEAD_DIM), BF16)
        lf_ref[...] = jnp.minimum(zb, 0.0) - jnp.log(1.0 + jnp.exp(-jnp.abs(zb)))

    return pl.pallas_call(
        body, name="prep_fwd", grid=(T // tr,),
        in_specs=[pl.BlockSpec((tr, PROJ_W), lambda i: (i, 0)), pl.BlockSpec((8, 128), lambda i: (0, 0))],
        out_specs=[pl.BlockSpec((tr, PROJ_W), lambda i: (i, 0)), pl.BlockSpec((tr, HEAD_DIM), lambda i: (i, 0)),
                   pl.BlockSpec((FOX_W, tr), lambda i: (0, i)), pl.BlockSpec((FOX_W, tr), lambda i: (0, i))],
        out_shape=[jax.ShapeDtypeStruct((T, PROJ_W), BF16), jax.ShapeDtypeStruct((T, HEAD_DIM), F32),
                   jax.ShapeDtypeStruct((FOX_W, T), BF16), jax.ShapeDtypeStruct((FOX_W, T), BF16)],
        compiler_params=_params(("parallel",)),
    )(proj, gains)


def _prep_bwd(proj, gains, dfq, dfk, dfv, dsq, dsk, dsv, dmq, dlogf):
    T = proj.shape[0]
    tr = _tile(T, 256, 8)
    d_normed = {C_FQ: 0, C_FK: 1, C_SQ: 3, C_SK: 4, C_MQ: 6}
    d_plain = {C_FV: 2, C_SV: 5}

    def body(p_ref, g_ref, *rest):
        d_refs, dlf_ref, o_ref, dg_ref = rest[:7], rest[7], rest[8], rest[9]
        rows = []
        for start, heads, row in NORMED:
            gn = g_ref[row:row + 1, :]
            d_ref = d_refs[d_normed[start]]
            tot = jnp.zeros((1, HEAD_DIM), F32)
            for hh in range(heads):
                sl = slice(start + hh * HEAD_DIM, start + (hh + 1) * HEAD_DIM)
                dx, dgn = _head_norm_bwd(p_ref[:, sl], gn, d_ref[:, hh * HEAD_DIM:(hh + 1) * HEAD_DIM])
                o_ref[:, sl] = dx.astype(BF16)
                tot = tot + dgn
            rows.append(tot)
        for start, heads in PLAIN:
            o_ref[:, start:start + heads * HEAD_DIM] = d_refs[d_plain[start]][...].astype(BF16)
        zb = p_ref[:, C_FL:C_FL + HEAD_DIM] + g_ref[5:6, :]
        lane = lax.broadcasted_iota(jnp.int32, (tr, HEAD_DIM), 1)
        dz = jnp.where(lane < FOX_HEADS, dlf_ref[...] * (1.0 - _sigmoid(zb)), 0.0)
        o_ref[:, C_FL:C_FL + HEAD_DIM] = dz.astype(BF16)
        rows.append(jnp.sum(dz, axis=0, keepdims=True))
        part = jnp.concatenate(rows + [jnp.zeros((2, HEAD_DIM), F32)], axis=0)

        @pl.when(pl.program_id(0) == 0)
        def _():
            dg_ref[...] = part

        @pl.when(pl.program_id(0) > 0)
        def _():
            dg_ref[...] += part

    def rows_of(w):
        return pl.BlockSpec((tr, w), lambda i: (i, 0))

    small = pl.BlockSpec((8, 128), lambda i: (0, 0))
    ds = [dfq, dfk, dfv, dsq, dsk, dsv, dmq]
    return pl.pallas_call(
        body, name="prep_bwd", grid=(T // tr,),
        in_specs=[rows_of(PROJ_W), small] + [rows_of(d.shape[1]) for d in ds] + [rows_of(HEAD_DIM)],
        out_specs=[rows_of(PROJ_W), small],
        out_shape=[jax.ShapeDtypeStruct((T, PROJ_W), BF16), jax.ShapeDtypeStruct((8, 128), F32)],
        compiler_params=_params(("arbitrary",)),
    )(proj, gains, *ds, dlogf)


def _head_norm_rows(x, gain):
    R, W = x.shape

    def body(x_ref, g_ref, o_ref):
        for hh in range(W // HEAD_DIM):
            sl = slice(hh * HEAD_DIM, (hh + 1) * HEAD_DIM)
            o_ref[:, sl] = _head_norm(x_ref[:, sl], g_ref[...]).astype(BF16)

    return pl.pallas_call(body, name="mem_k_norm", out_shape=jax.ShapeDtypeStruct((R, W), BF16))(x, gain)


def _head_norm_rows_bwd(x, gain, dy):
    R, W = x.shape

    def body(x_ref, g_ref, dy_ref, dx_ref, dg_ref):
        tot = jnp.zeros((1, HEAD_DIM), F32)
        for hh in range(W // HEAD_DIM):
            sl = slice(hh * HEAD_DIM, (hh + 1) * HEAD_DIM)
            dx, dgn = _head_norm_bwd(x_ref[:, sl], g_ref[...], dy_ref[:, sl])
            dx_ref[:, sl] = dx.astype(BF16)
            tot = tot + dgn
        dg_ref[...] = tot

    return pl.pallas_call(
        body, name="mem_k_norm_bwd",
        out_shape=[jax.ShapeDtypeStruct((R, W), BF16), jax.ShapeDtypeStruct((1, HEAD_DIM), F32)])(x, gain, dy)


def _cumsum_rows(name, xs, reverse, columns=None):
    T, W = xs[0].shape
    tb = _tile(T, 512, 8)
    nb = T // tb
    n_in = len(xs) + (0 if columns is None else 1)

    def body(*refs):
        o_ref, carry = refs[n_in], refs[n_in + 1]

        @pl.when(pl.program_id(0) == 0)
        def _():
            carry[...] = jnp.zeros_like(carry)

        xv = refs[0][...]
        for x_ref in refs[1:len(xs)]:
            xv = xv + x_ref[...]
        if columns is not None:
            lane = lax.broadcasted_iota(jnp.int32, (tb, W), 1)
            for hh in range(columns.shape[0]):
                xv = xv + jnp.where(lane == hh, refs[len(xs)][hh], 0.0)
        r = lax.broadcasted_iota(jnp.int32, (tb, tb), 0)
        cc = lax.broadcasted_iota(jnp.int32, (tb, tb), 1)
        tri = jnp.where((cc >= r) if reverse else (cc <= r), 1.0, 0.0).astype(F32)
        o_ref[...] = jnp.dot(tri, xv, precision=lax.Precision.HIGHEST, preferred_element_type=F32) + carry[...]
        carry[...] += jnp.sum(xv, axis=0, keepdims=True)

    idx = (lambda i: (nb - 1 - i, 0)) if reverse else (lambda i: (i, 0))
    in_specs = [pl.BlockSpec((tb, W), idx)] * len(xs)
    if columns is not None:
        in_specs.append(pl.BlockSpec((columns.shape[0], tb, 1), lambda i: (0, idx(i)[0], 0)))
    return pl.pallas_call(
        body, name=name, grid=(nb,), in_specs=in_specs, out_specs=pl.BlockSpec((tb, W), idx),
        out_shape=jax.ShapeDtypeStruct((T, W), F32), scratch_shapes=[pltpu.VMEM((1, W), F32)],
        compiler_params=_params(("arbitrary",)),
    )(*xs, *([] if columns is None else [columns]))


def _triangle(nq, by_column):
    if by_column:
        blocks = [(i, j) for j in range(nq) for i in range(j, nq)]
    else:
        blocks = [(i, j) for i in range(nq) for j in range(i + 1)]
    return jnp.asarray(np.array(blocks, np.int32).T)


def _fox_scores_t(k, q, cq_row, ck_rep, on_diagonal):
    n = q.shape[0]
    s = _dot(k, q, "nt") * SCALE + (cq_row - jnp.tile(ck_rep, (1, n // HEAD_DIM)))
    if on_diagonal:
        s = jnp.where(lax.broadcasted_iota(jnp.int32, (n, n), 0) <= lax.broadcasted_iota(jnp.int32, (n, n), 1), s, NEG_INF)
    return s


def _fox_fwd(qkv, v_t, cq_row, ck_rep):
    T = qkv.shape[0]
    tq = _tile(T, 512)
    nq = T // tq
    steps = nq * (nq + 1) // 2
    HQ, HK = C_FQ // HEAD_DIM, C_FK // HEAD_DIM

    def body(tab, q_ref, k_ref, vt_ref, cq_ref, ck_ref, o_ref, of_ref, lse_ref, m_sc, l_sc, acc_sc):
        i, j = tab[0, pl.program_id(1)], tab[1, pl.program_id(1)]

        @pl.when(j == 0)
        def _():
            m_sc[...] = jnp.full_like(m_sc, NEG_INF)
            l_sc[...] = jnp.zeros_like(l_sc)
            acc_sc[...] = jnp.zeros_like(acc_sc)

        def step(on_diagonal):
            s = _fox_scores_t(k_ref[...], q_ref[...], cq_ref[...], ck_ref[...], on_diagonal)
            m_new = jnp.maximum(m_sc[...], jnp.max(s, axis=0, keepdims=True))
            alpha = jnp.exp(m_sc[...] - m_new)
            p = jnp.exp(s - m_new)
            l_sc[...] = alpha * l_sc[...] + jnp.sum(p, axis=0, keepdims=True)
            acc_sc[...] = alpha * acc_sc[...] + _dot(vt_ref[...], p.astype(BF16), "nn")
            m_sc[...] = m_new

        @pl.when(j < i)
        def _():
            step(False)

        @pl.when(j == i)
        def _():
            step(True)
            o = (acc_sc[...] / l_sc[...]).T
            o_ref[...] = o.astype(BF16)
            of_ref[...] = o
            lse_ref[...] = m_sc[...] + jnp.log(l_sc[...])

    qrow = pl.BlockSpec((None, 1, tq), lambda h, s, tab: (h, 0, tab[0, s]))
    return pl.pallas_call(
        body, name="fox_fwd",
        grid_spec=pltpu.PrefetchScalarGridSpec(
            num_scalar_prefetch=1, grid=(FOX_HEADS, steps),
            in_specs=[pl.BlockSpec((tq, HEAD_DIM), lambda h, s, tab: (tab[0, s], HQ + h)),
                      pl.BlockSpec((tq, HEAD_DIM), lambda h, s, tab: (tab[1, s], HK + h)),
                      pl.BlockSpec((HEAD_DIM, tq), lambda h, s, tab: (h, tab[1, s])), qrow,
                      pl.BlockSpec((None, tq, HEAD_DIM), lambda h, s, tab: (h, tab[1, s], 0))],
            out_specs=[pl.BlockSpec((tq, HEAD_DIM), lambda h, s, tab: (tab[0, s], h)),
                       pl.BlockSpec((tq, HEAD_DIM), lambda h, s, tab: (tab[0, s], h)), qrow],
            scratch_shapes=[pltpu.VMEM((1, tq), F32), pltpu.VMEM((1, tq), F32), pltpu.VMEM((HEAD_DIM, tq), F32)]),
        out_shape=[jax.ShapeDtypeStruct((T, FOX_W), BF16), jax.ShapeDtypeStruct((T, FOX_W), F32),
                   jax.ShapeDtypeStruct((FOX_HEADS, 1, T), F32)],
        compiler_params=_params(("parallel", "arbitrary")),
    )(_triangle(nq, False), qkv, qkv, v_t, cq_row, ck_rep)


def _fox_delta(dmix, out_f32):
    T = out_f32.shape[0]
    tr = _tile(T, 512, 8)

    def body(do_ref, o_ref, d_ref):
        lane = lax.broadcasted_iota(jnp.int32, (tr, HEAD_DIM), 1)
        acc = jnp.zeros((tr, HEAD_DIM), F32)
        for hh in range(FOX_HEADS):
            sl = slice(hh * HEAD_DIM, (hh + 1) * HEAD_DIM)
            d = jnp.sum(do_ref[:, sl].astype(F32) * o_ref[:, sl], axis=-1, keepdims=True)
            acc = jnp.where(lane == hh, d, acc)
        d_ref[...] = acc

    blk = pl.BlockSpec((tr, FOX_W), lambda i: (i, 0))
    return pl.pallas_call(
        body, name="fox_delta", grid=(T // tr,), in_specs=[blk, blk], out_specs=pl.BlockSpec((tr, HEAD_DIM), lambda i: (i, 0)),
        out_shape=jax.ShapeDtypeStruct((T, HEAD_DIM), F32), compiler_params=_params(("parallel",)),
    )(dmix, out_f32)


def _fox_bwd(qkv, k_t, cq_row, ck_rep, delta_row, lse, dmix):
    T = qkv.shape[0]
    tq = _tile(T, 512)
    nq = T // tq
    steps = nq * (nq + 1) // 2
    HQ, HK, HV = C_FQ // HEAD_DIM, C_FK // HEAD_DIM, C_FV // HEAD_DIM

    def body(tab, q_ref, k_ref, kt_ref, v_ref, cq_ref, ck_ref, delta_ref, lse_ref, do_ref,
             dq_ref, dk_ref, dv_ref, dck_ref, dcq_ref, dk_sc, dv_sc, dc_sc, dqt_sc):
        qi, kj = tab[0, pl.program_id(1)], tab[1, pl.program_id(1)]

        @pl.when(qi == kj)
        def _():
            dk_sc[...] = jnp.zeros_like(dk_sc)
            dv_sc[...] = jnp.zeros_like(dv_sc)
            dc_sc[...] = jnp.zeros_like(dc_sc)

        def step(on_diagonal):
            q, k, v, do = q_ref[...], k_ref[...], v_ref[...], do_ref[...]
            p = jnp.exp(_fox_scores_t(k, q, cq_ref[...], ck_ref[...], on_diagonal) - lse_ref[...])
            dp = _dot(v, do, "nt")
            ds = p * (dp - delta_ref[...])
            dsb = ds.astype(BF16)
            dv_sc[...] += _dot(p.astype(BF16), do, "nn")
            dk_sc[...] += _dot(dsb, q, "nn")
            dc_sc[...] += jnp.sum(ds, axis=1, keepdims=True)
            dq_part = _dot(kt_ref[...], dsb, "nn") * SCALE
            dcq_part = jnp.sum(ds, axis=0, keepdims=True)

            @pl.when(kj == 0)
            def _():
                dqt_sc[qi] = dq_part
                dcq_ref[qi] = dcq_part

            @pl.when(kj > 0)
            def _():
                dqt_sc[qi] += dq_part
                dcq_ref[qi] += dcq_part

            if on_diagonal:
                dq_ref[...] = dqt_sc[qi].T

        @pl.when(qi > kj)
        def _():
            step(False)

        @pl.when(qi == kj)
        def _():
            step(True)

        @pl.when(qi == nq - 1)
        def _():
            dk_ref[...] = dk_sc[...] * SCALE
            dv_ref[...] = dv_sc[...]
            dck_ref[...] = -dc_sc[...]

    def rows(base):
        return pl.BlockSpec((tq, HEAD_DIM), lambda h, s, tab: (tab[0, s], base + h))

    def cols(base):
        return pl.BlockSpec((tq, HEAD_DIM), lambda h, s, tab: (tab[1, s], base + h))

    qrow = pl.BlockSpec((None, 1, tq), lambda h, s, tab: (h, 0, tab[0, s]))
    sds = jax.ShapeDtypeStruct((T, FOX_W), F32)
    return pl.pallas_call(
        body, name="fox_bwd",
        grid_spec=pltpu.PrefetchScalarGridSpec(
            num_scalar_prefetch=1, grid=(FOX_HEADS, steps),
            in_specs=[rows(HQ), cols(HK), pl.BlockSpec((HEAD_DIM, tq), lambda h, s, tab: (h, tab[1, s])), cols(HV), qrow,
                      pl.BlockSpec((None, tq, HEAD_DIM), lambda h, s, tab: (h, tab[1, s], 0)), qrow, qrow, rows(0)],
            out_specs=[cols(0), cols(0), cols(0), pl.BlockSpec((None, tq, 1), lambda h, s, tab: (h, tab[1, s], 0)),
                       pl.BlockSpec((None, nq, 1, tq), lambda h, s, tab: (h, 0, 0, 0))],
            scratch_shapes=[pltpu.VMEM((tq, HEAD_DIM), F32), pltpu.VMEM((tq, HEAD_DIM), F32), pltpu.VMEM((tq, 1), F32),
                            pltpu.VMEM((nq, HEAD_DIM, tq), F32)]),
        out_shape=[sds, sds, sds, jax.ShapeDtypeStruct((FOX_HEADS, T, 1), F32), jax.ShapeDtypeStruct((FOX_HEADS, nq, 1, tq), F32)],
        compiler_params=_params(("parallel", "arbitrary")),
    )(_triangle(nq, True), qkv, qkv, k_t, qkv, cq_row, ck_rep, delta_row, lse, dmix)


GW = SWA_GROUP * HEAD_DIM
GR = SWA_GROUP * WINDOW


def _swa_scores(q_ref, kp_ref, kc_ref, slope_ref, n):
    q = q_ref[...]
    qs = jnp.concatenate([q[:, t * HEAD_DIM:(t + 1) * HEAD_DIM] for t in range(SWA_GROUP)], axis=0)
    kb = jnp.concatenate([kp_ref[...], kc_ref[...]], axis=0)
    r = lax.broadcasted_iota(jnp.int32, (GR, 2 * WINDOW), 0) & (WINDOW - 1)
    jj = lax.broadcasted_iota(jnp.int32, (GR, 2 * WINDOW), 1)
    dist = WINDOW + r - jj
    valid = (dist >= 0) & (dist < WINDOW) & ((n > 0) | (jj >= WINDOW))
    s = _dot(qs, kb, "nt") * SCALE - slope_ref[...] * dist.astype(F32)
    return qs, kb, jnp.where(valid, s, NEG_INF), valid


def _swa_specs():
    HQ, HK, HV = C_SQ // GW, C_SK // HEAD_DIM, C_SV // HEAD_DIM
    q_spec = pl.BlockSpec((WINDOW, GW), lambda g, n: (n, HQ + g))

    def prev(base):
        return pl.BlockSpec((WINDOW, HEAD_DIM), lambda g, n: (jnp.maximum(n - 1, 0), base + g))

    def cur(base):
        return pl.BlockSpec((WINDOW, HEAD_DIM), lambda g, n: (n, base + g))

    col = pl.BlockSpec((None, GR, 1), lambda g, n: (g, 0, 0))
    return q_spec, prev(HK), cur(HK), prev(HV), cur(HV), col


def _swa_fwd(qkv, slopes, sinks):
    T = qkv.shape[0]
    nb = T // WINDOW
    assert C_SQ % GW == 0

    def body(q_ref, kp_ref, kc_ref, vp_ref, vc_ref, slope_ref, sink_ref, o_ref, lse_ref):
        n = pl.program_id(1)
        _, _, s, _ = _swa_scores(q_ref, kp_ref, kc_ref, slope_ref, n)
        m = jnp.maximum(jnp.max(s, axis=-1, keepdims=True), sink_ref[...])
        p = jnp.exp(s - m)
        l = jnp.sum(p, axis=-1, keepdims=True) + jnp.exp(sink_ref[...] - m)
        vb = jnp.concatenate([vp_ref[...], vc_ref[...]], axis=0)
        o = _dot(p.astype(BF16), vb, "nn") / l
        for t in range(SWA_GROUP):
            o_ref[:, t * HEAD_DIM:(t + 1) * HEAD_DIM] = o[t * WINDOW:(t + 1) * WINDOW, :].astype(BF16)
        lse_ref[...] = m + jnp.log(l)

    q_spec, kp, kc, vp, vc, col = _swa_specs()
    return pl.pallas_call(
        body, name="swa_fwd", grid=(SWA_KV_HEADS, nb), in_specs=[q_spec, kp, kc, vp, vc, col, col],
        out_specs=[pl.BlockSpec((WINDOW, GW), lambda g, n: (n, g)), pl.BlockSpec((None, None, GR, 1), lambda g, n: (g, n, 0, 0))],
        out_shape=[jax.ShapeDtypeStruct((T, SWA_HEADS * HEAD_DIM), BF16), jax.ShapeDtypeStruct((SWA_KV_HEADS, nb, GR, 1), F32)],
        compiler_params=_params(("parallel", "arbitrary")),
    )(qkv, qkv, qkv, qkv, qkv, slopes, sinks)


def _swa_bwd(qkv, slopes, sinks, out, lse, dmix):
    T = qkv.shape[0]
    nb = T // WINDOW
    DO = FOX_W // GW
    assert FOX_W % GW == 0

    def body(q_ref, kp_ref, kc_ref, vp_ref, vc_ref, slope_ref, sink_ref, o_ref, lse_ref, do_ref,
             dq_ref, dk_ref, dv_ref, dsink_ref, sink_sc):
        n = pl.program_id(1)

        @pl.when(n == 0)
        def _():
            dk_ref[...] = jnp.zeros_like(dk_ref)
            dv_ref[...] = jnp.zeros_like(dv_ref)
            sink_sc[...] = jnp.zeros_like(sink_sc)

        qs, kb, s, valid = _swa_scores(q_ref, kp_ref, kc_ref, slope_ref, n)
        lse = lse_ref[...]
        p = jnp.where(valid, jnp.exp(s - lse), 0.0)
        vb = jnp.concatenate([vp_ref[...], vc_ref[...]], axis=0)
        do = jnp.concatenate([do_ref[:, t * HEAD_DIM:(t + 1) * HEAD_DIM] for t in range(SWA_GROUP)], axis=0)
        oo = jnp.concatenate([o_ref[:, t * HEAD_DIM:(t + 1) * HEAD_DIM] for t in range(SWA_GROUP)], axis=0)
        dp = _dot(do, vb, "nt")
        delta = jnp.sum(do.astype(F32) * oo.astype(F32), axis=-1, keepdims=True)
        ds = p * (dp - delta)
        dsb = ds.astype(BF16)
        dq = _dot(dsb, kb, "nn") * SCALE
        for t in range(SWA_GROUP):
            dq_ref[:, t * HEAD_DIM:(t + 1) * HEAD_DIM] = dq[t * WINDOW:(t + 1) * WINDOW, :]
        dkb = _dot(dsb, qs, "tn") * SCALE
        dvb = _dot(p.astype(BF16), do, "tn")
        r_prev = pl.ds(pl.multiple_of(jnp.maximum(n - 1, 0) * WINDOW, WINDOW), WINDOW)
        r_cur = pl.ds(pl.multiple_of(n * WINDOW, WINDOW), WINDOW)
        dk_ref[r_prev, :] += dkb[:WINDOW, :]
        dk_ref[r_cur, :] += dkb[WINDOW:, :]
        dv_ref[r_prev, :] += dvb[:WINDOW, :]
        dv_ref[r_cur, :] += dvb[WINDOW:, :]
        sink_sc[...] -= jnp.exp(sink_ref[...] - lse) * delta

        @pl.when(n == nb - 1)
        def _():
            tot = [jnp.zeros((1, 128), F32) + jnp.sum(sink_sc[t * WINDOW:(t + 1) * WINDOW, :]) for t in range(SWA_GROUP)]
            dsink_ref[...] = jnp.concatenate(tot + [jnp.zeros((8 - SWA_GROUP, 128), F32)], axis=0)

    q_spec, kp, kc, vp, vc, col = _swa_specs()
    kv_acc = pl.BlockSpec((T, HEAD_DIM), lambda g, n: (0, g))
    return pl.pallas_call(
        body, name="swa_bwd", grid=(SWA_KV_HEADS, nb),
        in_specs=[q_spec, kp, kc, vp, vc, col, col, pl.BlockSpec((WINDOW, GW), lambda g, n: (n, g)),
                  pl.BlockSpec((None, None, GR, 1), lambda g, n: (g, n, 0, 0)), pl.BlockSpec((WINDOW, GW), lambda g, n: (n, DO + g))],
        out_specs=[pl.BlockSpec((WINDOW, GW), lambda g, n: (n, g)), kv_acc, kv_acc, pl.BlockSpec((None, 8, 128), lambda g, n: (g, 0, 0))],
        out_shape=[jax.ShapeDtypeStruct((T, SWA_HEADS * HEAD_DIM), F32), jax.ShapeDtypeStruct((T, SWA_KV_HEADS * HEAD_DIM), F32),
                   jax.ShapeDtypeStruct((T, SWA_KV_HEADS * HEAD_DIM), F32), jax.ShapeDtypeStruct((SWA_KV_HEADS, 8, 128), F32)],
        scratch_shapes=[pltpu.VMEM((GR, 1), F32)],
        compiler_params=_params(("parallel", "arbitrary")),
    )(qkv, qkv, qkv, qkv, qkv, slopes, sinks, out, lse, dmix)


def _mem_fwd(qkv, mk, mv):
    T, ML = qkv.shape[0], mk.shape[0]
    tq = _tile(T, 512)
    HQ = C_MQ // HEAD_DIM

    def body(q_ref, k_ref, v_ref, o_ref, lse_ref):
        s = _dot(q_ref[...], k_ref[...], "nt") * SCALE
        m = jnp.max(s, axis=-1, keepdims=True)
        p = jnp.exp(s - m)
        l = jnp.sum(p, axis=-1, keepdims=True)
        o_ref[...] = (_dot(p.astype(BF16), v_ref[...], "nn") / l).astype(BF16)
        lse_ref[...] = m + jnp.log(l)

    kv = pl.BlockSpec((ML, HEAD_DIM), lambda h, i: (0, h))
    return pl.pallas_call(
        body, name="mem_fwd", grid=(MEM_HEADS, T // tq),
        in_specs=[pl.BlockSpec((tq, HEAD_DIM), lambda h, i: (i, HQ + h)), kv, kv],
        out_specs=[pl.BlockSpec((tq, HEAD_DIM), lambda h, i: (i, h)), pl.BlockSpec((None, tq, 1), lambda h, i: (h, i, 0))],
        out_shape=[jax.ShapeDtypeStruct((T, MEM_HEADS * HEAD_DIM), BF16), jax.ShapeDtypeStruct((MEM_HEADS, T, 1), F32)],
        compiler_params=_params(("parallel", "arbitrary")),
    )(qkv, mk, mv)


def _mem_bwd(qkv, mk, mv, out, lse, dmix):
    T, ML = qkv.shape[0], mk.shape[0]
    tq = _tile(T, 512)
    HQ = C_MQ // HEAD_DIM
    DO = (FOX_W + SWA_HEADS * HEAD_DIM) // HEAD_DIM

    def body(q_ref, k_ref, v_ref, o_ref, lse_ref, do_ref, dq_ref, dk_ref, dv_ref):
        q, k, v, do = q_ref[...], k_ref[...], v_ref[...], do_ref[...]
        p = jnp.exp(_dot(q, k, "nt") * SCALE - lse_ref[...])
        dp = _dot(do, v, "nt")
        delta = jnp.sum(do.astype(F32) * o_ref[...].astype(F32), axis=-1, keepdims=True)
        dsb = (p * (dp - delta)).astype(BF16)
        dq_ref[...] = _dot(dsb, k, "nn") * SCALE
        dk_part = _dot(dsb, q, "tn") * SCALE
        dv_part = _dot(p.astype(BF16), do, "tn")

        @pl.when(pl.program_id(1) == 0)
        def _():
            dk_ref[...] = dk_part
            dv_ref[...] = dv_part

        @pl.when(pl.program_id(1) > 0)
        def _():
            dk_ref[...] += dk_part
            dv_ref[...] += dv_part

    kv = pl.BlockSpec((ML, HEAD_DIM), lambda h, i: (0, h))
    qb = pl.BlockSpec((tq, HEAD_DIM), lambda h, i: (i, h))
    return pl.pallas_call(
        body, name="mem_bwd", grid=(MEM_HEADS, T // tq),
        in_specs=[pl.BlockSpec((tq, HEAD_DIM), lambda h, i: (i, HQ + h)), kv, kv, qb,
                  pl.BlockSpec((None, tq, 1), lambda h, i: (h, i, 0)), pl.BlockSpec((tq, HEAD_DIM), lambda h, i: (i, DO + h))],
        out_specs=[qb, kv, kv],
        out_shape=[jax.ShapeDtypeStruct((T, MEM_HEADS * HEAD_DIM), F32), jax.ShapeDtypeStruct((ML, MEM_HEADS * HEAD_DIM), F32),
                   jax.ShapeDtypeStruct((ML, MEM_HEADS * HEAD_DIM), F32)],
        compiler_params=_params(("parallel", "arbitrary")),
    )(qkv, mk, mv, out, lse, dmix)


HBM = pl.BlockSpec(memory_space=pltpu.HBM)


def _place():
    x, y, c = lax.axis_index("x"), lax.axis_index("y"), lax.axis_index("c")
    chips = [(1 - x, y), (x, 1 - y), (1 - x, 1 - y)]
    return x, y, c, chips


def _remote(src, dst, send_sem, recv_sem, device):
    return pltpu.make_async_remote_copy(src_ref=src, dst_ref=dst, send_sem=send_sem, recv_sem=recv_sem,
                                        device_id=device, device_id_type=MESH)


def _place_ids():
    x, y, c = lax.axis_index("x"), lax.axis_index("y"), lax.axis_index("c")
    order = [2 * x + y, 2 * (1 - x) + y, 2 * x + (1 - y), 2 * (1 - x) + (1 - y)]
    return jnp.stack([2 * x + y, c] + order).astype(jnp.int32)


def _cast_place(name, w, ids, after):
    R, C = w.shape
    tr = _tile(R, 256, 16)

    def body(ids_ref, w_ref, after_ref, o_ref):
        o_ref[...] = w_ref[...].astype(BF16)

    return pl.pallas_call(
        body, name=name,
        grid_spec=pltpu.PrefetchScalarGridSpec(
            num_scalar_prefetch=1, grid=(R // tr,),
            in_specs=[pl.BlockSpec((tr, C), lambda i, ids: (i, 0)), pl.BlockSpec(memory_space=pl.ANY)],
            out_specs=pl.BlockSpec((None, tr, C), lambda i, ids: (ids[0], i, 0))),
        out_shape=jax.ShapeDtypeStruct((N_CHIPS, R, C), BF16), compiler_params=_params(("parallel",)),
    )(ids, w, after)


SEM = pl.BlockSpec(memory_space=pltpu.SEMAPHORE)
EFFECT = pltpu.SideEffectType.DATAFLOW_SIDE_EFFECTING


def _hbm(a):
    return pltpu.with_memory_space_constraint(a, pltpu.HBM)


def _gather_start(name, placed, after):
    n = len(placed)

    ns = 3 * n

    def body(*refs):
        send, recv = refs[n + 1:n + 1 + ns], refs[n + 1 + ns:n + 1 + 2 * ns]
        buf = refs[n + 1 + 2 * ns:2 * n + 1 + 2 * ns]
        token = refs[2 * n + 1 + 2 * ns]
        x, y, c, chips = _place()
        me = 2 * x + y
        for a in range(n):
            half = buf[a].shape[1] // 2
            mine = buf[a].at[me, pl.ds(c * half, half)]
            for j, (cx, cy) in enumerate(chips):
                _remote(mine, mine, send[3 * a + j], recv[3 * a + j], (cx, cy, c)).start()
        token[...] = jnp.zeros_like(token)

    res = pl.pallas_call(
        body, name=name, in_specs=[HBM] * n + [pl.BlockSpec(memory_space=pl.ANY)],
        out_specs=[SEM] * (2 * ns) + [HBM] * n + [pl.BlockSpec(memory_space=pltpu.VMEM)],
        out_shape=[pltpu.SemaphoreType.DMA(())] * (2 * ns)
        + [pltpu.HBM(s.shape, s.dtype) for s in placed] + [jax.ShapeDtypeStruct((8, 128), F32)],
        input_output_aliases={a: 2 * ns + a for a in range(n)},
        compiler_params=pltpu.CompilerParams(has_side_effects=EFFECT),
    )(*[_hbm(s) for s in placed], after)
    return list(res[:ns]), list(res[ns:2 * ns]), list(res[2 * ns:2 * ns + n]), res[2 * ns + n]


def _gather_wait(name, send, recv, bufs, after):
    n = len(bufs)

    ns = 3 * n

    def body(*refs):
        buf = refs[:n]
        send_ref, recv_ref = refs[n:n + ns], refs[n + ns:n + 2 * ns]
        x, y, c, chips = _place()
        ids = [2 * cx + cy for cx, cy in chips]
        for a in range(n):
            half = buf[a].shape[1] // 2
            for j in range(3):
                landed = buf[a].at[ids[j], pl.ds(c * half, half)]
                cp = _remote(landed, landed, send_ref[3 * a + j], recv_ref[3 * a + j], (x, y, c))
                cp.wait_send()
                cp.wait_recv()

    res = pl.pallas_call(
        body, name=name, in_specs=[HBM] * n + [SEM] * (2 * ns) + [pl.BlockSpec(memory_space=pl.ANY)], out_specs=[HBM] * n,
        out_shape=[pltpu.HBM(s.shape, s.dtype) for s in bufs], input_output_aliases={a: a for a in range(n)},
        compiler_params=pltpu.CompilerParams(has_side_effects=EFFECT),
    )(*bufs, *send, *recv, after)
    return list(res)


def _gather_forward(name, bufs):
    n = len(bufs)

    def body(*refs):
        buf = refs[n:2 * n]
        send, recv = refs[2 * n:]
        x, y, c, chips = _place()
        ids = [2 * cx + cy for cx, cy in chips]
        copies = []
        for a in range(n):
            half = buf[a].shape[1] // 2
            for j in range(3):
                landed = buf[a].at[ids[j], pl.ds(c * half, half)]
                cp = _remote(landed, landed, send.at[a, j], recv.at[a, j], (x, y, 1 - c))
                cp.start()
                copies.append(cp)
        for a in range(n):
            half = buf[a].shape[1] // 2
            for j in range(3):
                landed = buf[a].at[ids[j], pl.ds((1 - c) * half, half)]
                _remote(landed, landed, send.at[a, j], recv.at[a, j], (x, y, c)).wait_recv()
        for cp in copies:
            cp.wait_send()

    return pl.pallas_call(
        body, name=name, in_specs=[HBM] * n, out_specs=[HBM] * n,
        out_shape=[jax.ShapeDtypeStruct(s.shape, s.dtype) for s in bufs], input_output_aliases={a: a for a in range(n)},
        scratch_shapes=[pltpu.SemaphoreType.DMA((n, 3)), pltpu.SemaphoreType.DMA((n, 3))],
    )(*bufs)


def _pair_start(name, grads):
    n = len(grads)
    ns = N_CHIPS * n

    def body(*refs):
        send, recv = refs[2 * n:2 * n + ns], refs[2 * n + ns:2 * n + 2 * ns]
        src = refs[2 * n + 2 * ns:3 * n + 2 * ns]
        land = refs[3 * n + 2 * ns:4 * n + 2 * ns]
        token = refs[4 * n + 2 * ns]
        x, y, c, chips = _place()
        order = [2 * x + y] + [2 * cx + cy for cx, cy in chips]
        for a in range(n):
            half = src[a].shape[1] // 2
            for j in range(N_CHIPS):
                _remote(src[a].at[order[j], pl.ds((1 - c) * half, half)], land[a].at[j],
                        send[N_CHIPS * a + j], recv[N_CHIPS * a + j], (x, y, 1 - c)).start()
        token[...] = jnp.zeros_like(token)

    lands = [jax.ShapeDtypeStruct((N_CHIPS, g.shape[1] // 2, g.shape[2]), g.dtype) for g in grads]
    res = pl.pallas_call(
        body, name=name, in_specs=[HBM] * (2 * n),
        out_specs=[SEM] * (2 * ns) + [HBM] * (2 * n) + [pl.BlockSpec(memory_space=pltpu.VMEM)],
        out_shape=[pltpu.SemaphoreType.DMA(())] * (2 * ns) + [pltpu.HBM(g.shape, g.dtype) for g in grads]
        + [pltpu.HBM(l.shape, l.dtype) for l in lands] + [jax.ShapeDtypeStruct((8, 128), F32)],
        input_output_aliases={a: 2 * ns + a for a in range(2 * n)},
        compiler_params=pltpu.CompilerParams(has_side_effects=EFFECT),
    )(*[_hbm(g) for g in grads], *[_hbm(lax.empty(l.shape, l.dtype)) for l in lands])
    return list(res[:ns]), list(res[ns:2 * ns]), list(res[2 * ns:2 * ns + n]), list(res[2 * ns + n:2 * ns + 2 * n]), res[2 * ns + 2 * n]


def _pair_wait(name, send, recv, grads, lands, after):
    n = len(grads)
    ns = N_CHIPS * n

    def body(*refs):
        src, land = refs[:n], refs[n:2 * n]
        send_ref, recv_ref = refs[2 * n:2 * n + ns], refs[2 * n + ns:2 * n + 2 * ns]
        x, y, c, _ = _place()
        for a in range(n):
            for j in range(N_CHIPS):
                cp = _remote(land[a].at[j], land[a].at[j], send_ref[N_CHIPS * a + j], recv_ref[N_CHIPS * a + j], (x, y, c))
                cp.wait_send()
                cp.wait_recv()

    res = pl.pallas_call(
        body, name=name, in_specs=[HBM] * (2 * n) + [SEM] * (2 * ns) + [pl.BlockSpec(memory_space=pl.ANY)],
        out_specs=[HBM] * (2 * n), out_shape=[pltpu.HBM(g.shape, g.dtype) for g in grads] + [pltpu.HBM(l.shape, l.dtype) for l in lands],
        input_output_aliases={a: a for a in range(2 * n)},
        compiler_params=pltpu.CompilerParams(has_side_effects=EFFECT),
    )(*grads, *lands, *send, *recv, after)
    return list(res[:n]), list(res[n:])


def _chip_start(name, parts):
    n = len(parts)
    ns = 3 * n

    def body(*refs):
        send, recv = refs[2 * n:2 * n + ns], refs[2 * n + ns:2 * n + 2 * ns]
        src = refs[2 * n + 2 * ns:3 * n + 2 * ns]
        land = refs[3 * n + 2 * ns:4 * n + 2 * ns]
        token = refs[4 * n + 2 * ns]
        x, y, c, chips = _place()
        for a in range(n):
            for j, (cx, cy) in enumerate(chips):
                _remote(src[a].at[j], land[a].at[j], send[3 * a + j], recv[3 * a + j], (cx, cy, c)).start()
        token[...] = jnp.zeros_like(token)

    res = pl.pallas_call(
        body, name=name, in_specs=[HBM] * (2 * n),
        out_specs=[SEM] * (2 * ns) + [HBM] * (2 * n) + [pl.BlockSpec(memory_space=pltpu.VMEM)],
        out_shape=[pltpu.SemaphoreType.DMA(())] * (2 * ns) + [pltpu.HBM(p.shape, p.dtype) for p in parts] * 2
        + [jax.ShapeDtypeStruct((8, 128), F32)],
        input_output_aliases={a: 2 * ns + a for a in range(2 * n)},
        compiler_params=pltpu.CompilerParams(has_side_effects=EFFECT),
    )(*[_hbm(p) for p in parts], *[_hbm(lax.empty(p.shape, p.dtype)) for p in parts])
    return list(res[:ns]), list(res[ns:2 * ns]), list(res[2 * ns:2 * ns + n]), list(res[2 * ns + n:2 * ns + 2 * n]), res[2 * ns + 2 * n]


def _chip_wait(name, send, recv, parts, lands, after):
    n = len(parts)
    ns = 3 * n

    def body(*refs):
        src, land = refs[:n], refs[n:2 * n]
        send_ref, recv_ref = refs[2 * n:2 * n + ns], refs[2 * n + ns:2 * n + 2 * ns]
        x, y, c, _ = _place()
        for a in range(n):
            for j in range(3):
                cp = _remote(src[a].at[j], land[a].at[j], send_ref[3 * a + j], recv_ref[3 * a + j], (x, y, c))
                cp.wait_send()
                cp.wait_recv()

    res = pl.pallas_call(
        body, name=name, in_specs=[HBM] * (2 * n) + [SEM] * (2 * ns) + [pl.BlockSpec(memory_space=pl.ANY)],
        out_specs=[HBM] * (2 * n), out_shape=[pltpu.HBM(p.shape, p.dtype) for p in parts] * 2,
        input_output_aliases={a: a for a in range(2 * n)},
        compiler_params=pltpu.CompilerParams(has_side_effects=EFFECT),
    )(*parts, *lands, *send, *recv, after)
    return list(res[n:])


def _pair_share(name, shards):
    n = len(shards)

    def body(*refs):
        buf = refs[n:2 * n]
        send, recv = refs[2 * n:]
        x, y, c, _ = _place()
        copies = []
        for a in range(n):
            half = buf[a].shape[0] // 2
            mine = buf[a].at[pl.ds(c * half, half)]
            cp = _remote(mine, mine, send.at[a], recv.at[a], (x, y, 1 - c))
            cp.start()
            copies.append(cp)
        for a, cp in enumerate(copies):
            half = buf[a].shape[0] // 2
            cp.wait_send()
            theirs = buf[a].at[pl.ds((1 - c) * half, half)]
            _remote(theirs, theirs, send.at[a], recv.at[a], (x, y, c)).wait_recv()

    return pl.pallas_call(
        body, name=name, in_specs=[HBM] * n, out_specs=[HBM] * n,
        out_shape=[jax.ShapeDtypeStruct(s.shape, s.dtype) for s in shards], input_output_aliases={a: a for a in range(n)},
        scratch_shapes=[pltpu.SemaphoreType.DMA((n,)), pltpu.SemaphoreType.DMA((n,))],
    )(*shards)


def _small_start(buf):
    R, W = buf.shape
    ns = N_DEV - 1

    def body(*refs):
        send, recv = refs[2:2 + ns], refs[2 + ns:2 + 2 * ns]
        src, land, token = refs[2 + 2 * ns], refs[3 + 2 * ns], refs[4 + 2 * ns]
        x, y, c, _ = _place()
        me = 4 * x + 2 * y + c
        for k in range(1, N_DEV):
            peer = (x ^ (k >> 2), y ^ ((k >> 1) & 1), c ^ (k & 1))
            _remote(src, land.at[me], send[k - 1], recv[k - 1], peer).start()
        token[...] = jnp.zeros_like(token)

    res = pl.pallas_call(
        body, name="small_start", in_specs=[HBM, HBM],
        out_specs=[SEM] * (2 * ns) + [HBM, HBM, pl.BlockSpec(memory_space=pltpu.VMEM)],
        out_shape=[pltpu.SemaphoreType.DMA(())] * (2 * ns) + [pltpu.HBM((R, W), F32), pltpu.HBM((N_DEV, R, W), F32),
                                                                jax.ShapeDtypeStruct((8, 128), F32)],
        input_output_aliases={0: 2 * ns, 1: 2 * ns + 1},
        compiler_params=pltpu.CompilerParams(has_side_effects=EFFECT),
    )(_hbm(buf), _hbm(jnp.zeros((N_DEV, R, W), F32)))
    return list(res[:ns]), list(res[ns:2 * ns]), res[2 * ns], res[2 * ns + 1], res[2 * ns + 2]


def _small_wait(send, recv, buf, land, after):
    ns = N_DEV - 1

    def body(*refs):
        land_ref = refs[1]
        send_ref, recv_ref = refs[2:2 + ns], refs[2 + ns:2 + 2 * ns]
        x, y, c, _ = _place()
        me = 4 * x + 2 * y + c
        for k in range(1, N_DEV):
            landed = land_ref.at[me ^ k]
            cp = _remote(landed, landed, send_ref[k - 1], recv_ref[k - 1], (x, y, c))
            cp.wait_send()
            cp.wait_recv()

    return pl.pallas_call(
        body, name="small_wait", in_specs=[HBM, HBM] + [SEM] * (2 * ns) + [pl.BlockSpec(memory_space=pl.ANY)],
        out_specs=[HBM, HBM], out_shape=[pltpu.HBM(buf.shape, buf.dtype), pltpu.HBM(land.shape, land.dtype)],
        input_output_aliases={0: 0, 1: 1}, compiler_params=pltpu.CompilerParams(has_side_effects=EFFECT),
    )(buf, land, *send, *recv, after)


def _small_sum(buf, land):
    def body(buf_ref, land_ref, out_ref):
        x, y, c, _ = _place()
        me = 4 * x + 2 * y + c
        total = None
        for d in range(N_DEV):
            term = jnp.where(me == d, buf_ref[...], land_ref[d])
            total = term if total is None else total + term
        out_ref[...] = total

    return pl.pallas_call(body, name="small_sum", out_shape=jax.ShapeDtypeStruct(buf.shape, F32))(buf, land)


def _pair_sum_bf16(name, grad, theirs, ids):
    _, R2, C = theirs.shape
    tr = _tile(R2, 256, 16)
    nrb = R2 // tr

    def body(ids_ref, a_ref, b_ref, o_ref):
        o_ref[...] = (a_ref[...] + b_ref[...]).astype(BF16)

    return pl.pallas_call(
        body, name=name,
        grid_spec=pltpu.PrefetchScalarGridSpec(
            num_scalar_prefetch=1, grid=(3, nrb),
            in_specs=[pl.BlockSpec((None, tr, C), lambda j, i, ids: (ids[3 + j], ids[1] * nrb + i, 0)),
                      pl.BlockSpec((None, tr, C), lambda j, i, ids: (j + 1, i, 0))],
            out_specs=pl.BlockSpec((None, tr, C), lambda j, i, ids: (j, i, 0))),
        out_shape=jax.ShapeDtypeStruct((3, R2, C), BF16), compiler_params=_params(("parallel", "parallel")),
    )(ids, grad, theirs)


def _chip_sum(name, grad, theirs, arrived, ids):
    _, R2, C = theirs.shape
    tr = _tile(R2, 256, 16)
    nrb = R2 // tr

    def body(ids_ref, a_ref, b_ref, r_ref, o_ref):
        tot = a_ref[...] + b_ref[...]
        for j in range(3):
            tot = tot + r_ref[j].astype(F32)
        o_ref[...] = tot

    return pl.pallas_call(
        body, name=name,
        grid_spec=pltpu.PrefetchScalarGridSpec(
            num_scalar_prefetch=1, grid=(nrb,),
            in_specs=[pl.BlockSpec((None, tr, C), lambda i, ids: (ids[0], ids[1] * nrb + i, 0)),
                      pl.BlockSpec((None, tr, C), lambda i, ids: (0, i, 0)),
                      pl.BlockSpec((3, tr, C), lambda i, ids: (0, i, 0))],
            out_specs=pl.BlockSpec((tr, C), lambda i, ids: (ids[1] * nrb + i, 0))),
        out_shape=jax.ShapeDtypeStruct((2 * R2, C), F32), compiler_params=_params(("parallel",)),
    )(ids, grad, theirs, arrived)


def _adamw(name, w, g, m, v, emit_grad=False):
    R, C = w.shape
    tr = _tile(R, 128, 8)
    c1 = 1.0 / (1.0 - ADAM_B1 ** ADAM_STEP)
    c2 = 1.0 / (1.0 - ADAM_B2 ** ADAM_STEP)
    n_out = 4 if emit_grad else 3

    def body(w_ref, g_ref, m_ref, v_ref, d_ref, mo_ref, vo_ref, *rest):
        gv = g_ref[...]
        mn = ADAM_B1 * m_ref[...] + (1.0 - ADAM_B1) * gv
        vn = ADAM_B2 * v_ref[...] + (1.0 - ADAM_B2) * (gv * gv)
        d_ref[...] = -ADAM_LR * ((mn * c1) / (jnp.sqrt(vn * c2) + ADAM_EPS) + ADAM_WD * w_ref[...])
        mo_ref[...] = mn
        vo_ref[...] = vn
        if emit_grad:
            rest[0][...] = gv

    spec = pl.BlockSpec((tr, C), lambda i: (i, 0))
    sds = jax.ShapeDtypeStruct((R, C), F32)
    return pl.pallas_call(body, name=name, grid=(R // tr,), in_specs=[spec] * 4, out_specs=[spec] * n_out, out_shape=[sds] * n_out,
                          compiler_params=_params(("parallel",)))(w, g, m, v)


SMALL = ["ffn1_norm", "mix_norm", "mem_norm", "forget_bias", "fox_q_gain", "fox_k_gain", "swa_q_gain", "swa_k_gain", "swa_sinks",
         "mem_q_gain", "mem_k_gain", "ffn2_norm"]
LARGE = ["ffn1_gate", "ffn1_up", "ffn1_down", "w_in", "w_mem_k", "w_mem_v", "w_out", "ffn2_gate", "ffn2_up", "ffn2_down"]
GATHER_GROUPS = [["ffn1_gate", "ffn1_up"], ["ffn1_down", "w_in", "w_mem_k", "w_mem_v"], ["w_out", "ffn2_gate", "ffn2_up", "ffn2_down"]]
WEIGHTS = ["ffn1_norm", "ffn1_gate", "ffn1_up", "ffn1_down", "mix_norm", "mem_norm", "w_in", "forget_bias", "w_mem_k", "w_mem_v",
           "fox_q_gain", "fox_k_gain", "swa_q_gain", "swa_k_gain", "swa_sinks", "mem_q_gain", "mem_k_gain", "w_out", "ffn2_norm",
           "ffn2_gate", "ffn2_up", "ffn2_down"]


def _pad_proj_cols(w):
    out = jnp.zeros((w.shape[0], PROJ_W), w.dtype)
    for start, width, pstart in REF_GROUPS:
        out = lax.dynamic_update_slice(out, w[:, start:start + width], (0, pstart))
    return out


def _unpad_proj_cols(w):
    return jnp.concatenate([w[:, pstart:pstart + width] for _, width, pstart in REF_GROUPS], axis=1)


def _pack_small(vals):
    flat = jnp.concatenate([vals[k].reshape(-1).astype(F32) for k in SMALL + ["loss"]])
    n = flat.shape[0]
    total = -(-n // 1024) * 1024
    return jnp.pad(flat, (0, total - n)).reshape(total // 128, 128)


def _unpack_small(buf, shapes):
    flat = buf.reshape(-1)
    out, off = {}, 0
    for k in SMALL + ["loss"]:
        size = int(np.prod(shapes[k]))
        out[k] = flat[off:off + size].reshape(shapes[k])
        off += size
    return out


def kernel(x, mem, ffn1_norm, ffn1_gate, ffn1_up, ffn1_down, mix_norm, mem_norm, w_in, forget_bias, w_mem_k, w_mem_v, fox_q_gain, fox_k_gain, swa_q_gain, swa_k_gain, swa_sinks, mem_q_gain, mem_k_gain, w_out, ffn2_norm, ffn2_gate, ffn2_up, ffn2_down, loss_target, m_ffn1_norm, m_ffn1_gate, m_ffn1_up, m_ffn1_down, m_mix_norm, m_mem_norm, m_w_in, m_forget_bias, m_w_mem_k, m_w_mem_v, m_fox_q_gain, m_fox_k_gain, m_swa_q_gain, m_swa_k_gain, m_swa_sinks, m_mem_q_gain, m_mem_k_gain, m_w_out, m_ffn2_norm, m_ffn2_gate, m_ffn2_up, m_ffn2_down, v_ffn1_norm, v_ffn1_gate, v_ffn1_up, v_ffn1_down, v_mix_norm, v_mem_norm, v_w_in, v_forget_bias, v_w_mem_k, v_w_mem_v, v_fox_q_gain, v_fox_k_gain, v_swa_q_gain, v_swa_k_gain, v_swa_sinks, v_mem_q_gain, v_mem_k_gain, v_w_out, v_ffn2_norm, v_ffn2_gate, v_ffn2_up, v_ffn2_down):
    given = dict(locals())
    T, D = x.shape[1], x.shape[2]
    ML = mem.shape[1]
    xin = x.reshape(T, D)
    target = loss_target.reshape(T, D)
    memin = mem.reshape(ML, D)

    ids = _place_ids()
    shard = {k: given[k][0] for k in LARGE}
    started, after = [], ids
    for gi, group in enumerate(GATHER_GROUPS):
        if "w_in" in group:
            shard["w_in"] = _pad_proj_cols(shard["w_in"] + after[0, 0])
        placed = [_cast_place("cast_" + k, shard[k], ids, after) for k in group]
        send, recv, bufs, after = _gather_start("gather_start_%d" % gi, placed, after)
        started.append((send, recv, bufs))

    def arrive(gi, done):
        send, recv, bufs = started[gi]
        bufs = _gather_wait("gather_wait_%d" % gi, send, recv, bufs, done)
        return dict(zip(GATHER_GROUPS[gi], _gather_forward("gather_forward_%d" % gi, bufs)))

    gains = jnp.concatenate([fox_q_gain, fox_k_gain, swa_q_gain, swa_k_gain, mem_q_gain,
                             jnp.pad(forget_bias, ((0, 0), (0, HEAD_DIM - FOX_HEADS))), jnp.zeros((2, HEAD_DIM), F32)], axis=0)
    slopes_np = 2.0 ** (-8.0 * np.arange(1, SWA_HEADS + 1) / SWA_HEADS)
    slopes = jnp.asarray(np.repeat(slopes_np, WINDOW).reshape(SWA_KV_HEADS, GR, 1), F32)
    sinks = jnp.repeat(swa_sinks.reshape(SWA_HEADS), WINDOW).reshape(SWA_KV_HEADS, GR, 1)

    h1 = _rms_fwd("ffn1_norm_fwd", xin, ffn1_norm + after[0, 0])
    full = arrive(0, h1)
    wg1, wu1 = full["ffn1_gate"], full["ffn1_up"]
    fg1, fu1, a1 = _ffn_gu("ffn1_gate_up", h1, wg1, wu1)
    full = arrive(1, a1)
    wd1 = full["ffn1_down"].reshape(-1, D)
    win = full["w_in"].reshape(D, PROJ_W)
    wmk = full["w_mem_k"].reshape(D, MEM_HEADS * HEAD_DIM)
    wmv = full["w_mem_v"].reshape(D, MEM_HEADS * HEAD_DIM)
    x1, h2 = _residual_norm("ffn1_down", a1, wd1, xin, 0.5, mix_norm, 256)
    proj = _mm2d("proj_in", h2, win, "nn", F32, tn=1408, tk=2048, n_outer=True)
    qkv, logf, k_t, v_t = _prep_fwd(proj, gains)
    cum = _cumsum_rows("forget_cumsum", [logf], False)
    cum_h = cum[:, :FOX_HEADS].T
    cq_row = cum_h.reshape(FOX_HEADS, 1, T)
    ck_rep = jnp.broadcast_to(cum_h[:, :, None], (FOX_HEADS, T, HEAD_DIM))
    mn = _rms_fwd("mem_norm_fwd", memin, mem_norm)
    mk_raw = _mm2d("mem_k_proj", mn, wmk, "nn", F32)
    mv = _mm2d("mem_v_proj", mn, wmv, "nn", BF16)
    mk = _head_norm_rows(mk_raw, mem_k_gain)
    out_a, out_a_f32, lse_a = _fox_fwd(qkv, v_t, cq_row, ck_rep)
    out_b, lse_b = _swa_fwd(qkv, slopes, sinks)
    out_c, lse_c = _mem_fwd(qkv, mk, mv)
    mixed = jnp.concatenate([out_a, out_b, out_c], axis=1)
    full = arrive(2, mixed)
    wo = full["w_out"].reshape(-1, D)
    wg2, wu2, wd2 = full["ffn2_gate"], full["ffn2_up"], full["ffn2_down"].reshape(-1, D)
    x2, h3 = _residual_norm("mix_out", mixed, wo, x1, 1.0, ffn2_norm, 512)
    fg2, fu2, a2 = _ffn_gu("ffn2_gate_up", h3, wg2, wu2)
    dx3, dyb3, loss_blocks = _ffn_down_loss("ffn2_down", a2, wd2, x2, target)

    grads, small, res = {}, {"loss": jnp.sum(loss_blocks[::8, 0])}, {}

    def pair_off(tag, group):
        send, recv, own, lands, token = _pair_start("grad_pair_start_" + tag, [grads[k] for k in group])
        return (group, send, recv, own, lands), token

    def chip_off(tag, started, done):
        group, send, recv, own, lands = started
        own, theirs = _pair_wait("grad_pair_wait_" + tag, send, recv, own, lands, done)
        grads.update(zip(group, own))
        to_chips = [_pair_sum_bf16("pair_sum_" + k, grads[k], b, ids) for k, b in zip(group, theirs)]
        send, recv, parts, lands, token = _chip_start("grad_chip_start_" + tag, to_chips)
        return (group, theirs, send, recv, parts, lands), token

    def finish(tag, state, done):
        group, theirs, send, recv, parts, lands = state
        arrived = _chip_wait("grad_chip_wait_" + tag, send, recv, parts, lands, done)
        halves = [_chip_sum("chip_sum_" + k, grads[k], b, r, ids) for k, b, r in zip(group, theirs, arrived)]
        reduced = dict(zip(group, _pair_share("grad_pair_share_" + tag, halves)))
        last = None
        for k in group:
            if k == "w_in":
                gk = _unpad_proj_cols(reduced[k])
                d, mo, vo = _adamw("adamw_" + k, given[k][0], gk, given["m_" + k][0], given["v_" + k][0])
            else:
                d, mo, vo, gk = _adamw("adamw_" + k, given[k][0], reduced[k], given["m_" + k][0], given["v_" + k][0], emit_grad=True)
            res[k] = tuple(t[None] for t in (gk, d, mo, vo))
            last = vo
        return last

    dg2, du2 = _ffn_bwd_act("ffn2", dyb3, wd2, fg2, fu2, N_CHIPS)
    grads["ffn2_down"] = _ffn_bwd_down("ffn2", a2, dyb3, N_CHIPS).reshape(N_CHIPS, -1, D)
    grads["ffn2_gate"], grads["ffn2_up"] = _ffn_bwd_gate_up("ffn2", h3, dg2, du2, N_CHIPS)
    started, token = pair_off("a", ["ffn2_gate", "ffn2_up", "ffn2_down"])
    dh3 = _ffn_bwd_x("ffn2", dg2, du2, wg2, wu2, token)
    state_a, token = chip_off("a", started, dh3)
    dx2, dx2b, small["ffn2_norm"] = _rms_bwd("ffn2_norm_bwd", dh3, x2, ffn2_norm + token[0, 0], dx3, 1.0)
    dmix = _mm2d("mix_out_dx", dx2b, wo, "nt", BF16, tk=2048, n_outer=True)
    grads["w_out"] = _mm2d("mix_out_dw", mixed, dx2b, "tn", F32, tk=T, n_outer=True, resident=True).reshape(N_CHIPS, -1, D)
    delta_row = _fox_delta(dmix, out_a_f32)[:, :FOX_HEADS].T.reshape(FOX_HEADS, 1, T)
    dfq, dfk, dfv, dck, dcq = _fox_bwd(qkv, k_t, cq_row, ck_rep, delta_row, lse_a, dmix)
    dsq, dsk, dsv, dsink = _swa_bwd(qkv, slopes, sinks, out_b, lse_b, dmix)
    dmq, dmk, dmv = _mem_bwd(qkv, mk, mv, out_c, lse_c, dmix)
    small["swa_sinks"] = dsink[:, :SWA_GROUP, 0].reshape(1, SWA_HEADS)
    dcum = jnp.pad(dcq.reshape(FOX_HEADS, T).T, ((0, 0), (0, HEAD_DIM - FOX_HEADS)))
    dlogf = _cumsum_rows("forget_cumsum_bwd", [dcum], True, columns=dck)
    dproj, dgains = _prep_bwd(proj, gains, dfq, dfk, dfv, dsq, dsk, dsv, dmq, dlogf)
    for row, k in enumerate(["fox_q_gain", "fox_k_gain", "swa_q_gain", "swa_k_gain", "mem_q_gain"]):
        small[k] = dgains[row:row + 1, :]
    small["forget_bias"] = dgains[5:6, :FOX_HEADS]
    grads["w_in"] = _mm2d("proj_in_dw", h2, dproj, "tn", F32, tn=1408, tk=T, n_outer=True, resident=True).reshape(N_CHIPS, -1, PROJ_W)
    dmk_raw, small["mem_k_gain"] = _head_norm_rows_bwd(mk_raw, mem_k_gain, dmk)
    dmvb = dmv.astype(BF16)
    grads["w_mem_k"] = _mm2d("mem_k_dw", mn, dmk_raw, "tn", F32).reshape(N_CHIPS, -1, MEM_HEADS * HEAD_DIM)
    grads["w_mem_v"] = _mm2d("mem_v_dw", mn, dmvb, "tn", F32).reshape(N_CHIPS, -1, MEM_HEADS * HEAD_DIM)
    dmn = _mm2d("mem_k_dx", dmk_raw, wmk, "nt", F32)
    dmn = _mm2d("mem_v_dx", dmvb, wmv, "nt", F32, extras=[dmn], epilogue=lambda accs, ex: [ex[0] + accs[0]])
    _, _, small["mem_norm"] = _rms_bwd("mem_norm_bwd", dmn, memin, mem_norm, jnp.zeros_like(memin), 1.0)
    started, token = pair_off("b", ["w_out", "w_in", "w_mem_k", "w_mem_v"])
    dh2 = _mm2d("proj_in_dx", dproj, win, "nt", F32, tm=1024, tk=1408, after=token)
    state_b, token = chip_off("b", started, dh2)
    dx1, dyb1, small["mix_norm"] = _rms_bwd("mix_norm_bwd", dh2, x1, mix_norm + token[0, 0], dx2, 0.5)
    grads["ffn1_down"] = _ffn_bwd_down("ffn1", a1, dyb1, N_CHIPS).reshape(N_CHIPS, -1, D)
    started, token = pair_off("c", ["ffn1_down"])
    dg1, du1 = _ffn_bwd_act("ffn1", dyb1, wd1, fg1, fu1, N_CHIPS, after=token)
    state_c, token = chip_off("c", started, dg1)
    grads["ffn1_gate"], grads["ffn1_up"] = _ffn_bwd_gate_up("ffn1", h1, dg1, du1, N_CHIPS, after=token)
    started, token = pair_off("d", ["ffn1_gate", "ffn1_up"])
    dh1 = _ffn_bwd_x("ffn1", dg1, du1, wg1, wu1, token)
    state_d, token = chip_off("d", started, dh1)
    grad_x, _, small["ffn1_norm"] = _rms_bwd("ffn1_norm_bwd", dh1, xin, ffn1_norm + token[0, 0], dx1, 1.0)

    s_send, s_recv, s_buf, s_land, token = _small_start(_pack_small(small))

    done = finish("a", state_a, token)
    done = finish("b", state_b, done)
    done = finish("c", state_c, done)
    done = finish("d", state_d, done)

    shapes = {k: given[k].shape for k in SMALL}
    shapes["loss"] = ()
    s_buf, s_land = _small_wait(s_send, s_recv, s_buf, s_land, done)
    red_small = _unpack_small(_small_sum(s_buf, s_land), shapes)
    loss = red_small["loss"]
    zero = {"loss": jnp.zeros((), F32)}
    packed = [_pack_small({**zero, **{k: src[k] for k in SMALL}}) for src in (
        {k: given[k] for k in SMALL}, red_small, {k: given["m_" + k] for k in SMALL}, {k: given["v_" + k] for k in SMALL})]
    d_s, m_s, v_s = (_unpack_small(t, shapes) for t in _adamw("adamw_small", *packed))
    for k in SMALL:
        res[k] = (red_small[k], d_s[k], m_s[k], v_s[k])

    outs = [loss, grad_x.reshape(1, T, D)]
    for part in range(4):
        outs += [res[k][part] for k in WEIGHTS]
    return tuple(outs)
```

```python
import functools

import numpy as np
import jax
import jax.numpy as jnp
from jax import lax
from jax.experimental import pallas as pl
from jax.experimental.pallas import tpu as pltpu

F32 = jnp.float32
BF16 = jnp.bfloat16
MESH = pl.DeviceIdType.MESH

HEAD_DIM = 128
FOX_HEADS = 6
SWA_HEADS = 6
SWA_KV_HEADS = 2
SWA_GROUP = SWA_HEADS // SWA_KV_HEADS
MEM_HEADS = 4
WINDOW = 128
EPS = 1e-6
NEG_INF = -1e30
SCALE = HEAD_DIM ** -0.5

C_FQ = 0
C_FK = C_FQ + FOX_HEADS * HEAD_DIM
C_FV = C_FK + FOX_HEADS * HEAD_DIM
C_SQ = C_FV + FOX_HEADS * HEAD_DIM
C_SK = C_SQ + SWA_HEADS * HEAD_DIM
C_SV = C_SK + SWA_KV_HEADS * HEAD_DIM
C_MQ = C_SV + SWA_KV_HEADS * HEAD_DIM
C_FL = C_MQ + MEM_HEADS * HEAD_DIM
PROJ_W = C_FL + HEAD_DIM
FOX_W = FOX_HEADS * HEAD_DIM
REF_GROUPS = [
    (0, FOX_W, C_FQ), (FOX_W, FOX_W, C_FK), (2 * FOX_W, FOX_W, C_FV), (3 * FOX_W, FOX_HEADS, C_FL),
    (3 * FOX_W + FOX_HEADS, SWA_HEADS * HEAD_DIM, C_SQ),
    (3 * FOX_W + FOX_HEADS + SWA_HEADS * HEAD_DIM, SWA_KV_HEADS * HEAD_DIM, C_SK),
    (3 * FOX_W + FOX_HEADS + (SWA_HEADS + SWA_KV_HEADS) * HEAD_DIM, SWA_KV_HEADS * HEAD_DIM, C_SV),
    (3 * FOX_W + FOX_HEADS + (SWA_HEADS + 2 * SWA_KV_HEADS) * HEAD_DIM, MEM_HEADS * HEAD_DIM, C_MQ),
]

ADAM_LR = 0.001
ADAM_B1 = 0.9
ADAM_B2 = 0.999
ADAM_EPS = 1e-08
ADAM_WD = 0.01
ADAM_STEP = 10

V7X_VMEM_LIMIT = 56 * 1024 * 1024
N_CHIPS = 4
N_DEV = 8


def _tile(n, pref, mult=128):
    t = (min(pref, n) // mult) * mult
    while t >= mult:
        if n % t == 0:
            return t
        t -= mult
    return n


def _params(sem):
    return pltpu.CompilerParams(dimension_semantics=sem, vmem_limit_bytes=V7X_VMEM_LIMIT)


_DIMS = {"nn": (((1,), (0,)), ((), ())), "nt": (((1,), (1,)), ((), ())), "tn": (((0,), (0,)), ((), ()))}


def _dot(a, b, mode):
    return lax.dot_general(a, b, _DIMS[mode], preferred_element_type=F32)


def _mm(name, grid, pairs, acc_of, acc_shapes, extras, outs, epilogue, after=None):
    n_p, n_e, n_o, n_a = len(pairs), len(extras), len(outs), len(acc_shapes)
    n_w = 0 if after is None else 1
    nk = grid[2]
    n_in = sum(1 if a is None else 2 for a, *_ in pairs)

    def body(*refs):
        ex = refs[n_in:n_in + n_e]
        out = refs[n_in + n_e + n_w:n_in + n_e + n_w + n_o]
        accs = refs[n_in + n_e + n_w + n_o:]
        parts = [None] * n_a
        at = 0
        for p in range(n_p):
            if pairs[p][0] is None:
                a_ref, b_ref = refs[0], refs[at]
                at += 1
            else:
                a_ref, b_ref = refs[at], refs[at + 1]
                at += 2
            d = _dot(a_ref[...], b_ref[...], pairs[p][4])
            parts[acc_of[p]] = d if parts[acc_of[p]] is None else parts[acc_of[p]] + d

        def finish(vals):
            for o, r in zip(out, epilogue(vals, [e[...] for e in ex])):
                o[...] = r.astype(o.dtype)

        if nk == 1:
            finish(parts)
            return
        k = pl.program_id(2)

        @pl.when(k == 0)
        def _():
            for a, d in zip(accs, parts):
                a[...] = d

        @pl.when((k > 0) & (k < nk - 1))
        def _():
            for a, d in zip(accs, parts):
                a[...] += d

        @pl.when(k == nk - 1)
        def _():
            finish([a[...] + d for a, d in zip(accs, parts)])

    in_specs, args = [], []
    for a, a_spec, b, b_spec, _ in pairs:
        if a is not None:
            in_specs.append(a_spec)
            args.append(a)
        in_specs.append(b_spec)
        args.append(b)
    for e, e_spec in extras:
        in_specs.append(e_spec)
        args.append(e)
    if after is not None:
        in_specs.append(pl.BlockSpec(memory_space=pl.ANY))
        args.append(after)
    res = pl.pallas_call(
        body, name=name, grid=grid, in_specs=in_specs,
        out_specs=[s for _, s in outs], out_shape=[o for o, _ in outs],
        scratch_shapes=[pltpu.VMEM(s, F32) for s in acc_shapes] if nk > 1 else [],
        compiler_params=_params(("parallel", "parallel", "arbitrary")),
    )(*args)
    return res


def _mm2d(name, a, b, mode, out_dtype, tm=512, tn=1024, tk=1024, extras=(), epilogue=None, n_out=1, after=None, n_outer=False,
          resident=False, rows=(), tile_stat=False):
    if mode == "nn":
        (M, K), N = a.shape, b.shape[1]
    elif mode == "nt":
        (M, K), N = a.shape, b.shape[0]
    else:
        (K, M), N = a.shape, b.shape[1]
    tm, tn, tk = _tile(M, tm), _tile(N, tn), _tile(K, tk)
    assert not resident or tk == K

    def spec(shape, index, single=False):
        mode_kw = {"pipeline_mode": pl.Buffered(1)} if single else {}
        if n_outer:
            return pl.BlockSpec(shape, lambda j, i, k: index(i, j, k), **mode_kw)
        return pl.BlockSpec(shape, index, **mode_kw)

    single_a, single_b = resident and not n_outer, resident and n_outer
    a_spec = spec((tk, tm), lambda i, j, k: (k, i), single_a) if mode == "tn" else spec((tm, tk), lambda i, j, k: (i, k), single_a)
    b_spec = spec((tn, tk), lambda i, j, k: (j, k), single_b) if mode == "nt" else spec((tk, tn), lambda i, j, k: (k, j), single_b)
    mn = spec((tm, tn), lambda i, j, k: (i, j))
    if epilogue is None:
        epilogue = lambda accs, ex: [accs[0]]
    if not isinstance(out_dtype, (list, tuple)):
        out_dtype = [out_dtype] * n_out
    grid = (N // tn, M // tm, K // tk) if n_outer else (M // tm, N // tn, K // tk)
    outs = [(jax.ShapeDtypeStruct((M, N), d), mn) for d in out_dtype]
    if tile_stat:
        assert tn == N
        outs.append((jax.ShapeDtypeStruct((8 * (M // tm), 128), F32), spec((8, 128), lambda i, j, k: (i, 0))))
    res = _mm(name, grid, [(a, a_spec, b, b_spec, mode)], [0], [(tm, tn)],
              [(e, mn) for e in extras] + [(r, spec((1, tn), lambda i, j, k: (0, j))) for r in rows], outs, epilogue, after=after)
    return res[0] if len(res) == 1 else res


def _sigmoid(x):
    return 1.0 / (1.0 + jnp.exp(-x))


def _sigmoid_fast(x):
    return pl.reciprocal(1.0 + jnp.exp(-x), approx=True)


def _ffn_gu(name, h, wg, wu):
    T, D = h.shape
    nf, _, F4 = wg.shape
    tm, tk = _tile(T, 512), _tile(D, 2048)
    a_spec = pl.BlockSpec((tm, tk), lambda j, i, k: (i, k))
    b_spec = pl.BlockSpec((None, tk, F4), lambda j, i, k: (j, k, 0))
    o_spec = pl.BlockSpec((tm, F4), lambda j, i, k: (i, j))

    def epilogue(accs, ex):
        g, u = accs
        s = _sigmoid_fast(g)
        gs = g * s
        return [(s + s * (g - gs)) * u, gs, gs * u]

    sds = jax.ShapeDtypeStruct((T, nf * F4), BF16)
    return _mm(name, (nf, T // tm, D // tk), [(h, a_spec, wg, b_spec, "nn"), (None, None, wu, b_spec, "nn")], [0, 1],
               [(tm, F4), (tm, F4)], [], [(sds, o_spec)] * 3, epilogue)


def _rms_rows(x, gain):
    return x * lax.rsqrt(jnp.mean(x * x, axis=-1, keepdims=True) + EPS) * gain


def _residual_norm(name, a, w, xres, scale, gain, tm):
    def epilogue(accs, ex):
        y = ex[0] + scale * accs[0]
        return [y, _rms_rows(y, ex[1])]

    return _mm2d(name, a, w, "nn", [F32, BF16], tm=tm, tn=w.shape[1], tk=w.shape[0], n_outer=True, resident=True, extras=[xres],
                 rows=[gain], epilogue=epilogue)


def _ffn_down_loss(name, a, wd, xres, target):
    D = wd.shape[1]

    def epilogue(accs, ex):
        e = ex[0] + 0.5 * accs[0] - ex[1]
        d = e * (1.0 / D)
        return [d, 0.5 * d, jnp.zeros((8, 128), F32) + (0.5 / D) * jnp.sum(e * e)]

    return _mm2d(name, a, wd, "nn", [F32, BF16], tm=256, tn=D, tk=wd.shape[0], n_outer=True, resident=True, extras=[xres, target],
                 tile_stat=True, epilogue=epilogue)


def _ffn_bwd_down(tag, a, dyb, nf, after=None):
    return _mm2d(tag + "_dwd", a, dyb, "tn", F32, tm=a.shape[1] // nf, tn=1024, tk=a.shape[0], resident=True, after=after)


def _ffn_bwd_act(tag, dyb, wd, da_dg, da_du, nf, after=None):
    def act_bwd(accs, ex):
        return [accs[0] * ex[0].astype(F32), accs[0] * ex[1].astype(F32)]

    return _mm2d(tag + "_da", dyb, wd, "nt", BF16, tn=wd.shape[0] // nf, tk=2048, extras=[da_dg, da_du], epilogue=act_bwd, n_out=2,
                 n_outer=True, after=after)


def _ffn_bwd_gate_up(tag, h, dg, du, nf, after=None):
    T, D = h.shape
    F4 = dg.shape[1] // nf
    tm = _tile(D, 512)
    h_spec = pl.BlockSpec((T, tm), lambda j, i, k: (0, i))
    d_spec = pl.BlockSpec((T, F4), lambda j, i, k: (0, j), pipeline_mode=pl.Buffered(1))
    w_spec = pl.BlockSpec((None, tm, F4), lambda j, i, k: (j, i, 0))
    sds = jax.ShapeDtypeStruct((nf, D, F4), F32)
    return _mm(tag + "_dwgu", (nf, D // tm, 1), [(h, h_spec, dg, d_spec, "tn"), (None, None, du, d_spec, "tn")],
               [0, 1], [(tm, F4), (tm, F4)], [], [(sds, w_spec)] * 2, lambda accs, ex: accs, after=after)


def _ffn_bwd_x(tag, dg, du, wg, wu, after):
    T = dg.shape[0]
    nf, D, F4 = wg.shape
    tm, tn = _tile(T, 512), _tile(D, 512)

    def body(dg_ref, du_ref, wg_ref, wu_ref, after_ref, o_ref):
        acc = None
        for j in range(nf):
            cols = slice(j * F4, (j + 1) * F4)
            part = _dot(dg_ref[:, cols], wg_ref[j], "nt") + _dot(du_ref[:, cols], wu_ref[j], "nt")
            acc = part if acc is None else acc + part
        o_ref[...] = acc

    a_spec = pl.BlockSpec((tm, nf * F4), lambda n, i: (i, 0))
    b_spec = pl.BlockSpec((nf, tn, F4), lambda n, i: (0, n, 0), pipeline_mode=pl.Buffered(1))
    return pl.pallas_call(
        body, name=tag + "_dh", grid=(D // tn, T // tm),
        in_specs=[a_spec, a_spec, b_spec, b_spec, pl.BlockSpec(memory_space=pl.ANY)],
        out_specs=pl.BlockSpec((tm, tn), lambda n, i: (i, n)), out_shape=jax.ShapeDtypeStruct((T, D), F32),
        compiler_params=_params(("parallel", "parallel")),
    )(dg, du, wg, wu, after)


def _rms_fwd(name, x, gain):
    R, D = x.shape
    tr = _tile(R, 256, 8)

    def body(x_ref, g_ref, o_ref):
        xv = x_ref[...]
        r = lax.rsqrt(jnp.mean(xv * xv, axis=-1, keepdims=True) + EPS)
        o_ref[...] = (xv * r * g_ref[...]).astype(BF16)

    return pl.pallas_call(
        body, name=name, grid=(R // tr,),
        in_specs=[pl.BlockSpec((tr, D), lambda i: (i, 0)), pl.BlockSpec((1, D), lambda i: (0, 0))],
        out_specs=pl.BlockSpec((tr, D), lambda i: (i, 0)), out_shape=jax.ShapeDtypeStruct((R, D), BF16),
        compiler_params=_params(("parallel",)),
    )(x, gain)


def _rms_bwd(name, dh, x, gain, dres, bscale):
    R, D = x.shape
    tr = _tile(R, 256, 8)

    def body(dh_ref, x_ref, g_ref, dres_ref, dx_ref, dxb_ref, dg_ref):
        xv, dy = x_ref[...], dh_ref[...]
        r = lax.rsqrt(jnp.mean(xv * xv, axis=-1, keepdims=True) + EPS)
        xn = xv * r
        uu = dy * g_ref[...]
        dx = dres_ref[...] + r * (uu - xn * jnp.mean(xn * uu, axis=-1, keepdims=True))
        dx_ref[...] = dx
        dxb_ref[...] = (bscale * dx).astype(BF16)
        part = jnp.sum(dy * xn, axis=0, keepdims=True)

        @pl.when(pl.program_id(0) == 0)
        def _():
            dg_ref[...] = part

        @pl.when(pl.program_id(0) > 0)
        def _():
            dg_ref[...] += part

    row = pl.BlockSpec((tr, D), lambda i: (i, 0))
    vec = pl.BlockSpec((1, D), lambda i: (0, 0))
    return pl.pallas_call(
        body, name=name, grid=(R // tr,), in_specs=[row, row, vec, row], out_specs=[row, row, vec],
        out_shape=[jax.ShapeDtypeStruct((R, D), F32), jax.ShapeDtypeStruct((R, D), BF16), jax.ShapeDtypeStruct((1, D), F32)],
        compiler_params=_params(("arbitrary",)),
    )(dh, x, gain, dres)


def _head_norm(xs, g):
    r = lax.rsqrt(jnp.mean(xs * xs, axis=-1, keepdims=True) + EPS)
    return xs * r * g


def _head_norm_bwd(xs, g, dy):
    r = lax.rsqrt(jnp.mean(xs * xs, axis=-1, keepdims=True) + EPS)
    xn = xs * r
    uu = dy * g
    return r * (uu - xn * jnp.mean(xn * uu, axis=-1, keepdims=True)), jnp.sum(dy * xn, axis=0, keepdims=True)


NORMED = [(C_FQ, FOX_HEADS, 0), (C_FK, FOX_HEADS, 1), (C_SQ, SWA_HEADS, 2), (C_SK, SWA_KV_HEADS, 3), (C_MQ, MEM_HEADS, 4)]
PLAIN = [(C_FV, FOX_HEADS), (C_SV, SWA_KV_HEADS)]


def _prep_fwd(proj, gains):
    T = proj.shape[0]
    tr = _tile(T, 256, 128)

    def body(p_ref, g_ref, o_ref, lf_ref, kt_ref, vt_ref):
        for start, heads, row in NORMED:
            gn = g_ref[row:row + 1, :]
            for hh in range(heads):
                sl = slice(start + hh * HEAD_DIM, start + (hh + 1) * HEAD_DIM)
                y = _head_norm(p_ref[:, sl], gn)
                o_ref[:, sl] = y.astype(BF16)
                if start == C_FK:
                    kt_ref[hh * HEAD_DIM:(hh + 1) * HEAD_DIM, :] = y.T.astype(BF16)
        for start, heads in PLAIN:
            sl = slice(start, start + heads * HEAD_DIM)
            o_ref[:, sl] = p_ref[:, sl].astype(BF16)
        for hh in range(FOX_HEADS):
            sl = slice(C_FV + hh * HEAD_DIM, C_FV + (hh + 1) * HEAD_DIM)
            vt_ref[hh * HEAD_DIM:(hh + 1) * HEAD_DIM, :] = p_ref[:, sl].T.astype(BF16)
        zb = p_ref[:, C_FL:C_FL + HEAD_DIM] + g_ref[5:6, :]
        o_ref[:, C_FL:C_FL + HEAD_DIM] = jnp.zeros((tr, HEAD_DIM), BF16)
        lf_ref[...] = jnp.minimum(zb, 0.0) - jnp.log(1.0 + jnp.exp(-jnp.abs(zb)))

    return pl.pallas_call(
        body, name="prep_fwd", grid=(T // tr,),
        in_specs=[pl.BlockSpec((tr, PROJ_W), lambda i: (i, 0)), pl.BlockSpec((8, 128), lambda i: (0, 0))],
        out_specs=[pl.BlockSpec((tr, PROJ_W), lambda i: (i, 0)), pl.BlockSpec((tr, HEAD_DIM), lambda i: (i, 0)),
                   pl.BlockSpec((FOX_W, tr), lambda i: (0, i)), pl.BlockSpec((FOX_W, tr), lambda i: (0, i))],
        out_shape=[jax.ShapeDtypeStruct((T, PROJ_W), BF16), jax.ShapeDtypeStruct((T, HEAD_DIM), F32),
                   jax.ShapeDtypeStruct((FOX_W, T), BF16), jax.ShapeDtypeStruct((FOX_W, T), BF16)],
        compiler_params=_params(("parallel",)),
    )(proj, gains)


def _prep_bwd(proj, gains, dfq, dfk, dfv, dsq, dsk, dsv, dmq, dlogf):
    T = proj.shape[0]
    tr = _tile(T, 256, 8)
    d_normed = {C_FQ: 0, C_FK: 1, C_SQ: 3, C_SK: 4, C_MQ: 6}
    d_plain = {C_FV: 2, C_SV: 5}

    def body(p_ref, g_ref, *rest):
        d_refs, dlf_ref, o_ref, dg_ref = rest[:7], rest[7], rest[8], rest[9]
        rows = []
        for start, heads, row in NORMED:
            gn = g_ref[row:row + 1, :]
            d_ref = d_refs[d_normed[start]]
            tot = jnp.zeros((1, HEAD_DIM), F32)
            for hh in range(heads):
                sl = slice(start + hh * HEAD_DIM, start + (hh + 1) * HEAD_DIM)
                dx, dgn = _head_norm_bwd(p_ref[:, sl], gn, d_ref[:, hh * HEAD_DIM:(hh + 1) * HEAD_DIM])
                o_ref[:, sl] = dx.astype(BF16)
                tot = tot + dgn
            rows.append(tot)
        for start, heads in PLAIN:
            o_ref[:, start:start + heads * HEAD_DIM] = d_refs[d_plain[start]][...].astype(BF16)
        zb = p_ref[:, C_FL:C_FL + HEAD_DIM] + g_ref[5:6, :]
        lane = lax.broadcasted_iota(jnp.int32, (tr, HEAD_DIM), 1)
        dz = jnp.where(lane < FOX_HEADS, dlf_ref[...] * (1.0 - _sigmoid(zb)), 0.0)
        o_ref[:, C_FL:C_FL + HEAD_DIM] = dz.astype(BF16)
        rows.append(jnp.sum(dz, axis=0, keepdims=True))
        part = jnp.concatenate(rows + [jnp.zeros((2, HEAD_DIM), F32)], axis=0)

        @pl.when(pl.program_id(0) == 0)
        def _():
            dg_ref[...] = part

        @pl.when(pl.program_id(0) > 0)
        def _():
            dg_ref[...] += part

    def rows_of(w):
        return pl.BlockSpec((tr, w), lambda i: (i, 0))

    small = pl.BlockSpec((8, 128), lambda i: (0, 0))
    ds = [dfq, dfk, dfv, dsq, dsk, dsv, dmq]
    return pl.pallas_call(
        body, name="prep_bwd", grid=(T // tr,),
        in_specs=[rows_of(PROJ_W), small] + [rows_of(d.shape[1]) for d in ds] + [rows_of(HEAD_DIM)],
        out_specs=[rows_of(PROJ_W), small],
        out_shape=[jax.ShapeDtypeStruct((T, PROJ_W), BF16), jax.ShapeDtypeStruct((8, 128), F32)],
        compiler_params=_params(("arbitrary",)),
    )(proj, gains, *ds, dlogf)


def _head_norm_rows(x, gain):
    R, W = x.shape

    def body(x_ref, g_ref, o_ref):
        for hh in range(W // HEAD_DIM):
            sl = slice(hh * HEAD_DIM, (hh + 1) * HEAD_DIM)
            o_ref[:, sl] = _head_norm(x_ref[:, sl], g_ref[...]).astype(BF16)

    return pl.pallas_call(body, name="mem_k_norm", out_shape=jax.ShapeDtypeStruct((R, W), BF16))(x, gain)


def _head_norm_rows_bwd(x, gain, dy):
    R, W = x.shape

    def body(x_ref, g_ref, dy_ref, dx_ref, dg_ref):
        tot = jnp.zeros((1, HEAD_DIM), F32)
        for hh in range(W // HEAD_DIM):
            sl = slice(hh * HEAD_DIM, (hh + 1) * HEAD_DIM)
            dx, dgn = _head_norm_bwd(x_ref[:, sl], g_ref[...], dy_ref[:, sl])
            dx_ref[:, sl] = dx.astype(BF16)
            tot = tot + dgn
        dg_ref[...] = tot

    return pl.pallas_call(
        body, name="mem_k_norm_bwd",
        out_shape=[jax.ShapeDtypeStruct((R, W), BF16), jax.ShapeDtypeStruct((1, HEAD_DIM), F32)])(x, gain, dy)


def _cumsum_rows(name, xs, reverse, columns=None):
    T, W = xs[0].shape
    tb = _tile(T, 512, 8)
    nb = T // tb
    n_in = len(xs) + (0 if columns is None else 1)

    def body(*refs):
        o_ref, carry = refs[n_in], refs[n_in + 1]

        @pl.when(pl.program_id(0) == 0)
        def _():
            carry[...] = jnp.zeros_like(carry)

        xv = refs[0][...]
        for x_ref in refs[1:len(xs)]:
            xv = xv + x_ref[...]
        if columns is not None:
            lane = lax.broadcasted_iota(jnp.int32, (tb, W), 1)
            for hh in range(columns.shape[0]):
                xv = xv + jnp.where(lane == hh, refs[len(xs)][hh], 0.0)
        r = lax.broadcasted_iota(jnp.int32, (tb, tb), 0)
        cc = lax.broadcasted_iota(jnp.int32, (tb, tb), 1)
        tri = jnp.where((cc >= r) if reverse else (cc <= r), 1.0, 0.0).astype(F32)
        o_ref[...] = jnp.dot(tri, xv, precision=lax.Precision.HIGHEST, preferred_element_type=F32) + carry[...]
        carry[...] += jnp.sum(xv, axis=0, keepdims=True)

    idx = (lambda i: (nb - 1 - i, 0)) if reverse else (lambda i: (i, 0))
    in_specs = [pl.BlockSpec((tb, W), idx)] * len(xs)
    if columns is not None:
        in_specs.append(pl.BlockSpec((columns.shape[0], tb, 1), lambda i: (0, idx(i)[0], 0)))
    return pl.pallas_call(
        body, name=name, grid=(nb,), in_specs=in_specs, out_specs=pl.BlockSpec((tb, W), idx),
        out_shape=jax.ShapeDtypeStruct((T, W), F32), scratch_shapes=[pltpu.VMEM((1, W), F32)],
        compiler_params=_params(("arbitrary",)),
    )(*xs, *([] if columns is None else [columns]))


def _triangle(nq, by_column):
    if by_column:
        blocks = [(i, j) for j in range(nq) for i in range(j, nq)]
    else:
        blocks = [(i, j) for i in range(nq) for j in range(i + 1)]
    return jnp.asarray(np.array(blocks, np.int32).T)


def _fox_scores_t(k, q, cq_row, ck_rep, on_diagonal):
    n = q.shape[0]
    s = _dot(k, q, "nt") * SCALE + (cq_row - jnp.tile(ck_rep, (1, n // HEAD_DIM)))
    if on_diagonal:
        s = jnp.where(lax.broadcasted_iota(jnp.int32, (n, n), 0) <= lax.broadcasted_iota(jnp.int32, (n, n), 1), s, NEG_INF)
    return s


def _fox_fwd(qkv, v_t, cq_row, ck_rep):
    T = qkv.shape[0]
    tq = _tile(T, 512)
    nq = T // tq
    steps = nq * (nq + 1) // 2
    HQ, HK = C_FQ // HEAD_DIM, C_FK // HEAD_DIM

    def body(tab, q_ref, k_ref, vt_ref, cq_ref, ck_ref, o_ref, of_ref, lse_ref, m_sc, l_sc, acc_sc):
        i, j = tab[0, pl.program_id(1)], tab[1, pl.program_id(1)]

        @pl.when(j == 0)
        def _():
            m_sc[...] = jnp.full_like(m_sc, NEG_INF)
            l_sc[...] = jnp.zeros_like(l_sc)
            acc_sc[...] = jnp.zeros_like(acc_sc)

        def step(on_diagonal):
            s = _fox_scores_t(k_ref[...], q_ref[...], cq_ref[...], ck_ref[...], on_diagonal)
            m_new = jnp.maximum(m_sc[...], jnp.max(s, axis=0, keepdims=True))
            alpha = jnp.exp(m_sc[...] - m_new)
            p = jnp.exp(s - m_new)
            l_sc[...] = alpha * l_sc[...] + jnp.sum(p, axis=0, keepdims=True)
            acc_sc[...] = alpha * acc_sc[...] + _dot(vt_ref[...], p.astype(BF16), "nn")
            m_sc[...] = m_new

        @pl.when(j < i)
        def _():
            step(False)

        @pl.when(j == i)
        def _():
            step(True)
            o = (acc_sc[...] / l_sc[...]).T
            o_ref[...] = o.astype(BF16)
            of_ref[...] = o
            lse_ref[...] = m_sc[...] + jnp.log(l_sc[...])

    qrow = pl.BlockSpec((None, 1, tq), lambda h, s, tab: (h, 0, tab[0, s]))
    return pl.pallas_call(
        body, name="fox_fwd",
        grid_spec=pltpu.PrefetchScalarGridSpec(
            num_scalar_prefetch=1, grid=(FOX_HEADS, steps),
            in_specs=[pl.BlockSpec((tq, HEAD_DIM), lambda h, s, tab: (tab[0, s], HQ + h)),
                      pl.BlockSpec((tq, HEAD_DIM), lambda h, s, tab: (tab[1, s], HK + h)),
                      pl.BlockSpec((HEAD_DIM, tq), lambda h, s, tab: (h, tab[1, s])), qrow,
                      pl.BlockSpec((None, tq, HEAD_DIM), lambda h, s, tab: (h, tab[1, s], 0))],
            out_specs=[pl.BlockSpec((tq, HEAD_DIM), lambda h, s, tab: (tab[0, s], h)),
                       pl.BlockSpec((tq, HEAD_DIM), lambda h, s, tab: (tab[0, s], h)), qrow],
            scratch_shapes=[pltpu.VMEM((1, tq), F32), pltpu.VMEM((1, tq), F32), pltpu.VMEM((HEAD_DIM, tq), F32)]),
        out_shape=[jax.ShapeDtypeStruct((T, FOX_W), BF16), jax.ShapeDtypeStruct((T, FOX_W), F32),
                   jax.ShapeDtypeStruct((FOX_HEADS, 1, T), F32)],
        compiler_params=_params(("parallel", "arbitrary")),
    )(_triangle(nq, False), qkv, qkv, v_t, cq_row, ck_rep)


def _fox_delta(dmix, out_f32):
    T = out_f32.shape[0]
    tr = _tile(T, 512, 8)

    def body(do_ref, o_ref, d_ref):
        lane = lax.broadcasted_iota(jnp.int32, (tr, HEAD_DIM), 1)
        acc = jnp.zeros((tr, HEAD_DIM), F32)
        for hh in range(FOX_HEADS):
            sl = slice(hh * HEAD_DIM, (hh + 1) * HEAD_DIM)
            d = jnp.sum(do_ref[:, sl].astype(F32) * o_ref[:, sl], axis=-1, keepdims=True)
            acc = jnp.where(lane == hh, d, acc)
        d_ref[...] = acc

    blk = pl.BlockSpec((tr, FOX_W), lambda i: (i, 0))
    return pl.pallas_call(
        body, name="fox_delta", grid=(T // tr,), in_specs=[blk, blk], out_specs=pl.BlockSpec((tr, HEAD_DIM), lambda i: (i, 0)),
        out_shape=jax.ShapeDtypeStruct((T, HEAD_DIM), F32), compiler_params=_params(("parallel",)),
    )(dmix, out_f32)


def _fox_bwd(qkv, k_t, cq_row, ck_rep, delta_row, lse, dmix):
    T = qkv.shape[0]
    tq = _tile(T, 512)
    nq = T // tq
    steps = nq * (nq + 1) // 2
    HQ, HK, HV = C_FQ // HEAD_DIM, C_FK // HEAD_DIM, C_FV // HEAD_DIM

    def body(tab, q_ref, k_ref, kt_ref, v_ref, cq_ref, ck_ref, delta_ref, lse_ref, do_ref,
             dq_ref, dk_ref, dv_ref, dck_ref, dcq_ref, dk_sc, dv_sc, dc_sc, dqt_sc):
        qi, kj = tab[0, pl.program_id(1)], tab[1, pl.program_id(1)]

        @pl.when(qi == kj)
        def _():
            dk_sc[...] = jnp.zeros_like(dk_sc)
            dv_sc[...] = jnp.zeros_like(dv_sc)
            dc_sc[...] = jnp.zeros_like(dc_sc)

        def step(on_diagonal):
            q, k, v, do = q_ref[...], k_ref[...], v_ref[...], do_ref[...]
            p = jnp.exp(_fox_scores_t(k, q, cq_ref[...], ck_ref[...], on_diagonal) - lse_ref[...])
            dp = _dot(v, do, "nt")
            ds = p * (dp - delta_ref[...])
            dsb = ds.astype(BF16)
            dv_sc[...] += _dot(p.astype(BF16), do, "nn")
            dk_sc[...] += _dot(dsb, q, "nn")
            dc_sc[...] += jnp.sum(ds, axis=1, keepdims=True)
            dq_part = _dot(kt_ref[...], dsb, "nn") * SCALE
            dcq_part = jnp.sum(ds, axis=0, keepdims=True)

            @pl.when(kj == 0)
            def _():
                dqt_sc[qi] = dq_part
                dcq_ref[qi] = dcq_part

            @pl.when(kj > 0)
            def _():
                dqt_sc[qi] += dq_part
                dcq_ref[qi] += dcq_part

            if on_diagonal:
                dq_ref[...] = dqt_sc[qi].T

        @pl.when(qi > kj)
        def _():
            step(False)

        @pl.when(qi == kj)
        def _():
            step(True)

        @pl.when(qi == nq - 1)
        def _():
            dk_ref[...] = dk_sc[...] * SCALE
            dv_ref[...] = dv_sc[...]
            dck_ref[...] = -dc_sc[...]

    def rows(base):
        return pl.BlockSpec((tq, HEAD_DIM), lambda h, s, tab: (tab[0, s], base + h))

    def cols(base):
        return pl.BlockSpec((tq, HEAD_DIM), lambda h, s, tab: (tab[1, s], base + h))

    qrow = pl.BlockSpec((None, 1, tq), lambda h, s, tab: (h, 0, tab[0, s]))
    sds = jax.ShapeDtypeStruct((T, FOX_W), F32)
    return pl.pallas_call(
        body, name="fox_bwd",
        grid_spec=pltpu.PrefetchScalarGridSpec(
            num_scalar_prefetch=1, grid=(FOX_HEADS, steps),
            in_specs=[rows(HQ), cols(HK), pl.BlockSpec((HEAD_DIM, tq), lambda h, s, tab: (h, tab[1, s])), cols(HV), qrow,
                      pl.BlockSpec((None, tq, HEAD_DIM), lambda h, s, tab: (h, tab[1, s], 0)), qrow, qrow, rows(0)],
            out_specs=[cols(0), cols(0), cols(0), pl.BlockSpec((None, tq, 1), lambda h, s, tab: (h, tab[1, s], 0)),
                       pl.BlockSpec((None, nq, 1, tq), lambda h, s, tab: (h, 0, 0, 0))],
            scratch_shapes=[pltpu.VMEM((tq, HEAD_DIM), F32), pltpu.VMEM((tq, HEAD_DIM), F32), pltpu.VMEM((tq, 1), F32),
                            pltpu.VMEM((nq, HEAD_DIM, tq), F32)]),
        out_shape=[sds, sds, sds, jax.ShapeDtypeStruct((FOX_HEADS, T, 1), F32), jax.ShapeDtypeStruct((FOX_HEADS, nq, 1, tq), F32)],
        compiler_params=_params(("parallel", "arbitrary")),
    )(_triangle(nq, True), qkv, qkv, k_t, qkv, cq_row, ck_rep, delta_row, lse, dmix)


GW = SWA_GROUP * HEAD_DIM
GR = SWA_GROUP * WINDOW


def _swa_scores(q_ref, kp_ref, kc_ref, slope_ref, n):
    q = q_ref[...]
    qs = jnp.concatenate([q[:, t * HEAD_DIM:(t + 1) * HEAD_DIM] for t in range(SWA_GROUP)], axis=0)
    kb = jnp.concatenate([kp_ref[...], kc_ref[...]], axis=0)
    r = lax.broadcasted_iota(jnp.int32, (GR, 2 * WINDOW), 0) & (WINDOW - 1)
    jj = lax.broadcasted_iota(jnp.int32, (GR, 2 * WINDOW), 1)
    dist = WINDOW + r - jj
    valid = (dist >= 0) & (dist < WINDOW) & ((n > 0) | (jj >= WINDOW))
    s = _dot(qs, kb, "nt") * SCALE - slope_ref[...] * dist.astype(F32)
    return qs, kb, jnp.where(valid, s, NEG_INF), valid


def _swa_specs():
    HQ, HK, HV = C_SQ // GW, C_SK // HEAD_DIM, C_SV // HEAD_DIM
    q_spec = pl.BlockSpec((WINDOW, GW), lambda g, n: (n, HQ + g))

    def prev(base):
        return pl.BlockSpec((WINDOW, HEAD_DIM), lambda g, n: (jnp.maximum(n - 1, 0), base + g))

    def cur(base):
        return pl.BlockSpec((WINDOW, HEAD_DIM), lambda g, n: (n, base + g))

    col = pl.BlockSpec((None, GR, 1), lambda g, n: (g, 0, 0))
    return q_spec, prev(HK), cur(HK), prev(HV), cur(HV), col


def _swa_fwd(qkv, slopes, sinks):
    T = qkv.shape[0]
    nb = T // WINDOW
    assert C_SQ % GW == 0

    def body(q_ref, kp_ref, kc_ref, vp_ref, vc_ref, slope_ref, sink_ref, o_ref, lse_ref):
        n = pl.program_id(1)
        _, _, s, _ = _swa_scores(q_ref, kp_ref, kc_ref, slope_ref, n)
        m = jnp.maximum(jnp.max(s, axis=-1, keepdims=True), sink_ref[...])
        p = jnp.exp(s - m)
        l = jnp.sum(p, axis=-1, keepdims=True) + jnp.exp(sink_ref[...] - m)
        vb = jnp.concatenate([vp_ref[...], vc_ref[...]], axis=0)
        o = _dot(p.astype(BF16), vb, "nn") / l
        for t in range(SWA_GROUP):
            o_ref[:, t * HEAD_DIM:(t + 1) * HEAD_DIM] = o[t * WINDOW:(t + 1) * WINDOW, :].astype(BF16)
        lse_ref[...] = m + jnp.log(l)

    q_spec, kp, kc, vp, vc, col = _swa_specs()
    return pl.pallas_call(
        body, name="swa_fwd", grid=(SWA_KV_HEADS, nb), in_specs=[q_spec, kp, kc, vp, vc, col, col],
        out_specs=[pl.BlockSpec((WINDOW, GW), lambda g, n: (n, g)), pl.BlockSpec((None, None, GR, 1), lambda g, n: (g, n, 0, 0))],
        out_shape=[jax.ShapeDtypeStruct((T, SWA_HEADS * HEAD_DIM), BF16), jax.ShapeDtypeStruct((SWA_KV_HEADS, nb, GR, 1), F32)],
        compiler_params=_params(("parallel", "arbitrary")),
    )(qkv, qkv, qkv, qkv, qkv, slopes, sinks)


def _swa_bwd(qkv, slopes, sinks, out, lse, dmix):
    T = qkv.shape[0]
    nb = T // WINDOW
    DO = FOX_W // GW
    assert FOX_W % GW == 0

    def body(q_ref, kp_ref, kc_ref, vp_ref, vc_ref, slope_ref, sink_ref, o_ref, lse_ref, do_ref,
             dq_ref, dk_ref, dv_ref, dsink_ref, sink_sc):
        n = pl.program_id(1)

        @pl.when(n == 0)
        def _():
            dk_ref[...] = jnp.zeros_like(dk_ref)
            dv_ref[...] = jnp.zeros_like(dv_ref)
            sink_sc[...] = jnp.zeros_like(sink_sc)

        qs, kb, s, valid = _swa_scores(q_ref, kp_ref, kc_ref, slope_ref, n)
        lse = lse_ref[...]
        p = jnp.where(valid, jnp.exp(s - lse), 0.0)
        vb = jnp.concatenate([vp_ref[...], vc_ref[...]], axis=0)
        do = jnp.concatenate([do_ref[:, t * HEAD_DIM:(t + 1) * HEAD_DIM] for t in range(SWA_GROUP)], axis=0)
        oo = jnp.concatenate([o_ref[:, t * HEAD_DIM:(t + 1) * HEAD_DIM] for t in range(SWA_GROUP)], axis=0)
        dp = _dot(do, vb, "nt")
        delta = jnp.sum(do.astype(F32) * oo.astype(F32), axis=-1, keepdims=True)
        ds = p * (dp - delta)
        dsb = ds.astype(BF16)
        dq = _dot(dsb, kb, "nn") * SCALE
        for t in range(SWA_GROUP):
            dq_ref[:, t * HEAD_DIM:(t + 1) * HEAD_DIM] = dq[t * WINDOW:(t + 1) * WINDOW, :]
        dkb = _dot(dsb, qs, "tn") * SCALE
        dvb = _dot(p.astype(BF16), do, "tn")
        r_prev = pl.ds(pl.multiple_of(jnp.maximum(n - 1, 0) * WINDOW, WINDOW), WINDOW)
        r_cur = pl.ds(pl.multiple_of(n * WINDOW, WINDOW), WINDOW)
        dk_ref[r_prev, :] += dkb[:WINDOW, :]
        dk_ref[r_cur, :] += dkb[WINDOW:, :]
        dv_ref[r_prev, :] += dvb[:WINDOW, :]
        dv_ref[r_cur, :] += dvb[WINDOW:, :]
        sink_sc[...] -= jnp.exp(sink_ref[...] - lse) * delta

        @pl.when(n == nb - 1)
        def _():
            tot = [jnp.zeros((1, 128), F32) + jnp.sum(sink_sc[t * WINDOW:(t + 1) * WINDOW, :]) for t in range(SWA_GROUP)]
            dsink_ref[...] = jnp.concatenate(tot + [jnp.zeros((8 - SWA_GROUP, 128), F32)], axis=0)

    q_spec, kp, kc, vp, vc, col = _swa_specs()
    kv_acc = pl.BlockSpec((T, HEAD_DIM), lambda g, n: (0, g))
    return pl.pallas_call(
        body, name="swa_bwd", grid=(SWA_KV_HEADS, nb),
        in_specs=[q_spec, kp, kc, vp, vc, col, col, pl.BlockSpec((WINDOW, GW), lambda g, n: (n, g)),
                  pl.BlockSpec((None, None, GR, 1), lambda g, n: (g, n, 0, 0)), pl.BlockSpec((WINDOW, GW), lambda g, n: (n, DO + g))],
        out_specs=[pl.BlockSpec((WINDOW, GW), lambda g, n: (n, g)), kv_acc, kv_acc, pl.BlockSpec((None, 8, 128), lambda g, n: (g, 0, 0))],
        out_shape=[jax.ShapeDtypeStruct((T, SWA_HEADS * HEAD_DIM), F32), jax.ShapeDtypeStruct((T, SWA_KV_HEADS * HEAD_DIM), F32),
                   jax.ShapeDtypeStruct((T, SWA_KV_HEADS * HEAD_DIM), F32), jax.ShapeDtypeStruct((SWA_KV_HEADS, 8, 128), F32)],
        scratch_shapes=[pltpu.VMEM((GR, 1), F32)],
        compiler_params=_params(("parallel", "arbitrary")),
    )(qkv, qkv, qkv, qkv, qkv, slopes, sinks, out, lse, dmix)


def _mem_fwd(qkv, mk, mv):
    T, ML = qkv.shape[0], mk.shape[0]
    tq = _tile(T, 512)
    HQ = C_MQ // HEAD_DIM

    def body(q_ref, k_ref, v_ref, o_ref, lse_ref):
        s = _dot(q_ref[...], k_ref[...], "nt") * SCALE
        m = jnp.max(s, axis=-1, keepdims=True)
        p = jnp.exp(s - m)
        l = jnp.sum(p, axis=-1, keepdims=True)
        o_ref[...] = (_dot(p.astype(BF16), v_ref[...], "nn") / l).astype(BF16)
        lse_ref[...] = m + jnp.log(l)

    kv = pl.BlockSpec((ML, HEAD_DIM), lambda h, i: (0, h))
    return pl.pallas_call(
        body, name="mem_fwd", grid=(MEM_HEADS, T // tq),
        in_specs=[pl.BlockSpec((tq, HEAD_DIM), lambda h, i: (i, HQ + h)), kv, kv],
        out_specs=[pl.BlockSpec((tq, HEAD_DIM), lambda h, i: (i, h)), pl.BlockSpec((None, tq, 1), lambda h, i: (h, i, 0))],
        out_shape=[jax.ShapeDtypeStruct((T, MEM_HEADS * HEAD_DIM), BF16), jax.ShapeDtypeStruct((MEM_HEADS, T, 1), F32)],
        compiler_params=_params(("parallel", "arbitrary")),
    )(qkv, mk, mv)


def _mem_bwd(qkv, mk, mv, out, lse, dmix):
    T, ML = qkv.shape[0], mk.shape[0]
    tq = _tile(T, 512)
    HQ = C_MQ // HEAD_DIM
    DO = (FOX_W + SWA_HEADS * HEAD_DIM) // HEAD_DIM

    def body(q_ref, k_ref, v_ref, o_ref, lse_ref, do_ref, dq_ref, dk_ref, dv_ref):
        q, k, v, do = q_ref[...], k_ref[...], v_ref[...], do_ref[...]
        p = jnp.exp(_dot(q, k, "nt") * SCALE - lse_ref[...])
        dp = _dot(do, v, "nt")
        delta = jnp.sum(do.astype(F32) * o_ref[...].astype(F32), axis=-1, keepdims=True)
        dsb = (p * (dp - delta)).astype(BF16)
        dq_ref[...] = _dot(dsb, k, "nn") * SCALE
        dk_part = _dot(dsb, q, "tn") * SCALE
        dv_part = _dot(p.astype(BF16), do, "tn")

        @pl.when(pl.program_id(1) == 0)
        def _():
            dk_ref[...] = dk_part
            dv_ref[...] = dv_part

        @pl.when(pl.program_id(1) > 0)
        def _():
            dk_ref[...] += dk_part
            dv_ref[...] += dv_part

    kv = pl.BlockSpec((ML, HEAD_DIM), lambda h, i: (0, h))
    qb = pl.BlockSpec((tq, HEAD_DIM), lambda h, i: (i, h))
    return pl.pallas_call(
        body, name="mem_bwd", grid=(MEM_HEADS, T // tq),
        in_specs=[pl.BlockSpec((tq, HEAD_DIM), lambda h, i: (i, HQ + h)), kv, kv, qb,
                  pl.BlockSpec((None, tq, 1), lambda h, i: (h, i, 0)), pl.BlockSpec((tq, HEAD_DIM), lambda h, i: (i, DO + h))],
        out_specs=[qb, kv, kv],
        out_shape=[jax.ShapeDtypeStruct((T, MEM_HEADS * HEAD_DIM), F32), jax.ShapeDtypeStruct((ML, MEM_HEADS * HEAD_DIM), F32),
                   jax.ShapeDtypeStruct((ML, MEM_HEADS * HEAD_DIM), F32)],
        compiler_params=_params(("parallel", "arbitrary")),
    )(qkv, mk, mv, out, lse, dmix)


HBM = pl.BlockSpec(memory_space=pltpu.HBM)


def _place():
    x, y, c = lax.axis_index("x"), lax.axis_index("y"), lax.axis_index("c")
    chips = [(1 - x, y), (x, 1 - y), (1 - x, 1 - y)]
    return x, y, c, chips


def _remote(src, dst, send_sem, recv_sem, device):
    return pltpu.make_async_remote_copy(src_ref=src, dst_ref=dst, send_sem=send_sem, recv_sem=recv_sem,
                                        device_id=device, device_id_type=MESH)


def _place_ids():
    x, y, c = lax.axis_index("x"), lax.axis_index("y"), lax.axis_index("c")
    order = [2 * x + y, 2 * (1 - x) + y, 2 * x + (1 - y), 2 * (1 - x) + (1 - y)]
    return jnp.stack([2 * x + y, c] + order).astype(jnp.int32)


def _cast_place(name, w, ids, after):
    R, C = w.shape
    tr = _tile(R, 256, 16)

    def body(ids_ref, w_ref, after_ref, o_ref):
        o_ref[...] = w_ref[...].astype(BF16)

    return pl.pallas_call(
        body, name=name,
        grid_spec=pltpu.PrefetchScalarGridSpec(
            num_scalar_prefetch=1, grid=(R // tr,),
            in_specs=[pl.BlockSpec((tr, C), lambda i, ids: (i, 0)), pl.BlockSpec(memory_space=pl.ANY)],
            out_specs=pl.BlockSpec((None, tr, C), lambda i, ids: (ids[0], i, 0))),
        out_shape=jax.ShapeDtypeStruct((N_CHIPS, R, C), BF16), compiler_params=_params(("parallel",)),
    )(ids, w, after)


SEM = pl.BlockSpec(memory_space=pltpu.SEMAPHORE)
EFFECT = pltpu.SideEffectType.DATAFLOW_SIDE_EFFECTING


def _hbm(a):
    return pltpu.with_memory_space_constraint(a, pltpu.HBM)


def _gather_start(name, placed, after):
    n = len(placed)

    ns = 3 * n

    def body(*refs):
        send, recv = refs[n + 1:n + 1 + ns], refs[n + 1 + ns:n + 1 + 2 * ns]
        buf = refs[n + 1 + 2 * ns:2 * n + 1 + 2 * ns]
        token = refs[2 * n + 1 + 2 * ns]
        x, y, c, chips = _place()
        me = 2 * x + y
        for a in range(n):
            half = buf[a].shape[1] // 2
            mine = buf[a].at[me, pl.ds(c * half, half)]
            for j, (cx, cy) in enumerate(chips):
                _remote(mine, mine, send[3 * a + j], recv[3 * a + j], (cx, cy, c)).start()
        token[...] = jnp.zeros_like(token)

    res = pl.pallas_call(
        body, name=name, in_specs=[HBM] * n + [pl.BlockSpec(memory_space=pl.ANY)],
        out_specs=[SEM] * (2 * ns) + [HBM] * n + [pl.BlockSpec(memory_space=pltpu.VMEM)],
        out_shape=[pltpu.SemaphoreType.DMA(())] * (2 * ns)
        + [pltpu.HBM(s.shape, s.dtype) for s in placed] + [jax.ShapeDtypeStruct((8, 128), F32)],
        input_output_aliases={a: 2 * ns + a for a in range(n)},
        compiler_params=pltpu.CompilerParams(has_side_effects=EFFECT),
    )(*[_hbm(s) for s in placed], after)
    return list(res[:ns]), list(res[ns:2 * ns]), list(res[2 * ns:2 * ns + n]), res[2 * ns + n]


def _gather_wait(name, send, recv, bufs, after):
    n = len(bufs)

    ns = 3 * n

    def body(*refs):
        buf = refs[:n]
        send_ref, recv_ref = refs[n:n + ns], refs[n + ns:n + 2 * ns]
        x, y, c, chips = _place()
        ids = [2 * cx + cy for cx, cy in chips]
        for a in range(n):
            half = buf[a].shape[1] // 2
            for j in range(3):
                landed = buf[a].at[ids[j], pl.ds(c * half, half)]
                cp = _remote(landed, landed, send_ref[3 * a + j], recv_ref[3 * a + j], (x, y, c))
                cp.wait_send()
                cp.wait_recv()

    res = pl.pallas_call(
        body, name=name, in_specs=[HBM] * n + [SEM] * (2 * ns) + [pl.BlockSpec(memory_space=pl.ANY)], out_specs=[HBM] * n,
        out_shape=[pltpu.HBM(s.shape, s.dtype) for s in bufs], input_output_aliases={a: a for a in range(n)},
        compiler_params=pltpu.CompilerParams(has_side_effects=EFFECT),
    )(*bufs, *send, *recv, after)
    return list(res)


def _gather_forward(name, bufs):
    n = len(bufs)

    def body(*refs):
        buf = refs[n:2 * n]
        send, recv = refs[2 * n:]
        x, y, c, chips = _place()
        ids = [2 * cx + cy for cx, cy in chips]
        copies = []
        for a in range(n):
            half = buf[a].shape[1] // 2
            for j in range(3):
                landed = buf[a].at[ids[j], pl.ds(c * half, half)]
                cp = _remote(landed, landed, send.at[a, j], recv.at[a, j], (x, y, 1 - c))
                cp.start()
                copies.append(cp)
        for a in range(n):
            half = buf[a].shape[1] // 2
            for j in range(3):
                landed = buf[a].at[ids[j], pl.ds((1 - c) * half, half)]
                _remote(landed, landed, send.at[a, j], recv.at[a, j], (x, y, c)).wait_recv()
        for cp in copies:
            cp.wait_send()

    return pl.pallas_call(
        body, name=name, in_specs=[HBM] * n, out_specs=[HBM] * n,
        out_shape=[jax.ShapeDtypeStruct(s.shape, s.dtype) for s in bufs], input_output_aliases={a: a for a in range(n)},
        scratch_shapes=[pltpu.SemaphoreType.DMA((n, 3)), pltpu.SemaphoreType.DMA((n, 3))],
    )(*bufs)


def _pair_start(name, grads):
    n = len(grads)
    ns = N_CHIPS * n

    def body(*refs):
        send, recv = refs[2 * n:2 * n + ns], refs[2 * n + ns:2 * n + 2 * ns]
        src = refs[2 * n + 2 * ns:3 * n + 2 * ns]
        land = refs[3 * n + 2 * ns:4 * n + 2 * ns]
        token = refs[4 * n + 2 * ns]
        x, y, c, chips = _place()
        order = [2 * x + y] + [2 * cx + cy for cx, cy in chips]
        for a in range(n):
            half = src[a].shape[1] // 2
            for j in range(N_CHIPS):
                _remote(src[a].at[order[j], pl.ds((1 - c) * half, half)], land[a].at[j],
                        send[N_CHIPS * a + j], recv[N_CHIPS * a + j], (x, y, 1 - c)).start()
        token[...] = jnp.zeros_like(token)

    lands = [jax.ShapeDtypeStruct((N_CHIPS, g.shape[1] // 2, g.shape[2]), g.dtype) for g in grads]
    res = pl.pallas_call(
        body, name=name, in_specs=[HBM] * (2 * n),
        out_specs=[SEM] * (2 * ns) + [HBM] * (2 * n) + [pl.BlockSpec(memory_space=pltpu.VMEM)],
        out_shape=[pltpu.SemaphoreType.DMA(())] * (2 * ns) + [pltpu.HBM(g.shape, g.dtype) for g in grads]
        + [pltpu.HBM(l.shape, l.dtype) for l in lands] + [jax.ShapeDtypeStruct((8, 128), F32)],
        input_output_aliases={a: 2 * ns + a for a in range(2 * n)},
        compiler_params=pltpu.CompilerParams(has_side_effects=EFFECT),
    )(*[_hbm(g) for g in grads], *[_hbm(lax.empty(l.shape, l.dtype)) for l in lands])
    return list(res[:ns]), list(res[ns:2 * ns]), list(res[2 * ns:2 * ns + n]), list(res[2 * ns + n:2 * ns + 2 * n]), res[2 * ns + 2 * n]


def _pair_wait(name, send, recv, grads, lands, after):
    n = len(grads)
    ns = N_CHIPS * n

    def body(*refs):
        src, land = refs[:n], refs[n:2 * n]
        send_ref, recv_ref = refs[2 * n:2 * n + ns], refs[2 * n + ns:2 * n + 2 * ns]
        x, y, c, _ = _place()
        for a in range(n):
            for j in range(N_CHIPS):
                cp = _remote(land[a].at[j], land[a].at[j], send_ref[N_CHIPS * a + j], recv_ref[N_CHIPS * a + j], (x, y, c))
                cp.wait_send()
                cp.wait_recv()

    res = pl.pallas_call(
        body, name=name, in_specs=[HBM] * (2 * n) + [SEM] * (2 * ns) + [pl.BlockSpec(memory_space=pl.ANY)],
        out_specs=[HBM] * (2 * n), out_shape=[pltpu.HBM(g.shape, g.dtype) for g in grads] + [pltpu.HBM(l.shape, l.dtype) for l in lands],
        input_output_aliases={a: a for a in range(2 * n)},
        compiler_params=pltpu.CompilerParams(has_side_effects=EFFECT),
    )(*grads, *lands, *send, *recv, after)
    return list(res[:n]), list(res[n:])


def _chip_start(name, parts):
    n = len(parts)
    ns = 3 * n

    def body(*refs):
        send, recv = refs[2 * n:2 * n + ns], refs[2 * n + ns:2 * n + 2 * ns]
        src = refs[2 * n + 2 * ns:3 * n + 2 * ns]
        land = refs[3 * n + 2 * ns:4 * n + 2 * ns]
        token = refs[4 * n + 2 * ns]
        x, y, c, chips = _place()
        for a in range(n):
            for j, (cx, cy) in enumerate(chips):
                _remote(src[a].at[j], land[a].at[j], send[3 * a + j], recv[3 * a + j], (cx, cy, c)).start()
        token[...] = jnp.zeros_like(token)

    res = pl.pallas_call(
        body, name=name, in_specs=[HBM] * (2 * n),
        out_specs=[SEM] * (2 * ns) + [HBM] * (2 * n) + [pl.BlockSpec(memory_space=pltpu.VMEM)],
        out_shape=[pltpu.SemaphoreType.DMA(())] * (2 * ns) + [pltpu.HBM(p.shape, p.dtype) for p in parts] * 2
        + [jax.ShapeDtypeStruct((8, 128), F32)],
        input_output_aliases={a: 2 * ns + a for a in range(2 * n)},
        compiler_params=pltpu.CompilerParams(has_side_effects=EFFECT),
    )(*[_hbm(p) for p in parts], *[_hbm(lax.empty(p.shape, p.dtype)) for p in parts])
    return list(res[:ns]), list(res[ns:2 * ns]), list(res[2 * ns:2 * ns + n]), list(res[2 * ns + n:2 * ns + 2 * n]), res[2 * ns + 2 * n]


def _chip_wait(name, send, recv, parts, lands, after):
    n = len(parts)
    ns = 3 * n

    def body(*refs):
        src, land = refs[:n], refs[n:2 * n]
        send_ref, recv_ref = refs[2 * n:2 * n + ns], refs[2 * n + ns:2 * n + 2 * ns]
        x, y, c, _ = _place()
        for a in range(n):
            for j in range(3):
                cp = _remote(src[a].at[j], land[a].at[j], send_ref[3 * a + j], recv_ref[3 * a + j], (x, y, c))
                cp.wait_send()
                cp.wait_recv()

    res = pl.pallas_call(
        body, name=name, in_specs=[HBM] * (2 * n) + [SEM] * (2 * ns) + [pl.BlockSpec(memory_space=pl.ANY)],
        out_specs=[HBM] * (2 * n), out_shape=[pltpu.HBM(p.shape, p.dtype) for p in parts] * 2,
        input_output_aliases={a: a for a in range(2 * n)},
        compiler_params=pltpu.CompilerParams(has_side_effects=EFFECT),
    )(*parts, *lands, *send, *recv, after)
    return list(res[n:])


def _pair_share(name, shards):
    n = len(shards)

    def body(*refs):
        buf = refs[n:2 * n]
        send, recv = refs[2 * n:]
        x, y, c, _ = _place()
        copies = []
        for a in range(n):
            half = buf[a].shape[0] // 2
            mine = buf[a].at[pl.ds(c * half, half)]
            cp = _remote(mine, mine, send.at[a], recv.at[a], (x, y, 1 - c))
            cp.start()
            copies.append(cp)
        for a, cp in enumerate(copies):
            half = buf[a].shape[0] // 2
            cp.wait_send()
            theirs = buf[a].at[pl.ds((1 - c) * half, half)]
            _remote(theirs, theirs, send.at[a], recv.at[a], (x, y, c)).wait_recv()

    return pl.pallas_call(
        body, name=name, in_specs=[HBM] * n, out_specs=[HBM] * n,
        out_shape=[jax.ShapeDtypeStruct(s.shape, s.dtype) for s in shards], input_output_aliases={a: a for a in range(n)},
        scratch_shapes=[pltpu.SemaphoreType.DMA((n,)), pltpu.SemaphoreType.DMA((n,))],
    )(*shards)


def _small_start(buf):
    R, W = buf.shape
    ns = N_DEV - 1

    def body(*refs):
        send, recv = refs[2:2 + ns], refs[2 + ns:2 + 2 * ns]
        src, land, token = refs[2 + 2 * ns], refs[3 + 2 * ns], refs[4 + 2 * ns]
        x, y, c, _ = _place()
        me = 4 * x + 2 * y + c
        for k in range(1, N_DEV):
            peer = (x ^ (k >> 2), y ^ ((k >> 1) & 1), c ^ (k & 1))
            _remote(src, land.at[me], send[k - 1], recv[k - 1], peer).start()
        token[...] = jnp.zeros_like(token)

    res = pl.pallas_call(
        body, name="small_start", in_specs=[HBM, HBM],
        out_specs=[SEM] * (2 * ns) + [HBM, HBM, pl.BlockSpec(memory_space=pltpu.VMEM)],
        out_shape=[pltpu.SemaphoreType.DMA(())] * (2 * ns) + [pltpu.HBM((R, W), F32), pltpu.HBM((N_DEV, R, W), F32),
                                                                jax.ShapeDtypeStruct((8, 128), F32)],
        input_output_aliases={0: 2 * ns, 1: 2 * ns + 1},
        compiler_params=pltpu.CompilerParams(has_side_effects=EFFECT),
    )(_hbm(buf), _hbm(jnp.zeros((N_DEV, R, W), F32)))
    return list(res[:ns]), list(res[ns:2 * ns]), res[2 * ns], res[2 * ns + 1], res[2 * ns + 2]


def _small_wait(send, recv, buf, land, after):
    ns = N_DEV - 1

    def body(*refs):
        land_ref = refs[1]
        send_ref, recv_ref = refs[2:2 + ns], refs[2 + ns:2 + 2 * ns]
        x, y, c, _ = _place()
        me = 4 * x + 2 * y + c
        for k in range(1, N_DEV):
            landed = land_ref.at[me ^ k]
            cp = _remote(landed, landed, send_ref[k - 1], recv_ref[k - 1], (x, y, c))
            cp.wait_send()
            cp.wait_recv()

    return pl.pallas_call(
        body, name="small_wait", in_specs=[HBM, HBM] + [SEM] * (2 * ns) + [pl.BlockSpec(memory_space=pl.ANY)],
        out_specs=[HBM, HBM], out_shape=[pltpu.HBM(buf.shape, buf.dtype), pltpu.HBM(land.shape, land.dtype)],
        input_output_aliases={0: 0, 1: 1}, compiler_params=pltpu.CompilerParams(has_side_effects=EFFECT),
    )(buf, land, *send, *recv, after)


def _small_sum(buf, land):
    def body(buf_ref, land_ref, out_ref):
        x, y, c, _ = _place()
        me = 4 * x + 2 * y + c
        total = None
        for d in range(N_DEV):
            term = jnp.where(me == d, buf_ref[...], land_ref[d])
            total = term if total is None else total + term
        out_ref[...] = total

    return pl.pallas_call(body, name="small_sum", out_shape=jax.ShapeDtypeStruct(buf.shape, F32))(buf, land)


def _pair_sum_bf16(name, grad, theirs, ids):
    _, R2, C = theirs.shape
    tr = _tile(R2, 256, 16)
    nrb = R2 // tr

    def body(ids_ref, a_ref, b_ref, o_ref):
        o_ref[...] = (a_ref[...] + b_ref[...]).astype(BF16)

    return pl.pallas_call(
        body, name=name,
        grid_spec=pltpu.PrefetchScalarGridSpec(
            num_scalar_prefetch=1, grid=(3, nrb),
            in_specs=[pl.BlockSpec((None, tr, C), lambda j, i, ids: (ids[3 + j], ids[1] * nrb + i, 0)),
                      pl.BlockSpec((None, tr, C), lambda j, i, ids: (j + 1, i, 0))],
            out_specs=pl.BlockSpec((None, tr, C), lambda j, i, ids: (j, i, 0))),
        out_shape=jax.ShapeDtypeStruct((3, R2, C), BF16), compiler_params=_params(("parallel", "parallel")),
    )(ids, grad, theirs)


def _chip_sum(name, grad, theirs, arrived, ids):
    _, R2, C = theirs.shape
    tr = _tile(R2, 256, 16)
    nrb = R2 // tr

    def body(ids_ref, a_ref, b_ref, r_ref, o_ref):
        tot = a_ref[...] + b_ref[...]
        for j in range(3):
            tot = tot + r_ref[j].astype(F32)
        o_ref[...] = tot

    return pl.pallas_call(
        body, name=name,
        grid_spec=pltpu.PrefetchScalarGridSpec(
            num_scalar_prefetch=1, grid=(nrb,),
            in_specs=[pl.BlockSpec((None, tr, C), lambda i, ids: (ids[0], ids[1] * nrb + i, 0)),
                      pl.BlockSpec((None, tr, C), lambda i, ids: (0, i, 0)),
                      pl.BlockSpec((3, tr, C), lambda i, ids: (0, i, 0))],
            out_specs=pl.BlockSpec((tr, C), lambda i, ids: (ids[1] * nrb + i, 0))),
        out_shape=jax.ShapeDtypeStruct((2 * R2, C), F32), compiler_params=_params(("parallel",)),
    )(ids, grad, theirs, arrived)


def _adamw(name, w, g, m, v, emit_grad=False):
    R, C = w.shape
    tr = _tile(R, 128, 8)
    c1 = 1.0 / (1.0 - ADAM_B1 ** ADAM_STEP)
    c2 = 1.0 / (1.0 - ADAM_B2 ** ADAM_STEP)
    n_out = 4 if emit_grad else 3

    def body(w_ref, g_ref, m_ref, v_ref, d_ref, mo_ref, vo_ref, *rest):
        gv = g_ref[...]
        mn = ADAM_B1 * m_ref[...] + (1.0 - ADAM_B1) * gv
        vn = ADAM_B2 * v_ref[...] + (1.0 - ADAM_B2) * (gv * gv)
        d_ref[...] = -ADAM_LR * ((mn * c1) / (jnp.sqrt(vn * c2) + ADAM_EPS) + ADAM_WD * w_ref[...])
        mo_ref[...] = mn
        vo_ref[...] = vn
        if emit_grad:
            rest[0][...] = gv

    spec = pl.BlockSpec((tr, C), lambda i: (i, 0))
    sds = jax.ShapeDtypeStruct((R, C), F32)
    return pl.pallas_call(body, name=name, grid=(R // tr,), in_specs=[spec] * 4, out_specs=[spec] * n_out, out_shape=[sds] * n_out,
                          compiler_params=_params(("parallel",)))(w, g, m, v)


SMALL = ["ffn1_norm", "mix_norm", "mem_norm", "forget_bias", "fox_q_gain", "fox_k_gain", "swa_q_gain", "swa_k_gain", "swa_sinks",
         "mem_q_gain", "mem_k_gain", "ffn2_norm"]
LARGE = ["ffn1_gate", "ffn1_up", "ffn1_down", "w_in", "w_mem_k", "w_mem_v", "w_out", "ffn2_gate", "ffn2_up", "ffn2_down"]
GATHER_GROUPS = [["ffn1_gate", "ffn1_up"], ["ffn1_down", "w_in", "w_mem_k", "w_mem_v"], ["w_out", "ffn2_gate", "ffn2_up", "ffn2_down"]]
WEIGHTS = ["ffn1_norm", "ffn1_gate", "ffn1_up", "ffn1_down", "mix_norm", "mem_norm", "w_in", "forget_bias", "w_mem_k", "w_mem_v",
           "fox_q_gain", "fox_k_gain", "swa_q_gain", "swa_k_gain", "swa_sinks", "mem_q_gain", "mem_k_gain", "w_out", "ffn2_norm",
           "ffn2_gate", "ffn2_up", "ffn2_down"]


def _pad_proj_cols(w):
    out = jnp.zeros((w.shape[0], PROJ_W), w.dtype)
    for start, width, pstart in REF_GROUPS:
        out = lax.dynamic_update_slice(out, w[:, start:start + width], (0, pstart))
    return out


def _unpad_proj_cols(w):
    return jnp.concatenate([w[:, pstart:pstart + width] for _, width, pstart in REF_GROUPS], axis=1)


def _pack_small(vals):
    flat = jnp.concatenate([vals[k].reshape(-1).astype(F32) for k in SMALL + ["loss"]])
    n = flat.shape[0]
    total = -(-n // 1024) * 1024
    return jnp.pad(flat, (0, total - n)).reshape(total // 128, 128)


def _unpack_small(buf, shapes):
    flat = buf.reshape(-1)
    out, off = {}, 0
    for k in SMALL + ["loss"]:
        size = int(np.prod(shapes[k]))
        out[k] = flat[off:off + size].reshape(shapes[k])
        off += size
    return out


def kernel(x, mem, ffn1_norm, ffn1_gate, ffn1_up, ffn1_down, mix_norm, mem_norm, w_in, forget_bias, w_mem_k, w_mem_v, fox_q_gain, fox_k_gain, swa_q_gain, swa_k_gain, swa_sinks, mem_q_gain, mem_k_gain, w_out, ffn2_norm, ffn2_gate, ffn2_up, ffn2_down, loss_target, m_ffn1_norm, m_ffn1_gate, m_ffn1_up, m_ffn1_down, m_mix_norm, m_mem_norm, m_w_in, m_forget_bias, m_w_mem_k, m_w_mem_v, m_fox_q_gain, m_fox_k_gain, m_swa_q_gain, m_swa_k_gain, m_swa_sinks, m_mem_q_gain, m_mem_k_gain, m_w_out, m_ffn2_norm, m_ffn2_gate, m_ffn2_up, m_ffn2_down, v_ffn1_norm, v_ffn1_gate, v_ffn1_up, v_ffn1_down, v_mix_norm, v_mem_norm, v_w_in, v_forget_bias, v_w_mem_k, v_w_mem_v, v_fox_q_gain, v_fox_k_gain, v_swa_q_gain, v_swa_k_gain, v_swa_sinks, v_mem_q_gain, v_mem_k_gain, v_w_out, v_ffn2_norm, v_ffn2_gate, v_ffn2_up, v_ffn2_down):
    given = dict(locals())
    T, D = x.shape[1], x.shape[2]
    ML = mem.shape[1]
    xin = x.reshape(T, D)
    target = loss_target.reshape(T, D)
    memin = mem.reshape(ML, D)

    ids = _place_ids()
    shard = {k: given[k][0] for k in LARGE}
    started, after = [], ids
    for gi, group in enumerate(GATHER_GROUPS):
        if "w_in" in group:
            shard["w_in"] = _pad_proj_cols(shard["w_in"] + after[0, 0])
        placed = [_cast_place("cast_" + k, shard[k], ids, after) for k in group]
        send, recv, bufs, after = _gather_start("gather_start_%d" % gi, placed, after)
        started.append((send, recv, bufs))

    def arrive(gi, done):
        send, recv, bufs = started[gi]
        bufs = _gather_wait("gather_wait_%d" % gi, send, recv, bufs, done)
        return dict(zip(GATHER_GROUPS[gi], _gather_forward("gather_forward_%d" % gi, bufs)))

    gains = jnp.concatenate([fox_q_gain, fox_k_gain, swa_q_gain, swa_k_gain, mem_q_gain,
                             jnp.pad(forget_bias, ((0, 0), (0, HEAD_DIM - FOX_HEADS))), jnp.zeros((2, HEAD_DIM), F32)], axis=0)
    slopes_np = 2.0 ** (-8.0 * np.arange(1, SWA_HEADS + 1) / SWA_HEADS)
    slopes = jnp.asarray(np.repeat(slopes_np, WINDOW).reshape(SWA_KV_HEADS, GR, 1), F32)
    sinks = jnp.repeat(swa_sinks.reshape(SWA_HEADS), WINDOW).reshape(SWA_KV_HEADS, GR, 1)

    h1 = _rms_fwd("ffn1_norm_fwd", xin, ffn1_norm + after[0, 0])
    full = arrive(0, h1)
    wg1, wu1 = full["ffn1_gate"], full["ffn1_up"]
    fg1, fu1, a1 = _ffn_gu("ffn1_gate_up", h1, wg1, wu1)
    full = arrive(1, a1)
    wd1 = full["ffn1_down"].reshape(-1, D)
    win = full["w_in"].reshape(D, PROJ_W)
    wmk = full["w_mem_k"].reshape(D, MEM_HEADS * HEAD_DIM)
    wmv = full["w_mem_v"].reshape(D, MEM_HEADS * HEAD_DIM)
    x1, h2 = _residual_norm("ffn1_down", a1, wd1, xin, 0.5, mix_norm, 256)
    proj = _mm2d("proj_in", h2, win, "nn", F32, tn=1408, tk=2048, n_outer=True)
    qkv, logf, k_t, v_t = _prep_fwd(proj, gains)
    cum = _cumsum_rows("forget_cumsum", [logf], False)
    cum_h = cum[:, :FOX_HEADS].T
    cq_row = cum_h.reshape(FOX_HEADS, 1, T)
    ck_rep = jnp.broadcast_to(cum_h[:, :, None], (FOX_HEADS, T, HEAD_DIM))
    mn = _rms_fwd("mem_norm_fwd", memin, mem_norm)
    mk_raw = _mm2d("mem_k_proj", mn, wmk, "nn", F32)
    mv = _mm2d("mem_v_proj", mn, wmv, "nn", BF16)
    mk = _head_norm_rows(mk_raw, mem_k_gain)
    out_a, out_a_f32, lse_a = _fox_fwd(qkv, v_t, cq_row, ck_rep)
    out_b, lse_b = _swa_fwd(qkv, slopes, sinks)
    out_c, lse_c = _mem_fwd(qkv, mk, mv)
    mixed = jnp.concatenate([out_a, out_b, out_c], axis=1)
    full = arrive(2, mixed)
    wo = full["w_out"].reshape(-1, D)
    wg2, wu2, wd2 = full["ffn2_gate"], full["ffn2_up"], full["ffn2_down"].reshape(-1, D)
    x2, h3 = _residual_norm("mix_out", mixed, wo, x1, 1.0, ffn2_norm, 512)
    fg2, fu2, a2 = _ffn_gu("ffn2_gate_up", h3, wg2, wu2)
    dx3, dyb3, loss_blocks = _ffn_down_loss("ffn2_down", a2, wd2, x2, target)

    grads, small, res = {}, {"loss": jnp.sum(loss_blocks[::8, 0])}, {}

    def pair_off(tag, group):
        send, recv, own, lands, token = _pair_start("grad_pair_start_" + tag, [grads[k] for k in group])
        return (group, send, recv, own, lands), token

    def chip_off(tag, started, done):
        group, send, recv, own, lands = started
        own, theirs = _pair_wait("grad_pair_wait_" + tag, send, recv, own, lands, done)
        grads.update(zip(group, own))
        to_chips = [_pair_sum_bf16("pair_sum_" + k, grads[k], b, ids) for k, b in zip(group, theirs)]
        send, recv, parts, lands, token = _chip_start("grad_chip_start_" + tag, to_chips)
        return (group, theirs, send, recv, parts, lands), token

    def finish(tag, state, done):
        group, theirs, send, recv, parts, lands = state
        arrived = _chip_wait("grad_chip_wait_" + tag, send, recv, parts, lands, done)
        halves = [_chip_sum("chip_sum_" + k, grads[k], b, r, ids) for k, b, r in zip(group, theirs, arrived)]
        reduced = dict(zip(group, _pair_share("grad_pair_share_" + tag, halves)))
        last = None
        for k in group:
            if k == "w_in":
                gk = _unpad_proj_cols(reduced[k])
                d, mo, vo = _adamw("adamw_" + k, given[k][0], gk, given["m_" + k][0], given["v_" + k][0])
            else:
                d, mo, vo, gk = _adamw("adamw_" + k, given[k][0], reduced[k], given["m_" + k][0], given["v_" + k][0], emit_grad=True)
            res[k] = tuple(t[None] for t in (gk, d, mo, vo))
            last = vo
        return last

    dg2, du2 = _ffn_bwd_act("ffn2", dyb3, wd2, fg2, fu2, N_CHIPS)
    grads["ffn2_down"] = _ffn_bwd_down("ffn2", a2, dyb3, N_CHIPS).reshape(N_CHIPS, -1, D)
    grads["ffn2_gate"], grads["ffn2_up"] = _ffn_bwd_gate_up("ffn2", h3, dg2, du2, N_CHIPS)
    started, token = pair_off("a", ["ffn2_gate", "ffn2_up", "ffn2_down"])
    dh3 = _ffn_bwd_x("ffn2", dg2, du2, wg2, wu2, token)
    state_a, token = chip_off("a", started, dh3)
    dx2, dx2b, small["ffn2_norm"] = _rms_bwd("ffn2_norm_bwd", dh3, x2, ffn2_norm + token[0, 0], dx3, 1.0)
    dmix = _mm2d("mix_out_dx", dx2b, wo, "nt", BF16, tk=2048, n_outer=True)
    grads["w_out"] = _mm2d("mix_out_dw", mixed, dx2b, "tn", F32, tk=T, n_outer=True, resident=True).reshape(N_CHIPS, -1, D)
    delta_row = _fox_delta(dmix, out_a_f32)[:, :FOX_HEADS].T.reshape(FOX_HEADS, 1, T)
    dfq, dfk, dfv, dck, dcq = _fox_bwd(qkv, k_t, cq_row, ck_rep, delta_row, lse_a, dmix)
    dsq, dsk, dsv, dsink = _swa_bwd(qkv, slopes, sinks, out_b, lse_b, dmix)
    dmq, dmk, dmv = _mem_bwd(qkv, mk, mv, out_c, lse_c, dmix)
    small["swa_sinks"] = dsink[:, :SWA_GROUP, 0].reshape(1, SWA_HEADS)
    dcum = jnp.pad(dcq.reshape(FOX_HEADS, T).T, ((0, 0), (0, HEAD_DIM - FOX_HEADS)))
    dlogf = _cumsum_rows("forget_cumsum_bwd", [dcum], True, columns=dck)
    dproj, dgains = _prep_bwd(proj, gains, dfq, dfk, dfv, dsq, dsk, dsv, dmq, dlogf)
    for row, k in enumerate(["fox_q_gain", "fox_k_gain", "swa_q_gain", "swa_k_gain", "mem_q_gain"]):
        small[k] = dgains[row:row + 1, :]
    small["forget_bias"] = dgains[5:6, :FOX_HEADS]
    grads["w_in"] = _mm2d("proj_in_dw", h2, dproj, "tn", F32, tn=1408, tk=T, n_outer=True, resident=True).reshape(N_CHIPS, -1, PROJ_W)
    dmk_raw, small["mem_k_gain"] = _head_norm_rows_bwd(mk_raw, mem_k_gain, dmk)
    dmvb = dmv.astype(BF16)
    grads["w_mem_k"] = _mm2d("mem_k_dw", mn, dmk_raw, "tn", F32).reshape(N_CHIPS, -1, MEM_HEADS * HEAD_DIM)
    grads["w_mem_v"] = _mm2d("mem_v_dw", mn, dmvb, "tn", F32).reshape(N_CHIPS, -1, MEM_HEADS * HEAD_DIM)
    dmn = _mm2d("mem_k_dx", dmk_raw, wmk, "nt", F32)
    dmn = _mm2d("mem_v_dx", dmvb, wmv, "nt", F32, extras=[dmn], epilogue=lambda accs, ex: [ex[0] + accs[0]])
    _, _, small["mem_norm"] = _rms_bwd("mem_norm_bwd", dmn, memin, mem_norm, jnp.zeros_like(memin), 1.0)
    started, token = pair_off("b", ["w_out", "w_in", "w_mem_k", "w_mem_v"])
    dh2 = _mm2d("proj_in_dx", dproj, win, "nt", F32, tm=512, tn=1024, tk=PROJ_W, n_outer=True, resident=True, after=token)
    state_b, token = chip_off("b", started, dh2)
    dx1, dyb1, small["mix_norm"] = _rms_bwd("mix_norm_bwd", dh2, x1, mix_norm + token[0, 0], dx2, 0.5)
    grads["ffn1_down"] = _ffn_bwd_down("ffn1", a1, dyb1, N_CHIPS).reshape(N_CHIPS, -1, D)
    started, token = pair_off("c", ["ffn1_down"])
    dg1, du1 = _ffn_bwd_act("ffn1", dyb1, wd1, fg1, fu1, N_CHIPS, after=token)
    state_c, token = chip_off("c", started, dg1)
    grads["ffn1_gate"], grads["ffn1_up"] = _ffn_bwd_gate_up("ffn1", h1, dg1, du1, N_CHIPS, after=token)
    started, token = pair_off("d", ["ffn1_gate", "ffn1_up"])
    dh1 = _ffn_bwd_x("ffn1", dg1, du1, wg1, wu1, token)
    state_d, token = chip_off("d", started, dh1)
    grad_x, _, small["ffn1_norm"] = _rms_bwd("ffn1_norm_bwd", dh1, xin, ffn1_norm + token[0, 0], dx1, 1.0)

    s_send, s_recv, s_buf, s_land, token = _small_start(_pack_small(small))

    done = finish("a", state_a, token)
    done = finish("b", state_b, done)
    done = finish("c", state_c, done)
    done = finish("d", state_d, done)

    shapes = {k: given[k].shape for k in SMALL}
    shapes["loss"] = ()
    s_buf, s_land = _small_wait(s_send, s_recv, s_buf, s_land, done)
    red_small = _unpack_small(_small_sum(s_buf, s_land), shapes)
    loss = red_small["loss"]
    zero = {"loss": jnp.zeros((), F32)}
    packed = [_pack_small({**zero, **{k: src[k] for k in SMALL}}) for src in (
        {k: given[k] for k in SMALL}, red_small, {k: given["m_" + k] for k in SMALL}, {k: given["v_" + k] for k in SMALL})]
    d_s, m_s, v_s = (_unpack_small(t, shapes) for t in _adamw("adamw_small", *packed))
    for k in SMALL:
        res[k] = (red_small[k], d_s[k], m_s[k], v_s[k])

    outs = [loss, grad_x.reshape(1, T, D)]
    for part in range(4):
        outs += [res[k][part] for k in WEIGHTS]
    return tuple(outs)
```

```python
import functools

import numpy as np
import jax
import jax.numpy as jnp
from jax import lax
from jax.experimental import pallas as pl
from jax.experimental.pallas import tpu as pltpu

F32 = jnp.float32
BF16 = jnp.bfloat16
MESH = pl.DeviceIdType.MESH

HEAD_DIM = 128
FOX_HEADS = 6
SWA_HEADS = 6
SWA_KV_HEADS = 2
SWA_GROUP = SWA_HEADS // SWA_KV_HEADS
MEM_HEADS = 4
WINDOW = 128
EPS = 1e-6
NEG_INF = -1e30
SCALE = HEAD_DIM ** -0.5

C_FQ = 0
C_FK = C_FQ + FOX_HEADS * HEAD_DIM
C_FV = C_FK + FOX_HEADS * HEAD_DIM
C_SQ = C_FV + FOX_HEADS * HEAD_DIM
C_SK = C_SQ + SWA_HEADS * HEAD_DIM
C_SV = C_SK + SWA_KV_HEADS * HEAD_DIM
C_MQ = C_SV + SWA_KV_HEADS * HEAD_DIM
C_FL = C_MQ + MEM_HEADS * HEAD_DIM
PROJ_W = C_FL + HEAD_DIM
FOX_W = FOX_HEADS * HEAD_DIM
REF_GROUPS = [
    (0, FOX_W, C_FQ), (FOX_W, FOX_W, C_FK), (2 * FOX_W, FOX_W, C_FV), (3 * FOX_W, FOX_HEADS, C_FL),
    (3 * FOX_W + FOX_HEADS, SWA_HEADS * HEAD_DIM, C_SQ),
    (3 * FOX_W + FOX_HEADS + SWA_HEADS * HEAD_DIM, SWA_KV_HEADS * HEAD_DIM, C_SK),
    (3 * FOX_W + FOX_HEADS + (SWA_HEADS + SWA_KV_HEADS) * HEAD_DIM, SWA_KV_HEADS * HEAD_DIM, C_SV),
    (3 * FOX_W + FOX_HEADS + (SWA_HEADS + 2 * SWA_KV_HEADS) * HEAD_DIM, MEM_HEADS * HEAD_DIM, C_MQ),
]

ADAM_LR = 0.001
ADAM_B1 = 0.9
ADAM_B2 = 0.999
ADAM_EPS = 1e-08
ADAM_WD = 0.01
ADAM_STEP = 10

V7X_VMEM_LIMIT = 56 * 1024 * 1024
N_CHIPS = 4
N_DEV = 8


def _tile(n, pref, mult=128):
    t = (min(pref, n) // mult) * mult
    while t >= mult:
        if n % t == 0:
            return t
        t -= mult
    return n


def _params(sem):
    return pltpu.CompilerParams(dimension_semantics=sem, vmem_limit_bytes=V7X_VMEM_LIMIT)


_DIMS = {"nn": (((1,), (0,)), ((), ())), "nt": (((1,), (1,)), ((), ())), "tn": (((0,), (0,)), ((), ()))}


def _dot(a, b, mode):
    return lax.dot_general(a, b, _DIMS[mode], preferred_element_type=F32)


def _mm(name, grid, pairs, acc_of, acc_shapes, extras, outs, epilogue, after=None):
    n_p, n_e, n_o, n_a = len(pairs), len(extras), len(outs), len(acc_shapes)
    n_w = 0 if after is None else 1
    nk = grid[2]
    n_in = sum(1 if a is None else 2 for a, *_ in pairs)

    def body(*refs):
        ex = refs[n_in:n_in + n_e]
        out = refs[n_in + n_e + n_w:n_in + n_e + n_w + n_o]
        accs = refs[n_in + n_e + n_w + n_o:]
        parts = [None] * n_a
        at = 0
        for p in range(n_p):
            if pairs[p][0] is None:
                a_ref, b_ref = refs[0], refs[at]
                at += 1
            else:
                a_ref, b_ref = refs[at], refs[at + 1]
                at += 2
            d = _dot(a_ref[...], b_ref[...], pairs[p][4])
            parts[acc_of[p]] = d if parts[acc_of[p]] is None else parts[acc_of[p]] + d

        def finish(vals):
            for o, r in zip(out, epilogue(vals, [e[...] for e in ex])):
                o[...] = r.astype(o.dtype)

        if nk == 1:
            finish(parts)
            return
        k = pl.program_id(2)

        @pl.when(k == 0)
        def _():
            for a, d in zip(accs, parts):
                a[...] = d

        @pl.when((k > 0) & (k < nk - 1))
        def _():
            for a, d in zip(accs, parts):
                a[...] += d

        @pl.when(k == nk - 1)
        def _():
            finish([a[...] + d for a, d in zip(accs, parts)])

    in_specs, args = [], []
    for a, a_spec, b, b_spec, _ in pairs:
        if a is not None:
            in_specs.append(a_spec)
            args.append(a)
        in_specs.append(b_spec)
        args.append(b)
    for e, e_spec in extras:
        in_specs.append(e_spec)
        args.append(e)
    if after is not None:
        in_specs.append(pl.BlockSpec(memory_space=pl.ANY))
        args.append(after)
    res = pl.pallas_call(
        body, name=name, grid=grid, in_specs=in_specs,
        out_specs=[s for _, s in outs], out_shape=[o for o, _ in outs],
        scratch_shapes=[pltpu.VMEM(s, F32) for s in acc_shapes] if nk > 1 else [],
        compiler_params=_params(("parallel", "parallel", "arbitrary")),
    )(*args)
    return res


def _mm2d(name, a, b, mode, out_dtype, tm=512, tn=1024, tk=1024, extras=(), epilogue=None, n_out=1, after=None, n_outer=False,
          resident=False, rows=(), tile_stat=False):
    if mode == "nn":
        (M, K), N = a.shape, b.shape[1]
    elif mode == "nt":
        (M, K), N = a.shape, b.shape[0]
    else:
        (K, M), N = a.shape, b.shape[1]
    tm, tn, tk = _tile(M, tm), _tile(N, tn), _tile(K, tk)
    assert not resident or tk == K

    def spec(shape, index, single=False):
        mode_kw = {"pipeline_mode": pl.Buffered(1)} if single else {}
        if n_outer:
            return pl.BlockSpec(shape, lambda j, i, k: index(i, j, k), **mode_kw)
        return pl.BlockSpec(shape, index, **mode_kw)

    single_a, single_b = resident and not n_outer, resident and n_outer
    a_spec = spec((tk, tm), lambda i, j, k: (k, i), single_a) if mode == "tn" else spec((tm, tk), lambda i, j, k: (i, k), single_a)
    b_spec = spec((tn, tk), lambda i, j, k: (j, k), single_b) if mode == "nt" else spec((tk, tn), lambda i, j, k: (k, j), single_b)
    mn = spec((tm, tn), lambda i, j, k: (i, j))
    if epilogue is None:
        epilogue = lambda accs, ex: [accs[0]]
    if not isinstance(out_dtype, (list, tuple)):
        out_dtype = [out_dtype] * n_out
    grid = (N // tn, M // tm, K // tk) if n_outer else (M // tm, N // tn, K // tk)
    outs = [(jax.ShapeDtypeStruct((M, N), d), mn) for d in out_dtype]
    if tile_stat:
        assert tn == N
        outs.append((jax.ShapeDtypeStruct((8 * (M // tm), 128), F32), spec((8, 128), lambda i, j, k: (i, 0))))
    res = _mm(name, grid, [(a, a_spec, b, b_spec, mode)], [0], [(tm, tn)],
              [(e, mn) for e in extras] + [(r, spec((1, tn), lambda i, j, k: (0, j))) for r in rows], outs, epilogue, after=after)
    return res[0] if len(res) == 1 else res


def _sigmoid(x):
    return 1.0 / (1.0 + jnp.exp(-x))


def _sigmoid_fast(x):
    return pl.reciprocal(1.0 + jnp.exp(-x), approx=True)


def _ffn_gu(name, h, wg, wu):
    T, D = h.shape
    nf, _, F4 = wg.shape
    tm, tk = _tile(T, 512), _tile(D, 2048)
    a_spec = pl.BlockSpec((tm, tk), lambda j, i, k: (i, k))
    b_spec = pl.BlockSpec((None, tk, F4), lambda j, i, k: (j, k, 0))
    o_spec = pl.BlockSpec((tm, F4), lambda j, i, k: (i, j))

    def epilogue(accs, ex):
        g, u = accs
        s = _sigmoid_fast(g)
        gs = g * s
        return [(s + s * (g - gs)) * u, gs, gs * u]

    sds = jax.ShapeDtypeStruct((T, nf * F4), BF16)
    return _mm(name, (nf, T // tm, D // tk), [(h, a_spec, wg, b_spec, "nn"), (None, None, wu, b_spec, "nn")], [0, 1],
               [(tm, F4), (tm, F4)], [], [(sds, o_spec)] * 3, epilogue)


def _rms_rows(x, gain):
    return x * lax.rsqrt(jnp.mean(x * x, axis=-1, keepdims=True) + EPS) * gain


def _residual_norm(name, a, w, xres, scale, gain, tm):
    def epilogue(accs, ex):
        y = ex[0] + scale * accs[0]
        return [y, _rms_rows(y, ex[1])]

    return _mm2d(name, a, w, "nn", [F32, BF16], tm=tm, tn=w.shape[1], tk=w.shape[0], n_outer=True, resident=True, extras=[xres],
                 rows=[gain], epilogue=epilogue)


def _ffn_down_loss(name, a, wd, xres, target):
    D = wd.shape[1]

    def epilogue(accs, ex):
        e = ex[0] + 0.5 * accs[0] - ex[1]
        d = e * (1.0 / D)
        return [d, 0.5 * d, jnp.zeros((8, 128), F32) + (0.5 / D) * jnp.sum(e * e)]

    return _mm2d(name, a, wd, "nn", [F32, BF16], tm=256, tn=D, tk=wd.shape[0], n_outer=True, resident=True, extras=[xres, target],
                 tile_stat=True, epilogue=epilogue)


def _ffn_bwd_down(tag, a, dyb, nf, after=None):
    return _mm2d(tag + "_dwd", a, dyb, "tn", F32, tm=a.shape[1] // nf, tn=1024, tk=a.shape[0], resident=True, after=after)


def _ffn_bwd_act(tag, dyb, wd, da_dg, da_du, nf, after=None):
    def act_bwd(accs, ex):
        return [accs[0] * ex[0].astype(F32), accs[0] * ex[1].astype(F32)]

    return _mm2d(tag + "_da", dyb, wd, "nt", BF16, tn=wd.shape[0] // nf, tk=2048, extras=[da_dg, da_du], epilogue=act_bwd, n_out=2,
                 n_outer=True, after=after)


def _ffn_bwd_gate_up(tag, h, dg, du, nf, after=None):
    T, D = h.shape
    F4 = dg.shape[1] // nf
    tm = _tile(D, 512)
    h_spec = pl.BlockSpec((T, tm), lambda j, i, k: (0, i))
    d_spec = pl.BlockSpec((T, F4), lambda j, i, k: (0, j), pipeline_mode=pl.Buffered(1))
    w_spec = pl.BlockSpec((None, tm, F4), lambda j, i, k: (j, i, 0))
    sds = jax.ShapeDtypeStruct((nf, D, F4), F32)
    return _mm(tag + "_dwgu", (nf, D // tm, 1), [(h, h_spec, dg, d_spec, "tn"), (None, None, du, d_spec, "tn")],
               [0, 1], [(tm, F4), (tm, F4)], [], [(sds, w_spec)] * 2, lambda accs, ex: accs, after=after)


def _ffn_bwd_x(tag, dg, du, wg, wu, after):
    T = dg.shape[0]
    nf, D, F4 = wg.shape
    tm, tn = _tile(T, 512), _tile(D, 512)

    def body(dg_ref, du_ref, wg_ref, wu_ref, after_ref, o_ref):
        acc = None
        for j in range(nf):
            cols = slice(j * F4, (j + 1) * F4)
            part = _dot(dg_ref[:, cols], wg_ref[j], "nt") + _dot(du_ref[:, cols], wu_ref[j], "nt")
            acc = part if acc is None else acc + part
        o_ref[...] = acc

    a_spec = pl.BlockSpec((tm, nf * F4), lambda n, i: (i, 0))
    b_spec = pl.BlockSpec((nf, tn, F4), lambda n, i: (0, n, 0), pipeline_mode=pl.Buffered(1))
    return pl.pallas_call(
        body, name=tag + "_dh", grid=(D // tn, T // tm),
        in_specs=[a_spec, a_spec, b_spec, b_spec, pl.BlockSpec(memory_space=pl.ANY)],
        out_specs=pl.BlockSpec((tm, tn), lambda n, i: (i, n)), out_shape=jax.ShapeDtypeStruct((T, D), F32),
        compiler_params=_params(("parallel", "parallel")),
    )(dg, du, wg, wu, after)


def _rms_fwd(name, x, gain):
    R, D = x.shape
    tr = _tile(R, 256, 8)

    def body(x_ref, g_ref, o_ref):
        xv = x_ref[...]
        r = lax.rsqrt(jnp.mean(xv * xv, axis=-1, keepdims=True) + EPS)
        o_ref[...] = (xv * r * g_ref[...]).astype(BF16)

    return pl.pallas_call(
        body, name=name, grid=(R // tr,),
        in_specs=[pl.BlockSpec((tr, D), lambda i: (i, 0)), pl.BlockSpec((1, D), lambda i: (0, 0))],
        out_specs=pl.BlockSpec((tr, D), lambda i: (i, 0)), out_shape=jax.ShapeDtypeStruct((R, D), BF16),
        compiler_params=_params(("parallel",)),
    )(x, gain)


def _rms_bwd(name, dh, x, gain, dres, bscale):
    R, D = x.shape
    tr = _tile(R, 256, 8)

    def body(dh_ref, x_ref, g_ref, dres_ref, dx_ref, dxb_ref, dg_ref):
        xv, dy = x_ref[...], dh_ref[...]
        r = lax.rsqrt(jnp.mean(xv * xv, axis=-1, keepdims=True) + EPS)
        xn = xv * r
        uu = dy * g_ref[...]
        dx = dres_ref[...] + r * (uu - xn * jnp.mean(xn * uu, axis=-1, keepdims=True))
        dx_ref[...] = dx
        dxb_ref[...] = (bscale * dx).astype(BF16)
        part = jnp.sum(dy * xn, axis=0, keepdims=True)

        @pl.when(pl.program_id(0) == 0)
        def _():
            dg_ref[...] = part

        @pl.when(pl.program_id(0) > 0)
        def _():
            dg_ref[...] += part

    row = pl.BlockSpec((tr, D), lambda i: (i, 0))
    vec = pl.BlockSpec((1, D), lambda i: (0, 0))
    return pl.pallas_call(
        body, name=name, grid=(R // tr,), in_specs=[row, row, vec, row], out_specs=[row, row, vec],
        out_shape=[jax.ShapeDtypeStruct((R, D), F32), jax.ShapeDtypeStruct((R, D), BF16), jax.ShapeDtypeStruct((1, D), F32)],
        compiler_params=_params(("arbitrary",)),
    )(dh, x, gain, dres)


def _head_norm(xs, g):
    r = lax.rsqrt(jnp.mean(xs * xs, axis=-1, keepdims=True) + EPS)
    return xs * r * g


def _head_norm_bwd(xs, g, dy):
    r = lax.rsqrt(jnp.mean(xs * xs, axis=-1, keepdims=True) + EPS)
    xn = xs * r
    uu = dy * g
    return r * (uu - xn * jnp.mean(xn * uu, axis=-1, keepdims=True)), jnp.sum(dy * xn, axis=0, keepdims=True)


NORMED = [(C_FQ, FOX_HEADS, 0), (C_FK, FOX_HEADS, 1), (C_SQ, SWA_HEADS, 2), (C_SK, SWA_KV_HEADS, 3), (C_MQ, MEM_HEADS, 4)]
PLAIN = [(C_FV, FOX_HEADS), (C_SV, SWA_KV_HEADS)]


def _prep_fwd(proj, gains):
    T = proj.shape[0]
    tr = _tile(T, 256, 128)

    def body(p_ref, g_ref, o_ref, lf_ref, kt_ref, vt_ref):
        for start, heads, row in NORMED:
            gn = g_ref[row:row + 1, :]
            for hh in range(heads):
                sl = slice(start + hh * HEAD_DIM, start + (hh + 1) * HEAD_DIM)
                y = _head_norm(p_ref[:, sl], gn)
                o_ref[:, sl] = y.astype(BF16)
                if start == C_FK:
                    kt_ref[hh * HEAD_DIM:(hh + 1) * HEAD_DIM, :] = y.T.astype(BF16)
        for start, heads in PLAIN:
            sl = slice(start, start + heads * HEAD_DIM)
            o_ref[:, sl] = p_ref[:, sl].astype(BF16)
        for hh in range(FOX_HEADS):
            sl = slice(C_FV + hh * HEAD_DIM, C_FV + (hh + 1) * HEAD_DIM)
            vt_ref[hh * HEAD_DIM:(hh + 1) * HEAD_DIM, :] = p_ref[:, sl].T.astype(BF16)
        zb = p_ref[:, C_FL:C_FL + HEAD_DIM] + g_ref[5:6, :]
        o_ref[:, C_FL:C_FL + HEAD_DIM] = jnp.zeros((tr, HEAD_DIM), BF16)
        lf_ref[...] = jnp.minimum(zb, 0.0) - jnp.log(1.0 + jnp.exp(-jnp.abs(zb)))

    return pl.pallas_call(
        body, name="prep_fwd", grid=(T // tr,),
        in_specs=[pl.BlockSpec((tr, PROJ_W), lambda i: (i, 0)), pl.BlockSpec((8, 128), lambda i: (0, 0))],
        out_specs=[pl.BlockSpec((tr, PROJ_W), lambda i: (i, 0)), pl.BlockSpec((tr, HEAD_DIM), lambda i: (i, 0)),
                   pl.BlockSpec((FOX_W, tr), lambda i: (0, i)), pl.BlockSpec((FOX_W, tr), lambda i: (0, i))],
        out_shape=[jax.ShapeDtypeStruct((T, PROJ_W), BF16), jax.ShapeDtypeStruct((T, HEAD_DIM), F32),
                   jax.ShapeDtypeStruct((FOX_W, T), BF16), jax.ShapeDtypeStruct((FOX_W, T), BF16)],
        compiler_params=_params(("parallel",)),
    )(proj, gains)


def _prep_bwd(proj, gains, dfq, dfk, dfv, dsq, dsk, dsv, dmq, dlogf):
    T = proj.shape[0]
    tr = _tile(T, 256, 8)
    d_normed = {C_FQ: 0, C_FK: 1, C_SQ: 3, C_SK: 4, C_MQ: 6}
    d_plain = {C_FV: 2, C_SV: 5}

    def body(p_ref, g_ref, *rest):
        d_refs, dlf_ref, o_ref, dg_ref = rest[:7], rest[7], rest[8], rest[9]
        rows = []
        for start, heads, row in NORMED:
            gn = g_ref[row:row + 1, :]
            d_ref = d_refs[d_normed[start]]
            tot = jnp.zeros((1, HEAD_DIM), F32)
            for hh in range(heads):
                sl = slice(start + hh * HEAD_DIM, start + (hh + 1) * HEAD_DIM)
                dx, dgn = _head_norm_bwd(p_ref[:, sl], gn, d_ref[:, hh * HEAD_DIM:(hh + 1) * HEAD_DIM])
                o_ref[:, sl] = dx.astype(BF16)
                tot = tot + dgn
            rows.append(tot)
        for start, heads in PLAIN:
            o_ref[:, start:start + heads * HEAD_DIM] = d_refs[d_plain[start]][...].astype(BF16)
        zb = p_ref[:, C_FL:C_FL + HEAD_DIM] + g_ref[5:6, :]
        lane = lax.broadcasted_iota(jnp.int32, (tr, HEAD_DIM), 1)
        dz = jnp.where(lane < FOX_HEADS, dlf_ref[...] * (1.0 - _sigmoid(zb)), 0.0)
        o_ref[:, C_FL:C_FL + HEAD_DIM] = dz.astype(BF16)
        rows.append(jnp.sum(dz, axis=0, keepdims=True))
        part = jnp.concatenate(rows + [jnp.zeros((2, HEAD_DIM), F32)], axis=0)

        @pl.when(pl.program_id(0) == 0)
        def _():
            dg_ref[...] = part

        @pl.when(pl.program_id(0) > 0)
        def _():
            dg_ref[...] += part

    def rows_of(w):
        return pl.BlockSpec((tr, w), lambda i: (i, 0))

    small = pl.BlockSpec((8, 128), lambda i: (0, 0))
    ds = [dfq, dfk, dfv, dsq, dsk, dsv, dmq]
    return pl.pallas_call(
        body, name="prep_bwd", grid=(T // tr,),
        in_specs=[rows_of(PROJ_W), small] + [rows_of(d.shape[1]) for d in ds] + [rows_of(HEAD_DIM)],
        out_specs=[rows_of(PROJ_W), small],
        out_shape=[jax.ShapeDtypeStruct((T, PROJ_W), BF16), jax.ShapeDtypeStruct((8, 128), F32)],
        compiler_params=_params(("arbitrary",)),
    )(proj, gains, *ds, dlogf)


def _head_norm_rows(x, gain):
    R, W = x.shape

    def body(x_ref, g_ref, o_ref):
        for hh in range(W // HEAD_DIM):
            sl = slice(hh * HEAD_DIM, (hh + 1) * HEAD_DIM)
            o_ref[:, sl] = _head_norm(x_ref[:, sl], g_ref[...]).astype(BF16)

    return pl.pallas_call(body, name="mem_k_norm", out_shape=jax.ShapeDtypeStruct((R, W), BF16))(x, gain)


def _head_norm_rows_bwd(x, gain, dy):
    R, W = x.shape

    def body(x_ref, g_ref, dy_ref, dx_ref, dg_ref):
        tot = jnp.zeros((1, HEAD_DIM), F32)
        for hh in range(W // HEAD_DIM):
            sl = slice(hh * HEAD_DIM, (hh + 1) * HEAD_DIM)
            dx, dgn = _head_norm_bwd(x_ref[:, sl], g_ref[...], dy_ref[:, sl])
            dx_ref[:, sl] = dx.astype(BF16)
            tot = tot + dgn
        dg_ref[...] = tot

    return pl.pallas_call(
        body, name="mem_k_norm_bwd",
        out_shape=[jax.ShapeDtypeStruct((R, W), BF16), jax.ShapeDtypeStruct((1, HEAD_DIM), F32)])(x, gain, dy)


def _cumsum_rows(name, xs, reverse, columns=None):
    T, W = xs[0].shape
    tb = _tile(T, 512, 8)
    nb = T // tb
    n_in = len(xs) + (0 if columns is None else 1)

    def body(*refs):
        o_ref, carry = refs[n_in], refs[n_in + 1]

        @pl.when(pl.program_id(0) == 0)
        def _():
            carry[...] = jnp.zeros_like(carry)

        xv = refs[0][...]
        for x_ref in refs[1:len(xs)]:
            xv = xv + x_ref[...]
        if columns is not None:
            lane = lax.broadcasted_iota(jnp.int32, (tb, W), 1)
            for hh in range(columns.shape[0]):
                xv = xv + jnp.where(lane == hh, refs[len(xs)][hh], 0.0)
        r = lax.broadcasted_iota(jnp.int32, (tb, tb), 0)
        cc = lax.broadcasted_iota(jnp.int32, (tb, tb), 1)
        tri = jnp.where((cc >= r) if reverse else (cc <= r), 1.0, 0.0).astype(F32)
        o_ref[...] = jnp.dot(tri, xv, precision=lax.Precision.HIGHEST, preferred_element_type=F32) + carry[...]
        carry[...] += jnp.sum(xv, axis=0, keepdims=True)

    idx = (lambda i: (nb - 1 - i, 0)) if reverse else (lambda i: (i, 0))
    in_specs = [pl.BlockSpec((tb, W), idx)] * len(xs)
    if columns is not None:
        in_specs.append(pl.BlockSpec((columns.shape[0], tb, 1), lambda i: (0, idx(i)[0], 0)))
    return pl.pallas_call(
        body, name=name, grid=(nb,), in_specs=in_specs, out_specs=pl.BlockSpec((tb, W), idx),
        out_shape=jax.ShapeDtypeStruct((T, W), F32), scratch_shapes=[pltpu.VMEM((1, W), F32)],
        compiler_params=_params(("arbitrary",)),
    )(*xs, *([] if columns is None else [columns]))


def _triangle(nq, by_column):
    if by_column:
        blocks = [(i, j) for j in range(nq) for i in range(j, nq)]
    else:
        blocks = [(i, j) for i in range(nq) for j in range(i + 1)]
    return jnp.asarray(np.array(blocks, np.int32).T)


def _fox_scores_t(k, q, cq_row, ck_rep, on_diagonal):
    n = q.shape[0]
    s = _dot(k, q, "nt") * SCALE + (cq_row - jnp.tile(ck_rep, (1, n // HEAD_DIM)))
    if on_diagonal:
        s = jnp.where(lax.broadcasted_iota(jnp.int32, (n, n), 0) <= lax.broadcasted_iota(jnp.int32, (n, n), 1), s, NEG_INF)
    return s


def _fox_fwd(qkv, v_t, cq_row, ck_rep):
    T = qkv.shape[0]
    tq = _tile(T, 1024)
    nq = T // tq
    steps = nq * (nq + 1) // 2
    HQ, HK = C_FQ // HEAD_DIM, C_FK // HEAD_DIM

    def body(tab, q_ref, k_ref, vt_ref, cq_ref, ck_ref, o_ref, of_ref, lse_ref, m_sc, l_sc, acc_sc):
        i, j = tab[0, pl.program_id(1)], tab[1, pl.program_id(1)]

        @pl.when(j == 0)
        def _():
            m_sc[...] = jnp.full_like(m_sc, NEG_INF)
            l_sc[...] = jnp.zeros_like(l_sc)
            acc_sc[...] = jnp.zeros_like(acc_sc)

        def step(on_diagonal):
            s = _fox_scores_t(k_ref[...], q_ref[...], cq_ref[...], ck_ref[...], on_diagonal)
            m_new = jnp.maximum(m_sc[...], jnp.max(s, axis=0, keepdims=True))
            alpha = jnp.exp(m_sc[...] - m_new)
            p = jnp.exp(s - m_new)
            l_sc[...] = alpha * l_sc[...] + jnp.sum(p, axis=0, keepdims=True)
            acc_sc[...] = alpha * acc_sc[...] + _dot(vt_ref[...], p.astype(BF16), "nn")
            m_sc[...] = m_new

        @pl.when(j < i)
        def _():
            step(False)

        @pl.when(j == i)
        def _():
            step(True)
            o = (acc_sc[...] / l_sc[...]).T
            o_ref[...] = o.astype(BF16)
            of_ref[...] = o
            lse_ref[...] = m_sc[...] + jnp.log(l_sc[...])

    qrow = pl.BlockSpec((None, 1, tq), lambda h, s, tab: (h, 0, tab[0, s]))
    return pl.pallas_call(
        body, name="fox_fwd",
        grid_spec=pltpu.PrefetchScalarGridSpec(
            num_scalar_prefetch=1, grid=(FOX_HEADS, steps),
            in_specs=[pl.BlockSpec((tq, HEAD_DIM), lambda h, s, tab: (tab[0, s], HQ + h)),
                      pl.BlockSpec((tq, HEAD_DIM), lambda h, s, tab: (tab[1, s], HK + h)),
                      pl.BlockSpec((HEAD_DIM, tq), lambda h, s, tab: (h, tab[1, s])), qrow,
                      pl.BlockSpec((None, tq, HEAD_DIM), lambda h, s, tab: (h, tab[1, s], 0))],
            out_specs=[pl.BlockSpec((tq, HEAD_DIM), lambda h, s, tab: (tab[0, s], h)),
                       pl.BlockSpec((tq, HEAD_DIM), lambda h, s, tab: (tab[0, s], h)), qrow],
            scratch_shapes=[pltpu.VMEM((1, tq), F32), pltpu.VMEM((1, tq), F32), pltpu.VMEM((HEAD_DIM, tq), F32)]),
        out_shape=[jax.ShapeDtypeStruct((T, FOX_W), BF16), jax.ShapeDtypeStruct((T, FOX_W), F32),
                   jax.ShapeDtypeStruct((FOX_HEADS, 1, T), F32)],
        compiler_params=_params(("parallel", "arbitrary")),
    )(_triangle(nq, False), qkv, qkv, v_t, cq_row, ck_rep)


def _fox_delta(dmix, out_f32):
    T = out_f32.shape[0]
    tr = _tile(T, 512, 8)

    def body(do_ref, o_ref, d_ref):
        lane = lax.broadcasted_iota(jnp.int32, (tr, HEAD_DIM), 1)
        acc = jnp.zeros((tr, HEAD_DIM), F32)
        for hh in range(FOX_HEADS):
            sl = slice(hh * HEAD_DIM, (hh + 1) * HEAD_DIM)
            d = jnp.sum(do_ref[:, sl].astype(F32) * o_ref[:, sl], axis=-1, keepdims=True)
            acc = jnp.where(lane == hh, d, acc)
        d_ref[...] = acc

    blk = pl.BlockSpec((tr, FOX_W), lambda i: (i, 0))
    return pl.pallas_call(
        body, name="fox_delta", grid=(T // tr,), in_specs=[blk, blk], out_specs=pl.BlockSpec((tr, HEAD_DIM), lambda i: (i, 0)),
        out_shape=jax.ShapeDtypeStruct((T, HEAD_DIM), F32), compiler_params=_params(("parallel",)),
    )(dmix, out_f32)


def _fox_bwd(qkv, k_t, cq_row, ck_rep, delta_row, lse, dmix):
    T = qkv.shape[0]
    tq = _tile(T, 1024)
    nq = T // tq
    steps = nq * (nq + 1) // 2
    HQ, HK, HV = C_FQ // HEAD_DIM, C_FK // HEAD_DIM, C_FV // HEAD_DIM

    def body(tab, q_ref, k_ref, kt_ref, v_ref, cq_ref, ck_ref, delta_ref, lse_ref, do_ref,
             dq_ref, dk_ref, dv_ref, dck_ref, dcq_ref, dk_sc, dv_sc, dc_sc, dqt_sc):
        qi, kj = tab[0, pl.program_id(1)], tab[1, pl.program_id(1)]

        @pl.when(qi == kj)
        def _():
            dk_sc[...] = jnp.zeros_like(dk_sc)
            dv_sc[...] = jnp.zeros_like(dv_sc)
            dc_sc[...] = jnp.zeros_like(dc_sc)

        def step(on_diagonal):
            q, k, v, do = q_ref[...], k_ref[...], v_ref[...], do_ref[...]
            p = jnp.exp(_fox_scores_t(k, q, cq_ref[...], ck_ref[...], on_diagonal) - lse_ref[...])
            dp = _dot(v, do, "nt")
            ds = p * (dp - delta_ref[...])
            dsb = ds.astype(BF16)
            dv_sc[...] += _dot(p.astype(BF16), do, "nn")
            dk_sc[...] += _dot(dsb, q, "nn")
            dc_sc[...] += jnp.sum(ds, axis=1, keepdims=True)
            dq_part = _dot(kt_ref[...], dsb, "nn") * SCALE
            dcq_part = jnp.sum(ds, axis=0, keepdims=True)

            @pl.when(kj == 0)
            def _():
                dqt_sc[qi] = dq_part
                dcq_ref[qi] = dcq_part

            @pl.when(kj > 0)
            def _():
                dqt_sc[qi] += dq_part
                dcq_ref[qi] += dcq_part

            if on_diagonal:
                dq_ref[...] = dqt_sc[qi].T

        @pl.when(qi > kj)
        def _():
            step(False)

        @pl.when(qi == kj)
        def _():
            step(True)

        @pl.when(qi == nq - 1)
        def _():
            dk_ref[...] = dk_sc[...] * SCALE
            dv_ref[...] = dv_sc[...]
            dck_ref[...] = -dc_sc[...]

    def rows(base):
        return pl.BlockSpec((tq, HEAD_DIM), lambda h, s, tab: (tab[0, s], base + h))

    def cols(base):
        return pl.BlockSpec((tq, HEAD_DIM), lambda h, s, tab: (tab[1, s], base + h))

    qrow = pl.BlockSpec((None, 1, tq), lambda h, s, tab: (h, 0, tab[0, s]))
    sds = jax.ShapeDtypeStruct((T, FOX_W), F32)
    return pl.pallas_call(
        body, name="fox_bwd",
        grid_spec=pltpu.PrefetchScalarGridSpec(
            num_scalar_prefetch=1, grid=(FOX_HEADS, steps),
            in_specs=[rows(HQ), cols(HK), pl.BlockSpec((HEAD_DIM, tq), lambda h, s, tab: (h, tab[1, s])), cols(HV), qrow,
                      pl.BlockSpec((None, tq, HEAD_DIM), lambda h, s, tab: (h, tab[1, s], 0)), qrow, qrow, rows(0)],
            out_specs=[cols(0), cols(0), cols(0), pl.BlockSpec((None, tq, 1), lambda h, s, tab: (h, tab[1, s], 0)),
                       pl.BlockSpec((None, nq, 1, tq), lambda h, s, tab: (h, 0, 0, 0))],
            scratch_shapes=[pltpu.VMEM((tq, HEAD_DIM), F32), pltpu.VMEM((tq, HEAD_DIM), F32), pltpu.VMEM((tq, 1), F32),
                            pltpu.VMEM((nq, HEAD_DIM, tq), F32)]),
        out_shape=[sds, sds, sds, jax.ShapeDtypeStruct((FOX_HEADS, T, 1), F32), jax.ShapeDtypeStruct((FOX_HEADS, nq, 1, tq), F32)],
        compiler_params=_params(("parallel", "arbitrary")),
    )(_triangle(nq, True), qkv, qkv, k_t, qkv, cq_row, ck_rep, delta_row, lse, dmix)


GW = SWA_GROUP * HEAD_DIM
GR = SWA_GROUP * WINDOW


def _swa_scores(q_ref, kp_ref, kc_ref, slope_ref, n):
    q = q_ref[...]
    qs = jnp.concatenate([q[:, t * HEAD_DIM:(t + 1) * HEAD_DIM] for t in range(SWA_GROUP)], axis=0)
    kb = jnp.concatenate([kp_ref[...], kc_ref[...]], axis=0)
    r = lax.broadcasted_iota(jnp.int32, (GR, 2 * WINDOW), 0) & (WINDOW - 1)
    jj = lax.broadcasted_iota(jnp.int32, (GR, 2 * WINDOW), 1)
    dist = WINDOW + r - jj
    valid = (dist >= 0) & (dist < WINDOW) & ((n > 0) | (jj >= WINDOW))
    s = _dot(qs, kb, "nt") * SCALE - slope_ref[...] * dist.astype(F32)
    return qs, kb, jnp.where(valid, s, NEG_INF), valid


def _swa_specs():
    HQ, HK, HV = C_SQ // GW, C_SK // HEAD_DIM, C_SV // HEAD_DIM
    q_spec = pl.BlockSpec((WINDOW, GW), lambda g, n: (n, HQ + g))

    def prev(base):
        return pl.BlockSpec((WINDOW, HEAD_DIM), lambda g, n: (jnp.maximum(n - 1, 0), base + g))

    def cur(base):
        return pl.BlockSpec((WINDOW, HEAD_DIM), lambda g, n: (n, base + g))

    col = pl.BlockSpec((None, GR, 1), lambda g, n: (g, 0, 0))
    return q_spec, prev(HK), cur(HK), prev(HV), cur(HV), col


def _swa_fwd(qkv, slopes, sinks):
    T = qkv.shape[0]
    nb = T // WINDOW
    assert C_SQ % GW == 0

    def body(q_ref, kp_ref, kc_ref, vp_ref, vc_ref, slope_ref, sink_ref, o_ref, lse_ref):
        n = pl.program_id(1)
        _, _, s, _ = _swa_scores(q_ref, kp_ref, kc_ref, slope_ref, n)
        m = jnp.maximum(jnp.max(s, axis=-1, keepdims=True), sink_ref[...])
        p = jnp.exp(s - m)
        l = jnp.sum(p, axis=-1, keepdims=True) + jnp.exp(sink_ref[...] - m)
        vb = jnp.concatenate([vp_ref[...], vc_ref[...]], axis=0)
        o = _dot(p.astype(BF16), vb, "nn") / l
        for t in range(SWA_GROUP):
            o_ref[:, t * HEAD_DIM:(t + 1) * HEAD_DIM] = o[t * WINDOW:(t + 1) * WINDOW, :].astype(BF16)
        lse_ref[...] = m + jnp.log(l)

    q_spec, kp, kc, vp, vc, col = _swa_specs()
    return pl.pallas_call(
        body, name="swa_fwd", grid=(SWA_KV_HEADS, nb), in_specs=[q_spec, kp, kc, vp, vc, col, col],
        out_specs=[pl.BlockSpec((WINDOW, GW), lambda g, n: (n, g)), pl.BlockSpec((None, None, GR, 1), lambda g, n: (g, n, 0, 0))],
        out_shape=[jax.ShapeDtypeStruct((T, SWA_HEADS * HEAD_DIM), BF16), jax.ShapeDtypeStruct((SWA_KV_HEADS, nb, GR, 1), F32)],
        compiler_params=_params(("parallel", "arbitrary")),
    )(qkv, qkv, qkv, qkv, qkv, slopes, sinks)


def _swa_bwd(qkv, slopes, sinks, out, lse, dmix):
    T = qkv.shape[0]
    nb = T // WINDOW
    DO = FOX_W // GW
    assert FOX_W % GW == 0

    def body(q_ref, kp_ref, kc_ref, vp_ref, vc_ref, slope_ref, sink_ref, o_ref, lse_ref, do_ref,
             dq_ref, dk_ref, dv_ref, dsink_ref, sink_sc):
        n = pl.program_id(1)

        @pl.when(n == 0)
        def _():
            dk_ref[...] = jnp.zeros_like(dk_ref)
            dv_ref[...] = jnp.zeros_like(dv_ref)
            sink_sc[...] = jnp.zeros_like(sink_sc)

        qs, kb, s, valid = _swa_scores(q_ref, kp_ref, kc_ref, slope_ref, n)
        lse = lse_ref[...]
        p = jnp.where(valid, jnp.exp(s - lse), 0.0)
        vb = jnp.concatenate([vp_ref[...], vc_ref[...]], axis=0)
        do = jnp.concatenate([do_ref[:, t * HEAD_DIM:(t + 1) * HEAD_DIM] for t in range(SWA_GROUP)], axis=0)
        oo = jnp.concatenate([o_ref[:, t * HEAD_DIM:(t + 1) * HEAD_DIM] for t in range(SWA_GROUP)], axis=0)
        dp = _dot(do, vb, "nt")
        delta = jnp.sum(do.astype(F32) * oo.astype(F32), axis=-1, keepdims=True)
        ds = p * (dp - delta)
        dsb = ds.astype(BF16)
        dq = _dot(dsb, kb, "nn") * SCALE
        for t in range(SWA_GROUP):
            dq_ref[:, t * HEAD_DIM:(t + 1) * HEAD_DIM] = dq[t * WINDOW:(t + 1) * WINDOW, :]
        dkb = _dot(dsb, qs, "tn") * SCALE
        dvb = _dot(p.astype(BF16), do, "tn")
        r_prev = pl.ds(pl.multiple_of(jnp.maximum(n - 1, 0) * WINDOW, WINDOW), WINDOW)
        r_cur = pl.ds(pl.multiple_of(n * WINDOW, WINDOW), WINDOW)
        dk_ref[r_prev, :] += dkb[:WINDOW, :]
        dk_ref[r_cur, :] += dkb[WINDOW:, :]
        dv_ref[r_prev, :] += dvb[:WINDOW, :]
        dv_ref[r_cur, :] += dvb[WINDOW:, :]
        sink_sc[...] -= jnp.exp(sink_ref[...] - lse) * delta

        @pl.when(n == nb - 1)
        def _():
            tot = [jnp.zeros((1, 128), F32) + jnp.sum(sink_sc[t * WINDOW:(t + 1) * WINDOW, :]) for t in range(SWA_GROUP)]
            dsink_ref[...] = jnp.concatenate(tot + [jnp.zeros((8 - SWA_GROUP, 128), F32)], axis=0)

    q_spec, kp, kc, vp, vc, col = _swa_specs()
    kv_acc = pl.BlockSpec((T, HEAD_DIM), lambda g, n: (0, g))
    return pl.pallas_call(
        body, name="swa_bwd", grid=(SWA_KV_HEADS, nb),
        in_specs=[q_spec, kp, kc, vp, vc, col, col, pl.BlockSpec((WINDOW, GW), lambda g, n: (n, g)),
                  pl.BlockSpec((None, None, GR, 1), lambda g, n: (g, n, 0, 0)), pl.BlockSpec((WINDOW, GW), lambda g, n: (n, DO + g))],
        out_specs=[pl.BlockSpec((WINDOW, GW), lambda g, n: (n, g)), kv_acc, kv_acc, pl.BlockSpec((None, 8, 128), lambda g, n: (g, 0, 0))],
        out_shape=[jax.ShapeDtypeStruct((T, SWA_HEADS * HEAD_DIM), F32), jax.ShapeDtypeStruct((T, SWA_KV_HEADS * HEAD_DIM), F32),
                   jax.ShapeDtypeStruct((T, SWA_KV_HEADS * HEAD_DIM), F32), jax.ShapeDtypeStruct((SWA_KV_HEADS, 8, 128), F32)],
        scratch_shapes=[pltpu.VMEM((GR, 1), F32)],
        compiler_params=_params(("parallel", "arbitrary")),
    )(qkv, qkv, qkv, qkv, qkv, slopes, sinks, out, lse, dmix)


def _mem_fwd(qkv, mk, mv):
    T, ML = qkv.shape[0], mk.shape[0]
    tq = _tile(T, 512)
    HQ = C_MQ // HEAD_DIM

    def body(q_ref, k_ref, v_ref, o_ref, lse_ref):
        s = _dot(q_ref[...], k_ref[...], "nt") * SCALE
        m = jnp.max(s, axis=-1, keepdims=True)
        p = jnp.exp(s - m)
        l = jnp.sum(p, axis=-1, keepdims=True)
        o_ref[...] = (_dot(p.astype(BF16), v_ref[...], "nn") / l).astype(BF16)
        lse_ref[...] = m + jnp.log(l)

    kv = pl.BlockSpec((ML, HEAD_DIM), lambda h, i: (0, h))
    return pl.pallas_call(
        body, name="mem_fwd", grid=(MEM_HEADS, T // tq),
        in_specs=[pl.BlockSpec((tq, HEAD_DIM), lambda h, i: (i, HQ + h)), kv, kv],
        out_specs=[pl.BlockSpec((tq, HEAD_DIM), lambda h, i: (i, h)), pl.BlockSpec((None, tq, 1), lambda h, i: (h, i, 0))],
        out_shape=[jax.ShapeDtypeStruct((T, MEM_HEADS * HEAD_DIM), BF16), jax.ShapeDtypeStruct((MEM_HEADS, T, 1), F32)],
        compiler_params=_params(("parallel", "arbitrary")),
    )(qkv, mk, mv)


def _mem_bwd(qkv, mk, mv, out, lse, dmix):
    T, ML = qkv.shape[0], mk.shape[0]
    tq = _tile(T, 512)
    HQ = C_MQ // HEAD_DIM
    DO = (FOX_W + SWA_HEADS * HEAD_DIM) // HEAD_DIM

    def body(q_ref, k_ref, v_ref, o_ref, lse_ref, do_ref, dq_ref, dk_ref, dv_ref):
        q, k, v, do = q_ref[...], k_ref[...], v_ref[...], do_ref[...]
        p = jnp.exp(_dot(q, k, "nt") * SCALE - lse_ref[...])
        dp = _dot(do, v, "nt")
        delta = jnp.sum(do.astype(F32) * o_ref[...].astype(F32), axis=-1, keepdims=True)
        dsb = (p * (dp - delta)).astype(BF16)
        dq_ref[...] = _dot(dsb, k, "nn") * SCALE
        dk_part = _dot(dsb, q, "tn") * SCALE
        dv_part = _dot(p.astype(BF16), do, "tn")

        @pl.when(pl.program_id(1) == 0)
        def _():
            dk_ref[...] = dk_part
            dv_ref[...] = dv_part

        @pl.when(pl.program_id(1) > 0)
        def _():
            dk_ref[...] += dk_part
            dv_ref[...] += dv_part

    kv = pl.BlockSpec((ML, HEAD_DIM), lambda h, i: (0, h))
    qb = pl.BlockSpec((tq, HEAD_DIM), lambda h, i: (i, h))
    return pl.pallas_call(
        body, name="mem_bwd", grid=(MEM_HEADS, T // tq),
        in_specs=[pl.BlockSpec((tq, HEAD_DIM), lambda h, i: (i, HQ + h)), kv, kv, qb,
                  pl.BlockSpec((None, tq, 1), lambda h, i: (h, i, 0)), pl.BlockSpec((tq, HEAD_DIM), lambda h, i: (i, DO + h))],
        out_specs=[qb, kv, kv],
        out_shape=[jax.ShapeDtypeStruct((T, MEM_HEADS * HEAD_DIM), F32), jax.ShapeDtypeStruct((ML, MEM_HEADS * HEAD_DIM), F32),
                   jax.ShapeDtypeStruct((ML, MEM_HEADS * HEAD_DIM), F32)],
        compiler_params=_params(("parallel", "arbitrary")),
    )(qkv, mk, mv, out, lse, dmix)


HBM = pl.BlockSpec(memory_space=pltpu.HBM)


def _place():
    x, y, c = lax.axis_index("x"), lax.axis_index("y"), lax.axis_index("c")
    chips = [(1 - x, y), (x, 1 - y), (1 - x, 1 - y)]
    return x, y, c, chips


def _remote(src, dst, send_sem, recv_sem, device):
    return pltpu.make_async_remote_copy(src_ref=src, dst_ref=dst, send_sem=send_sem, recv_sem=recv_sem,
                                        device_id=device, device_id_type=MESH)


def _place_ids():
    x, y, c = lax.axis_index("x"), lax.axis_index("y"), lax.axis_index("c")
    order = [2 * x + y, 2 * (1 - x) + y, 2 * x + (1 - y), 2 * (1 - x) + (1 - y)]
    return jnp.stack([2 * x + y, c] + order).astype(jnp.int32)


def _cast_place(name, w, ids, after):
    R, C = w.shape
    tr = _tile(R, 256, 16)

    def body(ids_ref, w_ref, after_ref, o_ref):
        o_ref[...] = w_ref[...].astype(BF16)

    return pl.pallas_call(
        body, name=name,
        grid_spec=pltpu.PrefetchScalarGridSpec(
            num_scalar_prefetch=1, grid=(R // tr,),
            in_specs=[pl.BlockSpec((tr, C), lambda i, ids: (i, 0)), pl.BlockSpec(memory_space=pl.ANY)],
            out_specs=pl.BlockSpec((None, tr, C), lambda i, ids: (ids[0], i, 0))),
        out_shape=jax.ShapeDtypeStruct((N_CHIPS, R, C), BF16), compiler_params=_params(("parallel",)),
    )(ids, w, after)


SEM = pl.BlockSpec(memory_space=pltpu.SEMAPHORE)
EFFECT = pltpu.SideEffectType.DATAFLOW_SIDE_EFFECTING


def _hbm(a):
    return pltpu.with_memory_space_constraint(a, pltpu.HBM)


def _gather_start(name, placed, after):
    n = len(placed)

    ns = 3 * n

    def body(*refs):
        send, recv = refs[n + 1:n + 1 + ns], refs[n + 1 + ns:n + 1 + 2 * ns]
        buf = refs[n + 1 + 2 * ns:2 * n + 1 + 2 * ns]
        token = refs[2 * n + 1 + 2 * ns]
        x, y, c, chips = _place()
        me = 2 * x + y
        for a in range(n):
            half = buf[a].shape[1] // 2
            mine = buf[a].at[me, pl.ds(c * half, half)]
            for j, (cx, cy) in enumerate(chips):
                _remote(mine, mine, send[3 * a + j], recv[3 * a + j], (cx, cy, c)).start()
        token[...] = jnp.zeros_like(token)

    res = pl.pallas_call(
        body, name=name, in_specs=[HBM] * n + [pl.BlockSpec(memory_space=pl.ANY)],
        out_specs=[SEM] * (2 * ns) + [HBM] * n + [pl.BlockSpec(memory_space=pltpu.VMEM)],
        out_shape=[pltpu.SemaphoreType.DMA(())] * (2 * ns)
        + [pltpu.HBM(s.shape, s.dtype) for s in placed] + [jax.ShapeDtypeStruct((8, 128), F32)],
        input_output_aliases={a: 2 * ns + a for a in range(n)},
        compiler_params=pltpu.CompilerParams(has_side_effects=EFFECT),
    )(*[_hbm(s) for s in placed], after)
    return list(res[:ns]), list(res[ns:2 * ns]), list(res[2 * ns:2 * ns + n]), res[2 * ns + n]


def _gather_wait(name, send, recv, bufs, after):
    n = len(bufs)

    ns = 3 * n

    def body(*refs):
        buf = refs[:n]
        send_ref, recv_ref = refs[n:n + ns], refs[n + ns:n + 2 * ns]
        x, y, c, chips = _place()
        ids = [2 * cx + cy for cx, cy in chips]
        for a in range(n):
            half = buf[a].shape[1] // 2
            for j in range(3):
                landed = buf[a].at[ids[j], pl.ds(c * half, half)]
                cp = _remote(landed, landed, send_ref[3 * a + j], recv_ref[3 * a + j], (x, y, c))
                cp.wait_send()
                cp.wait_recv()

    res = pl.pallas_call(
        body, name=name, in_specs=[HBM] * n + [SEM] * (2 * ns) + [pl.BlockSpec(memory_space=pl.ANY)], out_specs=[HBM] * n,
        out_shape=[pltpu.HBM(s.shape, s.dtype) for s in bufs], input_output_aliases={a: a for a in range(n)},
        compiler_params=pltpu.CompilerParams(has_side_effects=EFFECT),
    )(*bufs, *send, *recv, after)
    return list(res)


def _gather_forward(name, bufs):
    n = len(bufs)

    def body(*refs):
        buf = refs[n:2 * n]
        send, recv = refs[2 * n:]
        x, y, c, chips = _place()
        ids = [2 * cx + cy for cx, cy in chips]
        copies = []
        for a in range(n):
            half = buf[a].shape[1] // 2
            for j in range(3):
                landed = buf[a].at[ids[j], pl.ds(c * half, half)]
                cp = _remote(landed, landed, send.at[a, j], recv.at[a, j], (x, y, 1 - c))
                cp.start()
                copies.append(cp)
        for a in range(n):
            half = buf[a].shape[1] // 2
            for j in range(3):
                landed = buf[a].at[ids[j], pl.ds((1 - c) * half, half)]
                _remote(landed, landed, send.at[a, j], recv.at[a, j], (x, y, c)).wait_recv()
        for cp in copies:
            cp.wait_send()

    return pl.pallas_call(
        body, name=name, in_specs=[HBM] * n, out_specs=[HBM] * n,
        out_shape=[jax.ShapeDtypeStruct(s.shape, s.dtype) for s in bufs], input_output_aliases={a: a for a in range(n)},
        scratch_shapes=[pltpu.SemaphoreType.DMA((n, 3)), pltpu.SemaphoreType.DMA((n, 3))],
    )(*bufs)


def _pair_start(name, grads):
    n = len(grads)
    ns = N_CHIPS * n

    def body(*refs):
        send, recv = refs[2 * n:2 * n + ns], refs[2 * n + ns:2 * n + 2 * ns]
        src = refs[2 * n + 2 * ns:3 * n + 2 * ns]
        land = refs[3 * n + 2 * ns:4 * n + 2 * ns]
        token = refs[4 * n + 2 * ns]
        x, y, c, chips = _place()
        order = [2 * x + y] + [2 * cx + cy for cx, cy in chips]
        for a in range(n):
            half = src[a].shape[1] // 2
            for j in range(N_CHIPS):
                _remote(src[a].at[order[j], pl.ds((1 - c) * half, half)], land[a].at[j],
                        send[N_CHIPS * a + j], recv[N_CHIPS * a + j], (x, y, 1 - c)).start()
        token[...] = jnp.zeros_like(token)

    lands = [jax.ShapeDtypeStruct((N_CHIPS, g.shape[1] // 2, g.shape[2]), g.dtype) for g in grads]
    res = pl.pallas_call(
        body, name=name, in_specs=[HBM] * (2 * n),
        out_specs=[SEM] * (2 * ns) + [HBM] * (2 * n) + [pl.BlockSpec(memory_space=pltpu.VMEM)],
        out_shape=[pltpu.SemaphoreType.DMA(())] * (2 * ns) + [pltpu.HBM(g.shape, g.dtype) for g in grads]
        + [pltpu.HBM(l.shape, l.dtype) for l in lands] + [jax.ShapeDtypeStruct((8, 128), F32)],
        input_output_aliases={a: 2 * ns + a for a in range(2 * n)},
        compiler_params=pltpu.CompilerParams(has_side_effects=EFFECT),
    )(*[_hbm(g) for g in grads], *[_hbm(lax.empty(l.shape, l.dtype)) for l in lands])
    return list(res[:ns]), list(res[ns:2 * ns]), list(res[2 * ns:2 * ns + n]), list(res[2 * ns + n:2 * ns + 2 * n]), res[2 * ns + 2 * n]


def _pair_wait(name, send, recv, grads, lands, after):
    n = len(grads)
    ns = N_CHIPS * n

    def body(*refs):
        src, land = refs[:n], refs[n:2 * n]
        send_ref, recv_ref = refs[2 * n:2 * n + ns], refs[2 * n + ns:2 * n + 2 * ns]
        x, y, c, _ = _place()
        for a in range(n):
            for j in range(N_CHIPS):
                cp = _remote(land[a].at[j], land[a].at[j], send_ref[N_CHIPS * a + j], recv_ref[N_CHIPS * a + j], (x, y, c))
                cp.wait_send()
                cp.wait_recv()

    res = pl.pallas_call(
        body, name=name, in_specs=[HBM] * (2 * n) + [SEM] * (2 * ns) + [pl.BlockSpec(memory_space=pl.ANY)],
        out_specs=[HBM] * (2 * n), out_shape=[pltpu.HBM(g.shape, g.dtype) for g in grads] + [pltpu.HBM(l.shape, l.dtype) for l in lands],
        input_output_aliases={a: a for a in range(2 * n)},
        compiler_params=pltpu.CompilerParams(has_side_effects=EFFECT),
    )(*grads, *lands, *send, *recv, after)
    return list(res[:n]), list(res[n:])


def _chip_start(name, parts):
    n = len(parts)
    ns = 3 * n

    def body(*refs):
        send, recv = refs[2 * n:2 * n + ns], refs[2 * n + ns:2 * n + 2 * ns]
        src = refs[2 * n + 2 * ns:3 * n + 2 * ns]
        land = refs[3 * n + 2 * ns:4 * n + 2 * ns]
        token = refs[4 * n + 2 * ns]
        x, y, c, chips = _place()
        for a in range(n):
            for j, (cx, cy) in enumerate(chips):
                _remote(src[a].at[j], land[a].at[j], send[3 * a + j], recv[3 * a + j], (cx, cy, c)).start()
        token[...] = jnp.zeros_like(token)

    res = pl.pallas_call(
        body, name=name, in_specs=[HBM] * (2 * n),
        out_specs=[SEM] * (2 * ns) + [HBM] * (2 * n) + [pl.BlockSpec(memory_space=pltpu.VMEM)],
        out_shape=[pltpu.SemaphoreType.DMA(())] * (2 * ns) + [pltpu.HBM(p.shape, p.dtype) for p in parts] * 2
        + [jax.ShapeDtypeStruct((8, 128), F32)],
        input_output_aliases={a: 2 * ns + a for a in range(2 * n)},
        compiler_params=pltpu.CompilerParams(has_side_effects=EFFECT),
    )(*[_hbm(p) for p in parts], *[_hbm(lax.empty(p.shape, p.dtype)) for p in parts])
    return list(res[:ns]), list(res[ns:2 * ns]), list(res[2 * ns:2 * ns + n]), list(res[2 * ns + n:2 * ns + 2 * n]), res[2 * ns + 2 * n]


def _chip_wait(name, send, recv, parts, lands, after):
    n = len(parts)
    ns = 3 * n

    def body(*refs):
        src, land = refs[:n], refs[n:2 * n]
        send_ref, recv_ref = refs[2 * n:2 * n + ns], refs[2 * n + ns:2 * n + 2 * ns]
        x, y, c, _ = _place()
        for a in range(n):
            for j in range(3):
                cp = _remote(src[a].at[j], land[a].at[j], send_ref[3 * a + j], recv_ref[3 * a + j], (x, y, c))
                cp.wait_send()
                cp.wait_recv()

    res = pl.pallas_call(
        body, name=name, in_specs=[HBM] * (2 * n) + [SEM] * (2 * ns) + [pl.BlockSpec(memory_space=pl.ANY)],
        out_specs=[HBM] * (2 * n), out_shape=[pltpu.HBM(p.shape, p.dtype) for p in parts] * 2,
        input_output_aliases={a: a for a in range(2 * n)},
        compiler_params=pltpu.CompilerParams(has_side_effects=EFFECT),
    )(*parts, *lands, *send, *recv, after)
    return list(res[n:])


def _pair_share(name, shards):
    n = len(shards)

    def body(*refs):
        buf = refs[n:2 * n]
        send, recv = refs[2 * n:]
        x, y, c, _ = _place()
        copies = []
        for a in range(n):
            half = buf[a].shape[0] // 2
            mine = buf[a].at[pl.ds(c * half, half)]
            cp = _remote(mine, mine, send.at[a], recv.at[a], (x, y, 1 - c))
            cp.start()
            copies.append(cp)
        for a, cp in enumerate(copies):
            half = buf[a].shape[0] // 2
            cp.wait_send()
            theirs = buf[a].at[pl.ds((1 - c) * half, half)]
            _remote(theirs, theirs, send.at[a], recv.at[a], (x, y, c)).wait_recv()

    return pl.pallas_call(
        body, name=name, in_specs=[HBM] * n, out_specs=[HBM] * n,
        out_shape=[jax.ShapeDtypeStruct(s.shape, s.dtype) for s in shards], input_output_aliases={a: a for a in range(n)},
        scratch_shapes=[pltpu.SemaphoreType.DMA((n,)), pltpu.SemaphoreType.DMA((n,))],
    )(*shards)


def _small_start(buf):
    R, W = buf.shape
    ns = N_DEV - 1

    def body(*refs):
        send, recv = refs[2:2 + ns], refs[2 + ns:2 + 2 * ns]
        src, land, token = refs[2 + 2 * ns], refs[3 + 2 * ns], refs[4 + 2 * ns]
        x, y, c, _ = _place()
        me = 4 * x + 2 * y + c
        for k in range(1, N_DEV):
            peer = (x ^ (k >> 2), y ^ ((k >> 1) & 1), c ^ (k & 1))
            _remote(src, land.at[me], send[k - 1], recv[k - 1], peer).start()
        token[...] = jnp.zeros_like(token)

    res = pl.pallas_call(
        body, name="small_start", in_specs=[HBM, HBM],
        out_specs=[SEM] * (2 * ns) + [HBM, HBM, pl.BlockSpec(memory_space=pltpu.VMEM)],
        out_shape=[pltpu.SemaphoreType.DMA(())] * (2 * ns) + [pltpu.HBM((R, W), F32), pltpu.HBM((N_DEV, R, W), F32),
                                                                jax.ShapeDtypeStruct((8, 128), F32)],
        input_output_aliases={0: 2 * ns, 1: 2 * ns + 1},
        compiler_params=pltpu.CompilerParams(has_side_effects=EFFECT),
    )(_hbm(buf), _hbm(jnp.zeros((N_DEV, R, W), F32)))
    return list(res[:ns]), list(res[ns:2 * ns]), res[2 * ns], res[2 * ns + 1], res[2 * ns + 2]


def _small_wait(send, recv, buf, land, after):
    ns = N_DEV - 1

    def body(*refs):
        land_ref = refs[1]
        send_ref, recv_ref = refs[2:2 + ns], refs[2 + ns:2 + 2 * ns]
        x, y, c, _ = _place()
        me = 4 * x + 2 * y + c
        for k in range(1, N_DEV):
            landed = land_ref.at[me ^ k]
            cp = _remote(landed, landed, send_ref[k - 1], recv_ref[k - 1], (x, y, c))
            cp.wait_send()
            cp.wait_recv()

    return pl.pallas_call(
        body, name="small_wait", in_specs=[HBM, HBM] + [SEM] * (2 * ns) + [pl.BlockSpec(memory_space=pl.ANY)],
        out_specs=[HBM, HBM], out_shape=[pltpu.HBM(buf.shape, buf.dtype), pltpu.HBM(land.shape, land.dtype)],
        input_output_aliases={0: 0, 1: 1}, compiler_params=pltpu.CompilerParams(has_side_effects=EFFECT),
    )(buf, land, *send, *recv, after)


def _small_sum(buf, land):
    def body(buf_ref, land_ref, out_ref):
        x, y, c, _ = _place()
        me = 4 * x + 2 * y + c
        total = None
        for d in range(N_DEV):
            term = jnp.where(me == d, buf_ref[...], land_ref[d])
            total = term if total is None else total + term
        out_ref[...] = total

    return pl.pallas_call(body, name="small_sum", out_shape=jax.ShapeDtypeStruct(buf.shape, F32))(buf, land)


def _pair_sum_bf16(name, grad, theirs, ids):
    _, R2, C = theirs.shape
    tr = _tile(R2, 256, 16)
    nrb = R2 // tr

    def body(ids_ref, a_ref, b_ref, o_ref):
        o_ref[...] = (a_ref[...] + b_ref[...]).astype(BF16)

    return pl.pallas_call(
        body, name=name,
        grid_spec=pltpu.PrefetchScalarGridSpec(
            num_scalar_prefetch=1, grid=(3, nrb),
            in_specs=[pl.BlockSpec((None, tr, C), lambda j, i, ids: (ids[3 + j], ids[1] * nrb + i, 0)),
                      pl.BlockSpec((None, tr, C), lambda j, i, ids: (j + 1, i, 0))],
            out_specs=pl.BlockSpec((None, tr, C), lambda j, i, ids: (j, i, 0))),
        out_shape=jax.ShapeDtypeStruct((3, R2, C), BF16), compiler_params=_params(("parallel", "parallel")),
    )(ids, grad, theirs)


def _chip_sum(name, grad, theirs, arrived, ids):
    _, R2, C = theirs.shape
    tr = _tile(R2, 256, 16)
    nrb = R2 // tr

    def body(ids_ref, a_ref, b_ref, r_ref, o_ref):
        tot = a_ref[...] + b_ref[...]
        for j in range(3):
            tot = tot + r_ref[j].astype(F32)
        o_ref[...] = tot

    return pl.pallas_call(
        body, name=name,
        grid_spec=pltpu.PrefetchScalarGridSpec(
            num_scalar_prefetch=1, grid=(nrb,),
            in_specs=[pl.BlockSpec((None, tr, C), lambda i, ids: (ids[0], ids[1] * nrb + i, 0)),
                      pl.BlockSpec((None, tr, C), lambda i, ids: (0, i, 0)),
                      pl.BlockSpec((3, tr, C), lambda i, ids: (0, i, 0))],
            out_specs=pl.BlockSpec((tr, C), lambda i, ids: (ids[1] * nrb + i, 0))),
        out_shape=jax.ShapeDtypeStruct((2 * R2, C), F32), compiler_params=_params(("parallel",)),
    )(ids, grad, theirs, arrived)


def _adamw(name, w, g, m, v, emit_grad=False):
    R, C = w.shape
    tr = _tile(R, 128, 8)
    c1 = 1.0 / (1.0 - ADAM_B1 ** ADAM_STEP)
    c2 = 1.0 / (1.0 - ADAM_B2 ** ADAM_STEP)
    n_out = 4 if emit_grad else 3

    def body(w_ref, g_ref, m_ref, v_ref, d_ref, mo_ref, vo_ref, *rest):
        gv = g_ref[...]
        mn = ADAM_B1 * m_ref[...] + (1.0 - ADAM_B1) * gv
        vn = ADAM_B2 * v_ref[...] + (1.0 - ADAM_B2) * (gv * gv)
        d_ref[...] = -ADAM_LR * ((mn * c1) / (jnp.sqrt(vn * c2) + ADAM_EPS) + ADAM_WD * w_ref[...])
        mo_ref[...] = mn
        vo_ref[...] = vn
        if emit_grad:
            rest[0][...] = gv

    spec = pl.BlockSpec((tr, C), lambda i: (i, 0))
    sds = jax.ShapeDtypeStruct((R, C), F32)
    return pl.pallas_call(body, name=name, grid=(R // tr,), in_specs=[spec] * 4, out_specs=[spec] * n_out, out_shape=[sds] * n_out,
                          compiler_params=_params(("parallel",)))(w, g, m, v)


SMALL = ["ffn1_norm", "mix_norm", "mem_norm", "forget_bias", "fox_q_gain", "fox_k_gain", "swa_q_gain", "swa_k_gain", "swa_sinks",
         "mem_q_gain", "mem_k_gain", "ffn2_norm"]
LARGE = ["ffn1_gate", "ffn1_up", "ffn1_down", "w_in", "w_mem_k", "w_mem_v", "w_out", "ffn2_gate", "ffn2_up", "ffn2_down"]
GATHER_GROUPS = [["ffn1_gate", "ffn1_up"], ["ffn1_down", "w_in", "w_mem_k", "w_mem_v"], ["w_out", "ffn2_gate", "ffn2_up", "ffn2_down"]]
WEIGHTS = ["ffn1_norm", "ffn1_gate", "ffn1_up", "ffn1_down", "mix_norm", "mem_norm", "w_in", "forget_bias", "w_mem_k", "w_mem_v",
           "fox_q_gain", "fox_k_gain", "swa_q_gain", "swa_k_gain", "swa_sinks", "mem_q_gain", "mem_k_gain", "w_out", "ffn2_norm",
           "ffn2_gate", "ffn2_up", "ffn2_down"]


def _pad_proj_cols(w):
    out = jnp.zeros((w.shape[0], PROJ_W), w.dtype)
    for start, width, pstart in REF_GROUPS:
        out = lax.dynamic_update_slice(out, w[:, start:start + width], (0, pstart))
    return out


def _unpad_proj_cols(w):
    return jnp.concatenate([w[:, pstart:pstart + width] for _, width, pstart in REF_GROUPS], axis=1)


def _pack_small(vals):
    flat = jnp.concatenate([vals[k].reshape(-1).astype(F32) for k in SMALL + ["loss"]])
    n = flat.shape[0]
    total = -(-n // 1024) * 1024
    return jnp.pad(flat, (0, total - n)).reshape(total // 128, 128)


def _unpack_small(buf, shapes):
    flat = buf.reshape(-1)
    out, off = {}, 0
    for k in SMALL + ["loss"]:
        size = int(np.prod(shapes[k]))
        out[k] = flat[off:off + size].reshape(shapes[k])
        off += size
    return out


def kernel(x, mem, ffn1_norm, ffn1_gate, ffn1_up, ffn1_down, mix_norm, mem_norm, w_in, forget_bias, w_mem_k, w_mem_v, fox_q_gain, fox_k_gain, swa_q_gain, swa_k_gain, swa_sinks, mem_q_gain, mem_k_gain, w_out, ffn2_norm, ffn2_gate, ffn2_up, ffn2_down, loss_target, m_ffn1_norm, m_ffn1_gate, m_ffn1_up, m_ffn1_down, m_mix_norm, m_mem_norm, m_w_in, m_forget_bias, m_w_mem_k, m_w_mem_v, m_fox_q_gain, m_fox_k_gain, m_swa_q_gain, m_swa_k_gain, m_swa_sinks, m_mem_q_gain, m_mem_k_gain, m_w_out, m_ffn2_norm, m_ffn2_gate, m_ffn2_up, m_ffn2_down, v_ffn1_norm, v_ffn1_gate, v_ffn1_up, v_ffn1_down, v_mix_norm, v_mem_norm, v_w_in, v_forget_bias, v_w_mem_k, v_w_mem_v, v_fox_q_gain, v_fox_k_gain, v_swa_q_gain, v_swa_k_gain, v_swa_sinks, v_mem_q_gain, v_mem_k_gain, v_w_out, v_ffn2_norm, v_ffn2_gate, v_ffn2_up, v_ffn2_down):
    given = dict(locals())
    T, D = x.shape[1], x.shape[2]
    ML = mem.shape[1]
    xin = x.reshape(T, D)
    target = loss_target.reshape(T, D)
    memin = mem.reshape(ML, D)

    ids = _place_ids()
    shard = {k: given[k][0] for k in LARGE}
    started, after = [], ids
    for gi, group in enumerate(GATHER_GROUPS):
        if "w_in" in group:
            shard["w_in"] = _pad_proj_cols(shard["w_in"] + after[0, 0])
        placed = [_cast_place("cast_" + k, shard[k], ids, after) for k in group]
        send, recv, bufs, after = _gather_start("gather_start_%d" % gi, placed, after)
        started.append((send, recv, bufs))

    def arrive(gi, done):
        send, recv, bufs = started[gi]
        bufs = _gather_wait("gather_wait_%d" % gi, send, recv, bufs, done)
        return dict(zip(GATHER_GROUPS[gi], _gather_forward("gather_forward_%d" % gi, bufs)))

    gains = jnp.concatenate([fox_q_gain, fox_k_gain, swa_q_gain, swa_k_gain, mem_q_gain,
                             jnp.pad(forget_bias, ((0, 0), (0, HEAD_DIM - FOX_HEADS))), jnp.zeros((2, HEAD_DIM), F32)], axis=0)
    slopes_np = 2.0 ** (-8.0 * np.arange(1, SWA_HEADS + 1) / SWA_HEADS)
    slopes = jnp.asarray(np.repeat(slopes_np, WINDOW).reshape(SWA_KV_HEADS, GR, 1), F32)
    sinks = jnp.repeat(swa_sinks.reshape(SWA_HEADS), WINDOW).reshape(SWA_KV_HEADS, GR, 1)

    h1 = _rms_fwd("ffn1_norm_fwd", xin, ffn1_norm + after[0, 0])
    full = arrive(0, h1)
    wg1, wu1 = full["ffn1_gate"], full["ffn1_up"]
    fg1, fu1, a1 = _ffn_gu("ffn1_gate_up", h1, wg1, wu1)
    full = arrive(1, a1)
    wd1 = full["ffn1_down"].reshape(-1, D)
    win = full["w_in"].reshape(D, PROJ_W)
    wmk = full["w_mem_k"].reshape(D, MEM_HEADS * HEAD_DIM)
    wmv = full["w_mem_v"].reshape(D, MEM_HEADS * HEAD_DIM)
    x1, h2 = _residual_norm("ffn1_down", a1, wd1, xin, 0.5, mix_norm, 256)
    proj = _mm2d("proj_in", h2, win, "nn", F32, tn=1408, tk=2048, n_outer=True)
    qkv, logf, k_t, v_t = _prep_fwd(proj, gains)
    cum = _cumsum_rows("forget_cumsum", [logf], False)
    cum_h = cum[:, :FOX_HEADS].T
    cq_row = cum_h.reshape(FOX_HEADS, 1, T)
    ck_rep = jnp.broadcast_to(cum_h[:, :, None], (FOX_HEADS, T, HEAD_DIM))
    mn = _rms_fwd("mem_norm_fwd", memin, mem_norm)
    mk_raw = _mm2d("mem_k_proj", mn, wmk, "nn", F32)
    mv = _mm2d("mem_v_proj", mn, wmv, "nn", BF16)
    mk = _head_norm_rows(mk_raw, mem_k_gain)
    out_a, out_a_f32, lse_a = _fox_fwd(qkv, v_t, cq_row, ck_rep)
    out_b, lse_b = _swa_fwd(qkv, slopes, sinks)
    out_c, lse_c = _mem_fwd(qkv, mk, mv)
    mixed = jnp.concatenate([out_a, out_b, out_c], axis=1)
    full = arrive(2, mixed)
    wo = full["w_out"].reshape(-1, D)
    wg2, wu2, wd2 = full["ffn2_gate"], full["ffn2_up"], full["ffn2_down"].reshape(-1, D)
    x2, h3 = _residual_norm("mix_out", mixed, wo, x1, 1.0, ffn2_norm, 512)
    fg2, fu2, a2 = _ffn_gu("ffn2_gate_up", h3, wg2, wu2)
    dx3, dyb3, loss_blocks = _ffn_down_loss("ffn2_down", a2, wd2, x2, target)

    grads, small, res = {}, {"loss": jnp.sum(loss_blocks[::8, 0])}, {}

    def pair_off(tag, group):
        send, recv, own, lands, token = _pair_start("grad_pair_start_" + tag, [grads[k] for k in group])
        return (group, send, recv, own, lands), token

    def chip_off(tag, started, done):
        group, send, recv, own, lands = started
        own, theirs = _pair_wait("grad_pair_wait_" + tag, send, recv, own, lands, done)
        grads.update(zip(group, own))
        to_chips = [_pair_sum_bf16("pair_sum_" + k, grads[k], b, ids) for k, b in zip(group, theirs)]
        send, recv, parts, lands, token = _chip_start("grad_chip_start_" + tag, to_chips)
        return (group, theirs, send, recv, parts, lands), token

    def finish(tag, state, done):
        group, theirs, send, recv, parts, lands = state
        arrived = _chip_wait("grad_chip_wait_" + tag, send, recv, parts, lands, done)
        halves = [_chip_sum("chip_sum_" + k, grads[k], b, r, ids) for k, b, r in zip(group, theirs, arrived)]
        reduced = dict(zip(group, _pair_share("grad_pair_share_" + tag, halves)))
        last = None
        for k in group:
            if k == "w_in":
                gk = _unpad_proj_cols(reduced[k])
                d, mo, vo = _adamw("adamw_" + k, given[k][0], gk, given["m_" + k][0], given["v_" + k][0])
            else:
                d, mo, vo, gk = _adamw("adamw_" + k, given[k][0], reduced[k], given["m_" + k][0], given["v_" + k][0], emit_grad=True)
            res[k] = tuple(t[None] for t in (gk, d, mo, vo))
            last = vo
        return last

    dg2, du2 = _ffn_bwd_act("ffn2", dyb3, wd2, fg2, fu2, N_CHIPS)
    grads["ffn2_down"] = _ffn_bwd_down("ffn2", a2, dyb3, N_CHIPS).reshape(N_CHIPS, -1, D)
    grads["ffn2_gate"], grads["ffn2_up"] = _ffn_bwd_gate_up("ffn2", h3, dg2, du2, N_CHIPS)
    started, token = pair_off("a", ["ffn2_gate", "ffn2_up", "ffn2_down"])
    dh3 = _ffn_bwd_x("ffn2", dg2, du2, wg2, wu2, token)
    state_a, token = chip_off("a", started, dh3)
    dx2, dx2b, small["ffn2_norm"] = _rms_bwd("ffn2_norm_bwd", dh3, x2, ffn2_norm + token[0, 0], dx3, 1.0)
    dmix = _mm2d("mix_out_dx", dx2b, wo, "nt", BF16, tk=2048, n_outer=True)
    grads["w_out"] = _mm2d("mix_out_dw", mixed, dx2b, "tn", F32, tk=T, n_outer=True, resident=True).reshape(N_CHIPS, -1, D)
    delta_row = _fox_delta(dmix, out_a_f32)[:, :FOX_HEADS].T.reshape(FOX_HEADS, 1, T)
    dfq, dfk, dfv, dck, dcq = _fox_bwd(qkv, k_t, cq_row, ck_rep, delta_row, lse_a, dmix)
    dsq, dsk, dsv, dsink = _swa_bwd(qkv, slopes, sinks, out_b, lse_b, dmix)
    dmq, dmk, dmv = _mem_bwd(qkv, mk, mv, out_c, lse_c, dmix)
    small["swa_sinks"] = dsink[:, :SWA_GROUP, 0].reshape(1, SWA_HEADS)
    dcum = jnp.pad(dcq.reshape(FOX_HEADS, T).T, ((0, 0), (0, HEAD_DIM - FOX_HEADS)))
    dlogf = _cumsum_rows("forget_cumsum_bwd", [dcum], True, columns=dck)
    dproj, dgains = _prep_bwd(proj, gains, dfq, dfk, dfv, dsq, dsk, dsv, dmq, dlogf)
    for row, k in enumerate(["fox_q_gain", "fox_k_gain", "swa_q_gain", "swa_k_gain", "mem_q_gain"]):
        small[k] = dgains[row:row + 1, :]
    small["forget_bias"] = dgains[5:6, :FOX_HEADS]
    grads["w_in"] = _mm2d("proj_in_dw", h2, dproj, "tn", F32, tn=1408, tk=T, n_outer=True, resident=True).reshape(N_CHIPS, -1, PROJ_W)
    dmk_raw, small["mem_k_gain"] = _head_norm_rows_bwd(mk_raw, mem_k_gain, dmk)
    dmvb = dmv.astype(BF16)
    grads["w_mem_k"] = _mm2d("mem_k_dw", mn, dmk_raw, "tn", F32).reshape(N_CHIPS, -1, MEM_HEADS * HEAD_DIM)
    grads["w_mem_v"] = _mm2d("mem_v_dw", mn, dmvb, "tn", F32).reshape(N_CHIPS, -1, MEM_HEADS * HEAD_DIM)
    dmn = _mm2d("mem_k_dx", dmk_raw, wmk, "nt", F32)
    dmn = _mm2d("mem_v_dx", dmvb, wmv, "nt", F32, extras=[dmn], epilogue=lambda accs, ex: [ex[0] + accs[0]])
    _, _, small["mem_norm"] = _rms_bwd("mem_norm_bwd", dmn, memin, mem_norm, jnp.zeros_like(memin), 1.0)
    started, token = pair_off("b", ["w_out", "w_in", "w_mem_k", "w_mem_v"])
    dh2 = _mm2d("proj_in_dx", dproj, win, "nt", F32, tm=512, tn=1024, tk=PROJ_W, n_outer=True, resident=True, after=token)
    state_b, token = chip_off("b", started, dh2)
    dx1, dyb1, small["mix_norm"] = _rms_bwd("mix_norm_bwd", dh2, x1, mix_norm + token[0, 0], dx2, 0.5)
    grads["ffn1_down"] = _ffn_bwd_down("ffn1", a1, dyb1, N_CHIPS).reshape(N_CHIPS, -1, D)
    started, token = pair_off("c", ["ffn1_down"])
    dg1, du1 = _ffn_bwd_act("ffn1", dyb1, wd1, fg1, fu1, N_CHIPS, after=token)
    state_c, token = chip_off("c", started, dg1)
    grads["ffn1_gate"], grads["ffn1_up"] = _ffn_bwd_gate_up("ffn1", h1, dg1, du1, N_CHIPS, after=token)
    started, token = pair_off("d", ["ffn1_gate", "ffn1_up"])
    dh1 = _ffn_bwd_x("ffn1", dg1, du1, wg1, wu1, token)
    state_d, token = chip_off("d", started, dh1)
    grad_x, _, small["ffn1_norm"] = _rms_bwd("ffn1_norm_bwd", dh1, xin, ffn1_norm + token[0, 0], dx1, 1.0)

    s_send, s_recv, s_buf, s_land, token = _small_start(_pack_small(small))

    done = finish("a", state_a, token)
    done = finish("b", state_b, done)
    done = finish("c", state_c, done)
    done = finish("d", state_d, done)

    shapes = {k: given[k].shape for k in SMALL}
    shapes["loss"] = ()
    s_buf, s_land = _small_wait(s_send, s_recv, s_buf, s_land, done)
    red_small = _unpack_small(_small_sum(s_buf, s_land), shapes)
    loss = red_small["loss"]
    zero = {"loss": jnp.zeros((), F32)}
    packed = [_pack_small({**zero, **{k: src[k] for k in SMALL}}) for src in (
        {k: given[k] for k in SMALL}, red_small, {k: given["m_" + k] for k in SMALL}, {k: given["v_" + k] for k in SMALL})]
    d_s, m_s, v_s = (_unpack_small(t, shapes) for t in _adamw("adamw_small", *packed))
    for k in SMALL:
        res[k] = (red_small[k], d_s[k], m_s[k], v_s[k])

    outs = [loss, grad_x.reshape(1, T, D)]
    for part in range(4):
        outs += [res[k][part] for k in WEIGHTS]
    return tuple(outs)
```

```python
import functools

import numpy as np
import jax
import jax.numpy as jnp
from jax import lax
from jax.experimental import pallas as pl
from jax.experimental.pallas import tpu as pltpu

F32 = jnp.float32
BF16 = jnp.bfloat16
MESH = pl.DeviceIdType.MESH

HEAD_DIM = 128
FOX_HEADS = 6
SWA_HEADS = 6
SWA_KV_HEADS = 2
SWA_GROUP = SWA_HEADS // SWA_KV_HEADS
MEM_HEADS = 4
WINDOW = 128
EPS = 1e-6
NEG_INF = -1e30
SCALE = HEAD_DIM ** -0.5

C_FQ = 0
C_FK = C_FQ + FOX_HEADS * HEAD_DIM
C_FV = C_FK + FOX_HEADS * HEAD_DIM
C_SQ = C_FV + FOX_HEADS * HEAD_DIM
C_SK = C_SQ + SWA_HEADS * HEAD_DIM
C_SV = C_SK + SWA_KV_HEADS * HEAD_DIM
C_MQ = C_SV + SWA_KV_HEADS * HEAD_DIM
C_FL = C_MQ + MEM_HEADS * HEAD_DIM
PROJ_W = C_FL + HEAD_DIM
FOX_W = FOX_HEADS * HEAD_DIM
REF_GROUPS = [
    (0, FOX_W, C_FQ), (FOX_W, FOX_W, C_FK), (2 * FOX_W, FOX_W, C_FV), (3 * FOX_W, FOX_HEADS, C_FL),
    (3 * FOX_W + FOX_HEADS, SWA_HEADS * HEAD_DIM, C_SQ),
    (3 * FOX_W + FOX_HEADS + SWA_HEADS * HEAD_DIM, SWA_KV_HEADS * HEAD_DIM, C_SK),
    (3 * FOX_W + FOX_HEADS + (SWA_HEADS + SWA_KV_HEADS) * HEAD_DIM, SWA_KV_HEADS * HEAD_DIM, C_SV),
    (3 * FOX_W + FOX_HEADS + (SWA_HEADS + 2 * SWA_KV_HEADS) * HEAD_DIM, MEM_HEADS * HEAD_DIM, C_MQ),
]

ADAM_LR = 0.001
ADAM_B1 = 0.9
ADAM_B2 = 0.999
ADAM_EPS = 1e-08
ADAM_WD = 0.01
ADAM_STEP = 10

V7X_VMEM_LIMIT = 56 * 1024 * 1024
N_CHIPS = 4
N_DEV = 8


def _tile(n, pref, mult=128):
    t = (min(pref, n) // mult) * mult
    while t >= mult:
        if n % t == 0:
            return t
        t -= mult
    return n


def _params(sem):
    return pltpu.CompilerParams(dimension_semantics=sem, vmem_limit_bytes=V7X_VMEM_LIMIT)


_DIMS = {"nn": (((1,), (0,)), ((), ())), "nt": (((1,), (1,)), ((), ())), "tn": (((0,), (0,)), ((), ()))}


def _dot(a, b, mode):
    return lax.dot_general(a, b, _DIMS[mode], preferred_element_type=F32)


def _mm(name, grid, pairs, acc_of, acc_shapes, extras, outs, epilogue, after=None):
    n_p, n_e, n_o, n_a = len(pairs), len(extras), len(outs), len(acc_shapes)
    n_w = 0 if after is None else 1
    nk = grid[2]
    n_in = sum(1 if a is None else 2 for a, *_ in pairs)

    def body(*refs):
        ex = refs[n_in:n_in + n_e]
        out = refs[n_in + n_e + n_w:n_in + n_e + n_w + n_o]
        accs = refs[n_in + n_e + n_w + n_o:]
        parts = [None] * n_a
        at = 0
        for p in range(n_p):
            if pairs[p][0] is None:
                a_ref, b_ref = refs[0], refs[at]
                at += 1
            else:
                a_ref, b_ref = refs[at], refs[at + 1]
                at += 2
            d = _dot(a_ref[...], b_ref[...], pairs[p][4])
            parts[acc_of[p]] = d if parts[acc_of[p]] is None else parts[acc_of[p]] + d

        def finish(vals):
            for o, r in zip(out, epilogue(vals, [e[...] for e in ex])):
                o[...] = r.astype(o.dtype)

        if nk == 1:
            finish(parts)
            return
        k = pl.program_id(2)

        @pl.when(k == 0)
        def _():
            for a, d in zip(accs, parts):
                a[...] = d

        @pl.when((k > 0) & (k < nk - 1))
        def _():
            for a, d in zip(accs, parts):
                a[...] += d

        @pl.when(k == nk - 1)
        def _():
            finish([a[...] + d for a, d in zip(accs, parts)])

    in_specs, args = [], []
    for a, a_spec, b, b_spec, _ in pairs:
        if a is not None:
            in_specs.append(a_spec)
            args.append(a)
        in_specs.append(b_spec)
        args.append(b)
    for e, e_spec in extras:
        in_specs.append(e_spec)
        args.append(e)
    if after is not None:
        in_specs.append(pl.BlockSpec(memory_space=pl.ANY))
        args.append(after)
    res = pl.pallas_call(
        body, name=name, grid=grid, in_specs=in_specs,
        out_specs=[s for _, s in outs], out_shape=[o for o, _ in outs],
        scratch_shapes=[pltpu.VMEM(s, F32) for s in acc_shapes] if nk > 1 else [],
        compiler_params=_params(("parallel", "parallel", "arbitrary")),
    )(*args)
    return res


def _mm2d(name, a, b, mode, out_dtype, tm=512, tn=1024, tk=1024, extras=(), epilogue=None, n_out=1, after=None, n_outer=False,
          resident=False, rows=(), tile_stat=False):
    if mode == "nn":
        (M, K), N = a.shape, b.shape[1]
    elif mode == "nt":
        (M, K), N = a.shape, b.shape[0]
    else:
        (K, M), N = a.shape, b.shape[1]
    tm, tn, tk = _tile(M, tm), _tile(N, tn), _tile(K, tk)
    assert not resident or tk == K

    def spec(shape, index, single=False):
        mode_kw = {"pipeline_mode": pl.Buffered(1)} if single else {}
        if n_outer:
            return pl.BlockSpec(shape, lambda j, i, k: index(i, j, k), **mode_kw)
        return pl.BlockSpec(shape, index, **mode_kw)

    single_a, single_b = resident and not n_outer, resident and n_outer
    a_spec = spec((tk, tm), lambda i, j, k: (k, i), single_a) if mode == "tn" else spec((tm, tk), lambda i, j, k: (i, k), single_a)
    b_spec = spec((tn, tk), lambda i, j, k: (j, k), single_b) if mode == "nt" else spec((tk, tn), lambda i, j, k: (k, j), single_b)
    mn = spec((tm, tn), lambda i, j, k: (i, j))
    if epilogue is None:
        epilogue = lambda accs, ex: [accs[0]]
    if not isinstance(out_dtype, (list, tuple)):
        out_dtype = [out_dtype] * n_out
    grid = (N // tn, M // tm, K // tk) if n_outer else (M // tm, N // tn, K // tk)
    outs = [(jax.ShapeDtypeStruct((M, N), d), mn) for d in out_dtype]
    if tile_stat:
        assert tn == N
        outs.append((jax.ShapeDtypeStruct((8 * (M // tm), 128), F32), spec((8, 128), lambda i, j, k: (i, 0))))
    res = _mm(name, grid, [(a, a_spec, b, b_spec, mode)], [0], [(tm, tn)],
              [(e, mn) for e in extras] + [(r, spec((1, tn), lambda i, j, k: (0, j))) for r in rows], outs, epilogue, after=after)
    return res[0] if len(res) == 1 else res


def _sigmoid(x):
    return 1.0 / (1.0 + jnp.exp(-x))


def _sigmoid_fast(x):
    return pl.reciprocal(1.0 + jnp.exp(-x), approx=True)


def _ffn_gu(name, h, wg, wu):
    T, D = h.shape
    nf, _, F4 = wg.shape
    tm, tk = _tile(T, 512), _tile(D, 2048)
    a_spec = pl.BlockSpec((tm, tk), lambda j, i, k: (i, k))
    b_spec = pl.BlockSpec((None, tk, F4), lambda j, i, k: (j, k, 0))
    o_spec = pl.BlockSpec((tm, F4), lambda j, i, k: (i, j))

    def epilogue(accs, ex):
        g, u = accs
        s = _sigmoid_fast(g)
        gs = g * s
        return [(s + s * (g - gs)) * u, gs, gs * u]

    sds = jax.ShapeDtypeStruct((T, nf * F4), BF16)
    return _mm(name, (nf, T // tm, D // tk), [(h, a_spec, wg, b_spec, "nn"), (None, None, wu, b_spec, "nn")], [0, 1],
               [(tm, F4), (tm, F4)], [], [(sds, o_spec)] * 3, epilogue)


def _rms_rows(x, gain):
    return x * lax.rsqrt(jnp.mean(x * x, axis=-1, keepdims=True) + EPS) * gain


def _residual_norm(name, a, w, xres, scale, gain, tm):
    def epilogue(accs, ex):
        y = ex[0] + scale * accs[0]
        return [y, _rms_rows(y, ex[1])]

    return _mm2d(name, a, w, "nn", [F32, BF16], tm=tm, tn=w.shape[1], tk=w.shape[0], n_outer=True, resident=True, extras=[xres],
                 rows=[gain], epilogue=epilogue)


def _ffn_down_loss(name, a, wd, xres, target):
    D = wd.shape[1]

    def epilogue(accs, ex):
        e = ex[0] + 0.5 * accs[0] - ex[1]
        d = e * (1.0 / D)
        return [d, 0.5 * d, jnp.zeros((8, 128), F32) + (0.5 / D) * jnp.sum(e * e)]

    return _mm2d(name, a, wd, "nn", [F32, BF16], tm=256, tn=D, tk=wd.shape[0], n_outer=True, resident=True, extras=[xres, target],
                 tile_stat=True, epilogue=epilogue)


def _ffn_bwd_down(tag, a, dyb, nf, after=None):
    return _mm2d(tag + "_dwd", a, dyb, "tn", F32, tm=a.shape[1] // nf, tn=1024, tk=a.shape[0], resident=True, after=after)


def _ffn_bwd_act(tag, dyb, wd, da_dg, da_du, nf, after=None):
    def act_bwd(accs, ex):
        return [accs[0] * ex[0].astype(F32), accs[0] * ex[1].astype(F32)]

    return _mm2d(tag + "_da", dyb, wd, "nt", BF16, tn=wd.shape[0] // nf, tk=2048, extras=[da_dg, da_du], epilogue=act_bwd, n_out=2,
                 n_outer=True, after=after)


def _ffn_bwd_gate_up(tag, h, dg, du, nf, after=None):
    T, D = h.shape
    F4 = dg.shape[1] // nf
    tm = _tile(D, 512)
    h_spec = pl.BlockSpec((T, tm), lambda j, i, k: (0, i))
    d_spec = pl.BlockSpec((T, F4), lambda j, i, k: (0, j), pipeline_mode=pl.Buffered(1))
    w_spec = pl.BlockSpec((None, tm, F4), lambda j, i, k: (j, i, 0))
    sds = jax.ShapeDtypeStruct((nf, D, F4), F32)
    return _mm(tag + "_dwgu", (nf, D // tm, 1), [(h, h_spec, dg, d_spec, "tn"), (None, None, du, d_spec, "tn")],
               [0, 1], [(tm, F4), (tm, F4)], [], [(sds, w_spec)] * 2, lambda accs, ex: accs, after=after)


def _ffn_bwd_x(tag, dg, du, wg, wu, after):
    T = dg.shape[0]
    nf, D, F4 = wg.shape
    tm, tn = _tile(T, 512), _tile(D, 512)

    def body(dg_ref, du_ref, wg_ref, wu_ref, after_ref, o_ref):
        acc = None
        for j in range(nf):
            cols = slice(j * F4, (j + 1) * F4)
            part = _dot(dg_ref[:, cols], wg_ref[j], "nt") + _dot(du_ref[:, cols], wu_ref[j], "nt")
            acc = part if acc is None else acc + part
        o_ref[...] = acc

    a_spec = pl.BlockSpec((tm, nf * F4), lambda n, i: (i, 0))
    b_spec = pl.BlockSpec((nf, tn, F4), lambda n, i: (0, n, 0), pipeline_mode=pl.Buffered(1))
    return pl.pallas_call(
        body, name=tag + "_dh", grid=(D // tn, T // tm),
        in_specs=[a_spec, a_spec, b_spec, b_spec, pl.BlockSpec(memory_space=pl.ANY)],
        out_specs=pl.BlockSpec((tm, tn), lambda n, i: (i, n)), out_shape=jax.ShapeDtypeStruct((T, D), F32),
        compiler_params=_params(("parallel", "parallel")),
    )(dg, du, wg, wu, after)


def _rms_fwd(name, x, gain):
    R, D = x.shape
    tr = _tile(R, 256, 8)

    def body(x_ref, g_ref, o_ref):
        xv = x_ref[...]
        r = lax.rsqrt(jnp.mean(xv * xv, axis=-1, keepdims=True) + EPS)
        o_ref[...] = (xv * r * g_ref[...]).astype(BF16)

    return pl.pallas_call(
        body, name=name, grid=(R // tr,),
        in_specs=[pl.BlockSpec((tr, D), lambda i: (i, 0)), pl.BlockSpec((1, D), lambda i: (0, 0))],
        out_specs=pl.BlockSpec((tr, D), lambda i: (i, 0)), out_shape=jax.ShapeDtypeStruct((R, D), BF16),
        compiler_params=_params(("parallel",)),
    )(x, gain)


def _rms_bwd(name, dh, x, gain, dres, bscale):
    R, D = x.shape
    tr = _tile(R, 256, 8)

    def body(dh_ref, x_ref, g_ref, dres_ref, dx_ref, dxb_ref, dg_ref):
        xv, dy = x_ref[...], dh_ref[...]
        r = lax.rsqrt(jnp.mean(xv * xv, axis=-1, keepdims=True) + EPS)
        xn = xv * r
        uu = dy * g_ref[...]
        dx = dres_ref[...] + r * (uu - xn * jnp.mean(xn * uu, axis=-1, keepdims=True))
        dx_ref[...] = dx
        dxb_ref[...] = (bscale * dx).astype(BF16)
        part = jnp.sum(dy * xn, axis=0, keepdims=True)

        @pl.when(pl.program_id(0) == 0)
        def _():
            dg_ref[...] = part

        @pl.when(pl.program_id(0) > 0)
        def _():
            dg_ref[...] += part

    row = pl.BlockSpec((tr, D), lambda i: (i, 0))
    vec = pl.BlockSpec((1, D), lambda i: (0, 0))
    return pl.pallas_call(
        body, name=name, grid=(R // tr,), in_specs=[row, row, vec, row], out_specs=[row, row, vec],
        out_shape=[jax.ShapeDtypeStruct((R, D), F32), jax.ShapeDtypeStruct((R, D), BF16), jax.ShapeDtypeStruct((1, D), F32)],
        compiler_params=_params(("arbitrary",)),
    )(dh, x, gain, dres)


def _head_norm(xs, g):
    r = lax.rsqrt(jnp.mean(xs * xs, axis=-1, keepdims=True) + EPS)
    return xs * r * g


def _head_norm_bwd(xs, g, dy):
    r = lax.rsqrt(jnp.mean(xs * xs, axis=-1, keepdims=True) + EPS)
    xn = xs * r
    uu = dy * g
    return r * (uu - xn * jnp.mean(xn * uu, axis=-1, keepdims=True)), jnp.sum(dy * xn, axis=0, keepdims=True)


NORMED = [(C_FQ, FOX_HEADS, 0), (C_FK, FOX_HEADS, 1), (C_SQ, SWA_HEADS, 2), (C_SK, SWA_KV_HEADS, 3), (C_MQ, MEM_HEADS, 4)]
PLAIN = [(C_FV, FOX_HEADS), (C_SV, SWA_KV_HEADS)]


def _prep_fwd(proj, gains):
    T = proj.shape[0]
    tr = _tile(T, 256, 128)

    def body(p_ref, g_ref, o_ref, lf_ref, kt_ref, vt_ref):
        for start, heads, row in NORMED:
            gn = g_ref[row:row + 1, :]
            for hh in range(heads):
                sl = slice(start + hh * HEAD_DIM, start + (hh + 1) * HEAD_DIM)
                y = _head_norm(p_ref[:, sl], gn)
                o_ref[:, sl] = y.astype(BF16)
                if start == C_FK:
                    kt_ref[hh * HEAD_DIM:(hh + 1) * HEAD_DIM, :] = y.T.astype(BF16)
        for start, heads in PLAIN:
            sl = slice(start, start + heads * HEAD_DIM)
            o_ref[:, sl] = p_ref[:, sl].astype(BF16)
        for hh in range(FOX_HEADS):
            sl = slice(C_FV + hh * HEAD_DIM, C_FV + (hh + 1) * HEAD_DIM)
            vt_ref[hh * HEAD_DIM:(hh + 1) * HEAD_DIM, :] = p_ref[:, sl].T.astype(BF16)
        zb = p_ref[:, C_FL:C_FL + HEAD_DIM] + g_ref[5:6, :]
        o_ref[:, C_FL:C_FL + HEAD_DIM] = jnp.zeros((tr, HEAD_DIM), BF16)
        lf_ref[...] = jnp.minimum(zb, 0.0) - jnp.log(1.0 + jnp.exp(-jnp.abs(zb)))

    return pl.pallas_call(
        body, name="prep_fwd", grid=(T // tr,),
        in_specs=[pl.BlockSpec((tr, PROJ_W), lambda i: (i, 0)), pl.BlockSpec((8, 128), lambda i: (0, 0))],
        out_specs=[pl.BlockSpec((tr, PROJ_W), lambda i: (i, 0)), pl.BlockSpec((tr, HEAD_DIM), lambda i: (i, 0)),
                   pl.BlockSpec((FOX_W, tr), lambda i: (0, i)), pl.BlockSpec((FOX_W, tr), lambda i: (0, i))],
        out_shape=[jax.ShapeDtypeStruct((T, PROJ_W), BF16), jax.ShapeDtypeStruct((T, HEAD_DIM), F32),
                   jax.ShapeDtypeStruct((FOX_W, T), BF16), jax.ShapeDtypeStruct((FOX_W, T), BF16)],
        compiler_params=_params(("parallel",)),
    )(proj, gains)


def _prep_bwd(proj, gains, dfq, dfk, dfv, dsq, dsk, dsv, dmq, dlogf):
    T = proj.shape[0]
    tr = _tile(T, 256, 8)
    d_normed = {C_FQ: 0, C_FK: 1, C_SQ: 3, C_SK: 4, C_MQ: 6}
    d_plain = {C_FV: 2, C_SV: 5}

    def body(p_ref, g_ref, *rest):
        d_refs, dlf_ref, o_ref, dg_ref = rest[:7], rest[7], rest[8], rest[9]
        rows = []
        for start, heads, row in NORMED:
            gn = g_ref[row:row + 1, :]
            d_ref = d_refs[d_normed[start]]
            tot = jnp.zeros((1, HEAD_DIM), F32)
            for hh in range(heads):
                sl = slice(start + hh * HEAD_DIM, start + (hh + 1) * HEAD_DIM)
                dx, dgn = _head_norm_bwd(p_ref[:, sl], gn, d_ref[:, hh * HEAD_DIM:(hh + 1) * HEAD_DIM])
                o_ref[:, sl] = dx.astype(BF16)
                tot = tot + dgn
            rows.append(tot)
        for start, heads in PLAIN:
            o_ref[:, start:start + heads * HEAD_DIM] = d_refs[d_plain[start]][...].astype(BF16)
        zb = p_ref[:, C_FL:C_FL + HEAD_DIM] + g_ref[5:6, :]
        lane = lax.broadcasted_iota(jnp.int32, (tr, HEAD_DIM), 1)
        dz = jnp.where(lane < FOX_HEADS, dlf_ref[...] * (1.0 - _sigmoid(zb)), 0.0)
        o_ref[:, C_FL:C_FL + HEAD_DIM] = dz.astype(BF16)
        rows.append(jnp.sum(dz, axis=0, keepdims=True))
        part = jnp.concatenate(rows + [jnp.zeros((2, HEAD_DIM), F32)], axis=0)

        @pl.when(pl.program_id(0) == 0)
        def _():
            dg_ref[...] = part

        @pl.when(pl.program_id(0) > 0)
        def _():
            dg_ref[...] += part

    def rows_of(w):
        return pl.BlockSpec((tr, w), lambda i: (i, 0))

    small = pl.BlockSpec((8, 128), lambda i: (0, 0))
    ds = [dfq, dfk, dfv, dsq, dsk, dsv, dmq]
    return pl.pallas_call(
        body, name="prep_bwd", grid=(T // tr,),
        in_specs=[rows_of(PROJ_W), small] + [rows_of(d.shape[1]) for d in ds] + [rows_of(HEAD_DIM)],
        out_specs=[rows_of(PROJ_W), small],
        out_shape=[jax.ShapeDtypeStruct((T, PROJ_W), BF16), jax.ShapeDtypeStruct((8, 128), F32)],
        compiler_params=_params(("arbitrary",)),
    )(proj, gains, *ds, dlogf)


def _head_norm_rows(x, gain):
    R, W = x.shape

    def body(x_ref, g_ref, o_ref):
        for hh in range(W // HEAD_DIM):
            sl = slice(hh * HEAD_DIM, (hh + 1) * HEAD_DIM)
            o_ref[:, sl] = _head_norm(x_ref[:, sl], g_ref[...]).astype(BF16)

    return pl.pallas_call(body, name="mem_k_norm", out_shape=jax.ShapeDtypeStruct((R, W), BF16))(x, gain)


def _head_norm_rows_bwd(x, gain, dy):
    R, W = x.shape

    def body(x_ref, g_ref, dy_ref, dx_ref, dg_ref):
        tot = jnp.zeros((1, HEAD_DIM), F32)
        for hh in range(W // HEAD_DIM):
            sl = slice(hh * HEAD_DIM, (hh + 1) * HEAD_DIM)
            dx, dgn = _head_norm_bwd(x_ref[:, sl], g_ref[...], dy_ref[:, sl])
            dx_ref[:, sl] = dx.astype(BF16)
            tot = tot + dgn
        dg_ref[...] = tot

    return pl.pallas_call(
        body, name="mem_k_norm_bwd",
        out_shape=[jax.ShapeDtypeStruct((R, W), BF16), jax.ShapeDtypeStruct((1, HEAD_DIM), F32)])(x, gain, dy)


def _cumsum_rows(name, xs, reverse, columns=None):
    T, W = xs[0].shape
    tb = _tile(T, 512, 8)
    nb = T // tb
    n_in = len(xs) + (0 if columns is None else 1)

    def body(*refs):
        o_ref, carry = refs[n_in], refs[n_in + 1]

        @pl.when(pl.program_id(0) == 0)
        def _():
            carry[...] = jnp.zeros_like(carry)

        xv = refs[0][...]
        for x_ref in refs[1:len(xs)]:
            xv = xv + x_ref[...]
        if columns is not None:
            lane = lax.broadcasted_iota(jnp.int32, (tb, W), 1)
            for hh in range(columns.shape[0]):
                xv = xv + jnp.where(lane == hh, refs[len(xs)][hh], 0.0)
        r = lax.broadcasted_iota(jnp.int32, (tb, tb), 0)
        cc = lax.broadcasted_iota(jnp.int32, (tb, tb), 1)
        tri = jnp.where((cc >= r) if reverse else (cc <= r), 1.0, 0.0).astype(F32)
        o_ref[...] = jnp.dot(tri, xv, precision=lax.Precision.HIGHEST, preferred_element_type=F32) + carry[...]
        carry[...] += jnp.sum(xv, axis=0, keepdims=True)

    idx = (lambda i: (nb - 1 - i, 0)) if reverse else (lambda i: (i, 0))
    in_specs = [pl.BlockSpec((tb, W), idx)] * len(xs)
    if columns is not None:
        in_specs.append(pl.BlockSpec((columns.shape[0], tb, 1), lambda i: (0, idx(i)[0], 0)))
    return pl.pallas_call(
        body, name=name, grid=(nb,), in_specs=in_specs, out_specs=pl.BlockSpec((tb, W), idx),
        out_shape=jax.ShapeDtypeStruct((T, W), F32), scratch_shapes=[pltpu.VMEM((1, W), F32)],
        compiler_params=_params(("arbitrary",)),
    )(*xs, *([] if columns is None else [columns]))


def _triangle(nq, by_column):
    if by_column:
        blocks = [(i, j) for j in range(nq) for i in range(j, nq)]
    else:
        blocks = [(i, j) for i in range(nq) for j in range(i + 1)]
    return jnp.asarray(np.array(blocks, np.int32).T)


def _fox_scores_t(k, q, cq_row, ck_rep, on_diagonal):
    n = q.shape[0]
    s = _dot(k, q, "nt") * SCALE + (cq_row - jnp.tile(ck_rep, (1, n // HEAD_DIM)))
    if on_diagonal:
        s = jnp.where(lax.broadcasted_iota(jnp.int32, (n, n), 0) <= lax.broadcasted_iota(jnp.int32, (n, n), 1), s, NEG_INF)
    return s


def _fox_fwd(qkv, v_t, cq_row, ck_rep):
    T = qkv.shape[0]
    tq = _tile(T, 1024)
    nq = T // tq
    steps = nq * (nq + 1) // 2
    HQ, HK = C_FQ // HEAD_DIM, C_FK // HEAD_DIM

    def body(tab, q_ref, k_ref, vt_ref, cq_ref, ck_ref, o_ref, of_ref, lse_ref, m_sc, l_sc, acc_sc):
        i, j = tab[0, pl.program_id(1)], tab[1, pl.program_id(1)]

        @pl.when(j == 0)
        def _():
            m_sc[...] = jnp.full_like(m_sc, NEG_INF)
            l_sc[...] = jnp.zeros_like(l_sc)
            acc_sc[...] = jnp.zeros_like(acc_sc)

        def step(on_diagonal):
            s = _fox_scores_t(k_ref[...], q_ref[...], cq_ref[...], ck_ref[...], on_diagonal)
            m_new = jnp.maximum(m_sc[...], jnp.max(s, axis=0, keepdims=True))
            alpha = jnp.exp(m_sc[...] - m_new)
            p = jnp.exp(s - m_new)
            l_sc[...] = alpha * l_sc[...] + jnp.sum(p, axis=0, keepdims=True)
            acc_sc[...] = alpha * acc_sc[...] + _dot(vt_ref[...], p.astype(BF16), "nn")
            m_sc[...] = m_new

        @pl.when(j < i)
        def _():
            step(False)

        @pl.when(j == i)
        def _():
            step(True)
            o = (acc_sc[...] / l_sc[...]).T
            o_ref[...] = o.astype(BF16)
            of_ref[...] = o
            lse_ref[...] = m_sc[...] + jnp.log(l_sc[...])

    qrow = pl.BlockSpec((None, 1, tq), lambda h, s, tab: (h, 0, tab[0, s]))
    return pl.pallas_call(
        body, name="fox_fwd",
        grid_spec=pltpu.PrefetchScalarGridSpec(
            num_scalar_prefetch=1, grid=(FOX_HEADS, steps),
            in_specs=[pl.BlockSpec((tq, HEAD_DIM), lambda h, s, tab: (tab[0, s], HQ + h)),
                      pl.BlockSpec((tq, HEAD_DIM), lambda h, s, tab: (tab[1, s], HK + h)),
                      pl.BlockSpec((HEAD_DIM, tq), lambda h, s, tab: (h, tab[1, s])), qrow,
                      pl.BlockSpec((None, tq, HEAD_DIM), lambda h, s, tab: (h, tab[1, s], 0))],
            out_specs=[pl.BlockSpec((tq, HEAD_DIM), lambda h, s, tab: (tab[0, s], h)),
                       pl.BlockSpec((tq, HEAD_DIM), lambda h, s, tab: (tab[0, s], h)), qrow],
            scratch_shapes=[pltpu.VMEM((1, tq), F32), pltpu.VMEM((1, tq), F32), pltpu.VMEM((HEAD_DIM, tq), F32)]),
        out_shape=[jax.ShapeDtypeStruct((T, FOX_W), BF16), jax.ShapeDtypeStruct((T, FOX_W), F32),
                   jax.ShapeDtypeStruct((FOX_HEADS, 1, T), F32)],
        compiler_params=_params(("parallel", "arbitrary")),
    )(_triangle(nq, False), qkv, qkv, v_t, cq_row, ck_rep)


def _fox_delta(dmix, out_f32):
    T = out_f32.shape[0]
    tr = _tile(T, 512, 8)

    def body(do_ref, o_ref, d_ref):
        lane = lax.broadcasted_iota(jnp.int32, (tr, HEAD_DIM), 1)
        acc = jnp.zeros((tr, HEAD_DIM), F32)
        for hh in range(FOX_HEADS):
            sl = slice(hh * HEAD_DIM, (hh + 1) * HEAD_DIM)
            d = jnp.sum(do_ref[:, sl].astype(F32) * o_ref[:, sl], axis=-1, keepdims=True)
            acc = jnp.where(lane == hh, d, acc)
        d_ref[...] = acc

    blk = pl.BlockSpec((tr, FOX_W), lambda i: (i, 0))
    return pl.pallas_call(
        body, name="fox_delta", grid=(T // tr,), in_specs=[blk, blk], out_specs=pl.BlockSpec((tr, HEAD_DIM), lambda i: (i, 0)),
        out_shape=jax.ShapeDtypeStruct((T, HEAD_DIM), F32), compiler_params=_params(("parallel",)),
    )(dmix, out_f32)


def _fox_bwd(qkv, k_t, cq_row, ck_rep, delta_row, lse, dmix):
    T = qkv.shape[0]
    tq = _tile(T, 1024)
    nq = T // tq
    steps = nq * (nq + 1) // 2
    HQ, HK, HV = C_FQ // HEAD_DIM, C_FK // HEAD_DIM, C_FV // HEAD_DIM

    def body(tab, q_ref, k_ref, kt_ref, v_ref, cq_ref, ck_ref, delta_ref, lse_ref, do_ref,
             dq_ref, dk_ref, dv_ref, dck_ref, dcq_ref, dk_sc, dv_sc, dc_sc, dqt_sc):
        qi, kj = tab[0, pl.program_id(1)], tab[1, pl.program_id(1)]

        @pl.when(qi == kj)
        def _():
            dk_sc[...] = jnp.zeros_like(dk_sc)
            dv_sc[...] = jnp.zeros_like(dv_sc)
            dc_sc[...] = jnp.zeros_like(dc_sc)

        def step(on_diagonal):
            q, k, v, do = q_ref[...], k_ref[...], v_ref[...], do_ref[...]
            p = jnp.exp(_fox_scores_t(k, q, cq_ref[...], ck_ref[...], on_diagonal) - lse_ref[...])
            dp = _dot(v, do, "nt")
            ds = p * (dp - delta_ref[...])
            dsb = ds.astype(BF16)
            dv_sc[...] += _dot(p.astype(BF16), do, "nn")
            dk_sc[...] += _dot(dsb, q, "nn")
            dc_sc[...] += jnp.sum(ds, axis=1, keepdims=True)
            dq_part = _dot(kt_ref[...], dsb, "nn") * SCALE
            dcq_part = jnp.sum(ds, axis=0, keepdims=True)

            @pl.when(kj == 0)
            def _():
                dqt_sc[qi] = dq_part
                dcq_ref[qi] = dcq_part

            @pl.when(kj > 0)
            def _():
                dqt_sc[qi] += dq_part
                dcq_ref[qi] += dcq_part

            if on_diagonal:
                dq_ref[...] = dqt_sc[qi].T

        @pl.when(qi > kj)
        def _():
            step(False)

        @pl.when(qi == kj)
        def _():
            step(True)

        @pl.when(qi == nq - 1)
        def _():
            dk_ref[...] = dk_sc[...] * SCALE
            dv_ref[...] = dv_sc[...]
            dck_ref[...] = -dc_sc[...]

    def rows(base):
        return pl.BlockSpec((tq, HEAD_DIM), lambda h, s, tab: (tab[0, s], base + h))

    def cols(base):
        return pl.BlockSpec((tq, HEAD_DIM), lambda h, s, tab: (tab[1, s], base + h))

    qrow = pl.BlockSpec((None, 1, tq), lambda h, s, tab: (h, 0, tab[0, s]))
    sds = jax.ShapeDtypeStruct((T, FOX_W), F32)
    return pl.pallas_call(
        body, name="fox_bwd",
        grid_spec=pltpu.PrefetchScalarGridSpec(
            num_scalar_prefetch=1, grid=(FOX_HEADS, steps),
            in_specs=[rows(HQ), cols(HK), pl.BlockSpec((HEAD_DIM, tq), lambda h, s, tab: (h, tab[1, s])), cols(HV), qrow,
                      pl.BlockSpec((None, tq, HEAD_DIM), lambda h, s, tab: (h, tab[1, s], 0)), qrow, qrow, rows(0)],
            out_specs=[cols(0), cols(0), cols(0), pl.BlockSpec((None, tq, 1), lambda h, s, tab: (h, tab[1, s], 0)),
                       pl.BlockSpec((None, nq, 1, tq), lambda h, s, tab: (h, 0, 0, 0))],
            scratch_shapes=[pltpu.VMEM((tq, HEAD_DIM), F32), pltpu.VMEM((tq, HEAD_DIM), F32), pltpu.VMEM((tq, 1), F32),
                            pltpu.VMEM((nq, HEAD_DIM, tq), F32)]),
        out_shape=[sds, sds, sds, jax.ShapeDtypeStruct((FOX_HEADS, T, 1), F32), jax.ShapeDtypeStruct((FOX_HEADS, nq, 1, tq), F32)],
        compiler_params=_params(("parallel", "arbitrary")),
    )(_triangle(nq, True), qkv, qkv, k_t, qkv, cq_row, ck_rep, delta_row, lse, dmix)


GW = SWA_GROUP * HEAD_DIM
GR = SWA_GROUP * WINDOW


def _swa_scores(q_ref, kp_ref, kc_ref, slope_ref, n):
    q = q_ref[...]
    qs = jnp.concatenate([q[:, t * HEAD_DIM:(t + 1) * HEAD_DIM] for t in range(SWA_GROUP)], axis=0)
    kb = jnp.concatenate([kp_ref[...], kc_ref[...]], axis=0)
    r = lax.broadcasted_iota(jnp.int32, (GR, 2 * WINDOW), 0) & (WINDOW - 1)
    jj = lax.broadcasted_iota(jnp.int32, (GR, 2 * WINDOW), 1)
    dist = WINDOW + r - jj
    valid = (dist >= 0) & (dist < WINDOW) & ((n > 0) | (jj >= WINDOW))
    s = _dot(qs, kb, "nt") * SCALE - slope_ref[...] * dist.astype(F32)
    return qs, kb, jnp.where(valid, s, NEG_INF), valid


def _swa_specs():
    HQ, HK, HV = C_SQ // GW, C_SK // HEAD_DIM, C_SV // HEAD_DIM
    q_spec = pl.BlockSpec((WINDOW, GW), lambda g, n: (n, HQ + g))

    def prev(base):
        return pl.BlockSpec((WINDOW, HEAD_DIM), lambda g, n: (jnp.maximum(n - 1, 0), base + g))

    def cur(base):
        return pl.BlockSpec((WINDOW, HEAD_DIM), lambda g, n: (n, base + g))

    col = pl.BlockSpec((None, GR, 1), lambda g, n: (g, 0, 0))
    return q_spec, prev(HK), cur(HK), prev(HV), cur(HV), col


def _swa_fwd(qkv, slopes, sinks):
    T = qkv.shape[0]
    nb = T // WINDOW
    assert C_SQ % GW == 0

    def body(q_ref, kp_ref, kc_ref, vp_ref, vc_ref, slope_ref, sink_ref, o_ref, lse_ref):
        n = pl.program_id(1)
        _, _, s, _ = _swa_scores(q_ref, kp_ref, kc_ref, slope_ref, n)
        m = jnp.maximum(jnp.max(s, axis=-1, keepdims=True), sink_ref[...])
        p = jnp.exp(s - m)
        l = jnp.sum(p, axis=-1, keepdims=True) + jnp.exp(sink_ref[...] - m)
        vb = jnp.concatenate([vp_ref[...], vc_ref[...]], axis=0)
        o = _dot(p.astype(BF16), vb, "nn") / l
        for t in range(SWA_GROUP):
            o_ref[:, t * HEAD_DIM:(t + 1) * HEAD_DIM] = o[t * WINDOW:(t + 1) * WINDOW, :].astype(BF16)
        lse_ref[...] = m + jnp.log(l)

    q_spec, kp, kc, vp, vc, col = _swa_specs()
    return pl.pallas_call(
        body, name="swa_fwd", grid=(SWA_KV_HEADS, nb), in_specs=[q_spec, kp, kc, vp, vc, col, col],
        out_specs=[pl.BlockSpec((WINDOW, GW), lambda g, n: (n, g)), pl.BlockSpec((None, None, GR, 1), lambda g, n: (g, n, 0, 0))],
        out_shape=[jax.ShapeDtypeStruct((T, SWA_HEADS * HEAD_DIM), BF16), jax.ShapeDtypeStruct((SWA_KV_HEADS, nb, GR, 1), F32)],
        compiler_params=_params(("parallel", "arbitrary")),
    )(qkv, qkv, qkv, qkv, qkv, slopes, sinks)


def _swa_bwd(qkv, slopes, sinks, out, lse, dmix):
    T = qkv.shape[0]
    nb = T // WINDOW
    DO = FOX_W // GW
    assert FOX_W % GW == 0

    def body(q_ref, kp_ref, kc_ref, vp_ref, vc_ref, slope_ref, sink_ref, o_ref, lse_ref, do_ref,
             dq_ref, dk_ref, dv_ref, dsink_ref, sink_sc):
        n = pl.program_id(1)

        @pl.when(n == 0)
        def _():
            dk_ref[...] = jnp.zeros_like(dk_ref)
            dv_ref[...] = jnp.zeros_like(dv_ref)
            sink_sc[...] = jnp.zeros_like(sink_sc)

        qs, kb, s, valid = _swa_scores(q_ref, kp_ref, kc_ref, slope_ref, n)
        lse = lse_ref[...]
        p = jnp.where(valid, jnp.exp(s - lse), 0.0)
        vb = jnp.concatenate([vp_ref[...], vc_ref[...]], axis=0)
        do = jnp.concatenate([do_ref[:, t * HEAD_DIM:(t + 1) * HEAD_DIM] for t in range(SWA_GROUP)], axis=0)
        oo = jnp.concatenate([o_ref[:, t * HEAD_DIM:(t + 1) * HEAD_DIM] for t in range(SWA_GROUP)], axis=0)
        dp = _dot(do, vb, "nt")
        delta = jnp.sum(do.astype(F32) * oo.astype(F32), axis=-1, keepdims=True)
        ds = p * (dp - delta)
        dsb = ds.astype(BF16)
        dq = _dot(dsb, kb, "nn") * SCALE
        for t in range(SWA_GROUP):
            dq_ref[:, t * HEAD_DIM:(t + 1) * HEAD_DIM] = dq[t * WINDOW:(t + 1) * WINDOW, :]
        dkb = _dot(dsb, qs, "tn") * SCALE
        dvb = _dot(p.astype(BF16), do, "tn")
        r_prev = pl.ds(pl.multiple_of(jnp.maximum(n - 1, 0) * WINDOW, WINDOW), WINDOW)
        r_cur = pl.ds(pl.multiple_of(n * WINDOW, WINDOW), WINDOW)
        dk_ref[r_prev, :] += dkb[:WINDOW, :]
        dk_ref[r_cur, :] += dkb[WINDOW:, :]
        dv_ref[r_prev, :] += dvb[:WINDOW, :]
        dv_ref[r_cur, :] += dvb[WINDOW:, :]
        sink_sc[...] -= jnp.exp(sink_ref[...] - lse) * delta

        @pl.when(n == nb - 1)
        def _():
            tot = [jnp.zeros((1, 128), F32) + jnp.sum(sink_sc[t * WINDOW:(t + 1) * WINDOW, :]) for t in range(SWA_GROUP)]
            dsink_ref[...] = jnp.concatenate(tot + [jnp.zeros((8 - SWA_GROUP, 128), F32)], axis=0)

    q_spec, kp, kc, vp, vc, col = _swa_specs()
    kv_acc = pl.BlockSpec((T, HEAD_DIM), lambda g, n: (0, g))
    return pl.pallas_call(
        body, name="swa_bwd", grid=(SWA_KV_HEADS, nb),
        in_specs=[q_spec, kp, kc, vp, vc, col, col, pl.BlockSpec((WINDOW, GW), lambda g, n: (n, g)),
                  pl.BlockSpec((None, None, GR, 1), lambda g, n: (g, n, 0, 0)), pl.BlockSpec((WINDOW, GW), lambda g, n: (n, DO + g))],
        out_specs=[pl.BlockSpec((WINDOW, GW), lambda g, n: (n, g)), kv_acc, kv_acc, pl.BlockSpec((None, 8, 128), lambda g, n: (g, 0, 0))],
        out_shape=[jax.ShapeDtypeStruct((T, SWA_HEADS * HEAD_DIM), F32), jax.ShapeDtypeStruct((T, SWA_KV_HEADS * HEAD_DIM), F32),
                   jax.ShapeDtypeStruct((T, SWA_KV_HEADS * HEAD_DIM), F32), jax.ShapeDtypeStruct((SWA_KV_HEADS, 8, 128), F32)],
        scratch_shapes=[pltpu.VMEM((GR, 1), F32)],
        compiler_params=_params(("parallel", "arbitrary")),
    )(qkv, qkv, qkv, qkv, qkv, slopes, sinks, out, lse, dmix)


def _mem_fwd(qkv, mk, mv):
    T, ML = qkv.shape[0], mk.shape[0]
    tq = _tile(T, 1024)
    HQ = C_MQ // HEAD_DIM

    def body(q_ref, k_ref, v_ref, o_ref, lse_ref):
        s = _dot(q_ref[...], k_ref[...], "nt") * SCALE
        m = jnp.max(s, axis=-1, keepdims=True)
        p = jnp.exp(s - m)
        l = jnp.sum(p, axis=-1, keepdims=True)
        o_ref[...] = (_dot(p.astype(BF16), v_ref[...], "nn") / l).astype(BF16)
        lse_ref[...] = m + jnp.log(l)

    kv = pl.BlockSpec((ML, HEAD_DIM), lambda h, i: (0, h))
    return pl.pallas_call(
        body, name="mem_fwd", grid=(MEM_HEADS, T // tq),
        in_specs=[pl.BlockSpec((tq, HEAD_DIM), lambda h, i: (i, HQ + h)), kv, kv],
        out_specs=[pl.BlockSpec((tq, HEAD_DIM), lambda h, i: (i, h)), pl.BlockSpec((None, tq, 1), lambda h, i: (h, i, 0))],
        out_shape=[jax.ShapeDtypeStruct((T, MEM_HEADS * HEAD_DIM), BF16), jax.ShapeDtypeStruct((MEM_HEADS, T, 1), F32)],
        compiler_params=_params(("parallel", "arbitrary")),
    )(qkv, mk, mv)


def _mem_bwd(qkv, mk, mv, out, lse, dmix):
    T, ML = qkv.shape[0], mk.shape[0]
    tq = _tile(T, 1024)
    HQ = C_MQ // HEAD_DIM
    DO = (FOX_W + SWA_HEADS * HEAD_DIM) // HEAD_DIM

    def body(q_ref, k_ref, v_ref, o_ref, lse_ref, do_ref, dq_ref, dk_ref, dv_ref):
        q, k, v, do = q_ref[...], k_ref[...], v_ref[...], do_ref[...]
        p = jnp.exp(_dot(q, k, "nt") * SCALE - lse_ref[...])
        dp = _dot(do, v, "nt")
        delta = jnp.sum(do.astype(F32) * o_ref[...].astype(F32), axis=-1, keepdims=True)
        dsb = (p * (dp - delta)).astype(BF16)
        dq_ref[...] = _dot(dsb, k, "nn") * SCALE
        dk_part = _dot(dsb, q, "tn") * SCALE
        dv_part = _dot(p.astype(BF16), do, "tn")

        @pl.when(pl.program_id(1) == 0)
        def _():
            dk_ref[...] = dk_part
            dv_ref[...] = dv_part

        @pl.when(pl.program_id(1) > 0)
        def _():
            dk_ref[...] += dk_part
            dv_ref[...] += dv_part

    kv = pl.BlockSpec((ML, HEAD_DIM), lambda h, i: (0, h))
    qb = pl.BlockSpec((tq, HEAD_DIM), lambda h, i: (i, h))
    return pl.pallas_call(
        body, name="mem_bwd", grid=(MEM_HEADS, T // tq),
        in_specs=[pl.BlockSpec((tq, HEAD_DIM), lambda h, i: (i, HQ + h)), kv, kv, qb,
                  pl.BlockSpec((None, tq, 1), lambda h, i: (h, i, 0)), pl.BlockSpec((tq, HEAD_DIM), lambda h, i: (i, DO + h))],
        out_specs=[qb, kv, kv],
        out_shape=[jax.ShapeDtypeStruct((T, MEM_HEADS * HEAD_DIM), F32), jax.ShapeDtypeStruct((ML, MEM_HEADS * HEAD_DIM), F32),
                   jax.ShapeDtypeStruct((ML, MEM_HEADS * HEAD_DIM), F32)],
        compiler_params=_params(("parallel", "arbitrary")),
    )(qkv, mk, mv, out, lse, dmix)


HBM = pl.BlockSpec(memory_space=pltpu.HBM)


def _place():
    x, y, c = lax.axis_index("x"), lax.axis_index("y"), lax.axis_index("c")
    chips = [(1 - x, y), (x, 1 - y), (1 - x, 1 - y)]
    return x, y, c, chips


def _remote(src, dst, send_sem, recv_sem, device):
    return pltpu.make_async_remote_copy(src_ref=src, dst_ref=dst, send_sem=send_sem, recv_sem=recv_sem,
                                        device_id=device, device_id_type=MESH)


def _place_ids():
    x, y, c = lax.axis_index("x"), lax.axis_index("y"), lax.axis_index("c")
    order = [2 * x + y, 2 * (1 - x) + y, 2 * x + (1 - y), 2 * (1 - x) + (1 - y)]
    return jnp.stack([2 * x + y, c] + order).astype(jnp.int32)


def _cast_place(name, w, ids, *after):
    R, C = w.shape
    tr = _tile(R, 256, 16)

    def body(ids_ref, w_ref, *rest):
        rest[-1][...] = w_ref[...].astype(BF16)

    return pl.pallas_call(
        body, name=name,
        grid_spec=pltpu.PrefetchScalarGridSpec(
            num_scalar_prefetch=1, grid=(R // tr,),
            in_specs=[pl.BlockSpec((tr, C), lambda i, ids: (i, 0))] + [pl.BlockSpec(memory_space=pl.ANY)] * len(after),
            out_specs=pl.BlockSpec((None, tr, C), lambda i, ids: (ids[0], i, 0))),
        out_shape=jax.ShapeDtypeStruct((N_CHIPS, R, C), BF16), compiler_params=_params(("parallel",)),
    )(ids, w, *after)


SEM = pl.BlockSpec(memory_space=pltpu.SEMAPHORE)
EFFECT = pltpu.SideEffectType.DATAFLOW_SIDE_EFFECTING


def _hbm(a):
    return pltpu.with_memory_space_constraint(a, pltpu.HBM)


def _gather_start(name, placed, after):
    n = len(placed)

    ns = 3 * n

    def body(*refs):
        send, recv = refs[n + 1:n + 1 + ns], refs[n + 1 + ns:n + 1 + 2 * ns]
        buf = refs[n + 1 + 2 * ns:2 * n + 1 + 2 * ns]
        token = refs[2 * n + 1 + 2 * ns]
        x, y, c, chips = _place()
        me = 2 * x + y
        for a in range(n):
            half = buf[a].shape[1] // 2
            mine = buf[a].at[me, pl.ds(c * half, half)]
            for j, (cx, cy) in enumerate(chips):
                _remote(mine, mine, send[3 * a + j], recv[3 * a + j], (cx, cy, c)).start()
        token[...] = jnp.zeros_like(token)

    res = pl.pallas_call(
        body, name=name, in_specs=[HBM] * n + [pl.BlockSpec(memory_space=pl.ANY)],
        out_specs=[SEM] * (2 * ns) + [HBM] * n + [pl.BlockSpec(memory_space=pltpu.VMEM)],
        out_shape=[pltpu.SemaphoreType.DMA(())] * (2 * ns)
        + [pltpu.HBM(s.shape, s.dtype) for s in placed] + [jax.ShapeDtypeStruct((8, 128), F32)],
        input_output_aliases={a: 2 * ns + a for a in range(n)},
        compiler_params=pltpu.CompilerParams(has_side_effects=EFFECT),
    )(*[_hbm(s) for s in placed], after)
    return list(res[:ns]), list(res[ns:2 * ns]), list(res[2 * ns:2 * ns + n]), res[2 * ns + n]


def _gather_wait(name, send, recv, bufs, after):
    n = len(bufs)

    ns = 3 * n

    def body(*refs):
        buf = refs[:n]
        send_ref, recv_ref = refs[n:n + ns], refs[n + ns:n + 2 * ns]
        x, y, c, chips = _place()
        ids = [2 * cx + cy for cx, cy in chips]
        for a in range(n):
            half = buf[a].shape[1] // 2
            for j in range(3):
                landed = buf[a].at[ids[j], pl.ds(c * half, half)]
                cp = _remote(landed, landed, send_ref[3 * a + j], recv_ref[3 * a + j], (x, y, c))
                cp.wait_send()
                cp.wait_recv()

    res = pl.pallas_call(
        body, name=name, in_specs=[HBM] * n + [SEM] * (2 * ns) + [pl.BlockSpec(memory_space=pl.ANY)], out_specs=[HBM] * n,
        out_shape=[pltpu.HBM(s.shape, s.dtype) for s in bufs], input_output_aliases={a: a for a in range(n)},
        compiler_params=pltpu.CompilerParams(has_side_effects=EFFECT),
    )(*bufs, *send, *recv, after)
    return list(res)


def _gather_forward(name, bufs):
    n = len(bufs)

    def body(*refs):
        buf = refs[n:2 * n]
        send, recv = refs[2 * n:]
        x, y, c, chips = _place()
        ids = [2 * cx + cy for cx, cy in chips]
        copies = []
        for a in range(n):
            half = buf[a].shape[1] // 2
            for j in range(3):
                landed = buf[a].at[ids[j], pl.ds(c * half, half)]
                cp = _remote(landed, landed, send.at[a, j], recv.at[a, j], (x, y, 1 - c))
                cp.start()
                copies.append(cp)
        for a in range(n):
            half = buf[a].shape[1] // 2
            for j in range(3):
                landed = buf[a].at[ids[j], pl.ds((1 - c) * half, half)]
                _remote(landed, landed, send.at[a, j], recv.at[a, j], (x, y, c)).wait_recv()
        for cp in copies:
            cp.wait_send()

    return pl.pallas_call(
        body, name=name, in_specs=[HBM] * n, out_specs=[HBM] * n,
        out_shape=[jax.ShapeDtypeStruct(s.shape, s.dtype) for s in bufs], input_output_aliases={a: a for a in range(n)},
        scratch_shapes=[pltpu.SemaphoreType.DMA((n, 3)), pltpu.SemaphoreType.DMA((n, 3))],
    )(*bufs)


def _pair_start(name, grads):
    n = len(grads)
    ns = N_CHIPS * n

    def body(*refs):
        send, recv = refs[2 * n:2 * n + ns], refs[2 * n + ns:2 * n + 2 * ns]
        src = refs[2 * n + 2 * ns:3 * n + 2 * ns]
        land = refs[3 * n + 2 * ns:4 * n + 2 * ns]
        token = refs[4 * n + 2 * ns]
        x, y, c, chips = _place()
        order = [2 * x + y] + [2 * cx + cy for cx, cy in chips]
        for a in range(n):
            half = src[a].shape[1] // 2
            for j in range(N_CHIPS):
                _remote(src[a].at[order[j], pl.ds((1 - c) * half, half)], land[a].at[j],
                        send[N_CHIPS * a + j], recv[N_CHIPS * a + j], (x, y, 1 - c)).start()
        token[...] = jnp.zeros_like(token)

    lands = [jax.ShapeDtypeStruct((N_CHIPS, g.shape[1] // 2, g.shape[2]), g.dtype) for g in grads]
    res = pl.pallas_call(
        body, name=name, in_specs=[HBM] * (2 * n),
        out_specs=[SEM] * (2 * ns) + [HBM] * (2 * n) + [pl.BlockSpec(memory_space=pltpu.VMEM)],
        out_shape=[pltpu.SemaphoreType.DMA(())] * (2 * ns) + [pltpu.HBM(g.shape, g.dtype) for g in grads]
        + [pltpu.HBM(l.shape, l.dtype) for l in lands] + [jax.ShapeDtypeStruct((8, 128), F32)],
        input_output_aliases={a: 2 * ns + a for a in range(2 * n)},
        compiler_params=pltpu.CompilerParams(has_side_effects=EFFECT),
    )(*[_hbm(g) for g in grads], *[_hbm(lax.empty(l.shape, l.dtype)) for l in lands])
    return list(res[:ns]), list(res[ns:2 * ns]), list(res[2 * ns:2 * ns + n]), list(res[2 * ns + n:2 * ns + 2 * n]), res[2 * ns + 2 * n]


def _pair_wait(name, send, recv, grads, lands, after):
    n = len(grads)
    ns = N_CHIPS * n

    def body(*refs):
        src, land = refs[:n], refs[n:2 * n]
        send_ref, recv_ref = refs[2 * n:2 * n + ns], refs[2 * n + ns:2 * n + 2 * ns]
        x, y, c, _ = _place()
        for a in range(n):
            for j in range(N_CHIPS):
                cp = _remote(land[a].at[j], land[a].at[j], send_ref[N_CHIPS * a + j], recv_ref[N_CHIPS * a + j], (x, y, c))
                cp.wait_send()
                cp.wait_recv()

    res = pl.pallas_call(
        body, name=name, in_specs=[HBM] * (2 * n) + [SEM] * (2 * ns) + [pl.BlockSpec(memory_space=pl.ANY)],
        out_specs=[HBM] * (2 * n), out_shape=[pltpu.HBM(g.shape, g.dtype) for g in grads] + [pltpu.HBM(l.shape, l.dtype) for l in lands],
        input_output_aliases={a: a for a in range(2 * n)},
        compiler_params=pltpu.CompilerParams(has_side_effects=EFFECT),
    )(*grads, *lands, *send, *recv, after)
    return list(res[:n]), list(res[n:])


def _chip_start(name, parts):
    n = len(parts)
    ns = 3 * n

    def body(*refs):
        send, recv = refs[2 * n:2 * n + ns], refs[2 * n + ns:2 * n + 2 * ns]
        src = refs[2 * n + 2 * ns:3 * n + 2 * ns]
        land = refs[3 * n + 2 * ns:4 * n + 2 * ns]
        token = refs[4 * n + 2 * ns]
        x, y, c, chips = _place()
        for a in range(n):
            for j, (cx, cy) in enumerate(chips):
                _remote(src[a].at[j], land[a].at[j], send[3 * a + j], recv[3 * a + j], (cx, cy, c)).start()
        token[...] = jnp.zeros_like(token)

    res = pl.pallas_call(
        body, name=name, in_specs=[HBM] * (2 * n),
        out_specs=[SEM] * (2 * ns) + [HBM] * (2 * n) + [pl.BlockSpec(memory_space=pltpu.VMEM)],
        out_shape=[pltpu.SemaphoreType.DMA(())] * (2 * ns) + [pltpu.HBM(p.shape, p.dtype) for p in parts] * 2
        + [jax.ShapeDtypeStruct((8, 128), F32)],
        input_output_aliases={a: 2 * ns + a for a in range(2 * n)},
        compiler_params=pltpu.CompilerParams(has_side_effects=EFFECT),
    )(*[_hbm(p) for p in parts], *[_hbm(lax.empty(p.shape, p.dtype)) for p in parts])
    return list(res[:ns]), list(res[ns:2 * ns]), list(res[2 * ns:2 * ns + n]), list(res[2 * ns + n:2 * ns + 2 * n]), res[2 * ns + 2 * n]


def _chip_wait(name, send, recv, parts, lands, after):
    n = len(parts)
    ns = 3 * n

    def body(*refs):
        src, land = refs[:n], refs[n:2 * n]
        send_ref, recv_ref = refs[2 * n:2 * n + ns], refs[2 * n + ns:2 * n + 2 * ns]
        x, y, c, _ = _place()
        for a in range(n):
            for j in range(3):
                cp = _remote(src[a].at[j], land[a].at[j], send_ref[3 * a + j], recv_ref[3 * a + j], (x, y, c))
                cp.wait_send()
                cp.wait_recv()

    res = pl.pallas_call(
        body, name=name, in_specs=[HBM] * (2 * n) + [SEM] * (2 * ns) + [pl.BlockSpec(memory_space=pl.ANY)],
        out_specs=[HBM] * (2 * n), out_shape=[pltpu.HBM(p.shape, p.dtype) for p in parts] * 2,
        input_output_aliases={a: a for a in range(2 * n)},
        compiler_params=pltpu.CompilerParams(has_side_effects=EFFECT),
    )(*parts, *lands, *send, *recv, after)
    return list(res[n:])


def _pair_share(name, shards):
    n = len(shards)

    def body(*refs):
        buf = refs[n:2 * n]
        send, recv = refs[2 * n:]
        x, y, c, _ = _place()
        copies = []
        for a in range(n):
            half = buf[a].shape[0] // 2
            mine = buf[a].at[pl.ds(c * half, half)]
            cp = _remote(mine, mine, send.at[a], recv.at[a], (x, y, 1 - c))
            cp.start()
            copies.append(cp)
        for a, cp in enumerate(copies):
            half = buf[a].shape[0] // 2
            cp.wait_send()
            theirs = buf[a].at[pl.ds((1 - c) * half, half)]
            _remote(theirs, theirs, send.at[a], recv.at[a], (x, y, c)).wait_recv()

    return pl.pallas_call(
        body, name=name, in_specs=[HBM] * n, out_specs=[HBM] * n,
        out_shape=[jax.ShapeDtypeStruct(s.shape, s.dtype) for s in shards], input_output_aliases={a: a for a in range(n)},
        scratch_shapes=[pltpu.SemaphoreType.DMA((n,)), pltpu.SemaphoreType.DMA((n,))],
    )(*shards)


def _small_start(buf):
    R, W = buf.shape
    ns = N_DEV - 1

    def body(*refs):
        send, recv = refs[2:2 + ns], refs[2 + ns:2 + 2 * ns]
        src, land, token = refs[2 + 2 * ns], refs[3 + 2 * ns], refs[4 + 2 * ns]
        x, y, c, _ = _place()
        me = 4 * x + 2 * y + c
        for k in range(1, N_DEV):
            peer = (x ^ (k >> 2), y ^ ((k >> 1) & 1), c ^ (k & 1))
            _remote(src, land.at[me], send[k - 1], recv[k - 1], peer).start()
        token[...] = jnp.zeros_like(token)

    res = pl.pallas_call(
        body, name="small_start", in_specs=[HBM, HBM],
        out_specs=[SEM] * (2 * ns) + [HBM, HBM, pl.BlockSpec(memory_space=pltpu.VMEM)],
        out_shape=[pltpu.SemaphoreType.DMA(())] * (2 * ns) + [pltpu.HBM((R, W), F32), pltpu.HBM((N_DEV, R, W), F32),
                                                                jax.ShapeDtypeStruct((8, 128), F32)],
        input_output_aliases={0: 2 * ns, 1: 2 * ns + 1},
        compiler_params=pltpu.CompilerParams(has_side_effects=EFFECT),
    )(_hbm(buf), _hbm(jnp.zeros((N_DEV, R, W), F32)))
    return list(res[:ns]), list(res[ns:2 * ns]), res[2 * ns], res[2 * ns + 1], res[2 * ns + 2]


def _small_wait(send, recv, buf, land, after):
    ns = N_DEV - 1

    def body(*refs):
        land_ref = refs[1]
        send_ref, recv_ref = refs[2:2 + ns], refs[2 + ns:2 + 2 * ns]
        x, y, c, _ = _place()
        me = 4 * x + 2 * y + c
        for k in range(1, N_DEV):
            landed = land_ref.at[me ^ k]
            cp = _remote(landed, landed, send_ref[k - 1], recv_ref[k - 1], (x, y, c))
            cp.wait_send()
            cp.wait_recv()

    return pl.pallas_call(
        body, name="small_wait", in_specs=[HBM, HBM] + [SEM] * (2 * ns) + [pl.BlockSpec(memory_space=pl.ANY)],
        out_specs=[HBM, HBM], out_shape=[pltpu.HBM(buf.shape, buf.dtype), pltpu.HBM(land.shape, land.dtype)],
        input_output_aliases={0: 0, 1: 1}, compiler_params=pltpu.CompilerParams(has_side_effects=EFFECT),
    )(buf, land, *send, *recv, after)


def _small_sum(buf, land):
    def body(buf_ref, land_ref, out_ref):
        x, y, c, _ = _place()
        me = 4 * x + 2 * y + c
        total = None
        for d in range(N_DEV):
            term = jnp.where(me == d, buf_ref[...], land_ref[d])
            total = term if total is None else total + term
        out_ref[...] = total

    return pl.pallas_call(body, name="small_sum", out_shape=jax.ShapeDtypeStruct(buf.shape, F32))(buf, land)


def _pair_sum_bf16(name, grad, theirs, ids):
    _, R2, C = theirs.shape
    tr = _tile(R2, 256, 16)
    nrb = R2 // tr

    def body(ids_ref, a_ref, b_ref, o_ref):
        o_ref[...] = (a_ref[...] + b_ref[...]).astype(BF16)

    return pl.pallas_call(
        body, name=name,
        grid_spec=pltpu.PrefetchScalarGridSpec(
            num_scalar_prefetch=1, grid=(3, nrb),
            in_specs=[pl.BlockSpec((None, tr, C), lambda j, i, ids: (ids[3 + j], ids[1] * nrb + i, 0)),
                      pl.BlockSpec((None, tr, C), lambda j, i, ids: (j + 1, i, 0))],
            out_specs=pl.BlockSpec((None, tr, C), lambda j, i, ids: (j, i, 0))),
        out_shape=jax.ShapeDtypeStruct((3, R2, C), BF16), compiler_params=_params(("parallel", "parallel")),
    )(ids, grad, theirs)


def _chip_sum(name, grad, theirs, arrived, ids):
    _, R2, C = theirs.shape
    tr = _tile(R2, 256, 16)
    nrb = R2 // tr

    def body(ids_ref, a_ref, b_ref, r_ref, o_ref):
        tot = a_ref[...] + b_ref[...]
        for j in range(3):
            tot = tot + r_ref[j].astype(F32)
        o_ref[...] = tot

    return pl.pallas_call(
        body, name=name,
        grid_spec=pltpu.PrefetchScalarGridSpec(
            num_scalar_prefetch=1, grid=(nrb,),
            in_specs=[pl.BlockSpec((None, tr, C), lambda i, ids: (ids[0], ids[1] * nrb + i, 0)),
                      pl.BlockSpec((None, tr, C), lambda i, ids: (0, i, 0)),
                      pl.BlockSpec((3, tr, C), lambda i, ids: (0, i, 0))],
            out_specs=pl.BlockSpec((tr, C), lambda i, ids: (ids[1] * nrb + i, 0))),
        out_shape=jax.ShapeDtypeStruct((2 * R2, C), F32), compiler_params=_params(("parallel",)),
    )(ids, grad, theirs, arrived)


def _adamw(name, w, g, m, v, emit_grad=False):
    R, C = w.shape
    tr = _tile(R, 128, 8)
    c1 = 1.0 / (1.0 - ADAM_B1 ** ADAM_STEP)
    c2 = 1.0 / (1.0 - ADAM_B2 ** ADAM_STEP)
    n_out = 4 if emit_grad else 3

    def body(w_ref, g_ref, m_ref, v_ref, d_ref, mo_ref, vo_ref, *rest):
        gv = g_ref[...]
        mn = ADAM_B1 * m_ref[...] + (1.0 - ADAM_B1) * gv
        vn = ADAM_B2 * v_ref[...] + (1.0 - ADAM_B2) * (gv * gv)
        d_ref[...] = -ADAM_LR * ((mn * c1) / (jnp.sqrt(vn * c2) + ADAM_EPS) + ADAM_WD * w_ref[...])
        mo_ref[...] = mn
        vo_ref[...] = vn
        if emit_grad:
            rest[0][...] = gv

    spec = pl.BlockSpec((tr, C), lambda i: (i, 0))
    sds = jax.ShapeDtypeStruct((R, C), F32)
    return pl.pallas_call(body, name=name, grid=(R // tr,), in_specs=[spec] * 4, out_specs=[spec] * n_out, out_shape=[sds] * n_out,
                          compiler_params=_params(("parallel",)))(w, g, m, v)


SMALL = ["ffn1_norm", "mix_norm", "mem_norm", "forget_bias", "fox_q_gain", "fox_k_gain", "swa_q_gain", "swa_k_gain", "swa_sinks",
         "mem_q_gain", "mem_k_gain", "ffn2_norm"]
LARGE = ["ffn1_gate", "ffn1_up", "ffn1_down", "w_in", "w_mem_k", "w_mem_v", "w_out", "ffn2_gate", "ffn2_up", "ffn2_down"]
GATHER_GROUPS = [["ffn1_gate", "ffn1_up"], ["ffn1_down", "w_in", "w_mem_k", "w_mem_v"], ["w_out", "ffn2_gate", "ffn2_up", "ffn2_down"]]
WEIGHTS = ["ffn1_norm", "ffn1_gate", "ffn1_up", "ffn1_down", "mix_norm", "mem_norm", "w_in", "forget_bias", "w_mem_k", "w_mem_v",
           "fox_q_gain", "fox_k_gain", "swa_q_gain", "swa_k_gain", "swa_sinks", "mem_q_gain", "mem_k_gain", "w_out", "ffn2_norm",
           "ffn2_gate", "ffn2_up", "ffn2_down"]


def _pad_proj_cols(w):
    out = jnp.zeros((w.shape[0], PROJ_W), w.dtype)
    for start, width, pstart in REF_GROUPS:
        out = lax.dynamic_update_slice(out, w[:, start:start + width], (0, pstart))
    return out


def _unpad_proj_cols(w):
    return jnp.concatenate([w[:, pstart:pstart + width] for _, width, pstart in REF_GROUPS], axis=1)


def _pack_small(vals):
    flat = jnp.concatenate([vals[k].reshape(-1).astype(F32) for k in SMALL + ["loss"]])
    n = flat.shape[0]
    total = -(-n // 1024) * 1024
    return jnp.pad(flat, (0, total - n)).reshape(total // 128, 128)


def _unpack_small(buf, shapes):
    flat = buf.reshape(-1)
    out, off = {}, 0
    for k in SMALL + ["loss"]:
        size = int(np.prod(shapes[k]))
        out[k] = flat[off:off + size].reshape(shapes[k])
        off += size
    return out


def kernel(x, mem, ffn1_norm, ffn1_gate, ffn1_up, ffn1_down, mix_norm, mem_norm, w_in, forget_bias, w_mem_k, w_mem_v, fox_q_gain, fox_k_gain, swa_q_gain, swa_k_gain, swa_sinks, mem_q_gain, mem_k_gain, w_out, ffn2_norm, ffn2_gate, ffn2_up, ffn2_down, loss_target, m_ffn1_norm, m_ffn1_gate, m_ffn1_up, m_ffn1_down, m_mix_norm, m_mem_norm, m_w_in, m_forget_bias, m_w_mem_k, m_w_mem_v, m_fox_q_gain, m_fox_k_gain, m_swa_q_gain, m_swa_k_gain, m_swa_sinks, m_mem_q_gain, m_mem_k_gain, m_w_out, m_ffn2_norm, m_ffn2_gate, m_ffn2_up, m_ffn2_down, v_ffn1_norm, v_ffn1_gate, v_ffn1_up, v_ffn1_down, v_mix_norm, v_mem_norm, v_w_in, v_forget_bias, v_w_mem_k, v_w_mem_v, v_fox_q_gain, v_fox_k_gain, v_swa_q_gain, v_swa_k_gain, v_swa_sinks, v_mem_q_gain, v_mem_k_gain, v_w_out, v_ffn2_norm, v_ffn2_gate, v_ffn2_up, v_ffn2_down):
    given = dict(locals())
    T, D = x.shape[1], x.shape[2]
    ML = mem.shape[1]
    xin = x.reshape(T, D)
    target = loss_target.reshape(T, D)
    memin = mem.reshape(ML, D)

    ids = _place_ids()
    shard = {k: given[k][0] for k in LARGE}
    started, after = [], ids
    for gi, group in enumerate(GATHER_GROUPS):
        also = {}
        if "w_in" in group:
            tied = lax.optimization_barrier((given["w_in"], given["m_w_in"], given["v_w_in"], after))
            w_in_rows = tuple(t[0] for t in tied[:3])
            shard["w_in"] = _pad_proj_cols(w_in_rows[0])
            also["w_in"] = w_in_rows[1:]
        placed = [_cast_place("cast_" + k, shard[k], ids, after, *also.get(k, ())) for k in group]
        send, recv, bufs, after = _gather_start("gather_start_%d" % gi, placed, after)
        started.append((send, recv, bufs))

    def arrive(gi, done):
        send, recv, bufs = started[gi]
        bufs = _gather_wait("gather_wait_%d" % gi, send, recv, bufs, done)
        return dict(zip(GATHER_GROUPS[gi], _gather_forward("gather_forward_%d" % gi, bufs)))

    gains = jnp.concatenate([fox_q_gain, fox_k_gain, swa_q_gain, swa_k_gain, mem_q_gain,
                             jnp.pad(forget_bias, ((0, 0), (0, HEAD_DIM - FOX_HEADS))), jnp.zeros((2, HEAD_DIM), F32)], axis=0)
    slopes_np = 2.0 ** (-8.0 * np.arange(1, SWA_HEADS + 1) / SWA_HEADS)
    slopes = jnp.asarray(np.repeat(slopes_np, WINDOW).reshape(SWA_KV_HEADS, GR, 1), F32)
    sinks = jnp.repeat(swa_sinks.reshape(SWA_HEADS), WINDOW).reshape(SWA_KV_HEADS, GR, 1)

    h1 = _rms_fwd("ffn1_norm_fwd", xin, ffn1_norm + after[0, 0])
    full = arrive(0, h1)
    wg1, wu1 = full["ffn1_gate"], full["ffn1_up"]
    fg1, fu1, a1 = _ffn_gu("ffn1_gate_up", h1, wg1, wu1)
    full = arrive(1, a1)
    wd1 = full["ffn1_down"].reshape(-1, D)
    win = full["w_in"].reshape(D, PROJ_W)
    wmk = full["w_mem_k"].reshape(D, MEM_HEADS * HEAD_DIM)
    wmv = full["w_mem_v"].reshape(D, MEM_HEADS * HEAD_DIM)
    x1, h2 = _residual_norm("ffn1_down", a1, wd1, xin, 0.5, mix_norm, 256)
    proj = _mm2d("proj_in", h2, win, "nn", F32, tn=1408, tk=2048, n_outer=True)
    qkv, logf, k_t, v_t = _prep_fwd(proj, gains)
    cum = _cumsum_rows("forget_cumsum", [logf], False)
    cum_h = cum[:, :FOX_HEADS].T
    cq_row = cum_h.reshape(FOX_HEADS, 1, T)
    ck_rep = jnp.broadcast_to(cum_h[:, :, None], (FOX_HEADS, T, HEAD_DIM))
    mn = _rms_fwd("mem_norm_fwd", memin, mem_norm)
    mk_raw = _mm2d("mem_k_proj", mn, wmk, "nn", F32)
    mv = _mm2d("mem_v_proj", mn, wmv, "nn", BF16)
    mk = _head_norm_rows(mk_raw, mem_k_gain)
    out_a, out_a_f32, lse_a = _fox_fwd(qkv, v_t, cq_row, ck_rep)
    out_b, lse_b = _swa_fwd(qkv, slopes, sinks)
    out_c, lse_c = _mem_fwd(qkv, mk, mv)
    mixed = jnp.concatenate([out_a, out_b, out_c], axis=1)
    full = arrive(2, mixed)
    wo = full["w_out"].reshape(-1, D)
    wg2, wu2, wd2 = full["ffn2_gate"], full["ffn2_up"], full["ffn2_down"].reshape(-1, D)
    x2, h3 = _residual_norm("mix_out", mixed, wo, x1, 1.0, ffn2_norm, 512)
    fg2, fu2, a2 = _ffn_gu("ffn2_gate_up", h3, wg2, wu2)
    dx3, dyb3, loss_blocks = _ffn_down_loss("ffn2_down", a2, wd2, x2, target)

    grads, small, res = {}, {"loss": jnp.sum(loss_blocks[::8, 0])}, {}

    def pair_off(tag, group):
        send, recv, own, lands, token = _pair_start("grad_pair_start_" + tag, [grads[k] for k in group])
        return (group, send, recv, own, lands), token

    def chip_off(tag, started, done):
        group, send, recv, own, lands = started
        own, theirs = _pair_wait("grad_pair_wait_" + tag, send, recv, own, lands, done)
        grads.update(zip(group, own))
        to_chips = [_pair_sum_bf16("pair_sum_" + k, grads[k], b, ids) for k, b in zip(group, theirs)]
        send, recv, parts, lands, token = _chip_start("grad_chip_start_" + tag, to_chips)
        return (group, theirs, send, recv, parts, lands), token

    def finish(tag, state, done):
        group, theirs, send, recv, parts, lands = state
        arrived = _chip_wait("grad_chip_wait_" + tag, send, recv, parts, lands, done)
        halves = [_chip_sum("chip_sum_" + k, grads[k], b, r, ids) for k, b, r in zip(group, theirs, arrived)]
        reduced = dict(zip(group, _pair_share("grad_pair_share_" + tag, halves)))
        last = None
        for k in group:
            if k == "w_in":
                gk = _unpad_proj_cols(reduced[k])
                d, mo, vo = _adamw("adamw_" + k, w_in_rows[0], gk, w_in_rows[1], w_in_rows[2])
            else:
                d, mo, vo, gk = _adamw("adamw_" + k, given[k][0], reduced[k], given["m_" + k][0], given["v_" + k][0], emit_grad=True)
            res[k] = tuple(t[None] for t in (gk, d, mo, vo))
            last = vo
        return last

    dg2, du2 = _ffn_bwd_act("ffn2", dyb3, wd2, fg2, fu2, N_CHIPS)
    grads["ffn2_down"] = _ffn_bwd_down("ffn2", a2, dyb3, N_CHIPS).reshape(N_CHIPS, -1, D)
    grads["ffn2_gate"], grads["ffn2_up"] = _ffn_bwd_gate_up("ffn2", h3, dg2, du2, N_CHIPS)
    started, token = pair_off("a", ["ffn2_gate", "ffn2_up", "ffn2_down"])
    dh3 = _ffn_bwd_x("ffn2", dg2, du2, wg2, wu2, token)
    state_a, token = chip_off("a", started, dh3)
    dx2, dx2b, small["ffn2_norm"] = _rms_bwd("ffn2_norm_bwd", dh3, x2, ffn2_norm + token[0, 0], dx3, 1.0)
    dmix = _mm2d("mix_out_dx", dx2b, wo, "nt", BF16, tk=2048, n_outer=True)
    grads["w_out"] = _mm2d("mix_out_dw", mixed, dx2b, "tn", F32, tk=T, n_outer=True, resident=True).reshape(N_CHIPS, -1, D)
    delta_row = _fox_delta(dmix, out_a_f32)[:, :FOX_HEADS].T.reshape(FOX_HEADS, 1, T)
    dfq, dfk, dfv, dck, dcq = _fox_bwd(qkv, k_t, cq_row, ck_rep, delta_row, lse_a, dmix)
    dsq, dsk, dsv, dsink = _swa_bwd(qkv, slopes, sinks, out_b, lse_b, dmix)
    dmq, dmk, dmv = _mem_bwd(qkv, mk, mv, out_c, lse_c, dmix)
    small["swa_sinks"] = dsink[:, :SWA_GROUP, 0].reshape(1, SWA_HEADS)
    dcum = jnp.pad(dcq.reshape(FOX_HEADS, T).T, ((0, 0), (0, HEAD_DIM - FOX_HEADS)))
    dlogf = _cumsum_rows("forget_cumsum_bwd", [dcum], True, columns=dck)
    dproj, dgains = _prep_bwd(proj, gains, dfq, dfk, dfv, dsq, dsk, dsv, dmq, dlogf)
    for row, k in enumerate(["fox_q_gain", "fox_k_gain", "swa_q_gain", "swa_k_gain", "mem_q_gain"]):
        small[k] = dgains[row:row + 1, :]
    small["forget_bias"] = dgains[5:6, :FOX_HEADS]
    grads["w_in"] = _mm2d("proj_in_dw", h2, dproj, "tn", F32, tn=1408, tk=T, n_outer=True, resident=True).reshape(N_CHIPS, -1, PROJ_W)
    dmk_raw, small["mem_k_gain"] = _head_norm_rows_bwd(mk_raw, mem_k_gain, dmk)
    dmvb = dmv.astype(BF16)
    grads["w_mem_k"] = _mm2d("mem_k_dw", mn, dmk_raw, "tn", F32).reshape(N_CHIPS, -1, MEM_HEADS * HEAD_DIM)
    grads["w_mem_v"] = _mm2d("mem_v_dw", mn, dmvb, "tn", F32).reshape(N_CHIPS, -1, MEM_HEADS * HEAD_DIM)
    dmn = _mm2d("mem_k_dx", dmk_raw, wmk, "nt", F32)
    dmn = _mm2d("mem_v_dx", dmvb, wmv, "nt", F32, extras=[dmn], epilogue=lambda accs, ex: [ex[0] + accs[0]])
    _, _, small["mem_norm"] = _rms_bwd("mem_norm_bwd", dmn, memin, mem_norm, jnp.zeros_like(memin), 1.0)
    started, token = pair_off("b", ["w_out", "w_in", "w_mem_k", "w_mem_v"])
    dh2 = _mm2d("proj_in_dx", dproj, win, "nt", F32, tm=512, tn=1024, tk=PROJ_W, n_outer=True, resident=True, after=token)
    state_b, token = chip_off("b", started, dh2)
    dx1, dyb1, small["mix_norm"] = _rms_bwd("mix_norm_bwd", dh2, x1, mix_norm + token[0, 0], dx2, 0.5)
    grads["ffn1_down"] = _ffn_bwd_down("ffn1", a1, dyb1, N_CHIPS).reshape(N_CHIPS, -1, D)
    started, token = pair_off("c", ["ffn1_down"])
    dg1, du1 = _ffn_bwd_act("ffn1", dyb1, wd1, fg1, fu1, N_CHIPS, after=token)
    state_c, token = chip_off("c", started, dg1)
    grads["ffn1_gate"], grads["ffn1_up"] = _ffn_bwd_gate_up("ffn1", h1, dg1, du1, N_CHIPS, after=token)
    started, token = pair_off("d", ["ffn1_gate", "ffn1_up"])
    dh1 = _ffn_bwd_x("ffn1", dg1, du1, wg1, wu1, token)
    state_d, token = chip_off("d", started, dh1)
    grad_x, _, small["ffn1_norm"] = _rms_bwd("ffn1_norm_bwd", dh1, xin, ffn1_norm + token[0, 0], dx1, 1.0)

    s_send, s_recv, s_buf, s_land, token = _small_start(_pack_small(small))

    done = finish("a", state_a, token)
    done = finish("b", state_b, done)
    done = finish("c", state_c, done)
    done = finish("d", state_d, done)

    shapes = {k: given[k].shape for k in SMALL}
    shapes["loss"] = ()
    s_buf, s_land = _small_wait(s_send, s_recv, s_buf, s_land, done)
    red_small = _unpack_small(_small_sum(s_buf, s_land), shapes)
    loss = red_small["loss"]
    zero = {"loss": jnp.zeros((), F32)}
    packed = [_pack_small({**zero, **{k: src[k] for k in SMALL}}) for src in (
        {k: given[k] for k in SMALL}, red_small, {k: given["m_" + k] for k in SMALL}, {k: given["v_" + k] for k in SMALL})]
    d_s, m_s, v_s = (_unpack_small(t, shapes) for t in _adamw("adamw_small", *packed))
    for k in SMALL:
        res[k] = (red_small[k], d_s[k], m_s[k], v_s[k])

    outs = [loss, grad_x.reshape(1, T, D)]
    for part in range(4):
        outs += [res[k][part] for k in WEIGHTS]
    return tuple(outs)
```

```python
import functools

import numpy as np
import jax
import jax.numpy as jnp
from jax import lax
from jax.experimental import pallas as pl
from jax.experimental.pallas import tpu as pltpu

F32 = jnp.float32
BF16 = jnp.bfloat16
MESH = pl.DeviceIdType.MESH

HEAD_DIM = 128
FOX_HEADS = 6
SWA_HEADS = 6
SWA_KV_HEADS = 2
SWA_GROUP = SWA_HEADS // SWA_KV_HEADS
MEM_HEADS = 4
WINDOW = 128
EPS = 1e-6
NEG_INF = -1e30
SCALE = HEAD_DIM ** -0.5

C_FQ = 0
C_FK = C_FQ + FOX_HEADS * HEAD_DIM
C_FV = C_FK + FOX_HEADS * HEAD_DIM
C_SQ = C_FV + FOX_HEADS * HEAD_DIM
C_SK = C_SQ + SWA_HEADS * HEAD_DIM
C_SV = C_SK + SWA_KV_HEADS * HEAD_DIM
C_MQ = C_SV + SWA_KV_HEADS * HEAD_DIM
C_FL = C_MQ + MEM_HEADS * HEAD_DIM
PROJ_W = C_FL + HEAD_DIM
FOX_W = FOX_HEADS * HEAD_DIM
REF_GROUPS = [
    (0, FOX_W, C_FQ), (FOX_W, FOX_W, C_FK), (2 * FOX_W, FOX_W, C_FV), (3 * FOX_W, FOX_HEADS, C_FL),
    (3 * FOX_W + FOX_HEADS, SWA_HEADS * HEAD_DIM, C_SQ),
    (3 * FOX_W + FOX_HEADS + SWA_HEADS * HEAD_DIM, SWA_KV_HEADS * HEAD_DIM, C_SK),
    (3 * FOX_W + FOX_HEADS + (SWA_HEADS + SWA_KV_HEADS) * HEAD_DIM, SWA_KV_HEADS * HEAD_DIM, C_SV),
    (3 * FOX_W + FOX_HEADS + (SWA_HEADS + 2 * SWA_KV_HEADS) * HEAD_DIM, MEM_HEADS * HEAD_DIM, C_MQ),
]

ADAM_LR = 0.001
ADAM_B1 = 0.9
ADAM_B2 = 0.999
ADAM_EPS = 1e-08
ADAM_WD = 0.01
ADAM_STEP = 10

V7X_VMEM_LIMIT = 56 * 1024 * 1024
N_CHIPS = 4
N_DEV = 8


def _tile(n, pref, mult=128):
    t = (min(pref, n) // mult) * mult
    while t >= mult:
        if n % t == 0:
            return t
        t -= mult
    return n


def _params(sem):
    return pltpu.CompilerParams(dimension_semantics=sem, vmem_limit_bytes=V7X_VMEM_LIMIT)


_DIMS = {"nn": (((1,), (0,)), ((), ())), "nt": (((1,), (1,)), ((), ())), "tn": (((0,), (0,)), ((), ()))}


def _dot(a, b, mode):
    return lax.dot_general(a, b, _DIMS[mode], preferred_element_type=F32)


def _mm(name, grid, pairs, acc_of, acc_shapes, extras, outs, epilogue, after=None):
    n_p, n_e, n_o, n_a = len(pairs), len(extras), len(outs), len(acc_shapes)
    n_w = 0 if after is None else 1
    nk = grid[2]
    n_in = sum(1 if a is None else 2 for a, *_ in pairs)

    def body(*refs):
        ex = refs[n_in:n_in + n_e]
        out = refs[n_in + n_e + n_w:n_in + n_e + n_w + n_o]
        accs = refs[n_in + n_e + n_w + n_o:]
        parts = [None] * n_a
        at = 0
        for p in range(n_p):
            if pairs[p][0] is None:
                a_ref, b_ref = refs[0], refs[at]
                at += 1
            else:
                a_ref, b_ref = refs[at], refs[at + 1]
                at += 2
            d = _dot(a_ref[...], b_ref[...], pairs[p][4])
            parts[acc_of[p]] = d if parts[acc_of[p]] is None else parts[acc_of[p]] + d

        def finish(vals):
            for o, r in zip(out, epilogue(vals, [e[...] for e in ex])):
                o[...] = r.astype(o.dtype)

        if nk == 1:
            finish(parts)
            return
        k = pl.program_id(2)

        @pl.when(k == 0)
        def _():
            for a, d in zip(accs, parts):
                a[...] = d

        @pl.when((k > 0) & (k < nk - 1))
        def _():
            for a, d in zip(accs, parts):
                a[...] += d

        @pl.when(k == nk - 1)
        def _():
            finish([a[...] + d for a, d in zip(accs, parts)])

    in_specs, args = [], []
    for a, a_spec, b, b_spec, _ in pairs:
        if a is not None:
            in_specs.append(a_spec)
            args.append(a)
        in_specs.append(b_spec)
        args.append(b)
    for e, e_spec in extras:
        in_specs.append(e_spec)
        args.append(e)
    if after is not None:
        in_specs.append(pl.BlockSpec(memory_space=pl.ANY))
        args.append(after)
    res = pl.pallas_call(
        body, name=name, grid=grid, in_specs=in_specs,
        out_specs=[s for _, s in outs], out_shape=[o for o, _ in outs],
        scratch_shapes=[pltpu.VMEM(s, F32) for s in acc_shapes] if nk > 1 else [],
        compiler_params=_params(("parallel", "parallel", "arbitrary")),
    )(*args)
    return res


def _mm2d(name, a, b, mode, out_dtype, tm=512, tn=1024, tk=1024, extras=(), epilogue=None, n_out=1, after=None, n_outer=False,
          resident=False, rows=(), tile_stat=False):
    if mode == "nn":
        (M, K), N = a.shape, b.shape[1]
    elif mode == "nt":
        (M, K), N = a.shape, b.shape[0]
    else:
        (K, M), N = a.shape, b.shape[1]
    tm, tn, tk = _tile(M, tm), _tile(N, tn), _tile(K, tk)
    assert not resident or tk == K

    def spec(shape, index, single=False):
        mode_kw = {"pipeline_mode": pl.Buffered(1)} if single else {}
        if n_outer:
            return pl.BlockSpec(shape, lambda j, i, k: index(i, j, k), **mode_kw)
        return pl.BlockSpec(shape, index, **mode_kw)

    single_a, single_b = resident and not n_outer, resident and n_outer
    a_spec = spec((tk, tm), lambda i, j, k: (k, i), single_a) if mode == "tn" else spec((tm, tk), lambda i, j, k: (i, k), single_a)
    b_spec = spec((tn, tk), lambda i, j, k: (j, k), single_b) if mode == "nt" else spec((tk, tn), lambda i, j, k: (k, j), single_b)
    mn = spec((tm, tn), lambda i, j, k: (i, j))
    if epilogue is None:
        epilogue = lambda accs, ex: [accs[0]]
    if not isinstance(out_dtype, (list, tuple)):
        out_dtype = [out_dtype] * n_out
    grid = (N // tn, M // tm, K // tk) if n_outer else (M // tm, N // tn, K // tk)
    outs = [(jax.ShapeDtypeStruct((M, N), d), mn) for d in out_dtype]
    if tile_stat:
        assert tn == N
        outs.append((jax.ShapeDtypeStruct((8 * (M // tm), 128), F32), spec((8, 128), lambda i, j, k: (i, 0))))
    res = _mm(name, grid, [(a, a_spec, b, b_spec, mode)], [0], [(tm, tn)],
              [(e, mn) for e in extras] + [(r, spec((1, tn), lambda i, j, k: (0, j))) for r in rows], outs, epilogue, after=after)
    return res[0] if len(res) == 1 else res


def _sigmoid(x):
    return 1.0 / (1.0 + jnp.exp(-x))


def _sigmoid_fast(x):
    return pl.reciprocal(1.0 + jnp.exp(-x), approx=True)


def _ffn_gu(name, h, wg, wu):
    T, D = h.shape
    nf, _, F4 = wg.shape
    tm, tk = _tile(T, 512), _tile(D, 2048)
    a_spec = pl.BlockSpec((tm, tk), lambda j, i, k: (i, k))
    b_spec = pl.BlockSpec((None, tk, F4), lambda j, i, k: (j, k, 0))
    o_spec = pl.BlockSpec((tm, F4), lambda j, i, k: (i, j))

    def epilogue(accs, ex):
        g, u = accs
        s = _sigmoid_fast(g)
        gs = g * s
        return [(s + s * (g - gs)) * u, gs, gs * u]

    sds = jax.ShapeDtypeStruct((T, nf * F4), BF16)
    return _mm(name, (nf, T // tm, D // tk), [(h, a_spec, wg, b_spec, "nn"), (None, None, wu, b_spec, "nn")], [0, 1],
               [(tm, F4), (tm, F4)], [], [(sds, o_spec)] * 3, epilogue)


def _rms_rows(x, gain):
    return x * lax.rsqrt(jnp.mean(x * x, axis=-1, keepdims=True) + EPS) * gain


def _residual_norm(name, a, w, xres, scale, gain, tm):
    def epilogue(accs, ex):
        y = ex[0] + scale * accs[0]
        return [y, _rms_rows(y, ex[1])]

    return _mm2d(name, a, w, "nn", [F32, BF16], tm=tm, tn=w.shape[1], tk=w.shape[0], n_outer=True, resident=True, extras=[xres],
                 rows=[gain], epilogue=epilogue)


def _ffn_down_loss(name, a, wd, xres, target):
    D = wd.shape[1]

    def epilogue(accs, ex):
        e = ex[0] + 0.5 * accs[0] - ex[1]
        d = e * (1.0 / D)
        return [d, 0.5 * d, jnp.zeros((8, 128), F32) + (0.5 / D) * jnp.sum(e * e)]

    return _mm2d(name, a, wd, "nn", [F32, BF16], tm=256, tn=D, tk=wd.shape[0], n_outer=True, resident=True, extras=[xres, target],
                 tile_stat=True, epilogue=epilogue)


def _ffn_bwd_down(tag, a, dyb, nf, after=None):
    return _mm2d(tag + "_dwd", a, dyb, "tn", F32, tm=a.shape[1] // nf, tn=1024, tk=a.shape[0], resident=True, after=after)


def _ffn_bwd_act(tag, dyb, wd, da_dg, da_du, nf, after=None):
    def act_bwd(accs, ex):
        return [accs[0] * ex[0].astype(F32), accs[0] * ex[1].astype(F32)]

    return _mm2d(tag + "_da", dyb, wd, "nt", BF16, tn=wd.shape[0] // nf, tk=2048, extras=[da_dg, da_du], epilogue=act_bwd, n_out=2,
                 n_outer=True, after=after)


def _ffn_bwd_gate_up(tag, h, dg, du, nf, after=None):
    T, D = h.shape
    F4 = dg.shape[1] // nf
    tm = _tile(D, 512)
    h_spec = pl.BlockSpec((T, tm), lambda j, i, k: (0, i))
    d_spec = pl.BlockSpec((T, F4), lambda j, i, k: (0, j), pipeline_mode=pl.Buffered(1))
    w_spec = pl.BlockSpec((None, tm, F4), lambda j, i, k: (j, i, 0))
    sds = jax.ShapeDtypeStruct((nf, D, F4), F32)
    return _mm(tag + "_dwgu", (nf, D // tm, 1), [(h, h_spec, dg, d_spec, "tn"), (None, None, du, d_spec, "tn")],
               [0, 1], [(tm, F4), (tm, F4)], [], [(sds, w_spec)] * 2, lambda accs, ex: accs, after=after)


def _ffn_bwd_x(tag, dg, du, wg, wu, after):
    T = dg.shape[0]
    nf, D, F4 = wg.shape
    tm, tn = _tile(T, 256), _tile(D, 1024)

    def body(dg_ref, du_ref, wg_ref, wu_ref, after_ref, o_ref):
        acc = None
        for j in range(nf):
            cols = slice(j * F4, (j + 1) * F4)
            part = _dot(dg_ref[:, cols], wg_ref[j], "nt") + _dot(du_ref[:, cols], wu_ref[j], "nt")
            acc = part if acc is None else acc + part
        o_ref[...] = acc

    a_spec = pl.BlockSpec((tm, nf * F4), lambda n, i: (i, 0))
    b_spec = pl.BlockSpec((nf, tn, F4), lambda n, i: (0, n, 0), pipeline_mode=pl.Buffered(1))
    return pl.pallas_call(
        body, name=tag + "_dh", grid=(D // tn, T // tm),
        in_specs=[a_spec, a_spec, b_spec, b_spec, pl.BlockSpec(memory_space=pl.ANY)],
        out_specs=pl.BlockSpec((tm, tn), lambda n, i: (i, n)), out_shape=jax.ShapeDtypeStruct((T, D), F32),
        compiler_params=_params(("parallel", "parallel")),
    )(dg, du, wg, wu, after)


def _rms_fwd(name, x, gain):
    R, D = x.shape
    tr = _tile(R, 256, 8)

    def body(x_ref, g_ref, o_ref):
        xv = x_ref[...]
        r = lax.rsqrt(jnp.mean(xv * xv, axis=-1, keepdims=True) + EPS)
        o_ref[...] = (xv * r * g_ref[...]).astype(BF16)

    return pl.pallas_call(
        body, name=name, grid=(R // tr,),
        in_specs=[pl.BlockSpec((tr, D), lambda i: (i, 0)), pl.BlockSpec((1, D), lambda i: (0, 0))],
        out_specs=pl.BlockSpec((tr, D), lambda i: (i, 0)), out_shape=jax.ShapeDtypeStruct((R, D), BF16),
        compiler_params=_params(("parallel",)),
    )(x, gain)


def _rms_bwd(name, dh, x, gain, dres, bscale):
    R, D = x.shape
    tr = _tile(R, 256, 8)

    def body(dh_ref, x_ref, g_ref, dres_ref, dx_ref, dxb_ref, dg_ref):
        xv, dy = x_ref[...], dh_ref[...]
        r = lax.rsqrt(jnp.mean(xv * xv, axis=-1, keepdims=True) + EPS)
        xn = xv * r
        uu = dy * g_ref[...]
        dx = dres_ref[...] + r * (uu - xn * jnp.mean(xn * uu, axis=-1, keepdims=True))
        dx_ref[...] = dx
        dxb_ref[...] = (bscale * dx).astype(BF16)
        part = jnp.sum(dy * xn, axis=0, keepdims=True)

        @pl.when(pl.program_id(0) == 0)
        def _():
            dg_ref[...] = part

        @pl.when(pl.program_id(0) > 0)
        def _():
            dg_ref[...] += part

    row = pl.BlockSpec((tr, D), lambda i: (i, 0))
    vec = pl.BlockSpec((1, D), lambda i: (0, 0))
    return pl.pallas_call(
        body, name=name, grid=(R // tr,), in_specs=[row, row, vec, row], out_specs=[row, row, vec],
        out_shape=[jax.ShapeDtypeStruct((R, D), F32), jax.ShapeDtypeStruct((R, D), BF16), jax.ShapeDtypeStruct((1, D), F32)],
        compiler_params=_params(("arbitrary",)),
    )(dh, x, gain, dres)


def _head_norm(xs, g):
    r = lax.rsqrt(jnp.mean(xs * xs, axis=-1, keepdims=True) + EPS)
    return xs * r * g


def _head_norm_bwd(xs, g, dy):
    r = lax.rsqrt(jnp.mean(xs * xs, axis=-1, keepdims=True) + EPS)
    xn = xs * r
    uu = dy * g
    return r * (uu - xn * jnp.mean(xn * uu, axis=-1, keepdims=True)), jnp.sum(dy * xn, axis=0, keepdims=True)


NORMED = [(C_FQ, FOX_HEADS, 0), (C_FK, FOX_HEADS, 1), (C_SQ, SWA_HEADS, 2), (C_SK, SWA_KV_HEADS, 3), (C_MQ, MEM_HEADS, 4)]
PLAIN = [(C_FV, FOX_HEADS), (C_SV, SWA_KV_HEADS)]


def _prep_fwd(proj, gains):
    T = proj.shape[0]
    tr = _tile(T, 256, 128)

    def body(p_ref, g_ref, o_ref, lf_ref, kt_ref, vt_ref):
        for start, heads, row in NORMED:
            gn = g_ref[row:row + 1, :]
            for hh in range(heads):
                sl = slice(start + hh * HEAD_DIM, start + (hh + 1) * HEAD_DIM)
                y = _head_norm(p_ref[:, sl], gn)
                o_ref[:, sl] = y.astype(BF16)
                if start == C_FK:
                    kt_ref[hh * HEAD_DIM:(hh + 1) * HEAD_DIM, :] = y.T.astype(BF16)
        for start, heads in PLAIN:
            sl = slice(start, start + heads * HEAD_DIM)
            o_ref[:, sl] = p_ref[:, sl].astype(BF16)
        for hh in range(FOX_HEADS):
            sl = slice(C_FV + hh * HEAD_DIM, C_FV + (hh + 1) * HEAD_DIM)
            vt_ref[hh * HEAD_DIM:(hh + 1) * HEAD_DIM, :] = p_ref[:, sl].T.astype(BF16)
        zb = p_ref[:, C_FL:C_FL + HEAD_DIM] + g_ref[5:6, :]
        o_ref[:, C_FL:C_FL + HEAD_DIM] = jnp.zeros((tr, HEAD_DIM), BF16)
        lf_ref[...] = jnp.minimum(zb, 0.0) - jnp.log(1.0 + jnp.exp(-jnp.abs(zb)))

    return pl.pallas_call(
        body, name="prep_fwd", grid=(T // tr,),
        in_specs=[pl.BlockSpec((tr, PROJ_W), lambda i: (i, 0)), pl.BlockSpec((8, 128), lambda i: (0, 0))],
        out_specs=[pl.BlockSpec((tr, PROJ_W), lambda i: (i, 0)), pl.BlockSpec((tr, HEAD_DIM), lambda i: (i, 0)),
                   pl.BlockSpec((FOX_W, tr), lambda i: (0, i)), pl.BlockSpec((FOX_W, tr), lambda i: (0, i))],
        out_shape=[jax.ShapeDtypeStruct((T, PROJ_W), BF16), jax.ShapeDtypeStruct((T, HEAD_DIM), F32),
                   jax.ShapeDtypeStruct((FOX_W, T), BF16), jax.ShapeDtypeStruct((FOX_W, T), BF16)],
        compiler_params=_params(("parallel",)),
    )(proj, gains)


def _prep_bwd(proj, gains, dfq, dfk, dfv, dsq, dsk, dsv, dmq, dlogf):
    T = proj.shape[0]
    tr = _tile(T, 256, 8)
    d_normed = {C_FQ: 0, C_FK: 1, C_SQ: 3, C_SK: 4, C_MQ: 6}
    d_plain = {C_FV: 2, C_SV: 5}

    def body(p_ref, g_ref, *rest):
        d_refs, dlf_ref, o_ref, dg_ref = rest[:7], rest[7], rest[8], rest[9]
        rows = []
        for start, heads, row in NORMED:
            gn = g_ref[row:row + 1, :]
            d_ref = d_refs[d_normed[start]]
            tot = jnp.zeros((1, HEAD_DIM), F32)
            for hh in range(heads):
                sl = slice(start + hh * HEAD_DIM, start + (hh + 1) * HEAD_DIM)
                dx, dgn = _head_norm_bwd(p_ref[:, sl], gn, d_ref[:, hh * HEAD_DIM:(hh + 1) * HEAD_DIM])
                o_ref[:, sl] = dx.astype(BF16)
                tot = tot + dgn
            rows.append(tot)
        for start, heads in PLAIN:
            o_ref[:, start:start + heads * HEAD_DIM] = d_refs[d_plain[start]][...].astype(BF16)
        zb = p_ref[:, C_FL:C_FL + HEAD_DIM] + g_ref[5:6, :]
        lane = lax.broadcasted_iota(jnp.int32, (tr, HEAD_DIM), 1)
        dz = jnp.where(lane < FOX_HEADS, dlf_ref[...] * (1.0 - _sigmoid(zb)), 0.0)
        o_ref[:, C_FL:C_FL + HEAD_DIM] = dz.astype(BF16)
        rows.append(jnp.sum(dz, axis=0, keepdims=True))
        part = jnp.concatenate(rows + [jnp.zeros((2, HEAD_DIM), F32)], axis=0)

        @pl.when(pl.program_id(0) == 0)
        def _():
            dg_ref[...] = part

        @pl.when(pl.program_id(0) > 0)
        def _():
            dg_ref[...] += part

    def rows_of(w):
        return pl.BlockSpec((tr, w), lambda i: (i, 0))

    small = pl.BlockSpec((8, 128), lambda i: (0, 0))
    ds = [dfq, dfk, dfv, dsq, dsk, dsv, dmq]
    return pl.pallas_call(
        body, name="prep_bwd", grid=(T // tr,),
        in_specs=[rows_of(PROJ_W), small] + [rows_of(d.shape[1]) for d in ds] + [rows_of(HEAD_DIM)],
        out_specs=[rows_of(PROJ_W), small],
        out_shape=[jax.ShapeDtypeStruct((T, PROJ_W), BF16), jax.ShapeDtypeStruct((8, 128), F32)],
        compiler_params=_params(("arbitrary",)),
    )(proj, gains, *ds, dlogf)


def _head_norm_rows(x, gain):
    R, W = x.shape

    def body(x_ref, g_ref, o_ref):
        for hh in range(W // HEAD_DIM):
            sl = slice(hh * HEAD_DIM, (hh + 1) * HEAD_DIM)
            o_ref[:, sl] = _head_norm(x_ref[:, sl], g_ref[...]).astype(BF16)

    return pl.pallas_call(body, name="mem_k_norm", out_shape=jax.ShapeDtypeStruct((R, W), BF16))(x, gain)


def _head_norm_rows_bwd(x, gain, dy):
    R, W = x.shape

    def body(x_ref, g_ref, dy_ref, dx_ref, dg_ref):
        tot = jnp.zeros((1, HEAD_DIM), F32)
        for hh in range(W // HEAD_DIM):
            sl = slice(hh * HEAD_DIM, (hh + 1) * HEAD_DIM)
            dx, dgn = _head_norm_bwd(x_ref[:, sl], g_ref[...], dy_ref[:, sl])
            dx_ref[:, sl] = dx.astype(BF16)
            tot = tot + dgn
        dg_ref[...] = tot

    return pl.pallas_call(
        body, name="mem_k_norm_bwd",
        out_shape=[jax.ShapeDtypeStruct((R, W), BF16), jax.ShapeDtypeStruct((1, HEAD_DIM), F32)])(x, gain, dy)


def _cumsum_rows(name, xs, reverse, columns=None):
    T, W = xs[0].shape
    tb = _tile(T, 512, 8)
    nb = T // tb
    n_in = len(xs) + (0 if columns is None else 1)

    def body(*refs):
        o_ref, carry = refs[n_in], refs[n_in + 1]

        @pl.when(pl.program_id(0) == 0)
        def _():
            carry[...] = jnp.zeros_like(carry)

        xv = refs[0][...]
        for x_ref in refs[1:len(xs)]:
            xv = xv + x_ref[...]
        if columns is not None:
            lane = lax.broadcasted_iota(jnp.int32, (tb, W), 1)
            for hh in range(columns.shape[0]):
                xv = xv + jnp.where(lane == hh, refs[len(xs)][hh], 0.0)
        r = lax.broadcasted_iota(jnp.int32, (tb, tb), 0)
        cc = lax.broadcasted_iota(jnp.int32, (tb, tb), 1)
        tri = jnp.where((cc >= r) if reverse else (cc <= r), 1.0, 0.0).astype(F32)
        o_ref[...] = jnp.dot(tri, xv, precision=lax.Precision.HIGHEST, preferred_element_type=F32) + carry[...]
        carry[...] += jnp.sum(xv, axis=0, keepdims=True)

    idx = (lambda i: (nb - 1 - i, 0)) if reverse else (lambda i: (i, 0))
    in_specs = [pl.BlockSpec((tb, W), idx)] * len(xs)
    if columns is not None:
        in_specs.append(pl.BlockSpec((columns.shape[0], tb, 1), lambda i: (0, idx(i)[0], 0)))
    return pl.pallas_call(
        body, name=name, grid=(nb,), in_specs=in_specs, out_specs=pl.BlockSpec((tb, W), idx),
        out_shape=jax.ShapeDtypeStruct((T, W), F32), scratch_shapes=[pltpu.VMEM((1, W), F32)],
        compiler_params=_params(("arbitrary",)),
    )(*xs, *([] if columns is None else [columns]))


def _triangle(nq, by_column):
    if by_column:
        blocks = [(i, j) for j in range(nq) for i in range(j, nq)]
    else:
        blocks = [(i, j) for i in range(nq) for j in range(i + 1)]
    return jnp.asarray(np.array(blocks, np.int32).T)


def _fox_scores_t(k, q, cq_row, ck_rep, on_diagonal):
    n = q.shape[0]
    s = _dot(k, q, "nt") * SCALE + (cq_row - jnp.tile(ck_rep, (1, n // HEAD_DIM)))
    if on_diagonal:
        s = jnp.where(lax.broadcasted_iota(jnp.int32, (n, n), 0) <= lax.broadcasted_iota(jnp.int32, (n, n), 1), s, NEG_INF)
    return s


def _fox_fwd(qkv, v_t, cq_row, ck_rep):
    T = qkv.shape[0]
    tq = _tile(T, 1024)
    nq = T // tq
    steps = nq * (nq + 1) // 2
    HQ, HK = C_FQ // HEAD_DIM, C_FK // HEAD_DIM

    def body(tab, q_ref, k_ref, vt_ref, cq_ref, ck_ref, o_ref, of_ref, lse_ref, m_sc, l_sc, acc_sc):
        i, j = tab[0, pl.program_id(1)], tab[1, pl.program_id(1)]

        @pl.when(j == 0)
        def _():
            m_sc[...] = jnp.full_like(m_sc, NEG_INF)
            l_sc[...] = jnp.zeros_like(l_sc)
            acc_sc[...] = jnp.zeros_like(acc_sc)

        def step(on_diagonal):
            s = _fox_scores_t(k_ref[...], q_ref[...], cq_ref[...], ck_ref[...], on_diagonal)
            m_new = jnp.maximum(m_sc[...], jnp.max(s, axis=0, keepdims=True))
            alpha = jnp.exp(m_sc[...] - m_new)
            p = jnp.exp(s - m_new)
            l_sc[...] = alpha * l_sc[...] + jnp.sum(p, axis=0, keepdims=True)
            acc_sc[...] = alpha * acc_sc[...] + _dot(vt_ref[...], p.astype(BF16), "nn")
            m_sc[...] = m_new

        @pl.when(j < i)
        def _():
            step(False)

        @pl.when(j == i)
        def _():
            step(True)
            o = (acc_sc[...] / l_sc[...]).T
            o_ref[...] = o.astype(BF16)
            of_ref[...] = o
            lse_ref[...] = m_sc[...] + jnp.log(l_sc[...])

    qrow = pl.BlockSpec((None, 1, tq), lambda h, s, tab: (h, 0, tab[0, s]))
    return pl.pallas_call(
        body, name="fox_fwd",
        grid_spec=pltpu.PrefetchScalarGridSpec(
            num_scalar_prefetch=1, grid=(FOX_HEADS, steps),
            in_specs=[pl.BlockSpec((tq, HEAD_DIM), lambda h, s, tab: (tab[0, s], HQ + h)),
                      pl.BlockSpec((tq, HEAD_DIM), lambda h, s, tab: (tab[1, s], HK + h)),
                      pl.BlockSpec((HEAD_DIM, tq), lambda h, s, tab: (h, tab[1, s])), qrow,
                      pl.BlockSpec((None, tq, HEAD_DIM), lambda h, s, tab: (h, tab[1, s], 0))],
            out_specs=[pl.BlockSpec((tq, HEAD_DIM), lambda h, s, tab: (tab[0, s], h)),
                       pl.BlockSpec((tq, HEAD_DIM), lambda h, s, tab: (tab[0, s], h)), qrow],
            scratch_shapes=[pltpu.VMEM((1, tq), F32), pltpu.VMEM((1, tq), F32), pltpu.VMEM((HEAD_DIM, tq), F32)]),
        out_shape=[jax.ShapeDtypeStruct((T, FOX_W), BF16), jax.ShapeDtypeStruct((T, FOX_W), F32),
                   jax.ShapeDtypeStruct((FOX_HEADS, 1, T), F32)],
        compiler_params=_params(("parallel", "arbitrary")),
    )(_triangle(nq, False), qkv, qkv, v_t, cq_row, ck_rep)


def _fox_delta(dmix, out_f32):
    T = out_f32.shape[0]
    tr = _tile(T, 512, 8)

    def body(do_ref, o_ref, d_ref):
        lane = lax.broadcasted_iota(jnp.int32, (tr, HEAD_DIM), 1)
        acc = jnp.zeros((tr, HEAD_DIM), F32)
        for hh in range(FOX_HEADS):
            sl = slice(hh * HEAD_DIM, (hh + 1) * HEAD_DIM)
            d = jnp.sum(do_ref[:, sl].astype(F32) * o_ref[:, sl], axis=-1, keepdims=True)
            acc = jnp.where(lane == hh, d, acc)
        d_ref[...] = acc

    blk = pl.BlockSpec((tr, FOX_W), lambda i: (i, 0))
    return pl.pallas_call(
        body, name="fox_delta", grid=(T // tr,), in_specs=[blk, blk], out_specs=pl.BlockSpec((tr, HEAD_DIM), lambda i: (i, 0)),
        out_shape=jax.ShapeDtypeStruct((T, HEAD_DIM), F32), compiler_params=_params(("parallel",)),
    )(dmix, out_f32)


def _fox_bwd(qkv, k_t, cq_row, ck_rep, delta_row, lse, dmix):
    T = qkv.shape[0]
    tq = _tile(T, 1024)
    nq = T // tq
    steps = nq * (nq + 1) // 2
    HQ, HK, HV = C_FQ // HEAD_DIM, C_FK // HEAD_DIM, C_FV // HEAD_DIM

    def body(tab, q_ref, k_ref, kt_ref, v_ref, cq_ref, ck_ref, delta_ref, lse_ref, do_ref,
             dq_ref, dk_ref, dv_ref, dck_ref, dcq_ref, dk_sc, dv_sc, dc_sc, dqt_sc):
        qi, kj = tab[0, pl.program_id(1)], tab[1, pl.program_id(1)]

        @pl.when(qi == kj)
        def _():
            dk_sc[...] = jnp.zeros_like(dk_sc)
            dv_sc[...] = jnp.zeros_like(dv_sc)
            dc_sc[...] = jnp.zeros_like(dc_sc)

        def step(on_diagonal):
            q, k, v, do = q_ref[...], k_ref[...], v_ref[...], do_ref[...]
            p = jnp.exp(_fox_scores_t(k, q, cq_ref[...], ck_ref[...], on_diagonal) - lse_ref[...])
            dp = _dot(v, do, "nt")
            ds = p * (dp - delta_ref[...])
            dsb = ds.astype(BF16)
            dv_sc[...] += _dot(p.astype(BF16), do, "nn")
            dk_sc[...] += _dot(dsb, q, "nn")
            dc_sc[...] += jnp.sum(ds, axis=1, keepdims=True)
            dq_part = _dot(kt_ref[...], dsb, "nn") * SCALE
            dcq_part = jnp.sum(ds, axis=0, keepdims=True)

            @pl.when(kj == 0)
            def _():
                dqt_sc[qi] = dq_part
                dcq_ref[qi] = dcq_part

            @pl.when(kj > 0)
            def _():
                dqt_sc[qi] += dq_part
                dcq_ref[qi] += dcq_part

            if on_diagonal:
                dq_ref[...] = dqt_sc[qi].T

        @pl.when(qi > kj)
        def _():
            step(False)

        @pl.when(qi == kj)
        def _():
            step(True)

        @pl.when(qi == nq - 1)
        def _():
            dk_ref[...] = dk_sc[...] * SCALE
            dv_ref[...] = dv_sc[...]
            dck_ref[...] = -dc_sc[...]

    def rows(base):
        return pl.BlockSpec((tq, HEAD_DIM), lambda h, s, tab: (tab[0, s], base + h))

    def cols(base):
        return pl.BlockSpec((tq, HEAD_DIM), lambda h, s, tab: (tab[1, s], base + h))

    qrow = pl.BlockSpec((None, 1, tq), lambda h, s, tab: (h, 0, tab[0, s]))
    sds = jax.ShapeDtypeStruct((T, FOX_W), F32)
    return pl.pallas_call(
        body, name="fox_bwd",
        grid_spec=pltpu.PrefetchScalarGridSpec(
            num_scalar_prefetch=1, grid=(FOX_HEADS, steps),
            in_specs=[rows(HQ), cols(HK), pl.BlockSpec((HEAD_DIM, tq), lambda h, s, tab: (h, tab[1, s])), cols(HV), qrow,
                      pl.BlockSpec((None, tq, HEAD_DIM), lambda h, s, tab: (h, tab[1, s], 0)), qrow, qrow, rows(0)],
            out_specs=[cols(0), cols(0), cols(0), pl.BlockSpec((None, tq, 1), lambda h, s, tab: (h, tab[1, s], 0)),
                       pl.BlockSpec((None, nq, 1, tq), lambda h, s, tab: (h, 0, 0, 0))],
            scratch_shapes=[pltpu.VMEM((tq, HEAD_DIM), F32), pltpu.VMEM((tq, HEAD_DIM), F32), pltpu.VMEM((tq, 1), F32),
                            pltpu.VMEM((nq, HEAD_DIM, tq), F32)]),
        out_shape=[sds, sds, sds, jax.ShapeDtypeStruct((FOX_HEADS, T, 1), F32), jax.ShapeDtypeStruct((FOX_HEADS, nq, 1, tq), F32)],
        compiler_params=_params(("parallel", "arbitrary")),
    )(_triangle(nq, True), qkv, qkv, k_t, qkv, cq_row, ck_rep, delta_row, lse, dmix)


GW = SWA_GROUP * HEAD_DIM
GR = SWA_GROUP * WINDOW


def _swa_scores(q_ref, kp_ref, kc_ref, slope_ref, n):
    q = q_ref[...]
    qs = jnp.concatenate([q[:, t * HEAD_DIM:(t + 1) * HEAD_DIM] for t in range(SWA_GROUP)], axis=0)
    kb = jnp.concatenate([kp_ref[...], kc_ref[...]], axis=0)
    r = lax.broadcasted_iota(jnp.int32, (GR, 2 * WINDOW), 0) & (WINDOW - 1)
    jj = lax.broadcasted_iota(jnp.int32, (GR, 2 * WINDOW), 1)
    dist = WINDOW + r - jj
    valid = (dist >= 0) & (dist < WINDOW) & ((n > 0) | (jj >= WINDOW))
    s = _dot(qs, kb, "nt") * SCALE - slope_ref[...] * dist.astype(F32)
    return qs, kb, jnp.where(valid, s, NEG_INF), valid


def _swa_specs():
    HQ, HK, HV = C_SQ // GW, C_SK // HEAD_DIM, C_SV // HEAD_DIM
    q_spec = pl.BlockSpec((WINDOW, GW), lambda g, n: (n, HQ + g))

    def prev(base):
        return pl.BlockSpec((WINDOW, HEAD_DIM), lambda g, n: (jnp.maximum(n - 1, 0), base + g))

    def cur(base):
        return pl.BlockSpec((WINDOW, HEAD_DIM), lambda g, n: (n, base + g))

    col = pl.BlockSpec((None, GR, 1), lambda g, n: (g, 0, 0))
    return q_spec, prev(HK), cur(HK), prev(HV), cur(HV), col


def _swa_fwd(qkv, slopes, sinks):
    T = qkv.shape[0]
    nb = T // WINDOW
    assert C_SQ % GW == 0

    def body(q_ref, kp_ref, kc_ref, vp_ref, vc_ref, slope_ref, sink_ref, o_ref, lse_ref):
        n = pl.program_id(1)
        _, _, s, _ = _swa_scores(q_ref, kp_ref, kc_ref, slope_ref, n)
        m = jnp.maximum(jnp.max(s, axis=-1, keepdims=True), sink_ref[...])
        p = jnp.exp(s - m)
        l = jnp.sum(p, axis=-1, keepdims=True) + jnp.exp(sink_ref[...] - m)
        vb = jnp.concatenate([vp_ref[...], vc_ref[...]], axis=0)
        o = _dot(p.astype(BF16), vb, "nn") / l
        for t in range(SWA_GROUP):
            o_ref[:, t * HEAD_DIM:(t + 1) * HEAD_DIM] = o[t * WINDOW:(t + 1) * WINDOW, :].astype(BF16)
        lse_ref[...] = m + jnp.log(l)

    q_spec, kp, kc, vp, vc, col = _swa_specs()
    return pl.pallas_call(
        body, name="swa_fwd", grid=(SWA_KV_HEADS, nb), in_specs=[q_spec, kp, kc, vp, vc, col, col],
        out_specs=[pl.BlockSpec((WINDOW, GW), lambda g, n: (n, g)), pl.BlockSpec((None, None, GR, 1), lambda g, n: (g, n, 0, 0))],
        out_shape=[jax.ShapeDtypeStruct((T, SWA_HEADS * HEAD_DIM), BF16), jax.ShapeDtypeStruct((SWA_KV_HEADS, nb, GR, 1), F32)],
        compiler_params=_params(("parallel", "arbitrary")),
    )(qkv, qkv, qkv, qkv, qkv, slopes, sinks)


def _swa_bwd(qkv, slopes, sinks, out, lse, dmix):
    T = qkv.shape[0]
    nb = T // WINDOW
    DO = FOX_W // GW
    assert FOX_W % GW == 0

    def body(q_ref, kp_ref, kc_ref, vp_ref, vc_ref, slope_ref, sink_ref, o_ref, lse_ref, do_ref,
             dq_ref, dk_ref, dv_ref, dsink_ref, sink_sc):
        n = pl.program_id(1)

        @pl.when(n == 0)
        def _():
            dk_ref[...] = jnp.zeros_like(dk_ref)
            dv_ref[...] = jnp.zeros_like(dv_ref)
            sink_sc[...] = jnp.zeros_like(sink_sc)

        qs, kb, s, valid = _swa_scores(q_ref, kp_ref, kc_ref, slope_ref, n)
        lse = lse_ref[...]
        p = jnp.where(valid, jnp.exp(s - lse), 0.0)
        vb = jnp.concatenate([vp_ref[...], vc_ref[...]], axis=0)
        do = jnp.concatenate([do_ref[:, t * HEAD_DIM:(t + 1) * HEAD_DIM] for t in range(SWA_GROUP)], axis=0)
        oo = jnp.concatenate([o_ref[:, t * HEAD_DIM:(t + 1) * HEAD_DIM] for t in range(SWA_GROUP)], axis=0)
        dp = _dot(do, vb, "nt")
        delta = jnp.sum(do.astype(F32) * oo.astype(F32), axis=-1, keepdims=True)
        ds = p * (dp - delta)
        dsb = ds.astype(BF16)
        dq = _dot(dsb, kb, "nn") * SCALE
        for t in range(SWA_GROUP):
            dq_ref[:, t * HEAD_DIM:(t + 1) * HEAD_DIM] = dq[t * WINDOW:(t + 1) * WINDOW, :]
        dkb = _dot(dsb, qs, "tn") * SCALE
        dvb = _dot(p.astype(BF16), do, "tn")
        r_prev = pl.ds(pl.multiple_of(jnp.maximum(n - 1, 0) * WINDOW, WINDOW), WINDOW)
        r_cur = pl.ds(pl.multiple_of(n * WINDOW, WINDOW), WINDOW)
        dk_ref[r_prev, :] += dkb[:WINDOW, :]
        dk_ref[r_cur, :] += dkb[WINDOW:, :]
        dv_ref[r_prev, :] += dvb[:WINDOW, :]
        dv_ref[r_cur, :] += dvb[WINDOW:, :]
        sink_sc[...] -= jnp.exp(sink_ref[...] - lse) * delta

        @pl.when(n == nb - 1)
        def _():
            tot = [jnp.zeros((1, 128), F32) + jnp.sum(sink_sc[t * WINDOW:(t + 1) * WINDOW, :]) for t in range(SWA_GROUP)]
            dsink_ref[...] = jnp.concatenate(tot + [jnp.zeros((8 - SWA_GROUP, 128), F32)], axis=0)

    q_spec, kp, kc, vp, vc, col = _swa_specs()
    kv_acc = pl.BlockSpec((T, HEAD_DIM), lambda g, n: (0, g))
    return pl.pallas_call(
        body, name="swa_bwd", grid=(SWA_KV_HEADS, nb),
        in_specs=[q_spec, kp, kc, vp, vc, col, col, pl.BlockSpec((WINDOW, GW), lambda g, n: (n, g)),
                  pl.BlockSpec((None, None, GR, 1), lambda g, n: (g, n, 0, 0)), pl.BlockSpec((WINDOW, GW), lambda g, n: (n, DO + g))],
        out_specs=[pl.BlockSpec((WINDOW, GW), lambda g, n: (n, g)), kv_acc, kv_acc, pl.BlockSpec((None, 8, 128), lambda g, n: (g, 0, 0))],
        out_shape=[jax.ShapeDtypeStruct((T, SWA_HEADS * HEAD_DIM), F32), jax.ShapeDtypeStruct((T, SWA_KV_HEADS * HEAD_DIM), F32),
                   jax.ShapeDtypeStruct((T, SWA_KV_HEADS * HEAD_DIM), F32), jax.ShapeDtypeStruct((SWA_KV_HEADS, 8, 128), F32)],
        scratch_shapes=[pltpu.VMEM((GR, 1), F32)],
        compiler_params=_params(("parallel", "arbitrary")),
    )(qkv, qkv, qkv, qkv, qkv, slopes, sinks, out, lse, dmix)


def _mem_fwd(qkv, mk, mv):
    T, ML = qkv.shape[0], mk.shape[0]
    tq = _tile(T, 1024)
    HQ = C_MQ // HEAD_DIM

    def body(q_ref, k_ref, v_ref, o_ref, lse_ref):
        s = _dot(q_ref[...], k_ref[...], "nt") * SCALE
        m = jnp.max(s, axis=-1, keepdims=True)
        p = jnp.exp(s - m)
        l = jnp.sum(p, axis=-1, keepdims=True)
        o_ref[...] = (_dot(p.astype(BF16), v_ref[...], "nn") / l).astype(BF16)
        lse_ref[...] = m + jnp.log(l)

    kv = pl.BlockSpec((ML, HEAD_DIM), lambda h, i: (0, h))
    return pl.pallas_call(
        body, name="mem_fwd", grid=(MEM_HEADS, T // tq),
        in_specs=[pl.BlockSpec((tq, HEAD_DIM), lambda h, i: (i, HQ + h)), kv, kv],
        out_specs=[pl.BlockSpec((tq, HEAD_DIM), lambda h, i: (i, h)), pl.BlockSpec((None, tq, 1), lambda h, i: (h, i, 0))],
        out_shape=[jax.ShapeDtypeStruct((T, MEM_HEADS * HEAD_DIM), BF16), jax.ShapeDtypeStruct((MEM_HEADS, T, 1), F32)],
        compiler_params=_params(("parallel", "arbitrary")),
    )(qkv, mk, mv)


def _mem_bwd(qkv, mk, mv, out, lse, dmix):
    T, ML = qkv.shape[0], mk.shape[0]
    tq = _tile(T, 1024)
    HQ = C_MQ // HEAD_DIM
    DO = (FOX_W + SWA_HEADS * HEAD_DIM) // HEAD_DIM

    def body(q_ref, k_ref, v_ref, o_ref, lse_ref, do_ref, dq_ref, dk_ref, dv_ref):
        q, k, v, do = q_ref[...], k_ref[...], v_ref[...], do_ref[...]
        p = jnp.exp(_dot(q, k, "nt") * SCALE - lse_ref[...])
        dp = _dot(do, v, "nt")
        delta = jnp.sum(do.astype(F32) * o_ref[...].astype(F32), axis=-1, keepdims=True)
        dsb = (p * (dp - delta)).astype(BF16)
        dq_ref[...] = _dot(dsb, k, "nn") * SCALE
        dk_part = _dot(dsb, q, "tn") * SCALE
        dv_part = _dot(p.astype(BF16), do, "tn")

        @pl.when(pl.program_id(1) == 0)
        def _():
            dk_ref[...] = dk_part
            dv_ref[...] = dv_part

        @pl.when(pl.program_id(1) > 0)
        def _():
            dk_ref[...] += dk_part
            dv_ref[...] += dv_part

    kv = pl.BlockSpec((ML, HEAD_DIM), lambda h, i: (0, h))
    qb = pl.BlockSpec((tq, HEAD_DIM), lambda h, i: (i, h))
    return pl.pallas_call(
        body, name="mem_bwd", grid=(MEM_HEADS, T // tq),
        in_specs=[pl.BlockSpec((tq, HEAD_DIM), lambda h, i: (i, HQ + h)), kv, kv, qb,
                  pl.BlockSpec((None, tq, 1), lambda h, i: (h, i, 0)), pl.BlockSpec((tq, HEAD_DIM), lambda h, i: (i, DO + h))],
        out_specs=[qb, kv, kv],
        out_shape=[jax.ShapeDtypeStruct((T, MEM_HEADS * HEAD_DIM), F32), jax.ShapeDtypeStruct((ML, MEM_HEADS * HEAD_DIM), F32),
                   jax.ShapeDtypeStruct((ML, MEM_HEADS * HEAD_DIM), F32)],
        compiler_params=_params(("parallel", "arbitrary")),
    )(qkv, mk, mv, out, lse, dmix)


HBM = pl.BlockSpec(memory_space=pltpu.HBM)


def _place():
    x, y, c = lax.axis_index("x"), lax.axis_index("y"), lax.axis_index("c")
    chips = [(1 - x, y), (x, 1 - y), (1 - x, 1 - y)]
    return x, y, c, chips


def _remote(src, dst, send_sem, recv_sem, device):
    return pltpu.make_async_remote_copy(src_ref=src, dst_ref=dst, send_sem=send_sem, recv_sem=recv_sem,
                                        device_id=device, device_id_type=MESH)


def _place_ids():
    x, y, c = lax.axis_index("x"), lax.axis_index("y"), lax.axis_index("c")
    order = [2 * x + y, 2 * (1 - x) + y, 2 * x + (1 - y), 2 * (1 - x) + (1 - y)]
    return jnp.stack([2 * x + y, c] + order).astype(jnp.int32)


def _cast_place(name, w, ids, *after):
    R, C = w.shape
    tr = _tile(R, 256, 16)

    def body(ids_ref, w_ref, *rest):
        rest[-1][...] = w_ref[...].astype(BF16)

    return pl.pallas_call(
        body, name=name,
        grid_spec=pltpu.PrefetchScalarGridSpec(
            num_scalar_prefetch=1, grid=(R // tr,),
            in_specs=[pl.BlockSpec((tr, C), lambda i, ids: (i, 0))] + [pl.BlockSpec(memory_space=pl.ANY)] * len(after),
            out_specs=pl.BlockSpec((None, tr, C), lambda i, ids: (ids[0], i, 0))),
        out_shape=jax.ShapeDtypeStruct((N_CHIPS, R, C), BF16), compiler_params=_params(("parallel",)),
    )(ids, w, *after)


SEM = pl.BlockSpec(memory_space=pltpu.SEMAPHORE)
EFFECT = pltpu.SideEffectType.DATAFLOW_SIDE_EFFECTING


def _hbm(a):
    return pltpu.with_memory_space_constraint(a, pltpu.HBM)


def _gather_start(name, placed, after, to_sibling=False):
    n = len(placed)

    ns = 3 * n

    def body(*refs):
        send, recv = refs[n + 1:n + 1 + ns], refs[n + 1 + ns:n + 1 + 2 * ns]
        buf = refs[n + 1 + 2 * ns:2 * n + 1 + 2 * ns]
        token = refs[2 * n + 1 + 2 * ns]
        x, y, c, chips = _place()
        me = 2 * x + y
        for a in range(n):
            half = buf[a].shape[1] // 2
            for j, (cx, cy) in enumerate(chips):
                block, peer = (2 * cx + cy, (x, y, 1 - c)) if to_sibling else (me, (cx, cy, c))
                part = buf[a].at[block, pl.ds(c * half, half)]
                _remote(part, part, send[3 * a + j], recv[3 * a + j], peer).start()
        token[...] = jnp.zeros_like(token)

    res = pl.pallas_call(
        body, name=name, in_specs=[HBM] * n + [pl.BlockSpec(memory_space=pl.ANY)],
        out_specs=[SEM] * (2 * ns) + [HBM] * n + [pl.BlockSpec(memory_space=pltpu.VMEM)],
        out_shape=[pltpu.SemaphoreType.DMA(())] * (2 * ns)
        + [pltpu.HBM(s.shape, s.dtype) for s in placed] + [jax.ShapeDtypeStruct((8, 128), F32)],
        input_output_aliases={a: 2 * ns + a for a in range(n)},
        compiler_params=pltpu.CompilerParams(has_side_effects=EFFECT),
    )(*[_hbm(s) for s in placed], after)
    return list(res[:ns]), list(res[ns:2 * ns]), list(res[2 * ns:2 * ns + n]), res[2 * ns + n]


def _gather_wait(name, send, recv, bufs, after, to_sibling=False):
    n = len(bufs)

    ns = 3 * n

    def body(*refs):
        buf = refs[:n]
        send_ref, recv_ref = refs[n:n + ns], refs[n + ns:n + 2 * ns]
        x, y, c, chips = _place()
        ids = [2 * cx + cy for cx, cy in chips]
        for a in range(n):
            half = buf[a].shape[1] // 2
            for j in range(3):
                sent = buf[a].at[ids[j], pl.ds(c * half, half)]
                landed = buf[a].at[ids[j], pl.ds((1 - c) * half, half)] if to_sibling else sent
                cp = _remote(sent, landed, send_ref[3 * a + j], recv_ref[3 * a + j], (x, y, c))
                cp.wait_send()
                cp.wait_recv()

    res = pl.pallas_call(
        body, name=name, in_specs=[HBM] * n + [SEM] * (2 * ns) + [pl.BlockSpec(memory_space=pl.ANY)], out_specs=[HBM] * n,
        out_shape=[pltpu.HBM(s.shape, s.dtype) for s in bufs], input_output_aliases={a: a for a in range(n)},
        compiler_params=pltpu.CompilerParams(has_side_effects=EFFECT),
    )(*bufs, *send, *recv, after)
    return list(res)


def _gather_forward(name, bufs):
    n = len(bufs)

    def body(*refs):
        buf = refs[n:2 * n]
        send, recv = refs[2 * n:]
        x, y, c, chips = _place()
        ids = [2 * cx + cy for cx, cy in chips]
        copies = []
        for a in range(n):
            half = buf[a].shape[1] // 2
            for j in range(3):
                landed = buf[a].at[ids[j], pl.ds(c * half, half)]
                cp = _remote(landed, landed, send.at[a, j], recv.at[a, j], (x, y, 1 - c))
                cp.start()
                copies.append(cp)
        for a in range(n):
            half = buf[a].shape[1] // 2
            for j in range(3):
                landed = buf[a].at[ids[j], pl.ds((1 - c) * half, half)]
                _remote(landed, landed, send.at[a, j], recv.at[a, j], (x, y, c)).wait_recv()
        for cp in copies:
            cp.wait_send()

    return pl.pallas_call(
        body, name=name, in_specs=[HBM] * n, out_specs=[HBM] * n,
        out_shape=[jax.ShapeDtypeStruct(s.shape, s.dtype) for s in bufs], input_output_aliases={a: a for a in range(n)},
        scratch_shapes=[pltpu.SemaphoreType.DMA((n, 3)), pltpu.SemaphoreType.DMA((n, 3))],
    )(*bufs)


def _pair_start(name, grads):
    n = len(grads)
    ns = N_CHIPS * n

    def body(*refs):
        send, recv = refs[2 * n:2 * n + ns], refs[2 * n + ns:2 * n + 2 * ns]
        src = refs[2 * n + 2 * ns:3 * n + 2 * ns]
        land = refs[3 * n + 2 * ns:4 * n + 2 * ns]
        token = refs[4 * n + 2 * ns]
        x, y, c, chips = _place()
        order = [2 * x + y] + [2 * cx + cy for cx, cy in chips]
        for a in range(n):
            half = src[a].shape[1] // 2
            for j in range(N_CHIPS):
                _remote(src[a].at[order[j], pl.ds((1 - c) * half, half)], land[a].at[j],
                        send[N_CHIPS * a + j], recv[N_CHIPS * a + j], (x, y, 1 - c)).start()
        token[...] = jnp.zeros_like(token)

    lands = [jax.ShapeDtypeStruct((N_CHIPS, g.shape[1] // 2, g.shape[2]), g.dtype) for g in grads]
    res = pl.pallas_call(
        body, name=name, in_specs=[HBM] * (2 * n),
        out_specs=[SEM] * (2 * ns) + [HBM] * (2 * n) + [pl.BlockSpec(memory_space=pltpu.VMEM)],
        out_shape=[pltpu.SemaphoreType.DMA(())] * (2 * ns) + [pltpu.HBM(g.shape, g.dtype) for g in grads]
        + [pltpu.HBM(l.shape, l.dtype) for l in lands] + [jax.ShapeDtypeStruct((8, 128), F32)],
        input_output_aliases={a: 2 * ns + a for a in range(2 * n)},
        compiler_params=pltpu.CompilerParams(has_side_effects=EFFECT),
    )(*[_hbm(g) for g in grads], *[_hbm(lax.empty(l.shape, l.dtype)) for l in lands])
    return list(res[:ns]), list(res[ns:2 * ns]), list(res[2 * ns:2 * ns + n]), list(res[2 * ns + n:2 * ns + 2 * n]), res[2 * ns + 2 * n]


def _pair_wait(name, send, recv, grads, lands, after):
    n = len(grads)
    ns = N_CHIPS * n

    def body(*refs):
        src, land = refs[:n], refs[n:2 * n]
        send_ref, recv_ref = refs[2 * n:2 * n + ns], refs[2 * n + ns:2 * n + 2 * ns]
        x, y, c, _ = _place()
        for a in range(n):
            for j in range(N_CHIPS):
                cp = _remote(land[a].at[j], land[a].at[j], send_ref[N_CHIPS * a + j], recv_ref[N_CHIPS * a + j], (x, y, c))
                cp.wait_send()
                cp.wait_recv()

    res = pl.pallas_call(
        body, name=name, in_specs=[HBM] * (2 * n) + [SEM] * (2 * ns) + [pl.BlockSpec(memory_space=pl.ANY)],
        out_specs=[HBM] * (2 * n), out_shape=[pltpu.HBM(g.shape, g.dtype) for g in grads] + [pltpu.HBM(l.shape, l.dtype) for l in lands],
        input_output_aliases={a: a for a in range(2 * n)},
        compiler_params=pltpu.CompilerParams(has_side_effects=EFFECT),
    )(*grads, *lands, *send, *recv, after)
    return list(res[:n]), list(res[n:])


def _chip_start(name, parts):
    n = len(parts)
    ns = 3 * n

    def body(*refs):
        send, recv = refs[2 * n:2 * n + ns], refs[2 * n + ns:2 * n + 2 * ns]
        src = refs[2 * n + 2 * ns:3 * n + 2 * ns]
        land = refs[3 * n + 2 * ns:4 * n + 2 * ns]
        token = refs[4 * n + 2 * ns]
        x, y, c, chips = _place()
        for a in range(n):
            for j, (cx, cy) in enumerate(chips):
                _remote(src[a].at[j], land[a].at[j], send[3 * a + j], recv[3 * a + j], (cx, cy, c)).start()
        token[...] = jnp.zeros_like(token)

    res = pl.pallas_call(
        body, name=name, in_specs=[HBM] * (2 * n),
        out_specs=[SEM] * (2 * ns) + [HBM] * (2 * n) + [pl.BlockSpec(memory_space=pltpu.VMEM)],
        out_shape=[pltpu.SemaphoreType.DMA(())] * (2 * ns) + [pltpu.HBM(p.shape, p.dtype) for p in parts] * 2
        + [jax.ShapeDtypeStruct((8, 128), F32)],
        input_output_aliases={a: 2 * ns + a for a in range(2 * n)},
        compiler_params=pltpu.CompilerParams(has_side_effects=EFFECT),
    )(*[_hbm(p) for p in parts], *[_hbm(lax.empty(p.shape, p.dtype)) for p in parts])
    return list(res[:ns]), list(res[ns:2 * ns]), list(res[2 * ns:2 * ns + n]), list(res[2 * ns + n:2 * ns + 2 * n]), res[2 * ns + 2 * n]


def _chip_wait(name, send, recv, parts, lands, after):
    n = len(parts)
    ns = 3 * n

    def body(*refs):
        src, land = refs[:n], refs[n:2 * n]
        send_ref, recv_ref = refs[2 * n:2 * n + ns], refs[2 * n + ns:2 * n + 2 * ns]
        x, y, c, _ = _place()
        for a in range(n):
            for j in range(3):
                cp = _remote(src[a].at[j], land[a].at[j], send_ref[3 * a + j], recv_ref[3 * a + j], (x, y, c))
                cp.wait_send()
                cp.wait_recv()

    res = pl.pallas_call(
        body, name=name, in_specs=[HBM] * (2 * n) + [SEM] * (2 * ns) + [pl.BlockSpec(memory_space=pl.ANY)],
        out_specs=[HBM] * (2 * n), out_shape=[pltpu.HBM(p.shape, p.dtype) for p in parts] * 2,
        input_output_aliases={a: a for a in range(2 * n)},
        compiler_params=pltpu.CompilerParams(has_side_effects=EFFECT),
    )(*parts, *lands, *send, *recv, after)
    return list(res[n:])


def _pair_share(name, shards):
    n = len(shards)

    def body(*refs):
        buf = refs[n:2 * n]
        send, recv = refs[2 * n:]
        x, y, c, _ = _place()
        copies = []
        for a in range(n):
            half = buf[a].shape[0] // 2
            mine = buf[a].at[pl.ds(c * half, half)]
            cp = _remote(mine, mine, send.at[a], recv.at[a], (x, y, 1 - c))
            cp.start()
            copies.append(cp)
        for a, cp in enumerate(copies):
            half = buf[a].shape[0] // 2
            cp.wait_send()
            theirs = buf[a].at[pl.ds((1 - c) * half, half)]
            _remote(theirs, theirs, send.at[a], recv.at[a], (x, y, c)).wait_recv()

    return pl.pallas_call(
        body, name=name, in_specs=[HBM] * n, out_specs=[HBM] * n,
        out_shape=[jax.ShapeDtypeStruct(s.shape, s.dtype) for s in shards], input_output_aliases={a: a for a in range(n)},
        scratch_shapes=[pltpu.SemaphoreType.DMA((n,)), pltpu.SemaphoreType.DMA((n,))],
    )(*shards)


def _small_start(buf):
    R, W = buf.shape
    ns = N_DEV - 1

    def body(*refs):
        send, recv = refs[2:2 + ns], refs[2 + ns:2 + 2 * ns]
        src, land, token = refs[2 + 2 * ns], refs[3 + 2 * ns], refs[4 + 2 * ns]
        x, y, c, _ = _place()
        me = 4 * x + 2 * y + c
        for k in range(1, N_DEV):
            peer = (x ^ (k >> 2), y ^ ((k >> 1) & 1), c ^ (k & 1))
            _remote(src, land.at[me], send[k - 1], recv[k - 1], peer).start()
        token[...] = jnp.zeros_like(token)

    res = pl.pallas_call(
        body, name="small_start", in_specs=[HBM, HBM],
        out_specs=[SEM] * (2 * ns) + [HBM, HBM, pl.BlockSpec(memory_space=pltpu.VMEM)],
        out_shape=[pltpu.SemaphoreType.DMA(())] * (2 * ns) + [pltpu.HBM((R, W), F32), pltpu.HBM((N_DEV, R, W), F32),
                                                                jax.ShapeDtypeStruct((8, 128), F32)],
        input_output_aliases={0: 2 * ns, 1: 2 * ns + 1},
        compiler_params=pltpu.CompilerParams(has_side_effects=EFFECT),
    )(_hbm(buf), _hbm(jnp.zeros((N_DEV, R, W), F32)))
    return list(res[:ns]), list(res[ns:2 * ns]), res[2 * ns], res[2 * ns + 1], res[2 * ns + 2]


def _small_wait(send, recv, buf, land, after):
    ns = N_DEV - 1

    def body(*refs):
        land_ref = refs[1]
        send_ref, recv_ref = refs[2:2 + ns], refs[2 + ns:2 + 2 * ns]
        x, y, c, _ = _place()
        me = 4 * x + 2 * y + c
        for k in range(1, N_DEV):
            landed = land_ref.at[me ^ k]
            cp = _remote(landed, landed, send_ref[k - 1], recv_ref[k - 1], (x, y, c))
            cp.wait_send()
            cp.wait_recv()

    return pl.pallas_call(
        body, name="small_wait", in_specs=[HBM, HBM] + [SEM] * (2 * ns) + [pl.BlockSpec(memory_space=pl.ANY)],
        out_specs=[HBM, HBM], out_shape=[pltpu.HBM(buf.shape, buf.dtype), pltpu.HBM(land.shape, land.dtype)],
        input_output_aliases={0: 0, 1: 1}, compiler_params=pltpu.CompilerParams(has_side_effects=EFFECT),
    )(buf, land, *send, *recv, after)


def _small_sum(buf, land):
    def body(buf_ref, land_ref, out_ref):
        x, y, c, _ = _place()
        me = 4 * x + 2 * y + c
        total = None
        for d in range(N_DEV):
            term = jnp.where(me == d, buf_ref[...], land_ref[d])
            total = term if total is None else total + term
        out_ref[...] = total

    return pl.pallas_call(body, name="small_sum", out_shape=jax.ShapeDtypeStruct(buf.shape, F32))(buf, land)


def _pair_sum_bf16(name, grad, theirs, ids):
    _, R2, C = theirs.shape
    tr = _tile(R2, 256, 16)
    nrb = R2 // tr

    def body(ids_ref, a_ref, b_ref, o_ref):
        o_ref[...] = (a_ref[...] + b_ref[...]).astype(BF16)

    return pl.pallas_call(
        body, name=name,
        grid_spec=pltpu.PrefetchScalarGridSpec(
            num_scalar_prefetch=1, grid=(3, nrb),
            in_specs=[pl.BlockSpec((None, tr, C), lambda j, i, ids: (ids[3 + j], ids[1] * nrb + i, 0)),
                      pl.BlockSpec((None, tr, C), lambda j, i, ids: (j + 1, i, 0))],
            out_specs=pl.BlockSpec((None, tr, C), lambda j, i, ids: (j, i, 0))),
        out_shape=jax.ShapeDtypeStruct((3, R2, C), BF16), compiler_params=_params(("parallel", "parallel")),
    )(ids, grad, theirs)


def _chip_sum(name, grad, theirs, arrived, ids):
    _, R2, C = theirs.shape
    tr = _tile(R2, 256, 16)
    nrb = R2 // tr

    def body(ids_ref, a_ref, b_ref, r_ref, o_ref):
        tot = a_ref[...] + b_ref[...]
        for j in range(3):
            tot = tot + r_ref[j].astype(F32)
        o_ref[...] = tot

    return pl.pallas_call(
        body, name=name,
        grid_spec=pltpu.PrefetchScalarGridSpec(
            num_scalar_prefetch=1, grid=(nrb,),
            in_specs=[pl.BlockSpec((None, tr, C), lambda i, ids: (ids[0], ids[1] * nrb + i, 0)),
                      pl.BlockSpec((None, tr, C), lambda i, ids: (0, i, 0)),
                      pl.BlockSpec((3, tr, C), lambda i, ids: (0, i, 0))],
            out_specs=pl.BlockSpec((tr, C), lambda i, ids: (ids[1] * nrb + i, 0))),
        out_shape=jax.ShapeDtypeStruct((2 * R2, C), F32), compiler_params=_params(("parallel",)),
    )(ids, grad, theirs, arrived)


def _adamw(name, w, g, m, v, emit_grad=False):
    R, C = w.shape
    tr = _tile(R, 128, 8)
    c1 = 1.0 / (1.0 - ADAM_B1 ** ADAM_STEP)
    c2 = 1.0 / (1.0 - ADAM_B2 ** ADAM_STEP)
    n_out = 4 if emit_grad else 3

    def body(w_ref, g_ref, m_ref, v_ref, d_ref, mo_ref, vo_ref, *rest):
        gv = g_ref[...]
        mn = ADAM_B1 * m_ref[...] + (1.0 - ADAM_B1) * gv
        vn = ADAM_B2 * v_ref[...] + (1.0 - ADAM_B2) * (gv * gv)
        d_ref[...] = -ADAM_LR * ((mn * c1) / (jnp.sqrt(vn * c2) + ADAM_EPS) + ADAM_WD * w_ref[...])
        mo_ref[...] = mn
        vo_ref[...] = vn
        if emit_grad:
            rest[0][...] = gv

    spec = pl.BlockSpec((tr, C), lambda i: (i, 0))
    sds = jax.ShapeDtypeStruct((R, C), F32)
    return pl.pallas_call(body, name=name, grid=(R // tr,), in_specs=[spec] * 4, out_specs=[spec] * n_out, out_shape=[sds] * n_out,
                          compiler_params=_params(("parallel",)))(w, g, m, v)


SMALL = ["ffn1_norm", "mix_norm", "mem_norm", "forget_bias", "fox_q_gain", "fox_k_gain", "swa_q_gain", "swa_k_gain", "swa_sinks",
         "mem_q_gain", "mem_k_gain", "ffn2_norm"]
LARGE = ["ffn1_gate", "ffn1_up", "ffn1_down", "w_in", "w_mem_k", "w_mem_v", "w_out", "ffn2_gate", "ffn2_up", "ffn2_down"]
GATHER_GROUPS = [["ffn1_gate", "ffn1_up"], ["ffn1_down", "w_in", "w_mem_k", "w_mem_v"], ["w_out", "ffn2_gate", "ffn2_up", "ffn2_down"]]
WEIGHTS = ["ffn1_norm", "ffn1_gate", "ffn1_up", "ffn1_down", "mix_norm", "mem_norm", "w_in", "forget_bias", "w_mem_k", "w_mem_v",
           "fox_q_gain", "fox_k_gain", "swa_q_gain", "swa_k_gain", "swa_sinks", "mem_q_gain", "mem_k_gain", "w_out", "ffn2_norm",
           "ffn2_gate", "ffn2_up", "ffn2_down"]


def _pad_proj_cols(w):
    out = jnp.zeros((w.shape[0], PROJ_W), w.dtype)
    for start, width, pstart in REF_GROUPS:
        out = lax.dynamic_update_slice(out, w[:, start:start + width], (0, pstart))
    return out


def _unpad_proj_cols(w):
    return jnp.concatenate([w[:, pstart:pstart + width] for _, width, pstart in REF_GROUPS], axis=1)


def _pack_small(vals):
    flat = jnp.concatenate([vals[k].reshape(-1).astype(F32) for k in SMALL + ["loss"]])
    n = flat.shape[0]
    total = -(-n // 1024) * 1024
    return jnp.pad(flat, (0, total - n)).reshape(total // 128, 128)


def _unpack_small(buf, shapes):
    flat = buf.reshape(-1)
    out, off = {}, 0
    for k in SMALL + ["loss"]:
        size = int(np.prod(shapes[k]))
        out[k] = flat[off:off + size].reshape(shapes[k])
        off += size
    return out


def kernel(x, mem, ffn1_norm, ffn1_gate, ffn1_up, ffn1_down, mix_norm, mem_norm, w_in, forget_bias, w_mem_k, w_mem_v, fox_q_gain, fox_k_gain, swa_q_gain, swa_k_gain, swa_sinks, mem_q_gain, mem_k_gain, w_out, ffn2_norm, ffn2_gate, ffn2_up, ffn2_down, loss_target, m_ffn1_norm, m_ffn1_gate, m_ffn1_up, m_ffn1_down, m_mix_norm, m_mem_norm, m_w_in, m_forget_bias, m_w_mem_k, m_w_mem_v, m_fox_q_gain, m_fox_k_gain, m_swa_q_gain, m_swa_k_gain, m_swa_sinks, m_mem_q_gain, m_mem_k_gain, m_w_out, m_ffn2_norm, m_ffn2_gate, m_ffn2_up, m_ffn2_down, v_ffn1_norm, v_ffn1_gate, v_ffn1_up, v_ffn1_down, v_mix_norm, v_mem_norm, v_w_in, v_forget_bias, v_w_mem_k, v_w_mem_v, v_fox_q_gain, v_fox_k_gain, v_swa_q_gain, v_swa_k_gain, v_swa_sinks, v_mem_q_gain, v_mem_k_gain, v_w_out, v_ffn2_norm, v_ffn2_gate, v_ffn2_up, v_ffn2_down):
    given = dict(locals())
    T, D = x.shape[1], x.shape[2]
    ML = mem.shape[1]
    xin = x.reshape(T, D)
    target = loss_target.reshape(T, D)
    memin = mem.reshape(ML, D)

    ids = _place_ids()
    shard = {k: given[k][0] for k in LARGE}
    started, after = [], ids
    for gi, group in enumerate(GATHER_GROUPS):
        also = {}
        if "w_in" in group:
            tied = lax.optimization_barrier((given["w_in"], given["m_w_in"], given["v_w_in"], after))
            w_in_rows = tuple(t[0] for t in tied[:3])
            shard["w_in"] = _pad_proj_cols(w_in_rows[0])
            also["w_in"] = w_in_rows[1:]
        placed = [_cast_place("cast_" + k, shard[k], ids, after, *also.get(k, ())) for k in group]
        send, recv, bufs, after = _gather_start("gather_start_%d" % gi, placed, after)
        started.append((send, recv, bufs))

    def arrive(gi, done):
        send, recv, bufs = started[gi]
        bufs = _gather_wait("gather_wait_%d" % gi, send, recv, bufs, done)
        return dict(zip(GATHER_GROUPS[gi], _gather_forward("gather_forward_%d" % gi, bufs)))

    gains = jnp.concatenate([fox_q_gain, fox_k_gain, swa_q_gain, swa_k_gain, mem_q_gain,
                             jnp.pad(forget_bias, ((0, 0), (0, HEAD_DIM - FOX_HEADS))), jnp.zeros((2, HEAD_DIM), F32)], axis=0)
    slopes_np = 2.0 ** (-8.0 * np.arange(1, SWA_HEADS + 1) / SWA_HEADS)
    slopes = jnp.asarray(np.repeat(slopes_np, WINDOW).reshape(SWA_KV_HEADS, GR, 1), F32)
    sinks = jnp.repeat(swa_sinks.reshape(SWA_HEADS), WINDOW).reshape(SWA_KV_HEADS, GR, 1)

    h1 = _rms_fwd("ffn1_norm_fwd", xin, ffn1_norm + after[0, 0])
    full = arrive(0, h1)
    wg1, wu1 = full["ffn1_gate"], full["ffn1_up"]
    fg1, fu1, a1 = _ffn_gu("ffn1_gate_up", h1, wg1, wu1)
    full = arrive(1, a1)
    wd1 = full["ffn1_down"].reshape(-1, D)
    win = full["w_in"].reshape(D, PROJ_W)
    wmk = full["w_mem_k"].reshape(D, MEM_HEADS * HEAD_DIM)
    wmv = full["w_mem_v"].reshape(D, MEM_HEADS * HEAD_DIM)
    x1, h2 = _residual_norm("ffn1_down", a1, wd1, xin, 0.5, mix_norm, 256)
    proj = _mm2d("proj_in", h2, win, "nn", F32, tn=1408, tk=2048, n_outer=True)
    qkv, logf, k_t, v_t = _prep_fwd(proj, gains)
    cum = _cumsum_rows("forget_cumsum", [logf], False)
    cum_h = cum[:, :FOX_HEADS].T
    cq_row = cum_h.reshape(FOX_HEADS, 1, T)
    ck_rep = jnp.broadcast_to(cum_h[:, :, None], (FOX_HEADS, T, HEAD_DIM))
    mn = _rms_fwd("mem_norm_fwd", memin, mem_norm)
    mk_raw = _mm2d("mem_k_proj", mn, wmk, "nn", F32)
    mv = _mm2d("mem_v_proj", mn, wmv, "nn", BF16)
    mk = _head_norm_rows(mk_raw, mem_k_gain)
    out_a, out_a_f32, lse_a = _fox_fwd(qkv, v_t, cq_row, ck_rep)
    send, recv, bufs = started[2]
    bufs = _gather_wait("gather_wait_2", send, recv, bufs, out_a)
    send, recv, bufs, token = _gather_start("gather_pass_start_2", bufs, out_a, to_sibling=True)
    out_b, lse_b = _swa_fwd(qkv, slopes + token[0, 0], sinks)
    out_c, lse_c = _mem_fwd(qkv, mk, mv)
    mixed = jnp.concatenate([out_a, out_b, out_c], axis=1)
    full = dict(zip(GATHER_GROUPS[2], _gather_wait("gather_pass_wait_2", send, recv, bufs, mixed, to_sibling=True)))
    wo = full["w_out"].reshape(-1, D)
    wg2, wu2, wd2 = full["ffn2_gate"], full["ffn2_up"], full["ffn2_down"].reshape(-1, D)
    x2, h3 = _residual_norm("mix_out", mixed, wo, x1, 1.0, ffn2_norm, 512)
    fg2, fu2, a2 = _ffn_gu("ffn2_gate_up", h3, wg2, wu2)
    dx3, dyb3, loss_blocks = _ffn_down_loss("ffn2_down", a2, wd2, x2, target)

    grads, small, res = {}, {"loss": jnp.sum(loss_blocks[::8, 0])}, {}

    def pair_off(tag, group):
        send, recv, own, lands, token = _pair_start("grad_pair_start_" + tag, [grads[k] for k in group])
        return (group, send, recv, own, lands), token

    def chip_off(tag, started, done):
        group, send, recv, own, lands = started
        own, theirs = _pair_wait("grad_pair_wait_" + tag, send, recv, own, lands, done)
        grads.update(zip(group, own))
        to_chips = [_pair_sum_bf16("pair_sum_" + k, grads[k], b, ids) for k, b in zip(group, theirs)]
        send, recv, parts, lands, token = _chip_start("grad_chip_start_" + tag, to_chips)
        return (group, theirs, send, recv, parts, lands), token

    def finish(tag, state, done):
        group, theirs, send, recv, parts, lands = state
        arrived = _chip_wait("grad_chip_wait_" + tag, send, recv, parts, lands, done)
        halves = [_chip_sum("chip_sum_" + k, grads[k], b, r, ids) for k, b, r in zip(group, theirs, arrived)]
        reduced = dict(zip(group, _pair_share("grad_pair_share_" + tag, halves)))
        last = None
        for k in group:
            if k == "w_in":
                gk = _unpad_proj_cols(reduced[k])
                d, mo, vo = _adamw("adamw_" + k, w_in_rows[0], gk, w_in_rows[1], w_in_rows[2])
            else:
                d, mo, vo, gk = _adamw("adamw_" + k, given[k][0], reduced[k], given["m_" + k][0], given["v_" + k][0], emit_grad=True)
            res[k] = tuple(t[None] for t in (gk, d, mo, vo))
            last = vo
        return last

    dg2, du2 = _ffn_bwd_act("ffn2", dyb3, wd2, fg2, fu2, N_CHIPS)
    grads["ffn2_down"] = _ffn_bwd_down("ffn2", a2, dyb3, N_CHIPS).reshape(N_CHIPS, -1, D)
    grads["ffn2_gate"], grads["ffn2_up"] = _ffn_bwd_gate_up("ffn2", h3, dg2, du2, N_CHIPS)
    started, token = pair_off("a", ["ffn2_gate", "ffn2_up", "ffn2_down"])
    dh3 = _ffn_bwd_x("ffn2", dg2, du2, wg2, wu2, token)
    state_a, token = chip_off("a", started, dh3)
    dx2, dx2b, small["ffn2_norm"] = _rms_bwd("ffn2_norm_bwd", dh3, x2, ffn2_norm + token[0, 0], dx3, 1.0)
    dmix = _mm2d("mix_out_dx", dx2b, wo, "nt", BF16, tk=2048, n_outer=True)
    grads["w_out"] = _mm2d("mix_out_dw", mixed, dx2b, "tn", F32, tk=T, n_outer=True, resident=True).reshape(N_CHIPS, -1, D)
    delta_row = _fox_delta(dmix, out_a_f32)[:, :FOX_HEADS].T.reshape(FOX_HEADS, 1, T)
    dfq, dfk, dfv, dck, dcq = _fox_bwd(qkv, k_t, cq_row, ck_rep, delta_row, lse_a, dmix)
    dsq, dsk, dsv, dsink = _swa_bwd(qkv, slopes, sinks, out_b, lse_b, dmix)
    dmq, dmk, dmv = _mem_bwd(qkv, mk, mv, out_c, lse_c, dmix)
    small["swa_sinks"] = dsink[:, :SWA_GROUP, 0].reshape(1, SWA_HEADS)
    dcum = jnp.pad(dcq.reshape(FOX_HEADS, T).T, ((0, 0), (0, HEAD_DIM - FOX_HEADS)))
    dlogf = _cumsum_rows("forget_cumsum_bwd", [dcum], True, columns=dck)
    dproj, dgains = _prep_bwd(proj, gains, dfq, dfk, dfv, dsq, dsk, dsv, dmq, dlogf)
    for row, k in enumerate(["fox_q_gain", "fox_k_gain", "swa_q_gain", "swa_k_gain", "mem_q_gain"]):
        small[k] = dgains[row:row + 1, :]
    small["forget_bias"] = dgains[5:6, :FOX_HEADS]
    grads["w_in"] = _mm2d("proj_in_dw", h2, dproj, "tn", F32, tn=1408, tk=T, n_outer=True, resident=True).reshape(N_CHIPS, -1, PROJ_W)
    dmk_raw, small["mem_k_gain"] = _head_norm_rows_bwd(mk_raw, mem_k_gain, dmk)
    dmvb = dmv.astype(BF16)
    grads["w_mem_k"] = _mm2d("mem_k_dw", mn, dmk_raw, "tn", F32).reshape(N_CHIPS, -1, MEM_HEADS * HEAD_DIM)
    grads["w_mem_v"] = _mm2d("mem_v_dw", mn, dmvb, "tn", F32).reshape(N_CHIPS, -1, MEM_HEADS * HEAD_DIM)
    dmn = _mm2d("mem_k_dx", dmk_raw, wmk, "nt", F32)
    dmn = _mm2d("mem_v_dx", dmvb, wmv, "nt", F32, extras=[dmn], epilogue=lambda accs, ex: [ex[0] + accs[0]])
    _, _, small["mem_norm"] = _rms_bwd("mem_norm_bwd", dmn, memin, mem_norm, jnp.zeros_like(memin), 1.0)
    started, token = pair_off("b", ["w_out", "w_in", "w_mem_k", "w_mem_v"])
    dh2 = _mm2d("proj_in_dx", dproj, win, "nt", F32, tm=512, tn=1024, tk=PROJ_W, n_outer=True, resident=True, after=token)
    state_b, token = chip_off("b", started, dh2)
    dx1, dyb1, small["mix_norm"] = _rms_bwd("mix_norm_bwd", dh2, x1, mix_norm + token[0, 0], dx2, 0.5)
    grads["ffn1_down"] = _ffn_bwd_down("ffn1", a1, dyb1, N_CHIPS).reshape(N_CHIPS, -1, D)
    started, token = pair_off("c", ["ffn1_down"])
    dg1, du1 = _ffn_bwd_act("ffn1", dyb1, wd1, fg1, fu1, N_CHIPS, after=token)
    state_c, token = chip_off("c", started, dg1)
    grads["ffn1_gate"], grads["ffn1_up"] = _ffn_bwd_gate_up("ffn1", h1, dg1, du1, N_CHIPS, after=token)
    started, token = pair_off("d", ["ffn1_gate", "ffn1_up"])
    dh1 = _ffn_bwd_x("ffn1", dg1, du1, wg1, wu1, token)
    state_d, token = chip_off("d", started, dh1)
    grad_x, _, small["ffn1_norm"] = _rms_bwd("ffn1_norm_bwd", dh1, xin, ffn1_norm + token[0, 0], dx1, 1.0)

    s_send, s_recv, s_buf, s_land, token = _small_start(_pack_small(small))

    done = finish("a", state_a, token)
    done = finish("b", state_b, done)
    done = finish("c", state_c, done)
    done = finish("d", state_d, done)

    shapes = {k: given[k].shape for k in SMALL}
    shapes["loss"] = ()
    s_buf, s_land = _small_wait(s_send, s_recv, s_buf, s_land, done)
    red_small = _unpack_small(_small_sum(s_buf, s_land), shapes)
    loss = red_small["loss"]
    zero = {"loss": jnp.zeros((), F32)}
    packed = [_pack_small({**zero, **{k: src[k] for k in SMALL}}) for src in (
        {k: given[k] for k in SMALL}, red_small, {k: given["m_" + k] for k in SMALL}, {k: given["v_" + k] for k in SMALL})]
    d_s, m_s, v_s = (_unpack_small(t, shapes) for t in _adamw("adamw_small", *packed))
    for k in SMALL:
        res[k] = (red_small[k], d_s[k], m_s[k], v_s[k])

    outs = [loss, grad_x.reshape(1, T, D)]
    for part in range(4):
        outs += [res[k][part] for k in WEIGHTS]
    return tuple(outs)
```

```python
import functools

import numpy as np
import jax
import jax.numpy as jnp
from jax import lax
from jax.experimental import pallas as pl
from jax.experimental.pallas import tpu as pltpu

F32 = jnp.float32
BF16 = jnp.bfloat16
MESH = pl.DeviceIdType.MESH

HEAD_DIM = 128
FOX_HEADS = 6
SWA_HEADS = 6
SWA_KV_HEADS = 2
SWA_GROUP = SWA_HEADS // SWA_KV_HEADS
MEM_HEADS = 4
WINDOW = 128
EPS = 1e-6
NEG_INF = -1e30
SCALE = HEAD_DIM ** -0.5

C_FQ = 0
C_FK = C_FQ + FOX_HEADS * HEAD_DIM
C_FV = C_FK + FOX_HEADS * HEAD_DIM
C_SQ = C_FV + FOX_HEADS * HEAD_DIM
C_SK = C_SQ + SWA_HEADS * HEAD_DIM
C_SV = C_SK + SWA_KV_HEADS * HEAD_DIM
C_MQ = C_SV + SWA_KV_HEADS * HEAD_DIM
C_FL = C_MQ + MEM_HEADS * HEAD_DIM
PROJ_W = C_FL + HEAD_DIM
FOX_W = FOX_HEADS * HEAD_DIM
REF_GROUPS = [
    (0, FOX_W, C_FQ), (FOX_W, FOX_W, C_FK), (2 * FOX_W, FOX_W, C_FV), (3 * FOX_W, FOX_HEADS, C_FL),
    (3 * FOX_W + FOX_HEADS, SWA_HEADS * HEAD_DIM, C_SQ),
    (3 * FOX_W + FOX_HEADS + SWA_HEADS * HEAD_DIM, SWA_KV_HEADS * HEAD_DIM, C_SK),
    (3 * FOX_W + FOX_HEADS + (SWA_HEADS + SWA_KV_HEADS) * HEAD_DIM, SWA_KV_HEADS * HEAD_DIM, C_SV),
    (3 * FOX_W + FOX_HEADS + (SWA_HEADS + 2 * SWA_KV_HEADS) * HEAD_DIM, MEM_HEADS * HEAD_DIM, C_MQ),
]

ADAM_LR = 0.001
ADAM_B1 = 0.9
ADAM_B2 = 0.999
ADAM_EPS = 1e-08
ADAM_WD = 0.01
ADAM_STEP = 10

V7X_VMEM_LIMIT = 56 * 1024 * 1024
N_CHIPS = 4
N_DEV = 8


def _tile(n, pref, mult=128):
    t = (min(pref, n) // mult) * mult
    while t >= mult:
        if n % t == 0:
            return t
        t -= mult
    return n


def _params(sem):
    return pltpu.CompilerParams(dimension_semantics=sem, vmem_limit_bytes=V7X_VMEM_LIMIT)


_DIMS = {"nn": (((1,), (0,)), ((), ())), "nt": (((1,), (1,)), ((), ())), "tn": (((0,), (0,)), ((), ()))}


def _dot(a, b, mode):
    return lax.dot_general(a, b, _DIMS[mode], preferred_element_type=F32)


def _mm(name, grid, pairs, acc_of, acc_shapes, extras, outs, epilogue, after=None):
    n_p, n_e, n_o, n_a = len(pairs), len(extras), len(outs), len(acc_shapes)
    n_w = 0 if after is None else 1
    nk = grid[2]
    n_in = sum(1 if a is None else 2 for a, *_ in pairs)

    def body(*refs):
        ex = refs[n_in:n_in + n_e]
        out = refs[n_in + n_e + n_w:n_in + n_e + n_w + n_o]
        accs = refs[n_in + n_e + n_w + n_o:]
        parts = [None] * n_a
        at = 0
        for p in range(n_p):
            if pairs[p][0] is None:
                a_ref, b_ref = refs[0], refs[at]
                at += 1
            else:
                a_ref, b_ref = refs[at], refs[at + 1]
                at += 2
            d = _dot(a_ref[...], b_ref[...], pairs[p][4])
            parts[acc_of[p]] = d if parts[acc_of[p]] is None else parts[acc_of[p]] + d

        def finish(vals):
            for o, r in zip(out, epilogue(vals, [e[...] for e in ex])):
                o[...] = r.astype(o.dtype)

        if nk == 1:
            finish(parts)
            return
        k = pl.program_id(2)

        @pl.when(k == 0)
        def _():
            for a, d in zip(accs, parts):
                a[...] = d

        @pl.when((k > 0) & (k < nk - 1))
        def _():
            for a, d in zip(accs, parts):
                a[...] += d

        @pl.when(k == nk - 1)
        def _():
            finish([a[...] + d for a, d in zip(accs, parts)])

    in_specs, args = [], []
    for a, a_spec, b, b_spec, _ in pairs:
        if a is not None:
            in_specs.append(a_spec)
            args.append(a)
        in_specs.append(b_spec)
        args.append(b)
    for e, e_spec in extras:
        in_specs.append(e_spec)
        args.append(e)
    if after is not None:
        in_specs.append(pl.BlockSpec(memory_space=pl.ANY))
        args.append(after)
    res = pl.pallas_call(
        body, name=name, grid=grid, in_specs=in_specs,
        out_specs=[s for _, s in outs], out_shape=[o for o, _ in outs],
        scratch_shapes=[pltpu.VMEM(s, F32) for s in acc_shapes] if nk > 1 else [],
        compiler_params=_params(("parallel", "parallel", "arbitrary")),
    )(*args)
    return res


def _mm2d(name, a, b, mode, out_dtype, tm=512, tn=1024, tk=1024, extras=(), epilogue=None, n_out=1, after=None, n_outer=False,
          resident=False, rows=(), tile_stat=False):
    if mode == "nn":
        (M, K), N = a.shape, b.shape[1]
    elif mode == "nt":
        (M, K), N = a.shape, b.shape[0]
    else:
        (K, M), N = a.shape, b.shape[1]
    tm, tn, tk = _tile(M, tm), _tile(N, tn), _tile(K, tk)
    assert not resident or tk == K

    def spec(shape, index, single=False):
        mode_kw = {"pipeline_mode": pl.Buffered(1)} if single else {}
        if n_outer:
            return pl.BlockSpec(shape, lambda j, i, k: index(i, j, k), **mode_kw)
        return pl.BlockSpec(shape, index, **mode_kw)

    single_a, single_b = resident and not n_outer, resident and n_outer
    a_spec = spec((tk, tm), lambda i, j, k: (k, i), single_a) if mode == "tn" else spec((tm, tk), lambda i, j, k: (i, k), single_a)
    b_spec = spec((tn, tk), lambda i, j, k: (j, k), single_b) if mode == "nt" else spec((tk, tn), lambda i, j, k: (k, j), single_b)
    mn = spec((tm, tn), lambda i, j, k: (i, j))
    if epilogue is None:
        epilogue = lambda accs, ex: [accs[0]]
    if not isinstance(out_dtype, (list, tuple)):
        out_dtype = [out_dtype] * n_out
    grid = (N // tn, M // tm, K // tk) if n_outer else (M // tm, N // tn, K // tk)
    outs = [(jax.ShapeDtypeStruct((M, N), d), mn) for d in out_dtype]
    if tile_stat:
        assert tn == N
        outs.append((jax.ShapeDtypeStruct((8 * (M // tm), 128), F32), spec((8, 128), lambda i, j, k: (i, 0))))
    res = _mm(name, grid, [(a, a_spec, b, b_spec, mode)], [0], [(tm, tn)],
              [(e, mn) for e in extras] + [(r, spec((1, tn), lambda i, j, k: (0, j))) for r in rows], outs, epilogue, after=after)
    return res[0] if len(res) == 1 else res


def _sigmoid(x):
    return 1.0 / (1.0 + jnp.exp(-x))


def _sigmoid_fast(x):
    return pl.reciprocal(1.0 + jnp.exp(-x), approx=True)


def _ffn_gu(name, h, wg, wu):
    T, D = h.shape
    nf, _, F4 = wg.shape
    tm, tk = _tile(T, 512), _tile(D, 2048)
    a_spec = pl.BlockSpec((tm, tk), lambda j, i, k: (i, k))
    b_spec = pl.BlockSpec((None, tk, F4), lambda j, i, k: (j, k, 0))
    o_spec = pl.BlockSpec((tm, F4), lambda j, i, k: (i, j))

    def epilogue(accs, ex):
        g, u = accs
        s = _sigmoid_fast(g)
        gs = g * s
        return [(s + s * (g - gs)) * u, gs, gs * u]

    sds = jax.ShapeDtypeStruct((T, nf * F4), BF16)
    return _mm(name, (nf, T // tm, D // tk), [(h, a_spec, wg, b_spec, "nn"), (None, None, wu, b_spec, "nn")], [0, 1],
               [(tm, F4), (tm, F4)], [], [(sds, o_spec)] * 3, epilogue)


def _rms_rows(x, gain):
    return x * lax.rsqrt(jnp.mean(x * x, axis=-1, keepdims=True) + EPS) * gain


def _residual_norm(name, a, w, xres, scale, gain, tm):
    def epilogue(accs, ex):
        y = ex[0] + scale * accs[0]
        return [y, _rms_rows(y, ex[1])]

    return _mm2d(name, a, w, "nn", [F32, BF16], tm=tm, tn=w.shape[1], tk=w.shape[0], n_outer=True, resident=True, extras=[xres],
                 rows=[gain], epilogue=epilogue)


def _ffn_down_loss(name, a, wd, xres, target):
    D = wd.shape[1]

    def epilogue(accs, ex):
        e = ex[0] + 0.5 * accs[0] - ex[1]
        d = e * (1.0 / D)
        return [d, 0.5 * d, jnp.zeros((8, 128), F32) + (0.5 / D) * jnp.sum(e * e)]

    return _mm2d(name, a, wd, "nn", [F32, BF16], tm=256, tn=D, tk=wd.shape[0], n_outer=True, resident=True, extras=[xres, target],
                 tile_stat=True, epilogue=epilogue)


def _ffn_bwd_down(tag, a, dyb, nf, after=None):
    return _mm2d(tag + "_dwd", a, dyb, "tn", F32, tm=a.shape[1] // nf, tn=1024, tk=a.shape[0], resident=True, after=after)


def _ffn_bwd_act(tag, dyb, wd, da_dg, da_du, nf, after=None):
    def act_bwd(accs, ex):
        return [accs[0] * ex[0].astype(F32), accs[0] * ex[1].astype(F32)]

    return _mm2d(tag + "_da", dyb, wd, "nt", BF16, tn=wd.shape[0] // nf, tk=2048, extras=[da_dg, da_du], epilogue=act_bwd, n_out=2,
                 n_outer=True, after=after)


def _ffn_bwd_gate_up(tag, h, dg, du, nf, after=None):
    T, D = h.shape
    F4 = dg.shape[1] // nf
    tm = _tile(D, 512)
    h_spec = pl.BlockSpec((T, tm), lambda j, i, k: (0, i))
    d_spec = pl.BlockSpec((T, F4), lambda j, i, k: (0, j), pipeline_mode=pl.Buffered(1))
    w_spec = pl.BlockSpec((None, tm, F4), lambda j, i, k: (j, i, 0))
    sds = jax.ShapeDtypeStruct((nf, D, F4), F32)
    return _mm(tag + "_dwgu", (nf, D // tm, 1), [(h, h_spec, dg, d_spec, "tn"), (None, None, du, d_spec, "tn")],
               [0, 1], [(tm, F4), (tm, F4)], [], [(sds, w_spec)] * 2, lambda accs, ex: accs, after=after)


def _ffn_bwd_x(tag, dg, du, wg, wu, after):
    T = dg.shape[0]
    nf, D, F4 = wg.shape
    tm, tn = _tile(T, 256), _tile(D, 1024)

    def body(dg_ref, du_ref, wg_ref, wu_ref, after_ref, o_ref):
        acc = None
        for j in range(nf):
            cols = slice(j * F4, (j + 1) * F4)
            part = _dot(dg_ref[:, cols], wg_ref[j], "nt") + _dot(du_ref[:, cols], wu_ref[j], "nt")
            acc = part if acc is None else acc + part
        o_ref[...] = acc

    a_spec = pl.BlockSpec((tm, nf * F4), lambda n, i: (i, 0))
    b_spec = pl.BlockSpec((nf, tn, F4), lambda n, i: (0, n, 0), pipeline_mode=pl.Buffered(1))
    return pl.pallas_call(
        body, name=tag + "_dh", grid=(D // tn, T // tm),
        in_specs=[a_spec, a_spec, b_spec, b_spec, pl.BlockSpec(memory_space=pl.ANY)],
        out_specs=pl.BlockSpec((tm, tn), lambda n, i: (i, n)), out_shape=jax.ShapeDtypeStruct((T, D), F32),
        compiler_params=_params(("parallel", "parallel")),
    )(dg, du, wg, wu, after)


def _rms_fwd(name, x, gain):
    R, D = x.shape
    tr = _tile(R, 256, 8)

    def body(x_ref, g_ref, o_ref):
        xv = x_ref[...]
        r = lax.rsqrt(jnp.mean(xv * xv, axis=-1, keepdims=True) + EPS)
        o_ref[...] = (xv * r * g_ref[...]).astype(BF16)

    return pl.pallas_call(
        body, name=name, grid=(R // tr,),
        in_specs=[pl.BlockSpec((tr, D), lambda i: (i, 0)), pl.BlockSpec((1, D), lambda i: (0, 0))],
        out_specs=pl.BlockSpec((tr, D), lambda i: (i, 0)), out_shape=jax.ShapeDtypeStruct((R, D), BF16),
        compiler_params=_params(("parallel",)),
    )(x, gain)


def _rms_bwd(name, dh, x, gain, dres, bscale):
    R, D = x.shape
    tr = _tile(R, 256, 8)

    def body(dh_ref, x_ref, g_ref, dres_ref, dx_ref, dxb_ref, dg_ref):
        xv, dy = x_ref[...], dh_ref[...]
        r = lax.rsqrt(jnp.mean(xv * xv, axis=-1, keepdims=True) + EPS)
        xn = xv * r
        uu = dy * g_ref[...]
        dx = dres_ref[...] + r * (uu - xn * jnp.mean(xn * uu, axis=-1, keepdims=True))
        dx_ref[...] = dx
        dxb_ref[...] = (bscale * dx).astype(BF16)
        part = jnp.sum(dy * xn, axis=0, keepdims=True)

        @pl.when(pl.program_id(0) == 0)
        def _():
            dg_ref[...] = part

        @pl.when(pl.program_id(0) > 0)
        def _():
            dg_ref[...] += part

    row = pl.BlockSpec((tr, D), lambda i: (i, 0))
    vec = pl.BlockSpec((1, D), lambda i: (0, 0))
    return pl.pallas_call(
        body, name=name, grid=(R // tr,), in_specs=[row, row, vec, row], out_specs=[row, row, vec],
        out_shape=[jax.ShapeDtypeStruct((R, D), F32), jax.ShapeDtypeStruct((R, D), BF16), jax.ShapeDtypeStruct((1, D), F32)],
        compiler_params=_params(("arbitrary",)),
    )(dh, x, gain, dres)


def _head_norm(xs, g):
    r = lax.rsqrt(jnp.mean(xs * xs, axis=-1, keepdims=True) + EPS)
    return xs * r * g


def _head_norm_bwd(xs, g, dy):
    r = lax.rsqrt(jnp.mean(xs * xs, axis=-1, keepdims=True) + EPS)
    xn = xs * r
    uu = dy * g
    return r * (uu - xn * jnp.mean(xn * uu, axis=-1, keepdims=True)), jnp.sum(dy * xn, axis=0, keepdims=True)


NORMED = [(C_FQ, FOX_HEADS, 0), (C_FK, FOX_HEADS, 1), (C_SQ, SWA_HEADS, 2), (C_SK, SWA_KV_HEADS, 3), (C_MQ, MEM_HEADS, 4)]
PLAIN = [(C_FV, FOX_HEADS), (C_SV, SWA_KV_HEADS)]


def _prep_fwd(proj, gains):
    T = proj.shape[0]
    tr = _tile(T, 256, 128)

    def body(p_ref, g_ref, o_ref, lf_ref, kt_ref, vt_ref):
        for start, heads, row in NORMED:
            gn = g_ref[row:row + 1, :]
            for hh in range(heads):
                sl = slice(start + hh * HEAD_DIM, start + (hh + 1) * HEAD_DIM)
                y = _head_norm(p_ref[:, sl], gn)
                o_ref[:, sl] = y.astype(BF16)
                if start == C_FK:
                    kt_ref[hh * HEAD_DIM:(hh + 1) * HEAD_DIM, :] = y.T.astype(BF16)
        for start, heads in PLAIN:
            sl = slice(start, start + heads * HEAD_DIM)
            o_ref[:, sl] = p_ref[:, sl].astype(BF16)
        for hh in range(FOX_HEADS):
            sl = slice(C_FV + hh * HEAD_DIM, C_FV + (hh + 1) * HEAD_DIM)
            vt_ref[hh * HEAD_DIM:(hh + 1) * HEAD_DIM, :] = p_ref[:, sl].T.astype(BF16)
        zb = p_ref[:, C_FL:C_FL + HEAD_DIM] + g_ref[5:6, :]
        o_ref[:, C_FL:C_FL + HEAD_DIM] = jnp.zeros((tr, HEAD_DIM), BF16)
        lf_ref[...] = jnp.minimum(zb, 0.0) - jnp.log(1.0 + jnp.exp(-jnp.abs(zb)))

    return pl.pallas_call(
        body, name="prep_fwd", grid=(T // tr,),
        in_specs=[pl.BlockSpec((tr, PROJ_W), lambda i: (i, 0)), pl.BlockSpec((8, 128), lambda i: (0, 0))],
        out_specs=[pl.BlockSpec((tr, PROJ_W), lambda i: (i, 0)), pl.BlockSpec((tr, HEAD_DIM), lambda i: (i, 0)),
                   pl.BlockSpec((FOX_W, tr), lambda i: (0, i)), pl.BlockSpec((FOX_W, tr), lambda i: (0, i))],
        out_shape=[jax.ShapeDtypeStruct((T, PROJ_W), BF16), jax.ShapeDtypeStruct((T, HEAD_DIM), F32),
                   jax.ShapeDtypeStruct((FOX_W, T), BF16), jax.ShapeDtypeStruct((FOX_W, T), BF16)],
        compiler_params=_params(("parallel",)),
    )(proj, gains)


def _prep_bwd(proj, gains, dfq, dfk, dfv, dsq, dsk, dsv, dmq, dlogf):
    T = proj.shape[0]
    tr = _tile(T, 256, 8)
    d_normed = {C_FQ: 0, C_FK: 1, C_SQ: 3, C_SK: 4, C_MQ: 6}
    d_plain = {C_FV: 2, C_SV: 5}

    def body(p_ref, g_ref, *rest):
        d_refs, dlf_ref, o_ref, dg_ref = rest[:7], rest[7], rest[8], rest[9]
        rows = []
        for start, heads, row in NORMED:
            gn = g_ref[row:row + 1, :]
            d_ref = d_refs[d_normed[start]]
            tot = jnp.zeros((1, HEAD_DIM), F32)
            for hh in range(heads):
                sl = slice(start + hh * HEAD_DIM, start + (hh + 1) * HEAD_DIM)
                dx, dgn = _head_norm_bwd(p_ref[:, sl], gn, d_ref[:, hh * HEAD_DIM:(hh + 1) * HEAD_DIM])
                o_ref[:, sl] = dx.astype(BF16)
                tot = tot + dgn
            rows.append(tot)
        for start, heads in PLAIN:
            o_ref[:, start:start + heads * HEAD_DIM] = d_refs[d_plain[start]][...].astype(BF16)
        zb = p_ref[:, C_FL:C_FL + HEAD_DIM] + g_ref[5:6, :]
        lane = lax.broadcasted_iota(jnp.int32, (tr, HEAD_DIM), 1)
        dz = jnp.where(lane < FOX_HEADS, dlf_ref[...] * (1.0 - _sigmoid(zb)), 0.0)
        o_ref[:, C_FL:C_FL + HEAD_DIM] = dz.astype(BF16)
        rows.append(jnp.sum(dz, axis=0, keepdims=True))
        part = jnp.concatenate(rows + [jnp.zeros((2, HEAD_DIM), F32)], axis=0)

        @pl.when(pl.program_id(0) == 0)
        def _():
            dg_ref[...] = part

        @pl.when(pl.program_id(0) > 0)
        def _():
            dg_ref[...] += part

    def rows_of(w):
        return pl.BlockSpec((tr, w), lambda i: (i, 0))

    small = pl.BlockSpec((8, 128), lambda i: (0, 0))
    ds = [dfq, dfk, dfv, dsq, dsk, dsv, dmq]
    return pl.pallas_call(
        body, name="prep_bwd", grid=(T // tr,),
        in_specs=[rows_of(PROJ_W), small] + [rows_of(d.shape[1]) for d in ds] + [rows_of(HEAD_DIM)],
        out_specs=[rows_of(PROJ_W), small],
        out_shape=[jax.ShapeDtypeStruct((T, PROJ_W), BF16), jax.ShapeDtypeStruct((8, 128), F32)],
        compiler_params=_params(("arbitrary",)),
    )(proj, gains, *ds, dlogf)


def _head_norm_rows(x, gain):
    R, W = x.shape

    def body(x_ref, g_ref, o_ref):
        for hh in range(W // HEAD_DIM):
            sl = slice(hh * HEAD_DIM, (hh + 1) * HEAD_DIM)
            o_ref[:, sl] = _head_norm(x_ref[:, sl], g_ref[...]).astype(BF16)

    return pl.pallas_call(body, name="mem_k_norm", out_shape=jax.ShapeDtypeStruct((R, W), BF16))(x, gain)


def _head_norm_rows_bwd(x, gain, dy):
    R, W = x.shape

    def body(x_ref, g_ref, dy_ref, dx_ref, dg_ref):
        tot = jnp.zeros((1, HEAD_DIM), F32)
        for hh in range(W // HEAD_DIM):
            sl = slice(hh * HEAD_DIM, (hh + 1) * HEAD_DIM)
            dx, dgn = _head_norm_bwd(x_ref[:, sl], g_ref[...], dy_ref[:, sl])
            dx_ref[:, sl] = dx.astype(BF16)
            tot = tot + dgn
        dg_ref[...] = tot

    return pl.pallas_call(
        body, name="mem_k_norm_bwd",
        out_shape=[jax.ShapeDtypeStruct((R, W), BF16), jax.ShapeDtypeStruct((1, HEAD_DIM), F32)])(x, gain, dy)


def _cumsum_rows(name, xs, reverse, columns=None):
    T, W = xs[0].shape
    tb = _tile(T, 512, 8)
    nb = T // tb
    n_in = len(xs) + (0 if columns is None else 1)

    def body(*refs):
        o_ref, carry = refs[n_in], refs[n_in + 1]

        @pl.when(pl.program_id(0) == 0)
        def _():
            carry[...] = jnp.zeros_like(carry)

        xv = refs[0][...]
        for x_ref in refs[1:len(xs)]:
            xv = xv + x_ref[...]
        if columns is not None:
            lane = lax.broadcasted_iota(jnp.int32, (tb, W), 1)
            for hh in range(columns.shape[0]):
                xv = xv + jnp.where(lane == hh, refs[len(xs)][hh], 0.0)
        r = lax.broadcasted_iota(jnp.int32, (tb, tb), 0)
        cc = lax.broadcasted_iota(jnp.int32, (tb, tb), 1)
        tri = jnp.where((cc >= r) if reverse else (cc <= r), 1.0, 0.0).astype(F32)
        o_ref[...] = jnp.dot(tri, xv, precision=lax.Precision.HIGHEST, preferred_element_type=F32) + carry[...]
        carry[...] += jnp.sum(xv, axis=0, keepdims=True)

    idx = (lambda i: (nb - 1 - i, 0)) if reverse else (lambda i: (i, 0))
    in_specs = [pl.BlockSpec((tb, W), idx)] * len(xs)
    if columns is not None:
        in_specs.append(pl.BlockSpec((columns.shape[0], tb, 1), lambda i: (0, idx(i)[0], 0)))
    return pl.pallas_call(
        body, name=name, grid=(nb,), in_specs=in_specs, out_specs=pl.BlockSpec((tb, W), idx),
        out_shape=jax.ShapeDtypeStruct((T, W), F32), scratch_shapes=[pltpu.VMEM((1, W), F32)],
        compiler_params=_params(("arbitrary",)),
    )(*xs, *([] if columns is None else [columns]))


def _triangle(nq, by_column):
    if by_column:
        blocks = [(i, j) for j in range(nq) for i in range(j, nq)]
    else:
        blocks = [(i, j) for i in range(nq) for j in range(i + 1)]
    return jnp.asarray(np.array(blocks, np.int32).T)


def _fox_scores_t(k, q, cq_row, ck_rep, on_diagonal):
    n = q.shape[0]
    s = _dot(k, q, "nt") * SCALE + (cq_row - jnp.tile(ck_rep, (1, n // HEAD_DIM)))
    if on_diagonal:
        s = jnp.where(lax.broadcasted_iota(jnp.int32, (n, n), 0) <= lax.broadcasted_iota(jnp.int32, (n, n), 1), s, NEG_INF)
    return s


def _fox_fwd(qkv, v_t, cq_row, ck_rep):
    T = qkv.shape[0]
    tq = _tile(T, 1024)
    nq = T // tq
    steps = nq * (nq + 1) // 2
    HQ, HK = C_FQ // HEAD_DIM, C_FK // HEAD_DIM

    def body(tab, q_ref, k_ref, vt_ref, cq_ref, ck_ref, o_ref, of_ref, lse_ref, m_sc, l_sc, acc_sc):
        i, j = tab[0, pl.program_id(1)], tab[1, pl.program_id(1)]

        @pl.when(j == 0)
        def _():
            m_sc[...] = jnp.full_like(m_sc, NEG_INF)
            l_sc[...] = jnp.zeros_like(l_sc)
            acc_sc[...] = jnp.zeros_like(acc_sc)

        def step(on_diagonal):
            s = _fox_scores_t(k_ref[...], q_ref[...], cq_ref[...], ck_ref[...], on_diagonal)
            m_new = jnp.maximum(m_sc[...], jnp.max(s, axis=0, keepdims=True))
            alpha = jnp.exp(m_sc[...] - m_new)
            p = jnp.exp(s - m_new)
            l_sc[...] = alpha * l_sc[...] + jnp.sum(p, axis=0, keepdims=True)
            acc_sc[...] = alpha * acc_sc[...] + _dot(vt_ref[...], p.astype(BF16), "nn")
            m_sc[...] = m_new

        @pl.when(j < i)
        def _():
            step(False)

        @pl.when(j == i)
        def _():
            step(True)
            o = (acc_sc[...] / l_sc[...]).T
            o_ref[...] = o.astype(BF16)
            of_ref[...] = o
            lse_ref[...] = m_sc[...] + jnp.log(l_sc[...])

    qrow = pl.BlockSpec((None, 1, tq), lambda h, s, tab: (h, 0, tab[0, s]))
    return pl.pallas_call(
        body, name="fox_fwd",
        grid_spec=pltpu.PrefetchScalarGridSpec(
            num_scalar_prefetch=1, grid=(FOX_HEADS, steps),
            in_specs=[pl.BlockSpec((tq, HEAD_DIM), lambda h, s, tab: (tab[0, s], HQ + h)),
                      pl.BlockSpec((tq, HEAD_DIM), lambda h, s, tab: (tab[1, s], HK + h)),
                      pl.BlockSpec((HEAD_DIM, tq), lambda h, s, tab: (h, tab[1, s])), qrow,
                      pl.BlockSpec((None, tq, HEAD_DIM), lambda h, s, tab: (h, tab[1, s], 0))],
            out_specs=[pl.BlockSpec((tq, HEAD_DIM), lambda h, s, tab: (tab[0, s], h)),
                       pl.BlockSpec((tq, HEAD_DIM), lambda h, s, tab: (tab[0, s], h)), qrow],
            scratch_shapes=[pltpu.VMEM((1, tq), F32), pltpu.VMEM((1, tq), F32), pltpu.VMEM((HEAD_DIM, tq), F32)]),
        out_shape=[jax.ShapeDtypeStruct((T, FOX_W), BF16), jax.ShapeDtypeStruct((T, FOX_W), F32),
                   jax.ShapeDtypeStruct((FOX_HEADS, 1, T), F32)],
        compiler_params=_params(("parallel", "arbitrary")),
    )(_triangle(nq, False), qkv, qkv, v_t, cq_row, ck_rep)


def _fox_delta(dmix, out_f32):
    T = out_f32.shape[0]
    tr = _tile(T, 512, 8)

    def body(do_ref, o_ref, d_ref):
        lane = lax.broadcasted_iota(jnp.int32, (tr, HEAD_DIM), 1)
        acc = jnp.zeros((tr, HEAD_DIM), F32)
        for hh in range(FOX_HEADS):
            sl = slice(hh * HEAD_DIM, (hh + 1) * HEAD_DIM)
            d = jnp.sum(do_ref[:, sl].astype(F32) * o_ref[:, sl], axis=-1, keepdims=True)
            acc = jnp.where(lane == hh, d, acc)
        d_ref[...] = acc

    blk = pl.BlockSpec((tr, FOX_W), lambda i: (i, 0))
    return pl.pallas_call(
        body, name="fox_delta", grid=(T // tr,), in_specs=[blk, blk], out_specs=pl.BlockSpec((tr, HEAD_DIM), lambda i: (i, 0)),
        out_shape=jax.ShapeDtypeStruct((T, HEAD_DIM), F32), compiler_params=_params(("parallel",)),
    )(dmix, out_f32)


def _fox_bwd(qkv, k_t, cq_row, ck_rep, delta_row, lse, dmix):
    T = qkv.shape[0]
    tq = _tile(T, 1024)
    nq = T // tq
    steps = nq * (nq + 1) // 2
    HQ, HK, HV = C_FQ // HEAD_DIM, C_FK // HEAD_DIM, C_FV // HEAD_DIM

    def body(tab, q_ref, k_ref, kt_ref, v_ref, cq_ref, ck_ref, delta_ref, lse_ref, do_ref,
             dq_ref, dk_ref, dv_ref, dck_ref, dcq_ref, dk_sc, dv_sc, dc_sc, dqt_sc):
        qi, kj = tab[0, pl.program_id(1)], tab[1, pl.program_id(1)]

        @pl.when(qi == kj)
        def _():
            dk_sc[...] = jnp.zeros_like(dk_sc)
            dv_sc[...] = jnp.zeros_like(dv_sc)
            dc_sc[...] = jnp.zeros_like(dc_sc)

        def step(on_diagonal):
            q, k, v, do = q_ref[...], k_ref[...], v_ref[...], do_ref[...]
            p = jnp.exp(_fox_scores_t(k, q, cq_ref[...], ck_ref[...], on_diagonal) - lse_ref[...])
            dp = _dot(v, do, "nt")
            ds = p * (dp - delta_ref[...])
            dsb = ds.astype(BF16)
            dv_sc[...] += _dot(p.astype(BF16), do, "nn")
            dk_sc[...] += _dot(dsb, q, "nn")
            dc_sc[...] += jnp.sum(ds, axis=1, keepdims=True)
            dq_part = _dot(kt_ref[...], dsb, "nn") * SCALE
            dcq_part = jnp.sum(ds, axis=0, keepdims=True)

            @pl.when(kj == 0)
            def _():
                dqt_sc[qi] = dq_part
                dcq_ref[qi] = dcq_part

            @pl.when(kj > 0)
            def _():
                dqt_sc[qi] += dq_part
                dcq_ref[qi] += dcq_part

            if on_diagonal:
                dq_ref[...] = dqt_sc[qi].T

        @pl.when(qi > kj)
        def _():
            step(False)

        @pl.when(qi == kj)
        def _():
            step(True)

        @pl.when(qi == nq - 1)
        def _():
            dk_ref[...] = dk_sc[...] * SCALE
            dv_ref[...] = dv_sc[...]
            dck_ref[...] = -dc_sc[...]

    def rows(base):
        return pl.BlockSpec((tq, HEAD_DIM), lambda h, s, tab: (tab[0, s], base + h))

    def cols(base):
        return pl.BlockSpec((tq, HEAD_DIM), lambda h, s, tab: (tab[1, s], base + h))

    qrow = pl.BlockSpec((None, 1, tq), lambda h, s, tab: (h, 0, tab[0, s]))
    sds = jax.ShapeDtypeStruct((T, FOX_W), F32)
    return pl.pallas_call(
        body, name="fox_bwd",
        grid_spec=pltpu.PrefetchScalarGridSpec(
            num_scalar_prefetch=1, grid=(FOX_HEADS, steps),
            in_specs=[rows(HQ), cols(HK), pl.BlockSpec((HEAD_DIM, tq), lambda h, s, tab: (h, tab[1, s])), cols(HV), qrow,
                      pl.BlockSpec((None, tq, HEAD_DIM), lambda h, s, tab: (h, tab[1, s], 0)), qrow, qrow, rows(0)],
            out_specs=[cols(0), cols(0), cols(0), pl.BlockSpec((None, tq, 1), lambda h, s, tab: (h, tab[1, s], 0)),
                       pl.BlockSpec((None, nq, 1, tq), lambda h, s, tab: (h, 0, 0, 0))],
            scratch_shapes=[pltpu.VMEM((tq, HEAD_DIM), F32), pltpu.VMEM((tq, HEAD_DIM), F32), pltpu.VMEM((tq, 1), F32),
                            pltpu.VMEM((nq, HEAD_DIM, tq), F32)]),
        out_shape=[sds, sds, sds, jax.ShapeDtypeStruct((FOX_HEADS, T, 1), F32), jax.ShapeDtypeStruct((FOX_HEADS, nq, 1, tq), F32)],
        compiler_params=_params(("parallel", "arbitrary")),
    )(_triangle(nq, True), qkv, qkv, k_t, qkv, cq_row, ck_rep, delta_row, lse, dmix)


GW = SWA_GROUP * HEAD_DIM
GR = SWA_GROUP * WINDOW


def _swa_scores(q_ref, kp_ref, kc_ref, slope_ref, n):
    q = q_ref[...]
    qs = jnp.concatenate([q[:, t * HEAD_DIM:(t + 1) * HEAD_DIM] for t in range(SWA_GROUP)], axis=0)
    kb = jnp.concatenate([kp_ref[...], kc_ref[...]], axis=0)
    r = lax.broadcasted_iota(jnp.int32, (GR, 2 * WINDOW), 0) & (WINDOW - 1)
    jj = lax.broadcasted_iota(jnp.int32, (GR, 2 * WINDOW), 1)
    dist = WINDOW + r - jj
    valid = (dist >= 0) & (dist < WINDOW) & ((n > 0) | (jj >= WINDOW))
    s = _dot(qs, kb, "nt") * SCALE - slope_ref[...] * dist.astype(F32)
    return qs, kb, jnp.where(valid, s, NEG_INF), valid


def _swa_specs():
    HQ, HK, HV = C_SQ // GW, C_SK // HEAD_DIM, C_SV // HEAD_DIM
    q_spec = pl.BlockSpec((WINDOW, GW), lambda g, n: (n, HQ + g))

    def prev(base):
        return pl.BlockSpec((WINDOW, HEAD_DIM), lambda g, n: (jnp.maximum(n - 1, 0), base + g))

    def cur(base):
        return pl.BlockSpec((WINDOW, HEAD_DIM), lambda g, n: (n, base + g))

    col = pl.BlockSpec((None, GR, 1), lambda g, n: (g, 0, 0))
    return q_spec, prev(HK), cur(HK), prev(HV), cur(HV), col


def _swa_fwd(qkv, slopes, sinks):
    T = qkv.shape[0]
    nb = T // WINDOW
    assert C_SQ % GW == 0

    def body(q_ref, kp_ref, kc_ref, vp_ref, vc_ref, slope_ref, sink_ref, o_ref, lse_ref):
        n = pl.program_id(1)
        _, _, s, _ = _swa_scores(q_ref, kp_ref, kc_ref, slope_ref, n)
        m = jnp.maximum(jnp.max(s, axis=-1, keepdims=True), sink_ref[...])
        p = jnp.exp(s - m)
        l = jnp.sum(p, axis=-1, keepdims=True) + jnp.exp(sink_ref[...] - m)
        vb = jnp.concatenate([vp_ref[...], vc_ref[...]], axis=0)
        o = _dot(p.astype(BF16), vb, "nn") / l
        for t in range(SWA_GROUP):
            o_ref[:, t * HEAD_DIM:(t + 1) * HEAD_DIM] = o[t * WINDOW:(t + 1) * WINDOW, :].astype(BF16)
        lse_ref[...] = m + jnp.log(l)

    q_spec, kp, kc, vp, vc, col = _swa_specs()
    return pl.pallas_call(
        body, name="swa_fwd", grid=(SWA_KV_HEADS, nb), in_specs=[q_spec, kp, kc, vp, vc, col, col],
        out_specs=[pl.BlockSpec((WINDOW, GW), lambda g, n: (n, g)), pl.BlockSpec((None, None, GR, 1), lambda g, n: (g, n, 0, 0))],
        out_shape=[jax.ShapeDtypeStruct((T, SWA_HEADS * HEAD_DIM), BF16), jax.ShapeDtypeStruct((SWA_KV_HEADS, nb, GR, 1), F32)],
        compiler_params=_params(("parallel", "arbitrary")),
    )(qkv, qkv, qkv, qkv, qkv, slopes, sinks)


def _swa_bwd(qkv, slopes, sinks, out, lse, dmix):
    T = qkv.shape[0]
    nb = T // WINDOW
    DO = FOX_W // GW
    assert FOX_W % GW == 0

    def body(q_ref, kp_ref, kc_ref, vp_ref, vc_ref, slope_ref, sink_ref, o_ref, lse_ref, do_ref,
             dq_ref, dk_ref, dv_ref, dsink_ref, sink_sc):
        n = pl.program_id(1)

        @pl.when(n == 0)
        def _():
            dk_ref[...] = jnp.zeros_like(dk_ref)
            dv_ref[...] = jnp.zeros_like(dv_ref)
            sink_sc[...] = jnp.zeros_like(sink_sc)

        qs, kb, s, valid = _swa_scores(q_ref, kp_ref, kc_ref, slope_ref, n)
        lse = lse_ref[...]
        p = jnp.where(valid, jnp.exp(s - lse), 0.0)
        vb = jnp.concatenate([vp_ref[...], vc_ref[...]], axis=0)
        do = jnp.concatenate([do_ref[:, t * HEAD_DIM:(t + 1) * HEAD_DIM] for t in range(SWA_GROUP)], axis=0)
        oo = jnp.concatenate([o_ref[:, t * HEAD_DIM:(t + 1) * HEAD_DIM] for t in range(SWA_GROUP)], axis=0)
        dp = _dot(do, vb, "nt")
        delta = jnp.sum(do.astype(F32) * oo.astype(F32), axis=-1, keepdims=True)
        ds = p * (dp - delta)
        dsb = ds.astype(BF16)
        dq = _dot(dsb, kb, "nn") * SCALE
        for t in range(SWA_GROUP):
            dq_ref[:, t * HEAD_DIM:(t + 1) * HEAD_DIM] = dq[t * WINDOW:(t + 1) * WINDOW, :]
        dkb = _dot(dsb, qs, "tn") * SCALE
        dvb = _dot(p.astype(BF16), do, "tn")
        r_prev = pl.ds(pl.multiple_of(jnp.maximum(n - 1, 0) * WINDOW, WINDOW), WINDOW)
        r_cur = pl.ds(pl.multiple_of(n * WINDOW, WINDOW), WINDOW)
        dk_ref[r_prev, :] += dkb[:WINDOW, :]
        dk_ref[r_cur, :] += dkb[WINDOW:, :]
        dv_ref[r_prev, :] += dvb[:WINDOW, :]
        dv_ref[r_cur, :] += dvb[WINDOW:, :]
        sink_sc[...] -= jnp.exp(sink_ref[...] - lse) * delta

        @pl.when(n == nb - 1)
        def _():
            tot = [jnp.zeros((1, 128), F32) + jnp.sum(sink_sc[t * WINDOW:(t + 1) * WINDOW, :]) for t in range(SWA_GROUP)]
            dsink_ref[...] = jnp.concatenate(tot + [jnp.zeros((8 - SWA_GROUP, 128), F32)], axis=0)

    q_spec, kp, kc, vp, vc, col = _swa_specs()
    kv_acc = pl.BlockSpec((T, HEAD_DIM), lambda g, n: (0, g))
    return pl.pallas_call(
        body, name="swa_bwd", grid=(SWA_KV_HEADS, nb),
        in_specs=[q_spec, kp, kc, vp, vc, col, col, pl.BlockSpec((WINDOW, GW), lambda g, n: (n, g)),
                  pl.BlockSpec((None, None, GR, 1), lambda g, n: (g, n, 0, 0)), pl.BlockSpec((WINDOW, GW), lambda g, n: (n, DO + g))],
        out_specs=[pl.BlockSpec((WINDOW, GW), lambda g, n: (n, g)), kv_acc, kv_acc, pl.BlockSpec((None, 8, 128), lambda g, n: (g, 0, 0))],
        out_shape=[jax.ShapeDtypeStruct((T, SWA_HEADS * HEAD_DIM), F32), jax.ShapeDtypeStruct((T, SWA_KV_HEADS * HEAD_DIM), F32),
                   jax.ShapeDtypeStruct((T, SWA_KV_HEADS * HEAD_DIM), F32), jax.ShapeDtypeStruct((SWA_KV_HEADS, 8, 128), F32)],
        scratch_shapes=[pltpu.VMEM((GR, 1), F32)],
        compiler_params=_params(("parallel", "arbitrary")),
    )(qkv, qkv, qkv, qkv, qkv, slopes, sinks, out, lse, dmix)


def _mem_fwd(qkv, mk, mv):
    T, ML = qkv.shape[0], mk.shape[0]
    tq = _tile(T, 1024)
    HQ = C_MQ // HEAD_DIM

    def body(q_ref, k_ref, v_ref, o_ref, lse_ref):
        s = _dot(q_ref[...], k_ref[...], "nt") * SCALE
        m = jnp.max(s, axis=-1, keepdims=True)
        p = jnp.exp(s - m)
        l = jnp.sum(p, axis=-1, keepdims=True)
        o_ref[...] = (_dot(p.astype(BF16), v_ref[...], "nn") / l).astype(BF16)
        lse_ref[...] = m + jnp.log(l)

    kv = pl.BlockSpec((ML, HEAD_DIM), lambda h, i: (0, h))
    return pl.pallas_call(
        body, name="mem_fwd", grid=(MEM_HEADS, T // tq),
        in_specs=[pl.BlockSpec((tq, HEAD_DIM), lambda h, i: (i, HQ + h)), kv, kv],
        out_specs=[pl.BlockSpec((tq, HEAD_DIM), lambda h, i: (i, h)), pl.BlockSpec((None, tq, 1), lambda h, i: (h, i, 0))],
        out_shape=[jax.ShapeDtypeStruct((T, MEM_HEADS * HEAD_DIM), BF16), jax.ShapeDtypeStruct((MEM_HEADS, T, 1), F32)],
        compiler_params=_params(("parallel", "arbitrary")),
    )(qkv, mk, mv)


def _mem_bwd(qkv, mk, mv, out, lse, dmix):
    T, ML = qkv.shape[0], mk.shape[0]
    tq = _tile(T, 1024)
    HQ = C_MQ // HEAD_DIM
    DO = (FOX_W + SWA_HEADS * HEAD_DIM) // HEAD_DIM

    def body(q_ref, k_ref, v_ref, o_ref, lse_ref, do_ref, dq_ref, dk_ref, dv_ref):
        q, k, v, do = q_ref[...], k_ref[...], v_ref[...], do_ref[...]
        p = jnp.exp(_dot(q, k, "nt") * SCALE - lse_ref[...])
        dp = _dot(do, v, "nt")
        delta = jnp.sum(do.astype(F32) * o_ref[...].astype(F32), axis=-1, keepdims=True)
        dsb = (p * (dp - delta)).astype(BF16)
        dq_ref[...] = _dot(dsb, k, "nn") * SCALE
        dk_part = _dot(dsb, q, "tn") * SCALE
        dv_part = _dot(p.astype(BF16), do, "tn")

        @pl.when(pl.program_id(1) == 0)
        def _():
            dk_ref[...] = dk_part
            dv_ref[...] = dv_part

        @pl.when(pl.program_id(1) > 0)
        def _():
            dk_ref[...] += dk_part
            dv_ref[...] += dv_part

    kv = pl.BlockSpec((ML, HEAD_DIM), lambda h, i: (0, h))
    qb = pl.BlockSpec((tq, HEAD_DIM), lambda h, i: (i, h))
    return pl.pallas_call(
        body, name="mem_bwd", grid=(MEM_HEADS, T // tq),
        in_specs=[pl.BlockSpec((tq, HEAD_DIM), lambda h, i: (i, HQ + h)), kv, kv, qb,
                  pl.BlockSpec((None, tq, 1), lambda h, i: (h, i, 0)), pl.BlockSpec((tq, HEAD_DIM), lambda h, i: (i, DO + h))],
        out_specs=[qb, kv, kv],
        out_shape=[jax.ShapeDtypeStruct((T, MEM_HEADS * HEAD_DIM), F32), jax.ShapeDtypeStruct((ML, MEM_HEADS * HEAD_DIM), F32),
                   jax.ShapeDtypeStruct((ML, MEM_HEADS * HEAD_DIM), F32)],
        compiler_params=_params(("parallel", "arbitrary")),
    )(qkv, mk, mv, out, lse, dmix)


HBM = pl.BlockSpec(memory_space=pltpu.HBM)


def _place():
    x, y, c = lax.axis_index("x"), lax.axis_index("y"), lax.axis_index("c")
    chips = [(1 - x, y), (x, 1 - y), (1 - x, 1 - y)]
    return x, y, c, chips


def _remote(src, dst, send_sem, recv_sem, device):
    return pltpu.make_async_remote_copy(src_ref=src, dst_ref=dst, send_sem=send_sem, recv_sem=recv_sem,
                                        device_id=device, device_id_type=MESH)


def _place_ids():
    x, y, c = lax.axis_index("x"), lax.axis_index("y"), lax.axis_index("c")
    order = [2 * x + y, 2 * (1 - x) + y, 2 * x + (1 - y), 2 * (1 - x) + (1 - y)]
    return jnp.stack([2 * x + y, c] + order).astype(jnp.int32)


def _cast_place(name, w, ids, *after):
    R, C = w.shape
    tr = _tile(R, 256, 16)

    def body(ids_ref, w_ref, *rest):
        rest[-1][...] = w_ref[...].astype(BF16)

    return pl.pallas_call(
        body, name=name,
        grid_spec=pltpu.PrefetchScalarGridSpec(
            num_scalar_prefetch=1, grid=(R // tr,),
            in_specs=[pl.BlockSpec((tr, C), lambda i, ids: (i, 0))] + [pl.BlockSpec(memory_space=pl.ANY)] * len(after),
            out_specs=pl.BlockSpec((None, tr, C), lambda i, ids: (ids[0], i, 0))),
        out_shape=jax.ShapeDtypeStruct((N_CHIPS, R, C), BF16), compiler_params=_params(("parallel",)),
    )(ids, w, *after)


SEM = pl.BlockSpec(memory_space=pltpu.SEMAPHORE)
EFFECT = pltpu.SideEffectType.DATAFLOW_SIDE_EFFECTING


def _hbm(a):
    return pltpu.with_memory_space_constraint(a, pltpu.HBM)


def _gather_start(name, placed, after, to_sibling=False):
    n = len(placed)

    ns = 3 * n

    def body(*refs):
        send, recv = refs[n + 1:n + 1 + ns], refs[n + 1 + ns:n + 1 + 2 * ns]
        buf = refs[n + 1 + 2 * ns:2 * n + 1 + 2 * ns]
        token = refs[2 * n + 1 + 2 * ns]
        x, y, c, chips = _place()
        me = 2 * x + y
        for a in range(n):
            half = buf[a].shape[1] // 2
            for j, (cx, cy) in enumerate(chips):
                block, peer = (2 * cx + cy, (x, y, 1 - c)) if to_sibling else (me, (cx, cy, c))
                part = buf[a].at[block, pl.ds(c * half, half)]
                _remote(part, part, send[3 * a + j], recv[3 * a + j], peer).start()
        token[...] = jnp.zeros_like(token)

    res = pl.pallas_call(
        body, name=name, in_specs=[HBM] * n + [pl.BlockSpec(memory_space=pl.ANY)],
        out_specs=[SEM] * (2 * ns) + [HBM] * n + [pl.BlockSpec(memory_space=pltpu.VMEM)],
        out_shape=[pltpu.SemaphoreType.DMA(())] * (2 * ns)
        + [pltpu.HBM(s.shape, s.dtype) for s in placed] + [jax.ShapeDtypeStruct((8, 128), F32)],
        input_output_aliases={a: 2 * ns + a for a in range(n)},
        compiler_params=pltpu.CompilerParams(has_side_effects=EFFECT),
    )(*[_hbm(s) for s in placed], after)
    return list(res[:ns]), list(res[ns:2 * ns]), list(res[2 * ns:2 * ns + n]), res[2 * ns + n]


def _gather_wait(name, send, recv, bufs, after, to_sibling=False):
    n = len(bufs)

    ns = 3 * n

    def body(*refs):
        buf = refs[:n]
        send_ref, recv_ref = refs[n:n + ns], refs[n + ns:n + 2 * ns]
        x, y, c, chips = _place()
        ids = [2 * cx + cy for cx, cy in chips]
        for a in range(n):
            half = buf[a].shape[1] // 2
            for j in range(3):
                sent = buf[a].at[ids[j], pl.ds(c * half, half)]
                landed = buf[a].at[ids[j], pl.ds((1 - c) * half, half)] if to_sibling else sent
                cp = _remote(sent, landed, send_ref[3 * a + j], recv_ref[3 * a + j], (x, y, c))
                cp.wait_send()
                cp.wait_recv()

    res = pl.pallas_call(
        body, name=name, in_specs=[HBM] * n + [SEM] * (2 * ns) + [pl.BlockSpec(memory_space=pl.ANY)], out_specs=[HBM] * n,
        out_shape=[pltpu.HBM(s.shape, s.dtype) for s in bufs], input_output_aliases={a: a for a in range(n)},
        compiler_params=pltpu.CompilerParams(has_side_effects=EFFECT),
    )(*bufs, *send, *recv, after)
    return list(res)


def _gather_forward(name, bufs):
    n = len(bufs)

    def body(*refs):
        buf = refs[n:2 * n]
        send, recv = refs[2 * n:]
        x, y, c, chips = _place()
        ids = [2 * cx + cy for cx, cy in chips]
        copies = []
        for a in range(n):
            half = buf[a].shape[1] // 2
            for j in range(3):
                landed = buf[a].at[ids[j], pl.ds(c * half, half)]
                cp = _remote(landed, landed, send.at[a, j], recv.at[a, j], (x, y, 1 - c))
                cp.start()
                copies.append(cp)
        for a in range(n):
            half = buf[a].shape[1] // 2
            for j in range(3):
                landed = buf[a].at[ids[j], pl.ds((1 - c) * half, half)]
                _remote(landed, landed, send.at[a, j], recv.at[a, j], (x, y, c)).wait_recv()
        for cp in copies:
            cp.wait_send()

    return pl.pallas_call(
        body, name=name, in_specs=[HBM] * n, out_specs=[HBM] * n,
        out_shape=[jax.ShapeDtypeStruct(s.shape, s.dtype) for s in bufs], input_output_aliases={a: a for a in range(n)},
        scratch_shapes=[pltpu.SemaphoreType.DMA((n, 3)), pltpu.SemaphoreType.DMA((n, 3))],
    )(*bufs)


def _pair_start(name, grads):
    n = len(grads)
    ns = N_CHIPS * n

    def body(*refs):
        send, recv = refs[2 * n:2 * n + ns], refs[2 * n + ns:2 * n + 2 * ns]
        src = refs[2 * n + 2 * ns:3 * n + 2 * ns]
        land = refs[3 * n + 2 * ns:4 * n + 2 * ns]
        token = refs[4 * n + 2 * ns]
        x, y, c, chips = _place()
        order = [2 * x + y] + [2 * cx + cy for cx, cy in chips]
        for a in range(n):
            half = src[a].shape[1] // 2
            for j in range(N_CHIPS):
                _remote(src[a].at[order[j], pl.ds((1 - c) * half, half)], land[a].at[j],
                        send[N_CHIPS * a + j], recv[N_CHIPS * a + j], (x, y, 1 - c)).start()
        token[...] = jnp.zeros_like(token)

    lands = [jax.ShapeDtypeStruct((N_CHIPS, g.shape[1] // 2, g.shape[2]), g.dtype) for g in grads]
    res = pl.pallas_call(
        body, name=name, in_specs=[HBM] * (2 * n),
        out_specs=[SEM] * (2 * ns) + [HBM] * (2 * n) + [pl.BlockSpec(memory_space=pltpu.VMEM)],
        out_shape=[pltpu.SemaphoreType.DMA(())] * (2 * ns) + [pltpu.HBM(g.shape, g.dtype) for g in grads]
        + [pltpu.HBM(l.shape, l.dtype) for l in lands] + [jax.ShapeDtypeStruct((8, 128), F32)],
        input_output_aliases={a: 2 * ns + a for a in range(2 * n)},
        compiler_params=pltpu.CompilerParams(has_side_effects=EFFECT),
    )(*[_hbm(g) for g in grads], *[_hbm(lax.empty(l.shape, l.dtype)) for l in lands])
    return list(res[:ns]), list(res[ns:2 * ns]), list(res[2 * ns:2 * ns + n]), list(res[2 * ns + n:2 * ns + 2 * n]), res[2 * ns + 2 * n]


def _pair_wait(name, send, recv, grads, lands, after):
    n = len(grads)
    ns = N_CHIPS * n

    def body(*refs):
        src, land = refs[:n], refs[n:2 * n]
        send_ref, recv_ref = refs[2 * n:2 * n + ns], refs[2 * n + ns:2 * n + 2 * ns]
        x, y, c, _ = _place()
        for a in range(n):
            for j in range(N_CHIPS):
                cp = _remote(land[a].at[j], land[a].at[j], send_ref[N_CHIPS * a + j], recv_ref[N_CHIPS * a + j], (x, y, c))
                cp.wait_send()
                cp.wait_recv()

    res = pl.pallas_call(
        body, name=name, in_specs=[HBM] * (2 * n) + [SEM] * (2 * ns) + [pl.BlockSpec(memory_space=pl.ANY)],
        out_specs=[HBM] * (2 * n), out_shape=[pltpu.HBM(g.shape, g.dtype) for g in grads] + [pltpu.HBM(l.shape, l.dtype) for l in lands],
        input_output_aliases={a: a for a in range(2 * n)},
        compiler_params=pltpu.CompilerParams(has_side_effects=EFFECT),
    )(*grads, *lands, *send, *recv, after)
    return list(res[:n]), list(res[n:])


def _chip_start(name, parts):
    n = len(parts)
    ns = 3 * n

    def body(*refs):
        send, recv = refs[2 * n:2 * n + ns], refs[2 * n + ns:2 * n + 2 * ns]
        src = refs[2 * n + 2 * ns:3 * n + 2 * ns]
        land = refs[3 * n + 2 * ns:4 * n + 2 * ns]
        token = refs[4 * n + 2 * ns]
        x, y, c, chips = _place()
        for a in range(n):
            for j, (cx, cy) in enumerate(chips):
                _remote(src[a].at[j], land[a].at[j], send[3 * a + j], recv[3 * a + j], (cx, cy, c)).start()
        token[...] = jnp.zeros_like(token)

    res = pl.pallas_call(
        body, name=name, in_specs=[HBM] * (2 * n),
        out_specs=[SEM] * (2 * ns) + [HBM] * (2 * n) + [pl.BlockSpec(memory_space=pltpu.VMEM)],
        out_shape=[pltpu.SemaphoreType.DMA(())] * (2 * ns) + [pltpu.HBM(p.shape, p.dtype) for p in parts] * 2
        + [jax.ShapeDtypeStruct((8, 128), F32)],
        input_output_aliases={a: 2 * ns + a for a in range(2 * n)},
        compiler_params=pltpu.CompilerParams(has_side_effects=EFFECT),
    )(*[_hbm(p) for p in parts], *[_hbm(lax.empty(p.shape, p.dtype)) for p in parts])
    return list(res[:ns]), list(res[ns:2 * ns]), list(res[2 * ns:2 * ns + n]), list(res[2 * ns + n:2 * ns + 2 * n]), res[2 * ns + 2 * n]


def _chip_wait(name, send, recv, parts, lands, after):
    n = len(parts)
    ns = 3 * n

    def body(*refs):
        src, land = refs[:n], refs[n:2 * n]
        send_ref, recv_ref = refs[2 * n:2 * n + ns], refs[2 * n + ns:2 * n + 2 * ns]
        x, y, c, _ = _place()
        for a in range(n):
            for j in range(3):
                cp = _remote(src[a].at[j], land[a].at[j], send_ref[3 * a + j], recv_ref[3 * a + j], (x, y, c))
                cp.wait_send()
                cp.wait_recv()

    res = pl.pallas_call(
        body, name=name, in_specs=[HBM] * (2 * n) + [SEM] * (2 * ns) + [pl.BlockSpec(memory_space=pl.ANY)],
        out_specs=[HBM] * (2 * n), out_shape=[pltpu.HBM(p.shape, p.dtype) for p in parts] * 2,
        input_output_aliases={a: a for a in range(2 * n)},
        compiler_params=pltpu.CompilerParams(has_side_effects=EFFECT),
    )(*parts, *lands, *send, *recv, after)
    return list(res[n:])


def _share_start(name, shards):
    n = len(shards)

    def body(*refs):
        send, recv = refs[n:2 * n], refs[2 * n:3 * n]
        buf = refs[3 * n:4 * n]
        token = refs[4 * n]
        x, y, c, _ = _place()
        for a in range(n):
            half = buf[a].shape[0] // 2
            mine = buf[a].at[pl.ds(c * half, half)]
            _remote(mine, mine, send[a], recv[a], (x, y, 1 - c)).start()
        token[...] = jnp.zeros_like(token)

    res = pl.pallas_call(
        body, name=name, in_specs=[HBM] * n,
        out_specs=[SEM] * (2 * n) + [HBM] * n + [pl.BlockSpec(memory_space=pltpu.VMEM)],
        out_shape=[pltpu.SemaphoreType.DMA(())] * (2 * n) + [pltpu.HBM(s.shape, s.dtype) for s in shards]
        + [jax.ShapeDtypeStruct((8, 128), F32)],
        input_output_aliases={a: 2 * n + a for a in range(n)},
        compiler_params=pltpu.CompilerParams(has_side_effects=EFFECT),
    )(*[_hbm(s) for s in shards])
    return list(res[:n]), list(res[n:2 * n]), list(res[2 * n:3 * n]), res[3 * n]


def _share_wait(name, send, recv, shards, after):
    n = len(shards)

    def body(*refs):
        buf = refs[:n]
        send_ref, recv_ref = refs[n:2 * n], refs[2 * n:3 * n]
        x, y, c, _ = _place()
        for a in range(n):
            half = buf[a].shape[0] // 2
            cp = _remote(buf[a].at[pl.ds(c * half, half)], buf[a].at[pl.ds((1 - c) * half, half)], send_ref[a], recv_ref[a], (x, y, c))
            cp.wait_send()
            cp.wait_recv()

    res = pl.pallas_call(
        body, name=name, in_specs=[HBM] * n + [SEM] * (2 * n) + [pl.BlockSpec(memory_space=pl.ANY)], out_specs=[HBM] * n,
        out_shape=[pltpu.HBM(s.shape, s.dtype) for s in shards], input_output_aliases={a: a for a in range(n)},
        compiler_params=pltpu.CompilerParams(has_side_effects=EFFECT),
    )(*shards, *send, *recv, after)
    return list(res)


def _small_start(buf):
    R, W = buf.shape
    ns = N_DEV - 1

    def body(*refs):
        send, recv = refs[2:2 + ns], refs[2 + ns:2 + 2 * ns]
        src, land, token = refs[2 + 2 * ns], refs[3 + 2 * ns], refs[4 + 2 * ns]
        x, y, c, _ = _place()
        me = 4 * x + 2 * y + c
        for k in range(1, N_DEV):
            peer = (x ^ (k >> 2), y ^ ((k >> 1) & 1), c ^ (k & 1))
            _remote(src, land.at[me], send[k - 1], recv[k - 1], peer).start()
        token[...] = jnp.zeros_like(token)

    res = pl.pallas_call(
        body, name="small_start", in_specs=[HBM, HBM],
        out_specs=[SEM] * (2 * ns) + [HBM, HBM, pl.BlockSpec(memory_space=pltpu.VMEM)],
        out_shape=[pltpu.SemaphoreType.DMA(())] * (2 * ns) + [pltpu.HBM((R, W), F32), pltpu.HBM((N_DEV, R, W), F32),
                                                                jax.ShapeDtypeStruct((8, 128), F32)],
        input_output_aliases={0: 2 * ns, 1: 2 * ns + 1},
        compiler_params=pltpu.CompilerParams(has_side_effects=EFFECT),
    )(_hbm(buf), _hbm(jnp.zeros((N_DEV, R, W), F32)))
    return list(res[:ns]), list(res[ns:2 * ns]), res[2 * ns], res[2 * ns + 1], res[2 * ns + 2]


def _small_wait(send, recv, buf, land, after):
    ns = N_DEV - 1

    def body(*refs):
        land_ref = refs[1]
        send_ref, recv_ref = refs[2:2 + ns], refs[2 + ns:2 + 2 * ns]
        x, y, c, _ = _place()
        me = 4 * x + 2 * y + c
        for k in range(1, N_DEV):
            landed = land_ref.at[me ^ k]
            cp = _remote(landed, landed, send_ref[k - 1], recv_ref[k - 1], (x, y, c))
            cp.wait_send()
            cp.wait_recv()

    return pl.pallas_call(
        body, name="small_wait", in_specs=[HBM, HBM] + [SEM] * (2 * ns) + [pl.BlockSpec(memory_space=pl.ANY)],
        out_specs=[HBM, HBM], out_shape=[pltpu.HBM(buf.shape, buf.dtype), pltpu.HBM(land.shape, land.dtype)],
        input_output_aliases={0: 0, 1: 1}, compiler_params=pltpu.CompilerParams(has_side_effects=EFFECT),
    )(buf, land, *send, *recv, after)


def _small_sum(buf, land):
    def body(buf_ref, land_ref, out_ref):
        x, y, c, _ = _place()
        me = 4 * x + 2 * y + c
        total = None
        for d in range(N_DEV):
            term = jnp.where(me == d, buf_ref[...], land_ref[d])
            total = term if total is None else total + term
        out_ref[...] = total

    return pl.pallas_call(body, name="small_sum", out_shape=jax.ShapeDtypeStruct(buf.shape, F32))(buf, land)


def _pair_sum_bf16(name, grad, theirs, ids):
    _, R2, C = theirs.shape
    tr = _tile(R2, 256, 16)
    nrb = R2 // tr

    def body(ids_ref, a_ref, b_ref, o_ref):
        o_ref[...] = (a_ref[...] + b_ref[...]).astype(BF16)

    return pl.pallas_call(
        body, name=name,
        grid_spec=pltpu.PrefetchScalarGridSpec(
            num_scalar_prefetch=1, grid=(3, nrb),
            in_specs=[pl.BlockSpec((None, tr, C), lambda j, i, ids: (ids[3 + j], ids[1] * nrb + i, 0)),
                      pl.BlockSpec((None, tr, C), lambda j, i, ids: (j + 1, i, 0))],
            out_specs=pl.BlockSpec((None, tr, C), lambda j, i, ids: (j, i, 0))),
        out_shape=jax.ShapeDtypeStruct((3, R2, C), BF16), compiler_params=_params(("parallel", "parallel")),
    )(ids, grad, theirs)


def _chip_sum(name, grad, theirs, arrived, ids):
    _, R2, C = theirs.shape
    tr = _tile(R2, 256, 16)
    nrb = R2 // tr

    def body(ids_ref, a_ref, b_ref, r_ref, o_ref):
        tot = a_ref[...] + b_ref[...]
        for j in range(3):
            tot = tot + r_ref[j].astype(F32)
        o_ref[...] = tot

    return pl.pallas_call(
        body, name=name,
        grid_spec=pltpu.PrefetchScalarGridSpec(
            num_scalar_prefetch=1, grid=(nrb,),
            in_specs=[pl.BlockSpec((None, tr, C), lambda i, ids: (ids[0], ids[1] * nrb + i, 0)),
                      pl.BlockSpec((None, tr, C), lambda i, ids: (0, i, 0)),
                      pl.BlockSpec((3, tr, C), lambda i, ids: (0, i, 0))],
            out_specs=pl.BlockSpec((tr, C), lambda i, ids: (ids[1] * nrb + i, 0))),
        out_shape=jax.ShapeDtypeStruct((2 * R2, C), F32), compiler_params=_params(("parallel",)),
    )(ids, grad, theirs, arrived)


def _adamw(name, w, g, m, v, emit_grad=False):
    R, C = w.shape
    tr = _tile(R, 128, 8)
    c1 = 1.0 / (1.0 - ADAM_B1 ** ADAM_STEP)
    c2 = 1.0 / (1.0 - ADAM_B2 ** ADAM_STEP)
    n_out = 4 if emit_grad else 3

    def body(w_ref, g_ref, m_ref, v_ref, d_ref, mo_ref, vo_ref, *rest):
        gv = g_ref[...]
        mn = ADAM_B1 * m_ref[...] + (1.0 - ADAM_B1) * gv
        vn = ADAM_B2 * v_ref[...] + (1.0 - ADAM_B2) * (gv * gv)
        d_ref[...] = -ADAM_LR * ((mn * c1) / (jnp.sqrt(vn * c2) + ADAM_EPS) + ADAM_WD * w_ref[...])
        mo_ref[...] = mn
        vo_ref[...] = vn
        if emit_grad:
            rest[0][...] = gv

    spec = pl.BlockSpec((tr, C), lambda i: (i, 0))
    sds = jax.ShapeDtypeStruct((R, C), F32)
    return pl.pallas_call(body, name=name, grid=(R // tr,), in_specs=[spec] * 4, out_specs=[spec] * n_out, out_shape=[sds] * n_out,
                          compiler_params=_params(("parallel",)))(w, g, m, v)


SMALL = ["ffn1_norm", "mix_norm", "mem_norm", "forget_bias", "fox_q_gain", "fox_k_gain", "swa_q_gain", "swa_k_gain", "swa_sinks",
         "mem_q_gain", "mem_k_gain", "ffn2_norm"]
LARGE = ["ffn1_gate", "ffn1_up", "ffn1_down", "w_in", "w_mem_k", "w_mem_v", "w_out", "ffn2_gate", "ffn2_up", "ffn2_down"]
GATHER_GROUPS = [["ffn1_gate", "ffn1_up"], ["ffn1_down", "w_in", "w_mem_k", "w_mem_v"], ["w_out", "ffn2_gate", "ffn2_up", "ffn2_down"]]
WEIGHTS = ["ffn1_norm", "ffn1_gate", "ffn1_up", "ffn1_down", "mix_norm", "mem_norm", "w_in", "forget_bias", "w_mem_k", "w_mem_v",
           "fox_q_gain", "fox_k_gain", "swa_q_gain", "swa_k_gain", "swa_sinks", "mem_q_gain", "mem_k_gain", "w_out", "ffn2_norm",
           "ffn2_gate", "ffn2_up", "ffn2_down"]


def _pad_proj_cols(w):
    out = jnp.zeros((w.shape[0], PROJ_W), w.dtype)
    for start, width, pstart in REF_GROUPS:
        out = lax.dynamic_update_slice(out, w[:, start:start + width], (0, pstart))
    return out


def _unpad_proj_cols(w):
    return jnp.concatenate([w[:, pstart:pstart + width] for _, width, pstart in REF_GROUPS], axis=1)


def _pack_small(vals):
    flat = jnp.concatenate([vals[k].reshape(-1).astype(F32) for k in SMALL + ["loss"]])
    n = flat.shape[0]
    total = -(-n // 1024) * 1024
    return jnp.pad(flat, (0, total - n)).reshape(total // 128, 128)


def _unpack_small(buf, shapes):
    flat = buf.reshape(-1)
    out, off = {}, 0
    for k in SMALL + ["loss"]:
        size = int(np.prod(shapes[k]))
        out[k] = flat[off:off + size].reshape(shapes[k])
        off += size
    return out


def kernel(x, mem, ffn1_norm, ffn1_gate, ffn1_up, ffn1_down, mix_norm, mem_norm, w_in, forget_bias, w_mem_k, w_mem_v, fox_q_gain, fox_k_gain, swa_q_gain, swa_k_gain, swa_sinks, mem_q_gain, mem_k_gain, w_out, ffn2_norm, ffn2_gate, ffn2_up, ffn2_down, loss_target, m_ffn1_norm, m_ffn1_gate, m_ffn1_up, m_ffn1_down, m_mix_norm, m_mem_norm, m_w_in, m_forget_bias, m_w_mem_k, m_w_mem_v, m_fox_q_gain, m_fox_k_gain, m_swa_q_gain, m_swa_k_gain, m_swa_sinks, m_mem_q_gain, m_mem_k_gain, m_w_out, m_ffn2_norm, m_ffn2_gate, m_ffn2_up, m_ffn2_down, v_ffn1_norm, v_ffn1_gate, v_ffn1_up, v_ffn1_down, v_mix_norm, v_mem_norm, v_w_in, v_forget_bias, v_w_mem_k, v_w_mem_v, v_fox_q_gain, v_fox_k_gain, v_swa_q_gain, v_swa_k_gain, v_swa_sinks, v_mem_q_gain, v_mem_k_gain, v_w_out, v_ffn2_norm, v_ffn2_gate, v_ffn2_up, v_ffn2_down):
    given = dict(locals())
    T, D = x.shape[1], x.shape[2]
    ML = mem.shape[1]
    xin = x.reshape(T, D)
    target = loss_target.reshape(T, D)
    memin = mem.reshape(ML, D)

    ids = _place_ids()
    shard = {k: given[k][0] for k in LARGE}
    started, after = [], ids
    for gi, group in enumerate(GATHER_GROUPS):
        also = {}
        if "w_in" in group:
            tied = lax.optimization_barrier((given["w_in"], given["m_w_in"], given["v_w_in"], after))
            w_in_rows = tuple(t[0] for t in tied[:3])
            shard["w_in"] = _pad_proj_cols(w_in_rows[0])
            also["w_in"] = w_in_rows[1:]
        placed = [_cast_place("cast_" + k, shard[k], ids, after, *also.get(k, ())) for k in group]
        send, recv, bufs, after = _gather_start("gather_start_%d" % gi, placed, after)
        started.append((send, recv, bufs))

    def arrive(gi, done):
        send, recv, bufs = started[gi]
        bufs = _gather_wait("gather_wait_%d" % gi, send, recv, bufs, done)
        return dict(zip(GATHER_GROUPS[gi], _gather_forward("gather_forward_%d" % gi, bufs)))

    gains = jnp.concatenate([fox_q_gain, fox_k_gain, swa_q_gain, swa_k_gain, mem_q_gain,
                             jnp.pad(forget_bias, ((0, 0), (0, HEAD_DIM - FOX_HEADS))), jnp.zeros((2, HEAD_DIM), F32)], axis=0)
    slopes_np = 2.0 ** (-8.0 * np.arange(1, SWA_HEADS + 1) / SWA_HEADS)
    slopes = jnp.asarray(np.repeat(slopes_np, WINDOW).reshape(SWA_KV_HEADS, GR, 1), F32)
    sinks = jnp.repeat(swa_sinks.reshape(SWA_HEADS), WINDOW).reshape(SWA_KV_HEADS, GR, 1)

    h1 = _rms_fwd("ffn1_norm_fwd", xin, ffn1_norm + after[0, 0])
    full = arrive(0, h1)
    wg1, wu1 = full["ffn1_gate"], full["ffn1_up"]
    fg1, fu1, a1 = _ffn_gu("ffn1_gate_up", h1, wg1, wu1)
    full = arrive(1, a1)
    wd1 = full["ffn1_down"].reshape(-1, D)
    win = full["w_in"].reshape(D, PROJ_W)
    wmk = full["w_mem_k"].reshape(D, MEM_HEADS * HEAD_DIM)
    wmv = full["w_mem_v"].reshape(D, MEM_HEADS * HEAD_DIM)
    x1, h2 = _residual_norm("ffn1_down", a1, wd1, xin, 0.5, mix_norm, 256)
    proj = _mm2d("proj_in", h2, win, "nn", F32, tn=1408, tk=2048, n_outer=True)
    qkv, logf, k_t, v_t = _prep_fwd(proj, gains)
    cum = _cumsum_rows("forget_cumsum", [logf], False)
    cum_h = cum[:, :FOX_HEADS].T
    cq_row = cum_h.reshape(FOX_HEADS, 1, T)
    ck_rep = jnp.broadcast_to(cum_h[:, :, None], (FOX_HEADS, T, HEAD_DIM))
    mn = _rms_fwd("mem_norm_fwd", memin, mem_norm)
    mk_raw = _mm2d("mem_k_proj", mn, wmk, "nn", F32)
    mv = _mm2d("mem_v_proj", mn, wmv, "nn", BF16)
    mk = _head_norm_rows(mk_raw, mem_k_gain)
    out_a, out_a_f32, lse_a = _fox_fwd(qkv, v_t, cq_row, ck_rep)
    send, recv, bufs = started[2]
    bufs = _gather_wait("gather_wait_2", send, recv, bufs, out_a)
    send, recv, bufs, token = _gather_start("gather_pass_start_2", bufs, out_a, to_sibling=True)
    out_b, lse_b = _swa_fwd(qkv, slopes + token[0, 0], sinks)
    out_c, lse_c = _mem_fwd(qkv, mk, mv)
    mixed = jnp.concatenate([out_a, out_b, out_c], axis=1)
    full = dict(zip(GATHER_GROUPS[2], _gather_wait("gather_pass_wait_2", send, recv, bufs, mixed, to_sibling=True)))
    wo = full["w_out"].reshape(-1, D)
    wg2, wu2, wd2 = full["ffn2_gate"], full["ffn2_up"], full["ffn2_down"].reshape(-1, D)
    x2, h3 = _residual_norm("mix_out", mixed, wo, x1, 1.0, ffn2_norm, 512)
    fg2, fu2, a2 = _ffn_gu("ffn2_gate_up", h3, wg2, wu2)
    dx3, dyb3, loss_blocks = _ffn_down_loss("ffn2_down", a2, wd2, x2, target)

    grads, small, res = {}, {"loss": jnp.sum(loss_blocks[::8, 0])}, {}

    def pair_off(tag, group):
        send, recv, own, lands, token = _pair_start("grad_pair_start_" + tag, [grads[k] for k in group])
        return (group, send, recv, own, lands), token

    def chip_off(tag, started, done):
        group, send, recv, own, lands = started
        own, theirs = _pair_wait("grad_pair_wait_" + tag, send, recv, own, lands, done)
        grads.update(zip(group, own))
        to_chips = [_pair_sum_bf16("pair_sum_" + k, grads[k], b, ids) for k, b in zip(group, theirs)]
        send, recv, parts, lands, token = _chip_start("grad_chip_start_" + tag, to_chips)
        return (group, theirs, send, recv, parts, lands), token

    def reduce_half(tag, state, done):
        group, theirs, send, recv, parts, lands = state
        arrived = _chip_wait("grad_chip_wait_" + tag, send, recv, parts, lands, done)
        halves = [_chip_sum("chip_sum_" + k, grads[k], b, r, ids) for k, b, r in zip(group, theirs, arrived)]
        send, recv, shards, token = _share_start("grad_share_start_" + tag, halves)
        return (group, send, recv, shards), token

    def update(tag, shared, done):
        group, send, recv, shards = shared
        reduced = dict(zip(group, _share_wait("grad_share_wait_" + tag, send, recv, shards, done)))
        last = None
        for k in group:
            if k == "w_in":
                gk = _unpad_proj_cols(reduced[k])
                d, mo, vo = _adamw("adamw_" + k, w_in_rows[0], gk, w_in_rows[1], w_in_rows[2])
            else:
                d, mo, vo, gk = _adamw("adamw_" + k, given[k][0], reduced[k], given["m_" + k][0], given["v_" + k][0], emit_grad=True)
            res[k] = tuple(t[None] for t in (gk, d, mo, vo))
            last = vo
        return last

    dg2, du2 = _ffn_bwd_act("ffn2", dyb3, wd2, fg2, fu2, N_CHIPS)
    grads["ffn2_down"] = _ffn_bwd_down("ffn2", a2, dyb3, N_CHIPS).reshape(N_CHIPS, -1, D)
    grads["ffn2_gate"], grads["ffn2_up"] = _ffn_bwd_gate_up("ffn2", h3, dg2, du2, N_CHIPS)
    started, token = pair_off("a", ["ffn2_gate", "ffn2_up", "ffn2_down"])
    dh3 = _ffn_bwd_x("ffn2", dg2, du2, wg2, wu2, token)
    state_a, token = chip_off("a", started, dh3)
    dx2, dx2b, small["ffn2_norm"] = _rms_bwd("ffn2_norm_bwd", dh3, x2, ffn2_norm + token[0, 0], dx3, 1.0)
    dmix = _mm2d("mix_out_dx", dx2b, wo, "nt", BF16, tk=2048, n_outer=True)
    grads["w_out"] = _mm2d("mix_out_dw", mixed, dx2b, "tn", F32, tk=T, n_outer=True, resident=True).reshape(N_CHIPS, -1, D)
    delta_row = _fox_delta(dmix, out_a_f32)[:, :FOX_HEADS].T.reshape(FOX_HEADS, 1, T)
    dfq, dfk, dfv, dck, dcq = _fox_bwd(qkv, k_t, cq_row, ck_rep, delta_row, lse_a, dmix)
    dsq, dsk, dsv, dsink = _swa_bwd(qkv, slopes, sinks, out_b, lse_b, dmix)
    dmq, dmk, dmv = _mem_bwd(qkv, mk, mv, out_c, lse_c, dmix)
    small["swa_sinks"] = dsink[:, :SWA_GROUP, 0].reshape(1, SWA_HEADS)
    dcum = jnp.pad(dcq.reshape(FOX_HEADS, T).T, ((0, 0), (0, HEAD_DIM - FOX_HEADS)))
    dlogf = _cumsum_rows("forget_cumsum_bwd", [dcum], True, columns=dck)
    dproj, dgains = _prep_bwd(proj, gains, dfq, dfk, dfv, dsq, dsk, dsv, dmq, dlogf)
    for row, k in enumerate(["fox_q_gain", "fox_k_gain", "swa_q_gain", "swa_k_gain", "mem_q_gain"]):
        small[k] = dgains[row:row + 1, :]
    small["forget_bias"] = dgains[5:6, :FOX_HEADS]
    grads["w_in"] = _mm2d("proj_in_dw", h2, dproj, "tn", F32, tn=1408, tk=T, n_outer=True, resident=True).reshape(N_CHIPS, -1, PROJ_W)
    dmk_raw, small["mem_k_gain"] = _head_norm_rows_bwd(mk_raw, mem_k_gain, dmk)
    dmvb = dmv.astype(BF16)
    grads["w_mem_k"] = _mm2d("mem_k_dw", mn, dmk_raw, "tn", F32).reshape(N_CHIPS, -1, MEM_HEADS * HEAD_DIM)
    grads["w_mem_v"] = _mm2d("mem_v_dw", mn, dmvb, "tn", F32).reshape(N_CHIPS, -1, MEM_HEADS * HEAD_DIM)
    dmn = _mm2d("mem_k_dx", dmk_raw, wmk, "nt", F32)
    dmn = _mm2d("mem_v_dx", dmvb, wmv, "nt", F32, extras=[dmn], epilogue=lambda accs, ex: [ex[0] + accs[0]])
    _, _, small["mem_norm"] = _rms_bwd("mem_norm_bwd", dmn, memin, mem_norm, jnp.zeros_like(memin), 1.0)
    started, token = pair_off("b", ["w_out", "w_in", "w_mem_k", "w_mem_v"])
    dh2 = _mm2d("proj_in_dx", dproj, win, "nt", F32, tm=512, tn=1024, tk=PROJ_W, n_outer=True, resident=True, after=token)
    state_b, token = chip_off("b", started, dh2)
    dx1, dyb1, small["mix_norm"] = _rms_bwd("mix_norm_bwd", dh2, x1, mix_norm + token[0, 0], dx2, 0.5)
    grads["ffn1_down"] = _ffn_bwd_down("ffn1", a1, dyb1, N_CHIPS).reshape(N_CHIPS, -1, D)
    started, token = pair_off("c", ["ffn1_down"])
    dg1, du1 = _ffn_bwd_act("ffn1", dyb1, wd1, fg1, fu1, N_CHIPS, after=token)
    state_c, token = chip_off("c", started, dg1)
    grads["ffn1_gate"], grads["ffn1_up"] = _ffn_bwd_gate_up("ffn1", h1, dg1, du1, N_CHIPS, after=token)
    started, token = pair_off("d", ["ffn1_gate", "ffn1_up"])
    dh1 = _ffn_bwd_x("ffn1", dg1, du1, wg1, wu1, token)
    state_d, token = chip_off("d", started, dh1)
    grad_x, _, small["ffn1_norm"] = _rms_bwd("ffn1_norm_bwd", dh1, xin, ffn1_norm + token[0, 0], dx1, 1.0)

    s_send, s_recv, s_buf, s_land, token = _small_start(_pack_small(small))

    shared_a, token = reduce_half("a", state_a, token)
    shared_b, token = reduce_half("b", state_b, token)
    done = update("a", shared_a, token)
    shared_c, token = reduce_half("c", state_c, done)
    done = update("b", shared_b, token)
    shared_d, token = reduce_half("d", state_d, done)
    done = update("c", shared_c, token)
    done = update("d", shared_d, done)

    shapes = {k: given[k].shape for k in SMALL}
    shapes["loss"] = ()
    s_buf, s_land = _small_wait(s_send, s_recv, s_buf, s_land, done)
    red_small = _unpack_small(_small_sum(s_buf, s_land), shapes)
    loss = red_small["loss"]
    zero = {"loss": jnp.zeros((), F32)}
    packed = [_pack_small({**zero, **{k: src[k] for k in SMALL}}) for src in (
        {k: given[k] for k in SMALL}, red_small, {k: given["m_" + k] for k in SMALL}, {k: given["v_" + k] for k in SMALL})]
    d_s, m_s, v_s = (_unpack_small(t, shapes) for t in _adamw("adamw_small", *packed))
    for k in SMALL:
        res[k] = (red_small[k], d_s[k], m_s[k], v_s[k])

    outs = [loss, grad_x.reshape(1, T, D)]
    for part in range(4):
        outs += [res[k][part] for k in WEIGHTS]
    return tuple(outs)
```

```python
import functools

import numpy as np
import jax
import jax.numpy as jnp
from jax import lax
from jax.experimental import pallas as pl
from jax.experimental.pallas import tpu as pltpu

F32 = jnp.float32
BF16 = jnp.bfloat16
MESH = pl.DeviceIdType.MESH

HEAD_DIM = 128
FOX_HEADS = 6
SWA_HEADS = 6
SWA_KV_HEADS = 2
SWA_GROUP = SWA_HEADS // SWA_KV_HEADS
MEM_HEADS = 4
WINDOW = 128
EPS = 1e-6
NEG_INF = -1e30
SCALE = HEAD_DIM ** -0.5

C_FQ = 0
C_FK = C_FQ + FOX_HEADS * HEAD_DIM
C_FV = C_FK + FOX_HEADS * HEAD_DIM
C_SQ = C_FV + FOX_HEADS * HEAD_DIM
C_SK = C_SQ + SWA_HEADS * HEAD_DIM
C_SV = C_SK + SWA_KV_HEADS * HEAD_DIM
C_MQ = C_SV + SWA_KV_HEADS * HEAD_DIM
C_FL = C_MQ + MEM_HEADS * HEAD_DIM
PROJ_W = C_FL + HEAD_DIM
FOX_W = FOX_HEADS * HEAD_DIM
REF_GROUPS = [
    (0, FOX_W, C_FQ), (FOX_W, FOX_W, C_FK), (2 * FOX_W, FOX_W, C_FV), (3 * FOX_W, FOX_HEADS, C_FL),
    (3 * FOX_W + FOX_HEADS, SWA_HEADS * HEAD_DIM, C_SQ),
    (3 * FOX_W + FOX_HEADS + SWA_HEADS * HEAD_DIM, SWA_KV_HEADS * HEAD_DIM, C_SK),
    (3 * FOX_W + FOX_HEADS + (SWA_HEADS + SWA_KV_HEADS) * HEAD_DIM, SWA_KV_HEADS * HEAD_DIM, C_SV),
    (3 * FOX_W + FOX_HEADS + (SWA_HEADS + 2 * SWA_KV_HEADS) * HEAD_DIM, MEM_HEADS * HEAD_DIM, C_MQ),
]

ADAM_LR = 0.001
ADAM_B1 = 0.9
ADAM_B2 = 0.999
ADAM_EPS = 1e-08
ADAM_WD = 0.01
ADAM_STEP = 10

V7X_VMEM_LIMIT = 56 * 1024 * 1024
N_CHIPS = 4
N_DEV = 8


def _tile(n, pref, mult=128):
    t = (min(pref, n) // mult) * mult
    while t >= mult:
        if n % t == 0:
            return t
        t -= mult
    return n


def _params(sem):
    return pltpu.CompilerParams(dimension_semantics=sem, vmem_limit_bytes=V7X_VMEM_LIMIT)


_DIMS = {"nn": (((1,), (0,)), ((), ())), "nt": (((1,), (1,)), ((), ())), "tn": (((0,), (0,)), ((), ()))}


def _dot(a, b, mode):
    return lax.dot_general(a, b, _DIMS[mode], preferred_element_type=F32)


def _mm(name, grid, pairs, acc_of, acc_shapes, extras, outs, epilogue, after=None):
    n_p, n_e, n_o, n_a = len(pairs), len(extras), len(outs), len(acc_shapes)
    n_w = 0 if after is None else 1
    nk = grid[2]
    n_in = sum(1 if a is None else 2 for a, *_ in pairs)

    def body(*refs):
        ex = refs[n_in:n_in + n_e]
        out = refs[n_in + n_e + n_w:n_in + n_e + n_w + n_o]
        accs = refs[n_in + n_e + n_w + n_o:]
        parts = [None] * n_a
        at = 0
        for p in range(n_p):
            if pairs[p][0] is None:
                a_ref, b_ref = refs[0], refs[at]
                at += 1
            else:
                a_ref, b_ref = refs[at], refs[at + 1]
                at += 2
            d = _dot(a_ref[...], b_ref[...], pairs[p][4])
            parts[acc_of[p]] = d if parts[acc_of[p]] is None else parts[acc_of[p]] + d

        def finish(vals):
            for o, r in zip(out, epilogue(vals, [e[...] for e in ex])):
                o[...] = r.astype(o.dtype)

        if nk == 1:
            finish(parts)
            return
        k = pl.program_id(2)

        @pl.when(k == 0)
        def _():
            for a, d in zip(accs, parts):
                a[...] = d

        @pl.when((k > 0) & (k < nk - 1))
        def _():
            for a, d in zip(accs, parts):
                a[...] += d

        @pl.when(k == nk - 1)
        def _():
            finish([a[...] + d for a, d in zip(accs, parts)])

    in_specs, args = [], []
    for a, a_spec, b, b_spec, _ in pairs:
        if a is not None:
            in_specs.append(a_spec)
            args.append(a)
        in_specs.append(b_spec)
        args.append(b)
    for e, e_spec in extras:
        in_specs.append(e_spec)
        args.append(e)
    if after is not None:
        in_specs.append(pl.BlockSpec(memory_space=pl.ANY))
        args.append(after)
    res = pl.pallas_call(
        body, name=name, grid=grid, in_specs=in_specs,
        out_specs=[s for _, s in outs], out_shape=[o for o, _ in outs],
        scratch_shapes=[pltpu.VMEM(s, F32) for s in acc_shapes] if nk > 1 else [],
        compiler_params=_params(("parallel", "parallel", "arbitrary")),
    )(*args)
    return res


def _mm2d(name, a, b, mode, out_dtype, tm=512, tn=1024, tk=1024, extras=(), epilogue=None, n_out=1, after=None, n_outer=False,
          resident=False, rows=(), tile_stat=False):
    if mode == "nn":
        (M, K), N = a.shape, b.shape[1]
    elif mode == "nt":
        (M, K), N = a.shape, b.shape[0]
    else:
        (K, M), N = a.shape, b.shape[1]
    tm, tn, tk = _tile(M, tm), _tile(N, tn), _tile(K, tk)
    assert not resident or tk == K

    def spec(shape, index, single=False):
        mode_kw = {"pipeline_mode": pl.Buffered(1)} if single else {}
        if n_outer:
            return pl.BlockSpec(shape, lambda j, i, k: index(i, j, k), **mode_kw)
        return pl.BlockSpec(shape, index, **mode_kw)

    single_a, single_b = resident and not n_outer, resident and n_outer
    a_spec = spec((tk, tm), lambda i, j, k: (k, i), single_a) if mode == "tn" else spec((tm, tk), lambda i, j, k: (i, k), single_a)
    b_spec = spec((tn, tk), lambda i, j, k: (j, k), single_b) if mode == "nt" else spec((tk, tn), lambda i, j, k: (k, j), single_b)
    mn = spec((tm, tn), lambda i, j, k: (i, j))
    if epilogue is None:
        epilogue = lambda accs, ex: [accs[0]]
    if not isinstance(out_dtype, (list, tuple)):
        out_dtype = [out_dtype] * n_out
    grid = (N // tn, M // tm, K // tk) if n_outer else (M // tm, N // tn, K // tk)
    outs = [(jax.ShapeDtypeStruct((M, N), d), mn) for d in out_dtype]
    if tile_stat:
        assert tn == N
        outs.append((jax.ShapeDtypeStruct((8 * (M // tm), 128), F32), spec((8, 128), lambda i, j, k: (i, 0))))
    res = _mm(name, grid, [(a, a_spec, b, b_spec, mode)], [0], [(tm, tn)],
              [(e, mn) for e in extras] + [(r, spec((1, tn), lambda i, j, k: (0, j))) for r in rows], outs, epilogue, after=after)
    return res[0] if len(res) == 1 else res


def _sigmoid(x):
    return 1.0 / (1.0 + jnp.exp(-x))


def _sigmoid_fast(x):
    return pl.reciprocal(1.0 + jnp.exp(-x), approx=True)


def _ffn_gu(name, h, wg, wu):
    T, D = h.shape
    nf, _, F4 = wg.shape
    tm, tk = _tile(T, 512), _tile(D, 2048)
    a_spec = pl.BlockSpec((tm, tk), lambda j, i, k: (i, k))
    b_spec = pl.BlockSpec((None, tk, F4), lambda j, i, k: (j, k, 0))
    o_spec = pl.BlockSpec((tm, F4), lambda j, i, k: (i, j))

    def epilogue(accs, ex):
        g, u = accs
        s = _sigmoid_fast(g)
        gs = g * s
        return [(s + s * (g - gs)) * u, gs, gs * u]

    sds = jax.ShapeDtypeStruct((T, nf * F4), BF16)
    return _mm(name, (nf, T // tm, D // tk), [(h, a_spec, wg, b_spec, "nn"), (None, None, wu, b_spec, "nn")], [0, 1],
               [(tm, F4), (tm, F4)], [], [(sds, o_spec)] * 3, epilogue)


def _ffn_gu_blocks(name, h, wg, wu, blocks, n_total, prev=None):
    T, D = h.shape
    F4 = wg.shape[2]
    tm = _tile(T, 512)

    def body(tab, h_ref, wg_ref, wu_ref, *rest):
        out = rest[-3:]
        hv = h_ref[...]
        g, u = _dot(hv, wg_ref[...], "nn"), _dot(hv, wu_ref[...], "nn")
        s = _sigmoid_fast(g)
        gs = g * s
        out[0][...] = ((s + s * (g - gs)) * u).astype(BF16)
        out[1][...] = gs.astype(BF16)
        out[2][...] = (gs * u).astype(BF16)

    w_spec = pl.BlockSpec((None, D, F4), lambda j, i, tab: (tab[0, j], 0, 0))
    o_spec = pl.BlockSpec((tm, F4), lambda j, i, tab: (i, tab[1, j]))
    filled = [] if prev is None else list(prev)
    sds = jax.ShapeDtypeStruct((T, n_total * F4), BF16)
    return pl.pallas_call(
        body, name=name,
        grid_spec=pltpu.PrefetchScalarGridSpec(
            num_scalar_prefetch=1, grid=(blocks.shape[1], T // tm),
            in_specs=[pl.BlockSpec((tm, D), lambda j, i, tab: (i, 0)), w_spec, w_spec] + [pl.BlockSpec(memory_space=pl.ANY)] * len(filled),
            out_specs=[o_spec] * 3),
        out_shape=[sds] * 3, input_output_aliases={4 + t: t for t in range(len(filled))},
        compiler_params=_params(("parallel", "parallel")),
    )(blocks, h, wg, wu, *filled)


def _rms_rows(x, gain):
    return x * lax.rsqrt(jnp.mean(x * x, axis=-1, keepdims=True) + EPS) * gain


def _residual_norm(name, a, w, xres, scale, gain, tm):
    def epilogue(accs, ex):
        y = ex[0] + scale * accs[0]
        return [y, _rms_rows(y, ex[1])]

    return _mm2d(name, a, w, "nn", [F32, BF16], tm=tm, tn=w.shape[1], tk=w.shape[0], n_outer=True, resident=True, extras=[xres],
                 rows=[gain], epilogue=epilogue)


def _ffn_down_loss(name, a, wd, xres, target):
    D = wd.shape[1]

    def epilogue(accs, ex):
        e = ex[0] + 0.5 * accs[0] - ex[1]
        d = e * (1.0 / D)
        return [d, 0.5 * d, jnp.zeros((8, 128), F32) + (0.5 / D) * jnp.sum(e * e)]

    return _mm2d(name, a, wd, "nn", [F32, BF16], tm=256, tn=D, tk=wd.shape[0], n_outer=True, resident=True, extras=[xres, target],
                 tile_stat=True, epilogue=epilogue)


def _ffn_bwd_down(tag, a, dyb, nf, after=None):
    return _mm2d(tag + "_dwd", a, dyb, "tn", F32, tm=a.shape[1] // nf, tn=1024, tk=a.shape[0], resident=True, after=after)


def _ffn_bwd_act(tag, dyb, wd, da_dg, da_du, nf, after=None):
    def act_bwd(accs, ex):
        return [accs[0] * ex[0].astype(F32), accs[0] * ex[1].astype(F32)]

    return _mm2d(tag + "_da", dyb, wd, "nt", BF16, tn=wd.shape[0] // nf, tk=2048, extras=[da_dg, da_du], epilogue=act_bwd, n_out=2,
                 n_outer=True, after=after)


def _ffn_bwd_gate_up(tag, h, dg, du, nf, after=None):
    T, D = h.shape
    F4 = dg.shape[1] // nf
    tm = _tile(D, 512)
    h_spec = pl.BlockSpec((T, tm), lambda j, i, k: (0, i))
    d_spec = pl.BlockSpec((T, F4), lambda j, i, k: (0, j), pipeline_mode=pl.Buffered(1))
    w_spec = pl.BlockSpec((None, tm, F4), lambda j, i, k: (j, i, 0))
    sds = jax.ShapeDtypeStruct((nf, D, F4), F32)
    return _mm(tag + "_dwgu", (nf, D // tm, 1), [(h, h_spec, dg, d_spec, "tn"), (None, None, du, d_spec, "tn")],
               [0, 1], [(tm, F4), (tm, F4)], [], [(sds, w_spec)] * 2, lambda accs, ex: accs, after=after)


def _ffn_bwd_x(tag, dg, du, wg, wu, after):
    T = dg.shape[0]
    nf, D, F4 = wg.shape
    tm, tn = _tile(T, 256), _tile(D, 1024)

    def body(dg_ref, du_ref, wg_ref, wu_ref, after_ref, o_ref):
        acc = None
        for j in range(nf):
            cols = slice(j * F4, (j + 1) * F4)
            part = _dot(dg_ref[:, cols], wg_ref[j], "nt") + _dot(du_ref[:, cols], wu_ref[j], "nt")
            acc = part if acc is None else acc + part
        o_ref[...] = acc

    a_spec = pl.BlockSpec((tm, nf * F4), lambda n, i: (i, 0))
    b_spec = pl.BlockSpec((nf, tn, F4), lambda n, i: (0, n, 0), pipeline_mode=pl.Buffered(1))
    return pl.pallas_call(
        body, name=tag + "_dh", grid=(D // tn, T // tm),
        in_specs=[a_spec, a_spec, b_spec, b_spec, pl.BlockSpec(memory_space=pl.ANY)],
        out_specs=pl.BlockSpec((tm, tn), lambda n, i: (i, n)), out_shape=jax.ShapeDtypeStruct((T, D), F32),
        compiler_params=_params(("parallel", "parallel")),
    )(dg, du, wg, wu, after)


def _rms_fwd(name, x, gain):
    R, D = x.shape
    tr = _tile(R, 256, 8)

    def body(x_ref, g_ref, o_ref):
        xv = x_ref[...]
        r = lax.rsqrt(jnp.mean(xv * xv, axis=-1, keepdims=True) + EPS)
        o_ref[...] = (xv * r * g_ref[...]).astype(BF16)

    return pl.pallas_call(
        body, name=name, grid=(R // tr,),
        in_specs=[pl.BlockSpec((tr, D), lambda i: (i, 0)), pl.BlockSpec((1, D), lambda i: (0, 0))],
        out_specs=pl.BlockSpec((tr, D), lambda i: (i, 0)), out_shape=jax.ShapeDtypeStruct((R, D), BF16),
        compiler_params=_params(("parallel",)),
    )(x, gain)


def _rms_bwd(name, dh, x, gain, dres, bscale):
    R, D = x.shape
    tr = _tile(R, 256, 8)

    def body(dh_ref, x_ref, g_ref, dres_ref, dx_ref, dxb_ref, dg_ref):
        xv, dy = x_ref[...], dh_ref[...]
        r = lax.rsqrt(jnp.mean(xv * xv, axis=-1, keepdims=True) + EPS)
        xn = xv * r
        uu = dy * g_ref[...]
        dx = dres_ref[...] + r * (uu - xn * jnp.mean(xn * uu, axis=-1, keepdims=True))
        dx_ref[...] = dx
        dxb_ref[...] = (bscale * dx).astype(BF16)
        part = jnp.sum(dy * xn, axis=0, keepdims=True)

        @pl.when(pl.program_id(0) == 0)
        def _():
            dg_ref[...] = part

        @pl.when(pl.program_id(0) > 0)
        def _():
            dg_ref[...] += part

    row = pl.BlockSpec((tr, D), lambda i: (i, 0))
    vec = pl.BlockSpec((1, D), lambda i: (0, 0))
    return pl.pallas_call(
        body, name=name, grid=(R // tr,), in_specs=[row, row, vec, row], out_specs=[row, row, vec],
        out_shape=[jax.ShapeDtypeStruct((R, D), F32), jax.ShapeDtypeStruct((R, D), BF16), jax.ShapeDtypeStruct((1, D), F32)],
        compiler_params=_params(("arbitrary",)),
    )(dh, x, gain, dres)


def _head_norm(xs, g):
    r = lax.rsqrt(jnp.mean(xs * xs, axis=-1, keepdims=True) + EPS)
    return xs * r * g


def _head_norm_bwd(xs, g, dy):
    r = lax.rsqrt(jnp.mean(xs * xs, axis=-1, keepdims=True) + EPS)
    xn = xs * r
    uu = dy * g
    return r * (uu - xn * jnp.mean(xn * uu, axis=-1, keepdims=True)), jnp.sum(dy * xn, axis=0, keepdims=True)


NORMED = [(C_FQ, FOX_HEADS, 0), (C_FK, FOX_HEADS, 1), (C_SQ, SWA_HEADS, 2), (C_SK, SWA_KV_HEADS, 3), (C_MQ, MEM_HEADS, 4)]
PLAIN = [(C_FV, FOX_HEADS), (C_SV, SWA_KV_HEADS)]


def _prep_fwd(proj, gains):
    T = proj.shape[0]
    tr = _tile(T, 256, 128)

    def body(p_ref, g_ref, o_ref, lf_ref, kt_ref, vt_ref):
        for start, heads, row in NORMED:
            gn = g_ref[row:row + 1, :]
            for hh in range(heads):
                sl = slice(start + hh * HEAD_DIM, start + (hh + 1) * HEAD_DIM)
                y = _head_norm(p_ref[:, sl], gn)
                o_ref[:, sl] = y.astype(BF16)
                if start == C_FK:
                    kt_ref[hh * HEAD_DIM:(hh + 1) * HEAD_DIM, :] = y.T.astype(BF16)
        for start, heads in PLAIN:
            sl = slice(start, start + heads * HEAD_DIM)
            o_ref[:, sl] = p_ref[:, sl].astype(BF16)
        for hh in range(FOX_HEADS):
            sl = slice(C_FV + hh * HEAD_DIM, C_FV + (hh + 1) * HEAD_DIM)
            vt_ref[hh * HEAD_DIM:(hh + 1) * HEAD_DIM, :] = p_ref[:, sl].T.astype(BF16)
        zb = p_ref[:, C_FL:C_FL + HEAD_DIM] + g_ref[5:6, :]
        o_ref[:, C_FL:C_FL + HEAD_DIM] = jnp.zeros((tr, HEAD_DIM), BF16)
        lf_ref[...] = jnp.minimum(zb, 0.0) - jnp.log(1.0 + jnp.exp(-jnp.abs(zb)))

    return pl.pallas_call(
        body, name="prep_fwd", grid=(T // tr,),
        in_specs=[pl.BlockSpec((tr, PROJ_W), lambda i: (i, 0)), pl.BlockSpec((8, 128), lambda i: (0, 0))],
        out_specs=[pl.BlockSpec((tr, PROJ_W), lambda i: (i, 0)), pl.BlockSpec((tr, HEAD_DIM), lambda i: (i, 0)),
                   pl.BlockSpec((FOX_W, tr), lambda i: (0, i)), pl.BlockSpec((FOX_W, tr), lambda i: (0, i))],
        out_shape=[jax.ShapeDtypeStruct((T, PROJ_W), BF16), jax.ShapeDtypeStruct((T, HEAD_DIM), F32),
                   jax.ShapeDtypeStruct((FOX_W, T), BF16), jax.ShapeDtypeStruct((FOX_W, T), BF16)],
        compiler_params=_params(("parallel",)),
    )(proj, gains)


def _prep_bwd(proj, gains, dfq, dfk, dfv, dsq, dsk, dsv, dmq, dlogf):
    T = proj.shape[0]
    tr = _tile(T, 256, 8)
    d_normed = {C_FQ: 0, C_FK: 1, C_SQ: 3, C_SK: 4, C_MQ: 6}
    d_plain = {C_FV: 2, C_SV: 5}

    def body(p_ref, g_ref, *rest):
        d_refs, dlf_ref, o_ref, dg_ref = rest[:7], rest[7], rest[8], rest[9]
        rows = []
        for start, heads, row in NORMED:
            gn = g_ref[row:row + 1, :]
            d_ref = d_refs[d_normed[start]]
            tot = jnp.zeros((1, HEAD_DIM), F32)
            for hh in range(heads):
                sl = slice(start + hh * HEAD_DIM, start + (hh + 1) * HEAD_DIM)
                dx, dgn = _head_norm_bwd(p_ref[:, sl], gn, d_ref[:, hh * HEAD_DIM:(hh + 1) * HEAD_DIM])
                o_ref[:, sl] = dx.astype(BF16)
                tot = tot + dgn
            rows.append(tot)
        for start, heads in PLAIN:
            o_ref[:, start:start + heads * HEAD_DIM] = d_refs[d_plain[start]][...].astype(BF16)
        zb = p_ref[:, C_FL:C_FL + HEAD_DIM] + g_ref[5:6, :]
        lane = lax.broadcasted_iota(jnp.int32, (tr, HEAD_DIM), 1)
        dz = jnp.where(lane < FOX_HEADS, dlf_ref[...] * (1.0 - _sigmoid(zb)), 0.0)
        o_ref[:, C_FL:C_FL + HEAD_DIM] = dz.astype(BF16)
        rows.append(jnp.sum(dz, axis=0, keepdims=True))
        part = jnp.concatenate(rows + [jnp.zeros((2, HEAD_DIM), F32)], axis=0)

        @pl.when(pl.program_id(0) == 0)
        def _():
            dg_ref[...] = part

        @pl.when(pl.program_id(0) > 0)
        def _():
            dg_ref[...] += part

    def rows_of(w):
        return pl.BlockSpec((tr, w), lambda i: (i, 0))

    small = pl.BlockSpec((8, 128), lambda i: (0, 0))
    ds = [dfq, dfk, dfv, dsq, dsk, dsv, dmq]
    return pl.pallas_call(
        body, name="prep_bwd", grid=(T // tr,),
        in_specs=[rows_of(PROJ_W), small] + [rows_of(d.shape[1]) for d in ds] + [rows_of(HEAD_DIM)],
        out_specs=[rows_of(PROJ_W), small],
        out_shape=[jax.ShapeDtypeStruct((T, PROJ_W), BF16), jax.ShapeDtypeStruct((8, 128), F32)],
        compiler_params=_params(("arbitrary",)),
    )(proj, gains, *ds, dlogf)


def _head_norm_rows(x, gain):
    R, W = x.shape

    def body(x_ref, g_ref, o_ref):
        for hh in range(W // HEAD_DIM):
            sl = slice(hh * HEAD_DIM, (hh + 1) * HEAD_DIM)
            o_ref[:, sl] = _head_norm(x_ref[:, sl], g_ref[...]).astype(BF16)

    return pl.pallas_call(body, name="mem_k_norm", out_shape=jax.ShapeDtypeStruct((R, W), BF16))(x, gain)


def _head_norm_rows_bwd(x, gain, dy):
    R, W = x.shape

    def body(x_ref, g_ref, dy_ref, dx_ref, dg_ref):
        tot = jnp.zeros((1, HEAD_DIM), F32)
        for hh in range(W // HEAD_DIM):
            sl = slice(hh * HEAD_DIM, (hh + 1) * HEAD_DIM)
            dx, dgn = _head_norm_bwd(x_ref[:, sl], g_ref[...], dy_ref[:, sl])
            dx_ref[:, sl] = dx.astype(BF16)
            tot = tot + dgn
        dg_ref[...] = tot

    return pl.pallas_call(
        body, name="mem_k_norm_bwd",
        out_shape=[jax.ShapeDtypeStruct((R, W), BF16), jax.ShapeDtypeStruct((1, HEAD_DIM), F32)])(x, gain, dy)


def _cumsum_rows(name, xs, reverse, columns=None):
    T, W = xs[0].shape
    tb = _tile(T, 512, 8)
    nb = T // tb
    n_in = len(xs) + (0 if columns is None else 1)

    def body(*refs):
        o_ref, carry = refs[n_in], refs[n_in + 1]

        @pl.when(pl.program_id(0) == 0)
        def _():
            carry[...] = jnp.zeros_like(carry)

        xv = refs[0][...]
        for x_ref in refs[1:len(xs)]:
            xv = xv + x_ref[...]
        if columns is not None:
            lane = lax.broadcasted_iota(jnp.int32, (tb, W), 1)
            for hh in range(columns.shape[0]):
                xv = xv + jnp.where(lane == hh, refs[len(xs)][hh], 0.0)
        r = lax.broadcasted_iota(jnp.int32, (tb, tb), 0)
        cc = lax.broadcasted_iota(jnp.int32, (tb, tb), 1)
        tri = jnp.where((cc >= r) if reverse else (cc <= r), 1.0, 0.0).astype(F32)
        o_ref[...] = jnp.dot(tri, xv, precision=lax.Precision.HIGHEST, preferred_element_type=F32) + carry[...]
        carry[...] += jnp.sum(xv, axis=0, keepdims=True)

    idx = (lambda i: (nb - 1 - i, 0)) if reverse else (lambda i: (i, 0))
    in_specs = [pl.BlockSpec((tb, W), idx)] * len(xs)
    if columns is not None:
        in_specs.append(pl.BlockSpec((columns.shape[0], tb, 1), lambda i: (0, idx(i)[0], 0)))
    return pl.pallas_call(
        body, name=name, grid=(nb,), in_specs=in_specs, out_specs=pl.BlockSpec((tb, W), idx),
        out_shape=jax.ShapeDtypeStruct((T, W), F32), scratch_shapes=[pltpu.VMEM((1, W), F32)],
        compiler_params=_params(("arbitrary",)),
    )(*xs, *([] if columns is None else [columns]))


def _triangle(nq, by_column):
    if by_column:
        blocks = [(i, j) for j in range(nq) for i in range(j, nq)]
    else:
        blocks = [(i, j) for i in range(nq) for j in range(i + 1)]
    return jnp.asarray(np.array(blocks, np.int32).T)


def _fox_scores_t(k, q, cq_row, ck_rep, on_diagonal):
    n = q.shape[0]
    s = _dot(k, q, "nt") * SCALE + (cq_row - jnp.tile(ck_rep, (1, n // HEAD_DIM)))
    if on_diagonal:
        s = jnp.where(lax.broadcasted_iota(jnp.int32, (n, n), 0) <= lax.broadcasted_iota(jnp.int32, (n, n), 1), s, NEG_INF)
    return s


def _fox_fwd(qkv, v_t, cq_row, ck_rep):
    T = qkv.shape[0]
    tq = _tile(T, 1024)
    nq = T // tq
    steps = nq * (nq + 1) // 2
    HQ, HK = C_FQ // HEAD_DIM, C_FK // HEAD_DIM

    def body(tab, q_ref, k_ref, vt_ref, cq_ref, ck_ref, o_ref, of_ref, lse_ref, m_sc, l_sc, acc_sc):
        i, j = tab[0, pl.program_id(1)], tab[1, pl.program_id(1)]

        @pl.when(j == 0)
        def _():
            m_sc[...] = jnp.full_like(m_sc, NEG_INF)
            l_sc[...] = jnp.zeros_like(l_sc)
            acc_sc[...] = jnp.zeros_like(acc_sc)

        def step(on_diagonal):
            s = _fox_scores_t(k_ref[...], q_ref[...], cq_ref[...], ck_ref[...], on_diagonal)
            m_new = jnp.maximum(m_sc[...], jnp.max(s, axis=0, keepdims=True))
            alpha = jnp.exp(m_sc[...] - m_new)
            p = jnp.exp(s - m_new)
            l_sc[...] = alpha * l_sc[...] + jnp.sum(p, axis=0, keepdims=True)
            acc_sc[...] = alpha * acc_sc[...] + _dot(vt_ref[...], p.astype(BF16), "nn")
            m_sc[...] = m_new

        @pl.when(j < i)
        def _():
            step(False)

        @pl.when(j == i)
        def _():
            step(True)
            o = (acc_sc[...] / l_sc[...]).T
            o_ref[...] = o.astype(BF16)
            of_ref[...] = o
            lse_ref[...] = m_sc[...] + jnp.log(l_sc[...])

    qrow = pl.BlockSpec((None, 1, tq), lambda h, s, tab: (h, 0, tab[0, s]))
    return pl.pallas_call(
        body, name="fox_fwd",
        grid_spec=pltpu.PrefetchScalarGridSpec(
            num_scalar_prefetch=1, grid=(FOX_HEADS, steps),
            in_specs=[pl.BlockSpec((tq, HEAD_DIM), lambda h, s, tab: (tab[0, s], HQ + h)),
                      pl.BlockSpec((tq, HEAD_DIM), lambda h, s, tab: (tab[1, s], HK + h)),
                      pl.BlockSpec((HEAD_DIM, tq), lambda h, s, tab: (h, tab[1, s])), qrow,
                      pl.BlockSpec((None, tq, HEAD_DIM), lambda h, s, tab: (h, tab[1, s], 0))],
            out_specs=[pl.BlockSpec((tq, HEAD_DIM), lambda h, s, tab: (tab[0, s], h)),
                       pl.BlockSpec((tq, HEAD_DIM), lambda h, s, tab: (tab[0, s], h)), qrow],
            scratch_shapes=[pltpu.VMEM((1, tq), F32), pltpu.VMEM((1, tq), F32), pltpu.VMEM((HEAD_DIM, tq), F32)]),
        out_shape=[jax.ShapeDtypeStruct((T, FOX_W), BF16), jax.ShapeDtypeStruct((T, FOX_W), F32),
                   jax.ShapeDtypeStruct((FOX_HEADS, 1, T), F32)],
        compiler_params=_params(("parallel", "arbitrary")),
    )(_triangle(nq, False), qkv, qkv, v_t, cq_row, ck_rep)


def _fox_delta(dmix, out_f32):
    T = out_f32.shape[0]
    tr = _tile(T, 512, 8)

    def body(do_ref, o_ref, d_ref):
        lane = lax.broadcasted_iota(jnp.int32, (tr, HEAD_DIM), 1)
        acc = jnp.zeros((tr, HEAD_DIM), F32)
        for hh in range(FOX_HEADS):
            sl = slice(hh * HEAD_DIM, (hh + 1) * HEAD_DIM)
            d = jnp.sum(do_ref[:, sl].astype(F32) * o_ref[:, sl], axis=-1, keepdims=True)
            acc = jnp.where(lane == hh, d, acc)
        d_ref[...] = acc

    blk = pl.BlockSpec((tr, FOX_W), lambda i: (i, 0))
    return pl.pallas_call(
        body, name="fox_delta", grid=(T // tr,), in_specs=[blk, blk], out_specs=pl.BlockSpec((tr, HEAD_DIM), lambda i: (i, 0)),
        out_shape=jax.ShapeDtypeStruct((T, HEAD_DIM), F32), compiler_params=_params(("parallel",)),
    )(dmix, out_f32)


def _fox_bwd(qkv, k_t, cq_row, ck_rep, delta_row, lse, dmix):
    T = qkv.shape[0]
    tq = _tile(T, 1024)
    nq = T // tq
    steps = nq * (nq + 1) // 2
    HQ, HK, HV = C_FQ // HEAD_DIM, C_FK // HEAD_DIM, C_FV // HEAD_DIM

    def body(tab, q_ref, k_ref, kt_ref, v_ref, cq_ref, ck_ref, delta_ref, lse_ref, do_ref,
             dq_ref, dk_ref, dv_ref, dck_ref, dcq_ref, dk_sc, dv_sc, dc_sc, dqt_sc):
        qi, kj = tab[0, pl.program_id(1)], tab[1, pl.program_id(1)]

        @pl.when(qi == kj)
        def _():
            dk_sc[...] = jnp.zeros_like(dk_sc)
            dv_sc[...] = jnp.zeros_like(dv_sc)
            dc_sc[...] = jnp.zeros_like(dc_sc)

        def step(on_diagonal):
            q, k, v, do = q_ref[...], k_ref[...], v_ref[...], do_ref[...]
            p = jnp.exp(_fox_scores_t(k, q, cq_ref[...], ck_ref[...], on_diagonal) - lse_ref[...])
            dp = _dot(v, do, "nt")
            ds = p * (dp - delta_ref[...])
            dsb = ds.astype(BF16)
            dv_sc[...] += _dot(p.astype(BF16), do, "nn")
            dk_sc[...] += _dot(dsb, q, "nn")
            dc_sc[...] += jnp.sum(ds, axis=1, keepdims=True)
            dq_part = _dot(kt_ref[...], dsb, "nn") * SCALE
            dcq_part = jnp.sum(ds, axis=0, keepdims=True)

            @pl.when(kj == 0)
            def _():
                dqt_sc[qi] = dq_part
                dcq_ref[qi] = dcq_part

            @pl.when(kj > 0)
            def _():
                dqt_sc[qi] += dq_part
                dcq_ref[qi] += dcq_part

            if on_diagonal:
                dq_ref[...] = dqt_sc[qi].T

        @pl.when(qi > kj)
        def _():
            step(False)

        @pl.when(qi == kj)
        def _():
            step(True)

        @pl.when(qi == nq - 1)
        def _():
            dk_ref[...] = dk_sc[...] * SCALE
            dv_ref[...] = dv_sc[...]
            dck_ref[...] = -dc_sc[...]

    def rows(base):
        return pl.BlockSpec((tq, HEAD_DIM), lambda h, s, tab: (tab[0, s], base + h))

    def cols(base):
        return pl.BlockSpec((tq, HEAD_DIM), lambda h, s, tab: (tab[1, s], base + h))

    qrow = pl.BlockSpec((None, 1, tq), lambda h, s, tab: (h, 0, tab[0, s]))
    sds = jax.ShapeDtypeStruct((T, FOX_W), F32)
    return pl.pallas_call(
        body, name="fox_bwd",
        grid_spec=pltpu.PrefetchScalarGridSpec(
            num_scalar_prefetch=1, grid=(FOX_HEADS, steps),
            in_specs=[rows(HQ), cols(HK), pl.BlockSpec((HEAD_DIM, tq), lambda h, s, tab: (h, tab[1, s])), cols(HV), qrow,
                      pl.BlockSpec((None, tq, HEAD_DIM), lambda h, s, tab: (h, tab[1, s], 0)), qrow, qrow, rows(0)],
            out_specs=[cols(0), cols(0), cols(0), pl.BlockSpec((None, tq, 1), lambda h, s, tab: (h, tab[1, s], 0)),
                       pl.BlockSpec((None, nq, 1, tq), lambda h, s, tab: (h, 0, 0, 0))],
            scratch_shapes=[pltpu.VMEM((tq, HEAD_DIM), F32), pltpu.VMEM((tq, HEAD_DIM), F32), pltpu.VMEM((tq, 1), F32),
                            pltpu.VMEM((nq, HEAD_DIM, tq), F32)]),
        out_shape=[sds, sds, sds, jax.ShapeDtypeStruct((FOX_HEADS, T, 1), F32), jax.ShapeDtypeStruct((FOX_HEADS, nq, 1, tq), F32)],
        compiler_params=_params(("parallel", "arbitrary")),
    )(_triangle(nq, True), qkv, qkv, k_t, qkv, cq_row, ck_rep, delta_row, lse, dmix)


GW = SWA_GROUP * HEAD_DIM
GR = SWA_GROUP * WINDOW


def _swa_scores(q_ref, kp_ref, kc_ref, slope_ref, n):
    q = q_ref[...]
    qs = jnp.concatenate([q[:, t * HEAD_DIM:(t + 1) * HEAD_DIM] for t in range(SWA_GROUP)], axis=0)
    kb = jnp.concatenate([kp_ref[...], kc_ref[...]], axis=0)
    r = lax.broadcasted_iota(jnp.int32, (GR, 2 * WINDOW), 0) & (WINDOW - 1)
    jj = lax.broadcasted_iota(jnp.int32, (GR, 2 * WINDOW), 1)
    dist = WINDOW + r - jj
    valid = (dist >= 0) & (dist < WINDOW) & ((n > 0) | (jj >= WINDOW))
    s = _dot(qs, kb, "nt") * SCALE - slope_ref[...] * dist.astype(F32)
    return qs, kb, jnp.where(valid, s, NEG_INF), valid


def _swa_specs():
    HQ, HK, HV = C_SQ // GW, C_SK // HEAD_DIM, C_SV // HEAD_DIM
    q_spec = pl.BlockSpec((WINDOW, GW), lambda g, n: (n, HQ + g))

    def prev(base):
        return pl.BlockSpec((WINDOW, HEAD_DIM), lambda g, n: (jnp.maximum(n - 1, 0), base + g))

    def cur(base):
        return pl.BlockSpec((WINDOW, HEAD_DIM), lambda g, n: (n, base + g))

    col = pl.BlockSpec((None, GR, 1), lambda g, n: (g, 0, 0))
    return q_spec, prev(HK), cur(HK), prev(HV), cur(HV), col


def _swa_fwd(qkv, slopes, sinks):
    T = qkv.shape[0]
    nb = T // WINDOW
    assert C_SQ % GW == 0

    def body(q_ref, kp_ref, kc_ref, vp_ref, vc_ref, slope_ref, sink_ref, o_ref, lse_ref):
        n = pl.program_id(1)
        _, _, s, _ = _swa_scores(q_ref, kp_ref, kc_ref, slope_ref, n)
        m = jnp.maximum(jnp.max(s, axis=-1, keepdims=True), sink_ref[...])
        p = jnp.exp(s - m)
        l = jnp.sum(p, axis=-1, keepdims=True) + jnp.exp(sink_ref[...] - m)
        vb = jnp.concatenate([vp_ref[...], vc_ref[...]], axis=0)
        o = _dot(p.astype(BF16), vb, "nn") / l
        for t in range(SWA_GROUP):
            o_ref[:, t * HEAD_DIM:(t + 1) * HEAD_DIM] = o[t * WINDOW:(t + 1) * WINDOW, :].astype(BF16)
        lse_ref[...] = m + jnp.log(l)

    q_spec, kp, kc, vp, vc, col = _swa_specs()
    return pl.pallas_call(
        body, name="swa_fwd", grid=(SWA_KV_HEADS, nb), in_specs=[q_spec, kp, kc, vp, vc, col, col],
        out_specs=[pl.BlockSpec((WINDOW, GW), lambda g, n: (n, g)), pl.BlockSpec((None, None, GR, 1), lambda g, n: (g, n, 0, 0))],
        out_shape=[jax.ShapeDtypeStruct((T, SWA_HEADS * HEAD_DIM), BF16), jax.ShapeDtypeStruct((SWA_KV_HEADS, nb, GR, 1), F32)],
        compiler_params=_params(("parallel", "arbitrary")),
    )(qkv, qkv, qkv, qkv, qkv, slopes, sinks)


def _swa_bwd(qkv, slopes, sinks, out, lse, dmix):
    T = qkv.shape[0]
    nb = T // WINDOW
    DO = FOX_W // GW
    assert FOX_W % GW == 0

    def body(q_ref, kp_ref, kc_ref, vp_ref, vc_ref, slope_ref, sink_ref, o_ref, lse_ref, do_ref,
             dq_ref, dk_ref, dv_ref, dsink_ref, sink_sc):
        n = pl.program_id(1)

        @pl.when(n == 0)
        def _():
            dk_ref[...] = jnp.zeros_like(dk_ref)
            dv_ref[...] = jnp.zeros_like(dv_ref)
            sink_sc[...] = jnp.zeros_like(sink_sc)

        qs, kb, s, valid = _swa_scores(q_ref, kp_ref, kc_ref, slope_ref, n)
        lse = lse_ref[...]
        p = jnp.where(valid, jnp.exp(s - lse), 0.0)
        vb = jnp.concatenate([vp_ref[...], vc_ref[...]], axis=0)
        do = jnp.concatenate([do_ref[:, t * HEAD_DIM:(t + 1) * HEAD_DIM] for t in range(SWA_GROUP)], axis=0)
        oo = jnp.concatenate([o_ref[:, t * HEAD_DIM:(t + 1) * HEAD_DIM] for t in range(SWA_GROUP)], axis=0)
        dp = _dot(do, vb, "nt")
        delta = jnp.sum(do.astype(F32) * oo.astype(F32), axis=-1, keepdims=True)
        ds = p * (dp - delta)
        dsb = ds.astype(BF16)
        dq = _dot(dsb, kb, "nn") * SCALE
        for t in range(SWA_GROUP):
            dq_ref[:, t * HEAD_DIM:(t + 1) * HEAD_DIM] = dq[t * WINDOW:(t + 1) * WINDOW, :]
        dkb = _dot(dsb, qs, "tn") * SCALE
        dvb = _dot(p.astype(BF16), do, "tn")
        r_prev = pl.ds(pl.multiple_of(jnp.maximum(n - 1, 0) * WINDOW, WINDOW), WINDOW)
        r_cur = pl.ds(pl.multiple_of(n * WINDOW, WINDOW), WINDOW)
        dk_ref[r_prev, :] += dkb[:WINDOW, :]
        dk_ref[r_cur, :] += dkb[WINDOW:, :]
        dv_ref[r_prev, :] += dvb[:WINDOW, :]
        dv_ref[r_cur, :] += dvb[WINDOW:, :]
        sink_sc[...] -= jnp.exp(sink_ref[...] - lse) * delta

        @pl.when(n == nb - 1)
        def _():
            tot = [jnp.zeros((1, 128), F32) + jnp.sum(sink_sc[t * WINDOW:(t + 1) * WINDOW, :]) for t in range(SWA_GROUP)]
            dsink_ref[...] = jnp.concatenate(tot + [jnp.zeros((8 - SWA_GROUP, 128), F32)], axis=0)

    q_spec, kp, kc, vp, vc, col = _swa_specs()
    kv_acc = pl.BlockSpec((T, HEAD_DIM), lambda g, n: (0, g))
    return pl.pallas_call(
        body, name="swa_bwd", grid=(SWA_KV_HEADS, nb),
        in_specs=[q_spec, kp, kc, vp, vc, col, col, pl.BlockSpec((WINDOW, GW), lambda g, n: (n, g)),
                  pl.BlockSpec((None, None, GR, 1), lambda g, n: (g, n, 0, 0)), pl.BlockSpec((WINDOW, GW), lambda g, n: (n, DO + g))],
        out_specs=[pl.BlockSpec((WINDOW, GW), lambda g, n: (n, g)), kv_acc, kv_acc, pl.BlockSpec((None, 8, 128), lambda g, n: (g, 0, 0))],
        out_shape=[jax.ShapeDtypeStruct((T, SWA_HEADS * HEAD_DIM), F32), jax.ShapeDtypeStruct((T, SWA_KV_HEADS * HEAD_DIM), F32),
                   jax.ShapeDtypeStruct((T, SWA_KV_HEADS * HEAD_DIM), F32), jax.ShapeDtypeStruct((SWA_KV_HEADS, 8, 128), F32)],
        scratch_shapes=[pltpu.VMEM((GR, 1), F32)],
        compiler_params=_params(("parallel", "arbitrary")),
    )(qkv, qkv, qkv, qkv, qkv, slopes, sinks, out, lse, dmix)


def _mem_fwd(qkv, mk, mv):
    T, ML = qkv.shape[0], mk.shape[0]
    tq = _tile(T, 1024)
    HQ = C_MQ // HEAD_DIM

    def body(q_ref, k_ref, v_ref, o_ref, lse_ref):
        s = _dot(q_ref[...], k_ref[...], "nt") * SCALE
        m = jnp.max(s, axis=-1, keepdims=True)
        p = jnp.exp(s - m)
        l = jnp.sum(p, axis=-1, keepdims=True)
        o_ref[...] = (_dot(p.astype(BF16), v_ref[...], "nn") / l).astype(BF16)
        lse_ref[...] = m + jnp.log(l)

    kv = pl.BlockSpec((ML, HEAD_DIM), lambda h, i: (0, h))
    return pl.pallas_call(
        body, name="mem_fwd", grid=(MEM_HEADS, T // tq),
        in_specs=[pl.BlockSpec((tq, HEAD_DIM), lambda h, i: (i, HQ + h)), kv, kv],
        out_specs=[pl.BlockSpec((tq, HEAD_DIM), lambda h, i: (i, h)), pl.BlockSpec((None, tq, 1), lambda h, i: (h, i, 0))],
        out_shape=[jax.ShapeDtypeStruct((T, MEM_HEADS * HEAD_DIM), BF16), jax.ShapeDtypeStruct((MEM_HEADS, T, 1), F32)],
        compiler_params=_params(("parallel", "arbitrary")),
    )(qkv, mk, mv)


def _mem_bwd(qkv, mk, mv, out, lse, dmix):
    T, ML = qkv.shape[0], mk.shape[0]
    tq = _tile(T, 1024)
    HQ = C_MQ // HEAD_DIM
    DO = (FOX_W + SWA_HEADS * HEAD_DIM) // HEAD_DIM

    def body(q_ref, k_ref, v_ref, o_ref, lse_ref, do_ref, dq_ref, dk_ref, dv_ref):
        q, k, v, do = q_ref[...], k_ref[...], v_ref[...], do_ref[...]
        p = jnp.exp(_dot(q, k, "nt") * SCALE - lse_ref[...])
        dp = _dot(do, v, "nt")
        delta = jnp.sum(do.astype(F32) * o_ref[...].astype(F32), axis=-1, keepdims=True)
        dsb = (p * (dp - delta)).astype(BF16)
        dq_ref[...] = _dot(dsb, k, "nn") * SCALE
        dk_part = _dot(dsb, q, "tn") * SCALE
        dv_part = _dot(p.astype(BF16), do, "tn")

        @pl.when(pl.program_id(1) == 0)
        def _():
            dk_ref[...] = dk_part
            dv_ref[...] = dv_part

        @pl.when(pl.program_id(1) > 0)
        def _():
            dk_ref[...] += dk_part
            dv_ref[...] += dv_part

    kv = pl.BlockSpec((ML, HEAD_DIM), lambda h, i: (0, h))
    qb = pl.BlockSpec((tq, HEAD_DIM), lambda h, i: (i, h))
    return pl.pallas_call(
        body, name="mem_bwd", grid=(MEM_HEADS, T // tq),
        in_specs=[pl.BlockSpec((tq, HEAD_DIM), lambda h, i: (i, HQ + h)), kv, kv, qb,
                  pl.BlockSpec((None, tq, 1), lambda h, i: (h, i, 0)), pl.BlockSpec((tq, HEAD_DIM), lambda h, i: (i, DO + h))],
        out_specs=[qb, kv, kv],
        out_shape=[jax.ShapeDtypeStruct((T, MEM_HEADS * HEAD_DIM), F32), jax.ShapeDtypeStruct((ML, MEM_HEADS * HEAD_DIM), F32),
                   jax.ShapeDtypeStruct((ML, MEM_HEADS * HEAD_DIM), F32)],
        compiler_params=_params(("parallel", "arbitrary")),
    )(qkv, mk, mv, out, lse, dmix)


HBM = pl.BlockSpec(memory_space=pltpu.HBM)


def _place():
    x, y, c = lax.axis_index("x"), lax.axis_index("y"), lax.axis_index("c")
    chips = [(1 - x, y), (x, 1 - y), (1 - x, 1 - y)]
    return x, y, c, chips


def _remote(src, dst, send_sem, recv_sem, device):
    return pltpu.make_async_remote_copy(src_ref=src, dst_ref=dst, send_sem=send_sem, recv_sem=recv_sem,
                                        device_id=device, device_id_type=MESH)


def _place_ids():
    x, y, c = lax.axis_index("x"), lax.axis_index("y"), lax.axis_index("c")
    order = [2 * x + y, 2 * (1 - x) + y, 2 * x + (1 - y), 2 * (1 - x) + (1 - y)]
    return jnp.stack([2 * x + y, c] + order).astype(jnp.int32)


def _cast_place(name, w, ids, *after, keep_own=False):
    R, C = w.shape
    tr = _tile(R, 256, 16)
    n_out = 2 if keep_own else 1

    def body(ids_ref, w_ref, *rest):
        for o_ref in rest[-n_out:]:
            o_ref[...] = w_ref[...].astype(BF16)

    res = pl.pallas_call(
        body, name=name,
        grid_spec=pltpu.PrefetchScalarGridSpec(
            num_scalar_prefetch=1, grid=(R // tr,),
            in_specs=[pl.BlockSpec((tr, C), lambda i, ids: (i, 0))] + [pl.BlockSpec(memory_space=pl.ANY)] * len(after),
            out_specs=[pl.BlockSpec((None, tr, C), lambda i, ids: (ids[0], i, 0)), pl.BlockSpec((tr, C), lambda i, ids: (i, 0))][:n_out]),
        out_shape=[jax.ShapeDtypeStruct((N_CHIPS, R, C), BF16), jax.ShapeDtypeStruct((R, C), BF16)][:n_out],
        compiler_params=_params(("parallel",)),
    )(ids, w, *after)
    return res if keep_own else res[0]


SEM = pl.BlockSpec(memory_space=pltpu.SEMAPHORE)
EFFECT = pltpu.SideEffectType.DATAFLOW_SIDE_EFFECTING


def _hbm(a):
    return pltpu.with_memory_space_constraint(a, pltpu.HBM)


def _gather_start(name, placed, after, to_sibling=False):
    n = len(placed)

    ns = 3 * n

    def body(*refs):
        send, recv = refs[n + 1:n + 1 + ns], refs[n + 1 + ns:n + 1 + 2 * ns]
        buf = refs[n + 1 + 2 * ns:2 * n + 1 + 2 * ns]
        token = refs[2 * n + 1 + 2 * ns]
        x, y, c, chips = _place()
        me = 2 * x + y
        for a in range(n):
            half = buf[a].shape[1] // 2
            for j, (cx, cy) in enumerate(chips):
                block, peer = (2 * cx + cy, (x, y, 1 - c)) if to_sibling else (me, (cx, cy, c))
                part = buf[a].at[block, pl.ds(c * half, half)]
                _remote(part, part, send[3 * a + j], recv[3 * a + j], peer).start()
        token[...] = jnp.zeros_like(token)

    res = pl.pallas_call(
        body, name=name, in_specs=[HBM] * n + [pl.BlockSpec(memory_space=pl.ANY)],
        out_specs=[SEM] * (2 * ns) + [HBM] * n + [pl.BlockSpec(memory_space=pltpu.VMEM)],
        out_shape=[pltpu.SemaphoreType.DMA(())] * (2 * ns)
        + [pltpu.HBM(s.shape, s.dtype) for s in placed] + [jax.ShapeDtypeStruct((8, 128), F32)],
        input_output_aliases={a: 2 * ns + a for a in range(n)},
        compiler_params=pltpu.CompilerParams(has_side_effects=EFFECT),
    )(*[_hbm(s) for s in placed], after)
    return list(res[:ns]), list(res[ns:2 * ns]), list(res[2 * ns:2 * ns + n]), res[2 * ns + n]


def _gather_wait(name, send, recv, bufs, after, to_sibling=False):
    n = len(bufs)

    ns = 3 * n

    def body(*refs):
        buf = refs[:n]
        send_ref, recv_ref = refs[n:n + ns], refs[n + ns:n + 2 * ns]
        x, y, c, chips = _place()
        ids = [2 * cx + cy for cx, cy in chips]
        for a in range(n):
            half = buf[a].shape[1] // 2
            for j in range(3):
                sent = buf[a].at[ids[j], pl.ds(c * half, half)]
                landed = buf[a].at[ids[j], pl.ds((1 - c) * half, half)] if to_sibling else sent
                cp = _remote(sent, landed, send_ref[3 * a + j], recv_ref[3 * a + j], (x, y, c))
                cp.wait_send()
                cp.wait_recv()

    res = pl.pallas_call(
        body, name=name, in_specs=[HBM] * n + [SEM] * (2 * ns) + [pl.BlockSpec(memory_space=pl.ANY)], out_specs=[HBM] * n,
        out_shape=[pltpu.HBM(s.shape, s.dtype) for s in bufs], input_output_aliases={a: a for a in range(n)},
        compiler_params=pltpu.CompilerParams(has_side_effects=EFFECT),
    )(*bufs, *send, *recv, after)
    return list(res)


def _gather_forward(name, bufs):
    n = len(bufs)

    def body(*refs):
        buf = refs[n:2 * n]
        send, recv = refs[2 * n:]
        x, y, c, chips = _place()
        ids = [2 * cx + cy for cx, cy in chips]
        copies = []
        for a in range(n):
            half = buf[a].shape[1] // 2
            for j in range(3):
                landed = buf[a].at[ids[j], pl.ds(c * half, half)]
                cp = _remote(landed, landed, send.at[a, j], recv.at[a, j], (x, y, 1 - c))
                cp.start()
                copies.append(cp)
        for a in range(n):
            half = buf[a].shape[1] // 2
            for j in range(3):
                landed = buf[a].at[ids[j], pl.ds((1 - c) * half, half)]
                _remote(landed, landed, send.at[a, j], recv.at[a, j], (x, y, c)).wait_recv()
        for cp in copies:
            cp.wait_send()

    return pl.pallas_call(
        body, name=name, in_specs=[HBM] * n, out_specs=[HBM] * n,
        out_shape=[jax.ShapeDtypeStruct(s.shape, s.dtype) for s in bufs], input_output_aliases={a: a for a in range(n)},
        scratch_shapes=[pltpu.SemaphoreType.DMA((n, 3)), pltpu.SemaphoreType.DMA((n, 3))],
    )(*bufs)


def _pair_start(name, grads):
    n = len(grads)
    ns = N_CHIPS * n

    def body(*refs):
        send, recv = refs[2 * n:2 * n + ns], refs[2 * n + ns:2 * n + 2 * ns]
        src = refs[2 * n + 2 * ns:3 * n + 2 * ns]
        land = refs[3 * n + 2 * ns:4 * n + 2 * ns]
        token = refs[4 * n + 2 * ns]
        x, y, c, chips = _place()
        order = [2 * x + y] + [2 * cx + cy for cx, cy in chips]
        for a in range(n):
            half = src[a].shape[1] // 2
            for j in range(N_CHIPS):
                _remote(src[a].at[order[j], pl.ds((1 - c) * half, half)], land[a].at[j],
                        send[N_CHIPS * a + j], recv[N_CHIPS * a + j], (x, y, 1 - c)).start()
        token[...] = jnp.zeros_like(token)

    lands = [jax.ShapeDtypeStruct((N_CHIPS, g.shape[1] // 2, g.shape[2]), g.dtype) for g in grads]
    res = pl.pallas_call(
        body, name=name, in_specs=[HBM] * (2 * n),
        out_specs=[SEM] * (2 * ns) + [HBM] * (2 * n) + [pl.BlockSpec(memory_space=pltpu.VMEM)],
        out_shape=[pltpu.SemaphoreType.DMA(())] * (2 * ns) + [pltpu.HBM(g.shape, g.dtype) for g in grads]
        + [pltpu.HBM(l.shape, l.dtype) for l in lands] + [jax.ShapeDtypeStruct((8, 128), F32)],
        input_output_aliases={a: 2 * ns + a for a in range(2 * n)},
        compiler_params=pltpu.CompilerParams(has_side_effects=EFFECT),
    )(*[_hbm(g) for g in grads], *[_hbm(lax.empty(l.shape, l.dtype)) for l in lands])
    return list(res[:ns]), list(res[ns:2 * ns]), list(res[2 * ns:2 * ns + n]), list(res[2 * ns + n:2 * ns + 2 * n]), res[2 * ns + 2 * n]


def _pair_wait(name, send, recv, grads, lands, after):
    n = len(grads)
    ns = N_CHIPS * n

    def body(*refs):
        src, land = refs[:n], refs[n:2 * n]
        send_ref, recv_ref = refs[2 * n:2 * n + ns], refs[2 * n + ns:2 * n + 2 * ns]
        x, y, c, _ = _place()
        for a in range(n):
            for j in range(N_CHIPS):
                cp = _remote(land[a].at[j], land[a].at[j], send_ref[N_CHIPS * a + j], recv_ref[N_CHIPS * a + j], (x, y, c))
                cp.wait_send()
                cp.wait_recv()

    res = pl.pallas_call(
        body, name=name, in_specs=[HBM] * (2 * n) + [SEM] * (2 * ns) + [pl.BlockSpec(memory_space=pl.ANY)],
        out_specs=[HBM] * (2 * n), out_shape=[pltpu.HBM(g.shape, g.dtype) for g in grads] + [pltpu.HBM(l.shape, l.dtype) for l in lands],
        input_output_aliases={a: a for a in range(2 * n)},
        compiler_params=pltpu.CompilerParams(has_side_effects=EFFECT),
    )(*grads, *lands, *send, *recv, after)
    return list(res[:n]), list(res[n:])


def _chip_start(name, parts):
    n = len(parts)
    ns = 3 * n

    def body(*refs):
        send, recv = refs[2 * n:2 * n + ns], refs[2 * n + ns:2 * n + 2 * ns]
        src = refs[2 * n + 2 * ns:3 * n + 2 * ns]
        land = refs[3 * n + 2 * ns:4 * n + 2 * ns]
        token = refs[4 * n + 2 * ns]
        x, y, c, chips = _place()
        for a in range(n):
            for j, (cx, cy) in enumerate(chips):
                _remote(src[a].at[j], land[a].at[j], send[3 * a + j], recv[3 * a + j], (cx, cy, c)).start()
        token[...] = jnp.zeros_like(token)

    res = pl.pallas_call(
        body, name=name, in_specs=[HBM] * (2 * n),
        out_specs=[SEM] * (2 * ns) + [HBM] * (2 * n) + [pl.BlockSpec(memory_space=pltpu.VMEM)],
        out_shape=[pltpu.SemaphoreType.DMA(())] * (2 * ns) + [pltpu.HBM(p.shape, p.dtype) for p in parts] * 2
        + [jax.ShapeDtypeStruct((8, 128), F32)],
        input_output_aliases={a: 2 * ns + a for a in range(2 * n)},
        compiler_params=pltpu.CompilerParams(has_side_effects=EFFECT),
    )(*[_hbm(p) for p in parts], *[_hbm(lax.empty(p.shape, p.dtype)) for p in parts])
    return list(res[:ns]), list(res[ns:2 * ns]), list(res[2 * ns:2 * ns + n]), list(res[2 * ns + n:2 * ns + 2 * n]), res[2 * ns + 2 * n]


def _chip_wait(name, send, recv, parts, lands, after):
    n = len(parts)
    ns = 3 * n

    def body(*refs):
        src, land = refs[:n], refs[n:2 * n]
        send_ref, recv_ref = refs[2 * n:2 * n + ns], refs[2 * n + ns:2 * n + 2 * ns]
        x, y, c, _ = _place()
        for a in range(n):
            for j in range(3):
                cp = _remote(src[a].at[j], land[a].at[j], send_ref[3 * a + j], recv_ref[3 * a + j], (x, y, c))
                cp.wait_send()
                cp.wait_recv()

    res = pl.pallas_call(
        body, name=name, in_specs=[HBM] * (2 * n) + [SEM] * (2 * ns) + [pl.BlockSpec(memory_space=pl.ANY)],
        out_specs=[HBM] * (2 * n), out_shape=[pltpu.HBM(p.shape, p.dtype) for p in parts] * 2,
        input_output_aliases={a: a for a in range(2 * n)},
        compiler_params=pltpu.CompilerParams(has_side_effects=EFFECT),
    )(*parts, *lands, *send, *recv, after)
    return list(res[n:])


def _share_start(name, shards):
    n = len(shards)

    def body(*refs):
        send, recv = refs[n:2 * n], refs[2 * n:3 * n]
        buf = refs[3 * n:4 * n]
        token = refs[4 * n]
        x, y, c, _ = _place()
        for a in range(n):
            half = buf[a].shape[0] // 2
            mine = buf[a].at[pl.ds(c * half, half)]
            _remote(mine, mine, send[a], recv[a], (x, y, 1 - c)).start()
        token[...] = jnp.zeros_like(token)

    res = pl.pallas_call(
        body, name=name, in_specs=[HBM] * n,
        out_specs=[SEM] * (2 * n) + [HBM] * n + [pl.BlockSpec(memory_space=pltpu.VMEM)],
        out_shape=[pltpu.SemaphoreType.DMA(())] * (2 * n) + [pltpu.HBM(s.shape, s.dtype) for s in shards]
        + [jax.ShapeDtypeStruct((8, 128), F32)],
        input_output_aliases={a: 2 * n + a for a in range(n)},
        compiler_params=pltpu.CompilerParams(has_side_effects=EFFECT),
    )(*[_hbm(s) for s in shards])
    return list(res[:n]), list(res[n:2 * n]), list(res[2 * n:3 * n]), res[3 * n]


def _share_wait(name, send, recv, shards, after):
    n = len(shards)

    def body(*refs):
        buf = refs[:n]
        send_ref, recv_ref = refs[n:2 * n], refs[2 * n:3 * n]
        x, y, c, _ = _place()
        for a in range(n):
            half = buf[a].shape[0] // 2
            cp = _remote(buf[a].at[pl.ds(c * half, half)], buf[a].at[pl.ds((1 - c) * half, half)], send_ref[a], recv_ref[a], (x, y, c))
            cp.wait_send()
            cp.wait_recv()

    res = pl.pallas_call(
        body, name=name, in_specs=[HBM] * n + [SEM] * (2 * n) + [pl.BlockSpec(memory_space=pl.ANY)], out_specs=[HBM] * n,
        out_shape=[pltpu.HBM(s.shape, s.dtype) for s in shards], input_output_aliases={a: a for a in range(n)},
        compiler_params=pltpu.CompilerParams(has_side_effects=EFFECT),
    )(*shards, *send, *recv, after)
    return list(res)


def _small_start(buf):
    R, W = buf.shape
    ns = N_DEV - 1

    def body(*refs):
        send, recv = refs[2:2 + ns], refs[2 + ns:2 + 2 * ns]
        src, land, token = refs[2 + 2 * ns], refs[3 + 2 * ns], refs[4 + 2 * ns]
        x, y, c, _ = _place()
        me = 4 * x + 2 * y + c
        for k in range(1, N_DEV):
            peer = (x ^ (k >> 2), y ^ ((k >> 1) & 1), c ^ (k & 1))
            _remote(src, land.at[me], send[k - 1], recv[k - 1], peer).start()
        token[...] = jnp.zeros_like(token)

    res = pl.pallas_call(
        body, name="small_start", in_specs=[HBM, HBM],
        out_specs=[SEM] * (2 * ns) + [HBM, HBM, pl.BlockSpec(memory_space=pltpu.VMEM)],
        out_shape=[pltpu.SemaphoreType.DMA(())] * (2 * ns) + [pltpu.HBM((R, W), F32), pltpu.HBM((N_DEV, R, W), F32),
                                                                jax.ShapeDtypeStruct((8, 128), F32)],
        input_output_aliases={0: 2 * ns, 1: 2 * ns + 1},
        compiler_params=pltpu.CompilerParams(has_side_effects=EFFECT),
    )(_hbm(buf), _hbm(jnp.zeros((N_DEV, R, W), F32)))
    return list(res[:ns]), list(res[ns:2 * ns]), res[2 * ns], res[2 * ns + 1], res[2 * ns + 2]


def _small_wait(send, recv, buf, land, after):
    ns = N_DEV - 1

    def body(*refs):
        land_ref = refs[1]
        send_ref, recv_ref = refs[2:2 + ns], refs[2 + ns:2 + 2 * ns]
        x, y, c, _ = _place()
        me = 4 * x + 2 * y + c
        for k in range(1, N_DEV):
            landed = land_ref.at[me ^ k]
            cp = _remote(landed, landed, send_ref[k - 1], recv_ref[k - 1], (x, y, c))
            cp.wait_send()
            cp.wait_recv()

    return pl.pallas_call(
        body, name="small_wait", in_specs=[HBM, HBM] + [SEM] * (2 * ns) + [pl.BlockSpec(memory_space=pl.ANY)],
        out_specs=[HBM, HBM], out_shape=[pltpu.HBM(buf.shape, buf.dtype), pltpu.HBM(land.shape, land.dtype)],
        input_output_aliases={0: 0, 1: 1}, compiler_params=pltpu.CompilerParams(has_side_effects=EFFECT),
    )(buf, land, *send, *recv, after)


def _small_sum(buf, land):
    def body(buf_ref, land_ref, out_ref):
        x, y, c, _ = _place()
        me = 4 * x + 2 * y + c
        total = None
        for d in range(N_DEV):
            term = jnp.where(me == d, buf_ref[...], land_ref[d])
            total = term if total is None else total + term
        out_ref[...] = total

    return pl.pallas_call(body, name="small_sum", out_shape=jax.ShapeDtypeStruct(buf.shape, F32))(buf, land)


def _pair_sum_bf16(name, grad, theirs, ids):
    _, R2, C = theirs.shape
    tr = _tile(R2, 256, 16)
    nrb = R2 // tr

    def body(ids_ref, a_ref, b_ref, o_ref):
        o_ref[...] = (a_ref[...] + b_ref[...]).astype(BF16)

    return pl.pallas_call(
        body, name=name,
        grid_spec=pltpu.PrefetchScalarGridSpec(
            num_scalar_prefetch=1, grid=(3, nrb),
            in_specs=[pl.BlockSpec((None, tr, C), lambda j, i, ids: (ids[3 + j], ids[1] * nrb + i, 0)),
                      pl.BlockSpec((None, tr, C), lambda j, i, ids: (j + 1, i, 0))],
            out_specs=pl.BlockSpec((None, tr, C), lambda j, i, ids: (j, i, 0))),
        out_shape=jax.ShapeDtypeStruct((3, R2, C), BF16), compiler_params=_params(("parallel", "parallel")),
    )(ids, grad, theirs)


def _chip_sum(name, grad, theirs, arrived, ids):
    _, R2, C = theirs.shape
    tr = _tile(R2, 256, 16)
    nrb = R2 // tr

    def body(ids_ref, a_ref, b_ref, r_ref, o_ref):
        tot = a_ref[...] + b_ref[...]
        for j in range(3):
            tot = tot + r_ref[j].astype(F32)
        o_ref[...] = tot

    return pl.pallas_call(
        body, name=name,
        grid_spec=pltpu.PrefetchScalarGridSpec(
            num_scalar_prefetch=1, grid=(nrb,),
            in_specs=[pl.BlockSpec((None, tr, C), lambda i, ids: (ids[0], ids[1] * nrb + i, 0)),
                      pl.BlockSpec((None, tr, C), lambda i, ids: (0, i, 0)),
                      pl.BlockSpec((3, tr, C), lambda i, ids: (0, i, 0))],
            out_specs=pl.BlockSpec((tr, C), lambda i, ids: (ids[1] * nrb + i, 0))),
        out_shape=jax.ShapeDtypeStruct((2 * R2, C), F32), compiler_params=_params(("parallel",)),
    )(ids, grad, theirs, arrived)


def _adamw(name, w, g, m, v, emit_grad=False):
    R, C = w.shape
    tr = _tile(R, 128, 8)
    c1 = 1.0 / (1.0 - ADAM_B1 ** ADAM_STEP)
    c2 = 1.0 / (1.0 - ADAM_B2 ** ADAM_STEP)
    n_out = 4 if emit_grad else 3

    def body(w_ref, g_ref, m_ref, v_ref, d_ref, mo_ref, vo_ref, *rest):
        gv = g_ref[...]
        mn = ADAM_B1 * m_ref[...] + (1.0 - ADAM_B1) * gv
        vn = ADAM_B2 * v_ref[...] + (1.0 - ADAM_B2) * (gv * gv)
        d_ref[...] = -ADAM_LR * ((mn * c1) / (jnp.sqrt(vn * c2) + ADAM_EPS) + ADAM_WD * w_ref[...])
        mo_ref[...] = mn
        vo_ref[...] = vn
        if emit_grad:
            rest[0][...] = gv

    spec = pl.BlockSpec((tr, C), lambda i: (i, 0))
    sds = jax.ShapeDtypeStruct((R, C), F32)
    return pl.pallas_call(body, name=name, grid=(R // tr,), in_specs=[spec] * 4, out_specs=[spec] * n_out, out_shape=[sds] * n_out,
                          compiler_params=_params(("parallel",)))(w, g, m, v)


SMALL = ["ffn1_norm", "mix_norm", "mem_norm", "forget_bias", "fox_q_gain", "fox_k_gain", "swa_q_gain", "swa_k_gain", "swa_sinks",
         "mem_q_gain", "mem_k_gain", "ffn2_norm"]
LARGE = ["ffn1_gate", "ffn1_up", "ffn1_down", "w_in", "w_mem_k", "w_mem_v", "w_out", "ffn2_gate", "ffn2_up", "ffn2_down"]
GATHER_GROUPS = [["ffn1_gate", "ffn1_up"], ["ffn1_down", "w_in", "w_mem_k", "w_mem_v"], ["w_out", "ffn2_gate", "ffn2_up", "ffn2_down"]]
WEIGHTS = ["ffn1_norm", "ffn1_gate", "ffn1_up", "ffn1_down", "mix_norm", "mem_norm", "w_in", "forget_bias", "w_mem_k", "w_mem_v",
           "fox_q_gain", "fox_k_gain", "swa_q_gain", "swa_k_gain", "swa_sinks", "mem_q_gain", "mem_k_gain", "w_out", "ffn2_norm",
           "ffn2_gate", "ffn2_up", "ffn2_down"]


def _pad_proj_cols(w):
    out = jnp.zeros((w.shape[0], PROJ_W), w.dtype)
    for start, width, pstart in REF_GROUPS:
        out = lax.dynamic_update_slice(out, w[:, start:start + width], (0, pstart))
    return out


def _unpad_proj_cols(w):
    return jnp.concatenate([w[:, pstart:pstart + width] for _, width, pstart in REF_GROUPS], axis=1)


def _pack_small(vals):
    flat = jnp.concatenate([vals[k].reshape(-1).astype(F32) for k in SMALL + ["loss"]])
    n = flat.shape[0]
    total = -(-n // 1024) * 1024
    return jnp.pad(flat, (0, total - n)).reshape(total // 128, 128)


def _unpack_small(buf, shapes):
    flat = buf.reshape(-1)
    out, off = {}, 0
    for k in SMALL + ["loss"]:
        size = int(np.prod(shapes[k]))
        out[k] = flat[off:off + size].reshape(shapes[k])
        off += size
    return out


def kernel(x, mem, ffn1_norm, ffn1_gate, ffn1_up, ffn1_down, mix_norm, mem_norm, w_in, forget_bias, w_mem_k, w_mem_v, fox_q_gain, fox_k_gain, swa_q_gain, swa_k_gain, swa_sinks, mem_q_gain, mem_k_gain, w_out, ffn2_norm, ffn2_gate, ffn2_up, ffn2_down, loss_target, m_ffn1_norm, m_ffn1_gate, m_ffn1_up, m_ffn1_down, m_mix_norm, m_mem_norm, m_w_in, m_forget_bias, m_w_mem_k, m_w_mem_v, m_fox_q_gain, m_fox_k_gain, m_swa_q_gain, m_swa_k_gain, m_swa_sinks, m_mem_q_gain, m_mem_k_gain, m_w_out, m_ffn2_norm, m_ffn2_gate, m_ffn2_up, m_ffn2_down, v_ffn1_norm, v_ffn1_gate, v_ffn1_up, v_ffn1_down, v_mix_norm, v_mem_norm, v_w_in, v_forget_bias, v_w_mem_k, v_w_mem_v, v_fox_q_gain, v_fox_k_gain, v_swa_q_gain, v_swa_k_gain, v_swa_sinks, v_mem_q_gain, v_mem_k_gain, v_w_out, v_ffn2_norm, v_ffn2_gate, v_ffn2_up, v_ffn2_down):
    given = dict(locals())
    T, D = x.shape[1], x.shape[2]
    ML = mem.shape[1]
    xin = x.reshape(T, D)
    target = loss_target.reshape(T, D)
    memin = mem.reshape(ML, D)

    ids = _place_ids()
    shard = {k: given[k][0] for k in LARGE}
    started, after = [], ids
    for gi, group in enumerate(GATHER_GROUPS):
        also = {}
        if "w_in" in group:
            tied = lax.optimization_barrier((given["w_in"], given["m_w_in"], given["v_w_in"], after))
            w_in_rows = tuple(t[0] for t in tied[:3])
            shard["w_in"] = _pad_proj_cols(w_in_rows[0])
            also["w_in"] = w_in_rows[1:]
        if gi == 0:
            placed, own = zip(*[_cast_place("cast_" + k, shard[k], ids, after, keep_own=True) for k in group])
        else:
            placed = [_cast_place("cast_" + k, shard[k], ids, after, *also.get(k, ())) for k in group]
        send, recv, bufs, after = _gather_start("gather_start_%d" % gi, list(placed), after)
        started.append((send, recv, bufs))

    def arrive(gi, done):
        send, recv, bufs = started[gi]
        bufs = _gather_wait("gather_wait_%d" % gi, send, recv, bufs, done)
        return dict(zip(GATHER_GROUPS[gi], _gather_forward("gather_forward_%d" % gi, bufs)))

    gains = jnp.concatenate([fox_q_gain, fox_k_gain, swa_q_gain, swa_k_gain, mem_q_gain,
                             jnp.pad(forget_bias, ((0, 0), (0, HEAD_DIM - FOX_HEADS))), jnp.zeros((2, HEAD_DIM), F32)], axis=0)
    slopes_np = 2.0 ** (-8.0 * np.arange(1, SWA_HEADS + 1) / SWA_HEADS)
    slopes = jnp.asarray(np.repeat(slopes_np, WINDOW).reshape(SWA_KV_HEADS, GR, 1), F32)
    sinks = jnp.repeat(swa_sinks.reshape(SWA_HEADS), WINDOW).reshape(SWA_KV_HEADS, GR, 1)

    h1 = _rms_fwd("ffn1_norm_fwd", xin, ffn1_norm + after[0, 0])
    own_block = jnp.stack([jnp.zeros((), jnp.int32), ids[0]]).reshape(2, 1)
    part = _ffn_gu_blocks("ffn1_gate_up_own", h1, own[0][None], own[1][None], own_block, N_CHIPS)
    full = arrive(0, part[2])
    wg1, wu1 = full["ffn1_gate"], full["ffn1_up"]
    fg1, fu1, a1 = _ffn_gu_blocks("ffn1_gate_up", h1, wg1, wu1, jnp.stack([ids[3:], ids[3:]]), N_CHIPS, prev=part)
    full = arrive(1, a1)
    wd1 = full["ffn1_down"].reshape(-1, D)
    win = full["w_in"].reshape(D, PROJ_W)
    wmk = full["w_mem_k"].reshape(D, MEM_HEADS * HEAD_DIM)
    wmv = full["w_mem_v"].reshape(D, MEM_HEADS * HEAD_DIM)
    x1, h2 = _residual_norm("ffn1_down", a1, wd1, xin, 0.5, mix_norm, 256)
    proj = _mm2d("proj_in", h2, win, "nn", F32, tn=1408, tk=2048, n_outer=True)
    qkv, logf, k_t, v_t = _prep_fwd(proj, gains)
    cum = _cumsum_rows("forget_cumsum", [logf], False)
    cum_h = cum[:, :FOX_HEADS].T
    cq_row = cum_h.reshape(FOX_HEADS, 1, T)
    ck_rep = jnp.broadcast_to(cum_h[:, :, None], (FOX_HEADS, T, HEAD_DIM))
    mn = _rms_fwd("mem_norm_fwd", memin, mem_norm)
    mk_raw = _mm2d("mem_k_proj", mn, wmk, "nn", F32)
    mv = _mm2d("mem_v_proj", mn, wmv, "nn", BF16)
    mk = _head_norm_rows(mk_raw, mem_k_gain)
    out_a, out_a_f32, lse_a = _fox_fwd(qkv, v_t, cq_row, ck_rep)
    send, recv, bufs = started[2]
    bufs = _gather_wait("gather_wait_2", send, recv, bufs, out_a)
    send, recv, bufs, token = _gather_start("gather_pass_start_2", bufs, out_a, to_sibling=True)
    out_b, lse_b = _swa_fwd(qkv, slopes + token[0, 0], sinks)
    out_c, lse_c = _mem_fwd(qkv, mk, mv)
    mixed = jnp.concatenate([out_a, out_b, out_c], axis=1)
    full = dict(zip(GATHER_GROUPS[2], _gather_wait("gather_pass_wait_2", send, recv, bufs, mixed, to_sibling=True)))
    wo = full["w_out"].reshape(-1, D)
    wg2, wu2, wd2 = full["ffn2_gate"], full["ffn2_up"], full["ffn2_down"].reshape(-1, D)
    x2, h3 = _residual_norm("mix_out", mixed, wo, x1, 1.0, ffn2_norm, 512)
    fg2, fu2, a2 = _ffn_gu("ffn2_gate_up", h3, wg2, wu2)
    dx3, dyb3, loss_blocks = _ffn_down_loss("ffn2_down", a2, wd2, x2, target)

    grads, small, res = {}, {"loss": jnp.sum(loss_blocks[::8, 0])}, {}

    def pair_off(tag, group):
        send, recv, own, lands, token = _pair_start("grad_pair_start_" + tag, [grads[k] for k in group])
        return (group, send, recv, own, lands), token

    def chip_off(tag, started, done):
        group, send, recv, own, lands = started
        own, theirs = _pair_wait("grad_pair_wait_" + tag, send, recv, own, lands, done)
        grads.update(zip(group, own))
        to_chips = [_pair_sum_bf16("pair_sum_" + k, grads[k], b, ids) for k, b in zip(group, theirs)]
        send, recv, parts, lands, token = _chip_start("grad_chip_start_" + tag, to_chips)
        return (group, theirs, send, recv, parts, lands), token

    def reduce_half(tag, state, done):
        group, theirs, send, recv, parts, lands = state
        arrived = _chip_wait("grad_chip_wait_" + tag, send, recv, parts, lands, done)
        halves = [_chip_sum("chip_sum_" + k, grads[k], b, r, ids) for k, b, r in zip(group, theirs, arrived)]
        send, recv, shards, token = _share_start("grad_share_start_" + tag, halves)
        return (group, send, recv, shards), token

    def update(tag, shared, done):
        group, send, recv, shards = shared
        reduced = dict(zip(group, _share_wait("grad_share_wait_" + tag, send, recv, shards, done)))
        last = None
        for k in group:
            if k == "w_in":
                gk = _unpad_proj_cols(reduced[k])
                d, mo, vo = _adamw("adamw_" + k, w_in_rows[0], gk, w_in_rows[1], w_in_rows[2])
            else:
                d, mo, vo, gk = _adamw("adamw_" + k, given[k][0], reduced[k], given["m_" + k][0], given["v_" + k][0], emit_grad=True)
            res[k] = tuple(t[None] for t in (gk, d, mo, vo))
            last = vo
        return last

    dg2, du2 = _ffn_bwd_act("ffn2", dyb3, wd2, fg2, fu2, N_CHIPS)
    grads["ffn2_down"] = _ffn_bwd_down("ffn2", a2, dyb3, N_CHIPS).reshape(N_CHIPS, -1, D)
    grads["ffn2_gate"], grads["ffn2_up"] = _ffn_bwd_gate_up("ffn2", h3, dg2, du2, N_CHIPS)
    started, token = pair_off("a", ["ffn2_gate", "ffn2_up", "ffn2_down"])
    dh3 = _ffn_bwd_x("ffn2", dg2, du2, wg2, wu2, token)
    state_a, token = chip_off("a", started, dh3)
    dx2, dx2b, small["ffn2_norm"] = _rms_bwd("ffn2_norm_bwd", dh3, x2, ffn2_norm + token[0, 0], dx3, 1.0)
    dmix = _mm2d("mix_out_dx", dx2b, wo, "nt", BF16, tk=2048, n_outer=True)
    grads["w_out"] = _mm2d("mix_out_dw", mixed, dx2b, "tn", F32, tk=T, n_outer=True, resident=True).reshape(N_CHIPS, -1, D)
    delta_row = _fox_delta(dmix, out_a_f32)[:, :FOX_HEADS].T.reshape(FOX_HEADS, 1, T)
    dfq, dfk, dfv, dck, dcq = _fox_bwd(qkv, k_t, cq_row, ck_rep, delta_row, lse_a, dmix)
    dsq, dsk, dsv, dsink = _swa_bwd(qkv, slopes, sinks, out_b, lse_b, dmix)
    dmq, dmk, dmv = _mem_bwd(qkv, mk, mv, out_c, lse_c, dmix)
    small["swa_sinks"] = dsink[:, :SWA_GROUP, 0].reshape(1, SWA_HEADS)
    dcum = jnp.pad(dcq.reshape(FOX_HEADS, T).T, ((0, 0), (0, HEAD_DIM - FOX_HEADS)))
    dlogf = _cumsum_rows("forget_cumsum_bwd", [dcum], True, columns=dck)
    dproj, dgains = _prep_bwd(proj, gains, dfq, dfk, dfv, dsq, dsk, dsv, dmq, dlogf)
    for row, k in enumerate(["fox_q_gain", "fox_k_gain", "swa_q_gain", "swa_k_gain", "mem_q_gain"]):
        small[k] = dgains[row:row + 1, :]
    small["forget_bias"] = dgains[5:6, :FOX_HEADS]
    grads["w_in"] = _mm2d("proj_in_dw", h2, dproj, "tn", F32, tn=1408, tk=T, n_outer=True, resident=True).reshape(N_CHIPS, -1, PROJ_W)
    dmk_raw, small["mem_k_gain"] = _head_norm_rows_bwd(mk_raw, mem_k_gain, dmk)
    dmvb = dmv.astype(BF16)
    grads["w_mem_k"] = _mm2d("mem_k_dw", mn, dmk_raw, "tn", F32).reshape(N_CHIPS, -1, MEM_HEADS * HEAD_DIM)
    grads["w_mem_v"] = _mm2d("mem_v_dw", mn, dmvb, "tn", F32).reshape(N_CHIPS, -1, MEM_HEADS * HEAD_DIM)
    dmn = _mm2d("mem_k_dx", dmk_raw, wmk, "nt", F32)
    dmn = _mm2d("mem_v_dx", dmvb, wmv, "nt", F32, extras=[dmn], epilogue=lambda accs, ex: [ex[0] + accs[0]])
    _, _, small["mem_norm"] = _rms_bwd("mem_norm_bwd", dmn, memin, mem_norm, jnp.zeros_like(memin), 1.0)
    started, token = pair_off("b", ["w_out", "w_in", "w_mem_k", "w_mem_v"])
    dh2 = _mm2d("proj_in_dx", dproj, win, "nt", F32, tm=512, tn=1024, tk=PROJ_W, n_outer=True, resident=True, after=token)
    state_b, token = chip_off("b", started, dh2)
    dx1, dyb1, small["mix_norm"] = _rms_bwd("mix_norm_bwd", dh2, x1, mix_norm + token[0, 0], dx2, 0.5)
    grads["ffn1_down"] = _ffn_bwd_down("ffn1", a1, dyb1, N_CHIPS).reshape(N_CHIPS, -1, D)
    started, token = pair_off("c", ["ffn1_down"])
    dg1, du1 = _ffn_bwd_act("ffn1", dyb1, wd1, fg1, fu1, N_CHIPS, after=token)
    state_c, token = chip_off("c", started, dg1)
    grads["ffn1_gate"], grads["ffn1_up"] = _ffn_bwd_gate_up("ffn1", h1, dg1, du1, N_CHIPS, after=token)
    started, token = pair_off("d", ["ffn1_gate", "ffn1_up"])
    dh1 = _ffn_bwd_x("ffn1", dg1, du1, wg1, wu1, token)
    state_d, token = chip_off("d", started, dh1)
    grad_x, _, small["ffn1_norm"] = _rms_bwd("ffn1_norm_bwd", dh1, xin, ffn1_norm + token[0, 0], dx1, 1.0)

    s_send, s_recv, s_buf, s_land, token = _small_start(_pack_small(small))

    shared_a, token = reduce_half("a", state_a, token)
    shared_b, token = reduce_half("b", state_b, token)
    done = update("a", shared_a, token)
    shared_c, token = reduce_half("c", state_c, done)
    done = update("b", shared_b, token)
    shared_d, token = reduce_half("d", state_d, done)
    done = update("c", shared_c, token)
    done = update("d", shared_d, done)

    shapes = {k: given[k].shape for k in SMALL}
    shapes["loss"] = ()
    s_buf, s_land = _small_wait(s_send, s_recv, s_buf, s_land, done)
    red_small = _unpack_small(_small_sum(s_buf, s_land), shapes)
    loss = red_small["loss"]
    zero = {"loss": jnp.zeros((), F32)}
    packed = [_pack_small({**zero, **{k: src[k] for k in SMALL}}) for src in (
        {k: given[k] for k in SMALL}, red_small, {k: given["m_" + k] for k in SMALL}, {k: given["v_" + k] for k in SMALL})]
    d_s, m_s, v_s = (_unpack_small(t, shapes) for t in _adamw("adamw_small", *packed))
    for k in SMALL:
        res[k] = (red_small[k], d_s[k], m_s[k], v_s[k])

    outs = [loss, grad_x.reshape(1, T, D)]
    for part in range(4):
        outs += [res[k][part] for k in WEIGHTS]
    return tuple(outs)
```

```python
import functools

import numpy as np
import jax
import jax.numpy as jnp
from jax import lax
from jax.experimental import pallas as pl
from jax.experimental.pallas import tpu as pltpu

F32 = jnp.float32
BF16 = jnp.bfloat16
MESH = pl.DeviceIdType.MESH

HEAD_DIM = 128
FOX_HEADS = 6
SWA_HEADS = 6
SWA_KV_HEADS = 2
SWA_GROUP = SWA_HEADS // SWA_KV_HEADS
MEM_HEADS = 4
WINDOW = 128
EPS = 1e-6
NEG_INF = -1e30
SCALE = HEAD_DIM ** -0.5

C_FQ = 0
C_FK = C_FQ + FOX_HEADS * HEAD_DIM
C_FV = C_FK + FOX_HEADS * HEAD_DIM
C_SQ = C_FV + FOX_HEADS * HEAD_DIM
C_SK = C_SQ + SWA_HEADS * HEAD_DIM
C_SV = C_SK + SWA_KV_HEADS * HEAD_DIM
C_MQ = C_SV + SWA_KV_HEADS * HEAD_DIM
C_FL = C_MQ + MEM_HEADS * HEAD_DIM
PROJ_W = C_FL + HEAD_DIM
FOX_W = FOX_HEADS * HEAD_DIM
REF_GROUPS = [
    (0, FOX_W, C_FQ), (FOX_W, FOX_W, C_FK), (2 * FOX_W, FOX_W, C_FV), (3 * FOX_W, FOX_HEADS, C_FL),
    (3 * FOX_W + FOX_HEADS, SWA_HEADS * HEAD_DIM, C_SQ),
    (3 * FOX_W + FOX_HEADS + SWA_HEADS * HEAD_DIM, SWA_KV_HEADS * HEAD_DIM, C_SK),
    (3 * FOX_W + FOX_HEADS + (SWA_HEADS + SWA_KV_HEADS) * HEAD_DIM, SWA_KV_HEADS * HEAD_DIM, C_SV),
    (3 * FOX_W + FOX_HEADS + (SWA_HEADS + 2 * SWA_KV_HEADS) * HEAD_DIM, MEM_HEADS * HEAD_DIM, C_MQ),
]

ADAM_LR = 0.001
ADAM_B1 = 0.9
ADAM_B2 = 0.999
ADAM_EPS = 1e-08
ADAM_WD = 0.01
ADAM_STEP = 10

V7X_VMEM_LIMIT = 56 * 1024 * 1024
N_CHIPS = 4
N_DEV = 8


def _tile(n, pref, mult=128):
    t = (min(pref, n) // mult) * mult
    while t >= mult:
        if n % t == 0:
            return t
        t -= mult
    return n


def _params(sem):
    return pltpu.CompilerParams(dimension_semantics=sem, vmem_limit_bytes=V7X_VMEM_LIMIT)


_DIMS = {"nn": (((1,), (0,)), ((), ())), "nt": (((1,), (1,)), ((), ())), "tn": (((0,), (0,)), ((), ()))}


def _dot(a, b, mode):
    return lax.dot_general(a, b, _DIMS[mode], preferred_element_type=F32)


def _mm(name, grid, pairs, acc_of, acc_shapes, extras, outs, epilogue, after=None):
    n_p, n_e, n_o, n_a = len(pairs), len(extras), len(outs), len(acc_shapes)
    n_w = 0 if after is None else 1
    nk = grid[2]
    n_in = sum(1 if a is None else 2 for a, *_ in pairs)

    def body(*refs):
        ex = refs[n_in:n_in + n_e]
        out = refs[n_in + n_e + n_w:n_in + n_e + n_w + n_o]
        accs = refs[n_in + n_e + n_w + n_o:]
        parts = [None] * n_a
        at = 0
        for p in range(n_p):
            if pairs[p][0] is None:
                a_ref, b_ref = refs[0], refs[at]
                at += 1
            else:
                a_ref, b_ref = refs[at], refs[at + 1]
                at += 2
            d = _dot(a_ref[...], b_ref[...], pairs[p][4])
            parts[acc_of[p]] = d if parts[acc_of[p]] is None else parts[acc_of[p]] + d

        def finish(vals):
            for o, r in zip(out, epilogue(vals, [e[...] for e in ex])):
                o[...] = r.astype(o.dtype)

        if nk == 1:
            finish(parts)
            return
        k = pl.program_id(2)

        @pl.when(k == 0)
        def _():
            for a, d in zip(accs, parts):
                a[...] = d

        @pl.when((k > 0) & (k < nk - 1))
        def _():
            for a, d in zip(accs, parts):
                a[...] += d

        @pl.when(k == nk - 1)
        def _():
            finish([a[...] + d for a, d in zip(accs, parts)])

    in_specs, args = [], []
    for a, a_spec, b, b_spec, _ in pairs:
        if a is not None:
            in_specs.append(a_spec)
            args.append(a)
        in_specs.append(b_spec)
        args.append(b)
    for e, e_spec in extras:
        in_specs.append(e_spec)
        args.append(e)
    if after is not None:
        in_specs.append(pl.BlockSpec(memory_space=pl.ANY))
        args.append(after)
    res = pl.pallas_call(
        body, name=name, grid=grid, in_specs=in_specs,
        out_specs=[s for _, s in outs], out_shape=[o for o, _ in outs],
        scratch_shapes=[pltpu.VMEM(s, F32) for s in acc_shapes] if nk > 1 else [],
        compiler_params=_params(("parallel", "parallel", "arbitrary")),
    )(*args)
    return res


def _mm2d(name, a, b, mode, out_dtype, tm=512, tn=1024, tk=1024, extras=(), epilogue=None, n_out=1, after=None, n_outer=False,
          resident=False, rows=(), tile_stat=False):
    if mode == "nn":
        (M, K), N = a.shape, b.shape[1]
    elif mode == "nt":
        (M, K), N = a.shape, b.shape[0]
    else:
        (K, M), N = a.shape, b.shape[1]
    tm, tn, tk = _tile(M, tm), _tile(N, tn), _tile(K, tk)
    assert not resident or tk == K

    def spec(shape, index, single=False):
        mode_kw = {"pipeline_mode": pl.Buffered(1)} if single else {}
        if n_outer:
            return pl.BlockSpec(shape, lambda j, i, k: index(i, j, k), **mode_kw)
        return pl.BlockSpec(shape, index, **mode_kw)

    single_a, single_b = resident and not n_outer, resident and n_outer
    a_spec = spec((tk, tm), lambda i, j, k: (k, i), single_a) if mode == "tn" else spec((tm, tk), lambda i, j, k: (i, k), single_a)
    b_spec = spec((tn, tk), lambda i, j, k: (j, k), single_b) if mode == "nt" else spec((tk, tn), lambda i, j, k: (k, j), single_b)
    mn = spec((tm, tn), lambda i, j, k: (i, j))
    if epilogue is None:
        epilogue = lambda accs, ex: [accs[0]]
    if not isinstance(out_dtype, (list, tuple)):
        out_dtype = [out_dtype] * n_out
    grid = (N // tn, M // tm, K // tk) if n_outer else (M // tm, N // tn, K // tk)
    outs = [(jax.ShapeDtypeStruct((M, N), d), mn) for d in out_dtype]
    if tile_stat:
        assert tn == N
        outs.append((jax.ShapeDtypeStruct((8 * (M // tm), 128), F32), spec((8, 128), lambda i, j, k: (i, 0))))
    res = _mm(name, grid, [(a, a_spec, b, b_spec, mode)], [0], [(tm, tn)],
              [(e, mn) for e in extras] + [(r, spec((1, tn), lambda i, j, k: (0, j))) for r in rows], outs, epilogue, after=after)
    return res[0] if len(res) == 1 else res


def _sigmoid(x):
    return 1.0 / (1.0 + jnp.exp(-x))


def _sigmoid_fast(x):
    return pl.reciprocal(1.0 + jnp.exp(-x), approx=True)


def _ffn_gu(name, h, wg, wu):
    T, D = h.shape
    nf, _, F4 = wg.shape
    tm, tk = _tile(T, 512), _tile(D, 2048)
    a_spec = pl.BlockSpec((tm, tk), lambda j, i, k: (i, k))
    b_spec = pl.BlockSpec((None, tk, F4), lambda j, i, k: (j, k, 0))
    o_spec = pl.BlockSpec((tm, F4), lambda j, i, k: (i, j))

    def epilogue(accs, ex):
        g, u = accs
        s = _sigmoid_fast(g)
        gs = g * s
        return [(s + s * (g - gs)) * u, gs, gs * u]

    sds = jax.ShapeDtypeStruct((T, nf * F4), BF16)
    return _mm(name, (nf, T // tm, D // tk), [(h, a_spec, wg, b_spec, "nn"), (None, None, wu, b_spec, "nn")], [0, 1],
               [(tm, F4), (tm, F4)], [], [(sds, o_spec)] * 3, epilogue)


def _ffn_gu_blocks(name, h, wg, wu, blocks, n_total, prev=None):
    T, D = h.shape
    F4 = wg.shape[2]
    tm = _tile(T, 512)

    def body(tab, h_ref, wg_ref, wu_ref, *rest):
        out = rest[-3:]
        hv = h_ref[...]
        g, u = _dot(hv, wg_ref[...], "nn"), _dot(hv, wu_ref[...], "nn")
        s = _sigmoid_fast(g)
        gs = g * s
        out[0][...] = ((s + s * (g - gs)) * u).astype(BF16)
        out[1][...] = gs.astype(BF16)
        out[2][...] = (gs * u).astype(BF16)

    w_spec = pl.BlockSpec((None, D, F4), lambda j, i, tab: (tab[0, j], 0, 0))
    o_spec = pl.BlockSpec((tm, F4), lambda j, i, tab: (i, tab[1, j]))
    filled = [] if prev is None else list(prev)
    sds = jax.ShapeDtypeStruct((T, n_total * F4), BF16)
    return pl.pallas_call(
        body, name=name,
        grid_spec=pltpu.PrefetchScalarGridSpec(
            num_scalar_prefetch=1, grid=(blocks.shape[1], T // tm),
            in_specs=[pl.BlockSpec((tm, D), lambda j, i, tab: (i, 0)), w_spec, w_spec] + [pl.BlockSpec(memory_space=pl.ANY)] * len(filled),
            out_specs=[o_spec] * 3),
        out_shape=[sds] * 3, input_output_aliases={4 + t: t for t in range(len(filled))},
        compiler_params=_params(("parallel", "parallel")),
    )(blocks, h, wg, wu, *filled)


def _rms_rows(x, gain):
    return x * lax.rsqrt(jnp.mean(x * x, axis=-1, keepdims=True) + EPS) * gain


def _residual_norm(name, a, w, xres, scale, gain, tm):
    def epilogue(accs, ex):
        y = ex[0] + scale * accs[0]
        return [y, _rms_rows(y, ex[1])]

    return _mm2d(name, a, w, "nn", [F32, BF16], tm=tm, tn=w.shape[1], tk=w.shape[0], n_outer=True, resident=True, extras=[xres],
                 rows=[gain], epilogue=epilogue)


def _ffn_down_loss(name, a, wd, xres, target):
    D = wd.shape[1]

    def epilogue(accs, ex):
        e = ex[0] + 0.5 * accs[0] - ex[1]
        d = e * (1.0 / D)
        return [d, 0.5 * d, jnp.zeros((8, 128), F32) + (0.5 / D) * jnp.sum(e * e)]

    return _mm2d(name, a, wd, "nn", [F32, BF16], tm=256, tn=D, tk=wd.shape[0], n_outer=True, resident=True, extras=[xres, target],
                 tile_stat=True, epilogue=epilogue)


def _ffn_bwd_down(tag, a, dyb, nf, after=None):
    return _mm2d(tag + "_dwd", a, dyb, "tn", F32, tm=a.shape[1] // nf, tn=1024, tk=a.shape[0], resident=True, after=after)


def _ffn_bwd_act(tag, dyb, wd, da_dg, da_du, nf, after=None):
    def act_bwd(accs, ex):
        return [accs[0] * ex[0].astype(F32), accs[0] * ex[1].astype(F32)]

    return _mm2d(tag + "_da", dyb, wd, "nt", BF16, tn=wd.shape[0] // nf, tk=2048, extras=[da_dg, da_du], epilogue=act_bwd, n_out=2,
                 n_outer=True, after=after)


def _ffn_bwd_gate_up(tag, h, dg, du, nf, after=None):
    T, D = h.shape
    F4 = dg.shape[1] // nf
    tm = _tile(D, 512)
    h_spec = pl.BlockSpec((T, tm), lambda j, i, k: (0, i))
    d_spec = pl.BlockSpec((T, F4), lambda j, i, k: (0, j), pipeline_mode=pl.Buffered(1))
    w_spec = pl.BlockSpec((None, tm, F4), lambda j, i, k: (j, i, 0))
    sds = jax.ShapeDtypeStruct((nf, D, F4), F32)
    return _mm(tag + "_dwgu", (nf, D // tm, 1), [(h, h_spec, dg, d_spec, "tn"), (None, None, du, d_spec, "tn")],
               [0, 1], [(tm, F4), (tm, F4)], [], [(sds, w_spec)] * 2, lambda accs, ex: accs, after=after)


def _ffn_bwd_x(tag, dg, du, wg, wu, after):
    T = dg.shape[0]
    nf, D, F4 = wg.shape
    tm, tn = _tile(T, 256), _tile(D, 1024)

    def body(dg_ref, du_ref, wg_ref, wu_ref, after_ref, o_ref):
        acc = None
        for j in range(nf):
            cols = slice(j * F4, (j + 1) * F4)
            part = _dot(dg_ref[:, cols], wg_ref[j], "nt") + _dot(du_ref[:, cols], wu_ref[j], "nt")
            acc = part if acc is None else acc + part
        o_ref[...] = acc

    a_spec = pl.BlockSpec((tm, nf * F4), lambda n, i: (i, 0))
    b_spec = pl.BlockSpec((nf, tn, F4), lambda n, i: (0, n, 0), pipeline_mode=pl.Buffered(1))
    return pl.pallas_call(
        body, name=tag + "_dh", grid=(D // tn, T // tm),
        in_specs=[a_spec, a_spec, b_spec, b_spec, pl.BlockSpec(memory_space=pl.ANY)],
        out_specs=pl.BlockSpec((tm, tn), lambda n, i: (i, n)), out_shape=jax.ShapeDtypeStruct((T, D), F32),
        compiler_params=_params(("parallel", "parallel")),
    )(dg, du, wg, wu, after)


def _rms_fwd(name, x, gain):
    R, D = x.shape
    tr = _tile(R, 256, 8)

    def body(x_ref, g_ref, o_ref):
        xv = x_ref[...]
        r = lax.rsqrt(jnp.mean(xv * xv, axis=-1, keepdims=True) + EPS)
        o_ref[...] = (xv * r * g_ref[...]).astype(BF16)

    return pl.pallas_call(
        body, name=name, grid=(R // tr,),
        in_specs=[pl.BlockSpec((tr, D), lambda i: (i, 0)), pl.BlockSpec((1, D), lambda i: (0, 0))],
        out_specs=pl.BlockSpec((tr, D), lambda i: (i, 0)), out_shape=jax.ShapeDtypeStruct((R, D), BF16),
        compiler_params=_params(("parallel",)),
    )(x, gain)


def _rms_bwd(name, dh, x, gain, dres, bscale):
    R, D = x.shape
    tr = _tile(R, 256, 8)

    def body(dh_ref, x_ref, g_ref, dres_ref, dx_ref, dxb_ref, dg_ref):
        xv, dy = x_ref[...], dh_ref[...]
        r = lax.rsqrt(jnp.mean(xv * xv, axis=-1, keepdims=True) + EPS)
        xn = xv * r
        uu = dy * g_ref[...]
        dx = dres_ref[...] + r * (uu - xn * jnp.mean(xn * uu, axis=-1, keepdims=True))
        dx_ref[...] = dx
        dxb_ref[...] = (bscale * dx).astype(BF16)
        part = jnp.sum(dy * xn, axis=0, keepdims=True)

        @pl.when(pl.program_id(0) == 0)
        def _():
            dg_ref[...] = part

        @pl.when(pl.program_id(0) > 0)
        def _():
            dg_ref[...] += part

    row = pl.BlockSpec((tr, D), lambda i: (i, 0))
    vec = pl.BlockSpec((1, D), lambda i: (0, 0))
    return pl.pallas_call(
        body, name=name, grid=(R // tr,), in_specs=[row, row, vec, row], out_specs=[row, row, vec],
        out_shape=[jax.ShapeDtypeStruct((R, D), F32), jax.ShapeDtypeStruct((R, D), BF16), jax.ShapeDtypeStruct((1, D), F32)],
        compiler_params=_params(("arbitrary",)),
    )(dh, x, gain, dres)


def _head_norm(xs, g):
    r = lax.rsqrt(jnp.mean(xs * xs, axis=-1, keepdims=True) + EPS)
    return xs * r * g


def _head_norm_bwd(xs, g, dy):
    r = lax.rsqrt(jnp.mean(xs * xs, axis=-1, keepdims=True) + EPS)
    xn = xs * r
    uu = dy * g
    return r * (uu - xn * jnp.mean(xn * uu, axis=-1, keepdims=True)), jnp.sum(dy * xn, axis=0, keepdims=True)


NORMED = [(C_FQ, FOX_HEADS, 0), (C_FK, FOX_HEADS, 1), (C_SQ, SWA_HEADS, 2), (C_SK, SWA_KV_HEADS, 3), (C_MQ, MEM_HEADS, 4)]
PLAIN = [(C_FV, FOX_HEADS), (C_SV, SWA_KV_HEADS)]


def _prep_fwd(proj, gains):
    T = proj.shape[0]
    tr = _tile(T, 256, 128)

    def body(p_ref, g_ref, o_ref, lf_ref, kt_ref, vt_ref):
        for start, heads, row in NORMED:
            gn = g_ref[row:row + 1, :]
            for hh in range(heads):
                sl = slice(start + hh * HEAD_DIM, start + (hh + 1) * HEAD_DIM)
                y = _head_norm(p_ref[:, sl], gn)
                o_ref[:, sl] = y.astype(BF16)
                if start == C_FK:
                    kt_ref[hh * HEAD_DIM:(hh + 1) * HEAD_DIM, :] = y.T.astype(BF16)
        for start, heads in PLAIN:
            sl = slice(start, start + heads * HEAD_DIM)
            o_ref[:, sl] = p_ref[:, sl].astype(BF16)
        for hh in range(FOX_HEADS):
            sl = slice(C_FV + hh * HEAD_DIM, C_FV + (hh + 1) * HEAD_DIM)
            vt_ref[hh * HEAD_DIM:(hh + 1) * HEAD_DIM, :] = p_ref[:, sl].T.astype(BF16)
        zb = p_ref[:, C_FL:C_FL + HEAD_DIM] + g_ref[5:6, :]
        o_ref[:, C_FL:C_FL + HEAD_DIM] = jnp.zeros((tr, HEAD_DIM), BF16)
        lf_ref[...] = jnp.minimum(zb, 0.0) - jnp.log(1.0 + jnp.exp(-jnp.abs(zb)))

    return pl.pallas_call(
        body, name="prep_fwd", grid=(T // tr,),
        in_specs=[pl.BlockSpec((tr, PROJ_W), lambda i: (i, 0)), pl.BlockSpec((8, 128), lambda i: (0, 0))],
        out_specs=[pl.BlockSpec((tr, PROJ_W), lambda i: (i, 0)), pl.BlockSpec((tr, HEAD_DIM), lambda i: (i, 0)),
                   pl.BlockSpec((FOX_W, tr), lambda i: (0, i)), pl.BlockSpec((FOX_W, tr), lambda i: (0, i))],
        out_shape=[jax.ShapeDtypeStruct((T, PROJ_W), BF16), jax.ShapeDtypeStruct((T, HEAD_DIM), F32),
                   jax.ShapeDtypeStruct((FOX_W, T), BF16), jax.ShapeDtypeStruct((FOX_W, T), BF16)],
        compiler_params=_params(("parallel",)),
    )(proj, gains)


def _prep_bwd(proj, gains, dfq, dfk, dfv, dsq, dsk, dsv, dmq, dlogf):
    T = proj.shape[0]
    tr = _tile(T, 256, 8)
    d_normed = {C_FQ: 0, C_FK: 1, C_SQ: 3, C_SK: 4, C_MQ: 6}
    d_plain = {C_FV: 2, C_SV: 5}

    def body(p_ref, g_ref, *rest):
        d_refs, dlf_ref, o_ref, dg_ref = rest[:7], rest[7], rest[8], rest[9]
        rows = []
        for start, heads, row in NORMED:
            gn = g_ref[row:row + 1, :]
            d_ref = d_refs[d_normed[start]]
            tot = jnp.zeros((1, HEAD_DIM), F32)
            for hh in range(heads):
                sl = slice(start + hh * HEAD_DIM, start + (hh + 1) * HEAD_DIM)
                dx, dgn = _head_norm_bwd(p_ref[:, sl], gn, d_ref[:, hh * HEAD_DIM:(hh + 1) * HEAD_DIM])
                o_ref[:, sl] = dx.astype(BF16)
                tot = tot + dgn
            rows.append(tot)
        for start, heads in PLAIN:
            o_ref[:, start:start + heads * HEAD_DIM] = d_refs[d_plain[start]][...].astype(BF16)
        zb = p_ref[:, C_FL:C_FL + HEAD_DIM] + g_ref[5:6, :]
        lane = lax.broadcasted_iota(jnp.int32, (tr, HEAD_DIM), 1)
        dz = jnp.where(lane < FOX_HEADS, dlf_ref[...] * (1.0 - _sigmoid(zb)), 0.0)
        o_ref[:, C_FL:C_FL + HEAD_DIM] = dz.astype(BF16)
        rows.append(jnp.sum(dz, axis=0, keepdims=True))
        part = jnp.concatenate(rows + [jnp.zeros((2, HEAD_DIM), F32)], axis=0)

        @pl.when(pl.program_id(0) == 0)
        def _():
            dg_ref[...] = part

        @pl.when(pl.program_id(0) > 0)
        def _():
            dg_ref[...] += part

    def rows_of(w):
        return pl.BlockSpec((tr, w), lambda i: (i, 0))

    small = pl.BlockSpec((8, 128), lambda i: (0, 0))
    ds = [dfq, dfk, dfv, dsq, dsk, dsv, dmq]
    return pl.pallas_call(
        body, name="prep_bwd", grid=(T // tr,),
        in_specs=[rows_of(PROJ_W), small] + [rows_of(d.shape[1]) for d in ds] + [rows_of(HEAD_DIM)],
        out_specs=[rows_of(PROJ_W), small],
        out_shape=[jax.ShapeDtypeStruct((T, PROJ_W), BF16), jax.ShapeDtypeStruct((8, 128), F32)],
        compiler_params=_params(("arbitrary",)),
    )(proj, gains, *ds, dlogf)


def _head_norm_rows(x, gain):
    R, W = x.shape

    def body(x_ref, g_ref, o_ref):
        for hh in range(W // HEAD_DIM):
            sl = slice(hh * HEAD_DIM, (hh + 1) * HEAD_DIM)
            o_ref[:, sl] = _head_norm(x_ref[:, sl], g_ref[...]).astype(BF16)

    return pl.pallas_call(body, name="mem_k_norm", out_shape=jax.ShapeDtypeStruct((R, W), BF16))(x, gain)


def _head_norm_rows_bwd(x, gain, dy):
    R, W = x.shape

    def body(x_ref, g_ref, dy_ref, dx_ref, dg_ref):
        tot = jnp.zeros((1, HEAD_DIM), F32)
        for hh in range(W // HEAD_DIM):
            sl = slice(hh * HEAD_DIM, (hh + 1) * HEAD_DIM)
            dx, dgn = _head_norm_bwd(x_ref[:, sl], g_ref[...], dy_ref[:, sl])
            dx_ref[:, sl] = dx.astype(BF16)
            tot = tot + dgn
        dg_ref[...] = tot

    return pl.pallas_call(
        body, name="mem_k_norm_bwd",
        out_shape=[jax.ShapeDtypeStruct((R, W), BF16), jax.ShapeDtypeStruct((1, HEAD_DIM), F32)])(x, gain, dy)


def _cumsum_rows(name, xs, reverse, columns=None):
    T, W = xs[0].shape
    tb = _tile(T, 512, 8)
    nb = T // tb
    n_in = len(xs) + (0 if columns is None else 1)

    def body(*refs):
        o_ref, carry = refs[n_in], refs[n_in + 1]

        @pl.when(pl.program_id(0) == 0)
        def _():
            carry[...] = jnp.zeros_like(carry)

        xv = refs[0][...]
        for x_ref in refs[1:len(xs)]:
            xv = xv + x_ref[...]
        if columns is not None:
            lane = lax.broadcasted_iota(jnp.int32, (tb, W), 1)
            for hh in range(columns.shape[0]):
                xv = xv + jnp.where(lane == hh, refs[len(xs)][hh], 0.0)
        r = lax.broadcasted_iota(jnp.int32, (tb, tb), 0)
        cc = lax.broadcasted_iota(jnp.int32, (tb, tb), 1)
        tri = jnp.where((cc >= r) if reverse else (cc <= r), 1.0, 0.0).astype(F32)
        o_ref[...] = jnp.dot(tri, xv, precision=lax.Precision.HIGHEST, preferred_element_type=F32) + carry[...]
        carry[...] += jnp.sum(xv, axis=0, keepdims=True)

    idx = (lambda i: (nb - 1 - i, 0)) if reverse else (lambda i: (i, 0))
    in_specs = [pl.BlockSpec((tb, W), idx)] * len(xs)
    if columns is not None:
        in_specs.append(pl.BlockSpec((columns.shape[0], tb, 1), lambda i: (0, idx(i)[0], 0)))
    return pl.pallas_call(
        body, name=name, grid=(nb,), in_specs=in_specs, out_specs=pl.BlockSpec((tb, W), idx),
        out_shape=jax.ShapeDtypeStruct((T, W), F32), scratch_shapes=[pltpu.VMEM((1, W), F32)],
        compiler_params=_params(("arbitrary",)),
    )(*xs, *([] if columns is None else [columns]))


def _triangle(nq, by_column):
    if by_column:
        blocks = [(i, j) for j in range(nq) for i in range(j, nq)]
    else:
        blocks = [(i, j) for i in range(nq) for j in range(i + 1)]
    return jnp.asarray(np.array(blocks, np.int32).T)


def _fox_scores_t(k, q, cq_row, ck_rep, on_diagonal):
    n = q.shape[0]
    s = _dot(k, q, "nt") * SCALE + (cq_row - jnp.tile(ck_rep, (1, n // HEAD_DIM)))
    if on_diagonal:
        s = jnp.where(lax.broadcasted_iota(jnp.int32, (n, n), 0) <= lax.broadcasted_iota(jnp.int32, (n, n), 1), s, NEG_INF)
    return s


def _fox_fwd(qkv, v_t, cq_row, ck_rep):
    T = qkv.shape[0]
    tq = _tile(T, 1024)
    nq = T // tq
    steps = nq * (nq + 1) // 2
    HQ, HK = C_FQ // HEAD_DIM, C_FK // HEAD_DIM

    def body(tab, q_ref, k_ref, vt_ref, cq_ref, ck_ref, o_ref, of_ref, lse_ref, m_sc, l_sc, acc_sc):
        i, j = tab[0, pl.program_id(1)], tab[1, pl.program_id(1)]

        @pl.when(j == 0)
        def _():
            m_sc[...] = jnp.full_like(m_sc, NEG_INF)
            l_sc[...] = jnp.zeros_like(l_sc)
            acc_sc[...] = jnp.zeros_like(acc_sc)

        def step(on_diagonal):
            s = _fox_scores_t(k_ref[...], q_ref[...], cq_ref[...], ck_ref[...], on_diagonal)
            m_new = jnp.maximum(m_sc[...], jnp.max(s, axis=0, keepdims=True))
            alpha = jnp.exp(m_sc[...] - m_new)
            p = jnp.exp(s - m_new)
            l_sc[...] = alpha * l_sc[...] + jnp.sum(p, axis=0, keepdims=True)
            acc_sc[...] = alpha * acc_sc[...] + _dot(vt_ref[...], p.astype(BF16), "nn")
            m_sc[...] = m_new

        @pl.when(j < i)
        def _():
            step(False)

        @pl.when(j == i)
        def _():
            step(True)
            o = (acc_sc[...] / l_sc[...]).T
            o_ref[...] = o.astype(BF16)
            of_ref[...] = o
            lse_ref[...] = m_sc[...] + jnp.log(l_sc[...])

    qrow = pl.BlockSpec((None, 1, tq), lambda h, s, tab: (h, 0, tab[0, s]))
    return pl.pallas_call(
        body, name="fox_fwd",
        grid_spec=pltpu.PrefetchScalarGridSpec(
            num_scalar_prefetch=1, grid=(FOX_HEADS, steps),
            in_specs=[pl.BlockSpec((tq, HEAD_DIM), lambda h, s, tab: (tab[0, s], HQ + h)),
                      pl.BlockSpec((tq, HEAD_DIM), lambda h, s, tab: (tab[1, s], HK + h)),
                      pl.BlockSpec((HEAD_DIM, tq), lambda h, s, tab: (h, tab[1, s])), qrow,
                      pl.BlockSpec((None, tq, HEAD_DIM), lambda h, s, tab: (h, tab[1, s], 0))],
            out_specs=[pl.BlockSpec((tq, HEAD_DIM), lambda h, s, tab: (tab[0, s], h)),
                       pl.BlockSpec((tq, HEAD_DIM), lambda h, s, tab: (tab[0, s], h)), qrow],
            scratch_shapes=[pltpu.VMEM((1, tq), F32), pltpu.VMEM((1, tq), F32), pltpu.VMEM((HEAD_DIM, tq), F32)]),
        out_shape=[jax.ShapeDtypeStruct((T, FOX_W), BF16), jax.ShapeDtypeStruct((T, FOX_W), F32),
                   jax.ShapeDtypeStruct((FOX_HEADS, 1, T), F32)],
        compiler_params=_params(("parallel", "arbitrary")),
    )(_triangle(nq, False), qkv, qkv, v_t, cq_row, ck_rep)


def _fox_delta(dmix, out_f32):
    T = out_f32.shape[0]
    tr = _tile(T, 512, 8)

    def body(do_ref, o_ref, d_ref):
        lane = lax.broadcasted_iota(jnp.int32, (tr, HEAD_DIM), 1)
        acc = jnp.zeros((tr, HEAD_DIM), F32)
        for hh in range(FOX_HEADS):
            sl = slice(hh * HEAD_DIM, (hh + 1) * HEAD_DIM)
            d = jnp.sum(do_ref[:, sl].astype(F32) * o_ref[:, sl], axis=-1, keepdims=True)
            acc = jnp.where(lane == hh, d, acc)
        d_ref[...] = acc

    blk = pl.BlockSpec((tr, FOX_W), lambda i: (i, 0))
    return pl.pallas_call(
        body, name="fox_delta", grid=(T // tr,), in_specs=[blk, blk], out_specs=pl.BlockSpec((tr, HEAD_DIM), lambda i: (i, 0)),
        out_shape=jax.ShapeDtypeStruct((T, HEAD_DIM), F32), compiler_params=_params(("parallel",)),
    )(dmix, out_f32)


def _fox_bwd(qkv, k_t, cq_row, ck_rep, delta_row, lse, dmix):
    T = qkv.shape[0]
    tq = _tile(T, 1024)
    nq = T // tq
    steps = nq * (nq + 1) // 2
    HQ, HK, HV = C_FQ // HEAD_DIM, C_FK // HEAD_DIM, C_FV // HEAD_DIM

    def body(tab, q_ref, k_ref, kt_ref, v_ref, cq_ref, ck_ref, delta_ref, lse_ref, do_ref,
             dq_ref, dk_ref, dv_ref, dck_ref, dcq_ref, dk_sc, dv_sc, dc_sc, dqt_sc):
        qi, kj = tab[0, pl.program_id(1)], tab[1, pl.program_id(1)]

        @pl.when(qi == kj)
        def _():
            dk_sc[...] = jnp.zeros_like(dk_sc)
            dv_sc[...] = jnp.zeros_like(dv_sc)
            dc_sc[...] = jnp.zeros_like(dc_sc)

        def step(on_diagonal):
            q, k, v, do = q_ref[...], k_ref[...], v_ref[...], do_ref[...]
            p = jnp.exp(_fox_scores_t(k, q, cq_ref[...], ck_ref[...], on_diagonal) - lse_ref[...])
            dp = _dot(v, do, "nt")
            ds = p * (dp - delta_ref[...])
            dsb = ds.astype(BF16)
            dv_sc[...] += _dot(p.astype(BF16), do, "nn")
            dk_sc[...] += _dot(dsb, q, "nn")
            dc_sc[...] += jnp.sum(ds, axis=1, keepdims=True)
            dq_part = _dot(kt_ref[...], dsb, "nn") * SCALE
            dcq_part = jnp.sum(ds, axis=0, keepdims=True)

            @pl.when(kj == 0)
            def _():
                dqt_sc[qi] = dq_part
                dcq_ref[qi] = dcq_part

            @pl.when(kj > 0)
            def _():
                dqt_sc[qi] += dq_part
                dcq_ref[qi] += dcq_part

            if on_diagonal:
                dq_ref[...] = dqt_sc[qi].T

        @pl.when(qi > kj)
        def _():
            step(False)

        @pl.when(qi == kj)
        def _():
            step(True)

        @pl.when(qi == nq - 1)
        def _():
            dk_ref[...] = dk_sc[...] * SCALE
            dv_ref[...] = dv_sc[...]
            dck_ref[...] = -dc_sc[...]

    def rows(base):
        return pl.BlockSpec((tq, HEAD_DIM), lambda h, s, tab: (tab[0, s], base + h))

    def cols(base):
        return pl.BlockSpec((tq, HEAD_DIM), lambda h, s, tab: (tab[1, s], base + h))

    qrow = pl.BlockSpec((None, 1, tq), lambda h, s, tab: (h, 0, tab[0, s]))
    sds = jax.ShapeDtypeStruct((T, FOX_W), F32)
    return pl.pallas_call(
        body, name="fox_bwd",
        grid_spec=pltpu.PrefetchScalarGridSpec(
            num_scalar_prefetch=1, grid=(FOX_HEADS, steps),
            in_specs=[rows(HQ), cols(HK), pl.BlockSpec((HEAD_DIM, tq), lambda h, s, tab: (h, tab[1, s])), cols(HV), qrow,
                      pl.BlockSpec((None, tq, HEAD_DIM), lambda h, s, tab: (h, tab[1, s], 0)), qrow, qrow, rows(0)],
            out_specs=[cols(0), cols(0), cols(0), pl.BlockSpec((None, tq, 1), lambda h, s, tab: (h, tab[1, s], 0)),
                       pl.BlockSpec((None, nq, 1, tq), lambda h, s, tab: (h, 0, 0, 0))],
            scratch_shapes=[pltpu.VMEM((tq, HEAD_DIM), F32), pltpu.VMEM((tq, HEAD_DIM), F32), pltpu.VMEM((tq, 1), F32),
                            pltpu.VMEM((nq, HEAD_DIM, tq), F32)]),
        out_shape=[sds, sds, sds, jax.ShapeDtypeStruct((FOX_HEADS, T, 1), F32), jax.ShapeDtypeStruct((FOX_HEADS, nq, 1, tq), F32)],
        compiler_params=_params(("parallel", "arbitrary")),
    )(_triangle(nq, True), qkv, qkv, k_t, qkv, cq_row, ck_rep, delta_row, lse, dmix)


GW = SWA_GROUP * HEAD_DIM
GR = SWA_GROUP * WINDOW


def _swa_scores(q_ref, kp_ref, kc_ref, slope_ref, n):
    q = q_ref[...]
    qs = jnp.concatenate([q[:, t * HEAD_DIM:(t + 1) * HEAD_DIM] for t in range(SWA_GROUP)], axis=0)
    kb = jnp.concatenate([kp_ref[...], kc_ref[...]], axis=0)
    r = lax.broadcasted_iota(jnp.int32, (GR, 2 * WINDOW), 0) & (WINDOW - 1)
    jj = lax.broadcasted_iota(jnp.int32, (GR, 2 * WINDOW), 1)
    dist = WINDOW + r - jj
    valid = (dist >= 0) & (dist < WINDOW) & ((n > 0) | (jj >= WINDOW))
    s = _dot(qs, kb, "nt") * SCALE - slope_ref[...] * dist.astype(F32)
    return qs, kb, jnp.where(valid, s, NEG_INF), valid


def _swa_specs():
    HQ, HK, HV = C_SQ // GW, C_SK // HEAD_DIM, C_SV // HEAD_DIM
    q_spec = pl.BlockSpec((WINDOW, GW), lambda g, n: (n, HQ + g))

    def prev(base):
        return pl.BlockSpec((WINDOW, HEAD_DIM), lambda g, n: (jnp.maximum(n - 1, 0), base + g))

    def cur(base):
        return pl.BlockSpec((WINDOW, HEAD_DIM), lambda g, n: (n, base + g))

    col = pl.BlockSpec((None, GR, 1), lambda g, n: (g, 0, 0))
    return q_spec, prev(HK), cur(HK), prev(HV), cur(HV), col


def _swa_fwd(qkv, slopes, sinks):
    T = qkv.shape[0]
    nb = T // WINDOW
    assert C_SQ % GW == 0

    def body(q_ref, kp_ref, kc_ref, vp_ref, vc_ref, slope_ref, sink_ref, o_ref, lse_ref):
        n = pl.program_id(1)
        _, _, s, _ = _swa_scores(q_ref, kp_ref, kc_ref, slope_ref, n)
        m = jnp.maximum(jnp.max(s, axis=-1, keepdims=True), sink_ref[...])
        p = jnp.exp(s - m)
        l = jnp.sum(p, axis=-1, keepdims=True) + jnp.exp(sink_ref[...] - m)
        vb = jnp.concatenate([vp_ref[...], vc_ref[...]], axis=0)
        o = _dot(p.astype(BF16), vb, "nn") / l
        for t in range(SWA_GROUP):
            o_ref[:, t * HEAD_DIM:(t + 1) * HEAD_DIM] = o[t * WINDOW:(t + 1) * WINDOW, :].astype(BF16)
        lse_ref[...] = m + jnp.log(l)

    q_spec, kp, kc, vp, vc, col = _swa_specs()
    return pl.pallas_call(
        body, name="swa_fwd", grid=(SWA_KV_HEADS, nb), in_specs=[q_spec, kp, kc, vp, vc, col, col],
        out_specs=[pl.BlockSpec((WINDOW, GW), lambda g, n: (n, g)), pl.BlockSpec((None, None, GR, 1), lambda g, n: (g, n, 0, 0))],
        out_shape=[jax.ShapeDtypeStruct((T, SWA_HEADS * HEAD_DIM), BF16), jax.ShapeDtypeStruct((SWA_KV_HEADS, nb, GR, 1), F32)],
        compiler_params=_params(("parallel", "arbitrary")),
    )(qkv, qkv, qkv, qkv, qkv, slopes, sinks)


def _swa_bwd(qkv, slopes, sinks, out, lse, dmix):
    T = qkv.shape[0]
    nb = T // WINDOW
    DO = FOX_W // GW
    assert FOX_W % GW == 0

    def body(q_ref, kp_ref, kc_ref, vp_ref, vc_ref, slope_ref, sink_ref, o_ref, lse_ref, do_ref,
             dq_ref, dk_ref, dv_ref, dsink_ref, sink_sc):
        n = pl.program_id(1)

        @pl.when(n == 0)
        def _():
            dk_ref[...] = jnp.zeros_like(dk_ref)
            dv_ref[...] = jnp.zeros_like(dv_ref)
            sink_sc[...] = jnp.zeros_like(sink_sc)

        qs, kb, s, valid = _swa_scores(q_ref, kp_ref, kc_ref, slope_ref, n)
        lse = lse_ref[...]
        p = jnp.where(valid, jnp.exp(s - lse), 0.0)
        vb = jnp.concatenate([vp_ref[...], vc_ref[...]], axis=0)
        do = jnp.concatenate([do_ref[:, t * HEAD_DIM:(t + 1) * HEAD_DIM] for t in range(SWA_GROUP)], axis=0)
        oo = jnp.concatenate([o_ref[:, t * HEAD_DIM:(t + 1) * HEAD_DIM] for t in range(SWA_GROUP)], axis=0)
        dp = _dot(do, vb, "nt")
        delta = jnp.sum(do.astype(F32) * oo.astype(F32), axis=-1, keepdims=True)
        ds = p * (dp - delta)
        dsb = ds.astype(BF16)
        dq = _dot(dsb, kb, "nn") * SCALE
        for t in range(SWA_GROUP):
            dq_ref[:, t * HEAD_DIM:(t + 1) * HEAD_DIM] = dq[t * WINDOW:(t + 1) * WINDOW, :]
        dkb = _dot(dsb, qs, "tn") * SCALE
        dvb = _dot(p.astype(BF16), do, "tn")
        r_prev = pl.ds(pl.multiple_of(jnp.maximum(n - 1, 0) * WINDOW, WINDOW), WINDOW)
        r_cur = pl.ds(pl.multiple_of(n * WINDOW, WINDOW), WINDOW)
        dk_ref[r_prev, :] += dkb[:WINDOW, :]
        dk_ref[r_cur, :] += dkb[WINDOW:, :]
        dv_ref[r_prev, :] += dvb[:WINDOW, :]
        dv_ref[r_cur, :] += dvb[WINDOW:, :]
        sink_sc[...] -= jnp.exp(sink_ref[...] - lse) * delta

        @pl.when(n == nb - 1)
        def _():
            tot = [jnp.zeros((1, 128), F32) + jnp.sum(sink_sc[t * WINDOW:(t + 1) * WINDOW, :]) for t in range(SWA_GROUP)]
            dsink_ref[...] = jnp.concatenate(tot + [jnp.zeros((8 - SWA_GROUP, 128), F32)], axis=0)

    q_spec, kp, kc, vp, vc, col = _swa_specs()
    kv_acc = pl.BlockSpec((T, HEAD_DIM), lambda g, n: (0, g))
    return pl.pallas_call(
        body, name="swa_bwd", grid=(SWA_KV_HEADS, nb),
        in_specs=[q_spec, kp, kc, vp, vc, col, col, pl.BlockSpec((WINDOW, GW), lambda g, n: (n, g)),
                  pl.BlockSpec((None, None, GR, 1), lambda g, n: (g, n, 0, 0)), pl.BlockSpec((WINDOW, GW), lambda g, n: (n, DO + g))],
        out_specs=[pl.BlockSpec((WINDOW, GW), lambda g, n: (n, g)), kv_acc, kv_acc, pl.BlockSpec((None, 8, 128), lambda g, n: (g, 0, 0))],
        out_shape=[jax.ShapeDtypeStruct((T, SWA_HEADS * HEAD_DIM), F32), jax.ShapeDtypeStruct((T, SWA_KV_HEADS * HEAD_DIM), F32),
                   jax.ShapeDtypeStruct((T, SWA_KV_HEADS * HEAD_DIM), F32), jax.ShapeDtypeStruct((SWA_KV_HEADS, 8, 128), F32)],
        scratch_shapes=[pltpu.VMEM((GR, 1), F32)],
        compiler_params=_params(("parallel", "arbitrary")),
    )(qkv, qkv, qkv, qkv, qkv, slopes, sinks, out, lse, dmix)


def _mem_fwd(qkv, mk, mv):
    T, ML = qkv.shape[0], mk.shape[0]
    tq = _tile(T, 1024)
    HQ = C_MQ // HEAD_DIM

    def body(q_ref, k_ref, v_ref, o_ref, lse_ref):
        s = _dot(q_ref[...], k_ref[...], "nt") * SCALE
        m = jnp.max(s, axis=-1, keepdims=True)
        p = jnp.exp(s - m)
        l = jnp.sum(p, axis=-1, keepdims=True)
        o_ref[...] = (_dot(p.astype(BF16), v_ref[...], "nn") / l).astype(BF16)
        lse_ref[...] = m + jnp.log(l)

    kv = pl.BlockSpec((ML, HEAD_DIM), lambda h, i: (0, h))
    return pl.pallas_call(
        body, name="mem_fwd", grid=(MEM_HEADS, T // tq),
        in_specs=[pl.BlockSpec((tq, HEAD_DIM), lambda h, i: (i, HQ + h)), kv, kv],
        out_specs=[pl.BlockSpec((tq, HEAD_DIM), lambda h, i: (i, h)), pl.BlockSpec((None, tq, 1), lambda h, i: (h, i, 0))],
        out_shape=[jax.ShapeDtypeStruct((T, MEM_HEADS * HEAD_DIM), BF16), jax.ShapeDtypeStruct((MEM_HEADS, T, 1), F32)],
        compiler_params=_params(("parallel", "arbitrary")),
    )(qkv, mk, mv)


def _mem_bwd(qkv, mk, mv, out, lse, dmix):
    T, ML = qkv.shape[0], mk.shape[0]
    tq = _tile(T, 1024)
    HQ = C_MQ // HEAD_DIM
    DO = (FOX_W + SWA_HEADS * HEAD_DIM) // HEAD_DIM

    def body(q_ref, k_ref, v_ref, o_ref, lse_ref, do_ref, dq_ref, dk_ref, dv_ref):
        q, k, v, do = q_ref[...], k_ref[...], v_ref[...], do_ref[...]
        p = jnp.exp(_dot(q, k, "nt") * SCALE - lse_ref[...])
        dp = _dot(do, v, "nt")
        delta = jnp.sum(do.astype(F32) * o_ref[...].astype(F32), axis=-1, keepdims=True)
        dsb = (p * (dp - delta)).astype(BF16)
        dq_ref[...] = _dot(dsb, k, "nn") * SCALE
        dk_part = _dot(dsb, q, "tn") * SCALE
        dv_part = _dot(p.astype(BF16), do, "tn")

        @pl.when(pl.program_id(1) == 0)
        def _():
            dk_ref[...] = dk_part
            dv_ref[...] = dv_part

        @pl.when(pl.program_id(1) > 0)
        def _():
            dk_ref[...] += dk_part
            dv_ref[...] += dv_part

    kv = pl.BlockSpec((ML, HEAD_DIM), lambda h, i: (0, h))
    qb = pl.BlockSpec((tq, HEAD_DIM), lambda h, i: (i, h))
    return pl.pallas_call(
        body, name="mem_bwd", grid=(MEM_HEADS, T // tq),
        in_specs=[pl.BlockSpec((tq, HEAD_DIM), lambda h, i: (i, HQ + h)), kv, kv, qb,
                  pl.BlockSpec((None, tq, 1), lambda h, i: (h, i, 0)), pl.BlockSpec((tq, HEAD_DIM), lambda h, i: (i, DO + h))],
        out_specs=[qb, kv, kv],
        out_shape=[jax.ShapeDtypeStruct((T, MEM_HEADS * HEAD_DIM), F32), jax.ShapeDtypeStruct((ML, MEM_HEADS * HEAD_DIM), F32),
                   jax.ShapeDtypeStruct((ML, MEM_HEADS * HEAD_DIM), F32)],
        compiler_params=_params(("parallel", "arbitrary")),
    )(qkv, mk, mv, out, lse, dmix)


HBM = pl.BlockSpec(memory_space=pltpu.HBM)


def _place():
    x, y, c = lax.axis_index("x"), lax.axis_index("y"), lax.axis_index("c")
    chips = [(1 - x, y), (x, 1 - y), (1 - x, 1 - y)]
    return x, y, c, chips


def _remote(src, dst, send_sem, recv_sem, device):
    return pltpu.make_async_remote_copy(src_ref=src, dst_ref=dst, send_sem=send_sem, recv_sem=recv_sem,
                                        device_id=device, device_id_type=MESH)


def _place_ids():
    x, y, c = lax.axis_index("x"), lax.axis_index("y"), lax.axis_index("c")
    order = [2 * x + y, 2 * (1 - x) + y, 2 * x + (1 - y), 2 * (1 - x) + (1 - y)]
    return jnp.stack([2 * x + y, c] + order).astype(jnp.int32)


def _cast_place(name, w, ids, *after, keep_own=False):
    R, C = w.shape
    tr = _tile(R, 256, 16)
    n_out = 2 if keep_own else 1

    def body(ids_ref, w_ref, *rest):
        for o_ref in rest[-n_out:]:
            o_ref[...] = w_ref[...].astype(BF16)

    res = pl.pallas_call(
        body, name=name,
        grid_spec=pltpu.PrefetchScalarGridSpec(
            num_scalar_prefetch=1, grid=(R // tr,),
            in_specs=[pl.BlockSpec((tr, C), lambda i, ids: (i, 0))] + [pl.BlockSpec(memory_space=pl.ANY)] * len(after),
            out_specs=[pl.BlockSpec((None, tr, C), lambda i, ids: (ids[0], i, 0)), pl.BlockSpec((tr, C), lambda i, ids: (i, 0))][:n_out]),
        out_shape=[jax.ShapeDtypeStruct((N_CHIPS, R, C), BF16), jax.ShapeDtypeStruct((R, C), BF16)][:n_out],
        compiler_params=_params(("parallel",)),
    )(ids, w, *after)
    return res if keep_own else res[0]


SEM = pl.BlockSpec(memory_space=pltpu.SEMAPHORE)
EFFECT = pltpu.SideEffectType.DATAFLOW_SIDE_EFFECTING


def _hbm(a):
    return pltpu.with_memory_space_constraint(a, pltpu.HBM)


def _gather_start(name, placed, after, to_sibling=False):
    n = len(placed)

    ns = 3 * n

    def body(*refs):
        send, recv = refs[n + 1:n + 1 + ns], refs[n + 1 + ns:n + 1 + 2 * ns]
        buf = refs[n + 1 + 2 * ns:2 * n + 1 + 2 * ns]
        token = refs[2 * n + 1 + 2 * ns]
        x, y, c, chips = _place()
        me = 2 * x + y
        for a in range(n):
            half = buf[a].shape[1] // 2
            for j, (cx, cy) in enumerate(chips):
                block, peer = (2 * cx + cy, (x, y, 1 - c)) if to_sibling else (me, (cx, cy, c))
                part = buf[a].at[block, pl.ds(c * half, half)]
                _remote(part, part, send[3 * a + j], recv[3 * a + j], peer).start()
        token[...] = jnp.zeros_like(token)

    res = pl.pallas_call(
        body, name=name, in_specs=[HBM] * n + [pl.BlockSpec(memory_space=pl.ANY)],
        out_specs=[SEM] * (2 * ns) + [HBM] * n + [pl.BlockSpec(memory_space=pltpu.VMEM)],
        out_shape=[pltpu.SemaphoreType.DMA(())] * (2 * ns)
        + [pltpu.HBM(s.shape, s.dtype) for s in placed] + [jax.ShapeDtypeStruct((8, 128), F32)],
        input_output_aliases={a: 2 * ns + a for a in range(n)},
        compiler_params=pltpu.CompilerParams(has_side_effects=EFFECT),
    )(*[_hbm(s) for s in placed], after)
    return list(res[:ns]), list(res[ns:2 * ns]), list(res[2 * ns:2 * ns + n]), res[2 * ns + n]


def _gather_wait(name, send, recv, bufs, after, to_sibling=False):
    n = len(bufs)

    ns = 3 * n

    def body(*refs):
        buf = refs[:n]
        send_ref, recv_ref = refs[n:n + ns], refs[n + ns:n + 2 * ns]
        x, y, c, chips = _place()
        ids = [2 * cx + cy for cx, cy in chips]
        for a in range(n):
            half = buf[a].shape[1] // 2
            for j in range(3):
                sent = buf[a].at[ids[j], pl.ds(c * half, half)]
                landed = buf[a].at[ids[j], pl.ds((1 - c) * half, half)] if to_sibling else sent
                cp = _remote(sent, landed, send_ref[3 * a + j], recv_ref[3 * a + j], (x, y, c))
                cp.wait_send()
                cp.wait_recv()

    res = pl.pallas_call(
        body, name=name, in_specs=[HBM] * n + [SEM] * (2 * ns) + [pl.BlockSpec(memory_space=pl.ANY)], out_specs=[HBM] * n,
        out_shape=[pltpu.HBM(s.shape, s.dtype) for s in bufs], input_output_aliases={a: a for a in range(n)},
        compiler_params=pltpu.CompilerParams(has_side_effects=EFFECT),
    )(*bufs, *send, *recv, after)
    return list(res)


def _gather_forward(name, bufs):
    n = len(bufs)

    def body(*refs):
        buf = refs[n:2 * n]
        send, recv = refs[2 * n:]
        x, y, c, chips = _place()
        ids = [2 * cx + cy for cx, cy in chips]
        copies = []
        for a in range(n):
            half = buf[a].shape[1] // 2
            for j in range(3):
                landed = buf[a].at[ids[j], pl.ds(c * half, half)]
                cp = _remote(landed, landed, send.at[a, j], recv.at[a, j], (x, y, 1 - c))
                cp.start()
                copies.append(cp)
        for a in range(n):
            half = buf[a].shape[1] // 2
            for j in range(3):
                landed = buf[a].at[ids[j], pl.ds((1 - c) * half, half)]
                _remote(landed, landed, send.at[a, j], recv.at[a, j], (x, y, c)).wait_recv()
        for cp in copies:
            cp.wait_send()

    return pl.pallas_call(
        body, name=name, in_specs=[HBM] * n, out_specs=[HBM] * n,
        out_shape=[jax.ShapeDtypeStruct(s.shape, s.dtype) for s in bufs], input_output_aliases={a: a for a in range(n)},
        scratch_shapes=[pltpu.SemaphoreType.DMA((n, 3)), pltpu.SemaphoreType.DMA((n, 3))],
    )(*bufs)


def _pair_start(name, grads):
    n = len(grads)
    ns = N_CHIPS * n

    def body(*refs):
        send, recv = refs[2 * n:2 * n + ns], refs[2 * n + ns:2 * n + 2 * ns]
        src = refs[2 * n + 2 * ns:3 * n + 2 * ns]
        land = refs[3 * n + 2 * ns:4 * n + 2 * ns]
        token = refs[4 * n + 2 * ns]
        x, y, c, chips = _place()
        order = [2 * x + y] + [2 * cx + cy for cx, cy in chips]
        for a in range(n):
            half = src[a].shape[1] // 2
            for j in range(N_CHIPS):
                _remote(src[a].at[order[j], pl.ds((1 - c) * half, half)], land[a].at[j],
                        send[N_CHIPS * a + j], recv[N_CHIPS * a + j], (x, y, 1 - c)).start()
        token[...] = jnp.zeros_like(token)

    lands = [jax.ShapeDtypeStruct((N_CHIPS, g.shape[1] // 2, g.shape[2]), g.dtype) for g in grads]
    res = pl.pallas_call(
        body, name=name, in_specs=[HBM] * (2 * n),
        out_specs=[SEM] * (2 * ns) + [HBM] * (2 * n) + [pl.BlockSpec(memory_space=pltpu.VMEM)],
        out_shape=[pltpu.SemaphoreType.DMA(())] * (2 * ns) + [pltpu.HBM(g.shape, g.dtype) for g in grads]
        + [pltpu.HBM(l.shape, l.dtype) for l in lands] + [jax.ShapeDtypeStruct((8, 128), F32)],
        input_output_aliases={a: 2 * ns + a for a in range(2 * n)},
        compiler_params=pltpu.CompilerParams(has_side_effects=EFFECT),
    )(*[_hbm(g) for g in grads], *[_hbm(lax.empty(l.shape, l.dtype)) for l in lands])
    return list(res[:ns]), list(res[ns:2 * ns]), list(res[2 * ns:2 * ns + n]), list(res[2 * ns + n:2 * ns + 2 * n]), res[2 * ns + 2 * n]


def _pair_wait(name, send, recv, grads, lands, after):
    n = len(grads)
    ns = N_CHIPS * n

    def body(*refs):
        src, land = refs[:n], refs[n:2 * n]
        send_ref, recv_ref = refs[2 * n:2 * n + ns], refs[2 * n + ns:2 * n + 2 * ns]
        x, y, c, _ = _place()
        for a in range(n):
            for j in range(N_CHIPS):
                cp = _remote(land[a].at[j], land[a].at[j], send_ref[N_CHIPS * a + j], recv_ref[N_CHIPS * a + j], (x, y, c))
                cp.wait_send()
                cp.wait_recv()

    res = pl.pallas_call(
        body, name=name, in_specs=[HBM] * (2 * n) + [SEM] * (2 * ns) + [pl.BlockSpec(memory_space=pl.ANY)],
        out_specs=[HBM] * (2 * n), out_shape=[pltpu.HBM(g.shape, g.dtype) for g in grads] + [pltpu.HBM(l.shape, l.dtype) for l in lands],
        input_output_aliases={a: a for a in range(2 * n)},
        compiler_params=pltpu.CompilerParams(has_side_effects=EFFECT),
    )(*grads, *lands, *send, *recv, after)
    return list(res[:n]), list(res[n:])


def _chip_start(name, parts):
    n = len(parts)
    ns = 3 * n

    def body(*refs):
        send, recv = refs[2 * n:2 * n + ns], refs[2 * n + ns:2 * n + 2 * ns]
        src = refs[2 * n + 2 * ns:3 * n + 2 * ns]
        land = refs[3 * n + 2 * ns:4 * n + 2 * ns]
        token = refs[4 * n + 2 * ns]
        x, y, c, chips = _place()
        for a in range(n):
            for j, (cx, cy) in enumerate(chips):
                _remote(src[a].at[j], land[a].at[j], send[3 * a + j], recv[3 * a + j], (cx, cy, c)).start()
        token[...] = jnp.zeros_like(token)

    res = pl.pallas_call(
        body, name=name, in_specs=[HBM] * (2 * n),
        out_specs=[SEM] * (2 * ns) + [HBM] * (2 * n) + [pl.BlockSpec(memory_space=pltpu.VMEM)],
        out_shape=[pltpu.SemaphoreType.DMA(())] * (2 * ns) + [pltpu.HBM(p.shape, p.dtype) for p in parts] * 2
        + [jax.ShapeDtypeStruct((8, 128), F32)],
        input_output_aliases={a: 2 * ns + a for a in range(2 * n)},
        compiler_params=pltpu.CompilerParams(has_side_effects=EFFECT),
    )(*[_hbm(p) for p in parts], *[_hbm(lax.empty(p.shape, p.dtype)) for p in parts])
    return list(res[:ns]), list(res[ns:2 * ns]), list(res[2 * ns:2 * ns + n]), list(res[2 * ns + n:2 * ns + 2 * n]), res[2 * ns + 2 * n]


def _chip_wait(name, send, recv, parts, lands, after):
    n = len(parts)
    ns = 3 * n

    def body(*refs):
        src, land = refs[:n], refs[n:2 * n]
        send_ref, recv_ref = refs[2 * n:2 * n + ns], refs[2 * n + ns:2 * n + 2 * ns]
        x, y, c, _ = _place()
        for a in range(n):
            for j in range(3):
                cp = _remote(src[a].at[j], land[a].at[j], send_ref[3 * a + j], recv_ref[3 * a + j], (x, y, c))
                cp.wait_send()
                cp.wait_recv()

    res = pl.pallas_call(
        body, name=name, in_specs=[HBM] * (2 * n) + [SEM] * (2 * ns) + [pl.BlockSpec(memory_space=pl.ANY)],
        out_specs=[HBM] * (2 * n), out_shape=[pltpu.HBM(p.shape, p.dtype) for p in parts] * 2,
        input_output_aliases={a: a for a in range(2 * n)},
        compiler_params=pltpu.CompilerParams(has_side_effects=EFFECT),
    )(*parts, *lands, *send, *recv, after)
    return list(res[n:])


def _share_start(name, shards):
    n = len(shards)

    def body(*refs):
        send, recv = refs[n:2 * n], refs[2 * n:3 * n]
        buf = refs[3 * n:4 * n]
        token = refs[4 * n]
        x, y, c, _ = _place()
        for a in range(n):
            half = buf[a].shape[0] // 2
            mine = buf[a].at[pl.ds(c * half, half)]
            _remote(mine, mine, send[a], recv[a], (x, y, 1 - c)).start()
        token[...] = jnp.zeros_like(token)

    res = pl.pallas_call(
        body, name=name, in_specs=[HBM] * n,
        out_specs=[SEM] * (2 * n) + [HBM] * n + [pl.BlockSpec(memory_space=pltpu.VMEM)],
        out_shape=[pltpu.SemaphoreType.DMA(())] * (2 * n) + [pltpu.HBM(s.shape, s.dtype) for s in shards]
        + [jax.ShapeDtypeStruct((8, 128), F32)],
        input_output_aliases={a: 2 * n + a for a in range(n)},
        compiler_params=pltpu.CompilerParams(has_side_effects=EFFECT),
    )(*[_hbm(s) for s in shards])
    return list(res[:n]), list(res[n:2 * n]), list(res[2 * n:3 * n]), res[3 * n]


def _share_wait(name, send, recv, shards, after):
    n = len(shards)

    def body(*refs):
        buf = refs[:n]
        send_ref, recv_ref = refs[n:2 * n], refs[2 * n:3 * n]
        x, y, c, _ = _place()
        for a in range(n):
            half = buf[a].shape[0] // 2
            cp = _remote(buf[a].at[pl.ds(c * half, half)], buf[a].at[pl.ds((1 - c) * half, half)], send_ref[a], recv_ref[a], (x, y, c))
            cp.wait_send()
            cp.wait_recv()

    res = pl.pallas_call(
        body, name=name, in_specs=[HBM] * n + [SEM] * (2 * n) + [pl.BlockSpec(memory_space=pl.ANY)], out_specs=[HBM] * n,
        out_shape=[pltpu.HBM(s.shape, s.dtype) for s in shards], input_output_aliases={a: a for a in range(n)},
        compiler_params=pltpu.CompilerParams(has_side_effects=EFFECT),
    )(*shards, *send, *recv, after)
    return list(res)


def _small_start(buf):
    R, W = buf.shape
    ns = N_DEV - 1

    def body(*refs):
        send, recv = refs[2:2 + ns], refs[2 + ns:2 + 2 * ns]
        src, land, token = refs[2 + 2 * ns], refs[3 + 2 * ns], refs[4 + 2 * ns]
        x, y, c, _ = _place()
        me = 4 * x + 2 * y + c
        for k in range(1, N_DEV):
            peer = (x ^ (k >> 2), y ^ ((k >> 1) & 1), c ^ (k & 1))
            _remote(src, land.at[me], send[k - 1], recv[k - 1], peer).start()
        token[...] = jnp.zeros_like(token)

    res = pl.pallas_call(
        body, name="small_start", in_specs=[HBM, HBM],
        out_specs=[SEM] * (2 * ns) + [HBM, HBM, pl.BlockSpec(memory_space=pltpu.VMEM)],
        out_shape=[pltpu.SemaphoreType.DMA(())] * (2 * ns) + [pltpu.HBM((R, W), F32), pltpu.HBM((N_DEV, R, W), F32),
                                                                jax.ShapeDtypeStruct((8, 128), F32)],
        input_output_aliases={0: 2 * ns, 1: 2 * ns + 1},
        compiler_params=pltpu.CompilerParams(has_side_effects=EFFECT),
    )(_hbm(buf), _hbm(jnp.zeros((N_DEV, R, W), F32)))
    return list(res[:ns]), list(res[ns:2 * ns]), res[2 * ns], res[2 * ns + 1], res[2 * ns + 2]


def _small_wait(send, recv, buf, land, after):
    ns = N_DEV - 1

    def body(*refs):
        land_ref = refs[1]
        send_ref, recv_ref = refs[2:2 + ns], refs[2 + ns:2 + 2 * ns]
        x, y, c, _ = _place()
        me = 4 * x + 2 * y + c
        for k in range(1, N_DEV):
            landed = land_ref.at[me ^ k]
            cp = _remote(landed, landed, send_ref[k - 1], recv_ref[k - 1], (x, y, c))
            cp.wait_send()
            cp.wait_recv()

    return pl.pallas_call(
        body, name="small_wait", in_specs=[HBM, HBM] + [SEM] * (2 * ns) + [pl.BlockSpec(memory_space=pl.ANY)],
        out_specs=[HBM, HBM], out_shape=[pltpu.HBM(buf.shape, buf.dtype), pltpu.HBM(land.shape, land.dtype)],
        input_output_aliases={0: 0, 1: 1}, compiler_params=pltpu.CompilerParams(has_side_effects=EFFECT),
    )(buf, land, *send, *recv, after)


def _small_sum(buf, land):
    def body(buf_ref, land_ref, out_ref):
        x, y, c, _ = _place()
        me = 4 * x + 2 * y + c
        total = None
        for d in range(N_DEV):
            term = jnp.where(me == d, buf_ref[...], land_ref[d])
            total = term if total is None else total + term
        out_ref[...] = total

    return pl.pallas_call(body, name="small_sum", out_shape=jax.ShapeDtypeStruct(buf.shape, F32))(buf, land)


def _pair_sum_bf16(name, grad, theirs, ids):
    _, R2, C = theirs.shape
    tr = _tile(R2, 256, 16)
    nrb = R2 // tr

    def body(ids_ref, a_ref, b_ref, o_ref):
        o_ref[...] = (a_ref[...] + b_ref[...]).astype(BF16)

    return pl.pallas_call(
        body, name=name,
        grid_spec=pltpu.PrefetchScalarGridSpec(
            num_scalar_prefetch=1, grid=(3, nrb),
            in_specs=[pl.BlockSpec((None, tr, C), lambda j, i, ids: (ids[3 + j], ids[1] * nrb + i, 0)),
                      pl.BlockSpec((None, tr, C), lambda j, i, ids: (j + 1, i, 0))],
            out_specs=pl.BlockSpec((None, tr, C), lambda j, i, ids: (j, i, 0))),
        out_shape=jax.ShapeDtypeStruct((3, R2, C), BF16), compiler_params=_params(("parallel", "parallel")),
    )(ids, grad, theirs)


def _chip_sum(name, grad, theirs, arrived, ids):
    _, R2, C = theirs.shape
    tr = _tile(R2, 256, 16)
    nrb = R2 // tr

    def body(ids_ref, a_ref, b_ref, r_ref, o_ref):
        tot = a_ref[...] + b_ref[...]
        for j in range(3):
            tot = tot + r_ref[j].astype(F32)
        o_ref[...] = tot

    return pl.pallas_call(
        body, name=name,
        grid_spec=pltpu.PrefetchScalarGridSpec(
            num_scalar_prefetch=1, grid=(nrb,),
            in_specs=[pl.BlockSpec((None, tr, C), lambda i, ids: (ids[0], ids[1] * nrb + i, 0)),
                      pl.BlockSpec((None, tr, C), lambda i, ids: (0, i, 0)),
                      pl.BlockSpec((3, tr, C), lambda i, ids: (0, i, 0))],
            out_specs=pl.BlockSpec((tr, C), lambda i, ids: (ids[1] * nrb + i, 0))),
        out_shape=jax.ShapeDtypeStruct((2 * R2, C), F32), compiler_params=_params(("parallel",)),
    )(ids, grad, theirs, arrived)


def _adamw(name, w, g, m, v, emit_grad=False):
    R, C = w.shape
    tr = _tile(R, 128, 8)
    c1 = 1.0 / (1.0 - ADAM_B1 ** ADAM_STEP)
    c2 = 1.0 / (1.0 - ADAM_B2 ** ADAM_STEP)
    n_out = 4 if emit_grad else 3

    def body(w_ref, g_ref, m_ref, v_ref, d_ref, mo_ref, vo_ref, *rest):
        gv = g_ref[...]
        mn = ADAM_B1 * m_ref[...] + (1.0 - ADAM_B1) * gv
        vn = ADAM_B2 * v_ref[...] + (1.0 - ADAM_B2) * (gv * gv)
        d_ref[...] = -ADAM_LR * ((mn * c1) / (jnp.sqrt(vn * c2) + ADAM_EPS) + ADAM_WD * w_ref[...])
        mo_ref[...] = mn
        vo_ref[...] = vn
        if emit_grad:
            rest[0][...] = gv

    spec = pl.BlockSpec((tr, C), lambda i: (i, 0))
    sds = jax.ShapeDtypeStruct((R, C), F32)
    return pl.pallas_call(body, name=name, grid=(R // tr,), in_specs=[spec] * 4, out_specs=[spec] * n_out, out_shape=[sds] * n_out,
                          compiler_params=_params(("parallel",)))(w, g, m, v)


SMALL = ["ffn1_norm", "mix_norm", "mem_norm", "forget_bias", "fox_q_gain", "fox_k_gain", "swa_q_gain", "swa_k_gain", "swa_sinks",
         "mem_q_gain", "mem_k_gain", "ffn2_norm"]
LARGE = ["ffn1_gate", "ffn1_up", "ffn1_down", "w_in", "w_mem_k", "w_mem_v", "w_out", "ffn2_gate", "ffn2_up", "ffn2_down"]
GATHER_GROUPS = [["ffn1_gate", "ffn1_up"], ["ffn1_down"], ["w_in", "w_mem_k", "w_mem_v"], ["w_out", "ffn2_gate", "ffn2_up", "ffn2_down"]]
WEIGHTS = ["ffn1_norm", "ffn1_gate", "ffn1_up", "ffn1_down", "mix_norm", "mem_norm", "w_in", "forget_bias", "w_mem_k", "w_mem_v",
           "fox_q_gain", "fox_k_gain", "swa_q_gain", "swa_k_gain", "swa_sinks", "mem_q_gain", "mem_k_gain", "w_out", "ffn2_norm",
           "ffn2_gate", "ffn2_up", "ffn2_down"]


def _pad_proj_cols(w):
    out = jnp.zeros((w.shape[0], PROJ_W), w.dtype)
    for start, width, pstart in REF_GROUPS:
        out = lax.dynamic_update_slice(out, w[:, start:start + width], (0, pstart))
    return out


def _unpad_proj_cols(w):
    return jnp.concatenate([w[:, pstart:pstart + width] for _, width, pstart in REF_GROUPS], axis=1)


def _pack_small(vals):
    flat = jnp.concatenate([vals[k].reshape(-1).astype(F32) for k in SMALL + ["loss"]])
    n = flat.shape[0]
    total = -(-n // 1024) * 1024
    return jnp.pad(flat, (0, total - n)).reshape(total // 128, 128)


def _unpack_small(buf, shapes):
    flat = buf.reshape(-1)
    out, off = {}, 0
    for k in SMALL + ["loss"]:
        size = int(np.prod(shapes[k]))
        out[k] = flat[off:off + size].reshape(shapes[k])
        off += size
    return out


def kernel(x, mem, ffn1_norm, ffn1_gate, ffn1_up, ffn1_down, mix_norm, mem_norm, w_in, forget_bias, w_mem_k, w_mem_v, fox_q_gain, fox_k_gain, swa_q_gain, swa_k_gain, swa_sinks, mem_q_gain, mem_k_gain, w_out, ffn2_norm, ffn2_gate, ffn2_up, ffn2_down, loss_target, m_ffn1_norm, m_ffn1_gate, m_ffn1_up, m_ffn1_down, m_mix_norm, m_mem_norm, m_w_in, m_forget_bias, m_w_mem_k, m_w_mem_v, m_fox_q_gain, m_fox_k_gain, m_swa_q_gain, m_swa_k_gain, m_swa_sinks, m_mem_q_gain, m_mem_k_gain, m_w_out, m_ffn2_norm, m_ffn2_gate, m_ffn2_up, m_ffn2_down, v_ffn1_norm, v_ffn1_gate, v_ffn1_up, v_ffn1_down, v_mix_norm, v_mem_norm, v_w_in, v_forget_bias, v_w_mem_k, v_w_mem_v, v_fox_q_gain, v_fox_k_gain, v_swa_q_gain, v_swa_k_gain, v_swa_sinks, v_mem_q_gain, v_mem_k_gain, v_w_out, v_ffn2_norm, v_ffn2_gate, v_ffn2_up, v_ffn2_down):
    given = dict(locals())
    T, D = x.shape[1], x.shape[2]
    ML = mem.shape[1]
    xin = x.reshape(T, D)
    target = loss_target.reshape(T, D)
    memin = mem.reshape(ML, D)

    ids = _place_ids()
    shard = {k: given[k][0] for k in LARGE}
    started, after = [], ids
    for gi, group in enumerate(GATHER_GROUPS):
        also = {}
        if "w_in" in group:
            tied = lax.optimization_barrier((given["w_in"], given["m_w_in"], given["v_w_in"], after))
            w_in_rows = tuple(t[0] for t in tied[:3])
            shard["w_in"] = _pad_proj_cols(w_in_rows[0])
            also["w_in"] = w_in_rows[1:]
        if gi == 0:
            placed, own = zip(*[_cast_place("cast_" + k, shard[k], ids, after, keep_own=True) for k in group])
        else:
            placed = [_cast_place("cast_" + k, shard[k], ids, after, *also.get(k, ())) for k in group]
        send, recv, bufs, after = _gather_start("gather_start_%d" % gi, list(placed), after)
        started.append((send, recv, bufs))

    def arrive(gi, done):
        send, recv, bufs = started[gi]
        bufs = _gather_wait("gather_wait_%d" % gi, send, recv, bufs, done)
        return dict(zip(GATHER_GROUPS[gi], _gather_forward("gather_forward_%d" % gi, bufs)))

    gains = jnp.concatenate([fox_q_gain, fox_k_gain, swa_q_gain, swa_k_gain, mem_q_gain,
                             jnp.pad(forget_bias, ((0, 0), (0, HEAD_DIM - FOX_HEADS))), jnp.zeros((2, HEAD_DIM), F32)], axis=0)
    slopes_np = 2.0 ** (-8.0 * np.arange(1, SWA_HEADS + 1) / SWA_HEADS)
    slopes = jnp.asarray(np.repeat(slopes_np, WINDOW).reshape(SWA_KV_HEADS, GR, 1), F32)
    sinks = jnp.repeat(swa_sinks.reshape(SWA_HEADS), WINDOW).reshape(SWA_KV_HEADS, GR, 1)

    h1 = _rms_fwd("ffn1_norm_fwd", xin, ffn1_norm + after[0, 0])
    own_block = jnp.stack([jnp.zeros((), jnp.int32), ids[0]]).reshape(2, 1)
    part = _ffn_gu_blocks("ffn1_gate_up_own", h1, own[0][None], own[1][None], own_block, N_CHIPS)
    full = arrive(0, part[2])
    wg1, wu1 = full["ffn1_gate"], full["ffn1_up"]
    fg1, fu1, a1 = _ffn_gu_blocks("ffn1_gate_up", h1, wg1, wu1, jnp.stack([ids[3:], ids[3:]]), N_CHIPS, prev=part)
    wd1 = arrive(1, a1)["ffn1_down"].reshape(-1, D)
    x1, h2 = _residual_norm("ffn1_down", a1, wd1, xin, 0.5, mix_norm, 256)
    full = arrive(2, h2)
    win = full["w_in"].reshape(D, PROJ_W)
    wmk = full["w_mem_k"].reshape(D, MEM_HEADS * HEAD_DIM)
    wmv = full["w_mem_v"].reshape(D, MEM_HEADS * HEAD_DIM)
    proj = _mm2d("proj_in", h2, win, "nn", F32, tn=1408, tk=2048, n_outer=True)
    qkv, logf, k_t, v_t = _prep_fwd(proj, gains)
    cum = _cumsum_rows("forget_cumsum", [logf], False)
    cum_h = cum[:, :FOX_HEADS].T
    cq_row = cum_h.reshape(FOX_HEADS, 1, T)
    ck_rep = jnp.broadcast_to(cum_h[:, :, None], (FOX_HEADS, T, HEAD_DIM))
    mn = _rms_fwd("mem_norm_fwd", memin, mem_norm)
    mk_raw = _mm2d("mem_k_proj", mn, wmk, "nn", F32)
    mv = _mm2d("mem_v_proj", mn, wmv, "nn", BF16)
    mk = _head_norm_rows(mk_raw, mem_k_gain)
    out_a, out_a_f32, lse_a = _fox_fwd(qkv, v_t, cq_row, ck_rep)
    send, recv, bufs = started[3]
    bufs = _gather_wait("gather_wait_3", send, recv, bufs, out_a)
    send, recv, bufs, token = _gather_start("gather_pass_start_3", bufs, out_a, to_sibling=True)
    out_b, lse_b = _swa_fwd(qkv, slopes + token[0, 0], sinks)
    out_c, lse_c = _mem_fwd(qkv, mk, mv)
    mixed = jnp.concatenate([out_a, out_b, out_c], axis=1)
    full = dict(zip(GATHER_GROUPS[3], _gather_wait("gather_pass_wait_3", send, recv, bufs, mixed, to_sibling=True)))
    wo = full["w_out"].reshape(-1, D)
    wg2, wu2, wd2 = full["ffn2_gate"], full["ffn2_up"], full["ffn2_down"].reshape(-1, D)
    x2, h3 = _residual_norm("mix_out", mixed, wo, x1, 1.0, ffn2_norm, 512)
    fg2, fu2, a2 = _ffn_gu("ffn2_gate_up", h3, wg2, wu2)
    dx3, dyb3, loss_blocks = _ffn_down_loss("ffn2_down", a2, wd2, x2, target)

    grads, small, res = {}, {"loss": jnp.sum(loss_blocks[::8, 0])}, {}

    def pair_off(tag, group):
        send, recv, own, lands, token = _pair_start("grad_pair_start_" + tag, [grads[k] for k in group])
        return (group, send, recv, own, lands), token

    def chip_off(tag, started, done):
        group, send, recv, own, lands = started
        own, theirs = _pair_wait("grad_pair_wait_" + tag, send, recv, own, lands, done)
        grads.update(zip(group, own))
        to_chips = [_pair_sum_bf16("pair_sum_" + k, grads[k], b, ids) for k, b in zip(group, theirs)]
        send, recv, parts, lands, token = _chip_start("grad_chip_start_" + tag, to_chips)
        return (group, theirs, send, recv, parts, lands), token

    def reduce_half(tag, state, done):
        group, theirs, send, recv, parts, lands = state
        arrived = _chip_wait("grad_chip_wait_" + tag, send, recv, parts, lands, done)
        halves = [_chip_sum("chip_sum_" + k, grads[k], b, r, ids) for k, b, r in zip(group, theirs, arrived)]
        send, recv, shards, token = _share_start("grad_share_start_" + tag, halves)
        return (group, send, recv, shards), token

    def update(tag, shared, done):
        group, send, recv, shards = shared
        reduced = dict(zip(group, _share_wait("grad_share_wait_" + tag, send, recv, shards, done)))
        last = None
        for k in group:
            if k == "w_in":
                gk = _unpad_proj_cols(reduced[k])
                d, mo, vo = _adamw("adamw_" + k, w_in_rows[0], gk, w_in_rows[1], w_in_rows[2])
            else:
                d, mo, vo, gk = _adamw("adamw_" + k, given[k][0], reduced[k], given["m_" + k][0], given["v_" + k][0], emit_grad=True)
            res[k] = tuple(t[None] for t in (gk, d, mo, vo))
            last = vo
        return last

    dg2, du2 = _ffn_bwd_act("ffn2", dyb3, wd2, fg2, fu2, N_CHIPS)
    grads["ffn2_down"] = _ffn_bwd_down("ffn2", a2, dyb3, N_CHIPS).reshape(N_CHIPS, -1, D)
    grads["ffn2_gate"], grads["ffn2_up"] = _ffn_bwd_gate_up("ffn2", h3, dg2, du2, N_CHIPS)
    started, token = pair_off("a", ["ffn2_gate", "ffn2_up", "ffn2_down"])
    dh3 = _ffn_bwd_x("ffn2", dg2, du2, wg2, wu2, token)
    state_a, token = chip_off("a", started, dh3)
    dx2, dx2b, small["ffn2_norm"] = _rms_bwd("ffn2_norm_bwd", dh3, x2, ffn2_norm + token[0, 0], dx3, 1.0)
    dmix = _mm2d("mix_out_dx", dx2b, wo, "nt", BF16, tk=2048, n_outer=True)
    grads["w_out"] = _mm2d("mix_out_dw", mixed, dx2b, "tn", F32, tk=T, n_outer=True, resident=True).reshape(N_CHIPS, -1, D)
    delta_row = _fox_delta(dmix, out_a_f32)[:, :FOX_HEADS].T.reshape(FOX_HEADS, 1, T)
    dfq, dfk, dfv, dck, dcq = _fox_bwd(qkv, k_t, cq_row, ck_rep, delta_row, lse_a, dmix)
    dsq, dsk, dsv, dsink = _swa_bwd(qkv, slopes, sinks, out_b, lse_b, dmix)
    dmq, dmk, dmv = _mem_bwd(qkv, mk, mv, out_c, lse_c, dmix)
    small["swa_sinks"] = dsink[:, :SWA_GROUP, 0].reshape(1, SWA_HEADS)
    dcum = jnp.pad(dcq.reshape(FOX_HEADS, T).T, ((0, 0), (0, HEAD_DIM - FOX_HEADS)))
    dlogf = _cumsum_rows("forget_cumsum_bwd", [dcum], True, columns=dck)
    dproj, dgains = _prep_bwd(proj, gains, dfq, dfk, dfv, dsq, dsk, dsv, dmq, dlogf)
    for row, k in enumerate(["fox_q_gain", "fox_k_gain", "swa_q_gain", "swa_k_gain", "mem_q_gain"]):
        small[k] = dgains[row:row + 1, :]
    small["forget_bias"] = dgains[5:6, :FOX_HEADS]
    grads["w_in"] = _mm2d("proj_in_dw", h2, dproj, "tn", F32, tn=1408, tk=T, n_outer=True, resident=True).reshape(N_CHIPS, -1, PROJ_W)
    dmk_raw, small["mem_k_gain"] = _head_norm_rows_bwd(mk_raw, mem_k_gain, dmk)
    dmvb = dmv.astype(BF16)
    grads["w_mem_k"] = _mm2d("mem_k_dw", mn, dmk_raw, "tn", F32).reshape(N_CHIPS, -1, MEM_HEADS * HEAD_DIM)
    grads["w_mem_v"] = _mm2d("mem_v_dw", mn, dmvb, "tn", F32).reshape(N_CHIPS, -1, MEM_HEADS * HEAD_DIM)
    dmn = _mm2d("mem_k_dx", dmk_raw, wmk, "nt", F32)
    dmn = _mm2d("mem_v_dx", dmvb, wmv, "nt", F32, extras=[dmn], epilogue=lambda accs, ex: [ex[0] + accs[0]])
    _, _, small["mem_norm"] = _rms_bwd("mem_norm_bwd", dmn, memin, mem_norm, jnp.zeros_like(memin), 1.0)
    started, token = pair_off("b", ["w_out", "w_in", "w_mem_k", "w_mem_v"])
    dh2 = _mm2d("proj_in_dx", dproj, win, "nt", F32, tm=512, tn=1024, tk=PROJ_W, n_outer=True, resident=True, after=token)
    state_b, token = chip_off("b", started, dh2)
    dx1, dyb1, small["mix_norm"] = _rms_bwd("mix_norm_bwd", dh2, x1, mix_norm + token[0, 0], dx2, 0.5)
    grads["ffn1_down"] = _ffn_bwd_down("ffn1", a1, dyb1, N_CHIPS).reshape(N_CHIPS, -1, D)
    started, token = pair_off("c", ["ffn1_down"])
    dg1, du1 = _ffn_bwd_act("ffn1", dyb1, wd1, fg1, fu1, N_CHIPS, after=token)
    state_c, token = chip_off("c", started, dg1)
    grads["ffn1_gate"], grads["ffn1_up"] = _ffn_bwd_gate_up("ffn1", h1, dg1, du1, N_CHIPS, after=token)
    started, token = pair_off("d", ["ffn1_gate", "ffn1_up"])
    dh1 = _ffn_bwd_x("ffn1", dg1, du1, wg1, wu1, token)
    state_d, token = chip_off("d", started, dh1)
    grad_x, _, small["ffn1_norm"] = _rms_bwd("ffn1_norm_bwd", dh1, xin, ffn1_norm + token[0, 0], dx1, 1.0)

    s_send, s_recv, s_buf, s_land, token = _small_start(_pack_small(small))

    shared_a, token = reduce_half("a", state_a, token)
    shared_b, token = reduce_half("b", state_b, token)
    done = update("a", shared_a, token)
    shared_c, token = reduce_half("c", state_c, done)
    done = update("b", shared_b, token)
    shared_d, token = reduce_half("d", state_d, done)
    done = update("c", shared_c, token)
    done = update("d", shared_d, done)

    shapes = {k: given[k].shape for k in SMALL}
    shapes["loss"] = ()
    s_buf, s_land = _small_wait(s_send, s_recv, s_buf, s_land, done)
    red_small = _unpack_small(_small_sum(s_buf, s_land), shapes)
    loss = red_small["loss"]
    zero = {"loss": jnp.zeros((), F32)}
    packed = [_pack_small({**zero, **{k: src[k] for k in SMALL}}) for src in (
        {k: given[k] for k in SMALL}, red_small, {k: given["m_" + k] for k in SMALL}, {k: given["v_" + k] for k in SMALL})]
    d_s, m_s, v_s = (_unpack_small(t, shapes) for t in _adamw("adamw_small", *packed))
    for k in SMALL:
        res[k] = (red_small[k], d_s[k], m_s[k], v_s[k])

    outs = [loss, grad_x.reshape(1, T, D)]
    for part in range(4):
        outs += [res[k][part] for k in WEIGHTS]
    return tuple(outs)
```

```python
import functools

import numpy as np
import jax
import jax.numpy as jnp
from jax import lax
from jax.experimental import pallas as pl
from jax.experimental.pallas import tpu as pltpu

F32 = jnp.float32
BF16 = jnp.bfloat16
MESH = pl.DeviceIdType.MESH

HEAD_DIM = 128
FOX_HEADS = 6
SWA_HEADS = 6
SWA_KV_HEADS = 2
SWA_GROUP = SWA_HEADS // SWA_KV_HEADS
MEM_HEADS = 4
WINDOW = 128
EPS = 1e-6
NEG_INF = -1e30
SCALE = HEAD_DIM ** -0.5

C_FQ = 0
C_FK = C_FQ + FOX_HEADS * HEAD_DIM
C_FV = C_FK + FOX_HEADS * HEAD_DIM
C_SQ = C_FV + FOX_HEADS * HEAD_DIM
C_SK = C_SQ + SWA_HEADS * HEAD_DIM
C_SV = C_SK + SWA_KV_HEADS * HEAD_DIM
C_MQ = C_SV + SWA_KV_HEADS * HEAD_DIM
C_FL = C_MQ + MEM_HEADS * HEAD_DIM
PROJ_W = C_FL + HEAD_DIM
FOX_W = FOX_HEADS * HEAD_DIM
REF_GROUPS = [
    (0, FOX_W, C_FQ), (FOX_W, FOX_W, C_FK), (2 * FOX_W, FOX_W, C_FV), (3 * FOX_W, FOX_HEADS, C_FL),
    (3 * FOX_W + FOX_HEADS, SWA_HEADS * HEAD_DIM, C_SQ),
    (3 * FOX_W + FOX_HEADS + SWA_HEADS * HEAD_DIM, SWA_KV_HEADS * HEAD_DIM, C_SK),
    (3 * FOX_W + FOX_HEADS + (SWA_HEADS + SWA_KV_HEADS) * HEAD_DIM, SWA_KV_HEADS * HEAD_DIM, C_SV),
    (3 * FOX_W + FOX_HEADS + (SWA_HEADS + 2 * SWA_KV_HEADS) * HEAD_DIM, MEM_HEADS * HEAD_DIM, C_MQ),
]

ADAM_LR = 0.001
ADAM_B1 = 0.9
ADAM_B2 = 0.999
ADAM_EPS = 1e-08
ADAM_WD = 0.01
ADAM_STEP = 10

V7X_VMEM_LIMIT = 56 * 1024 * 1024
N_CHIPS = 4
N_DEV = 8


def _tile(n, pref, mult=128):
    t = (min(pref, n) // mult) * mult
    while t >= mult:
        if n % t == 0:
            return t
        t -= mult
    return n


def _params(sem):
    return pltpu.CompilerParams(dimension_semantics=sem, vmem_limit_bytes=V7X_VMEM_LIMIT)


_DIMS = {"nn": (((1,), (0,)), ((), ())), "nt": (((1,), (1,)), ((), ())), "tn": (((0,), (0,)), ((), ()))}


def _dot(a, b, mode):
    return lax.dot_general(a, b, _DIMS[mode], preferred_element_type=F32)


def _mm(name, grid, pairs, acc_of, acc_shapes, extras, outs, epilogue, after=None):
    n_p, n_e, n_o, n_a = len(pairs), len(extras), len(outs), len(acc_shapes)
    n_w = 0 if after is None else 1
    nk = grid[2]
    n_in = sum(1 if a is None else 2 for a, *_ in pairs)

    def body(*refs):
        ex = refs[n_in:n_in + n_e]
        out = refs[n_in + n_e + n_w:n_in + n_e + n_w + n_o]
        accs = refs[n_in + n_e + n_w + n_o:]
        parts = [None] * n_a
        at = 0
        for p in range(n_p):
            if pairs[p][0] is None:
                a_ref, b_ref = refs[0], refs[at]
                at += 1
            else:
                a_ref, b_ref = refs[at], refs[at + 1]
                at += 2
            d = _dot(a_ref[...], b_ref[...], pairs[p][4])
            parts[acc_of[p]] = d if parts[acc_of[p]] is None else parts[acc_of[p]] + d

        def finish(vals):
            for o, r in zip(out, epilogue(vals, [e[...] for e in ex])):
                o[...] = r.astype(o.dtype)

        if nk == 1:
            finish(parts)
            return
        k = pl.program_id(2)

        @pl.when(k == 0)
        def _():
            for a, d in zip(accs, parts):
                a[...] = d

        @pl.when((k > 0) & (k < nk - 1))
        def _():
            for a, d in zip(accs, parts):
                a[...] += d

        @pl.when(k == nk - 1)
        def _():
            finish([a[...] + d for a, d in zip(accs, parts)])

    in_specs, args = [], []
    for a, a_spec, b, b_spec, _ in pairs:
        if a is not None:
            in_specs.append(a_spec)
            args.append(a)
        in_specs.append(b_spec)
        args.append(b)
    for e, e_spec in extras:
        in_specs.append(e_spec)
        args.append(e)
    if after is not None:
        in_specs.append(pl.BlockSpec(memory_space=pl.ANY))
        args.append(after)
    res = pl.pallas_call(
        body, name=name, grid=grid, in_specs=in_specs,
        out_specs=[s for _, s in outs], out_shape=[o for o, _ in outs],
        scratch_shapes=[pltpu.VMEM(s, F32) for s in acc_shapes] if nk > 1 else [],
        compiler_params=_params(("parallel", "parallel", "arbitrary")),
    )(*args)
    return res


def _mm2d(name, a, b, mode, out_dtype, tm=512, tn=1024, tk=1024, extras=(), epilogue=None, n_out=1, after=None, n_outer=False,
          resident=False, rows=(), tile_stat=False):
    if mode == "nn":
        (M, K), N = a.shape, b.shape[1]
    elif mode == "nt":
        (M, K), N = a.shape, b.shape[0]
    else:
        (K, M), N = a.shape, b.shape[1]
    tm, tn, tk = _tile(M, tm), _tile(N, tn), _tile(K, tk)
    assert not resident or tk == K

    def spec(shape, index, single=False):
        mode_kw = {"pipeline_mode": pl.Buffered(1)} if single else {}
        if n_outer:
            return pl.BlockSpec(shape, lambda j, i, k: index(i, j, k), **mode_kw)
        return pl.BlockSpec(shape, index, **mode_kw)

    single_a, single_b = resident and not n_outer, resident and n_outer
    a_spec = spec((tk, tm), lambda i, j, k: (k, i), single_a) if mode == "tn" else spec((tm, tk), lambda i, j, k: (i, k), single_a)
    b_spec = spec((tn, tk), lambda i, j, k: (j, k), single_b) if mode == "nt" else spec((tk, tn), lambda i, j, k: (k, j), single_b)
    mn = spec((tm, tn), lambda i, j, k: (i, j))
    if epilogue is None:
        epilogue = lambda accs, ex: [accs[0]]
    if not isinstance(out_dtype, (list, tuple)):
        out_dtype = [out_dtype] * n_out
    grid = (N // tn, M // tm, K // tk) if n_outer else (M // tm, N // tn, K // tk)
    outs = [(jax.ShapeDtypeStruct((M, N), d), mn) for d in out_dtype]
    if tile_stat:
        assert tn == N
        outs.append((jax.ShapeDtypeStruct((8 * (M // tm), 128), F32), spec((8, 128), lambda i, j, k: (i, 0))))
    res = _mm(name, grid, [(a, a_spec, b, b_spec, mode)], [0], [(tm, tn)],
              [(e, mn) for e in extras] + [(r, spec((1, tn), lambda i, j, k: (0, j))) for r in rows], outs, epilogue, after=after)
    return res[0] if len(res) == 1 else res


def _sigmoid(x):
    return 1.0 / (1.0 + jnp.exp(-x))


def _sigmoid_fast(x):
    return pl.reciprocal(1.0 + jnp.exp(-x), approx=True)


def _ffn_gu(name, h, wg, wu):
    T, D = h.shape
    nf, _, F4 = wg.shape
    tm, tk = _tile(T, 512), _tile(D, 2048)
    a_spec = pl.BlockSpec((tm, tk), lambda j, i, k: (i, k))
    b_spec = pl.BlockSpec((None, tk, F4), lambda j, i, k: (j, k, 0))
    o_spec = pl.BlockSpec((tm, F4), lambda j, i, k: (i, j))

    def epilogue(accs, ex):
        g, u = accs
        s = _sigmoid_fast(g)
        gs = g * s
        return [(s + s * (g - gs)) * u, gs, gs * u]

    sds = jax.ShapeDtypeStruct((T, nf * F4), BF16)
    return _mm(name, (nf, T // tm, D // tk), [(h, a_spec, wg, b_spec, "nn"), (None, None, wu, b_spec, "nn")], [0, 1],
               [(tm, F4), (tm, F4)], [], [(sds, o_spec)] * 3, epilogue)


def _ffn_gu_blocks(name, h, wg, wu, blocks, n_total, prev=None):
    T, D = h.shape
    F4 = wg.shape[2]
    tm = _tile(T, 512)

    def body(tab, h_ref, wg_ref, wu_ref, *rest):
        out = rest[-3:]
        hv = h_ref[...]
        g, u = _dot(hv, wg_ref[...], "nn"), _dot(hv, wu_ref[...], "nn")
        s = _sigmoid_fast(g)
        gs = g * s
        out[0][...] = ((s + s * (g - gs)) * u).astype(BF16)
        out[1][...] = gs.astype(BF16)
        out[2][...] = (gs * u).astype(BF16)

    w_spec = pl.BlockSpec((None, D, F4), lambda j, i, tab: (tab[0, j], 0, 0))
    o_spec = pl.BlockSpec((tm, F4), lambda j, i, tab: (i, tab[1, j]))
    filled = [] if prev is None else list(prev)
    sds = jax.ShapeDtypeStruct((T, n_total * F4), BF16)
    return pl.pallas_call(
        body, name=name,
        grid_spec=pltpu.PrefetchScalarGridSpec(
            num_scalar_prefetch=1, grid=(blocks.shape[1], T // tm),
            in_specs=[pl.BlockSpec((tm, D), lambda j, i, tab: (i, 0)), w_spec, w_spec] + [pl.BlockSpec(memory_space=pl.ANY)] * len(filled),
            out_specs=[o_spec] * 3),
        out_shape=[sds] * 3, input_output_aliases={4 + t: t for t in range(len(filled))},
        compiler_params=_params(("parallel", "parallel")),
    )(blocks, h, wg, wu, *filled)


def _rms_rows(x, gain):
    return x * lax.rsqrt(jnp.mean(x * x, axis=-1, keepdims=True) + EPS) * gain


def _residual_norm(name, a, w, xres, scale, gain, tm):
    def epilogue(accs, ex):
        y = ex[0] + scale * accs[0]
        return [y, _rms_rows(y, ex[1])]

    return _mm2d(name, a, w, "nn", [F32, BF16], tm=tm, tn=w.shape[1], tk=w.shape[0], n_outer=True, resident=True, extras=[xres],
                 rows=[gain], epilogue=epilogue)


def _ffn_down_loss(name, a, wd, xres, target):
    D = wd.shape[1]

    def epilogue(accs, ex):
        e = ex[0] + 0.5 * accs[0] - ex[1]
        d = e * (1.0 / D)
        return [d, 0.5 * d, jnp.zeros((8, 128), F32) + (0.5 / D) * jnp.sum(e * e)]

    return _mm2d(name, a, wd, "nn", [F32, BF16], tm=256, tn=D, tk=wd.shape[0], n_outer=True, resident=True, extras=[xres, target],
                 tile_stat=True, epilogue=epilogue)


def _ffn_bwd_down(tag, a, dyb, nf, after=None):
    return _mm2d(tag + "_dwd", a, dyb, "tn", F32, tm=a.shape[1] // nf, tn=1024, tk=a.shape[0], resident=True, after=after)


def _ffn_bwd_act(tag, dyb, wd, da_dg, da_du, nf, after=None):
    def act_bwd(accs, ex):
        return [accs[0] * ex[0].astype(F32), accs[0] * ex[1].astype(F32)]

    return _mm2d(tag + "_da", dyb, wd, "nt", BF16, tn=wd.shape[0] // nf, tk=2048, extras=[da_dg, da_du], epilogue=act_bwd, n_out=2,
                 n_outer=True, after=after)


def _ffn_bwd_gate_up(tag, h, dg, du, nf, after=None):
    T, D = h.shape
    F4 = dg.shape[1] // nf
    tm = _tile(D, 512)
    h_spec = pl.BlockSpec((T, tm), lambda j, i, k: (0, i))
    d_spec = pl.BlockSpec((T, F4), lambda j, i, k: (0, j), pipeline_mode=pl.Buffered(1))
    w_spec = pl.BlockSpec((None, tm, F4), lambda j, i, k: (j, i, 0))
    sds = jax.ShapeDtypeStruct((nf, D, F4), F32)
    return _mm(tag + "_dwgu", (nf, D // tm, 1), [(h, h_spec, dg, d_spec, "tn"), (None, None, du, d_spec, "tn")],
               [0, 1], [(tm, F4), (tm, F4)], [], [(sds, w_spec)] * 2, lambda accs, ex: accs, after=after)


def _ffn_bwd_x(tag, dg, du, wg, wu, after):
    T = dg.shape[0]
    nf, D, F4 = wg.shape
    tm, tn = _tile(T, 256), _tile(D, 1024)

    def body(dg_ref, du_ref, wg_ref, wu_ref, after_ref, o_ref):
        acc = None
        for j in range(nf):
            cols = slice(j * F4, (j + 1) * F4)
            part = _dot(dg_ref[:, cols], wg_ref[j], "nt") + _dot(du_ref[:, cols], wu_ref[j], "nt")
            acc = part if acc is None else acc + part
        o_ref[...] = acc

    a_spec = pl.BlockSpec((tm, nf * F4), lambda n, i: (i, 0))
    b_spec = pl.BlockSpec((nf, tn, F4), lambda n, i: (0, n, 0), pipeline_mode=pl.Buffered(1))
    return pl.pallas_call(
        body, name=tag + "_dh", grid=(D // tn, T // tm),
        in_specs=[a_spec, a_spec, b_spec, b_spec, pl.BlockSpec(memory_space=pl.ANY)],
        out_specs=pl.BlockSpec((tm, tn), lambda n, i: (i, n)), out_shape=jax.ShapeDtypeStruct((T, D), F32),
        compiler_params=_params(("parallel", "parallel")),
    )(dg, du, wg, wu, after)


def _rms_fwd(name, x, gain):
    R, D = x.shape
    tr = _tile(R, 256, 8)

    def body(x_ref, g_ref, o_ref):
        xv = x_ref[...]
        r = lax.rsqrt(jnp.mean(xv * xv, axis=-1, keepdims=True) + EPS)
        o_ref[...] = (xv * r * g_ref[...]).astype(BF16)

    return pl.pallas_call(
        body, name=name, grid=(R // tr,),
        in_specs=[pl.BlockSpec((tr, D), lambda i: (i, 0)), pl.BlockSpec((1, D), lambda i: (0, 0))],
        out_specs=pl.BlockSpec((tr, D), lambda i: (i, 0)), out_shape=jax.ShapeDtypeStruct((R, D), BF16),
        compiler_params=_params(("parallel",)),
    )(x, gain)


def _rms_bwd(name, dh, x, gain, dres, bscale):
    R, D = x.shape
    tr = _tile(R, 256, 8)

    def body(dh_ref, x_ref, g_ref, dres_ref, dx_ref, dxb_ref, dg_ref):
        xv, dy = x_ref[...], dh_ref[...]
        r = lax.rsqrt(jnp.mean(xv * xv, axis=-1, keepdims=True) + EPS)
        xn = xv * r
        uu = dy * g_ref[...]
        dx = dres_ref[...] + r * (uu - xn * jnp.mean(xn * uu, axis=-1, keepdims=True))
        dx_ref[...] = dx
        dxb_ref[...] = (bscale * dx).astype(BF16)
        part = jnp.sum(dy * xn, axis=0, keepdims=True)

        @pl.when(pl.program_id(0) == 0)
        def _():
            dg_ref[...] = part

        @pl.when(pl.program_id(0) > 0)
        def _():
            dg_ref[...] += part

    row = pl.BlockSpec((tr, D), lambda i: (i, 0))
    vec = pl.BlockSpec((1, D), lambda i: (0, 0))
    return pl.pallas_call(
        body, name=name, grid=(R // tr,), in_specs=[row, row, vec, row], out_specs=[row, row, vec],
        out_shape=[jax.ShapeDtypeStruct((R, D), F32), jax.ShapeDtypeStruct((R, D), BF16), jax.ShapeDtypeStruct((1, D), F32)],
        compiler_params=_params(("arbitrary",)),
    )(dh, x, gain, dres)


def _head_norm(xs, g):
    r = lax.rsqrt(jnp.mean(xs * xs, axis=-1, keepdims=True) + EPS)
    return xs * r * g


def _head_norm_bwd(xs, g, dy):
    r = lax.rsqrt(jnp.mean(xs * xs, axis=-1, keepdims=True) + EPS)
    xn = xs * r
    uu = dy * g
    return r * (uu - xn * jnp.mean(xn * uu, axis=-1, keepdims=True)), jnp.sum(dy * xn, axis=0, keepdims=True)


NORMED = [(C_FQ, FOX_HEADS, 0), (C_FK, FOX_HEADS, 1), (C_SQ, SWA_HEADS, 2), (C_SK, SWA_KV_HEADS, 3), (C_MQ, MEM_HEADS, 4)]
PLAIN = [(C_FV, FOX_HEADS), (C_SV, SWA_KV_HEADS)]


def _prep_fwd(proj, gains):
    T = proj.shape[0]
    tr = _tile(T, 256, 128)

    def body(p_ref, g_ref, o_ref, lf_ref, kt_ref, vt_ref):
        for start, heads, row in NORMED:
            gn = g_ref[row:row + 1, :]
            for hh in range(heads):
                sl = slice(start + hh * HEAD_DIM, start + (hh + 1) * HEAD_DIM)
                y = _head_norm(p_ref[:, sl], gn)
                o_ref[:, sl] = y.astype(BF16)
                if start == C_FK:
                    kt_ref[hh * HEAD_DIM:(hh + 1) * HEAD_DIM, :] = y.T.astype(BF16)
        for start, heads in PLAIN:
            sl = slice(start, start + heads * HEAD_DIM)
            o_ref[:, sl] = p_ref[:, sl].astype(BF16)
        for hh in range(FOX_HEADS):
            sl = slice(C_FV + hh * HEAD_DIM, C_FV + (hh + 1) * HEAD_DIM)
            vt_ref[hh * HEAD_DIM:(hh + 1) * HEAD_DIM, :] = p_ref[:, sl].T.astype(BF16)
        zb = p_ref[:, C_FL:C_FL + HEAD_DIM] + g_ref[5:6, :]
        o_ref[:, C_FL:C_FL + HEAD_DIM] = jnp.zeros((tr, HEAD_DIM), BF16)
        lf_ref[...] = jnp.minimum(zb, 0.0) - jnp.log(1.0 + jnp.exp(-jnp.abs(zb)))

    return pl.pallas_call(
        body, name="prep_fwd", grid=(T // tr,),
        in_specs=[pl.BlockSpec((tr, PROJ_W), lambda i: (i, 0)), pl.BlockSpec((8, 128), lambda i: (0, 0))],
        out_specs=[pl.BlockSpec((tr, PROJ_W), lambda i: (i, 0)), pl.BlockSpec((tr, HEAD_DIM), lambda i: (i, 0)),
                   pl.BlockSpec((FOX_W, tr), lambda i: (0, i)), pl.BlockSpec((FOX_W, tr), lambda i: (0, i))],
        out_shape=[jax.ShapeDtypeStruct((T, PROJ_W), BF16), jax.ShapeDtypeStruct((T, HEAD_DIM), F32),
                   jax.ShapeDtypeStruct((FOX_W, T), BF16), jax.ShapeDtypeStruct((FOX_W, T), BF16)],
        compiler_params=_params(("parallel",)),
    )(proj, gains)


def _prep_bwd(proj, gains, dfq, dfk, dfv, dsq, dsk, dsv, dmq, dlogf):
    T = proj.shape[0]
    tr = _tile(T, 256, 8)
    d_normed = {C_FQ: 0, C_FK: 1, C_SQ: 3, C_SK: 4, C_MQ: 6}
    d_plain = {C_FV: 2, C_SV: 5}

    def body(p_ref, g_ref, *rest):
        d_refs, dlf_ref, o_ref, dg_ref = rest[:7], rest[7], rest[8], rest[9]
        rows = []
        for start, heads, row in NORMED:
            gn = g_ref[row:row + 1, :]
            d_ref = d_refs[d_normed[start]]
            tot = jnp.zeros((1, HEAD_DIM), F32)
            for hh in range(heads):
                sl = slice(start + hh * HEAD_DIM, start + (hh + 1) * HEAD_DIM)
                dx, dgn = _head_norm_bwd(p_ref[:, sl], gn, d_ref[:, hh * HEAD_DIM:(hh + 1) * HEAD_DIM])
                o_ref[:, sl] = dx.astype(BF16)
                tot = tot + dgn
            rows.append(tot)
        for start, heads in PLAIN:
            o_ref[:, start:start + heads * HEAD_DIM] = d_refs[d_plain[start]][...].astype(BF16)
        zb = p_ref[:, C_FL:C_FL + HEAD_DIM] + g_ref[5:6, :]
        lane = lax.broadcasted_iota(jnp.int32, (tr, HEAD_DIM), 1)
        dz = jnp.where(lane < FOX_HEADS, dlf_ref[...] * (1.0 - _sigmoid(zb)), 0.0)
        o_ref[:, C_FL:C_FL + HEAD_DIM] = dz.astype(BF16)
        rows.append(jnp.sum(dz, axis=0, keepdims=True))
        part = jnp.concatenate(rows + [jnp.zeros((2, HEAD_DIM), F32)], axis=0)

        @pl.when(pl.program_id(0) == 0)
        def _():
            dg_ref[...] = part

        @pl.when(pl.program_id(0) > 0)
        def _():
            dg_ref[...] += part

    def rows_of(w):
        return pl.BlockSpec((tr, w), lambda i: (i, 0))

    small = pl.BlockSpec((8, 128), lambda i: (0, 0))
    ds = [dfq, dfk, dfv, dsq, dsk, dsv, dmq]
    return pl.pallas_call(
        body, name="prep_bwd", grid=(T // tr,),
        in_specs=[rows_of(PROJ_W), small] + [rows_of(d.shape[1]) for d in ds] + [rows_of(HEAD_DIM)],
        out_specs=[rows_of(PROJ_W), small],
        out_shape=[jax.ShapeDtypeStruct((T, PROJ_W), BF16), jax.ShapeDtypeStruct((8, 128), F32)],
        compiler_params=_params(("arbitrary",)),
    )(proj, gains, *ds, dlogf)


def _head_norm_rows(x, gain):
    R, W = x.shape

    def body(x_ref, g_ref, o_ref):
        for hh in range(W // HEAD_DIM):
            sl = slice(hh * HEAD_DIM, (hh + 1) * HEAD_DIM)
            o_ref[:, sl] = _head_norm(x_ref[:, sl], g_ref[...]).astype(BF16)

    return pl.pallas_call(body, name="mem_k_norm", out_shape=jax.ShapeDtypeStruct((R, W), BF16))(x, gain)


def _head_norm_rows_bwd(x, gain, dy):
    R, W = x.shape

    def body(x_ref, g_ref, dy_ref, dx_ref, dg_ref):
        tot = jnp.zeros((1, HEAD_DIM), F32)
        for hh in range(W // HEAD_DIM):
            sl = slice(hh * HEAD_DIM, (hh + 1) * HEAD_DIM)
            dx, dgn = _head_norm_bwd(x_ref[:, sl], g_ref[...], dy_ref[:, sl])
            dx_ref[:, sl] = dx.astype(BF16)
            tot = tot + dgn
        dg_ref[...] = tot

    return pl.pallas_call(
        body, name="mem_k_norm_bwd",
        out_shape=[jax.ShapeDtypeStruct((R, W), BF16), jax.ShapeDtypeStruct((1, HEAD_DIM), F32)])(x, gain, dy)


def _cumsum_rows(name, xs, reverse, columns=None):
    T, W = xs[0].shape
    tb = _tile(T, 512, 8)
    nb = T // tb
    n_in = len(xs) + (0 if columns is None else 1)

    def body(*refs):
        o_ref, carry = refs[n_in], refs[n_in + 1]

        @pl.when(pl.program_id(0) == 0)
        def _():
            carry[...] = jnp.zeros_like(carry)

        xv = refs[0][...]
        for x_ref in refs[1:len(xs)]:
            xv = xv + x_ref[...]
        if columns is not None:
            lane = lax.broadcasted_iota(jnp.int32, (tb, W), 1)
            for hh in range(columns.shape[0]):
                xv = xv + jnp.where(lane == hh, refs[len(xs)][hh], 0.0)
        r = lax.broadcasted_iota(jnp.int32, (tb, tb), 0)
        cc = lax.broadcasted_iota(jnp.int32, (tb, tb), 1)
        tri = jnp.where((cc >= r) if reverse else (cc <= r), 1.0, 0.0).astype(F32)
        o_ref[...] = jnp.dot(tri, xv, precision=lax.Precision.HIGHEST, preferred_element_type=F32) + carry[...]
        carry[...] += jnp.sum(xv, axis=0, keepdims=True)

    idx = (lambda i: (nb - 1 - i, 0)) if reverse else (lambda i: (i, 0))
    in_specs = [pl.BlockSpec((tb, W), idx)] * len(xs)
    if columns is not None:
        in_specs.append(pl.BlockSpec((columns.shape[0], tb, 1), lambda i: (0, idx(i)[0], 0)))
    return pl.pallas_call(
        body, name=name, grid=(nb,), in_specs=in_specs, out_specs=pl.BlockSpec((tb, W), idx),
        out_shape=jax.ShapeDtypeStruct((T, W), F32), scratch_shapes=[pltpu.VMEM((1, W), F32)],
        compiler_params=_params(("arbitrary",)),
    )(*xs, *([] if columns is None else [columns]))


def _triangle(nq, by_column):
    if by_column:
        blocks = [(i, j) for j in range(nq) for i in range(j, nq)]
    else:
        blocks = [(i, j) for i in range(nq) for j in range(i + 1)]
    return jnp.asarray(np.array(blocks, np.int32).T)


def _fox_scores_t(k, q, cq_row, ck_rep, on_diagonal):
    n = q.shape[0]
    s = _dot(k, q, "nt") * SCALE + (cq_row - jnp.tile(ck_rep, (1, n // HEAD_DIM)))
    if on_diagonal:
        s = jnp.where(lax.broadcasted_iota(jnp.int32, (n, n), 0) <= lax.broadcasted_iota(jnp.int32, (n, n), 1), s, NEG_INF)
    return s


def _fox_fwd(qkv, v_t, cq_row, ck_rep):
    T = qkv.shape[0]
    tq = _tile(T, 1024)
    nq = T // tq
    steps = nq * (nq + 1) // 2
    HQ, HK = C_FQ // HEAD_DIM, C_FK // HEAD_DIM

    def body(tab, q_ref, k_ref, vt_ref, cq_ref, ck_ref, o_ref, of_ref, lse_ref, m_sc, l_sc, acc_sc):
        i, j = tab[0, pl.program_id(1)], tab[1, pl.program_id(1)]

        @pl.when(j == 0)
        def _():
            m_sc[...] = jnp.full_like(m_sc, NEG_INF)
            l_sc[...] = jnp.zeros_like(l_sc)
            acc_sc[...] = jnp.zeros_like(acc_sc)

        def step(on_diagonal):
            s = _fox_scores_t(k_ref[...], q_ref[...], cq_ref[...], ck_ref[...], on_diagonal)
            m_new = jnp.maximum(m_sc[...], jnp.max(s, axis=0, keepdims=True))
            alpha = jnp.exp(m_sc[...] - m_new)
            p = jnp.exp(s - m_new)
            l_sc[...] = alpha * l_sc[...] + jnp.sum(p, axis=0, keepdims=True)
            acc_sc[...] = alpha * acc_sc[...] + _dot(vt_ref[...], p.astype(BF16), "nn")
            m_sc[...] = m_new

        @pl.when(j < i)
        def _():
            step(False)

        @pl.when(j == i)
        def _():
            step(True)
            o = (acc_sc[...] / l_sc[...]).T
            o_ref[...] = o.astype(BF16)
            of_ref[...] = o
            lse_ref[...] = m_sc[...] + jnp.log(l_sc[...])

    qrow = pl.BlockSpec((None, 1, tq), lambda h, s, tab: (h, 0, tab[0, s]))
    return pl.pallas_call(
        body, name="fox_fwd",
        grid_spec=pltpu.PrefetchScalarGridSpec(
            num_scalar_prefetch=1, grid=(FOX_HEADS, steps),
            in_specs=[pl.BlockSpec((tq, HEAD_DIM), lambda h, s, tab: (tab[0, s], HQ + h)),
                      pl.BlockSpec((tq, HEAD_DIM), lambda h, s, tab: (tab[1, s], HK + h)),
                      pl.BlockSpec((HEAD_DIM, tq), lambda h, s, tab: (h, tab[1, s])), qrow,
                      pl.BlockSpec((None, tq, HEAD_DIM), lambda h, s, tab: (h, tab[1, s], 0))],
            out_specs=[pl.BlockSpec((tq, HEAD_DIM), lambda h, s, tab: (tab[0, s], h)),
                       pl.BlockSpec((tq, HEAD_DIM), lambda h, s, tab: (tab[0, s], h)), qrow],
            scratch_shapes=[pltpu.VMEM((1, tq), F32), pltpu.VMEM((1, tq), F32), pltpu.VMEM((HEAD_DIM, tq), F32)]),
        out_shape=[jax.ShapeDtypeStruct((T, FOX_W), BF16), jax.ShapeDtypeStruct((T, FOX_W), F32),
                   jax.ShapeDtypeStruct((FOX_HEADS, 1, T), F32)],
        compiler_params=_params(("parallel", "arbitrary")),
    )(_triangle(nq, False), qkv, qkv, v_t, cq_row, ck_rep)


def _fox_delta(dmix, out_f32):
    T = out_f32.shape[0]
    tr = _tile(T, 512, 8)

    def body(do_ref, o_ref, d_ref):
        lane = lax.broadcasted_iota(jnp.int32, (tr, HEAD_DIM), 1)
        acc = jnp.zeros((tr, HEAD_DIM), F32)
        for hh in range(FOX_HEADS):
            sl = slice(hh * HEAD_DIM, (hh + 1) * HEAD_DIM)
            d = jnp.sum(do_ref[:, sl].astype(F32) * o_ref[:, sl], axis=-1, keepdims=True)
            acc = jnp.where(lane == hh, d, acc)
        d_ref[...] = acc

    blk = pl.BlockSpec((tr, FOX_W), lambda i: (i, 0))
    return pl.pallas_call(
        body, name="fox_delta", grid=(T // tr,), in_specs=[blk, blk], out_specs=pl.BlockSpec((tr, HEAD_DIM), lambda i: (i, 0)),
        out_shape=jax.ShapeDtypeStruct((T, HEAD_DIM), F32), compiler_params=_params(("parallel",)),
    )(dmix, out_f32)


def _fox_bwd(qkv, k_t, cq_row, ck_rep, delta_row, lse, dmix):
    T = qkv.shape[0]
    tq = _tile(T, 1024)
    nq = T // tq
    steps = nq * (nq + 1) // 2
    HQ, HK, HV = C_FQ // HEAD_DIM, C_FK // HEAD_DIM, C_FV // HEAD_DIM

    def body(tab, q_ref, k_ref, kt_ref, v_ref, cq_ref, ck_ref, delta_ref, lse_ref, do_ref,
             dq_ref, dk_ref, dv_ref, dck_ref, dcq_ref, dk_sc, dv_sc, dc_sc, dqt_sc):
        qi, kj = tab[0, pl.program_id(1)], tab[1, pl.program_id(1)]

        @pl.when(qi == kj)
        def _():
            dk_sc[...] = jnp.zeros_like(dk_sc)
            dv_sc[...] = jnp.zeros_like(dv_sc)
            dc_sc[...] = jnp.zeros_like(dc_sc)

        def step(on_diagonal):
            q, k, v, do = q_ref[...], k_ref[...], v_ref[...], do_ref[...]
            p = jnp.exp(_fox_scores_t(k, q, cq_ref[...], ck_ref[...], on_diagonal) - lse_ref[...])
            dp = _dot(v, do, "nt")
            ds = p * (dp - delta_ref[...])
            dsb = ds.astype(BF16)
            dv_sc[...] += _dot(p.astype(BF16), do, "nn")
            dk_sc[...] += _dot(dsb, q, "nn")
            dc_sc[...] += jnp.sum(ds, axis=1, keepdims=True)
            dq_part = _dot(kt_ref[...], dsb, "nn") * SCALE
            dcq_part = jnp.sum(ds, axis=0, keepdims=True)

            @pl.when(kj == 0)
            def _():
                dqt_sc[qi] = dq_part
                dcq_ref[qi] = dcq_part

            @pl.when(kj > 0)
            def _():
                dqt_sc[qi] += dq_part
                dcq_ref[qi] += dcq_part

            if on_diagonal:
                dq_ref[...] = dqt_sc[qi].T

        @pl.when(qi > kj)
        def _():
            step(False)

        @pl.when(qi == kj)
        def _():
            step(True)

        @pl.when(qi == nq - 1)
        def _():
            dk_ref[...] = dk_sc[...] * SCALE
            dv_ref[...] = dv_sc[...]
            dck_ref[...] = -dc_sc[...]

    def rows(base):
        return pl.BlockSpec((tq, HEAD_DIM), lambda h, s, tab: (tab[0, s], base + h))

    def cols(base):
        return pl.BlockSpec((tq, HEAD_DIM), lambda h, s, tab: (tab[1, s], base + h))

    qrow = pl.BlockSpec((None, 1, tq), lambda h, s, tab: (h, 0, tab[0, s]))
    sds = jax.ShapeDtypeStruct((T, FOX_W), F32)
    return pl.pallas_call(
        body, name="fox_bwd",
        grid_spec=pltpu.PrefetchScalarGridSpec(
            num_scalar_prefetch=1, grid=(FOX_HEADS, steps),
            in_specs=[rows(HQ), cols(HK), pl.BlockSpec((HEAD_DIM, tq), lambda h, s, tab: (h, tab[1, s])), cols(HV), qrow,
                      pl.BlockSpec((None, tq, HEAD_DIM), lambda h, s, tab: (h, tab[1, s], 0)), qrow, qrow, rows(0)],
            out_specs=[cols(0), cols(0), cols(0), pl.BlockSpec((None, tq, 1), lambda h, s, tab: (h, tab[1, s], 0)),
                       pl.BlockSpec((None, nq, 1, tq), lambda h, s, tab: (h, 0, 0, 0))],
            scratch_shapes=[pltpu.VMEM((tq, HEAD_DIM), F32), pltpu.VMEM((tq, HEAD_DIM), F32), pltpu.VMEM((tq, 1), F32),
                            pltpu.VMEM((nq, HEAD_DIM, tq), F32)]),
        out_shape=[sds, sds, sds, jax.ShapeDtypeStruct((FOX_HEADS, T, 1), F32), jax.ShapeDtypeStruct((FOX_HEADS, nq, 1, tq), F32)],
        compiler_params=_params(("parallel", "arbitrary")),
    )(_triangle(nq, True), qkv, qkv, k_t, qkv, cq_row, ck_rep, delta_row, lse, dmix)


GW = SWA_GROUP * HEAD_DIM
GR = SWA_GROUP * WINDOW


def _swa_scores(q_ref, kp_ref, kc_ref, slope_ref, n):
    q = q_ref[...]
    qs = jnp.concatenate([q[:, t * HEAD_DIM:(t + 1) * HEAD_DIM] for t in range(SWA_GROUP)], axis=0)
    kb = jnp.concatenate([kp_ref[...], kc_ref[...]], axis=0)
    r = lax.broadcasted_iota(jnp.int32, (GR, 2 * WINDOW), 0) & (WINDOW - 1)
    jj = lax.broadcasted_iota(jnp.int32, (GR, 2 * WINDOW), 1)
    dist = WINDOW + r - jj
    valid = (dist >= 0) & (dist < WINDOW) & ((n > 0) | (jj >= WINDOW))
    s = _dot(qs, kb, "nt") * SCALE - slope_ref[...] * dist.astype(F32)
    return qs, kb, jnp.where(valid, s, NEG_INF), valid


def _swa_specs():
    HQ, HK, HV = C_SQ // GW, C_SK // HEAD_DIM, C_SV // HEAD_DIM
    q_spec = pl.BlockSpec((WINDOW, GW), lambda g, n: (n, HQ + g))

    def prev(base):
        return pl.BlockSpec((WINDOW, HEAD_DIM), lambda g, n: (jnp.maximum(n - 1, 0), base + g))

    def cur(base):
        return pl.BlockSpec((WINDOW, HEAD_DIM), lambda g, n: (n, base + g))

    col = pl.BlockSpec((None, GR, 1), lambda g, n: (g, 0, 0))
    return q_spec, prev(HK), cur(HK), prev(HV), cur(HV), col


def _swa_fwd(qkv, slopes, sinks):
    T = qkv.shape[0]
    nb = T // WINDOW
    assert C_SQ % GW == 0

    def body(q_ref, kp_ref, kc_ref, vp_ref, vc_ref, slope_ref, sink_ref, o_ref, lse_ref):
        n = pl.program_id(1)
        _, _, s, _ = _swa_scores(q_ref, kp_ref, kc_ref, slope_ref, n)
        m = jnp.maximum(jnp.max(s, axis=-1, keepdims=True), sink_ref[...])
        p = jnp.exp(s - m)
        l = jnp.sum(p, axis=-1, keepdims=True) + jnp.exp(sink_ref[...] - m)
        vb = jnp.concatenate([vp_ref[...], vc_ref[...]], axis=0)
        o = _dot(p.astype(BF16), vb, "nn") / l
        for t in range(SWA_GROUP):
            o_ref[:, t * HEAD_DIM:(t + 1) * HEAD_DIM] = o[t * WINDOW:(t + 1) * WINDOW, :].astype(BF16)
        lse_ref[...] = m + jnp.log(l)

    q_spec, kp, kc, vp, vc, col = _swa_specs()
    return pl.pallas_call(
        body, name="swa_fwd", grid=(SWA_KV_HEADS, nb), in_specs=[q_spec, kp, kc, vp, vc, col, col],
        out_specs=[pl.BlockSpec((WINDOW, GW), lambda g, n: (n, g)), pl.BlockSpec((None, None, GR, 1), lambda g, n: (g, n, 0, 0))],
        out_shape=[jax.ShapeDtypeStruct((T, SWA_HEADS * HEAD_DIM), BF16), jax.ShapeDtypeStruct((SWA_KV_HEADS, nb, GR, 1), F32)],
        compiler_params=_params(("parallel", "arbitrary")),
    )(qkv, qkv, qkv, qkv, qkv, slopes, sinks)


def _swa_bwd(qkv, slopes, sinks, out, lse, dmix):
    T = qkv.shape[0]
    nb = T // WINDOW
    DO = FOX_W // GW
    assert FOX_W % GW == 0

    def body(q_ref, kp_ref, kc_ref, vp_ref, vc_ref, slope_ref, sink_ref, o_ref, lse_ref, do_ref,
             dq_ref, dk_ref, dv_ref, dsink_ref, sink_sc):
        n = pl.program_id(1)

        @pl.when(n == 0)
        def _():
            dk_ref[...] = jnp.zeros_like(dk_ref)
            dv_ref[...] = jnp.zeros_like(dv_ref)
            sink_sc[...] = jnp.zeros_like(sink_sc)

        qs, kb, s, valid = _swa_scores(q_ref, kp_ref, kc_ref, slope_ref, n)
        lse = lse_ref[...]
        p = jnp.where(valid, jnp.exp(s - lse), 0.0)
        vb = jnp.concatenate([vp_ref[...], vc_ref[...]], axis=0)
        do = jnp.concatenate([do_ref[:, t * HEAD_DIM:(t + 1) * HEAD_DIM] for t in range(SWA_GROUP)], axis=0)
        oo = jnp.concatenate([o_ref[:, t * HEAD_DIM:(t + 1) * HEAD_DIM] for t in range(SWA_GROUP)], axis=0)
        dp = _dot(do, vb, "nt")
        delta = jnp.sum(do.astype(F32) * oo.astype(F32), axis=-1, keepdims=True)
        ds = p * (dp - delta)
        dsb = ds.astype(BF16)
        dq = _dot(dsb, kb, "nn") * SCALE
        for t in range(SWA_GROUP):
            dq_ref[:, t * HEAD_DIM:(t + 1) * HEAD_DIM] = dq[t * WINDOW:(t + 1) * WINDOW, :]
        dkb = _dot(dsb, qs, "tn") * SCALE
        dvb = _dot(p.astype(BF16), do, "tn")
        r_prev = pl.ds(pl.multiple_of(jnp.maximum(n - 1, 0) * WINDOW, WINDOW), WINDOW)
        r_cur = pl.ds(pl.multiple_of(n * WINDOW, WINDOW), WINDOW)
        dk_ref[r_prev, :] += dkb[:WINDOW, :]
        dk_ref[r_cur, :] += dkb[WINDOW:, :]
        dv_ref[r_prev, :] += dvb[:WINDOW, :]
        dv_ref[r_cur, :] += dvb[WINDOW:, :]
        sink_sc[...] -= jnp.exp(sink_ref[...] - lse) * delta

        @pl.when(n == nb - 1)
        def _():
            tot = [jnp.zeros((1, 128), F32) + jnp.sum(sink_sc[t * WINDOW:(t + 1) * WINDOW, :]) for t in range(SWA_GROUP)]
            dsink_ref[...] = jnp.concatenate(tot + [jnp.zeros((8 - SWA_GROUP, 128), F32)], axis=0)

    q_spec, kp, kc, vp, vc, col = _swa_specs()
    kv_acc = pl.BlockSpec((T, HEAD_DIM), lambda g, n: (0, g))
    return pl.pallas_call(
        body, name="swa_bwd", grid=(SWA_KV_HEADS, nb),
        in_specs=[q_spec, kp, kc, vp, vc, col, col, pl.BlockSpec((WINDOW, GW), lambda g, n: (n, g)),
                  pl.BlockSpec((None, None, GR, 1), lambda g, n: (g, n, 0, 0)), pl.BlockSpec((WINDOW, GW), lambda g, n: (n, DO + g))],
        out_specs=[pl.BlockSpec((WINDOW, GW), lambda g, n: (n, g)), kv_acc, kv_acc, pl.BlockSpec((None, 8, 128), lambda g, n: (g, 0, 0))],
        out_shape=[jax.ShapeDtypeStruct((T, SWA_HEADS * HEAD_DIM), F32), jax.ShapeDtypeStruct((T, SWA_KV_HEADS * HEAD_DIM), F32),
                   jax.ShapeDtypeStruct((T, SWA_KV_HEADS * HEAD_DIM), F32), jax.ShapeDtypeStruct((SWA_KV_HEADS, 8, 128), F32)],
        scratch_shapes=[pltpu.VMEM((GR, 1), F32)],
        compiler_params=_params(("parallel", "arbitrary")),
    )(qkv, qkv, qkv, qkv, qkv, slopes, sinks, out, lse, dmix)


def _mem_fwd(qkv, mk, mv):
    T, ML = qkv.shape[0], mk.shape[0]
    tq = _tile(T, 1024)
    HQ = C_MQ // HEAD_DIM

    def body(q_ref, k_ref, v_ref, o_ref, lse_ref):
        s = _dot(q_ref[...], k_ref[...], "nt") * SCALE
        m = jnp.max(s, axis=-1, keepdims=True)
        p = jnp.exp(s - m)
        l = jnp.sum(p, axis=-1, keepdims=True)
        o_ref[...] = (_dot(p.astype(BF16), v_ref[...], "nn") / l).astype(BF16)
        lse_ref[...] = m + jnp.log(l)

    kv = pl.BlockSpec((ML, HEAD_DIM), lambda h, i: (0, h))
    return pl.pallas_call(
        body, name="mem_fwd", grid=(MEM_HEADS, T // tq),
        in_specs=[pl.BlockSpec((tq, HEAD_DIM), lambda h, i: (i, HQ + h)), kv, kv],
        out_specs=[pl.BlockSpec((tq, HEAD_DIM), lambda h, i: (i, h)), pl.BlockSpec((None, tq, 1), lambda h, i: (h, i, 0))],
        out_shape=[jax.ShapeDtypeStruct((T, MEM_HEADS * HEAD_DIM), BF16), jax.ShapeDtypeStruct((MEM_HEADS, T, 1), F32)],
        compiler_params=_params(("parallel", "arbitrary")),
    )(qkv, mk, mv)


def _mem_bwd(qkv, mk, mv, out, lse, dmix):
    T, ML = qkv.shape[0], mk.shape[0]
    tq = _tile(T, 1024)
    HQ = C_MQ // HEAD_DIM
    DO = (FOX_W + SWA_HEADS * HEAD_DIM) // HEAD_DIM

    def body(q_ref, k_ref, v_ref, o_ref, lse_ref, do_ref, dq_ref, dk_ref, dv_ref):
        q, k, v, do = q_ref[...], k_ref[...], v_ref[...], do_ref[...]
        p = jnp.exp(_dot(q, k, "nt") * SCALE - lse_ref[...])
        dp = _dot(do, v, "nt")
        delta = jnp.sum(do.astype(F32) * o_ref[...].astype(F32), axis=-1, keepdims=True)
        dsb = (p * (dp - delta)).astype(BF16)
        dq_ref[...] = _dot(dsb, k, "nn") * SCALE
        dk_part = _dot(dsb, q, "tn") * SCALE
        dv_part = _dot(p.astype(BF16), do, "tn")

        @pl.when(pl.program_id(1) == 0)
        def _():
            dk_ref[...] = dk_part
            dv_ref[...] = dv_part

        @pl.when(pl.program_id(1) > 0)
        def _():
            dk_ref[...] += dk_part
            dv_ref[...] += dv_part

    kv = pl.BlockSpec((ML, HEAD_DIM), lambda h, i: (0, h))
    qb = pl.BlockSpec((tq, HEAD_DIM), lambda h, i: (i, h))
    return pl.pallas_call(
        body, name="mem_bwd", grid=(MEM_HEADS, T // tq),
        in_specs=[pl.BlockSpec((tq, HEAD_DIM), lambda h, i: (i, HQ + h)), kv, kv, qb,
                  pl.BlockSpec((None, tq, 1), lambda h, i: (h, i, 0)), pl.BlockSpec((tq, HEAD_DIM), lambda h, i: (i, DO + h))],
        out_specs=[qb, kv, kv],
        out_shape=[jax.ShapeDtypeStruct((T, MEM_HEADS * HEAD_DIM), F32), jax.ShapeDtypeStruct((ML, MEM_HEADS * HEAD_DIM), F32),
                   jax.ShapeDtypeStruct((ML, MEM_HEADS * HEAD_DIM), F32)],
        compiler_params=_params(("parallel", "arbitrary")),
    )(qkv, mk, mv, out, lse, dmix)


HBM = pl.BlockSpec(memory_space=pltpu.HBM)


def _place():
    x, y, c = lax.axis_index("x"), lax.axis_index("y"), lax.axis_index("c")
    chips = [(1 - x, y), (x, 1 - y), (1 - x, 1 - y)]
    return x, y, c, chips


def _remote(src, dst, send_sem, recv_sem, device):
    return pltpu.make_async_remote_copy(src_ref=src, dst_ref=dst, send_sem=send_sem, recv_sem=recv_sem,
                                        device_id=device, device_id_type=MESH)


def _place_ids():
    x, y, c = lax.axis_index("x"), lax.axis_index("y"), lax.axis_index("c")
    order = [2 * x + y, 2 * (1 - x) + y, 2 * x + (1 - y), 2 * (1 - x) + (1 - y)]
    return jnp.stack([2 * x + y, c] + order).astype(jnp.int32)


def _cast_place(name, w, ids, *after, keep_own=False):
    R, C = w.shape
    tr = _tile(R, 256, 16)
    n_out = 2 if keep_own else 1

    def body(ids_ref, w_ref, *rest):
        for o_ref in rest[-n_out:]:
            o_ref[...] = w_ref[...].astype(BF16)

    res = pl.pallas_call(
        body, name=name,
        grid_spec=pltpu.PrefetchScalarGridSpec(
            num_scalar_prefetch=1, grid=(R // tr,),
            in_specs=[pl.BlockSpec((tr, C), lambda i, ids: (i, 0))] + [pl.BlockSpec(memory_space=pl.ANY)] * len(after),
            out_specs=[pl.BlockSpec((None, tr, C), lambda i, ids: (ids[0], i, 0)), pl.BlockSpec((tr, C), lambda i, ids: (i, 0))][:n_out]),
        out_shape=[jax.ShapeDtypeStruct((N_CHIPS, R, C), BF16), jax.ShapeDtypeStruct((R, C), BF16)][:n_out],
        compiler_params=_params(("parallel",)),
    )(ids, w, *after)
    return res if keep_own else res[0]


SEM = pl.BlockSpec(memory_space=pltpu.SEMAPHORE)
EFFECT = pltpu.SideEffectType.DATAFLOW_SIDE_EFFECTING


def _hbm(a):
    return pltpu.with_memory_space_constraint(a, pltpu.HBM)


def _gather_start(name, placed, after, to_sibling=False):
    n = len(placed)

    ns = 3 * n

    def body(*refs):
        send, recv = refs[n + 1:n + 1 + ns], refs[n + 1 + ns:n + 1 + 2 * ns]
        buf = refs[n + 1 + 2 * ns:2 * n + 1 + 2 * ns]
        token = refs[2 * n + 1 + 2 * ns]
        x, y, c, chips = _place()
        me = 2 * x + y
        for a in range(n):
            half = buf[a].shape[1] // 2
            for j, (cx, cy) in enumerate(chips):
                block, peer = (2 * cx + cy, (x, y, 1 - c)) if to_sibling else (me, (cx, cy, c))
                part = buf[a].at[block, pl.ds(c * half, half)]
                _remote(part, part, send[3 * a + j], recv[3 * a + j], peer).start()
        token[...] = jnp.zeros_like(token)

    res = pl.pallas_call(
        body, name=name, in_specs=[HBM] * n + [pl.BlockSpec(memory_space=pl.ANY)],
        out_specs=[SEM] * (2 * ns) + [HBM] * n + [pl.BlockSpec(memory_space=pltpu.VMEM)],
        out_shape=[pltpu.SemaphoreType.DMA(())] * (2 * ns)
        + [pltpu.HBM(s.shape, s.dtype) for s in placed] + [jax.ShapeDtypeStruct((8, 128), F32)],
        input_output_aliases={a: 2 * ns + a for a in range(n)},
        compiler_params=pltpu.CompilerParams(has_side_effects=EFFECT),
    )(*[_hbm(s) for s in placed], after)
    return list(res[:ns]), list(res[ns:2 * ns]), list(res[2 * ns:2 * ns + n]), res[2 * ns + n]


def _gather_wait(name, send, recv, bufs, after, to_sibling=False):
    n = len(bufs)

    ns = 3 * n

    def body(*refs):
        buf = refs[:n]
        send_ref, recv_ref = refs[n:n + ns], refs[n + ns:n + 2 * ns]
        x, y, c, chips = _place()
        ids = [2 * cx + cy for cx, cy in chips]
        for a in range(n):
            half = buf[a].shape[1] // 2
            for j in range(3):
                sent = buf[a].at[ids[j], pl.ds(c * half, half)]
                landed = buf[a].at[ids[j], pl.ds((1 - c) * half, half)] if to_sibling else sent
                cp = _remote(sent, landed, send_ref[3 * a + j], recv_ref[3 * a + j], (x, y, c))
                cp.wait_send()
                cp.wait_recv()

    res = pl.pallas_call(
        body, name=name, in_specs=[HBM] * n + [SEM] * (2 * ns) + [pl.BlockSpec(memory_space=pl.ANY)], out_specs=[HBM] * n,
        out_shape=[pltpu.HBM(s.shape, s.dtype) for s in bufs], input_output_aliases={a: a for a in range(n)},
        compiler_params=pltpu.CompilerParams(has_side_effects=EFFECT),
    )(*bufs, *send, *recv, after)
    return list(res)


def _gather_forward(name, bufs):
    n = len(bufs)

    def body(*refs):
        buf = refs[n:2 * n]
        send, recv = refs[2 * n:]
        x, y, c, chips = _place()
        ids = [2 * cx + cy for cx, cy in chips]
        copies = []
        for a in range(n):
            half = buf[a].shape[1] // 2
            for j in range(3):
                landed = buf[a].at[ids[j], pl.ds(c * half, half)]
                cp = _remote(landed, landed, send.at[a, j], recv.at[a, j], (x, y, 1 - c))
                cp.start()
                copies.append(cp)
        for a in range(n):
            half = buf[a].shape[1] // 2
            for j in range(3):
                landed = buf[a].at[ids[j], pl.ds((1 - c) * half, half)]
                _remote(landed, landed, send.at[a, j], recv.at[a, j], (x, y, c)).wait_recv()
        for cp in copies:
            cp.wait_send()

    return pl.pallas_call(
        body, name=name, in_specs=[HBM] * n, out_specs=[HBM] * n,
        out_shape=[jax.ShapeDtypeStruct(s.shape, s.dtype) for s in bufs], input_output_aliases={a: a for a in range(n)},
        scratch_shapes=[pltpu.SemaphoreType.DMA((n, 3)), pltpu.SemaphoreType.DMA((n, 3))],
    )(*bufs)


def _pair_start(name, grads):
    n = len(grads)
    ns = N_CHIPS * n

    def body(*refs):
        send, recv = refs[2 * n:2 * n + ns], refs[2 * n + ns:2 * n + 2 * ns]
        src = refs[2 * n + 2 * ns:3 * n + 2 * ns]
        land = refs[3 * n + 2 * ns:4 * n + 2 * ns]
        token = refs[4 * n + 2 * ns]
        x, y, c, chips = _place()
        order = [2 * x + y] + [2 * cx + cy for cx, cy in chips]
        for a in range(n):
            half = src[a].shape[1] // 2
            for j in range(N_CHIPS):
                _remote(src[a].at[order[j], pl.ds((1 - c) * half, half)], land[a].at[j],
                        send[N_CHIPS * a + j], recv[N_CHIPS * a + j], (x, y, 1 - c)).start()
        token[...] = jnp.zeros_like(token)

    lands = [jax.ShapeDtypeStruct((N_CHIPS, g.shape[1] // 2, g.shape[2]), g.dtype) for g in grads]
    res = pl.pallas_call(
        body, name=name, in_specs=[HBM] * (2 * n),
        out_specs=[SEM] * (2 * ns) + [HBM] * (2 * n) + [pl.BlockSpec(memory_space=pltpu.VMEM)],
        out_shape=[pltpu.SemaphoreType.DMA(())] * (2 * ns) + [pltpu.HBM(g.shape, g.dtype) for g in grads]
        + [pltpu.HBM(l.shape, l.dtype) for l in lands] + [jax.ShapeDtypeStruct((8, 128), F32)],
        input_output_aliases={a: 2 * ns + a for a in range(2 * n)},
        compiler_params=pltpu.CompilerParams(has_side_effects=EFFECT),
    )(*[_hbm(g) for g in grads], *[_hbm(lax.empty(l.shape, l.dtype)) for l in lands])
    return list(res[:ns]), list(res[ns:2 * ns]), list(res[2 * ns:2 * ns + n]), list(res[2 * ns + n:2 * ns + 2 * n]), res[2 * ns + 2 * n]


def _pair_wait(name, send, recv, grads, lands, after):
    n = len(grads)
    ns = N_CHIPS * n

    def body(*refs):
        src, land = refs[:n], refs[n:2 * n]
        send_ref, recv_ref = refs[2 * n:2 * n + ns], refs[2 * n + ns:2 * n + 2 * ns]
        x, y, c, _ = _place()
        for a in range(n):
            for j in range(N_CHIPS):
                cp = _remote(land[a].at[j], land[a].at[j], send_ref[N_CHIPS * a + j], recv_ref[N_CHIPS * a + j], (x, y, c))
                cp.wait_send()
                cp.wait_recv()

    res = pl.pallas_call(
        body, name=name, in_specs=[HBM] * (2 * n) + [SEM] * (2 * ns) + [pl.BlockSpec(memory_space=pl.ANY)],
        out_specs=[HBM] * (2 * n), out_shape=[pltpu.HBM(g.shape, g.dtype) for g in grads] + [pltpu.HBM(l.shape, l.dtype) for l in lands],
        input_output_aliases={a: a for a in range(2 * n)},
        compiler_params=pltpu.CompilerParams(has_side_effects=EFFECT),
    )(*grads, *lands, *send, *recv, after)
    return list(res[:n]), list(res[n:])


def _chip_start(name, parts):
    n = len(parts)
    ns = 3 * n

    def body(*refs):
        send, recv = refs[2 * n:2 * n + ns], refs[2 * n + ns:2 * n + 2 * ns]
        src = refs[2 * n + 2 * ns:3 * n + 2 * ns]
        land = refs[3 * n + 2 * ns:4 * n + 2 * ns]
        token = refs[4 * n + 2 * ns]
        x, y, c, chips = _place()
        for a in range(n):
            for j, (cx, cy) in enumerate(chips):
                _remote(src[a].at[j], land[a].at[j], send[3 * a + j], recv[3 * a + j], (cx, cy, c)).start()
        token[...] = jnp.zeros_like(token)

    res = pl.pallas_call(
        body, name=name, in_specs=[HBM] * (2 * n),
        out_specs=[SEM] * (2 * ns) + [HBM] * (2 * n) + [pl.BlockSpec(memory_space=pltpu.VMEM)],
        out_shape=[pltpu.SemaphoreType.DMA(())] * (2 * ns) + [pltpu.HBM(p.shape, p.dtype) for p in parts] * 2
        + [jax.ShapeDtypeStruct((8, 128), F32)],
        input_output_aliases={a: 2 * ns + a for a in range(2 * n)},
        compiler_params=pltpu.CompilerParams(has_side_effects=EFFECT),
    )(*[_hbm(p) for p in parts], *[_hbm(lax.empty(p.shape, p.dtype)) for p in parts])
    return list(res[:ns]), list(res[ns:2 * ns]), list(res[2 * ns:2 * ns + n]), list(res[2 * ns + n:2 * ns + 2 * n]), res[2 * ns + 2 * n]


def _chip_wait(name, send, recv, parts, lands, after):
    n = len(parts)
    ns = 3 * n

    def body(*refs):
        src, land = refs[:n], refs[n:2 * n]
        send_ref, recv_ref = refs[2 * n:2 * n + ns], refs[2 * n + ns:2 * n + 2 * ns]
        x, y, c, _ = _place()
        for a in range(n):
            for j in range(3):
                cp = _remote(src[a].at[j], land[a].at[j], send_ref[3 * a + j], recv_ref[3 * a + j], (x, y, c))
                cp.wait_send()
                cp.wait_recv()

    res = pl.pallas_call(
        body, name=name, in_specs=[HBM] * (2 * n) + [SEM] * (2 * ns) + [pl.BlockSpec(memory_space=pl.ANY)],
        out_specs=[HBM] * (2 * n), out_shape=[pltpu.HBM(p.shape, p.dtype) for p in parts] * 2,
        input_output_aliases={a: a for a in range(2 * n)},
        compiler_params=pltpu.CompilerParams(has_side_effects=EFFECT),
    )(*parts, *lands, *send, *recv, after)
    return list(res[n:])


def _share_start(name, shards):
    n = len(shards)

    def body(*refs):
        send, recv = refs[n:2 * n], refs[2 * n:3 * n]
        buf = refs[3 * n:4 * n]
        token = refs[4 * n]
        x, y, c, _ = _place()
        for a in range(n):
            half = buf[a].shape[0] // 2
            mine = buf[a].at[pl.ds(c * half, half)]
            _remote(mine, mine, send[a], recv[a], (x, y, 1 - c)).start()
        token[...] = jnp.zeros_like(token)

    res = pl.pallas_call(
        body, name=name, in_specs=[HBM] * n,
        out_specs=[SEM] * (2 * n) + [HBM] * n + [pl.BlockSpec(memory_space=pltpu.VMEM)],
        out_shape=[pltpu.SemaphoreType.DMA(())] * (2 * n) + [pltpu.HBM(s.shape, s.dtype) for s in shards]
        + [jax.ShapeDtypeStruct((8, 128), F32)],
        input_output_aliases={a: 2 * n + a for a in range(n)},
        compiler_params=pltpu.CompilerParams(has_side_effects=EFFECT),
    )(*[_hbm(s) for s in shards])
    return list(res[:n]), list(res[n:2 * n]), list(res[2 * n:3 * n]), res[3 * n]


def _share_wait(name, send, recv, shards, after):
    n = len(shards)

    def body(*refs):
        buf = refs[:n]
        send_ref, recv_ref = refs[n:2 * n], refs[2 * n:3 * n]
        x, y, c, _ = _place()
        for a in range(n):
            half = buf[a].shape[0] // 2
            cp = _remote(buf[a].at[pl.ds(c * half, half)], buf[a].at[pl.ds((1 - c) * half, half)], send_ref[a], recv_ref[a], (x, y, c))
            cp.wait_send()
            cp.wait_recv()

    res = pl.pallas_call(
        body, name=name, in_specs=[HBM] * n + [SEM] * (2 * n) + [pl.BlockSpec(memory_space=pl.ANY)], out_specs=[HBM] * n,
        out_shape=[pltpu.HBM(s.shape, s.dtype) for s in shards], input_output_aliases={a: a for a in range(n)},
        compiler_params=pltpu.CompilerParams(has_side_effects=EFFECT),
    )(*shards, *send, *recv, after)
    return list(res)


def _small_start(buf):
    R, W = buf.shape
    ns = N_DEV - 1

    def body(*refs):
        send, recv = refs[2:2 + ns], refs[2 + ns:2 + 2 * ns]
        src, land, token = refs[2 + 2 * ns], refs[3 + 2 * ns], refs[4 + 2 * ns]
        x, y, c, _ = _place()
        me = 4 * x + 2 * y + c
        for k in range(1, N_DEV):
            peer = (x ^ (k >> 2), y ^ ((k >> 1) & 1), c ^ (k & 1))
            _remote(src, land.at[me], send[k - 1], recv[k - 1], peer).start()
        token[...] = jnp.zeros_like(token)

    res = pl.pallas_call(
        body, name="small_start", in_specs=[HBM, HBM],
        out_specs=[SEM] * (2 * ns) + [HBM, HBM, pl.BlockSpec(memory_space=pltpu.VMEM)],
        out_shape=[pltpu.SemaphoreType.DMA(())] * (2 * ns) + [pltpu.HBM((R, W), F32), pltpu.HBM((N_DEV, R, W), F32),
                                                                jax.ShapeDtypeStruct((8, 128), F32)],
        input_output_aliases={0: 2 * ns, 1: 2 * ns + 1},
        compiler_params=pltpu.CompilerParams(has_side_effects=EFFECT),
    )(_hbm(buf), _hbm(jnp.zeros((N_DEV, R, W), F32)))
    return list(res[:ns]), list(res[ns:2 * ns]), res[2 * ns], res[2 * ns + 1], res[2 * ns + 2]


def _small_wait(send, recv, buf, land, after):
    ns = N_DEV - 1

    def body(*refs):
        land_ref = refs[1]
        send_ref, recv_ref = refs[2:2 + ns], refs[2 + ns:2 + 2 * ns]
        x, y, c, _ = _place()
        me = 4 * x + 2 * y + c
        for k in range(1, N_DEV):
            landed = land_ref.at[me ^ k]
            cp = _remote(landed, landed, send_ref[k - 1], recv_ref[k - 1], (x, y, c))
            cp.wait_send()
            cp.wait_recv()

    return pl.pallas_call(
        body, name="small_wait", in_specs=[HBM, HBM] + [SEM] * (2 * ns) + [pl.BlockSpec(memory_space=pl.ANY)],
        out_specs=[HBM, HBM], out_shape=[pltpu.HBM(buf.shape, buf.dtype), pltpu.HBM(land.shape, land.dtype)],
        input_output_aliases={0: 0, 1: 1}, compiler_params=pltpu.CompilerParams(has_side_effects=EFFECT),
    )(buf, land, *send, *recv, after)


def _small_sum(buf, land):
    def body(buf_ref, land_ref, out_ref):
        x, y, c, _ = _place()
        me = 4 * x + 2 * y + c
        total = None
        for d in range(N_DEV):
            term = jnp.where(me == d, buf_ref[...], land_ref[d])
            total = term if total is None else total + term
        out_ref[...] = total

    return pl.pallas_call(body, name="small_sum", out_shape=jax.ShapeDtypeStruct(buf.shape, F32))(buf, land)


def _pair_sum_bf16(name, grad, theirs, ids):
    _, R2, C = theirs.shape
    tr = _tile(R2, 512 if C <= 2048 else 256, 16)
    nrb = R2 // tr

    def body(ids_ref, a_ref, b_ref, o_ref):
        o_ref[...] = (a_ref[...] + b_ref[...]).astype(BF16)

    return pl.pallas_call(
        body, name=name,
        grid_spec=pltpu.PrefetchScalarGridSpec(
            num_scalar_prefetch=1, grid=(3, nrb),
            in_specs=[pl.BlockSpec((None, tr, C), lambda j, i, ids: (ids[3 + j], ids[1] * nrb + i, 0)),
                      pl.BlockSpec((None, tr, C), lambda j, i, ids: (j + 1, i, 0))],
            out_specs=pl.BlockSpec((None, tr, C), lambda j, i, ids: (j, i, 0))),
        out_shape=jax.ShapeDtypeStruct((3, R2, C), BF16), compiler_params=_params(("parallel", "parallel")),
    )(ids, grad, theirs)


def _chip_sum(name, grad, theirs, arrived, ids):
    _, R2, C = theirs.shape
    tr = _tile(R2, 512 if C <= 2048 else 256, 16)
    nrb = R2 // tr

    def body(ids_ref, a_ref, b_ref, r_ref, o_ref):
        tot = a_ref[...] + b_ref[...]
        for j in range(3):
            tot = tot + r_ref[j].astype(F32)
        o_ref[...] = tot

    return pl.pallas_call(
        body, name=name,
        grid_spec=pltpu.PrefetchScalarGridSpec(
            num_scalar_prefetch=1, grid=(nrb,),
            in_specs=[pl.BlockSpec((None, tr, C), lambda i, ids: (ids[0], ids[1] * nrb + i, 0)),
                      pl.BlockSpec((None, tr, C), lambda i, ids: (0, i, 0)),
                      pl.BlockSpec((3, tr, C), lambda i, ids: (0, i, 0))],
            out_specs=pl.BlockSpec((tr, C), lambda i, ids: (ids[1] * nrb + i, 0))),
        out_shape=jax.ShapeDtypeStruct((2 * R2, C), F32), compiler_params=_params(("parallel",)),
    )(ids, grad, theirs, arrived)


def _adamw(name, w, g, m, v, emit_grad=False):
    R, C = w.shape
    tr = _tile(R, 256 if C <= 2048 else 128, 8)
    c1 = 1.0 / (1.0 - ADAM_B1 ** ADAM_STEP)
    c2 = 1.0 / (1.0 - ADAM_B2 ** ADAM_STEP)
    n_out = 4 if emit_grad else 3

    def body(w_ref, g_ref, m_ref, v_ref, d_ref, mo_ref, vo_ref, *rest):
        gv = g_ref[...]
        mn = ADAM_B1 * m_ref[...] + (1.0 - ADAM_B1) * gv
        vn = ADAM_B2 * v_ref[...] + (1.0 - ADAM_B2) * (gv * gv)
        d_ref[...] = -ADAM_LR * ((mn * c1) / (jnp.sqrt(vn * c2) + ADAM_EPS) + ADAM_WD * w_ref[...])
        mo_ref[...] = mn
        vo_ref[...] = vn
        if emit_grad:
            rest[0][...] = gv

    spec = pl.BlockSpec((tr, C), lambda i: (i, 0))
    sds = jax.ShapeDtypeStruct((R, C), F32)
    return pl.pallas_call(body, name=name, grid=(R // tr,), in_specs=[spec] * 4, out_specs=[spec] * n_out, out_shape=[sds] * n_out,
                          compiler_params=_params(("parallel",)))(w, g, m, v)


SMALL = ["ffn1_norm", "mix_norm", "mem_norm", "forget_bias", "fox_q_gain", "fox_k_gain", "swa_q_gain", "swa_k_gain", "swa_sinks",
         "mem_q_gain", "mem_k_gain", "ffn2_norm"]
LARGE = ["ffn1_gate", "ffn1_up", "ffn1_down", "w_in", "w_mem_k", "w_mem_v", "w_out", "ffn2_gate", "ffn2_up", "ffn2_down"]
GATHER_GROUPS = [["ffn1_gate", "ffn1_up"], ["ffn1_down"], ["w_in", "w_mem_k", "w_mem_v"], ["w_out", "ffn2_gate", "ffn2_up", "ffn2_down"]]
WEIGHTS = ["ffn1_norm", "ffn1_gate", "ffn1_up", "ffn1_down", "mix_norm", "mem_norm", "w_in", "forget_bias", "w_mem_k", "w_mem_v",
           "fox_q_gain", "fox_k_gain", "swa_q_gain", "swa_k_gain", "swa_sinks", "mem_q_gain", "mem_k_gain", "w_out", "ffn2_norm",
           "ffn2_gate", "ffn2_up", "ffn2_down"]


def _pad_proj_cols(w):
    out = jnp.zeros((w.shape[0], PROJ_W), w.dtype)
    for start, width, pstart in REF_GROUPS:
        out = lax.dynamic_update_slice(out, w[:, start:start + width], (0, pstart))
    return out


def _unpad_proj_cols(w):
    return jnp.concatenate([w[:, pstart:pstart + width] for _, width, pstart in REF_GROUPS], axis=1)


def _pack_small(vals):
    flat = jnp.concatenate([vals[k].reshape(-1).astype(F32) for k in SMALL + ["loss"]])
    n = flat.shape[0]
    total = -(-n // 1024) * 1024
    return jnp.pad(flat, (0, total - n)).reshape(total // 128, 128)


def _unpack_small(buf, shapes):
    flat = buf.reshape(-1)
    out, off = {}, 0
    for k in SMALL + ["loss"]:
        size = int(np.prod(shapes[k]))
        out[k] = flat[off:off + size].reshape(shapes[k])
        off += size
    return out


def kernel(x, mem, ffn1_norm, ffn1_gate, ffn1_up, ffn1_down, mix_norm, mem_norm, w_in, forget_bias, w_mem_k, w_mem_v, fox_q_gain, fox_k_gain, swa_q_gain, swa_k_gain, swa_sinks, mem_q_gain, mem_k_gain, w_out, ffn2_norm, ffn2_gate, ffn2_up, ffn2_down, loss_target, m_ffn1_norm, m_ffn1_gate, m_ffn1_up, m_ffn1_down, m_mix_norm, m_mem_norm, m_w_in, m_forget_bias, m_w_mem_k, m_w_mem_v, m_fox_q_gain, m_fox_k_gain, m_swa_q_gain, m_swa_k_gain, m_swa_sinks, m_mem_q_gain, m_mem_k_gain, m_w_out, m_ffn2_norm, m_ffn2_gate, m_ffn2_up, m_ffn2_down, v_ffn1_norm, v_ffn1_gate, v_ffn1_up, v_ffn1_down, v_mix_norm, v_mem_norm, v_w_in, v_forget_bias, v_w_mem_k, v_w_mem_v, v_fox_q_gain, v_fox_k_gain, v_swa_q_gain, v_swa_k_gain, v_swa_sinks, v_mem_q_gain, v_mem_k_gain, v_w_out, v_ffn2_norm, v_ffn2_gate, v_ffn2_up, v_ffn2_down):
    given = dict(locals())
    T, D = x.shape[1], x.shape[2]
    ML = mem.shape[1]
    xin = x.reshape(T, D)
    target = loss_target.reshape(T, D)
    memin = mem.reshape(ML, D)

    ids = _place_ids()
    shard = {k: given[k][0] for k in LARGE}
    started, after = [], ids
    for gi, group in enumerate(GATHER_GROUPS):
        also = {}
        if "w_in" in group:
            tied = lax.optimization_barrier((given["w_in"], given["m_w_in"], given["v_w_in"], after))
            w_in_rows = tuple(t[0] for t in tied[:3])
            shard["w_in"] = _pad_proj_cols(w_in_rows[0])
            also["w_in"] = w_in_rows[1:]
        if gi == 0:
            placed, own = zip(*[_cast_place("cast_" + k, shard[k], ids, after, keep_own=True) for k in group])
        else:
            placed = [_cast_place("cast_" + k, shard[k], ids, after, *also.get(k, ())) for k in group]
        send, recv, bufs, after = _gather_start("gather_start_%d" % gi, list(placed), after)
        started.append((send, recv, bufs))

    def arrive(gi, done):
        send, recv, bufs = started[gi]
        bufs = _gather_wait("gather_wait_%d" % gi, send, recv, bufs, done)
        return dict(zip(GATHER_GROUPS[gi], _gather_forward("gather_forward_%d" % gi, bufs)))

    gains = jnp.concatenate([fox_q_gain, fox_k_gain, swa_q_gain, swa_k_gain, mem_q_gain,
                             jnp.pad(forget_bias, ((0, 0), (0, HEAD_DIM - FOX_HEADS))), jnp.zeros((2, HEAD_DIM), F32)], axis=0)
    slopes_np = 2.0 ** (-8.0 * np.arange(1, SWA_HEADS + 1) / SWA_HEADS)
    slopes = jnp.asarray(np.repeat(slopes_np, WINDOW).reshape(SWA_KV_HEADS, GR, 1), F32)
    sinks = jnp.repeat(swa_sinks.reshape(SWA_HEADS), WINDOW).reshape(SWA_KV_HEADS, GR, 1)

    h1 = _rms_fwd("ffn1_norm_fwd", xin, ffn1_norm + after[0, 0])
    own_block = jnp.stack([jnp.zeros((), jnp.int32), ids[0]]).reshape(2, 1)
    part = _ffn_gu_blocks("ffn1_gate_up_own", h1, own[0][None], own[1][None], own_block, N_CHIPS)
    full = arrive(0, part[2])
    wg1, wu1 = full["ffn1_gate"], full["ffn1_up"]
    fg1, fu1, a1 = _ffn_gu_blocks("ffn1_gate_up", h1, wg1, wu1, jnp.stack([ids[3:], ids[3:]]), N_CHIPS, prev=part)
    wd1 = arrive(1, a1)["ffn1_down"].reshape(-1, D)
    x1, h2 = _residual_norm("ffn1_down", a1, wd1, xin, 0.5, mix_norm, 256)
    full = arrive(2, h2)
    win = full["w_in"].reshape(D, PROJ_W)
    wmk = full["w_mem_k"].reshape(D, MEM_HEADS * HEAD_DIM)
    wmv = full["w_mem_v"].reshape(D, MEM_HEADS * HEAD_DIM)
    proj = _mm2d("proj_in", h2, win, "nn", F32, tn=1408, tk=2048, n_outer=True)
    qkv, logf, k_t, v_t = _prep_fwd(proj, gains)
    cum = _cumsum_rows("forget_cumsum", [logf], False)
    cum_h = cum[:, :FOX_HEADS].T
    cq_row = cum_h.reshape(FOX_HEADS, 1, T)
    ck_rep = jnp.broadcast_to(cum_h[:, :, None], (FOX_HEADS, T, HEAD_DIM))
    mn = _rms_fwd("mem_norm_fwd", memin, mem_norm)
    mk_raw = _mm2d("mem_k_proj", mn, wmk, "nn", F32)
    mv = _mm2d("mem_v_proj", mn, wmv, "nn", BF16)
    mk = _head_norm_rows(mk_raw, mem_k_gain)
    out_a, out_a_f32, lse_a = _fox_fwd(qkv, v_t, cq_row, ck_rep)
    send, recv, bufs = started[3]
    bufs = _gather_wait("gather_wait_3", send, recv, bufs, out_a)
    send, recv, bufs, token = _gather_start("gather_pass_start_3", bufs, out_a, to_sibling=True)
    out_b, lse_b = _swa_fwd(qkv, slopes + token[0, 0], sinks)
    out_c, lse_c = _mem_fwd(qkv, mk, mv)
    mixed = jnp.concatenate([out_a, out_b, out_c], axis=1)
    full = dict(zip(GATHER_GROUPS[3], _gather_wait("gather_pass_wait_3", send, recv, bufs, mixed, to_sibling=True)))
    wo = full["w_out"].reshape(-1, D)
    wg2, wu2, wd2 = full["ffn2_gate"], full["ffn2_up"], full["ffn2_down"].reshape(-1, D)
    x2, h3 = _residual_norm("mix_out", mixed, wo, x1, 1.0, ffn2_norm, 512)
    fg2, fu2, a2 = _ffn_gu("ffn2_gate_up", h3, wg2, wu2)
    dx3, dyb3, loss_blocks = _ffn_down_loss("ffn2_down", a2, wd2, x2, target)

    grads, small, res = {}, {"loss": jnp.sum(loss_blocks[::8, 0])}, {}

    def pair_off(tag, group):
        send, recv, own, lands, token = _pair_start("grad_pair_start_" + tag, [grads[k] for k in group])
        return (group, send, recv, own, lands), token

    def chip_off(tag, started, done):
        group, send, recv, own, lands = started
        own, theirs = _pair_wait("grad_pair_wait_" + tag, send, recv, own, lands, done)
        grads.update(zip(group, own))
        to_chips = [_pair_sum_bf16("pair_sum_" + k, grads[k], b, ids) for k, b in zip(group, theirs)]
        send, recv, parts, lands, token = _chip_start("grad_chip_start_" + tag, to_chips)
        return (group, theirs, send, recv, parts, lands), token

    def reduce_half(tag, state, done):
        group, theirs, send, recv, parts, lands = state
        arrived = _chip_wait("grad_chip_wait_" + tag, send, recv, parts, lands, done)
        halves = [_chip_sum("chip_sum_" + k, grads[k], b, r, ids) for k, b, r in zip(group, theirs, arrived)]
        send, recv, shards, token = _share_start("grad_share_start_" + tag, halves)
        return (group, send, recv, shards), token

    def update(tag, shared, done):
        group, send, recv, shards = shared
        reduced = dict(zip(group, _share_wait("grad_share_wait_" + tag, send, recv, shards, done)))
        last = None
        for k in group:
            if k == "w_in":
                gk = _unpad_proj_cols(reduced[k])
                d, mo, vo = _adamw("adamw_" + k, w_in_rows[0], gk, w_in_rows[1], w_in_rows[2])
            else:
                d, mo, vo, gk = _adamw("adamw_" + k, given[k][0], reduced[k], given["m_" + k][0], given["v_" + k][0], emit_grad=True)
            res[k] = tuple(t[None] for t in (gk, d, mo, vo))
            last = vo
        return last

    dg2, du2 = _ffn_bwd_act("ffn2", dyb3, wd2, fg2, fu2, N_CHIPS)
    grads["ffn2_down"] = _ffn_bwd_down("ffn2", a2, dyb3, N_CHIPS).reshape(N_CHIPS, -1, D)
    grads["ffn2_gate"], grads["ffn2_up"] = _ffn_bwd_gate_up("ffn2", h3, dg2, du2, N_CHIPS)
    started, token = pair_off("a", ["ffn2_gate", "ffn2_up", "ffn2_down"])
    dh3 = _ffn_bwd_x("ffn2", dg2, du2, wg2, wu2, token)
    state_a, token = chip_off("a", started, dh3)
    dx2, dx2b, small["ffn2_norm"] = _rms_bwd("ffn2_norm_bwd", dh3, x2, ffn2_norm + token[0, 0], dx3, 1.0)
    dmix = _mm2d("mix_out_dx", dx2b, wo, "nt", BF16, tk=2048, n_outer=True)
    grads["w_out"] = _mm2d("mix_out_dw", mixed, dx2b, "tn", F32, tk=T, n_outer=True, resident=True).reshape(N_CHIPS, -1, D)
    delta_row = _fox_delta(dmix, out_a_f32)[:, :FOX_HEADS].T.reshape(FOX_HEADS, 1, T)
    dfq, dfk, dfv, dck, dcq = _fox_bwd(qkv, k_t, cq_row, ck_rep, delta_row, lse_a, dmix)
    dsq, dsk, dsv, dsink = _swa_bwd(qkv, slopes, sinks, out_b, lse_b, dmix)
    dmq, dmk, dmv = _mem_bwd(qkv, mk, mv, out_c, lse_c, dmix)
    small["swa_sinks"] = dsink[:, :SWA_GROUP, 0].reshape(1, SWA_HEADS)
    dcum = jnp.pad(dcq.reshape(FOX_HEADS, T).T, ((0, 0), (0, HEAD_DIM - FOX_HEADS)))
    dlogf = _cumsum_rows("forget_cumsum_bwd", [dcum], True, columns=dck)
    dproj, dgains = _prep_bwd(proj, gains, dfq, dfk, dfv, dsq, dsk, dsv, dmq, dlogf)
    for row, k in enumerate(["fox_q_gain", "fox_k_gain", "swa_q_gain", "swa_k_gain", "mem_q_gain"]):
        small[k] = dgains[row:row + 1, :]
    small["forget_bias"] = dgains[5:6, :FOX_HEADS]
    grads["w_in"] = _mm2d("proj_in_dw", h2, dproj, "tn", F32, tn=1408, tk=T, n_outer=True, resident=True).reshape(N_CHIPS, -1, PROJ_W)
    dmk_raw, small["mem_k_gain"] = _head_norm_rows_bwd(mk_raw, mem_k_gain, dmk)
    dmvb = dmv.astype(BF16)
    grads["w_mem_k"] = _mm2d("mem_k_dw", mn, dmk_raw, "tn", F32).reshape(N_CHIPS, -1, MEM_HEADS * HEAD_DIM)
    grads["w_mem_v"] = _mm2d("mem_v_dw", mn, dmvb, "tn", F32).reshape(N_CHIPS, -1, MEM_HEADS * HEAD_DIM)
    dmn = _mm2d("mem_k_dx", dmk_raw, wmk, "nt", F32)
    dmn = _mm2d("mem_v_dx", dmvb, wmv, "nt", F32, extras=[dmn], epilogue=lambda accs, ex: [ex[0] + accs[0]])
    _, _, small["mem_norm"] = _rms_bwd("mem_norm_bwd", dmn, memin, mem_norm, jnp.zeros_like(memin), 1.0)
    started, token = pair_off("b", ["w_out", "w_in", "w_mem_k", "w_mem_v"])
    dh2 = _mm2d("proj_in_dx", dproj, win, "nt", F32, tm=512, tn=1024, tk=PROJ_W, n_outer=True, resident=True, after=token)
    state_b, token = chip_off("b", started, dh2)
    dx1, dyb1, small["mix_norm"] = _rms_bwd("mix_norm_bwd", dh2, x1, mix_norm + token[0, 0], dx2, 0.5)
    grads["ffn1_down"] = _ffn_bwd_down("ffn1", a1, dyb1, N_CHIPS).reshape(N_CHIPS, -1, D)
    started, token = pair_off("c", ["ffn1_down"])
    dg1, du1 = _ffn_bwd_act("ffn1", dyb1, wd1, fg1, fu1, N_CHIPS, after=token)
    state_c, token = chip_off("c", started, dg1)
    grads["ffn1_gate"], grads["ffn1_up"] = _ffn_bwd_gate_up("ffn1", h1, dg1, du1, N_CHIPS, after=token)
    started, token = pair_off("d", ["ffn1_gate", "ffn1_up"])
    dh1 = _ffn_bwd_x("ffn1", dg1, du1, wg1, wu1, token)
    state_d, token = chip_off("d", started, dh1)
    grad_x, _, small["ffn1_norm"] = _rms_bwd("ffn1_norm_bwd", dh1, xin, ffn1_norm + token[0, 0], dx1, 1.0)

    s_send, s_recv, s_buf, s_land, token = _small_start(_pack_small(small))

    shared_a, token = reduce_half("a", state_a, token)
    shared_b, token = reduce_half("b", state_b, token)
    done = update("a", shared_a, token)
    shared_c, token = reduce_half("c", state_c, done)
    done = update("b", shared_b, token)
    shared_d, token = reduce_half("d", state_d, done)
    done = update("c", shared_c, token)
    done = update("d", shared_d, done)

    shapes = {k: given[k].shape for k in SMALL}
    shapes["loss"] = ()
    s_buf, s_land = _small_wait(s_send, s_recv, s_buf, s_land, done)
    red_small = _unpack_small(_small_sum(s_buf, s_land), shapes)
    loss = red_small["loss"]
    zero = {"loss": jnp.zeros((), F32)}
    packed = [_pack_small({**zero, **{k: src[k] for k in SMALL}}) for src in (
        {k: given[k] for k in SMALL}, red_small, {k: given["m_" + k] for k in SMALL}, {k: given["v_" + k] for k in SMALL})]
    d_s, m_s, v_s = (_unpack_small(t, shapes) for t in _adamw("adamw_small", *packed))
    for k in SMALL:
        res[k] = (red_small[k], d_s[k], m_s[k], v_s[k])

    outs = [loss, grad_x.reshape(1, T, D)]
    for part in range(4):
        outs += [res[k][part] for k in WEIGHTS]
    return tuple(outs)
```

```python
import functools

import numpy as np
import jax
import jax.numpy as jnp
from jax import lax
from jax.experimental import pallas as pl
from jax.experimental.pallas import tpu as pltpu

F32 = jnp.float32
BF16 = jnp.bfloat16
MESH = pl.DeviceIdType.MESH

HEAD_DIM = 128
FOX_HEADS = 6
SWA_HEADS = 6
SWA_KV_HEADS = 2
SWA_GROUP = SWA_HEADS // SWA_KV_HEADS
MEM_HEADS = 4
WINDOW = 128
EPS = 1e-6
NEG_INF = -1e30
SCALE = HEAD_DIM ** -0.5

C_FQ = 0
C_FK = C_FQ + FOX_HEADS * HEAD_DIM
C_FV = C_FK + FOX_HEADS * HEAD_DIM
C_SQ = C_FV + FOX_HEADS * HEAD_DIM
C_SK = C_SQ + SWA_HEADS * HEAD_DIM
C_SV = C_SK + SWA_KV_HEADS * HEAD_DIM
C_MQ = C_SV + SWA_KV_HEADS * HEAD_DIM
C_FL = C_MQ + MEM_HEADS * HEAD_DIM
PROJ_W = C_FL + HEAD_DIM
FOX_W = FOX_HEADS * HEAD_DIM
REF_GROUPS = [
    (0, FOX_W, C_FQ), (FOX_W, FOX_W, C_FK), (2 * FOX_W, FOX_W, C_FV), (3 * FOX_W, FOX_HEADS, C_FL),
    (3 * FOX_W + FOX_HEADS, SWA_HEADS * HEAD_DIM, C_SQ),
    (3 * FOX_W + FOX_HEADS + SWA_HEADS * HEAD_DIM, SWA_KV_HEADS * HEAD_DIM, C_SK),
    (3 * FOX_W + FOX_HEADS + (SWA_HEADS + SWA_KV_HEADS) * HEAD_DIM, SWA_KV_HEADS * HEAD_DIM, C_SV),
    (3 * FOX_W + FOX_HEADS + (SWA_HEADS + 2 * SWA_KV_HEADS) * HEAD_DIM, MEM_HEADS * HEAD_DIM, C_MQ),
]

ADAM_LR = 0.001
ADAM_B1 = 0.9
ADAM_B2 = 0.999
ADAM_EPS = 1e-08
ADAM_WD = 0.01
ADAM_STEP = 10

V7X_VMEM_LIMIT = 56 * 1024 * 1024
N_CHIPS = 4
N_DEV = 8


def _tile(n, pref, mult=128):
    t = (min(pref, n) // mult) * mult
    while t >= mult:
        if n % t == 0:
            return t
        t -= mult
    return n


def _params(sem):
    return pltpu.CompilerParams(dimension_semantics=sem, vmem_limit_bytes=V7X_VMEM_LIMIT)


_DIMS = {"nn": (((1,), (0,)), ((), ())), "nt": (((1,), (1,)), ((), ())), "tn": (((0,), (0,)), ((), ()))}


def _dot(a, b, mode):
    return lax.dot_general(a, b, _DIMS[mode], preferred_element_type=F32)


def _mm(name, grid, pairs, acc_of, acc_shapes, extras, outs, epilogue, after=None):
    n_p, n_e, n_o, n_a = len(pairs), len(extras), len(outs), len(acc_shapes)
    n_w = 0 if after is None else 1
    nk = grid[2]
    n_in = sum(1 if a is None else 2 for a, *_ in pairs)

    def body(*refs):
        ex = refs[n_in:n_in + n_e]
        out = refs[n_in + n_e + n_w:n_in + n_e + n_w + n_o]
        accs = refs[n_in + n_e + n_w + n_o:]
        parts = [None] * n_a
        at = 0
        for p in range(n_p):
            if pairs[p][0] is None:
                a_ref, b_ref = refs[0], refs[at]
                at += 1
            else:
                a_ref, b_ref = refs[at], refs[at + 1]
                at += 2
            d = _dot(a_ref[...], b_ref[...], pairs[p][4])
            parts[acc_of[p]] = d if parts[acc_of[p]] is None else parts[acc_of[p]] + d

        def finish(vals):
            for o, r in zip(out, epilogue(vals, [e[...] for e in ex])):
                o[...] = r.astype(o.dtype)

        if nk == 1:
            finish(parts)
            return
        k = pl.program_id(2)

        @pl.when(k == 0)
        def _():
            for a, d in zip(accs, parts):
                a[...] = d

        @pl.when((k > 0) & (k < nk - 1))
        def _():
            for a, d in zip(accs, parts):
                a[...] += d

        @pl.when(k == nk - 1)
        def _():
            finish([a[...] + d for a, d in zip(accs, parts)])

    in_specs, args = [], []
    for a, a_spec, b, b_spec, _ in pairs:
        if a is not None:
            in_specs.append(a_spec)
            args.append(a)
        in_specs.append(b_spec)
        args.append(b)
    for e, e_spec in extras:
        in_specs.append(e_spec)
        args.append(e)
    if after is not None:
        in_specs.append(pl.BlockSpec(memory_space=pl.ANY))
        args.append(after)
    res = pl.pallas_call(
        body, name=name, grid=grid, in_specs=in_specs,
        out_specs=[s for _, s in outs], out_shape=[o for o, _ in outs],
        scratch_shapes=[pltpu.VMEM(s, F32) for s in acc_shapes] if nk > 1 else [],
        compiler_params=_params(("parallel", "parallel", "arbitrary")),
    )(*args)
    return res


def _mm2d(name, a, b, mode, out_dtype, tm=512, tn=1024, tk=1024, extras=(), epilogue=None, n_out=1, after=None, n_outer=False,
          resident=False, rows=(), tile_stat=False):
    if mode == "nn":
        (M, K), N = a.shape, b.shape[1]
    elif mode == "nt":
        (M, K), N = a.shape, b.shape[0]
    else:
        (K, M), N = a.shape, b.shape[1]
    tm, tn, tk = _tile(M, tm), _tile(N, tn), _tile(K, tk)
    assert not resident or tk == K

    def spec(shape, index, single=False):
        mode_kw = {"pipeline_mode": pl.Buffered(1)} if single else {}
        if n_outer:
            return pl.BlockSpec(shape, lambda j, i, k: index(i, j, k), **mode_kw)
        return pl.BlockSpec(shape, index, **mode_kw)

    single_a, single_b = resident and not n_outer, resident and n_outer
    a_spec = spec((tk, tm), lambda i, j, k: (k, i), single_a) if mode == "tn" else spec((tm, tk), lambda i, j, k: (i, k), single_a)
    b_spec = spec((tn, tk), lambda i, j, k: (j, k), single_b) if mode == "nt" else spec((tk, tn), lambda i, j, k: (k, j), single_b)
    mn = spec((tm, tn), lambda i, j, k: (i, j))
    if epilogue is None:
        epilogue = lambda accs, ex: [accs[0]]
    if not isinstance(out_dtype, (list, tuple)):
        out_dtype = [out_dtype] * n_out
    grid = (N // tn, M // tm, K // tk) if n_outer else (M // tm, N // tn, K // tk)
    outs = [(jax.ShapeDtypeStruct((M, N), d), mn) for d in out_dtype]
    if tile_stat:
        assert tn == N
        outs.append((jax.ShapeDtypeStruct((8 * (M // tm), 128), F32), spec((8, 128), lambda i, j, k: (i, 0))))
    res = _mm(name, grid, [(a, a_spec, b, b_spec, mode)], [0], [(tm, tn)],
              [(e, mn) for e in extras] + [(r, spec((1, tn), lambda i, j, k: (0, j))) for r in rows], outs, epilogue, after=after)
    return res[0] if len(res) == 1 else res


def _sigmoid(x):
    return 1.0 / (1.0 + jnp.exp(-x))


def _sigmoid_fast(x):
    return pl.reciprocal(1.0 + jnp.exp(-x), approx=True)


def _ffn_gu(name, h, wg, wu):
    T, D = h.shape
    nf, _, F4 = wg.shape
    tm, tk = _tile(T, 512), _tile(D, 2048)
    a_spec = pl.BlockSpec((tm, tk), lambda j, i, k: (i, k))
    b_spec = pl.BlockSpec((None, tk, F4), lambda j, i, k: (j, k, 0))
    o_spec = pl.BlockSpec((tm, F4), lambda j, i, k: (i, j))

    def epilogue(accs, ex):
        g, u = accs
        s = _sigmoid_fast(g)
        gs = g * s
        return [(s + s * (g - gs)) * u, gs, gs * u]

    sds = jax.ShapeDtypeStruct((T, nf * F4), BF16)
    return _mm(name, (nf, T // tm, D // tk), [(h, a_spec, wg, b_spec, "nn"), (None, None, wu, b_spec, "nn")], [0, 1],
               [(tm, F4), (tm, F4)], [], [(sds, o_spec)] * 3, epilogue)


def _ffn_gu_blocks(name, h, wg, wu, blocks, n_total, prev=None):
    T, D = h.shape
    F4 = wg.shape[2]
    tm = _tile(T, 512)

    def body(tab, h_ref, wg_ref, wu_ref, *rest):
        out = rest[-3:]
        hv = h_ref[...]
        g, u = _dot(hv, wg_ref[...], "nn"), _dot(hv, wu_ref[...], "nn")
        s = _sigmoid_fast(g)
        gs = g * s
        out[0][...] = ((s + s * (g - gs)) * u).astype(BF16)
        out[1][...] = gs.astype(BF16)
        out[2][...] = (gs * u).astype(BF16)

    w_spec = pl.BlockSpec((None, D, F4), lambda j, i, tab: (tab[0, j], 0, 0))
    o_spec = pl.BlockSpec((tm, F4), lambda j, i, tab: (i, tab[1, j]))
    filled = [] if prev is None else list(prev)
    sds = jax.ShapeDtypeStruct((T, n_total * F4), BF16)
    return pl.pallas_call(
        body, name=name,
        grid_spec=pltpu.PrefetchScalarGridSpec(
            num_scalar_prefetch=1, grid=(blocks.shape[1], T // tm),
            in_specs=[pl.BlockSpec((tm, D), lambda j, i, tab: (i, 0)), w_spec, w_spec] + [pl.BlockSpec(memory_space=pl.ANY)] * len(filled),
            out_specs=[o_spec] * 3),
        out_shape=[sds] * 3, input_output_aliases={4 + t: t for t in range(len(filled))},
        compiler_params=_params(("parallel", "parallel")),
    )(blocks, h, wg, wu, *filled)


def _rms_rows(x, gain):
    return x * lax.rsqrt(jnp.mean(x * x, axis=-1, keepdims=True) + EPS) * gain


def _residual_norm(name, a, w, xres, scale, gain, tm):
    def epilogue(accs, ex):
        y = ex[0] + scale * accs[0]
        return [y, _rms_rows(y, ex[1])]

    return _mm2d(name, a, w, "nn", [F32, BF16], tm=tm, tn=w.shape[1], tk=w.shape[0], n_outer=True, resident=True, extras=[xres],
                 rows=[gain], epilogue=epilogue)


def _ffn_down_loss(name, a, wd, xres, target):
    D = wd.shape[1]

    def epilogue(accs, ex):
        e = ex[0] + 0.5 * accs[0] - ex[1]
        d = e * (1.0 / D)
        return [d, 0.5 * d, jnp.zeros((8, 128), F32) + (0.5 / D) * jnp.sum(e * e)]

    return _mm2d(name, a, wd, "nn", [F32, BF16], tm=256, tn=D, tk=wd.shape[0], n_outer=True, resident=True, extras=[xres, target],
                 tile_stat=True, epilogue=epilogue)


def _ffn_bwd_down(tag, a, dyb, nf, after=None):
    return _mm2d(tag + "_dwd", a, dyb, "tn", F32, tm=a.shape[1] // nf, tn=1024, tk=a.shape[0], resident=True, after=after)


def _ffn_bwd_act(tag, dyb, wd, da_dg, da_du, nf, after=None):
    def act_bwd(accs, ex):
        return [accs[0] * ex[0].astype(F32), accs[0] * ex[1].astype(F32)]

    return _mm2d(tag + "_da", dyb, wd, "nt", BF16, tn=wd.shape[0] // nf, tk=2048, extras=[da_dg, da_du], epilogue=act_bwd, n_out=2,
                 n_outer=True, after=after)


def _ffn_bwd_gate_up(tag, h, dg, du, nf, after=None):
    T, D = h.shape
    F4 = dg.shape[1] // nf
    tm = _tile(D, 512)
    h_spec = pl.BlockSpec((T, tm), lambda j, i, k: (0, i))
    d_spec = pl.BlockSpec((T, F4), lambda j, i, k: (0, j), pipeline_mode=pl.Buffered(1))
    w_spec = pl.BlockSpec((None, tm, F4), lambda j, i, k: (j, i, 0))
    sds = jax.ShapeDtypeStruct((nf, D, F4), F32)
    return _mm(tag + "_dwgu", (nf, D // tm, 1), [(h, h_spec, dg, d_spec, "tn"), (None, None, du, d_spec, "tn")],
               [0, 1], [(tm, F4), (tm, F4)], [], [(sds, w_spec)] * 2, lambda accs, ex: accs, after=after)


def _ffn_bwd_x(tag, dg, du, wg, wu, after):
    T = dg.shape[0]
    nf, D, F4 = wg.shape
    tm, tn = _tile(T, 256), _tile(D, 1024)

    def body(dg_ref, du_ref, wg_ref, wu_ref, after_ref, o_ref):
        acc = None
        for j in range(nf):
            cols = slice(j * F4, (j + 1) * F4)
            part = _dot(dg_ref[:, cols], wg_ref[j], "nt") + _dot(du_ref[:, cols], wu_ref[j], "nt")
            acc = part if acc is None else acc + part
        o_ref[...] = acc

    a_spec = pl.BlockSpec((tm, nf * F4), lambda n, i: (i, 0))
    b_spec = pl.BlockSpec((nf, tn, F4), lambda n, i: (0, n, 0), pipeline_mode=pl.Buffered(1))
    return pl.pallas_call(
        body, name=tag + "_dh", grid=(D // tn, T // tm),
        in_specs=[a_spec, a_spec, b_spec, b_spec, pl.BlockSpec(memory_space=pl.ANY)],
        out_specs=pl.BlockSpec((tm, tn), lambda n, i: (i, n)), out_shape=jax.ShapeDtypeStruct((T, D), F32),
        compiler_params=_params(("parallel", "parallel")),
    )(dg, du, wg, wu, after)


def _rms_fwd(name, x, gain):
    R, D = x.shape
    tr = _tile(R, 256, 8)

    def body(x_ref, g_ref, o_ref):
        xv = x_ref[...]
        r = lax.rsqrt(jnp.mean(xv * xv, axis=-1, keepdims=True) + EPS)
        o_ref[...] = (xv * r * g_ref[...]).astype(BF16)

    return pl.pallas_call(
        body, name=name, grid=(R // tr,),
        in_specs=[pl.BlockSpec((tr, D), lambda i: (i, 0)), pl.BlockSpec((1, D), lambda i: (0, 0))],
        out_specs=pl.BlockSpec((tr, D), lambda i: (i, 0)), out_shape=jax.ShapeDtypeStruct((R, D), BF16),
        compiler_params=_params(("parallel",)),
    )(x, gain)


def _rms_bwd(name, dh, x, gain, dres, bscale):
    R, D = x.shape
    tr = _tile(R, 256, 8)

    def body(dh_ref, x_ref, g_ref, dres_ref, dx_ref, dxb_ref, dg_ref):
        xv, dy = x_ref[...], dh_ref[...]
        r = lax.rsqrt(jnp.mean(xv * xv, axis=-1, keepdims=True) + EPS)
        xn = xv * r
        uu = dy * g_ref[...]
        dx = dres_ref[...] + r * (uu - xn * jnp.mean(xn * uu, axis=-1, keepdims=True))
        dx_ref[...] = dx
        dxb_ref[...] = (bscale * dx).astype(BF16)
        part = jnp.sum(dy * xn, axis=0, keepdims=True)

        @pl.when(pl.program_id(0) == 0)
        def _():
            dg_ref[...] = part

        @pl.when(pl.program_id(0) > 0)
        def _():
            dg_ref[...] += part

    row = pl.BlockSpec((tr, D), lambda i: (i, 0))
    vec = pl.BlockSpec((1, D), lambda i: (0, 0))
    return pl.pallas_call(
        body, name=name, grid=(R // tr,), in_specs=[row, row, vec, row], out_specs=[row, row, vec],
        out_shape=[jax.ShapeDtypeStruct((R, D), F32), jax.ShapeDtypeStruct((R, D), BF16), jax.ShapeDtypeStruct((1, D), F32)],
        compiler_params=_params(("arbitrary",)),
    )(dh, x, gain, dres)


def _head_norm(xs, g):
    r = lax.rsqrt(jnp.mean(xs * xs, axis=-1, keepdims=True) + EPS)
    return xs * r * g


def _head_norm_bwd(xs, g, dy):
    r = lax.rsqrt(jnp.mean(xs * xs, axis=-1, keepdims=True) + EPS)
    xn = xs * r
    uu = dy * g
    return r * (uu - xn * jnp.mean(xn * uu, axis=-1, keepdims=True)), jnp.sum(dy * xn, axis=0, keepdims=True)


NORMED = [(C_FQ, FOX_HEADS, 0), (C_FK, FOX_HEADS, 1), (C_SQ, SWA_HEADS, 2), (C_SK, SWA_KV_HEADS, 3), (C_MQ, MEM_HEADS, 4)]
PLAIN = [(C_FV, FOX_HEADS), (C_SV, SWA_KV_HEADS)]


def _prep_fwd(proj, gains):
    T = proj.shape[0]
    tr = _tile(T, 256, 128)

    def body(p_ref, g_ref, o_ref, lf_ref, kt_ref, vt_ref):
        for start, heads, row in NORMED:
            gn = g_ref[row:row + 1, :]
            for hh in range(heads):
                sl = slice(start + hh * HEAD_DIM, start + (hh + 1) * HEAD_DIM)
                y = _head_norm(p_ref[:, sl], gn)
                o_ref[:, sl] = y.astype(BF16)
                if start == C_FK:
                    kt_ref[hh * HEAD_DIM:(hh + 1) * HEAD_DIM, :] = y.T.astype(BF16)
        for start, heads in PLAIN:
            sl = slice(start, start + heads * HEAD_DIM)
            o_ref[:, sl] = p_ref[:, sl].astype(BF16)
        for hh in range(FOX_HEADS):
            sl = slice(C_FV + hh * HEAD_DIM, C_FV + (hh + 1) * HEAD_DIM)
            vt_ref[hh * HEAD_DIM:(hh + 1) * HEAD_DIM, :] = p_ref[:, sl].T.astype(BF16)
        zb = p_ref[:, C_FL:C_FL + HEAD_DIM] + g_ref[5:6, :]
        o_ref[:, C_FL:C_FL + HEAD_DIM] = jnp.zeros((tr, HEAD_DIM), BF16)
        lf_ref[...] = jnp.minimum(zb, 0.0) - jnp.log(1.0 + jnp.exp(-jnp.abs(zb)))

    return pl.pallas_call(
        body, name="prep_fwd", grid=(T // tr,),
        in_specs=[pl.BlockSpec((tr, PROJ_W), lambda i: (i, 0)), pl.BlockSpec((8, 128), lambda i: (0, 0))],
        out_specs=[pl.BlockSpec((tr, PROJ_W), lambda i: (i, 0)), pl.BlockSpec((tr, HEAD_DIM), lambda i: (i, 0)),
                   pl.BlockSpec((FOX_W, tr), lambda i: (0, i)), pl.BlockSpec((FOX_W, tr), lambda i: (0, i))],
        out_shape=[jax.ShapeDtypeStruct((T, PROJ_W), BF16), jax.ShapeDtypeStruct((T, HEAD_DIM), F32),
                   jax.ShapeDtypeStruct((FOX_W, T), BF16), jax.ShapeDtypeStruct((FOX_W, T), BF16)],
        compiler_params=_params(("parallel",)),
    )(proj, gains)


def _prep_bwd(proj, gains, dfq, dfk, dfv, dsq, dsk, dsv, dmq, dlogf):
    T = proj.shape[0]
    tr = _tile(T, 256, 8)
    d_normed = {C_FQ: 0, C_FK: 1, C_SQ: 3, C_SK: 4, C_MQ: 6}
    d_plain = {C_FV: 2, C_SV: 5}

    def body(p_ref, g_ref, *rest):
        d_refs, dlf_ref, o_ref, dg_ref = rest[:7], rest[7], rest[8], rest[9]
        rows = []
        for start, heads, row in NORMED:
            gn = g_ref[row:row + 1, :]
            d_ref = d_refs[d_normed[start]]
            tot = jnp.zeros((1, HEAD_DIM), F32)
            for hh in range(heads):
                sl = slice(start + hh * HEAD_DIM, start + (hh + 1) * HEAD_DIM)
                dx, dgn = _head_norm_bwd(p_ref[:, sl], gn, d_ref[:, hh * HEAD_DIM:(hh + 1) * HEAD_DIM])
                o_ref[:, sl] = dx.astype(BF16)
                tot = tot + dgn
            rows.append(tot)
        for start, heads in PLAIN:
            o_ref[:, start:start + heads * HEAD_DIM] = d_refs[d_plain[start]][...].astype(BF16)
        zb = p_ref[:, C_FL:C_FL + HEAD_DIM] + g_ref[5:6, :]
        lane = lax.broadcasted_iota(jnp.int32, (tr, HEAD_DIM), 1)
        dz = jnp.where(lane < FOX_HEADS, dlf_ref[...] * (1.0 - _sigmoid(zb)), 0.0)
        o_ref[:, C_FL:C_FL + HEAD_DIM] = dz.astype(BF16)
        rows.append(jnp.sum(dz, axis=0, keepdims=True))
        part = jnp.concatenate(rows + [jnp.zeros((2, HEAD_DIM), F32)], axis=0)

        @pl.when(pl.program_id(0) == 0)
        def _():
            dg_ref[...] = part

        @pl.when(pl.program_id(0) > 0)
        def _():
            dg_ref[...] += part

    def rows_of(w):
        return pl.BlockSpec((tr, w), lambda i: (i, 0))

    small = pl.BlockSpec((8, 128), lambda i: (0, 0))
    ds = [dfq, dfk, dfv, dsq, dsk, dsv, dmq]
    return pl.pallas_call(
        body, name="prep_bwd", grid=(T // tr,),
        in_specs=[rows_of(PROJ_W), small] + [rows_of(d.shape[1]) for d in ds] + [rows_of(HEAD_DIM)],
        out_specs=[rows_of(PROJ_W), small],
        out_shape=[jax.ShapeDtypeStruct((T, PROJ_W), BF16), jax.ShapeDtypeStruct((8, 128), F32)],
        compiler_params=_params(("arbitrary",)),
    )(proj, gains, *ds, dlogf)


def _head_norm_rows(x, gain):
    R, W = x.shape

    def body(x_ref, g_ref, o_ref):
        for hh in range(W // HEAD_DIM):
            sl = slice(hh * HEAD_DIM, (hh + 1) * HEAD_DIM)
            o_ref[:, sl] = _head_norm(x_ref[:, sl], g_ref[...]).astype(BF16)

    return pl.pallas_call(body, name="mem_k_norm", out_shape=jax.ShapeDtypeStruct((R, W), BF16))(x, gain)


def _head_norm_rows_bwd(x, gain, dy):
    R, W = x.shape

    def body(x_ref, g_ref, dy_ref, dx_ref, dg_ref):
        tot = jnp.zeros((1, HEAD_DIM), F32)
        for hh in range(W // HEAD_DIM):
            sl = slice(hh * HEAD_DIM, (hh + 1) * HEAD_DIM)
            dx, dgn = _head_norm_bwd(x_ref[:, sl], g_ref[...], dy_ref[:, sl])
            dx_ref[:, sl] = dx.astype(BF16)
            tot = tot + dgn
        dg_ref[...] = tot

    return pl.pallas_call(
        body, name="mem_k_norm_bwd",
        out_shape=[jax.ShapeDtypeStruct((R, W), BF16), jax.ShapeDtypeStruct((1, HEAD_DIM), F32)])(x, gain, dy)


def _cumsum_rows(name, xs, reverse, columns=None):
    T, W = xs[0].shape
    tb = _tile(T, 512, 8)
    nb = T // tb
    n_in = len(xs) + (0 if columns is None else 1)

    def body(*refs):
        o_ref, carry = refs[n_in], refs[n_in + 1]

        @pl.when(pl.program_id(0) == 0)
        def _():
            carry[...] = jnp.zeros_like(carry)

        xv = refs[0][...]
        for x_ref in refs[1:len(xs)]:
            xv = xv + x_ref[...]
        if columns is not None:
            lane = lax.broadcasted_iota(jnp.int32, (tb, W), 1)
            for hh in range(columns.shape[0]):
                xv = xv + jnp.where(lane == hh, refs[len(xs)][hh], 0.0)
        r = lax.broadcasted_iota(jnp.int32, (tb, tb), 0)
        cc = lax.broadcasted_iota(jnp.int32, (tb, tb), 1)
        tri = jnp.where((cc >= r) if reverse else (cc <= r), 1.0, 0.0).astype(F32)
        o_ref[...] = jnp.dot(tri, xv, precision=lax.Precision.HIGHEST, preferred_element_type=F32) + carry[...]
        carry[...] += jnp.sum(xv, axis=0, keepdims=True)

    idx = (lambda i: (nb - 1 - i, 0)) if reverse else (lambda i: (i, 0))
    in_specs = [pl.BlockSpec((tb, W), idx)] * len(xs)
    if columns is not None:
        in_specs.append(pl.BlockSpec((columns.shape[0], tb, 1), lambda i: (0, idx(i)[0], 0)))
    return pl.pallas_call(
        body, name=name, grid=(nb,), in_specs=in_specs, out_specs=pl.BlockSpec((tb, W), idx),
        out_shape=jax.ShapeDtypeStruct((T, W), F32), scratch_shapes=[pltpu.VMEM((1, W), F32)],
        compiler_params=_params(("arbitrary",)),
    )(*xs, *([] if columns is None else [columns]))


def _triangle(nq, by_column):
    if by_column:
        blocks = [(i, j) for j in range(nq) for i in range(j, nq)]
    else:
        blocks = [(i, j) for i in range(nq) for j in range(i + 1)]
    return jnp.asarray(np.array(blocks, np.int32).T)


def _fox_scores_t(k, q, cq_row, ck_rep, on_diagonal):
    n = q.shape[0]
    s = _dot(k, q, "nt") * SCALE + (cq_row - jnp.tile(ck_rep, (1, n // HEAD_DIM)))
    if on_diagonal:
        s = jnp.where(lax.broadcasted_iota(jnp.int32, (n, n), 0) <= lax.broadcasted_iota(jnp.int32, (n, n), 1), s, NEG_INF)
    return s


def _fox_fwd(qkv, v_t, cq_row, ck_rep):
    T = qkv.shape[0]
    tq = _tile(T, 1024)
    nq = T // tq
    steps = nq * (nq + 1) // 2
    HQ, HK = C_FQ // HEAD_DIM, C_FK // HEAD_DIM

    def body(tab, q_ref, k_ref, vt_ref, cq_ref, ck_ref, o_ref, of_ref, lse_ref, m_sc, l_sc, acc_sc):
        i, j = tab[0, pl.program_id(1)], tab[1, pl.program_id(1)]

        @pl.when(j == 0)
        def _():
            m_sc[...] = jnp.full_like(m_sc, NEG_INF)
            l_sc[...] = jnp.zeros_like(l_sc)
            acc_sc[...] = jnp.zeros_like(acc_sc)

        def step(on_diagonal):
            s = _fox_scores_t(k_ref[...], q_ref[...], cq_ref[...], ck_ref[...], on_diagonal)
            m_new = jnp.maximum(m_sc[...], jnp.max(s, axis=0, keepdims=True))
            alpha = jnp.exp(m_sc[...] - m_new)
            p = jnp.exp(s - m_new)
            l_sc[...] = alpha * l_sc[...] + jnp.sum(p, axis=0, keepdims=True)
            acc_sc[...] = alpha * acc_sc[...] + _dot(vt_ref[...], p.astype(BF16), "nn")
            m_sc[...] = m_new

        @pl.when(j < i)
        def _():
            step(False)

        @pl.when(j == i)
        def _():
            step(True)
            o = (acc_sc[...] / l_sc[...]).T
            o_ref[...] = o.astype(BF16)
            of_ref[...] = o
            lse_ref[...] = m_sc[...] + jnp.log(l_sc[...])

    qrow = pl.BlockSpec((None, 1, tq), lambda h, s, tab: (h, 0, tab[0, s]))
    return pl.pallas_call(
        body, name="fox_fwd",
        grid_spec=pltpu.PrefetchScalarGridSpec(
            num_scalar_prefetch=1, grid=(FOX_HEADS, steps),
            in_specs=[pl.BlockSpec((tq, HEAD_DIM), lambda h, s, tab: (tab[0, s], HQ + h)),
                      pl.BlockSpec((tq, HEAD_DIM), lambda h, s, tab: (tab[1, s], HK + h)),
                      pl.BlockSpec((HEAD_DIM, tq), lambda h, s, tab: (h, tab[1, s])), qrow,
                      pl.BlockSpec((None, tq, HEAD_DIM), lambda h, s, tab: (h, tab[1, s], 0))],
            out_specs=[pl.BlockSpec((tq, HEAD_DIM), lambda h, s, tab: (tab[0, s], h)),
                       pl.BlockSpec((tq, HEAD_DIM), lambda h, s, tab: (tab[0, s], h)), qrow],
            scratch_shapes=[pltpu.VMEM((1, tq), F32), pltpu.VMEM((1, tq), F32), pltpu.VMEM((HEAD_DIM, tq), F32)]),
        out_shape=[jax.ShapeDtypeStruct((T, FOX_W), BF16), jax.ShapeDtypeStruct((T, FOX_W), F32),
                   jax.ShapeDtypeStruct((FOX_HEADS, 1, T), F32)],
        compiler_params=_params(("parallel", "arbitrary")),
    )(_triangle(nq, False), qkv, qkv, v_t, cq_row, ck_rep)


def _fox_delta(dmix, out_f32):
    T = out_f32.shape[0]
    tr = _tile(T, 512, 8)

    def body(do_ref, o_ref, d_ref):
        lane = lax.broadcasted_iota(jnp.int32, (tr, HEAD_DIM), 1)
        acc = jnp.zeros((tr, HEAD_DIM), F32)
        for hh in range(FOX_HEADS):
            sl = slice(hh * HEAD_DIM, (hh + 1) * HEAD_DIM)
            d = jnp.sum(do_ref[:, sl].astype(F32) * o_ref[:, sl], axis=-1, keepdims=True)
            acc = jnp.where(lane == hh, d, acc)
        d_ref[...] = acc

    blk = pl.BlockSpec((tr, FOX_W), lambda i: (i, 0))
    return pl.pallas_call(
        body, name="fox_delta", grid=(T // tr,), in_specs=[blk, blk], out_specs=pl.BlockSpec((tr, HEAD_DIM), lambda i: (i, 0)),
        out_shape=jax.ShapeDtypeStruct((T, HEAD_DIM), F32), compiler_params=_params(("parallel",)),
    )(dmix, out_f32)


def _fox_bwd(qkv, k_t, cq_row, ck_rep, delta_row, lse, dmix):
    T = qkv.shape[0]
    tq = _tile(T, 1024)
    nq = T // tq
    steps = nq * (nq + 1) // 2
    HQ, HK, HV = C_FQ // HEAD_DIM, C_FK // HEAD_DIM, C_FV // HEAD_DIM

    def body(tab, q_ref, k_ref, kt_ref, v_ref, cq_ref, ck_ref, delta_ref, lse_ref, do_ref,
             dq_ref, dk_ref, dv_ref, dck_ref, dcq_ref, dk_sc, dv_sc, dc_sc, dqt_sc):
        qi, kj = tab[0, pl.program_id(1)], tab[1, pl.program_id(1)]

        @pl.when(qi == kj)
        def _():
            dk_sc[...] = jnp.zeros_like(dk_sc)
            dv_sc[...] = jnp.zeros_like(dv_sc)
            dc_sc[...] = jnp.zeros_like(dc_sc)

        def step(on_diagonal):
            q, k, v, do = q_ref[...], k_ref[...], v_ref[...], do_ref[...]
            p = jnp.exp(_fox_scores_t(k, q, cq_ref[...], ck_ref[...], on_diagonal) - lse_ref[...])
            dp = _dot(v, do, "nt")
            ds = p * (dp - delta_ref[...])
            dsb = ds.astype(BF16)
            dv_sc[...] += _dot(p.astype(BF16), do, "nn")
            dk_sc[...] += _dot(dsb, q, "nn")
            dc_sc[...] += jnp.sum(ds, axis=1, keepdims=True)
            dq_part = _dot(kt_ref[...], dsb, "nn") * SCALE
            dcq_part = jnp.sum(ds, axis=0, keepdims=True)

            @pl.when(kj == 0)
            def _():
                dqt_sc[qi] = dq_part
                dcq_ref[qi] = dcq_part

            @pl.when(kj > 0)
            def _():
                dqt_sc[qi] += dq_part
                dcq_ref[qi] += dcq_part

            if on_diagonal:
                dq_ref[...] = dqt_sc[qi].T

        @pl.when(qi > kj)
        def _():
            step(False)

        @pl.when(qi == kj)
        def _():
            step(True)

        @pl.when(qi == nq - 1)
        def _():
            dk_ref[...] = dk_sc[...] * SCALE
            dv_ref[...] = dv_sc[...]
            dck_ref[...] = -dc_sc[...]

    def rows(base):
        return pl.BlockSpec((tq, HEAD_DIM), lambda h, s, tab: (tab[0, s], base + h))

    def cols(base):
        return pl.BlockSpec((tq, HEAD_DIM), lambda h, s, tab: (tab[1, s], base + h))

    qrow = pl.BlockSpec((None, 1, tq), lambda h, s, tab: (h, 0, tab[0, s]))
    sds = jax.ShapeDtypeStruct((T, FOX_W), F32)
    return pl.pallas_call(
        body, name="fox_bwd",
        grid_spec=pltpu.PrefetchScalarGridSpec(
            num_scalar_prefetch=1, grid=(FOX_HEADS, steps),
            in_specs=[rows(HQ), cols(HK), pl.BlockSpec((HEAD_DIM, tq), lambda h, s, tab: (h, tab[1, s])), cols(HV), qrow,
                      pl.BlockSpec((None, tq, HEAD_DIM), lambda h, s, tab: (h, tab[1, s], 0)), qrow, qrow, rows(0)],
            out_specs=[cols(0), cols(0), cols(0), pl.BlockSpec((None, tq, 1), lambda h, s, tab: (h, tab[1, s], 0)),
                       pl.BlockSpec((None, nq, 1, tq), lambda h, s, tab: (h, 0, 0, 0))],
            scratch_shapes=[pltpu.VMEM((tq, HEAD_DIM), F32), pltpu.VMEM((tq, HEAD_DIM), F32), pltpu.VMEM((tq, 1), F32),
                            pltpu.VMEM((nq, HEAD_DIM, tq), F32)]),
        out_shape=[sds, sds, sds, jax.ShapeDtypeStruct((FOX_HEADS, T, 1), F32), jax.ShapeDtypeStruct((FOX_HEADS, nq, 1, tq), F32)],
        compiler_params=_params(("parallel", "arbitrary")),
    )(_triangle(nq, True), qkv, qkv, k_t, qkv, cq_row, ck_rep, delta_row, lse, dmix)


GW = SWA_GROUP * HEAD_DIM
GR = SWA_GROUP * WINDOW


def _swa_bias_table():
    slopes = 2.0 ** (-8.0 * np.arange(1, SWA_HEADS + 1) / SWA_HEADS)
    row = np.arange(GR)
    dist = WINDOW + (row % WINDOW)[:, None] - np.arange(2 * WINDOW)[None, :]
    slope = slopes.reshape(SWA_KV_HEADS, SWA_GROUP)[:, row // WINDOW]
    table = np.where((dist >= 0) & (dist < WINDOW), -slope[:, :, None] * dist[None], NEG_INF)
    return jnp.asarray(table, F32)


def _swa_scores(q_ref, kp_ref, kc_ref, bias_ref, n):
    q = q_ref[...]
    qs = jnp.concatenate([q[:, t * HEAD_DIM:(t + 1) * HEAD_DIM] for t in range(SWA_GROUP)], axis=0)
    kb = jnp.concatenate([kp_ref[...], kc_ref[...]], axis=0)
    jj = lax.broadcasted_iota(jnp.int32, (GR, 2 * WINDOW), 1)
    s = _dot(qs, kb, "nt") * SCALE + bias_ref[...]
    return qs, kb, jnp.where((n > 0) | (jj >= WINDOW), s, NEG_INF)


def _swa_specs():
    HQ, HK, HV = C_SQ // GW, C_SK // HEAD_DIM, C_SV // HEAD_DIM
    q_spec = pl.BlockSpec((WINDOW, GW), lambda g, n: (n, HQ + g))

    def prev(base):
        return pl.BlockSpec((WINDOW, HEAD_DIM), lambda g, n: (jnp.maximum(n - 1, 0), base + g))

    def cur(base):
        return pl.BlockSpec((WINDOW, HEAD_DIM), lambda g, n: (n, base + g))

    col = pl.BlockSpec((None, GR, 1), lambda g, n: (g, 0, 0))
    table = pl.BlockSpec((None, GR, 2 * WINDOW), lambda g, n: (g, 0, 0))
    return q_spec, prev(HK), cur(HK), prev(HV), cur(HV), col, table


def _swa_fwd(qkv, slopes, sinks):
    T = qkv.shape[0]
    nb = T // WINDOW
    assert C_SQ % GW == 0

    def body(q_ref, kp_ref, kc_ref, vp_ref, vc_ref, slope_ref, sink_ref, o_ref, lse_ref):
        n = pl.program_id(1)
        _, _, s = _swa_scores(q_ref, kp_ref, kc_ref, slope_ref, n)
        m = jnp.maximum(jnp.max(s, axis=-1, keepdims=True), sink_ref[...])
        p = jnp.exp(s - m)
        l = jnp.sum(p, axis=-1, keepdims=True) + jnp.exp(sink_ref[...] - m)
        vb = jnp.concatenate([vp_ref[...], vc_ref[...]], axis=0)
        o = _dot(p.astype(BF16), vb, "nn") / l
        for t in range(SWA_GROUP):
            o_ref[:, t * HEAD_DIM:(t + 1) * HEAD_DIM] = o[t * WINDOW:(t + 1) * WINDOW, :].astype(BF16)
        lse_ref[...] = m + jnp.log(l)

    q_spec, kp, kc, vp, vc, col, table = _swa_specs()
    return pl.pallas_call(
        body, name="swa_fwd", grid=(SWA_KV_HEADS, nb), in_specs=[q_spec, kp, kc, vp, vc, table, col],
        out_specs=[pl.BlockSpec((WINDOW, GW), lambda g, n: (n, g)), pl.BlockSpec((None, None, GR, 1), lambda g, n: (g, n, 0, 0))],
        out_shape=[jax.ShapeDtypeStruct((T, SWA_HEADS * HEAD_DIM), BF16), jax.ShapeDtypeStruct((SWA_KV_HEADS, nb, GR, 1), F32)],
        compiler_params=_params(("parallel", "arbitrary")),
    )(qkv, qkv, qkv, qkv, qkv, slopes, sinks)


def _swa_bwd(qkv, slopes, sinks, out, lse, dmix):
    T = qkv.shape[0]
    nb = T // WINDOW
    DO = FOX_W // GW
    assert FOX_W % GW == 0

    def body(q_ref, kp_ref, kc_ref, vp_ref, vc_ref, slope_ref, sink_ref, o_ref, lse_ref, do_ref,
             dq_ref, dk_ref, dv_ref, dsink_ref, sink_sc):
        n = pl.program_id(1)

        @pl.when(n == 0)
        def _():
            dk_ref[...] = jnp.zeros_like(dk_ref)
            dv_ref[...] = jnp.zeros_like(dv_ref)
            sink_sc[...] = jnp.zeros_like(sink_sc)

        qs, kb, s = _swa_scores(q_ref, kp_ref, kc_ref, slope_ref, n)
        lse = lse_ref[...]
        p = jnp.exp(s - lse)
        vb = jnp.concatenate([vp_ref[...], vc_ref[...]], axis=0)
        do = jnp.concatenate([do_ref[:, t * HEAD_DIM:(t + 1) * HEAD_DIM] for t in range(SWA_GROUP)], axis=0)
        oo = jnp.concatenate([o_ref[:, t * HEAD_DIM:(t + 1) * HEAD_DIM] for t in range(SWA_GROUP)], axis=0)
        dp = _dot(do, vb, "nt")
        delta = jnp.sum(do.astype(F32) * oo.astype(F32), axis=-1, keepdims=True)
        ds = p * (dp - delta)
        dsb = ds.astype(BF16)
        dq = _dot(dsb, kb, "nn") * SCALE
        for t in range(SWA_GROUP):
            dq_ref[:, t * HEAD_DIM:(t + 1) * HEAD_DIM] = dq[t * WINDOW:(t + 1) * WINDOW, :]
        dkb = _dot(dsb, qs, "tn") * SCALE
        dvb = _dot(p.astype(BF16), do, "tn")
        r_prev = pl.ds(pl.multiple_of(jnp.maximum(n - 1, 0) * WINDOW, WINDOW), WINDOW)
        r_cur = pl.ds(pl.multiple_of(n * WINDOW, WINDOW), WINDOW)
        dk_ref[r_prev, :] += dkb[:WINDOW, :]
        dk_ref[r_cur, :] += dkb[WINDOW:, :]
        dv_ref[r_prev, :] += dvb[:WINDOW, :]
        dv_ref[r_cur, :] += dvb[WINDOW:, :]
        sink_sc[...] -= jnp.exp(sink_ref[...] - lse) * delta

        @pl.when(n == nb - 1)
        def _():
            tot = [jnp.zeros((1, 128), F32) + jnp.sum(sink_sc[t * WINDOW:(t + 1) * WINDOW, :]) for t in range(SWA_GROUP)]
            dsink_ref[...] = jnp.concatenate(tot + [jnp.zeros((8 - SWA_GROUP, 128), F32)], axis=0)

    q_spec, kp, kc, vp, vc, col, table = _swa_specs()
    kv_acc = pl.BlockSpec((T, HEAD_DIM), lambda g, n: (0, g))
    return pl.pallas_call(
        body, name="swa_bwd", grid=(SWA_KV_HEADS, nb),
        in_specs=[q_spec, kp, kc, vp, vc, table, col, pl.BlockSpec((WINDOW, GW), lambda g, n: (n, g)),
                  pl.BlockSpec((None, None, GR, 1), lambda g, n: (g, n, 0, 0)), pl.BlockSpec((WINDOW, GW), lambda g, n: (n, DO + g))],
        out_specs=[pl.BlockSpec((WINDOW, GW), lambda g, n: (n, g)), kv_acc, kv_acc, pl.BlockSpec((None, 8, 128), lambda g, n: (g, 0, 0))],
        out_shape=[jax.ShapeDtypeStruct((T, SWA_HEADS * HEAD_DIM), F32), jax.ShapeDtypeStruct((T, SWA_KV_HEADS * HEAD_DIM), F32),
                   jax.ShapeDtypeStruct((T, SWA_KV_HEADS * HEAD_DIM), F32), jax.ShapeDtypeStruct((SWA_KV_HEADS, 8, 128), F32)],
        scratch_shapes=[pltpu.VMEM((GR, 1), F32)],
        compiler_params=_params(("parallel", "arbitrary")),
    )(qkv, qkv, qkv, qkv, qkv, slopes, sinks, out, lse, dmix)


def _mem_fwd(qkv, mk, mv):
    T, ML = qkv.shape[0], mk.shape[0]
    tq = _tile(T, 1024)
    HQ = C_MQ // HEAD_DIM

    def body(q_ref, k_ref, v_ref, o_ref, lse_ref):
        s = _dot(q_ref[...], k_ref[...], "nt") * SCALE
        m = jnp.max(s, axis=-1, keepdims=True)
        p = jnp.exp(s - m)
        l = jnp.sum(p, axis=-1, keepdims=True)
        o_ref[...] = (_dot(p.astype(BF16), v_ref[...], "nn") / l).astype(BF16)
        lse_ref[...] = m + jnp.log(l)

    kv = pl.BlockSpec((ML, HEAD_DIM), lambda h, i: (0, h))
    return pl.pallas_call(
        body, name="mem_fwd", grid=(MEM_HEADS, T // tq),
        in_specs=[pl.BlockSpec((tq, HEAD_DIM), lambda h, i: (i, HQ + h)), kv, kv],
        out_specs=[pl.BlockSpec((tq, HEAD_DIM), lambda h, i: (i, h)), pl.BlockSpec((None, tq, 1), lambda h, i: (h, i, 0))],
        out_shape=[jax.ShapeDtypeStruct((T, MEM_HEADS * HEAD_DIM), BF16), jax.ShapeDtypeStruct((MEM_HEADS, T, 1), F32)],
        compiler_params=_params(("parallel", "arbitrary")),
    )(qkv, mk, mv)


def _mem_bwd(qkv, mk, mv, out, lse, dmix):
    T, ML = qkv.shape[0], mk.shape[0]
    tq = _tile(T, 1024)
    HQ = C_MQ // HEAD_DIM
    DO = (FOX_W + SWA_HEADS * HEAD_DIM) // HEAD_DIM

    def body(q_ref, k_ref, v_ref, o_ref, lse_ref, do_ref, dq_ref, dk_ref, dv_ref):
        q, k, v, do = q_ref[...], k_ref[...], v_ref[...], do_ref[...]
        p = jnp.exp(_dot(q, k, "nt") * SCALE - lse_ref[...])
        dp = _dot(do, v, "nt")
        delta = jnp.sum(do.astype(F32) * o_ref[...].astype(F32), axis=-1, keepdims=True)
        dsb = (p * (dp - delta)).astype(BF16)
        dq_ref[...] = _dot(dsb, k, "nn") * SCALE
        dk_part = _dot(dsb, q, "tn") * SCALE
        dv_part = _dot(p.astype(BF16), do, "tn")

        @pl.when(pl.program_id(1) == 0)
        def _():
            dk_ref[...] = dk_part
            dv_ref[...] = dv_part

        @pl.when(pl.program_id(1) > 0)
        def _():
            dk_ref[...] += dk_part
            dv_ref[...] += dv_part

    kv = pl.BlockSpec((ML, HEAD_DIM), lambda h, i: (0, h))
    qb = pl.BlockSpec((tq, HEAD_DIM), lambda h, i: (i, h))
    return pl.pallas_call(
        body, name="mem_bwd", grid=(MEM_HEADS, T // tq),
        in_specs=[pl.BlockSpec((tq, HEAD_DIM), lambda h, i: (i, HQ + h)), kv, kv, qb,
                  pl.BlockSpec((None, tq, 1), lambda h, i: (h, i, 0)), pl.BlockSpec((tq, HEAD_DIM), lambda h, i: (i, DO + h))],
        out_specs=[qb, kv, kv],
        out_shape=[jax.ShapeDtypeStruct((T, MEM_HEADS * HEAD_DIM), F32), jax.ShapeDtypeStruct((ML, MEM_HEADS * HEAD_DIM), F32),
                   jax.ShapeDtypeStruct((ML, MEM_HEADS * HEAD_DIM), F32)],
        compiler_params=_params(("parallel", "arbitrary")),
    )(qkv, mk, mv, out, lse, dmix)


HBM = pl.BlockSpec(memory_space=pltpu.HBM)


def _place():
    x, y, c = lax.axis_index("x"), lax.axis_index("y"), lax.axis_index("c")
    chips = [(1 - x, y), (x, 1 - y), (1 - x, 1 - y)]
    return x, y, c, chips


def _remote(src, dst, send_sem, recv_sem, device):
    return pltpu.make_async_remote_copy(src_ref=src, dst_ref=dst, send_sem=send_sem, recv_sem=recv_sem,
                                        device_id=device, device_id_type=MESH)


def _place_ids():
    x, y, c = lax.axis_index("x"), lax.axis_index("y"), lax.axis_index("c")
    order = [2 * x + y, 2 * (1 - x) + y, 2 * x + (1 - y), 2 * (1 - x) + (1 - y)]
    return jnp.stack([2 * x + y, c] + order).astype(jnp.int32)


def _cast_place(name, w, ids, *after, keep_own=False):
    R, C = w.shape
    tr = _tile(R, 256, 16)
    n_out = 2 if keep_own else 1

    def body(ids_ref, w_ref, *rest):
        for o_ref in rest[-n_out:]:
            o_ref[...] = w_ref[...].astype(BF16)

    res = pl.pallas_call(
        body, name=name,
        grid_spec=pltpu.PrefetchScalarGridSpec(
            num_scalar_prefetch=1, grid=(R // tr,),
            in_specs=[pl.BlockSpec((tr, C), lambda i, ids: (i, 0))] + [pl.BlockSpec(memory_space=pl.ANY)] * len(after),
            out_specs=[pl.BlockSpec((None, tr, C), lambda i, ids: (ids[0], i, 0)), pl.BlockSpec((tr, C), lambda i, ids: (i, 0))][:n_out]),
        out_shape=[jax.ShapeDtypeStruct((N_CHIPS, R, C), BF16), jax.ShapeDtypeStruct((R, C), BF16)][:n_out],
        compiler_params=_params(("parallel",)),
    )(ids, w, *after)
    return res if keep_own else res[0]


SEM = pl.BlockSpec(memory_space=pltpu.SEMAPHORE)
EFFECT = pltpu.SideEffectType.DATAFLOW_SIDE_EFFECTING


def _hbm(a):
    return pltpu.with_memory_space_constraint(a, pltpu.HBM)


def _gather_start(name, placed, after, to_sibling=False):
    n = len(placed)

    ns = 3 * n

    def body(*refs):
        send, recv = refs[n + 1:n + 1 + ns], refs[n + 1 + ns:n + 1 + 2 * ns]
        buf = refs[n + 1 + 2 * ns:2 * n + 1 + 2 * ns]
        token = refs[2 * n + 1 + 2 * ns]
        x, y, c, chips = _place()
        me = 2 * x + y
        for a in range(n):
            half = buf[a].shape[1] // 2
            for j, (cx, cy) in enumerate(chips):
                block, peer = (2 * cx + cy, (x, y, 1 - c)) if to_sibling else (me, (cx, cy, c))
                part = buf[a].at[block, pl.ds(c * half, half)]
                _remote(part, part, send[3 * a + j], recv[3 * a + j], peer).start()
        token[...] = jnp.zeros_like(token)

    res = pl.pallas_call(
        body, name=name, in_specs=[HBM] * n + [pl.BlockSpec(memory_space=pl.ANY)],
        out_specs=[SEM] * (2 * ns) + [HBM] * n + [pl.BlockSpec(memory_space=pltpu.VMEM)],
        out_shape=[pltpu.SemaphoreType.DMA(())] * (2 * ns)
        + [pltpu.HBM(s.shape, s.dtype) for s in placed] + [jax.ShapeDtypeStruct((8, 128), F32)],
        input_output_aliases={a: 2 * ns + a for a in range(n)},
        compiler_params=pltpu.CompilerParams(has_side_effects=EFFECT),
    )(*[_hbm(s) for s in placed], after)
    return list(res[:ns]), list(res[ns:2 * ns]), list(res[2 * ns:2 * ns + n]), res[2 * ns + n]


def _gather_wait(name, send, recv, bufs, after, to_sibling=False):
    n = len(bufs)

    ns = 3 * n

    def body(*refs):
        buf = refs[:n]
        send_ref, recv_ref = refs[n:n + ns], refs[n + ns:n + 2 * ns]
        x, y, c, chips = _place()
        ids = [2 * cx + cy for cx, cy in chips]
        for a in range(n):
            half = buf[a].shape[1] // 2
            for j in range(3):
                sent = buf[a].at[ids[j], pl.ds(c * half, half)]
                landed = buf[a].at[ids[j], pl.ds((1 - c) * half, half)] if to_sibling else sent
                cp = _remote(sent, landed, send_ref[3 * a + j], recv_ref[3 * a + j], (x, y, c))
                cp.wait_send()
                cp.wait_recv()

    res = pl.pallas_call(
        body, name=name, in_specs=[HBM] * n + [SEM] * (2 * ns) + [pl.BlockSpec(memory_space=pl.ANY)], out_specs=[HBM] * n,
        out_shape=[pltpu.HBM(s.shape, s.dtype) for s in bufs], input_output_aliases={a: a for a in range(n)},
        compiler_params=pltpu.CompilerParams(has_side_effects=EFFECT),
    )(*bufs, *send, *recv, after)
    return list(res)


def _gather_forward(name, bufs):
    n = len(bufs)

    def body(*refs):
        buf = refs[n:2 * n]
        send, recv = refs[2 * n:]
        x, y, c, chips = _place()
        ids = [2 * cx + cy for cx, cy in chips]
        copies = []
        for a in range(n):
            half = buf[a].shape[1] // 2
            for j in range(3):
                landed = buf[a].at[ids[j], pl.ds(c * half, half)]
                cp = _remote(landed, landed, send.at[a, j], recv.at[a, j], (x, y, 1 - c))
                cp.start()
                copies.append(cp)
        for a in range(n):
            half = buf[a].shape[1] // 2
            for j in range(3):
                landed = buf[a].at[ids[j], pl.ds((1 - c) * half, half)]
                _remote(landed, landed, send.at[a, j], recv.at[a, j], (x, y, c)).wait_recv()
        for cp in copies:
            cp.wait_send()

    return pl.pallas_call(
        body, name=name, in_specs=[HBM] * n, out_specs=[HBM] * n,
        out_shape=[jax.ShapeDtypeStruct(s.shape, s.dtype) for s in bufs], input_output_aliases={a: a for a in range(n)},
        scratch_shapes=[pltpu.SemaphoreType.DMA((n, 3)), pltpu.SemaphoreType.DMA((n, 3))],
    )(*bufs)


def _pair_start(name, grads):
    n = len(grads)
    ns = N_CHIPS * n

    def body(*refs):
        send, recv = refs[2 * n:2 * n + ns], refs[2 * n + ns:2 * n + 2 * ns]
        src = refs[2 * n + 2 * ns:3 * n + 2 * ns]
        land = refs[3 * n + 2 * ns:4 * n + 2 * ns]
        token = refs[4 * n + 2 * ns]
        x, y, c, chips = _place()
        order = [2 * x + y] + [2 * cx + cy for cx, cy in chips]
        for a in range(n):
            half = src[a].shape[1] // 2
            for j in range(N_CHIPS):
                _remote(src[a].at[order[j], pl.ds((1 - c) * half, half)], land[a].at[j],
                        send[N_CHIPS * a + j], recv[N_CHIPS * a + j], (x, y, 1 - c)).start()
        token[...] = jnp.zeros_like(token)

    lands = [jax.ShapeDtypeStruct((N_CHIPS, g.shape[1] // 2, g.shape[2]), g.dtype) for g in grads]
    res = pl.pallas_call(
        body, name=name, in_specs=[HBM] * (2 * n),
        out_specs=[SEM] * (2 * ns) + [HBM] * (2 * n) + [pl.BlockSpec(memory_space=pltpu.VMEM)],
        out_shape=[pltpu.SemaphoreType.DMA(())] * (2 * ns) + [pltpu.HBM(g.shape, g.dtype) for g in grads]
        + [pltpu.HBM(l.shape, l.dtype) for l in lands] + [jax.ShapeDtypeStruct((8, 128), F32)],
        input_output_aliases={a: 2 * ns + a for a in range(2 * n)},
        compiler_params=pltpu.CompilerParams(has_side_effects=EFFECT),
    )(*[_hbm(g) for g in grads], *[_hbm(lax.empty(l.shape, l.dtype)) for l in lands])
    return list(res[:ns]), list(res[ns:2 * ns]), list(res[2 * ns:2 * ns + n]), list(res[2 * ns + n:2 * ns + 2 * n]), res[2 * ns + 2 * n]


def _pair_wait(name, send, recv, grads, lands, after):
    n = len(grads)
    ns = N_CHIPS * n

    def body(*refs):
        src, land = refs[:n], refs[n:2 * n]
        send_ref, recv_ref = refs[2 * n:2 * n + ns], refs[2 * n + ns:2 * n + 2 * ns]
        x, y, c, _ = _place()
        for a in range(n):
            for j in range(N_CHIPS):
                cp = _remote(land[a].at[j], land[a].at[j], send_ref[N_CHIPS * a + j], recv_ref[N_CHIPS * a + j], (x, y, c))
                cp.wait_send()
                cp.wait_recv()

    res = pl.pallas_call(
        body, name=name, in_specs=[HBM] * (2 * n) + [SEM] * (2 * ns) + [pl.BlockSpec(memory_space=pl.ANY)],
        out_specs=[HBM] * (2 * n), out_shape=[pltpu.HBM(g.shape, g.dtype) for g in grads] + [pltpu.HBM(l.shape, l.dtype) for l in lands],
        input_output_aliases={a: a for a in range(2 * n)},
        compiler_params=pltpu.CompilerParams(has_side_effects=EFFECT),
    )(*grads, *lands, *send, *recv, after)
    return list(res[:n]), list(res[n:])


def _chip_start(name, parts):
    n = len(parts)
    ns = 3 * n

    def body(*refs):
        send, recv = refs[2 * n:2 * n + ns], refs[2 * n + ns:2 * n + 2 * ns]
        src = refs[2 * n + 2 * ns:3 * n + 2 * ns]
        land = refs[3 * n + 2 * ns:4 * n + 2 * ns]
        token = refs[4 * n + 2 * ns]
        x, y, c, chips = _place()
        for a in range(n):
            for j, (cx, cy) in enumerate(chips):
                _remote(src[a].at[j], land[a].at[j], send[3 * a + j], recv[3 * a + j], (cx, cy, c)).start()
        token[...] = jnp.zeros_like(token)

    res = pl.pallas_call(
        body, name=name, in_specs=[HBM] * (2 * n),
        out_specs=[SEM] * (2 * ns) + [HBM] * (2 * n) + [pl.BlockSpec(memory_space=pltpu.VMEM)],
        out_shape=[pltpu.SemaphoreType.DMA(())] * (2 * ns) + [pltpu.HBM(p.shape, p.dtype) for p in parts] * 2
        + [jax.ShapeDtypeStruct((8, 128), F32)],
        input_output_aliases={a: 2 * ns + a for a in range(2 * n)},
        compiler_params=pltpu.CompilerParams(has_side_effects=EFFECT),
    )(*[_hbm(p) for p in parts], *[_hbm(lax.empty(p.shape, p.dtype)) for p in parts])
    return list(res[:ns]), list(res[ns:2 * ns]), list(res[2 * ns:2 * ns + n]), list(res[2 * ns + n:2 * ns + 2 * n]), res[2 * ns + 2 * n]


def _chip_wait(name, send, recv, parts, lands, after):
    n = len(parts)
    ns = 3 * n

    def body(*refs):
        src, land = refs[:n], refs[n:2 * n]
        send_ref, recv_ref = refs[2 * n:2 * n + ns], refs[2 * n + ns:2 * n + 2 * ns]
        x, y, c, _ = _place()
        for a in range(n):
            for j in range(3):
                cp = _remote(src[a].at[j], land[a].at[j], send_ref[3 * a + j], recv_ref[3 * a + j], (x, y, c))
                cp.wait_send()
                cp.wait_recv()

    res = pl.pallas_call(
        body, name=name, in_specs=[HBM] * (2 * n) + [SEM] * (2 * ns) + [pl.BlockSpec(memory_space=pl.ANY)],
        out_specs=[HBM] * (2 * n), out_shape=[pltpu.HBM(p.shape, p.dtype) for p in parts] * 2,
        input_output_aliases={a: a for a in range(2 * n)},
        compiler_params=pltpu.CompilerParams(has_side_effects=EFFECT),
    )(*parts, *lands, *send, *recv, after)
    return list(res[n:])


def _share_start(name, shards):
    n = len(shards)

    def body(*refs):
        send, recv = refs[n:2 * n], refs[2 * n:3 * n]
        buf = refs[3 * n:4 * n]
        token = refs[4 * n]
        x, y, c, _ = _place()
        for a in range(n):
            half = buf[a].shape[0] // 2
            mine = buf[a].at[pl.ds(c * half, half)]
            _remote(mine, mine, send[a], recv[a], (x, y, 1 - c)).start()
        token[...] = jnp.zeros_like(token)

    res = pl.pallas_call(
        body, name=name, in_specs=[HBM] * n,
        out_specs=[SEM] * (2 * n) + [HBM] * n + [pl.BlockSpec(memory_space=pltpu.VMEM)],
        out_shape=[pltpu.SemaphoreType.DMA(())] * (2 * n) + [pltpu.HBM(s.shape, s.dtype) for s in shards]
        + [jax.ShapeDtypeStruct((8, 128), F32)],
        input_output_aliases={a: 2 * n + a for a in range(n)},
        compiler_params=pltpu.CompilerParams(has_side_effects=EFFECT),
    )(*[_hbm(s) for s in shards])
    return list(res[:n]), list(res[n:2 * n]), list(res[2 * n:3 * n]), res[3 * n]


def _share_wait(name, send, recv, shards, after):
    n = len(shards)

    def body(*refs):
        buf = refs[:n]
        send_ref, recv_ref = refs[n:2 * n], refs[2 * n:3 * n]
        x, y, c, _ = _place()
        for a in range(n):
            half = buf[a].shape[0] // 2
            cp = _remote(buf[a].at[pl.ds(c * half, half)], buf[a].at[pl.ds((1 - c) * half, half)], send_ref[a], recv_ref[a], (x, y, c))
            cp.wait_send()
            cp.wait_recv()

    res = pl.pallas_call(
        body, name=name, in_specs=[HBM] * n + [SEM] * (2 * n) + [pl.BlockSpec(memory_space=pl.ANY)], out_specs=[HBM] * n,
        out_shape=[pltpu.HBM(s.shape, s.dtype) for s in shards], input_output_aliases={a: a for a in range(n)},
        compiler_params=pltpu.CompilerParams(has_side_effects=EFFECT),
    )(*shards, *send, *recv, after)
    return list(res)


def _small_start(buf):
    R, W = buf.shape
    ns = N_DEV - 1

    def body(*refs):
        send, recv = refs[2:2 + ns], refs[2 + ns:2 + 2 * ns]
        src, land, token = refs[2 + 2 * ns], refs[3 + 2 * ns], refs[4 + 2 * ns]
        x, y, c, _ = _place()
        me = 4 * x + 2 * y + c
        for k in range(1, N_DEV):
            peer = (x ^ (k >> 2), y ^ ((k >> 1) & 1), c ^ (k & 1))
            _remote(src, land.at[me], send[k - 1], recv[k - 1], peer).start()
        token[...] = jnp.zeros_like(token)

    res = pl.pallas_call(
        body, name="small_start", in_specs=[HBM, HBM],
        out_specs=[SEM] * (2 * ns) + [HBM, HBM, pl.BlockSpec(memory_space=pltpu.VMEM)],
        out_shape=[pltpu.SemaphoreType.DMA(())] * (2 * ns) + [pltpu.HBM((R, W), F32), pltpu.HBM((N_DEV, R, W), F32),
                                                                jax.ShapeDtypeStruct((8, 128), F32)],
        input_output_aliases={0: 2 * ns, 1: 2 * ns + 1},
        compiler_params=pltpu.CompilerParams(has_side_effects=EFFECT),
    )(_hbm(buf), _hbm(jnp.zeros((N_DEV, R, W), F32)))
    return list(res[:ns]), list(res[ns:2 * ns]), res[2 * ns], res[2 * ns + 1], res[2 * ns + 2]


def _small_wait(send, recv, buf, land, after):
    ns = N_DEV - 1

    def body(*refs):
        land_ref = refs[1]
        send_ref, recv_ref = refs[2:2 + ns], refs[2 + ns:2 + 2 * ns]
        x, y, c, _ = _place()
        me = 4 * x + 2 * y + c
        for k in range(1, N_DEV):
            landed = land_ref.at[me ^ k]
            cp = _remote(landed, landed, send_ref[k - 1], recv_ref[k - 1], (x, y, c))
            cp.wait_send()
            cp.wait_recv()

    return pl.pallas_call(
        body, name="small_wait", in_specs=[HBM, HBM] + [SEM] * (2 * ns) + [pl.BlockSpec(memory_space=pl.ANY)],
        out_specs=[HBM, HBM], out_shape=[pltpu.HBM(buf.shape, buf.dtype), pltpu.HBM(land.shape, land.dtype)],
        input_output_aliases={0: 0, 1: 1}, compiler_params=pltpu.CompilerParams(has_side_effects=EFFECT),
    )(buf, land, *send, *recv, after)


def _small_sum(buf, land):
    def body(buf_ref, land_ref, out_ref):
        x, y, c, _ = _place()
        me = 4 * x + 2 * y + c
        total = None
        for d in range(N_DEV):
            term = jnp.where(me == d, buf_ref[...], land_ref[d])
            total = term if total is None else total + term
        out_ref[...] = total

    return pl.pallas_call(body, name="small_sum", out_shape=jax.ShapeDtypeStruct(buf.shape, F32))(buf, land)


def _pair_sum_bf16(name, grad, theirs, ids):
    _, R2, C = theirs.shape
    tr = _tile(R2, 512 if C <= 2048 else 256, 16)
    nrb = R2 // tr

    def body(ids_ref, a_ref, b_ref, o_ref):
        o_ref[...] = (a_ref[...] + b_ref[...]).astype(BF16)

    return pl.pallas_call(
        body, name=name,
        grid_spec=pltpu.PrefetchScalarGridSpec(
            num_scalar_prefetch=1, grid=(3, nrb),
            in_specs=[pl.BlockSpec((None, tr, C), lambda j, i, ids: (ids[3 + j], ids[1] * nrb + i, 0)),
                      pl.BlockSpec((None, tr, C), lambda j, i, ids: (j + 1, i, 0))],
            out_specs=pl.BlockSpec((None, tr, C), lambda j, i, ids: (j, i, 0))),
        out_shape=jax.ShapeDtypeStruct((3, R2, C), BF16), compiler_params=_params(("parallel", "parallel")),
    )(ids, grad, theirs)


def _chip_sum(name, grad, theirs, arrived, ids):
    _, R2, C = theirs.shape
    tr = _tile(R2, 512 if C <= 2048 else 256, 16)
    nrb = R2 // tr

    def body(ids_ref, a_ref, b_ref, r_ref, o_ref):
        tot = a_ref[...] + b_ref[...]
        for j in range(3):
            tot = tot + r_ref[j].astype(F32)
        o_ref[...] = tot

    return pl.pallas_call(
        body, name=name,
        grid_spec=pltpu.PrefetchScalarGridSpec(
            num_scalar_prefetch=1, grid=(nrb,),
            in_specs=[pl.BlockSpec((None, tr, C), lambda i, ids: (ids[0], ids[1] * nrb + i, 0)),
                      pl.BlockSpec((None, tr, C), lambda i, ids: (0, i, 0)),
                      pl.BlockSpec((3, tr, C), lambda i, ids: (0, i, 0))],
            out_specs=pl.BlockSpec((tr, C), lambda i, ids: (ids[1] * nrb + i, 0))),
        out_shape=jax.ShapeDtypeStruct((2 * R2, C), F32), compiler_params=_params(("parallel",)),
    )(ids, grad, theirs, arrived)


def _adamw(name, w, g, m, v, emit_grad=False):
    R, C = w.shape
    tr = _tile(R, 256 if C <= 2048 else 128, 8)
    c1 = 1.0 / (1.0 - ADAM_B1 ** ADAM_STEP)
    c2 = 1.0 / (1.0 - ADAM_B2 ** ADAM_STEP)
    n_out = 4 if emit_grad else 3

    def body(w_ref, g_ref, m_ref, v_ref, d_ref, mo_ref, vo_ref, *rest):
        gv = g_ref[...]
        mn = ADAM_B1 * m_ref[...] + (1.0 - ADAM_B1) * gv
        vn = ADAM_B2 * v_ref[...] + (1.0 - ADAM_B2) * (gv * gv)
        d_ref[...] = -ADAM_LR * ((mn * c1) / (jnp.sqrt(vn * c2) + ADAM_EPS) + ADAM_WD * w_ref[...])
        mo_ref[...] = mn
        vo_ref[...] = vn
        if emit_grad:
            rest[0][...] = gv

    spec = pl.BlockSpec((tr, C), lambda i: (i, 0))
    sds = jax.ShapeDtypeStruct((R, C), F32)
    return pl.pallas_call(body, name=name, grid=(R // tr,), in_specs=[spec] * 4, out_specs=[spec] * n_out, out_shape=[sds] * n_out,
                          compiler_params=_params(("parallel",)))(w, g, m, v)


SMALL = ["ffn1_norm", "mix_norm", "mem_norm", "forget_bias", "fox_q_gain", "fox_k_gain", "swa_q_gain", "swa_k_gain", "swa_sinks",
         "mem_q_gain", "mem_k_gain", "ffn2_norm"]
LARGE = ["ffn1_gate", "ffn1_up", "ffn1_down", "w_in", "w_mem_k", "w_mem_v", "w_out", "ffn2_gate", "ffn2_up", "ffn2_down"]
GATHER_GROUPS = [["ffn1_gate", "ffn1_up"], ["ffn1_down"], ["w_in", "w_mem_k", "w_mem_v"], ["w_out", "ffn2_gate", "ffn2_up", "ffn2_down"]]
WEIGHTS = ["ffn1_norm", "ffn1_gate", "ffn1_up", "ffn1_down", "mix_norm", "mem_norm", "w_in", "forget_bias", "w_mem_k", "w_mem_v",
           "fox_q_gain", "fox_k_gain", "swa_q_gain", "swa_k_gain", "swa_sinks", "mem_q_gain", "mem_k_gain", "w_out", "ffn2_norm",
           "ffn2_gate", "ffn2_up", "ffn2_down"]


def _pad_proj_cols(w):
    out = jnp.zeros((w.shape[0], PROJ_W), w.dtype)
    for start, width, pstart in REF_GROUPS:
        out = lax.dynamic_update_slice(out, w[:, start:start + width], (0, pstart))
    return out


def _unpad_proj_cols(w):
    return jnp.concatenate([w[:, pstart:pstart + width] for _, width, pstart in REF_GROUPS], axis=1)


def _pack_small(vals):
    flat = jnp.concatenate([vals[k].reshape(-1).astype(F32) for k in SMALL + ["loss"]])
    n = flat.shape[0]
    total = -(-n // 1024) * 1024
    return jnp.pad(flat, (0, total - n)).reshape(total // 128, 128)


def _unpack_small(buf, shapes):
    flat = buf.reshape(-1)
    out, off = {}, 0
    for k in SMALL + ["loss"]:
        size = int(np.prod(shapes[k]))
        out[k] = flat[off:off + size].reshape(shapes[k])
        off += size
    return out


def kernel(x, mem, ffn1_norm, ffn1_gate, ffn1_up, ffn1_down, mix_norm, mem_norm, w_in, forget_bias, w_mem_k, w_mem_v, fox_q_gain, fox_k_gain, swa_q_gain, swa_k_gain, swa_sinks, mem_q_gain, mem_k_gain, w_out, ffn2_norm, ffn2_gate, ffn2_up, ffn2_down, loss_target, m_ffn1_norm, m_ffn1_gate, m_ffn1_up, m_ffn1_down, m_mix_norm, m_mem_norm, m_w_in, m_forget_bias, m_w_mem_k, m_w_mem_v, m_fox_q_gain, m_fox_k_gain, m_swa_q_gain, m_swa_k_gain, m_swa_sinks, m_mem_q_gain, m_mem_k_gain, m_w_out, m_ffn2_norm, m_ffn2_gate, m_ffn2_up, m_ffn2_down, v_ffn1_norm, v_ffn1_gate, v_ffn1_up, v_ffn1_down, v_mix_norm, v_mem_norm, v_w_in, v_forget_bias, v_w_mem_k, v_w_mem_v, v_fox_q_gain, v_fox_k_gain, v_swa_q_gain, v_swa_k_gain, v_swa_sinks, v_mem_q_gain, v_mem_k_gain, v_w_out, v_ffn2_norm, v_ffn2_gate, v_ffn2_up, v_ffn2_down):
    given = dict(locals())
    T, D = x.shape[1], x.shape[2]
    ML = mem.shape[1]
    xin = x.reshape(T, D)
    target = loss_target.reshape(T, D)
    memin = mem.reshape(ML, D)

    ids = _place_ids()
    shard = {k: given[k][0] for k in LARGE}
    started, after = [], ids
    for gi, group in enumerate(GATHER_GROUPS):
        also = {}
        if "w_in" in group:
            tied = lax.optimization_barrier((given["w_in"], given["m_w_in"], given["v_w_in"], after))
            w_in_rows = tuple(t[0] for t in tied[:3])
            shard["w_in"] = _pad_proj_cols(w_in_rows[0])
            also["w_in"] = w_in_rows[1:]
        if gi == 0:
            placed, own = zip(*[_cast_place("cast_" + k, shard[k], ids, after, keep_own=True) for k in group])
        else:
            placed = [_cast_place("cast_" + k, shard[k], ids, after, *also.get(k, ())) for k in group]
        send, recv, bufs, after = _gather_start("gather_start_%d" % gi, list(placed), after)
        started.append((send, recv, bufs))

    def arrive(gi, done):
        send, recv, bufs = started[gi]
        bufs = _gather_wait("gather_wait_%d" % gi, send, recv, bufs, done)
        return dict(zip(GATHER_GROUPS[gi], _gather_forward("gather_forward_%d" % gi, bufs)))

    gains = jnp.concatenate([fox_q_gain, fox_k_gain, swa_q_gain, swa_k_gain, mem_q_gain,
                             jnp.pad(forget_bias, ((0, 0), (0, HEAD_DIM - FOX_HEADS))), jnp.zeros((2, HEAD_DIM), F32)], axis=0)
    slopes = _swa_bias_table()
    sinks = jnp.repeat(swa_sinks.reshape(SWA_HEADS), WINDOW).reshape(SWA_KV_HEADS, GR, 1)

    h1 = _rms_fwd("ffn1_norm_fwd", xin, ffn1_norm + after[0, 0])
    own_block = jnp.stack([jnp.zeros((), jnp.int32), ids[0]]).reshape(2, 1)
    part = _ffn_gu_blocks("ffn1_gate_up_own", h1, own[0][None], own[1][None], own_block, N_CHIPS)
    full = arrive(0, part[2])
    wg1, wu1 = full["ffn1_gate"], full["ffn1_up"]
    fg1, fu1, a1 = _ffn_gu_blocks("ffn1_gate_up", h1, wg1, wu1, jnp.stack([ids[3:], ids[3:]]), N_CHIPS, prev=part)
    wd1 = arrive(1, a1)["ffn1_down"].reshape(-1, D)
    x1, h2 = _residual_norm("ffn1_down", a1, wd1, xin, 0.5, mix_norm, 256)
    full = arrive(2, h2)
    win = full["w_in"].reshape(D, PROJ_W)
    wmk = full["w_mem_k"].reshape(D, MEM_HEADS * HEAD_DIM)
    wmv = full["w_mem_v"].reshape(D, MEM_HEADS * HEAD_DIM)
    proj = _mm2d("proj_in", h2, win, "nn", F32, tn=1408, tk=2048, n_outer=True)
    qkv, logf, k_t, v_t = _prep_fwd(proj, gains)
    cum = _cumsum_rows("forget_cumsum", [logf], False)
    cum_h = cum[:, :FOX_HEADS].T
    cq_row = cum_h.reshape(FOX_HEADS, 1, T)
    ck_rep = jnp.broadcast_to(cum_h[:, :, None], (FOX_HEADS, T, HEAD_DIM))
    mn = _rms_fwd("mem_norm_fwd", memin, mem_norm)
    mk_raw = _mm2d("mem_k_proj", mn, wmk, "nn", F32)
    mv = _mm2d("mem_v_proj", mn, wmv, "nn", BF16)
    mk = _head_norm_rows(mk_raw, mem_k_gain)
    out_a, out_a_f32, lse_a = _fox_fwd(qkv, v_t, cq_row, ck_rep)
    send, recv, bufs = started[3]
    bufs = _gather_wait("gather_wait_3", send, recv, bufs, out_a)
    send, recv, bufs, token = _gather_start("gather_pass_start_3", bufs, out_a, to_sibling=True)
    out_b, lse_b = _swa_fwd(qkv, slopes + token[0, 0], sinks)
    out_c, lse_c = _mem_fwd(qkv, mk, mv)
    mixed = jnp.concatenate([out_a, out_b, out_c], axis=1)
    full = dict(zip(GATHER_GROUPS[3], _gather_wait("gather_pass_wait_3", send, recv, bufs, mixed, to_sibling=True)))
    wo = full["w_out"].reshape(-1, D)
    wg2, wu2, wd2 = full["ffn2_gate"], full["ffn2_up"], full["ffn2_down"].reshape(-1, D)
    x2, h3 = _residual_norm("mix_out", mixed, wo, x1, 1.0, ffn2_norm, 512)
    fg2, fu2, a2 = _ffn_gu("ffn2_gate_up", h3, wg2, wu2)
    dx3, dyb3, loss_blocks = _ffn_down_loss("ffn2_down", a2, wd2, x2, target)

    grads, small, res = {}, {"loss": jnp.sum(loss_blocks[::8, 0])}, {}

    def pair_off(tag, group):
        send, recv, own, lands, token = _pair_start("grad_pair_start_" + tag, [grads[k] for k in group])
        return (group, send, recv, own, lands), token

    def chip_off(tag, started, done):
        group, send, recv, own, lands = started
        own, theirs = _pair_wait("grad_pair_wait_" + tag, send, recv, own, lands, done)
        grads.update(zip(group, own))
        to_chips = [_pair_sum_bf16("pair_sum_" + k, grads[k], b, ids) for k, b in zip(group, theirs)]
        send, recv, parts, lands, token = _chip_start("grad_chip_start_" + tag, to_chips)
        return (group, theirs, send, recv, parts, lands), token

    def reduce_half(tag, state, done):
        group, theirs, send, recv, parts, lands = state
        arrived = _chip_wait("grad_chip_wait_" + tag, send, recv, parts, lands, done)
        halves = [_chip_sum("chip_sum_" + k, grads[k], b, r, ids) for k, b, r in zip(group, theirs, arrived)]
        send, recv, shards, token = _share_start("grad_share_start_" + tag, halves)
        return (group, send, recv, shards), token

    def update(tag, shared, done):
        group, send, recv, shards = shared
        reduced = dict(zip(group, _share_wait("grad_share_wait_" + tag, send, recv, shards, done)))
        last = None
        for k in group:
            if k == "w_in":
                gk = _unpad_proj_cols(reduced[k])
                d, mo, vo = _adamw("adamw_" + k, w_in_rows[0], gk, w_in_rows[1], w_in_rows[2])
            else:
                d, mo, vo, gk = _adamw("adamw_" + k, given[k][0], reduced[k], given["m_" + k][0], given["v_" + k][0], emit_grad=True)
            res[k] = tuple(t[None] for t in (gk, d, mo, vo))
            last = vo
        return last

    dg2, du2 = _ffn_bwd_act("ffn2", dyb3, wd2, fg2, fu2, N_CHIPS)
    grads["ffn2_down"] = _ffn_bwd_down("ffn2", a2, dyb3, N_CHIPS).reshape(N_CHIPS, -1, D)
    grads["ffn2_gate"], grads["ffn2_up"] = _ffn_bwd_gate_up("ffn2", h3, dg2, du2, N_CHIPS)
    started, token = pair_off("a", ["ffn2_gate", "ffn2_up", "ffn2_down"])
    dh3 = _ffn_bwd_x("ffn2", dg2, du2, wg2, wu2, token)
    state_a, token = chip_off("a", started, dh3)
    dx2, dx2b, small["ffn2_norm"] = _rms_bwd("ffn2_norm_bwd", dh3, x2, ffn2_norm + token[0, 0], dx3, 1.0)
    dmix = _mm2d("mix_out_dx", dx2b, wo, "nt", BF16, tk=2048, n_outer=True)
    grads["w_out"] = _mm2d("mix_out_dw", mixed, dx2b, "tn", F32, tk=T, n_outer=True, resident=True).reshape(N_CHIPS, -1, D)
    delta_row = _fox_delta(dmix, out_a_f32)[:, :FOX_HEADS].T.reshape(FOX_HEADS, 1, T)
    dfq, dfk, dfv, dck, dcq = _fox_bwd(qkv, k_t, cq_row, ck_rep, delta_row, lse_a, dmix)
    dsq, dsk, dsv, dsink = _swa_bwd(qkv, slopes, sinks, out_b, lse_b, dmix)
    dmq, dmk, dmv = _mem_bwd(qkv, mk, mv, out_c, lse_c, dmix)
    small["swa_sinks"] = dsink[:, :SWA_GROUP, 0].reshape(1, SWA_HEADS)
    dcum = jnp.pad(dcq.reshape(FOX_HEADS, T).T, ((0, 0), (0, HEAD_DIM - FOX_HEADS)))
    dlogf = _cumsum_rows("forget_cumsum_bwd", [dcum], True, columns=dck)
    dproj, dgains = _prep_bwd(proj, gains, dfq, dfk, dfv, dsq, dsk, dsv, dmq, dlogf)
    for row, k in enumerate(["fox_q_gain", "fox_k_gain", "swa_q_gain", "swa_k_gain", "mem_q_gain"]):
        small[k] = dgains[row:row + 1, :]
    small["forget_bias"] = dgains[5:6, :FOX_HEADS]
    grads["w_in"] = _mm2d("proj_in_dw", h2, dproj, "tn", F32, tn=1408, tk=T, n_outer=True, resident=True).reshape(N_CHIPS, -1, PROJ_W)
    dmk_raw, small["mem_k_gain"] = _head_norm_rows_bwd(mk_raw, mem_k_gain, dmk)
    dmvb = dmv.astype(BF16)
    grads["w_mem_k"] = _mm2d("mem_k_dw", mn, dmk_raw, "tn", F32).reshape(N_CHIPS, -1, MEM_HEADS * HEAD_DIM)
    grads["w_mem_v"] = _mm2d("mem_v_dw", mn, dmvb, "tn", F32).reshape(N_CHIPS, -1, MEM_HEADS * HEAD_DIM)
    dmn = _mm2d("mem_k_dx", dmk_raw, wmk, "nt", F32)
    dmn = _mm2d("mem_v_dx", dmvb, wmv, "nt", F32, extras=[dmn], epilogue=lambda accs, ex: [ex[0] + accs[0]])
    _, _, small["mem_norm"] = _rms_bwd("mem_norm_bwd", dmn, memin, mem_norm, jnp.zeros_like(memin), 1.0)
    started, token = pair_off("b", ["w_out", "w_in", "w_mem_k", "w_mem_v"])
    dh2 = _mm2d("proj_in_dx", dproj, win, "nt", F32, tm=512, tn=1024, tk=PROJ_W, n_outer=True, resident=True, after=token)
    state_b, token = chip_off("b", started, dh2)
    dx1, dyb1, small["mix_norm"] = _rms_bwd("mix_norm_bwd", dh2, x1, mix_norm + token[0, 0], dx2, 0.5)
    grads["ffn1_down"] = _ffn_bwd_down("ffn1", a1, dyb1, N_CHIPS).reshape(N_CHIPS, -1, D)
    started, token = pair_off("c", ["ffn1_down"])
    dg1, du1 = _ffn_bwd_act("ffn1", dyb1, wd1, fg1, fu1, N_CHIPS, after=token)
    state_c, token = chip_off("c", started, dg1)
    grads["ffn1_gate"], grads["ffn1_up"] = _ffn_bwd_gate_up("ffn1", h1, dg1, du1, N_CHIPS, after=token)
    started, token = pair_off("d", ["ffn1_gate", "ffn1_up"])
    dh1 = _ffn_bwd_x("ffn1", dg1, du1, wg1, wu1, token)
    state_d, token = chip_off("d", started, dh1)
    grad_x, _, small["ffn1_norm"] = _rms_bwd("ffn1_norm_bwd", dh1, xin, ffn1_norm + token[0, 0], dx1, 1.0)

    s_send, s_recv, s_buf, s_land, token = _small_start(_pack_small(small))

    shared_a, token = reduce_half("a", state_a, token)
    shared_b, token = reduce_half("b", state_b, token)
    done = update("a", shared_a, token)
    shared_c, token = reduce_half("c", state_c, done)
    done = update("b", shared_b, token)
    shared_d, token = reduce_half("d", state_d, done)
    done = update("c", shared_c, token)
    done = update("d", shared_d, done)

    shapes = {k: given[k].shape for k in SMALL}
    shapes["loss"] = ()
    s_buf, s_land = _small_wait(s_send, s_recv, s_buf, s_land, done)
    red_small = _unpack_small(_small_sum(s_buf, s_land), shapes)
    loss = red_small["loss"]
    zero = {"loss": jnp.zeros((), F32)}
    packed = [_pack_small({**zero, **{k: src[k] for k in SMALL}}) for src in (
        {k: given[k] for k in SMALL}, red_small, {k: given["m_" + k] for k in SMALL}, {k: given["v_" + k] for k in SMALL})]
    d_s, m_s, v_s = (_unpack_small(t, shapes) for t in _adamw("adamw_small", *packed))
    for k in SMALL:
        res[k] = (red_small[k], d_s[k], m_s[k], v_s[k])

    outs = [loss, grad_x.reshape(1, T, D)]
    for part in range(4):
        outs += [res[k][part] for k in WEIGHTS]
    return tuple(outs)
```

```python
import functools

import numpy as np
import jax
import jax.numpy as jnp
from jax import lax
from jax.experimental import pallas as pl
from jax.experimental.pallas import tpu as pltpu

F32 = jnp.float32
BF16 = jnp.bfloat16
MESH = pl.DeviceIdType.MESH

HEAD_DIM = 128
FOX_HEADS = 6
SWA_HEADS = 6
SWA_KV_HEADS = 2
SWA_GROUP = SWA_HEADS // SWA_KV_HEADS
MEM_HEADS = 4
WINDOW = 128
EPS = 1e-6
NEG_INF = -1e30
SCALE = HEAD_DIM ** -0.5

C_FQ = 0
C_FK = C_FQ + FOX_HEADS * HEAD_DIM
C_FV = C_FK + FOX_HEADS * HEAD_DIM
C_SQ = C_FV + FOX_HEADS * HEAD_DIM
C_SK = C_SQ + SWA_HEADS * HEAD_DIM
C_SV = C_SK + SWA_KV_HEADS * HEAD_DIM
C_MQ = C_SV + SWA_KV_HEADS * HEAD_DIM
C_FL = C_MQ + MEM_HEADS * HEAD_DIM
PROJ_W = C_FL + HEAD_DIM
FOX_W = FOX_HEADS * HEAD_DIM
REF_GROUPS = [
    (0, FOX_W, C_FQ), (FOX_W, FOX_W, C_FK), (2 * FOX_W, FOX_W, C_FV), (3 * FOX_W, FOX_HEADS, C_FL),
    (3 * FOX_W + FOX_HEADS, SWA_HEADS * HEAD_DIM, C_SQ),
    (3 * FOX_W + FOX_HEADS + SWA_HEADS * HEAD_DIM, SWA_KV_HEADS * HEAD_DIM, C_SK),
    (3 * FOX_W + FOX_HEADS + (SWA_HEADS + SWA_KV_HEADS) * HEAD_DIM, SWA_KV_HEADS * HEAD_DIM, C_SV),
    (3 * FOX_W + FOX_HEADS + (SWA_HEADS + 2 * SWA_KV_HEADS) * HEAD_DIM, MEM_HEADS * HEAD_DIM, C_MQ),
]

ADAM_LR = 0.001
ADAM_B1 = 0.9
ADAM_B2 = 0.999
ADAM_EPS = 1e-08
ADAM_WD = 0.01
ADAM_STEP = 10

V7X_VMEM_LIMIT = 56 * 1024 * 1024
N_CHIPS = 4
N_DEV = 8


def _tile(n, pref, mult=128):
    t = (min(pref, n) // mult) * mult
    while t >= mult:
        if n % t == 0:
            return t
        t -= mult
    return n


def _params(sem):
    return pltpu.CompilerParams(dimension_semantics=sem, vmem_limit_bytes=V7X_VMEM_LIMIT)


_DIMS = {"nn": (((1,), (0,)), ((), ())), "nt": (((1,), (1,)), ((), ())), "tn": (((0,), (0,)), ((), ()))}


def _dot(a, b, mode):
    return lax.dot_general(a, b, _DIMS[mode], preferred_element_type=F32)


def _mm(name, grid, pairs, acc_of, acc_shapes, extras, outs, epilogue, after=None):
    n_p, n_e, n_o, n_a = len(pairs), len(extras), len(outs), len(acc_shapes)
    n_w = 0 if after is None else 1
    nk = grid[2]
    n_in = sum(1 if a is None else 2 for a, *_ in pairs)

    def body(*refs):
        ex = refs[n_in:n_in + n_e]
        out = refs[n_in + n_e + n_w:n_in + n_e + n_w + n_o]
        accs = refs[n_in + n_e + n_w + n_o:]
        parts = [None] * n_a
        at = 0
        for p in range(n_p):
            if pairs[p][0] is None:
                a_ref, b_ref = refs[0], refs[at]
                at += 1
            else:
                a_ref, b_ref = refs[at], refs[at + 1]
                at += 2
            d = _dot(a_ref[...], b_ref[...], pairs[p][4])
            parts[acc_of[p]] = d if parts[acc_of[p]] is None else parts[acc_of[p]] + d

        def finish(vals):
            for o, r in zip(out, epilogue(vals, [e[...] for e in ex])):
                o[...] = r.astype(o.dtype)

        if nk == 1:
            finish(parts)
            return
        k = pl.program_id(2)

        @pl.when(k == 0)
        def _():
            for a, d in zip(accs, parts):
                a[...] = d

        @pl.when((k > 0) & (k < nk - 1))
        def _():
            for a, d in zip(accs, parts):
                a[...] += d

        @pl.when(k == nk - 1)
        def _():
            finish([a[...] + d for a, d in zip(accs, parts)])

    in_specs, args = [], []
    for a, a_spec, b, b_spec, _ in pairs:
        if a is not None:
            in_specs.append(a_spec)
            args.append(a)
        in_specs.append(b_spec)
        args.append(b)
    for e, e_spec in extras:
        in_specs.append(e_spec)
        args.append(e)
    if after is not None:
        in_specs.append(pl.BlockSpec(memory_space=pl.ANY))
        args.append(after)
    res = pl.pallas_call(
        body, name=name, grid=grid, in_specs=in_specs,
        out_specs=[s for _, s in outs], out_shape=[o for o, _ in outs],
        scratch_shapes=[pltpu.VMEM(s, F32) for s in acc_shapes] if nk > 1 else [],
        compiler_params=_params(("parallel", "parallel", "arbitrary")),
    )(*args)
    return res


def _mm2d(name, a, b, mode, out_dtype, tm=512, tn=1024, tk=1024, extras=(), epilogue=None, n_out=1, after=None, n_outer=False,
          resident=False, rows=(), tile_stat=False):
    if mode == "nn":
        (M, K), N = a.shape, b.shape[1]
    elif mode == "nt":
        (M, K), N = a.shape, b.shape[0]
    else:
        (K, M), N = a.shape, b.shape[1]
    tm, tn, tk = _tile(M, tm), _tile(N, tn), _tile(K, tk)
    assert not resident or tk == K

    def spec(shape, index, single=False):
        mode_kw = {"pipeline_mode": pl.Buffered(1)} if single else {}
        if n_outer:
            return pl.BlockSpec(shape, lambda j, i, k: index(i, j, k), **mode_kw)
        return pl.BlockSpec(shape, index, **mode_kw)

    single_a, single_b = resident and not n_outer, resident and n_outer
    a_spec = spec((tk, tm), lambda i, j, k: (k, i), single_a) if mode == "tn" else spec((tm, tk), lambda i, j, k: (i, k), single_a)
    b_spec = spec((tn, tk), lambda i, j, k: (j, k), single_b) if mode == "nt" else spec((tk, tn), lambda i, j, k: (k, j), single_b)
    mn = spec((tm, tn), lambda i, j, k: (i, j))
    if epilogue is None:
        epilogue = lambda accs, ex: [accs[0]]
    if not isinstance(out_dtype, (list, tuple)):
        out_dtype = [out_dtype] * n_out
    grid = (N // tn, M // tm, K // tk) if n_outer else (M // tm, N // tn, K // tk)
    outs = [(jax.ShapeDtypeStruct((M, N), d), mn) for d in out_dtype]
    if tile_stat:
        assert tn == N
        outs.append((jax.ShapeDtypeStruct((8 * (M // tm), 128), F32), spec((8, 128), lambda i, j, k: (i, 0))))
    res = _mm(name, grid, [(a, a_spec, b, b_spec, mode)], [0], [(tm, tn)],
              [(e, mn) for e in extras] + [(r, spec((1, tn), lambda i, j, k: (0, j))) for r in rows], outs, epilogue, after=after)
    return res[0] if len(res) == 1 else res


def _sigmoid(x):
    return 1.0 / (1.0 + jnp.exp(-x))


def _sigmoid_fast(x):
    return pl.reciprocal(1.0 + jnp.exp(-x), approx=True)


def _ffn_gu(name, h, wg, wu):
    T, D = h.shape
    nf, _, F4 = wg.shape
    tm, tk = _tile(T, 512), _tile(D, 2048)
    a_spec = pl.BlockSpec((tm, tk), lambda j, i, k: (i, k))
    b_spec = pl.BlockSpec((None, tk, F4), lambda j, i, k: (j, k, 0))
    o_spec = pl.BlockSpec((tm, F4), lambda j, i, k: (i, j))

    def epilogue(accs, ex):
        g, u = accs
        s = _sigmoid_fast(g)
        gs = g * s
        return [(s + s * (g - gs)) * u, gs, gs * u]

    sds = jax.ShapeDtypeStruct((T, nf * F4), BF16)
    return _mm(name, (nf, T // tm, D // tk), [(h, a_spec, wg, b_spec, "nn"), (None, None, wu, b_spec, "nn")], [0, 1],
               [(tm, F4), (tm, F4)], [], [(sds, o_spec)] * 3, epilogue)


def _ffn_gu_blocks(name, h, wg, wu, blocks, n_total, prev=None):
    T, D = h.shape
    F4 = wg.shape[2]
    tm = _tile(T, 512)

    def body(tab, h_ref, wg_ref, wu_ref, *rest):
        out = rest[-3:]
        hv = h_ref[...]
        g, u = _dot(hv, wg_ref[...], "nn"), _dot(hv, wu_ref[...], "nn")
        s = _sigmoid_fast(g)
        gs = g * s
        out[0][...] = ((s + s * (g - gs)) * u).astype(BF16)
        out[1][...] = gs.astype(BF16)
        out[2][...] = (gs * u).astype(BF16)

    w_spec = pl.BlockSpec((None, D, F4), lambda j, i, tab: (tab[0, j], 0, 0))
    o_spec = pl.BlockSpec((tm, F4), lambda j, i, tab: (i, tab[1, j]))
    filled = [] if prev is None else list(prev)
    sds = jax.ShapeDtypeStruct((T, n_total * F4), BF16)
    return pl.pallas_call(
        body, name=name,
        grid_spec=pltpu.PrefetchScalarGridSpec(
            num_scalar_prefetch=1, grid=(blocks.shape[1], T // tm),
            in_specs=[pl.BlockSpec((tm, D), lambda j, i, tab: (i, 0)), w_spec, w_spec] + [pl.BlockSpec(memory_space=pl.ANY)] * len(filled),
            out_specs=[o_spec] * 3),
        out_shape=[sds] * 3, input_output_aliases={4 + t: t for t in range(len(filled))},
        compiler_params=_params(("parallel", "parallel")),
    )(blocks, h, wg, wu, *filled)


def _rms_rows(x, gain):
    return x * lax.rsqrt(jnp.mean(x * x, axis=-1, keepdims=True) + EPS) * gain


def _residual_norm(name, a, w, xres, scale, gain, tm):
    def epilogue(accs, ex):
        y = ex[0] + scale * accs[0]
        return [y, _rms_rows(y, ex[1])]

    return _mm2d(name, a, w, "nn", [F32, BF16], tm=tm, tn=w.shape[1], tk=w.shape[0], n_outer=True, resident=True, extras=[xres],
                 rows=[gain], epilogue=epilogue)


def _ffn_down_loss(name, a, wd, xres, target):
    D = wd.shape[1]

    def epilogue(accs, ex):
        e = ex[0] + 0.5 * accs[0] - ex[1]
        d = e * (1.0 / D)
        return [d, 0.5 * d, jnp.zeros((8, 128), F32) + (0.5 / D) * jnp.sum(e * e)]

    return _mm2d(name, a, wd, "nn", [F32, BF16], tm=256, tn=D, tk=wd.shape[0], n_outer=True, resident=True, extras=[xres, target],
                 tile_stat=True, epilogue=epilogue)


def _ffn_bwd_down(tag, a, dyb, nf, after=None):
    return _mm2d(tag + "_dwd", a, dyb, "tn", F32, tm=a.shape[1] // nf, tn=1024, tk=a.shape[0], resident=True, after=after)


def _ffn_bwd_act(tag, dyb, wd, da_dg, da_du, nf, after=None):
    def act_bwd(accs, ex):
        return [accs[0] * ex[0].astype(F32), accs[0] * ex[1].astype(F32)]

    return _mm2d(tag + "_da", dyb, wd, "nt", BF16, tm=1024, tn=wd.shape[0] // nf, tk=2048, extras=[da_dg, da_du], epilogue=act_bwd, n_out=2,
                 n_outer=True, after=after)


def _ffn_bwd_gate_up(tag, h, dg, du, nf, after=None):
    T, D = h.shape
    F4 = dg.shape[1] // nf
    tm = _tile(D, 512)
    h_spec = pl.BlockSpec((T, tm), lambda j, i, k: (0, i))
    d_spec = pl.BlockSpec((T, F4), lambda j, i, k: (0, j), pipeline_mode=pl.Buffered(1))
    w_spec = pl.BlockSpec((None, tm, F4), lambda j, i, k: (j, i, 0))
    sds = jax.ShapeDtypeStruct((nf, D, F4), F32)
    return _mm(tag + "_dwgu", (nf, D // tm, 1), [(h, h_spec, dg, d_spec, "tn"), (None, None, du, d_spec, "tn")],
               [0, 1], [(tm, F4), (tm, F4)], [], [(sds, w_spec)] * 2, lambda accs, ex: accs, after=after)


def _ffn_bwd_x(tag, dg, du, wg, wu, after):
    T = dg.shape[0]
    nf, D, F4 = wg.shape
    tm, tn = _tile(T, 256), _tile(D, 1024)

    def body(dg_ref, du_ref, wg_ref, wu_ref, after_ref, o_ref):
        acc = None
        for j in range(nf):
            cols = slice(j * F4, (j + 1) * F4)
            part = _dot(dg_ref[:, cols], wg_ref[j], "nt") + _dot(du_ref[:, cols], wu_ref[j], "nt")
            acc = part if acc is None else acc + part
        o_ref[...] = acc

    a_spec = pl.BlockSpec((tm, nf * F4), lambda n, i: (i, 0))
    b_spec = pl.BlockSpec((nf, tn, F4), lambda n, i: (0, n, 0), pipeline_mode=pl.Buffered(1))
    return pl.pallas_call(
        body, name=tag + "_dh", grid=(D // tn, T // tm),
        in_specs=[a_spec, a_spec, b_spec, b_spec, pl.BlockSpec(memory_space=pl.ANY)],
        out_specs=pl.BlockSpec((tm, tn), lambda n, i: (i, n)), out_shape=jax.ShapeDtypeStruct((T, D), F32),
        compiler_params=_params(("parallel", "parallel")),
    )(dg, du, wg, wu, after)


def _rms_fwd(name, x, gain):
    R, D = x.shape
    tr = _tile(R, 256, 8)

    def body(x_ref, g_ref, o_ref):
        xv = x_ref[...]
        r = lax.rsqrt(jnp.mean(xv * xv, axis=-1, keepdims=True) + EPS)
        o_ref[...] = (xv * r * g_ref[...]).astype(BF16)

    return pl.pallas_call(
        body, name=name, grid=(R // tr,),
        in_specs=[pl.BlockSpec((tr, D), lambda i: (i, 0)), pl.BlockSpec((1, D), lambda i: (0, 0))],
        out_specs=pl.BlockSpec((tr, D), lambda i: (i, 0)), out_shape=jax.ShapeDtypeStruct((R, D), BF16),
        compiler_params=_params(("parallel",)),
    )(x, gain)


def _rms_bwd(name, dh, x, gain, dres, bscale):
    R, D = x.shape
    tr = _tile(R, 256, 8)

    def body(dh_ref, x_ref, g_ref, dres_ref, dx_ref, dxb_ref, dg_ref):
        xv, dy = x_ref[...], dh_ref[...]
        r = lax.rsqrt(jnp.mean(xv * xv, axis=-1, keepdims=True) + EPS)
        xn = xv * r
        uu = dy * g_ref[...]
        dx = dres_ref[...] + r * (uu - xn * jnp.mean(xn * uu, axis=-1, keepdims=True))
        dx_ref[...] = dx
        dxb_ref[...] = (bscale * dx).astype(BF16)
        part = jnp.sum(dy * xn, axis=0, keepdims=True)

        @pl.when(pl.program_id(0) == 0)
        def _():
            dg_ref[...] = part

        @pl.when(pl.program_id(0) > 0)
        def _():
            dg_ref[...] += part

    row = pl.BlockSpec((tr, D), lambda i: (i, 0))
    vec = pl.BlockSpec((1, D), lambda i: (0, 0))
    return pl.pallas_call(
        body, name=name, grid=(R // tr,), in_specs=[row, row, vec, row], out_specs=[row, row, vec],
        out_shape=[jax.ShapeDtypeStruct((R, D), F32), jax.ShapeDtypeStruct((R, D), BF16), jax.ShapeDtypeStruct((1, D), F32)],
        compiler_params=_params(("arbitrary",)),
    )(dh, x, gain, dres)


def _head_norm(xs, g):
    r = lax.rsqrt(jnp.mean(xs * xs, axis=-1, keepdims=True) + EPS)
    return xs * r * g


def _head_norm_bwd(xs, g, dy):
    r = lax.rsqrt(jnp.mean(xs * xs, axis=-1, keepdims=True) + EPS)
    xn = xs * r
    uu = dy * g
    return r * (uu - xn * jnp.mean(xn * uu, axis=-1, keepdims=True)), jnp.sum(dy * xn, axis=0, keepdims=True)


NORMED = [(C_FQ, FOX_HEADS, 0), (C_FK, FOX_HEADS, 1), (C_SQ, SWA_HEADS, 2), (C_SK, SWA_KV_HEADS, 3), (C_MQ, MEM_HEADS, 4)]
PLAIN = [(C_FV, FOX_HEADS), (C_SV, SWA_KV_HEADS)]


def _prep_fwd(proj, gains):
    T = proj.shape[0]
    tr = _tile(T, 256, 128)

    def body(p_ref, g_ref, o_ref, lf_ref, kt_ref, vt_ref):
        for start, heads, row in NORMED:
            gn = g_ref[row:row + 1, :]
            for hh in range(heads):
                sl = slice(start + hh * HEAD_DIM, start + (hh + 1) * HEAD_DIM)
                y = _head_norm(p_ref[:, sl], gn)
                o_ref[:, sl] = y.astype(BF16)
                if start == C_FK:
                    kt_ref[hh * HEAD_DIM:(hh + 1) * HEAD_DIM, :] = y.T.astype(BF16)
        for start, heads in PLAIN:
            sl = slice(start, start + heads * HEAD_DIM)
            o_ref[:, sl] = p_ref[:, sl].astype(BF16)
        for hh in range(FOX_HEADS):
            sl = slice(C_FV + hh * HEAD_DIM, C_FV + (hh + 1) * HEAD_DIM)
            vt_ref[hh * HEAD_DIM:(hh + 1) * HEAD_DIM, :] = p_ref[:, sl].T.astype(BF16)
        zb = p_ref[:, C_FL:C_FL + HEAD_DIM] + g_ref[5:6, :]
        o_ref[:, C_FL:C_FL + HEAD_DIM] = jnp.zeros((tr, HEAD_DIM), BF16)
        lf_ref[...] = jnp.minimum(zb, 0.0) - jnp.log(1.0 + jnp.exp(-jnp.abs(zb)))

    return pl.pallas_call(
        body, name="prep_fwd", grid=(T // tr,),
        in_specs=[pl.BlockSpec((tr, PROJ_W), lambda i: (i, 0)), pl.BlockSpec((8, 128), lambda i: (0, 0))],
        out_specs=[pl.BlockSpec((tr, PROJ_W), lambda i: (i, 0)), pl.BlockSpec((tr, HEAD_DIM), lambda i: (i, 0)),
                   pl.BlockSpec((FOX_W, tr), lambda i: (0, i)), pl.BlockSpec((FOX_W, tr), lambda i: (0, i))],
        out_shape=[jax.ShapeDtypeStruct((T, PROJ_W), BF16), jax.ShapeDtypeStruct((T, HEAD_DIM), F32),
                   jax.ShapeDtypeStruct((FOX_W, T), BF16), jax.ShapeDtypeStruct((FOX_W, T), BF16)],
        compiler_params=_params(("parallel",)),
    )(proj, gains)


def _prep_bwd(proj, gains, dfq, dfk, dfv, dsq, dsk, dsv, dmq, dlogf):
    T = proj.shape[0]
    tr = _tile(T, 256, 8)
    d_normed = {C_FQ: 0, C_FK: 1, C_SQ: 3, C_SK: 4, C_MQ: 6}
    d_plain = {C_FV: 2, C_SV: 5}

    def body(p_ref, g_ref, *rest):
        d_refs, dlf_ref, o_ref, dg_ref = rest[:7], rest[7], rest[8], rest[9]
        rows = []
        for start, heads, row in NORMED:
            gn = g_ref[row:row + 1, :]
            d_ref = d_refs[d_normed[start]]
            tot = jnp.zeros((1, HEAD_DIM), F32)
            for hh in range(heads):
                sl = slice(start + hh * HEAD_DIM, start + (hh + 1) * HEAD_DIM)
                dx, dgn = _head_norm_bwd(p_ref[:, sl], gn, d_ref[:, hh * HEAD_DIM:(hh + 1) * HEAD_DIM])
                o_ref[:, sl] = dx.astype(BF16)
                tot = tot + dgn
            rows.append(tot)
        for start, heads in PLAIN:
            o_ref[:, start:start + heads * HEAD_DIM] = d_refs[d_plain[start]][...].astype(BF16)
        zb = p_ref[:, C_FL:C_FL + HEAD_DIM] + g_ref[5:6, :]
        lane = lax.broadcasted_iota(jnp.int32, (tr, HEAD_DIM), 1)
        dz = jnp.where(lane < FOX_HEADS, dlf_ref[...] * (1.0 - _sigmoid(zb)), 0.0)
        o_ref[:, C_FL:C_FL + HEAD_DIM] = dz.astype(BF16)
        rows.append(jnp.sum(dz, axis=0, keepdims=True))
        part = jnp.concatenate(rows + [jnp.zeros((2, HEAD_DIM), F32)], axis=0)

        @pl.when(pl.program_id(0) == 0)
        def _():
            dg_ref[...] = part

        @pl.when(pl.program_id(0) > 0)
        def _():
            dg_ref[...] += part

    def rows_of(w):
        return pl.BlockSpec((tr, w), lambda i: (i, 0))

    small = pl.BlockSpec((8, 128), lambda i: (0, 0))
    ds = [dfq, dfk, dfv, dsq, dsk, dsv, dmq]
    return pl.pallas_call(
        body, name="prep_bwd", grid=(T // tr,),
        in_specs=[rows_of(PROJ_W), small] + [rows_of(d.shape[1]) for d in ds] + [rows_of(HEAD_DIM)],
        out_specs=[rows_of(PROJ_W), small],
        out_shape=[jax.ShapeDtypeStruct((T, PROJ_W), BF16), jax.ShapeDtypeStruct((8, 128), F32)],
        compiler_params=_params(("arbitrary",)),
    )(proj, gains, *ds, dlogf)


def _head_norm_rows(x, gain):
    R, W = x.shape

    def body(x_ref, g_ref, o_ref):
        for hh in range(W // HEAD_DIM):
            sl = slice(hh * HEAD_DIM, (hh + 1) * HEAD_DIM)
            o_ref[:, sl] = _head_norm(x_ref[:, sl], g_ref[...]).astype(BF16)

    return pl.pallas_call(body, name="mem_k_norm", out_shape=jax.ShapeDtypeStruct((R, W), BF16))(x, gain)


def _head_norm_rows_bwd(x, gain, dy):
    R, W = x.shape

    def body(x_ref, g_ref, dy_ref, dx_ref, dg_ref):
        tot = jnp.zeros((1, HEAD_DIM), F32)
        for hh in range(W // HEAD_DIM):
            sl = slice(hh * HEAD_DIM, (hh + 1) * HEAD_DIM)
            dx, dgn = _head_norm_bwd(x_ref[:, sl], g_ref[...], dy_ref[:, sl])
            dx_ref[:, sl] = dx.astype(BF16)
            tot = tot + dgn
        dg_ref[...] = tot

    return pl.pallas_call(
        body, name="mem_k_norm_bwd",
        out_shape=[jax.ShapeDtypeStruct((R, W), BF16), jax.ShapeDtypeStruct((1, HEAD_DIM), F32)])(x, gain, dy)


def _cumsum_rows(name, xs, reverse, columns=None):
    T, W = xs[0].shape
    tb = _tile(T, 512, 8)
    nb = T // tb
    n_in = len(xs) + (0 if columns is None else 1)

    def body(*refs):
        o_ref, carry = refs[n_in], refs[n_in + 1]

        @pl.when(pl.program_id(0) == 0)
        def _():
            carry[...] = jnp.zeros_like(carry)

        xv = refs[0][...]
        for x_ref in refs[1:len(xs)]:
            xv = xv + x_ref[...]
        if columns is not None:
            lane = lax.broadcasted_iota(jnp.int32, (tb, W), 1)
            for hh in range(columns.shape[0]):
                xv = xv + jnp.where(lane == hh, refs[len(xs)][hh], 0.0)
        r = lax.broadcasted_iota(jnp.int32, (tb, tb), 0)
        cc = lax.broadcasted_iota(jnp.int32, (tb, tb), 1)
        tri = jnp.where((cc >= r) if reverse else (cc <= r), 1.0, 0.0).astype(F32)
        o_ref[...] = jnp.dot(tri, xv, precision=lax.Precision.HIGHEST, preferred_element_type=F32) + carry[...]
        carry[...] += jnp.sum(xv, axis=0, keepdims=True)

    idx = (lambda i: (nb - 1 - i, 0)) if reverse else (lambda i: (i, 0))
    in_specs = [pl.BlockSpec((tb, W), idx)] * len(xs)
    if columns is not None:
        in_specs.append(pl.BlockSpec((columns.shape[0], tb, 1), lambda i: (0, idx(i)[0], 0)))
    return pl.pallas_call(
        body, name=name, grid=(nb,), in_specs=in_specs, out_specs=pl.BlockSpec((tb, W), idx),
        out_shape=jax.ShapeDtypeStruct((T, W), F32), scratch_shapes=[pltpu.VMEM((1, W), F32)],
        compiler_params=_params(("arbitrary",)),
    )(*xs, *([] if columns is None else [columns]))


def _triangle(nq, by_column):
    if by_column:
        blocks = [(i, j) for j in range(nq) for i in range(j, nq)]
    else:
        blocks = [(i, j) for i in range(nq) for j in range(i + 1)]
    return jnp.asarray(np.array(blocks, np.int32).T)


def _fox_scores_t(k, q, cq_row, ck_rep, on_diagonal):
    n = q.shape[0]
    s = _dot(k, q, "nt") * SCALE + (cq_row - jnp.tile(ck_rep, (1, n // HEAD_DIM)))
    if on_diagonal:
        s = jnp.where(lax.broadcasted_iota(jnp.int32, (n, n), 0) <= lax.broadcasted_iota(jnp.int32, (n, n), 1), s, NEG_INF)
    return s


def _fox_fwd(qkv, v_t, cq_row, ck_rep):
    T = qkv.shape[0]
    tq = _tile(T, 1024)
    nq = T // tq
    steps = nq * (nq + 1) // 2
    HQ, HK = C_FQ // HEAD_DIM, C_FK // HEAD_DIM

    def body(tab, q_ref, k_ref, vt_ref, cq_ref, ck_ref, o_ref, of_ref, lse_ref, m_sc, l_sc, acc_sc):
        i, j = tab[0, pl.program_id(1)], tab[1, pl.program_id(1)]

        @pl.when(j == 0)
        def _():
            m_sc[...] = jnp.full_like(m_sc, NEG_INF)
            l_sc[...] = jnp.zeros_like(l_sc)
            acc_sc[...] = jnp.zeros_like(acc_sc)

        def step(on_diagonal):
            s = _fox_scores_t(k_ref[...], q_ref[...], cq_ref[...], ck_ref[...], on_diagonal)
            m_new = jnp.maximum(m_sc[...], jnp.max(s, axis=0, keepdims=True))
            alpha = jnp.exp(m_sc[...] - m_new)
            p = jnp.exp(s - m_new)
            l_sc[...] = alpha * l_sc[...] + jnp.sum(p, axis=0, keepdims=True)
            acc_sc[...] = alpha * acc_sc[...] + _dot(vt_ref[...], p.astype(BF16), "nn")
            m_sc[...] = m_new

        @pl.when(j < i)
        def _():
            step(False)

        @pl.when(j == i)
        def _():
            step(True)
            o = (acc_sc[...] / l_sc[...]).T
            o_ref[...] = o.astype(BF16)
            of_ref[...] = o
            lse_ref[...] = m_sc[...] + jnp.log(l_sc[...])

    qrow = pl.BlockSpec((None, 1, tq), lambda h, s, tab: (h, 0, tab[0, s]))
    return pl.pallas_call(
        body, name="fox_fwd",
        grid_spec=pltpu.PrefetchScalarGridSpec(
            num_scalar_prefetch=1, grid=(FOX_HEADS, steps),
            in_specs=[pl.BlockSpec((tq, HEAD_DIM), lambda h, s, tab: (tab[0, s], HQ + h)),
                      pl.BlockSpec((tq, HEAD_DIM), lambda h, s, tab: (tab[1, s], HK + h)),
                      pl.BlockSpec((HEAD_DIM, tq), lambda h, s, tab: (h, tab[1, s])), qrow,
                      pl.BlockSpec((None, tq, HEAD_DIM), lambda h, s, tab: (h, tab[1, s], 0))],
            out_specs=[pl.BlockSpec((tq, HEAD_DIM), lambda h, s, tab: (tab[0, s], h)),
                       pl.BlockSpec((tq, HEAD_DIM), lambda h, s, tab: (tab[0, s], h)), qrow],
            scratch_shapes=[pltpu.VMEM((1, tq), F32), pltpu.VMEM((1, tq), F32), pltpu.VMEM((HEAD_DIM, tq), F32)]),
        out_shape=[jax.ShapeDtypeStruct((T, FOX_W), BF16), jax.ShapeDtypeStruct((T, FOX_W), F32),
                   jax.ShapeDtypeStruct((FOX_HEADS, 1, T), F32)],
        compiler_params=_params(("parallel", "arbitrary")),
    )(_triangle(nq, False), qkv, qkv, v_t, cq_row, ck_rep)


def _fox_delta(dmix, out_f32):
    T = out_f32.shape[0]
    tr = _tile(T, 512, 8)

    def body(do_ref, o_ref, d_ref):
        lane = lax.broadcasted_iota(jnp.int32, (tr, HEAD_DIM), 1)
        acc = jnp.zeros((tr, HEAD_DIM), F32)
        for hh in range(FOX_HEADS):
            sl = slice(hh * HEAD_DIM, (hh + 1) * HEAD_DIM)
            d = jnp.sum(do_ref[:, sl].astype(F32) * o_ref[:, sl], axis=-1, keepdims=True)
            acc = jnp.where(lane == hh, d, acc)
        d_ref[...] = acc

    blk = pl.BlockSpec((tr, FOX_W), lambda i: (i, 0))
    return pl.pallas_call(
        body, name="fox_delta", grid=(T // tr,), in_specs=[blk, blk], out_specs=pl.BlockSpec((tr, HEAD_DIM), lambda i: (i, 0)),
        out_shape=jax.ShapeDtypeStruct((T, HEAD_DIM), F32), compiler_params=_params(("parallel",)),
    )(dmix, out_f32)


def _fox_bwd(qkv, k_t, cq_row, ck_rep, delta_row, lse, dmix):
    T = qkv.shape[0]
    tq = _tile(T, 1024)
    nq = T // tq
    steps = nq * (nq + 1) // 2
    HQ, HK, HV = C_FQ // HEAD_DIM, C_FK // HEAD_DIM, C_FV // HEAD_DIM

    def body(tab, q_ref, k_ref, kt_ref, v_ref, cq_ref, ck_ref, delta_ref, lse_ref, do_ref,
             dq_ref, dk_ref, dv_ref, dck_ref, dcq_ref, dk_sc, dv_sc, dc_sc, dqt_sc):
        qi, kj = tab[0, pl.program_id(1)], tab[1, pl.program_id(1)]

        @pl.when(qi == kj)
        def _():
            dk_sc[...] = jnp.zeros_like(dk_sc)
            dv_sc[...] = jnp.zeros_like(dv_sc)
            dc_sc[...] = jnp.zeros_like(dc_sc)

        def step(on_diagonal):
            q, k, v, do = q_ref[...], k_ref[...], v_ref[...], do_ref[...]
            p = jnp.exp(_fox_scores_t(k, q, cq_ref[...], ck_ref[...], on_diagonal) - lse_ref[...])
            dp = _dot(v, do, "nt")
            ds = p * (dp - delta_ref[...])
            dsb = ds.astype(BF16)
            dv_sc[...] += _dot(p.astype(BF16), do, "nn")
            dk_sc[...] += _dot(dsb, q, "nn")
            dc_sc[...] += jnp.sum(ds, axis=1, keepdims=True)
            dq_part = _dot(kt_ref[...], dsb, "nn") * SCALE
            dcq_part = jnp.sum(ds, axis=0, keepdims=True)

            @pl.when(kj == 0)
            def _():
                dqt_sc[qi] = dq_part
                dcq_ref[qi] = dcq_part

            @pl.when(kj > 0)
            def _():
                dqt_sc[qi] += dq_part
                dcq_ref[qi] += dcq_part

            if on_diagonal:
                dq_ref[...] = dqt_sc[qi].T

        @pl.when(qi > kj)
        def _():
            step(False)

        @pl.when(qi == kj)
        def _():
            step(True)

        @pl.when(qi == nq - 1)
        def _():
            dk_ref[...] = dk_sc[...] * SCALE
            dv_ref[...] = dv_sc[...]
            dck_ref[...] = -dc_sc[...]

    def rows(base):
        return pl.BlockSpec((tq, HEAD_DIM), lambda h, s, tab: (tab[0, s], base + h))

    def cols(base):
        return pl.BlockSpec((tq, HEAD_DIM), lambda h, s, tab: (tab[1, s], base + h))

    qrow = pl.BlockSpec((None, 1, tq), lambda h, s, tab: (h, 0, tab[0, s]))
    sds = jax.ShapeDtypeStruct((T, FOX_W), F32)
    return pl.pallas_call(
        body, name="fox_bwd",
        grid_spec=pltpu.PrefetchScalarGridSpec(
            num_scalar_prefetch=1, grid=(FOX_HEADS, steps),
            in_specs=[rows(HQ), cols(HK), pl.BlockSpec((HEAD_DIM, tq), lambda h, s, tab: (h, tab[1, s])), cols(HV), qrow,
                      pl.BlockSpec((None, tq, HEAD_DIM), lambda h, s, tab: (h, tab[1, s], 0)), qrow, qrow, rows(0)],
            out_specs=[cols(0), cols(0), cols(0), pl.BlockSpec((None, tq, 1), lambda h, s, tab: (h, tab[1, s], 0)),
                       pl.BlockSpec((None, nq, 1, tq), lambda h, s, tab: (h, 0, 0, 0))],
            scratch_shapes=[pltpu.VMEM((tq, HEAD_DIM), F32), pltpu.VMEM((tq, HEAD_DIM), F32), pltpu.VMEM((tq, 1), F32),
                            pltpu.VMEM((nq, HEAD_DIM, tq), F32)]),
        out_shape=[sds, sds, sds, jax.ShapeDtypeStruct((FOX_HEADS, T, 1), F32), jax.ShapeDtypeStruct((FOX_HEADS, nq, 1, tq), F32)],
        compiler_params=_params(("parallel", "arbitrary")),
    )(_triangle(nq, True), qkv, qkv, k_t, qkv, cq_row, ck_rep, delta_row, lse, dmix)


GW = SWA_GROUP * HEAD_DIM
GR = SWA_GROUP * WINDOW


def _swa_bias_table():
    slopes = 2.0 ** (-8.0 * np.arange(1, SWA_HEADS + 1) / SWA_HEADS)
    row = np.arange(GR)
    dist = WINDOW + (row % WINDOW)[:, None] - np.arange(2 * WINDOW)[None, :]
    slope = slopes.reshape(SWA_KV_HEADS, SWA_GROUP)[:, row // WINDOW]
    table = np.where((dist >= 0) & (dist < WINDOW), -slope[:, :, None] * dist[None], NEG_INF)
    return jnp.asarray(table, F32)


def _swa_scores(q_ref, kp_ref, kc_ref, bias_ref, n):
    q = q_ref[...]
    qs = jnp.concatenate([q[:, t * HEAD_DIM:(t + 1) * HEAD_DIM] for t in range(SWA_GROUP)], axis=0)
    kb = jnp.concatenate([kp_ref[...], kc_ref[...]], axis=0)
    jj = lax.broadcasted_iota(jnp.int32, (GR, 2 * WINDOW), 1)
    s = _dot(qs, kb, "nt") * SCALE + bias_ref[...]
    return qs, kb, jnp.where((n > 0) | (jj >= WINDOW), s, NEG_INF)


def _swa_specs():
    HQ, HK, HV = C_SQ // GW, C_SK // HEAD_DIM, C_SV // HEAD_DIM
    q_spec = pl.BlockSpec((WINDOW, GW), lambda g, n: (n, HQ + g))

    def prev(base):
        return pl.BlockSpec((WINDOW, HEAD_DIM), lambda g, n: (jnp.maximum(n - 1, 0), base + g))

    def cur(base):
        return pl.BlockSpec((WINDOW, HEAD_DIM), lambda g, n: (n, base + g))

    col = pl.BlockSpec((None, GR, 1), lambda g, n: (g, 0, 0))
    table = pl.BlockSpec((None, GR, 2 * WINDOW), lambda g, n: (g, 0, 0))
    return q_spec, prev(HK), cur(HK), prev(HV), cur(HV), col, table


def _swa_fwd(qkv, slopes, sinks):
    T = qkv.shape[0]
    nb = T // WINDOW
    assert C_SQ % GW == 0

    def body(q_ref, kp_ref, kc_ref, vp_ref, vc_ref, slope_ref, sink_ref, o_ref, lse_ref):
        n = pl.program_id(1)
        _, _, s = _swa_scores(q_ref, kp_ref, kc_ref, slope_ref, n)
        m = jnp.maximum(jnp.max(s, axis=-1, keepdims=True), sink_ref[...])
        p = jnp.exp(s - m)
        l = jnp.sum(p, axis=-1, keepdims=True) + jnp.exp(sink_ref[...] - m)
        vb = jnp.concatenate([vp_ref[...], vc_ref[...]], axis=0)
        o = _dot(p.astype(BF16), vb, "nn") / l
        for t in range(SWA_GROUP):
            o_ref[:, t * HEAD_DIM:(t + 1) * HEAD_DIM] = o[t * WINDOW:(t + 1) * WINDOW, :].astype(BF16)
        lse_ref[...] = m + jnp.log(l)

    q_spec, kp, kc, vp, vc, col, table = _swa_specs()
    return pl.pallas_call(
        body, name="swa_fwd", grid=(SWA_KV_HEADS, nb), in_specs=[q_spec, kp, kc, vp, vc, table, col],
        out_specs=[pl.BlockSpec((WINDOW, GW), lambda g, n: (n, g)), pl.BlockSpec((None, None, GR, 1), lambda g, n: (g, n, 0, 0))],
        out_shape=[jax.ShapeDtypeStruct((T, SWA_HEADS * HEAD_DIM), BF16), jax.ShapeDtypeStruct((SWA_KV_HEADS, nb, GR, 1), F32)],
        compiler_params=_params(("parallel", "arbitrary")),
    )(qkv, qkv, qkv, qkv, qkv, slopes, sinks)


def _swa_bwd(qkv, slopes, sinks, out, lse, dmix):
    T = qkv.shape[0]
    nb = T // WINDOW
    DO = FOX_W // GW
    assert FOX_W % GW == 0

    def body(q_ref, kp_ref, kc_ref, vp_ref, vc_ref, slope_ref, sink_ref, o_ref, lse_ref, do_ref,
             dq_ref, dk_ref, dv_ref, dsink_ref, sink_sc):
        n = pl.program_id(1)

        @pl.when(n == 0)
        def _():
            dk_ref[...] = jnp.zeros_like(dk_ref)
            dv_ref[...] = jnp.zeros_like(dv_ref)
            sink_sc[...] = jnp.zeros_like(sink_sc)

        qs, kb, s = _swa_scores(q_ref, kp_ref, kc_ref, slope_ref, n)
        lse = lse_ref[...]
        p = jnp.exp(s - lse)
        vb = jnp.concatenate([vp_ref[...], vc_ref[...]], axis=0)
        do = jnp.concatenate([do_ref[:, t * HEAD_DIM:(t + 1) * HEAD_DIM] for t in range(SWA_GROUP)], axis=0)
        oo = jnp.concatenate([o_ref[:, t * HEAD_DIM:(t + 1) * HEAD_DIM] for t in range(SWA_GROUP)], axis=0)
        dp = _dot(do, vb, "nt")
        delta = jnp.sum(do.astype(F32) * oo.astype(F32), axis=-1, keepdims=True)
        ds = p * (dp - delta)
        dsb = ds.astype(BF16)
        dq = _dot(dsb, kb, "nn") * SCALE
        for t in range(SWA_GROUP):
            dq_ref[:, t * HEAD_DIM:(t + 1) * HEAD_DIM] = dq[t * WINDOW:(t + 1) * WINDOW, :]
        dkb = _dot(dsb, qs, "tn") * SCALE
        dvb = _dot(p.astype(BF16), do, "tn")
        r_prev = pl.ds(pl.multiple_of(jnp.maximum(n - 1, 0) * WINDOW, WINDOW), WINDOW)
        r_cur = pl.ds(pl.multiple_of(n * WINDOW, WINDOW), WINDOW)
        dk_ref[r_prev, :] += dkb[:WINDOW, :]
        dk_ref[r_cur, :] += dkb[WINDOW:, :]
        dv_ref[r_prev, :] += dvb[:WINDOW, :]
        dv_ref[r_cur, :] += dvb[WINDOW:, :]
        sink_sc[...] -= jnp.exp(sink_ref[...] - lse) * delta

        @pl.when(n == nb - 1)
        def _():
            tot = [jnp.zeros((1, 128), F32) + jnp.sum(sink_sc[t * WINDOW:(t + 1) * WINDOW, :]) for t in range(SWA_GROUP)]
            dsink_ref[...] = jnp.concatenate(tot + [jnp.zeros((8 - SWA_GROUP, 128), F32)], axis=0)

    q_spec, kp, kc, vp, vc, col, table = _swa_specs()
    kv_acc = pl.BlockSpec((T, HEAD_DIM), lambda g, n: (0, g))
    return pl.pallas_call(
        body, name="swa_bwd", grid=(SWA_KV_HEADS, nb),
        in_specs=[q_spec, kp, kc, vp, vc, table, col, pl.BlockSpec((WINDOW, GW), lambda g, n: (n, g)),
                  pl.BlockSpec((None, None, GR, 1), lambda g, n: (g, n, 0, 0)), pl.BlockSpec((WINDOW, GW), lambda g, n: (n, DO + g))],
        out_specs=[pl.BlockSpec((WINDOW, GW), lambda g, n: (n, g)), kv_acc, kv_acc, pl.BlockSpec((None, 8, 128), lambda g, n: (g, 0, 0))],
        out_shape=[jax.ShapeDtypeStruct((T, SWA_HEADS * HEAD_DIM), F32), jax.ShapeDtypeStruct((T, SWA_KV_HEADS * HEAD_DIM), F32),
                   jax.ShapeDtypeStruct((T, SWA_KV_HEADS * HEAD_DIM), F32), jax.ShapeDtypeStruct((SWA_KV_HEADS, 8, 128), F32)],
        scratch_shapes=[pltpu.VMEM((GR, 1), F32)],
        compiler_params=_params(("parallel", "arbitrary")),
    )(qkv, qkv, qkv, qkv, qkv, slopes, sinks, out, lse, dmix)


def _mem_fwd(qkv, mk, mv):
    T, ML = qkv.shape[0], mk.shape[0]
    tq = _tile(T, 1024)
    HQ = C_MQ // HEAD_DIM

    def body(q_ref, k_ref, v_ref, o_ref, lse_ref):
        s = _dot(q_ref[...], k_ref[...], "nt") * SCALE
        m = jnp.max(s, axis=-1, keepdims=True)
        p = jnp.exp(s - m)
        l = jnp.sum(p, axis=-1, keepdims=True)
        o_ref[...] = (_dot(p.astype(BF16), v_ref[...], "nn") / l).astype(BF16)
        lse_ref[...] = m + jnp.log(l)

    kv = pl.BlockSpec((ML, HEAD_DIM), lambda h, i: (0, h))
    return pl.pallas_call(
        body, name="mem_fwd", grid=(MEM_HEADS, T // tq),
        in_specs=[pl.BlockSpec((tq, HEAD_DIM), lambda h, i: (i, HQ + h)), kv, kv],
        out_specs=[pl.BlockSpec((tq, HEAD_DIM), lambda h, i: (i, h)), pl.BlockSpec((None, tq, 1), lambda h, i: (h, i, 0))],
        out_shape=[jax.ShapeDtypeStruct((T, MEM_HEADS * HEAD_DIM), BF16), jax.ShapeDtypeStruct((MEM_HEADS, T, 1), F32)],
        compiler_params=_params(("parallel", "arbitrary")),
    )(qkv, mk, mv)


def _mem_bwd(qkv, mk, mv, out, lse, dmix):
    T, ML = qkv.shape[0], mk.shape[0]
    tq = _tile(T, 1024)
    HQ = C_MQ // HEAD_DIM
    DO = (FOX_W + SWA_HEADS * HEAD_DIM) // HEAD_DIM

    def body(q_ref, k_ref, v_ref, o_ref, lse_ref, do_ref, dq_ref, dk_ref, dv_ref):
        q, k, v, do = q_ref[...], k_ref[...], v_ref[...], do_ref[...]
        p = jnp.exp(_dot(q, k, "nt") * SCALE - lse_ref[...])
        dp = _dot(do, v, "nt")
        delta = jnp.sum(do.astype(F32) * o_ref[...].astype(F32), axis=-1, keepdims=True)
        dsb = (p * (dp - delta)).astype(BF16)
        dq_ref[...] = _dot(dsb, k, "nn") * SCALE
        dk_part = _dot(dsb, q, "tn") * SCALE
        dv_part = _dot(p.astype(BF16), do, "tn")

        @pl.when(pl.program_id(1) == 0)
        def _():
            dk_ref[...] = dk_part
            dv_ref[...] = dv_part

        @pl.when(pl.program_id(1) > 0)
        def _():
            dk_ref[...] += dk_part
            dv_ref[...] += dv_part

    kv = pl.BlockSpec((ML, HEAD_DIM), lambda h, i: (0, h))
    qb = pl.BlockSpec((tq, HEAD_DIM), lambda h, i: (i, h))
    return pl.pallas_call(
        body, name="mem_bwd", grid=(MEM_HEADS, T // tq),
        in_specs=[pl.BlockSpec((tq, HEAD_DIM), lambda h, i: (i, HQ + h)), kv, kv, qb,
                  pl.BlockSpec((None, tq, 1), lambda h, i: (h, i, 0)), pl.BlockSpec((tq, HEAD_DIM), lambda h, i: (i, DO + h))],
        out_specs=[qb, kv, kv],
        out_shape=[jax.ShapeDtypeStruct((T, MEM_HEADS * HEAD_DIM), F32), jax.ShapeDtypeStruct((ML, MEM_HEADS * HEAD_DIM), F32),
                   jax.ShapeDtypeStruct((ML, MEM_HEADS * HEAD_DIM), F32)],
        compiler_params=_params(("parallel", "arbitrary")),
    )(qkv, mk, mv, out, lse, dmix)


HBM = pl.BlockSpec(memory_space=pltpu.HBM)


def _place():
    x, y, c = lax.axis_index("x"), lax.axis_index("y"), lax.axis_index("c")
    chips = [(1 - x, y), (x, 1 - y), (1 - x, 1 - y)]
    return x, y, c, chips


def _remote(src, dst, send_sem, recv_sem, device):
    return pltpu.make_async_remote_copy(src_ref=src, dst_ref=dst, send_sem=send_sem, recv_sem=recv_sem,
                                        device_id=device, device_id_type=MESH)


def _place_ids():
    x, y, c = lax.axis_index("x"), lax.axis_index("y"), lax.axis_index("c")
    order = [2 * x + y, 2 * (1 - x) + y, 2 * x + (1 - y), 2 * (1 - x) + (1 - y)]
    return jnp.stack([2 * x + y, c] + order).astype(jnp.int32)


def _cast_place(name, w, ids, *after, keep_own=False):
    R, C = w.shape
    tr = _tile(R, 256, 16)
    n_out = 2 if keep_own else 1

    def body(ids_ref, w_ref, *rest):
        for o_ref in rest[-n_out:]:
            o_ref[...] = w_ref[...].astype(BF16)

    res = pl.pallas_call(
        body, name=name,
        grid_spec=pltpu.PrefetchScalarGridSpec(
            num_scalar_prefetch=1, grid=(R // tr,),
            in_specs=[pl.BlockSpec((tr, C), lambda i, ids: (i, 0))] + [pl.BlockSpec(memory_space=pl.ANY)] * len(after),
            out_specs=[pl.BlockSpec((None, tr, C), lambda i, ids: (ids[0], i, 0)), pl.BlockSpec((tr, C), lambda i, ids: (i, 0))][:n_out]),
        out_shape=[jax.ShapeDtypeStruct((N_CHIPS, R, C), BF16), jax.ShapeDtypeStruct((R, C), BF16)][:n_out],
        compiler_params=_params(("parallel",)),
    )(ids, w, *after)
    return res if keep_own else res[0]


SEM = pl.BlockSpec(memory_space=pltpu.SEMAPHORE)
EFFECT = pltpu.SideEffectType.DATAFLOW_SIDE_EFFECTING


def _hbm(a):
    return pltpu.with_memory_space_constraint(a, pltpu.HBM)


def _gather_start(name, placed, after, to_sibling=False):
    n = len(placed)

    ns = 3 * n

    def body(*refs):
        send, recv = refs[n + 1:n + 1 + ns], refs[n + 1 + ns:n + 1 + 2 * ns]
        buf = refs[n + 1 + 2 * ns:2 * n + 1 + 2 * ns]
        token = refs[2 * n + 1 + 2 * ns]
        x, y, c, chips = _place()
        me = 2 * x + y
        for a in range(n):
            half = buf[a].shape[1] // 2
            for j, (cx, cy) in enumerate(chips):
                block, peer = (2 * cx + cy, (x, y, 1 - c)) if to_sibling else (me, (cx, cy, c))
                part = buf[a].at[block, pl.ds(c * half, half)]
                _remote(part, part, send[3 * a + j], recv[3 * a + j], peer).start()
        token[...] = jnp.zeros_like(token)

    res = pl.pallas_call(
        body, name=name, in_specs=[HBM] * n + [pl.BlockSpec(memory_space=pl.ANY)],
        out_specs=[SEM] * (2 * ns) + [HBM] * n + [pl.BlockSpec(memory_space=pltpu.VMEM)],
        out_shape=[pltpu.SemaphoreType.DMA(())] * (2 * ns)
        + [pltpu.HBM(s.shape, s.dtype) for s in placed] + [jax.ShapeDtypeStruct((8, 128), F32)],
        input_output_aliases={a: 2 * ns + a for a in range(n)},
        compiler_params=pltpu.CompilerParams(has_side_effects=EFFECT),
    )(*[_hbm(s) for s in placed], after)
    return list(res[:ns]), list(res[ns:2 * ns]), list(res[2 * ns:2 * ns + n]), res[2 * ns + n]


def _gather_wait(name, send, recv, bufs, after, to_sibling=False):
    n = len(bufs)

    ns = 3 * n

    def body(*refs):
        buf = refs[:n]
        send_ref, recv_ref = refs[n:n + ns], refs[n + ns:n + 2 * ns]
        x, y, c, chips = _place()
        ids = [2 * cx + cy for cx, cy in chips]
        for a in range(n):
            half = buf[a].shape[1] // 2
            for j in range(3):
                sent = buf[a].at[ids[j], pl.ds(c * half, half)]
                landed = buf[a].at[ids[j], pl.ds((1 - c) * half, half)] if to_sibling else sent
                cp = _remote(sent, landed, send_ref[3 * a + j], recv_ref[3 * a + j], (x, y, c))
                cp.wait_send()
                cp.wait_recv()

    res = pl.pallas_call(
        body, name=name, in_specs=[HBM] * n + [SEM] * (2 * ns) + [pl.BlockSpec(memory_space=pl.ANY)], out_specs=[HBM] * n,
        out_shape=[pltpu.HBM(s.shape, s.dtype) for s in bufs], input_output_aliases={a: a for a in range(n)},
        compiler_params=pltpu.CompilerParams(has_side_effects=EFFECT),
    )(*bufs, *send, *recv, after)
    return list(res)


def _gather_forward(name, bufs):
    n = len(bufs)

    def body(*refs):
        buf = refs[n:2 * n]
        send, recv = refs[2 * n:]
        x, y, c, chips = _place()
        ids = [2 * cx + cy for cx, cy in chips]
        copies = []
        for a in range(n):
            half = buf[a].shape[1] // 2
            for j in range(3):
                landed = buf[a].at[ids[j], pl.ds(c * half, half)]
                cp = _remote(landed, landed, send.at[a, j], recv.at[a, j], (x, y, 1 - c))
                cp.start()
                copies.append(cp)
        for a in range(n):
            half = buf[a].shape[1] // 2
            for j in range(3):
                landed = buf[a].at[ids[j], pl.ds((1 - c) * half, half)]
                _remote(landed, landed, send.at[a, j], recv.at[a, j], (x, y, c)).wait_recv()
        for cp in copies:
            cp.wait_send()

    return pl.pallas_call(
        body, name=name, in_specs=[HBM] * n, out_specs=[HBM] * n,
        out_shape=[jax.ShapeDtypeStruct(s.shape, s.dtype) for s in bufs], input_output_aliases={a: a for a in range(n)},
        scratch_shapes=[pltpu.SemaphoreType.DMA((n, 3)), pltpu.SemaphoreType.DMA((n, 3))],
    )(*bufs)


def _pair_start(name, grads):
    n = len(grads)
    ns = N_CHIPS * n

    def body(*refs):
        send, recv = refs[2 * n:2 * n + ns], refs[2 * n + ns:2 * n + 2 * ns]
        src = refs[2 * n + 2 * ns:3 * n + 2 * ns]
        land = refs[3 * n + 2 * ns:4 * n + 2 * ns]
        token = refs[4 * n + 2 * ns]
        x, y, c, chips = _place()
        order = [2 * x + y] + [2 * cx + cy for cx, cy in chips]
        for a in range(n):
            half = src[a].shape[1] // 2
            for j in range(N_CHIPS):
                _remote(src[a].at[order[j], pl.ds((1 - c) * half, half)], land[a].at[j],
                        send[N_CHIPS * a + j], recv[N_CHIPS * a + j], (x, y, 1 - c)).start()
        token[...] = jnp.zeros_like(token)

    lands = [jax.ShapeDtypeStruct((N_CHIPS, g.shape[1] // 2, g.shape[2]), g.dtype) for g in grads]
    res = pl.pallas_call(
        body, name=name, in_specs=[HBM] * (2 * n),
        out_specs=[SEM] * (2 * ns) + [HBM] * (2 * n) + [pl.BlockSpec(memory_space=pltpu.VMEM)],
        out_shape=[pltpu.SemaphoreType.DMA(())] * (2 * ns) + [pltpu.HBM(g.shape, g.dtype) for g in grads]
        + [pltpu.HBM(l.shape, l.dtype) for l in lands] + [jax.ShapeDtypeStruct((8, 128), F32)],
        input_output_aliases={a: 2 * ns + a for a in range(2 * n)},
        compiler_params=pltpu.CompilerParams(has_side_effects=EFFECT),
    )(*[_hbm(g) for g in grads], *[_hbm(lax.empty(l.shape, l.dtype)) for l in lands])
    return list(res[:ns]), list(res[ns:2 * ns]), list(res[2 * ns:2 * ns + n]), list(res[2 * ns + n:2 * ns + 2 * n]), res[2 * ns + 2 * n]


def _pair_wait(name, send, recv, grads, lands, after):
    n = len(grads)
    ns = N_CHIPS * n

    def body(*refs):
        src, land = refs[:n], refs[n:2 * n]
        send_ref, recv_ref = refs[2 * n:2 * n + ns], refs[2 * n + ns:2 * n + 2 * ns]
        x, y, c, _ = _place()
        for a in range(n):
            for j in range(N_CHIPS):
                cp = _remote(land[a].at[j], land[a].at[j], send_ref[N_CHIPS * a + j], recv_ref[N_CHIPS * a + j], (x, y, c))
                cp.wait_send()
                cp.wait_recv()

    res = pl.pallas_call(
        body, name=name, in_specs=[HBM] * (2 * n) + [SEM] * (2 * ns) + [pl.BlockSpec(memory_space=pl.ANY)],
        out_specs=[HBM] * (2 * n), out_shape=[pltpu.HBM(g.shape, g.dtype) for g in grads] + [pltpu.HBM(l.shape, l.dtype) for l in lands],
        input_output_aliases={a: a for a in range(2 * n)},
        compiler_params=pltpu.CompilerParams(has_side_effects=EFFECT),
    )(*grads, *lands, *send, *recv, after)
    return list(res[:n]), list(res[n:])


def _chip_start(name, parts):
    n = len(parts)
    ns = 3 * n

    def body(*refs):
        send, recv = refs[2 * n:2 * n + ns], refs[2 * n + ns:2 * n + 2 * ns]
        src = refs[2 * n + 2 * ns:3 * n + 2 * ns]
        land = refs[3 * n + 2 * ns:4 * n + 2 * ns]
        token = refs[4 * n + 2 * ns]
        x, y, c, chips = _place()
        for a in range(n):
            for j, (cx, cy) in enumerate(chips):
                _remote(src[a].at[j], land[a].at[j], send[3 * a + j], recv[3 * a + j], (cx, cy, c)).start()
        token[...] = jnp.zeros_like(token)

    res = pl.pallas_call(
        body, name=name, in_specs=[HBM] * (2 * n),
        out_specs=[SEM] * (2 * ns) + [HBM] * (2 * n) + [pl.BlockSpec(memory_space=pltpu.VMEM)],
        out_shape=[pltpu.SemaphoreType.DMA(())] * (2 * ns) + [pltpu.HBM(p.shape, p.dtype) for p in parts] * 2
        + [jax.ShapeDtypeStruct((8, 128), F32)],
        input_output_aliases={a: 2 * ns + a for a in range(2 * n)},
        compiler_params=pltpu.CompilerParams(has_side_effects=EFFECT),
    )(*[_hbm(p) for p in parts], *[_hbm(lax.empty(p.shape, p.dtype)) for p in parts])
    return list(res[:ns]), list(res[ns:2 * ns]), list(res[2 * ns:2 * ns + n]), list(res[2 * ns + n:2 * ns + 2 * n]), res[2 * ns + 2 * n]


def _chip_wait(name, send, recv, parts, lands, after):
    n = len(parts)
    ns = 3 * n

    def body(*refs):
        src, land = refs[:n], refs[n:2 * n]
        send_ref, recv_ref = refs[2 * n:2 * n + ns], refs[2 * n + ns:2 * n + 2 * ns]
        x, y, c, _ = _place()
        for a in range(n):
            for j in range(3):
                cp = _remote(src[a].at[j], land[a].at[j], send_ref[3 * a + j], recv_ref[3 * a + j], (x, y, c))
                cp.wait_send()
                cp.wait_recv()

    res = pl.pallas_call(
        body, name=name, in_specs=[HBM] * (2 * n) + [SEM] * (2 * ns) + [pl.BlockSpec(memory_space=pl.ANY)],
        out_specs=[HBM] * (2 * n), out_shape=[pltpu.HBM(p.shape, p.dtype) for p in parts] * 2,
        input_output_aliases={a: a for a in range(2 * n)},
        compiler_params=pltpu.CompilerParams(has_side_effects=EFFECT),
    )(*parts, *lands, *send, *recv, after)
    return list(res[n:])


def _share_start(name, shards):
    n = len(shards)

    def body(*refs):
        send, recv = refs[n:2 * n], refs[2 * n:3 * n]
        buf = refs[3 * n:4 * n]
        token = refs[4 * n]
        x, y, c, _ = _place()
        for a in range(n):
            half = buf[a].shape[0] // 2
            mine = buf[a].at[pl.ds(c * half, half)]
            _remote(mine, mine, send[a], recv[a], (x, y, 1 - c)).start()
        token[...] = jnp.zeros_like(token)

    res = pl.pallas_call(
        body, name=name, in_specs=[HBM] * n,
        out_specs=[SEM] * (2 * n) + [HBM] * n + [pl.BlockSpec(memory_space=pltpu.VMEM)],
        out_shape=[pltpu.SemaphoreType.DMA(())] * (2 * n) + [pltpu.HBM(s.shape, s.dtype) for s in shards]
        + [jax.ShapeDtypeStruct((8, 128), F32)],
        input_output_aliases={a: 2 * n + a for a in range(n)},
        compiler_params=pltpu.CompilerParams(has_side_effects=EFFECT),
    )(*[_hbm(s) for s in shards])
    return list(res[:n]), list(res[n:2 * n]), list(res[2 * n:3 * n]), res[3 * n]


def _share_wait(name, send, recv, shards, after):
    n = len(shards)

    def body(*refs):
        buf = refs[:n]
        send_ref, recv_ref = refs[n:2 * n], refs[2 * n:3 * n]
        x, y, c, _ = _place()
        for a in range(n):
            half = buf[a].shape[0] // 2
            cp = _remote(buf[a].at[pl.ds(c * half, half)], buf[a].at[pl.ds((1 - c) * half, half)], send_ref[a], recv_ref[a], (x, y, c))
            cp.wait_send()
            cp.wait_recv()

    res = pl.pallas_call(
        body, name=name, in_specs=[HBM] * n + [SEM] * (2 * n) + [pl.BlockSpec(memory_space=pl.ANY)], out_specs=[HBM] * n,
        out_shape=[pltpu.HBM(s.shape, s.dtype) for s in shards], input_output_aliases={a: a for a in range(n)},
        compiler_params=pltpu.CompilerParams(has_side_effects=EFFECT),
    )(*shards, *send, *recv, after)
    return list(res)


def _small_start(buf):
    R, W = buf.shape
    ns = N_DEV - 1

    def body(*refs):
        send, recv = refs[2:2 + ns], refs[2 + ns:2 + 2 * ns]
        src, land, token = refs[2 + 2 * ns], refs[3 + 2 * ns], refs[4 + 2 * ns]
        x, y, c, _ = _place()
        me = 4 * x + 2 * y + c
        for k in range(1, N_DEV):
            peer = (x ^ (k >> 2), y ^ ((k >> 1) & 1), c ^ (k & 1))
            _remote(src, land.at[me], send[k - 1], recv[k - 1], peer).start()
        token[...] = jnp.zeros_like(token)

    res = pl.pallas_call(
        body, name="small_start", in_specs=[HBM, HBM],
        out_specs=[SEM] * (2 * ns) + [HBM, HBM, pl.BlockSpec(memory_space=pltpu.VMEM)],
        out_shape=[pltpu.SemaphoreType.DMA(())] * (2 * ns) + [pltpu.HBM((R, W), F32), pltpu.HBM((N_DEV, R, W), F32),
                                                                jax.ShapeDtypeStruct((8, 128), F32)],
        input_output_aliases={0: 2 * ns, 1: 2 * ns + 1},
        compiler_params=pltpu.CompilerParams(has_side_effects=EFFECT),
    )(_hbm(buf), _hbm(jnp.zeros((N_DEV, R, W), F32)))
    return list(res[:ns]), list(res[ns:2 * ns]), res[2 * ns], res[2 * ns + 1], res[2 * ns + 2]


def _small_wait(send, recv, buf, land, after):
    ns = N_DEV - 1

    def body(*refs):
        land_ref = refs[1]
        send_ref, recv_ref = refs[2:2 + ns], refs[2 + ns:2 + 2 * ns]
        x, y, c, _ = _place()
        me = 4 * x + 2 * y + c
        for k in range(1, N_DEV):
            landed = land_ref.at[me ^ k]
            cp = _remote(landed, landed, send_ref[k - 1], recv_ref[k - 1], (x, y, c))
            cp.wait_send()
            cp.wait_recv()

    return pl.pallas_call(
        body, name="small_wait", in_specs=[HBM, HBM] + [SEM] * (2 * ns) + [pl.BlockSpec(memory_space=pl.ANY)],
        out_specs=[HBM, HBM], out_shape=[pltpu.HBM(buf.shape, buf.dtype), pltpu.HBM(land.shape, land.dtype)],
        input_output_aliases={0: 0, 1: 1}, compiler_params=pltpu.CompilerParams(has_side_effects=EFFECT),
    )(buf, land, *send, *recv, after)


def _small_sum(buf, land):
    def body(buf_ref, land_ref, out_ref):
        x, y, c, _ = _place()
        me = 4 * x + 2 * y + c
        total = None
        for d in range(N_DEV):
            term = jnp.where(me == d, buf_ref[...], land_ref[d])
            total = term if total is None else total + term
        out_ref[...] = total

    return pl.pallas_call(body, name="small_sum", out_shape=jax.ShapeDtypeStruct(buf.shape, F32))(buf, land)


def _pair_sum_bf16(name, grad, theirs, ids):
    _, R2, C = theirs.shape
    tr = _tile(R2, 512 if C <= 2048 else 256, 16)
    nrb = R2 // tr

    def body(ids_ref, a_ref, b_ref, o_ref):
        o_ref[...] = (a_ref[...] + b_ref[...]).astype(BF16)

    return pl.pallas_call(
        body, name=name,
        grid_spec=pltpu.PrefetchScalarGridSpec(
            num_scalar_prefetch=1, grid=(3, nrb),
            in_specs=[pl.BlockSpec((None, tr, C), lambda j, i, ids: (ids[3 + j], ids[1] * nrb + i, 0)),
                      pl.BlockSpec((None, tr, C), lambda j, i, ids: (j + 1, i, 0))],
            out_specs=pl.BlockSpec((None, tr, C), lambda j, i, ids: (j, i, 0))),
        out_shape=jax.ShapeDtypeStruct((3, R2, C), BF16), compiler_params=_params(("parallel", "parallel")),
    )(ids, grad, theirs)


def _chip_sum(name, grad, theirs, arrived, ids):
    _, R2, C = theirs.shape
    tr = _tile(R2, 512 if C <= 2048 else 256, 16)
    nrb = R2 // tr

    def body(ids_ref, a_ref, b_ref, r_ref, o_ref):
        tot = a_ref[...] + b_ref[...]
        for j in range(3):
            tot = tot + r_ref[j].astype(F32)
        o_ref[...] = tot

    return pl.pallas_call(
        body, name=name,
        grid_spec=pltpu.PrefetchScalarGridSpec(
            num_scalar_prefetch=1, grid=(nrb,),
            in_specs=[pl.BlockSpec((None, tr, C), lambda i, ids: (ids[0], ids[1] * nrb + i, 0)),
                      pl.BlockSpec((None, tr, C), lambda i, ids: (0, i, 0)),
                      pl.BlockSpec((3, tr, C), lambda i, ids: (0, i, 0))],
            out_specs=pl.BlockSpec((tr, C), lambda i, ids: (ids[1] * nrb + i, 0))),
        out_shape=jax.ShapeDtypeStruct((2 * R2, C), F32), compiler_params=_params(("parallel",)),
    )(ids, grad, theirs, arrived)


def _adamw(name, w, g, m, v, emit_grad=False):
    R, C = w.shape
    tr = _tile(R, 256 if C <= 2048 else 128, 8)
    c1 = 1.0 / (1.0 - ADAM_B1 ** ADAM_STEP)
    c2 = 1.0 / (1.0 - ADAM_B2 ** ADAM_STEP)
    n_out = 4 if emit_grad else 3

    def body(w_ref, g_ref, m_ref, v_ref, d_ref, mo_ref, vo_ref, *rest):
        gv = g_ref[...]
        mn = ADAM_B1 * m_ref[...] + (1.0 - ADAM_B1) * gv
        vn = ADAM_B2 * v_ref[...] + (1.0 - ADAM_B2) * (gv * gv)
        d_ref[...] = -ADAM_LR * ((mn * c1) / (jnp.sqrt(vn * c2) + ADAM_EPS) + ADAM_WD * w_ref[...])
        mo_ref[...] = mn
        vo_ref[...] = vn
        if emit_grad:
            rest[0][...] = gv

    spec = pl.BlockSpec((tr, C), lambda i: (i, 0))
    sds = jax.ShapeDtypeStruct((R, C), F32)
    return pl.pallas_call(body, name=name, grid=(R // tr,), in_specs=[spec] * 4, out_specs=[spec] * n_out, out_shape=[sds] * n_out,
                          compiler_params=_params(("parallel",)))(w, g, m, v)


SMALL = ["ffn1_norm", "mix_norm", "mem_norm", "forget_bias", "fox_q_gain", "fox_k_gain", "swa_q_gain", "swa_k_gain", "swa_sinks",
         "mem_q_gain", "mem_k_gain", "ffn2_norm"]
LARGE = ["ffn1_gate", "ffn1_up", "ffn1_down", "w_in", "w_mem_k", "w_mem_v", "w_out", "ffn2_gate", "ffn2_up", "ffn2_down"]
GATHER_GROUPS = [["ffn1_gate", "ffn1_up"], ["ffn1_down"], ["w_in", "w_mem_k", "w_mem_v"], ["w_out", "ffn2_gate", "ffn2_up", "ffn2_down"]]
WEIGHTS = ["ffn1_norm", "ffn1_gate", "ffn1_up", "ffn1_down", "mix_norm", "mem_norm", "w_in", "forget_bias", "w_mem_k", "w_mem_v",
           "fox_q_gain", "fox_k_gain", "swa_q_gain", "swa_k_gain", "swa_sinks", "mem_q_gain", "mem_k_gain", "w_out", "ffn2_norm",
           "ffn2_gate", "ffn2_up", "ffn2_down"]


def _pad_proj_cols(w):
    out = jnp.zeros((w.shape[0], PROJ_W), w.dtype)
    for start, width, pstart in REF_GROUPS:
        out = lax.dynamic_update_slice(out, w[:, start:start + width], (0, pstart))
    return out


def _unpad_proj_cols(w):
    return jnp.concatenate([w[:, pstart:pstart + width] for _, width, pstart in REF_GROUPS], axis=1)


def _pack_small(vals):
    flat = jnp.concatenate([vals[k].reshape(-1).astype(F32) for k in SMALL + ["loss"]])
    n = flat.shape[0]
    total = -(-n // 1024) * 1024
    return jnp.pad(flat, (0, total - n)).reshape(total // 128, 128)


def _unpack_small(buf, shapes):
    flat = buf.reshape(-1)
    out, off = {}, 0
    for k in SMALL + ["loss"]:
        size = int(np.prod(shapes[k]))
        out[k] = flat[off:off + size].reshape(shapes[k])
        off += size
    return out


def kernel(x, mem, ffn1_norm, ffn1_gate, ffn1_up, ffn1_down, mix_norm, mem_norm, w_in, forget_bias, w_mem_k, w_mem_v, fox_q_gain, fox_k_gain, swa_q_gain, swa_k_gain, swa_sinks, mem_q_gain, mem_k_gain, w_out, ffn2_norm, ffn2_gate, ffn2_up, ffn2_down, loss_target, m_ffn1_norm, m_ffn1_gate, m_ffn1_up, m_ffn1_down, m_mix_norm, m_mem_norm, m_w_in, m_forget_bias, m_w_mem_k, m_w_mem_v, m_fox_q_gain, m_fox_k_gain, m_swa_q_gain, m_swa_k_gain, m_swa_sinks, m_mem_q_gain, m_mem_k_gain, m_w_out, m_ffn2_norm, m_ffn2_gate, m_ffn2_up, m_ffn2_down, v_ffn1_norm, v_ffn1_gate, v_ffn1_up, v_ffn1_down, v_mix_norm, v_mem_norm, v_w_in, v_forget_bias, v_w_mem_k, v_w_mem_v, v_fox_q_gain, v_fox_k_gain, v_swa_q_gain, v_swa_k_gain, v_swa_sinks, v_mem_q_gain, v_mem_k_gain, v_w_out, v_ffn2_norm, v_ffn2_gate, v_ffn2_up, v_ffn2_down):
    given = dict(locals())
    T, D = x.shape[1], x.shape[2]
    ML = mem.shape[1]
    xin = x.reshape(T, D)
    target = loss_target.reshape(T, D)
    memin = mem.reshape(ML, D)

    ids = _place_ids()
    shard = {k: given[k][0] for k in LARGE}
    started, after = [], ids
    for gi, group in enumerate(GATHER_GROUPS):
        also = {}
        if "w_in" in group:
            tied = lax.optimization_barrier((given["w_in"], given["m_w_in"], given["v_w_in"], after))
            w_in_rows = tuple(t[0] for t in tied[:3])
            shard["w_in"] = _pad_proj_cols(w_in_rows[0])
            also["w_in"] = w_in_rows[1:]
        if gi == 0:
            placed, own = zip(*[_cast_place("cast_" + k, shard[k], ids, after, keep_own=True) for k in group])
        else:
            placed = [_cast_place("cast_" + k, shard[k], ids, after, *also.get(k, ())) for k in group]
        send, recv, bufs, after = _gather_start("gather_start_%d" % gi, list(placed), after)
        started.append((send, recv, bufs))

    def arrive(gi, done):
        send, recv, bufs = started[gi]
        bufs = _gather_wait("gather_wait_%d" % gi, send, recv, bufs, done)
        return dict(zip(GATHER_GROUPS[gi], _gather_forward("gather_forward_%d" % gi, bufs)))

    gains = jnp.concatenate([fox_q_gain, fox_k_gain, swa_q_gain, swa_k_gain, mem_q_gain,
                             jnp.pad(forget_bias, ((0, 0), (0, HEAD_DIM - FOX_HEADS))), jnp.zeros((2, HEAD_DIM), F32)], axis=0)
    slopes = _swa_bias_table()
    sinks = jnp.repeat(swa_sinks.reshape(SWA_HEADS), WINDOW).reshape(SWA_KV_HEADS, GR, 1)

    h1 = _rms_fwd("ffn1_norm_fwd", xin, ffn1_norm + after[0, 0])
    own_block = jnp.stack([jnp.zeros((), jnp.int32), ids[0]]).reshape(2, 1)
    part = _ffn_gu_blocks("ffn1_gate_up_own", h1, own[0][None], own[1][None], own_block, N_CHIPS)
    full = arrive(0, part[2])
    wg1, wu1 = full["ffn1_gate"], full["ffn1_up"]
    fg1, fu1, a1 = _ffn_gu_blocks("ffn1_gate_up", h1, wg1, wu1, jnp.stack([ids[3:], ids[3:]]), N_CHIPS, prev=part)
    wd1 = arrive(1, a1)["ffn1_down"].reshape(-1, D)
    x1, h2 = _residual_norm("ffn1_down", a1, wd1, xin, 0.5, mix_norm, 256)
    full = arrive(2, h2)
    win = full["w_in"].reshape(D, PROJ_W)
    wmk = full["w_mem_k"].reshape(D, MEM_HEADS * HEAD_DIM)
    wmv = full["w_mem_v"].reshape(D, MEM_HEADS * HEAD_DIM)
    proj = _mm2d("proj_in", h2, win, "nn", F32, tn=1408, tk=2048, n_outer=True)
    qkv, logf, k_t, v_t = _prep_fwd(proj, gains)
    cum = _cumsum_rows("forget_cumsum", [logf], False)
    cum_h = cum[:, :FOX_HEADS].T
    cq_row = cum_h.reshape(FOX_HEADS, 1, T)
    ck_rep = jnp.broadcast_to(cum_h[:, :, None], (FOX_HEADS, T, HEAD_DIM))
    mn = _rms_fwd("mem_norm_fwd", memin, mem_norm)
    mk_raw = _mm2d("mem_k_proj", mn, wmk, "nn", F32)
    mv = _mm2d("mem_v_proj", mn, wmv, "nn", BF16)
    mk = _head_norm_rows(mk_raw, mem_k_gain)
    out_a, out_a_f32, lse_a = _fox_fwd(qkv, v_t, cq_row, ck_rep)
    send, recv, bufs = started[3]
    bufs = _gather_wait("gather_wait_3", send, recv, bufs, out_a)
    send, recv, bufs, token = _gather_start("gather_pass_start_3", bufs, out_a, to_sibling=True)
    out_b, lse_b = _swa_fwd(qkv, slopes + token[0, 0], sinks)
    out_c, lse_c = _mem_fwd(qkv, mk, mv)
    mixed = jnp.concatenate([out_a, out_b, out_c], axis=1)
    full = dict(zip(GATHER_GROUPS[3], _gather_wait("gather_pass_wait_3", send, recv, bufs, mixed, to_sibling=True)))
    wo = full["w_out"].reshape(-1, D)
    wg2, wu2, wd2 = full["ffn2_gate"], full["ffn2_up"], full["ffn2_down"].reshape(-1, D)
    x2, h3 = _residual_norm("mix_out", mixed, wo, x1, 1.0, ffn2_norm, 512)
    fg2, fu2, a2 = _ffn_gu("ffn2_gate_up", h3, wg2, wu2)
    dx3, dyb3, loss_blocks = _ffn_down_loss("ffn2_down", a2, wd2, x2, target)

    grads, small, res = {}, {"loss": jnp.sum(loss_blocks[::8, 0])}, {}

    def pair_off(tag, group):
        send, recv, own, lands, token = _pair_start("grad_pair_start_" + tag, [grads[k] for k in group])
        return (group, send, recv, own, lands), token

    def chip_off(tag, started, done):
        group, send, recv, own, lands = started
        own, theirs = _pair_wait("grad_pair_wait_" + tag, send, recv, own, lands, done)
        grads.update(zip(group, own))
        to_chips = [_pair_sum_bf16("pair_sum_" + k, grads[k], b, ids) for k, b in zip(group, theirs)]
        send, recv, parts, lands, token = _chip_start("grad_chip_start_" + tag, to_chips)
        return (group, theirs, send, recv, parts, lands), token

    def reduce_half(tag, state, done):
        group, theirs, send, recv, parts, lands = state
        arrived = _chip_wait("grad_chip_wait_" + tag, send, recv, parts, lands, done)
        halves = [_chip_sum("chip_sum_" + k, grads[k], b, r, ids) for k, b, r in zip(group, theirs, arrived)]
        send, recv, shards, token = _share_start("grad_share_start_" + tag, halves)
        return (group, send, recv, shards), token

    def update(tag, shared, done):
        group, send, recv, shards = shared
        reduced = dict(zip(group, _share_wait("grad_share_wait_" + tag, send, recv, shards, done)))
        last = None
        for k in group:
            if k == "w_in":
                gk = _unpad_proj_cols(reduced[k])
                d, mo, vo = _adamw("adamw_" + k, w_in_rows[0], gk, w_in_rows[1], w_in_rows[2])
            else:
                d, mo, vo, gk = _adamw("adamw_" + k, given[k][0], reduced[k], given["m_" + k][0], given["v_" + k][0], emit_grad=True)
            res[k] = tuple(t[None] for t in (gk, d, mo, vo))
            last = vo
        return last

    dg2, du2 = _ffn_bwd_act("ffn2", dyb3, wd2, fg2, fu2, N_CHIPS)
    grads["ffn2_down"] = _ffn_bwd_down("ffn2", a2, dyb3, N_CHIPS).reshape(N_CHIPS, -1, D)
    grads["ffn2_gate"], grads["ffn2_up"] = _ffn_bwd_gate_up("ffn2", h3, dg2, du2, N_CHIPS)
    started, token = pair_off("a", ["ffn2_gate", "ffn2_up", "ffn2_down"])
    dh3 = _ffn_bwd_x("ffn2", dg2, du2, wg2, wu2, token)
    state_a, token = chip_off("a", started, dh3)
    dx2, dx2b, small["ffn2_norm"] = _rms_bwd("ffn2_norm_bwd", dh3, x2, ffn2_norm + token[0, 0], dx3, 1.0)
    dmix = _mm2d("mix_out_dx", dx2b, wo, "nt", BF16, tk=2048, n_outer=True)
    grads["w_out"] = _mm2d("mix_out_dw", mixed, dx2b, "tn", F32, tk=T, n_outer=True, resident=True).reshape(N_CHIPS, -1, D)
    delta_row = _fox_delta(dmix, out_a_f32)[:, :FOX_HEADS].T.reshape(FOX_HEADS, 1, T)
    dfq, dfk, dfv, dck, dcq = _fox_bwd(qkv, k_t, cq_row, ck_rep, delta_row, lse_a, dmix)
    dsq, dsk, dsv, dsink = _swa_bwd(qkv, slopes, sinks, out_b, lse_b, dmix)
    dmq, dmk, dmv = _mem_bwd(qkv, mk, mv, out_c, lse_c, dmix)
    small["swa_sinks"] = dsink[:, :SWA_GROUP, 0].reshape(1, SWA_HEADS)
    dcum = jnp.pad(dcq.reshape(FOX_HEADS, T).T, ((0, 0), (0, HEAD_DIM - FOX_HEADS)))
    dlogf = _cumsum_rows("forget_cumsum_bwd", [dcum], True, columns=dck)
    dproj, dgains = _prep_bwd(proj, gains, dfq, dfk, dfv, dsq, dsk, dsv, dmq, dlogf)
    for row, k in enumerate(["fox_q_gain", "fox_k_gain", "swa_q_gain", "swa_k_gain", "mem_q_gain"]):
        small[k] = dgains[row:row + 1, :]
    small["forget_bias"] = dgains[5:6, :FOX_HEADS]
    grads["w_in"] = _mm2d("proj_in_dw", h2, dproj, "tn", F32, tn=1408, tk=T, n_outer=True, resident=True).reshape(N_CHIPS, -1, PROJ_W)
    dmk_raw, small["mem_k_gain"] = _head_norm_rows_bwd(mk_raw, mem_k_gain, dmk)
    dmvb = dmv.astype(BF16)
    grads["w_mem_k"] = _mm2d("mem_k_dw", mn, dmk_raw, "tn", F32).reshape(N_CHIPS, -1, MEM_HEADS * HEAD_DIM)
    grads["w_mem_v"] = _mm2d("mem_v_dw", mn, dmvb, "tn", F32).reshape(N_CHIPS, -1, MEM_HEADS * HEAD_DIM)
    dmn = _mm2d("mem_k_dx", dmk_raw, wmk, "nt", F32)
    dmn = _mm2d("mem_v_dx", dmvb, wmv, "nt", F32, extras=[dmn], epilogue=lambda accs, ex: [ex[0] + accs[0]])
    _, _, small["mem_norm"] = _rms_bwd("mem_norm_bwd", dmn, memin, mem_norm, jnp.zeros_like(memin), 1.0)
    started, token = pair_off("b", ["w_out", "w_in", "w_mem_k", "w_mem_v"])
    dh2 = _mm2d("proj_in_dx", dproj, win, "nt", F32, tm=512, tn=1024, tk=PROJ_W, n_outer=True, resident=True, after=token)
    state_b, token = chip_off("b", started, dh2)
    dx1, dyb1, small["mix_norm"] = _rms_bwd("mix_norm_bwd", dh2, x1, mix_norm + token[0, 0], dx2, 0.5)
    grads["ffn1_down"] = _ffn_bwd_down("ffn1", a1, dyb1, N_CHIPS).reshape(N_CHIPS, -1, D)
    started, token = pair_off("c", ["ffn1_down"])
    dg1, du1 = _ffn_bwd_act("ffn1", dyb1, wd1, fg1, fu1, N_CHIPS, after=token)
    state_c, token = chip_off("c", started, dg1)
    grads["ffn1_gate"], grads["ffn1_up"] = _ffn_bwd_gate_up("ffn1", h1, dg1, du1, N_CHIPS, after=token)
    started, token = pair_off("d", ["ffn1_gate", "ffn1_up"])
    dh1 = _ffn_bwd_x("ffn1", dg1, du1, wg1, wu1, token)
    state_d, token = chip_off("d", started, dh1)
    grad_x, _, small["ffn1_norm"] = _rms_bwd("ffn1_norm_bwd", dh1, xin, ffn1_norm + token[0, 0], dx1, 1.0)

    s_send, s_recv, s_buf, s_land, token = _small_start(_pack_small(small))

    shared_a, token = reduce_half("a", state_a, token)
    shared_b, token = reduce_half("b", state_b, token)
    done = update("a", shared_a, token)
    shared_c, token = reduce_half("c", state_c, done)
    done = update("b", shared_b, token)
    shared_d, token = reduce_half("d", state_d, done)
    done = update("c", shared_c, token)
    done = update("d", shared_d, done)

    shapes = {k: given[k].shape for k in SMALL}
    shapes["loss"] = ()
    s_buf, s_land = _small_wait(s_send, s_recv, s_buf, s_land, done)
    red_small = _unpack_small(_small_sum(s_buf, s_land), shapes)
    loss = red_small["loss"]
    zero = {"loss": jnp.zeros((), F32)}
    packed = [_pack_small({**zero, **{k: src[k] for k in SMALL}}) for src in (
        {k: given[k] for k in SMALL}, red_small, {k: given["m_" + k] for k in SMALL}, {k: given["v_" + k] for k in SMALL})]
    d_s, m_s, v_s = (_unpack_small(t, shapes) for t in _adamw("adamw_small", *packed))
    for k in SMALL:
        res[k] = (red_small[k], d_s[k], m_s[k], v_s[k])

    outs = [loss, grad_x.reshape(1, T, D)]
    for part in range(4):
        outs += [res[k][part] for k in WEIGHTS]
    return tuple(outs)
```
